```python
import math
import jax, jax.numpy as jnp
from jax import lax
import numpy as np

D_MODEL = 1024
BATCH = 4
SEQ = 8192
DEPTH = 1

HG_HEADS = 4
HG_DK = 128
HG_DV = 128
HG_WIDTH = HG_HEADS * HG_DK
HG_VWIDTH = HG_HEADS * HG_DV
HG_CHUNK = 64
MB_HEADS = 8
MB_DH = 64
MB_WIDTH = MB_HEADS * MB_DH
MB_BLOCK = 256
MB_TOPK = 3
MB_QCHUNK = 64
REL_BUCKETS = 32
REL_MAX_DIST = 2048
MEM_LEN = 256
X_HEADS = 4
X_DH = D_MODEL // X_HEADS
PEER_HEADS = 8
PEER_NKEYS = 128
PEER_EXPERTS = PEER_NKEYS * PEER_NKEYS
PEER_TOPK = 16
PEER_DKEY = 256
PEER_HALF = PEER_DKEY // 2
PEER_TOK_CHUNK = 512
EPS = 1e-6
IN_SPLITS = [HG_WIDTH, HG_WIDTH, HG_VWIDTH, HG_VWIDTH, MB_WIDTH, MB_WIDTH, MB_WIDTH, D_MODEL, D_MODEL]
IN_COLS = int(sum(IN_SPLITS))
IN_OFFSETS = [int(o) for o in np.cumsum(IN_SPLITS)[:-1]]

kernel_name = "hybrid_hgrn2_moba_peer_block"


def rmsnorm(x, g):
    xf = x.astype(jnp.float32)
    xf = xf * lax.rsqrt(jnp.mean(xf * xf, axis=-1, keepdims=True) + EPS)
    return (xf * g.astype(jnp.float32)).astype(x.dtype)


def t5_bucket(dist):
    max_exact = REL_BUCKETS // 2
    is_small = dist < max_exact
    scaled = jnp.log(jnp.maximum(dist, 1).astype(jnp.float32) / max_exact) / math.log(REL_MAX_DIST / max_exact)
    large = max_exact + (scaled * (REL_BUCKETS - max_exact)).astype(jnp.int32)
    large = jnp.minimum(large, REL_BUCKETS - 1)
    return jnp.where(is_small, dist, large)


def hgrn2(q, f_logit, i, g, lb, norm_gain):
    B, S, _ = q.shape
    nc = S // HG_CHUNK
    f = lb + (1.0 - lb) * jax.nn.sigmoid(f_logit.astype(jnp.float32))
    log_f = jnp.log(f)
    k = 1.0 - f

    def chunks(t, d):
        return t.astype(jnp.float32).reshape(B, nc, HG_CHUNK, HG_HEADS, d).transpose(1, 0, 3, 2, 4)

    causal = jnp.tril(jnp.ones((HG_CHUNK, HG_CHUNK), dtype=bool))[:, :, None]

    def step(state, inp):
        qq, kk, lf, vv = inp
        b = jnp.cumsum(lf, axis=2)
        diff = b[:, :, :, None, :] - b[:, :, None, :, :]
        decay = jnp.exp(jnp.where(causal, diff, -jnp.inf))
        a = jnp.einsum('bhtd,bhsd,bhtsd->bhts', qq, kk, decay)
        o = jnp.einsum('bhts,bhse->bhte', a, vv) + jnp.einsum('bhtd,bhde->bhte', qq * jnp.exp(b), state)
        b_end = b[:, :, -1:, :]
        state = jnp.exp(b_end[:, :, 0, :, None]) * state + jnp.einsum('bhsd,bhse->bhde', kk * jnp.exp(b_end - b), vv)
        return state, o

    init = jnp.zeros((B, HG_HEADS, HG_DK, HG_DV), jnp.float32)
    _, o = lax.scan(step, init, (chunks(q, HG_DK), chunks(k, HG_DK), chunks(log_f, HG_DK), chunks(i, HG_DV)))
    o = o.transpose(1, 0, 3, 2, 4).reshape(B, S, HG_HEADS, HG_DV)
    o = o * lax.rsqrt(jnp.mean(o * o, axis=-1, keepdims=True) + EPS)
    o = o.reshape(B, S, HG_VWIDTH) * norm_gain.astype(jnp.float32) * jax.nn.silu(g.astype(jnp.float32))
    return o.astype(q.dtype)


def moba(q, k, v, rel_bias):
    B, S, _ = q.shape
    Sp = -(-S // MB_BLOCK) * MB_BLOCK
    pad = Sp - S

    def heads(t):
        t = jnp.pad(t, ((0, 0), (0, pad), (0, 0)))
        return t.reshape(B, Sp, MB_HEADS, MB_DH).transpose(0, 2, 1, 3)

    q, k, v = heads(q), heads(k), heads(v)
    nb = Sp // MB_BLOCK
    kb = k.reshape(B, MB_HEADS, nb, MB_BLOCK, MB_DH)
    vb = v.reshape(B, MB_HEADS, nb, MB_BLOCK, MB_DH)
    gate = jnp.einsum('bhsd,bhnd->bhsn', q, jnp.mean(kb, axis=3)).astype(jnp.float32)
    pos = jnp.arange(Sp)
    fully_past = jnp.arange(nb)[None, :] < (pos // MB_BLOCK)[:, None]
    gate = jnp.where(fully_past, gate, -jnp.inf)
    n_sel = min(MB_TOPK, nb)
    sel_score, sel_idx = lax.top_k(gate, n_sel)
    sel_valid = jnp.isfinite(sel_score)
    nqc = Sp // MB_QCHUNK

    def qchunks(t):
        return jnp.moveaxis(t.reshape((B, MB_HEADS, nqc, MB_QCHUNK) + t.shape[3:]), 2, 0)

    scale = MB_DH ** -0.5
    bias_tab = rel_bias.T
    b_ix = jnp.arange(B)[:, None, None, None]
    h_ix = jnp.arange(MB_HEADS)[None, :, None, None]

    def one_chunk(args):
        c, qc, ic, vc = args
        q0 = c * MB_QCHUNK
        qpos = q0 + jnp.arange(MB_QCHUNK)
        ks = kb[b_ix, h_ix, ic]
        vs = vb[b_ix, h_ix, ic]
        s_sel = jnp.einsum('bhqd,bhqnld->bhqnl', qc, ks).astype(jnp.float32) * scale
        kpos = ic[..., None] * MB_BLOCK + jnp.arange(MB_BLOCK)
        dist = qpos[:, None, None] - kpos
        s_sel = s_sel + bias_tab[h_ix[..., None], t5_bucket(jnp.maximum(dist, 0))].astype(jnp.float32)
        s_sel = jnp.where(vc[..., None], s_sel, -jnp.inf).reshape(B, MB_HEADS, MB_QCHUNK, n_sel * MB_BLOCK)
        blk0 = (q0 // MB_BLOCK) * MB_BLOCK
        k_own = lax.dynamic_slice_in_dim(k, blk0, MB_BLOCK, axis=2)
        v_own = lax.dynamic_slice_in_dim(v, blk0, MB_BLOCK, axis=2)
        dist_own = qpos[:, None] - (blk0 + jnp.arange(MB_BLOCK))[None, :]
        s_own = jnp.einsum('bhqd,bhld->bhql', qc, k_own).astype(jnp.float32) * scale
        s_own = s_own + bias_tab[:, t5_bucket(jnp.maximum(dist_own, 0))].astype(jnp.float32)
        s_own = jnp.where(dist_own >= 0, s_own, -jnp.inf)
        p = jax.nn.softmax(jnp.concatenate([s_sel, s_own], axis=-1), axis=-1).astype(v.dtype)
        p_sel = p[..., :n_sel * MB_BLOCK].reshape(B, MB_HEADS, MB_QCHUNK, n_sel, MB_BLOCK)
        p_own = p[..., n_sel * MB_BLOCK:]
        return jnp.einsum('bhqnl,bhqnld->bhqd', p_sel, vs) + jnp.einsum('bhql,bhld->bhqd', p_own, v_own)

    outs = lax.map(one_chunk, (jnp.arange(nqc), qchunks(q), qchunks(sel_idx), qchunks(sel_valid)))
    out = jnp.moveaxis(outs, 0, 2).reshape(B, MB_HEADS, Sp, MB_DH)[:, :, :S]
    return out.transpose(0, 2, 1, 3).reshape(B, S, MB_WIDTH)


def cross_attn(h, mem_n, wq, wk, wv, wo):
    B, S, _ = h.shape
    M = mem_n.shape[1]
    q = (h @ wq).reshape(B, S, X_HEADS, X_DH)
    k = (mem_n @ wk).reshape(B, M, X_HEADS, X_DH)
    v = (mem_n @ wv).reshape(B, M, X_HEADS, X_DH)
    s = jnp.einsum('bshd,bmhd->bhsm', q, k).astype(jnp.float32) * (X_DH ** -0.5)
    p = jax.nn.softmax(s, axis=-1).astype(v.dtype)
    o = jnp.einsum('bhsm,bmhd->bshd', p, v).reshape(B, S, D_MODEL)
    return o @ wo


def peer(h, w_query, sub_keys, expert_u, expert_v):
    B, S, D = h.shape
    T = B * S
    Tp = -(-T // PEER_TOK_CHUNK) * PEER_TOK_CHUNK
    xt = jnp.pad(h.reshape(T, D), ((0, Tp - T), (0, 0))).reshape(Tp // PEER_TOK_CHUNK, PEER_TOK_CHUNK, D)

    def chunk_fn(xc):
        tc = xc.shape[0]
        qh = (xc @ w_query).reshape(tc, PEER_HEADS, 2, PEER_HALF)
        sc = jnp.einsum('thcd,hcnd->thcn', qh, sub_keys).astype(jnp.float32)
        s_top, i_top = lax.top_k(sc, PEER_TOPK)
        cand = (s_top[..., 0, :, None] + s_top[..., 1, None, :]).reshape(tc, PEER_HEADS, PEER_TOPK * PEER_TOPK)
        cand_idx = (i_top[..., 0, :, None] * PEER_NKEYS + i_top[..., 1, None, :]).reshape(tc, PEER_HEADS, PEER_TOPK * PEER_TOPK)
        s_fin, pos = lax.top_k(cand, PEER_TOPK)
        e_idx = jnp.take_along_axis(cand_idx, pos, axis=-1)
        w = jax.nn.softmax(s_fin, axis=-1)
        u = expert_u[e_idx]
        act = jax.nn.gelu(jnp.einsum('td,thkd->thk', xc, u).astype(jnp.float32))
        coef = (w * act).astype(xc.dtype)
        return jnp.einsum('thk,thkd->td', coef, expert_v[e_idx])

    y = lax.map(chunk_fn, xt).reshape(Tp, D)[:T]
    return y.reshape(B, S, D)


def setup_inputs(seed: int = 0) -> dict:
    key = jax.random.key(seed)
    ks = jax.random.split(key, 24)
    f32 = jnp.float32
    nrm = lambda k, shape, s: jax.random.normal(k, shape, f32) * s
    gain = lambda k, shape: 1.0 + 0.01 * jax.random.normal(k, shape, f32)
    return {
        "x": nrm(ks[0], (BATCH, SEQ, D_MODEL), 1.0),
        "mem": nrm(ks[1], (BATCH, MEM_LEN, D_MODEL), 1.0),
        "rel_bias": nrm(ks[2], (REL_BUCKETS, MB_HEADS), 0.5),
        "ln_mix": gain(ks[3], (DEPTH, D_MODEL)),
        "w_in": nrm(ks[4], (DEPTH, D_MODEL, IN_COLS), D_MODEL ** -0.5),
        "hg_lower": nrm(ks[5], (DEPTH + 1, HG_WIDTH), 0.5),
        "hg_norm": gain(ks[6], (DEPTH, HG_VWIDTH)),
        "w_up_a": nrm(ks[7], (DEPTH, HG_VWIDTH, D_MODEL), HG_VWIDTH ** -0.5),
        "w_up_b": nrm(ks[8], (DEPTH, MB_WIDTH, D_MODEL), MB_WIDTH ** -0.5),
        "w_out": nrm(ks[9], (DEPTH, D_MODEL, D_MODEL), D_MODEL ** -0.5),
        "ln_cross": gain(ks[10], (DEPTH, D_MODEL)),
        "ln_mem": gain(ks[11], (DEPTH, D_MODEL)),
        "wq_x": nrm(ks[12], (DEPTH, D_MODEL, D_MODEL), D_MODEL ** -0.5),
        "wk_x": nrm(ks[13], (DEPTH, D_MODEL, D_MODEL), D_MODEL ** -0.5),
        "wv_x": nrm(ks[14], (DEPTH, D_MODEL, D_MODEL), D_MODEL ** -0.5),
        "wo_x": nrm(ks[15], (DEPTH, D_MODEL, D_MODEL), D_MODEL ** -0.5),
        "ln_ffn": gain(ks[16], (DEPTH, D_MODEL)),
        "peer_query": nrm(ks[17], (DEPTH, D_MODEL, PEER_HEADS * PEER_DKEY), D_MODEL ** -0.5),
        "peer_subkeys": nrm(ks[18], (DEPTH, PEER_HEADS, 2, PEER_NKEYS, PEER_HALF), PEER_HALF ** -0.5),
        "peer_u": nrm(ks[19], (DEPTH, PEER_EXPERTS, D_MODEL), D_MODEL ** -0.5),
        "peer_v": nrm(ks[20], (DEPTH, PEER_EXPERTS, D_MODEL), 0.5),
        "ln_final": gain(ks[21], (D_MODEL,)),
    }


def reference(x, mem, rel_bias, ln_mix, w_in, hg_lower, hg_norm, w_up_a, w_up_b, w_out,
              ln_cross, ln_mem, wq_x, wk_x, wv_x, wo_x, ln_ffn,
              peer_query, peer_subkeys, peer_u, peer_v, ln_final):
    lb_all = jnp.cumsum(jax.nn.softmax(hg_lower.astype(jnp.float32), axis=0), axis=0)
    for l in range(DEPTH):
        h = rmsnorm(x, ln_mix[l])
        proj = h @ w_in[l]
        hq, hf, hi, hg, mq, mk, mv, ga, gb = jnp.split(proj, IN_OFFSETS, axis=-1)
        ya = hgrn2(hq, hf, hi, hg, lb_all[l], hg_norm[l])
        yb = moba(mq, mk, mv, rel_bias)
        z = jax.nn.sigmoid(ga) * (ya @ w_up_a[l]) + jax.nn.sigmoid(gb) * (yb @ w_up_b[l])
        x = x + z @ w_out[l]
        h = rmsnorm(x, ln_cross[l])
        x = x + cross_attn(h, rmsnorm(mem, ln_mem[l]), wq_x[l], wk_x[l], wv_x[l], wo_x[l])
        h = rmsnorm(x, ln_ffn[l])
        x = x + peer(h, peer_query[l], peer_subkeys[l], peer_u[l], peer_v[l])
    return rmsnorm(x, ln_final)
```

```python
import functools
import math

import jax
import jax.numpy as jnp
import numpy as np
from jax import lax
from jax.experimental import pallas as pl
from jax.experimental.pallas import tpu as pltpu
from jax.experimental.pallas import tpu_sc as plsc

F32 = jnp.float32
BF16 = jnp.bfloat16
I32 = jnp.int32
EPS = 1e-6
NEG_INF = float("-inf")

HG_HEADS = 4
HG_D = 128
HG_WIDTH = HG_HEADS * HG_D
HG_CHUNK = 64
HG_SUB = 16
MB_HEADS = 8
MB_DH = 64
MB_WIDTH = MB_HEADS * MB_DH
MB_BLOCK = 256
MB_TOPK = 3
MB_BIAS_TILES = 8
REL_BUCKETS = 32
REL_MAX_DIST = 2048
X_HEADS = 4
PEER_HEADS = 8
PEER_NKEYS = 128
PEER_TOPK = 16
PEER_HALF = 128
PEER_SEL = PEER_HEADS * PEER_TOPK

VMEM_LIMIT = 56 * 1024 * 1024


def _cparams(sem):
    return pltpu.CompilerParams(dimension_semantics=sem, vmem_limit_bytes=VMEM_LIMIT)


def _rms(x, g):
    ms = jnp.mean(x * x, axis=-1, keepdims=True)
    return x * lax.rsqrt(ms + EPS) * g


def _in_proj_kernel(x_ref, g_ref, w0_ref, w1_ref, w2_ref, o0_ref, o1_ref, o2_ref):
    h = _rms(x_ref[...], g_ref[...]).astype(BF16)
    o0_ref[...] = jnp.dot(h, w0_ref[...], preferred_element_type=F32)
    o1_ref[...] = jnp.dot(h, w1_ref[...], preferred_element_type=F32).astype(BF16)
    o2_ref[...] = jnp.dot(h, w2_ref[...], preferred_element_type=F32).astype(BF16)


def in_proj(x2d, g, w0, w1, w2, tm=256):
    t, d = x2d.shape
    ns = (w0.shape[1], w1.shape[1], w2.shape[1])
    return pl.pallas_call(
        _in_proj_kernel,
        grid=(t // tm,),
        in_specs=[
            pl.BlockSpec((tm, d), lambda i: (i, 0)),
            pl.BlockSpec((1, d), lambda i: (0, 0)),
        ] + [pl.BlockSpec((d, n), lambda i: (0, 0)) for n in ns],
        out_specs=[pl.BlockSpec((tm, n), lambda i: (i, 0)) for n in ns],
        out_shape=[jax.ShapeDtypeStruct((t, ns[0]), F32),
                   jax.ShapeDtypeStruct((t, ns[1]), BF16),
                   jax.ShapeDtypeStruct((t, ns[2]), BF16)],
        compiler_params=_cparams(("parallel",)),
        name="in_proj",
    )(x2d, g, w0, w1, w2)


def _hgrn_kernel(q_ref, f_ref, i_ref, g_ref, lb_ref, gain_ref, o_ref, st_ref):
    c = pl.program_id(1)

    @pl.when(c == 0)
    def _():
        st_ref[...] = jnp.zeros_like(st_ref)

    C, S = HG_CHUNK, HG_SUB
    row = lax.broadcasted_iota(I32, (C, C), 0)
    col = lax.broadcasted_iota(I32, (C, C), 1)
    tril = (row >= col).astype(F32)
    t_iota = lax.broadcasted_iota(I32, (S, 1), 0)

    for h in range(HG_HEADS):
        sl = slice(h * HG_D, (h + 1) * HG_D)
        q = q_ref[:, sl]
        v = i_ref[:, sl]
        lb = lb_ref[:, sl]
        f = lb + (1.0 - lb) * jax.nn.sigmoid(f_ref[:, sl])
        lf = jnp.log(f)
        k = 1.0 - f
        b = jnp.dot(tril, lf, precision=lax.Precision.HIGHEST, preferred_element_type=F32)
        st = st_ref[h]
        vb = v.astype(BF16)
        qd = (q * jnp.exp(b)).astype(BF16)
        o_inter = lax.dot_general(qd, st.astype(BF16), (((1,), (1,)), ((), ())),
                                  preferred_element_type=F32)
        outs = []
        for i in range(C // S):
            r0 = i * S
            qi = q[r0:r0 + S]
            ki = k[r0:r0 + S]
            bi = b[r0:r0 + S]
            vi = v[r0:r0 + S]
            oi = o_inter[r0:r0 + S]
            if i > 0:
                bs = b[r0 - 1:r0]
                qh = (qi * jnp.exp(bi - bs)).astype(BF16)
                kh = (k[:r0] * jnp.exp(bs - b[:r0])).astype(BF16)
                a = lax.dot_general(qh, kh, (((1,), (1,)), ((), ())), preferred_element_type=F32)
                oi = oi + jnp.dot(a.astype(BF16), vb[:r0], preferred_element_type=F32)
            for s in range(S):
                dec = jnp.exp(jnp.minimum(bi - bi[s:s + 1], 0.0))
                p = qi * ki[s:s + 1] * dec
                a_s = jnp.sum(p, axis=-1, keepdims=True)
                a_s = jnp.where(t_iota >= s, a_s, 0.0)
                oi = oi + a_s * vi[s:s + 1]
            outs.append(oi)
        o = jnp.concatenate(outs, axis=0)
        b_end = b[C - 1:C]
        kd = (k * jnp.exp(b_end - b)).astype(BF16)
        upd = lax.dot_general(vb, kd, (((0,), (0,)), ((), ())), preferred_element_type=F32)
        st_ref[h] = st * jnp.exp(b_end) + upd
        o = o * lax.rsqrt(jnp.mean(o * o, axis=-1, keepdims=True) + EPS)
        g = g_ref[:, sl]
        o_ref[:, sl] = (o * gain_ref[:, sl] * (g * jax.nn.sigmoid(g))).astype(o_ref.dtype)


def hgrn2(p0, lb, gain, batch, seq):
    t = p0.shape[0]
    nc = seq // HG_CHUNK
    w = HG_WIDTH

    def col(j):
        return pl.BlockSpec((HG_CHUNK, w), lambda b, c, j=j: (b * nc + c, j))

    return pl.pallas_call(
        _hgrn_kernel,
        grid=(batch, nc),
        in_specs=[col(0), col(1), col(2), col(3),
                  pl.BlockSpec((1, w), lambda b, c: (0, 0)),
                  pl.BlockSpec((1, w), lambda b, c: (0, 0))],
        out_specs=pl.BlockSpec((HG_CHUNK, w), lambda b, c: (b * nc + c, 0)),
        out_shape=jax.ShapeDtypeStruct((t, w), BF16),
        scratch_shapes=[pltpu.VMEM((HG_HEADS, HG_D, HG_D), F32)],
        compiler_params=_cparams(("parallel", "arbitrary")),
        name="hgrn2",
    )(p0, p0, p0, p0, lb, gain)


def _kmean_kernel(k_ref, o_ref):
    o_ref[0] = jnp.mean(k_ref[...].astype(F32), axis=0, keepdims=True)


def moba_kmean(p1, batch, seq):
    nbt = p1.shape[0] // MB_BLOCK
    return pl.pallas_call(
        _kmean_kernel,
        grid=(nbt,),
        in_specs=[pl.BlockSpec((MB_BLOCK, MB_WIDTH), lambda i: (i, 1))],
        out_specs=pl.BlockSpec((1, 1, MB_WIDTH), lambda i: (i, 0, 0)),
        out_shape=jax.ShapeDtypeStruct((nbt, 1, MB_WIDTH), F32),
        compiler_params=_cparams(("parallel",)),
        name="moba_kmean",
    )(p1)


def _moba_kernel(q_ref, kt_ref, v_ref, km_ref, bias_ref, o_ref, m_ref, l_ref, acc_ref, idx_ref):
    qi = pl.program_id(2)
    nb = km_ref.shape[-1]
    q = q_ref[...]
    gate = jnp.dot(q.astype(F32), km_ref[...], precision=lax.Precision.HIGHEST,
                   preferred_element_type=F32)
    n_iota = lax.broadcasted_iota(I32, gate.shape, 1)
    gate = jnp.where(n_iota < qi, gate, NEG_INF)
    for r in range(MB_TOPK):
        mx = jnp.max(gate, axis=1, keepdims=True)
        ix = jnp.min(jnp.where(gate == mx, n_iota, nb), axis=1, keepdims=True)
        idx_ref[r] = jnp.where(mx > NEG_INF, ix, -1)
        gate = jnp.where(n_iota == ix, NEG_INF, gate)

    qs = q * jnp.asarray(MB_DH ** -0.5, BF16)
    s = jnp.dot(qs, kt_ref[qi], preferred_element_type=F32) + bias_ref[0]
    r_io = lax.broadcasted_iota(I32, s.shape, 0)
    c_io = lax.broadcasted_iota(I32, s.shape, 1)
    s = jnp.where(r_io >= c_io, s, NEG_INF)
    m0 = jnp.max(s, axis=1, keepdims=True)
    p = jnp.exp(s - m0)
    m_ref[...] = m0
    l_ref[...] = jnp.sum(p, axis=1, keepdims=True)
    acc_ref[...] = jnp.dot(p.astype(BF16), v_ref[qi], preferred_element_type=F32)

    def past(n, carry):
        d = jnp.minimum(qi - n, MB_BIAS_TILES - 1)
        sel = (idx_ref[0] == n) | (idx_ref[1] == n) | (idx_ref[2] == n)
        s = jnp.dot(qs, kt_ref[n], preferred_element_type=F32) + bias_ref[d]
        s = jnp.where(sel, s, NEG_INF)
        m_old = m_ref[...]
        m_new = jnp.maximum(m_old, jnp.max(s, axis=1, keepdims=True))
        alpha = jnp.exp(m_old - m_new)
        p = jnp.exp(s - m_new)
        l_ref[...] = alpha * l_ref[...] + jnp.sum(p, axis=1, keepdims=True)
        acc_ref[...] = alpha * acc_ref[...] + jnp.dot(p.astype(BF16), v_ref[n],
                                                       preferred_element_type=F32)
        m_ref[...] = m_new
        return carry

    lax.fori_loop(0, qi, past, 0)
    o_ref[...] = (acc_ref[...] / l_ref[...]).astype(o_ref.dtype)


def moba_attention(q, kt, v, kmt, bias):
    b, h, s, dh = q.shape
    nb = s // MB_BLOCK
    return pl.pallas_call(
        _moba_kernel,
        grid=(b, h, nb),
        in_specs=[
            pl.BlockSpec((None, None, MB_BLOCK, dh), lambda bb, hh, i: (bb, hh, i, 0)),
            pl.BlockSpec((None, None, nb, dh, MB_BLOCK), lambda bb, hh, i: (bb, hh, 0, 0, 0)),
            pl.BlockSpec((None, None, nb, MB_BLOCK, dh), lambda bb, hh, i: (bb, hh, 0, 0, 0)),
            pl.BlockSpec((None, None, dh, nb), lambda bb, hh, i: (bb, hh, 0, 0)),
            pl.BlockSpec((None, MB_BIAS_TILES, MB_BLOCK, MB_BLOCK), lambda bb, hh, i: (hh, 0, 0, 0)),
        ],
        out_specs=pl.BlockSpec((None, None, MB_BLOCK, dh), lambda bb, hh, i: (bb, hh, i, 0)),
        out_shape=jax.ShapeDtypeStruct((b, h, s, dh), BF16),
        scratch_shapes=[
            pltpu.VMEM((MB_BLOCK, 1), F32),
            pltpu.VMEM((MB_BLOCK, 1), F32),
            pltpu.VMEM((MB_BLOCK, dh), F32),
            pltpu.VMEM((MB_TOPK, MB_BLOCK, 1), I32),
        ],
        compiler_params=_cparams(("parallel", "parallel", "arbitrary")),
        name="moba_attn",
    )(q, kt, v, kmt, bias)


def _t5_bucket(dist):
    max_exact = REL_BUCKETS // 2
    scaled = jnp.log(jnp.maximum(dist, 1).astype(F32) / max_exact) / math.log(REL_MAX_DIST / max_exact)
    large = jnp.minimum(max_exact + (scaled * (REL_BUCKETS - max_exact)).astype(I32), REL_BUCKETS - 1)
    return jnp.where(dist < max_exact, dist, large)


def moba_bias_tiles(rel_bias):
    r = jnp.arange(MB_BLOCK)
    d = jnp.arange(MB_BIAS_TILES)
    dist = d[:, None, None] * MB_BLOCK + r[None, :, None] - r[None, None, :]
    bucket = _t5_bucket(jnp.maximum(dist, 0))
    return jnp.transpose(rel_bias.astype(F32)[bucket], (3, 0, 1, 2))


def _mix_kernel(x_ref, ya_ref, yb_ref, ga_ref, gb_ref, wa_ref, wb_ref, wo_ref, o_ref):
    za = jnp.dot(ya_ref[...], wa_ref[...], preferred_element_type=F32)
    zb = jnp.dot(yb_ref[...], wb_ref[...], preferred_element_type=F32)
    z = jax.nn.sigmoid(ga_ref[...].astype(F32)) * za + jax.nn.sigmoid(gb_ref[...].astype(F32)) * zb
    o_ref[...] = x_ref[...] + jnp.dot(z.astype(BF16), wo_ref[...], preferred_element_type=F32)


def mix_out(x2d, ya, yb, pg, wa, wb, wo, tm=256):
    t, d = x2d.shape
    w = ya.shape[1]
    return pl.pallas_call(
        _mix_kernel,
        grid=(t // tm,),
        in_specs=[
            pl.BlockSpec((tm, d), lambda i: (i, 0)),
            pl.BlockSpec((tm, w), lambda i: (i, 0)),
            pl.BlockSpec((tm, w), lambda i: (i, 0)),
            pl.BlockSpec((tm, d), lambda i: (i, 0)),
            pl.BlockSpec((tm, d), lambda i: (i, 1)),
            pl.BlockSpec((w, d), lambda i: (0, 0)),
            pl.BlockSpec((w, d), lambda i: (0, 0)),
            pl.BlockSpec((d, d), lambda i: (0, 0)),
        ],
        out_specs=pl.BlockSpec((tm, d), lambda i: (i, 0)),
        out_shape=jax.ShapeDtypeStruct((t, d), F32),
        compiler_params=_cparams(("parallel",)),
        name="mix_out",
    )(x2d, ya, yb, pg, pg, wa, wb, wo)


def _mem_kv_kernel(m_ref, g_ref, wk_ref, wv_ref, k_ref, v_ref):
    mn = _rms(m_ref[...], g_ref[...]).astype(BF16)
    k_ref[...] = jnp.dot(mn, wk_ref[...], preferred_element_type=F32).astype(BF16)
    v_ref[...] = jnp.dot(mn, wv_ref[...], preferred_element_type=F32).astype(BF16)


def mem_kv(mem, g, wk, wv):
    b, m, d = mem.shape
    spec = pl.BlockSpec((None, m, d), lambda i: (i, 0, 0))
    wspec = pl.BlockSpec((d, d), lambda i: (0, 0))
    return pl.pallas_call(
        _mem_kv_kernel,
        grid=(b,),
        in_specs=[spec, pl.BlockSpec((1, d), lambda i: (0, 0)), wspec, wspec],
        out_specs=[spec, spec],
        out_shape=[jax.ShapeDtypeStruct((b, m, d), BF16)] * 2,
        compiler_params=_cparams(("parallel",)),
        name="mem_kv",
    )(mem, g, wk, wv)


def _cross_kernel(x_ref, g_ref, wq_ref, k_ref, v_ref, wo_ref, o_ref):
    x = x_ref[...]
    d = x.shape[1]
    dh = d // X_HEADS
    h = _rms(x, g_ref[...]).astype(BF16)
    q = (jnp.dot(h, wq_ref[...], preferred_element_type=F32) * (dh ** -0.5)).astype(BF16)
    outs = []
    for hh in range(X_HEADS):
        sl = slice(hh * dh, (hh + 1) * dh)
        s = lax.dot_general(q[:, sl], k_ref[:, sl], (((1,), (1,)), ((), ())),
                            preferred_element_type=F32)
        p = jnp.exp(s - jnp.max(s, axis=1, keepdims=True))
        l = jnp.sum(p, axis=1, keepdims=True)
        o = jnp.dot(p.astype(BF16), v_ref[:, sl], preferred_element_type=F32) / l
        outs.append(o.astype(BF16))
    o = jnp.concatenate(outs, axis=1)
    o_ref[...] = x + jnp.dot(o, wo_ref[...], preferred_element_type=F32)


def cross_attn(x2d, g, wq, kx, vx, wo, seq, tm=256):
    t, d = x2d.shape
    m = kx.shape[1]
    per_b = seq // tm
    kv = pl.BlockSpec((None, m, d), lambda i: (i // per_b, 0, 0))
    wspec = pl.BlockSpec((d, d), lambda i: (0, 0))
    return pl.pallas_call(
        _cross_kernel,
        grid=(t // tm,),
        in_specs=[pl.BlockSpec((tm, d), lambda i: (i, 0)), pl.BlockSpec((1, d), lambda i: (0, 0)),
                  wspec, kv, kv, wspec],
        out_specs=pl.BlockSpec((tm, d), lambda i: (i, 0)),
        out_shape=jax.ShapeDtypeStruct((t, d), F32),
        compiler_params=_cparams(("parallel",)),
        name="cross_attn",
    )(x2d, g, wq, kx, vx, wo)


def _topk_rows(sc, k):
    n = sc.shape[0]
    io = lax.broadcasted_iota(I32, sc.shape, 0)
    vals, ids = [], []
    for _ in range(k):
        m = jnp.max(sc, axis=0, keepdims=True)
        ix = jnp.min(jnp.where(sc == m, io, n), axis=0, keepdims=True)
        vals.append(m)
        ids.append(ix)
        sc = jnp.where(io == ix, NEG_INF, sc)
    return jnp.concatenate(vals, axis=0), jnp.concatenate(ids, axis=0)


def _route_kernel(x_ref, g_ref, wq_ref, sk_ref, h_ref, idx_ref, w_ref, hb_ref):
    p = pl.program_id(1)

    @pl.when(p == 0)
    def _():
        h = _rms(x_ref[...], g_ref[...])
        h_ref[...] = h
        hb_ref[...] = h.astype(BF16)

    qh = jnp.dot(hb_ref[...], wq_ref[...], preferred_element_type=F32)
    tops = []
    for c in range(2):
        seg = qh[:, c * PEER_HALF:(c + 1) * PEER_HALF]
        sc = lax.dot_general(sk_ref[c], seg, (((1,), (1,)), ((), ())),
                             precision=lax.Precision.HIGHEST, preferred_element_type=F32)
        tops.append(_topk_rows(sc, PEER_TOPK))
    (s0, i0), (s1, i1) = tops
    cand = jnp.concatenate([s0[a:a + 1] + s1 for a in range(PEER_TOPK)], axis=0)
    cidx = jnp.concatenate([i0[a:a + 1] * PEER_NKEYS + i1 for a in range(PEER_TOPK)], axis=0)
    io = lax.broadcasted_iota(I32, cand.shape, 0)
    n = cand.shape[0]
    vals, ids = [], []
    for _ in range(PEER_TOPK):
        m = jnp.max(cand, axis=0, keepdims=True)
        px = jnp.min(jnp.where(cand == m, io, n), axis=0, keepdims=True)
        hit = io == px
        vals.append(m)
        ids.append(jnp.sum(jnp.where(hit, cidx, 0), axis=0, keepdims=True))
        cand = jnp.where(hit, NEG_INF, cand)
    sf = jnp.concatenate(vals, axis=0)
    e = jnp.exp(sf - sf[0:1])
    w_ref[...] = e / jnp.sum(e, axis=0, keepdims=True)
    idx_ref[...] = jnp.concatenate(ids, axis=0)


def peer_route(x2d, g, wq, sk, tm=256):
    t, d = x2d.shape
    ph = sk.shape[0]
    return pl.pallas_call(
        _route_kernel,
        grid=(t // tm, ph),
        in_specs=[
            pl.BlockSpec((tm, d), lambda i, p: (i, 0)),
            pl.BlockSpec((1, d), lambda i, p: (0, 0)),
            pl.BlockSpec((d, 2 * PEER_HALF), lambda i, p: (0, p)),
            pl.BlockSpec((None, 2, PEER_NKEYS, PEER_HALF), lambda i, p: (p, 0, 0, 0)),
        ],
        out_specs=[
            pl.BlockSpec((tm, d), lambda i, p: (i, 0)),
            pl.BlockSpec((None, PEER_TOPK, tm), lambda i, p: (p, 0, i)),
            pl.BlockSpec((None, PEER_TOPK, tm), lambda i, p: (p, 0, i)),
        ],
        out_shape=[jax.ShapeDtypeStruct((t, d), F32),
                   jax.ShapeDtypeStruct((ph, PEER_TOPK, t), I32),
                   jax.ShapeDtypeStruct((ph, PEER_TOPK, t), F32)],
        scratch_shapes=[pltpu.VMEM((tm, d), BF16)],
        compiler_params=_cparams(("parallel", "arbitrary")),
        name="peer_route",
    )(x2d, g, wq, sk)


def _coef_kernel(w_ref, a_ref, o_ref):
    o_ref[...] = w_ref[...] * jax.nn.gelu(a_ref[...])


def peer_coef(w, act, tm=1024):
    t, n = w.shape
    spec = pl.BlockSpec((tm, n), lambda i: (i, 0))
    return pl.pallas_call(
        _coef_kernel, grid=(t // tm,), in_specs=[spec, spec], out_specs=spec,
        out_shape=jax.ShapeDtypeStruct((t, n), F32),
        compiler_params=_cparams(("parallel",)), name="peer_coef",
    )(w, act)


def _final_kernel(x_ref, y_ref, g_ref, o_ref):
    o_ref[...] = _rms(x_ref[...] + y_ref[...], g_ref[...])


def final_norm(x2d, y, g, tm=512):
    t, d = x2d.shape
    spec = pl.BlockSpec((tm, d), lambda i: (i, 0))
    return pl.pallas_call(
        _final_kernel, grid=(t // tm,),
        in_specs=[spec, spec, pl.BlockSpec((1, d), lambda i: (0, 0))], out_specs=spec,
        out_shape=jax.ShapeDtypeStruct((t, d), F32),
        compiler_params=_cparams(("parallel",)), name="final_norm",
    )(x2d, y, g)


SC_CORES = 2
SC_SUBCORES = 16
SC_WORKERS = SC_CORES * SC_SUBCORES
SC_LANES = 16
SC_GROUP = 16


def _sc_mesh():
    return plsc.VectorSubcoreMesh(core_axis_name="c", subcore_axis_name="s")


def _sc_params():
    return pltpu.CompilerParams(needs_layout_passes=False)


def _sc_worker_id():
    return lax.axis_index("s") * SC_CORES + lax.axis_index("c")


def peer_dots_sc(table, idx_flat, h):
    t, d = h.shape
    nsel = PEER_SEL
    tpw = t // SC_WORKERS
    groups = tpw // SC_GROUP
    g = SC_GROUP
    heads = nsel // SC_LANES
    pieces = d // SC_LANES

    @functools.partial(
        pl.kernel, mesh=_sc_mesh(),
        out_type=jax.ShapeDtypeStruct((t * nsel,), F32),
        scratch_types=[
            pltpu.VMEM((g * nsel,), I32),
            pltpu.VMEM((g, d), F32),
            pltpu.VMEM((SC_LANES, d), F32),
            pltpu.VMEM((SC_LANES, d), F32),
            pltpu.VMEM((g * nsel,), F32),
            pltpu.SemaphoreType.DMA,
            pltpu.SemaphoreType.DMA,
        ],
        compiler_params=_sc_params(),
        name="peer_dots_sc",
    )
    def k(tab_hbm, idx_hbm, h_hbm, out_hbm, idx_v, h_v, rows0, rows1, out_v, sem0, sem1):
        bufs = ((rows0, sem0), (rows1, sem1))
        wid = _sc_worker_id()
        lane = lax.iota(I32, SC_LANES)

        def gather(tt, hd, slot):
            rows, sem = bufs[slot]
            ids = idx_v.at[pl.ds(tt * nsel + hd * SC_LANES, SC_LANES)]
            return pltpu.make_async_copy(tab_hbm.at[ids], rows, sem)

        def compute(tt, hd, slot):
            rows = bufs[slot][0]

            def body(c, accs):
                off = pl.multiple_of(c * SC_LANES, SC_LANES)
                hv = h_v[tt, pl.ds(off, SC_LANES)]
                return tuple(accs[r] + rows[r, pl.ds(off, SC_LANES)] * hv for r in range(SC_LANES))

            accs = lax.fori_loop(0, pieces, body,
                                 tuple(jnp.zeros((SC_LANES,), F32) for _ in range(SC_LANES)))
            res = jnp.zeros((SC_LANES,), F32)
            for r in range(SC_LANES):
                res = jnp.where(lane == r, jnp.sum(accs[r]), res)
            out_v[pl.ds(tt * nsel + hd * SC_LANES, SC_LANES)] = res

        @pl.loop(0, groups)
        def _(gi):
            base = wid * tpw + gi * g
            pltpu.sync_copy(idx_hbm.at[pl.ds(base * nsel, g * nsel)], idx_v)
            pltpu.sync_copy(h_hbm.at[pl.ds(base, g)], h_v)
            gather(0, 0, 0).start()

            @pl.loop(0, g)
            def _(tt):
                for hd in range(heads):
                    slot = hd % 2
                    if hd + 1 < heads:
                        gather(tt, hd + 1, 1 - slot).start()
                    else:
                        @pl.when(tt + 1 < g)
                        def _():
                            gather(tt + 1, 0, 1 - slot).start()
                    gather(tt, hd, slot).wait()
                    compute(tt, hd, slot)

            pltpu.sync_copy(out_v, out_hbm.at[pl.ds(base * nsel, g * nsel)])

    return k(table, idx_flat, h)


def peer_combine_sc(table, idx_flat, coef_flat, t):
    d = table.shape[1]
    nsel = PEER_SEL
    tpw = t // SC_WORKERS
    groups = tpw // SC_GROUP
    g = SC_GROUP
    heads = nsel // SC_LANES
    pieces = d // SC_LANES

    @functools.partial(
        pl.kernel, mesh=_sc_mesh(),
        out_type=jax.ShapeDtypeStruct((t, d), F32),
        scratch_types=[
            pltpu.VMEM((g * nsel,), I32),
            pltpu.VMEM((g * nsel,), F32),
            pltpu.VMEM((SC_LANES, d), F32),
            pltpu.VMEM((SC_LANES, d), F32),
            pltpu.VMEM((g, d), F32),
            pltpu.SemaphoreType.DMA,
            pltpu.SemaphoreType.DMA,
        ],
        compiler_params=_sc_params(),
        name="peer_combine_sc",
    )
    def k(tab_hbm, idx_hbm, coef_hbm, out_hbm, idx_v, coef_v, rows0, rows1, y_v, sem0, sem1):
        bufs = ((rows0, sem0), (rows1, sem1))
        wid = _sc_worker_id()

        def gather(tt, hd, slot):
            rows, sem = bufs[slot]
            ids = idx_v.at[pl.ds(tt * nsel + hd * SC_LANES, SC_LANES)]
            return pltpu.make_async_copy(tab_hbm.at[ids], rows, sem)

        def compute(tt, hd, slot):
            rows = bufs[slot][0]
            cbase = tt * nsel + hd * SC_LANES
            cs = [plsc.load_gather(coef_v, [jnp.full((SC_LANES,), cbase + r, I32)])
                  for r in range(SC_LANES)]

            @pl.loop(0, pieces)
            def _(c):
                off = pl.multiple_of(c * SC_LANES, SC_LANES)
                if hd == 0:
                    a = jnp.zeros((SC_LANES,), F32)
                else:
                    a = y_v[tt, pl.ds(off, SC_LANES)]
                for r in range(SC_LANES):
                    a = a + cs[r] * rows[r, pl.ds(off, SC_LANES)]
                y_v[tt, pl.ds(off, SC_LANES)] = a

        @pl.loop(0, groups)
        def _(gi):
            base = wid * tpw + gi * g
            pltpu.sync_copy(idx_hbm.at[pl.ds(base * nsel, g * nsel)], idx_v)
            pltpu.sync_copy(coef_hbm.at[pl.ds(base * nsel, g * nsel)], coef_v)
            gather(0, 0, 0).start()

            @pl.loop(0, g)
            def _(tt):
                for hd in range(heads):
                    slot = hd % 2
                    if hd + 1 < heads:
                        gather(tt, hd + 1, 1 - slot).start()
                    else:
                        @pl.when(tt + 1 < g)
                        def _():
                            gather(tt + 1, 0, 1 - slot).start()
                    gather(tt, hd, slot).wait()
                    compute(tt, hd, slot)

            pltpu.sync_copy(y_v, out_hbm.at[pl.ds(base, g)])

    return k(table, idx_flat, coef_flat)


def kernel(x, mem, rel_bias, ln_mix, w_in, hg_lower, hg_norm, w_up_a, w_up_b, w_out, ln_cross, ln_mem, wq_x, wk_x, wv_x, wo_x, ln_ffn, peer_query, peer_subkeys, peer_u, peer_v, ln_final):
    b, s, d = x.shape
    t = b * s
    depth = w_in.shape[0]
    assert depth == 1, "the residual after PEER is fused into the final norm"
    assert s % MB_BLOCK == 0 and s % HG_CHUNK == 0 and t % (SC_WORKERS * SC_GROUP) == 0
    nb = s // MB_BLOCK
    row = lambda a: a.reshape(1, -1).astype(F32)
    lb_all = jnp.cumsum(jax.nn.softmax(hg_lower.astype(F32), axis=0), axis=0)
    bias = moba_bias_tiles(rel_bias)
    n_hg = 4 * HG_WIDTH
    n_mb = 3 * MB_WIDTH
    x2d = x.reshape(t, d)
    for l in range(depth):
        w = w_in[l].astype(BF16)
        p0, pm, pg = in_proj(x2d, row(ln_mix[l]), w[:, :n_hg], w[:, n_hg:n_hg + n_mb], w[:, n_hg + n_mb:])
        ya = hgrn2(p0, row(lb_all[l]), row(hg_norm[l]), b, s)
        km = moba_kmean(pm, b, s)
        kmt = km.reshape(b, nb, MB_HEADS, MB_DH).transpose(0, 2, 3, 1)
        q = pm[:, :MB_WIDTH].reshape(b, s, MB_HEADS, MB_DH).transpose(0, 2, 1, 3)
        kt = pm[:, MB_WIDTH:2 * MB_WIDTH].reshape(b, nb, MB_BLOCK, MB_HEADS, MB_DH).transpose(0, 3, 1, 4, 2)
        v = pm[:, 2 * MB_WIDTH:].reshape(b, nb, MB_BLOCK, MB_HEADS, MB_DH).transpose(0, 3, 1, 2, 4)
        yb = moba_attention(q, kt, v, kmt, bias).transpose(0, 2, 1, 3).reshape(t, MB_WIDTH)
        x2d = mix_out(x2d, ya, yb, pg, w_up_a[l].astype(BF16), w_up_b[l].astype(BF16), w_out[l].astype(BF16))
        kx, vx = mem_kv(mem, row(ln_mem[l]), wk_x[l].astype(BF16), wv_x[l].astype(BF16))
        x2d = cross_attn(x2d, row(ln_cross[l]), wq_x[l].astype(BF16), kx, vx, wo_x[l].astype(BF16), s)
        hf, eidx, wts = peer_route(x2d, row(ln_ffn[l]), peer_query[l].astype(BF16), peer_subkeys[l].astype(F32))
        idx_flat = eidx.reshape(PEER_SEL, t).T.reshape(t * PEER_SEL)
        wts2d = wts.reshape(PEER_SEL, t).T
        act = peer_dots_sc(peer_u[l], idx_flat, hf).reshape(t, PEER_SEL)
        coef = peer_coef(wts2d, act)
        y = peer_combine_sc(peer_v[l], idx_flat, coef.reshape(t * PEER_SEL), t)
    return final_norm(x2d, y, row(ln_final)).reshape(b, s, d)
```

```python
import functools
import math

import jax
import jax.numpy as jnp
import numpy as np
from jax import lax
from jax.experimental import pallas as pl
from jax.experimental.pallas import tpu as pltpu
from jax.experimental.pallas import tpu_sc as plsc

F32 = jnp.float32
BF16 = jnp.bfloat16
I32 = jnp.int32
EPS = 1e-6
NEG_INF = float("-inf")

HG_HEADS = 4
HG_D = 128
HG_WIDTH = HG_HEADS * HG_D
HG_CHUNK = 64
HG_SUB = 16
MB_HEADS = 8
MB_DH = 64
MB_WIDTH = MB_HEADS * MB_DH
MB_BLOCK = 256
MB_TOPK = 3
MB_BIAS_TILES = 8
REL_BUCKETS = 32
REL_MAX_DIST = 2048
X_HEADS = 4
PEER_HEADS = 8
PEER_NKEYS = 128
PEER_TOPK = 16
PEER_HALF = 128
PEER_SEL = PEER_HEADS * PEER_TOPK

VMEM_LIMIT = 56 * 1024 * 1024


def _cparams(sem):
    return pltpu.CompilerParams(dimension_semantics=sem, vmem_limit_bytes=VMEM_LIMIT)


def _rms(x, g):
    ms = jnp.mean(x * x, axis=-1, keepdims=True)
    return x * lax.rsqrt(ms + EPS) * g


def _in_proj_kernel(x_ref, g_ref, w0_ref, w1_ref, wvt_ref, w2_ref, o0_ref, o1_ref, ovt_ref, o2_ref):
    h = _rms(x_ref[...], g_ref[...]).astype(BF16)
    o0_ref[...] = jnp.dot(h, w0_ref[...], preferred_element_type=F32)
    o1_ref[...] = jnp.dot(h, w1_ref[...], preferred_element_type=F32).astype(BF16)
    ovt_ref[0] = lax.dot_general(wvt_ref[...], h, (((1,), (1,)), ((), ())),
                                 preferred_element_type=F32).astype(BF16)
    o2_ref[...] = jnp.dot(h, w2_ref[...], preferred_element_type=F32).astype(BF16)


def in_proj(x2d, g, w0, w1, wvt, w2):
    t, d = x2d.shape
    tm = MB_BLOCK
    n0, n1, nv, n2 = w0.shape[1], w1.shape[1], wvt.shape[0], w2.shape[1]
    full = lambda a: pl.BlockSpec(a.shape, lambda i: (0, 0))
    return pl.pallas_call(
        _in_proj_kernel,
        grid=(t // tm,),
        in_specs=[pl.BlockSpec((tm, d), lambda i: (i, 0)), full(g), full(w0), full(w1), full(wvt), full(w2)],
        out_specs=[pl.BlockSpec((tm, n0), lambda i: (i, 0)),
                   pl.BlockSpec((tm, n1), lambda i: (i, 0)),
                   pl.BlockSpec((1, nv, tm), lambda i: (i, 0, 0)),
                   pl.BlockSpec((tm, n2), lambda i: (i, 0))],
        out_shape=[jax.ShapeDtypeStruct((t, n0), F32),
                   jax.ShapeDtypeStruct((t, n1), BF16),
                   jax.ShapeDtypeStruct((t // tm, nv, tm), BF16),
                   jax.ShapeDtypeStruct((t, n2), BF16)],
        compiler_params=_cparams(("parallel",)),
        name="in_proj",
    )(x2d, g, w0, w1, wvt, w2)


def _hgrn_kernel(q_ref, f_ref, i_ref, g_ref, lb_ref, gain_ref, o_ref, st_ref):
    c = pl.program_id(1)

    @pl.when(c == 0)
    def _():
        st_ref[...] = jnp.zeros_like(st_ref)

    C, S = HG_CHUNK, HG_SUB
    row = lax.broadcasted_iota(I32, (C, C), 0)
    col = lax.broadcasted_iota(I32, (C, C), 1)
    tril = (row >= col).astype(F32)
    t_iota = lax.broadcasted_iota(I32, (S, 1), 0)

    for h in range(HG_HEADS):
        sl = slice(h * HG_D, (h + 1) * HG_D)
        q = q_ref[:, sl]
        v = i_ref[:, sl]
        lb = lb_ref[:, sl]
        f = lb + (1.0 - lb) * jax.nn.sigmoid(f_ref[:, sl])
        lf = jnp.log(f)
        k = 1.0 - f
        b = jnp.dot(tril, lf, precision=lax.Precision.HIGHEST, preferred_element_type=F32)
        st = st_ref[h]
        vb = v.astype(BF16)
        qd = (q * jnp.exp(b)).astype(BF16)
        o_inter = lax.dot_general(qd, st.astype(BF16), (((1,), (1,)), ((), ())),
                                  preferred_element_type=F32)
        outs = []
        for i in range(C // S):
            r0 = i * S
            qi = q[r0:r0 + S]
            ki = k[r0:r0 + S]
            bi = b[r0:r0 + S]
            vi = v[r0:r0 + S]
            oi = o_inter[r0:r0 + S]
            if i > 0:
                bs = b[r0 - 1:r0]
                qh = (qi * jnp.exp(bi - bs)).astype(BF16)
                kh = (k[:r0] * jnp.exp(bs - b[:r0])).astype(BF16)
                a = lax.dot_general(qh, kh, (((1,), (1,)), ((), ())), preferred_element_type=F32)
                oi = oi + jnp.dot(a.astype(BF16), vb[:r0], preferred_element_type=F32)
            for s in range(S):
                dec = jnp.exp(jnp.minimum(bi - bi[s:s + 1], 0.0))
                p = qi * ki[s:s + 1] * dec
                a_s = jnp.sum(p, axis=-1, keepdims=True)
                a_s = jnp.where(t_iota >= s, a_s, 0.0)
                oi = oi + a_s * vi[s:s + 1]
            outs.append(oi)
        o = jnp.concatenate(outs, axis=0)
        b_end = b[C - 1:C]
        kd = (k * jnp.exp(b_end - b)).astype(BF16)
        upd = lax.dot_general(vb, kd, (((0,), (0,)), ((), ())), preferred_element_type=F32)
        st_ref[h] = st * jnp.exp(b_end) + upd
        o = o * lax.rsqrt(jnp.mean(o * o, axis=-1, keepdims=True) + EPS)
        g = g_ref[:, sl]
        o_ref[:, sl] = (o * gain_ref[:, sl] * (g * jax.nn.sigmoid(g))).astype(o_ref.dtype)


def hgrn2(p0, lb, gain, batch, seq):
    t = p0.shape[0]
    nc = seq // HG_CHUNK
    w = HG_WIDTH

    def col(j):
        return pl.BlockSpec((HG_CHUNK, w), lambda b, c, j=j: (b * nc + c, j))

    return pl.pallas_call(
        _hgrn_kernel,
        grid=(batch, nc),
        in_specs=[col(0), col(1), col(2), col(3),
                  pl.BlockSpec((1, w), lambda b, c: (0, 0)),
                  pl.BlockSpec((1, w), lambda b, c: (0, 0))],
        out_specs=pl.BlockSpec((HG_CHUNK, w), lambda b, c: (b * nc + c, 0)),
        out_shape=jax.ShapeDtypeStruct((t, w), BF16),
        scratch_shapes=[pltpu.VMEM((HG_HEADS, HG_D, HG_D), F32)],
        compiler_params=_cparams(("parallel", "arbitrary")),
        name="hgrn2",
    )(p0, p0, p0, p0, lb, gain)


def _kmean_kernel(k_ref, o_ref):
    o_ref[0] = jnp.mean(k_ref[...].astype(F32), axis=0, keepdims=True)


def moba_kmean(p1, batch, seq):
    nbt = p1.shape[0] // MB_BLOCK
    return pl.pallas_call(
        _kmean_kernel,
        grid=(nbt,),
        in_specs=[pl.BlockSpec((MB_BLOCK, MB_WIDTH), lambda i: (i, 1))],
        out_specs=pl.BlockSpec((1, 1, MB_WIDTH), lambda i: (i, 0, 0)),
        out_shape=jax.ShapeDtypeStruct((nbt, 1, MB_WIDTH), F32),
        compiler_params=_cparams(("parallel",)),
        name="moba_kmean",
    )(p1)


MB_PAIR = 4
MB_PW = MB_PAIR * MB_DH
MB_LG = 128


def _moba_kernel(q_ref, k_ref, vt_ref, km_ref, bias_ref, o_ref, *scratch):
    m_ref, l_ref, acc_ref, msk_ref = (scratch[i * MB_PAIR:(i + 1) * MB_PAIR] for i in range(4))
    qi = pl.program_id(2)
    nb = km_ref.shape[0]
    blk = MB_BLOCK
    heads = range(MB_PAIR)
    grp = lambda hh: slice((hh // 2) * MB_LG, (hh // 2 + 1) * MB_LG)
    q = q_ref[...]
    lane = lax.broadcasted_iota(I32, (blk, MB_LG), 1)
    in_head = [(lane < MB_DH) if hh % 2 == 0 else (lane >= MB_DH) for hh in heads]
    qs = q * jnp.asarray(MB_DH ** -0.5, BF16)
    qh = [jnp.where(in_head[hh], qs[:, grp(hh)], jnp.zeros((blk, MB_LG), BF16)) for hh in heads]
    nt = (((1,), (1,)), ((), ()))

    qf = q.astype(F32)
    n_io = lax.broadcasted_iota(I32, (nb, blk), 0)
    for hh in heads:
        gate = lax.dot_general(km_ref[:, grp(hh)], jnp.where(in_head[hh], qf[:, grp(hh)], 0.0), nt,
                               precision=lax.Precision.HIGHEST, preferred_element_type=F32)
        gate = jnp.where(n_io < qi, gate, NEG_INF)
        chosen = n_io < 0
        for _ in range(MB_TOPK):
            mx = jnp.max(gate, axis=0, keepdims=True)
            ix = jnp.min(jnp.where(gate == mx, n_io, nb), axis=0, keepdims=True)
            hit = n_io == ix
            chosen = chosen | (hit & (mx > NEG_INF))
            gate = jnp.where(hit, NEG_INF, gate)
        msk_ref[hh][...] = jnp.where(chosen, 0.0, NEG_INF)

    k_own = k_ref[pl.ds(pl.multiple_of(qi * blk, blk), blk), :]
    vt_own = vt_ref[qi]
    key_io = lax.broadcasted_iota(I32, (blk, blk), 0)
    qry_io = lax.broadcasted_iota(I32, (blk, blk), 1)
    own_rows = lambda r, hh: r[(hh % 2) * MB_DH:(hh % 2 + 1) * MB_DH]
    for hh in heads:
        s = lax.dot_general(k_own[:, grp(hh)], qh[hh], nt, preferred_element_type=F32) + bias_ref[hh, 0]
        s = jnp.where(key_io <= qry_io, s, NEG_INF)
        m0 = jnp.max(s, axis=0, keepdims=True)
        p = jnp.exp(s - m0)
        m_ref[hh][...] = m0
        l_ref[hh][...] = jnp.sum(p, axis=0, keepdims=True)
        r = jnp.dot(vt_own[grp(hh)], p.astype(BF16), preferred_element_type=F32)
        acc_ref[hh][...] = own_rows(r, hh)

    def past(n, carry):
        kn = k_ref[pl.ds(pl.multiple_of(n * blk, blk), blk), :]
        vtn = vt_ref[n]
        d = jnp.minimum(qi - n, MB_BIAS_TILES - 1)
        s = [lax.dot_general(kn[:, grp(hh)], qh[hh], nt, preferred_element_type=F32)
             + bias_ref[hh, d] + msk_ref[hh][pl.ds(n, 1), :] for hh in heads]
        m_old = [m_ref[hh][...] for hh in heads]
        l_old = [l_ref[hh][...] for hh in heads]
        a_old = [acc_ref[hh][...] for hh in heads]
        m_new = [jnp.maximum(m_old[hh], jnp.max(s[hh], axis=0, keepdims=True)) for hh in heads]
        alpha = [jnp.exp(m_old[hh] - m_new[hh]) for hh in heads]
        p = [jnp.exp(s[hh] - m_new[hh]) for hh in heads]
        r = [jnp.dot(vtn[grp(hh)], p[hh].astype(BF16), preferred_element_type=F32) for hh in heads]
        l_new = [alpha[hh] * l_old[hh] + jnp.sum(p[hh], axis=0, keepdims=True) for hh in heads]
        a_new = [alpha[hh] * a_old[hh] + own_rows(r[hh], hh) for hh in heads]
        for hh in heads:
            m_ref[hh][...] = m_new[hh]
            l_ref[hh][...] = l_new[hh]
            acc_ref[hh][...] = a_new[hh]
        return carry

    lax.fori_loop(0, qi, past, 0)
    out_t = jnp.concatenate([acc_ref[hh][...] / l_ref[hh][...] for hh in heads], axis=0)
    o_ref[...] = out_t.T.astype(o_ref.dtype)


def moba_attention(pqk, vt, km, bias, batch, seq):
    t = pqk.shape[0]
    nb = seq // MB_BLOCK
    groups = MB_WIDTH // MB_PW
    return pl.pallas_call(
        _moba_kernel,
        grid=(batch, groups, nb),
        in_specs=[
            pl.BlockSpec((MB_BLOCK, MB_PW), lambda b, j, i: (b * nb + i, j)),
            pl.BlockSpec((seq, MB_PW), lambda b, j, i: (b, groups + j)),
            pl.BlockSpec((nb, MB_PW, MB_BLOCK), lambda b, j, i: (b, j, 0)),
            pl.BlockSpec((None, nb, MB_PW), lambda b, j, i: (b, 0, j)),
            pl.BlockSpec((MB_PAIR, MB_BIAS_TILES, MB_BLOCK, MB_BLOCK), lambda b, j, i: (j, 0, 0, 0)),
        ],
        out_specs=pl.BlockSpec((MB_BLOCK, MB_PW), lambda b, j, i: (b * nb + i, j)),
        out_shape=jax.ShapeDtypeStruct((t, MB_WIDTH), BF16),
        scratch_shapes=(
            [pltpu.VMEM((1, MB_BLOCK), F32)] * (2 * MB_PAIR)
            + [pltpu.VMEM((MB_DH, MB_BLOCK), F32)] * MB_PAIR
            + [pltpu.VMEM((nb, MB_BLOCK), F32)] * MB_PAIR
        ),
        compiler_params=_cparams(("parallel", "parallel", "arbitrary")),
        name="moba_attn",
    )(pqk, pqk, vt, km, bias)


def _t5_bucket(dist):
    max_exact = REL_BUCKETS // 2
    scaled = jnp.log(jnp.maximum(dist, 1).astype(F32) / max_exact) / math.log(REL_MAX_DIST / max_exact)
    large = jnp.minimum(max_exact + (scaled * (REL_BUCKETS - max_exact)).astype(I32), REL_BUCKETS - 1)
    return jnp.where(dist < max_exact, dist, large)


def moba_bias_tiles(rel_bias):
    blk = MB_BLOCK
    span = 2 * blk - 1
    x = jnp.arange(span) - (blk - 1)
    dist = jnp.maximum(jnp.arange(MB_BIAS_TILES)[:, None] * blk + x[None, :], 0)
    w = rel_bias.astype(F32).T[:, _t5_bucket(dist)]
    h = w.shape[0]
    wp = jnp.pad(w, ((0, 0), (0, 0), (0, 1)))
    a = jnp.broadcast_to(wp[:, :, None, :], (h, MB_BIAS_TILES, blk, span + 1))
    a = a.reshape(h, MB_BIAS_TILES, blk * (span + 1))[:, :, :blk * span]
    return a.reshape(h, MB_BIAS_TILES, blk, span)[:, :, :, blk - 1:]


def _mix_kernel(x_ref, ya_ref, yb_ref, ga_ref, gb_ref, wa_ref, wb_ref, wo_ref, o_ref):
    za = jnp.dot(ya_ref[...], wa_ref[...], preferred_element_type=F32)
    zb = jnp.dot(yb_ref[...], wb_ref[...], preferred_element_type=F32)
    z = jax.nn.sigmoid(ga_ref[...].astype(F32)) * za + jax.nn.sigmoid(gb_ref[...].astype(F32)) * zb
    o_ref[...] = x_ref[...] + jnp.dot(z.astype(BF16), wo_ref[...], preferred_element_type=F32)


def mix_out(x2d, ya, yb, pg, wa, wb, wo, tm=256):
    t, d = x2d.shape
    w = ya.shape[1]
    return pl.pallas_call(
        _mix_kernel,
        grid=(t // tm,),
        in_specs=[
            pl.BlockSpec((tm, d), lambda i: (i, 0)),
            pl.BlockSpec((tm, w), lambda i: (i, 0)),
            pl.BlockSpec((tm, w), lambda i: (i, 0)),
            pl.BlockSpec((tm, d), lambda i: (i, 0)),
            pl.BlockSpec((tm, d), lambda i: (i, 1)),
            pl.BlockSpec((w, d), lambda i: (0, 0)),
            pl.BlockSpec((w, d), lambda i: (0, 0)),
            pl.BlockSpec((d, d), lambda i: (0, 0)),
        ],
        out_specs=pl.BlockSpec((tm, d), lambda i: (i, 0)),
        out_shape=jax.ShapeDtypeStruct((t, d), F32),
        compiler_params=_cparams(("parallel",)),
        name="mix_out",
    )(x2d, ya, yb, pg, pg, wa, wb, wo)


def _mem_kv_kernel(m_ref, g_ref, wk_ref, wv_ref, k_ref, v_ref):
    mn = _rms(m_ref[...], g_ref[...]).astype(BF16)
    k_ref[...] = jnp.dot(mn, wk_ref[...], preferred_element_type=F32).astype(BF16)
    v_ref[...] = jnp.dot(mn, wv_ref[...], preferred_element_type=F32).astype(BF16)


def mem_kv(mem, g, wk, wv):
    b, m, d = mem.shape
    spec = pl.BlockSpec((None, m, d), lambda i: (i, 0, 0))
    wspec = pl.BlockSpec((d, d), lambda i: (0, 0))
    return pl.pallas_call(
        _mem_kv_kernel,
        grid=(b,),
        in_specs=[spec, pl.BlockSpec((1, d), lambda i: (0, 0)), wspec, wspec],
        out_specs=[spec, spec],
        out_shape=[jax.ShapeDtypeStruct((b, m, d), BF16)] * 2,
        compiler_params=_cparams(("parallel",)),
        name="mem_kv",
    )(mem, g, wk, wv)


def _cross_kernel(x_ref, g_ref, wq_ref, k_ref, v_ref, wo_ref, o_ref):
    x = x_ref[...]
    d = x.shape[1]
    dh = d // X_HEADS
    h = _rms(x, g_ref[...]).astype(BF16)
    q = (jnp.dot(h, wq_ref[...], preferred_element_type=F32) * (dh ** -0.5)).astype(BF16)
    outs = []
    for hh in range(X_HEADS):
        sl = slice(hh * dh, (hh + 1) * dh)
        s = lax.dot_general(q[:, sl], k_ref[:, sl], (((1,), (1,)), ((), ())),
                            preferred_element_type=F32)
        p = jnp.exp(s - jnp.max(s, axis=1, keepdims=True))
        l = jnp.sum(p, axis=1, keepdims=True)
        o = jnp.dot(p.astype(BF16), v_ref[:, sl], preferred_element_type=F32) / l
        outs.append(o.astype(BF16))
    o = jnp.concatenate(outs, axis=1)
    o_ref[...] = x + jnp.dot(o, wo_ref[...], preferred_element_type=F32)


def cross_attn(x2d, g, wq, kx, vx, wo, seq, tm=256):
    t, d = x2d.shape
    m = kx.shape[1]
    per_b = seq // tm
    kv = pl.BlockSpec((None, m, d), lambda i: (i // per_b, 0, 0))
    wspec = pl.BlockSpec((d, d), lambda i: (0, 0))
    return pl.pallas_call(
        _cross_kernel,
        grid=(t // tm,),
        in_specs=[pl.BlockSpec((tm, d), lambda i: (i, 0)), pl.BlockSpec((1, d), lambda i: (0, 0)),
                  wspec, kv, kv, wspec],
        out_specs=pl.BlockSpec((tm, d), lambda i: (i, 0)),
        out_shape=jax.ShapeDtypeStruct((t, d), F32),
        compiler_params=_cparams(("parallel",)),
        name="cross_attn",
    )(x2d, g, wq, kx, vx, wo)


def _topk_rows(sc, k):
    n = sc.shape[0]
    io = lax.broadcasted_iota(I32, sc.shape, 0)
    vals, ids = [], []
    for _ in range(k):
        m = jnp.max(sc, axis=0, keepdims=True)
        ix = jnp.min(jnp.where(sc == m, io, n), axis=0, keepdims=True)
        vals.append(m)
        ids.append(ix)
        sc = jnp.where(io == ix, NEG_INF, sc)
    return jnp.concatenate(vals, axis=0), jnp.concatenate(ids, axis=0)


def _route_kernel(x_ref, g_ref, wq_ref, sk_ref, h_ref, idx_ref, w_ref, hb_ref):
    p = pl.program_id(1)

    @pl.when(p == 0)
    def _():
        h = _rms(x_ref[...], g_ref[...])
        h_ref[...] = h
        hb_ref[...] = h.astype(BF16)

    qh = jnp.dot(hb_ref[...], wq_ref[...], preferred_element_type=F32)
    tops = []
    for c in range(2):
        seg = qh[:, c * PEER_HALF:(c + 1) * PEER_HALF]
        sc = lax.dot_general(sk_ref[c], seg, (((1,), (1,)), ((), ())),
                             precision=lax.Precision.HIGHEST, preferred_element_type=F32)
        tops.append(_topk_rows(sc, PEER_TOPK))
    (s0, i0), (s1, i1) = tops
    cand = jnp.concatenate([s0[a:a + 1] + s1 for a in range(PEER_TOPK)], axis=0)
    cidx = jnp.concatenate([i0[a:a + 1] * PEER_NKEYS + i1 for a in range(PEER_TOPK)], axis=0)
    io = lax.broadcasted_iota(I32, cand.shape, 0)
    n = cand.shape[0]
    vals, ids = [], []
    for _ in range(PEER_TOPK):
        m = jnp.max(cand, axis=0, keepdims=True)
        px = jnp.min(jnp.where(cand == m, io, n), axis=0, keepdims=True)
        hit = io == px
        vals.append(m)
        ids.append(jnp.sum(jnp.where(hit, cidx, 0), axis=0, keepdims=True))
        cand = jnp.where(hit, NEG_INF, cand)
    sf = jnp.concatenate(vals, axis=0)
    e = jnp.exp(sf - sf[0:1])
    w_ref[...] = e / jnp.sum(e, axis=0, keepdims=True)
    idx_ref[...] = jnp.concatenate(ids, axis=0)


def peer_route(x2d, g, wq, sk, tm=256):
    t, d = x2d.shape
    ph = sk.shape[0]
    return pl.pallas_call(
        _route_kernel,
        grid=(t // tm, ph),
        in_specs=[
            pl.BlockSpec((tm, d), lambda i, p: (i, 0)),
            pl.BlockSpec((1, d), lambda i, p: (0, 0)),
            pl.BlockSpec((d, 2 * PEER_HALF), lambda i, p: (0, p)),
            pl.BlockSpec((None, 2, PEER_NKEYS, PEER_HALF), lambda i, p: (p, 0, 0, 0)),
        ],
        out_specs=[
            pl.BlockSpec((tm, d), lambda i, p: (i, 0)),
            pl.BlockSpec((None, PEER_TOPK, tm), lambda i, p: (p, 0, i)),
            pl.BlockSpec((None, PEER_TOPK, tm), lambda i, p: (p, 0, i)),
        ],
        out_shape=[jax.ShapeDtypeStruct((t, d), F32),
                   jax.ShapeDtypeStruct((ph, PEER_TOPK, t), I32),
                   jax.ShapeDtypeStruct((ph, PEER_TOPK, t), F32)],
        scratch_shapes=[pltpu.VMEM((tm, d), BF16)],
        compiler_params=_cparams(("parallel", "arbitrary")),
        name="peer_route",
    )(x2d, g, wq, sk)


def _coef_kernel(w_ref, a_ref, o_ref):
    o_ref[...] = w_ref[...] * jax.nn.gelu(a_ref[...])


def peer_coef(w, act, tm=1024):
    t, n = w.shape
    spec = pl.BlockSpec((tm, n), lambda i: (i, 0))
    return pl.pallas_call(
        _coef_kernel, grid=(t // tm,), in_specs=[spec, spec], out_specs=spec,
        out_shape=jax.ShapeDtypeStruct((t, n), F32),
        compiler_params=_cparams(("parallel",)), name="peer_coef",
    )(w, act)


def _final_kernel(x_ref, y_ref, g_ref, o_ref):
    o_ref[...] = _rms(x_ref[...] + y_ref[...], g_ref[...])


def final_norm(x2d, y, g, tm=512):
    t, d = x2d.shape
    spec = pl.BlockSpec((tm, d), lambda i: (i, 0))
    return pl.pallas_call(
        _final_kernel, grid=(t // tm,),
        in_specs=[spec, spec, pl.BlockSpec((1, d), lambda i: (0, 0))], out_specs=spec,
        out_shape=jax.ShapeDtypeStruct((t, d), F32),
        compiler_params=_cparams(("parallel",)), name="final_norm",
    )(x2d, y, g)


SC_CORES = 2
SC_SUBCORES = 16
SC_WORKERS = SC_CORES * SC_SUBCORES
SC_LANES = 16
SC_GROUP = 16


def _sc_mesh():
    return plsc.VectorSubcoreMesh(core_axis_name="c", subcore_axis_name="s")


def _sc_params():
    return pltpu.CompilerParams(needs_layout_passes=False)


def _sc_worker_id():
    return lax.axis_index("s") * SC_CORES + lax.axis_index("c")


SC_RING = 4
SC_ROW_SUB = 8
SC_ROW_LANE = 128


def _sc_ring(n_units, start, wait, compute):
    for u in range(SC_RING - 1):
        start(u, u)

    @pl.loop(0, n_units, step=SC_RING)
    def _(uu):
        for b in range(SC_RING):
            u = uu + b
            nxt = u + (SC_RING - 1)

            @pl.when(nxt < n_units)
            def _():
                start(nxt, (b + SC_RING - 1) % SC_RING)

            wait(u, b)
            compute(u, b)


def _sc_unit_off(u):
    off = u * SC_LANES
    return off if isinstance(off, int) else pl.multiple_of(off, SC_LANES)


def _sc_row_piece(rows, r, c):
    per = SC_ROW_LANE // SC_LANES
    return rows[r, c // per, pl.ds(pl.multiple_of((c % per) * SC_LANES, SC_LANES), SC_LANES)]


def peer_dots_sc(table, idx_flat, h):
    t, d = h.shape
    nsel = PEER_SEL
    tpw = t // SC_WORKERS
    g = SC_GROUP
    groups = tpw // g
    heads = nsel // SC_LANES
    pieces = d // SC_LANES
    units = g * heads
    row_buf = pltpu.VMEM((SC_LANES, SC_ROW_SUB, SC_ROW_LANE), F32)

    @functools.partial(
        pl.kernel, mesh=_sc_mesh(),
        out_type=jax.ShapeDtypeStruct((t * nsel,), F32),
        scratch_types=[
            pltpu.VMEM((g * nsel,), I32),
            pltpu.VMEM((g, d), F32),
            pltpu.VMEM((g * nsel,), F32),
            pltpu.VMEM((SC_LANES * SC_LANES,), F32),
            [row_buf] * SC_RING,
            [pltpu.SemaphoreType.DMA] * SC_RING,
        ],
        compiler_params=_sc_params(),
        name="peer_dots_sc",
    )
    def k(tab_hbm, idx_hbm, h_hbm, out_hbm, idx_v, h_v, out_v, red_v, rows, sems):
        wid = _sc_worker_id()
        lane = lax.iota(I32, SC_LANES)

        def copy(u, slot):
            ids = idx_v.at[pl.ds(_sc_unit_off(u), SC_LANES)]
            return pltpu.make_async_copy(tab_hbm.at[ids], rows[slot], sems[slot])

        def compute(u, slot):
            tt = u // heads

            def body(c, accs):
                hv = h_v[tt, pl.ds(pl.multiple_of(c * SC_LANES, SC_LANES), SC_LANES)]
                return tuple(accs[r] + _sc_row_piece(rows[slot], r, c) * hv for r in range(SC_LANES))

            accs = lax.fori_loop(0, pieces, body,
                                 tuple(jnp.zeros((SC_LANES,), F32) for _ in range(SC_LANES)))
            for r in range(SC_LANES):
                red_v[pl.ds(r * SC_LANES, SC_LANES)] = accs[r]
            cols = [plsc.load_gather(red_v, [lane * SC_LANES + j]) for j in range(SC_LANES)]
            while len(cols) > 1:
                cols = [cols[i] + cols[i + 1] for i in range(0, len(cols), 2)]
            out_v[pl.ds(_sc_unit_off(u), SC_LANES)] = cols[0]

        @pl.loop(0, groups)
        def _(gi):
            base = wid * tpw + gi * g
            pltpu.sync_copy(idx_hbm.at[pl.ds(base * nsel, g * nsel)], idx_v)
            pltpu.sync_copy(h_hbm.at[pl.ds(base, g)], h_v)
            _sc_ring(units, lambda u, s: copy(u, s).start(), lambda u, s: copy(u, s).wait(), compute)
            pltpu.sync_copy(out_v, out_hbm.at[pl.ds(base * nsel, g * nsel)])

    return k(table, idx_flat, h)


def peer_combine_sc(table, idx_flat, coef_flat, t):
    d = table.shape[1] * table.shape[2]
    nsel = PEER_SEL
    tpw = t // SC_WORKERS
    g = SC_GROUP
    groups = tpw // g
    heads = nsel // SC_LANES
    pieces = d // SC_LANES
    units = g * heads
    row_buf = pltpu.VMEM((SC_LANES, SC_ROW_SUB, SC_ROW_LANE), F32)

    @functools.partial(
        pl.kernel, mesh=_sc_mesh(),
        out_type=jax.ShapeDtypeStruct((t, d), F32),
        scratch_types=[
            pltpu.VMEM((g * nsel,), I32),
            pltpu.VMEM((g * nsel,), F32),
            pltpu.VMEM((g, d), F32),
            [row_buf] * SC_RING,
            [pltpu.SemaphoreType.DMA] * SC_RING,
        ],
        compiler_params=_sc_params(),
        name="peer_combine_sc",
    )
    def k(tab_hbm, idx_hbm, coef_hbm, out_hbm, idx_v, coef_v, y_v, rows, sems):
        wid = _sc_worker_id()

        def copy(u, slot):
            ids = idx_v.at[pl.ds(_sc_unit_off(u), SC_LANES)]
            return pltpu.make_async_copy(tab_hbm.at[ids], rows[slot], sems[slot])

        def compute(u, slot):
            tt = u // heads
            first = (u % heads) == 0
            cs = [plsc.load_gather(coef_v, [jnp.full((SC_LANES,), u * SC_LANES + r, I32)])
                  for r in range(SC_LANES)]

            @pl.loop(0, pieces)
            def _(c):
                off = pl.multiple_of(c * SC_LANES, SC_LANES)
                terms = [cs[r] * _sc_row_piece(rows[slot], r, c) for r in range(SC_LANES)]
                while len(terms) > 1:
                    terms = [terms[i] + terms[i + 1] for i in range(0, len(terms), 2)]
                prev = y_v[tt, pl.ds(off, SC_LANES)]
                y_v[tt, pl.ds(off, SC_LANES)] = terms[0] + jnp.where(first, 0.0, prev)

        @pl.loop(0, groups)
        def _(gi):
            base = wid * tpw + gi * g
            pltpu.sync_copy(idx_hbm.at[pl.ds(base * nsel, g * nsel)], idx_v)
            pltpu.sync_copy(coef_hbm.at[pl.ds(base * nsel, g * nsel)], coef_v)
            _sc_ring(units, lambda u, s: copy(u, s).start(), lambda u, s: copy(u, s).wait(), compute)
            pltpu.sync_copy(y_v, out_hbm.at[pl.ds(base, g)])

    return k(table, idx_flat, coef_flat)


def kernel(x, mem, rel_bias, ln_mix, w_in, hg_lower, hg_norm, w_up_a, w_up_b, w_out, ln_cross, ln_mem, wq_x, wk_x, wv_x, wo_x, ln_ffn, peer_query, peer_subkeys, peer_u, peer_v, ln_final):
    b, s, d = x.shape
    t = b * s
    depth = w_in.shape[0]
    assert depth == 1, "the residual after PEER is fused into the final norm"
    assert s % MB_BLOCK == 0 and s % HG_CHUNK == 0 and t % (SC_WORKERS * SC_GROUP) == 0
    nb = s // MB_BLOCK
    row = lambda a: a.reshape(1, -1).astype(F32)
    lb_all = jnp.cumsum(jax.nn.softmax(hg_lower.astype(F32), axis=0), axis=0)
    bias = moba_bias_tiles(rel_bias)
    n_hg = 4 * HG_WIDTH
    n_mb = 3 * MB_WIDTH
    x2d = x.reshape(t, d)
    for l in range(depth):
        w = w_in[l].astype(BF16)
        n_qk = 2 * MB_WIDTH
        p0, pqk, vt, pg = in_proj(x2d, row(ln_mix[l]), w[:, :n_hg], w[:, n_hg:n_hg + n_qk],
                                  w[:, n_hg + n_qk:n_hg + n_mb].T, w[:, n_hg + n_mb:])
        ya = hgrn2(p0, row(lb_all[l]), row(hg_norm[l]), b, s)
        km = moba_kmean(pqk, b, s).reshape(b, nb, MB_WIDTH)
        yb = moba_attention(pqk, vt, km, bias, b, s)
        x2d = mix_out(x2d, ya, yb, pg, w_up_a[l].astype(BF16), w_up_b[l].astype(BF16), w_out[l].astype(BF16))
        kx, vx = mem_kv(mem, row(ln_mem[l]), wk_x[l].astype(BF16), wv_x[l].astype(BF16))
        x2d = cross_attn(x2d, row(ln_cross[l]), wq_x[l].astype(BF16), kx, vx, wo_x[l].astype(BF16), s)
        hf, eidx, wts = peer_route(x2d, row(ln_ffn[l]), peer_query[l].astype(BF16), peer_subkeys[l].astype(F32))
        idx_flat = eidx.reshape(PEER_SEL, t).T.reshape(t * PEER_SEL)
        wts2d = wts.reshape(PEER_SEL, t).T
        tab3 = lambda a: a.astype(F32).reshape(a.shape[0], SC_ROW_SUB, SC_ROW_LANE)
        act = peer_dots_sc(tab3(peer_u[l]), idx_flat, hf).reshape(t, PEER_SEL)
        coef = peer_coef(wts2d, act)
        y = peer_combine_sc(tab3(peer_v[l]), idx_flat, coef.reshape(t * PEER_SEL), t)
    return final_norm(x2d, y, row(ln_final)).reshape(b, s, d)
```

```python
import functools
import math

import jax
import jax.numpy as jnp
import numpy as np
from jax import lax
from jax.experimental import pallas as pl
from jax.experimental.pallas import tpu as pltpu
from jax.experimental.pallas import tpu_sc as plsc

F32 = jnp.float32
BF16 = jnp.bfloat16
I32 = jnp.int32
EPS = 1e-6
NEG_INF = float("-inf")

HG_HEADS = 4
HG_D = 128
HG_WIDTH = HG_HEADS * HG_D
HG_CHUNK = 64
HG_SUB = 16
MB_HEADS = 8
MB_DH = 64
MB_WIDTH = MB_HEADS * MB_DH
MB_BLOCK = 256
MB_TOPK = 3
MB_BIAS_TILES = 8
REL_BUCKETS = 32
REL_MAX_DIST = 2048
X_HEADS = 4
PEER_HEADS = 8
PEER_NKEYS = 128
PEER_TOPK = 16
PEER_HALF = 128
PEER_SEL = PEER_HEADS * PEER_TOPK

VMEM_LIMIT = 56 * 1024 * 1024


def _cparams(sem):
    return pltpu.CompilerParams(dimension_semantics=sem, vmem_limit_bytes=VMEM_LIMIT)


def _rms(x, g):
    ms = jnp.mean(x * x, axis=-1, keepdims=True)
    return x * lax.rsqrt(ms + EPS) * g


def _in_proj_kernel(x_ref, g_ref, w0_ref, w1_ref, wvt_ref, w2_ref, o0_ref, o1_ref, ovt_ref, o2_ref):
    h = _rms(x_ref[...], g_ref[...]).astype(BF16)
    o0_ref[...] = jnp.dot(h, w0_ref[...], preferred_element_type=F32)
    o1_ref[...] = jnp.dot(h, w1_ref[...], preferred_element_type=F32).astype(BF16)
    ovt_ref[0] = lax.dot_general(wvt_ref[...], h, (((1,), (1,)), ((), ())),
                                 preferred_element_type=F32).astype(BF16)
    o2_ref[...] = jnp.dot(h, w2_ref[...], preferred_element_type=F32).astype(BF16)


def in_proj(x2d, g, w0, w1, wvt, w2):
    t, d = x2d.shape
    tm = MB_BLOCK
    n0, n1, nv, n2 = w0.shape[1], w1.shape[1], wvt.shape[0], w2.shape[1]
    full = lambda a: pl.BlockSpec(a.shape, lambda i: (0, 0))
    return pl.pallas_call(
        _in_proj_kernel,
        grid=(t // tm,),
        in_specs=[pl.BlockSpec((tm, d), lambda i: (i, 0)), full(g), full(w0), full(w1), full(wvt), full(w2)],
        out_specs=[pl.BlockSpec((tm, n0), lambda i: (i, 0)),
                   pl.BlockSpec((tm, n1), lambda i: (i, 0)),
                   pl.BlockSpec((1, nv, tm), lambda i: (i, 0, 0)),
                   pl.BlockSpec((tm, n2), lambda i: (i, 0))],
        out_shape=[jax.ShapeDtypeStruct((t, n0), F32),
                   jax.ShapeDtypeStruct((t, n1), BF16),
                   jax.ShapeDtypeStruct((t // tm, nv, tm), BF16),
                   jax.ShapeDtypeStruct((t, n2), BF16)],
        compiler_params=_cparams(("parallel",)),
        name="in_proj",
    )(x2d, g, w0, w1, wvt, w2)


def _hgrn_kernel(q_ref, f_ref, i_ref, g_ref, lb_ref, gain_ref, o_ref, st_ref):
    c = pl.program_id(1)

    @pl.when(c == 0)
    def _():
        st_ref[...] = jnp.zeros_like(st_ref)

    C, S = HG_CHUNK, HG_SUB
    row = lax.broadcasted_iota(I32, (C, C), 0)
    col = lax.broadcasted_iota(I32, (C, C), 1)
    tril = (row >= col).astype(F32)
    t_iota = lax.broadcasted_iota(I32, (S, 1), 0)

    for h in range(HG_HEADS):
        sl = slice(h * HG_D, (h + 1) * HG_D)
        q = q_ref[:, sl]
        v = i_ref[:, sl]
        lb = lb_ref[:, sl]
        f = lb + (1.0 - lb) * jax.nn.sigmoid(f_ref[:, sl])
        lf = jnp.log(f)
        k = 1.0 - f
        b = jnp.dot(tril, lf, precision=lax.Precision.HIGHEST, preferred_element_type=F32)
        st = st_ref[h]
        vb = v.astype(BF16)
        qd = (q * jnp.exp(b)).astype(BF16)
        o_inter = lax.dot_general(qd, st.astype(BF16), (((1,), (1,)), ((), ())),
                                  preferred_element_type=F32)
        outs = []
        for i in range(C // S):
            r0 = i * S
            qi = q[r0:r0 + S]
            ki = k[r0:r0 + S]
            bi = b[r0:r0 + S]
            vi = v[r0:r0 + S]
            oi = o_inter[r0:r0 + S]
            if i > 0:
                bs = b[r0 - 1:r0]
                qh = (qi * jnp.exp(bi - bs)).astype(BF16)
                kh = (k[:r0] * jnp.exp(bs - b[:r0])).astype(BF16)
                a = lax.dot_general(qh, kh, (((1,), (1,)), ((), ())), preferred_element_type=F32)
                oi = oi + jnp.dot(a.astype(BF16), vb[:r0], preferred_element_type=F32)
            for s in range(S):
                dec = jnp.exp(jnp.minimum(bi - bi[s:s + 1], 0.0))
                p = qi * ki[s:s + 1] * dec
                a_s = jnp.sum(p, axis=-1, keepdims=True)
                a_s = jnp.where(t_iota >= s, a_s, 0.0)
                oi = oi + a_s * vi[s:s + 1]
            outs.append(oi)
        o = jnp.concatenate(outs, axis=0)
        b_end = b[C - 1:C]
        kd = (k * jnp.exp(b_end - b)).astype(BF16)
        upd = lax.dot_general(vb, kd, (((0,), (0,)), ((), ())), preferred_element_type=F32)
        st_ref[h] = st * jnp.exp(b_end) + upd
        o = o * lax.rsqrt(jnp.mean(o * o, axis=-1, keepdims=True) + EPS)
        g = g_ref[:, sl]
        o_ref[:, sl] = (o * gain_ref[:, sl] * (g * jax.nn.sigmoid(g))).astype(o_ref.dtype)


def hgrn2(p0, lb, gain, batch, seq):
    t = p0.shape[0]
    nc = seq // HG_CHUNK
    w = HG_WIDTH

    def col(j):
        return pl.BlockSpec((HG_CHUNK, w), lambda b, c, j=j: (b * nc + c, j))

    return pl.pallas_call(
        _hgrn_kernel,
        grid=(batch, nc),
        in_specs=[col(0), col(1), col(2), col(3),
                  pl.BlockSpec((1, w), lambda b, c: (0, 0)),
                  pl.BlockSpec((1, w), lambda b, c: (0, 0))],
        out_specs=pl.BlockSpec((HG_CHUNK, w), lambda b, c: (b * nc + c, 0)),
        out_shape=jax.ShapeDtypeStruct((t, w), BF16),
        scratch_shapes=[pltpu.VMEM((HG_HEADS, HG_D, HG_D), F32)],
        compiler_params=_cparams(("parallel", "arbitrary")),
        name="hgrn2",
    )(p0, p0, p0, p0, lb, gain)


def _kmean_kernel(k_ref, o_ref):
    o_ref[0] = jnp.mean(k_ref[...].astype(F32), axis=0, keepdims=True)


def moba_kmean(p1, batch, seq):
    nbt = p1.shape[0] // MB_BLOCK
    return pl.pallas_call(
        _kmean_kernel,
        grid=(nbt,),
        in_specs=[pl.BlockSpec((MB_BLOCK, MB_WIDTH), lambda i: (i, 1))],
        out_specs=pl.BlockSpec((1, 1, MB_WIDTH), lambda i: (i, 0, 0)),
        out_shape=jax.ShapeDtypeStruct((nbt, 1, MB_WIDTH), F32),
        compiler_params=_cparams(("parallel",)),
        name="moba_kmean",
    )(p1)


MB_PAIR = 4
MB_PW = MB_PAIR * MB_DH
MB_LG = 128


def _moba_kernel(q_ref, k_ref, vt_ref, km_ref, bias_ref, o_ref, *scratch):
    m_ref, l_ref, acc_ref, msk_ref = (scratch[i * MB_PAIR:(i + 1) * MB_PAIR] for i in range(4))
    qi = pl.program_id(2)
    nb = km_ref.shape[0]
    blk = MB_BLOCK
    heads = range(MB_PAIR)
    grp = lambda hh: slice((hh // 2) * MB_LG, (hh // 2 + 1) * MB_LG)
    q = q_ref[...]
    lane = lax.broadcasted_iota(I32, (blk, MB_LG), 1)
    in_head = [(lane < MB_DH) if hh % 2 == 0 else (lane >= MB_DH) for hh in heads]
    qs = q * jnp.asarray(MB_DH ** -0.5, BF16)
    qh = [jnp.where(in_head[hh], qs[:, grp(hh)], jnp.zeros((blk, MB_LG), BF16)) for hh in heads]
    nt = (((1,), (1,)), ((), ()))

    qf = q.astype(F32)
    n_io = lax.broadcasted_iota(I32, (nb, blk), 0)
    for hh in heads:
        gate = lax.dot_general(km_ref[:, grp(hh)], jnp.where(in_head[hh], qf[:, grp(hh)], 0.0), nt,
                               precision=lax.Precision.HIGHEST, preferred_element_type=F32)
        gate = jnp.where(n_io < qi, gate, NEG_INF)
        chosen = n_io < 0
        for _ in range(MB_TOPK):
            mx = jnp.max(gate, axis=0, keepdims=True)
            ix = jnp.min(jnp.where(gate == mx, n_io, nb), axis=0, keepdims=True)
            hit = n_io == ix
            chosen = chosen | (hit & (mx > NEG_INF))
            gate = jnp.where(hit, NEG_INF, gate)
        msk_ref[hh][...] = jnp.where(chosen, 0.0, NEG_INF)

    k_own = k_ref[pl.ds(pl.multiple_of(qi * blk, blk), blk), :]
    vt_own = vt_ref[qi]
    key_io = lax.broadcasted_iota(I32, (blk, blk), 0)
    qry_io = lax.broadcasted_iota(I32, (blk, blk), 1)
    own_rows = lambda r, hh: r[(hh % 2) * MB_DH:(hh % 2 + 1) * MB_DH]
    for hh in heads:
        s = lax.dot_general(k_own[:, grp(hh)], qh[hh], nt, preferred_element_type=F32) + bias_ref[hh, 0]
        s = jnp.where(key_io <= qry_io, s, NEG_INF)
        m0 = jnp.max(s, axis=0, keepdims=True)
        p = jnp.exp(s - m0)
        m_ref[hh][...] = m0
        l_ref[hh][...] = jnp.sum(p, axis=0, keepdims=True)
        r = jnp.dot(vt_own[grp(hh)], p.astype(BF16), preferred_element_type=F32)
        acc_ref[hh][...] = own_rows(r, hh)

    def past(n, carry):
        kn = k_ref[pl.ds(pl.multiple_of(n * blk, blk), blk), :]
        vtn = vt_ref[n]
        d = jnp.minimum(qi - n, MB_BIAS_TILES - 1)
        s = [lax.dot_general(kn[:, grp(hh)], qh[hh], nt, preferred_element_type=F32)
             + bias_ref[hh, d] + msk_ref[hh][pl.ds(n, 1), :] for hh in heads]
        m_old = [m_ref[hh][...] for hh in heads]
        l_old = [l_ref[hh][...] for hh in heads]
        a_old = [acc_ref[hh][...] for hh in heads]
        m_new = [jnp.maximum(m_old[hh], jnp.max(s[hh], axis=0, keepdims=True)) for hh in heads]
        alpha = [jnp.exp(m_old[hh] - m_new[hh]) for hh in heads]
        p = [jnp.exp(s[hh] - m_new[hh]) for hh in heads]
        r = [jnp.dot(vtn[grp(hh)], p[hh].astype(BF16), preferred_element_type=F32) for hh in heads]
        l_new = [alpha[hh] * l_old[hh] + jnp.sum(p[hh], axis=0, keepdims=True) for hh in heads]
        a_new = [alpha[hh] * a_old[hh] + own_rows(r[hh], hh) for hh in heads]
        for hh in heads:
            m_ref[hh][...] = m_new[hh]
            l_ref[hh][...] = l_new[hh]
            acc_ref[hh][...] = a_new[hh]
        return carry

    lax.fori_loop(0, qi, past, 0)
    out_t = jnp.concatenate([acc_ref[hh][...] / l_ref[hh][...] for hh in heads], axis=0)
    o_ref[...] = out_t.T.astype(o_ref.dtype)


def moba_attention(pqk, vt, km, bias, batch, seq):
    t = pqk.shape[0]
    nb = seq // MB_BLOCK
    groups = MB_WIDTH // MB_PW
    return pl.pallas_call(
        _moba_kernel,
        grid=(batch, groups, nb),
        in_specs=[
            pl.BlockSpec((MB_BLOCK, MB_PW), lambda b, j, i: (b * nb + i, j)),
            pl.BlockSpec((seq, MB_PW), lambda b, j, i: (b, groups + j)),
            pl.BlockSpec((nb, MB_PW, MB_BLOCK), lambda b, j, i: (b, j, 0)),
            pl.BlockSpec((None, nb, MB_PW), lambda b, j, i: (b, 0, j)),
            pl.BlockSpec((MB_PAIR, MB_BIAS_TILES, MB_BLOCK, MB_BLOCK), lambda b, j, i: (j, 0, 0, 0)),
        ],
        out_specs=pl.BlockSpec((MB_BLOCK, MB_PW), lambda b, j, i: (b * nb + i, j)),
        out_shape=jax.ShapeDtypeStruct((t, MB_WIDTH), BF16),
        scratch_shapes=(
            [pltpu.VMEM((1, MB_BLOCK), F32)] * (2 * MB_PAIR)
            + [pltpu.VMEM((MB_DH, MB_BLOCK), F32)] * MB_PAIR
            + [pltpu.VMEM((nb, MB_BLOCK), F32)] * MB_PAIR
        ),
        compiler_params=_cparams(("parallel", "parallel", "arbitrary")),
        name="moba_attn",
    )(pqk, pqk, vt, km, bias)


def _t5_bucket(dist):
    max_exact = REL_BUCKETS // 2
    scaled = jnp.log(jnp.maximum(dist, 1).astype(F32) / max_exact) / math.log(REL_MAX_DIST / max_exact)
    large = jnp.minimum(max_exact + (scaled * (REL_BUCKETS - max_exact)).astype(I32), REL_BUCKETS - 1)
    return jnp.where(dist < max_exact, dist, large)


def moba_bias_tiles(rel_bias):
    blk = MB_BLOCK
    span = 2 * blk - 1
    x = jnp.arange(span) - (blk - 1)
    dist = jnp.maximum(jnp.arange(MB_BIAS_TILES)[:, None] * blk + x[None, :], 0)
    w = rel_bias.astype(F32).T[:, _t5_bucket(dist)]
    h = w.shape[0]
    wp = jnp.pad(w, ((0, 0), (0, 0), (0, 1)))
    a = jnp.broadcast_to(wp[:, :, None, :], (h, MB_BIAS_TILES, blk, span + 1))
    a = a.reshape(h, MB_BIAS_TILES, blk * (span + 1))[:, :, :blk * span]
    return a.reshape(h, MB_BIAS_TILES, blk, span)[:, :, :, blk - 1:]


def _mix_kernel(x_ref, ya_ref, yb_ref, ga_ref, gb_ref, wa_ref, wb_ref, wo_ref, o_ref):
    za = jnp.dot(ya_ref[...], wa_ref[...], preferred_element_type=F32)
    zb = jnp.dot(yb_ref[...], wb_ref[...], preferred_element_type=F32)
    z = jax.nn.sigmoid(ga_ref[...].astype(F32)) * za + jax.nn.sigmoid(gb_ref[...].astype(F32)) * zb
    o_ref[...] = x_ref[...] + jnp.dot(z.astype(BF16), wo_ref[...], preferred_element_type=F32)


def mix_out(x2d, ya, yb, pg, wa, wb, wo, tm=256):
    t, d = x2d.shape
    w = ya.shape[1]
    return pl.pallas_call(
        _mix_kernel,
        grid=(t // tm,),
        in_specs=[
            pl.BlockSpec((tm, d), lambda i: (i, 0)),
            pl.BlockSpec((tm, w), lambda i: (i, 0)),
            pl.BlockSpec((tm, w), lambda i: (i, 0)),
            pl.BlockSpec((tm, d), lambda i: (i, 0)),
            pl.BlockSpec((tm, d), lambda i: (i, 1)),
            pl.BlockSpec((w, d), lambda i: (0, 0)),
            pl.BlockSpec((w, d), lambda i: (0, 0)),
            pl.BlockSpec((d, d), lambda i: (0, 0)),
        ],
        out_specs=pl.BlockSpec((tm, d), lambda i: (i, 0)),
        out_shape=jax.ShapeDtypeStruct((t, d), F32),
        compiler_params=_cparams(("parallel",)),
        name="mix_out",
    )(x2d, ya, yb, pg, pg, wa, wb, wo)


def _mem_kv_kernel(m_ref, g_ref, wk_ref, wv_ref, k_ref, v_ref):
    mn = _rms(m_ref[...], g_ref[...]).astype(BF16)
    k_ref[...] = jnp.dot(mn, wk_ref[...], preferred_element_type=F32).astype(BF16)
    v_ref[...] = jnp.dot(mn, wv_ref[...], preferred_element_type=F32).astype(BF16)


def mem_kv(mem, g, wk, wv):
    b, m, d = mem.shape
    spec = pl.BlockSpec((None, m, d), lambda i: (i, 0, 0))
    wspec = pl.BlockSpec((d, d), lambda i: (0, 0))
    return pl.pallas_call(
        _mem_kv_kernel,
        grid=(b,),
        in_specs=[spec, pl.BlockSpec((1, d), lambda i: (0, 0)), wspec, wspec],
        out_specs=[spec, spec],
        out_shape=[jax.ShapeDtypeStruct((b, m, d), BF16)] * 2,
        compiler_params=_cparams(("parallel",)),
        name="mem_kv",
    )(mem, g, wk, wv)


def _cross_kernel(x_ref, g_ref, wq_ref, k_ref, v_ref, wo_ref, o_ref):
    x = x_ref[...]
    d = x.shape[1]
    dh = d // X_HEADS
    h = _rms(x, g_ref[...]).astype(BF16)
    q = (jnp.dot(h, wq_ref[...], preferred_element_type=F32) * (dh ** -0.5)).astype(BF16)
    outs = []
    for hh in range(X_HEADS):
        sl = slice(hh * dh, (hh + 1) * dh)
        s = lax.dot_general(q[:, sl], k_ref[:, sl], (((1,), (1,)), ((), ())),
                            preferred_element_type=F32)
        p = jnp.exp(s - jnp.max(s, axis=1, keepdims=True))
        l = jnp.sum(p, axis=1, keepdims=True)
        o = jnp.dot(p.astype(BF16), v_ref[:, sl], preferred_element_type=F32) / l
        outs.append(o.astype(BF16))
    o = jnp.concatenate(outs, axis=1)
    o_ref[...] = x + jnp.dot(o, wo_ref[...], preferred_element_type=F32)


def cross_attn(x2d, g, wq, kx, vx, wo, seq, tm=256):
    t, d = x2d.shape
    m = kx.shape[1]
    per_b = seq // tm
    kv = pl.BlockSpec((None, m, d), lambda i: (i // per_b, 0, 0))
    wspec = pl.BlockSpec((d, d), lambda i: (0, 0))
    return pl.pallas_call(
        _cross_kernel,
        grid=(t // tm,),
        in_specs=[pl.BlockSpec((tm, d), lambda i: (i, 0)), pl.BlockSpec((1, d), lambda i: (0, 0)),
                  wspec, kv, kv, wspec],
        out_specs=pl.BlockSpec((tm, d), lambda i: (i, 0)),
        out_shape=jax.ShapeDtypeStruct((t, d), F32),
        compiler_params=_cparams(("parallel",)),
        name="cross_attn",
    )(x2d, g, wq, kx, vx, wo)


def _topk_rows(sc, k):
    n = sc.shape[0]
    io = lax.broadcasted_iota(I32, sc.shape, 0)
    vals, ids = [], []
    for _ in range(k):
        m = jnp.max(sc, axis=0, keepdims=True)
        ix = jnp.min(jnp.where(sc == m, io, n), axis=0, keepdims=True)
        vals.append(m)
        ids.append(ix)
        sc = jnp.where(io == ix, NEG_INF, sc)
    return jnp.concatenate(vals, axis=0), jnp.concatenate(ids, axis=0)


def _route_kernel(x_ref, g_ref, wq_ref, sk_ref, h_ref, idx_ref, w_ref, hb_ref):
    p = pl.program_id(1)

    @pl.when(p == 0)
    def _():
        h = _rms(x_ref[...], g_ref[...])
        h_ref[...] = h
        hb_ref[...] = h.astype(BF16)

    qh = jnp.dot(hb_ref[...], wq_ref[...], preferred_element_type=F32)
    tops = []
    for c in range(2):
        seg = qh[:, c * PEER_HALF:(c + 1) * PEER_HALF]
        sc = lax.dot_general(sk_ref[c], seg, (((1,), (1,)), ((), ())),
                             precision=lax.Precision.HIGHEST, preferred_element_type=F32)
        tops.append(_topk_rows(sc, PEER_TOPK))
    (s0, i0), (s1, i1) = tops
    cand = jnp.concatenate([s0[a:a + 1] + s1 for a in range(PEER_TOPK)], axis=0)
    cidx = jnp.concatenate([i0[a:a + 1] * PEER_NKEYS + i1 for a in range(PEER_TOPK)], axis=0)
    io = lax.broadcasted_iota(I32, cand.shape, 0)
    n = cand.shape[0]
    vals, ids = [], []
    for _ in range(PEER_TOPK):
        m = jnp.max(cand, axis=0, keepdims=True)
        px = jnp.min(jnp.where(cand == m, io, n), axis=0, keepdims=True)
        hit = io == px
        vals.append(m)
        ids.append(jnp.sum(jnp.where(hit, cidx, 0), axis=0, keepdims=True))
        cand = jnp.where(hit, NEG_INF, cand)
    sf = jnp.concatenate(vals, axis=0)
    e = jnp.exp(sf - sf[0:1])
    w_ref[...] = e / jnp.sum(e, axis=0, keepdims=True)
    idx_ref[...] = jnp.concatenate(ids, axis=0)


def peer_route(x2d, g, wq, sk, tm=256):
    t, d = x2d.shape
    ph = sk.shape[0]
    return pl.pallas_call(
        _route_kernel,
        grid=(t // tm, ph),
        in_specs=[
            pl.BlockSpec((tm, d), lambda i, p: (i, 0)),
            pl.BlockSpec((1, d), lambda i, p: (0, 0)),
            pl.BlockSpec((d, 2 * PEER_HALF), lambda i, p: (0, p)),
            pl.BlockSpec((None, 2, PEER_NKEYS, PEER_HALF), lambda i, p: (p, 0, 0, 0)),
        ],
        out_specs=[
            pl.BlockSpec((tm, d), lambda i, p: (i, 0)),
            pl.BlockSpec((None, PEER_TOPK, tm), lambda i, p: (p, 0, i)),
            pl.BlockSpec((None, PEER_TOPK, tm), lambda i, p: (p, 0, i)),
        ],
        out_shape=[jax.ShapeDtypeStruct((t, d), F32),
                   jax.ShapeDtypeStruct((ph, PEER_TOPK, t), I32),
                   jax.ShapeDtypeStruct((ph, PEER_TOPK, t), F32)],
        scratch_shapes=[pltpu.VMEM((tm, d), BF16)],
        compiler_params=_cparams(("parallel", "arbitrary")),
        name="peer_route",
    )(x2d, g, wq, sk)


def _coef_kernel(w_ref, a_ref, o_ref):
    o_ref[...] = w_ref[...] * jax.nn.gelu(a_ref[...])


def peer_coef(w, act, tm=1024):
    t, n = w.shape
    spec = pl.BlockSpec((tm, n), lambda i: (i, 0))
    return pl.pallas_call(
        _coef_kernel, grid=(t // tm,), in_specs=[spec, spec], out_specs=spec,
        out_shape=jax.ShapeDtypeStruct((t, n), F32),
        compiler_params=_cparams(("parallel",)), name="peer_coef",
    )(w, act)


def _final_kernel(x_ref, y_ref, g_ref, o_ref):
    o_ref[...] = _rms(x_ref[...] + y_ref[...], g_ref[...])


def final_norm(x2d, y, g, tm=512):
    t, d = x2d.shape
    spec = pl.BlockSpec((tm, d), lambda i: (i, 0))
    return pl.pallas_call(
        _final_kernel, grid=(t // tm,),
        in_specs=[spec, spec, pl.BlockSpec((1, d), lambda i: (0, 0))], out_specs=spec,
        out_shape=jax.ShapeDtypeStruct((t, d), F32),
        compiler_params=_cparams(("parallel",)), name="final_norm",
    )(x2d, y, g)


SC_CORES = 2
SC_SUBCORES = 16
SC_WORKERS = SC_CORES * SC_SUBCORES
SC_LANES = 16
SC_GROUP = 16


def _sc_mesh():
    return plsc.VectorSubcoreMesh(core_axis_name="c", subcore_axis_name="s")


def _sc_params():
    return pltpu.CompilerParams(needs_layout_passes=False)


def _sc_worker_id():
    return lax.axis_index("s") * SC_CORES + lax.axis_index("c")


SC_RING = 4
SC_ROW_SUB = 8
SC_ROW_LANE = 128


def _sc_ring(n_units, start, wait, compute):
    for u in range(SC_RING - 1):
        start(u, u)

    @pl.loop(0, n_units, step=SC_RING)
    def _(uu):
        for b in range(SC_RING):
            u = uu + b
            nxt = u + (SC_RING - 1)

            @pl.when(nxt < n_units)
            def _():
                start(nxt, (b + SC_RING - 1) % SC_RING)

            wait(u, b)
            compute(u, b)


def _sc_unit_off(u):
    off = u * SC_LANES
    return off if isinstance(off, int) else pl.multiple_of(off, SC_LANES)


def _sc_row_piece(rows, r, c):
    per = SC_ROW_LANE // SC_LANES
    return rows[r, c // per, pl.ds(pl.multiple_of((c % per) * SC_LANES, SC_LANES), SC_LANES)]


def peer_dots_sc(table, idx_flat, h):
    t, d = h.shape
    nsel = PEER_SEL
    tpw = t // SC_WORKERS
    g = SC_GROUP
    groups = tpw // g
    heads = nsel // SC_LANES
    pieces = d // SC_LANES
    units = g * heads
    row_buf = pltpu.VMEM((SC_LANES, SC_ROW_SUB, SC_ROW_LANE), F32)

    @functools.partial(
        pl.kernel, mesh=_sc_mesh(),
        out_type=jax.ShapeDtypeStruct((t * nsel,), F32),
        scratch_types=[
            pltpu.VMEM((g * nsel,), I32),
            pltpu.VMEM((g, d), F32),
            pltpu.VMEM((g * nsel,), F32),
            pltpu.VMEM((SC_LANES * SC_LANES,), F32),
            [row_buf] * SC_RING,
            [pltpu.SemaphoreType.DMA] * SC_RING,
        ],
        compiler_params=_sc_params(),
        name="peer_dots_sc",
    )
    def k(tab_hbm, idx_hbm, h_hbm, out_hbm, idx_v, h_v, out_v, red_v, rows, sems):
        wid = _sc_worker_id()
        lane = lax.iota(I32, SC_LANES)

        def copy(u, slot):
            ids = idx_v.at[pl.ds(_sc_unit_off(u), SC_LANES)]
            return pltpu.make_async_copy(tab_hbm.at[ids], rows[slot], sems[slot])

        def compute(u, slot):
            tt = u // heads

            def body(c, accs):
                hv = h_v[tt, pl.ds(pl.multiple_of(c * SC_LANES, SC_LANES), SC_LANES)]
                return tuple(accs[r] + _sc_row_piece(rows[slot], r, c) * hv for r in range(SC_LANES))

            accs = lax.fori_loop(0, pieces, body,
                                 tuple(jnp.zeros((SC_LANES,), F32) for _ in range(SC_LANES)))
            for r in range(SC_LANES):
                red_v[pl.ds(r * SC_LANES, SC_LANES)] = accs[r]
            cols = [plsc.load_gather(red_v, [lane * SC_LANES + j]) for j in range(SC_LANES)]
            while len(cols) > 1:
                cols = [cols[i] + cols[i + 1] for i in range(0, len(cols), 2)]
            out_v[pl.ds(_sc_unit_off(u), SC_LANES)] = cols[0]

        @pl.loop(0, groups)
        def _(gi):
            base = wid * tpw + gi * g
            pltpu.sync_copy(idx_hbm.at[pl.ds(base * nsel, g * nsel)], idx_v)
            pltpu.sync_copy(h_hbm.at[pl.ds(base, g)], h_v)
            _sc_ring(units, lambda u, s: copy(u, s).start(), lambda u, s: copy(u, s).wait(), compute)
            pltpu.sync_copy(out_v, out_hbm.at[pl.ds(base * nsel, g * nsel)])

    return k(table, idx_flat, h)


def peer_combine_sc(table, idx_flat, coef_flat, t):
    d = table.shape[1] * table.shape[2]
    nsel = PEER_SEL
    tpw = t // SC_WORKERS
    g = SC_GROUP
    groups = tpw // g
    heads = nsel // SC_LANES
    pieces = d // SC_LANES
    units = g * heads
    row_buf = pltpu.VMEM((SC_LANES, SC_ROW_SUB, SC_ROW_LANE), F32)

    @functools.partial(
        pl.kernel, mesh=_sc_mesh(),
        out_type=jax.ShapeDtypeStruct((t, d), F32),
        scratch_types=[
            pltpu.VMEM((g * nsel,), I32),
            pltpu.VMEM((g * nsel,), F32),
            pltpu.VMEM((g, d), F32),
            [row_buf] * SC_RING,
            [pltpu.SemaphoreType.DMA] * SC_RING,
        ],
        compiler_params=_sc_params(),
        name="peer_combine_sc",
    )
    def k(tab_hbm, idx_hbm, coef_hbm, out_hbm, idx_v, coef_v, y_v, rows, sems):
        wid = _sc_worker_id()

        def copy(u, slot):
            ids = idx_v.at[pl.ds(_sc_unit_off(u), SC_LANES)]
            return pltpu.make_async_copy(tab_hbm.at[ids], rows[slot], sems[slot])

        def compute(u, slot):
            tt = u // heads
            first = (u % heads) == 0
            cs = [plsc.load_gather(coef_v, [jnp.full((SC_LANES,), u * SC_LANES + r, I32)])
                  for r in range(SC_LANES)]

            @plsc.parallel_loop(0, pieces, unroll=2)
            def _(c):
                off = pl.multiple_of(c * SC_LANES, SC_LANES)
                terms = [cs[r] * _sc_row_piece(rows[slot], r, c) for r in range(SC_LANES)]
                while len(terms) > 1:
                    terms = [terms[i] + terms[i + 1] for i in range(0, len(terms), 2)]
                prev = y_v[tt, pl.ds(off, SC_LANES)]
                y_v[tt, pl.ds(off, SC_LANES)] = terms[0] + jnp.where(first, 0.0, prev)

        @pl.loop(0, groups)
        def _(gi):
            base = wid * tpw + gi * g
            pltpu.sync_copy(idx_hbm.at[pl.ds(base * nsel, g * nsel)], idx_v)
            pltpu.sync_copy(coef_hbm.at[pl.ds(base * nsel, g * nsel)], coef_v)
            _sc_ring(units, lambda u, s: copy(u, s).start(), lambda u, s: copy(u, s).wait(), compute)
            pltpu.sync_copy(y_v, out_hbm.at[pl.ds(base, g)])

    return k(table, idx_flat, coef_flat)


def kernel(x, mem, rel_bias, ln_mix, w_in, hg_lower, hg_norm, w_up_a, w_up_b, w_out, ln_cross, ln_mem, wq_x, wk_x, wv_x, wo_x, ln_ffn, peer_query, peer_subkeys, peer_u, peer_v, ln_final):
    b, s, d = x.shape
    t = b * s
    depth = w_in.shape[0]
    assert depth == 1, "the residual after PEER is fused into the final norm"
    assert s % MB_BLOCK == 0 and s % HG_CHUNK == 0 and t % (SC_WORKERS * SC_GROUP) == 0
    nb = s // MB_BLOCK
    row = lambda a: a.reshape(1, -1).astype(F32)
    lb_all = jnp.cumsum(jax.nn.softmax(hg_lower.astype(F32), axis=0), axis=0)
    bias = moba_bias_tiles(rel_bias)
    n_hg = 4 * HG_WIDTH
    n_mb = 3 * MB_WIDTH
    x2d = x.reshape(t, d)
    for l in range(depth):
        w = w_in[l].astype(BF16)
        n_qk = 2 * MB_WIDTH
        p0, pqk, vt, pg = in_proj(x2d, row(ln_mix[l]), w[:, :n_hg], w[:, n_hg:n_hg + n_qk],
                                  w[:, n_hg + n_qk:n_hg + n_mb].T, w[:, n_hg + n_mb:])
        ya = hgrn2(p0, row(lb_all[l]), row(hg_norm[l]), b, s)
        km = moba_kmean(pqk, b, s).reshape(b, nb, MB_WIDTH)
        yb = moba_attention(pqk, vt, km, bias, b, s)
        x2d = mix_out(x2d, ya, yb, pg, w_up_a[l].astype(BF16), w_up_b[l].astype(BF16), w_out[l].astype(BF16))
        kx, vx = mem_kv(mem, row(ln_mem[l]), wk_x[l].astype(BF16), wv_x[l].astype(BF16))
        x2d = cross_attn(x2d, row(ln_cross[l]), wq_x[l].astype(BF16), kx, vx, wo_x[l].astype(BF16), s)
        hf, eidx, wts = peer_route(x2d, row(ln_ffn[l]), peer_query[l].astype(BF16), peer_subkeys[l].astype(F32))
        idx_flat = eidx.reshape(PEER_SEL, t).T.reshape(t * PEER_SEL)
        wts2d = wts.reshape(PEER_SEL, t).T
        tab3 = lambda a: a.astype(F32).reshape(a.shape[0], SC_ROW_SUB, SC_ROW_LANE)
        act = peer_dots_sc(tab3(peer_u[l]), idx_flat, hf).reshape(t, PEER_SEL)
        coef = peer_coef(wts2d, act)
        y = peer_combine_sc(tab3(peer_v[l]), idx_flat, coef.reshape(t * PEER_SEL), t)
    return final_norm(x2d, y, row(ln_final)).reshape(b, s, d)
```

```python
import functools
import math

import jax
import jax.numpy as jnp
import numpy as np
from jax import lax
from jax.experimental import pallas as pl
from jax.experimental.pallas import tpu as pltpu
from jax.experimental.pallas import tpu_sc as plsc

F32 = jnp.float32
BF16 = jnp.bfloat16
I32 = jnp.int32
EPS = 1e-6
NEG_INF = float("-inf")

HG_HEADS = 4
HG_D = 128
HG_WIDTH = HG_HEADS * HG_D
HG_CHUNK = 64
HG_SUB = 16
MB_HEADS = 8
MB_DH = 64
MB_WIDTH = MB_HEADS * MB_DH
MB_BLOCK = 256
MB_TOPK = 3
MB_BIAS_TILES = 8
REL_BUCKETS = 32
REL_MAX_DIST = 2048
X_HEADS = 4
PEER_HEADS = 8
PEER_NKEYS = 128
PEER_TOPK = 16
PEER_HALF = 128
PEER_SEL = PEER_HEADS * PEER_TOPK

VMEM_LIMIT = 56 * 1024 * 1024


def _cparams(sem):
    return pltpu.CompilerParams(dimension_semantics=sem, vmem_limit_bytes=VMEM_LIMIT)


def _rms(x, g):
    ms = jnp.mean(x * x, axis=-1, keepdims=True)
    return x * lax.rsqrt(ms + EPS) * g


def _in_proj_kernel(x_ref, g_ref, w0_ref, w1_ref, wvt_ref, w2_ref, o0_ref, o1_ref, ovt_ref, o2_ref):
    h = _rms(x_ref[...], g_ref[...]).astype(BF16)
    o0_ref[...] = jnp.dot(h, w0_ref[...], preferred_element_type=F32)
    o1_ref[...] = jnp.dot(h, w1_ref[...], preferred_element_type=F32).astype(BF16)
    ovt_ref[0] = lax.dot_general(wvt_ref[...], h, (((1,), (1,)), ((), ())),
                                 preferred_element_type=F32).astype(BF16)
    o2_ref[...] = jnp.dot(h, w2_ref[...], preferred_element_type=F32).astype(BF16)


def in_proj(x2d, g, w0, w1, wvt, w2):
    t, d = x2d.shape
    tm = MB_BLOCK
    n0, n1, nv, n2 = w0.shape[1], w1.shape[1], wvt.shape[0], w2.shape[1]
    full = lambda a: pl.BlockSpec(a.shape, lambda i: (0, 0))
    return pl.pallas_call(
        _in_proj_kernel,
        grid=(t // tm,),
        in_specs=[pl.BlockSpec((tm, d), lambda i: (i, 0)), full(g), full(w0), full(w1), full(wvt), full(w2)],
        out_specs=[pl.BlockSpec((tm, n0), lambda i: (i, 0)),
                   pl.BlockSpec((tm, n1), lambda i: (i, 0)),
                   pl.BlockSpec((1, nv, tm), lambda i: (i, 0, 0)),
                   pl.BlockSpec((tm, n2), lambda i: (i, 0))],
        out_shape=[jax.ShapeDtypeStruct((t, n0), F32),
                   jax.ShapeDtypeStruct((t, n1), BF16),
                   jax.ShapeDtypeStruct((t // tm, nv, tm), BF16),
                   jax.ShapeDtypeStruct((t, n2), BF16)],
        compiler_params=_cparams(("parallel",)),
        name="in_proj",
    )(x2d, g, w0, w1, wvt, w2)


def _hgrn_kernel(q_ref, f_ref, i_ref, g_ref, lb_ref, gain_ref, o_ref, st_ref):
    c = pl.program_id(1)

    @pl.when(c == 0)
    def _():
        st_ref[...] = jnp.zeros_like(st_ref)

    C, S = HG_CHUNK, HG_SUB
    row = lax.broadcasted_iota(I32, (C, C), 0)
    col = lax.broadcasted_iota(I32, (C, C), 1)
    tril = (row >= col).astype(F32)
    t_iota = lax.broadcasted_iota(I32, (S, 1), 0)

    for h in range(HG_HEADS):
        sl = slice(h * HG_D, (h + 1) * HG_D)
        q = q_ref[:, sl]
        v = i_ref[:, sl]
        lb = lb_ref[:, sl]
        f = lb + (1.0 - lb) * jax.nn.sigmoid(f_ref[:, sl])
        lf = jnp.log(f)
        k = 1.0 - f
        b = jnp.dot(tril, lf, precision=lax.Precision.HIGHEST, preferred_element_type=F32)
        st = st_ref[h]
        vb = v.astype(BF16)
        qd = (q * jnp.exp(b)).astype(BF16)
        o_inter = lax.dot_general(qd, st.astype(BF16), (((1,), (1,)), ((), ())),
                                  preferred_element_type=F32)
        outs = []
        for i in range(C // S):
            r0 = i * S
            qi = q[r0:r0 + S]
            ki = k[r0:r0 + S]
            bi = b[r0:r0 + S]
            vi = v[r0:r0 + S]
            oi = o_inter[r0:r0 + S]
            if i > 0:
                bs = b[r0 - 1:r0]
                qh = (qi * jnp.exp(bi - bs)).astype(BF16)
                kh = (k[:r0] * jnp.exp(bs - b[:r0])).astype(BF16)
                a = lax.dot_general(qh, kh, (((1,), (1,)), ((), ())), preferred_element_type=F32)
                oi = oi + jnp.dot(a.astype(BF16), vb[:r0], preferred_element_type=F32)
            for s in range(S):
                dec = jnp.exp(jnp.minimum(bi - bi[s:s + 1], 0.0))
                p = qi * ki[s:s + 1] * dec
                a_s = jnp.sum(p, axis=-1, keepdims=True)
                a_s = jnp.where(t_iota >= s, a_s, 0.0)
                oi = oi + a_s * vi[s:s + 1]
            outs.append(oi)
        o = jnp.concatenate(outs, axis=0)
        b_end = b[C - 1:C]
        kd = (k * jnp.exp(b_end - b)).astype(BF16)
        upd = lax.dot_general(vb, kd, (((0,), (0,)), ((), ())), preferred_element_type=F32)
        st_ref[h] = st * jnp.exp(b_end) + upd
        o = o * lax.rsqrt(jnp.mean(o * o, axis=-1, keepdims=True) + EPS)
        g = g_ref[:, sl]
        o_ref[:, sl] = (o * gain_ref[:, sl] * (g * jax.nn.sigmoid(g))).astype(o_ref.dtype)


def hgrn2(p0, lb, gain, batch, seq):
    t = p0.shape[0]
    nc = seq // HG_CHUNK
    w = HG_WIDTH

    def col(j):
        return pl.BlockSpec((HG_CHUNK, w), lambda b, c, j=j: (b * nc + c, j))

    return pl.pallas_call(
        _hgrn_kernel,
        grid=(batch, nc),
        in_specs=[col(0), col(1), col(2), col(3),
                  pl.BlockSpec((1, w), lambda b, c: (0, 0)),
                  pl.BlockSpec((1, w), lambda b, c: (0, 0))],
        out_specs=pl.BlockSpec((HG_CHUNK, w), lambda b, c: (b * nc + c, 0)),
        out_shape=jax.ShapeDtypeStruct((t, w), BF16),
        scratch_shapes=[pltpu.VMEM((HG_HEADS, HG_D, HG_D), F32)],
        compiler_params=_cparams(("parallel", "arbitrary")),
        name="hgrn2",
    )(p0, p0, p0, p0, lb, gain)


def _kmean_kernel(k_ref, o_ref):
    o_ref[0] = jnp.mean(k_ref[...].astype(F32), axis=0, keepdims=True)


def moba_kmean(p1, batch, seq):
    nbt = p1.shape[0] // MB_BLOCK
    return pl.pallas_call(
        _kmean_kernel,
        grid=(nbt,),
        in_specs=[pl.BlockSpec((MB_BLOCK, MB_WIDTH), lambda i: (i, 1))],
        out_specs=pl.BlockSpec((1, 1, MB_WIDTH), lambda i: (i, 0, 0)),
        out_shape=jax.ShapeDtypeStruct((nbt, 1, MB_WIDTH), F32),
        compiler_params=_cparams(("parallel",)),
        name="moba_kmean",
    )(p1)


MB_PAIR = 4
MB_PW = MB_PAIR * MB_DH
MB_LG = 128


def _moba_kernel(q_ref, k_ref, vt_ref, km_ref, bias_ref, o_ref, *scratch):
    m_ref, l_ref, acc_ref, msk_ref = (scratch[i * MB_PAIR:(i + 1) * MB_PAIR] for i in range(4))
    qi = pl.program_id(2)
    nb = km_ref.shape[0]
    blk = MB_BLOCK
    heads = range(MB_PAIR)
    grp = lambda hh: slice((hh // 2) * MB_LG, (hh // 2 + 1) * MB_LG)
    q = q_ref[...]
    lane = lax.broadcasted_iota(I32, (blk, MB_LG), 1)
    in_head = [(lane < MB_DH) if hh % 2 == 0 else (lane >= MB_DH) for hh in heads]
    qs = q * jnp.asarray(MB_DH ** -0.5, BF16)
    qh = [jnp.where(in_head[hh], qs[:, grp(hh)], jnp.zeros((blk, MB_LG), BF16)) for hh in heads]
    nt = (((1,), (1,)), ((), ()))

    qf = q.astype(F32)
    n_io = lax.broadcasted_iota(I32, (nb, blk), 0)
    for hh in heads:
        gate = lax.dot_general(km_ref[:, grp(hh)], jnp.where(in_head[hh], qf[:, grp(hh)], 0.0), nt,
                               precision=lax.Precision.HIGHEST, preferred_element_type=F32)
        gate = jnp.where(n_io < qi, gate, NEG_INF)
        chosen = n_io < 0
        for _ in range(MB_TOPK):
            mx = jnp.max(gate, axis=0, keepdims=True)
            ix = jnp.min(jnp.where(gate == mx, n_io, nb), axis=0, keepdims=True)
            hit = n_io == ix
            chosen = chosen | (hit & (mx > NEG_INF))
            gate = jnp.where(hit, NEG_INF, gate)
        msk_ref[hh][...] = jnp.where(chosen, 0.0, NEG_INF)

    k_own = k_ref[pl.ds(pl.multiple_of(qi * blk, blk), blk), :]
    vt_own = vt_ref[qi]
    key_io = lax.broadcasted_iota(I32, (blk, blk), 0)
    qry_io = lax.broadcasted_iota(I32, (blk, blk), 1)
    own_rows = lambda r, hh: r[(hh % 2) * MB_DH:(hh % 2 + 1) * MB_DH]
    for hh in heads:
        s = lax.dot_general(k_own[:, grp(hh)], qh[hh], nt, preferred_element_type=F32) + bias_ref[hh, 0]
        s = jnp.where(key_io <= qry_io, s, NEG_INF)
        m0 = jnp.max(s, axis=0, keepdims=True)
        p = jnp.exp(s - m0)
        m_ref[hh][...] = m0
        l_ref[hh][...] = jnp.sum(p, axis=0, keepdims=True)
        r = jnp.dot(vt_own[grp(hh)], p.astype(BF16), preferred_element_type=F32)
        acc_ref[hh][...] = own_rows(r, hh)

    def past(n, carry):
        kn = k_ref[pl.ds(pl.multiple_of(n * blk, blk), blk), :]
        vtn = vt_ref[n]
        d = jnp.minimum(qi - n, MB_BIAS_TILES - 1)
        s = [lax.dot_general(kn[:, grp(hh)], qh[hh], nt, preferred_element_type=F32)
             + bias_ref[hh, d] + msk_ref[hh][pl.ds(n, 1), :] for hh in heads]
        m_old = [m_ref[hh][...] for hh in heads]
        l_old = [l_ref[hh][...] for hh in heads]
        a_old = [acc_ref[hh][...] for hh in heads]
        m_new = [jnp.maximum(m_old[hh], jnp.max(s[hh], axis=0, keepdims=True)) for hh in heads]
        alpha = [jnp.exp(m_old[hh] - m_new[hh]) for hh in heads]
        p = [jnp.exp(s[hh] - m_new[hh]) for hh in heads]
        r = [jnp.dot(vtn[grp(hh)], p[hh].astype(BF16), preferred_element_type=F32) for hh in heads]
        l_new = [alpha[hh] * l_old[hh] + jnp.sum(p[hh], axis=0, keepdims=True) for hh in heads]
        a_new = [alpha[hh] * a_old[hh] + own_rows(r[hh], hh) for hh in heads]
        for hh in heads:
            m_ref[hh][...] = m_new[hh]
            l_ref[hh][...] = l_new[hh]
            acc_ref[hh][...] = a_new[hh]
        return carry

    lax.fori_loop(0, qi, past, 0)
    out_t = jnp.concatenate([acc_ref[hh][...] / l_ref[hh][...] for hh in heads], axis=0)
    o_ref[...] = out_t.T.astype(o_ref.dtype)


def moba_attention(pqk, vt, km, bias, batch, seq):
    t = pqk.shape[0]
    nb = seq // MB_BLOCK
    groups = MB_WIDTH // MB_PW
    return pl.pallas_call(
        _moba_kernel,
        grid=(batch, groups, nb),
        in_specs=[
            pl.BlockSpec((MB_BLOCK, MB_PW), lambda b, j, i: (b * nb + i, j)),
            pl.BlockSpec((seq, MB_PW), lambda b, j, i: (b, groups + j)),
            pl.BlockSpec((nb, MB_PW, MB_BLOCK), lambda b, j, i: (b, j, 0)),
            pl.BlockSpec((None, nb, MB_PW), lambda b, j, i: (b, 0, j)),
            pl.BlockSpec((MB_PAIR, MB_BIAS_TILES, MB_BLOCK, MB_BLOCK), lambda b, j, i: (j, 0, 0, 0)),
        ],
        out_specs=pl.BlockSpec((MB_BLOCK, MB_PW), lambda b, j, i: (b * nb + i, j)),
        out_shape=jax.ShapeDtypeStruct((t, MB_WIDTH), BF16),
        scratch_shapes=(
            [pltpu.VMEM((1, MB_BLOCK), F32)] * (2 * MB_PAIR)
            + [pltpu.VMEM((MB_DH, MB_BLOCK), F32)] * MB_PAIR
            + [pltpu.VMEM((nb, MB_BLOCK), F32)] * MB_PAIR
        ),
        compiler_params=_cparams(("parallel", "parallel", "arbitrary")),
        name="moba_attn",
    )(pqk, pqk, vt, km, bias)


def _t5_bucket(dist):
    max_exact = REL_BUCKETS // 2
    scaled = jnp.log(jnp.maximum(dist, 1).astype(F32) / max_exact) / math.log(REL_MAX_DIST / max_exact)
    large = jnp.minimum(max_exact + (scaled * (REL_BUCKETS - max_exact)).astype(I32), REL_BUCKETS - 1)
    return jnp.where(dist < max_exact, dist, large)


def moba_bias_tiles(rel_bias):
    blk = MB_BLOCK
    span = 2 * blk - 1
    x = jnp.arange(span) - (blk - 1)
    dist = jnp.maximum(jnp.arange(MB_BIAS_TILES)[:, None] * blk + x[None, :], 0)
    w = rel_bias.astype(F32).T[:, _t5_bucket(dist)]
    h = w.shape[0]
    wp = jnp.pad(w, ((0, 0), (0, 0), (0, 1)))
    a = jnp.broadcast_to(wp[:, :, None, :], (h, MB_BIAS_TILES, blk, span + 1))
    a = a.reshape(h, MB_BIAS_TILES, blk * (span + 1))[:, :, :blk * span]
    return a.reshape(h, MB_BIAS_TILES, blk, span)[:, :, :, blk - 1:]


def _mix_kernel(x_ref, ya_ref, yb_ref, ga_ref, gb_ref, wa_ref, wb_ref, wo_ref, o_ref):
    za = jnp.dot(ya_ref[...], wa_ref[...], preferred_element_type=F32)
    zb = jnp.dot(yb_ref[...], wb_ref[...], preferred_element_type=F32)
    z = jax.nn.sigmoid(ga_ref[...].astype(F32)) * za + jax.nn.sigmoid(gb_ref[...].astype(F32)) * zb
    o_ref[...] = x_ref[...] + jnp.dot(z.astype(BF16), wo_ref[...], preferred_element_type=F32)


def mix_out(x2d, ya, yb, pg, wa, wb, wo, tm=256):
    t, d = x2d.shape
    w = ya.shape[1]
    return pl.pallas_call(
        _mix_kernel,
        grid=(t // tm,),
        in_specs=[
            pl.BlockSpec((tm, d), lambda i: (i, 0)),
            pl.BlockSpec((tm, w), lambda i: (i, 0)),
            pl.BlockSpec((tm, w), lambda i: (i, 0)),
            pl.BlockSpec((tm, d), lambda i: (i, 0)),
            pl.BlockSpec((tm, d), lambda i: (i, 1)),
            pl.BlockSpec((w, d), lambda i: (0, 0)),
            pl.BlockSpec((w, d), lambda i: (0, 0)),
            pl.BlockSpec((d, d), lambda i: (0, 0)),
        ],
        out_specs=pl.BlockSpec((tm, d), lambda i: (i, 0)),
        out_shape=jax.ShapeDtypeStruct((t, d), F32),
        compiler_params=_cparams(("parallel",)),
        name="mix_out",
    )(x2d, ya, yb, pg, pg, wa, wb, wo)


def _mem_kv_kernel(m_ref, g_ref, wk_ref, wv_ref, k_ref, v_ref):
    mn = _rms(m_ref[...], g_ref[...]).astype(BF16)
    k_ref[...] = jnp.dot(mn, wk_ref[...], preferred_element_type=F32).astype(BF16)
    v_ref[...] = jnp.dot(mn, wv_ref[...], preferred_element_type=F32).astype(BF16)


def mem_kv(mem, g, wk, wv):
    b, m, d = mem.shape
    spec = pl.BlockSpec((None, m, d), lambda i: (i, 0, 0))
    wspec = pl.BlockSpec((d, d), lambda i: (0, 0))
    return pl.pallas_call(
        _mem_kv_kernel,
        grid=(b,),
        in_specs=[spec, pl.BlockSpec((1, d), lambda i: (0, 0)), wspec, wspec],
        out_specs=[spec, spec],
        out_shape=[jax.ShapeDtypeStruct((b, m, d), BF16)] * 2,
        compiler_params=_cparams(("parallel",)),
        name="mem_kv",
    )(mem, g, wk, wv)


def _cross_kernel(x_ref, g_ref, wq_ref, k_ref, v_ref, wo_ref, o_ref):
    x = x_ref[...]
    d = x.shape[1]
    dh = d // X_HEADS
    h = _rms(x, g_ref[...]).astype(BF16)
    q = (jnp.dot(h, wq_ref[...], preferred_element_type=F32) * (dh ** -0.5)).astype(BF16)
    outs = []
    for hh in range(X_HEADS):
        sl = slice(hh * dh, (hh + 1) * dh)
        s = lax.dot_general(q[:, sl], k_ref[:, sl], (((1,), (1,)), ((), ())),
                            preferred_element_type=F32)
        p = jnp.exp(s - jnp.max(s, axis=1, keepdims=True))
        l = jnp.sum(p, axis=1, keepdims=True)
        o = jnp.dot(p.astype(BF16), v_ref[:, sl], preferred_element_type=F32) / l
        outs.append(o.astype(BF16))
    o = jnp.concatenate(outs, axis=1)
    o_ref[...] = x + jnp.dot(o, wo_ref[...], preferred_element_type=F32)


def cross_attn(x2d, g, wq, kx, vx, wo, seq, tm=256):
    t, d = x2d.shape
    m = kx.shape[1]
    per_b = seq // tm
    kv = pl.BlockSpec((None, m, d), lambda i: (i // per_b, 0, 0))
    wspec = pl.BlockSpec((d, d), lambda i: (0, 0))
    return pl.pallas_call(
        _cross_kernel,
        grid=(t // tm,),
        in_specs=[pl.BlockSpec((tm, d), lambda i: (i, 0)), pl.BlockSpec((1, d), lambda i: (0, 0)),
                  wspec, kv, kv, wspec],
        out_specs=pl.BlockSpec((tm, d), lambda i: (i, 0)),
        out_shape=jax.ShapeDtypeStruct((t, d), F32),
        compiler_params=_cparams(("parallel",)),
        name="cross_attn",
    )(x2d, g, wq, kx, vx, wo)


def _topk_rows(sc, k):
    n = sc.shape[0]
    io = lax.broadcasted_iota(I32, sc.shape, 0)
    vals, ids = [], []
    for _ in range(k):
        m = jnp.max(sc, axis=0, keepdims=True)
        ix = jnp.min(jnp.where(sc == m, io, n), axis=0, keepdims=True)
        vals.append(m)
        ids.append(ix)
        sc = jnp.where(io == ix, NEG_INF, sc)
    return jnp.concatenate(vals, axis=0), jnp.concatenate(ids, axis=0)


def _route_kernel(x_ref, g_ref, wq_ref, sk_ref, h_ref, idx_ref, w_ref, hb_ref):
    p = pl.program_id(1)

    @pl.when(p == 0)
    def _():
        h = _rms(x_ref[...], g_ref[...])
        h_ref[...] = h
        hb_ref[...] = h.astype(BF16)

    qh = jnp.dot(hb_ref[...], wq_ref[...], preferred_element_type=F32)
    tops = []
    for c in range(2):
        seg = qh[:, c * PEER_HALF:(c + 1) * PEER_HALF]
        sc = lax.dot_general(sk_ref[c], seg, (((1,), (1,)), ((), ())),
                             precision=lax.Precision.HIGHEST, preferred_element_type=F32)
        tops.append(_topk_rows(sc, PEER_TOPK))
    (s0, i0), (s1, i1) = tops
    cand = jnp.concatenate([s0[a:a + 1] + s1 for a in range(PEER_TOPK)], axis=0)
    cidx = jnp.concatenate([i0[a:a + 1] * PEER_NKEYS + i1 for a in range(PEER_TOPK)], axis=0)
    io = lax.broadcasted_iota(I32, cand.shape, 0)
    n = cand.shape[0]
    vals, ids = [], []
    for _ in range(PEER_TOPK):
        m = jnp.max(cand, axis=0, keepdims=True)
        px = jnp.min(jnp.where(cand == m, io, n), axis=0, keepdims=True)
        hit = io == px
        vals.append(m)
        ids.append(jnp.sum(jnp.where(hit, cidx, 0), axis=0, keepdims=True))
        cand = jnp.where(hit, NEG_INF, cand)
    sf = jnp.concatenate(vals, axis=0)
    e = jnp.exp(sf - sf[0:1])
    w_ref[...] = e / jnp.sum(e, axis=0, keepdims=True)
    idx_ref[...] = jnp.concatenate(ids, axis=0)


def peer_route(x2d, g, wq, sk, tm=256):
    t, d = x2d.shape
    ph = sk.shape[0]
    return pl.pallas_call(
        _route_kernel,
        grid=(t // tm, ph),
        in_specs=[
            pl.BlockSpec((tm, d), lambda i, p: (i, 0)),
            pl.BlockSpec((1, d), lambda i, p: (0, 0)),
            pl.BlockSpec((d, 2 * PEER_HALF), lambda i, p: (0, p)),
            pl.BlockSpec((None, 2, PEER_NKEYS, PEER_HALF), lambda i, p: (p, 0, 0, 0)),
        ],
        out_specs=[
            pl.BlockSpec((tm, d), lambda i, p: (i, 0)),
            pl.BlockSpec((None, PEER_TOPK, tm), lambda i, p: (p, 0, i)),
            pl.BlockSpec((None, PEER_TOPK, tm), lambda i, p: (p, 0, i)),
        ],
        out_shape=[jax.ShapeDtypeStruct((t, d), F32),
                   jax.ShapeDtypeStruct((ph, PEER_TOPK, t), I32),
                   jax.ShapeDtypeStruct((ph, PEER_TOPK, t), F32)],
        scratch_shapes=[pltpu.VMEM((tm, d), BF16)],
        compiler_params=_cparams(("parallel", "arbitrary")),
        name="peer_route",
    )(x2d, g, wq, sk)


def _coef_kernel(w_ref, a_ref, o_ref):
    o_ref[...] = w_ref[...] * jax.nn.gelu(a_ref[...])


def peer_coef(w, act, tm=1024):
    t, n = w.shape
    spec = pl.BlockSpec((tm, n), lambda i: (i, 0))
    return pl.pallas_call(
        _coef_kernel, grid=(t // tm,), in_specs=[spec, spec], out_specs=spec,
        out_shape=jax.ShapeDtypeStruct((t, n), F32),
        compiler_params=_cparams(("parallel",)), name="peer_coef",
    )(w, act)


def _final_kernel(x_ref, y_ref, g_ref, o_ref):
    o_ref[...] = _rms(x_ref[...] + y_ref[...], g_ref[...])


def final_norm(x2d, y, g, tm=512):
    t, d = x2d.shape
    spec = pl.BlockSpec((tm, d), lambda i: (i, 0))
    return pl.pallas_call(
        _final_kernel, grid=(t // tm,),
        in_specs=[spec, spec, pl.BlockSpec((1, d), lambda i: (0, 0))], out_specs=spec,
        out_shape=jax.ShapeDtypeStruct((t, d), F32),
        compiler_params=_cparams(("parallel",)), name="final_norm",
    )(x2d, y, g)


SC_CORES = 2
SC_SUBCORES = 16
SC_WORKERS = SC_CORES * SC_SUBCORES
SC_LANES = 16
SC_GROUP = 16


def _sc_mesh():
    return plsc.VectorSubcoreMesh(core_axis_name="c", subcore_axis_name="s")


def _sc_params():
    return pltpu.CompilerParams(needs_layout_passes=False)


def _sc_worker_id():
    return lax.axis_index("s") * SC_CORES + lax.axis_index("c")


SC_RING = 4
SC_ROW_SUB = 8
SC_ROW_LANE = 128


def _sc_ring(n_units, start, wait, compute):
    for u in range(SC_RING - 1):
        start(u, u)

    @pl.loop(0, n_units, step=SC_RING)
    def _(uu):
        for b in range(SC_RING):
            u = uu + b
            nxt = u + (SC_RING - 1)

            @pl.when(nxt < n_units)
            def _():
                start(nxt, (b + SC_RING - 1) % SC_RING)

            wait(u, b)
            compute(u, b)


def _sc_unit_off(u):
    off = u * SC_LANES
    return off if isinstance(off, int) else pl.multiple_of(off, SC_LANES)


def _sc_row_piece(rows, r, c):
    per = SC_ROW_LANE // SC_LANES
    return rows[r, c // per, pl.ds(pl.multiple_of((c % per) * SC_LANES, SC_LANES), SC_LANES)]


def peer_dots_sc(table, idx_flat, h):
    t, d = h.shape
    nsel = PEER_SEL
    tpw = t // SC_WORKERS
    g = SC_GROUP
    groups = tpw // g
    heads = nsel // SC_LANES
    pieces = d // SC_LANES
    units = g * heads
    row_buf = pltpu.VMEM((SC_LANES, SC_ROW_SUB, SC_ROW_LANE), F32)

    @functools.partial(
        pl.kernel, mesh=_sc_mesh(),
        out_type=jax.ShapeDtypeStruct((t * nsel,), F32),
        scratch_types=[
            pltpu.VMEM((g * nsel,), I32),
            pltpu.VMEM((g, d), F32),
            pltpu.VMEM((g * nsel,), F32),
            pltpu.VMEM((SC_LANES * SC_LANES,), F32),
            [row_buf] * SC_RING,
            [pltpu.SemaphoreType.DMA] * SC_RING,
        ],
        compiler_params=_sc_params(),
        name="peer_dots_sc",
    )
    def k(tab_hbm, idx_hbm, h_hbm, out_hbm, idx_v, h_v, out_v, red_v, rows, sems):
        wid = _sc_worker_id()
        lane = lax.iota(I32, SC_LANES)

        def copy(u, slot):
            ids = idx_v.at[pl.ds(_sc_unit_off(u), SC_LANES)]
            return pltpu.make_async_copy(tab_hbm.at[ids], rows[slot], sems[slot])

        def compute(u, slot):
            tt = u // heads

            def body(c, accs):
                hv = h_v[tt, pl.ds(pl.multiple_of(c * SC_LANES, SC_LANES), SC_LANES)]
                return tuple(accs[r] + _sc_row_piece(rows[slot], r, c) * hv for r in range(SC_LANES))

            accs = lax.fori_loop(0, pieces, body,
                                 tuple(jnp.zeros((SC_LANES,), F32) for _ in range(SC_LANES)))
            for r in range(SC_LANES):
                red_v[pl.ds(r * SC_LANES, SC_LANES)] = accs[r]
            cols = [plsc.load_gather(red_v, [lane * SC_LANES + j]) for j in range(SC_LANES)]
            while len(cols) > 1:
                cols = [cols[i] + cols[i + 1] for i in range(0, len(cols), 2)]
            out_v[pl.ds(_sc_unit_off(u), SC_LANES)] = cols[0]

        @pl.loop(0, groups)
        def _(gi):
            base = wid * tpw + gi * g
            pltpu.sync_copy(idx_hbm.at[pl.ds(base * nsel, g * nsel)], idx_v)
            pltpu.sync_copy(h_hbm.at[pl.ds(base, g)], h_v)
            _sc_ring(units, lambda u, s: copy(u, s).start(), lambda u, s: copy(u, s).wait(), compute)
            pltpu.sync_copy(out_v, out_hbm.at[pl.ds(base * nsel, g * nsel)])

    return k(table, idx_flat, h)


def peer_combine_sc(table, idx_flat, coef_flat, t):
    d = table.shape[1] * table.shape[2]
    nsel = PEER_SEL
    tpw = t // SC_WORKERS
    g = SC_GROUP
    groups = tpw // g
    heads = nsel // SC_LANES
    pieces = d // SC_LANES
    units = g * heads
    row_buf = pltpu.VMEM((SC_LANES, SC_ROW_SUB, SC_ROW_LANE), F32)

    @functools.partial(
        pl.kernel, mesh=_sc_mesh(),
        out_type=jax.ShapeDtypeStruct((t, d), F32),
        scratch_types=[
            pltpu.VMEM((g * nsel,), I32),
            pltpu.VMEM((g * nsel,), F32),
            pltpu.VMEM((g, d), F32),
            [row_buf] * SC_RING,
            [pltpu.SemaphoreType.DMA] * SC_RING,
        ],
        compiler_params=_sc_params(),
        name="peer_combine_sc",
    )
    def k(tab_hbm, idx_hbm, coef_hbm, out_hbm, idx_v, coef_v, y_v, rows, sems):
        wid = _sc_worker_id()

        def copy(u, slot):
            ids = idx_v.at[pl.ds(_sc_unit_off(u), SC_LANES)]
            return pltpu.make_async_copy(tab_hbm.at[ids], rows[slot], sems[slot])

        def compute(u, slot):
            tt = u // heads
            first = (u % heads) == 0
            cs = [plsc.load_gather(coef_v, [jnp.full((SC_LANES,), u * SC_LANES + r, I32)])
                  for r in range(SC_LANES)]

            @plsc.parallel_loop(0, pieces, unroll=2)
            def _(c):
                off = pl.multiple_of(c * SC_LANES, SC_LANES)
                terms = [cs[r] * _sc_row_piece(rows[slot], r, c) for r in range(SC_LANES)]
                while len(terms) > 1:
                    terms = [terms[i] + terms[i + 1] for i in range(0, len(terms), 2)]
                prev = y_v[tt, pl.ds(off, SC_LANES)]
                y_v[tt, pl.ds(off, SC_LANES)] = terms[0] + jnp.where(first, 0.0, prev)

        @pl.loop(0, groups)
        def _(gi):
            base = wid * tpw + gi * g
            pltpu.sync_copy(idx_hbm.at[pl.ds(base * nsel, g * nsel)], idx_v)
            pltpu.sync_copy(coef_hbm.at[pl.ds(base * nsel, g * nsel)], coef_v)
            _sc_ring(units, lambda u, s: copy(u, s).start(), lambda u, s: copy(u, s).wait(), compute)
            pltpu.sync_copy(y_v, out_hbm.at[pl.ds(base, g)])

    return k(table, idx_flat, coef_flat)


def kernel(x, mem, rel_bias, ln_mix, w_in, hg_lower, hg_norm, w_up_a, w_up_b, w_out, ln_cross, ln_mem, wq_x, wk_x, wv_x, wo_x, ln_ffn, peer_query, peer_subkeys, peer_u, peer_v, ln_final):
    b, s, d = x.shape
    depth = w_in.shape[0]
    assert depth == 1, "the residual after PEER is fused into the final norm"
    assert s % MB_BLOCK == 0 and s % HG_CHUNK == 0 and s % (SC_WORKERS * SC_GROUP) == 0
    nb = s // MB_BLOCK
    row = lambda a: a.reshape(1, -1).astype(F32)
    lb_all = jnp.cumsum(jax.nn.softmax(hg_lower.astype(F32), axis=0), axis=0)
    bias = moba_bias_tiles(rel_bias)
    n_hg = 4 * HG_WIDTH
    n_qk = 2 * MB_WIDTH
    n_mb = 3 * MB_WIDTH
    l = 0
    w = w_in[l].astype(BF16)
    w_hg, w_qk, w_vt, w_g = w[:, :n_hg], w[:, n_hg:n_hg + n_qk], w[:, n_hg + n_qk:n_hg + n_mb].T, w[:, n_hg + n_mb:]
    wa, wb, wo = w_up_a[l].astype(BF16), w_up_b[l].astype(BF16), w_out[l].astype(BF16)
    wqx, wox = wq_x[l].astype(BF16), wo_x[l].astype(BF16)
    wpq, sk = peer_query[l].astype(BF16), peer_subkeys[l].astype(F32)
    tab3 = lambda a: a.astype(F32).reshape(a.shape[0], SC_ROW_SUB, SC_ROW_LANE)
    tab_u, tab_v = tab3(peer_u[l]), tab3(peer_v[l])
    kx, vx = mem_kv(mem, row(ln_mem[l]), wk_x[l].astype(BF16), wv_x[l].astype(BF16))

    def tc_mixers(bi):
        x2d = x[bi]
        p0, pqk, vt, pg = in_proj(x2d, row(ln_mix[l]), w_hg, w_qk, w_vt, w_g)
        ya = hgrn2(p0, row(lb_all[l]), row(hg_norm[l]), 1, s)
        km = moba_kmean(pqk, 1, s).reshape(1, nb, MB_WIDTH)
        yb = moba_attention(pqk, vt, km, bias, 1, s)
        x2d = mix_out(x2d, ya, yb, pg, wa, wb, wo)
        x2d = cross_attn(x2d, row(ln_cross[l]), wqx, kx[bi:bi + 1], vx[bi:bi + 1], wox, s)
        hf, eidx, wts = peer_route(x2d, row(ln_ffn[l]), wpq, sk)
        idx_flat = eidx.reshape(PEER_SEL, s).T.reshape(s * PEER_SEL)
        return dict(x=x2d, h=hf, idx=idx_flat, w=wts.reshape(PEER_SEL, s).T)

    def sc_dots(st):
        st["act"] = peer_dots_sc(tab_u, st["idx"], st["h"]).reshape(s, PEER_SEL)

    def sc_combine(st):
        coef = peer_coef(st["w"], st["act"])
        st["y"] = peer_combine_sc(tab_v, st["idx"], coef.reshape(s * PEER_SEL), s)

    def tc_final(st):
        return final_norm(st["x"], st["y"], row(ln_final))

    states, outs = [], []
    for step in range(b + 2):
        if step < b:
            states.append(tc_mixers(step))
            sc_dots(states[step])
        if 1 <= step <= b:
            sc_combine(states[step - 1])
        if step >= 2:
            outs.append(tc_final(states[step - 2]))
    return jnp.stack(outs, axis=0)
```

```python
import functools
import math

import jax
import jax.numpy as jnp
import numpy as np
from jax import lax
from jax.experimental import pallas as pl
from jax.experimental.pallas import tpu as pltpu
from jax.experimental.pallas import tpu_sc as plsc

F32 = jnp.float32
BF16 = jnp.bfloat16
I32 = jnp.int32
EPS = 1e-6
NEG_INF = float("-inf")

HG_HEADS = 4
HG_D = 128
HG_WIDTH = HG_HEADS * HG_D
HG_CHUNK = 64
HG_SUB = 16
MB_HEADS = 8
MB_DH = 64
MB_WIDTH = MB_HEADS * MB_DH
MB_BLOCK = 256
MB_TOPK = 3
MB_BIAS_TILES = 8
REL_BUCKETS = 32
REL_MAX_DIST = 2048
X_HEADS = 4
PEER_HEADS = 8
PEER_NKEYS = 128
PEER_TOPK = 16
PEER_HALF = 128
PEER_SEL = PEER_HEADS * PEER_TOPK

VMEM_LIMIT = 56 * 1024 * 1024


def _cparams(sem):
    return pltpu.CompilerParams(dimension_semantics=sem, vmem_limit_bytes=VMEM_LIMIT)


def _rms(x, g):
    ms = jnp.mean(x * x, axis=-1, keepdims=True)
    return x * lax.rsqrt(ms + EPS) * g


def _in_proj_kernel(x_ref, g_ref, w0_ref, w1_ref, wvt_ref, w2_ref, o0_ref, o1_ref, ovt_ref, o2_ref):
    h = _rms(x_ref[...], g_ref[...]).astype(BF16)
    o0_ref[...] = jnp.dot(h, w0_ref[...], preferred_element_type=F32)
    o1_ref[...] = jnp.dot(h, w1_ref[...], preferred_element_type=F32).astype(BF16)
    ovt_ref[0] = lax.dot_general(wvt_ref[...], h, (((1,), (1,)), ((), ())),
                                 preferred_element_type=F32).astype(BF16)
    o2_ref[...] = jnp.dot(h, w2_ref[...], preferred_element_type=F32).astype(BF16)


def in_proj(x2d, g, w0, w1, wvt, w2):
    t, d = x2d.shape
    tm = MB_BLOCK
    n0, n1, nv, n2 = w0.shape[1], w1.shape[1], wvt.shape[0], w2.shape[1]
    full = lambda a: pl.BlockSpec(a.shape, lambda i: (0, 0))
    return pl.pallas_call(
        _in_proj_kernel,
        grid=(t // tm,),
        in_specs=[pl.BlockSpec((tm, d), lambda i: (i, 0)), full(g), full(w0), full(w1), full(wvt), full(w2)],
        out_specs=[pl.BlockSpec((tm, n0), lambda i: (i, 0)),
                   pl.BlockSpec((tm, n1), lambda i: (i, 0)),
                   pl.BlockSpec((1, nv, tm), lambda i: (i, 0, 0)),
                   pl.BlockSpec((tm, n2), lambda i: (i, 0))],
        out_shape=[jax.ShapeDtypeStruct((t, n0), F32),
                   jax.ShapeDtypeStruct((t, n1), BF16),
                   jax.ShapeDtypeStruct((t // tm, nv, tm), BF16),
                   jax.ShapeDtypeStruct((t, n2), BF16)],
        compiler_params=_cparams(("parallel",)),
        name="in_proj",
    )(x2d, g, w0, w1, wvt, w2)


def _hgrn_kernel(q_ref, f_ref, i_ref, g_ref, lb_ref, gain_ref, o_ref, st_ref):
    c = pl.program_id(1)

    @pl.when(c == 0)
    def _():
        st_ref[...] = jnp.zeros_like(st_ref)

    C, S = HG_CHUNK, HG_SUB
    row = lax.broadcasted_iota(I32, (C, C), 0)
    col = lax.broadcasted_iota(I32, (C, C), 1)
    tril = (row >= col).astype(F32)
    t_iota = lax.broadcasted_iota(I32, (S, 1), 0)

    for h in range(HG_HEADS):
        sl = slice(h * HG_D, (h + 1) * HG_D)
        q = q_ref[:, sl]
        v = i_ref[:, sl]
        lb = lb_ref[:, sl]
        f = lb + (1.0 - lb) * jax.nn.sigmoid(f_ref[:, sl])
        lf = jnp.log(f)
        k = 1.0 - f
        b = jnp.dot(tril, lf, precision=lax.Precision.HIGHEST, preferred_element_type=F32)
        st = st_ref[h]
        vb = v.astype(BF16)
        qd = (q * jnp.exp(b)).astype(BF16)
        o_inter = lax.dot_general(qd, st.astype(BF16), (((1,), (1,)), ((), ())),
                                  preferred_element_type=F32)
        outs = []
        for i in range(C // S):
            r0 = i * S
            qi = q[r0:r0 + S]
            ki = k[r0:r0 + S]
            bi = b[r0:r0 + S]
            vi = v[r0:r0 + S]
            oi = o_inter[r0:r0 + S]
            if i > 0:
                bs = b[r0 - 1:r0]
                qh = (qi * jnp.exp(bi - bs)).astype(BF16)
                kh = (k[:r0] * jnp.exp(bs - b[:r0])).astype(BF16)
                a = lax.dot_general(qh, kh, (((1,), (1,)), ((), ())), preferred_element_type=F32)
                oi = oi + jnp.dot(a.astype(BF16), vb[:r0], preferred_element_type=F32)
            for s in range(S):
                dec = jnp.exp(jnp.minimum(bi - bi[s:s + 1], 0.0))
                p = qi * ki[s:s + 1] * dec
                a_s = jnp.sum(p, axis=-1, keepdims=True)
                a_s = jnp.where(t_iota >= s, a_s, 0.0)
                oi = oi + a_s * vi[s:s + 1]
            outs.append(oi)
        o = jnp.concatenate(outs, axis=0)
        b_end = b[C - 1:C]
        kd = (k * jnp.exp(b_end - b)).astype(BF16)
        upd = lax.dot_general(vb, kd, (((0,), (0,)), ((), ())), preferred_element_type=F32)
        st_ref[h] = st * jnp.exp(b_end) + upd
        o = o * lax.rsqrt(jnp.mean(o * o, axis=-1, keepdims=True) + EPS)
        g = g_ref[:, sl]
        o_ref[:, sl] = (o * gain_ref[:, sl] * (g * jax.nn.sigmoid(g))).astype(o_ref.dtype)


def hgrn2(p0, lb, gain, batch, seq):
    t = p0.shape[0]
    nc = seq // HG_CHUNK
    w = HG_WIDTH

    def col(j):
        return pl.BlockSpec((HG_CHUNK, w), lambda b, c, j=j: (b * nc + c, j))

    return pl.pallas_call(
        _hgrn_kernel,
        grid=(batch, nc),
        in_specs=[col(0), col(1), col(2), col(3),
                  pl.BlockSpec((1, w), lambda b, c: (0, 0)),
                  pl.BlockSpec((1, w), lambda b, c: (0, 0))],
        out_specs=pl.BlockSpec((HG_CHUNK, w), lambda b, c: (b * nc + c, 0)),
        out_shape=jax.ShapeDtypeStruct((t, w), BF16),
        scratch_shapes=[pltpu.VMEM((HG_HEADS, HG_D, HG_D), F32)],
        compiler_params=_cparams(("parallel", "arbitrary")),
        name="hgrn2",
    )(p0, p0, p0, p0, lb, gain)


def _kmean_kernel(k_ref, o_ref):
    o_ref[0] = jnp.mean(k_ref[...].astype(F32), axis=0, keepdims=True)


def moba_kmean(p1, batch, seq):
    nbt = p1.shape[0] // MB_BLOCK
    return pl.pallas_call(
        _kmean_kernel,
        grid=(nbt,),
        in_specs=[pl.BlockSpec((MB_BLOCK, MB_WIDTH), lambda i: (i, 1))],
        out_specs=pl.BlockSpec((1, 1, MB_WIDTH), lambda i: (i, 0, 0)),
        out_shape=jax.ShapeDtypeStruct((nbt, 1, MB_WIDTH), F32),
        compiler_params=_cparams(("parallel",)),
        name="moba_kmean",
    )(p1)


MB_PAIR = 4
MB_PW = MB_PAIR * MB_DH
MB_LG = 128


def _moba_kernel(q_ref, k_ref, vt_ref, km_ref, bias_ref, o_ref, *scratch):
    m_ref, l_ref, acc_ref, msk_ref = (scratch[i * MB_PAIR:(i + 1) * MB_PAIR] for i in range(4))
    qi = pl.program_id(2)
    nb = km_ref.shape[0]
    blk = MB_BLOCK
    heads = range(MB_PAIR)
    grp = lambda hh: slice((hh // 2) * MB_LG, (hh // 2 + 1) * MB_LG)
    q = q_ref[...]
    lane = lax.broadcasted_iota(I32, (blk, MB_LG), 1)
    in_head = [(lane < MB_DH) if hh % 2 == 0 else (lane >= MB_DH) for hh in heads]
    qs = q * jnp.asarray(MB_DH ** -0.5, BF16)
    qh = [jnp.where(in_head[hh], qs[:, grp(hh)], jnp.zeros((blk, MB_LG), BF16)) for hh in heads]
    nt = (((1,), (1,)), ((), ()))

    qf = q.astype(F32)
    n_io = lax.broadcasted_iota(I32, (nb, blk), 0)
    for hh in heads:
        gate = lax.dot_general(km_ref[:, grp(hh)], jnp.where(in_head[hh], qf[:, grp(hh)], 0.0), nt,
                               precision=lax.Precision.HIGHEST, preferred_element_type=F32)
        gate = jnp.where(n_io < qi, gate, NEG_INF)
        chosen = n_io < 0
        for _ in range(MB_TOPK):
            mx = jnp.max(gate, axis=0, keepdims=True)
            ix = jnp.min(jnp.where(gate == mx, n_io, nb), axis=0, keepdims=True)
            hit = n_io == ix
            chosen = chosen | (hit & (mx > NEG_INF))
            gate = jnp.where(hit, NEG_INF, gate)
        msk_ref[hh][...] = jnp.where(chosen, 0.0, NEG_INF)

    k_own = k_ref[pl.ds(pl.multiple_of(qi * blk, blk), blk), :]
    vt_own = vt_ref[qi]
    key_io = lax.broadcasted_iota(I32, (blk, blk), 0)
    qry_io = lax.broadcasted_iota(I32, (blk, blk), 1)
    own_rows = lambda r, hh: r[(hh % 2) * MB_DH:(hh % 2 + 1) * MB_DH]
    for hh in heads:
        s = lax.dot_general(k_own[:, grp(hh)], qh[hh], nt, preferred_element_type=F32) + bias_ref[hh, 0]
        s = jnp.where(key_io <= qry_io, s, NEG_INF)
        m0 = jnp.max(s, axis=0, keepdims=True)
        p = jnp.exp(s - m0)
        m_ref[hh][...] = m0
        l_ref[hh][...] = jnp.sum(p, axis=0, keepdims=True)
        r = jnp.dot(vt_own[grp(hh)], p.astype(BF16), preferred_element_type=F32)
        acc_ref[hh][...] = own_rows(r, hh)

    def past(n, carry):
        kn = k_ref[pl.ds(pl.multiple_of(n * blk, blk), blk), :]
        vtn = vt_ref[n]
        d = jnp.minimum(qi - n, MB_BIAS_TILES - 1)
        s = [lax.dot_general(kn[:, grp(hh)], qh[hh], nt, preferred_element_type=F32)
             + bias_ref[hh, d] + msk_ref[hh][pl.ds(n, 1), :] for hh in heads]
        m_old = [m_ref[hh][...] for hh in heads]
        l_old = [l_ref[hh][...] for hh in heads]
        a_old = [acc_ref[hh][...] for hh in heads]
        m_new = [jnp.maximum(m_old[hh], jnp.max(s[hh], axis=0, keepdims=True)) for hh in heads]
        alpha = [jnp.exp(m_old[hh] - m_new[hh]) for hh in heads]
        p = [jnp.exp(s[hh] - m_new[hh]) for hh in heads]
        r = [jnp.dot(vtn[grp(hh)], p[hh].astype(BF16), preferred_element_type=F32) for hh in heads]
        l_new = [alpha[hh] * l_old[hh] + jnp.sum(p[hh], axis=0, keepdims=True) for hh in heads]
        a_new = [alpha[hh] * a_old[hh] + own_rows(r[hh], hh) for hh in heads]
        for hh in heads:
            m_ref[hh][...] = m_new[hh]
            l_ref[hh][...] = l_new[hh]
            acc_ref[hh][...] = a_new[hh]
        return carry

    lax.fori_loop(0, qi, past, 0)
    out_t = jnp.concatenate([acc_ref[hh][...] / l_ref[hh][...] for hh in heads], axis=0)
    o_ref[...] = out_t.T.astype(o_ref.dtype)


def moba_attention(pqk, vt, km, bias, batch, seq):
    t = pqk.shape[0]
    nb = seq // MB_BLOCK
    groups = MB_WIDTH // MB_PW
    return pl.pallas_call(
        _moba_kernel,
        grid=(batch, groups, nb),
        in_specs=[
            pl.BlockSpec((MB_BLOCK, MB_PW), lambda b, j, i: (b * nb + i, j)),
            pl.BlockSpec((seq, MB_PW), lambda b, j, i: (b, groups + j)),
            pl.BlockSpec((nb, MB_PW, MB_BLOCK), lambda b, j, i: (b, j, 0)),
            pl.BlockSpec((None, nb, MB_PW), lambda b, j, i: (b, 0, j)),
            pl.BlockSpec((MB_PAIR, MB_BIAS_TILES, MB_BLOCK, MB_BLOCK), lambda b, j, i: (j, 0, 0, 0)),
        ],
        out_specs=pl.BlockSpec((MB_BLOCK, MB_PW), lambda b, j, i: (b * nb + i, j)),
        out_shape=jax.ShapeDtypeStruct((t, MB_WIDTH), BF16),
        scratch_shapes=(
            [pltpu.VMEM((1, MB_BLOCK), F32)] * (2 * MB_PAIR)
            + [pltpu.VMEM((MB_DH, MB_BLOCK), F32)] * MB_PAIR
            + [pltpu.VMEM((nb, MB_BLOCK), F32)] * MB_PAIR
        ),
        compiler_params=_cparams(("parallel", "parallel", "arbitrary")),
        name="moba_attn",
    )(pqk, pqk, vt, km, bias)


def _t5_bucket(dist):
    max_exact = REL_BUCKETS // 2
    scaled = jnp.log(jnp.maximum(dist, 1).astype(F32) / max_exact) / math.log(REL_MAX_DIST / max_exact)
    large = jnp.minimum(max_exact + (scaled * (REL_BUCKETS - max_exact)).astype(I32), REL_BUCKETS - 1)
    return jnp.where(dist < max_exact, dist, large)


def moba_bias_tiles(rel_bias):
    blk = MB_BLOCK
    span = 2 * blk - 1
    x = jnp.arange(span) - (blk - 1)
    dist = jnp.maximum(jnp.arange(MB_BIAS_TILES)[:, None] * blk + x[None, :], 0)
    w = rel_bias.astype(F32).T[:, _t5_bucket(dist)]
    h = w.shape[0]
    wp = jnp.pad(w, ((0, 0), (0, 0), (0, 1)))
    a = jnp.broadcast_to(wp[:, :, None, :], (h, MB_BIAS_TILES, blk, span + 1))
    a = a.reshape(h, MB_BIAS_TILES, blk * (span + 1))[:, :, :blk * span]
    return a.reshape(h, MB_BIAS_TILES, blk, span)[:, :, :, blk - 1:]


def _mix_kernel(x_ref, ya_ref, yb_ref, ga_ref, gb_ref, wa_ref, wb_ref, wo_ref, o_ref):
    za = jnp.dot(ya_ref[...], wa_ref[...], preferred_element_type=F32)
    zb = jnp.dot(yb_ref[...], wb_ref[...], preferred_element_type=F32)
    z = jax.nn.sigmoid(ga_ref[...].astype(F32)) * za + jax.nn.sigmoid(gb_ref[...].astype(F32)) * zb
    o_ref[...] = x_ref[...] + jnp.dot(z.astype(BF16), wo_ref[...], preferred_element_type=F32)


def mix_out(x2d, ya, yb, pg, wa, wb, wo, tm=256):
    t, d = x2d.shape
    w = ya.shape[1]
    return pl.pallas_call(
        _mix_kernel,
        grid=(t // tm,),
        in_specs=[
            pl.BlockSpec((tm, d), lambda i: (i, 0)),
            pl.BlockSpec((tm, w), lambda i: (i, 0)),
            pl.BlockSpec((tm, w), lambda i: (i, 0)),
            pl.BlockSpec((tm, d), lambda i: (i, 0)),
            pl.BlockSpec((tm, d), lambda i: (i, 1)),
            pl.BlockSpec((w, d), lambda i: (0, 0)),
            pl.BlockSpec((w, d), lambda i: (0, 0)),
            pl.BlockSpec((d, d), lambda i: (0, 0)),
        ],
        out_specs=pl.BlockSpec((tm, d), lambda i: (i, 0)),
        out_shape=jax.ShapeDtypeStruct((t, d), F32),
        compiler_params=_cparams(("parallel",)),
        name="mix_out",
    )(x2d, ya, yb, pg, pg, wa, wb, wo)


def _mem_kv_kernel(m_ref, g_ref, wk_ref, wv_ref, k_ref, v_ref):
    mn = _rms(m_ref[...], g_ref[...]).astype(BF16)
    k_ref[...] = jnp.dot(mn, wk_ref[...], preferred_element_type=F32).astype(BF16)
    v_ref[...] = jnp.dot(mn, wv_ref[...], preferred_element_type=F32).astype(BF16)


def mem_kv(mem, g, wk, wv):
    b, m, d = mem.shape
    spec = pl.BlockSpec((None, m, d), lambda i: (i, 0, 0))
    wspec = pl.BlockSpec((d, d), lambda i: (0, 0))
    return pl.pallas_call(
        _mem_kv_kernel,
        grid=(b,),
        in_specs=[spec, pl.BlockSpec((1, d), lambda i: (0, 0)), wspec, wspec],
        out_specs=[spec, spec],
        out_shape=[jax.ShapeDtypeStruct((b, m, d), BF16)] * 2,
        compiler_params=_cparams(("parallel",)),
        name="mem_kv",
    )(mem, g, wk, wv)


def _cross_kernel(x_ref, g_ref, wq_ref, k_ref, v_ref, wo_ref, o_ref):
    x = x_ref[...]
    d = x.shape[1]
    dh = d // X_HEADS
    h = _rms(x, g_ref[...]).astype(BF16)
    q = (jnp.dot(h, wq_ref[...], preferred_element_type=F32) * (dh ** -0.5)).astype(BF16)
    outs = []
    for hh in range(X_HEADS):
        sl = slice(hh * dh, (hh + 1) * dh)
        s = lax.dot_general(q[:, sl], k_ref[:, sl], (((1,), (1,)), ((), ())),
                            preferred_element_type=F32)
        p = jnp.exp(s - jnp.max(s, axis=1, keepdims=True))
        l = jnp.sum(p, axis=1, keepdims=True)
        o = jnp.dot(p.astype(BF16), v_ref[:, sl], preferred_element_type=F32) / l
        outs.append(o.astype(BF16))
    o = jnp.concatenate(outs, axis=1)
    o_ref[...] = x + jnp.dot(o, wo_ref[...], preferred_element_type=F32)


def cross_attn(x2d, g, wq, kx, vx, wo, seq, tm=256):
    t, d = x2d.shape
    m = kx.shape[1]
    per_b = seq // tm
    kv = pl.BlockSpec((None, m, d), lambda i: (i // per_b, 0, 0))
    wspec = pl.BlockSpec((d, d), lambda i: (0, 0))
    return pl.pallas_call(
        _cross_kernel,
        grid=(t // tm,),
        in_specs=[pl.BlockSpec((tm, d), lambda i: (i, 0)), pl.BlockSpec((1, d), lambda i: (0, 0)),
                  wspec, kv, kv, wspec],
        out_specs=pl.BlockSpec((tm, d), lambda i: (i, 0)),
        out_shape=jax.ShapeDtypeStruct((t, d), F32),
        compiler_params=_cparams(("parallel",)),
        name="cross_attn",
    )(x2d, g, wq, kx, vx, wo)


def _topk_rows(sc, k):
    n = sc.shape[0]
    io = lax.broadcasted_iota(I32, sc.shape, 0)
    vals, ids = [], []
    for _ in range(k):
        m = jnp.max(sc, axis=0, keepdims=True)
        ix = jnp.min(jnp.where(sc == m, io, n), axis=0, keepdims=True)
        vals.append(m)
        ids.append(ix)
        sc = jnp.where(io == ix, NEG_INF, sc)
    return jnp.concatenate(vals, axis=0), jnp.concatenate(ids, axis=0)


def _route_kernel(x_ref, g_ref, wq_ref, sk_ref, h_ref, idx_ref, w_ref, hb_ref):
    p = pl.program_id(1)

    @pl.when(p == 0)
    def _():
        h = _rms(x_ref[...], g_ref[...])
        h_ref[...] = h
        hb_ref[...] = h.astype(BF16)

    qh = jnp.dot(hb_ref[...], wq_ref[...], preferred_element_type=F32)
    tops = []
    for c in range(2):
        seg = qh[:, c * PEER_HALF:(c + 1) * PEER_HALF]
        sc = lax.dot_general(sk_ref[c], seg, (((1,), (1,)), ((), ())),
                             precision=lax.Precision.HIGHEST, preferred_element_type=F32)
        tops.append(_topk_rows(sc, PEER_TOPK))
    (s0, i0), (s1, i1) = tops
    cand = jnp.concatenate([s0[a:a + 1] + s1 for a in range(PEER_TOPK)], axis=0)
    cidx = jnp.concatenate([i0[a:a + 1] * PEER_NKEYS + i1 for a in range(PEER_TOPK)], axis=0)
    io = lax.broadcasted_iota(I32, cand.shape, 0)
    n = cand.shape[0]
    vals, ids = [], []
    for _ in range(PEER_TOPK):
        m = jnp.max(cand, axis=0, keepdims=True)
        px = jnp.min(jnp.where(cand == m, io, n), axis=0, keepdims=True)
        hit = io == px
        vals.append(m)
        ids.append(jnp.sum(jnp.where(hit, cidx, 0), axis=0, keepdims=True))
        cand = jnp.where(hit, NEG_INF, cand)
    sf = jnp.concatenate(vals, axis=0)
    e = jnp.exp(sf - sf[0:1])
    w_ref[...] = e / jnp.sum(e, axis=0, keepdims=True)
    idx_ref[...] = jnp.concatenate(ids, axis=0)


def peer_route(x2d, g, wq, sk, tm=256):
    t, d = x2d.shape
    ph = sk.shape[0]
    return pl.pallas_call(
        _route_kernel,
        grid=(t // tm, ph),
        in_specs=[
            pl.BlockSpec((tm, d), lambda i, p: (i, 0)),
            pl.BlockSpec((1, d), lambda i, p: (0, 0)),
            pl.BlockSpec((d, 2 * PEER_HALF), lambda i, p: (0, p)),
            pl.BlockSpec((None, 2, PEER_NKEYS, PEER_HALF), lambda i, p: (p, 0, 0, 0)),
        ],
        out_specs=[
            pl.BlockSpec((tm, d), lambda i, p: (i, 0)),
            pl.BlockSpec((None, PEER_TOPK, tm), lambda i, p: (p, 0, i)),
            pl.BlockSpec((None, PEER_TOPK, tm), lambda i, p: (p, 0, i)),
        ],
        out_shape=[jax.ShapeDtypeStruct((t, d), F32),
                   jax.ShapeDtypeStruct((ph, PEER_TOPK, t), I32),
                   jax.ShapeDtypeStruct((ph, PEER_TOPK, t), F32)],
        scratch_shapes=[pltpu.VMEM((tm, d), BF16)],
        compiler_params=_cparams(("parallel", "arbitrary")),
        name="peer_route",
    )(x2d, g, wq, sk)


def _coef_kernel(w_ref, a_ref, o_ref):
    o_ref[...] = w_ref[...] * jax.nn.gelu(a_ref[...])


def peer_coef(w, act, tm=1024):
    t, n = w.shape
    spec = pl.BlockSpec((tm, n), lambda i: (i, 0))
    return pl.pallas_call(
        _coef_kernel, grid=(t // tm,), in_specs=[spec, spec], out_specs=spec,
        out_shape=jax.ShapeDtypeStruct((t, n), F32),
        compiler_params=_cparams(("parallel",)), name="peer_coef",
    )(w, act)


def _final_kernel(x_ref, y_ref, g_ref, o_ref):
    o_ref[...] = _rms(x_ref[...] + y_ref[...], g_ref[...])


def final_norm(x2d, y, g, tm=512):
    t, d = x2d.shape
    spec = pl.BlockSpec((tm, d), lambda i: (i, 0))
    return pl.pallas_call(
        _final_kernel, grid=(t // tm,),
        in_specs=[spec, spec, pl.BlockSpec((1, d), lambda i: (0, 0))], out_specs=spec,
        out_shape=jax.ShapeDtypeStruct((t, d), F32),
        compiler_params=_cparams(("parallel",)), name="final_norm",
    )(x2d, y, g)


SC_CORES = 2
SC_SUBCORES = 16
SC_WORKERS = SC_CORES * SC_SUBCORES
SC_LANES = 16
SC_GROUP = 16


def _sc_mesh():
    return plsc.VectorSubcoreMesh(core_axis_name="c", subcore_axis_name="s")


def _sc_params():
    return pltpu.CompilerParams(needs_layout_passes=False)


def _sc_worker_id():
    return lax.axis_index("s") * SC_CORES + lax.axis_index("c")


SC_RING = 4
SC_ROW_SUB = 8
SC_ROW_LANE = 128


def _sc_ring(n_units, start, wait, compute):
    for u in range(SC_RING - 1):
        start(u, u)

    @pl.loop(0, n_units, step=SC_RING)
    def _(uu):
        for b in range(SC_RING):
            u = uu + b
            nxt = u + (SC_RING - 1)

            @pl.when(nxt < n_units)
            def _():
                start(nxt, (b + SC_RING - 1) % SC_RING)

            wait(u, b)
            compute(u, b)


def _sc_unit_off(u):
    off = u * SC_LANES
    return off if isinstance(off, int) else pl.multiple_of(off, SC_LANES)


def _sc_row_piece(rows, r, c):
    per = SC_ROW_LANE // SC_LANES
    return rows[r, c // per, pl.ds(pl.multiple_of((c % per) * SC_LANES, SC_LANES), SC_LANES)]


def peer_dots_sc(table, idx_flat, h):
    t, d = h.shape
    nsel = PEER_SEL
    tpw = t // SC_WORKERS
    g = SC_GROUP
    groups = tpw // g
    heads = nsel // SC_LANES
    pieces = d // SC_LANES
    units = g * heads
    row_buf = pltpu.VMEM((SC_LANES, SC_ROW_SUB, SC_ROW_LANE), F32)

    @functools.partial(
        pl.kernel, mesh=_sc_mesh(),
        out_type=jax.ShapeDtypeStruct((t * nsel,), F32),
        scratch_types=[
            pltpu.VMEM((g * nsel,), I32),
            pltpu.VMEM((g, d), F32),
            pltpu.VMEM((g * nsel,), F32),
            pltpu.VMEM((SC_LANES * SC_LANES,), F32),
            [row_buf] * SC_RING,
            [pltpu.SemaphoreType.DMA] * SC_RING,
        ],
        compiler_params=_sc_params(),
        name="peer_dots_sc",
    )
    def k(tab_hbm, idx_hbm, h_hbm, out_hbm, idx_v, h_v, out_v, red_v, rows, sems):
        wid = _sc_worker_id()
        lane = lax.iota(I32, SC_LANES)

        def copy(u, slot):
            ids = idx_v.at[pl.ds(_sc_unit_off(u), SC_LANES)]
            return pltpu.make_async_copy(tab_hbm.at[ids], rows[slot], sems[slot])

        def compute(u, slot):
            tt = u // heads

            def body(c, accs):
                hv = h_v[tt, pl.ds(pl.multiple_of(c * SC_LANES, SC_LANES), SC_LANES)]
                return tuple(accs[r] + _sc_row_piece(rows[slot], r, c) * hv for r in range(SC_LANES))

            accs = lax.fori_loop(0, pieces, body,
                                 tuple(jnp.zeros((SC_LANES,), F32) for _ in range(SC_LANES)))
            for r in range(SC_LANES):
                red_v[pl.ds(r * SC_LANES, SC_LANES)] = accs[r]
            cols = [plsc.load_gather(red_v, [lane * SC_LANES + j]) for j in range(SC_LANES)]
            while len(cols) > 1:
                cols = [cols[i] + cols[i + 1] for i in range(0, len(cols), 2)]
            out_v[pl.ds(_sc_unit_off(u), SC_LANES)] = cols[0]

        @pl.loop(0, groups)
        def _(gi):
            base = wid * tpw + gi * g
            pltpu.sync_copy(idx_hbm.at[pl.ds(base * nsel, g * nsel)], idx_v)
            pltpu.sync_copy(h_hbm.at[pl.ds(base, g)], h_v)
            _sc_ring(units, lambda u, s: copy(u, s).start(), lambda u, s: copy(u, s).wait(), compute)
            pltpu.sync_copy(out_v, out_hbm.at[pl.ds(base * nsel, g * nsel)])

    return k(table, idx_flat, h)


def peer_combine_sc(table, idx_flat, coef_flat, t):
    d = table.shape[1] * table.shape[2]
    nsel = PEER_SEL
    tpw = t // SC_WORKERS
    g = SC_GROUP
    groups = tpw // g
    heads = nsel // SC_LANES
    pieces = d // SC_LANES
    units = g * heads
    row_buf = pltpu.VMEM((SC_LANES, SC_ROW_SUB, SC_ROW_LANE), F32)

    @functools.partial(
        pl.kernel, mesh=_sc_mesh(),
        out_type=jax.ShapeDtypeStruct((t, d), F32),
        scratch_types=[
            pltpu.VMEM((g * nsel,), I32),
            pltpu.VMEM((g * nsel,), F32),
            pltpu.VMEM((g, d), F32),
            [row_buf] * SC_RING,
            [pltpu.SemaphoreType.DMA] * SC_RING,
        ],
        compiler_params=_sc_params(),
        name="peer_combine_sc",
    )
    def k(tab_hbm, idx_hbm, coef_hbm, out_hbm, idx_v, coef_v, y_v, rows, sems):
        wid = _sc_worker_id()

        def copy(u, slot):
            ids = idx_v.at[pl.ds(_sc_unit_off(u), SC_LANES)]
            return pltpu.make_async_copy(tab_hbm.at[ids], rows[slot], sems[slot])

        def compute(u, slot):
            tt = u // heads
            first = (u % heads) == 0
            cs = [plsc.load_gather(coef_v, [jnp.full((SC_LANES,), u * SC_LANES + r, I32)])
                  for r in range(SC_LANES)]

            @plsc.parallel_loop(0, pieces, unroll=2)
            def _(c):
                off = pl.multiple_of(c * SC_LANES, SC_LANES)
                terms = [cs[r] * _sc_row_piece(rows[slot], r, c) for r in range(SC_LANES)]
                while len(terms) > 1:
                    terms = [terms[i] + terms[i + 1] for i in range(0, len(terms), 2)]
                prev = y_v[tt, pl.ds(off, SC_LANES)]
                y_v[tt, pl.ds(off, SC_LANES)] = terms[0] + jnp.where(first, 0.0, prev)

        @pl.loop(0, groups)
        def _(gi):
            base = wid * tpw + gi * g
            pltpu.sync_copy(idx_hbm.at[pl.ds(base * nsel, g * nsel)], idx_v)
            pltpu.sync_copy(coef_hbm.at[pl.ds(base * nsel, g * nsel)], coef_v)
            _sc_ring(units, lambda u, s: copy(u, s).start(), lambda u, s: copy(u, s).wait(), compute)
            pltpu.sync_copy(y_v, out_hbm.at[pl.ds(base, g)])

    return k(table, idx_flat, coef_flat)


def kernel(x, mem, rel_bias, ln_mix, w_in, hg_lower, hg_norm, w_up_a, w_up_b, w_out, ln_cross, ln_mem, wq_x, wk_x, wv_x, wo_x, ln_ffn, peer_query, peer_subkeys, peer_u, peer_v, ln_final):
    b, s, d = x.shape
    depth = w_in.shape[0]
    assert depth == 1, "the residual after PEER is fused into the final norm"
    assert s % MB_BLOCK == 0 and s % HG_CHUNK == 0 and s % (SC_WORKERS * SC_GROUP) == 0
    nb = s // MB_BLOCK
    row = lambda a: a.reshape(1, -1).astype(F32)
    lb_all = jnp.cumsum(jax.nn.softmax(hg_lower.astype(F32), axis=0), axis=0)
    bias = moba_bias_tiles(rel_bias)
    n_hg = 4 * HG_WIDTH
    n_qk = 2 * MB_WIDTH
    n_mb = 3 * MB_WIDTH
    l = 0
    w = w_in[l].astype(BF16)
    w_hg, w_qk, w_vt, w_g = w[:, :n_hg], w[:, n_hg:n_hg + n_qk], w[:, n_hg + n_qk:n_hg + n_mb].T, w[:, n_hg + n_mb:]
    wa, wb, wo = w_up_a[l].astype(BF16), w_up_b[l].astype(BF16), w_out[l].astype(BF16)
    wqx, wox = wq_x[l].astype(BF16), wo_x[l].astype(BF16)
    wpq, sk = peer_query[l].astype(BF16), peer_subkeys[l].astype(F32)
    tab3 = lambda a: a.astype(F32).reshape(a.shape[0], SC_ROW_SUB, SC_ROW_LANE)
    tab_u, tab_v = tab3(peer_u[l]), tab3(peer_v[l])
    kx, vx = mem_kv(mem, row(ln_mem[l]), wk_x[l].astype(BF16), wv_x[l].astype(BF16))

    def tc_mixers(bi, x2d):
        p0, pqk, vt, pg = in_proj(x2d, row(ln_mix[l]), w_hg, w_qk, w_vt, w_g)
        ya = hgrn2(p0, row(lb_all[l]), row(hg_norm[l]), 1, s)
        km = moba_kmean(pqk, 1, s).reshape(1, nb, MB_WIDTH)
        yb = moba_attention(pqk, vt, km, bias, 1, s)
        x2d = mix_out(x2d, ya, yb, pg, wa, wb, wo)
        x2d = cross_attn(x2d, row(ln_cross[l]), wqx, kx[bi:bi + 1], vx[bi:bi + 1], wox, s)
        hf, eidx, wts = peer_route(x2d, row(ln_ffn[l]), wpq, sk)
        idx_flat = eidx.reshape(PEER_SEL, s).T.reshape(s * PEER_SEL)
        return dict(x=x2d, h=hf, idx=idx_flat, w=wts.reshape(PEER_SEL, s).T)

    def sc_dots(st):
        st["act"] = peer_dots_sc(tab_u, st["idx"], st["h"]).reshape(s, PEER_SEL)

    def sc_combine(st):
        st["coef"] = peer_coef(st["w"], st["act"])
        st["y"] = peer_combine_sc(tab_v, st["idx"], st["coef"].reshape(s * PEER_SEL), s)

    def tc_final(st):
        return final_norm(st["x"], st["y"], row(ln_final))

    states, outs = [], []
    for step in range(b + 2):
        if step < b:
            x2d = x[step]
            if step >= 2:
                x2d, _ = lax.optimization_barrier((x2d, states[step - 2]["coef"]))
            states.append(tc_mixers(step, x2d))
            sc_dots(states[step])
        if 1 <= step <= b:
            sc_combine(states[step - 1])
        if step >= 2:
            outs.append(tc_final(states[step - 2]))
    return jnp.stack(outs, axis=0)
```

```python
import functools
import math

import jax
import jax.numpy as jnp
import numpy as np
from jax import lax
from jax.experimental import pallas as pl
from jax.experimental.pallas import tpu as pltpu
from jax.experimental.pallas import tpu_sc as plsc

F32 = jnp.float32
BF16 = jnp.bfloat16
I32 = jnp.int32
EPS = 1e-6
NEG_INF = float("-inf")

HG_HEADS = 4
HG_D = 128
HG_WIDTH = HG_HEADS * HG_D
HG_CHUNK = 64
HG_SUB = 16
MB_HEADS = 8
MB_DH = 64
MB_WIDTH = MB_HEADS * MB_DH
MB_BLOCK = 256
MB_TOPK = 3
MB_BIAS_TILES = 8
REL_BUCKETS = 32
REL_MAX_DIST = 2048
X_HEADS = 4
PEER_HEADS = 8
PEER_NKEYS = 128
PEER_TOPK = 16
PEER_HALF = 128
PEER_SEL = PEER_HEADS * PEER_TOPK

VMEM_LIMIT = 56 * 1024 * 1024


def _cparams(sem):
    return pltpu.CompilerParams(dimension_semantics=sem, vmem_limit_bytes=VMEM_LIMIT)


def _rms(x, g):
    ms = jnp.mean(x * x, axis=-1, keepdims=True)
    return x * lax.rsqrt(ms + EPS) * g


def _in_proj_kernel(x_ref, g_ref, w0_ref, w1_ref, wvt_ref, w2_ref, o0_ref, o1_ref, ovt_ref, o2_ref):
    h = _rms(x_ref[...], g_ref[...]).astype(BF16)
    o0_ref[...] = jnp.dot(h, w0_ref[...], preferred_element_type=F32)
    o1_ref[...] = jnp.dot(h, w1_ref[...], preferred_element_type=F32).astype(BF16)
    ovt_ref[0] = lax.dot_general(wvt_ref[...], h, (((1,), (1,)), ((), ())),
                                 preferred_element_type=F32).astype(BF16)
    o2_ref[...] = jnp.dot(h, w2_ref[...], preferred_element_type=F32).astype(BF16)


def in_proj(x2d, g, w0, w1, wvt, w2):
    t, d = x2d.shape
    tm = MB_BLOCK
    n0, n1, nv, n2 = w0.shape[1], w1.shape[1], wvt.shape[0], w2.shape[1]
    full = lambda a: pl.BlockSpec(a.shape, lambda i: (0, 0))
    return pl.pallas_call(
        _in_proj_kernel,
        grid=(t // tm,),
        in_specs=[pl.BlockSpec((tm, d), lambda i: (i, 0)), full(g), full(w0), full(w1), full(wvt), full(w2)],
        out_specs=[pl.BlockSpec((tm, n0), lambda i: (i, 0)),
                   pl.BlockSpec((tm, n1), lambda i: (i, 0)),
                   pl.BlockSpec((1, nv, tm), lambda i: (i, 0, 0)),
                   pl.BlockSpec((tm, n2), lambda i: (i, 0))],
        out_shape=[jax.ShapeDtypeStruct((t, n0), F32),
                   jax.ShapeDtypeStruct((t, n1), BF16),
                   jax.ShapeDtypeStruct((t // tm, nv, tm), BF16),
                   jax.ShapeDtypeStruct((t, n2), BF16)],
        compiler_params=_cparams(("parallel",)),
        name="in_proj",
    )(x2d, g, w0, w1, wvt, w2)


def _hgrn_kernel(q_ref, f_ref, i_ref, g_ref, lb_ref, gain_ref, o_ref, st_ref):
    c = pl.program_id(1)

    @pl.when(c == 0)
    def _():
        st_ref[...] = jnp.zeros_like(st_ref)

    C, S = HG_CHUNK, HG_SUB
    row = lax.broadcasted_iota(I32, (C, C), 0)
    col = lax.broadcasted_iota(I32, (C, C), 1)
    tril = (row >= col).astype(F32)
    t_iota = lax.broadcasted_iota(I32, (S, 1), 0)

    for h in range(HG_HEADS):
        sl = slice(h * HG_D, (h + 1) * HG_D)
        q = q_ref[:, sl]
        v = i_ref[:, sl]
        lb = lb_ref[:, sl]
        f = lb + (1.0 - lb) * jax.nn.sigmoid(f_ref[:, sl])
        lf = jnp.log(f)
        k = 1.0 - f
        b = jnp.dot(tril, lf, precision=lax.Precision.HIGHEST, preferred_element_type=F32)
        st = st_ref[h]
        vb = v.astype(BF16)
        qd = (q * jnp.exp(b)).astype(BF16)
        o_inter = lax.dot_general(qd, st.astype(BF16), (((1,), (1,)), ((), ())),
                                  preferred_element_type=F32)
        outs = []
        for i in range(C // S):
            r0 = i * S
            qi = q[r0:r0 + S]
            ki = k[r0:r0 + S]
            bi = b[r0:r0 + S]
            vi = v[r0:r0 + S]
            oi = o_inter[r0:r0 + S]
            if i > 0:
                bs = b[r0 - 1:r0]
                qh = (qi * jnp.exp(bi - bs)).astype(BF16)
                kh = (k[:r0] * jnp.exp(bs - b[:r0])).astype(BF16)
                a = lax.dot_general(qh, kh, (((1,), (1,)), ((), ())), preferred_element_type=F32)
                oi = oi + jnp.dot(a.astype(BF16), vb[:r0], preferred_element_type=F32)
            for s in range(S):
                dec = jnp.exp(jnp.minimum(bi - bi[s:s + 1], 0.0))
                p = qi * ki[s:s + 1] * dec
                a_s = jnp.sum(p, axis=-1, keepdims=True)
                a_s = jnp.where(t_iota >= s, a_s, 0.0)
                oi = oi + a_s * vi[s:s + 1]
            outs.append(oi)
        o = jnp.concatenate(outs, axis=0)
        b_end = b[C - 1:C]
        kd = (k * jnp.exp(b_end - b)).astype(BF16)
        upd = lax.dot_general(vb, kd, (((0,), (0,)), ((), ())), preferred_element_type=F32)
        st_ref[h] = st * jnp.exp(b_end) + upd
        o = o * lax.rsqrt(jnp.mean(o * o, axis=-1, keepdims=True) + EPS)
        g = g_ref[:, sl]
        o_ref[:, sl] = (o * gain_ref[:, sl] * (g * jax.nn.sigmoid(g))).astype(o_ref.dtype)


def hgrn2(p0, lb, gain, batch, seq):
    t = p0.shape[0]
    nc = seq // HG_CHUNK
    w = HG_WIDTH

    def col(j):
        return pl.BlockSpec((HG_CHUNK, w), lambda b, c, j=j: (b * nc + c, j))

    return pl.pallas_call(
        _hgrn_kernel,
        grid=(batch, nc),
        in_specs=[col(0), col(1), col(2), col(3),
                  pl.BlockSpec((1, w), lambda b, c: (0, 0)),
                  pl.BlockSpec((1, w), lambda b, c: (0, 0))],
        out_specs=pl.BlockSpec((HG_CHUNK, w), lambda b, c: (b * nc + c, 0)),
        out_shape=jax.ShapeDtypeStruct((t, w), BF16),
        scratch_shapes=[pltpu.VMEM((HG_HEADS, HG_D, HG_D), F32)],
        compiler_params=_cparams(("parallel", "arbitrary")),
        name="hgrn2",
    )(p0, p0, p0, p0, lb, gain)


def _kmean_kernel(k_ref, o_ref):
    o_ref[0] = jnp.mean(k_ref[...].astype(F32), axis=0, keepdims=True)


def moba_kmean(p1, batch, seq):
    nbt = p1.shape[0] // MB_BLOCK
    return pl.pallas_call(
        _kmean_kernel,
        grid=(nbt,),
        in_specs=[pl.BlockSpec((MB_BLOCK, MB_WIDTH), lambda i: (i, 1))],
        out_specs=pl.BlockSpec((1, 1, MB_WIDTH), lambda i: (i, 0, 0)),
        out_shape=jax.ShapeDtypeStruct((nbt, 1, MB_WIDTH), F32),
        compiler_params=_cparams(("parallel",)),
        name="moba_kmean",
    )(p1)


MB_PAIR = 4
MB_PW = MB_PAIR * MB_DH
MB_LG = 128


def _moba_kernel(q_ref, k_ref, vt_ref, km_ref, bias_ref, o_ref, *scratch):
    m_ref, l_ref, acc_ref, msk_ref = (scratch[i * MB_PAIR:(i + 1) * MB_PAIR] for i in range(4))
    qi = pl.program_id(2)
    nb = km_ref.shape[0]
    blk = MB_BLOCK
    heads = range(MB_PAIR)
    grp = lambda hh: slice((hh // 2) * MB_LG, (hh // 2 + 1) * MB_LG)
    q = q_ref[...]
    lane = lax.broadcasted_iota(I32, (blk, MB_LG), 1)
    in_head = [(lane < MB_DH) if hh % 2 == 0 else (lane >= MB_DH) for hh in heads]
    qs = q * jnp.asarray(MB_DH ** -0.5, BF16)
    qh = [jnp.where(in_head[hh], qs[:, grp(hh)], jnp.zeros((blk, MB_LG), BF16)) for hh in heads]
    nt = (((1,), (1,)), ((), ()))

    qf = q.astype(F32)
    n_io = lax.broadcasted_iota(I32, (nb, blk), 0)
    for hh in heads:
        gate = lax.dot_general(km_ref[:, grp(hh)], jnp.where(in_head[hh], qf[:, grp(hh)], 0.0), nt,
                               precision=lax.Precision.HIGHEST, preferred_element_type=F32)
        gate = jnp.where(n_io < qi, gate, NEG_INF)
        chosen = n_io < 0
        for _ in range(MB_TOPK):
            mx = jnp.max(gate, axis=0, keepdims=True)
            ix = jnp.min(jnp.where(gate == mx, n_io, nb), axis=0, keepdims=True)
            hit = n_io == ix
            chosen = chosen | (hit & (mx > NEG_INF))
            gate = jnp.where(hit, NEG_INF, gate)
        msk_ref[hh][...] = jnp.where(chosen, 0.0, NEG_INF)

    k_own = k_ref[pl.ds(pl.multiple_of(qi * blk, blk), blk), :]
    vt_own = vt_ref[qi]
    key_io = lax.broadcasted_iota(I32, (blk, blk), 0)
    qry_io = lax.broadcasted_iota(I32, (blk, blk), 1)
    own_rows = lambda r, hh: r[(hh % 2) * MB_DH:(hh % 2 + 1) * MB_DH]
    for hh in heads:
        s = lax.dot_general(k_own[:, grp(hh)], qh[hh], nt, preferred_element_type=F32) + bias_ref[hh, 0]
        s = jnp.where(key_io <= qry_io, s, NEG_INF)
        m0 = jnp.max(s, axis=0, keepdims=True)
        p = jnp.exp(s - m0)
        m_ref[hh][...] = m0
        l_ref[hh][...] = jnp.sum(p, axis=0, keepdims=True)
        r = jnp.dot(vt_own[grp(hh)], p.astype(BF16), preferred_element_type=F32)
        acc_ref[hh][...] = own_rows(r, hh)

    def past(n, carry):
        kn = k_ref[pl.ds(pl.multiple_of(n * blk, blk), blk), :]
        vtn = vt_ref[n]
        d = jnp.minimum(qi - n, MB_BIAS_TILES - 1)
        s = [lax.dot_general(kn[:, grp(hh)], qh[hh], nt, preferred_element_type=F32)
             + bias_ref[hh, d] + msk_ref[hh][pl.ds(n, 1), :] for hh in heads]
        m_old = [m_ref[hh][...] for hh in heads]
        l_old = [l_ref[hh][...] for hh in heads]
        a_old = [acc_ref[hh][...] for hh in heads]
        m_new = [jnp.maximum(m_old[hh], jnp.max(s[hh], axis=0, keepdims=True)) for hh in heads]
        alpha = [jnp.exp(m_old[hh] - m_new[hh]) for hh in heads]
        p = [jnp.exp(s[hh] - m_new[hh]) for hh in heads]
        r = [jnp.dot(vtn[grp(hh)], p[hh].astype(BF16), preferred_element_type=F32) for hh in heads]
        l_new = [alpha[hh] * l_old[hh] + jnp.sum(p[hh], axis=0, keepdims=True) for hh in heads]
        a_new = [alpha[hh] * a_old[hh] + own_rows(r[hh], hh) for hh in heads]
        for hh in heads:
            m_ref[hh][...] = m_new[hh]
            l_ref[hh][...] = l_new[hh]
            acc_ref[hh][...] = a_new[hh]
        return carry

    lax.fori_loop(0, qi, past, 0)
    out_t = jnp.concatenate([acc_ref[hh][...] / l_ref[hh][...] for hh in heads], axis=0)
    o_ref[...] = out_t.T.astype(o_ref.dtype)


def moba_attention(pqk, vt, km, bias, batch, seq):
    t = pqk.shape[0]
    nb = seq // MB_BLOCK
    groups = MB_WIDTH // MB_PW
    return pl.pallas_call(
        _moba_kernel,
        grid=(batch, groups, nb),
        in_specs=[
            pl.BlockSpec((MB_BLOCK, MB_PW), lambda b, j, i: (b * nb + i, j)),
            pl.BlockSpec((seq, MB_PW), lambda b, j, i: (b, groups + j)),
            pl.BlockSpec((nb, MB_PW, MB_BLOCK), lambda b, j, i: (b, j, 0)),
            pl.BlockSpec((None, nb, MB_PW), lambda b, j, i: (b, 0, j)),
            pl.BlockSpec((MB_PAIR, MB_BIAS_TILES, MB_BLOCK, MB_BLOCK), lambda b, j, i: (j, 0, 0, 0)),
        ],
        out_specs=pl.BlockSpec((MB_BLOCK, MB_PW), lambda b, j, i: (b * nb + i, j)),
        out_shape=jax.ShapeDtypeStruct((t, MB_WIDTH), BF16),
        scratch_shapes=(
            [pltpu.VMEM((1, MB_BLOCK), F32)] * (2 * MB_PAIR)
            + [pltpu.VMEM((MB_DH, MB_BLOCK), F32)] * MB_PAIR
            + [pltpu.VMEM((nb, MB_BLOCK), F32)] * MB_PAIR
        ),
        compiler_params=_cparams(("parallel", "parallel", "arbitrary")),
        name="moba_attn",
    )(pqk, pqk, vt, km, bias)


def _t5_bucket(dist):
    max_exact = REL_BUCKETS // 2
    scaled = jnp.log(jnp.maximum(dist, 1).astype(F32) / max_exact) / math.log(REL_MAX_DIST / max_exact)
    large = jnp.minimum(max_exact + (scaled * (REL_BUCKETS - max_exact)).astype(I32), REL_BUCKETS - 1)
    return jnp.where(dist < max_exact, dist, large)


def moba_bias_tiles(rel_bias):
    blk = MB_BLOCK
    span = 2 * blk - 1
    x = jnp.arange(span) - (blk - 1)
    dist = jnp.maximum(jnp.arange(MB_BIAS_TILES)[:, None] * blk + x[None, :], 0)
    w = rel_bias.astype(F32).T[:, _t5_bucket(dist)]
    h = w.shape[0]
    wp = jnp.pad(w, ((0, 0), (0, 0), (0, 1)))
    a = jnp.broadcast_to(wp[:, :, None, :], (h, MB_BIAS_TILES, blk, span + 1))
    a = a.reshape(h, MB_BIAS_TILES, blk * (span + 1))[:, :, :blk * span]
    return a.reshape(h, MB_BIAS_TILES, blk, span)[:, :, :, blk - 1:]


def _mix_kernel(x_ref, ya_ref, yb_ref, ga_ref, gb_ref, wa_ref, wb_ref, wo_ref, o_ref):
    za = jnp.dot(ya_ref[...], wa_ref[...], preferred_element_type=F32)
    zb = jnp.dot(yb_ref[...], wb_ref[...], preferred_element_type=F32)
    z = jax.nn.sigmoid(ga_ref[...].astype(F32)) * za + jax.nn.sigmoid(gb_ref[...].astype(F32)) * zb
    o_ref[...] = x_ref[...] + jnp.dot(z.astype(BF16), wo_ref[...], preferred_element_type=F32)


def mix_out(x2d, ya, yb, pg, wa, wb, wo, tm=256):
    t, d = x2d.shape
    w = ya.shape[1]
    return pl.pallas_call(
        _mix_kernel,
        grid=(t // tm,),
        in_specs=[
            pl.BlockSpec((tm, d), lambda i: (i, 0)),
            pl.BlockSpec((tm, w), lambda i: (i, 0)),
            pl.BlockSpec((tm, w), lambda i: (i, 0)),
            pl.BlockSpec((tm, d), lambda i: (i, 0)),
            pl.BlockSpec((tm, d), lambda i: (i, 1)),
            pl.BlockSpec((w, d), lambda i: (0, 0)),
            pl.BlockSpec((w, d), lambda i: (0, 0)),
            pl.BlockSpec((d, d), lambda i: (0, 0)),
        ],
        out_specs=pl.BlockSpec((tm, d), lambda i: (i, 0)),
        out_shape=jax.ShapeDtypeStruct((t, d), F32),
        compiler_params=_cparams(("parallel",)),
        name="mix_out",
    )(x2d, ya, yb, pg, pg, wa, wb, wo)


def _mem_kv_kernel(m_ref, g_ref, wk_ref, wv_ref, k_ref, v_ref):
    mn = _rms(m_ref[...], g_ref[...]).astype(BF16)
    k_ref[...] = jnp.dot(mn, wk_ref[...], preferred_element_type=F32).astype(BF16)
    v_ref[...] = jnp.dot(mn, wv_ref[...], preferred_element_type=F32).astype(BF16)


def mem_kv(mem, g, wk, wv):
    b, m, d = mem.shape
    spec = pl.BlockSpec((None, m, d), lambda i: (i, 0, 0))
    wspec = pl.BlockSpec((d, d), lambda i: (0, 0))
    return pl.pallas_call(
        _mem_kv_kernel,
        grid=(b,),
        in_specs=[spec, pl.BlockSpec((1, d), lambda i: (0, 0)), wspec, wspec],
        out_specs=[spec, spec],
        out_shape=[jax.ShapeDtypeStruct((b, m, d), BF16)] * 2,
        compiler_params=_cparams(("parallel",)),
        name="mem_kv",
    )(mem, g, wk, wv)


def _cross_kernel(x_ref, g_ref, wq_ref, k_ref, v_ref, wo_ref, o_ref):
    x = x_ref[...]
    d = x.shape[1]
    dh = d // X_HEADS
    h = _rms(x, g_ref[...]).astype(BF16)
    q = (jnp.dot(h, wq_ref[...], preferred_element_type=F32) * (dh ** -0.5)).astype(BF16)
    outs = []
    for hh in range(X_HEADS):
        sl = slice(hh * dh, (hh + 1) * dh)
        s = lax.dot_general(q[:, sl], k_ref[:, sl], (((1,), (1,)), ((), ())),
                            preferred_element_type=F32)
        p = jnp.exp(s - jnp.max(s, axis=1, keepdims=True))
        l = jnp.sum(p, axis=1, keepdims=True)
        o = jnp.dot(p.astype(BF16), v_ref[:, sl], preferred_element_type=F32) / l
        outs.append(o.astype(BF16))
    o = jnp.concatenate(outs, axis=1)
    o_ref[...] = x + jnp.dot(o, wo_ref[...], preferred_element_type=F32)


def cross_attn(x2d, g, wq, kx, vx, wo, seq, tm=256):
    t, d = x2d.shape
    m = kx.shape[1]
    per_b = seq // tm
    kv = pl.BlockSpec((None, m, d), lambda i: (i // per_b, 0, 0))
    wspec = pl.BlockSpec((d, d), lambda i: (0, 0))
    return pl.pallas_call(
        _cross_kernel,
        grid=(t // tm,),
        in_specs=[pl.BlockSpec((tm, d), lambda i: (i, 0)), pl.BlockSpec((1, d), lambda i: (0, 0)),
                  wspec, kv, kv, wspec],
        out_specs=pl.BlockSpec((tm, d), lambda i: (i, 0)),
        out_shape=jax.ShapeDtypeStruct((t, d), F32),
        compiler_params=_cparams(("parallel",)),
        name="cross_attn",
    )(x2d, g, wq, kx, vx, wo)


def _topk_rows(sc, k):
    n = sc.shape[0]
    io = lax.broadcasted_iota(I32, sc.shape, 0)
    vals, ids = [], []
    for _ in range(k):
        m = jnp.max(sc, axis=0, keepdims=True)
        ix = jnp.min(jnp.where(sc == m, io, n), axis=0, keepdims=True)
        vals.append(m)
        ids.append(ix)
        sc = jnp.where(io == ix, NEG_INF, sc)
    return jnp.concatenate(vals, axis=0), jnp.concatenate(ids, axis=0)


def _route_kernel(x_ref, g_ref, wq_ref, sk_ref, h_ref, idx_ref, w_ref, hb_ref):
    p = pl.program_id(1)

    @pl.when(p == 0)
    def _():
        h = _rms(x_ref[...], g_ref[...])
        h_ref[...] = h
        hb_ref[...] = h.astype(BF16)

    qh = jnp.dot(hb_ref[...], wq_ref[...], preferred_element_type=F32)
    tops = []
    for c in range(2):
        seg = qh[:, c * PEER_HALF:(c + 1) * PEER_HALF]
        sc = lax.dot_general(sk_ref[c], seg, (((1,), (1,)), ((), ())),
                             precision=lax.Precision.HIGHEST, preferred_element_type=F32)
        tops.append(_topk_rows(sc, PEER_TOPK))
    (s0, i0), (s1, i1) = tops
    cand = jnp.concatenate([s0[a:a + 1] + s1 for a in range(PEER_TOPK)], axis=0)
    cidx = jnp.concatenate([i0[a:a + 1] * PEER_NKEYS + i1 for a in range(PEER_TOPK)], axis=0)
    io = lax.broadcasted_iota(I32, cand.shape, 0)
    n = cand.shape[0]
    vals, ids = [], []
    for _ in range(PEER_TOPK):
        m = jnp.max(cand, axis=0, keepdims=True)
        px = jnp.min(jnp.where(cand == m, io, n), axis=0, keepdims=True)
        hit = io == px
        vals.append(m)
        ids.append(jnp.sum(jnp.where(hit, cidx, 0), axis=0, keepdims=True))
        cand = jnp.where(hit, NEG_INF, cand)
    sf = jnp.concatenate(vals, axis=0)
    e = jnp.exp(sf - sf[0:1])
    w_ref[...] = e / jnp.sum(e, axis=0, keepdims=True)
    idx_ref[...] = jnp.concatenate(ids, axis=0)


def peer_route(x2d, g, wq, sk, tm=256):
    t, d = x2d.shape
    ph = sk.shape[0]
    return pl.pallas_call(
        _route_kernel,
        grid=(t // tm, ph),
        in_specs=[
            pl.BlockSpec((tm, d), lambda i, p: (i, 0)),
            pl.BlockSpec((1, d), lambda i, p: (0, 0)),
            pl.BlockSpec((d, 2 * PEER_HALF), lambda i, p: (0, p)),
            pl.BlockSpec((None, 2, PEER_NKEYS, PEER_HALF), lambda i, p: (p, 0, 0, 0)),
        ],
        out_specs=[
            pl.BlockSpec((tm, d), lambda i, p: (i, 0)),
            pl.BlockSpec((None, PEER_TOPK, tm), lambda i, p: (p, 0, i)),
            pl.BlockSpec((None, PEER_TOPK, tm), lambda i, p: (p, 0, i)),
        ],
        out_shape=[jax.ShapeDtypeStruct((t, d), F32),
                   jax.ShapeDtypeStruct((ph, PEER_TOPK, t), I32),
                   jax.ShapeDtypeStruct((ph, PEER_TOPK, t), F32)],
        scratch_shapes=[pltpu.VMEM((tm, d), BF16)],
        compiler_params=_cparams(("parallel", "arbitrary")),
        name="peer_route",
    )(x2d, g, wq, sk)


def _coef_kernel(w_ref, a_ref, o_ref):
    o_ref[...] = w_ref[...] * jax.nn.gelu(a_ref[...])


def peer_coef(w, act, tm=1024):
    t, n = w.shape
    spec = pl.BlockSpec((tm, n), lambda i: (i, 0))
    return pl.pallas_call(
        _coef_kernel, grid=(t // tm,), in_specs=[spec, spec], out_specs=spec,
        out_shape=jax.ShapeDtypeStruct((t, n), F32),
        compiler_params=_cparams(("parallel",)), name="peer_coef",
    )(w, act)


def _final_kernel(x_ref, y_ref, g_ref, o_ref):
    o_ref[...] = _rms(x_ref[...] + y_ref[...], g_ref[...])


def final_norm(x2d, y, g, tm=512):
    t, d = x2d.shape
    spec = pl.BlockSpec((tm, d), lambda i: (i, 0))
    return pl.pallas_call(
        _final_kernel, grid=(t // tm,),
        in_specs=[spec, spec, pl.BlockSpec((1, d), lambda i: (0, 0))], out_specs=spec,
        out_shape=jax.ShapeDtypeStruct((t, d), F32),
        compiler_params=_cparams(("parallel",)), name="final_norm",
    )(x2d, y, g)


SC_CORES = 2
SC_SUBCORES = 16
SC_WORKERS = SC_CORES * SC_SUBCORES
SC_LANES = 16
SC_GROUP = 16


def _sc_mesh():
    return plsc.VectorSubcoreMesh(core_axis_name="c", subcore_axis_name="s")


def _sc_params():
    return pltpu.CompilerParams(needs_layout_passes=False)


def _sc_worker_id():
    return lax.axis_index("s") * SC_CORES + lax.axis_index("c")


SC_RING = 4
SC_ROW_SUB = 8
SC_ROW_LANE = 128


def _sc_ring(n_units, start, wait, compute):
    for u in range(SC_RING - 1):
        start(u, u)

    @pl.loop(0, n_units, step=SC_RING)
    def _(uu):
        for b in range(SC_RING):
            u = uu + b
            nxt = u + (SC_RING - 1)

            @pl.when(nxt < n_units)
            def _():
                start(nxt, (b + SC_RING - 1) % SC_RING)

            wait(u, b)
            compute(u, b)


def _sc_unit_off(u):
    off = u * SC_LANES
    return off if isinstance(off, int) else pl.multiple_of(off, SC_LANES)


def _sc_row_piece(rows, r, c):
    per = SC_ROW_LANE // SC_LANES
    return rows[r, c // per, pl.ds(pl.multiple_of((c % per) * SC_LANES, SC_LANES), SC_LANES)]


def peer_dots_sc(table, idx_flat, h):
    t, d = h.shape
    nsel = PEER_SEL
    tpw = t // SC_WORKERS
    g = SC_GROUP
    groups = tpw // g
    heads = nsel // SC_LANES
    pieces = d // SC_LANES
    units = g * heads
    row_buf = pltpu.VMEM((SC_LANES, SC_ROW_SUB, SC_ROW_LANE), F32)

    @functools.partial(
        pl.kernel, mesh=_sc_mesh(),
        out_type=jax.ShapeDtypeStruct((t * nsel,), F32),
        scratch_types=[
            pltpu.VMEM((g * nsel,), I32),
            pltpu.VMEM((g, d), F32),
            pltpu.VMEM((g * nsel,), F32),
            pltpu.VMEM((SC_LANES * SC_LANES,), F32),
            [row_buf] * SC_RING,
            [pltpu.SemaphoreType.DMA] * SC_RING,
        ],
        compiler_params=_sc_params(),
        name="peer_dots_sc",
    )
    def k(tab_hbm, idx_hbm, h_hbm, out_hbm, idx_v, h_v, out_v, red_v, rows, sems):
        wid = _sc_worker_id()
        lane = lax.iota(I32, SC_LANES)

        def copy(u, slot):
            ids = idx_v.at[pl.ds(_sc_unit_off(u), SC_LANES)]
            return pltpu.make_async_copy(tab_hbm.at[ids], rows[slot], sems[slot])

        def compute(u, slot):
            tt = u // heads

            def body(c, accs):
                hv = h_v[tt, pl.ds(pl.multiple_of(c * SC_LANES, SC_LANES), SC_LANES)]
                return tuple(accs[r] + _sc_row_piece(rows[slot], r, c) * hv for r in range(SC_LANES))

            accs = lax.fori_loop(0, pieces, body,
                                 tuple(jnp.zeros((SC_LANES,), F32) for _ in range(SC_LANES)))
            for r in range(SC_LANES):
                red_v[pl.ds(r * SC_LANES, SC_LANES)] = accs[r]
            cols = [plsc.load_gather(red_v, [lane * SC_LANES + j]) for j in range(SC_LANES)]
            while len(cols) > 1:
                cols = [cols[i] + cols[i + 1] for i in range(0, len(cols), 2)]
            out_v[pl.ds(_sc_unit_off(u), SC_LANES)] = cols[0]

        @pl.loop(0, groups)
        def _(gi):
            base = wid * tpw + gi * g
            pltpu.sync_copy(idx_hbm.at[pl.ds(base * nsel, g * nsel)], idx_v)
            pltpu.sync_copy(h_hbm.at[pl.ds(base, g)], h_v)
            _sc_ring(units, lambda u, s: copy(u, s).start(), lambda u, s: copy(u, s).wait(), compute)
            pltpu.sync_copy(out_v, out_hbm.at[pl.ds(base * nsel, g * nsel)])

    return k(table, idx_flat, h)


def peer_combine_sc(table, idx_flat, coef_flat, t):
    d = table.shape[1] * table.shape[2]
    nsel = PEER_SEL
    tpw = t // SC_WORKERS
    g = SC_GROUP
    groups = tpw // g
    heads = nsel // SC_LANES
    pieces = d // SC_LANES
    units = g * heads
    row_buf = pltpu.VMEM((SC_LANES, SC_ROW_SUB, SC_ROW_LANE), F32)

    @functools.partial(
        pl.kernel, mesh=_sc_mesh(),
        out_type=jax.ShapeDtypeStruct((t, d), F32),
        scratch_types=[
            pltpu.VMEM((g * nsel,), I32),
            pltpu.VMEM((g * nsel,), F32),
            pltpu.VMEM((g, d), F32),
            [row_buf] * SC_RING,
            [pltpu.SemaphoreType.DMA] * SC_RING,
        ],
        compiler_params=_sc_params(),
        name="peer_combine_sc",
    )
    def k(tab_hbm, idx_hbm, coef_hbm, out_hbm, idx_v, coef_v, y_v, rows, sems):
        wid = _sc_worker_id()

        def copy(u, slot):
            ids = idx_v.at[pl.ds(_sc_unit_off(u), SC_LANES)]
            return pltpu.make_async_copy(tab_hbm.at[ids], rows[slot], sems[slot])

        def compute(u, slot):
            tt = u // heads
            first = (u % heads) == 0
            cs = [plsc.load_gather(coef_v, [jnp.full((SC_LANES,), u * SC_LANES + r, I32)])
                  for r in range(SC_LANES)]

            @plsc.parallel_loop(0, pieces, unroll=2)
            def _(c):
                off = pl.multiple_of(c * SC_LANES, SC_LANES)
                terms = [cs[r] * _sc_row_piece(rows[slot], r, c) for r in range(SC_LANES)]
                while len(terms) > 1:
                    terms = [terms[i] + terms[i + 1] for i in range(0, len(terms), 2)]
                prev = y_v[tt, pl.ds(off, SC_LANES)]
                y_v[tt, pl.ds(off, SC_LANES)] = terms[0] + jnp.where(first, 0.0, prev)

        @pl.loop(0, groups)
        def _(gi):
            base = wid * tpw + gi * g
            pltpu.sync_copy(idx_hbm.at[pl.ds(base * nsel, g * nsel)], idx_v)
            pltpu.sync_copy(coef_hbm.at[pl.ds(base * nsel, g * nsel)], coef_v)
            _sc_ring(units, lambda u, s: copy(u, s).start(), lambda u, s: copy(u, s).wait(), compute)
            pltpu.sync_copy(y_v, out_hbm.at[pl.ds(base, g)])

    return k(table, idx_flat, coef_flat)


GELU_C0 = math.sqrt(2.0 / math.pi)
GELU_C1 = 0.044715


def _gelu_tanh(x):
    z = GELU_C0 * (x + GELU_C1 * (x * x * x))
    th = 1.0 - 2.0 / (jnp.exp(2.0 * z) + 1.0)
    return 0.5 * x * (1.0 + th)


def peer_experts_sc(tab_u, tab_v, idx_flat, w_flat, h):
    t, d = h.shape
    nsel = PEER_SEL
    tpw = t // SC_WORKERS
    g = SC_GROUP
    groups = tpw // g
    heads = nsel // SC_LANES
    pieces = d // SC_LANES
    units = g * heads
    row_buf = pltpu.VMEM((SC_LANES, SC_ROW_SUB, SC_ROW_LANE), F32)

    @functools.partial(
        pl.kernel, mesh=_sc_mesh(),
        out_type=jax.ShapeDtypeStruct((t, d), F32),
        scratch_types=[
            pltpu.VMEM((g * nsel,), I32),
            pltpu.VMEM((g * nsel,), F32),
            pltpu.VMEM((g, d), F32),
            pltpu.VMEM((g, d), F32),
            pltpu.VMEM((SC_LANES * SC_LANES,), F32),
            [row_buf] * SC_RING,
            [pltpu.SemaphoreType.DMA] * SC_RING,
        ],
        compiler_params=_sc_params(),
        name="peer_experts_sc",
    )
    def k(u_hbm, v_hbm, idx_hbm, w_hbm, h_hbm, out_hbm, idx_v, coef_v, h_v, y_v, red_v, rows, sems):
        wid = _sc_worker_id()
        lane = lax.iota(I32, SC_LANES)

        def copy(tab_hbm, u, slot):
            ids = idx_v.at[pl.ds(_sc_unit_off(u), SC_LANES)]
            return pltpu.make_async_copy(tab_hbm.at[ids], rows[slot], sems[slot])

        def dots(u, slot):
            tt = u // heads

            def body(c, accs):
                hv = h_v[tt, pl.ds(pl.multiple_of(c * SC_LANES, SC_LANES), SC_LANES)]
                return tuple(accs[r] + _sc_row_piece(rows[slot], r, c) * hv for r in range(SC_LANES))

            accs = lax.fori_loop(0, pieces, body,
                                 tuple(jnp.zeros((SC_LANES,), F32) for _ in range(SC_LANES)))
            for r in range(SC_LANES):
                red_v[pl.ds(r * SC_LANES, SC_LANES)] = accs[r]
            cols = [plsc.load_gather(red_v, [lane * SC_LANES + j]) for j in range(SC_LANES)]
            while len(cols) > 1:
                cols = [cols[i] + cols[i + 1] for i in range(0, len(cols), 2)]
            sl = pl.ds(_sc_unit_off(u), SC_LANES)
            coef_v[sl] = coef_v[sl] * _gelu_tanh(cols[0])

        def combine(u, slot):
            tt = u // heads
            first = (u % heads) == 0
            cs = [plsc.load_gather(coef_v, [jnp.full((SC_LANES,), u * SC_LANES + r, I32)])
                  for r in range(SC_LANES)]

            @plsc.parallel_loop(0, pieces, unroll=2)
            def _(c):
                off = pl.multiple_of(c * SC_LANES, SC_LANES)
                terms = [cs[r] * _sc_row_piece(rows[slot], r, c) for r in range(SC_LANES)]
                while len(terms) > 1:
                    terms = [terms[i] + terms[i + 1] for i in range(0, len(terms), 2)]
                prev = y_v[tt, pl.ds(off, SC_LANES)]
                y_v[tt, pl.ds(off, SC_LANES)] = terms[0] + jnp.where(first, 0.0, prev)

        @pl.loop(0, groups)
        def _(gi):
            base = wid * tpw + gi * g
            pltpu.sync_copy(idx_hbm.at[pl.ds(base * nsel, g * nsel)], idx_v)
            pltpu.sync_copy(w_hbm.at[pl.ds(base * nsel, g * nsel)], coef_v)
            pltpu.sync_copy(h_hbm.at[pl.ds(base, g)], h_v)
            _sc_ring(units, lambda u, s: copy(u_hbm, u, s).start(), lambda u, s: copy(u_hbm, u, s).wait(), dots)
            _sc_ring(units, lambda u, s: copy(v_hbm, u, s).start(), lambda u, s: copy(v_hbm, u, s).wait(), combine)
            pltpu.sync_copy(y_v, out_hbm.at[pl.ds(base, g)])

    return k(tab_u, tab_v, idx_flat, w_flat, h)


def kernel(x, mem, rel_bias, ln_mix, w_in, hg_lower, hg_norm, w_up_a, w_up_b, w_out, ln_cross, ln_mem, wq_x, wk_x, wv_x, wo_x, ln_ffn, peer_query, peer_subkeys, peer_u, peer_v, ln_final):
    b, s, d = x.shape
    depth = w_in.shape[0]
    assert depth == 1, "the residual after PEER is fused into the final norm"
    assert s % MB_BLOCK == 0 and s % HG_CHUNK == 0 and s % (SC_WORKERS * SC_GROUP) == 0
    nb = s // MB_BLOCK
    row = lambda a: a.reshape(1, -1).astype(F32)
    lb_all = jnp.cumsum(jax.nn.softmax(hg_lower.astype(F32), axis=0), axis=0)
    bias = moba_bias_tiles(rel_bias)
    n_hg = 4 * HG_WIDTH
    n_qk = 2 * MB_WIDTH
    n_mb = 3 * MB_WIDTH
    l = 0
    w = w_in[l].astype(BF16)
    w_hg, w_qk, w_vt, w_g = w[:, :n_hg], w[:, n_hg:n_hg + n_qk], w[:, n_hg + n_qk:n_hg + n_mb].T, w[:, n_hg + n_mb:]
    wa, wb, wo = w_up_a[l].astype(BF16), w_up_b[l].astype(BF16), w_out[l].astype(BF16)
    wqx, wox = wq_x[l].astype(BF16), wo_x[l].astype(BF16)
    wpq, sk = peer_query[l].astype(BF16), peer_subkeys[l].astype(F32)
    tab3 = lambda a: a.astype(F32).reshape(a.shape[0], SC_ROW_SUB, SC_ROW_LANE)
    tab_u, tab_v = tab3(peer_u[l]), tab3(peer_v[l])
    kx, vx = mem_kv(mem, row(ln_mem[l]), wk_x[l].astype(BF16), wv_x[l].astype(BF16))

    outs = []
    for bi in range(b):
        x2d = x[bi]
        p0, pqk, vt, pg = in_proj(x2d, row(ln_mix[l]), w_hg, w_qk, w_vt, w_g)
        ya = hgrn2(p0, row(lb_all[l]), row(hg_norm[l]), 1, s)
        km = moba_kmean(pqk, 1, s).reshape(1, nb, MB_WIDTH)
        yb = moba_attention(pqk, vt, km, bias, 1, s)
        x2d = mix_out(x2d, ya, yb, pg, wa, wb, wo)
        x2d = cross_attn(x2d, row(ln_cross[l]), wqx, kx[bi:bi + 1], vx[bi:bi + 1], wox, s)
        hf, eidx, wts = peer_route(x2d, row(ln_ffn[l]), wpq, sk)
        idx_flat = eidx.reshape(PEER_SEL, s).T.reshape(s * PEER_SEL)
        w_flat = wts.reshape(PEER_SEL, s).T.reshape(s * PEER_SEL)
        y = peer_experts_sc(tab_u, tab_v, idx_flat, w_flat, hf)
        outs.append(final_norm(x2d, y, row(ln_final)))
    return jnp.stack(outs, axis=0)
```

```python
import functools
import math

import jax
import jax.numpy as jnp
import numpy as np
from jax import lax
from jax.experimental import pallas as pl
from jax.experimental.pallas import tpu as pltpu
from jax.experimental.pallas import tpu_sc as plsc

F32 = jnp.float32
BF16 = jnp.bfloat16
I32 = jnp.int32
EPS = 1e-6
NEG_INF = float("-inf")

HG_HEADS = 4
HG_D = 128
HG_WIDTH = HG_HEADS * HG_D
HG_CHUNK = 64
HG_SUB = 16
MB_HEADS = 8
MB_DH = 64
MB_WIDTH = MB_HEADS * MB_DH
MB_BLOCK = 256
MB_TOPK = 3
MB_BIAS_TILES = 8
REL_BUCKETS = 32
REL_MAX_DIST = 2048
X_HEADS = 4
PEER_HEADS = 8
PEER_NKEYS = 128
PEER_TOPK = 16
PEER_HALF = 128
PEER_SEL = PEER_HEADS * PEER_TOPK

VMEM_LIMIT = 56 * 1024 * 1024


def _cparams(sem):
    return pltpu.CompilerParams(dimension_semantics=sem, vmem_limit_bytes=VMEM_LIMIT)


def _rms(x, g):
    ms = jnp.mean(x * x, axis=-1, keepdims=True)
    return x * lax.rsqrt(ms + EPS) * g


def _in_proj_kernel(x_ref, g_ref, w0_ref, w1_ref, wvt_ref, w2_ref, o0_ref, o1_ref, ovt_ref, o2_ref):
    h = _rms(x_ref[...], g_ref[...]).astype(BF16)
    o0_ref[...] = jnp.dot(h, w0_ref[...], preferred_element_type=F32)
    o1_ref[...] = jnp.dot(h, w1_ref[...], preferred_element_type=F32).astype(BF16)
    ovt_ref[0] = lax.dot_general(wvt_ref[...], h, (((1,), (1,)), ((), ())),
                                 preferred_element_type=F32).astype(BF16)
    o2_ref[...] = jnp.dot(h, w2_ref[...], preferred_element_type=F32).astype(BF16)


def in_proj(x2d, g, w0, w1, wvt, w2):
    t, d = x2d.shape
    tm = MB_BLOCK
    n0, n1, nv, n2 = w0.shape[1], w1.shape[1], wvt.shape[0], w2.shape[1]
    full = lambda a: pl.BlockSpec(a.shape, lambda i: (0, 0))
    return pl.pallas_call(
        _in_proj_kernel,
        grid=(t // tm,),
        in_specs=[pl.BlockSpec((tm, d), lambda i: (i, 0)), full(g), full(w0), full(w1), full(wvt), full(w2)],
        out_specs=[pl.BlockSpec((tm, n0), lambda i: (i, 0)),
                   pl.BlockSpec((tm, n1), lambda i: (i, 0)),
                   pl.BlockSpec((1, nv, tm), lambda i: (i, 0, 0)),
                   pl.BlockSpec((tm, n2), lambda i: (i, 0))],
        out_shape=[jax.ShapeDtypeStruct((t, n0), F32),
                   jax.ShapeDtypeStruct((t, n1), BF16),
                   jax.ShapeDtypeStruct((t // tm, nv, tm), BF16),
                   jax.ShapeDtypeStruct((t, n2), BF16)],
        compiler_params=_cparams(("parallel",)),
        name="in_proj",
    )(x2d, g, w0, w1, wvt, w2)


def _hgrn_kernel(q_ref, f_ref, i_ref, g_ref, lb_ref, gain_ref, o_ref, st_ref):
    c = pl.program_id(1)

    @pl.when(c == 0)
    def _():
        st_ref[...] = jnp.zeros_like(st_ref)

    C, S = HG_CHUNK, HG_SUB
    row = lax.broadcasted_iota(I32, (C, C), 0)
    col = lax.broadcasted_iota(I32, (C, C), 1)
    tril = (row >= col).astype(F32)
    t_iota = lax.broadcasted_iota(I32, (S, 1), 0)

    for h in range(HG_HEADS):
        sl = slice(h * HG_D, (h + 1) * HG_D)
        q = q_ref[:, sl]
        v = i_ref[:, sl]
        lb = lb_ref[:, sl]
        f = lb + (1.0 - lb) * jax.nn.sigmoid(f_ref[:, sl])
        lf = jnp.log(f)
        k = 1.0 - f
        b = jnp.dot(tril, lf, precision=lax.Precision.HIGHEST, preferred_element_type=F32)
        st = st_ref[h]
        vb = v.astype(BF16)
        qd = (q * jnp.exp(b)).astype(BF16)
        o_inter = lax.dot_general(qd, st.astype(BF16), (((1,), (1,)), ((), ())),
                                  preferred_element_type=F32)
        outs = []
        for i in range(C // S):
            r0 = i * S
            qi = q[r0:r0 + S]
            ki = k[r0:r0 + S]
            bi = b[r0:r0 + S]
            vi = v[r0:r0 + S]
            oi = o_inter[r0:r0 + S]
            if i > 0:
                bs = b[r0 - 1:r0]
                qh = (qi * jnp.exp(bi - bs)).astype(BF16)
                kh = (k[:r0] * jnp.exp(bs - b[:r0])).astype(BF16)
                a = lax.dot_general(qh, kh, (((1,), (1,)), ((), ())), preferred_element_type=F32)
                oi = oi + jnp.dot(a.astype(BF16), vb[:r0], preferred_element_type=F32)
            for s in range(S):
                dec = jnp.exp(jnp.minimum(bi - bi[s:s + 1], 0.0))
                p = qi * ki[s:s + 1] * dec
                a_s = jnp.sum(p, axis=-1, keepdims=True)
                a_s = jnp.where(t_iota >= s, a_s, 0.0)
                oi = oi + a_s * vi[s:s + 1]
            outs.append(oi)
        o = jnp.concatenate(outs, axis=0)
        b_end = b[C - 1:C]
        kd = (k * jnp.exp(b_end - b)).astype(BF16)
        upd = lax.dot_general(vb, kd, (((0,), (0,)), ((), ())), preferred_element_type=F32)
        st_ref[h] = st * jnp.exp(b_end) + upd
        o = o * lax.rsqrt(jnp.mean(o * o, axis=-1, keepdims=True) + EPS)
        g = g_ref[:, sl]
        o_ref[:, sl] = (o * gain_ref[:, sl] * (g * jax.nn.sigmoid(g))).astype(o_ref.dtype)


def hgrn2(p0, lb, gain, batch, seq):
    t = p0.shape[0]
    nc = seq // HG_CHUNK
    w = HG_WIDTH

    def col(j):
        return pl.BlockSpec((HG_CHUNK, w), lambda b, c, j=j: (b * nc + c, j))

    return pl.pallas_call(
        _hgrn_kernel,
        grid=(batch, nc),
        in_specs=[col(0), col(1), col(2), col(3),
                  pl.BlockSpec((1, w), lambda b, c: (0, 0)),
                  pl.BlockSpec((1, w), lambda b, c: (0, 0))],
        out_specs=pl.BlockSpec((HG_CHUNK, w), lambda b, c: (b * nc + c, 0)),
        out_shape=jax.ShapeDtypeStruct((t, w), BF16),
        scratch_shapes=[pltpu.VMEM((HG_HEADS, HG_D, HG_D), F32)],
        compiler_params=_cparams(("parallel", "arbitrary")),
        name="hgrn2",
    )(p0, p0, p0, p0, lb, gain)


def _kmean_kernel(k_ref, o_ref):
    o_ref[0] = jnp.mean(k_ref[...].astype(F32), axis=0, keepdims=True)


def moba_kmean(p1, batch, seq):
    nbt = p1.shape[0] // MB_BLOCK
    return pl.pallas_call(
        _kmean_kernel,
        grid=(nbt,),
        in_specs=[pl.BlockSpec((MB_BLOCK, MB_WIDTH), lambda i: (i, 1))],
        out_specs=pl.BlockSpec((1, 1, MB_WIDTH), lambda i: (i, 0, 0)),
        out_shape=jax.ShapeDtypeStruct((nbt, 1, MB_WIDTH), F32),
        compiler_params=_cparams(("parallel",)),
        name="moba_kmean",
    )(p1)


MB_PAIR = 4
MB_PW = MB_PAIR * MB_DH
MB_LG = 128


def _moba_kernel(q_ref, k_ref, vt_ref, km_ref, bias_ref, o_ref, *scratch):
    m_ref, l_ref, acc_ref, msk_ref = (scratch[i * MB_PAIR:(i + 1) * MB_PAIR] for i in range(4))
    qi = pl.program_id(2)
    nb = km_ref.shape[0]
    blk = MB_BLOCK
    heads = range(MB_PAIR)
    grp = lambda hh: slice((hh // 2) * MB_LG, (hh // 2 + 1) * MB_LG)
    q = q_ref[...]
    lane = lax.broadcasted_iota(I32, (blk, MB_LG), 1)
    in_head = [(lane < MB_DH) if hh % 2 == 0 else (lane >= MB_DH) for hh in heads]
    qs = q * jnp.asarray(MB_DH ** -0.5, BF16)
    qh = [jnp.where(in_head[hh], qs[:, grp(hh)], jnp.zeros((blk, MB_LG), BF16)) for hh in heads]
    nt = (((1,), (1,)), ((), ()))

    qf = q.astype(F32)
    n_io = lax.broadcasted_iota(I32, (nb, blk), 0)
    for hh in heads:
        gate = lax.dot_general(km_ref[:, grp(hh)], jnp.where(in_head[hh], qf[:, grp(hh)], 0.0), nt,
                               precision=lax.Precision.HIGHEST, preferred_element_type=F32)
        gate = jnp.where(n_io < qi, gate, NEG_INF)
        chosen = n_io < 0
        for _ in range(MB_TOPK):
            mx = jnp.max(gate, axis=0, keepdims=True)
            ix = jnp.min(jnp.where(gate == mx, n_io, nb), axis=0, keepdims=True)
            hit = n_io == ix
            chosen = chosen | (hit & (mx > NEG_INF))
            gate = jnp.where(hit, NEG_INF, gate)
        msk_ref[hh][...] = jnp.where(chosen, 0.0, NEG_INF)

    k_own = k_ref[pl.ds(pl.multiple_of(qi * blk, blk), blk), :]
    vt_own = vt_ref[qi]
    key_io = lax.broadcasted_iota(I32, (blk, blk), 0)
    qry_io = lax.broadcasted_iota(I32, (blk, blk), 1)
    own_rows = lambda r, hh: r[(hh % 2) * MB_DH:(hh % 2 + 1) * MB_DH]
    for hh in heads:
        s = lax.dot_general(k_own[:, grp(hh)], qh[hh], nt, preferred_element_type=F32) + bias_ref[hh, 0]
        s = jnp.where(key_io <= qry_io, s, NEG_INF)
        m0 = jnp.max(s, axis=0, keepdims=True)
        p = jnp.exp(s - m0)
        m_ref[hh][...] = m0
        l_ref[hh][...] = jnp.sum(p, axis=0, keepdims=True)
        r = jnp.dot(vt_own[grp(hh)], p.astype(BF16), preferred_element_type=F32)
        acc_ref[hh][...] = own_rows(r, hh)

    def past(n, carry):
        kn = k_ref[pl.ds(pl.multiple_of(n * blk, blk), blk), :]
        vtn = vt_ref[n]
        d = jnp.minimum(qi - n, MB_BIAS_TILES - 1)
        s = [lax.dot_general(kn[:, grp(hh)], qh[hh], nt, preferred_element_type=F32)
             + bias_ref[hh, d] + msk_ref[hh][pl.ds(n, 1), :] for hh in heads]
        m_old = [m_ref[hh][...] for hh in heads]
        l_old = [l_ref[hh][...] for hh in heads]
        a_old = [acc_ref[hh][...] for hh in heads]
        m_new = [jnp.maximum(m_old[hh], jnp.max(s[hh], axis=0, keepdims=True)) for hh in heads]
        alpha = [jnp.exp(m_old[hh] - m_new[hh]) for hh in heads]
        p = [jnp.exp(s[hh] - m_new[hh]) for hh in heads]
        r = [jnp.dot(vtn[grp(hh)], p[hh].astype(BF16), preferred_element_type=F32) for hh in heads]
        l_new = [alpha[hh] * l_old[hh] + jnp.sum(p[hh], axis=0, keepdims=True) for hh in heads]
        a_new = [alpha[hh] * a_old[hh] + own_rows(r[hh], hh) for hh in heads]
        for hh in heads:
            m_ref[hh][...] = m_new[hh]
            l_ref[hh][...] = l_new[hh]
            acc_ref[hh][...] = a_new[hh]
        return carry

    lax.fori_loop(0, qi, past, 0)
    out_t = jnp.concatenate([acc_ref[hh][...] / l_ref[hh][...] for hh in heads], axis=0)
    o_ref[...] = out_t.T.astype(o_ref.dtype)


def moba_attention(pqk, vt, km, bias, batch, seq):
    t = pqk.shape[0]
    nb = seq // MB_BLOCK
    groups = MB_WIDTH // MB_PW
    return pl.pallas_call(
        _moba_kernel,
        grid=(batch, groups, nb),
        in_specs=[
            pl.BlockSpec((MB_BLOCK, MB_PW), lambda b, j, i: (b * nb + i, j)),
            pl.BlockSpec((seq, MB_PW), lambda b, j, i: (b, groups + j)),
            pl.BlockSpec((nb, MB_PW, MB_BLOCK), lambda b, j, i: (b, j, 0)),
            pl.BlockSpec((None, nb, MB_PW), lambda b, j, i: (b, 0, j)),
            pl.BlockSpec((MB_PAIR, MB_BIAS_TILES, MB_BLOCK, MB_BLOCK), lambda b, j, i: (j, 0, 0, 0)),
        ],
        out_specs=pl.BlockSpec((MB_BLOCK, MB_PW), lambda b, j, i: (b * nb + i, j)),
        out_shape=jax.ShapeDtypeStruct((t, MB_WIDTH), BF16),
        scratch_shapes=(
            [pltpu.VMEM((1, MB_BLOCK), F32)] * (2 * MB_PAIR)
            + [pltpu.VMEM((MB_DH, MB_BLOCK), F32)] * MB_PAIR
            + [pltpu.VMEM((nb, MB_BLOCK), F32)] * MB_PAIR
        ),
        compiler_params=_cparams(("parallel", "parallel", "arbitrary")),
        name="moba_attn",
    )(pqk, pqk, vt, km, bias)


def _t5_bucket(dist):
    max_exact = REL_BUCKETS // 2
    scaled = jnp.log(jnp.maximum(dist, 1).astype(F32) / max_exact) / math.log(REL_MAX_DIST / max_exact)
    large = jnp.minimum(max_exact + (scaled * (REL_BUCKETS - max_exact)).astype(I32), REL_BUCKETS - 1)
    return jnp.where(dist < max_exact, dist, large)


def moba_bias_tiles(rel_bias):
    blk = MB_BLOCK
    span = 2 * blk - 1
    x = jnp.arange(span) - (blk - 1)
    dist = jnp.maximum(jnp.arange(MB_BIAS_TILES)[:, None] * blk + x[None, :], 0)
    w = rel_bias.astype(F32).T[:, _t5_bucket(dist)]
    h = w.shape[0]
    wp = jnp.pad(w, ((0, 0), (0, 0), (0, 1)))
    a = jnp.broadcast_to(wp[:, :, None, :], (h, MB_BIAS_TILES, blk, span + 1))
    a = a.reshape(h, MB_BIAS_TILES, blk * (span + 1))[:, :, :blk * span]
    return a.reshape(h, MB_BIAS_TILES, blk, span)[:, :, :, blk - 1:]


def _mix_kernel(x_ref, ya_ref, yb_ref, ga_ref, gb_ref, wa_ref, wb_ref, wo_ref, o_ref):
    za = jnp.dot(ya_ref[...], wa_ref[...], preferred_element_type=F32)
    zb = jnp.dot(yb_ref[...], wb_ref[...], preferred_element_type=F32)
    z = jax.nn.sigmoid(ga_ref[...].astype(F32)) * za + jax.nn.sigmoid(gb_ref[...].astype(F32)) * zb
    o_ref[...] = x_ref[...] + jnp.dot(z.astype(BF16), wo_ref[...], preferred_element_type=F32)


def mix_out(x2d, ya, yb, pg, wa, wb, wo, tm=256):
    t, d = x2d.shape
    w = ya.shape[1]
    return pl.pallas_call(
        _mix_kernel,
        grid=(t // tm,),
        in_specs=[
            pl.BlockSpec((tm, d), lambda i: (i, 0)),
            pl.BlockSpec((tm, w), lambda i: (i, 0)),
            pl.BlockSpec((tm, w), lambda i: (i, 0)),
            pl.BlockSpec((tm, d), lambda i: (i, 0)),
            pl.BlockSpec((tm, d), lambda i: (i, 1)),
            pl.BlockSpec((w, d), lambda i: (0, 0)),
            pl.BlockSpec((w, d), lambda i: (0, 0)),
            pl.BlockSpec((d, d), lambda i: (0, 0)),
        ],
        out_specs=pl.BlockSpec((tm, d), lambda i: (i, 0)),
        out_shape=jax.ShapeDtypeStruct((t, d), F32),
        compiler_params=_cparams(("parallel",)),
        name="mix_out",
    )(x2d, ya, yb, pg, pg, wa, wb, wo)


def _mem_kv_kernel(m_ref, g_ref, wk_ref, wv_ref, k_ref, v_ref):
    mn = _rms(m_ref[...], g_ref[...]).astype(BF16)
    k_ref[...] = jnp.dot(mn, wk_ref[...], preferred_element_type=F32).astype(BF16)
    v_ref[...] = jnp.dot(mn, wv_ref[...], preferred_element_type=F32).astype(BF16)


def mem_kv(mem, g, wk, wv):
    b, m, d = mem.shape
    spec = pl.BlockSpec((None, m, d), lambda i: (i, 0, 0))
    wspec = pl.BlockSpec((d, d), lambda i: (0, 0))
    return pl.pallas_call(
        _mem_kv_kernel,
        grid=(b,),
        in_specs=[spec, pl.BlockSpec((1, d), lambda i: (0, 0)), wspec, wspec],
        out_specs=[spec, spec],
        out_shape=[jax.ShapeDtypeStruct((b, m, d), BF16)] * 2,
        compiler_params=_cparams(("parallel",)),
        name="mem_kv",
    )(mem, g, wk, wv)


def _cross_kernel(x_ref, g_ref, wq_ref, k_ref, v_ref, wo_ref, o_ref):
    x = x_ref[...]
    d = x.shape[1]
    dh = d // X_HEADS
    h = _rms(x, g_ref[...]).astype(BF16)
    q = (jnp.dot(h, wq_ref[...], preferred_element_type=F32) * (dh ** -0.5)).astype(BF16)
    outs = []
    for hh in range(X_HEADS):
        sl = slice(hh * dh, (hh + 1) * dh)
        s = lax.dot_general(q[:, sl], k_ref[:, sl], (((1,), (1,)), ((), ())),
                            preferred_element_type=F32)
        p = jnp.exp(s - jnp.max(s, axis=1, keepdims=True))
        l = jnp.sum(p, axis=1, keepdims=True)
        o = jnp.dot(p.astype(BF16), v_ref[:, sl], preferred_element_type=F32) / l
        outs.append(o.astype(BF16))
    o = jnp.concatenate(outs, axis=1)
    o_ref[...] = x + jnp.dot(o, wo_ref[...], preferred_element_type=F32)


def cross_attn(x2d, g, wq, kx, vx, wo, seq, tm=256):
    t, d = x2d.shape
    m = kx.shape[1]
    per_b = seq // tm
    kv = pl.BlockSpec((None, m, d), lambda i: (i // per_b, 0, 0))
    wspec = pl.BlockSpec((d, d), lambda i: (0, 0))
    return pl.pallas_call(
        _cross_kernel,
        grid=(t // tm,),
        in_specs=[pl.BlockSpec((tm, d), lambda i: (i, 0)), pl.BlockSpec((1, d), lambda i: (0, 0)),
                  wspec, kv, kv, wspec],
        out_specs=pl.BlockSpec((tm, d), lambda i: (i, 0)),
        out_shape=jax.ShapeDtypeStruct((t, d), F32),
        compiler_params=_cparams(("parallel",)),
        name="cross_attn",
    )(x2d, g, wq, kx, vx, wo)


def _topk_rows(sc, k):
    n = sc.shape[0]
    io = lax.broadcasted_iota(I32, sc.shape, 0)
    vals, ids = [], []
    for _ in range(k):
        m = jnp.max(sc, axis=0, keepdims=True)
        ix = jnp.min(jnp.where(sc == m, io, n), axis=0, keepdims=True)
        vals.append(m)
        ids.append(ix)
        sc = jnp.where(io == ix, NEG_INF, sc)
    return jnp.concatenate(vals, axis=0), jnp.concatenate(ids, axis=0)


def _pack_bf16_halves(h):
    bits = lax.bitcast_convert_type(h, I32)
    r = bits + 0x7FFF + (lax.shift_right_logical(bits, 16) & 1)
    half = h.shape[1] // 2
    return lax.shift_right_logical(r[:, :half], 16) | (r[:, half:] & HI_MASK)


def _route_kernel(x_ref, g_ref, wq_ref, sk_ref, hp_ref, idx_ref, w_ref, hb_ref, it_ref, wt_ref):
    p = pl.program_id(1)

    @pl.when(p == 0)
    def _():
        h = _rms(x_ref[...], g_ref[...])
        hp_ref[...] = _pack_bf16_halves(h)
        hb_ref[...] = h.astype(BF16)

    qh = jnp.dot(hb_ref[...], wq_ref[...], preferred_element_type=F32)
    tops = []
    for c in range(2):
        seg = qh[:, c * PEER_HALF:(c + 1) * PEER_HALF]
        sc = lax.dot_general(sk_ref[c], seg, (((1,), (1,)), ((), ())),
                             precision=lax.Precision.HIGHEST, preferred_element_type=F32)
        tops.append(_topk_rows(sc, PEER_TOPK))
    (s0, i0), (s1, i1) = tops
    cand = jnp.concatenate([s0[a:a + 1] + s1 for a in range(PEER_TOPK)], axis=0)
    cidx = jnp.concatenate([i0[a:a + 1] * PEER_NKEYS + i1 for a in range(PEER_TOPK)], axis=0)
    io = lax.broadcasted_iota(I32, cand.shape, 0)
    n = cand.shape[0]
    vals, ids = [], []
    for _ in range(PEER_TOPK):
        m = jnp.max(cand, axis=0, keepdims=True)
        px = jnp.min(jnp.where(cand == m, io, n), axis=0, keepdims=True)
        hit = io == px
        vals.append(m)
        ids.append(jnp.sum(jnp.where(hit, cidx, 0), axis=0, keepdims=True))
        cand = jnp.where(hit, NEG_INF, cand)
    sf = jnp.concatenate(vals, axis=0)
    e = jnp.exp(sf - sf[0:1])
    rows = pl.ds(pl.multiple_of(p * PEER_TOPK, PEER_TOPK), PEER_TOPK)
    wt_ref[rows, :] = e / jnp.sum(e, axis=0, keepdims=True)
    it_ref[rows, :] = jnp.concatenate(ids, axis=0)

    @pl.when(p == pl.num_programs(1) - 1)
    def _():
        idx_ref[...] = it_ref[...].T
        w_ref[...] = wt_ref[...].T


def peer_route(x2d, g, wq, sk, tm=256):
    t, d = x2d.shape
    ph = sk.shape[0]
    nsel = ph * PEER_TOPK
    return pl.pallas_call(
        _route_kernel,
        grid=(t // tm, ph),
        in_specs=[
            pl.BlockSpec((tm, d), lambda i, p: (i, 0)),
            pl.BlockSpec((1, d), lambda i, p: (0, 0)),
            pl.BlockSpec((d, 2 * PEER_HALF), lambda i, p: (0, p)),
            pl.BlockSpec((None, 2, PEER_NKEYS, PEER_HALF), lambda i, p: (p, 0, 0, 0)),
        ],
        out_specs=[
            pl.BlockSpec((tm, d // 2), lambda i, p: (i, 0)),
            pl.BlockSpec((tm, nsel), lambda i, p: (i, 0)),
            pl.BlockSpec((tm, nsel), lambda i, p: (i, 0)),
        ],
        out_shape=[jax.ShapeDtypeStruct((t, d // 2), I32),
                   jax.ShapeDtypeStruct((t, nsel), I32),
                   jax.ShapeDtypeStruct((t, nsel), F32)],
        scratch_shapes=[pltpu.VMEM((tm, d), BF16),
                        pltpu.VMEM((nsel, tm), I32),
                        pltpu.VMEM((nsel, tm), F32)],
        compiler_params=_cparams(("parallel", "arbitrary")),
        name="peer_route",
    )(x2d, g, wq, sk)


def _coef_kernel(w_ref, a_ref, o_ref):
    o_ref[...] = w_ref[...] * jax.nn.gelu(a_ref[...])


def peer_coef(w, act, tm=1024):
    t, n = w.shape
    spec = pl.BlockSpec((tm, n), lambda i: (i, 0))
    return pl.pallas_call(
        _coef_kernel, grid=(t // tm,), in_specs=[spec, spec], out_specs=spec,
        out_shape=jax.ShapeDtypeStruct((t, n), F32),
        compiler_params=_cparams(("parallel",)), name="peer_coef",
    )(w, act)


def _final_kernel(x_ref, y_ref, g_ref, o_ref):
    o_ref[...] = _rms(x_ref[...] + y_ref[...], g_ref[...])


def final_norm(x2d, y, g, tm=512):
    t, d = x2d.shape
    spec = pl.BlockSpec((tm, d), lambda i: (i, 0))
    return pl.pallas_call(
        _final_kernel, grid=(t // tm,),
        in_specs=[spec, spec, pl.BlockSpec((1, d), lambda i: (0, 0))], out_specs=spec,
        out_shape=jax.ShapeDtypeStruct((t, d), F32),
        compiler_params=_cparams(("parallel",)), name="final_norm",
    )(x2d, y, g)


SC_CORES = 2
SC_SUBCORES = 16
SC_WORKERS = SC_CORES * SC_SUBCORES
SC_LANES = 16
SC_GROUP = 16


def _sc_mesh():
    return plsc.VectorSubcoreMesh(core_axis_name="c", subcore_axis_name="s")


def _sc_params():
    return pltpu.CompilerParams(needs_layout_passes=False)


def _sc_worker_id():
    return lax.axis_index("s") * SC_CORES + lax.axis_index("c")


SC_RING = 4
SC_ROW_SUB = 8
SC_ROW_LANE = 128


def _sc_ring(n_units, start, wait, compute):
    for u in range(SC_RING - 1):
        start(u, u)

    @pl.loop(0, n_units, step=SC_RING)
    def _(uu):
        for b in range(SC_RING):
            u = uu + b
            nxt = u + (SC_RING - 1)

            @pl.when(nxt < n_units)
            def _():
                start(nxt, (b + SC_RING - 1) % SC_RING)

            wait(u, b)
            compute(u, b)


def _sc_unit_off(u):
    off = u * SC_LANES
    return off if isinstance(off, int) else pl.multiple_of(off, SC_LANES)


def _sc_row_piece(rows, r, c):
    per = SC_ROW_LANE // SC_LANES
    return rows[r, c // per, pl.ds(pl.multiple_of((c % per) * SC_LANES, SC_LANES), SC_LANES)]


def peer_dots_sc(table, idx_flat, h):
    t, d = h.shape
    nsel = PEER_SEL
    tpw = t // SC_WORKERS
    g = SC_GROUP
    groups = tpw // g
    heads = nsel // SC_LANES
    pieces = d // SC_LANES
    units = g * heads
    row_buf = pltpu.VMEM((SC_LANES, SC_ROW_SUB, SC_ROW_LANE), F32)

    @functools.partial(
        pl.kernel, mesh=_sc_mesh(),
        out_type=jax.ShapeDtypeStruct((t * nsel,), F32),
        scratch_types=[
            pltpu.VMEM((g * nsel,), I32),
            pltpu.VMEM((g, d), F32),
            pltpu.VMEM((g * nsel,), F32),
            pltpu.VMEM((SC_LANES * SC_LANES,), F32),
            [row_buf] * SC_RING,
            [pltpu.SemaphoreType.DMA] * SC_RING,
        ],
        compiler_params=_sc_params(),
        name="peer_dots_sc",
    )
    def k(tab_hbm, idx_hbm, h_hbm, out_hbm, idx_v, h_v, out_v, red_v, rows, sems):
        wid = _sc_worker_id()
        lane = lax.iota(I32, SC_LANES)

        def copy(u, slot):
            ids = idx_v.at[pl.ds(_sc_unit_off(u), SC_LANES)]
            return pltpu.make_async_copy(tab_hbm.at[ids], rows[slot], sems[slot])

        def compute(u, slot):
            tt = u // heads

            def body(c, accs):
                hv = h_v[tt, pl.ds(pl.multiple_of(c * SC_LANES, SC_LANES), SC_LANES)]
                return tuple(accs[r] + _sc_row_piece(rows[slot], r, c) * hv for r in range(SC_LANES))

            accs = lax.fori_loop(0, pieces, body,
                                 tuple(jnp.zeros((SC_LANES,), F32) for _ in range(SC_LANES)))
            for r in range(SC_LANES):
                red_v[pl.ds(r * SC_LANES, SC_LANES)] = accs[r]
            cols = [plsc.load_gather(red_v, [lane * SC_LANES + j]) for j in range(SC_LANES)]
            while len(cols) > 1:
                cols = [cols[i] + cols[i + 1] for i in range(0, len(cols), 2)]
            out_v[pl.ds(_sc_unit_off(u), SC_LANES)] = cols[0]

        @pl.loop(0, groups)
        def _(gi):
            base = wid * tpw + gi * g
            pltpu.sync_copy(idx_hbm.at[pl.ds(base * nsel, g * nsel)], idx_v)
            pltpu.sync_copy(h_hbm.at[pl.ds(base, g)], h_v)
            _sc_ring(units, lambda u, s: copy(u, s).start(), lambda u, s: copy(u, s).wait(), compute)
            pltpu.sync_copy(out_v, out_hbm.at[pl.ds(base * nsel, g * nsel)])

    return k(table, idx_flat, h)


def peer_combine_sc(table, idx_flat, coef_flat, t):
    d = table.shape[1] * table.shape[2]
    nsel = PEER_SEL
    tpw = t // SC_WORKERS
    g = SC_GROUP
    groups = tpw // g
    heads = nsel // SC_LANES
    pieces = d // SC_LANES
    units = g * heads
    row_buf = pltpu.VMEM((SC_LANES, SC_ROW_SUB, SC_ROW_LANE), F32)

    @functools.partial(
        pl.kernel, mesh=_sc_mesh(),
        out_type=jax.ShapeDtypeStruct((t, d), F32),
        scratch_types=[
            pltpu.VMEM((g * nsel,), I32),
            pltpu.VMEM((g * nsel,), F32),
            pltpu.VMEM((g, d), F32),
            [row_buf] * SC_RING,
            [pltpu.SemaphoreType.DMA] * SC_RING,
        ],
        compiler_params=_sc_params(),
        name="peer_combine_sc",
    )
    def k(tab_hbm, idx_hbm, coef_hbm, out_hbm, idx_v, coef_v, y_v, rows, sems):
        wid = _sc_worker_id()

        def copy(u, slot):
            ids = idx_v.at[pl.ds(_sc_unit_off(u), SC_LANES)]
            return pltpu.make_async_copy(tab_hbm.at[ids], rows[slot], sems[slot])

        def compute(u, slot):
            tt = u // heads
            first = (u % heads) == 0
            cs = [plsc.load_gather(coef_v, [jnp.full((SC_LANES,), u * SC_LANES + r, I32)])
                  for r in range(SC_LANES)]

            @plsc.parallel_loop(0, pieces, unroll=2)
            def _(c):
                off = pl.multiple_of(c * SC_LANES, SC_LANES)
                terms = [cs[r] * _sc_row_piece(rows[slot], r, c) for r in range(SC_LANES)]
                while len(terms) > 1:
                    terms = [terms[i] + terms[i + 1] for i in range(0, len(terms), 2)]
                prev = y_v[tt, pl.ds(off, SC_LANES)]
                y_v[tt, pl.ds(off, SC_LANES)] = terms[0] + jnp.where(first, 0.0, prev)

        @pl.loop(0, groups)
        def _(gi):
            base = wid * tpw + gi * g
            pltpu.sync_copy(idx_hbm.at[pl.ds(base * nsel, g * nsel)], idx_v)
            pltpu.sync_copy(coef_hbm.at[pl.ds(base * nsel, g * nsel)], coef_v)
            _sc_ring(units, lambda u, s: copy(u, s).start(), lambda u, s: copy(u, s).wait(), compute)
            pltpu.sync_copy(y_v, out_hbm.at[pl.ds(base, g)])

    return k(table, idx_flat, coef_flat)


GELU_C0 = math.sqrt(2.0 / math.pi)
GELU_C1 = 0.044715


def _gelu_tanh(x):
    z = GELU_C0 * (x + GELU_C1 * (x * x * x))
    th = 1.0 - 2.0 / (jnp.exp(2.0 * z) + 1.0)
    return 0.5 * x * (1.0 + th)


def peer_experts_sc(tab_u, tab_v, idx_flat, w_flat, h):
    t, d = h.shape
    nsel = PEER_SEL
    tpw = t // SC_WORKERS
    g = SC_GROUP
    groups = tpw // g
    heads = nsel // SC_LANES
    pieces = d // SC_LANES
    units = g * heads
    row_buf = pltpu.VMEM((SC_LANES, SC_ROW_SUB, SC_ROW_LANE), F32)

    @functools.partial(
        pl.kernel, mesh=_sc_mesh(),
        out_type=jax.ShapeDtypeStruct((t, d), F32),
        scratch_types=[
            pltpu.VMEM((g * nsel,), I32),
            pltpu.VMEM((g * nsel,), F32),
            pltpu.VMEM((g, d), F32),
            pltpu.VMEM((g, d), F32),
            pltpu.VMEM((SC_LANES * SC_LANES,), F32),
            [row_buf] * SC_RING,
            [pltpu.SemaphoreType.DMA] * SC_RING,
        ],
        compiler_params=_sc_params(),
        name="peer_experts_sc",
    )
    def k(u_hbm, v_hbm, idx_hbm, w_hbm, h_hbm, out_hbm, idx_v, coef_v, h_v, y_v, red_v, rows, sems):
        wid = _sc_worker_id()
        lane = lax.iota(I32, SC_LANES)

        def copy(tab_hbm, u, slot):
            ids = idx_v.at[pl.ds(_sc_unit_off(u), SC_LANES)]
            return pltpu.make_async_copy(tab_hbm.at[ids], rows[slot], sems[slot])

        def dots(u, slot):
            tt = u // heads

            def body(c, accs):
                hv = h_v[tt, pl.ds(pl.multiple_of(c * SC_LANES, SC_LANES), SC_LANES)]
                return tuple(accs[r] + _sc_row_piece(rows[slot], r, c) * hv for r in range(SC_LANES))

            accs = lax.fori_loop(0, pieces, body,
                                 tuple(jnp.zeros((SC_LANES,), F32) for _ in range(SC_LANES)))
            for r in range(SC_LANES):
                red_v[pl.ds(r * SC_LANES, SC_LANES)] = accs[r]
            cols = [plsc.load_gather(red_v, [lane * SC_LANES + j]) for j in range(SC_LANES)]
            while len(cols) > 1:
                cols = [cols[i] + cols[i + 1] for i in range(0, len(cols), 2)]
            sl = pl.ds(_sc_unit_off(u), SC_LANES)
            coef_v[sl] = coef_v[sl] * _gelu_tanh(cols[0])

        def combine(u, slot):
            tt = u // heads
            first = (u % heads) == 0
            cs = [plsc.load_gather(coef_v, [jnp.full((SC_LANES,), u * SC_LANES + r, I32)])
                  for r in range(SC_LANES)]

            @plsc.parallel_loop(0, pieces, unroll=2)
            def _(c):
                off = pl.multiple_of(c * SC_LANES, SC_LANES)
                terms = [cs[r] * _sc_row_piece(rows[slot], r, c) for r in range(SC_LANES)]
                while len(terms) > 1:
                    terms = [terms[i] + terms[i + 1] for i in range(0, len(terms), 2)]
                prev = y_v[tt, pl.ds(off, SC_LANES)]
                y_v[tt, pl.ds(off, SC_LANES)] = terms[0] + jnp.where(first, 0.0, prev)

        @pl.loop(0, groups)
        def _(gi):
            base = wid * tpw + gi * g
            pltpu.sync_copy(idx_hbm.at[pl.ds(base * nsel, g * nsel)], idx_v)
            pltpu.sync_copy(w_hbm.at[pl.ds(base * nsel, g * nsel)], coef_v)
            pltpu.sync_copy(h_hbm.at[pl.ds(base, g)], h_v)
            _sc_ring(units, lambda u, s: copy(u_hbm, u, s).start(), lambda u, s: copy(u_hbm, u, s).wait(), dots)
            _sc_ring(units, lambda u, s: copy(v_hbm, u, s).start(), lambda u, s: copy(v_hbm, u, s).wait(), combine)
            pltpu.sync_copy(y_v, out_hbm.at[pl.ds(base, g)])

    return k(tab_u, tab_v, idx_flat, w_flat, h)


SC_PK_RING = 8
SC_PK_SUB = 4
HI_MASK = -65536


def pack_bf16_pairs(a):
    half = a.shape[1] // 2
    bits = lax.bitcast_convert_type(a.astype(BF16), jnp.uint16).astype(jnp.uint32)
    return lax.bitcast_convert_type(bits[:, :half] | (bits[:, half:] << 16), I32)


def _unpack_halves(x32):
    w = plsc.bitcast(x32, I32)
    return plsc.bitcast(w << 16, F32), plsc.bitcast(w & HI_MASK, F32)


def _tree_sum(xs):
    while len(xs) > 1:
        xs = [xs[i] + xs[i + 1] for i in range(0, len(xs), 2)]
    return xs[0]


def peer_experts_pk_sc(tab_u, tab_v, idx_flat, w_flat, hp, d):
    t = hp.shape[0]
    nsel = PEER_SEL
    tpw = t // SC_WORKERS
    g = SC_GROUP
    groups = tpw // g
    heads = nsel // SC_LANES
    chunks = d // 32
    units = g * heads
    ring = SC_PK_RING
    row_buf = pltpu.VMEM((SC_LANES, SC_PK_SUB, SC_ROW_LANE), I32)

    def row_words(rows, r, wc):
        per = SC_ROW_LANE // SC_LANES
        return plsc.bitcast(rows[r, wc // per, pl.ds(pl.multiple_of((wc % per) * SC_LANES, SC_LANES), SC_LANES)],
                            BF16)

    def ring_loop(n_units, start, wait, compute):
        for u in range(ring - 1):
            start(u, u)

        @pl.loop(0, n_units, step=ring)
        def _(uu):
            for b in range(ring):
                u = uu + b
                nxt = u + (ring - 1)

                @pl.when(nxt < n_units)
                def _():
                    start(nxt, (b + ring - 1) % ring)

                wait(u, b)
                compute(u, b)

    @functools.partial(
        pl.kernel, mesh=_sc_mesh(),
        out_type=jax.ShapeDtypeStruct((t, d), F32),
        scratch_types=[
            pltpu.VMEM((g * nsel,), I32),
            pltpu.VMEM((g * nsel,), F32),
            pltpu.VMEM((g, d // 2), I32),
            pltpu.VMEM((g, d), F32),
            pltpu.VMEM((SC_LANES * SC_LANES,), F32),
            [row_buf] * ring,
            [pltpu.SemaphoreType.DMA] * ring,
        ],
        compiler_params=_sc_params(),
        name="peer_experts_pk_sc",
    )
    def k(u_hbm, v_hbm, idx_hbm, w_hbm, h_hbm, out_hbm, idx_v, coef_v, h_v, y_v, red_v, rows, sems):
        wid = _sc_worker_id()
        lane = lax.iota(I32, SC_LANES)

        def copy(tab_hbm, u, slot):
            ids = idx_v.at[pl.ds(_sc_unit_off(u), SC_LANES)]
            return pltpu.make_async_copy(tab_hbm.at[ids], rows[slot], sems[slot])

        def dots(u, slot):
            tt = u // heads

            def body(cp, accs):
                out = []
                hv = [plsc.bitcast(h_v[tt, pl.ds(pl.multiple_of((2 * cp + i) * SC_LANES, SC_LANES), SC_LANES)], BF16)
                      for i in range(2)]
                for r in range(SC_LANES):
                    pr = row_words(rows[slot], r, 2 * cp) * hv[0] + row_words(rows[slot], r, 2 * cp + 1) * hv[1]
                    lo, hi = _unpack_halves(pr)
                    out.append(accs[r] + lo + hi)
                return tuple(out)

            accs = lax.fori_loop(0, chunks // 2, body,
                                 tuple(jnp.zeros((SC_LANES,), F32) for _ in range(SC_LANES)))
            for r in range(SC_LANES):
                red_v[pl.ds(r * SC_LANES, SC_LANES)] = accs[r]
            act = _tree_sum([plsc.load_gather(red_v, [lane * SC_LANES + j]) for j in range(SC_LANES)])
            sl = pl.ds(_sc_unit_off(u), SC_LANES)
            coef_v[sl] = coef_v[sl] * _gelu_tanh(act)

        def combine(u, slot):
            tt = u // heads
            first = (u % heads) == 0
            cb = []
            for r in range(SC_LANES):
                c = plsc.load_gather(coef_v, [jnp.full((SC_LANES,), u * SC_LANES + r, I32)])
                cb.append(plsc.pack(c, c, format=plsc.PackFormat.INTERLEAVED))

            @plsc.parallel_loop(0, chunks, unroll=2)
            def _(wc):
                lo, hi = _unpack_halves(_tree_sum([cb[r] * row_words(rows[slot], r, wc) for r in range(SC_LANES)]))
                for half, val in ((0, lo), (1, hi)):
                    sl = pl.ds(pl.multiple_of(half * (d // 2) + wc * SC_LANES, SC_LANES), SC_LANES)
                    y_v[tt, sl] = val + jnp.where(first, 0.0, y_v[tt, sl])

        @pl.loop(0, groups)
        def _(gi):
            base = wid * tpw + gi * g
            pltpu.sync_copy(idx_hbm.at[pl.ds(base * nsel, g * nsel)], idx_v)
            pltpu.sync_copy(w_hbm.at[pl.ds(base * nsel, g * nsel)], coef_v)
            pltpu.sync_copy(h_hbm.at[pl.ds(base, g)], h_v)
            ring_loop(units, lambda u, s: copy(u_hbm, u, s).start(), lambda u, s: copy(u_hbm, u, s).wait(), dots)
            ring_loop(units, lambda u, s: copy(v_hbm, u, s).start(), lambda u, s: copy(v_hbm, u, s).wait(), combine)
            pltpu.sync_copy(y_v, out_hbm.at[pl.ds(base, g)])

    return k(tab_u, tab_v, idx_flat, w_flat, hp)


def kernel(x, mem, rel_bias, ln_mix, w_in, hg_lower, hg_norm, w_up_a, w_up_b, w_out, ln_cross, ln_mem, wq_x, wk_x, wv_x, wo_x, ln_ffn, peer_query, peer_subkeys, peer_u, peer_v, ln_final):
    b, s, d = x.shape
    depth = w_in.shape[0]
    assert depth == 1, "the residual after PEER is fused into the final norm"
    assert s % MB_BLOCK == 0 and s % HG_CHUNK == 0 and s % (SC_WORKERS * SC_GROUP) == 0
    nb = s // MB_BLOCK
    row = lambda a: a.reshape(1, -1).astype(F32)
    lb_all = jnp.cumsum(jax.nn.softmax(hg_lower.astype(F32), axis=0), axis=0)
    bias = moba_bias_tiles(rel_bias)
    n_hg = 4 * HG_WIDTH
    n_qk = 2 * MB_WIDTH
    n_mb = 3 * MB_WIDTH
    l = 0
    w = w_in[l].astype(BF16)
    w_hg, w_qk, w_vt, w_g = w[:, :n_hg], w[:, n_hg:n_hg + n_qk], w[:, n_hg + n_qk:n_hg + n_mb].T, w[:, n_hg + n_mb:]
    wa, wb, wo = w_up_a[l].astype(BF16), w_up_b[l].astype(BF16), w_out[l].astype(BF16)
    wqx, wox = wq_x[l].astype(BF16), wo_x[l].astype(BF16)
    wpq, sk = peer_query[l].astype(BF16), peer_subkeys[l].astype(F32)
    tab3 = lambda a: pack_bf16_pairs(a.astype(F32)).reshape(a.shape[0], SC_PK_SUB, SC_ROW_LANE)
    tab_u, tab_v = tab3(peer_u[l]), tab3(peer_v[l])
    kx, vx = mem_kv(mem, row(ln_mem[l]), wk_x[l].astype(BF16), wv_x[l].astype(BF16))

    outs = []
    for bi in range(b):
        x2d = x[bi]
        p0, pqk, vt, pg = in_proj(x2d, row(ln_mix[l]), w_hg, w_qk, w_vt, w_g)
        ya = hgrn2(p0, row(lb_all[l]), row(hg_norm[l]), 1, s)
        km = moba_kmean(pqk, 1, s).reshape(1, nb, MB_WIDTH)
        yb = moba_attention(pqk, vt, km, bias, 1, s)
        x2d = mix_out(x2d, ya, yb, pg, wa, wb, wo)
        x2d = cross_attn(x2d, row(ln_cross[l]), wqx, kx[bi:bi + 1], vx[bi:bi + 1], wox, s)
        hp, eidx, wts = peer_route(x2d, row(ln_ffn[l]), wpq, sk)
        y = peer_experts_pk_sc(tab_u, tab_v, eidx.reshape(s * PEER_SEL), wts.reshape(s * PEER_SEL), hp, d)
        outs.append(final_norm(x2d, y, row(ln_final)))
    return jnp.stack(outs, axis=0)
```

```python
import functools
import math

import jax
import jax.numpy as jnp
import numpy as np
from jax import lax
from jax.experimental import pallas as pl
from jax.experimental.pallas import tpu as pltpu
from jax.experimental.pallas import tpu_sc as plsc

F32 = jnp.float32
BF16 = jnp.bfloat16
I32 = jnp.int32
EPS = 1e-6
NEG_INF = float("-inf")

HG_HEADS = 4
HG_D = 128
HG_WIDTH = HG_HEADS * HG_D
HG_CHUNK = 64
HG_SUB = 16
MB_HEADS = 8
MB_DH = 64
MB_WIDTH = MB_HEADS * MB_DH
MB_BLOCK = 256
MB_TOPK = 3
MB_BIAS_TILES = 8
REL_BUCKETS = 32
REL_MAX_DIST = 2048
X_HEADS = 4
PEER_HEADS = 8
PEER_NKEYS = 128
PEER_TOPK = 16
PEER_HALF = 128
PEER_SEL = PEER_HEADS * PEER_TOPK
PEER_SLICES = 2

VMEM_LIMIT = 56 * 1024 * 1024


def _cparams(sem):
    return pltpu.CompilerParams(dimension_semantics=sem, vmem_limit_bytes=VMEM_LIMIT)


def _rms(x, g):
    ms = jnp.mean(x * x, axis=-1, keepdims=True)
    return x * lax.rsqrt(ms + EPS) * g


def _in_proj_kernel(x_ref, g_ref, w0_ref, w1_ref, wvt_ref, w2_ref, o0_ref, o1_ref, ovt_ref, o2_ref):
    h = _rms(x_ref[...], g_ref[...]).astype(BF16)
    o0_ref[...] = jnp.dot(h, w0_ref[...], preferred_element_type=F32)
    o1_ref[...] = jnp.dot(h, w1_ref[...], preferred_element_type=F32).astype(BF16)
    ovt_ref[0] = lax.dot_general(wvt_ref[...], h, (((1,), (1,)), ((), ())),
                                 preferred_element_type=F32).astype(BF16)
    o2_ref[...] = jnp.dot(h, w2_ref[...], preferred_element_type=F32).astype(BF16)


def in_proj(x2d, g, w0, w1, wvt, w2):
    t, d = x2d.shape
    tm = MB_BLOCK
    n0, n1, nv, n2 = w0.shape[1], w1.shape[1], wvt.shape[0], w2.shape[1]
    full = lambda a: pl.BlockSpec(a.shape, lambda i: (0, 0))
    return pl.pallas_call(
        _in_proj_kernel,
        grid=(t // tm,),
        in_specs=[pl.BlockSpec((tm, d), lambda i: (i, 0)), full(g), full(w0), full(w1), full(wvt), full(w2)],
        out_specs=[pl.BlockSpec((tm, n0), lambda i: (i, 0)),
                   pl.BlockSpec((tm, n1), lambda i: (i, 0)),
                   pl.BlockSpec((1, nv, tm), lambda i: (i, 0, 0)),
                   pl.BlockSpec((tm, n2), lambda i: (i, 0))],
        out_shape=[jax.ShapeDtypeStruct((t, n0), F32),
                   jax.ShapeDtypeStruct((t, n1), BF16),
                   jax.ShapeDtypeStruct((t // tm, nv, tm), BF16),
                   jax.ShapeDtypeStruct((t, n2), BF16)],
        compiler_params=_cparams(("parallel",)),
        name="in_proj",
    )(x2d, g, w0, w1, wvt, w2)


def _hgrn_kernel(q_ref, f_ref, i_ref, g_ref, lb_ref, gain_ref, o_ref, st_ref):
    c = pl.program_id(1)

    @pl.when(c == 0)
    def _():
        st_ref[...] = jnp.zeros_like(st_ref)

    C, S = HG_CHUNK, HG_SUB
    row = lax.broadcasted_iota(I32, (C, C), 0)
    col = lax.broadcasted_iota(I32, (C, C), 1)
    tril = (row >= col).astype(F32)
    t_iota = lax.broadcasted_iota(I32, (S, 1), 0)

    for h in range(HG_HEADS):
        sl = slice(h * HG_D, (h + 1) * HG_D)
        q = q_ref[:, sl]
        v = i_ref[:, sl]
        lb = lb_ref[:, sl]
        f = lb + (1.0 - lb) * jax.nn.sigmoid(f_ref[:, sl])
        lf = jnp.log(f)
        k = 1.0 - f
        b = jnp.dot(tril, lf, precision=lax.Precision.HIGHEST, preferred_element_type=F32)
        st = st_ref[h]
        vb = v.astype(BF16)
        qd = (q * jnp.exp(b)).astype(BF16)
        o_inter = lax.dot_general(qd, st.astype(BF16), (((1,), (1,)), ((), ())),
                                  preferred_element_type=F32)
        outs = []
        for i in range(C // S):
            r0 = i * S
            qi = q[r0:r0 + S]
            ki = k[r0:r0 + S]
            bi = b[r0:r0 + S]
            vi = v[r0:r0 + S]
            oi = o_inter[r0:r0 + S]
            if i > 0:
                bs = b[r0 - 1:r0]
                qh = (qi * jnp.exp(bi - bs)).astype(BF16)
                kh = (k[:r0] * jnp.exp(bs - b[:r0])).astype(BF16)
                a = lax.dot_general(qh, kh, (((1,), (1,)), ((), ())), preferred_element_type=F32)
                oi = oi + jnp.dot(a.astype(BF16), vb[:r0], preferred_element_type=F32)
            for s in range(S):
                dec = jnp.exp(jnp.minimum(bi - bi[s:s + 1], 0.0))
                p = qi * ki[s:s + 1] * dec
                a_s = jnp.sum(p, axis=-1, keepdims=True)
                a_s = jnp.where(t_iota >= s, a_s, 0.0)
                oi = oi + a_s * vi[s:s + 1]
            outs.append(oi)
        o = jnp.concatenate(outs, axis=0)
        b_end = b[C - 1:C]
        kd = (k * jnp.exp(b_end - b)).astype(BF16)
        upd = lax.dot_general(vb, kd, (((0,), (0,)), ((), ())), preferred_element_type=F32)
        st_ref[h] = st * jnp.exp(b_end) + upd
        o = o * lax.rsqrt(jnp.mean(o * o, axis=-1, keepdims=True) + EPS)
        g = g_ref[:, sl]
        o_ref[:, sl] = (o * gain_ref[:, sl] * (g * jax.nn.sigmoid(g))).astype(o_ref.dtype)


def hgrn2(p0, lb, gain, batch, seq):
    t = p0.shape[0]
    nc = seq // HG_CHUNK
    w = HG_WIDTH

    def col(j):
        return pl.BlockSpec((HG_CHUNK, w), lambda b, c, j=j: (b * nc + c, j))

    return pl.pallas_call(
        _hgrn_kernel,
        grid=(batch, nc),
        in_specs=[col(0), col(1), col(2), col(3),
                  pl.BlockSpec((1, w), lambda b, c: (0, 0)),
                  pl.BlockSpec((1, w), lambda b, c: (0, 0))],
        out_specs=pl.BlockSpec((HG_CHUNK, w), lambda b, c: (b * nc + c, 0)),
        out_shape=jax.ShapeDtypeStruct((t, w), BF16),
        scratch_shapes=[pltpu.VMEM((HG_HEADS, HG_D, HG_D), F32)],
        compiler_params=_cparams(("parallel", "arbitrary")),
        name="hgrn2",
    )(p0, p0, p0, p0, lb, gain)


def _kmean_kernel(k_ref, o_ref):
    o_ref[0] = jnp.mean(k_ref[...].astype(F32), axis=0, keepdims=True)


def moba_kmean(p1, batch, seq):
    nbt = p1.shape[0] // MB_BLOCK
    return pl.pallas_call(
        _kmean_kernel,
        grid=(nbt,),
        in_specs=[pl.BlockSpec((MB_BLOCK, MB_WIDTH), lambda i: (i, 1))],
        out_specs=pl.BlockSpec((1, 1, MB_WIDTH), lambda i: (i, 0, 0)),
        out_shape=jax.ShapeDtypeStruct((nbt, 1, MB_WIDTH), F32),
        compiler_params=_cparams(("parallel",)),
        name="moba_kmean",
    )(p1)


MB_PAIR = 4
MB_PW = MB_PAIR * MB_DH
MB_LG = 128


def _moba_kernel(q_ref, k_ref, vt_ref, km_ref, bias_ref, o_ref, *scratch):
    m_ref, l_ref, acc_ref, msk_ref = (scratch[i * MB_PAIR:(i + 1) * MB_PAIR] for i in range(4))
    qi = pl.program_id(2)
    nb = km_ref.shape[0]
    blk = MB_BLOCK
    heads = range(MB_PAIR)
    grp = lambda hh: slice((hh // 2) * MB_LG, (hh // 2 + 1) * MB_LG)
    q = q_ref[...]
    lane = lax.broadcasted_iota(I32, (blk, MB_LG), 1)
    in_head = [(lane < MB_DH) if hh % 2 == 0 else (lane >= MB_DH) for hh in heads]
    qs = q * jnp.asarray(MB_DH ** -0.5, BF16)
    qh = [jnp.where(in_head[hh], qs[:, grp(hh)], jnp.zeros((blk, MB_LG), BF16)) for hh in heads]
    nt = (((1,), (1,)), ((), ()))

    qf = q.astype(F32)
    n_io = lax.broadcasted_iota(I32, (nb, blk), 0)
    for hh in heads:
        gate = lax.dot_general(km_ref[:, grp(hh)], jnp.where(in_head[hh], qf[:, grp(hh)], 0.0), nt,
                               precision=lax.Precision.HIGHEST, preferred_element_type=F32)
        gate = jnp.where(n_io < qi, gate, NEG_INF)
        chosen = n_io < 0
        for _ in range(MB_TOPK):
            mx = jnp.max(gate, axis=0, keepdims=True)
            ix = jnp.min(jnp.where(gate == mx, n_io, nb), axis=0, keepdims=True)
            hit = n_io == ix
            chosen = chosen | (hit & (mx > NEG_INF))
            gate = jnp.where(hit, NEG_INF, gate)
        msk_ref[hh][...] = jnp.where(chosen, 0.0, NEG_INF)

    k_own = k_ref[pl.ds(pl.multiple_of(qi * blk, blk), blk), :]
    vt_own = vt_ref[qi]
    key_io = lax.broadcasted_iota(I32, (blk, blk), 0)
    qry_io = lax.broadcasted_iota(I32, (blk, blk), 1)
    own_rows = lambda r, hh: r[(hh % 2) * MB_DH:(hh % 2 + 1) * MB_DH]
    for hh in heads:
        s = lax.dot_general(k_own[:, grp(hh)], qh[hh], nt, preferred_element_type=F32) + bias_ref[hh, 0]
        s = jnp.where(key_io <= qry_io, s, NEG_INF)
        m0 = jnp.max(s, axis=0, keepdims=True)
        p = jnp.exp(s - m0)
        m_ref[hh][...] = m0
        l_ref[hh][...] = jnp.sum(p, axis=0, keepdims=True)
        r = jnp.dot(vt_own[grp(hh)], p.astype(BF16), preferred_element_type=F32)
        acc_ref[hh][...] = own_rows(r, hh)

    def past(n, carry):
        kn = k_ref[pl.ds(pl.multiple_of(n * blk, blk), blk), :]
        vtn = vt_ref[n]
        d = jnp.minimum(qi - n, MB_BIAS_TILES - 1)
        s = [lax.dot_general(kn[:, grp(hh)], qh[hh], nt, preferred_element_type=F32)
             + bias_ref[hh, d] + msk_ref[hh][pl.ds(n, 1), :] for hh in heads]
        m_old = [m_ref[hh][...] for hh in heads]
        l_old = [l_ref[hh][...] for hh in heads]
        a_old = [acc_ref[hh][...] for hh in heads]
        m_new = [jnp.maximum(m_old[hh], jnp.max(s[hh], axis=0, keepdims=True)) for hh in heads]
        alpha = [jnp.exp(m_old[hh] - m_new[hh]) for hh in heads]
        p = [jnp.exp(s[hh] - m_new[hh]) for hh in heads]
        r = [jnp.dot(vtn[grp(hh)], p[hh].astype(BF16), preferred_element_type=F32) for hh in heads]
        l_new = [alpha[hh] * l_old[hh] + jnp.sum(p[hh], axis=0, keepdims=True) for hh in heads]
        a_new = [alpha[hh] * a_old[hh] + own_rows(r[hh], hh) for hh in heads]
        for hh in heads:
            m_ref[hh][...] = m_new[hh]
            l_ref[hh][...] = l_new[hh]
            acc_ref[hh][...] = a_new[hh]
        return carry

    lax.fori_loop(0, qi, past, 0)
    out_t = jnp.concatenate([acc_ref[hh][...] / l_ref[hh][...] for hh in heads], axis=0)
    o_ref[...] = out_t.T.astype(o_ref.dtype)


def moba_attention(pqk, vt, km, bias, batch, seq):
    t = pqk.shape[0]
    nb = seq // MB_BLOCK
    groups = MB_WIDTH // MB_PW
    return pl.pallas_call(
        _moba_kernel,
        grid=(batch, groups, nb),
        in_specs=[
            pl.BlockSpec((MB_BLOCK, MB_PW), lambda b, j, i: (b * nb + i, j)),
            pl.BlockSpec((seq, MB_PW), lambda b, j, i: (b, groups + j)),
            pl.BlockSpec((nb, MB_PW, MB_BLOCK), lambda b, j, i: (b, j, 0)),
            pl.BlockSpec((None, nb, MB_PW), lambda b, j, i: (b, 0, j)),
            pl.BlockSpec((MB_PAIR, MB_BIAS_TILES, MB_BLOCK, MB_BLOCK), lambda b, j, i: (j, 0, 0, 0)),
        ],
        out_specs=pl.BlockSpec((MB_BLOCK, MB_PW), lambda b, j, i: (b * nb + i, j)),
        out_shape=jax.ShapeDtypeStruct((t, MB_WIDTH), BF16),
        scratch_shapes=(
            [pltpu.VMEM((1, MB_BLOCK), F32)] * (2 * MB_PAIR)
            + [pltpu.VMEM((MB_DH, MB_BLOCK), F32)] * MB_PAIR
            + [pltpu.VMEM((nb, MB_BLOCK), F32)] * MB_PAIR
        ),
        compiler_params=_cparams(("parallel", "parallel", "arbitrary")),
        name="moba_attn",
    )(pqk, pqk, vt, km, bias)


def _t5_bucket(dist):
    max_exact = REL_BUCKETS // 2
    scaled = jnp.log(jnp.maximum(dist, 1).astype(F32) / max_exact) / math.log(REL_MAX_DIST / max_exact)
    large = jnp.minimum(max_exact + (scaled * (REL_BUCKETS - max_exact)).astype(I32), REL_BUCKETS - 1)
    return jnp.where(dist < max_exact, dist, large)


def moba_bias_tiles(rel_bias):
    blk = MB_BLOCK
    span = 2 * blk - 1
    x = jnp.arange(span) - (blk - 1)
    dist = jnp.maximum(jnp.arange(MB_BIAS_TILES)[:, None] * blk + x[None, :], 0)
    w = rel_bias.astype(F32).T[:, _t5_bucket(dist)]
    h = w.shape[0]
    wp = jnp.pad(w, ((0, 0), (0, 0), (0, 1)))
    a = jnp.broadcast_to(wp[:, :, None, :], (h, MB_BIAS_TILES, blk, span + 1))
    a = a.reshape(h, MB_BIAS_TILES, blk * (span + 1))[:, :, :blk * span]
    return a.reshape(h, MB_BIAS_TILES, blk, span)[:, :, :, blk - 1:]


def _mix_kernel(x_ref, ya_ref, yb_ref, ga_ref, gb_ref, wa_ref, wb_ref, wo_ref, o_ref):
    za = jnp.dot(ya_ref[...], wa_ref[...], preferred_element_type=F32)
    zb = jnp.dot(yb_ref[...], wb_ref[...], preferred_element_type=F32)
    z = jax.nn.sigmoid(ga_ref[...].astype(F32)) * za + jax.nn.sigmoid(gb_ref[...].astype(F32)) * zb
    o_ref[...] = x_ref[...] + jnp.dot(z.astype(BF16), wo_ref[...], preferred_element_type=F32)


def mix_out(x2d, ya, yb, pg, wa, wb, wo, tm=256):
    t, d = x2d.shape
    w = ya.shape[1]
    return pl.pallas_call(
        _mix_kernel,
        grid=(t // tm,),
        in_specs=[
            pl.BlockSpec((tm, d), lambda i: (i, 0)),
            pl.BlockSpec((tm, w), lambda i: (i, 0)),
            pl.BlockSpec((tm, w), lambda i: (i, 0)),
            pl.BlockSpec((tm, d), lambda i: (i, 0)),
            pl.BlockSpec((tm, d), lambda i: (i, 1)),
            pl.BlockSpec((w, d), lambda i: (0, 0)),
            pl.BlockSpec((w, d), lambda i: (0, 0)),
            pl.BlockSpec((d, d), lambda i: (0, 0)),
        ],
        out_specs=pl.BlockSpec((tm, d), lambda i: (i, 0)),
        out_shape=jax.ShapeDtypeStruct((t, d), F32),
        compiler_params=_cparams(("parallel",)),
        name="mix_out",
    )(x2d, ya, yb, pg, pg, wa, wb, wo)


def _mem_kv_kernel(m_ref, g_ref, wk_ref, wv_ref, k_ref, v_ref):
    mn = _rms(m_ref[...], g_ref[...]).astype(BF16)
    k_ref[...] = jnp.dot(mn, wk_ref[...], preferred_element_type=F32).astype(BF16)
    v_ref[...] = jnp.dot(mn, wv_ref[...], preferred_element_type=F32).astype(BF16)


def mem_kv(mem, g, wk, wv):
    b, m, d = mem.shape
    spec = pl.BlockSpec((None, m, d), lambda i: (i, 0, 0))
    wspec = pl.BlockSpec((d, d), lambda i: (0, 0))
    return pl.pallas_call(
        _mem_kv_kernel,
        grid=(b,),
        in_specs=[spec, pl.BlockSpec((1, d), lambda i: (0, 0)), wspec, wspec],
        out_specs=[spec, spec],
        out_shape=[jax.ShapeDtypeStruct((b, m, d), BF16)] * 2,
        compiler_params=_cparams(("parallel",)),
        name="mem_kv",
    )(mem, g, wk, wv)


def _cross_kernel(x_ref, g_ref, wq_ref, k_ref, v_ref, wo_ref, o_ref):
    x = x_ref[...]
    d = x.shape[1]
    dh = d // X_HEADS
    h = _rms(x, g_ref[...]).astype(BF16)
    q = (jnp.dot(h, wq_ref[...], preferred_element_type=F32) * (dh ** -0.5)).astype(BF16)
    outs = []
    for hh in range(X_HEADS):
        sl = slice(hh * dh, (hh + 1) * dh)
        s = lax.dot_general(q[:, sl], k_ref[:, sl], (((1,), (1,)), ((), ())),
                            preferred_element_type=F32)
        p = jnp.exp(s - jnp.max(s, axis=1, keepdims=True))
        l = jnp.sum(p, axis=1, keepdims=True)
        o = jnp.dot(p.astype(BF16), v_ref[:, sl], preferred_element_type=F32) / l
        outs.append(o.astype(BF16))
    o = jnp.concatenate(outs, axis=1)
    o_ref[...] = x + jnp.dot(o, wo_ref[...], preferred_element_type=F32)


def cross_attn(x2d, g, wq, kx, vx, wo, seq, tm=256):
    t, d = x2d.shape
    m = kx.shape[1]
    per_b = seq // tm
    kv = pl.BlockSpec((None, m, d), lambda i: (i // per_b, 0, 0))
    wspec = pl.BlockSpec((d, d), lambda i: (0, 0))
    return pl.pallas_call(
        _cross_kernel,
        grid=(t // tm,),
        in_specs=[pl.BlockSpec((tm, d), lambda i: (i, 0)), pl.BlockSpec((1, d), lambda i: (0, 0)),
                  wspec, kv, kv, wspec],
        out_specs=pl.BlockSpec((tm, d), lambda i: (i, 0)),
        out_shape=jax.ShapeDtypeStruct((t, d), F32),
        compiler_params=_cparams(("parallel",)),
        name="cross_attn",
    )(x2d, g, wq, kx, vx, wo)


def _topk_rows(sc, k):
    n = sc.shape[0]
    io = lax.broadcasted_iota(I32, sc.shape, 0)
    vals, ids = [], []
    for _ in range(k):
        m = jnp.max(sc, axis=0, keepdims=True)
        ix = jnp.min(jnp.where(sc == m, io, n), axis=0, keepdims=True)
        vals.append(m)
        ids.append(ix)
        sc = jnp.where(io == ix, NEG_INF, sc)
    return jnp.concatenate(vals, axis=0), jnp.concatenate(ids, axis=0)


def _pack_bf16_halves(h):
    bits = lax.bitcast_convert_type(h, I32)
    r = bits + 0x7FFF + (lax.shift_right_logical(bits, 16) & 1)
    half = h.shape[1] // 2
    return lax.shift_right_logical(r[:, :half], 16) | (r[:, half:] & HI_MASK)


def _route_kernel(x_ref, g_ref, wq_ref, sk_ref, hp_ref, idx_ref, w_ref, hb_ref, it_ref, wt_ref):
    p = pl.program_id(1)

    @pl.when(p == 0)
    def _():
        h = _rms(x_ref[...], g_ref[...])
        hp_ref[...] = _pack_bf16_halves(h)
        hb_ref[...] = h.astype(BF16)

    qh = jnp.dot(hb_ref[...], wq_ref[...], preferred_element_type=F32)
    tops = []
    for c in range(2):
        seg = qh[:, c * PEER_HALF:(c + 1) * PEER_HALF]
        sc = lax.dot_general(sk_ref[c], seg, (((1,), (1,)), ((), ())),
                             precision=lax.Precision.HIGHEST, preferred_element_type=F32)
        tops.append(_topk_rows(sc, PEER_TOPK))
    (s0, i0), (s1, i1) = tops
    k = PEER_TOPK
    sub = 8
    tm = s0.shape[1]
    r8 = lax.broadcasted_iota(I32, (sub, tm), 0)
    r16 = lax.broadcasted_iota(I32, (k, tm), 0)
    cand_b = [s0[0:1] + s1, s0[1:2] + s1[:sub]]
    cidx_b = [i0[0:1] * PEER_NKEYS + i1, i0[1:2] * PEER_NKEYS + i1[:sub]]
    pos_b = [r16, k + r8]
    for a in range(2, sub):
        keep = r8 < (k // (a + 1))
        cand_b.append(jnp.where(keep, s0[a:a + 1] + s1[:sub], NEG_INF))
        cidx_b.append(i0[a:a + 1] * PEER_NKEYS + i1[:sub])
        pos_b.append(a * k + r8)
    cand_b.append(s0[sub:] + s1[0:1])
    cidx_b.append(i0[sub:] * PEER_NKEYS + i1[0:1])
    pos_b.append((sub + r8) * k)
    cand = jnp.concatenate(cand_b, axis=0)
    cidx = jnp.concatenate(cidx_b, axis=0)
    pos = jnp.concatenate(pos_b, axis=0)
    vals, ids = [], []
    for _ in range(k):
        m = jnp.max(cand, axis=0, keepdims=True)
        px = jnp.min(jnp.where(cand == m, pos, k * k), axis=0, keepdims=True)
        hit = pos == px
        vals.append(m)
        ids.append(jnp.sum(jnp.where(hit, cidx, 0), axis=0, keepdims=True))
        cand = jnp.where(hit, NEG_INF, cand)
    sf = jnp.concatenate(vals, axis=0)
    e = jnp.exp(sf - sf[0:1])
    rows = pl.ds(pl.multiple_of(p * PEER_TOPK, PEER_TOPK), PEER_TOPK)
    wt_ref[rows, :] = e / jnp.sum(e, axis=0, keepdims=True)
    it_ref[rows, :] = jnp.concatenate(ids, axis=0)

    @pl.when(p == pl.num_programs(1) - 1)
    def _():
        idx_ref[...] = it_ref[...].T
        w_ref[...] = wt_ref[...].T


def peer_route(x2d, g, wq, sk, tok0, t, tm=256):
    d = x2d.shape[1]
    ph = sk.shape[0]
    nsel = ph * PEER_TOPK
    blk0 = tok0 // tm
    return pl.pallas_call(
        _route_kernel,
        grid=(t // tm, ph),
        in_specs=[
            pl.BlockSpec((tm, d), lambda i, p: (blk0 + i, 0)),
            pl.BlockSpec((1, d), lambda i, p: (0, 0)),
            pl.BlockSpec((d, 2 * PEER_HALF), lambda i, p: (0, p)),
            pl.BlockSpec((None, 2, PEER_NKEYS, PEER_HALF), lambda i, p: (p, 0, 0, 0)),
        ],
        out_specs=[
            pl.BlockSpec((tm, d // 2), lambda i, p: (i, 0)),
            pl.BlockSpec((tm, nsel), lambda i, p: (i, 0)),
            pl.BlockSpec((tm, nsel), lambda i, p: (i, 0)),
        ],
        out_shape=[jax.ShapeDtypeStruct((t, d // 2), I32),
                   jax.ShapeDtypeStruct((t, nsel), I32),
                   jax.ShapeDtypeStruct((t, nsel), F32)],
        scratch_shapes=[pltpu.VMEM((tm, d), BF16),
                        pltpu.VMEM((nsel, tm), I32),
                        pltpu.VMEM((nsel, tm), F32)],
        compiler_params=_cparams(("parallel", "arbitrary")),
        name="peer_route",
    )(x2d, g, wq, sk)


def _coef_kernel(w_ref, a_ref, o_ref):
    o_ref[...] = w_ref[...] * jax.nn.gelu(a_ref[...])


def peer_coef(w, act, tm=1024):
    t, n = w.shape
    spec = pl.BlockSpec((tm, n), lambda i: (i, 0))
    return pl.pallas_call(
        _coef_kernel, grid=(t // tm,), in_specs=[spec, spec], out_specs=spec,
        out_shape=jax.ShapeDtypeStruct((t, n), F32),
        compiler_params=_cparams(("parallel",)), name="peer_coef",
    )(w, act)


def _final_kernel(x_ref, y_ref, g_ref, o_ref):
    o_ref[...] = _rms(x_ref[...] + y_ref[...], g_ref[...])


def final_norm(x2d, y, g, tok0, tm=512):
    t, d = y.shape
    blk0 = tok0 // tm
    spec = pl.BlockSpec((tm, d), lambda i: (i, 0))
    return pl.pallas_call(
        _final_kernel, grid=(t // tm,),
        in_specs=[pl.BlockSpec((tm, d), lambda i: (blk0 + i, 0)), spec, pl.BlockSpec((1, d), lambda i: (0, 0))],
        out_specs=spec,
        out_shape=jax.ShapeDtypeStruct((t, d), F32),
        compiler_params=_cparams(("parallel",)), name="final_norm",
    )(x2d, y, g)


SC_CORES = 2
SC_SUBCORES = 16
SC_WORKERS = SC_CORES * SC_SUBCORES
SC_LANES = 16
SC_GROUP = 16


def _sc_mesh():
    return plsc.VectorSubcoreMesh(core_axis_name="c", subcore_axis_name="s")


def _sc_params():
    return pltpu.CompilerParams(needs_layout_passes=False)


def _sc_worker_id():
    return lax.axis_index("s") * SC_CORES + lax.axis_index("c")


SC_RING = 4
SC_ROW_SUB = 8
SC_ROW_LANE = 128


def _sc_ring(n_units, start, wait, compute):
    for u in range(SC_RING - 1):
        start(u, u)

    @pl.loop(0, n_units, step=SC_RING)
    def _(uu):
        for b in range(SC_RING):
            u = uu + b
            nxt = u + (SC_RING - 1)

            @pl.when(nxt < n_units)
            def _():
                start(nxt, (b + SC_RING - 1) % SC_RING)

            wait(u, b)
            compute(u, b)


def _sc_unit_off(u):
    off = u * SC_LANES
    return off if isinstance(off, int) else pl.multiple_of(off, SC_LANES)


def _sc_row_piece(rows, r, c):
    per = SC_ROW_LANE // SC_LANES
    return rows[r, c // per, pl.ds(pl.multiple_of((c % per) * SC_LANES, SC_LANES), SC_LANES)]


def peer_dots_sc(table, idx_flat, h):
    t, d = h.shape
    nsel = PEER_SEL
    tpw = t // SC_WORKERS
    g = SC_GROUP
    groups = tpw // g
    heads = nsel // SC_LANES
    pieces = d // SC_LANES
    units = g * heads
    row_buf = pltpu.VMEM((SC_LANES, SC_ROW_SUB, SC_ROW_LANE), F32)

    @functools.partial(
        pl.kernel, mesh=_sc_mesh(),
        out_type=jax.ShapeDtypeStruct((t * nsel,), F32),
        scratch_types=[
            pltpu.VMEM((g * nsel,), I32),
            pltpu.VMEM((g, d), F32),
            pltpu.VMEM((g * nsel,), F32),
            pltpu.VMEM((SC_LANES * SC_LANES,), F32),
            [row_buf] * SC_RING,
            [pltpu.SemaphoreType.DMA] * SC_RING,
        ],
        compiler_params=_sc_params(),
        name="peer_dots_sc",
    )
    def k(tab_hbm, idx_hbm, h_hbm, out_hbm, idx_v, h_v, out_v, red_v, rows, sems):
        wid = _sc_worker_id()
        lane = lax.iota(I32, SC_LANES)

        def copy(u, slot):
            ids = idx_v.at[pl.ds(_sc_unit_off(u), SC_LANES)]
            return pltpu.make_async_copy(tab_hbm.at[ids], rows[slot], sems[slot])

        def compute(u, slot):
            tt = u // heads

            def body(c, accs):
                hv = h_v[tt, pl.ds(pl.multiple_of(c * SC_LANES, SC_LANES), SC_LANES)]
                return tuple(accs[r] + _sc_row_piece(rows[slot], r, c) * hv for r in range(SC_LANES))

            accs = lax.fori_loop(0, pieces, body,
                                 tuple(jnp.zeros((SC_LANES,), F32) for _ in range(SC_LANES)))
            for r in range(SC_LANES):
                red_v[pl.ds(r * SC_LANES, SC_LANES)] = accs[r]
            cols = [plsc.load_gather(red_v, [lane * SC_LANES + j]) for j in range(SC_LANES)]
            while len(cols) > 1:
                cols = [cols[i] + cols[i + 1] for i in range(0, len(cols), 2)]
            out_v[pl.ds(_sc_unit_off(u), SC_LANES)] = cols[0]

        @pl.loop(0, groups)
        def _(gi):
            base = wid * tpw + gi * g
            pltpu.sync_copy(idx_hbm.at[pl.ds(base * nsel, g * nsel)], idx_v)
            pltpu.sync_copy(h_hbm.at[pl.ds(base, g)], h_v)
            _sc_ring(units, lambda u, s: copy(u, s).start(), lambda u, s: copy(u, s).wait(), compute)
            pltpu.sync_copy(out_v, out_hbm.at[pl.ds(base * nsel, g * nsel)])

    return k(table, idx_flat, h)


def peer_combine_sc(table, idx_flat, coef_flat, t):
    d = table.shape[1] * table.shape[2]
    nsel = PEER_SEL
    tpw = t // SC_WORKERS
    g = SC_GROUP
    groups = tpw // g
    heads = nsel // SC_LANES
    pieces = d // SC_LANES
    units = g * heads
    row_buf = pltpu.VMEM((SC_LANES, SC_ROW_SUB, SC_ROW_LANE), F32)

    @functools.partial(
        pl.kernel, mesh=_sc_mesh(),
        out_type=jax.ShapeDtypeStruct((t, d), F32),
        scratch_types=[
            pltpu.VMEM((g * nsel,), I32),
            pltpu.VMEM((g * nsel,), F32),
            pltpu.VMEM((g, d), F32),
            [row_buf] * SC_RING,
            [pltpu.SemaphoreType.DMA] * SC_RING,
        ],
        compiler_params=_sc_params(),
        name="peer_combine_sc",
    )
    def k(tab_hbm, idx_hbm, coef_hbm, out_hbm, idx_v, coef_v, y_v, rows, sems):
        wid = _sc_worker_id()

        def copy(u, slot):
            ids = idx_v.at[pl.ds(_sc_unit_off(u), SC_LANES)]
            return pltpu.make_async_copy(tab_hbm.at[ids], rows[slot], sems[slot])

        def compute(u, slot):
            tt = u // heads
            first = (u % heads) == 0
            cs = [plsc.load_gather(coef_v, [jnp.full((SC_LANES,), u * SC_LANES + r, I32)])
                  for r in range(SC_LANES)]

            @plsc.parallel_loop(0, pieces, unroll=2)
            def _(c):
                off = pl.multiple_of(c * SC_LANES, SC_LANES)
                terms = [cs[r] * _sc_row_piece(rows[slot], r, c) for r in range(SC_LANES)]
                while len(terms) > 1:
                    terms = [terms[i] + terms[i + 1] for i in range(0, len(terms), 2)]
                prev = y_v[tt, pl.ds(off, SC_LANES)]
                y_v[tt, pl.ds(off, SC_LANES)] = terms[0] + jnp.where(first, 0.0, prev)

        @pl.loop(0, groups)
        def _(gi):
            base = wid * tpw + gi * g
            pltpu.sync_copy(idx_hbm.at[pl.ds(base * nsel, g * nsel)], idx_v)
            pltpu.sync_copy(coef_hbm.at[pl.ds(base * nsel, g * nsel)], coef_v)
            _sc_ring(units, lambda u, s: copy(u, s).start(), lambda u, s: copy(u, s).wait(), compute)
            pltpu.sync_copy(y_v, out_hbm.at[pl.ds(base, g)])

    return k(table, idx_flat, coef_flat)


GELU_C0 = math.sqrt(2.0 / math.pi)
GELU_C1 = 0.044715


def _gelu_tanh(x):
    z = GELU_C0 * (x + GELU_C1 * (x * x * x))
    th = 1.0 - 2.0 / (jnp.exp(2.0 * z) + 1.0)
    return 0.5 * x * (1.0 + th)


def peer_experts_sc(tab_u, tab_v, idx_flat, w_flat, h):
    t, d = h.shape
    nsel = PEER_SEL
    tpw = t // SC_WORKERS
    g = SC_GROUP
    groups = tpw // g
    heads = nsel // SC_LANES
    pieces = d // SC_LANES
    units = g * heads
    row_buf = pltpu.VMEM((SC_LANES, SC_ROW_SUB, SC_ROW_LANE), F32)

    @functools.partial(
        pl.kernel, mesh=_sc_mesh(),
        out_type=jax.ShapeDtypeStruct((t, d), F32),
        scratch_types=[
            pltpu.VMEM((g * nsel,), I32),
            pltpu.VMEM((g * nsel,), F32),
            pltpu.VMEM((g, d), F32),
            pltpu.VMEM((g, d), F32),
            pltpu.VMEM((SC_LANES * SC_LANES,), F32),
            [row_buf] * SC_RING,
            [pltpu.SemaphoreType.DMA] * SC_RING,
        ],
        compiler_params=_sc_params(),
        name="peer_experts_sc",
    )
    def k(u_hbm, v_hbm, idx_hbm, w_hbm, h_hbm, out_hbm, idx_v, coef_v, h_v, y_v, red_v, rows, sems):
        wid = _sc_worker_id()
        lane = lax.iota(I32, SC_LANES)

        def copy(tab_hbm, u, slot):
            ids = idx_v.at[pl.ds(_sc_unit_off(u), SC_LANES)]
            return pltpu.make_async_copy(tab_hbm.at[ids], rows[slot], sems[slot])

        def dots(u, slot):
            tt = u // heads

            def body(c, accs):
                hv = h_v[tt, pl.ds(pl.multiple_of(c * SC_LANES, SC_LANES), SC_LANES)]
                return tuple(accs[r] + _sc_row_piece(rows[slot], r, c) * hv for r in range(SC_LANES))

            accs = lax.fori_loop(0, pieces, body,
                                 tuple(jnp.zeros((SC_LANES,), F32) for _ in range(SC_LANES)))
            for r in range(SC_LANES):
                red_v[pl.ds(r * SC_LANES, SC_LANES)] = accs[r]
            cols = [plsc.load_gather(red_v, [lane * SC_LANES + j]) for j in range(SC_LANES)]
            while len(cols) > 1:
                cols = [cols[i] + cols[i + 1] for i in range(0, len(cols), 2)]
            sl = pl.ds(_sc_unit_off(u), SC_LANES)
            coef_v[sl] = coef_v[sl] * _gelu_tanh(cols[0])

        def combine(u, slot):
            tt = u // heads
            first = (u % heads) == 0
            cs = [plsc.load_gather(coef_v, [jnp.full((SC_LANES,), u * SC_LANES + r, I32)])
                  for r in range(SC_LANES)]

            @plsc.parallel_loop(0, pieces, unroll=2)
            def _(c):
                off = pl.multiple_of(c * SC_LANES, SC_LANES)
                terms = [cs[r] * _sc_row_piece(rows[slot], r, c) for r in range(SC_LANES)]
                while len(terms) > 1:
                    terms = [terms[i] + terms[i + 1] for i in range(0, len(terms), 2)]
                prev = y_v[tt, pl.ds(off, SC_LANES)]
                y_v[tt, pl.ds(off, SC_LANES)] = terms[0] + jnp.where(first, 0.0, prev)

        @pl.loop(0, groups)
        def _(gi):
            base = wid * tpw + gi * g
            pltpu.sync_copy(idx_hbm.at[pl.ds(base * nsel, g * nsel)], idx_v)
            pltpu.sync_copy(w_hbm.at[pl.ds(base * nsel, g * nsel)], coef_v)
            pltpu.sync_copy(h_hbm.at[pl.ds(base, g)], h_v)
            _sc_ring(units, lambda u, s: copy(u_hbm, u, s).start(), lambda u, s: copy(u_hbm, u, s).wait(), dots)
            _sc_ring(units, lambda u, s: copy(v_hbm, u, s).start(), lambda u, s: copy(v_hbm, u, s).wait(), combine)
            pltpu.sync_copy(y_v, out_hbm.at[pl.ds(base, g)])

    return k(tab_u, tab_v, idx_flat, w_flat, h)


SC_PK_RING = 8
SC_PK_SUB = 4
HI_MASK = -65536


def pack_bf16_pairs(a):
    half = a.shape[1] // 2
    bits = lax.bitcast_convert_type(a.astype(BF16), jnp.uint16).astype(jnp.uint32)
    return lax.bitcast_convert_type(bits[:, :half] | (bits[:, half:] << 16), I32)


def _unpack_halves(x32):
    w = plsc.bitcast(x32, I32)
    return plsc.bitcast(w << 16, F32), plsc.bitcast(w & HI_MASK, F32)


def _tree_sum(xs):
    while len(xs) > 1:
        xs = [xs[i] + xs[i + 1] for i in range(0, len(xs), 2)]
    return xs[0]


def peer_experts_pk_sc(tab_u, tab_v, idx_flat, w_flat, hp, d):
    t = hp.shape[0]
    nsel = PEER_SEL
    tpw = t // SC_WORKERS
    g = SC_GROUP
    groups = tpw // g
    heads = nsel // SC_LANES
    chunks = d // 32
    units = g * heads
    ring = SC_PK_RING
    row_buf = pltpu.VMEM((SC_LANES, SC_PK_SUB, SC_ROW_LANE), I32)

    def row_words(rows, r, wc):
        per = SC_ROW_LANE // SC_LANES
        return plsc.bitcast(rows[r, wc // per, pl.ds(pl.multiple_of((wc % per) * SC_LANES, SC_LANES), SC_LANES)],
                            BF16)

    def ring_loop(n_units, start, wait, compute):
        for u in range(ring - 1):
            start(u, u)

        @pl.loop(0, n_units, step=ring)
        def _(uu):
            for b in range(ring):
                u = uu + b
                nxt = u + (ring - 1)

                @pl.when(nxt < n_units)
                def _():
                    start(nxt, (b + ring - 1) % ring)

                wait(u, b)
                compute(u, b)

    @functools.partial(
        pl.kernel, mesh=_sc_mesh(),
        out_type=jax.ShapeDtypeStruct((t, d), F32),
        scratch_types=[
            pltpu.VMEM((g * nsel,), I32),
            pltpu.VMEM((g * nsel,), F32),
            pltpu.VMEM((g, d // 2), I32),
            pltpu.VMEM((g, d), F32),
            pltpu.VMEM((SC_LANES * SC_LANES,), F32),
            [row_buf] * ring,
            [pltpu.SemaphoreType.DMA] * ring,
        ],
        compiler_params=_sc_params(),
        name="peer_experts_pk_sc",
    )
    def k(u_hbm, v_hbm, idx_hbm, w_hbm, h_hbm, out_hbm, idx_v, coef_v, h_v, y_v, red_v, rows, sems):
        wid = _sc_worker_id()
        lane = lax.iota(I32, SC_LANES)

        def copy(tab_hbm, u, slot):
            ids = idx_v.at[pl.ds(_sc_unit_off(u), SC_LANES)]
            return pltpu.make_async_copy(tab_hbm.at[ids], rows[slot], sems[slot])

        def dots(u, slot):
            tt = u // heads

            def body(cp, accs):
                out = []
                hv = [plsc.bitcast(h_v[tt, pl.ds(pl.multiple_of((2 * cp + i) * SC_LANES, SC_LANES), SC_LANES)], BF16)
                      for i in range(2)]
                for r in range(SC_LANES):
                    pr = row_words(rows[slot], r, 2 * cp) * hv[0] + row_words(rows[slot], r, 2 * cp + 1) * hv[1]
                    lo, hi = _unpack_halves(pr)
                    out.append(accs[r] + lo + hi)
                return tuple(out)

            accs = lax.fori_loop(0, chunks // 2, body,
                                 tuple(jnp.zeros((SC_LANES,), F32) for _ in range(SC_LANES)))
            for r in range(SC_LANES):
                red_v[pl.ds(r * SC_LANES, SC_LANES)] = accs[r]
            act = _tree_sum([plsc.load_gather(red_v, [lane * SC_LANES + j]) for j in range(SC_LANES)])
            sl = pl.ds(_sc_unit_off(u), SC_LANES)
            coef_v[sl] = coef_v[sl] * _gelu_tanh(act)

        def combine(u, slot):
            tt = u // heads
            first = (u % heads) == 0
            cb = []
            for r in range(SC_LANES):
                c = plsc.load_gather(coef_v, [jnp.full((SC_LANES,), u * SC_LANES + r, I32)])
                cb.append(plsc.pack(c, c, format=plsc.PackFormat.INTERLEAVED))

            @plsc.parallel_loop(0, chunks, unroll=2)
            def _(wc):
                lo, hi = _unpack_halves(_tree_sum([cb[r] * row_words(rows[slot], r, wc) for r in range(SC_LANES)]))
                for half, val in ((0, lo), (1, hi)):
                    sl = pl.ds(pl.multiple_of(half * (d // 2) + wc * SC_LANES, SC_LANES), SC_LANES)
                    y_v[tt, sl] = val + jnp.where(first, 0.0, y_v[tt, sl])

        @pl.loop(0, groups)
        def _(gi):
            base = wid * tpw + gi * g
            pltpu.sync_copy(idx_hbm.at[pl.ds(base * nsel, g * nsel)], idx_v)
            pltpu.sync_copy(w_hbm.at[pl.ds(base * nsel, g * nsel)], coef_v)
            pltpu.sync_copy(h_hbm.at[pl.ds(base, g)], h_v)
            ring_loop(units, lambda u, s: copy(u_hbm, u, s).start(), lambda u, s: copy(u_hbm, u, s).wait(), dots)
            ring_loop(units, lambda u, s: copy(v_hbm, u, s).start(), lambda u, s: copy(v_hbm, u, s).wait(), combine)
            pltpu.sync_copy(y_v, out_hbm.at[pl.ds(base, g)])

    return k(tab_u, tab_v, idx_flat, w_flat, hp)


def kernel(x, mem, rel_bias, ln_mix, w_in, hg_lower, hg_norm, w_up_a, w_up_b, w_out, ln_cross, ln_mem, wq_x, wk_x, wv_x, wo_x, ln_ffn, peer_query, peer_subkeys, peer_u, peer_v, ln_final):
    b, s, d = x.shape
    depth = w_in.shape[0]
    assert depth == 1, "the residual after PEER is fused into the final norm"
    assert s % MB_BLOCK == 0 and s % HG_CHUNK == 0 and s % (PEER_SLICES * SC_WORKERS * SC_GROUP) == 0
    nb = s // MB_BLOCK
    row = lambda a: a.reshape(1, -1).astype(F32)
    lb_all = jnp.cumsum(jax.nn.softmax(hg_lower.astype(F32), axis=0), axis=0)
    bias = moba_bias_tiles(rel_bias)
    n_hg = 4 * HG_WIDTH
    n_qk = 2 * MB_WIDTH
    n_mb = 3 * MB_WIDTH
    l = 0
    w = w_in[l].astype(BF16)
    w_hg, w_qk, w_vt, w_g = w[:, :n_hg], w[:, n_hg:n_hg + n_qk], w[:, n_hg + n_qk:n_hg + n_mb].T, w[:, n_hg + n_mb:]
    wa, wb, wo = w_up_a[l].astype(BF16), w_up_b[l].astype(BF16), w_out[l].astype(BF16)
    wqx, wox = wq_x[l].astype(BF16), wo_x[l].astype(BF16)
    wpq, sk = peer_query[l].astype(BF16), peer_subkeys[l].astype(F32)
    tab3 = lambda a: pack_bf16_pairs(a.astype(F32)).reshape(a.shape[0], SC_PK_SUB, SC_ROW_LANE)
    tab_u, tab_v = tab3(peer_u[l]), tab3(peer_v[l])
    kx, vx = mem_kv(mem, row(ln_mem[l]), wk_x[l].astype(BF16), wv_x[l].astype(BF16))

    outs = []
    for bi in range(b):
        x2d = x[bi]
        p0, pqk, vt, pg = in_proj(x2d, row(ln_mix[l]), w_hg, w_qk, w_vt, w_g)
        ya = hgrn2(p0, row(lb_all[l]), row(hg_norm[l]), 1, s)
        km = moba_kmean(pqk, 1, s).reshape(1, nb, MB_WIDTH)
        yb = moba_attention(pqk, vt, km, bias, 1, s)
        x2d = mix_out(x2d, ya, yb, pg, wa, wb, wo)
        x2d = cross_attn(x2d, row(ln_cross[l]), wqx, kx[bi:bi + 1], vx[bi:bi + 1], wox, s)
        ts = s // PEER_SLICES
        for tok0 in range(0, s, ts):
            hp, eidx, wts = peer_route(x2d, row(ln_ffn[l]), wpq, sk, tok0, ts)
            y = peer_experts_pk_sc(tab_u, tab_v, eidx.reshape(ts * PEER_SEL), wts.reshape(ts * PEER_SEL), hp, d)
            outs.append(final_norm(x2d, y, row(ln_final), tok0))
    return jnp.concatenate(outs, axis=0).reshape(b, s, d)
```

```python
import functools
import math

import jax
import jax.numpy as jnp
import numpy as np
from jax import lax
from jax.experimental import pallas as pl
from jax.experimental.pallas import tpu as pltpu
from jax.experimental.pallas import tpu_sc as plsc

F32 = jnp.float32
BF16 = jnp.bfloat16
I32 = jnp.int32
EPS = 1e-6
NEG_INF = float("-inf")

HG_HEADS = 4
HG_D = 128
HG_WIDTH = HG_HEADS * HG_D
HG_CHUNK = 64
HG_SUB = 16
MB_HEADS = 8
MB_DH = 64
MB_WIDTH = MB_HEADS * MB_DH
MB_BLOCK = 256
MB_TOPK = 3
MB_BIAS_TILES = 8
REL_BUCKETS = 32
REL_MAX_DIST = 2048
X_HEADS = 4
PEER_HEADS = 8
PEER_NKEYS = 128
PEER_TOPK = 16
PEER_HALF = 128
PEER_SEL = PEER_HEADS * PEER_TOPK
PEER_SLICES = 2

VMEM_LIMIT = 56 * 1024 * 1024


def _cparams(sem):
    return pltpu.CompilerParams(dimension_semantics=sem, vmem_limit_bytes=VMEM_LIMIT)


def _rms(x, g):
    ms = jnp.mean(x * x, axis=-1, keepdims=True)
    return x * lax.rsqrt(ms + EPS) * g


def _in_proj_kernel(x_ref, g_ref, w0_ref, w1_ref, wvt_ref, w2_ref, o0_ref, o1_ref, ovt_ref, o2_ref):
    h = _rms(x_ref[...], g_ref[...]).astype(BF16)
    o0_ref[...] = jnp.dot(h, w0_ref[...], preferred_element_type=F32)
    o1_ref[...] = jnp.dot(h, w1_ref[...], preferred_element_type=F32).astype(BF16)
    ovt_ref[0] = lax.dot_general(wvt_ref[...], h, (((1,), (1,)), ((), ())),
                                 preferred_element_type=F32).astype(BF16)
    o2_ref[...] = jnp.dot(h, w2_ref[...], preferred_element_type=F32).astype(BF16)


def in_proj(x2d, g, w0, w1, wvt, w2):
    t, d = x2d.shape
    tm = MB_BLOCK
    n0, n1, nv, n2 = w0.shape[1], w1.shape[1], wvt.shape[0], w2.shape[1]
    full = lambda a: pl.BlockSpec(a.shape, lambda i: (0, 0))
    return pl.pallas_call(
        _in_proj_kernel,
        grid=(t // tm,),
        in_specs=[pl.BlockSpec((tm, d), lambda i: (i, 0)), full(g), full(w0), full(w1), full(wvt), full(w2)],
        out_specs=[pl.BlockSpec((tm, n0), lambda i: (i, 0)),
                   pl.BlockSpec((tm, n1), lambda i: (i, 0)),
                   pl.BlockSpec((1, nv, tm), lambda i: (i, 0, 0)),
                   pl.BlockSpec((tm, n2), lambda i: (i, 0))],
        out_shape=[jax.ShapeDtypeStruct((t, n0), F32),
                   jax.ShapeDtypeStruct((t, n1), BF16),
                   jax.ShapeDtypeStruct((t // tm, nv, tm), BF16),
                   jax.ShapeDtypeStruct((t, n2), BF16)],
        compiler_params=_cparams(("parallel",)),
        name="in_proj",
    )(x2d, g, w0, w1, wvt, w2)


def _hgrn_kernel(q_ref, f_ref, i_ref, g_ref, lb_ref, gain_ref, o_ref, st_ref):
    c = pl.program_id(1)

    @pl.when(c == 0)
    def _():
        st_ref[...] = jnp.zeros_like(st_ref)

    C, S = HG_CHUNK, HG_SUB
    row = lax.broadcasted_iota(I32, (C, C), 0)
    col = lax.broadcasted_iota(I32, (C, C), 1)
    tril = (row >= col).astype(F32)
    t_iota = lax.broadcasted_iota(I32, (S, 1), 0)

    for h in range(HG_HEADS):
        sl = slice(h * HG_D, (h + 1) * HG_D)
        q = q_ref[:, sl]
        v = i_ref[:, sl]
        lb = lb_ref[:, sl]
        f = lb + (1.0 - lb) * jax.nn.sigmoid(f_ref[:, sl])
        lf = jnp.log(f)
        k = 1.0 - f
        b = jnp.dot(tril, lf, precision=lax.Precision.HIGHEST, preferred_element_type=F32)
        st = st_ref[h]
        vb = v.astype(BF16)
        qd = (q * jnp.exp(b)).astype(BF16)
        o_inter = lax.dot_general(qd, st.astype(BF16), (((1,), (1,)), ((), ())),
                                  preferred_element_type=F32)
        outs = []
        for i in range(C // S):
            r0 = i * S
            qi = q[r0:r0 + S]
            ki = k[r0:r0 + S]
            bi = b[r0:r0 + S]
            vi = v[r0:r0 + S]
            oi = o_inter[r0:r0 + S]
            if i > 0:
                bs = b[r0 - 1:r0]
                qh = (qi * jnp.exp(bi - bs)).astype(BF16)
                kh = (k[:r0] * jnp.exp(bs - b[:r0])).astype(BF16)
                a = lax.dot_general(qh, kh, (((1,), (1,)), ((), ())), preferred_element_type=F32)
                oi = oi + jnp.dot(a.astype(BF16), vb[:r0], preferred_element_type=F32)
            for s in range(S):
                dec = jnp.exp(jnp.minimum(bi - bi[s:s + 1], 0.0))
                p = qi * ki[s:s + 1] * dec
                a_s = jnp.sum(p, axis=-1, keepdims=True)
                a_s = jnp.where(t_iota >= s, a_s, 0.0)
                oi = oi + a_s * vi[s:s + 1]
            outs.append(oi)
        o = jnp.concatenate(outs, axis=0)
        b_end = b[C - 1:C]
        kd = (k * jnp.exp(b_end - b)).astype(BF16)
        upd = lax.dot_general(vb, kd, (((0,), (0,)), ((), ())), preferred_element_type=F32)
        st_ref[h] = st * jnp.exp(b_end) + upd
        o = o * lax.rsqrt(jnp.mean(o * o, axis=-1, keepdims=True) + EPS)
        g = g_ref[:, sl]
        o_ref[:, sl] = (o * gain_ref[:, sl] * (g * jax.nn.sigmoid(g))).astype(o_ref.dtype)


def hgrn2(p0, lb, gain, batch, seq):
    t = p0.shape[0]
    nc = seq // HG_CHUNK
    w = HG_WIDTH

    def col(j):
        return pl.BlockSpec((HG_CHUNK, w), lambda b, c, j=j: (b * nc + c, j))

    return pl.pallas_call(
        _hgrn_kernel,
        grid=(batch, nc),
        in_specs=[col(0), col(1), col(2), col(3),
                  pl.BlockSpec((1, w), lambda b, c: (0, 0)),
                  pl.BlockSpec((1, w), lambda b, c: (0, 0))],
        out_specs=pl.BlockSpec((HG_CHUNK, w), lambda b, c: (b * nc + c, 0)),
        out_shape=jax.ShapeDtypeStruct((t, w), BF16),
        scratch_shapes=[pltpu.VMEM((HG_HEADS, HG_D, HG_D), F32)],
        compiler_params=_cparams(("parallel", "arbitrary")),
        name="hgrn2",
    )(p0, p0, p0, p0, lb, gain)


def _kmean_kernel(k_ref, o_ref):
    o_ref[0] = jnp.mean(k_ref[...].astype(F32), axis=0, keepdims=True)


def moba_kmean(p1, batch, seq):
    nbt = p1.shape[0] // MB_BLOCK
    return pl.pallas_call(
        _kmean_kernel,
        grid=(nbt,),
        in_specs=[pl.BlockSpec((MB_BLOCK, MB_WIDTH), lambda i: (i, 1))],
        out_specs=pl.BlockSpec((1, 1, MB_WIDTH), lambda i: (i, 0, 0)),
        out_shape=jax.ShapeDtypeStruct((nbt, 1, MB_WIDTH), F32),
        compiler_params=_cparams(("parallel",)),
        name="moba_kmean",
    )(p1)


MB_PAIR = 4
MB_PW = MB_PAIR * MB_DH
MB_LG = 128


def _moba_kernel(q_ref, k_ref, vt_ref, km_ref, bias_ref, o_ref, *scratch):
    m_ref, l_ref, acc_ref, msk_ref = (scratch[i * MB_PAIR:(i + 1) * MB_PAIR] for i in range(4))
    qi = pl.program_id(2)
    nb = km_ref.shape[0]
    blk = MB_BLOCK
    heads = range(MB_PAIR)
    grp = lambda hh: slice((hh // 2) * MB_LG, (hh // 2 + 1) * MB_LG)
    q = q_ref[...]
    lane = lax.broadcasted_iota(I32, (blk, MB_LG), 1)
    in_head = [(lane < MB_DH) if hh % 2 == 0 else (lane >= MB_DH) for hh in heads]
    qs = q * jnp.asarray(MB_DH ** -0.5, BF16)
    qh = [jnp.where(in_head[hh], qs[:, grp(hh)], jnp.zeros((blk, MB_LG), BF16)) for hh in heads]
    nt = (((1,), (1,)), ((), ()))

    qf = q.astype(F32)
    n_io = lax.broadcasted_iota(I32, (nb, blk), 0)
    for hh in heads:
        gate = lax.dot_general(km_ref[:, grp(hh)], jnp.where(in_head[hh], qf[:, grp(hh)], 0.0), nt,
                               precision=lax.Precision.HIGHEST, preferred_element_type=F32)
        gate = jnp.where(n_io < qi, gate, NEG_INF)
        chosen = n_io < 0
        for _ in range(MB_TOPK):
            mx = jnp.max(gate, axis=0, keepdims=True)
            ix = jnp.min(jnp.where(gate == mx, n_io, nb), axis=0, keepdims=True)
            hit = n_io == ix
            chosen = chosen | (hit & (mx > NEG_INF))
            gate = jnp.where(hit, NEG_INF, gate)
        msk_ref[hh][...] = jnp.where(chosen, 0.0, NEG_INF)

    k_own = k_ref[pl.ds(pl.multiple_of(qi * blk, blk), blk), :]
    vt_own = vt_ref[qi]
    key_io = lax.broadcasted_iota(I32, (blk, blk), 0)
    qry_io = lax.broadcasted_iota(I32, (blk, blk), 1)
    own_rows = lambda r, hh: r[(hh % 2) * MB_DH:(hh % 2 + 1) * MB_DH]
    for hh in heads:
        s = lax.dot_general(k_own[:, grp(hh)], qh[hh], nt, preferred_element_type=F32) + bias_ref[hh, 0]
        s = jnp.where(key_io <= qry_io, s, NEG_INF)
        m0 = jnp.max(s, axis=0, keepdims=True)
        p = jnp.exp(s - m0)
        m_ref[hh][...] = m0
        l_ref[hh][...] = jnp.sum(p, axis=0, keepdims=True)
        r = jnp.dot(vt_own[grp(hh)], p.astype(BF16), preferred_element_type=F32)
        acc_ref[hh][...] = own_rows(r, hh)

    def past(n, carry):
        kn = k_ref[pl.ds(pl.multiple_of(n * blk, blk), blk), :]
        vtn = vt_ref[n]
        d = jnp.minimum(qi - n, MB_BIAS_TILES - 1)
        s = [lax.dot_general(kn[:, grp(hh)], qh[hh], nt, preferred_element_type=F32)
             + bias_ref[hh, d] + msk_ref[hh][pl.ds(n, 1), :] for hh in heads]
        m_old = [m_ref[hh][...] for hh in heads]
        l_old = [l_ref[hh][...] for hh in heads]
        a_old = [acc_ref[hh][...] for hh in heads]
        m_new = [jnp.maximum(m_old[hh], jnp.max(s[hh], axis=0, keepdims=True)) for hh in heads]
        alpha = [jnp.exp(m_old[hh] - m_new[hh]) for hh in heads]
        p = [jnp.exp(s[hh] - m_new[hh]) for hh in heads]
        r = [jnp.dot(vtn[grp(hh)], p[hh].astype(BF16), preferred_element_type=F32) for hh in heads]
        l_new = [alpha[hh] * l_old[hh] + jnp.sum(p[hh], axis=0, keepdims=True) for hh in heads]
        a_new = [alpha[hh] * a_old[hh] + own_rows(r[hh], hh) for hh in heads]
        for hh in heads:
            m_ref[hh][...] = m_new[hh]
            l_ref[hh][...] = l_new[hh]
            acc_ref[hh][...] = a_new[hh]
        return carry

    lax.fori_loop(0, qi, past, 0)
    out_t = jnp.concatenate([acc_ref[hh][...] / l_ref[hh][...] for hh in heads], axis=0)
    o_ref[...] = out_t.T.astype(o_ref.dtype)


def moba_attention(pqk, vt, km, bias, batch, seq):
    t = pqk.shape[0]
    nb = seq // MB_BLOCK
    groups = MB_WIDTH // MB_PW
    return pl.pallas_call(
        _moba_kernel,
        grid=(batch, groups, nb),
        in_specs=[
            pl.BlockSpec((MB_BLOCK, MB_PW), lambda b, j, i: (b * nb + i, j)),
            pl.BlockSpec((seq, MB_PW), lambda b, j, i: (b, groups + j)),
            pl.BlockSpec((nb, MB_PW, MB_BLOCK), lambda b, j, i: (b, j, 0)),
            pl.BlockSpec((None, nb, MB_PW), lambda b, j, i: (b, 0, j)),
            pl.BlockSpec((MB_PAIR, MB_BIAS_TILES, MB_BLOCK, MB_BLOCK), lambda b, j, i: (j, 0, 0, 0)),
        ],
        out_specs=pl.BlockSpec((MB_BLOCK, MB_PW), lambda b, j, i: (b * nb + i, j)),
        out_shape=jax.ShapeDtypeStruct((t, MB_WIDTH), BF16),
        scratch_shapes=(
            [pltpu.VMEM((1, MB_BLOCK), F32)] * (2 * MB_PAIR)
            + [pltpu.VMEM((MB_DH, MB_BLOCK), F32)] * MB_PAIR
            + [pltpu.VMEM((nb, MB_BLOCK), F32)] * MB_PAIR
        ),
        compiler_params=_cparams(("parallel", "parallel", "arbitrary")),
        name="moba_attn",
    )(pqk, pqk, vt, km, bias)


def _t5_bucket(dist):
    max_exact = REL_BUCKETS // 2
    scaled = jnp.log(jnp.maximum(dist, 1).astype(F32) / max_exact) / math.log(REL_MAX_DIST / max_exact)
    large = jnp.minimum(max_exact + (scaled * (REL_BUCKETS - max_exact)).astype(I32), REL_BUCKETS - 1)
    return jnp.where(dist < max_exact, dist, large)


def moba_bias_tiles(rel_bias):
    blk = MB_BLOCK
    span = 2 * blk - 1
    x = jnp.arange(span) - (blk - 1)
    dist = jnp.maximum(jnp.arange(MB_BIAS_TILES)[:, None] * blk + x[None, :], 0)
    w = rel_bias.astype(F32).T[:, _t5_bucket(dist)]
    h = w.shape[0]
    wp = jnp.pad(w, ((0, 0), (0, 0), (0, 1)))
    a = jnp.broadcast_to(wp[:, :, None, :], (h, MB_BIAS_TILES, blk, span + 1))
    a = a.reshape(h, MB_BIAS_TILES, blk * (span + 1))[:, :, :blk * span]
    return a.reshape(h, MB_BIAS_TILES, blk, span)[:, :, :, blk - 1:]


def _mix_kernel(x_ref, ya_ref, yb_ref, ga_ref, gb_ref, wa_ref, wb_ref, wo_ref, o_ref):
    za = jnp.dot(ya_ref[...], wa_ref[...], preferred_element_type=F32)
    zb = jnp.dot(yb_ref[...], wb_ref[...], preferred_element_type=F32)
    z = jax.nn.sigmoid(ga_ref[...].astype(F32)) * za + jax.nn.sigmoid(gb_ref[...].astype(F32)) * zb
    o_ref[...] = x_ref[...] + jnp.dot(z.astype(BF16), wo_ref[...], preferred_element_type=F32)


def mix_out(x2d, ya, yb, pg, wa, wb, wo, tm=256):
    t, d = x2d.shape
    w = ya.shape[1]
    return pl.pallas_call(
        _mix_kernel,
        grid=(t // tm,),
        in_specs=[
            pl.BlockSpec((tm, d), lambda i: (i, 0)),
            pl.BlockSpec((tm, w), lambda i: (i, 0)),
            pl.BlockSpec((tm, w), lambda i: (i, 0)),
            pl.BlockSpec((tm, d), lambda i: (i, 0)),
            pl.BlockSpec((tm, d), lambda i: (i, 1)),
            pl.BlockSpec((w, d), lambda i: (0, 0)),
            pl.BlockSpec((w, d), lambda i: (0, 0)),
            pl.BlockSpec((d, d), lambda i: (0, 0)),
        ],
        out_specs=pl.BlockSpec((tm, d), lambda i: (i, 0)),
        out_shape=jax.ShapeDtypeStruct((t, d), F32),
        compiler_params=_cparams(("parallel",)),
        name="mix_out",
    )(x2d, ya, yb, pg, pg, wa, wb, wo)


def _mem_kv_kernel(m_ref, g_ref, wk_ref, wv_ref, k_ref, v_ref):
    mn = _rms(m_ref[...], g_ref[...]).astype(BF16)
    k_ref[...] = jnp.dot(mn, wk_ref[...], preferred_element_type=F32).astype(BF16)
    v_ref[...] = jnp.dot(mn, wv_ref[...], preferred_element_type=F32).astype(BF16)


def mem_kv(mem, g, wk, wv):
    b, m, d = mem.shape
    spec = pl.BlockSpec((None, m, d), lambda i: (i, 0, 0))
    wspec = pl.BlockSpec((d, d), lambda i: (0, 0))
    return pl.pallas_call(
        _mem_kv_kernel,
        grid=(b,),
        in_specs=[spec, pl.BlockSpec((1, d), lambda i: (0, 0)), wspec, wspec],
        out_specs=[spec, spec],
        out_shape=[jax.ShapeDtypeStruct((b, m, d), BF16)] * 2,
        compiler_params=_cparams(("parallel",)),
        name="mem_kv",
    )(mem, g, wk, wv)


def _cross_kernel(x_ref, g_ref, wq_ref, k_ref, v_ref, wo_ref, o_ref):
    x = x_ref[...]
    d = x.shape[1]
    dh = d // X_HEADS
    h = _rms(x, g_ref[...]).astype(BF16)
    q = (jnp.dot(h, wq_ref[...], preferred_element_type=F32) * (dh ** -0.5)).astype(BF16)
    outs = []
    for hh in range(X_HEADS):
        sl = slice(hh * dh, (hh + 1) * dh)
        s = lax.dot_general(q[:, sl], k_ref[:, sl], (((1,), (1,)), ((), ())),
                            preferred_element_type=F32)
        p = jnp.exp(s - jnp.max(s, axis=1, keepdims=True))
        l = jnp.sum(p, axis=1, keepdims=True)
        o = jnp.dot(p.astype(BF16), v_ref[:, sl], preferred_element_type=F32) / l
        outs.append(o.astype(BF16))
    o = jnp.concatenate(outs, axis=1)
    o_ref[...] = x + jnp.dot(o, wo_ref[...], preferred_element_type=F32)


def cross_attn(x2d, g, wq, kx, vx, wo, seq, tm=256):
    t, d = x2d.shape
    m = kx.shape[1]
    per_b = seq // tm
    kv = pl.BlockSpec((None, m, d), lambda i: (i // per_b, 0, 0))
    wspec = pl.BlockSpec((d, d), lambda i: (0, 0))
    return pl.pallas_call(
        _cross_kernel,
        grid=(t // tm,),
        in_specs=[pl.BlockSpec((tm, d), lambda i: (i, 0)), pl.BlockSpec((1, d), lambda i: (0, 0)),
                  wspec, kv, kv, wspec],
        out_specs=pl.BlockSpec((tm, d), lambda i: (i, 0)),
        out_shape=jax.ShapeDtypeStruct((t, d), F32),
        compiler_params=_cparams(("parallel",)),
        name="cross_attn",
    )(x2d, g, wq, kx, vx, wo)


def _topk_rows(sc, k):
    n = sc.shape[0]
    io = lax.broadcasted_iota(I32, sc.shape, 0)
    vals, ids = [], []
    for _ in range(k):
        m = jnp.max(sc, axis=0, keepdims=True)
        ix = jnp.min(jnp.where(sc == m, io, n), axis=0, keepdims=True)
        vals.append(m)
        ids.append(ix)
        sc = jnp.where(io == ix, NEG_INF, sc)
    return jnp.concatenate(vals, axis=0), jnp.concatenate(ids, axis=0)


def _pack_bf16_halves(h):
    bits = lax.bitcast_convert_type(h, I32)
    r = bits + 0x7FFF + (lax.shift_right_logical(bits, 16) & 1)
    half = h.shape[1] // 2
    return lax.shift_right_logical(r[:, :half], 16) | (r[:, half:] & HI_MASK)


def _route_kernel(x_ref, g_ref, wq_ref, sk_ref, hp_ref, idx_ref, w_ref, hb_ref, it_ref, wt_ref):
    p = pl.program_id(1)

    @pl.when(p == 0)
    def _():
        h = _rms(x_ref[...], g_ref[...])
        hp_ref[...] = _pack_bf16_halves(h)
        hb_ref[...] = h.astype(BF16)

    qh = jnp.dot(hb_ref[...], wq_ref[...], preferred_element_type=F32)
    tops = []
    for c in range(2):
        seg = qh[:, c * PEER_HALF:(c + 1) * PEER_HALF]
        sc = lax.dot_general(sk_ref[c], seg, (((1,), (1,)), ((), ())),
                             precision=lax.Precision.HIGHEST, preferred_element_type=F32)
        tops.append(_topk_rows(sc, PEER_TOPK))
    (s0, i0), (s1, i1) = tops
    k = PEER_TOPK
    sub = 8
    tm = s0.shape[1]
    r8 = lax.broadcasted_iota(I32, (sub, tm), 0)
    r16 = lax.broadcasted_iota(I32, (k, tm), 0)
    cand_b = [s0[0:1] + s1, s0[1:2] + s1[:sub]]
    cidx_b = [i0[0:1] * PEER_NKEYS + i1, i0[1:2] * PEER_NKEYS + i1[:sub]]
    pos_b = [r16, k + r8]
    for a in range(2, sub):
        keep = r8 < (k // (a + 1))
        cand_b.append(jnp.where(keep, s0[a:a + 1] + s1[:sub], NEG_INF))
        cidx_b.append(i0[a:a + 1] * PEER_NKEYS + i1[:sub])
        pos_b.append(a * k + r8)
    cand_b.append(s0[sub:] + s1[0:1])
    cidx_b.append(i0[sub:] * PEER_NKEYS + i1[0:1])
    pos_b.append((sub + r8) * k)
    cand = jnp.concatenate(cand_b, axis=0)
    cidx = jnp.concatenate(cidx_b, axis=0)
    pos = jnp.concatenate(pos_b, axis=0)
    vals, ids = [], []
    for _ in range(k):
        m = jnp.max(cand, axis=0, keepdims=True)
        px = jnp.min(jnp.where(cand == m, pos, k * k), axis=0, keepdims=True)
        hit = pos == px
        vals.append(m)
        ids.append(jnp.sum(jnp.where(hit, cidx, 0), axis=0, keepdims=True))
        cand = jnp.where(hit, NEG_INF, cand)
    sf = jnp.concatenate(vals, axis=0)
    e = jnp.exp(sf - sf[0:1])
    rows = pl.ds(pl.multiple_of(p * PEER_TOPK, PEER_TOPK), PEER_TOPK)
    wt_ref[rows, :] = e / jnp.sum(e, axis=0, keepdims=True)
    it_ref[rows, :] = jnp.concatenate(ids, axis=0)

    @pl.when(p == pl.num_programs(1) - 1)
    def _():
        idx_ref[...] = it_ref[...].T
        w_ref[...] = wt_ref[...].T


def peer_route(x2d, g, wq, sk, tok0, t, tm=256):
    d = x2d.shape[1]
    ph = sk.shape[0]
    nsel = ph * PEER_TOPK
    blk0 = tok0 // tm
    return pl.pallas_call(
        _route_kernel,
        grid=(t // tm, ph),
        in_specs=[
            pl.BlockSpec((tm, d), lambda i, p: (blk0 + i, 0)),
            pl.BlockSpec((1, d), lambda i, p: (0, 0)),
            pl.BlockSpec((d, 2 * PEER_HALF), lambda i, p: (0, p)),
            pl.BlockSpec((None, 2, PEER_NKEYS, PEER_HALF), lambda i, p: (p, 0, 0, 0)),
        ],
        out_specs=[
            pl.BlockSpec((tm, d // 2), lambda i, p: (i, 0)),
            pl.BlockSpec((tm, nsel), lambda i, p: (i, 0)),
            pl.BlockSpec((tm, nsel), lambda i, p: (i, 0)),
        ],
        out_shape=[jax.ShapeDtypeStruct((t, d // 2), I32),
                   jax.ShapeDtypeStruct((t, nsel), I32),
                   jax.ShapeDtypeStruct((t, nsel), F32)],
        scratch_shapes=[pltpu.VMEM((tm, d), BF16),
                        pltpu.VMEM((nsel, tm), I32),
                        pltpu.VMEM((nsel, tm), F32)],
        compiler_params=_cparams(("parallel", "arbitrary")),
        name="peer_route",
    )(x2d, g, wq, sk)


def _coef_kernel(w_ref, a_ref, o_ref):
    o_ref[...] = w_ref[...] * jax.nn.gelu(a_ref[...])


def peer_coef(w, act, tm=1024):
    t, n = w.shape
    spec = pl.BlockSpec((tm, n), lambda i: (i, 0))
    return pl.pallas_call(
        _coef_kernel, grid=(t // tm,), in_specs=[spec, spec], out_specs=spec,
        out_shape=jax.ShapeDtypeStruct((t, n), F32),
        compiler_params=_cparams(("parallel",)), name="peer_coef",
    )(w, act)


def _final_kernel(x_ref, y_ref, g_ref, o_ref):
    o_ref[...] = _rms(x_ref[...] + y_ref[...], g_ref[...])


def final_norm(x2d, y, g, tok0, tm=512):
    t, d = y.shape
    blk0 = tok0 // tm
    spec = pl.BlockSpec((tm, d), lambda i: (i, 0))
    return pl.pallas_call(
        _final_kernel, grid=(t // tm,),
        in_specs=[pl.BlockSpec((tm, d), lambda i: (blk0 + i, 0)), spec, pl.BlockSpec((1, d), lambda i: (0, 0))],
        out_specs=spec,
        out_shape=jax.ShapeDtypeStruct((t, d), F32),
        compiler_params=_cparams(("parallel",)), name="final_norm",
    )(x2d, y, g)


SC_CORES = 2
SC_SUBCORES = 16
SC_WORKERS = SC_CORES * SC_SUBCORES
SC_LANES = 16
SC_GROUP = 16


def _sc_mesh():
    return plsc.VectorSubcoreMesh(core_axis_name="c", subcore_axis_name="s")


def _sc_params():
    return pltpu.CompilerParams(needs_layout_passes=False)


def _sc_worker_id():
    return lax.axis_index("s") * SC_CORES + lax.axis_index("c")


SC_RING = 4
SC_ROW_SUB = 8
SC_ROW_LANE = 128


def _sc_ring(n_units, start, wait, compute):
    for u in range(SC_RING - 1):
        start(u, u)

    @pl.loop(0, n_units, step=SC_RING)
    def _(uu):
        for b in range(SC_RING):
            u = uu + b
            nxt = u + (SC_RING - 1)

            @pl.when(nxt < n_units)
            def _():
                start(nxt, (b + SC_RING - 1) % SC_RING)

            wait(u, b)
            compute(u, b)


def _sc_unit_off(u):
    off = u * SC_LANES
    return off if isinstance(off, int) else pl.multiple_of(off, SC_LANES)


def _sc_row_piece(rows, r, c):
    per = SC_ROW_LANE // SC_LANES
    return rows[r, c // per, pl.ds(pl.multiple_of((c % per) * SC_LANES, SC_LANES), SC_LANES)]


def peer_dots_sc(table, idx_flat, h):
    t, d = h.shape
    nsel = PEER_SEL
    tpw = t // SC_WORKERS
    g = SC_GROUP
    groups = tpw // g
    heads = nsel // SC_LANES
    pieces = d // SC_LANES
    units = g * heads
    row_buf = pltpu.VMEM((SC_LANES, SC_ROW_SUB, SC_ROW_LANE), F32)

    @functools.partial(
        pl.kernel, mesh=_sc_mesh(),
        out_type=jax.ShapeDtypeStruct((t * nsel,), F32),
        scratch_types=[
            pltpu.VMEM((g * nsel,), I32),
            pltpu.VMEM((g, d), F32),
            pltpu.VMEM((g * nsel,), F32),
            pltpu.VMEM((SC_LANES * SC_LANES,), F32),
            [row_buf] * SC_RING,
            [pltpu.SemaphoreType.DMA] * SC_RING,
        ],
        compiler_params=_sc_params(),
        name="peer_dots_sc",
    )
    def k(tab_hbm, idx_hbm, h_hbm, out_hbm, idx_v, h_v, out_v, red_v, rows, sems):
        wid = _sc_worker_id()
        lane = lax.iota(I32, SC_LANES)

        def copy(u, slot):
            ids = idx_v.at[pl.ds(_sc_unit_off(u), SC_LANES)]
            return pltpu.make_async_copy(tab_hbm.at[ids], rows[slot], sems[slot])

        def compute(u, slot):
            tt = u // heads

            def body(c, accs):
                hv = h_v[tt, pl.ds(pl.multiple_of(c * SC_LANES, SC_LANES), SC_LANES)]
                return tuple(accs[r] + _sc_row_piece(rows[slot], r, c) * hv for r in range(SC_LANES))

            accs = lax.fori_loop(0, pieces, body,
                                 tuple(jnp.zeros((SC_LANES,), F32) for _ in range(SC_LANES)))
            for r in range(SC_LANES):
                red_v[pl.ds(r * SC_LANES, SC_LANES)] = accs[r]
            cols = [plsc.load_gather(red_v, [lane * SC_LANES + j]) for j in range(SC_LANES)]
            while len(cols) > 1:
                cols = [cols[i] + cols[i + 1] for i in range(0, len(cols), 2)]
            out_v[pl.ds(_sc_unit_off(u), SC_LANES)] = cols[0]

        @pl.loop(0, groups)
        def _(gi):
            base = wid * tpw + gi * g
            pltpu.sync_copy(idx_hbm.at[pl.ds(base * nsel, g * nsel)], idx_v)
            pltpu.sync_copy(h_hbm.at[pl.ds(base, g)], h_v)
            _sc_ring(units, lambda u, s: copy(u, s).start(), lambda u, s: copy(u, s).wait(), compute)
            pltpu.sync_copy(out_v, out_hbm.at[pl.ds(base * nsel, g * nsel)])

    return k(table, idx_flat, h)


def peer_combine_sc(table, idx_flat, coef_flat, t):
    d = table.shape[1] * table.shape[2]
    nsel = PEER_SEL
    tpw = t // SC_WORKERS
    g = SC_GROUP
    groups = tpw // g
    heads = nsel // SC_LANES
    pieces = d // SC_LANES
    units = g * heads
    row_buf = pltpu.VMEM((SC_LANES, SC_ROW_SUB, SC_ROW_LANE), F32)

    @functools.partial(
        pl.kernel, mesh=_sc_mesh(),
        out_type=jax.ShapeDtypeStruct((t, d), F32),
        scratch_types=[
            pltpu.VMEM((g * nsel,), I32),
            pltpu.VMEM((g * nsel,), F32),
            pltpu.VMEM((g, d), F32),
            [row_buf] * SC_RING,
            [pltpu.SemaphoreType.DMA] * SC_RING,
        ],
        compiler_params=_sc_params(),
        name="peer_combine_sc",
    )
    def k(tab_hbm, idx_hbm, coef_hbm, out_hbm, idx_v, coef_v, y_v, rows, sems):
        wid = _sc_worker_id()

        def copy(u, slot):
            ids = idx_v.at[pl.ds(_sc_unit_off(u), SC_LANES)]
            return pltpu.make_async_copy(tab_hbm.at[ids], rows[slot], sems[slot])

        def compute(u, slot):
            tt = u // heads
            first = (u % heads) == 0
            cs = [plsc.load_gather(coef_v, [jnp.full((SC_LANES,), u * SC_LANES + r, I32)])
                  for r in range(SC_LANES)]

            @plsc.parallel_loop(0, pieces, unroll=2)
            def _(c):
                off = pl.multiple_of(c * SC_LANES, SC_LANES)
                terms = [cs[r] * _sc_row_piece(rows[slot], r, c) for r in range(SC_LANES)]
                while len(terms) > 1:
                    terms = [terms[i] + terms[i + 1] for i in range(0, len(terms), 2)]
                prev = y_v[tt, pl.ds(off, SC_LANES)]
                y_v[tt, pl.ds(off, SC_LANES)] = terms[0] + jnp.where(first, 0.0, prev)

        @pl.loop(0, groups)
        def _(gi):
            base = wid * tpw + gi * g
            pltpu.sync_copy(idx_hbm.at[pl.ds(base * nsel, g * nsel)], idx_v)
            pltpu.sync_copy(coef_hbm.at[pl.ds(base * nsel, g * nsel)], coef_v)
            _sc_ring(units, lambda u, s: copy(u, s).start(), lambda u, s: copy(u, s).wait(), compute)
            pltpu.sync_copy(y_v, out_hbm.at[pl.ds(base, g)])

    return k(table, idx_flat, coef_flat)


GELU_C0 = math.sqrt(2.0 / math.pi)
GELU_C1 = 0.044715


def _gelu_tanh(x):
    z = GELU_C0 * (x + GELU_C1 * (x * x * x))
    th = 1.0 - 2.0 / (jnp.exp(2.0 * z) + 1.0)
    return 0.5 * x * (1.0 + th)


def peer_experts_sc(tab_u, tab_v, idx_flat, w_flat, h):
    t, d = h.shape
    nsel = PEER_SEL
    tpw = t // SC_WORKERS
    g = SC_GROUP
    groups = tpw // g
    heads = nsel // SC_LANES
    pieces = d // SC_LANES
    units = g * heads
    row_buf = pltpu.VMEM((SC_LANES, SC_ROW_SUB, SC_ROW_LANE), F32)

    @functools.partial(
        pl.kernel, mesh=_sc_mesh(),
        out_type=jax.ShapeDtypeStruct((t, d), F32),
        scratch_types=[
            pltpu.VMEM((g * nsel,), I32),
            pltpu.VMEM((g * nsel,), F32),
            pltpu.VMEM((g, d), F32),
            pltpu.VMEM((g, d), F32),
            pltpu.VMEM((SC_LANES * SC_LANES,), F32),
            [row_buf] * SC_RING,
            [pltpu.SemaphoreType.DMA] * SC_RING,
        ],
        compiler_params=_sc_params(),
        name="peer_experts_sc",
    )
    def k(u_hbm, v_hbm, idx_hbm, w_hbm, h_hbm, out_hbm, idx_v, coef_v, h_v, y_v, red_v, rows, sems):
        wid = _sc_worker_id()
        lane = lax.iota(I32, SC_LANES)

        def copy(tab_hbm, u, slot):
            ids = idx_v.at[pl.ds(_sc_unit_off(u), SC_LANES)]
            return pltpu.make_async_copy(tab_hbm.at[ids], rows[slot], sems[slot])

        def dots(u, slot):
            tt = u // heads

            def body(c, accs):
                hv = h_v[tt, pl.ds(pl.multiple_of(c * SC_LANES, SC_LANES), SC_LANES)]
                return tuple(accs[r] + _sc_row_piece(rows[slot], r, c) * hv for r in range(SC_LANES))

            accs = lax.fori_loop(0, pieces, body,
                                 tuple(jnp.zeros((SC_LANES,), F32) for _ in range(SC_LANES)))
            for r in range(SC_LANES):
                red_v[pl.ds(r * SC_LANES, SC_LANES)] = accs[r]
            cols = [plsc.load_gather(red_v, [lane * SC_LANES + j]) for j in range(SC_LANES)]
            while len(cols) > 1:
                cols = [cols[i] + cols[i + 1] for i in range(0, len(cols), 2)]
            sl = pl.ds(_sc_unit_off(u), SC_LANES)
            coef_v[sl] = coef_v[sl] * _gelu_tanh(cols[0])

        def combine(u, slot):
            tt = u // heads
            first = (u % heads) == 0
            cs = [plsc.load_gather(coef_v, [jnp.full((SC_LANES,), u * SC_LANES + r, I32)])
                  for r in range(SC_LANES)]

            @plsc.parallel_loop(0, pieces, unroll=2)
            def _(c):
                off = pl.multiple_of(c * SC_LANES, SC_LANES)
                terms = [cs[r] * _sc_row_piece(rows[slot], r, c) for r in range(SC_LANES)]
                while len(terms) > 1:
                    terms = [terms[i] + terms[i + 1] for i in range(0, len(terms), 2)]
                prev = y_v[tt, pl.ds(off, SC_LANES)]
                y_v[tt, pl.ds(off, SC_LANES)] = terms[0] + jnp.where(first, 0.0, prev)

        @pl.loop(0, groups)
        def _(gi):
            base = wid * tpw + gi * g
            pltpu.sync_copy(idx_hbm.at[pl.ds(base * nsel, g * nsel)], idx_v)
            pltpu.sync_copy(w_hbm.at[pl.ds(base * nsel, g * nsel)], coef_v)
            pltpu.sync_copy(h_hbm.at[pl.ds(base, g)], h_v)
            _sc_ring(units, lambda u, s: copy(u_hbm, u, s).start(), lambda u, s: copy(u_hbm, u, s).wait(), dots)
            _sc_ring(units, lambda u, s: copy(v_hbm, u, s).start(), lambda u, s: copy(v_hbm, u, s).wait(), combine)
            pltpu.sync_copy(y_v, out_hbm.at[pl.ds(base, g)])

    return k(tab_u, tab_v, idx_flat, w_flat, h)


SC_PK_RING = 8
SC_PK_SUB = 4
HI_MASK = -65536


def pack_bf16_pairs(a):
    half = a.shape[1] // 2
    bits = lax.bitcast_convert_type(a.astype(BF16), jnp.uint16).astype(jnp.uint32)
    return lax.bitcast_convert_type(bits[:, :half] | (bits[:, half:] << 16), I32)


def _unpack_halves(x32):
    w = plsc.bitcast(x32, I32)
    return plsc.bitcast(w << 16, F32), plsc.bitcast(w & HI_MASK, F32)


def _tree_sum(xs):
    while len(xs) > 1:
        xs = [xs[i] + xs[i + 1] for i in range(0, len(xs), 2)]
    return xs[0]


def peer_experts_pk_sc(tab_u, tab_v, idx_flat, w_flat, hp, d):
    t = hp.shape[0]
    nsel = PEER_SEL
    tpw = t // SC_WORKERS
    g = SC_GROUP
    groups = tpw // g
    heads = nsel // SC_LANES
    chunks = d // 32
    units = g * heads
    ring = SC_PK_RING
    row_buf = pltpu.VMEM((SC_LANES, SC_PK_SUB, SC_ROW_LANE), I32)

    def row_words(rows, r, wc):
        per = SC_ROW_LANE // SC_LANES
        return plsc.bitcast(rows[r, wc // per, pl.ds(pl.multiple_of((wc % per) * SC_LANES, SC_LANES), SC_LANES)],
                            BF16)

    def ring_loop(n_units, start, wait, compute):
        for u in range(ring - 1):
            start(u, u)

        @pl.loop(0, n_units, step=ring)
        def _(uu):
            for b in range(ring):
                u = uu + b
                nxt = u + (ring - 1)

                @pl.when(nxt < n_units)
                def _():
                    start(nxt, (b + ring - 1) % ring)

                wait(u, b)
                compute(u, b)

    @functools.partial(
        pl.kernel, mesh=_sc_mesh(),
        out_type=jax.ShapeDtypeStruct((t, d), F32),
        scratch_types=[
            pltpu.VMEM((g * nsel,), I32),
            pltpu.VMEM((g * nsel,), F32),
            pltpu.VMEM((g, d // 2), I32),
            pltpu.VMEM((g, d), F32),
            pltpu.VMEM((SC_LANES * SC_LANES,), F32),
            [row_buf] * ring,
            [pltpu.SemaphoreType.DMA] * ring,
        ],
        compiler_params=_sc_params(),
        name="peer_experts_pk_sc",
    )
    def k(u_hbm, v_hbm, idx_hbm, w_hbm, h_hbm, out_hbm, idx_v, coef_v, h_v, y_v, red_v, rows, sems):
        wid = _sc_worker_id()
        lane = lax.iota(I32, SC_LANES)

        def copy(tab_hbm, u, slot):
            ids = idx_v.at[pl.ds(_sc_unit_off(u), SC_LANES)]
            return pltpu.make_async_copy(tab_hbm.at[ids], rows[slot], sems[slot])

        def dots(u, slot):
            tt = u // heads

            def body(cp, accs):
                out = []
                hv = [plsc.bitcast(h_v[tt, pl.ds(pl.multiple_of((2 * cp + i) * SC_LANES, SC_LANES), SC_LANES)], BF16)
                      for i in range(2)]
                for r in range(SC_LANES):
                    pr = row_words(rows[slot], r, 2 * cp) * hv[0] + row_words(rows[slot], r, 2 * cp + 1) * hv[1]
                    lo, hi = _unpack_halves(pr)
                    out.append(accs[r] + lo + hi)
                return tuple(out)

            accs = lax.fori_loop(0, chunks // 2, body,
                                 tuple(jnp.zeros((SC_LANES,), F32) for _ in range(SC_LANES)))
            for r in range(SC_LANES):
                red_v[pl.ds(r * SC_LANES, SC_LANES)] = accs[r]
            act = _tree_sum([plsc.load_gather(red_v, [lane * SC_LANES + j]) for j in range(SC_LANES)])
            sl = pl.ds(_sc_unit_off(u), SC_LANES)
            coef_v[sl] = coef_v[sl] * _gelu_tanh(act)

        def combine(u, slot):
            tt = u // heads
            first = (u % heads) == 0
            cb = []
            for r in range(SC_LANES):
                c = plsc.load_gather(coef_v, [jnp.full((SC_LANES,), u * SC_LANES + r, I32)])
                cb.append(plsc.pack(c, c, format=plsc.PackFormat.INTERLEAVED))

            @plsc.parallel_loop(0, chunks, unroll=2)
            def _(wc):
                lo, hi = _unpack_halves(_tree_sum([cb[r] * row_words(rows[slot], r, wc) for r in range(SC_LANES)]))
                for half, val in ((0, lo), (1, hi)):
                    sl = pl.ds(pl.multiple_of(half * (d // 2) + wc * SC_LANES, SC_LANES), SC_LANES)
                    y_v[tt, sl] = val + jnp.where(first, 0.0, y_v[tt, sl])

        @pl.loop(0, groups)
        def _(gi):
            base = wid * tpw + gi * g
            pltpu.sync_copy(idx_hbm.at[pl.ds(base * nsel, g * nsel)], idx_v)
            pltpu.sync_copy(w_hbm.at[pl.ds(base * nsel, g * nsel)], coef_v)
            pltpu.sync_copy(h_hbm.at[pl.ds(base, g)], h_v)
            ring_loop(units, lambda u, s: copy(u_hbm, u, s).start(), lambda u, s: copy(u_hbm, u, s).wait(), dots)
            ring_loop(units, lambda u, s: copy(v_hbm, u, s).start(), lambda u, s: copy(v_hbm, u, s).wait(), combine)
            pltpu.sync_copy(y_v, out_hbm.at[pl.ds(base, g)])

    return k(tab_u, tab_v, idx_flat, w_flat, hp)


def kernel(x, mem, rel_bias, ln_mix, w_in, hg_lower, hg_norm, w_up_a, w_up_b, w_out, ln_cross, ln_mem, wq_x, wk_x, wv_x, wo_x, ln_ffn, peer_query, peer_subkeys, peer_u, peer_v, ln_final):
    b, s, d = x.shape
    depth = w_in.shape[0]
    assert depth == 1, "the residual after PEER is fused into the final norm"
    assert s % MB_BLOCK == 0 and s % HG_CHUNK == 0 and s % (PEER_SLICES * SC_WORKERS * SC_GROUP) == 0
    nb = s // MB_BLOCK
    row = lambda a: a.reshape(1, -1).astype(F32)
    lb_all = jnp.cumsum(jax.nn.softmax(hg_lower.astype(F32), axis=0), axis=0)
    bias = moba_bias_tiles(rel_bias)
    n_hg = 4 * HG_WIDTH
    n_qk = 2 * MB_WIDTH
    n_mb = 3 * MB_WIDTH
    l = 0
    w = w_in[l].astype(BF16)
    w_hg, w_qk, w_vt, w_g = w[:, :n_hg], w[:, n_hg:n_hg + n_qk], w[:, n_hg + n_qk:n_hg + n_mb].T, w[:, n_hg + n_mb:]
    wa, wb, wo = w_up_a[l].astype(BF16), w_up_b[l].astype(BF16), w_out[l].astype(BF16)
    wqx, wox = wq_x[l].astype(BF16), wo_x[l].astype(BF16)
    wpq, sk = peer_query[l].astype(BF16), peer_subkeys[l].astype(F32)
    tab3 = lambda a: pack_bf16_pairs(a.astype(F32)).reshape(a.shape[0], SC_PK_SUB, SC_ROW_LANE)
    tab_u, tab_v = tab3(peer_u[l]), tab3(peer_v[l])
    kx, vx = mem_kv(mem, row(ln_mem[l]), wk_x[l].astype(BF16), wv_x[l].astype(BF16))

    outs = []
    routed = None
    for bi in range(b):
        x2d = x[bi]
        if routed is not None:
            x2d, _ = lax.optimization_barrier((x2d, routed))
        p0, pqk, vt, pg = in_proj(x2d, row(ln_mix[l]), w_hg, w_qk, w_vt, w_g)
        ya = hgrn2(p0, row(lb_all[l]), row(hg_norm[l]), 1, s)
        km = moba_kmean(pqk, 1, s).reshape(1, nb, MB_WIDTH)
        yb = moba_attention(pqk, vt, km, bias, 1, s)
        x2d = mix_out(x2d, ya, yb, pg, wa, wb, wo)
        x2d = cross_attn(x2d, row(ln_cross[l]), wqx, kx[bi:bi + 1], vx[bi:bi + 1], wox, s)
        ts = s // PEER_SLICES
        for tok0 in range(0, s, ts):
            hp, eidx, wts = peer_route(x2d, row(ln_ffn[l]), wpq, sk, tok0, ts)
            y = peer_experts_pk_sc(tab_u, tab_v, eidx.reshape(ts * PEER_SEL), wts.reshape(ts * PEER_SEL), hp, d)
            outs.append(final_norm(x2d, y, row(ln_final), tok0))
            routed = eidx
    return jnp.concatenate(outs, axis=0).reshape(b, s, d)
```

```python
import functools
import math

import jax
import jax.numpy as jnp
import numpy as np
from jax import lax
from jax.experimental import pallas as pl
from jax.experimental.pallas import tpu as pltpu
from jax.experimental.pallas import tpu_sc as plsc

F32 = jnp.float32
BF16 = jnp.bfloat16
I32 = jnp.int32
EPS = 1e-6
NEG_INF = float("-inf")

HG_HEADS = 4
HG_D = 128
HG_WIDTH = HG_HEADS * HG_D
HG_CHUNK = 64
HG_SUB = 16
MB_HEADS = 8
MB_DH = 64
MB_WIDTH = MB_HEADS * MB_DH
MB_BLOCK = 256
MB_TOPK = 3
MB_BIAS_TILES = 8
REL_BUCKETS = 32
REL_MAX_DIST = 2048
X_HEADS = 4
PEER_HEADS = 8
PEER_NKEYS = 128
PEER_TOPK = 16
PEER_HALF = 128
PEER_SEL = PEER_HEADS * PEER_TOPK
PEER_SLICES = 2

VMEM_LIMIT = 56 * 1024 * 1024


def _cparams(sem):
    return pltpu.CompilerParams(dimension_semantics=sem, vmem_limit_bytes=VMEM_LIMIT)


def _rms(x, g):
    ms = jnp.mean(x * x, axis=-1, keepdims=True)
    return x * lax.rsqrt(ms + EPS) * g


def _in_proj_kernel(x_ref, g_ref, w0_ref, w1_ref, wvt_ref, w2_ref, o0_ref, o1_ref, ovt_ref, o2_ref):
    h = _rms(x_ref[...], g_ref[...]).astype(BF16)
    o0_ref[...] = jnp.dot(h, w0_ref[...], preferred_element_type=F32)
    o1_ref[...] = jnp.dot(h, w1_ref[...], preferred_element_type=F32).astype(BF16)
    ovt_ref[0] = lax.dot_general(wvt_ref[...], h, (((1,), (1,)), ((), ())),
                                 preferred_element_type=F32).astype(BF16)
    o2_ref[...] = jnp.dot(h, w2_ref[...], preferred_element_type=F32).astype(BF16)


def in_proj(x2d, g, w0, w1, wvt, w2):
    t, d = x2d.shape
    tm = MB_BLOCK
    n0, n1, nv, n2 = w0.shape[1], w1.shape[1], wvt.shape[0], w2.shape[1]
    full = lambda a: pl.BlockSpec(a.shape, lambda i: (0, 0))
    return pl.pallas_call(
        _in_proj_kernel,
        grid=(t // tm,),
        in_specs=[pl.BlockSpec((tm, d), lambda i: (i, 0)), full(g), full(w0), full(w1), full(wvt), full(w2)],
        out_specs=[pl.BlockSpec((tm, n0), lambda i: (i, 0)),
                   pl.BlockSpec((tm, n1), lambda i: (i, 0)),
                   pl.BlockSpec((1, nv, tm), lambda i: (i, 0, 0)),
                   pl.BlockSpec((tm, n2), lambda i: (i, 0))],
        out_shape=[jax.ShapeDtypeStruct((t, n0), F32),
                   jax.ShapeDtypeStruct((t, n1), BF16),
                   jax.ShapeDtypeStruct((t // tm, nv, tm), BF16),
                   jax.ShapeDtypeStruct((t, n2), BF16)],
        compiler_params=_cparams(("parallel",)),
        name="in_proj",
    )(x2d, g, w0, w1, wvt, w2)


def _hgrn_kernel(q_ref, f_ref, i_ref, g_ref, lb_ref, gain_ref, o_ref, st_ref):
    c = pl.program_id(1)

    @pl.when(c == 0)
    def _():
        st_ref[...] = jnp.zeros_like(st_ref)

    C, S = HG_CHUNK, HG_SUB
    row = lax.broadcasted_iota(I32, (C, C), 0)
    col = lax.broadcasted_iota(I32, (C, C), 1)
    tril = (row >= col).astype(F32)
    t_iota = lax.broadcasted_iota(I32, (S, 1), 0)

    for h in range(HG_HEADS):
        sl = slice(h * HG_D, (h + 1) * HG_D)
        q = q_ref[:, sl]
        v = i_ref[:, sl]
        lb = lb_ref[:, sl]
        f = lb + (1.0 - lb) * jax.nn.sigmoid(f_ref[:, sl])
        lf = jnp.log(f)
        k = 1.0 - f
        b = jnp.dot(tril, lf, precision=lax.Precision.HIGHEST, preferred_element_type=F32)
        st = st_ref[h]
        vb = v.astype(BF16)
        qd = (q * jnp.exp(b)).astype(BF16)
        o_inter = lax.dot_general(qd, st.astype(BF16), (((1,), (1,)), ((), ())),
                                  preferred_element_type=F32)
        outs = []
        for i in range(C // S):
            r0 = i * S
            qi = q[r0:r0 + S]
            ki = k[r0:r0 + S]
            bi = b[r0:r0 + S]
            vi = v[r0:r0 + S]
            oi = o_inter[r0:r0 + S]
            if i > 0:
                bs = b[r0 - 1:r0]
                qh = (qi * jnp.exp(bi - bs)).astype(BF16)
                kh = (k[:r0] * jnp.exp(bs - b[:r0])).astype(BF16)
                a = lax.dot_general(qh, kh, (((1,), (1,)), ((), ())), preferred_element_type=F32)
                oi = oi + jnp.dot(a.astype(BF16), vb[:r0], preferred_element_type=F32)
            for s in range(S):
                dec = jnp.exp(jnp.minimum(bi - bi[s:s + 1], 0.0))
                p = qi * ki[s:s + 1] * dec
                a_s = jnp.sum(p, axis=-1, keepdims=True)
                a_s = jnp.where(t_iota >= s, a_s, 0.0)
                oi = oi + a_s * vi[s:s + 1]
            outs.append(oi)
        o = jnp.concatenate(outs, axis=0)
        b_end = b[C - 1:C]
        kd = (k * jnp.exp(b_end - b)).astype(BF16)
        upd = lax.dot_general(vb, kd, (((0,), (0,)), ((), ())), preferred_element_type=F32)
        st_ref[h] = st * jnp.exp(b_end) + upd
        o = o * lax.rsqrt(jnp.mean(o * o, axis=-1, keepdims=True) + EPS)
        g = g_ref[:, sl]
        o_ref[:, sl] = (o * gain_ref[:, sl] * (g * jax.nn.sigmoid(g))).astype(o_ref.dtype)


def hgrn2(p0, lb, gain, batch, seq):
    t = p0.shape[0]
    nc = seq // HG_CHUNK
    w = HG_WIDTH

    def col(j):
        return pl.BlockSpec((HG_CHUNK, w), lambda b, c, j=j: (b * nc + c, j))

    return pl.pallas_call(
        _hgrn_kernel,
        grid=(batch, nc),
        in_specs=[col(0), col(1), col(2), col(3),
                  pl.BlockSpec((1, w), lambda b, c: (0, 0)),
                  pl.BlockSpec((1, w), lambda b, c: (0, 0))],
        out_specs=pl.BlockSpec((HG_CHUNK, w), lambda b, c: (b * nc + c, 0)),
        out_shape=jax.ShapeDtypeStruct((t, w), BF16),
        scratch_shapes=[pltpu.VMEM((HG_HEADS, HG_D, HG_D), F32)],
        compiler_params=_cparams(("parallel", "arbitrary")),
        name="hgrn2",
    )(p0, p0, p0, p0, lb, gain)


def _kmean_kernel(k_ref, o_ref):
    o_ref[0] = jnp.mean(k_ref[...].astype(F32), axis=0, keepdims=True)


def moba_kmean(p1, batch, seq):
    nbt = p1.shape[0] // MB_BLOCK
    return pl.pallas_call(
        _kmean_kernel,
        grid=(nbt,),
        in_specs=[pl.BlockSpec((MB_BLOCK, MB_WIDTH), lambda i: (i, 1))],
        out_specs=pl.BlockSpec((1, 1, MB_WIDTH), lambda i: (i, 0, 0)),
        out_shape=jax.ShapeDtypeStruct((nbt, 1, MB_WIDTH), F32),
        compiler_params=_cparams(("parallel",)),
        name="moba_kmean",
    )(p1)


MB_PAIR = 4
MB_PW = MB_PAIR * MB_DH
MB_LG = 128


def _moba_kernel(q_ref, k_ref, vt_ref, km_ref, bias_ref, o_ref, *scratch):
    m_ref, l_ref, al_ref, acc_ref, msk_ref, s_ref, p_ref = (
        scratch[i * MB_PAIR:(i + 1) * MB_PAIR] for i in range(7))
    qi = pl.program_id(2)
    nb = km_ref.shape[0]
    blk = MB_BLOCK
    heads = range(MB_PAIR)
    grp = lambda hh: slice((hh // 2) * MB_LG, (hh // 2 + 1) * MB_LG)
    q = q_ref[...]
    lane = lax.broadcasted_iota(I32, (blk, MB_LG), 1)
    in_head = [(lane < MB_DH) if hh % 2 == 0 else (lane >= MB_DH) for hh in heads]
    qs = q * jnp.asarray(MB_DH ** -0.5, BF16)
    qh = [jnp.where(in_head[hh], qs[:, grp(hh)], jnp.zeros((blk, MB_LG), BF16)) for hh in heads]
    nt = (((1,), (1,)), ((), ()))

    qf = q.astype(F32)
    n_io = lax.broadcasted_iota(I32, (nb, blk), 0)
    for hh in heads:
        gate = lax.dot_general(km_ref[:, grp(hh)], jnp.where(in_head[hh], qf[:, grp(hh)], 0.0), nt,
                               precision=lax.Precision.HIGHEST, preferred_element_type=F32)
        gate = jnp.where(n_io < qi, gate, NEG_INF)
        chosen = n_io < 0
        for _ in range(MB_TOPK):
            mx = jnp.max(gate, axis=0, keepdims=True)
            ix = jnp.min(jnp.where(gate == mx, n_io, nb), axis=0, keepdims=True)
            hit = n_io == ix
            chosen = chosen | (hit & (mx > NEG_INF))
            gate = jnp.where(hit, NEG_INF, gate)
        msk_ref[hh][...] = jnp.where(chosen, 0.0, NEG_INF)

    own_rows = lambda r, hh: r[(hh % 2) * MB_DH:(hh % 2 + 1) * MB_DH]

    def pv_stage(blk_idx):
        vtb = vt_ref[blk_idx]
        r = [jnp.dot(vtb[grp(hh)], p_ref[hh][...], preferred_element_type=F32) for hh in heads]
        return [al_ref[hh][...] * acc_ref[hh][...] + own_rows(r[hh], hh) for hh in heads]

    def softmax_stage():
        s = [s_ref[hh][...] for hh in heads]
        m_old = [m_ref[hh][...] for hh in heads]
        l_old = [l_ref[hh][...] for hh in heads]
        m_new = [jnp.maximum(m_old[hh], jnp.max(s[hh], axis=0, keepdims=True)) for hh in heads]
        alpha = [jnp.exp(m_old[hh] - m_new[hh]) for hh in heads]
        p = [jnp.exp(s[hh] - m_new[hh]) for hh in heads]
        l_new = [alpha[hh] * l_old[hh] + jnp.sum(p[hh], axis=0, keepdims=True) for hh in heads]
        return [x.astype(BF16) for x in p], alpha, m_new, l_new

    def store_softmax(p, alpha, m_new, l_new):
        for hh in heads:
            p_ref[hh][...] = p[hh]
            al_ref[hh][...] = alpha[hh]
            m_ref[hh][...] = m_new[hh]
            l_ref[hh][...] = l_new[hh]

    k_own = k_ref[pl.ds(pl.multiple_of(qi * blk, blk), blk), :]
    key_io = lax.broadcasted_iota(I32, (blk, blk), 0)
    qry_io = lax.broadcasted_iota(I32, (blk, blk), 1)
    for hh in heads:
        s = lax.dot_general(k_own[:, grp(hh)], qh[hh], nt, preferred_element_type=F32) + bias_ref[hh, 0]
        s_ref[hh][...] = jnp.where(key_io <= qry_io, s, NEG_INF)
        m_ref[hh][...] = jnp.full((1, blk), NEG_INF, F32)
        l_ref[hh][...] = jnp.zeros((1, blk), F32)
        al_ref[hh][...] = jnp.ones((1, blk), F32)
        acc_ref[hh][...] = jnp.zeros((MB_DH, blk), F32)
        p_ref[hh][...] = jnp.zeros((blk, blk), BF16)

    def step(i, carry):
        a_new = pv_stage(jnp.where(i <= 1, qi, i - 2))
        sm = softmax_stage()
        kn = k_ref[pl.ds(pl.multiple_of(i * blk, blk), blk), :]
        d = jnp.minimum(qi - i, MB_BIAS_TILES - 1)
        s_next = [lax.dot_general(kn[:, grp(hh)], qh[hh], nt, preferred_element_type=F32)
                  + bias_ref[hh, d] + msk_ref[hh][pl.ds(i, 1), :] for hh in heads]
        for hh in heads:
            acc_ref[hh][...] = a_new[hh]
            s_ref[hh][...] = s_next[hh]
        store_softmax(*sm)
        return carry

    lax.fori_loop(0, qi, step, 0)
    a_new = pv_stage(jnp.where(qi <= 1, qi, qi - 2))
    sm = softmax_stage()
    for hh in heads:
        acc_ref[hh][...] = a_new[hh]
    store_softmax(*sm)
    a_fin = pv_stage(jnp.where(qi == 0, qi, qi - 1))
    out_t = jnp.concatenate([a_fin[hh] / l_ref[hh][...] for hh in heads], axis=0)
    o_ref[...] = out_t.T.astype(o_ref.dtype)


def moba_attention(pqk, vt, km, bias, batch, seq):
    t = pqk.shape[0]
    nb = seq // MB_BLOCK
    groups = MB_WIDTH // MB_PW
    return pl.pallas_call(
        _moba_kernel,
        grid=(batch, groups, nb),
        in_specs=[
            pl.BlockSpec((MB_BLOCK, MB_PW), lambda b, j, i: (b * nb + i, j)),
            pl.BlockSpec((seq, MB_PW), lambda b, j, i: (b, groups + j)),
            pl.BlockSpec((nb, MB_PW, MB_BLOCK), lambda b, j, i: (b, j, 0)),
            pl.BlockSpec((None, nb, MB_PW), lambda b, j, i: (b, 0, j)),
            pl.BlockSpec((MB_PAIR, MB_BIAS_TILES, MB_BLOCK, MB_BLOCK), lambda b, j, i: (j, 0, 0, 0)),
        ],
        out_specs=pl.BlockSpec((MB_BLOCK, MB_PW), lambda b, j, i: (b * nb + i, j)),
        out_shape=jax.ShapeDtypeStruct((t, MB_WIDTH), BF16),
        scratch_shapes=(
            [pltpu.VMEM((1, MB_BLOCK), F32)] * (3 * MB_PAIR)
            + [pltpu.VMEM((MB_DH, MB_BLOCK), F32)] * MB_PAIR
            + [pltpu.VMEM((nb, MB_BLOCK), F32)] * MB_PAIR
            + [pltpu.VMEM((MB_BLOCK, MB_BLOCK), F32)] * MB_PAIR
            + [pltpu.VMEM((MB_BLOCK, MB_BLOCK), BF16)] * MB_PAIR
        ),
        compiler_params=_cparams(("parallel", "parallel", "arbitrary")),
        name="moba_attn",
    )(pqk, pqk, vt, km, bias)


def _t5_bucket(dist):
    max_exact = REL_BUCKETS // 2
    scaled = jnp.log(jnp.maximum(dist, 1).astype(F32) / max_exact) / math.log(REL_MAX_DIST / max_exact)
    large = jnp.minimum(max_exact + (scaled * (REL_BUCKETS - max_exact)).astype(I32), REL_BUCKETS - 1)
    return jnp.where(dist < max_exact, dist, large)


def moba_bias_tiles(rel_bias):
    blk = MB_BLOCK
    span = 2 * blk - 1
    x = jnp.arange(span) - (blk - 1)
    dist = jnp.maximum(jnp.arange(MB_BIAS_TILES)[:, None] * blk + x[None, :], 0)
    w = rel_bias.astype(F32).T[:, _t5_bucket(dist)]
    h = w.shape[0]
    wp = jnp.pad(w, ((0, 0), (0, 0), (0, 1)))
    a = jnp.broadcast_to(wp[:, :, None, :], (h, MB_BIAS_TILES, blk, span + 1))
    a = a.reshape(h, MB_BIAS_TILES, blk * (span + 1))[:, :, :blk * span]
    return a.reshape(h, MB_BIAS_TILES, blk, span)[:, :, :, blk - 1:]


def _mix_kernel(x_ref, ya_ref, yb_ref, ga_ref, gb_ref, wa_ref, wb_ref, wo_ref, o_ref):
    za = jnp.dot(ya_ref[...], wa_ref[...], preferred_element_type=F32)
    zb = jnp.dot(yb_ref[...], wb_ref[...], preferred_element_type=F32)
    z = jax.nn.sigmoid(ga_ref[...].astype(F32)) * za + jax.nn.sigmoid(gb_ref[...].astype(F32)) * zb
    o_ref[...] = x_ref[...] + jnp.dot(z.astype(BF16), wo_ref[...], preferred_element_type=F32)


def mix_out(x2d, ya, yb, pg, wa, wb, wo, tm=256):
    t, d = x2d.shape
    w = ya.shape[1]
    return pl.pallas_call(
        _mix_kernel,
        grid=(t // tm,),
        in_specs=[
            pl.BlockSpec((tm, d), lambda i: (i, 0)),
            pl.BlockSpec((tm, w), lambda i: (i, 0)),
            pl.BlockSpec((tm, w), lambda i: (i, 0)),
            pl.BlockSpec((tm, d), lambda i: (i, 0)),
            pl.BlockSpec((tm, d), lambda i: (i, 1)),
            pl.BlockSpec((w, d), lambda i: (0, 0)),
            pl.BlockSpec((w, d), lambda i: (0, 0)),
            pl.BlockSpec((d, d), lambda i: (0, 0)),
        ],
        out_specs=pl.BlockSpec((tm, d), lambda i: (i, 0)),
        out_shape=jax.ShapeDtypeStruct((t, d), F32),
        compiler_params=_cparams(("parallel",)),
        name="mix_out",
    )(x2d, ya, yb, pg, pg, wa, wb, wo)


def _mem_kv_kernel(m_ref, g_ref, wk_ref, wv_ref, k_ref, v_ref):
    mn = _rms(m_ref[...], g_ref[...]).astype(BF16)
    k_ref[...] = jnp.dot(mn, wk_ref[...], preferred_element_type=F32).astype(BF16)
    v_ref[...] = jnp.dot(mn, wv_ref[...], preferred_element_type=F32).astype(BF16)


def mem_kv(mem, g, wk, wv):
    b, m, d = mem.shape
    spec = pl.BlockSpec((None, m, d), lambda i: (i, 0, 0))
    wspec = pl.BlockSpec((d, d), lambda i: (0, 0))
    return pl.pallas_call(
        _mem_kv_kernel,
        grid=(b,),
        in_specs=[spec, pl.BlockSpec((1, d), lambda i: (0, 0)), wspec, wspec],
        out_specs=[spec, spec],
        out_shape=[jax.ShapeDtypeStruct((b, m, d), BF16)] * 2,
        compiler_params=_cparams(("parallel",)),
        name="mem_kv",
    )(mem, g, wk, wv)


def _cross_kernel(x_ref, g_ref, wq_ref, k_ref, v_ref, wo_ref, o_ref):
    x = x_ref[...]
    d = x.shape[1]
    dh = d // X_HEADS
    h = _rms(x, g_ref[...]).astype(BF16)
    q = (jnp.dot(h, wq_ref[...], preferred_element_type=F32) * (dh ** -0.5)).astype(BF16)
    outs = []
    for hh in range(X_HEADS):
        sl = slice(hh * dh, (hh + 1) * dh)
        s = lax.dot_general(q[:, sl], k_ref[:, sl], (((1,), (1,)), ((), ())),
                            preferred_element_type=F32)
        p = jnp.exp(s - jnp.max(s, axis=1, keepdims=True))
        l = jnp.sum(p, axis=1, keepdims=True)
        o = jnp.dot(p.astype(BF16), v_ref[:, sl], preferred_element_type=F32) / l
        outs.append(o.astype(BF16))
    o = jnp.concatenate(outs, axis=1)
    o_ref[...] = x + jnp.dot(o, wo_ref[...], preferred_element_type=F32)


def cross_attn(x2d, g, wq, kx, vx, wo, seq, tm=256):
    t, d = x2d.shape
    m = kx.shape[1]
    per_b = seq // tm
    kv = pl.BlockSpec((None, m, d), lambda i: (i // per_b, 0, 0))
    wspec = pl.BlockSpec((d, d), lambda i: (0, 0))
    return pl.pallas_call(
        _cross_kernel,
        grid=(t // tm,),
        in_specs=[pl.BlockSpec((tm, d), lambda i: (i, 0)), pl.BlockSpec((1, d), lambda i: (0, 0)),
                  wspec, kv, kv, wspec],
        out_specs=pl.BlockSpec((tm, d), lambda i: (i, 0)),
        out_shape=jax.ShapeDtypeStruct((t, d), F32),
        compiler_params=_cparams(("parallel",)),
        name="cross_attn",
    )(x2d, g, wq, kx, vx, wo)


def _topk_rows(sc, k):
    n = sc.shape[0]
    io = lax.broadcasted_iota(I32, sc.shape, 0)
    vals, ids = [], []
    for _ in range(k):
        m = jnp.max(sc, axis=0, keepdims=True)
        ix = jnp.min(jnp.where(sc == m, io, n), axis=0, keepdims=True)
        vals.append(m)
        ids.append(ix)
        sc = jnp.where(io == ix, NEG_INF, sc)
    return jnp.concatenate(vals, axis=0), jnp.concatenate(ids, axis=0)


def _pack_bf16_halves(h):
    bits = lax.bitcast_convert_type(h, I32)
    r = bits + 0x7FFF + (lax.shift_right_logical(bits, 16) & 1)
    half = h.shape[1] // 2
    return lax.shift_right_logical(r[:, :half], 16) | (r[:, half:] & HI_MASK)


def _route_kernel(x_ref, g_ref, wq_ref, sk_ref, hp_ref, idx_ref, w_ref, hb_ref, it_ref, wt_ref):
    p = pl.program_id(1)

    @pl.when(p == 0)
    def _():
        h = _rms(x_ref[...], g_ref[...])
        hp_ref[...] = _pack_bf16_halves(h)
        hb_ref[...] = h.astype(BF16)

    qh = jnp.dot(hb_ref[...], wq_ref[...], preferred_element_type=F32)
    tops = []
    for c in range(2):
        seg = qh[:, c * PEER_HALF:(c + 1) * PEER_HALF]
        sc = lax.dot_general(sk_ref[c], seg, (((1,), (1,)), ((), ())),
                             precision=lax.Precision.HIGHEST, preferred_element_type=F32)
        tops.append(_topk_rows(sc, PEER_TOPK))
    (s0, i0), (s1, i1) = tops
    k = PEER_TOPK
    sub = 8
    tm = s0.shape[1]
    r8 = lax.broadcasted_iota(I32, (sub, tm), 0)
    r16 = lax.broadcasted_iota(I32, (k, tm), 0)
    cand_b = [s0[0:1] + s1, s0[1:2] + s1[:sub]]
    cidx_b = [i0[0:1] * PEER_NKEYS + i1, i0[1:2] * PEER_NKEYS + i1[:sub]]
    pos_b = [r16, k + r8]
    for a in range(2, sub):
        keep = r8 < (k // (a + 1))
        cand_b.append(jnp.where(keep, s0[a:a + 1] + s1[:sub], NEG_INF))
        cidx_b.append(i0[a:a + 1] * PEER_NKEYS + i1[:sub])
        pos_b.append(a * k + r8)
    cand_b.append(s0[sub:] + s1[0:1])
    cidx_b.append(i0[sub:] * PEER_NKEYS + i1[0:1])
    pos_b.append((sub + r8) * k)
    cand = jnp.concatenate(cand_b, axis=0)
    cidx = jnp.concatenate(cidx_b, axis=0)
    pos = jnp.concatenate(pos_b, axis=0)
    vals, ids = [], []
    for _ in range(k):
        m = jnp.max(cand, axis=0, keepdims=True)
        px = jnp.min(jnp.where(cand == m, pos, k * k), axis=0, keepdims=True)
        hit = pos == px
        vals.append(m)
        ids.append(jnp.sum(jnp.where(hit, cidx, 0), axis=0, keepdims=True))
        cand = jnp.where(hit, NEG_INF, cand)
    sf = jnp.concatenate(vals, axis=0)
    e = jnp.exp(sf - sf[0:1])
    rows = pl.ds(pl.multiple_of(p * PEER_TOPK, PEER_TOPK), PEER_TOPK)
    wt_ref[rows, :] = e / jnp.sum(e, axis=0, keepdims=True)
    it_ref[rows, :] = jnp.concatenate(ids, axis=0)

    @pl.when(p == pl.num_programs(1) - 1)
    def _():
        idx_ref[...] = it_ref[...].T
        w_ref[...] = wt_ref[...].T


def peer_route(x2d, g, wq, sk, tok0, t, tm=256):
    d = x2d.shape[1]
    ph = sk.shape[0]
    nsel = ph * PEER_TOPK
    blk0 = tok0 // tm
    return pl.pallas_call(
        _route_kernel,
        grid=(t // tm, ph),
        in_specs=[
            pl.BlockSpec((tm, d), lambda i, p: (blk0 + i, 0)),
            pl.BlockSpec((1, d), lambda i, p: (0, 0)),
            pl.BlockSpec((d, 2 * PEER_HALF), lambda i, p: (0, p)),
            pl.BlockSpec((None, 2, PEER_NKEYS, PEER_HALF), lambda i, p: (p, 0, 0, 0)),
        ],
        out_specs=[
            pl.BlockSpec((tm, d // 2), lambda i, p: (i, 0)),
            pl.BlockSpec((tm, nsel), lambda i, p: (i, 0)),
            pl.BlockSpec((tm, nsel), lambda i, p: (i, 0)),
        ],
        out_shape=[jax.ShapeDtypeStruct((t, d // 2), I32),
                   jax.ShapeDtypeStruct((t, nsel), I32),
                   jax.ShapeDtypeStruct((t, nsel), F32)],
        scratch_shapes=[pltpu.VMEM((tm, d), BF16),
                        pltpu.VMEM((nsel, tm), I32),
                        pltpu.VMEM((nsel, tm), F32)],
        compiler_params=_cparams(("parallel", "arbitrary")),
        name="peer_route",
    )(x2d, g, wq, sk)


def _coef_kernel(w_ref, a_ref, o_ref):
    o_ref[...] = w_ref[...] * jax.nn.gelu(a_ref[...])


def peer_coef(w, act, tm=1024):
    t, n = w.shape
    spec = pl.BlockSpec((tm, n), lambda i: (i, 0))
    return pl.pallas_call(
        _coef_kernel, grid=(t // tm,), in_specs=[spec, spec], out_specs=spec,
        out_shape=jax.ShapeDtypeStruct((t, n), F32),
        compiler_params=_cparams(("parallel",)), name="peer_coef",
    )(w, act)


def _final_kernel(x_ref, y_ref, g_ref, o_ref):
    o_ref[...] = _rms(x_ref[...] + y_ref[...], g_ref[...])


def final_norm(x2d, y, g, tok0, tm=512):
    t, d = y.shape
    blk0 = tok0 // tm
    spec = pl.BlockSpec((tm, d), lambda i: (i, 0))
    return pl.pallas_call(
        _final_kernel, grid=(t // tm,),
        in_specs=[pl.BlockSpec((tm, d), lambda i: (blk0 + i, 0)), spec, pl.BlockSpec((1, d), lambda i: (0, 0))],
        out_specs=spec,
        out_shape=jax.ShapeDtypeStruct((t, d), F32),
        compiler_params=_cparams(("parallel",)), name="final_norm",
    )(x2d, y, g)


SC_CORES = 2
SC_SUBCORES = 16
SC_WORKERS = SC_CORES * SC_SUBCORES
SC_LANES = 16
SC_GROUP = 16


def _sc_mesh():
    return plsc.VectorSubcoreMesh(core_axis_name="c", subcore_axis_name="s")


def _sc_params():
    return pltpu.CompilerParams(needs_layout_passes=False)


def _sc_worker_id():
    return lax.axis_index("s") * SC_CORES + lax.axis_index("c")


SC_RING = 4
SC_ROW_SUB = 8
SC_ROW_LANE = 128


def _sc_ring(n_units, start, wait, compute):
    for u in range(SC_RING - 1):
        start(u, u)

    @pl.loop(0, n_units, step=SC_RING)
    def _(uu):
        for b in range(SC_RING):
            u = uu + b
            nxt = u + (SC_RING - 1)

            @pl.when(nxt < n_units)
            def _():
                start(nxt, (b + SC_RING - 1) % SC_RING)

            wait(u, b)
            compute(u, b)


def _sc_unit_off(u):
    off = u * SC_LANES
    return off if isinstance(off, int) else pl.multiple_of(off, SC_LANES)


def _sc_row_piece(rows, r, c):
    per = SC_ROW_LANE // SC_LANES
    return rows[r, c // per, pl.ds(pl.multiple_of((c % per) * SC_LANES, SC_LANES), SC_LANES)]


def peer_dots_sc(table, idx_flat, h):
    t, d = h.shape
    nsel = PEER_SEL
    tpw = t // SC_WORKERS
    g = SC_GROUP
    groups = tpw // g
    heads = nsel // SC_LANES
    pieces = d // SC_LANES
    units = g * heads
    row_buf = pltpu.VMEM((SC_LANES, SC_ROW_SUB, SC_ROW_LANE), F32)

    @functools.partial(
        pl.kernel, mesh=_sc_mesh(),
        out_type=jax.ShapeDtypeStruct((t * nsel,), F32),
        scratch_types=[
            pltpu.VMEM((g * nsel,), I32),
            pltpu.VMEM((g, d), F32),
            pltpu.VMEM((g * nsel,), F32),
            pltpu.VMEM((SC_LANES * SC_LANES,), F32),
            [row_buf] * SC_RING,
            [pltpu.SemaphoreType.DMA] * SC_RING,
        ],
        compiler_params=_sc_params(),
        name="peer_dots_sc",
    )
    def k(tab_hbm, idx_hbm, h_hbm, out_hbm, idx_v, h_v, out_v, red_v, rows, sems):
        wid = _sc_worker_id()
        lane = lax.iota(I32, SC_LANES)

        def copy(u, slot):
            ids = idx_v.at[pl.ds(_sc_unit_off(u), SC_LANES)]
            return pltpu.make_async_copy(tab_hbm.at[ids], rows[slot], sems[slot])

        def compute(u, slot):
            tt = u // heads

            def body(c, accs):
                hv = h_v[tt, pl.ds(pl.multiple_of(c * SC_LANES, SC_LANES), SC_LANES)]
                return tuple(accs[r] + _sc_row_piece(rows[slot], r, c) * hv for r in range(SC_LANES))

            accs = lax.fori_loop(0, pieces, body,
                                 tuple(jnp.zeros((SC_LANES,), F32) for _ in range(SC_LANES)))
            for r in range(SC_LANES):
                red_v[pl.ds(r * SC_LANES, SC_LANES)] = accs[r]
            cols = [plsc.load_gather(red_v, [lane * SC_LANES + j]) for j in range(SC_LANES)]
            while len(cols) > 1:
                cols = [cols[i] + cols[i + 1] for i in range(0, len(cols), 2)]
            out_v[pl.ds(_sc_unit_off(u), SC_LANES)] = cols[0]

        @pl.loop(0, groups)
        def _(gi):
            base = wid * tpw + gi * g
            pltpu.sync_copy(idx_hbm.at[pl.ds(base * nsel, g * nsel)], idx_v)
            pltpu.sync_copy(h_hbm.at[pl.ds(base, g)], h_v)
            _sc_ring(units, lambda u, s: copy(u, s).start(), lambda u, s: copy(u, s).wait(), compute)
            pltpu.sync_copy(out_v, out_hbm.at[pl.ds(base * nsel, g * nsel)])

    return k(table, idx_flat, h)


def peer_combine_sc(table, idx_flat, coef_flat, t):
    d = table.shape[1] * table.shape[2]
    nsel = PEER_SEL
    tpw = t // SC_WORKERS
    g = SC_GROUP
    groups = tpw // g
    heads = nsel // SC_LANES
    pieces = d // SC_LANES
    units = g * heads
    row_buf = pltpu.VMEM((SC_LANES, SC_ROW_SUB, SC_ROW_LANE), F32)

    @functools.partial(
        pl.kernel, mesh=_sc_mesh(),
        out_type=jax.ShapeDtypeStruct((t, d), F32),
        scratch_types=[
            pltpu.VMEM((g * nsel,), I32),
            pltpu.VMEM((g * nsel,), F32),
            pltpu.VMEM((g, d), F32),
            [row_buf] * SC_RING,
            [pltpu.SemaphoreType.DMA] * SC_RING,
        ],
        compiler_params=_sc_params(),
        name="peer_combine_sc",
    )
    def k(tab_hbm, idx_hbm, coef_hbm, out_hbm, idx_v, coef_v, y_v, rows, sems):
        wid = _sc_worker_id()

        def copy(u, slot):
            ids = idx_v.at[pl.ds(_sc_unit_off(u), SC_LANES)]
            return pltpu.make_async_copy(tab_hbm.at[ids], rows[slot], sems[slot])

        def compute(u, slot):
            tt = u // heads
            first = (u % heads) == 0
            cs = [plsc.load_gather(coef_v, [jnp.full((SC_LANES,), u * SC_LANES + r, I32)])
                  for r in range(SC_LANES)]

            @plsc.parallel_loop(0, pieces, unroll=2)
            def _(c):
                off = pl.multiple_of(c * SC_LANES, SC_LANES)
                terms = [cs[r] * _sc_row_piece(rows[slot], r, c) for r in range(SC_LANES)]
                while len(terms) > 1:
                    terms = [terms[i] + terms[i + 1] for i in range(0, len(terms), 2)]
                prev = y_v[tt, pl.ds(off, SC_LANES)]
                y_v[tt, pl.ds(off, SC_LANES)] = terms[0] + jnp.where(first, 0.0, prev)

        @pl.loop(0, groups)
        def _(gi):
            base = wid * tpw + gi * g
            pltpu.sync_copy(idx_hbm.at[pl.ds(base * nsel, g * nsel)], idx_v)
            pltpu.sync_copy(coef_hbm.at[pl.ds(base * nsel, g * nsel)], coef_v)
            _sc_ring(units, lambda u, s: copy(u, s).start(), lambda u, s: copy(u, s).wait(), compute)
            pltpu.sync_copy(y_v, out_hbm.at[pl.ds(base, g)])

    return k(table, idx_flat, coef_flat)


GELU_C0 = math.sqrt(2.0 / math.pi)
GELU_C1 = 0.044715


def _gelu_tanh(x):
    z = GELU_C0 * (x + GELU_C1 * (x * x * x))
    th = 1.0 - 2.0 / (jnp.exp(2.0 * z) + 1.0)
    return 0.5 * x * (1.0 + th)


def peer_experts_sc(tab_u, tab_v, idx_flat, w_flat, h):
    t, d = h.shape
    nsel = PEER_SEL
    tpw = t // SC_WORKERS
    g = SC_GROUP
    groups = tpw // g
    heads = nsel // SC_LANES
    pieces = d // SC_LANES
    units = g * heads
    row_buf = pltpu.VMEM((SC_LANES, SC_ROW_SUB, SC_ROW_LANE), F32)

    @functools.partial(
        pl.kernel, mesh=_sc_mesh(),
        out_type=jax.ShapeDtypeStruct((t, d), F32),
        scratch_types=[
            pltpu.VMEM((g * nsel,), I32),
            pltpu.VMEM((g * nsel,), F32),
            pltpu.VMEM((g, d), F32),
            pltpu.VMEM((g, d), F32),
            pltpu.VMEM((SC_LANES * SC_LANES,), F32),
            [row_buf] * SC_RING,
            [pltpu.SemaphoreType.DMA] * SC_RING,
        ],
        compiler_params=_sc_params(),
        name="peer_experts_sc",
    )
    def k(u_hbm, v_hbm, idx_hbm, w_hbm, h_hbm, out_hbm, idx_v, coef_v, h_v, y_v, red_v, rows, sems):
        wid = _sc_worker_id()
        lane = lax.iota(I32, SC_LANES)

        def copy(tab_hbm, u, slot):
            ids = idx_v.at[pl.ds(_sc_unit_off(u), SC_LANES)]
            return pltpu.make_async_copy(tab_hbm.at[ids], rows[slot], sems[slot])

        def dots(u, slot):
            tt = u // heads

            def body(c, accs):
                hv = h_v[tt, pl.ds(pl.multiple_of(c * SC_LANES, SC_LANES), SC_LANES)]
                return tuple(accs[r] + _sc_row_piece(rows[slot], r, c) * hv for r in range(SC_LANES))

            accs = lax.fori_loop(0, pieces, body,
                                 tuple(jnp.zeros((SC_LANES,), F32) for _ in range(SC_LANES)))
            for r in range(SC_LANES):
                red_v[pl.ds(r * SC_LANES, SC_LANES)] = accs[r]
            cols = [plsc.load_gather(red_v, [lane * SC_LANES + j]) for j in range(SC_LANES)]
            while len(cols) > 1:
                cols = [cols[i] + cols[i + 1] for i in range(0, len(cols), 2)]
            sl = pl.ds(_sc_unit_off(u), SC_LANES)
            coef_v[sl] = coef_v[sl] * _gelu_tanh(cols[0])

        def combine(u, slot):
            tt = u // heads
            first = (u % heads) == 0
            cs = [plsc.load_gather(coef_v, [jnp.full((SC_LANES,), u * SC_LANES + r, I32)])
                  for r in range(SC_LANES)]

            @plsc.parallel_loop(0, pieces, unroll=2)
            def _(c):
                off = pl.multiple_of(c * SC_LANES, SC_LANES)
                terms = [cs[r] * _sc_row_piece(rows[slot], r, c) for r in range(SC_LANES)]
                while len(terms) > 1:
                    terms = [terms[i] + terms[i + 1] for i in range(0, len(terms), 2)]
                prev = y_v[tt, pl.ds(off, SC_LANES)]
                y_v[tt, pl.ds(off, SC_LANES)] = terms[0] + jnp.where(first, 0.0, prev)

        @pl.loop(0, groups)
        def _(gi):
            base = wid * tpw + gi * g
            pltpu.sync_copy(idx_hbm.at[pl.ds(base * nsel, g * nsel)], idx_v)
            pltpu.sync_copy(w_hbm.at[pl.ds(base * nsel, g * nsel)], coef_v)
            pltpu.sync_copy(h_hbm.at[pl.ds(base, g)], h_v)
            _sc_ring(units, lambda u, s: copy(u_hbm, u, s).start(), lambda u, s: copy(u_hbm, u, s).wait(), dots)
            _sc_ring(units, lambda u, s: copy(v_hbm, u, s).start(), lambda u, s: copy(v_hbm, u, s).wait(), combine)
            pltpu.sync_copy(y_v, out_hbm.at[pl.ds(base, g)])

    return k(tab_u, tab_v, idx_flat, w_flat, h)


SC_PK_RING = 8
SC_PK_SUB = 4
HI_MASK = -65536


def pack_bf16_pairs(a):
    half = a.shape[1] // 2
    bits = lax.bitcast_convert_type(a.astype(BF16), jnp.uint16).astype(jnp.uint32)
    return lax.bitcast_convert_type(bits[:, :half] | (bits[:, half:] << 16), I32)


def _unpack_halves(x32):
    w = plsc.bitcast(x32, I32)
    return plsc.bitcast(w << 16, F32), plsc.bitcast(w & HI_MASK, F32)


def _tree_sum(xs):
    while len(xs) > 1:
        xs = [xs[i] + xs[i + 1] for i in range(0, len(xs), 2)]
    return xs[0]


def peer_experts_pk_sc(tab_u, tab_v, idx_flat, w_flat, hp, d):
    t = hp.shape[0]
    nsel = PEER_SEL
    tpw = t // SC_WORKERS
    g = SC_GROUP
    groups = tpw // g
    heads = nsel // SC_LANES
    chunks = d // 32
    units = g * heads
    ring = SC_PK_RING
    row_buf = pltpu.VMEM((SC_LANES, SC_PK_SUB, SC_ROW_LANE), I32)

    def row_words(rows, r, wc):
        per = SC_ROW_LANE // SC_LANES
        return plsc.bitcast(rows[r, wc // per, pl.ds(pl.multiple_of((wc % per) * SC_LANES, SC_LANES), SC_LANES)],
                            BF16)

    def ring_loop(n_units, start, wait, compute):
        for u in range(ring - 1):
            start(u, u)

        @pl.loop(0, n_units, step=ring)
        def _(uu):
            for b in range(ring):
                u = uu + b
                nxt = u + (ring - 1)

                @pl.when(nxt < n_units)
                def _():
                    start(nxt, (b + ring - 1) % ring)

                wait(u, b)
                compute(u, b)

    @functools.partial(
        pl.kernel, mesh=_sc_mesh(),
        out_type=jax.ShapeDtypeStruct((t, d), F32),
        scratch_types=[
            pltpu.VMEM((g * nsel,), I32),
            pltpu.VMEM((g * nsel,), F32),
            pltpu.VMEM((g, d // 2), I32),
            pltpu.VMEM((g, d), F32),
            pltpu.VMEM((SC_LANES * SC_LANES,), F32),
            [row_buf] * ring,
            [pltpu.SemaphoreType.DMA] * ring,
        ],
        compiler_params=_sc_params(),
        name="peer_experts_pk_sc",
    )
    def k(u_hbm, v_hbm, idx_hbm, w_hbm, h_hbm, out_hbm, idx_v, coef_v, h_v, y_v, red_v, rows, sems):
        wid = _sc_worker_id()
        lane = lax.iota(I32, SC_LANES)

        def copy(tab_hbm, u, slot):
            ids = idx_v.at[pl.ds(_sc_unit_off(u), SC_LANES)]
            return pltpu.make_async_copy(tab_hbm.at[ids], rows[slot], sems[slot])

        def dots(u, slot):
            tt = u // heads

            def body(cp, accs):
                out = []
                hv = [plsc.bitcast(h_v[tt, pl.ds(pl.multiple_of((2 * cp + i) * SC_LANES, SC_LANES), SC_LANES)], BF16)
                      for i in range(2)]
                for r in range(SC_LANES):
                    pr = row_words(rows[slot], r, 2 * cp) * hv[0] + row_words(rows[slot], r, 2 * cp + 1) * hv[1]
                    lo, hi = _unpack_halves(pr)
                    out.append(accs[r] + lo + hi)
                return tuple(out)

            accs = lax.fori_loop(0, chunks // 2, body,
                                 tuple(jnp.zeros((SC_LANES,), F32) for _ in range(SC_LANES)))
            for r in range(SC_LANES):
                red_v[pl.ds(r * SC_LANES, SC_LANES)] = accs[r]
            act = _tree_sum([plsc.load_gather(red_v, [lane * SC_LANES + j]) for j in range(SC_LANES)])
            sl = pl.ds(_sc_unit_off(u), SC_LANES)
            coef_v[sl] = coef_v[sl] * _gelu_tanh(act)

        def combine(u, slot):
            tt = u // heads
            first = (u % heads) == 0
            cb = []
            for r in range(SC_LANES):
                c = plsc.load_gather(coef_v, [jnp.full((SC_LANES,), u * SC_LANES + r, I32)])
                cb.append(plsc.pack(c, c, format=plsc.PackFormat.INTERLEAVED))

            @plsc.parallel_loop(0, chunks, unroll=2)
            def _(wc):
                lo, hi = _unpack_halves(_tree_sum([cb[r] * row_words(rows[slot], r, wc) for r in range(SC_LANES)]))
                for half, val in ((0, lo), (1, hi)):
                    sl = pl.ds(pl.multiple_of(half * (d // 2) + wc * SC_LANES, SC_LANES), SC_LANES)
                    y_v[tt, sl] = val + jnp.where(first, 0.0, y_v[tt, sl])

        @pl.loop(0, groups)
        def _(gi):
            base = wid * tpw + gi * g
            pltpu.sync_copy(idx_hbm.at[pl.ds(base * nsel, g * nsel)], idx_v)
            pltpu.sync_copy(w_hbm.at[pl.ds(base * nsel, g * nsel)], coef_v)
            pltpu.sync_copy(h_hbm.at[pl.ds(base, g)], h_v)
            ring_loop(units, lambda u, s: copy(u_hbm, u, s).start(), lambda u, s: copy(u_hbm, u, s).wait(), dots)
            ring_loop(units, lambda u, s: copy(v_hbm, u, s).start(), lambda u, s: copy(v_hbm, u, s).wait(), combine)
            pltpu.sync_copy(y_v, out_hbm.at[pl.ds(base, g)])

    return k(tab_u, tab_v, idx_flat, w_flat, hp)


def kernel(x, mem, rel_bias, ln_mix, w_in, hg_lower, hg_norm, w_up_a, w_up_b, w_out, ln_cross, ln_mem, wq_x, wk_x, wv_x, wo_x, ln_ffn, peer_query, peer_subkeys, peer_u, peer_v, ln_final):
    b, s, d = x.shape
    depth = w_in.shape[0]
    assert depth == 1, "the residual after PEER is fused into the final norm"
    assert s % MB_BLOCK == 0 and s % HG_CHUNK == 0 and s % (PEER_SLICES * SC_WORKERS * SC_GROUP) == 0
    nb = s // MB_BLOCK
    row = lambda a: a.reshape(1, -1).astype(F32)
    lb_all = jnp.cumsum(jax.nn.softmax(hg_lower.astype(F32), axis=0), axis=0)
    bias = moba_bias_tiles(rel_bias)
    n_hg = 4 * HG_WIDTH
    n_qk = 2 * MB_WIDTH
    n_mb = 3 * MB_WIDTH
    l = 0
    w = w_in[l].astype(BF16)
    w_hg, w_qk, w_vt, w_g = w[:, :n_hg], w[:, n_hg:n_hg + n_qk], w[:, n_hg + n_qk:n_hg + n_mb].T, w[:, n_hg + n_mb:]
    wa, wb, wo = w_up_a[l].astype(BF16), w_up_b[l].astype(BF16), w_out[l].astype(BF16)
    wqx, wox = wq_x[l].astype(BF16), wo_x[l].astype(BF16)
    wpq, sk = peer_query[l].astype(BF16), peer_subkeys[l].astype(F32)
    tab3 = lambda a: pack_bf16_pairs(a.astype(F32)).reshape(a.shape[0], SC_PK_SUB, SC_ROW_LANE)
    tab_u, tab_v = tab3(peer_u[l]), tab3(peer_v[l])
    kx, vx = mem_kv(mem, row(ln_mem[l]), wk_x[l].astype(BF16), wv_x[l].astype(BF16))

    outs = []
    for bi in range(b):
        x2d = x[bi]
        p0, pqk, vt, pg = in_proj(x2d, row(ln_mix[l]), w_hg, w_qk, w_vt, w_g)
        ya = hgrn2(p0, row(lb_all[l]), row(hg_norm[l]), 1, s)
        km = moba_kmean(pqk, 1, s).reshape(1, nb, MB_WIDTH)
        yb = moba_attention(pqk, vt, km, bias, 1, s)
        x2d = mix_out(x2d, ya, yb, pg, wa, wb, wo)
        x2d = cross_attn(x2d, row(ln_cross[l]), wqx, kx[bi:bi + 1], vx[bi:bi + 1], wox, s)
        ts = s // PEER_SLICES
        for tok0 in range(0, s, ts):
            hp, eidx, wts = peer_route(x2d, row(ln_ffn[l]), wpq, sk, tok0, ts)
            y = peer_experts_pk_sc(tab_u, tab_v, eidx.reshape(ts * PEER_SEL), wts.reshape(ts * PEER_SEL), hp, d)
            outs.append(final_norm(x2d, y, row(ln_final), tok0))
    return jnp.concatenate(outs, axis=0).reshape(b, s, d)
```

```python
import functools
import math

import jax
import jax.numpy as jnp
import numpy as np
from jax import lax
from jax.experimental import pallas as pl
from jax.experimental.pallas import tpu as pltpu
from jax.experimental.pallas import tpu_sc as plsc

F32 = jnp.float32
BF16 = jnp.bfloat16
I32 = jnp.int32
EPS = 1e-6
NEG_INF = float("-inf")

HG_HEADS = 4
HG_D = 128
HG_WIDTH = HG_HEADS * HG_D
HG_CHUNK = 64
HG_SUB = 16
MB_HEADS = 8
MB_DH = 64
MB_WIDTH = MB_HEADS * MB_DH
MB_BLOCK = 256
MB_TOPK = 3
MB_BIAS_TILES = 8
REL_BUCKETS = 32
REL_MAX_DIST = 2048
X_HEADS = 4
PEER_HEADS = 8
PEER_NKEYS = 128
PEER_TOPK = 16
PEER_HALF = 128
PEER_SEL = PEER_HEADS * PEER_TOPK
PEER_SLICES = 4

VMEM_LIMIT = 56 * 1024 * 1024


def _cparams(sem):
    return pltpu.CompilerParams(dimension_semantics=sem, vmem_limit_bytes=VMEM_LIMIT)


def _rms(x, g):
    ms = jnp.mean(x * x, axis=-1, keepdims=True)
    return x * lax.rsqrt(ms + EPS) * g


def _in_proj_kernel(x_ref, g_ref, w0_ref, w1_ref, wvt_ref, w2_ref, o0_ref, o1_ref, ovt_ref, o2_ref):
    h = _rms(x_ref[...], g_ref[...]).astype(BF16)
    o0_ref[...] = jnp.dot(h, w0_ref[...], preferred_element_type=F32)
    o1_ref[...] = jnp.dot(h, w1_ref[...], preferred_element_type=F32).astype(BF16)
    ovt_ref[0] = lax.dot_general(wvt_ref[...], h, (((1,), (1,)), ((), ())),
                                 preferred_element_type=F32).astype(BF16)
    o2_ref[...] = jnp.dot(h, w2_ref[...], preferred_element_type=F32).astype(BF16)


def in_proj(x2d, g, w0, w1, wvt, w2):
    t, d = x2d.shape
    tm = MB_BLOCK
    n0, n1, nv, n2 = w0.shape[1], w1.shape[1], wvt.shape[0], w2.shape[1]
    full = lambda a: pl.BlockSpec(a.shape, lambda i: (0, 0))
    return pl.pallas_call(
        _in_proj_kernel,
        grid=(t // tm,),
        in_specs=[pl.BlockSpec((tm, d), lambda i: (i, 0)), full(g), full(w0), full(w1), full(wvt), full(w2)],
        out_specs=[pl.BlockSpec((tm, n0), lambda i: (i, 0)),
                   pl.BlockSpec((tm, n1), lambda i: (i, 0)),
                   pl.BlockSpec((1, nv, tm), lambda i: (i, 0, 0)),
                   pl.BlockSpec((tm, n2), lambda i: (i, 0))],
        out_shape=[jax.ShapeDtypeStruct((t, n0), F32),
                   jax.ShapeDtypeStruct((t, n1), BF16),
                   jax.ShapeDtypeStruct((t // tm, nv, tm), BF16),
                   jax.ShapeDtypeStruct((t, n2), BF16)],
        compiler_params=_cparams(("parallel",)),
        name="in_proj",
    )(x2d, g, w0, w1, wvt, w2)


def _hgrn_kernel(q_ref, f_ref, i_ref, g_ref, lb_ref, gain_ref, o_ref, st_ref):
    c = pl.program_id(1)

    @pl.when(c == 0)
    def _():
        st_ref[...] = jnp.zeros_like(st_ref)

    C, S = HG_CHUNK, HG_SUB
    row = lax.broadcasted_iota(I32, (C, C), 0)
    col = lax.broadcasted_iota(I32, (C, C), 1)
    tril = (row >= col).astype(F32)
    t_iota = lax.broadcasted_iota(I32, (S, 1), 0)

    for h in range(HG_HEADS):
        sl = slice(h * HG_D, (h + 1) * HG_D)
        q = q_ref[:, sl]
        v = i_ref[:, sl]
        lb = lb_ref[:, sl]
        f = lb + (1.0 - lb) * jax.nn.sigmoid(f_ref[:, sl])
        lf = jnp.log(f)
        k = 1.0 - f
        b = jnp.dot(tril, lf, precision=lax.Precision.HIGHEST, preferred_element_type=F32)
        st = st_ref[h]
        vb = v.astype(BF16)
        qd = (q * jnp.exp(b)).astype(BF16)
        o_inter = lax.dot_general(qd, st.astype(BF16), (((1,), (1,)), ((), ())),
                                  preferred_element_type=F32)
        outs = []
        for i in range(C // S):
            r0 = i * S
            qi = q[r0:r0 + S]
            ki = k[r0:r0 + S]
            bi = b[r0:r0 + S]
            vi = v[r0:r0 + S]
            oi = o_inter[r0:r0 + S]
            if i > 0:
                bs = b[r0 - 1:r0]
                qh = (qi * jnp.exp(bi - bs)).astype(BF16)
                kh = (k[:r0] * jnp.exp(bs - b[:r0])).astype(BF16)
                a = lax.dot_general(qh, kh, (((1,), (1,)), ((), ())), preferred_element_type=F32)
                oi = oi + jnp.dot(a.astype(BF16), vb[:r0], preferred_element_type=F32)
            for s in range(S):
                dec = jnp.exp(jnp.minimum(bi - bi[s:s + 1], 0.0))
                p = qi * ki[s:s + 1] * dec
                a_s = jnp.sum(p, axis=-1, keepdims=True)
                a_s = jnp.where(t_iota >= s, a_s, 0.0)
                oi = oi + a_s * vi[s:s + 1]
            outs.append(oi)
        o = jnp.concatenate(outs, axis=0)
        b_end = b[C - 1:C]
        kd = (k * jnp.exp(b_end - b)).astype(BF16)
        upd = lax.dot_general(vb, kd, (((0,), (0,)), ((), ())), preferred_element_type=F32)
        st_ref[h] = st * jnp.exp(b_end) + upd
        o = o * lax.rsqrt(jnp.mean(o * o, axis=-1, keepdims=True) + EPS)
        g = g_ref[:, sl]
        o_ref[:, sl] = (o * gain_ref[:, sl] * (g * jax.nn.sigmoid(g))).astype(o_ref.dtype)


def hgrn2(p0, lb, gain, batch, seq):
    t = p0.shape[0]
    nc = seq // HG_CHUNK
    w = HG_WIDTH

    def col(j):
        return pl.BlockSpec((HG_CHUNK, w), lambda b, c, j=j: (b * nc + c, j))

    return pl.pallas_call(
        _hgrn_kernel,
        grid=(batch, nc),
        in_specs=[col(0), col(1), col(2), col(3),
                  pl.BlockSpec((1, w), lambda b, c: (0, 0)),
                  pl.BlockSpec((1, w), lambda b, c: (0, 0))],
        out_specs=pl.BlockSpec((HG_CHUNK, w), lambda b, c: (b * nc + c, 0)),
        out_shape=jax.ShapeDtypeStruct((t, w), BF16),
        scratch_shapes=[pltpu.VMEM((HG_HEADS, HG_D, HG_D), F32)],
        compiler_params=_cparams(("parallel", "arbitrary")),
        name="hgrn2",
    )(p0, p0, p0, p0, lb, gain)


def _kmean_kernel(k_ref, o_ref):
    o_ref[0] = jnp.mean(k_ref[...].astype(F32), axis=0, keepdims=True)


def moba_kmean(p1, batch, seq):
    nbt = p1.shape[0] // MB_BLOCK
    return pl.pallas_call(
        _kmean_kernel,
        grid=(nbt,),
        in_specs=[pl.BlockSpec((MB_BLOCK, MB_WIDTH), lambda i: (i, 1))],
        out_specs=pl.BlockSpec((1, 1, MB_WIDTH), lambda i: (i, 0, 0)),
        out_shape=jax.ShapeDtypeStruct((nbt, 1, MB_WIDTH), F32),
        compiler_params=_cparams(("parallel",)),
        name="moba_kmean",
    )(p1)


MB_PAIR = 4
MB_PW = MB_PAIR * MB_DH
MB_LG = 128


def _moba_kernel(q_ref, k_ref, vt_ref, km_ref, bias_ref, o_ref, *scratch):
    m_ref, l_ref, al_ref, acc_ref, msk_ref, s_ref, p_ref = (
        scratch[i * MB_PAIR:(i + 1) * MB_PAIR] for i in range(7))
    qi = pl.program_id(2)
    nb = km_ref.shape[0]
    blk = MB_BLOCK
    heads = range(MB_PAIR)
    grp = lambda hh: slice((hh // 2) * MB_LG, (hh // 2 + 1) * MB_LG)
    q = q_ref[...]
    lane = lax.broadcasted_iota(I32, (blk, MB_LG), 1)
    in_head = [(lane < MB_DH) if hh % 2 == 0 else (lane >= MB_DH) for hh in heads]
    qs = q * jnp.asarray(MB_DH ** -0.5, BF16)
    qh = [jnp.where(in_head[hh], qs[:, grp(hh)], jnp.zeros((blk, MB_LG), BF16)) for hh in heads]
    nt = (((1,), (1,)), ((), ()))

    qf = q.astype(F32)
    n_io = lax.broadcasted_iota(I32, (nb, blk), 0)
    for hh in heads:
        gate = lax.dot_general(km_ref[:, grp(hh)], jnp.where(in_head[hh], qf[:, grp(hh)], 0.0), nt,
                               precision=lax.Precision.HIGHEST, preferred_element_type=F32)
        gate = jnp.where(n_io < qi, gate, NEG_INF)
        chosen = n_io < 0
        for _ in range(MB_TOPK):
            mx = jnp.max(gate, axis=0, keepdims=True)
            ix = jnp.min(jnp.where(gate == mx, n_io, nb), axis=0, keepdims=True)
            hit = n_io == ix
            chosen = chosen | (hit & (mx > NEG_INF))
            gate = jnp.where(hit, NEG_INF, gate)
        msk_ref[hh][...] = jnp.where(chosen, 0.0, NEG_INF)

    own_rows = lambda r, hh: r[(hh % 2) * MB_DH:(hh % 2 + 1) * MB_DH]

    def pv_stage(blk_idx):
        vtb = vt_ref[blk_idx]
        r = [jnp.dot(vtb[grp(hh)], p_ref[hh][...], preferred_element_type=F32) for hh in heads]
        return [al_ref[hh][...] * acc_ref[hh][...] + own_rows(r[hh], hh) for hh in heads]

    def softmax_stage():
        s = [s_ref[hh][...] for hh in heads]
        m_old = [m_ref[hh][...] for hh in heads]
        l_old = [l_ref[hh][...] for hh in heads]
        m_new = [jnp.maximum(m_old[hh], jnp.max(s[hh], axis=0, keepdims=True)) for hh in heads]
        alpha = [jnp.exp(m_old[hh] - m_new[hh]) for hh in heads]
        p = [jnp.exp(s[hh] - m_new[hh]) for hh in heads]
        l_new = [alpha[hh] * l_old[hh] + jnp.sum(p[hh], axis=0, keepdims=True) for hh in heads]
        return [x.astype(BF16) for x in p], alpha, m_new, l_new

    def store_softmax(p, alpha, m_new, l_new):
        for hh in heads:
            p_ref[hh][...] = p[hh]
            al_ref[hh][...] = alpha[hh]
            m_ref[hh][...] = m_new[hh]
            l_ref[hh][...] = l_new[hh]

    k_own = k_ref[pl.ds(pl.multiple_of(qi * blk, blk), blk), :]
    key_io = lax.broadcasted_iota(I32, (blk, blk), 0)
    qry_io = lax.broadcasted_iota(I32, (blk, blk), 1)
    for hh in heads:
        s = lax.dot_general(k_own[:, grp(hh)], qh[hh], nt, preferred_element_type=F32) + bias_ref[hh, 0]
        s_ref[hh][...] = jnp.where(key_io <= qry_io, s, NEG_INF)
        m_ref[hh][...] = jnp.full((1, blk), NEG_INF, F32)
        l_ref[hh][...] = jnp.zeros((1, blk), F32)
        al_ref[hh][...] = jnp.ones((1, blk), F32)
        acc_ref[hh][...] = jnp.zeros((MB_DH, blk), F32)
        p_ref[hh][...] = jnp.zeros((blk, blk), BF16)

    def step(i, carry):
        a_new = pv_stage(jnp.where(i <= 1, qi, i - 2))
        sm = softmax_stage()
        kn = k_ref[pl.ds(pl.multiple_of(i * blk, blk), blk), :]
        d = jnp.minimum(qi - i, MB_BIAS_TILES - 1)
        s_next = [lax.dot_general(kn[:, grp(hh)], qh[hh], nt, preferred_element_type=F32)
                  + bias_ref[hh, d] + msk_ref[hh][pl.ds(i, 1), :] for hh in heads]
        for hh in heads:
            acc_ref[hh][...] = a_new[hh]
            s_ref[hh][...] = s_next[hh]
        store_softmax(*sm)
        return carry

    lax.fori_loop(0, qi, step, 0)
    a_new = pv_stage(jnp.where(qi <= 1, qi, qi - 2))
    sm = softmax_stage()
    for hh in heads:
        acc_ref[hh][...] = a_new[hh]
    store_softmax(*sm)
    a_fin = pv_stage(jnp.where(qi == 0, qi, qi - 1))
    out_t = jnp.concatenate([a_fin[hh] / l_ref[hh][...] for hh in heads], axis=0)
    o_ref[...] = out_t.T.astype(o_ref.dtype)


def moba_attention(pqk, vt, km, bias, batch, seq):
    t = pqk.shape[0]
    nb = seq // MB_BLOCK
    groups = MB_WIDTH // MB_PW
    return pl.pallas_call(
        _moba_kernel,
        grid=(batch, groups, nb),
        in_specs=[
            pl.BlockSpec((MB_BLOCK, MB_PW), lambda b, j, i: (b * nb + i, j)),
            pl.BlockSpec((seq, MB_PW), lambda b, j, i: (b, groups + j)),
            pl.BlockSpec((nb, MB_PW, MB_BLOCK), lambda b, j, i: (b, j, 0)),
            pl.BlockSpec((None, nb, MB_PW), lambda b, j, i: (b, 0, j)),
            pl.BlockSpec((MB_PAIR, MB_BIAS_TILES, MB_BLOCK, MB_BLOCK), lambda b, j, i: (j, 0, 0, 0)),
        ],
        out_specs=pl.BlockSpec((MB_BLOCK, MB_PW), lambda b, j, i: (b * nb + i, j)),
        out_shape=jax.ShapeDtypeStruct((t, MB_WIDTH), BF16),
        scratch_shapes=(
            [pltpu.VMEM((1, MB_BLOCK), F32)] * (3 * MB_PAIR)
            + [pltpu.VMEM((MB_DH, MB_BLOCK), F32)] * MB_PAIR
            + [pltpu.VMEM((nb, MB_BLOCK), F32)] * MB_PAIR
            + [pltpu.VMEM((MB_BLOCK, MB_BLOCK), F32)] * MB_PAIR
            + [pltpu.VMEM((MB_BLOCK, MB_BLOCK), BF16)] * MB_PAIR
        ),
        compiler_params=_cparams(("parallel", "parallel", "arbitrary")),
        name="moba_attn",
    )(pqk, pqk, vt, km, bias)


def _t5_bucket(dist):
    max_exact = REL_BUCKETS // 2
    scaled = jnp.log(jnp.maximum(dist, 1).astype(F32) / max_exact) / math.log(REL_MAX_DIST / max_exact)
    large = jnp.minimum(max_exact + (scaled * (REL_BUCKETS - max_exact)).astype(I32), REL_BUCKETS - 1)
    return jnp.where(dist < max_exact, dist, large)


def moba_bias_tiles(rel_bias):
    blk = MB_BLOCK
    span = 2 * blk - 1
    x = jnp.arange(span) - (blk - 1)
    dist = jnp.maximum(jnp.arange(MB_BIAS_TILES)[:, None] * blk + x[None, :], 0)
    w = rel_bias.astype(F32).T[:, _t5_bucket(dist)]
    h = w.shape[0]
    wp = jnp.pad(w, ((0, 0), (0, 0), (0, 1)))
    a = jnp.broadcast_to(wp[:, :, None, :], (h, MB_BIAS_TILES, blk, span + 1))
    a = a.reshape(h, MB_BIAS_TILES, blk * (span + 1))[:, :, :blk * span]
    return a.reshape(h, MB_BIAS_TILES, blk, span)[:, :, :, blk - 1:]


def _mix_kernel(x_ref, ya_ref, yb_ref, ga_ref, gb_ref, wa_ref, wb_ref, wo_ref, o_ref):
    za = jnp.dot(ya_ref[...], wa_ref[...], preferred_element_type=F32)
    zb = jnp.dot(yb_ref[...], wb_ref[...], preferred_element_type=F32)
    z = jax.nn.sigmoid(ga_ref[...].astype(F32)) * za + jax.nn.sigmoid(gb_ref[...].astype(F32)) * zb
    o_ref[...] = x_ref[...] + jnp.dot(z.astype(BF16), wo_ref[...], preferred_element_type=F32)


def mix_out(x2d, ya, yb, pg, wa, wb, wo, tm=256):
    t, d = x2d.shape
    w = ya.shape[1]
    return pl.pallas_call(
        _mix_kernel,
        grid=(t // tm,),
        in_specs=[
            pl.BlockSpec((tm, d), lambda i: (i, 0)),
            pl.BlockSpec((tm, w), lambda i: (i, 0)),
            pl.BlockSpec((tm, w), lambda i: (i, 0)),
            pl.BlockSpec((tm, d), lambda i: (i, 0)),
            pl.BlockSpec((tm, d), lambda i: (i, 1)),
            pl.BlockSpec((w, d), lambda i: (0, 0)),
            pl.BlockSpec((w, d), lambda i: (0, 0)),
            pl.BlockSpec((d, d), lambda i: (0, 0)),
        ],
        out_specs=pl.BlockSpec((tm, d), lambda i: (i, 0)),
        out_shape=jax.ShapeDtypeStruct((t, d), F32),
        compiler_params=_cparams(("parallel",)),
        name="mix_out",
    )(x2d, ya, yb, pg, pg, wa, wb, wo)


def _mem_kv_kernel(m_ref, g_ref, wk_ref, wv_ref, k_ref, v_ref):
    mn = _rms(m_ref[...], g_ref[...]).astype(BF16)
    k_ref[...] = jnp.dot(mn, wk_ref[...], preferred_element_type=F32).astype(BF16)
    v_ref[...] = jnp.dot(mn, wv_ref[...], preferred_element_type=F32).astype(BF16)


def mem_kv(mem, g, wk, wv):
    b, m, d = mem.shape
    spec = pl.BlockSpec((None, m, d), lambda i: (i, 0, 0))
    wspec = pl.BlockSpec((d, d), lambda i: (0, 0))
    return pl.pallas_call(
        _mem_kv_kernel,
        grid=(b,),
        in_specs=[spec, pl.BlockSpec((1, d), lambda i: (0, 0)), wspec, wspec],
        out_specs=[spec, spec],
        out_shape=[jax.ShapeDtypeStruct((b, m, d), BF16)] * 2,
        compiler_params=_cparams(("parallel",)),
        name="mem_kv",
    )(mem, g, wk, wv)


def _cross_kernel(x_ref, g_ref, wq_ref, k_ref, v_ref, wo_ref, o_ref):
    x = x_ref[...]
    d = x.shape[1]
    dh = d // X_HEADS
    h = _rms(x, g_ref[...]).astype(BF16)
    q = (jnp.dot(h, wq_ref[...], preferred_element_type=F32) * (dh ** -0.5)).astype(BF16)
    outs = []
    for hh in range(X_HEADS):
        sl = slice(hh * dh, (hh + 1) * dh)
        s = lax.dot_general(q[:, sl], k_ref[:, sl], (((1,), (1,)), ((), ())),
                            preferred_element_type=F32)
        p = jnp.exp(s - jnp.max(s, axis=1, keepdims=True))
        l = jnp.sum(p, axis=1, keepdims=True)
        o = jnp.dot(p.astype(BF16), v_ref[:, sl], preferred_element_type=F32) / l
        outs.append(o.astype(BF16))
    o = jnp.concatenate(outs, axis=1)
    o_ref[...] = x + jnp.dot(o, wo_ref[...], preferred_element_type=F32)


def cross_attn(x2d, g, wq, kx, vx, wo, seq, tm=256):
    t, d = x2d.shape
    m = kx.shape[1]
    per_b = seq // tm
    kv = pl.BlockSpec((None, m, d), lambda i: (i // per_b, 0, 0))
    wspec = pl.BlockSpec((d, d), lambda i: (0, 0))
    return pl.pallas_call(
        _cross_kernel,
        grid=(t // tm,),
        in_specs=[pl.BlockSpec((tm, d), lambda i: (i, 0)), pl.BlockSpec((1, d), lambda i: (0, 0)),
                  wspec, kv, kv, wspec],
        out_specs=pl.BlockSpec((tm, d), lambda i: (i, 0)),
        out_shape=jax.ShapeDtypeStruct((t, d), F32),
        compiler_params=_cparams(("parallel",)),
        name="cross_attn",
    )(x2d, g, wq, kx, vx, wo)


def _topk_rows(sc, k):
    n = sc.shape[0]
    io = lax.broadcasted_iota(I32, sc.shape, 0)
    vals, ids = [], []
    for _ in range(k):
        m = jnp.max(sc, axis=0, keepdims=True)
        ix = jnp.min(jnp.where(sc == m, io, n), axis=0, keepdims=True)
        vals.append(m)
        ids.append(ix)
        sc = jnp.where(io == ix, NEG_INF, sc)
    return jnp.concatenate(vals, axis=0), jnp.concatenate(ids, axis=0)


def _pack_bf16_halves(h):
    bits = lax.bitcast_convert_type(h, I32)
    r = bits + 0x7FFF + (lax.shift_right_logical(bits, 16) & 1)
    half = h.shape[1] // 2
    return lax.shift_right_logical(r[:, :half], 16) | (r[:, half:] & HI_MASK)


def _route_kernel(x_ref, g_ref, wq_ref, sk_ref, hp_ref, idx_ref, w_ref, hb_ref, it_ref, wt_ref):
    p = pl.program_id(1)

    @pl.when(p == 0)
    def _():
        h = _rms(x_ref[...], g_ref[...])
        hp_ref[...] = _pack_bf16_halves(h)
        hb_ref[...] = h.astype(BF16)

    qh = jnp.dot(hb_ref[...], wq_ref[...], preferred_element_type=F32)
    tops = []
    for c in range(2):
        seg = qh[:, c * PEER_HALF:(c + 1) * PEER_HALF]
        sc = lax.dot_general(sk_ref[c], seg, (((1,), (1,)), ((), ())),
                             precision=lax.Precision.HIGHEST, preferred_element_type=F32)
        tops.append(_topk_rows(sc, PEER_TOPK))
    (s0, i0), (s1, i1) = tops
    k = PEER_TOPK
    sub = 8
    tm = s0.shape[1]
    r8 = lax.broadcasted_iota(I32, (sub, tm), 0)
    r16 = lax.broadcasted_iota(I32, (k, tm), 0)
    cand_b = [s0[0:1] + s1, s0[1:2] + s1[:sub]]
    cidx_b = [i0[0:1] * PEER_NKEYS + i1, i0[1:2] * PEER_NKEYS + i1[:sub]]
    pos_b = [r16, k + r8]
    for a in range(2, sub):
        keep = r8 < (k // (a + 1))
        cand_b.append(jnp.where(keep, s0[a:a + 1] + s1[:sub], NEG_INF))
        cidx_b.append(i0[a:a + 1] * PEER_NKEYS + i1[:sub])
        pos_b.append(a * k + r8)
    cand_b.append(s0[sub:] + s1[0:1])
    cidx_b.append(i0[sub:] * PEER_NKEYS + i1[0:1])
    pos_b.append((sub + r8) * k)
    cand = jnp.concatenate(cand_b, axis=0)
    cidx = jnp.concatenate(cidx_b, axis=0)
    pos = jnp.concatenate(pos_b, axis=0)
    vals, ids = [], []
    for _ in range(k):
        m = jnp.max(cand, axis=0, keepdims=True)
        px = jnp.min(jnp.where(cand == m, pos, k * k), axis=0, keepdims=True)
        hit = pos == px
        vals.append(m)
        ids.append(jnp.sum(jnp.where(hit, cidx, 0), axis=0, keepdims=True))
        cand = jnp.where(hit, NEG_INF, cand)
    sf = jnp.concatenate(vals, axis=0)
    e = jnp.exp(sf - sf[0:1])
    rows = pl.ds(pl.multiple_of(p * PEER_TOPK, PEER_TOPK), PEER_TOPK)
    wt_ref[rows, :] = e / jnp.sum(e, axis=0, keepdims=True)
    it_ref[rows, :] = jnp.concatenate(ids, axis=0)

    @pl.when(p == pl.num_programs(1) - 1)
    def _():
        idx_ref[...] = it_ref[...].T
        w_ref[...] = wt_ref[...].T


def peer_route(x2d, g, wq, sk, tok0, t, tm=256):
    d = x2d.shape[1]
    ph = sk.shape[0]
    nsel = ph * PEER_TOPK
    blk0 = tok0 // tm
    return pl.pallas_call(
        _route_kernel,
        grid=(t // tm, ph),
        in_specs=[
            pl.BlockSpec((tm, d), lambda i, p: (blk0 + i, 0)),
            pl.BlockSpec((1, d), lambda i, p: (0, 0)),
            pl.BlockSpec((d, 2 * PEER_HALF), lambda i, p: (0, p)),
            pl.BlockSpec((None, 2, PEER_NKEYS, PEER_HALF), lambda i, p: (p, 0, 0, 0)),
        ],
        out_specs=[
            pl.BlockSpec((tm, d // 2), lambda i, p: (i, 0)),
            pl.BlockSpec((tm, nsel), lambda i, p: (i, 0)),
            pl.BlockSpec((tm, nsel), lambda i, p: (i, 0)),
        ],
        out_shape=[jax.ShapeDtypeStruct((t, d // 2), I32),
                   jax.ShapeDtypeStruct((t, nsel), I32),
                   jax.ShapeDtypeStruct((t, nsel), F32)],
        scratch_shapes=[pltpu.VMEM((tm, d), BF16),
                        pltpu.VMEM((nsel, tm), I32),
                        pltpu.VMEM((nsel, tm), F32)],
        compiler_params=_cparams(("parallel", "arbitrary")),
        name="peer_route",
    )(x2d, g, wq, sk)


def _coef_kernel(w_ref, a_ref, o_ref):
    o_ref[...] = w_ref[...] * jax.nn.gelu(a_ref[...])


def peer_coef(w, act, tm=1024):
    t, n = w.shape
    spec = pl.BlockSpec((tm, n), lambda i: (i, 0))
    return pl.pallas_call(
        _coef_kernel, grid=(t // tm,), in_specs=[spec, spec], out_specs=spec,
        out_shape=jax.ShapeDtypeStruct((t, n), F32),
        compiler_params=_cparams(("parallel",)), name="peer_coef",
    )(w, act)


def _final_kernel(x_ref, y_ref, g_ref, o_ref):
    o_ref[...] = _rms(x_ref[...] + y_ref[...], g_ref[...])


def final_norm(x2d, y, g, tok0, tm=512):
    t, d = y.shape
    blk0 = tok0 // tm
    spec = pl.BlockSpec((tm, d), lambda i: (i, 0))
    return pl.pallas_call(
        _final_kernel, grid=(t // tm,),
        in_specs=[pl.BlockSpec((tm, d), lambda i: (blk0 + i, 0)), spec, pl.BlockSpec((1, d), lambda i: (0, 0))],
        out_specs=spec,
        out_shape=jax.ShapeDtypeStruct((t, d), F32),
        compiler_params=_cparams(("parallel",)), name="final_norm",
    )(x2d, y, g)


SC_CORES = 2
SC_SUBCORES = 16
SC_WORKERS = SC_CORES * SC_SUBCORES
SC_LANES = 16
SC_GROUP = 16


def _sc_mesh():
    return plsc.VectorSubcoreMesh(core_axis_name="c", subcore_axis_name="s")


def _sc_params():
    return pltpu.CompilerParams(needs_layout_passes=False)


def _sc_worker_id():
    return lax.axis_index("s") * SC_CORES + lax.axis_index("c")


SC_RING = 4
SC_ROW_SUB = 8
SC_ROW_LANE = 128


def _sc_ring(n_units, start, wait, compute):
    for u in range(SC_RING - 1):
        start(u, u)

    @pl.loop(0, n_units, step=SC_RING)
    def _(uu):
        for b in range(SC_RING):
            u = uu + b
            nxt = u + (SC_RING - 1)

            @pl.when(nxt < n_units)
            def _():
                start(nxt, (b + SC_RING - 1) % SC_RING)

            wait(u, b)
            compute(u, b)


def _sc_unit_off(u):
    off = u * SC_LANES
    return off if isinstance(off, int) else pl.multiple_of(off, SC_LANES)


def _sc_row_piece(rows, r, c):
    per = SC_ROW_LANE // SC_LANES
    return rows[r, c // per, pl.ds(pl.multiple_of((c % per) * SC_LANES, SC_LANES), SC_LANES)]


def peer_dots_sc(table, idx_flat, h):
    t, d = h.shape
    nsel = PEER_SEL
    tpw = t // SC_WORKERS
    g = SC_GROUP
    groups = tpw // g
    heads = nsel // SC_LANES
    pieces = d // SC_LANES
    units = g * heads
    row_buf = pltpu.VMEM((SC_LANES, SC_ROW_SUB, SC_ROW_LANE), F32)

    @functools.partial(
        pl.kernel, mesh=_sc_mesh(),
        out_type=jax.ShapeDtypeStruct((t * nsel,), F32),
        scratch_types=[
            pltpu.VMEM((g * nsel,), I32),
            pltpu.VMEM((g, d), F32),
            pltpu.VMEM((g * nsel,), F32),
            pltpu.VMEM((SC_LANES * SC_LANES,), F32),
            [row_buf] * SC_RING,
            [pltpu.SemaphoreType.DMA] * SC_RING,
        ],
        compiler_params=_sc_params(),
        name="peer_dots_sc",
    )
    def k(tab_hbm, idx_hbm, h_hbm, out_hbm, idx_v, h_v, out_v, red_v, rows, sems):
        wid = _sc_worker_id()
        lane = lax.iota(I32, SC_LANES)

        def copy(u, slot):
            ids = idx_v.at[pl.ds(_sc_unit_off(u), SC_LANES)]
            return pltpu.make_async_copy(tab_hbm.at[ids], rows[slot], sems[slot])

        def compute(u, slot):
            tt = u // heads

            def body(c, accs):
                hv = h_v[tt, pl.ds(pl.multiple_of(c * SC_LANES, SC_LANES), SC_LANES)]
                return tuple(accs[r] + _sc_row_piece(rows[slot], r, c) * hv for r in range(SC_LANES))

            accs = lax.fori_loop(0, pieces, body,
                                 tuple(jnp.zeros((SC_LANES,), F32) for _ in range(SC_LANES)))
            for r in range(SC_LANES):
                red_v[pl.ds(r * SC_LANES, SC_LANES)] = accs[r]
            cols = [plsc.load_gather(red_v, [lane * SC_LANES + j]) for j in range(SC_LANES)]
            while len(cols) > 1:
                cols = [cols[i] + cols[i + 1] for i in range(0, len(cols), 2)]
            out_v[pl.ds(_sc_unit_off(u), SC_LANES)] = cols[0]

        @pl.loop(0, groups)
        def _(gi):
            base = wid * tpw + gi * g
            pltpu.sync_copy(idx_hbm.at[pl.ds(base * nsel, g * nsel)], idx_v)
            pltpu.sync_copy(h_hbm.at[pl.ds(base, g)], h_v)
            _sc_ring(units, lambda u, s: copy(u, s).start(), lambda u, s: copy(u, s).wait(), compute)
            pltpu.sync_copy(out_v, out_hbm.at[pl.ds(base * nsel, g * nsel)])

    return k(table, idx_flat, h)


def peer_combine_sc(table, idx_flat, coef_flat, t):
    d = table.shape[1] * table.shape[2]
    nsel = PEER_SEL
    tpw = t // SC_WORKERS
    g = SC_GROUP
    groups = tpw // g
    heads = nsel // SC_LANES
    pieces = d // SC_LANES
    units = g * heads
    row_buf = pltpu.VMEM((SC_LANES, SC_ROW_SUB, SC_ROW_LANE), F32)

    @functools.partial(
        pl.kernel, mesh=_sc_mesh(),
        out_type=jax.ShapeDtypeStruct((t, d), F32),
        scratch_types=[
            pltpu.VMEM((g * nsel,), I32),
            pltpu.VMEM((g * nsel,), F32),
            pltpu.VMEM((g, d), F32),
            [row_buf] * SC_RING,
            [pltpu.SemaphoreType.DMA] * SC_RING,
        ],
        compiler_params=_sc_params(),
        name="peer_combine_sc",
    )
    def k(tab_hbm, idx_hbm, coef_hbm, out_hbm, idx_v, coef_v, y_v, rows, sems):
        wid = _sc_worker_id()

        def copy(u, slot):
            ids = idx_v.at[pl.ds(_sc_unit_off(u), SC_LANES)]
            return pltpu.make_async_copy(tab_hbm.at[ids], rows[slot], sems[slot])

        def compute(u, slot):
            tt = u // heads
            first = (u % heads) == 0
            cs = [plsc.load_gather(coef_v, [jnp.full((SC_LANES,), u * SC_LANES + r, I32)])
                  for r in range(SC_LANES)]

            @plsc.parallel_loop(0, pieces, unroll=2)
            def _(c):
                off = pl.multiple_of(c * SC_LANES, SC_LANES)
                terms = [cs[r] * _sc_row_piece(rows[slot], r, c) for r in range(SC_LANES)]
                while len(terms) > 1:
                    terms = [terms[i] + terms[i + 1] for i in range(0, len(terms), 2)]
                prev = y_v[tt, pl.ds(off, SC_LANES)]
                y_v[tt, pl.ds(off, SC_LANES)] = terms[0] + jnp.where(first, 0.0, prev)

        @pl.loop(0, groups)
        def _(gi):
            base = wid * tpw + gi * g
            pltpu.sync_copy(idx_hbm.at[pl.ds(base * nsel, g * nsel)], idx_v)
            pltpu.sync_copy(coef_hbm.at[pl.ds(base * nsel, g * nsel)], coef_v)
            _sc_ring(units, lambda u, s: copy(u, s).start(), lambda u, s: copy(u, s).wait(), compute)
            pltpu.sync_copy(y_v, out_hbm.at[pl.ds(base, g)])

    return k(table, idx_flat, coef_flat)


GELU_C0 = math.sqrt(2.0 / math.pi)
GELU_C1 = 0.044715


def _gelu_tanh(x):
    z = GELU_C0 * (x + GELU_C1 * (x * x * x))
    th = 1.0 - 2.0 / (jnp.exp(2.0 * z) + 1.0)
    return 0.5 * x * (1.0 + th)


def peer_experts_sc(tab_u, tab_v, idx_flat, w_flat, h):
    t, d = h.shape
    nsel = PEER_SEL
    tpw = t // SC_WORKERS
    g = SC_GROUP
    groups = tpw // g
    heads = nsel // SC_LANES
    pieces = d // SC_LANES
    units = g * heads
    row_buf = pltpu.VMEM((SC_LANES, SC_ROW_SUB, SC_ROW_LANE), F32)

    @functools.partial(
        pl.kernel, mesh=_sc_mesh(),
        out_type=jax.ShapeDtypeStruct((t, d), F32),
        scratch_types=[
            pltpu.VMEM((g * nsel,), I32),
            pltpu.VMEM((g * nsel,), F32),
            pltpu.VMEM((g, d), F32),
            pltpu.VMEM((g, d), F32),
            pltpu.VMEM((SC_LANES * SC_LANES,), F32),
            [row_buf] * SC_RING,
            [pltpu.SemaphoreType.DMA] * SC_RING,
        ],
        compiler_params=_sc_params(),
        name="peer_experts_sc",
    )
    def k(u_hbm, v_hbm, idx_hbm, w_hbm, h_hbm, out_hbm, idx_v, coef_v, h_v, y_v, red_v, rows, sems):
        wid = _sc_worker_id()
        lane = lax.iota(I32, SC_LANES)

        def copy(tab_hbm, u, slot):
            ids = idx_v.at[pl.ds(_sc_unit_off(u), SC_LANES)]
            return pltpu.make_async_copy(tab_hbm.at[ids], rows[slot], sems[slot])

        def dots(u, slot):
            tt = u // heads

            def body(c, accs):
                hv = h_v[tt, pl.ds(pl.multiple_of(c * SC_LANES, SC_LANES), SC_LANES)]
                return tuple(accs[r] + _sc_row_piece(rows[slot], r, c) * hv for r in range(SC_LANES))

            accs = lax.fori_loop(0, pieces, body,
                                 tuple(jnp.zeros((SC_LANES,), F32) for _ in range(SC_LANES)))
            for r in range(SC_LANES):
                red_v[pl.ds(r * SC_LANES, SC_LANES)] = accs[r]
            cols = [plsc.load_gather(red_v, [lane * SC_LANES + j]) for j in range(SC_LANES)]
            while len(cols) > 1:
                cols = [cols[i] + cols[i + 1] for i in range(0, len(cols), 2)]
            sl = pl.ds(_sc_unit_off(u), SC_LANES)
            coef_v[sl] = coef_v[sl] * _gelu_tanh(cols[0])

        def combine(u, slot):
            tt = u // heads
            first = (u % heads) == 0
            cs = [plsc.load_gather(coef_v, [jnp.full((SC_LANES,), u * SC_LANES + r, I32)])
                  for r in range(SC_LANES)]

            @plsc.parallel_loop(0, pieces, unroll=2)
            def _(c):
                off = pl.multiple_of(c * SC_LANES, SC_LANES)
                terms = [cs[r] * _sc_row_piece(rows[slot], r, c) for r in range(SC_LANES)]
                while len(terms) > 1:
                    terms = [terms[i] + terms[i + 1] for i in range(0, len(terms), 2)]
                prev = y_v[tt, pl.ds(off, SC_LANES)]
                y_v[tt, pl.ds(off, SC_LANES)] = terms[0] + jnp.where(first, 0.0, prev)

        @pl.loop(0, groups)
        def _(gi):
            base = wid * tpw + gi * g
            pltpu.sync_copy(idx_hbm.at[pl.ds(base * nsel, g * nsel)], idx_v)
            pltpu.sync_copy(w_hbm.at[pl.ds(base * nsel, g * nsel)], coef_v)
            pltpu.sync_copy(h_hbm.at[pl.ds(base, g)], h_v)
            _sc_ring(units, lambda u, s: copy(u_hbm, u, s).start(), lambda u, s: copy(u_hbm, u, s).wait(), dots)
            _sc_ring(units, lambda u, s: copy(v_hbm, u, s).start(), lambda u, s: copy(v_hbm, u, s).wait(), combine)
            pltpu.sync_copy(y_v, out_hbm.at[pl.ds(base, g)])

    return k(tab_u, tab_v, idx_flat, w_flat, h)


SC_PK_RING = 8
SC_PK_SUB = 4
HI_MASK = -65536


def pack_bf16_pairs(a):
    half = a.shape[1] // 2
    bits = lax.bitcast_convert_type(a.astype(BF16), jnp.uint16).astype(jnp.uint32)
    return lax.bitcast_convert_type(bits[:, :half] | (bits[:, half:] << 16), I32)


def _unpack_halves(x32):
    w = plsc.bitcast(x32, I32)
    return plsc.bitcast(w << 16, F32), plsc.bitcast(w & HI_MASK, F32)


def _tree_sum(xs):
    while len(xs) > 1:
        xs = [xs[i] + xs[i + 1] for i in range(0, len(xs), 2)]
    return xs[0]


def peer_experts_pk_sc(tab_u, tab_v, idx_flat, w_flat, hp, d):
    t = hp.shape[0]
    nsel = PEER_SEL
    tpw = t // SC_WORKERS
    g = SC_GROUP
    groups = tpw // g
    heads = nsel // SC_LANES
    chunks = d // 32
    units = g * heads
    ring = SC_PK_RING
    row_buf = pltpu.VMEM((SC_LANES, SC_PK_SUB, SC_ROW_LANE), I32)

    def row_words(rows, r, wc):
        per = SC_ROW_LANE // SC_LANES
        return plsc.bitcast(rows[r, wc // per, pl.ds(pl.multiple_of((wc % per) * SC_LANES, SC_LANES), SC_LANES)],
                            BF16)

    def ring_loop(n_units, start, wait, compute):
        for u in range(ring - 1):
            start(u, u)

        @pl.loop(0, n_units, step=ring)
        def _(uu):
            for b in range(ring):
                u = uu + b
                nxt = u + (ring - 1)

                @pl.when(nxt < n_units)
                def _():
                    start(nxt, (b + ring - 1) % ring)

                wait(u, b)
                compute(u, b)

    @functools.partial(
        pl.kernel, mesh=_sc_mesh(),
        out_type=jax.ShapeDtypeStruct((t, d), F32),
        scratch_types=[
            pltpu.VMEM((g * nsel,), I32),
            pltpu.VMEM((g * nsel,), F32),
            pltpu.VMEM((g, d // 2), I32),
            pltpu.VMEM((g, d), F32),
            pltpu.VMEM((SC_LANES * SC_LANES,), F32),
            [row_buf] * ring,
            [pltpu.SemaphoreType.DMA] * ring,
        ],
        compiler_params=_sc_params(),
        name="peer_experts_pk_sc",
    )
    def k(u_hbm, v_hbm, idx_hbm, w_hbm, h_hbm, out_hbm, idx_v, coef_v, h_v, y_v, red_v, rows, sems):
        wid = _sc_worker_id()
        lane = lax.iota(I32, SC_LANES)

        def copy(tab_hbm, u, slot):
            ids = idx_v.at[pl.ds(_sc_unit_off(u), SC_LANES)]
            return pltpu.make_async_copy(tab_hbm.at[ids], rows[slot], sems[slot])

        def dots(u, slot):
            tt = u // heads

            def body(cp, accs):
                out = []
                hv = [plsc.bitcast(h_v[tt, pl.ds(pl.multiple_of((2 * cp + i) * SC_LANES, SC_LANES), SC_LANES)], BF16)
                      for i in range(2)]
                for r in range(SC_LANES):
                    pr = row_words(rows[slot], r, 2 * cp) * hv[0] + row_words(rows[slot], r, 2 * cp + 1) * hv[1]
                    lo, hi = _unpack_halves(pr)
                    out.append(accs[r] + lo + hi)
                return tuple(out)

            accs = lax.fori_loop(0, chunks // 2, body,
                                 tuple(jnp.zeros((SC_LANES,), F32) for _ in range(SC_LANES)))
            for r in range(SC_LANES):
                red_v[pl.ds(r * SC_LANES, SC_LANES)] = accs[r]
            act = _tree_sum([plsc.load_gather(red_v, [lane * SC_LANES + j]) for j in range(SC_LANES)])
            sl = pl.ds(_sc_unit_off(u), SC_LANES)
            coef_v[sl] = coef_v[sl] * _gelu_tanh(act)

        def combine(u, slot):
            tt = u // heads
            first = (u % heads) == 0
            cb = []
            for r in range(SC_LANES):
                c = plsc.load_gather(coef_v, [jnp.full((SC_LANES,), u * SC_LANES + r, I32)])
                cb.append(plsc.pack(c, c, format=plsc.PackFormat.INTERLEAVED))

            @plsc.parallel_loop(0, chunks, unroll=2)
            def _(wc):
                lo, hi = _unpack_halves(_tree_sum([cb[r] * row_words(rows[slot], r, wc) for r in range(SC_LANES)]))
                for half, val in ((0, lo), (1, hi)):
                    sl = pl.ds(pl.multiple_of(half * (d // 2) + wc * SC_LANES, SC_LANES), SC_LANES)
                    y_v[tt, sl] = val + jnp.where(first, 0.0, y_v[tt, sl])

        @pl.loop(0, groups)
        def _(gi):
            base = wid * tpw + gi * g
            pltpu.sync_copy(idx_hbm.at[pl.ds(base * nsel, g * nsel)], idx_v)
            pltpu.sync_copy(w_hbm.at[pl.ds(base * nsel, g * nsel)], coef_v)
            pltpu.sync_copy(h_hbm.at[pl.ds(base, g)], h_v)
            ring_loop(units, lambda u, s: copy(u_hbm, u, s).start(), lambda u, s: copy(u_hbm, u, s).wait(), dots)
            ring_loop(units, lambda u, s: copy(v_hbm, u, s).start(), lambda u, s: copy(v_hbm, u, s).wait(), combine)
            pltpu.sync_copy(y_v, out_hbm.at[pl.ds(base, g)])

    return k(tab_u, tab_v, idx_flat, w_flat, hp)


def kernel(x, mem, rel_bias, ln_mix, w_in, hg_lower, hg_norm, w_up_a, w_up_b, w_out, ln_cross, ln_mem, wq_x, wk_x, wv_x, wo_x, ln_ffn, peer_query, peer_subkeys, peer_u, peer_v, ln_final):
    b, s, d = x.shape
    depth = w_in.shape[0]
    assert depth == 1, "the residual after PEER is fused into the final norm"
    assert s % MB_BLOCK == 0 and s % HG_CHUNK == 0 and s % (PEER_SLICES * SC_WORKERS * SC_GROUP) == 0
    nb = s // MB_BLOCK
    row = lambda a: a.reshape(1, -1).astype(F32)
    lb_all = jnp.cumsum(jax.nn.softmax(hg_lower.astype(F32), axis=0), axis=0)
    bias = moba_bias_tiles(rel_bias)
    n_hg = 4 * HG_WIDTH
    n_qk = 2 * MB_WIDTH
    n_mb = 3 * MB_WIDTH
    l = 0
    w = w_in[l].astype(BF16)
    w_hg, w_qk, w_vt, w_g = w[:, :n_hg], w[:, n_hg:n_hg + n_qk], w[:, n_hg + n_qk:n_hg + n_mb].T, w[:, n_hg + n_mb:]
    wa, wb, wo = w_up_a[l].astype(BF16), w_up_b[l].astype(BF16), w_out[l].astype(BF16)
    wqx, wox = wq_x[l].astype(BF16), wo_x[l].astype(BF16)
    wpq, sk = peer_query[l].astype(BF16), peer_subkeys[l].astype(F32)
    tab3 = lambda a: pack_bf16_pairs(a.astype(F32)).reshape(a.shape[0], SC_PK_SUB, SC_ROW_LANE)
    tab_u, tab_v = tab3(peer_u[l]), tab3(peer_v[l])
    kx, vx = mem_kv(mem, row(ln_mem[l]), wk_x[l].astype(BF16), wv_x[l].astype(BF16))

    outs = []
    for bi in range(b):
        x2d = x[bi]
        p0, pqk, vt, pg = in_proj(x2d, row(ln_mix[l]), w_hg, w_qk, w_vt, w_g)
        ya = hgrn2(p0, row(lb_all[l]), row(hg_norm[l]), 1, s)
        km = moba_kmean(pqk, 1, s).reshape(1, nb, MB_WIDTH)
        yb = moba_attention(pqk, vt, km, bias, 1, s)
        x2d = mix_out(x2d, ya, yb, pg, wa, wb, wo)
        x2d = cross_attn(x2d, row(ln_cross[l]), wqx, kx[bi:bi + 1], vx[bi:bi + 1], wox, s)
        ts = s // PEER_SLICES
        for tok0 in range(0, s, ts):
            hp, eidx, wts = peer_route(x2d, row(ln_ffn[l]), wpq, sk, tok0, ts)
            y = peer_experts_pk_sc(tab_u, tab_v, eidx.reshape(ts * PEER_SEL), wts.reshape(ts * PEER_SEL), hp, d)
            outs.append(final_norm(x2d, y, row(ln_final), tok0))
    return jnp.concatenate(outs, axis=0).reshape(b, s, d)
```

```python
import functools
import math

import jax
import jax.numpy as jnp
import numpy as np
from jax import lax
from jax.experimental import pallas as pl
from jax.experimental.pallas import tpu as pltpu
from jax.experimental.pallas import tpu_sc as plsc

F32 = jnp.float32
BF16 = jnp.bfloat16
I32 = jnp.int32
EPS = 1e-6
NEG_INF = float("-inf")

HG_HEADS = 4
HG_D = 128
HG_WIDTH = HG_HEADS * HG_D
HG_CHUNK = 64
HG_SUB = 16
MB_HEADS = 8
MB_DH = 64
MB_WIDTH = MB_HEADS * MB_DH
MB_BLOCK = 256
MB_TOPK = 3
MB_BIAS_TILES = 8
REL_BUCKETS = 32
REL_MAX_DIST = 2048
X_HEADS = 4
PEER_HEADS = 8
PEER_NKEYS = 128
PEER_TOPK = 16
PEER_HALF = 128
PEER_SEL = PEER_HEADS * PEER_TOPK
PEER_SLICES = 4

VMEM_LIMIT = 56 * 1024 * 1024


def _cparams(sem):
    return pltpu.CompilerParams(dimension_semantics=sem, vmem_limit_bytes=VMEM_LIMIT)


def _rms(x, g):
    ms = jnp.mean(x * x, axis=-1, keepdims=True)
    return x * lax.rsqrt(ms + EPS) * g


def _in_proj_kernel(x_ref, g_ref, w0_ref, w1_ref, wvt_ref, w2_ref, o0_ref, o1_ref, ovt_ref, o2_ref):
    h = _rms(x_ref[...], g_ref[...]).astype(BF16)
    o0_ref[...] = jnp.dot(h, w0_ref[...], preferred_element_type=F32)
    o1_ref[...] = jnp.dot(h, w1_ref[...], preferred_element_type=F32).astype(BF16)
    ovt_ref[0] = lax.dot_general(wvt_ref[...], h, (((1,), (1,)), ((), ())),
                                 preferred_element_type=F32).astype(BF16)
    o2_ref[...] = jnp.dot(h, w2_ref[...], preferred_element_type=F32).astype(BF16)


def in_proj(x2d, g, w0, w1, wvt, w2):
    t, d = x2d.shape
    tm = MB_BLOCK
    n0, n1, nv, n2 = w0.shape[1], w1.shape[1], wvt.shape[0], w2.shape[1]
    full = lambda a: pl.BlockSpec(a.shape, lambda i: (0, 0))
    return pl.pallas_call(
        _in_proj_kernel,
        grid=(t // tm,),
        in_specs=[pl.BlockSpec((tm, d), lambda i: (i, 0)), full(g), full(w0), full(w1), full(wvt), full(w2)],
        out_specs=[pl.BlockSpec((tm, n0), lambda i: (i, 0)),
                   pl.BlockSpec((tm, n1), lambda i: (i, 0)),
                   pl.BlockSpec((1, nv, tm), lambda i: (i, 0, 0)),
                   pl.BlockSpec((tm, n2), lambda i: (i, 0))],
        out_shape=[jax.ShapeDtypeStruct((t, n0), F32),
                   jax.ShapeDtypeStruct((t, n1), BF16),
                   jax.ShapeDtypeStruct((t // tm, nv, tm), BF16),
                   jax.ShapeDtypeStruct((t, n2), BF16)],
        compiler_params=_cparams(("parallel",)),
        name="in_proj",
    )(x2d, g, w0, w1, wvt, w2)


def _hgrn_kernel(q_ref, f_ref, i_ref, g_ref, lb_ref, gain_ref, o_ref, st_ref):
    c = pl.program_id(1)

    @pl.when(c == 0)
    def _():
        st_ref[...] = jnp.zeros_like(st_ref)

    C, S = HG_CHUNK, HG_SUB
    row = lax.broadcasted_iota(I32, (C, C), 0)
    col = lax.broadcasted_iota(I32, (C, C), 1)
    tril = (row >= col).astype(F32)
    t_iota = lax.broadcasted_iota(I32, (S, 1), 0)

    for h in range(HG_HEADS):
        sl = slice(h * HG_D, (h + 1) * HG_D)
        q = q_ref[:, sl]
        v = i_ref[:, sl]
        lb = lb_ref[:, sl]
        f = lb + (1.0 - lb) * jax.nn.sigmoid(f_ref[:, sl])
        lf = jnp.log(f)
        k = 1.0 - f
        b = jnp.dot(tril, lf, precision=lax.Precision.HIGHEST, preferred_element_type=F32)
        st = st_ref[h]
        vb = v.astype(BF16)
        qd = (q * jnp.exp(b)).astype(BF16)
        o_inter = lax.dot_general(qd, st.astype(BF16), (((1,), (1,)), ((), ())),
                                  preferred_element_type=F32)
        outs = []
        for i in range(C // S):
            r0 = i * S
            qi = q[r0:r0 + S]
            ki = k[r0:r0 + S]
            bi = b[r0:r0 + S]
            vi = v[r0:r0 + S]
            oi = o_inter[r0:r0 + S]
            if i > 0:
                bs = b[r0 - 1:r0]
                qh = (qi * jnp.exp(bi - bs)).astype(BF16)
                kh = (k[:r0] * jnp.exp(bs - b[:r0])).astype(BF16)
                a = lax.dot_general(qh, kh, (((1,), (1,)), ((), ())), preferred_element_type=F32)
                oi = oi + jnp.dot(a.astype(BF16), vb[:r0], preferred_element_type=F32)
            for s in range(S):
                dec = jnp.exp(jnp.minimum(bi - bi[s:s + 1], 0.0))
                p = qi * ki[s:s + 1] * dec
                a_s = jnp.sum(p, axis=-1, keepdims=True)
                a_s = jnp.where(t_iota >= s, a_s, 0.0)
                oi = oi + a_s * vi[s:s + 1]
            outs.append(oi)
        o = jnp.concatenate(outs, axis=0)
        b_end = b[C - 1:C]
        kd = (k * jnp.exp(b_end - b)).astype(BF16)
        upd = lax.dot_general(vb, kd, (((0,), (0,)), ((), ())), preferred_element_type=F32)
        st_ref[h] = st * jnp.exp(b_end) + upd
        o = o * lax.rsqrt(jnp.mean(o * o, axis=-1, keepdims=True) + EPS)
        g = g_ref[:, sl]
        o_ref[:, sl] = (o * gain_ref[:, sl] * (g * jax.nn.sigmoid(g))).astype(o_ref.dtype)


def hgrn2(p0, lb, gain, batch, seq):
    t = p0.shape[0]
    nc = seq // HG_CHUNK
    w = HG_WIDTH

    def col(j):
        return pl.BlockSpec((HG_CHUNK, w), lambda b, c, j=j: (b * nc + c, j))

    return pl.pallas_call(
        _hgrn_kernel,
        grid=(batch, nc),
        in_specs=[col(0), col(1), col(2), col(3),
                  pl.BlockSpec((1, w), lambda b, c: (0, 0)),
                  pl.BlockSpec((1, w), lambda b, c: (0, 0))],
        out_specs=pl.BlockSpec((HG_CHUNK, w), lambda b, c: (b * nc + c, 0)),
        out_shape=jax.ShapeDtypeStruct((t, w), BF16),
        scratch_shapes=[pltpu.VMEM((HG_HEADS, HG_D, HG_D), F32)],
        compiler_params=_cparams(("parallel", "arbitrary")),
        name="hgrn2",
    )(p0, p0, p0, p0, lb, gain)


def _kmean_kernel(k_ref, o_ref):
    o_ref[0] = jnp.mean(k_ref[...].astype(F32), axis=0, keepdims=True)


def moba_kmean(p1, batch, seq):
    nbt = p1.shape[0] // MB_BLOCK
    return pl.pallas_call(
        _kmean_kernel,
        grid=(nbt,),
        in_specs=[pl.BlockSpec((MB_BLOCK, MB_WIDTH), lambda i: (i, 1))],
        out_specs=pl.BlockSpec((1, 1, MB_WIDTH), lambda i: (i, 0, 0)),
        out_shape=jax.ShapeDtypeStruct((nbt, 1, MB_WIDTH), F32),
        compiler_params=_cparams(("parallel",)),
        name="moba_kmean",
    )(p1)


MB_PAIR = 4
MB_PW = MB_PAIR * MB_DH
MB_LG = 128


def _moba_kernel(q_ref, k_ref, vt_ref, km_ref, bias_ref, o_ref, *scratch):
    m_ref, l_ref, al_ref, acc_ref, msk_ref, s_ref, p_ref = (
        scratch[i * MB_PAIR:(i + 1) * MB_PAIR] for i in range(7))
    qi = pl.program_id(2)
    nb = km_ref.shape[0]
    blk = MB_BLOCK
    heads = range(MB_PAIR)
    grp = lambda hh: slice((hh // 2) * MB_LG, (hh // 2 + 1) * MB_LG)
    q = q_ref[...]
    lane = lax.broadcasted_iota(I32, (blk, MB_LG), 1)
    in_head = [(lane < MB_DH) if hh % 2 == 0 else (lane >= MB_DH) for hh in heads]
    qs = q * jnp.asarray(MB_DH ** -0.5, BF16)
    qh = [jnp.where(in_head[hh], qs[:, grp(hh)], jnp.zeros((blk, MB_LG), BF16)) for hh in heads]
    nt = (((1,), (1,)), ((), ()))

    qf = q.astype(F32)
    n_io = lax.broadcasted_iota(I32, (nb, blk), 0)
    for hh in heads:
        gate = lax.dot_general(km_ref[:, grp(hh)], jnp.where(in_head[hh], qf[:, grp(hh)], 0.0), nt,
                               precision=lax.Precision.HIGHEST, preferred_element_type=F32)
        gate = jnp.where(n_io < qi, gate, NEG_INF)
        chosen = n_io < 0
        for _ in range(MB_TOPK):
            mx = jnp.max(gate, axis=0, keepdims=True)
            ix = jnp.min(jnp.where(gate == mx, n_io, nb), axis=0, keepdims=True)
            hit = n_io == ix
            chosen = chosen | (hit & (mx > NEG_INF))
            gate = jnp.where(hit, NEG_INF, gate)
        msk_ref[hh][...] = jnp.where(chosen, 0.0, NEG_INF)

    own_rows = lambda r, hh: r[(hh % 2) * MB_DH:(hh % 2 + 1) * MB_DH]

    def pv_stage(blk_idx):
        vtb = vt_ref[blk_idx]
        r = [jnp.dot(vtb[grp(hh)], p_ref[hh][...], preferred_element_type=F32) for hh in heads]
        return [al_ref[hh][...] * acc_ref[hh][...] + own_rows(r[hh], hh) for hh in heads]

    def softmax_stage():
        s = [s_ref[hh][...] for hh in heads]
        m_old = [m_ref[hh][...] for hh in heads]
        l_old = [l_ref[hh][...] for hh in heads]
        m_new = [jnp.maximum(m_old[hh], jnp.max(s[hh], axis=0, keepdims=True)) for hh in heads]
        alpha = [jnp.exp(m_old[hh] - m_new[hh]) for hh in heads]
        p = [jnp.exp(s[hh] - m_new[hh]) for hh in heads]
        l_new = [alpha[hh] * l_old[hh] + jnp.sum(p[hh], axis=0, keepdims=True) for hh in heads]
        return [x.astype(BF16) for x in p], alpha, m_new, l_new

    def store_softmax(p, alpha, m_new, l_new):
        for hh in heads:
            p_ref[hh][...] = p[hh]
            al_ref[hh][...] = alpha[hh]
            m_ref[hh][...] = m_new[hh]
            l_ref[hh][...] = l_new[hh]

    k_own = k_ref[pl.ds(pl.multiple_of(qi * blk, blk), blk), :]
    key_io = lax.broadcasted_iota(I32, (blk, blk), 0)
    qry_io = lax.broadcasted_iota(I32, (blk, blk), 1)
    for hh in heads:
        s = lax.dot_general(k_own[:, grp(hh)], qh[hh], nt, preferred_element_type=F32) + bias_ref[hh, 0]
        s_ref[hh][...] = jnp.where(key_io <= qry_io, s, NEG_INF)
        m_ref[hh][...] = jnp.full((1, blk), NEG_INF, F32)
        l_ref[hh][...] = jnp.zeros((1, blk), F32)
        al_ref[hh][...] = jnp.ones((1, blk), F32)
        acc_ref[hh][...] = jnp.zeros((MB_DH, blk), F32)
        p_ref[hh][...] = jnp.zeros((blk, blk), BF16)

    def step(i, carry):
        a_new = pv_stage(jnp.where(i <= 1, qi, i - 2))
        sm = softmax_stage()
        kn = k_ref[pl.ds(pl.multiple_of(i * blk, blk), blk), :]
        d = jnp.minimum(qi - i, MB_BIAS_TILES - 1)
        s_next = [lax.dot_general(kn[:, grp(hh)], qh[hh], nt, preferred_element_type=F32)
                  + bias_ref[hh, d] + msk_ref[hh][pl.ds(i, 1), :] for hh in heads]
        for hh in heads:
            acc_ref[hh][...] = a_new[hh]
            s_ref[hh][...] = s_next[hh]
        store_softmax(*sm)
        return carry

    lax.fori_loop(0, qi, step, 0)
    a_new = pv_stage(jnp.where(qi <= 1, qi, qi - 2))
    sm = softmax_stage()
    for hh in heads:
        acc_ref[hh][...] = a_new[hh]
    store_softmax(*sm)
    a_fin = pv_stage(jnp.where(qi == 0, qi, qi - 1))
    out_t = jnp.concatenate([a_fin[hh] / l_ref[hh][...] for hh in heads], axis=0)
    o_ref[...] = out_t.T.astype(o_ref.dtype)


def moba_attention(pqk, vt, km, bias, batch, seq):
    t = pqk.shape[0]
    nb = seq // MB_BLOCK
    groups = MB_WIDTH // MB_PW
    return pl.pallas_call(
        _moba_kernel,
        grid=(batch, groups, nb),
        in_specs=[
            pl.BlockSpec((MB_BLOCK, MB_PW), lambda b, j, i: (b * nb + i, j)),
            pl.BlockSpec((seq, MB_PW), lambda b, j, i: (b, groups + j)),
            pl.BlockSpec((nb, MB_PW, MB_BLOCK), lambda b, j, i: (b, j, 0)),
            pl.BlockSpec((None, nb, MB_PW), lambda b, j, i: (b, 0, j)),
            pl.BlockSpec((MB_PAIR, MB_BIAS_TILES, MB_BLOCK, MB_BLOCK), lambda b, j, i: (j, 0, 0, 0)),
        ],
        out_specs=pl.BlockSpec((MB_BLOCK, MB_PW), lambda b, j, i: (b * nb + i, j)),
        out_shape=jax.ShapeDtypeStruct((t, MB_WIDTH), BF16),
        scratch_shapes=(
            [pltpu.VMEM((1, MB_BLOCK), F32)] * (3 * MB_PAIR)
            + [pltpu.VMEM((MB_DH, MB_BLOCK), F32)] * MB_PAIR
            + [pltpu.VMEM((nb, MB_BLOCK), F32)] * MB_PAIR
            + [pltpu.VMEM((MB_BLOCK, MB_BLOCK), F32)] * MB_PAIR
            + [pltpu.VMEM((MB_BLOCK, MB_BLOCK), BF16)] * MB_PAIR
        ),
        compiler_params=_cparams(("parallel", "parallel", "arbitrary")),
        name="moba_attn",
    )(pqk, pqk, vt, km, bias)


def _t5_bucket(dist):
    max_exact = REL_BUCKETS // 2
    scaled = jnp.log(jnp.maximum(dist, 1).astype(F32) / max_exact) / math.log(REL_MAX_DIST / max_exact)
    large = jnp.minimum(max_exact + (scaled * (REL_BUCKETS - max_exact)).astype(I32), REL_BUCKETS - 1)
    return jnp.where(dist < max_exact, dist, large)


def moba_bias_tiles(rel_bias):
    blk = MB_BLOCK
    span = 2 * blk - 1
    x = jnp.arange(span) - (blk - 1)
    dist = jnp.maximum(jnp.arange(MB_BIAS_TILES)[:, None] * blk + x[None, :], 0)
    w = rel_bias.astype(F32).T[:, _t5_bucket(dist)]
    h = w.shape[0]
    wp = jnp.pad(w, ((0, 0), (0, 0), (0, 1)))
    a = jnp.broadcast_to(wp[:, :, None, :], (h, MB_BIAS_TILES, blk, span + 1))
    a = a.reshape(h, MB_BIAS_TILES, blk * (span + 1))[:, :, :blk * span]
    return a.reshape(h, MB_BIAS_TILES, blk, span)[:, :, :, blk - 1:]


def _mix_kernel(x_ref, ya_ref, yb_ref, ga_ref, gb_ref, wa_ref, wb_ref, wo_ref, o_ref):
    za = jnp.dot(ya_ref[...], wa_ref[...], preferred_element_type=F32)
    zb = jnp.dot(yb_ref[...], wb_ref[...], preferred_element_type=F32)
    z = jax.nn.sigmoid(ga_ref[...].astype(F32)) * za + jax.nn.sigmoid(gb_ref[...].astype(F32)) * zb
    o_ref[...] = x_ref[...] + jnp.dot(z.astype(BF16), wo_ref[...], preferred_element_type=F32)


def mix_out(x2d, ya, yb, pg, wa, wb, wo, tm=256):
    t, d = x2d.shape
    w = ya.shape[1]
    return pl.pallas_call(
        _mix_kernel,
        grid=(t // tm,),
        in_specs=[
            pl.BlockSpec((tm, d), lambda i: (i, 0)),
            pl.BlockSpec((tm, w), lambda i: (i, 0)),
            pl.BlockSpec((tm, w), lambda i: (i, 0)),
            pl.BlockSpec((tm, d), lambda i: (i, 0)),
            pl.BlockSpec((tm, d), lambda i: (i, 1)),
            pl.BlockSpec((w, d), lambda i: (0, 0)),
            pl.BlockSpec((w, d), lambda i: (0, 0)),
            pl.BlockSpec((d, d), lambda i: (0, 0)),
        ],
        out_specs=pl.BlockSpec((tm, d), lambda i: (i, 0)),
        out_shape=jax.ShapeDtypeStruct((t, d), F32),
        compiler_params=_cparams(("parallel",)),
        name="mix_out",
    )(x2d, ya, yb, pg, pg, wa, wb, wo)


def _mem_kv_kernel(m_ref, g_ref, wk_ref, wv_ref, k_ref, v_ref):
    mn = _rms(m_ref[...], g_ref[...]).astype(BF16)
    k_ref[...] = jnp.dot(mn, wk_ref[...], preferred_element_type=F32).astype(BF16)
    v_ref[...] = jnp.dot(mn, wv_ref[...], preferred_element_type=F32).astype(BF16)


def mem_kv(mem, g, wk, wv):
    b, m, d = mem.shape
    spec = pl.BlockSpec((None, m, d), lambda i: (i, 0, 0))
    wspec = pl.BlockSpec((d, d), lambda i: (0, 0))
    return pl.pallas_call(
        _mem_kv_kernel,
        grid=(b,),
        in_specs=[spec, pl.BlockSpec((1, d), lambda i: (0, 0)), wspec, wspec],
        out_specs=[spec, spec],
        out_shape=[jax.ShapeDtypeStruct((b, m, d), BF16)] * 2,
        compiler_params=_cparams(("parallel",)),
        name="mem_kv",
    )(mem, g, wk, wv)


def _cross_kernel(x_ref, g_ref, wq_ref, k_ref, v_ref, wo_ref, o_ref):
    x = x_ref[...]
    d = x.shape[1]
    dh = d // X_HEADS
    h = _rms(x, g_ref[...]).astype(BF16)
    q = (jnp.dot(h, wq_ref[...], preferred_element_type=F32) * (dh ** -0.5)).astype(BF16)
    outs = []
    for hh in range(X_HEADS):
        sl = slice(hh * dh, (hh + 1) * dh)
        s = lax.dot_general(q[:, sl], k_ref[:, sl], (((1,), (1,)), ((), ())),
                            preferred_element_type=F32)
        p = jnp.exp(s - jnp.max(s, axis=1, keepdims=True))
        l = jnp.sum(p, axis=1, keepdims=True)
        o = jnp.dot(p.astype(BF16), v_ref[:, sl], preferred_element_type=F32) / l
        outs.append(o.astype(BF16))
    o = jnp.concatenate(outs, axis=1)
    o_ref[...] = x + jnp.dot(o, wo_ref[...], preferred_element_type=F32)


def cross_attn(x2d, g, wq, kx, vx, wo, seq, tm=256):
    t, d = x2d.shape
    m = kx.shape[1]
    per_b = seq // tm
    kv = pl.BlockSpec((None, m, d), lambda i: (i // per_b, 0, 0))
    wspec = pl.BlockSpec((d, d), lambda i: (0, 0))
    return pl.pallas_call(
        _cross_kernel,
        grid=(t // tm,),
        in_specs=[pl.BlockSpec((tm, d), lambda i: (i, 0)), pl.BlockSpec((1, d), lambda i: (0, 0)),
                  wspec, kv, kv, wspec],
        out_specs=pl.BlockSpec((tm, d), lambda i: (i, 0)),
        out_shape=jax.ShapeDtypeStruct((t, d), F32),
        compiler_params=_cparams(("parallel",)),
        name="cross_attn",
    )(x2d, g, wq, kx, vx, wo)


def _topk_rows(sc, k):
    n = sc.shape[0]
    io = lax.broadcasted_iota(I32, sc.shape, 0)
    vals, ids = [], []
    for _ in range(k):
        m = jnp.max(sc, axis=0, keepdims=True)
        ix = jnp.min(jnp.where(sc == m, io, n), axis=0, keepdims=True)
        vals.append(m)
        ids.append(ix)
        sc = jnp.where(io == ix, NEG_INF, sc)
    return jnp.concatenate(vals, axis=0), jnp.concatenate(ids, axis=0)


def _pack_bf16_halves(h):
    bits = lax.bitcast_convert_type(h, I32)
    r = bits + 0x7FFF + (lax.shift_right_logical(bits, 16) & 1)
    half = h.shape[1] // 2
    return lax.shift_right_logical(r[:, :half], 16) | (r[:, half:] & HI_MASK)


def _route_kernel(x_ref, g_ref, wq_ref, sk_ref, hp_ref, idx_ref, w_ref, hb_ref, it_ref, wt_ref):
    p = pl.program_id(1)

    @pl.when(p == 0)
    def _():
        h = _rms(x_ref[...], g_ref[...])
        hp_ref[...] = _pack_bf16_halves(h)
        hb_ref[...] = h.astype(BF16)

    qh = jnp.dot(hb_ref[...], wq_ref[...], preferred_element_type=F32)
    tops = []
    for c in range(2):
        seg = qh[:, c * PEER_HALF:(c + 1) * PEER_HALF]
        sc = lax.dot_general(sk_ref[c], seg, (((1,), (1,)), ((), ())),
                             precision=lax.Precision.HIGHEST, preferred_element_type=F32)
        tops.append(_topk_rows(sc, PEER_TOPK))
    (s0, i0), (s1, i1) = tops
    k = PEER_TOPK
    sub = 8
    tm = s0.shape[1]
    r8 = lax.broadcasted_iota(I32, (sub, tm), 0)
    r16 = lax.broadcasted_iota(I32, (k, tm), 0)
    cand_b = [s0[0:1] + s1, s0[1:2] + s1[:sub]]
    cidx_b = [i0[0:1] * PEER_NKEYS + i1, i0[1:2] * PEER_NKEYS + i1[:sub]]
    pos_b = [r16, k + r8]
    for a in range(2, sub):
        keep = r8 < (k // (a + 1))
        cand_b.append(jnp.where(keep, s0[a:a + 1] + s1[:sub], NEG_INF))
        cidx_b.append(i0[a:a + 1] * PEER_NKEYS + i1[:sub])
        pos_b.append(a * k + r8)
    cand_b.append(s0[sub:] + s1[0:1])
    cidx_b.append(i0[sub:] * PEER_NKEYS + i1[0:1])
    pos_b.append((sub + r8) * k)
    cand = jnp.concatenate(cand_b, axis=0)
    cidx = jnp.concatenate(cidx_b, axis=0)
    pos = jnp.concatenate(pos_b, axis=0)
    vals, ids = [], []
    for _ in range(k):
        m = jnp.max(cand, axis=0, keepdims=True)
        px = jnp.min(jnp.where(cand == m, pos, k * k), axis=0, keepdims=True)
        hit = pos == px
        vals.append(m)
        ids.append(jnp.sum(jnp.where(hit, cidx, 0), axis=0, keepdims=True))
        cand = jnp.where(hit, NEG_INF, cand)
    sf = jnp.concatenate(vals, axis=0)
    e = jnp.exp(sf - sf[0:1])
    rows = pl.ds(pl.multiple_of(p * PEER_TOPK, PEER_TOPK), PEER_TOPK)
    wt_ref[rows, :] = e / jnp.sum(e, axis=0, keepdims=True)
    it_ref[rows, :] = jnp.concatenate(ids, axis=0)

    @pl.when(p == pl.num_programs(1) - 1)
    def _():
        idx_ref[...] = it_ref[...].T
        w_ref[...] = wt_ref[...].T


def peer_route(x2d, g, wq, sk, tok0, t, tm=256):
    d = x2d.shape[1]
    ph = sk.shape[0]
    nsel = ph * PEER_TOPK
    blk0 = tok0 // tm
    return pl.pallas_call(
        _route_kernel,
        grid=(t // tm, ph),
        in_specs=[
            pl.BlockSpec((tm, d), lambda i, p: (blk0 + i, 0)),
            pl.BlockSpec((1, d), lambda i, p: (0, 0)),
            pl.BlockSpec((d, 2 * PEER_HALF), lambda i, p: (0, p)),
            pl.BlockSpec((None, 2, PEER_NKEYS, PEER_HALF), lambda i, p: (p, 0, 0, 0)),
        ],
        out_specs=[
            pl.BlockSpec((tm, d // 2), lambda i, p: (i, 0)),
            pl.BlockSpec((tm, nsel), lambda i, p: (i, 0)),
            pl.BlockSpec((tm, nsel), lambda i, p: (i, 0)),
        ],
        out_shape=[jax.ShapeDtypeStruct((t, d // 2), I32),
                   jax.ShapeDtypeStruct((t, nsel), I32),
                   jax.ShapeDtypeStruct((t, nsel), F32)],
        scratch_shapes=[pltpu.VMEM((tm, d), BF16),
                        pltpu.VMEM((nsel, tm), I32),
                        pltpu.VMEM((nsel, tm), F32)],
        compiler_params=_cparams(("parallel", "arbitrary")),
        name="peer_route",
    )(x2d, g, wq, sk)


def _coef_kernel(w_ref, a_ref, o_ref):
    o_ref[...] = w_ref[...] * jax.nn.gelu(a_ref[...])


def peer_coef(w, act, tm=1024):
    t, n = w.shape
    spec = pl.BlockSpec((tm, n), lambda i: (i, 0))
    return pl.pallas_call(
        _coef_kernel, grid=(t // tm,), in_specs=[spec, spec], out_specs=spec,
        out_shape=jax.ShapeDtypeStruct((t, n), F32),
        compiler_params=_cparams(("parallel",)), name="peer_coef",
    )(w, act)


def _final_kernel(x_ref, y_ref, g_ref, o_ref):
    o_ref[...] = _rms(x_ref[...] + y_ref[...], g_ref[...])


def final_norm(x2d, y, g, tok0, tm=512):
    t, d = y.shape
    blk0 = tok0 // tm
    spec = pl.BlockSpec((tm, d), lambda i: (i, 0))
    return pl.pallas_call(
        _final_kernel, grid=(t // tm,),
        in_specs=[pl.BlockSpec((tm, d), lambda i: (blk0 + i, 0)), spec, pl.BlockSpec((1, d), lambda i: (0, 0))],
        out_specs=spec,
        out_shape=jax.ShapeDtypeStruct((t, d), F32),
        compiler_params=_cparams(("parallel",)), name="final_norm",
    )(x2d, y, g)


SC_CORES = 2
SC_SUBCORES = 16
SC_WORKERS = SC_CORES * SC_SUBCORES
SC_LANES = 16
SC_GROUP = 16


def _sc_mesh():
    return plsc.VectorSubcoreMesh(core_axis_name="c", subcore_axis_name="s")


def _sc_params():
    return pltpu.CompilerParams(needs_layout_passes=False)


def _sc_worker_id():
    return lax.axis_index("s") * SC_CORES + lax.axis_index("c")


SC_RING = 4
SC_ROW_SUB = 8
SC_ROW_LANE = 128


def _sc_ring(n_units, start, wait, compute):
    for u in range(SC_RING - 1):
        start(u, u)

    @pl.loop(0, n_units, step=SC_RING)
    def _(uu):
        for b in range(SC_RING):
            u = uu + b
            nxt = u + (SC_RING - 1)

            @pl.when(nxt < n_units)
            def _():
                start(nxt, (b + SC_RING - 1) % SC_RING)

            wait(u, b)
            compute(u, b)


def _sc_unit_off(u):
    off = u * SC_LANES
    return off if isinstance(off, int) else pl.multiple_of(off, SC_LANES)


def _sc_row_piece(rows, r, c):
    per = SC_ROW_LANE // SC_LANES
    return rows[r, c // per, pl.ds(pl.multiple_of((c % per) * SC_LANES, SC_LANES), SC_LANES)]


def peer_dots_sc(table, idx_flat, h):
    t, d = h.shape
    nsel = PEER_SEL
    tpw = t // SC_WORKERS
    g = SC_GROUP
    groups = tpw // g
    heads = nsel // SC_LANES
    pieces = d // SC_LANES
    units = g * heads
    row_buf = pltpu.VMEM((SC_LANES, SC_ROW_SUB, SC_ROW_LANE), F32)

    @functools.partial(
        pl.kernel, mesh=_sc_mesh(),
        out_type=jax.ShapeDtypeStruct((t * nsel,), F32),
        scratch_types=[
            pltpu.VMEM((g * nsel,), I32),
            pltpu.VMEM((g, d), F32),
            pltpu.VMEM((g * nsel,), F32),
            pltpu.VMEM((SC_LANES * SC_LANES,), F32),
            [row_buf] * SC_RING,
            [pltpu.SemaphoreType.DMA] * SC_RING,
        ],
        compiler_params=_sc_params(),
        name="peer_dots_sc",
    )
    def k(tab_hbm, idx_hbm, h_hbm, out_hbm, idx_v, h_v, out_v, red_v, rows, sems):
        wid = _sc_worker_id()
        lane = lax.iota(I32, SC_LANES)

        def copy(u, slot):
            ids = idx_v.at[pl.ds(_sc_unit_off(u), SC_LANES)]
            return pltpu.make_async_copy(tab_hbm.at[ids], rows[slot], sems[slot])

        def compute(u, slot):
            tt = u // heads

            def body(c, accs):
                hv = h_v[tt, pl.ds(pl.multiple_of(c * SC_LANES, SC_LANES), SC_LANES)]
                return tuple(accs[r] + _sc_row_piece(rows[slot], r, c) * hv for r in range(SC_LANES))

            accs = lax.fori_loop(0, pieces, body,
                                 tuple(jnp.zeros((SC_LANES,), F32) for _ in range(SC_LANES)))
            for r in range(SC_LANES):
                red_v[pl.ds(r * SC_LANES, SC_LANES)] = accs[r]
            cols = [plsc.load_gather(red_v, [lane * SC_LANES + j]) for j in range(SC_LANES)]
            while len(cols) > 1:
                cols = [cols[i] + cols[i + 1] for i in range(0, len(cols), 2)]
            out_v[pl.ds(_sc_unit_off(u), SC_LANES)] = cols[0]

        @pl.loop(0, groups)
        def _(gi):
            base = wid * tpw + gi * g
            pltpu.sync_copy(idx_hbm.at[pl.ds(base * nsel, g * nsel)], idx_v)
            pltpu.sync_copy(h_hbm.at[pl.ds(base, g)], h_v)
            _sc_ring(units, lambda u, s: copy(u, s).start(), lambda u, s: copy(u, s).wait(), compute)
            pltpu.sync_copy(out_v, out_hbm.at[pl.ds(base * nsel, g * nsel)])

    return k(table, idx_flat, h)


def peer_combine_sc(table, idx_flat, coef_flat, t):
    d = table.shape[1] * table.shape[2]
    nsel = PEER_SEL
    tpw = t // SC_WORKERS
    g = SC_GROUP
    groups = tpw // g
    heads = nsel // SC_LANES
    pieces = d // SC_LANES
    units = g * heads
    row_buf = pltpu.VMEM((SC_LANES, SC_ROW_SUB, SC_ROW_LANE), F32)

    @functools.partial(
        pl.kernel, mesh=_sc_mesh(),
        out_type=jax.ShapeDtypeStruct((t, d), F32),
        scratch_types=[
            pltpu.VMEM((g * nsel,), I32),
            pltpu.VMEM((g * nsel,), F32),
            pltpu.VMEM((g, d), F32),
            [row_buf] * SC_RING,
            [pltpu.SemaphoreType.DMA] * SC_RING,
        ],
        compiler_params=_sc_params(),
        name="peer_combine_sc",
    )
    def k(tab_hbm, idx_hbm, coef_hbm, out_hbm, idx_v, coef_v, y_v, rows, sems):
        wid = _sc_worker_id()

        def copy(u, slot):
            ids = idx_v.at[pl.ds(_sc_unit_off(u), SC_LANES)]
            return pltpu.make_async_copy(tab_hbm.at[ids], rows[slot], sems[slot])

        def compute(u, slot):
            tt = u // heads
            first = (u % heads) == 0
            cs = [plsc.load_gather(coef_v, [jnp.full((SC_LANES,), u * SC_LANES + r, I32)])
                  for r in range(SC_LANES)]

            @plsc.parallel_loop(0, pieces, unroll=2)
            def _(c):
                off = pl.multiple_of(c * SC_LANES, SC_LANES)
                terms = [cs[r] * _sc_row_piece(rows[slot], r, c) for r in range(SC_LANES)]
                while len(terms) > 1:
                    terms = [terms[i] + terms[i + 1] for i in range(0, len(terms), 2)]
                prev = y_v[tt, pl.ds(off, SC_LANES)]
                y_v[tt, pl.ds(off, SC_LANES)] = terms[0] + jnp.where(first, 0.0, prev)

        @pl.loop(0, groups)
        def _(gi):
            base = wid * tpw + gi * g
            pltpu.sync_copy(idx_hbm.at[pl.ds(base * nsel, g * nsel)], idx_v)
            pltpu.sync_copy(coef_hbm.at[pl.ds(base * nsel, g * nsel)], coef_v)
            _sc_ring(units, lambda u, s: copy(u, s).start(), lambda u, s: copy(u, s).wait(), compute)
            pltpu.sync_copy(y_v, out_hbm.at[pl.ds(base, g)])

    return k(table, idx_flat, coef_flat)


GELU_C0 = math.sqrt(2.0 / math.pi)
GELU_C1 = 0.044715


def _gelu_tanh(x):
    z = GELU_C0 * (x + GELU_C1 * (x * x * x))
    th = 1.0 - 2.0 / (jnp.exp(2.0 * z) + 1.0)
    return 0.5 * x * (1.0 + th)


def peer_experts_sc(tab_u, tab_v, idx_flat, w_flat, h):
    t, d = h.shape
    nsel = PEER_SEL
    tpw = t // SC_WORKERS
    g = SC_GROUP
    groups = tpw // g
    heads = nsel // SC_LANES
    pieces = d // SC_LANES
    units = g * heads
    row_buf = pltpu.VMEM((SC_LANES, SC_ROW_SUB, SC_ROW_LANE), F32)

    @functools.partial(
        pl.kernel, mesh=_sc_mesh(),
        out_type=jax.ShapeDtypeStruct((t, d), F32),
        scratch_types=[
            pltpu.VMEM((g * nsel,), I32),
            pltpu.VMEM((g * nsel,), F32),
            pltpu.VMEM((g, d), F32),
            pltpu.VMEM((g, d), F32),
            pltpu.VMEM((SC_LANES * SC_LANES,), F32),
            [row_buf] * SC_RING,
            [pltpu.SemaphoreType.DMA] * SC_RING,
        ],
        compiler_params=_sc_params(),
        name="peer_experts_sc",
    )
    def k(u_hbm, v_hbm, idx_hbm, w_hbm, h_hbm, out_hbm, idx_v, coef_v, h_v, y_v, red_v, rows, sems):
        wid = _sc_worker_id()
        lane = lax.iota(I32, SC_LANES)

        def copy(tab_hbm, u, slot):
            ids = idx_v.at[pl.ds(_sc_unit_off(u), SC_LANES)]
            return pltpu.make_async_copy(tab_hbm.at[ids], rows[slot], sems[slot])

        def dots(u, slot):
            tt = u // heads

            def body(c, accs):
                hv = h_v[tt, pl.ds(pl.multiple_of(c * SC_LANES, SC_LANES), SC_LANES)]
                return tuple(accs[r] + _sc_row_piece(rows[slot], r, c) * hv for r in range(SC_LANES))

            accs = lax.fori_loop(0, pieces, body,
                                 tuple(jnp.zeros((SC_LANES,), F32) for _ in range(SC_LANES)))
            for r in range(SC_LANES):
                red_v[pl.ds(r * SC_LANES, SC_LANES)] = accs[r]
            cols = [plsc.load_gather(red_v, [lane * SC_LANES + j]) for j in range(SC_LANES)]
            while len(cols) > 1:
                cols = [cols[i] + cols[i + 1] for i in range(0, len(cols), 2)]
            sl = pl.ds(_sc_unit_off(u), SC_LANES)
            coef_v[sl] = coef_v[sl] * _gelu_tanh(cols[0])

        def combine(u, slot):
            tt = u // heads
            first = (u % heads) == 0
            cs = [plsc.load_gather(coef_v, [jnp.full((SC_LANES,), u * SC_LANES + r, I32)])
                  for r in range(SC_LANES)]

            @plsc.parallel_loop(0, pieces, unroll=2)
            def _(c):
                off = pl.multiple_of(c * SC_LANES, SC_LANES)
                terms = [cs[r] * _sc_row_piece(rows[slot], r, c) for r in range(SC_LANES)]
                while len(terms) > 1:
                    terms = [terms[i] + terms[i + 1] for i in range(0, len(terms), 2)]
                prev = y_v[tt, pl.ds(off, SC_LANES)]
                y_v[tt, pl.ds(off, SC_LANES)] = terms[0] + jnp.where(first, 0.0, prev)

        @pl.loop(0, groups)
        def _(gi):
            base = wid * tpw + gi * g
            pltpu.sync_copy(idx_hbm.at[pl.ds(base * nsel, g * nsel)], idx_v)
            pltpu.sync_copy(w_hbm.at[pl.ds(base * nsel, g * nsel)], coef_v)
            pltpu.sync_copy(h_hbm.at[pl.ds(base, g)], h_v)
            _sc_ring(units, lambda u, s: copy(u_hbm, u, s).start(), lambda u, s: copy(u_hbm, u, s).wait(), dots)
            _sc_ring(units, lambda u, s: copy(v_hbm, u, s).start(), lambda u, s: copy(v_hbm, u, s).wait(), combine)
            pltpu.sync_copy(y_v, out_hbm.at[pl.ds(base, g)])

    return k(tab_u, tab_v, idx_flat, w_flat, h)


SC_PK_RING = 8
SC_PK_SUB = 4
HI_MASK = -65536


def pack_bf16_pairs(a):
    half = a.shape[1] // 2
    bits = lax.bitcast_convert_type(a.astype(BF16), jnp.uint16).astype(jnp.uint32)
    return lax.bitcast_convert_type(bits[:, :half] | (bits[:, half:] << 16), I32)


def _unpack_halves(x32):
    w = plsc.bitcast(x32, I32)
    return plsc.bitcast(w << 16, F32), plsc.bitcast(w & HI_MASK, F32)


def _tree_sum(xs):
    while len(xs) > 1:
        xs = [xs[i] + xs[i + 1] for i in range(0, len(xs), 2)]
    return xs[0]


def peer_experts_pk_sc(tab_u, tab_v, idx_flat, w_flat, hp, d):
    t = hp.shape[0]
    nsel = PEER_SEL
    tpw = t // SC_WORKERS
    g = SC_GROUP
    groups = tpw // g
    heads = nsel // SC_LANES
    chunks = d // 32
    units = g * heads
    ring = SC_PK_RING
    row_buf = pltpu.VMEM((SC_LANES, SC_PK_SUB, SC_ROW_LANE), I32)

    def row_words(rows, r, wc):
        per = SC_ROW_LANE // SC_LANES
        return plsc.bitcast(rows[r, wc // per, pl.ds(pl.multiple_of((wc % per) * SC_LANES, SC_LANES), SC_LANES)],
                            BF16)

    def ring_loop(n_units, start, wait, compute):
        for u in range(ring - 1):
            start(u, u)

        @pl.loop(0, n_units, step=ring)
        def _(uu):
            for b in range(ring):
                u = uu + b
                nxt = u + (ring - 1)

                @pl.when(nxt < n_units)
                def _():
                    start(nxt, (b + ring - 1) % ring)

                wait(u, b)
                compute(u, b)

    @functools.partial(
        pl.kernel, mesh=_sc_mesh(),
        out_type=jax.ShapeDtypeStruct((t, d), F32),
        scratch_types=[
            pltpu.VMEM((g * nsel,), I32),
            pltpu.VMEM((g * nsel,), F32),
            pltpu.VMEM((g, d // 2), I32),
            pltpu.VMEM((g, d), F32),
            pltpu.VMEM((SC_LANES * SC_LANES,), F32),
            [row_buf] * ring,
            [pltpu.SemaphoreType.DMA] * ring,
        ],
        compiler_params=_sc_params(),
        name="peer_experts_pk_sc",
    )
    def k(u_hbm, v_hbm, idx_hbm, w_hbm, h_hbm, out_hbm, idx_v, coef_v, h_v, y_v, red_v, rows, sems):
        wid = _sc_worker_id()
        lane = lax.iota(I32, SC_LANES)

        def copy(tab_hbm, u, slot):
            ids = idx_v.at[pl.ds(_sc_unit_off(u), SC_LANES)]
            return pltpu.make_async_copy(tab_hbm.at[ids], rows[slot], sems[slot])

        def dots(u, slot):
            tt = u // heads

            def body(cp, accs):
                out = []
                hv = [plsc.bitcast(h_v[tt, pl.ds(pl.multiple_of((2 * cp + i) * SC_LANES, SC_LANES), SC_LANES)], BF16)
                      for i in range(2)]
                for r in range(SC_LANES):
                    pr = row_words(rows[slot], r, 2 * cp) * hv[0] + row_words(rows[slot], r, 2 * cp + 1) * hv[1]
                    lo, hi = _unpack_halves(pr)
                    out.append(accs[r] + lo + hi)
                return tuple(out)

            accs = lax.fori_loop(0, chunks // 2, body,
                                 tuple(jnp.zeros((SC_LANES,), F32) for _ in range(SC_LANES)))
            for r in range(SC_LANES):
                red_v[pl.ds(r * SC_LANES, SC_LANES)] = accs[r]
            act = _tree_sum([plsc.load_gather(red_v, [lane * SC_LANES + j]) for j in range(SC_LANES)])
            sl = pl.ds(_sc_unit_off(u), SC_LANES)
            coef_v[sl] = coef_v[sl] * _gelu_tanh(act)

        def combine(u, slot):
            tt = u // heads
            first = (u % heads) == 0
            cb = []
            for r in range(SC_LANES):
                c = plsc.load_gather(coef_v, [jnp.full((SC_LANES,), u * SC_LANES + r, I32)])
                cb.append(plsc.pack(c, c, format=plsc.PackFormat.INTERLEAVED))

            @plsc.parallel_loop(0, chunks, unroll=2)
            def _(wc):
                lo, hi = _unpack_halves(_tree_sum([cb[r] * row_words(rows[slot], r, wc) for r in range(SC_LANES)]))
                for half, val in ((0, lo), (1, hi)):
                    sl = pl.ds(pl.multiple_of(half * (d // 2) + wc * SC_LANES, SC_LANES), SC_LANES)
                    y_v[tt, sl] = val + jnp.where(first, 0.0, y_v[tt, sl])

        @pl.loop(0, groups)
        def _(gi):
            base = wid * tpw + gi * g
            pltpu.sync_copy(idx_hbm.at[pl.ds(base * nsel, g * nsel)], idx_v)
            pltpu.sync_copy(w_hbm.at[pl.ds(base * nsel, g * nsel)], coef_v)
            pltpu.sync_copy(h_hbm.at[pl.ds(base, g)], h_v)
            ring_loop(units, lambda u, s: copy(u_hbm, u, s).start(), lambda u, s: copy(u_hbm, u, s).wait(), dots)
            ring_loop(units, lambda u, s: copy(v_hbm, u, s).start(), lambda u, s: copy(v_hbm, u, s).wait(), combine)
            pltpu.sync_copy(y_v, out_hbm.at[pl.ds(base, g)])

    return k(tab_u, tab_v, idx_flat, w_flat, hp)


def kernel(x, mem, rel_bias, ln_mix, w_in, hg_lower, hg_norm, w_up_a, w_up_b, w_out, ln_cross, ln_mem, wq_x, wk_x, wv_x, wo_x, ln_ffn, peer_query, peer_subkeys, peer_u, peer_v, ln_final):
    b, s, d = x.shape
    depth = w_in.shape[0]
    assert depth == 1, "the residual after PEER is fused into the final norm"
    assert s % MB_BLOCK == 0 and s % HG_CHUNK == 0 and s % (PEER_SLICES * SC_WORKERS * SC_GROUP) == 0
    nb = s // MB_BLOCK
    row = lambda a: a.reshape(1, -1).astype(F32)
    lb_all = jnp.cumsum(jax.nn.softmax(hg_lower.astype(F32), axis=0), axis=0)
    bias = moba_bias_tiles(rel_bias)
    n_hg = 4 * HG_WIDTH
    n_qk = 2 * MB_WIDTH
    n_mb = 3 * MB_WIDTH
    l = 0
    w = w_in[l].astype(BF16)
    w_hg, w_qk, w_vt, w_g = w[:, :n_hg], w[:, n_hg:n_hg + n_qk], w[:, n_hg + n_qk:n_hg + n_mb].T, w[:, n_hg + n_mb:]
    wa, wb, wo = w_up_a[l].astype(BF16), w_up_b[l].astype(BF16), w_out[l].astype(BF16)
    wqx, wox = wq_x[l].astype(BF16), wo_x[l].astype(BF16)
    wpq, sk = peer_query[l].astype(BF16), peer_subkeys[l].astype(F32)
    tab3 = lambda a: pack_bf16_pairs(a.astype(F32)).reshape(a.shape[0], SC_PK_SUB, SC_ROW_LANE)
    tab_u, tab_v = tab3(peer_u[l]), tab3(peer_v[l])
    kx, vx = mem_kv(mem, row(ln_mem[l]), wk_x[l].astype(BF16), wv_x[l].astype(BF16))

    outs = []
    for bi in range(b):
        x2d = x[bi]
        p0, pqk, vt, pg = in_proj(x2d, row(ln_mix[l]), w_hg, w_qk, w_vt, w_g)
        ya = hgrn2(p0, row(lb_all[l]), row(hg_norm[l]), 1, s)
        km = moba_kmean(pqk, 1, s).reshape(1, nb, MB_WIDTH)
        yb = moba_attention(pqk, vt, km, bias, 1, s)
        x2d = mix_out(x2d, ya, yb, pg, wa, wb, wo)
        x2d = cross_attn(x2d, row(ln_cross[l]), wqx, kx[bi:bi + 1], vx[bi:bi + 1], wox, s)
        unit = SC_WORKERS * SC_GROUP
        sizes = [s // PEER_SLICES] * PEER_SLICES
        if bi == 0:
            sizes = sizes[:-1] + [sizes[-1] - unit, unit]
        tok0 = 0
        for ts in sizes:
            hp, eidx, wts = peer_route(x2d, row(ln_ffn[l]), wpq, sk, tok0, ts)
            y = peer_experts_pk_sc(tab_u, tab_v, eidx.reshape(ts * PEER_SEL), wts.reshape(ts * PEER_SEL), hp, d)
            outs.append(final_norm(x2d, y, row(ln_final), tok0))
            tok0 += ts
    return jnp.concatenate(outs, axis=0).reshape(b, s, d)
```

```python
import functools
import math

import jax
import jax.numpy as jnp
import numpy as np
from jax import lax
from jax.experimental import pallas as pl
from jax.experimental.pallas import tpu as pltpu
from jax.experimental.pallas import tpu_sc as plsc

F32 = jnp.float32
BF16 = jnp.bfloat16
I32 = jnp.int32
EPS = 1e-6
NEG_INF = float("-inf")

HG_HEADS = 4
HG_D = 128
HG_WIDTH = HG_HEADS * HG_D
HG_CHUNK = 64
HG_SUB = 16
MB_HEADS = 8
MB_DH = 64
MB_WIDTH = MB_HEADS * MB_DH
MB_BLOCK = 256
MB_TOPK = 3
MB_BIAS_TILES = 8
REL_BUCKETS = 32
REL_MAX_DIST = 2048
X_HEADS = 4
PEER_HEADS = 8
PEER_NKEYS = 128
PEER_TOPK = 16
PEER_HALF = 128
PEER_SEL = PEER_HEADS * PEER_TOPK
PEER_SLICES = 4

VMEM_LIMIT = 56 * 1024 * 1024


def _cparams(sem):
    return pltpu.CompilerParams(dimension_semantics=sem, vmem_limit_bytes=VMEM_LIMIT)


def _rms(x, g):
    ms = jnp.mean(x * x, axis=-1, keepdims=True)
    return x * lax.rsqrt(ms + EPS) * g


def _in_proj_kernel(x_ref, g_ref, w0_ref, w1_ref, wvt_ref, w2_ref, o0_ref, o1_ref, ovt_ref, o2_ref):
    h = _rms(x_ref[...], g_ref[...]).astype(BF16)
    o0_ref[...] = jnp.dot(h, w0_ref[...], preferred_element_type=F32)
    o1_ref[...] = jnp.dot(h, w1_ref[...], preferred_element_type=F32).astype(BF16)
    ovt_ref[0] = lax.dot_general(wvt_ref[...], h, (((1,), (1,)), ((), ())),
                                 preferred_element_type=F32).astype(BF16)
    o2_ref[...] = jnp.dot(h, w2_ref[...], preferred_element_type=F32).astype(BF16)


def in_proj(x2d, g, w0, w1, wvt, w2):
    t, d = x2d.shape
    tm = MB_BLOCK
    n0, n1, nv, n2 = w0.shape[1], w1.shape[1], wvt.shape[0], w2.shape[1]
    full = lambda a: pl.BlockSpec(a.shape, lambda i: (0, 0))
    return pl.pallas_call(
        _in_proj_kernel,
        grid=(t // tm,),
        in_specs=[pl.BlockSpec((tm, d), lambda i: (i, 0)), full(g), full(w0), full(w1), full(wvt), full(w2)],
        out_specs=[pl.BlockSpec((tm, n0), lambda i: (i, 0)),
                   pl.BlockSpec((tm, n1), lambda i: (i, 0)),
                   pl.BlockSpec((1, nv, tm), lambda i: (i, 0, 0)),
                   pl.BlockSpec((tm, n2), lambda i: (i, 0))],
        out_shape=[jax.ShapeDtypeStruct((t, n0), F32),
                   jax.ShapeDtypeStruct((t, n1), BF16),
                   jax.ShapeDtypeStruct((t // tm, nv, tm), BF16),
                   jax.ShapeDtypeStruct((t, n2), BF16)],
        compiler_params=_cparams(("parallel",)),
        name="in_proj",
    )(x2d, g, w0, w1, wvt, w2)


def _hgrn_kernel(q_ref, f_ref, i_ref, g_ref, lb_ref, gain_ref, o_ref, st_ref):
    c = pl.program_id(1)

    @pl.when(c == 0)
    def _():
        st_ref[...] = jnp.zeros_like(st_ref)

    C, S = HG_CHUNK, HG_SUB
    row = lax.broadcasted_iota(I32, (C, C), 0)
    col = lax.broadcasted_iota(I32, (C, C), 1)
    tril = (row >= col).astype(F32)
    t_iota = lax.broadcasted_iota(I32, (S, 1), 0)

    for h in range(HG_HEADS):
        sl = slice(h * HG_D, (h + 1) * HG_D)
        q = q_ref[:, sl]
        v = i_ref[:, sl]
        lb = lb_ref[:, sl]
        f = lb + (1.0 - lb) * jax.nn.sigmoid(f_ref[:, sl])
        lf = jnp.log(f)
        k = 1.0 - f
        b = jnp.dot(tril, lf, precision=lax.Precision.HIGHEST, preferred_element_type=F32)
        st = st_ref[h]
        vb = v.astype(BF16)
        qd = (q * jnp.exp(b)).astype(BF16)
        o_inter = lax.dot_general(qd, st.astype(BF16), (((1,), (1,)), ((), ())),
                                  preferred_element_type=F32)
        outs = []
        for i in range(C // S):
            r0 = i * S
            qi = q[r0:r0 + S]
            ki = k[r0:r0 + S]
            bi = b[r0:r0 + S]
            vi = v[r0:r0 + S]
            oi = o_inter[r0:r0 + S]
            if i > 0:
                bs = b[r0 - 1:r0]
                qh = (qi * jnp.exp(bi - bs)).astype(BF16)
                kh = (k[:r0] * jnp.exp(bs - b[:r0])).astype(BF16)
                a = lax.dot_general(qh, kh, (((1,), (1,)), ((), ())), preferred_element_type=F32)
                oi = oi + jnp.dot(a.astype(BF16), vb[:r0], preferred_element_type=F32)
            for s in range(S):
                dec = jnp.exp(jnp.minimum(bi - bi[s:s + 1], 0.0))
                p = qi * ki[s:s + 1] * dec
                a_s = jnp.sum(p, axis=-1, keepdims=True)
                a_s = jnp.where(t_iota >= s, a_s, 0.0)
                oi = oi + a_s * vi[s:s + 1]
            outs.append(oi)
        o = jnp.concatenate(outs, axis=0)
        b_end = b[C - 1:C]
        kd = (k * jnp.exp(b_end - b)).astype(BF16)
        upd = lax.dot_general(vb, kd, (((0,), (0,)), ((), ())), preferred_element_type=F32)
        st_ref[h] = st * jnp.exp(b_end) + upd
        o = o * lax.rsqrt(jnp.mean(o * o, axis=-1, keepdims=True) + EPS)
        g = g_ref[:, sl]
        o_ref[:, sl] = (o * gain_ref[:, sl] * (g * jax.nn.sigmoid(g))).astype(o_ref.dtype)


def hgrn2(p0, lb, gain, batch, seq):
    t = p0.shape[0]
    nc = seq // HG_CHUNK
    w = HG_WIDTH

    def col(j):
        return pl.BlockSpec((HG_CHUNK, w), lambda b, c, j=j: (b * nc + c, j))

    return pl.pallas_call(
        _hgrn_kernel,
        grid=(batch, nc),
        in_specs=[col(0), col(1), col(2), col(3),
                  pl.BlockSpec((1, w), lambda b, c: (0, 0)),
                  pl.BlockSpec((1, w), lambda b, c: (0, 0))],
        out_specs=pl.BlockSpec((HG_CHUNK, w), lambda b, c: (b * nc + c, 0)),
        out_shape=jax.ShapeDtypeStruct((t, w), BF16),
        scratch_shapes=[pltpu.VMEM((HG_HEADS, HG_D, HG_D), F32)],
        compiler_params=_cparams(("parallel", "arbitrary")),
        name="hgrn2",
    )(p0, p0, p0, p0, lb, gain)


def _kmean_kernel(k_ref, o_ref):
    o_ref[0] = jnp.mean(k_ref[...].astype(F32), axis=0, keepdims=True)


def moba_kmean(p1, batch, seq):
    nbt = p1.shape[0] // MB_BLOCK
    return pl.pallas_call(
        _kmean_kernel,
        grid=(nbt,),
        in_specs=[pl.BlockSpec((MB_BLOCK, MB_WIDTH), lambda i: (i, 1))],
        out_specs=pl.BlockSpec((1, 1, MB_WIDTH), lambda i: (i, 0, 0)),
        out_shape=jax.ShapeDtypeStruct((nbt, 1, MB_WIDTH), F32),
        compiler_params=_cparams(("parallel",)),
        name="moba_kmean",
    )(p1)


MB_PAIR = 4
MB_PW = MB_PAIR * MB_DH
MB_LG = 128


def _moba_kernel(q_ref, k_ref, vt_ref, km_ref, bias_ref, o_ref, *scratch, qb0):
    m_ref, l_ref, al_ref, acc_ref, msk_ref, s_ref, p_ref = (
        scratch[i * MB_PAIR:(i + 1) * MB_PAIR] for i in range(7))
    qi = pl.program_id(2) + qb0
    nb = km_ref.shape[0]
    blk = MB_BLOCK
    heads = range(MB_PAIR)
    grp = lambda hh: slice((hh // 2) * MB_LG, (hh // 2 + 1) * MB_LG)
    q = q_ref[...]
    lane = lax.broadcasted_iota(I32, (blk, MB_LG), 1)
    in_head = [(lane < MB_DH) if hh % 2 == 0 else (lane >= MB_DH) for hh in heads]
    qs = q * jnp.asarray(MB_DH ** -0.5, BF16)
    qh = [jnp.where(in_head[hh], qs[:, grp(hh)], jnp.zeros((blk, MB_LG), BF16)) for hh in heads]
    nt = (((1,), (1,)), ((), ()))

    qf = q.astype(F32)
    n_io = lax.broadcasted_iota(I32, (nb, blk), 0)
    for hh in heads:
        gate = lax.dot_general(km_ref[:, grp(hh)], jnp.where(in_head[hh], qf[:, grp(hh)], 0.0), nt,
                               precision=lax.Precision.HIGHEST, preferred_element_type=F32)
        gate = jnp.where(n_io < qi, gate, NEG_INF)
        chosen = n_io < 0
        for _ in range(MB_TOPK):
            mx = jnp.max(gate, axis=0, keepdims=True)
            ix = jnp.min(jnp.where(gate == mx, n_io, nb), axis=0, keepdims=True)
            hit = n_io == ix
            chosen = chosen | (hit & (mx > NEG_INF))
            gate = jnp.where(hit, NEG_INF, gate)
        msk_ref[hh][...] = jnp.where(chosen, 0.0, NEG_INF)

    own_rows = lambda r, hh: r[(hh % 2) * MB_DH:(hh % 2 + 1) * MB_DH]

    def pv_stage(blk_idx):
        vtb = vt_ref[blk_idx]
        r = [jnp.dot(vtb[grp(hh)], p_ref[hh][...], preferred_element_type=F32) for hh in heads]
        return [al_ref[hh][...] * acc_ref[hh][...] + own_rows(r[hh], hh) for hh in heads]

    def softmax_stage():
        s = [s_ref[hh][...] for hh in heads]
        m_old = [m_ref[hh][...] for hh in heads]
        l_old = [l_ref[hh][...] for hh in heads]
        m_new = [jnp.maximum(m_old[hh], jnp.max(s[hh], axis=0, keepdims=True)) for hh in heads]
        alpha = [jnp.exp(m_old[hh] - m_new[hh]) for hh in heads]
        p = [jnp.exp(s[hh] - m_new[hh]) for hh in heads]
        l_new = [alpha[hh] * l_old[hh] + jnp.sum(p[hh], axis=0, keepdims=True) for hh in heads]
        return [x.astype(BF16) for x in p], alpha, m_new, l_new

    def store_softmax(p, alpha, m_new, l_new):
        for hh in heads:
            p_ref[hh][...] = p[hh]
            al_ref[hh][...] = alpha[hh]
            m_ref[hh][...] = m_new[hh]
            l_ref[hh][...] = l_new[hh]

    k_own = k_ref[pl.ds(pl.multiple_of(qi * blk, blk), blk), :]
    key_io = lax.broadcasted_iota(I32, (blk, blk), 0)
    qry_io = lax.broadcasted_iota(I32, (blk, blk), 1)
    for hh in heads:
        s = lax.dot_general(k_own[:, grp(hh)], qh[hh], nt, preferred_element_type=F32) + bias_ref[hh, 0]
        s_ref[hh][...] = jnp.where(key_io <= qry_io, s, NEG_INF)
        m_ref[hh][...] = jnp.full((1, blk), NEG_INF, F32)
        l_ref[hh][...] = jnp.zeros((1, blk), F32)
        al_ref[hh][...] = jnp.ones((1, blk), F32)
        acc_ref[hh][...] = jnp.zeros((MB_DH, blk), F32)
        p_ref[hh][...] = jnp.zeros((blk, blk), BF16)

    def step(i, carry):
        a_new = pv_stage(jnp.where(i <= 1, qi, i - 2))
        sm = softmax_stage()
        kn = k_ref[pl.ds(pl.multiple_of(i * blk, blk), blk), :]
        d = jnp.minimum(qi - i, MB_BIAS_TILES - 1)
        s_next = [lax.dot_general(kn[:, grp(hh)], qh[hh], nt, preferred_element_type=F32)
                  + bias_ref[hh, d] + msk_ref[hh][pl.ds(i, 1), :] for hh in heads]
        for hh in heads:
            acc_ref[hh][...] = a_new[hh]
            s_ref[hh][...] = s_next[hh]
        store_softmax(*sm)
        return carry

    lax.fori_loop(0, qi, step, 0)
    a_new = pv_stage(jnp.where(qi <= 1, qi, qi - 2))
    sm = softmax_stage()
    for hh in heads:
        acc_ref[hh][...] = a_new[hh]
    store_softmax(*sm)
    a_fin = pv_stage(jnp.where(qi == 0, qi, qi - 1))
    out_t = jnp.concatenate([a_fin[hh] / l_ref[hh][...] for hh in heads], axis=0)
    o_ref[...] = out_t.T.astype(o_ref.dtype)


def moba_attention(pqk, vt, km, bias, batch, seq, qb0=0, nqb=None):
    nb = seq // MB_BLOCK
    nqb = nb if nqb is None else nqb
    t = batch * nqb * MB_BLOCK
    groups = MB_WIDTH // MB_PW
    return pl.pallas_call(
        functools.partial(_moba_kernel, qb0=qb0),
        grid=(batch, groups, nqb),
        in_specs=[
            pl.BlockSpec((MB_BLOCK, MB_PW), lambda b, j, i: (b * nb + qb0 + i, j)),
            pl.BlockSpec((seq, MB_PW), lambda b, j, i: (b, groups + j)),
            pl.BlockSpec((nb, MB_PW, MB_BLOCK), lambda b, j, i: (b, j, 0)),
            pl.BlockSpec((None, nb, MB_PW), lambda b, j, i: (b, 0, j)),
            pl.BlockSpec((MB_PAIR, MB_BIAS_TILES, MB_BLOCK, MB_BLOCK), lambda b, j, i: (j, 0, 0, 0)),
        ],
        out_specs=pl.BlockSpec((MB_BLOCK, MB_PW), lambda b, j, i: (b * nqb + i, j)),
        out_shape=jax.ShapeDtypeStruct((t, MB_WIDTH), BF16),
        scratch_shapes=(
            [pltpu.VMEM((1, MB_BLOCK), F32)] * (3 * MB_PAIR)
            + [pltpu.VMEM((MB_DH, MB_BLOCK), F32)] * MB_PAIR
            + [pltpu.VMEM((nb, MB_BLOCK), F32)] * MB_PAIR
            + [pltpu.VMEM((MB_BLOCK, MB_BLOCK), F32)] * MB_PAIR
            + [pltpu.VMEM((MB_BLOCK, MB_BLOCK), BF16)] * MB_PAIR
        ),
        compiler_params=_cparams(("parallel", "parallel", "arbitrary")),
        name="moba_attn",
    )(pqk, pqk, vt, km, bias)


def _t5_bucket(dist):
    max_exact = REL_BUCKETS // 2
    scaled = jnp.log(jnp.maximum(dist, 1).astype(F32) / max_exact) / math.log(REL_MAX_DIST / max_exact)
    large = jnp.minimum(max_exact + (scaled * (REL_BUCKETS - max_exact)).astype(I32), REL_BUCKETS - 1)
    return jnp.where(dist < max_exact, dist, large)


def moba_bias_tiles(rel_bias):
    blk = MB_BLOCK
    span = 2 * blk - 1
    x = jnp.arange(span) - (blk - 1)
    dist = jnp.maximum(jnp.arange(MB_BIAS_TILES)[:, None] * blk + x[None, :], 0)
    w = rel_bias.astype(F32).T[:, _t5_bucket(dist)]
    h = w.shape[0]
    wp = jnp.pad(w, ((0, 0), (0, 0), (0, 1)))
    a = jnp.broadcast_to(wp[:, :, None, :], (h, MB_BIAS_TILES, blk, span + 1))
    a = a.reshape(h, MB_BIAS_TILES, blk * (span + 1))[:, :, :blk * span]
    return a.reshape(h, MB_BIAS_TILES, blk, span)[:, :, :, blk - 1:]


def _mix_kernel(x_ref, ya_ref, yb_ref, ga_ref, gb_ref, wa_ref, wb_ref, wo_ref, o_ref):
    za = jnp.dot(ya_ref[...], wa_ref[...], preferred_element_type=F32)
    zb = jnp.dot(yb_ref[...], wb_ref[...], preferred_element_type=F32)
    z = jax.nn.sigmoid(ga_ref[...].astype(F32)) * za + jax.nn.sigmoid(gb_ref[...].astype(F32)) * zb
    o_ref[...] = x_ref[...] + jnp.dot(z.astype(BF16), wo_ref[...], preferred_element_type=F32)


def mix_out(x2d, ya, yb, pg, wa, wb, wo, tok0=0, tm=256):
    t = yb.shape[0]
    d = x2d.shape[1]
    w = ya.shape[1]
    b0 = tok0 // tm
    return pl.pallas_call(
        _mix_kernel,
        grid=(t // tm,),
        in_specs=[
            pl.BlockSpec((tm, d), lambda i: (b0 + i, 0)),
            pl.BlockSpec((tm, w), lambda i: (b0 + i, 0)),
            pl.BlockSpec((tm, w), lambda i: (i, 0)),
            pl.BlockSpec((tm, d), lambda i: (b0 + i, 0)),
            pl.BlockSpec((tm, d), lambda i: (b0 + i, 1)),
            pl.BlockSpec((w, d), lambda i: (0, 0)),
            pl.BlockSpec((w, d), lambda i: (0, 0)),
            pl.BlockSpec((d, d), lambda i: (0, 0)),
        ],
        out_specs=pl.BlockSpec((tm, d), lambda i: (i, 0)),
        out_shape=jax.ShapeDtypeStruct((t, d), F32),
        compiler_params=_cparams(("parallel",)),
        name="mix_out",
    )(x2d, ya, yb, pg, pg, wa, wb, wo)


def _mem_kv_kernel(m_ref, g_ref, wk_ref, wv_ref, k_ref, v_ref):
    mn = _rms(m_ref[...], g_ref[...]).astype(BF16)
    k_ref[...] = jnp.dot(mn, wk_ref[...], preferred_element_type=F32).astype(BF16)
    v_ref[...] = jnp.dot(mn, wv_ref[...], preferred_element_type=F32).astype(BF16)


def mem_kv(mem, g, wk, wv):
    b, m, d = mem.shape
    spec = pl.BlockSpec((None, m, d), lambda i: (i, 0, 0))
    wspec = pl.BlockSpec((d, d), lambda i: (0, 0))
    return pl.pallas_call(
        _mem_kv_kernel,
        grid=(b,),
        in_specs=[spec, pl.BlockSpec((1, d), lambda i: (0, 0)), wspec, wspec],
        out_specs=[spec, spec],
        out_shape=[jax.ShapeDtypeStruct((b, m, d), BF16)] * 2,
        compiler_params=_cparams(("parallel",)),
        name="mem_kv",
    )(mem, g, wk, wv)


def _cross_kernel(x_ref, g_ref, wq_ref, k_ref, v_ref, wo_ref, o_ref):
    x = x_ref[...]
    d = x.shape[1]
    dh = d // X_HEADS
    h = _rms(x, g_ref[...]).astype(BF16)
    q = (jnp.dot(h, wq_ref[...], preferred_element_type=F32) * (dh ** -0.5)).astype(BF16)
    outs = []
    for hh in range(X_HEADS):
        sl = slice(hh * dh, (hh + 1) * dh)
        s = lax.dot_general(q[:, sl], k_ref[:, sl], (((1,), (1,)), ((), ())),
                            preferred_element_type=F32)
        p = jnp.exp(s - jnp.max(s, axis=1, keepdims=True))
        l = jnp.sum(p, axis=1, keepdims=True)
        o = jnp.dot(p.astype(BF16), v_ref[:, sl], preferred_element_type=F32) / l
        outs.append(o.astype(BF16))
    o = jnp.concatenate(outs, axis=1)
    o_ref[...] = x + jnp.dot(o, wo_ref[...], preferred_element_type=F32)


def cross_attn(x2d, g, wq, kx, vx, wo, seq, tm=256):
    t, d = x2d.shape
    m = kx.shape[1]
    per_b = seq // tm
    kv = pl.BlockSpec((None, m, d), lambda i: (i // per_b, 0, 0))
    wspec = pl.BlockSpec((d, d), lambda i: (0, 0))
    return pl.pallas_call(
        _cross_kernel,
        grid=(t // tm,),
        in_specs=[pl.BlockSpec((tm, d), lambda i: (i, 0)), pl.BlockSpec((1, d), lambda i: (0, 0)),
                  wspec, kv, kv, wspec],
        out_specs=pl.BlockSpec((tm, d), lambda i: (i, 0)),
        out_shape=jax.ShapeDtypeStruct((t, d), F32),
        compiler_params=_cparams(("parallel",)),
        name="cross_attn",
    )(x2d, g, wq, kx, vx, wo)


def _topk_rows(sc, k):
    n = sc.shape[0]
    io = lax.broadcasted_iota(I32, sc.shape, 0)
    vals, ids = [], []
    for _ in range(k):
        m = jnp.max(sc, axis=0, keepdims=True)
        ix = jnp.min(jnp.where(sc == m, io, n), axis=0, keepdims=True)
        vals.append(m)
        ids.append(ix)
        sc = jnp.where(io == ix, NEG_INF, sc)
    return jnp.concatenate(vals, axis=0), jnp.concatenate(ids, axis=0)


def _pack_bf16_halves(h):
    bits = lax.bitcast_convert_type(h, I32)
    r = bits + 0x7FFF + (lax.shift_right_logical(bits, 16) & 1)
    half = h.shape[1] // 2
    return lax.shift_right_logical(r[:, :half], 16) | (r[:, half:] & HI_MASK)


def _route_kernel(x_ref, g_ref, wq_ref, sk_ref, hp_ref, idx_ref, w_ref, hb_ref, it_ref, wt_ref):
    p = pl.program_id(1)

    @pl.when(p == 0)
    def _():
        h = _rms(x_ref[...], g_ref[...])
        hp_ref[...] = _pack_bf16_halves(h)
        hb_ref[...] = h.astype(BF16)

    qh = jnp.dot(hb_ref[...], wq_ref[...], preferred_element_type=F32)
    tops = []
    for c in range(2):
        seg = qh[:, c * PEER_HALF:(c + 1) * PEER_HALF]
        sc = lax.dot_general(sk_ref[c], seg, (((1,), (1,)), ((), ())),
                             precision=lax.Precision.HIGHEST, preferred_element_type=F32)
        tops.append(_topk_rows(sc, PEER_TOPK))
    (s0, i0), (s1, i1) = tops
    k = PEER_TOPK
    sub = 8
    tm = s0.shape[1]
    r8 = lax.broadcasted_iota(I32, (sub, tm), 0)
    r16 = lax.broadcasted_iota(I32, (k, tm), 0)
    cand_b = [s0[0:1] + s1, s0[1:2] + s1[:sub]]
    cidx_b = [i0[0:1] * PEER_NKEYS + i1, i0[1:2] * PEER_NKEYS + i1[:sub]]
    pos_b = [r16, k + r8]
    for a in range(2, sub):
        keep = r8 < (k // (a + 1))
        cand_b.append(jnp.where(keep, s0[a:a + 1] + s1[:sub], NEG_INF))
        cidx_b.append(i0[a:a + 1] * PEER_NKEYS + i1[:sub])
        pos_b.append(a * k + r8)
    cand_b.append(s0[sub:] + s1[0:1])
    cidx_b.append(i0[sub:] * PEER_NKEYS + i1[0:1])
    pos_b.append((sub + r8) * k)
    cand = jnp.concatenate(cand_b, axis=0)
    cidx = jnp.concatenate(cidx_b, axis=0)
    pos = jnp.concatenate(pos_b, axis=0)
    vals, ids = [], []
    for _ in range(k):
        m = jnp.max(cand, axis=0, keepdims=True)
        px = jnp.min(jnp.where(cand == m, pos, k * k), axis=0, keepdims=True)
        hit = pos == px
        vals.append(m)
        ids.append(jnp.sum(jnp.where(hit, cidx, 0), axis=0, keepdims=True))
        cand = jnp.where(hit, NEG_INF, cand)
    sf = jnp.concatenate(vals, axis=0)
    e = jnp.exp(sf - sf[0:1])
    rows = pl.ds(pl.multiple_of(p * PEER_TOPK, PEER_TOPK), PEER_TOPK)
    wt_ref[rows, :] = e / jnp.sum(e, axis=0, keepdims=True)
    it_ref[rows, :] = jnp.concatenate(ids, axis=0)

    @pl.when(p == pl.num_programs(1) - 1)
    def _():
        idx_ref[...] = it_ref[...].T
        w_ref[...] = wt_ref[...].T


def peer_route(x2d, g, wq, sk, tok0, t, tm=256):
    d = x2d.shape[1]
    ph = sk.shape[0]
    nsel = ph * PEER_TOPK
    blk0 = tok0 // tm
    return pl.pallas_call(
        _route_kernel,
        grid=(t // tm, ph),
        in_specs=[
            pl.BlockSpec((tm, d), lambda i, p: (blk0 + i, 0)),
            pl.BlockSpec((1, d), lambda i, p: (0, 0)),
            pl.BlockSpec((d, 2 * PEER_HALF), lambda i, p: (0, p)),
            pl.BlockSpec((None, 2, PEER_NKEYS, PEER_HALF), lambda i, p: (p, 0, 0, 0)),
        ],
        out_specs=[
            pl.BlockSpec((tm, d // 2), lambda i, p: (i, 0)),
            pl.BlockSpec((tm, nsel), lambda i, p: (i, 0)),
            pl.BlockSpec((tm, nsel), lambda i, p: (i, 0)),
        ],
        out_shape=[jax.ShapeDtypeStruct((t, d // 2), I32),
                   jax.ShapeDtypeStruct((t, nsel), I32),
                   jax.ShapeDtypeStruct((t, nsel), F32)],
        scratch_shapes=[pltpu.VMEM((tm, d), BF16),
                        pltpu.VMEM((nsel, tm), I32),
                        pltpu.VMEM((nsel, tm), F32)],
        compiler_params=_cparams(("parallel", "arbitrary")),
        name="peer_route",
    )(x2d, g, wq, sk)


def _coef_kernel(w_ref, a_ref, o_ref):
    o_ref[...] = w_ref[...] * jax.nn.gelu(a_ref[...])


def peer_coef(w, act, tm=1024):
    t, n = w.shape
    spec = pl.BlockSpec((tm, n), lambda i: (i, 0))
    return pl.pallas_call(
        _coef_kernel, grid=(t // tm,), in_specs=[spec, spec], out_specs=spec,
        out_shape=jax.ShapeDtypeStruct((t, n), F32),
        compiler_params=_cparams(("parallel",)), name="peer_coef",
    )(w, act)


def _final_kernel(x_ref, y_ref, g_ref, o_ref):
    o_ref[...] = _rms(x_ref[...] + y_ref[...], g_ref[...])


def final_norm(x2d, y, g, tok0, tm=512):
    t, d = y.shape
    blk0 = tok0 // tm
    spec = pl.BlockSpec((tm, d), lambda i: (i, 0))
    return pl.pallas_call(
        _final_kernel, grid=(t // tm,),
        in_specs=[pl.BlockSpec((tm, d), lambda i: (blk0 + i, 0)), spec, pl.BlockSpec((1, d), lambda i: (0, 0))],
        out_specs=spec,
        out_shape=jax.ShapeDtypeStruct((t, d), F32),
        compiler_params=_cparams(("parallel",)), name="final_norm",
    )(x2d, y, g)


SC_CORES = 2
SC_SUBCORES = 16
SC_WORKERS = SC_CORES * SC_SUBCORES
SC_LANES = 16
SC_GROUP = 16


def _sc_mesh():
    return plsc.VectorSubcoreMesh(core_axis_name="c", subcore_axis_name="s")


def _sc_params():
    return pltpu.CompilerParams(needs_layout_passes=False)


def _sc_worker_id():
    return lax.axis_index("s") * SC_CORES + lax.axis_index("c")


SC_RING = 4
SC_ROW_SUB = 8
SC_ROW_LANE = 128


def _sc_ring(n_units, start, wait, compute):
    for u in range(SC_RING - 1):
        start(u, u)

    @pl.loop(0, n_units, step=SC_RING)
    def _(uu):
        for b in range(SC_RING):
            u = uu + b
            nxt = u + (SC_RING - 1)

            @pl.when(nxt < n_units)
            def _():
                start(nxt, (b + SC_RING - 1) % SC_RING)

            wait(u, b)
            compute(u, b)


def _sc_unit_off(u):
    off = u * SC_LANES
    return off if isinstance(off, int) else pl.multiple_of(off, SC_LANES)


def _sc_row_piece(rows, r, c):
    per = SC_ROW_LANE // SC_LANES
    return rows[r, c // per, pl.ds(pl.multiple_of((c % per) * SC_LANES, SC_LANES), SC_LANES)]


def peer_dots_sc(table, idx_flat, h):
    t, d = h.shape
    nsel = PEER_SEL
    tpw = t // SC_WORKERS
    g = SC_GROUP
    groups = tpw // g
    heads = nsel // SC_LANES
    pieces = d // SC_LANES
    units = g * heads
    row_buf = pltpu.VMEM((SC_LANES, SC_ROW_SUB, SC_ROW_LANE), F32)

    @functools.partial(
        pl.kernel, mesh=_sc_mesh(),
        out_type=jax.ShapeDtypeStruct((t * nsel,), F32),
        scratch_types=[
            pltpu.VMEM((g * nsel,), I32),
            pltpu.VMEM((g, d), F32),
            pltpu.VMEM((g * nsel,), F32),
            pltpu.VMEM((SC_LANES * SC_LANES,), F32),
            [row_buf] * SC_RING,
            [pltpu.SemaphoreType.DMA] * SC_RING,
        ],
        compiler_params=_sc_params(),
        name="peer_dots_sc",
    )
    def k(tab_hbm, idx_hbm, h_hbm, out_hbm, idx_v, h_v, out_v, red_v, rows, sems):
        wid = _sc_worker_id()
        lane = lax.iota(I32, SC_LANES)

        def copy(u, slot):
            ids = idx_v.at[pl.ds(_sc_unit_off(u), SC_LANES)]
            return pltpu.make_async_copy(tab_hbm.at[ids], rows[slot], sems[slot])

        def compute(u, slot):
            tt = u // heads

            def body(c, accs):
                hv = h_v[tt, pl.ds(pl.multiple_of(c * SC_LANES, SC_LANES), SC_LANES)]
                return tuple(accs[r] + _sc_row_piece(rows[slot], r, c) * hv for r in range(SC_LANES))

            accs = lax.fori_loop(0, pieces, body,
                                 tuple(jnp.zeros((SC_LANES,), F32) for _ in range(SC_LANES)))
            for r in range(SC_LANES):
                red_v[pl.ds(r * SC_LANES, SC_LANES)] = accs[r]
            cols = [plsc.load_gather(red_v, [lane * SC_LANES + j]) for j in range(SC_LANES)]
            while len(cols) > 1:
                cols = [cols[i] + cols[i + 1] for i in range(0, len(cols), 2)]
            out_v[pl.ds(_sc_unit_off(u), SC_LANES)] = cols[0]

        @pl.loop(0, groups)
        def _(gi):
            base = wid * tpw + gi * g
            pltpu.sync_copy(idx_hbm.at[pl.ds(base * nsel, g * nsel)], idx_v)
            pltpu.sync_copy(h_hbm.at[pl.ds(base, g)], h_v)
            _sc_ring(units, lambda u, s: copy(u, s).start(), lambda u, s: copy(u, s).wait(), compute)
            pltpu.sync_copy(out_v, out_hbm.at[pl.ds(base * nsel, g * nsel)])

    return k(table, idx_flat, h)


def peer_combine_sc(table, idx_flat, coef_flat, t):
    d = table.shape[1] * table.shape[2]
    nsel = PEER_SEL
    tpw = t // SC_WORKERS
    g = SC_GROUP
    groups = tpw // g
    heads = nsel // SC_LANES
    pieces = d // SC_LANES
    units = g * heads
    row_buf = pltpu.VMEM((SC_LANES, SC_ROW_SUB, SC_ROW_LANE), F32)

    @functools.partial(
        pl.kernel, mesh=_sc_mesh(),
        out_type=jax.ShapeDtypeStruct((t, d), F32),
        scratch_types=[
            pltpu.VMEM((g * nsel,), I32),
            pltpu.VMEM((g * nsel,), F32),
            pltpu.VMEM((g, d), F32),
            [row_buf] * SC_RING,
            [pltpu.SemaphoreType.DMA] * SC_RING,
        ],
        compiler_params=_sc_params(),
        name="peer_combine_sc",
    )
    def k(tab_hbm, idx_hbm, coef_hbm, out_hbm, idx_v, coef_v, y_v, rows, sems):
        wid = _sc_worker_id()

        def copy(u, slot):
            ids = idx_v.at[pl.ds(_sc_unit_off(u), SC_LANES)]
            return pltpu.make_async_copy(tab_hbm.at[ids], rows[slot], sems[slot])

        def compute(u, slot):
            tt = u // heads
            first = (u % heads) == 0
            cs = [plsc.load_gather(coef_v, [jnp.full((SC_LANES,), u * SC_LANES + r, I32)])
                  for r in range(SC_LANES)]

            @plsc.parallel_loop(0, pieces, unroll=2)
            def _(c):
                off = pl.multiple_of(c * SC_LANES, SC_LANES)
                terms = [cs[r] * _sc_row_piece(rows[slot], r, c) for r in range(SC_LANES)]
                while len(terms) > 1:
                    terms = [terms[i] + terms[i + 1] for i in range(0, len(terms), 2)]
                prev = y_v[tt, pl.ds(off, SC_LANES)]
                y_v[tt, pl.ds(off, SC_LANES)] = terms[0] + jnp.where(first, 0.0, prev)

        @pl.loop(0, groups)
        def _(gi):
            base = wid * tpw + gi * g
            pltpu.sync_copy(idx_hbm.at[pl.ds(base * nsel, g * nsel)], idx_v)
            pltpu.sync_copy(coef_hbm.at[pl.ds(base * nsel, g * nsel)], coef_v)
            _sc_ring(units, lambda u, s: copy(u, s).start(), lambda u, s: copy(u, s).wait(), compute)
            pltpu.sync_copy(y_v, out_hbm.at[pl.ds(base, g)])

    return k(table, idx_flat, coef_flat)


GELU_C0 = math.sqrt(2.0 / math.pi)
GELU_C1 = 0.044715


def _gelu_tanh(x):
    z = GELU_C0 * (x + GELU_C1 * (x * x * x))
    th = 1.0 - 2.0 / (jnp.exp(2.0 * z) + 1.0)
    return 0.5 * x * (1.0 + th)


def peer_experts_sc(tab_u, tab_v, idx_flat, w_flat, h):
    t, d = h.shape
    nsel = PEER_SEL
    tpw = t // SC_WORKERS
    g = SC_GROUP
    groups = tpw // g
    heads = nsel // SC_LANES
    pieces = d // SC_LANES
    units = g * heads
    row_buf = pltpu.VMEM((SC_LANES, SC_ROW_SUB, SC_ROW_LANE), F32)

    @functools.partial(
        pl.kernel, mesh=_sc_mesh(),
        out_type=jax.ShapeDtypeStruct((t, d), F32),
        scratch_types=[
            pltpu.VMEM((g * nsel,), I32),
            pltpu.VMEM((g * nsel,), F32),
            pltpu.VMEM((g, d), F32),
            pltpu.VMEM((g, d), F32),
            pltpu.VMEM((SC_LANES * SC_LANES,), F32),
            [row_buf] * SC_RING,
            [pltpu.SemaphoreType.DMA] * SC_RING,
        ],
        compiler_params=_sc_params(),
        name="peer_experts_sc",
    )
    def k(u_hbm, v_hbm, idx_hbm, w_hbm, h_hbm, out_hbm, idx_v, coef_v, h_v, y_v, red_v, rows, sems):
        wid = _sc_worker_id()
        lane = lax.iota(I32, SC_LANES)

        def copy(tab_hbm, u, slot):
            ids = idx_v.at[pl.ds(_sc_unit_off(u), SC_LANES)]
            return pltpu.make_async_copy(tab_hbm.at[ids], rows[slot], sems[slot])

        def dots(u, slot):
            tt = u // heads

            def body(c, accs):
                hv = h_v[tt, pl.ds(pl.multiple_of(c * SC_LANES, SC_LANES), SC_LANES)]
                return tuple(accs[r] + _sc_row_piece(rows[slot], r, c) * hv for r in range(SC_LANES))

            accs = lax.fori_loop(0, pieces, body,
                                 tuple(jnp.zeros((SC_LANES,), F32) for _ in range(SC_LANES)))
            for r in range(SC_LANES):
                red_v[pl.ds(r * SC_LANES, SC_LANES)] = accs[r]
            cols = [plsc.load_gather(red_v, [lane * SC_LANES + j]) for j in range(SC_LANES)]
            while len(cols) > 1:
                cols = [cols[i] + cols[i + 1] for i in range(0, len(cols), 2)]
            sl = pl.ds(_sc_unit_off(u), SC_LANES)
            coef_v[sl] = coef_v[sl] * _gelu_tanh(cols[0])

        def combine(u, slot):
            tt = u // heads
            first = (u % heads) == 0
            cs = [plsc.load_gather(coef_v, [jnp.full((SC_LANES,), u * SC_LANES + r, I32)])
                  for r in range(SC_LANES)]

            @plsc.parallel_loop(0, pieces, unroll=2)
            def _(c):
                off = pl.multiple_of(c * SC_LANES, SC_LANES)
                terms = [cs[r] * _sc_row_piece(rows[slot], r, c) for r in range(SC_LANES)]
                while len(terms) > 1:
                    terms = [terms[i] + terms[i + 1] for i in range(0, len(terms), 2)]
                prev = y_v[tt, pl.ds(off, SC_LANES)]
                y_v[tt, pl.ds(off, SC_LANES)] = terms[0] + jnp.where(first, 0.0, prev)

        @pl.loop(0, groups)
        def _(gi):
            base = wid * tpw + gi * g
            pltpu.sync_copy(idx_hbm.at[pl.ds(base * nsel, g * nsel)], idx_v)
            pltpu.sync_copy(w_hbm.at[pl.ds(base * nsel, g * nsel)], coef_v)
            pltpu.sync_copy(h_hbm.at[pl.ds(base, g)], h_v)
            _sc_ring(units, lambda u, s: copy(u_hbm, u, s).start(), lambda u, s: copy(u_hbm, u, s).wait(), dots)
            _sc_ring(units, lambda u, s: copy(v_hbm, u, s).start(), lambda u, s: copy(v_hbm, u, s).wait(), combine)
            pltpu.sync_copy(y_v, out_hbm.at[pl.ds(base, g)])

    return k(tab_u, tab_v, idx_flat, w_flat, h)


SC_PK_RING = 8
SC_PK_SUB = 4
HI_MASK = -65536


def pack_bf16_pairs(a):
    half = a.shape[1] // 2
    bits = lax.bitcast_convert_type(a.astype(BF16), jnp.uint16).astype(jnp.uint32)
    return lax.bitcast_convert_type(bits[:, :half] | (bits[:, half:] << 16), I32)


def _unpack_halves(x32):
    w = plsc.bitcast(x32, I32)
    return plsc.bitcast(w << 16, F32), plsc.bitcast(w & HI_MASK, F32)


def _tree_sum(xs):
    while len(xs) > 1:
        xs = [xs[i] + xs[i + 1] for i in range(0, len(xs), 2)]
    return xs[0]


def peer_experts_pk_sc(tab_u, tab_v, idx_flat, w_flat, hp, d):
    t = hp.shape[0]
    nsel = PEER_SEL
    tpw = t // SC_WORKERS
    g = SC_GROUP
    groups = tpw // g
    heads = nsel // SC_LANES
    chunks = d // 32
    units = g * heads
    ring = SC_PK_RING
    row_buf = pltpu.VMEM((SC_LANES, SC_PK_SUB, SC_ROW_LANE), I32)

    def row_words(rows, r, wc):
        per = SC_ROW_LANE // SC_LANES
        return plsc.bitcast(rows[r, wc // per, pl.ds(pl.multiple_of((wc % per) * SC_LANES, SC_LANES), SC_LANES)],
                            BF16)

    def ring_loop(n_units, start, wait, compute):
        for u in range(ring - 1):
            start(u, u)

        @pl.loop(0, n_units, step=ring)
        def _(uu):
            for b in range(ring):
                u = uu + b
                nxt = u + (ring - 1)

                @pl.when(nxt < n_units)
                def _():
                    start(nxt, (b + ring - 1) % ring)

                wait(u, b)
                compute(u, b)

    @functools.partial(
        pl.kernel, mesh=_sc_mesh(),
        out_type=jax.ShapeDtypeStruct((t, d), F32),
        scratch_types=[
            pltpu.VMEM((g * nsel,), I32),
            pltpu.VMEM((g * nsel,), F32),
            pltpu.VMEM((g, d // 2), I32),
            pltpu.VMEM((g, d), F32),
            pltpu.VMEM((SC_LANES * SC_LANES,), F32),
            [row_buf] * ring,
            [pltpu.SemaphoreType.DMA] * ring,
        ],
        compiler_params=_sc_params(),
        name="peer_experts_pk_sc",
    )
    def k(u_hbm, v_hbm, idx_hbm, w_hbm, h_hbm, out_hbm, idx_v, coef_v, h_v, y_v, red_v, rows, sems):
        wid = _sc_worker_id()
        lane = lax.iota(I32, SC_LANES)

        def copy(tab_hbm, u, slot):
            ids = idx_v.at[pl.ds(_sc_unit_off(u), SC_LANES)]
            return pltpu.make_async_copy(tab_hbm.at[ids], rows[slot], sems[slot])

        def dots(u, slot):
            tt = u // heads

            def body(cp, accs):
                out = []
                hv = [plsc.bitcast(h_v[tt, pl.ds(pl.multiple_of((2 * cp + i) * SC_LANES, SC_LANES), SC_LANES)], BF16)
                      for i in range(2)]
                for r in range(SC_LANES):
                    pr = row_words(rows[slot], r, 2 * cp) * hv[0] + row_words(rows[slot], r, 2 * cp + 1) * hv[1]
                    lo, hi = _unpack_halves(pr)
                    out.append(accs[r] + lo + hi)
                return tuple(out)

            accs = lax.fori_loop(0, chunks // 2, body,
                                 tuple(jnp.zeros((SC_LANES,), F32) for _ in range(SC_LANES)))
            for r in range(SC_LANES):
                red_v[pl.ds(r * SC_LANES, SC_LANES)] = accs[r]
            act = _tree_sum([plsc.load_gather(red_v, [lane * SC_LANES + j]) for j in range(SC_LANES)])
            sl = pl.ds(_sc_unit_off(u), SC_LANES)
            coef_v[sl] = coef_v[sl] * _gelu_tanh(act)

        def combine(u, slot):
            tt = u // heads
            first = (u % heads) == 0
            cb = []
            for r in range(SC_LANES):
                c = plsc.load_gather(coef_v, [jnp.full((SC_LANES,), u * SC_LANES + r, I32)])
                cb.append(plsc.pack(c, c, format=plsc.PackFormat.INTERLEAVED))

            @plsc.parallel_loop(0, chunks, unroll=2)
            def _(wc):
                lo, hi = _unpack_halves(_tree_sum([cb[r] * row_words(rows[slot], r, wc) for r in range(SC_LANES)]))
                for half, val in ((0, lo), (1, hi)):
                    sl = pl.ds(pl.multiple_of(half * (d // 2) + wc * SC_LANES, SC_LANES), SC_LANES)
                    y_v[tt, sl] = val + jnp.where(first, 0.0, y_v[tt, sl])

        @pl.loop(0, groups)
        def _(gi):
            base = wid * tpw + gi * g
            pltpu.sync_copy(idx_hbm.at[pl.ds(base * nsel, g * nsel)], idx_v)
            pltpu.sync_copy(w_hbm.at[pl.ds(base * nsel, g * nsel)], coef_v)
            pltpu.sync_copy(h_hbm.at[pl.ds(base, g)], h_v)
            ring_loop(units, lambda u, s: copy(u_hbm, u, s).start(), lambda u, s: copy(u_hbm, u, s).wait(), dots)
            ring_loop(units, lambda u, s: copy(v_hbm, u, s).start(), lambda u, s: copy(v_hbm, u, s).wait(), combine)
            pltpu.sync_copy(y_v, out_hbm.at[pl.ds(base, g)])

    return k(tab_u, tab_v, idx_flat, w_flat, hp)


def kernel(x, mem, rel_bias, ln_mix, w_in, hg_lower, hg_norm, w_up_a, w_up_b, w_out, ln_cross, ln_mem, wq_x, wk_x, wv_x, wo_x, ln_ffn, peer_query, peer_subkeys, peer_u, peer_v, ln_final):
    b, s, d = x.shape
    depth = w_in.shape[0]
    assert depth == 1, "the residual after PEER is fused into the final norm"
    assert s % MB_BLOCK == 0 and s % HG_CHUNK == 0 and s % (PEER_SLICES * SC_WORKERS * SC_GROUP) == 0
    nb = s // MB_BLOCK
    row = lambda a: a.reshape(1, -1).astype(F32)
    lb_all = jnp.cumsum(jax.nn.softmax(hg_lower.astype(F32), axis=0), axis=0)
    bias = moba_bias_tiles(rel_bias)
    n_hg = 4 * HG_WIDTH
    n_qk = 2 * MB_WIDTH
    n_mb = 3 * MB_WIDTH
    l = 0
    w = w_in[l].astype(BF16)
    w_hg, w_qk, w_vt, w_g = w[:, :n_hg], w[:, n_hg:n_hg + n_qk], w[:, n_hg + n_qk:n_hg + n_mb].T, w[:, n_hg + n_mb:]
    wa, wb, wo = w_up_a[l].astype(BF16), w_up_b[l].astype(BF16), w_out[l].astype(BF16)
    wqx, wox = wq_x[l].astype(BF16), wo_x[l].astype(BF16)
    wpq, sk = peer_query[l].astype(BF16), peer_subkeys[l].astype(F32)
    tab3 = lambda a: pack_bf16_pairs(a.astype(F32)).reshape(a.shape[0], SC_PK_SUB, SC_ROW_LANE)
    tab_u, tab_v = tab3(peer_u[l]), tab3(peer_v[l])
    kx, vx = mem_kv(mem, row(ln_mem[l]), wk_x[l].astype(BF16), wv_x[l].astype(BF16))

    outs = []
    for bi in range(b):
        x2d = x[bi]
        p0, pqk, vt, pg = in_proj(x2d, row(ln_mix[l]), w_hg, w_qk, w_vt, w_g)
        ya = hgrn2(p0, row(lb_all[l]), row(hg_norm[l]), 1, s)
        km = moba_kmean(pqk, 1, s).reshape(1, nb, MB_WIDTH)
        ts = s // PEER_SLICES
        for tok0 in range(0, s, ts):
            yb = moba_attention(pqk, vt, km, bias, 1, s, tok0 // MB_BLOCK, ts // MB_BLOCK)
            xs = mix_out(x2d, ya, yb, pg, wa, wb, wo, tok0)
            xs = cross_attn(xs, row(ln_cross[l]), wqx, kx[bi:bi + 1], vx[bi:bi + 1], wox, ts)
            hp, eidx, wts = peer_route(xs, row(ln_ffn[l]), wpq, sk, 0, ts)
            y = peer_experts_pk_sc(tab_u, tab_v, eidx.reshape(ts * PEER_SEL), wts.reshape(ts * PEER_SEL), hp, d)
            outs.append(final_norm(xs, y, row(ln_final), 0))
    return jnp.concatenate(outs, axis=0).reshape(b, s, d)
```

```python
import functools
import math

import jax
import jax.numpy as jnp
import numpy as np
from jax import lax
from jax.experimental import pallas as pl
from jax.experimental.pallas import tpu as pltpu
from jax.experimental.pallas import tpu_sc as plsc

F32 = jnp.float32
BF16 = jnp.bfloat16
I32 = jnp.int32
EPS = 1e-6
NEG_INF = float("-inf")

HG_HEADS = 4
HG_D = 128
HG_WIDTH = HG_HEADS * HG_D
HG_CHUNK = 64
HG_SUB = 16
MB_HEADS = 8
MB_DH = 64
MB_WIDTH = MB_HEADS * MB_DH
MB_BLOCK = 256
MB_TOPK = 3
MB_BIAS_TILES = 8
REL_BUCKETS = 32
REL_MAX_DIST = 2048
X_HEADS = 4
PEER_HEADS = 8
PEER_NKEYS = 128
PEER_TOPK = 16
PEER_HALF = 128
PEER_SEL = PEER_HEADS * PEER_TOPK
PEER_SLICES = 4

VMEM_LIMIT = 56 * 1024 * 1024


def _cparams(sem):
    return pltpu.CompilerParams(dimension_semantics=sem, vmem_limit_bytes=VMEM_LIMIT)


def _rms(x, g):
    ms = jnp.mean(x * x, axis=-1, keepdims=True)
    return x * lax.rsqrt(ms + EPS) * g


def _in_proj_kernel(x_ref, g_ref, w0_ref, w1_ref, wvt_ref, w2_ref, o0_ref, o1_ref, ovt_ref, o2_ref):
    h = _rms(x_ref[...], g_ref[...]).astype(BF16)
    o0_ref[...] = jnp.dot(h, w0_ref[...], preferred_element_type=F32)
    o1_ref[...] = jnp.dot(h, w1_ref[...], preferred_element_type=F32).astype(BF16)
    vt = lax.dot_general(wvt_ref[...], h, (((1,), (1,)), ((), ())), preferred_element_type=F32).astype(BF16)
    for g in range(MB_WIDTH // MB_LG):
        ovt_ref[0, g * MB_VROWS:g * MB_VROWS + MB_LG, :] = vt[g * MB_LG:(g + 1) * MB_LG]
        ovt_ref[0, g * MB_VROWS + MB_LG:(g + 1) * MB_VROWS, :] = jnp.ones((MB_ONES, vt.shape[1]), BF16)
    o2_ref[...] = jnp.dot(h, w2_ref[...], preferred_element_type=F32).astype(BF16)


def in_proj(x2d, g, w0, w1, wvt, w2):
    t, d = x2d.shape
    tm = MB_BLOCK
    assert wvt.shape[0] == MB_WIDTH
    n0, n1, nv, n2 = w0.shape[1], w1.shape[1], MB_VT_ROWS, w2.shape[1]
    full = lambda a: pl.BlockSpec(a.shape, lambda i: (0, 0))
    return pl.pallas_call(
        _in_proj_kernel,
        grid=(t // tm,),
        in_specs=[pl.BlockSpec((tm, d), lambda i: (i, 0)), full(g), full(w0), full(w1), full(wvt), full(w2)],
        out_specs=[pl.BlockSpec((tm, n0), lambda i: (i, 0)),
                   pl.BlockSpec((tm, n1), lambda i: (i, 0)),
                   pl.BlockSpec((1, nv, tm), lambda i: (i, 0, 0)),
                   pl.BlockSpec((tm, n2), lambda i: (i, 0))],
        out_shape=[jax.ShapeDtypeStruct((t, n0), F32),
                   jax.ShapeDtypeStruct((t, n1), BF16),
                   jax.ShapeDtypeStruct((t // tm, nv, tm), BF16),
                   jax.ShapeDtypeStruct((t, n2), BF16)],
        compiler_params=_cparams(("parallel",)),
        name="in_proj",
    )(x2d, g, w0, w1, wvt, w2)


def _hgrn_kernel(q_ref, f_ref, i_ref, g_ref, lb_ref, gain_ref, o_ref, st_ref):
    c = pl.program_id(1)

    @pl.when(c == 0)
    def _():
        st_ref[...] = jnp.zeros_like(st_ref)

    C, S = HG_CHUNK, HG_SUB
    row = lax.broadcasted_iota(I32, (C, C), 0)
    col = lax.broadcasted_iota(I32, (C, C), 1)
    tril = (row >= col).astype(F32)
    t_iota = lax.broadcasted_iota(I32, (S, 1), 0)

    for h in range(HG_HEADS):
        sl = slice(h * HG_D, (h + 1) * HG_D)
        q = q_ref[:, sl]
        v = i_ref[:, sl]
        lb = lb_ref[:, sl]
        f = lb + (1.0 - lb) * jax.nn.sigmoid(f_ref[:, sl])
        lf = jnp.log(f)
        k = 1.0 - f
        b = jnp.dot(tril, lf, precision=lax.Precision.HIGHEST, preferred_element_type=F32)
        st = st_ref[h]
        vb = v.astype(BF16)
        qd = (q * jnp.exp(b)).astype(BF16)
        o_inter = lax.dot_general(qd, st.astype(BF16), (((1,), (1,)), ((), ())),
                                  preferred_element_type=F32)
        outs = []
        for i in range(C // S):
            r0 = i * S
            qi = q[r0:r0 + S]
            ki = k[r0:r0 + S]
            bi = b[r0:r0 + S]
            vi = v[r0:r0 + S]
            oi = o_inter[r0:r0 + S]
            if i > 0:
                bs = b[r0 - 1:r0]
                qh = (qi * jnp.exp(bi - bs)).astype(BF16)
                kh = (k[:r0] * jnp.exp(bs - b[:r0])).astype(BF16)
                a = lax.dot_general(qh, kh, (((1,), (1,)), ((), ())), preferred_element_type=F32)
                oi = oi + jnp.dot(a.astype(BF16), vb[:r0], preferred_element_type=F32)
            for s in range(S):
                dec = jnp.exp(jnp.minimum(bi - bi[s:s + 1], 0.0))
                p = qi * ki[s:s + 1] * dec
                a_s = jnp.sum(p, axis=-1, keepdims=True)
                a_s = jnp.where(t_iota >= s, a_s, 0.0)
                oi = oi + a_s * vi[s:s + 1]
            outs.append(oi)
        o = jnp.concatenate(outs, axis=0)
        b_end = b[C - 1:C]
        kd = (k * jnp.exp(b_end - b)).astype(BF16)
        upd = lax.dot_general(vb, kd, (((0,), (0,)), ((), ())), preferred_element_type=F32)
        st_ref[h] = st * jnp.exp(b_end) + upd
        o = o * lax.rsqrt(jnp.mean(o * o, axis=-1, keepdims=True) + EPS)
        g = g_ref[:, sl]
        o_ref[:, sl] = (o * gain_ref[:, sl] * (g * jax.nn.sigmoid(g))).astype(o_ref.dtype)


def hgrn2(p0, lb, gain, batch, seq):
    t = p0.shape[0]
    nc = seq // HG_CHUNK
    w = HG_WIDTH

    def col(j):
        return pl.BlockSpec((HG_CHUNK, w), lambda b, c, j=j: (b * nc + c, j))

    return pl.pallas_call(
        _hgrn_kernel,
        grid=(batch, nc),
        in_specs=[col(0), col(1), col(2), col(3),
                  pl.BlockSpec((1, w), lambda b, c: (0, 0)),
                  pl.BlockSpec((1, w), lambda b, c: (0, 0))],
        out_specs=pl.BlockSpec((HG_CHUNK, w), lambda b, c: (b * nc + c, 0)),
        out_shape=jax.ShapeDtypeStruct((t, w), BF16),
        scratch_shapes=[pltpu.VMEM((HG_HEADS, HG_D, HG_D), F32)],
        compiler_params=_cparams(("parallel", "arbitrary")),
        name="hgrn2",
    )(p0, p0, p0, p0, lb, gain)


def _kmean_kernel(k_ref, o_ref):
    o_ref[0] = jnp.mean(k_ref[...].astype(F32), axis=0, keepdims=True)


def moba_kmean(p1, batch, seq):
    nbt = p1.shape[0] // MB_BLOCK
    return pl.pallas_call(
        _kmean_kernel,
        grid=(nbt,),
        in_specs=[pl.BlockSpec((MB_BLOCK, MB_WIDTH), lambda i: (i, 1))],
        out_specs=pl.BlockSpec((1, 1, MB_WIDTH), lambda i: (i, 0, 0)),
        out_shape=jax.ShapeDtypeStruct((nbt, 1, MB_WIDTH), F32),
        compiler_params=_cparams(("parallel",)),
        name="moba_kmean",
    )(p1)


MB_PAIR = 4
MB_PW = MB_PAIR * MB_DH
MB_LG = 128
MB_ONES = 16
MB_VROWS = MB_LG + MB_ONES
MB_VT_ROWS = (MB_WIDTH // MB_LG) * MB_VROWS


def _moba_kernel(q_ref, k_ref, vt_ref, km_ref, bias_ref, o_ref, *scratch, qb0):
    m_ref, l_ref, al_ref, acc_ref, msk_ref, s_ref, p_ref = (
        scratch[i * MB_PAIR:(i + 1) * MB_PAIR] for i in range(7))
    qi = pl.program_id(2) + qb0
    nb = km_ref.shape[0]
    blk = MB_BLOCK
    heads = range(MB_PAIR)
    grp = lambda hh: slice((hh // 2) * MB_LG, (hh // 2 + 1) * MB_LG)
    q = q_ref[...]
    lane = lax.broadcasted_iota(I32, (blk, MB_LG), 1)
    in_head = [(lane < MB_DH) if hh % 2 == 0 else (lane >= MB_DH) for hh in heads]
    qs = q * jnp.asarray(MB_DH ** -0.5, BF16)
    qh = [jnp.where(in_head[hh], qs[:, grp(hh)], jnp.zeros((blk, MB_LG), BF16)) for hh in heads]
    nt = (((1,), (1,)), ((), ()))

    qf = q.astype(F32)
    n_io = lax.broadcasted_iota(I32, (nb, blk), 0)
    for hh in heads:
        gate = lax.dot_general(km_ref[:, grp(hh)], jnp.where(in_head[hh], qf[:, grp(hh)], 0.0), nt,
                               precision=lax.Precision.HIGHEST, preferred_element_type=F32)
        gate = jnp.where(n_io < qi, gate, NEG_INF)
        chosen = n_io < 0
        for _ in range(MB_TOPK):
            mx = jnp.max(gate, axis=0, keepdims=True)
            ix = jnp.min(jnp.where(gate == mx, n_io, nb), axis=0, keepdims=True)
            hit = n_io == ix
            chosen = chosen | (hit & (mx > NEG_INF))
            gate = jnp.where(hit, NEG_INF, gate)
        msk_ref[hh][...] = jnp.where(chosen, 0.0, NEG_INF)

    own_rows = lambda r, hh: r[(hh % 2) * MB_DH:(hh % 2 + 1) * MB_DH]

    vgrp = lambda hh: slice((hh // 2) * MB_VROWS, (hh // 2 + 1) * MB_VROWS)

    def pv_stage(blk_idx):
        vtb = vt_ref[blk_idx]
        r = [jnp.dot(vtb[vgrp(hh)], p_ref[hh][...], preferred_element_type=F32) for hh in heads]
        al = [al_ref[hh][...] for hh in heads]
        a_new = [al[hh] * acc_ref[hh][...] + own_rows(r[hh], hh) for hh in heads]
        l_new = [al[hh] * l_ref[hh][...] + r[hh][MB_LG:MB_LG + 1] for hh in heads]
        return a_new, l_new

    def store_pv(a_new, l_new):
        for hh in heads:
            acc_ref[hh][...] = a_new[hh]
            l_ref[hh][...] = l_new[hh]

    def softmax_stage():
        s = [s_ref[hh][...] for hh in heads]
        m_old = [m_ref[hh][...] for hh in heads]
        m_new = [jnp.maximum(m_old[hh], jnp.max(s[hh], axis=0, keepdims=True)) for hh in heads]
        alpha = [jnp.exp(m_old[hh] - m_new[hh]) for hh in heads]
        p = [jnp.exp((s[hh] - m_new[hh]).astype(BF16)) for hh in heads]
        return p, alpha, m_new

    def store_softmax(p, alpha, m_new):
        for hh in heads:
            p_ref[hh][...] = p[hh]
            al_ref[hh][...] = alpha[hh]
            m_ref[hh][...] = m_new[hh]

    k_own = k_ref[pl.ds(pl.multiple_of(qi * blk, blk), blk), :]
    key_io = lax.broadcasted_iota(I32, (blk, blk), 0)
    qry_io = lax.broadcasted_iota(I32, (blk, blk), 1)
    for hh in heads:
        s = lax.dot_general(k_own[:, grp(hh)], qh[hh], nt, preferred_element_type=F32) + bias_ref[hh, 0]
        s_ref[hh][...] = jnp.where(key_io <= qry_io, s, NEG_INF)
        m_ref[hh][...] = jnp.full((1, blk), NEG_INF, F32)
        l_ref[hh][...] = jnp.zeros((1, blk), F32)
        al_ref[hh][...] = jnp.ones((1, blk), F32)
        acc_ref[hh][...] = jnp.zeros((MB_DH, blk), F32)
        p_ref[hh][...] = jnp.zeros((blk, blk), BF16)

    def step(i, carry):
        pv = pv_stage(jnp.where(i <= 1, qi, i - 2))
        sm = softmax_stage()
        kn = k_ref[pl.ds(pl.multiple_of(i * blk, blk), blk), :]
        d = jnp.minimum(qi - i, MB_BIAS_TILES - 1)
        s_next = [lax.dot_general(kn[:, grp(hh)], qh[hh], nt, preferred_element_type=F32)
                  + bias_ref[hh, d] + msk_ref[hh][pl.ds(i, 1), :] for hh in heads]
        store_pv(*pv)
        for hh in heads:
            s_ref[hh][...] = s_next[hh]
        store_softmax(*sm)
        return carry

    lax.fori_loop(0, qi, step, 0)
    pv = pv_stage(jnp.where(qi <= 1, qi, qi - 2))
    sm = softmax_stage()
    store_pv(*pv)
    store_softmax(*sm)
    a_fin, l_fin = pv_stage(jnp.where(qi == 0, qi, qi - 1))
    out_t = jnp.concatenate([a_fin[hh] / l_fin[hh] for hh in heads], axis=0)
    o_ref[...] = out_t.T.astype(o_ref.dtype)


def moba_attention(pqk, vt, km, bias, batch, seq, qb0=0, nqb=None):
    nb = seq // MB_BLOCK
    nqb = nb if nqb is None else nqb
    t = batch * nqb * MB_BLOCK
    groups = MB_WIDTH // MB_PW
    return pl.pallas_call(
        functools.partial(_moba_kernel, qb0=qb0),
        grid=(batch, groups, nqb),
        in_specs=[
            pl.BlockSpec((MB_BLOCK, MB_PW), lambda b, j, i: (b * nb + qb0 + i, j)),
            pl.BlockSpec((seq, MB_PW), lambda b, j, i: (b, groups + j)),
            pl.BlockSpec((nb, (MB_PW // MB_LG) * MB_VROWS, MB_BLOCK), lambda b, j, i: (b, j, 0)),
            pl.BlockSpec((None, nb, MB_PW), lambda b, j, i: (b, 0, j)),
            pl.BlockSpec((MB_PAIR, MB_BIAS_TILES, MB_BLOCK, MB_BLOCK), lambda b, j, i: (j, 0, 0, 0)),
        ],
        out_specs=pl.BlockSpec((MB_BLOCK, MB_PW), lambda b, j, i: (b * nqb + i, j)),
        out_shape=jax.ShapeDtypeStruct((t, MB_WIDTH), BF16),
        scratch_shapes=(
            [pltpu.VMEM((1, MB_BLOCK), F32)] * (3 * MB_PAIR)
            + [pltpu.VMEM((MB_DH, MB_BLOCK), F32)] * MB_PAIR
            + [pltpu.VMEM((nb, MB_BLOCK), F32)] * MB_PAIR
            + [pltpu.VMEM((MB_BLOCK, MB_BLOCK), F32)] * MB_PAIR
            + [pltpu.VMEM((MB_BLOCK, MB_BLOCK), BF16)] * MB_PAIR
        ),
        compiler_params=_cparams(("parallel", "parallel", "arbitrary")),
        name="moba_attn",
    )(pqk, pqk, vt, km, bias)


def _t5_bucket(dist):
    max_exact = REL_BUCKETS // 2
    scaled = jnp.log(jnp.maximum(dist, 1).astype(F32) / max_exact) / math.log(REL_MAX_DIST / max_exact)
    large = jnp.minimum(max_exact + (scaled * (REL_BUCKETS - max_exact)).astype(I32), REL_BUCKETS - 1)
    return jnp.where(dist < max_exact, dist, large)


def moba_bias_tiles(rel_bias):
    blk = MB_BLOCK
    span = 2 * blk - 1
    x = jnp.arange(span) - (blk - 1)
    dist = jnp.maximum(jnp.arange(MB_BIAS_TILES)[:, None] * blk + x[None, :], 0)
    w = rel_bias.astype(F32).T[:, _t5_bucket(dist)]
    h = w.shape[0]
    wp = jnp.pad(w, ((0, 0), (0, 0), (0, 1)))
    a = jnp.broadcast_to(wp[:, :, None, :], (h, MB_BIAS_TILES, blk, span + 1))
    a = a.reshape(h, MB_BIAS_TILES, blk * (span + 1))[:, :, :blk * span]
    return a.reshape(h, MB_BIAS_TILES, blk, span)[:, :, :, blk - 1:]


def _mix_kernel(x_ref, ya_ref, yb_ref, ga_ref, gb_ref, wa_ref, wb_ref, wo_ref, o_ref):
    za = jnp.dot(ya_ref[...], wa_ref[...], preferred_element_type=F32)
    zb = jnp.dot(yb_ref[...], wb_ref[...], preferred_element_type=F32)
    z = jax.nn.sigmoid(ga_ref[...].astype(F32)) * za + jax.nn.sigmoid(gb_ref[...].astype(F32)) * zb
    o_ref[...] = x_ref[...] + jnp.dot(z.astype(BF16), wo_ref[...], preferred_element_type=F32)


def mix_out(x2d, ya, yb, pg, wa, wb, wo, tok0=0, tm=256):
    t = yb.shape[0]
    d = x2d.shape[1]
    w = ya.shape[1]
    b0 = tok0 // tm
    return pl.pallas_call(
        _mix_kernel,
        grid=(t // tm,),
        in_specs=[
            pl.BlockSpec((tm, d), lambda i: (b0 + i, 0)),
            pl.BlockSpec((tm, w), lambda i: (b0 + i, 0)),
            pl.BlockSpec((tm, w), lambda i: (i, 0)),
            pl.BlockSpec((tm, d), lambda i: (b0 + i, 0)),
            pl.BlockSpec((tm, d), lambda i: (b0 + i, 1)),
            pl.BlockSpec((w, d), lambda i: (0, 0)),
            pl.BlockSpec((w, d), lambda i: (0, 0)),
            pl.BlockSpec((d, d), lambda i: (0, 0)),
        ],
        out_specs=pl.BlockSpec((tm, d), lambda i: (i, 0)),
        out_shape=jax.ShapeDtypeStruct((t, d), F32),
        compiler_params=_cparams(("parallel",)),
        name="mix_out",
    )(x2d, ya, yb, pg, pg, wa, wb, wo)


def _mem_kv_kernel(m_ref, g_ref, wk_ref, wv_ref, k_ref, v_ref):
    mn = _rms(m_ref[...], g_ref[...]).astype(BF16)
    k_ref[...] = jnp.dot(mn, wk_ref[...], preferred_element_type=F32).astype(BF16)
    v_ref[...] = jnp.dot(mn, wv_ref[...], preferred_element_type=F32).astype(BF16)


def mem_kv(mem, g, wk, wv):
    b, m, d = mem.shape
    spec = pl.BlockSpec((None, m, d), lambda i: (i, 0, 0))
    wspec = pl.BlockSpec((d, d), lambda i: (0, 0))
    return pl.pallas_call(
        _mem_kv_kernel,
        grid=(b,),
        in_specs=[spec, pl.BlockSpec((1, d), lambda i: (0, 0)), wspec, wspec],
        out_specs=[spec, spec],
        out_shape=[jax.ShapeDtypeStruct((b, m, d), BF16)] * 2,
        compiler_params=_cparams(("parallel",)),
        name="mem_kv",
    )(mem, g, wk, wv)


def _cross_kernel(x_ref, g_ref, wq_ref, k_ref, v_ref, wo_ref, o_ref):
    x = x_ref[...]
    d = x.shape[1]
    dh = d // X_HEADS
    h = _rms(x, g_ref[...]).astype(BF16)
    q = (jnp.dot(h, wq_ref[...], preferred_element_type=F32) * (dh ** -0.5)).astype(BF16)
    outs = []
    for hh in range(X_HEADS):
        sl = slice(hh * dh, (hh + 1) * dh)
        s = lax.dot_general(q[:, sl], k_ref[:, sl], (((1,), (1,)), ((), ())),
                            preferred_element_type=F32)
        p = jnp.exp(s - jnp.max(s, axis=1, keepdims=True))
        l = jnp.sum(p, axis=1, keepdims=True)
        o = jnp.dot(p.astype(BF16), v_ref[:, sl], preferred_element_type=F32) / l
        outs.append(o.astype(BF16))
    o = jnp.concatenate(outs, axis=1)
    o_ref[...] = x + jnp.dot(o, wo_ref[...], preferred_element_type=F32)


def cross_attn(x2d, g, wq, kx, vx, wo, seq, tm=256):
    t, d = x2d.shape
    m = kx.shape[1]
    per_b = seq // tm
    kv = pl.BlockSpec((None, m, d), lambda i: (i // per_b, 0, 0))
    wspec = pl.BlockSpec((d, d), lambda i: (0, 0))
    return pl.pallas_call(
        _cross_kernel,
        grid=(t // tm,),
        in_specs=[pl.BlockSpec((tm, d), lambda i: (i, 0)), pl.BlockSpec((1, d), lambda i: (0, 0)),
                  wspec, kv, kv, wspec],
        out_specs=pl.BlockSpec((tm, d), lambda i: (i, 0)),
        out_shape=jax.ShapeDtypeStruct((t, d), F32),
        compiler_params=_cparams(("parallel",)),
        name="cross_attn",
    )(x2d, g, wq, kx, vx, wo)


def _topk_rows(sc, k):
    n = sc.shape[0]
    io = lax.broadcasted_iota(I32, sc.shape, 0)
    vals, ids = [], []
    for _ in range(k):
        m = jnp.max(sc, axis=0, keepdims=True)
        ix = jnp.min(jnp.where(sc == m, io, n), axis=0, keepdims=True)
        vals.append(m)
        ids.append(ix)
        sc = jnp.where(io == ix, NEG_INF, sc)
    return jnp.concatenate(vals, axis=0), jnp.concatenate(ids, axis=0)


def _pack_bf16_halves(h):
    bits = lax.bitcast_convert_type(h, I32)
    r = bits + 0x7FFF + (lax.shift_right_logical(bits, 16) & 1)
    half = h.shape[1] // 2
    return lax.shift_right_logical(r[:, :half], 16) | (r[:, half:] & HI_MASK)


def _route_kernel(x_ref, g_ref, wq_ref, sk_ref, hp_ref, idx_ref, w_ref, hb_ref, it_ref, wt_ref):
    p = pl.program_id(1)

    @pl.when(p == 0)
    def _():
        h = _rms(x_ref[...], g_ref[...])
        hp_ref[...] = _pack_bf16_halves(h)
        hb_ref[...] = h.astype(BF16)

    qh = jnp.dot(hb_ref[...], wq_ref[...], preferred_element_type=F32)
    tops = []
    for c in range(2):
        seg = qh[:, c * PEER_HALF:(c + 1) * PEER_HALF]
        sc = lax.dot_general(sk_ref[c], seg, (((1,), (1,)), ((), ())),
                             precision=lax.Precision.HIGHEST, preferred_element_type=F32)
        tops.append(_topk_rows(sc, PEER_TOPK))
    (s0, i0), (s1, i1) = tops
    k = PEER_TOPK
    sub = 8
    tm = s0.shape[1]
    r8 = lax.broadcasted_iota(I32, (sub, tm), 0)
    r16 = lax.broadcasted_iota(I32, (k, tm), 0)
    cand_b = [s0[0:1] + s1, s0[1:2] + s1[:sub]]
    cidx_b = [i0[0:1] * PEER_NKEYS + i1, i0[1:2] * PEER_NKEYS + i1[:sub]]
    pos_b = [r16, k + r8]
    for a in range(2, sub):
        keep = r8 < (k // (a + 1))
        cand_b.append(jnp.where(keep, s0[a:a + 1] + s1[:sub], NEG_INF))
        cidx_b.append(i0[a:a + 1] * PEER_NKEYS + i1[:sub])
        pos_b.append(a * k + r8)
    cand_b.append(s0[sub:] + s1[0:1])
    cidx_b.append(i0[sub:] * PEER_NKEYS + i1[0:1])
    pos_b.append((sub + r8) * k)
    cand = jnp.concatenate(cand_b, axis=0)
    cidx = jnp.concatenate(cidx_b, axis=0)
    pos = jnp.concatenate(pos_b, axis=0)
    vals, ids = [], []
    for _ in range(k):
        m = jnp.max(cand, axis=0, keepdims=True)
        px = jnp.min(jnp.where(cand == m, pos, k * k), axis=0, keepdims=True)
        hit = pos == px
        vals.append(m)
        ids.append(jnp.sum(jnp.where(hit, cidx, 0), axis=0, keepdims=True))
        cand = jnp.where(hit, NEG_INF, cand)
    sf = jnp.concatenate(vals, axis=0)
    e = jnp.exp(sf - sf[0:1])
    rows = pl.ds(pl.multiple_of(p * PEER_TOPK, PEER_TOPK), PEER_TOPK)
    wt_ref[rows, :] = e / jnp.sum(e, axis=0, keepdims=True)
    it_ref[rows, :] = jnp.concatenate(ids, axis=0)

    @pl.when(p == pl.num_programs(1) - 1)
    def _():
        idx_ref[...] = it_ref[...].T
        w_ref[...] = wt_ref[...].T


def peer_route(x2d, g, wq, sk, tok0, t, tm=256):
    d = x2d.shape[1]
    ph = sk.shape[0]
    nsel = ph * PEER_TOPK
    blk0 = tok0 // tm
    return pl.pallas_call(
        _route_kernel,
        grid=(t // tm, ph),
        in_specs=[
            pl.BlockSpec((tm, d), lambda i, p: (blk0 + i, 0)),
            pl.BlockSpec((1, d), lambda i, p: (0, 0)),
            pl.BlockSpec((d, 2 * PEER_HALF), lambda i, p: (0, p)),
            pl.BlockSpec((None, 2, PEER_NKEYS, PEER_HALF), lambda i, p: (p, 0, 0, 0)),
        ],
        out_specs=[
            pl.BlockSpec((tm, d // 2), lambda i, p: (i, 0)),
            pl.BlockSpec((tm, nsel), lambda i, p: (i, 0)),
            pl.BlockSpec((tm, nsel), lambda i, p: (i, 0)),
        ],
        out_shape=[jax.ShapeDtypeStruct((t, d // 2), I32),
                   jax.ShapeDtypeStruct((t, nsel), I32),
                   jax.ShapeDtypeStruct((t, nsel), F32)],
        scratch_shapes=[pltpu.VMEM((tm, d), BF16),
                        pltpu.VMEM((nsel, tm), I32),
                        pltpu.VMEM((nsel, tm), F32)],
        compiler_params=_cparams(("parallel", "arbitrary")),
        name="peer_route",
    )(x2d, g, wq, sk)


def _coef_kernel(w_ref, a_ref, o_ref):
    o_ref[...] = w_ref[...] * jax.nn.gelu(a_ref[...])


def peer_coef(w, act, tm=1024):
    t, n = w.shape
    spec = pl.BlockSpec((tm, n), lambda i: (i, 0))
    return pl.pallas_call(
        _coef_kernel, grid=(t // tm,), in_specs=[spec, spec], out_specs=spec,
        out_shape=jax.ShapeDtypeStruct((t, n), F32),
        compiler_params=_cparams(("parallel",)), name="peer_coef",
    )(w, act)


def _final_kernel(x_ref, y_ref, g_ref, o_ref):
    o_ref[...] = _rms(x_ref[...] + y_ref[...], g_ref[...])


def final_norm(x2d, y, g, tok0, tm=512):
    t, d = y.shape
    blk0 = tok0 // tm
    spec = pl.BlockSpec((tm, d), lambda i: (i, 0))
    return pl.pallas_call(
        _final_kernel, grid=(t // tm,),
        in_specs=[pl.BlockSpec((tm, d), lambda i: (blk0 + i, 0)), spec, pl.BlockSpec((1, d), lambda i: (0, 0))],
        out_specs=spec,
        out_shape=jax.ShapeDtypeStruct((t, d), F32),
        compiler_params=_cparams(("parallel",)), name="final_norm",
    )(x2d, y, g)


SC_CORES = 2
SC_SUBCORES = 16
SC_WORKERS = SC_CORES * SC_SUBCORES
SC_LANES = 16
SC_GROUP = 16


def _sc_mesh():
    return plsc.VectorSubcoreMesh(core_axis_name="c", subcore_axis_name="s")


def _sc_params():
    return pltpu.CompilerParams(needs_layout_passes=False)


def _sc_worker_id():
    return lax.axis_index("s") * SC_CORES + lax.axis_index("c")


SC_RING = 4
SC_ROW_SUB = 8
SC_ROW_LANE = 128


def _sc_ring(n_units, start, wait, compute):
    for u in range(SC_RING - 1):
        start(u, u)

    @pl.loop(0, n_units, step=SC_RING)
    def _(uu):
        for b in range(SC_RING):
            u = uu + b
            nxt = u + (SC_RING - 1)

            @pl.when(nxt < n_units)
            def _():
                start(nxt, (b + SC_RING - 1) % SC_RING)

            wait(u, b)
            compute(u, b)


def _sc_unit_off(u):
    off = u * SC_LANES
    return off if isinstance(off, int) else pl.multiple_of(off, SC_LANES)


def _sc_row_piece(rows, r, c):
    per = SC_ROW_LANE // SC_LANES
    return rows[r, c // per, pl.ds(pl.multiple_of((c % per) * SC_LANES, SC_LANES), SC_LANES)]


def peer_dots_sc(table, idx_flat, h):
    t, d = h.shape
    nsel = PEER_SEL
    tpw = t // SC_WORKERS
    g = SC_GROUP
    groups = tpw // g
    heads = nsel // SC_LANES
    pieces = d // SC_LANES
    units = g * heads
    row_buf = pltpu.VMEM((SC_LANES, SC_ROW_SUB, SC_ROW_LANE), F32)

    @functools.partial(
        pl.kernel, mesh=_sc_mesh(),
        out_type=jax.ShapeDtypeStruct((t * nsel,), F32),
        scratch_types=[
            pltpu.VMEM((g * nsel,), I32),
            pltpu.VMEM((g, d), F32),
            pltpu.VMEM((g * nsel,), F32),
            pltpu.VMEM((SC_LANES * SC_LANES,), F32),
            [row_buf] * SC_RING,
            [pltpu.SemaphoreType.DMA] * SC_RING,
        ],
        compiler_params=_sc_params(),
        name="peer_dots_sc",
    )
    def k(tab_hbm, idx_hbm, h_hbm, out_hbm, idx_v, h_v, out_v, red_v, rows, sems):
        wid = _sc_worker_id()
        lane = lax.iota(I32, SC_LANES)

        def copy(u, slot):
            ids = idx_v.at[pl.ds(_sc_unit_off(u), SC_LANES)]
            return pltpu.make_async_copy(tab_hbm.at[ids], rows[slot], sems[slot])

        def compute(u, slot):
            tt = u // heads

            def body(c, accs):
                hv = h_v[tt, pl.ds(pl.multiple_of(c * SC_LANES, SC_LANES), SC_LANES)]
                return tuple(accs[r] + _sc_row_piece(rows[slot], r, c) * hv for r in range(SC_LANES))

            accs = lax.fori_loop(0, pieces, body,
                                 tuple(jnp.zeros((SC_LANES,), F32) for _ in range(SC_LANES)))
            for r in range(SC_LANES):
                red_v[pl.ds(r * SC_LANES, SC_LANES)] = accs[r]
            cols = [plsc.load_gather(red_v, [lane * SC_LANES + j]) for j in range(SC_LANES)]
            while len(cols) > 1:
                cols = [cols[i] + cols[i + 1] for i in range(0, len(cols), 2)]
            out_v[pl.ds(_sc_unit_off(u), SC_LANES)] = cols[0]

        @pl.loop(0, groups)
        def _(gi):
            base = wid * tpw + gi * g
            pltpu.sync_copy(idx_hbm.at[pl.ds(base * nsel, g * nsel)], idx_v)
            pltpu.sync_copy(h_hbm.at[pl.ds(base, g)], h_v)
            _sc_ring(units, lambda u, s: copy(u, s).start(), lambda u, s: copy(u, s).wait(), compute)
            pltpu.sync_copy(out_v, out_hbm.at[pl.ds(base * nsel, g * nsel)])

    return k(table, idx_flat, h)


def peer_combine_sc(table, idx_flat, coef_flat, t):
    d = table.shape[1] * table.shape[2]
    nsel = PEER_SEL
    tpw = t // SC_WORKERS
    g = SC_GROUP
    groups = tpw // g
    heads = nsel // SC_LANES
    pieces = d // SC_LANES
    units = g * heads
    row_buf = pltpu.VMEM((SC_LANES, SC_ROW_SUB, SC_ROW_LANE), F32)

    @functools.partial(
        pl.kernel, mesh=_sc_mesh(),
        out_type=jax.ShapeDtypeStruct((t, d), F32),
        scratch_types=[
            pltpu.VMEM((g * nsel,), I32),
            pltpu.VMEM((g * nsel,), F32),
            pltpu.VMEM((g, d), F32),
            [row_buf] * SC_RING,
            [pltpu.SemaphoreType.DMA] * SC_RING,
        ],
        compiler_params=_sc_params(),
        name="peer_combine_sc",
    )
    def k(tab_hbm, idx_hbm, coef_hbm, out_hbm, idx_v, coef_v, y_v, rows, sems):
        wid = _sc_worker_id()

        def copy(u, slot):
            ids = idx_v.at[pl.ds(_sc_unit_off(u), SC_LANES)]
            return pltpu.make_async_copy(tab_hbm.at[ids], rows[slot], sems[slot])

        def compute(u, slot):
            tt = u // heads
            first = (u % heads) == 0
            cs = [plsc.load_gather(coef_v, [jnp.full((SC_LANES,), u * SC_LANES + r, I32)])
                  for r in range(SC_LANES)]

            @plsc.parallel_loop(0, pieces, unroll=2)
            def _(c):
                off = pl.multiple_of(c * SC_LANES, SC_LANES)
                terms = [cs[r] * _sc_row_piece(rows[slot], r, c) for r in range(SC_LANES)]
                while len(terms) > 1:
                    terms = [terms[i] + terms[i + 1] for i in range(0, len(terms), 2)]
                prev = y_v[tt, pl.ds(off, SC_LANES)]
                y_v[tt, pl.ds(off, SC_LANES)] = terms[0] + jnp.where(first, 0.0, prev)

        @pl.loop(0, groups)
        def _(gi):
            base = wid * tpw + gi * g
            pltpu.sync_copy(idx_hbm.at[pl.ds(base * nsel, g * nsel)], idx_v)
            pltpu.sync_copy(coef_hbm.at[pl.ds(base * nsel, g * nsel)], coef_v)
            _sc_ring(units, lambda u, s: copy(u, s).start(), lambda u, s: copy(u, s).wait(), compute)
            pltpu.sync_copy(y_v, out_hbm.at[pl.ds(base, g)])

    return k(table, idx_flat, coef_flat)


GELU_C0 = math.sqrt(2.0 / math.pi)
GELU_C1 = 0.044715


def _gelu_tanh(x):
    z = GELU_C0 * (x + GELU_C1 * (x * x * x))
    th = 1.0 - 2.0 / (jnp.exp(2.0 * z) + 1.0)
    return 0.5 * x * (1.0 + th)


def peer_experts_sc(tab_u, tab_v, idx_flat, w_flat, h):
    t, d = h.shape
    nsel = PEER_SEL
    tpw = t // SC_WORKERS
    g = SC_GROUP
    groups = tpw // g
    heads = nsel // SC_LANES
    pieces = d // SC_LANES
    units = g * heads
    row_buf = pltpu.VMEM((SC_LANES, SC_ROW_SUB, SC_ROW_LANE), F32)

    @functools.partial(
        pl.kernel, mesh=_sc_mesh(),
        out_type=jax.ShapeDtypeStruct((t, d), F32),
        scratch_types=[
            pltpu.VMEM((g * nsel,), I32),
            pltpu.VMEM((g * nsel,), F32),
            pltpu.VMEM((g, d), F32),
            pltpu.VMEM((g, d), F32),
            pltpu.VMEM((SC_LANES * SC_LANES,), F32),
            [row_buf] * SC_RING,
            [pltpu.SemaphoreType.DMA] * SC_RING,
        ],
        compiler_params=_sc_params(),
        name="peer_experts_sc",
    )
    def k(u_hbm, v_hbm, idx_hbm, w_hbm, h_hbm, out_hbm, idx_v, coef_v, h_v, y_v, red_v, rows, sems):
        wid = _sc_worker_id()
        lane = lax.iota(I32, SC_LANES)

        def copy(tab_hbm, u, slot):
            ids = idx_v.at[pl.ds(_sc_unit_off(u), SC_LANES)]
            return pltpu.make_async_copy(tab_hbm.at[ids], rows[slot], sems[slot])

        def dots(u, slot):
            tt = u // heads

            def body(c, accs):
                hv = h_v[tt, pl.ds(pl.multiple_of(c * SC_LANES, SC_LANES), SC_LANES)]
                return tuple(accs[r] + _sc_row_piece(rows[slot], r, c) * hv for r in range(SC_LANES))

            accs = lax.fori_loop(0, pieces, body,
                                 tuple(jnp.zeros((SC_LANES,), F32) for _ in range(SC_LANES)))
            for r in range(SC_LANES):
                red_v[pl.ds(r * SC_LANES, SC_LANES)] = accs[r]
            cols = [plsc.load_gather(red_v, [lane * SC_LANES + j]) for j in range(SC_LANES)]
            while len(cols) > 1:
                cols = [cols[i] + cols[i + 1] for i in range(0, len(cols), 2)]
            sl = pl.ds(_sc_unit_off(u), SC_LANES)
            coef_v[sl] = coef_v[sl] * _gelu_tanh(cols[0])

        def combine(u, slot):
            tt = u // heads
            first = (u % heads) == 0
            cs = [plsc.load_gather(coef_v, [jnp.full((SC_LANES,), u * SC_LANES + r, I32)])
                  for r in range(SC_LANES)]

            @plsc.parallel_loop(0, pieces, unroll=2)
            def _(c):
                off = pl.multiple_of(c * SC_LANES, SC_LANES)
                terms = [cs[r] * _sc_row_piece(rows[slot], r, c) for r in range(SC_LANES)]
                while len(terms) > 1:
                    terms = [terms[i] + terms[i + 1] for i in range(0, len(terms), 2)]
                prev = y_v[tt, pl.ds(off, SC_LANES)]
                y_v[tt, pl.ds(off, SC_LANES)] = terms[0] + jnp.where(first, 0.0, prev)

        @pl.loop(0, groups)
        def _(gi):
            base = wid * tpw + gi * g
            pltpu.sync_copy(idx_hbm.at[pl.ds(base * nsel, g * nsel)], idx_v)
            pltpu.sync_copy(w_hbm.at[pl.ds(base * nsel, g * nsel)], coef_v)
            pltpu.sync_copy(h_hbm.at[pl.ds(base, g)], h_v)
            _sc_ring(units, lambda u, s: copy(u_hbm, u, s).start(), lambda u, s: copy(u_hbm, u, s).wait(), dots)
            _sc_ring(units, lambda u, s: copy(v_hbm, u, s).start(), lambda u, s: copy(v_hbm, u, s).wait(), combine)
            pltpu.sync_copy(y_v, out_hbm.at[pl.ds(base, g)])

    return k(tab_u, tab_v, idx_flat, w_flat, h)


SC_PK_RING = 8
SC_PK_SUB = 4
HI_MASK = -65536


def pack_bf16_pairs(a):
    half = a.shape[1] // 2
    bits = lax.bitcast_convert_type(a.astype(BF16), jnp.uint16).astype(jnp.uint32)
    return lax.bitcast_convert_type(bits[:, :half] | (bits[:, half:] << 16), I32)


def _unpack_halves(x32):
    w = plsc.bitcast(x32, I32)
    return plsc.bitcast(w << 16, F32), plsc.bitcast(w & HI_MASK, F32)


def _tree_sum(xs):
    while len(xs) > 1:
        xs = [xs[i] + xs[i + 1] for i in range(0, len(xs), 2)]
    return xs[0]


def peer_experts_pk_sc(tab_u, tab_v, idx_flat, w_flat, hp, d):
    t = hp.shape[0]
    nsel = PEER_SEL
    tpw = t // SC_WORKERS
    g = SC_GROUP
    groups = tpw // g
    heads = nsel // SC_LANES
    chunks = d // 32
    units = g * heads
    ring = SC_PK_RING
    row_buf = pltpu.VMEM((SC_LANES, SC_PK_SUB, SC_ROW_LANE), I32)

    def row_words(rows, r, wc):
        per = SC_ROW_LANE // SC_LANES
        return plsc.bitcast(rows[r, wc // per, pl.ds(pl.multiple_of((wc % per) * SC_LANES, SC_LANES), SC_LANES)],
                            BF16)

    def ring_loop(n_units, start, wait, compute):
        for u in range(ring - 1):
            start(u, u)

        @pl.loop(0, n_units, step=ring)
        def _(uu):
            for b in range(ring):
                u = uu + b
                nxt = u + (ring - 1)

                @pl.when(nxt < n_units)
                def _():
                    start(nxt, (b + ring - 1) % ring)

                wait(u, b)
                compute(u, b)

    @functools.partial(
        pl.kernel, mesh=_sc_mesh(),
        out_type=jax.ShapeDtypeStruct((t, d), F32),
        scratch_types=[
            pltpu.VMEM((g * nsel,), I32),
            pltpu.VMEM((g * nsel,), F32),
            pltpu.VMEM((g, d // 2), I32),
            pltpu.VMEM((g, d), F32),
            pltpu.VMEM((SC_LANES * SC_LANES,), F32),
            [row_buf] * ring,
            [pltpu.SemaphoreType.DMA] * ring,
        ],
        compiler_params=_sc_params(),
        name="peer_experts_pk_sc",
    )
    def k(u_hbm, v_hbm, idx_hbm, w_hbm, h_hbm, out_hbm, idx_v, coef_v, h_v, y_v, red_v, rows, sems):
        wid = _sc_worker_id()
        lane = lax.iota(I32, SC_LANES)

        def copy(tab_hbm, u, slot):
            ids = idx_v.at[pl.ds(_sc_unit_off(u), SC_LANES)]
            return pltpu.make_async_copy(tab_hbm.at[ids], rows[slot], sems[slot])

        def dots(u, slot):
            tt = u // heads

            def body(cp, accs):
                out = []
                hv = [plsc.bitcast(h_v[tt, pl.ds(pl.multiple_of((2 * cp + i) * SC_LANES, SC_LANES), SC_LANES)], BF16)
                      for i in range(2)]
                for r in range(SC_LANES):
                    pr = row_words(rows[slot], r, 2 * cp) * hv[0] + row_words(rows[slot], r, 2 * cp + 1) * hv[1]
                    lo, hi = _unpack_halves(pr)
                    out.append(accs[r] + lo + hi)
                return tuple(out)

            accs = lax.fori_loop(0, chunks // 2, body,
                                 tuple(jnp.zeros((SC_LANES,), F32) for _ in range(SC_LANES)))
            for r in range(SC_LANES):
                red_v[pl.ds(r * SC_LANES, SC_LANES)] = accs[r]
            act = _tree_sum([plsc.load_gather(red_v, [lane * SC_LANES + j]) for j in range(SC_LANES)])
            sl = pl.ds(_sc_unit_off(u), SC_LANES)
            coef_v[sl] = coef_v[sl] * _gelu_tanh(act)

        def combine(u, slot):
            tt = u // heads
            first = (u % heads) == 0
            cb = []
            for r in range(SC_LANES):
                c = plsc.load_gather(coef_v, [jnp.full((SC_LANES,), u * SC_LANES + r, I32)])
                cb.append(plsc.pack(c, c, format=plsc.PackFormat.INTERLEAVED))

            @plsc.parallel_loop(0, chunks, unroll=2)
            def _(wc):
                lo, hi = _unpack_halves(_tree_sum([cb[r] * row_words(rows[slot], r, wc) for r in range(SC_LANES)]))
                for half, val in ((0, lo), (1, hi)):
                    sl = pl.ds(pl.multiple_of(half * (d // 2) + wc * SC_LANES, SC_LANES), SC_LANES)
                    y_v[tt, sl] = val + jnp.where(first, 0.0, y_v[tt, sl])

        @pl.loop(0, groups)
        def _(gi):
            base = wid * tpw + gi * g
            pltpu.sync_copy(idx_hbm.at[pl.ds(base * nsel, g * nsel)], idx_v)
            pltpu.sync_copy(w_hbm.at[pl.ds(base * nsel, g * nsel)], coef_v)
            pltpu.sync_copy(h_hbm.at[pl.ds(base, g)], h_v)
            ring_loop(units, lambda u, s: copy(u_hbm, u, s).start(), lambda u, s: copy(u_hbm, u, s).wait(), dots)
            ring_loop(units, lambda u, s: copy(v_hbm, u, s).start(), lambda u, s: copy(v_hbm, u, s).wait(), combine)
            pltpu.sync_copy(y_v, out_hbm.at[pl.ds(base, g)])

    return k(tab_u, tab_v, idx_flat, w_flat, hp)


def kernel(x, mem, rel_bias, ln_mix, w_in, hg_lower, hg_norm, w_up_a, w_up_b, w_out, ln_cross, ln_mem, wq_x, wk_x, wv_x, wo_x, ln_ffn, peer_query, peer_subkeys, peer_u, peer_v, ln_final):
    b, s, d = x.shape
    depth = w_in.shape[0]
    assert depth == 1, "the residual after PEER is fused into the final norm"
    assert s % MB_BLOCK == 0 and s % HG_CHUNK == 0 and s % (PEER_SLICES * SC_WORKERS * SC_GROUP) == 0
    nb = s // MB_BLOCK
    row = lambda a: a.reshape(1, -1).astype(F32)
    lb_all = jnp.cumsum(jax.nn.softmax(hg_lower.astype(F32), axis=0), axis=0)
    bias = moba_bias_tiles(rel_bias)
    n_hg = 4 * HG_WIDTH
    n_qk = 2 * MB_WIDTH
    n_mb = 3 * MB_WIDTH
    l = 0
    w = w_in[l].astype(BF16)
    w_hg, w_qk, w_vt, w_g = w[:, :n_hg], w[:, n_hg:n_hg + n_qk], w[:, n_hg + n_qk:n_hg + n_mb].T, w[:, n_hg + n_mb:]
    wa, wb, wo = w_up_a[l].astype(BF16), w_up_b[l].astype(BF16), w_out[l].astype(BF16)
    wqx, wox = wq_x[l].astype(BF16), wo_x[l].astype(BF16)
    wpq, sk = peer_query[l].astype(BF16), peer_subkeys[l].astype(F32)
    tab3 = lambda a: pack_bf16_pairs(a.astype(F32)).reshape(a.shape[0], SC_PK_SUB, SC_ROW_LANE)
    tab_u, tab_v = tab3(peer_u[l]), tab3(peer_v[l])
    kx, vx = mem_kv(mem, row(ln_mem[l]), wk_x[l].astype(BF16), wv_x[l].astype(BF16))

    outs = []
    for bi in range(b):
        x2d = x[bi]
        p0, pqk, vt, pg = in_proj(x2d, row(ln_mix[l]), w_hg, w_qk, w_vt, w_g)
        ya = hgrn2(p0, row(lb_all[l]), row(hg_norm[l]), 1, s)
        km = moba_kmean(pqk, 1, s).reshape(1, nb, MB_WIDTH)
        ts = s // PEER_SLICES
        for tok0 in range(0, s, ts):
            yb = moba_attention(pqk, vt, km, bias, 1, s, tok0 // MB_BLOCK, ts // MB_BLOCK)
            xs = mix_out(x2d, ya, yb, pg, wa, wb, wo, tok0)
            xs = cross_attn(xs, row(ln_cross[l]), wqx, kx[bi:bi + 1], vx[bi:bi + 1], wox, ts)
            hp, eidx, wts = peer_route(xs, row(ln_ffn[l]), wpq, sk, 0, ts)
            y = peer_experts_pk_sc(tab_u, tab_v, eidx.reshape(ts * PEER_SEL), wts.reshape(ts * PEER_SEL), hp, d)
            outs.append(final_norm(xs, y, row(ln_final), 0))
    return jnp.concatenate(outs, axis=0).reshape(b, s, d)
```

```python
import functools
import math

import jax
import jax.numpy as jnp
import numpy as np
from jax import lax
from jax.experimental import pallas as pl
from jax.experimental.pallas import tpu as pltpu
from jax.experimental.pallas import tpu_sc as plsc

F32 = jnp.float32
BF16 = jnp.bfloat16
I32 = jnp.int32
EPS = 1e-6
NEG_INF = float("-inf")

HG_HEADS = 4
HG_D = 128
HG_WIDTH = HG_HEADS * HG_D
HG_CHUNK = 64
HG_SUB = 16
MB_HEADS = 8
MB_DH = 64
MB_WIDTH = MB_HEADS * MB_DH
MB_BLOCK = 256
MB_TOPK = 3
MB_BIAS_TILES = 8
REL_BUCKETS = 32
REL_MAX_DIST = 2048
X_HEADS = 4
PEER_HEADS = 8
PEER_NKEYS = 128
PEER_TOPK = 16
PEER_HALF = 128
PEER_SEL = PEER_HEADS * PEER_TOPK
PEER_SLICES = 4

VMEM_LIMIT = 56 * 1024 * 1024


def _cparams(sem):
    return pltpu.CompilerParams(dimension_semantics=sem, vmem_limit_bytes=VMEM_LIMIT)


def _rms(x, g):
    ms = jnp.mean(x * x, axis=-1, keepdims=True)
    return x * lax.rsqrt(ms + EPS) * g


def _in_proj_kernel(x_ref, g_ref, w0_ref, w1_ref, wvt_ref, w2_ref, o0_ref, o1_ref, ovt_ref, o2_ref):
    h = _rms(x_ref[...], g_ref[...]).astype(BF16)
    o0_ref[...] = jnp.dot(h, w0_ref[...], preferred_element_type=F32)
    o1_ref[...] = jnp.dot(h, w1_ref[...], preferred_element_type=F32).astype(BF16)
    vt = lax.dot_general(wvt_ref[...], h, (((1,), (1,)), ((), ())), preferred_element_type=F32).astype(BF16)
    for g in range(MB_WIDTH // MB_LG):
        ovt_ref[0, g * MB_VROWS:g * MB_VROWS + MB_LG, :] = vt[g * MB_LG:(g + 1) * MB_LG]
        ovt_ref[0, g * MB_VROWS + MB_LG:(g + 1) * MB_VROWS, :] = jnp.ones((MB_ONES, vt.shape[1]), BF16)
    o2_ref[...] = jnp.dot(h, w2_ref[...], preferred_element_type=F32).astype(BF16)


def in_proj(x2d, g, w0, w1, wvt, w2):
    t, d = x2d.shape
    tm = MB_BLOCK
    assert wvt.shape[0] == MB_WIDTH
    n0, n1, nv, n2 = w0.shape[1], w1.shape[1], MB_VT_ROWS, w2.shape[1]
    full = lambda a: pl.BlockSpec(a.shape, lambda i: (0, 0))
    return pl.pallas_call(
        _in_proj_kernel,
        grid=(t // tm,),
        in_specs=[pl.BlockSpec((tm, d), lambda i: (i, 0)), full(g), full(w0), full(w1), full(wvt), full(w2)],
        out_specs=[pl.BlockSpec((tm, n0), lambda i: (i, 0)),
                   pl.BlockSpec((tm, n1), lambda i: (i, 0)),
                   pl.BlockSpec((1, nv, tm), lambda i: (i, 0, 0)),
                   pl.BlockSpec((tm, n2), lambda i: (i, 0))],
        out_shape=[jax.ShapeDtypeStruct((t, n0), F32),
                   jax.ShapeDtypeStruct((t, n1), BF16),
                   jax.ShapeDtypeStruct((t // tm, nv, tm), BF16),
                   jax.ShapeDtypeStruct((t, n2), BF16)],
        compiler_params=_cparams(("parallel",)),
        name="in_proj",
    )(x2d, g, w0, w1, wvt, w2)


def _hgrn_kernel(q_ref, f_ref, i_ref, g_ref, lb_ref, gain_ref, o_ref, st_ref):
    c = pl.program_id(1)

    @pl.when(c == 0)
    def _():
        st_ref[...] = jnp.zeros_like(st_ref)

    C, S = HG_CHUNK, HG_SUB
    row = lax.broadcasted_iota(I32, (C, C), 0)
    col = lax.broadcasted_iota(I32, (C, C), 1)
    tril = (row >= col).astype(F32)
    t_iota = lax.broadcasted_iota(I32, (S, 1), 0)

    for h in range(HG_HEADS):
        sl = slice(h * HG_D, (h + 1) * HG_D)
        q = q_ref[:, sl]
        v = i_ref[:, sl]
        lb = lb_ref[:, sl]
        f = lb + (1.0 - lb) * jax.nn.sigmoid(f_ref[:, sl])
        lf = jnp.log(f)
        k = 1.0 - f
        b = jnp.dot(tril, lf, precision=lax.Precision.HIGHEST, preferred_element_type=F32)
        st = st_ref[h]
        vb = v.astype(BF16)
        qd = (q * jnp.exp(b)).astype(BF16)
        o_inter = lax.dot_general(qd, st.astype(BF16), (((1,), (1,)), ((), ())),
                                  preferred_element_type=F32)
        outs = []
        for i in range(C // S):
            r0 = i * S
            qi = q[r0:r0 + S]
            ki = k[r0:r0 + S]
            bi = b[r0:r0 + S]
            vi = v[r0:r0 + S]
            oi = o_inter[r0:r0 + S]
            if i > 0:
                bs = b[r0 - 1:r0]
                qh = (qi * jnp.exp(bi - bs)).astype(BF16)
                kh = (k[:r0] * jnp.exp(bs - b[:r0])).astype(BF16)
                a = lax.dot_general(qh, kh, (((1,), (1,)), ((), ())), preferred_element_type=F32)
                oi = oi + jnp.dot(a.astype(BF16), vb[:r0], preferred_element_type=F32)
            for s in range(S):
                dec = jnp.exp(jnp.minimum(bi - bi[s:s + 1], 0.0))
                p = qi * ki[s:s + 1] * dec
                a_s = jnp.sum(p, axis=-1, keepdims=True)
                a_s = jnp.where(t_iota >= s, a_s, 0.0)
                oi = oi + a_s * vi[s:s + 1]
            outs.append(oi)
        o = jnp.concatenate(outs, axis=0)
        b_end = b[C - 1:C]
        kd = (k * jnp.exp(b_end - b)).astype(BF16)
        upd = lax.dot_general(vb, kd, (((0,), (0,)), ((), ())), preferred_element_type=F32)
        st_ref[h] = st * jnp.exp(b_end) + upd
        o = o * lax.rsqrt(jnp.mean(o * o, axis=-1, keepdims=True) + EPS)
        g = g_ref[:, sl]
        o_ref[:, sl] = (o * gain_ref[:, sl] * (g * jax.nn.sigmoid(g))).astype(o_ref.dtype)


def hgrn2(p0, lb, gain, batch, seq):
    t = p0.shape[0]
    nc = seq // HG_CHUNK
    w = HG_WIDTH

    def col(j):
        return pl.BlockSpec((HG_CHUNK, w), lambda b, c, j=j: (b * nc + c, j))

    return pl.pallas_call(
        _hgrn_kernel,
        grid=(batch, nc),
        in_specs=[col(0), col(1), col(2), col(3),
                  pl.BlockSpec((1, w), lambda b, c: (0, 0)),
                  pl.BlockSpec((1, w), lambda b, c: (0, 0))],
        out_specs=pl.BlockSpec((HG_CHUNK, w), lambda b, c: (b * nc + c, 0)),
        out_shape=jax.ShapeDtypeStruct((t, w), BF16),
        scratch_shapes=[pltpu.VMEM((HG_HEADS, HG_D, HG_D), F32)],
        compiler_params=_cparams(("parallel", "arbitrary")),
        name="hgrn2",
    )(p0, p0, p0, p0, lb, gain)


def _kmean_kernel(k_ref, o_ref):
    o_ref[0] = jnp.mean(k_ref[...].astype(F32), axis=0, keepdims=True)


def moba_kmean(p1, batch, seq):
    nbt = p1.shape[0] // MB_BLOCK
    return pl.pallas_call(
        _kmean_kernel,
        grid=(nbt,),
        in_specs=[pl.BlockSpec((MB_BLOCK, MB_WIDTH), lambda i: (i, 1))],
        out_specs=pl.BlockSpec((1, 1, MB_WIDTH), lambda i: (i, 0, 0)),
        out_shape=jax.ShapeDtypeStruct((nbt, 1, MB_WIDTH), F32),
        compiler_params=_cparams(("parallel",)),
        name="moba_kmean",
    )(p1)


MB_PAIR = 4
MB_PW = MB_PAIR * MB_DH
MB_LG = 128
MB_ONES = 16
MB_VROWS = MB_LG + MB_ONES
MB_VT_ROWS = (MB_WIDTH // MB_LG) * MB_VROWS


def _moba_kernel(q_ref, k_ref, vt_ref, km_ref, bias_ref, o_ref, *scratch, qb0):
    m_ref, l_ref, al_ref, acc_ref, msk_ref, s_ref, p_ref = (
        scratch[i * MB_PAIR:(i + 1) * MB_PAIR] for i in range(7))
    qi = pl.program_id(2) + qb0
    nb = km_ref.shape[0]
    blk = MB_BLOCK
    heads = range(MB_PAIR)
    grp = lambda hh: slice((hh // 2) * MB_LG, (hh // 2 + 1) * MB_LG)
    q = q_ref[...]
    lane = lax.broadcasted_iota(I32, (blk, MB_LG), 1)
    in_head = [(lane < MB_DH) if hh % 2 == 0 else (lane >= MB_DH) for hh in heads]
    qs = q * jnp.asarray(MB_DH ** -0.5, BF16)
    qh = [jnp.where(in_head[hh], qs[:, grp(hh)], jnp.zeros((blk, MB_LG), BF16)) for hh in heads]
    nt = (((1,), (1,)), ((), ()))

    qf = q.astype(F32)
    n_io = lax.broadcasted_iota(I32, (nb, blk), 0)
    for hh in heads:
        gate = lax.dot_general(km_ref[:, grp(hh)], jnp.where(in_head[hh], qf[:, grp(hh)], 0.0), nt,
                               precision=lax.Precision.HIGHEST, preferred_element_type=F32)
        gate = jnp.where(n_io < qi, gate, NEG_INF)
        chosen = n_io < 0
        for _ in range(MB_TOPK):
            mx = jnp.max(gate, axis=0, keepdims=True)
            ix = jnp.min(jnp.where(gate == mx, n_io, nb), axis=0, keepdims=True)
            hit = n_io == ix
            chosen = chosen | (hit & (mx > NEG_INF))
            gate = jnp.where(hit, NEG_INF, gate)
        msk_ref[hh][...] = jnp.where(chosen, 0.0, NEG_INF)

    own_rows = lambda r, hh: r[(hh % 2) * MB_DH:(hh % 2 + 1) * MB_DH]

    vgrp = lambda hh: slice((hh // 2) * MB_VROWS, (hh // 2 + 1) * MB_VROWS)

    def pv_stage(blk_idx):
        vtb = vt_ref[blk_idx]
        r = [jnp.dot(vtb[vgrp(hh)], p_ref[hh][...], preferred_element_type=F32) for hh in heads]
        al = [al_ref[hh][...] for hh in heads]
        a_new = [al[hh] * acc_ref[hh][...] + own_rows(r[hh], hh) for hh in heads]
        l_new = [al[hh] * l_ref[hh][...] + r[hh][MB_LG:MB_LG + 1] for hh in heads]
        return a_new, l_new

    def store_pv(a_new, l_new):
        for hh in heads:
            acc_ref[hh][...] = a_new[hh]
            l_ref[hh][...] = l_new[hh]

    def softmax_stage():
        s = [s_ref[hh][...] for hh in heads]
        m_old = [m_ref[hh][...] for hh in heads]
        m_new = [jnp.maximum(m_old[hh], jnp.max(s[hh], axis=0, keepdims=True)) for hh in heads]
        alpha = [jnp.exp(m_old[hh] - m_new[hh]) for hh in heads]
        p = [jnp.exp((s[hh] - m_new[hh]).astype(BF16)) for hh in heads]
        return p, alpha, m_new

    def store_softmax(p, alpha, m_new):
        for hh in heads:
            p_ref[hh][...] = p[hh]
            al_ref[hh][...] = alpha[hh]
            m_ref[hh][...] = m_new[hh]

    k_own = k_ref[pl.ds(pl.multiple_of(qi * blk, blk), blk), :]
    key_io = lax.broadcasted_iota(I32, (blk, blk), 0)
    qry_io = lax.broadcasted_iota(I32, (blk, blk), 1)
    for hh in heads:
        s = lax.dot_general(k_own[:, grp(hh)], qh[hh], nt, preferred_element_type=F32) + bias_ref[hh, 0]
        s_ref[hh][...] = jnp.where(key_io <= qry_io, s, NEG_INF)
        m_ref[hh][...] = jnp.full((1, blk), NEG_INF, F32)
        l_ref[hh][...] = jnp.zeros((1, blk), F32)
        al_ref[hh][...] = jnp.ones((1, blk), F32)
        acc_ref[hh][...] = jnp.zeros((MB_DH, blk), F32)
        p_ref[hh][...] = jnp.zeros((blk, blk), BF16)

    def step(i, carry):
        pv = pv_stage(jnp.where(i <= 1, qi, i - 2))
        sm = softmax_stage()
        kn = k_ref[pl.ds(pl.multiple_of(i * blk, blk), blk), :]
        d = jnp.minimum(qi - i, MB_BIAS_TILES - 1)
        s_next = [lax.dot_general(kn[:, grp(hh)], qh[hh], nt, preferred_element_type=F32)
                  + bias_ref[hh, d] + msk_ref[hh][pl.ds(i, 1), :] for hh in heads]
        store_pv(*pv)
        for hh in heads:
            s_ref[hh][...] = s_next[hh]
        store_softmax(*sm)
        return carry

    lax.fori_loop(0, qi, step, 0)
    pv = pv_stage(jnp.where(qi <= 1, qi, qi - 2))
    sm = softmax_stage()
    store_pv(*pv)
    store_softmax(*sm)
    a_fin, l_fin = pv_stage(jnp.where(qi == 0, qi, qi - 1))
    out_t = jnp.concatenate([a_fin[hh] / l_fin[hh] for hh in heads], axis=0)
    o_ref[...] = out_t.T.astype(o_ref.dtype)


def moba_attention(pqk, vt, km, bias, batch, seq, qb0=0, nqb=None):
    nb = seq // MB_BLOCK
    nqb = nb if nqb is None else nqb
    t = batch * nqb * MB_BLOCK
    groups = MB_WIDTH // MB_PW
    return pl.pallas_call(
        functools.partial(_moba_kernel, qb0=qb0),
        grid=(batch, groups, nqb),
        in_specs=[
            pl.BlockSpec((MB_BLOCK, MB_PW), lambda b, j, i: (b * nb + qb0 + i, j)),
            pl.BlockSpec((seq, MB_PW), lambda b, j, i: (b, groups + j)),
            pl.BlockSpec((nb, (MB_PW // MB_LG) * MB_VROWS, MB_BLOCK), lambda b, j, i: (b, j, 0)),
            pl.BlockSpec((None, nb, MB_PW), lambda b, j, i: (b, 0, j)),
            pl.BlockSpec((MB_PAIR, MB_BIAS_TILES, MB_BLOCK, MB_BLOCK), lambda b, j, i: (j, 0, 0, 0)),
        ],
        out_specs=pl.BlockSpec((MB_BLOCK, MB_PW), lambda b, j, i: (b * nqb + i, j)),
        out_shape=jax.ShapeDtypeStruct((t, MB_WIDTH), BF16),
        scratch_shapes=(
            [pltpu.VMEM((1, MB_BLOCK), F32)] * (3 * MB_PAIR)
            + [pltpu.VMEM((MB_DH, MB_BLOCK), F32)] * MB_PAIR
            + [pltpu.VMEM((nb, MB_BLOCK), F32)] * MB_PAIR
            + [pltpu.VMEM((MB_BLOCK, MB_BLOCK), F32)] * MB_PAIR
            + [pltpu.VMEM((MB_BLOCK, MB_BLOCK), BF16)] * MB_PAIR
        ),
        compiler_params=_cparams(("parallel", "parallel", "arbitrary")),
        name="moba_attn",
    )(pqk, pqk, vt, km, bias)


def _t5_bucket(dist):
    max_exact = REL_BUCKETS // 2
    scaled = jnp.log(jnp.maximum(dist, 1).astype(F32) / max_exact) / math.log(REL_MAX_DIST / max_exact)
    large = jnp.minimum(max_exact + (scaled * (REL_BUCKETS - max_exact)).astype(I32), REL_BUCKETS - 1)
    return jnp.where(dist < max_exact, dist, large)


def moba_bias_tiles(rel_bias):
    blk = MB_BLOCK
    span = 2 * blk - 1
    x = jnp.arange(span) - (blk - 1)
    dist = jnp.maximum(jnp.arange(MB_BIAS_TILES)[:, None] * blk + x[None, :], 0)
    w = rel_bias.astype(F32).T[:, _t5_bucket(dist)]
    h = w.shape[0]
    wp = jnp.pad(w, ((0, 0), (0, 0), (0, 1)))
    a = jnp.broadcast_to(wp[:, :, None, :], (h, MB_BIAS_TILES, blk, span + 1))
    a = a.reshape(h, MB_BIAS_TILES, blk * (span + 1))[:, :, :blk * span]
    return a.reshape(h, MB_BIAS_TILES, blk, span)[:, :, :, blk - 1:]


def _mix_kernel(x_ref, ya_ref, yb_ref, ga_ref, gb_ref, wa_ref, wb_ref, wo_ref, o_ref):
    za = jnp.dot(ya_ref[...], wa_ref[...], preferred_element_type=F32)
    zb = jnp.dot(yb_ref[...], wb_ref[...], preferred_element_type=F32)
    z = jax.nn.sigmoid(ga_ref[...].astype(F32)) * za + jax.nn.sigmoid(gb_ref[...].astype(F32)) * zb
    o_ref[...] = x_ref[...] + jnp.dot(z.astype(BF16), wo_ref[...], preferred_element_type=F32)


def mix_out(x2d, ya, yb, pg, wa, wb, wo, tok0=0, tm=256):
    t = yb.shape[0]
    d = x2d.shape[1]
    w = ya.shape[1]
    b0 = tok0 // tm
    return pl.pallas_call(
        _mix_kernel,
        grid=(t // tm,),
        in_specs=[
            pl.BlockSpec((tm, d), lambda i: (b0 + i, 0)),
            pl.BlockSpec((tm, w), lambda i: (b0 + i, 0)),
            pl.BlockSpec((tm, w), lambda i: (i, 0)),
            pl.BlockSpec((tm, d), lambda i: (b0 + i, 0)),
            pl.BlockSpec((tm, d), lambda i: (b0 + i, 1)),
            pl.BlockSpec((w, d), lambda i: (0, 0)),
            pl.BlockSpec((w, d), lambda i: (0, 0)),
            pl.BlockSpec((d, d), lambda i: (0, 0)),
        ],
        out_specs=pl.BlockSpec((tm, d), lambda i: (i, 0)),
        out_shape=jax.ShapeDtypeStruct((t, d), F32),
        compiler_params=_cparams(("parallel",)),
        name="mix_out",
    )(x2d, ya, yb, pg, pg, wa, wb, wo)


def _mem_kv_kernel(m_ref, g_ref, wk_ref, wv_ref, k_ref, v_ref):
    mn = _rms(m_ref[...], g_ref[...]).astype(BF16)
    k_ref[...] = jnp.dot(mn, wk_ref[...], preferred_element_type=F32).astype(BF16)
    v_ref[...] = jnp.dot(mn, wv_ref[...], preferred_element_type=F32).astype(BF16)


def mem_kv(mem, g, wk, wv):
    b, m, d = mem.shape
    spec = pl.BlockSpec((None, m, d), lambda i: (i, 0, 0))
    wspec = pl.BlockSpec((d, d), lambda i: (0, 0))
    return pl.pallas_call(
        _mem_kv_kernel,
        grid=(b,),
        in_specs=[spec, pl.BlockSpec((1, d), lambda i: (0, 0)), wspec, wspec],
        out_specs=[spec, spec],
        out_shape=[jax.ShapeDtypeStruct((b, m, d), BF16)] * 2,
        compiler_params=_cparams(("parallel",)),
        name="mem_kv",
    )(mem, g, wk, wv)


def _cross_kernel(x_ref, g_ref, wq_ref, k_ref, v_ref, wo_ref, o_ref):
    x = x_ref[...]
    d = x.shape[1]
    dh = d // X_HEADS
    h = _rms(x, g_ref[...]).astype(BF16)
    q = (jnp.dot(h, wq_ref[...], preferred_element_type=F32) * (dh ** -0.5)).astype(BF16)
    outs = []
    for hh in range(X_HEADS):
        sl = slice(hh * dh, (hh + 1) * dh)
        s = lax.dot_general(q[:, sl], k_ref[:, sl], (((1,), (1,)), ((), ())),
                            preferred_element_type=F32)
        p = jnp.exp(s - jnp.max(s, axis=1, keepdims=True))
        l = jnp.sum(p, axis=1, keepdims=True)
        o = jnp.dot(p.astype(BF16), v_ref[:, sl], preferred_element_type=F32) / l
        outs.append(o.astype(BF16))
    o = jnp.concatenate(outs, axis=1)
    o_ref[...] = x + jnp.dot(o, wo_ref[...], preferred_element_type=F32)


def cross_attn(x2d, g, wq, kx, vx, wo, seq, tm=256):
    t, d = x2d.shape
    m = kx.shape[1]
    per_b = seq // tm
    kv = pl.BlockSpec((None, m, d), lambda i: (i // per_b, 0, 0))
    wspec = pl.BlockSpec((d, d), lambda i: (0, 0))
    return pl.pallas_call(
        _cross_kernel,
        grid=(t // tm,),
        in_specs=[pl.BlockSpec((tm, d), lambda i: (i, 0)), pl.BlockSpec((1, d), lambda i: (0, 0)),
                  wspec, kv, kv, wspec],
        out_specs=pl.BlockSpec((tm, d), lambda i: (i, 0)),
        out_shape=jax.ShapeDtypeStruct((t, d), F32),
        compiler_params=_cparams(("parallel",)),
        name="cross_attn",
    )(x2d, g, wq, kx, vx, wo)


def _topk_rows(sc, k):
    n = sc.shape[0]
    io = lax.broadcasted_iota(I32, sc.shape, 0)
    vals, ids = [], []
    for _ in range(k):
        m = jnp.max(sc, axis=0, keepdims=True)
        ix = jnp.min(jnp.where(sc == m, io, n), axis=0, keepdims=True)
        vals.append(m)
        ids.append(ix)
        sc = jnp.where(io == ix, NEG_INF, sc)
    return jnp.concatenate(vals, axis=0), jnp.concatenate(ids, axis=0)


def _pack_bf16_halves(h):
    bits = lax.bitcast_convert_type(h, I32)
    r = bits + 0x7FFF + (lax.shift_right_logical(bits, 16) & 1)
    half = h.shape[1] // 2
    return lax.shift_right_logical(r[:, :half], 16) | (r[:, half:] & HI_MASK)


def _route_kernel(x_ref, g_ref, wq_ref, sk_ref, hp_ref, idx_ref, w_ref, hb_ref, it_ref, wt_ref):
    p = pl.program_id(1)

    @pl.when(p == 0)
    def _():
        h = _rms(x_ref[...], g_ref[...])
        hp_ref[...] = _pack_bf16_halves(h)
        hb_ref[...] = h.astype(BF16)

    qh = jnp.dot(hb_ref[...], wq_ref[...], preferred_element_type=F32)
    tops = []
    for c in range(2):
        seg = qh[:, c * PEER_HALF:(c + 1) * PEER_HALF]
        sc = lax.dot_general(sk_ref[c], seg, (((1,), (1,)), ((), ())),
                             precision=lax.Precision.HIGHEST, preferred_element_type=F32)
        tops.append(_topk_rows(sc, PEER_TOPK))
    (s0, i0), (s1, i1) = tops
    k = PEER_TOPK
    sub = 8
    tm = s0.shape[1]
    r8 = lax.broadcasted_iota(I32, (sub, tm), 0)
    r16 = lax.broadcasted_iota(I32, (k, tm), 0)
    cand_b = [s0[0:1] + s1, s0[1:2] + s1[:sub]]
    cidx_b = [i0[0:1] * PEER_NKEYS + i1, i0[1:2] * PEER_NKEYS + i1[:sub]]
    pos_b = [r16, k + r8]
    for a in range(2, sub):
        keep = r8 < (k // (a + 1))
        cand_b.append(jnp.where(keep, s0[a:a + 1] + s1[:sub], NEG_INF))
        cidx_b.append(i0[a:a + 1] * PEER_NKEYS + i1[:sub])
        pos_b.append(a * k + r8)
    cand_b.append(s0[sub:] + s1[0:1])
    cidx_b.append(i0[sub:] * PEER_NKEYS + i1[0:1])
    pos_b.append((sub + r8) * k)
    cand = jnp.concatenate(cand_b, axis=0)
    cidx = jnp.concatenate(cidx_b, axis=0)
    pos = jnp.concatenate(pos_b, axis=0)
    vals, ids = [], []
    for _ in range(k):
        m = jnp.max(cand, axis=0, keepdims=True)
        px = jnp.min(jnp.where(cand == m, pos, k * k), axis=0, keepdims=True)
        hit = pos == px
        vals.append(m)
        ids.append(jnp.sum(jnp.where(hit, cidx, 0), axis=0, keepdims=True))
        cand = jnp.where(hit, NEG_INF, cand)
    sf = jnp.concatenate(vals, axis=0)
    e = jnp.exp(sf - sf[0:1])
    rows = pl.ds(pl.multiple_of(p * PEER_TOPK, PEER_TOPK), PEER_TOPK)
    wt_ref[rows, :] = e / jnp.sum(e, axis=0, keepdims=True)
    it_ref[rows, :] = jnp.concatenate(ids, axis=0)

    @pl.when(p == pl.num_programs(1) - 1)
    def _():
        idx_ref[...] = it_ref[...].T
        w_ref[...] = wt_ref[...].T


def peer_route(x2d, g, wq, sk, tok0, t, tm=256):
    d = x2d.shape[1]
    ph = sk.shape[0]
    nsel = ph * PEER_TOPK
    blk0 = tok0 // tm
    return pl.pallas_call(
        _route_kernel,
        grid=(t // tm, ph),
        in_specs=[
            pl.BlockSpec((tm, d), lambda i, p: (blk0 + i, 0)),
            pl.BlockSpec((1, d), lambda i, p: (0, 0)),
            pl.BlockSpec((d, 2 * PEER_HALF), lambda i, p: (0, p)),
            pl.BlockSpec((None, 2, PEER_NKEYS, PEER_HALF), lambda i, p: (p, 0, 0, 0)),
        ],
        out_specs=[
            pl.BlockSpec((tm, d // 2), lambda i, p: (i, 0)),
            pl.BlockSpec((tm, nsel), lambda i, p: (i, 0)),
            pl.BlockSpec((tm, nsel), lambda i, p: (i, 0)),
        ],
        out_shape=[jax.ShapeDtypeStruct((t, d // 2), I32),
                   jax.ShapeDtypeStruct((t, nsel), I32),
                   jax.ShapeDtypeStruct((t, nsel), F32)],
        scratch_shapes=[pltpu.VMEM((tm, d), BF16),
                        pltpu.VMEM((nsel, tm), I32),
                        pltpu.VMEM((nsel, tm), F32)],
        compiler_params=_cparams(("parallel", "arbitrary")),
        name="peer_route",
    )(x2d, g, wq, sk)


def _coef_kernel(w_ref, a_ref, o_ref):
    o_ref[...] = w_ref[...] * jax.nn.gelu(a_ref[...])


def peer_coef(w, act, tm=1024):
    t, n = w.shape
    spec = pl.BlockSpec((tm, n), lambda i: (i, 0))
    return pl.pallas_call(
        _coef_kernel, grid=(t // tm,), in_specs=[spec, spec], out_specs=spec,
        out_shape=jax.ShapeDtypeStruct((t, n), F32),
        compiler_params=_cparams(("parallel",)), name="peer_coef",
    )(w, act)


def _final_kernel(x_ref, y_ref, g_ref, o_ref):
    o_ref[...] = _rms(x_ref[...] + y_ref[...], g_ref[...])


def final_norm(x2d, y, g, tok0, tm=512):
    t, d = y.shape
    blk0 = tok0 // tm
    spec = pl.BlockSpec((tm, d), lambda i: (i, 0))
    return pl.pallas_call(
        _final_kernel, grid=(t // tm,),
        in_specs=[pl.BlockSpec((tm, d), lambda i: (blk0 + i, 0)), spec, pl.BlockSpec((1, d), lambda i: (0, 0))],
        out_specs=spec,
        out_shape=jax.ShapeDtypeStruct((t, d), F32),
        compiler_params=_cparams(("parallel",)), name="final_norm",
    )(x2d, y, g)


SC_CORES = 2
SC_SUBCORES = 16
SC_WORKERS = SC_CORES * SC_SUBCORES
SC_LANES = 16
SC_GROUP = 16


def _sc_mesh():
    return plsc.VectorSubcoreMesh(core_axis_name="c", subcore_axis_name="s")


def _sc_params():
    return pltpu.CompilerParams(needs_layout_passes=False)


def _sc_worker_id():
    return lax.axis_index("s") * SC_CORES + lax.axis_index("c")


SC_RING = 4
SC_ROW_SUB = 8
SC_ROW_LANE = 128


def _sc_ring(n_units, start, wait, compute):
    for u in range(SC_RING - 1):
        start(u, u)

    @pl.loop(0, n_units, step=SC_RING)
    def _(uu):
        for b in range(SC_RING):
            u = uu + b
            nxt = u + (SC_RING - 1)

            @pl.when(nxt < n_units)
            def _():
                start(nxt, (b + SC_RING - 1) % SC_RING)

            wait(u, b)
            compute(u, b)


def _sc_unit_off(u):
    off = u * SC_LANES
    return off if isinstance(off, int) else pl.multiple_of(off, SC_LANES)


def _sc_row_piece(rows, r, c):
    per = SC_ROW_LANE // SC_LANES
    return rows[r, c // per, pl.ds(pl.multiple_of((c % per) * SC_LANES, SC_LANES), SC_LANES)]


def peer_dots_sc(table, idx_flat, h):
    t, d = h.shape
    nsel = PEER_SEL
    tpw = t // SC_WORKERS
    g = SC_GROUP
    groups = tpw // g
    heads = nsel // SC_LANES
    pieces = d // SC_LANES
    units = g * heads
    row_buf = pltpu.VMEM((SC_LANES, SC_ROW_SUB, SC_ROW_LANE), F32)

    @functools.partial(
        pl.kernel, mesh=_sc_mesh(),
        out_type=jax.ShapeDtypeStruct((t * nsel,), F32),
        scratch_types=[
            pltpu.VMEM((g * nsel,), I32),
            pltpu.VMEM((g, d), F32),
            pltpu.VMEM((g * nsel,), F32),
            pltpu.VMEM((SC_LANES * SC_LANES,), F32),
            [row_buf] * SC_RING,
            [pltpu.SemaphoreType.DMA] * SC_RING,
        ],
        compiler_params=_sc_params(),
        name="peer_dots_sc",
    )
    def k(tab_hbm, idx_hbm, h_hbm, out_hbm, idx_v, h_v, out_v, red_v, rows, sems):
        wid = _sc_worker_id()
        lane = lax.iota(I32, SC_LANES)

        def copy(u, slot):
            ids = idx_v.at[pl.ds(_sc_unit_off(u), SC_LANES)]
            return pltpu.make_async_copy(tab_hbm.at[ids], rows[slot], sems[slot])

        def compute(u, slot):
            tt = u // heads

            def body(c, accs):
                hv = h_v[tt, pl.ds(pl.multiple_of(c * SC_LANES, SC_LANES), SC_LANES)]
                return tuple(accs[r] + _sc_row_piece(rows[slot], r, c) * hv for r in range(SC_LANES))

            accs = lax.fori_loop(0, pieces, body,
                                 tuple(jnp.zeros((SC_LANES,), F32) for _ in range(SC_LANES)))
            for r in range(SC_LANES):
                red_v[pl.ds(r * SC_LANES, SC_LANES)] = accs[r]
            cols = [plsc.load_gather(red_v, [lane * SC_LANES + j]) for j in range(SC_LANES)]
            while len(cols) > 1:
                cols = [cols[i] + cols[i + 1] for i in range(0, len(cols), 2)]
            out_v[pl.ds(_sc_unit_off(u), SC_LANES)] = cols[0]

        @pl.loop(0, groups)
        def _(gi):
            base = wid * tpw + gi * g
            pltpu.sync_copy(idx_hbm.at[pl.ds(base * nsel, g * nsel)], idx_v)
            pltpu.sync_copy(h_hbm.at[pl.ds(base, g)], h_v)
            _sc_ring(units, lambda u, s: copy(u, s).start(), lambda u, s: copy(u, s).wait(), compute)
            pltpu.sync_copy(out_v, out_hbm.at[pl.ds(base * nsel, g * nsel)])

    return k(table, idx_flat, h)


def peer_combine_sc(table, idx_flat, coef_flat, t):
    d = table.shape[1] * table.shape[2]
    nsel = PEER_SEL
    tpw = t // SC_WORKERS
    g = SC_GROUP
    groups = tpw // g
    heads = nsel // SC_LANES
    pieces = d // SC_LANES
    units = g * heads
    row_buf = pltpu.VMEM((SC_LANES, SC_ROW_SUB, SC_ROW_LANE), F32)

    @functools.partial(
        pl.kernel, mesh=_sc_mesh(),
        out_type=jax.ShapeDtypeStruct((t, d), F32),
        scratch_types=[
            pltpu.VMEM((g * nsel,), I32),
            pltpu.VMEM((g * nsel,), F32),
            pltpu.VMEM((g, d), F32),
            [row_buf] * SC_RING,
            [pltpu.SemaphoreType.DMA] * SC_RING,
        ],
        compiler_params=_sc_params(),
        name="peer_combine_sc",
    )
    def k(tab_hbm, idx_hbm, coef_hbm, out_hbm, idx_v, coef_v, y_v, rows, sems):
        wid = _sc_worker_id()

        def copy(u, slot):
            ids = idx_v.at[pl.ds(_sc_unit_off(u), SC_LANES)]
            return pltpu.make_async_copy(tab_hbm.at[ids], rows[slot], sems[slot])

        def compute(u, slot):
            tt = u // heads
            first = (u % heads) == 0
            cs = [plsc.load_gather(coef_v, [jnp.full((SC_LANES,), u * SC_LANES + r, I32)])
                  for r in range(SC_LANES)]

            @plsc.parallel_loop(0, pieces, unroll=2)
            def _(c):
                off = pl.multiple_of(c * SC_LANES, SC_LANES)
                terms = [cs[r] * _sc_row_piece(rows[slot], r, c) for r in range(SC_LANES)]
                while len(terms) > 1:
                    terms = [terms[i] + terms[i + 1] for i in range(0, len(terms), 2)]
                prev = y_v[tt, pl.ds(off, SC_LANES)]
                y_v[tt, pl.ds(off, SC_LANES)] = terms[0] + jnp.where(first, 0.0, prev)

        @pl.loop(0, groups)
        def _(gi):
            base = wid * tpw + gi * g
            pltpu.sync_copy(idx_hbm.at[pl.ds(base * nsel, g * nsel)], idx_v)
            pltpu.sync_copy(coef_hbm.at[pl.ds(base * nsel, g * nsel)], coef_v)
            _sc_ring(units, lambda u, s: copy(u, s).start(), lambda u, s: copy(u, s).wait(), compute)
            pltpu.sync_copy(y_v, out_hbm.at[pl.ds(base, g)])

    return k(table, idx_flat, coef_flat)


GELU_C0 = math.sqrt(2.0 / math.pi)
GELU_C1 = 0.044715


def _gelu_tanh(x):
    z = GELU_C0 * (x + GELU_C1 * (x * x * x))
    th = 1.0 - 2.0 / (jnp.exp(2.0 * z) + 1.0)
    return 0.5 * x * (1.0 + th)


def peer_experts_sc(tab_u, tab_v, idx_flat, w_flat, h):
    t, d = h.shape
    nsel = PEER_SEL
    tpw = t // SC_WORKERS
    g = SC_GROUP
    groups = tpw // g
    heads = nsel // SC_LANES
    pieces = d // SC_LANES
    units = g * heads
    row_buf = pltpu.VMEM((SC_LANES, SC_ROW_SUB, SC_ROW_LANE), F32)

    @functools.partial(
        pl.kernel, mesh=_sc_mesh(),
        out_type=jax.ShapeDtypeStruct((t, d), F32),
        scratch_types=[
            pltpu.VMEM((g * nsel,), I32),
            pltpu.VMEM((g * nsel,), F32),
            pltpu.VMEM((g, d), F32),
            pltpu.VMEM((g, d), F32),
            pltpu.VMEM((SC_LANES * SC_LANES,), F32),
            [row_buf] * SC_RING,
            [pltpu.SemaphoreType.DMA] * SC_RING,
        ],
        compiler_params=_sc_params(),
        name="peer_experts_sc",
    )
    def k(u_hbm, v_hbm, idx_hbm, w_hbm, h_hbm, out_hbm, idx_v, coef_v, h_v, y_v, red_v, rows, sems):
        wid = _sc_worker_id()
        lane = lax.iota(I32, SC_LANES)

        def copy(tab_hbm, u, slot):
            ids = idx_v.at[pl.ds(_sc_unit_off(u), SC_LANES)]
            return pltpu.make_async_copy(tab_hbm.at[ids], rows[slot], sems[slot])

        def dots(u, slot):
            tt = u // heads

            def body(c, accs):
                hv = h_v[tt, pl.ds(pl.multiple_of(c * SC_LANES, SC_LANES), SC_LANES)]
                return tuple(accs[r] + _sc_row_piece(rows[slot], r, c) * hv for r in range(SC_LANES))

            accs = lax.fori_loop(0, pieces, body,
                                 tuple(jnp.zeros((SC_LANES,), F32) for _ in range(SC_LANES)))
            for r in range(SC_LANES):
                red_v[pl.ds(r * SC_LANES, SC_LANES)] = accs[r]
            cols = [plsc.load_gather(red_v, [lane * SC_LANES + j]) for j in range(SC_LANES)]
            while len(cols) > 1:
                cols = [cols[i] + cols[i + 1] for i in range(0, len(cols), 2)]
            sl = pl.ds(_sc_unit_off(u), SC_LANES)
            coef_v[sl] = coef_v[sl] * _gelu_tanh(cols[0])

        def combine(u, slot):
            tt = u // heads
            first = (u % heads) == 0
            cs = [plsc.load_gather(coef_v, [jnp.full((SC_LANES,), u * SC_LANES + r, I32)])
                  for r in range(SC_LANES)]

            @plsc.parallel_loop(0, pieces, unroll=2)
            def _(c):
                off = pl.multiple_of(c * SC_LANES, SC_LANES)
                terms = [cs[r] * _sc_row_piece(rows[slot], r, c) for r in range(SC_LANES)]
                while len(terms) > 1:
                    terms = [terms[i] + terms[i + 1] for i in range(0, len(terms), 2)]
                prev = y_v[tt, pl.ds(off, SC_LANES)]
                y_v[tt, pl.ds(off, SC_LANES)] = terms[0] + jnp.where(first, 0.0, prev)

        @pl.loop(0, groups)
        def _(gi):
            base = wid * tpw + gi * g
            pltpu.sync_copy(idx_hbm.at[pl.ds(base * nsel, g * nsel)], idx_v)
            pltpu.sync_copy(w_hbm.at[pl.ds(base * nsel, g * nsel)], coef_v)
            pltpu.sync_copy(h_hbm.at[pl.ds(base, g)], h_v)
            _sc_ring(units, lambda u, s: copy(u_hbm, u, s).start(), lambda u, s: copy(u_hbm, u, s).wait(), dots)
            _sc_ring(units, lambda u, s: copy(v_hbm, u, s).start(), lambda u, s: copy(v_hbm, u, s).wait(), combine)
            pltpu.sync_copy(y_v, out_hbm.at[pl.ds(base, g)])

    return k(tab_u, tab_v, idx_flat, w_flat, h)


SC_PK_RING = 4
SC_PK_SUB = 4
HI_MASK = -65536


def pack_bf16_pairs(a):
    half = a.shape[1] // 2
    bits = lax.bitcast_convert_type(a.astype(BF16), jnp.uint16).astype(jnp.uint32)
    return lax.bitcast_convert_type(bits[:, :half] | (bits[:, half:] << 16), I32)


def _unpack_halves(x32):
    w = plsc.bitcast(x32, I32)
    return plsc.bitcast(w << 16, F32), plsc.bitcast(w & HI_MASK, F32)


def _tree_sum(xs):
    while len(xs) > 1:
        xs = [xs[i] + xs[i + 1] for i in range(0, len(xs), 2)]
    return xs[0]


def peer_experts_pk_sc(tab_uv, idx_flat, w_flat, hp, d):
    t = hp.shape[0]
    nsel = PEER_SEL
    tpw = t // SC_WORKERS
    g = SC_GROUP
    groups = tpw // g
    heads = nsel // SC_LANES
    chunks = d // 32
    units = g * heads
    ring = SC_PK_RING
    row_buf = pltpu.VMEM((SC_LANES, 2 * SC_PK_SUB, SC_ROW_LANE), I32)

    def row_words(rows, r, wc, sub0):
        per = SC_ROW_LANE // SC_LANES
        return plsc.bitcast(
            rows[r, sub0 + wc // per, pl.ds(pl.multiple_of((wc % per) * SC_LANES, SC_LANES), SC_LANES)], BF16)

    def ring_loop(n_units, start, wait, compute):
        for u in range(ring - 1):
            start(u, u)

        @pl.loop(0, n_units, step=ring)
        def _(uu):
            for b in range(ring):
                u = uu + b
                nxt = u + (ring - 1)

                @pl.when(nxt < n_units)
                def _():
                    start(nxt, (b + ring - 1) % ring)

                wait(u, b)
                compute(u, b)

    @functools.partial(
        pl.kernel, mesh=_sc_mesh(),
        out_type=jax.ShapeDtypeStruct((t, d), F32),
        scratch_types=[
            pltpu.VMEM((g * nsel,), I32),
            pltpu.VMEM((g * nsel,), F32),
            pltpu.VMEM((g, d // 2), I32),
            pltpu.VMEM((g, d), F32),
            pltpu.VMEM((SC_LANES * SC_LANES,), F32),
            [row_buf] * ring,
            [pltpu.SemaphoreType.DMA] * ring,
        ],
        compiler_params=_sc_params(),
        name="peer_experts_pk_sc",
    )
    def k(tab_hbm, idx_hbm, w_hbm, h_hbm, out_hbm, idx_v, coef_v, h_v, y_v, red_v, rows, sems):
        wid = _sc_worker_id()
        lane = lax.iota(I32, SC_LANES)

        def copy(u, slot):
            ids = idx_v.at[pl.ds(_sc_unit_off(u), SC_LANES)]
            return pltpu.make_async_copy(tab_hbm.at[ids], rows[slot], sems[slot])

        def dots(u, slot):
            tt = u // heads

            def body(cp, accs):
                out = []
                hv = [plsc.bitcast(h_v[tt, pl.ds(pl.multiple_of((2 * cp + i) * SC_LANES, SC_LANES), SC_LANES)], BF16)
                      for i in range(2)]
                for r in range(SC_LANES):
                    pr = (row_words(rows[slot], r, 2 * cp, 0) * hv[0]
                          + row_words(rows[slot], r, 2 * cp + 1, 0) * hv[1])
                    lo, hi = _unpack_halves(pr)
                    out.append(accs[r] + lo + hi)
                return tuple(out)

            accs = lax.fori_loop(0, chunks // 2, body,
                                 tuple(jnp.zeros((SC_LANES,), F32) for _ in range(SC_LANES)))
            for r in range(SC_LANES):
                red_v[pl.ds(r * SC_LANES, SC_LANES)] = accs[r]
            act = _tree_sum([plsc.load_gather(red_v, [lane * SC_LANES + j]) for j in range(SC_LANES)])
            sl = pl.ds(_sc_unit_off(u), SC_LANES)
            coef_v[sl] = coef_v[sl] * _gelu_tanh(act)

        def combine(u, slot):
            tt = u // heads
            first = (u % heads) == 0
            cb = []
            for r in range(SC_LANES):
                c = plsc.load_gather(coef_v, [jnp.full((SC_LANES,), u * SC_LANES + r, I32)])
                cb.append(plsc.pack(c, c, format=plsc.PackFormat.INTERLEAVED))

            @plsc.parallel_loop(0, chunks, unroll=2)
            def _(wc):
                lo, hi = _unpack_halves(
                    _tree_sum([cb[r] * row_words(rows[slot], r, wc, SC_PK_SUB) for r in range(SC_LANES)]))
                for half, val in ((0, lo), (1, hi)):
                    sl = pl.ds(pl.multiple_of(half * (d // 2) + wc * SC_LANES, SC_LANES), SC_LANES)
                    y_v[tt, sl] = val + jnp.where(first, 0.0, y_v[tt, sl])

        def unit(u, slot):
            dots(u, slot)
            combine(u, slot)

        @pl.loop(0, groups)
        def _(gi):
            base = wid * tpw + gi * g
            pltpu.sync_copy(idx_hbm.at[pl.ds(base * nsel, g * nsel)], idx_v)
            pltpu.sync_copy(w_hbm.at[pl.ds(base * nsel, g * nsel)], coef_v)
            pltpu.sync_copy(h_hbm.at[pl.ds(base, g)], h_v)
            ring_loop(units, lambda u, s: copy(u, s).start(), lambda u, s: copy(u, s).wait(), unit)
            pltpu.sync_copy(y_v, out_hbm.at[pl.ds(base, g)])

    return k(tab_uv, idx_flat, w_flat, hp)


def kernel(x, mem, rel_bias, ln_mix, w_in, hg_lower, hg_norm, w_up_a, w_up_b, w_out, ln_cross, ln_mem, wq_x, wk_x, wv_x, wo_x, ln_ffn, peer_query, peer_subkeys, peer_u, peer_v, ln_final):
    b, s, d = x.shape
    depth = w_in.shape[0]
    assert depth == 1, "the residual after PEER is fused into the final norm"
    assert s % MB_BLOCK == 0 and s % HG_CHUNK == 0 and s % (PEER_SLICES * SC_WORKERS * SC_GROUP) == 0
    nb = s // MB_BLOCK
    row = lambda a: a.reshape(1, -1).astype(F32)
    lb_all = jnp.cumsum(jax.nn.softmax(hg_lower.astype(F32), axis=0), axis=0)
    bias = moba_bias_tiles(rel_bias)
    n_hg = 4 * HG_WIDTH
    n_qk = 2 * MB_WIDTH
    n_mb = 3 * MB_WIDTH
    l = 0
    w = w_in[l].astype(BF16)
    w_hg, w_qk, w_vt, w_g = w[:, :n_hg], w[:, n_hg:n_hg + n_qk], w[:, n_hg + n_qk:n_hg + n_mb].T, w[:, n_hg + n_mb:]
    wa, wb, wo = w_up_a[l].astype(BF16), w_up_b[l].astype(BF16), w_out[l].astype(BF16)
    wqx, wox = wq_x[l].astype(BF16), wo_x[l].astype(BF16)
    wpq, sk = peer_query[l].astype(BF16), peer_subkeys[l].astype(F32)
    tab3 = lambda a: pack_bf16_pairs(a.astype(F32)).reshape(a.shape[0], SC_PK_SUB, SC_ROW_LANE)
    tab_uv = jnp.concatenate([tab3(peer_u[l]), tab3(peer_v[l])], axis=1)
    kx, vx = mem_kv(mem, row(ln_mem[l]), wk_x[l].astype(BF16), wv_x[l].astype(BF16))

    outs = []
    for bi in range(b):
        x2d = x[bi]
        p0, pqk, vt, pg = in_proj(x2d, row(ln_mix[l]), w_hg, w_qk, w_vt, w_g)
        ya = hgrn2(p0, row(lb_all[l]), row(hg_norm[l]), 1, s)
        km = moba_kmean(pqk, 1, s).reshape(1, nb, MB_WIDTH)
        ts = s // PEER_SLICES
        for tok0 in range(0, s, ts):
            yb = moba_attention(pqk, vt, km, bias, 1, s, tok0 // MB_BLOCK, ts // MB_BLOCK)
            xs = mix_out(x2d, ya, yb, pg, wa, wb, wo, tok0)
            xs = cross_attn(xs, row(ln_cross[l]), wqx, kx[bi:bi + 1], vx[bi:bi + 1], wox, ts)
            hp, eidx, wts = peer_route(xs, row(ln_ffn[l]), wpq, sk, 0, ts)
            y = peer_experts_pk_sc(tab_uv, eidx.reshape(ts * PEER_SEL), wts.reshape(ts * PEER_SEL), hp, d)
            outs.append(final_norm(xs, y, row(ln_final), 0))
    return jnp.concatenate(outs, axis=0).reshape(b, s, d)
```

```python
import functools
import math

import jax
import jax.numpy as jnp
import numpy as np
from jax import lax
from jax.experimental import pallas as pl
from jax.experimental.pallas import tpu as pltpu
from jax.experimental.pallas import tpu_sc as plsc

F32 = jnp.float32
BF16 = jnp.bfloat16
I32 = jnp.int32
EPS = 1e-6
NEG_INF = float("-inf")

HG_HEADS = 4
HG_D = 128
HG_WIDTH = HG_HEADS * HG_D
HG_CHUNK = 64
HG_SUB = 16
MB_HEADS = 8
MB_DH = 64
MB_WIDTH = MB_HEADS * MB_DH
MB_BLOCK = 256
MB_TOPK = 3
MB_BIAS_TILES = 8
REL_BUCKETS = 32
REL_MAX_DIST = 2048
X_HEADS = 4
PEER_HEADS = 8
PEER_NKEYS = 128
PEER_TOPK = 16
PEER_HALF = 128
PEER_SEL = PEER_HEADS * PEER_TOPK
PEER_SLICES = 4

VMEM_LIMIT = 56 * 1024 * 1024


def _cparams(sem):
    return pltpu.CompilerParams(dimension_semantics=sem, vmem_limit_bytes=VMEM_LIMIT)


def _rms(x, g):
    ms = jnp.mean(x * x, axis=-1, keepdims=True)
    return x * lax.rsqrt(ms + EPS) * g


def _in_proj_kernel(x_ref, g_ref, w0_ref, w1_ref, wvt_ref, w2_ref, o0_ref, o1_ref, ovt_ref, o2_ref):
    h = _rms(x_ref[...], g_ref[...]).astype(BF16)
    o0_ref[...] = jnp.dot(h, w0_ref[...], preferred_element_type=F32)
    o1_ref[...] = jnp.dot(h, w1_ref[...], preferred_element_type=F32).astype(BF16)
    vt = lax.dot_general(wvt_ref[...], h, (((1,), (1,)), ((), ())), preferred_element_type=F32).astype(BF16)
    for g in range(MB_WIDTH // MB_LG):
        ovt_ref[0, g * MB_VROWS:g * MB_VROWS + MB_LG, :] = vt[g * MB_LG:(g + 1) * MB_LG]
        ovt_ref[0, g * MB_VROWS + MB_LG:(g + 1) * MB_VROWS, :] = jnp.ones((MB_ONES, vt.shape[1]), BF16)
    o2_ref[...] = jnp.dot(h, w2_ref[...], preferred_element_type=F32).astype(BF16)


def in_proj(x2d, g, w0, w1, wvt, w2):
    t, d = x2d.shape
    tm = MB_BLOCK
    assert wvt.shape[0] == MB_WIDTH
    n0, n1, nv, n2 = w0.shape[1], w1.shape[1], MB_VT_ROWS, w2.shape[1]
    full = lambda a: pl.BlockSpec(a.shape, lambda i: (0, 0))
    return pl.pallas_call(
        _in_proj_kernel,
        grid=(t // tm,),
        in_specs=[pl.BlockSpec((tm, d), lambda i: (i, 0)), full(g), full(w0), full(w1), full(wvt), full(w2)],
        out_specs=[pl.BlockSpec((tm, n0), lambda i: (i, 0)),
                   pl.BlockSpec((tm, n1), lambda i: (i, 0)),
                   pl.BlockSpec((1, nv, tm), lambda i: (i, 0, 0)),
                   pl.BlockSpec((tm, n2), lambda i: (i, 0))],
        out_shape=[jax.ShapeDtypeStruct((t, n0), F32),
                   jax.ShapeDtypeStruct((t, n1), BF16),
                   jax.ShapeDtypeStruct((t // tm, nv, tm), BF16),
                   jax.ShapeDtypeStruct((t, n2), BF16)],
        compiler_params=_cparams(("parallel",)),
        name="in_proj",
    )(x2d, g, w0, w1, wvt, w2)


def _hgrn_kernel(q_ref, f_ref, i_ref, g_ref, lb_ref, gain_ref, o_ref, st_ref):
    c = pl.program_id(1)

    @pl.when(c == 0)
    def _():
        st_ref[...] = jnp.zeros_like(st_ref)

    C, S = HG_CHUNK, HG_SUB
    row = lax.broadcasted_iota(I32, (C, C), 0)
    col = lax.broadcasted_iota(I32, (C, C), 1)
    tril = (row >= col).astype(F32)
    t_iota = lax.broadcasted_iota(I32, (S, 1), 0)

    for h in range(HG_HEADS):
        sl = slice(h * HG_D, (h + 1) * HG_D)
        q = q_ref[:, sl]
        v = i_ref[:, sl]
        lb = lb_ref[:, sl]
        f = lb + (1.0 - lb) * jax.nn.sigmoid(f_ref[:, sl])
        lf = jnp.log(f)
        k = 1.0 - f
        b = jnp.dot(tril, lf, precision=lax.Precision.HIGHEST, preferred_element_type=F32)
        st = st_ref[h]
        vb = v.astype(BF16)
        qd = (q * jnp.exp(b)).astype(BF16)
        o_inter = lax.dot_general(qd, st.astype(BF16), (((1,), (1,)), ((), ())),
                                  preferred_element_type=F32)
        outs = []
        for i in range(C // S):
            r0 = i * S
            qi = q[r0:r0 + S]
            ki = k[r0:r0 + S]
            bi = b[r0:r0 + S]
            vi = v[r0:r0 + S]
            oi = o_inter[r0:r0 + S]
            if i > 0:
                bs = b[r0 - 1:r0]
                qh = (qi * jnp.exp(bi - bs)).astype(BF16)
                kh = (k[:r0] * jnp.exp(bs - b[:r0])).astype(BF16)
                a = lax.dot_general(qh, kh, (((1,), (1,)), ((), ())), preferred_element_type=F32)
                oi = oi + jnp.dot(a.astype(BF16), vb[:r0], preferred_element_type=F32)
            for s in range(S):
                dec = jnp.exp(jnp.minimum(bi - bi[s:s + 1], 0.0))
                p = qi * ki[s:s + 1] * dec
                a_s = jnp.sum(p, axis=-1, keepdims=True)
                a_s = jnp.where(t_iota >= s, a_s, 0.0)
                oi = oi + a_s * vi[s:s + 1]
            outs.append(oi)
        o = jnp.concatenate(outs, axis=0)
        b_end = b[C - 1:C]
        kd = (k * jnp.exp(b_end - b)).astype(BF16)
        upd = lax.dot_general(vb, kd, (((0,), (0,)), ((), ())), preferred_element_type=F32)
        st_ref[h] = st * jnp.exp(b_end) + upd
        o = o * lax.rsqrt(jnp.mean(o * o, axis=-1, keepdims=True) + EPS)
        g = g_ref[:, sl]
        o_ref[:, sl] = (o * gain_ref[:, sl] * (g * jax.nn.sigmoid(g))).astype(o_ref.dtype)


def hgrn2(p0, lb, gain, batch, seq):
    t = p0.shape[0]
    nc = seq // HG_CHUNK
    w = HG_WIDTH

    def col(j):
        return pl.BlockSpec((HG_CHUNK, w), lambda b, c, j=j: (b * nc + c, j))

    return pl.pallas_call(
        _hgrn_kernel,
        grid=(batch, nc),
        in_specs=[col(0), col(1), col(2), col(3),
                  pl.BlockSpec((1, w), lambda b, c: (0, 0)),
                  pl.BlockSpec((1, w), lambda b, c: (0, 0))],
        out_specs=pl.BlockSpec((HG_CHUNK, w), lambda b, c: (b * nc + c, 0)),
        out_shape=jax.ShapeDtypeStruct((t, w), BF16),
        scratch_shapes=[pltpu.VMEM((HG_HEADS, HG_D, HG_D), F32)],
        compiler_params=_cparams(("parallel", "arbitrary")),
        name="hgrn2",
    )(p0, p0, p0, p0, lb, gain)


def _kmean_kernel(k_ref, o_ref):
    o_ref[0] = jnp.mean(k_ref[...].astype(F32), axis=0, keepdims=True)


def moba_kmean(p1, batch, seq):
    nbt = p1.shape[0] // MB_BLOCK
    return pl.pallas_call(
        _kmean_kernel,
        grid=(nbt,),
        in_specs=[pl.BlockSpec((MB_BLOCK, MB_WIDTH), lambda i: (i, 1))],
        out_specs=pl.BlockSpec((1, 1, MB_WIDTH), lambda i: (i, 0, 0)),
        out_shape=jax.ShapeDtypeStruct((nbt, 1, MB_WIDTH), F32),
        compiler_params=_cparams(("parallel",)),
        name="moba_kmean",
    )(p1)


MB_PAIR = 4
MB_PW = MB_PAIR * MB_DH
MB_LG = 128
MB_ONES = 16
MB_VROWS = MB_LG + MB_ONES
MB_VT_ROWS = (MB_WIDTH // MB_LG) * MB_VROWS


def _moba_kernel(q_ref, k_ref, vt_ref, km_ref, bias_ref, o_ref, *scratch, qb0):
    m_ref, l_ref, al_ref, acc_ref, msk_ref, s_ref, p_ref = (
        scratch[i * MB_PAIR:(i + 1) * MB_PAIR] for i in range(7))
    qi = pl.program_id(2) + qb0
    nb = km_ref.shape[0]
    blk = MB_BLOCK
    heads = range(MB_PAIR)
    grp = lambda hh: slice((hh // 2) * MB_LG, (hh // 2 + 1) * MB_LG)
    q = q_ref[...]
    lane = lax.broadcasted_iota(I32, (blk, MB_LG), 1)
    in_head = [(lane < MB_DH) if hh % 2 == 0 else (lane >= MB_DH) for hh in heads]
    qs = q * jnp.asarray(MB_DH ** -0.5, BF16)
    nt = (((1,), (1,)), ((), ()))
    qf = q.astype(F32)
    qht = [jnp.where(in_head[hh], qs[:, grp(hh)].astype(F32), 0.0).T.astype(BF16) for hh in heads]

    n_io = lax.broadcasted_iota(I32, (nb, blk), 0)
    for hh in heads:
        gate = lax.dot_general(km_ref[:, grp(hh)], jnp.where(in_head[hh], qf[:, grp(hh)], 0.0), nt,
                               precision=lax.Precision.HIGHEST, preferred_element_type=F32)
        gate = jnp.where(n_io < qi, gate, NEG_INF)
        chosen = n_io < 0
        for _ in range(MB_TOPK):
            mx = jnp.max(gate, axis=0, keepdims=True)
            ix = jnp.min(jnp.where(gate == mx, n_io, nb), axis=0, keepdims=True)
            hit = n_io == ix
            chosen = chosen | (hit & (mx > NEG_INF))
            gate = jnp.where(hit, NEG_INF, gate)
        msk_ref[hh][...] = jnp.where(chosen, 0.0, NEG_INF)

    own_rows = lambda r, hh: r[(hh % 2) * MB_DH:(hh % 2 + 1) * MB_DH]

    vgrp = lambda hh: slice((hh // 2) * MB_VROWS, (hh // 2 + 1) * MB_VROWS)

    def pv_stage(blk_idx):
        vtb = vt_ref[blk_idx]
        r = [jnp.dot(vtb[vgrp(hh)], p_ref[hh][...], preferred_element_type=F32) for hh in heads]
        al = [al_ref[hh][...] for hh in heads]
        a_new = [al[hh] * acc_ref[hh][...] + own_rows(r[hh], hh) for hh in heads]
        l_new = [al[hh] * l_ref[hh][...] + r[hh][MB_LG:MB_LG + 1] for hh in heads]
        return a_new, l_new

    def store_pv(a_new, l_new):
        for hh in heads:
            acc_ref[hh][...] = a_new[hh]
            l_ref[hh][...] = l_new[hh]

    def softmax_stage():
        s = [s_ref[hh][...] for hh in heads]
        m_old = [m_ref[hh][...] for hh in heads]
        m_new = [jnp.maximum(m_old[hh], jnp.max(s[hh], axis=0, keepdims=True)) for hh in heads]
        alpha = [jnp.exp(m_old[hh] - m_new[hh]) for hh in heads]
        p = [jnp.exp((s[hh] - m_new[hh]).astype(BF16)) for hh in heads]
        return p, alpha, m_new

    def store_softmax(p, alpha, m_new):
        for hh in heads:
            p_ref[hh][...] = p[hh]
            al_ref[hh][...] = alpha[hh]
            m_ref[hh][...] = m_new[hh]

    k_own = k_ref[pl.ds(pl.multiple_of(qi * blk, blk), blk), :]
    key_io = lax.broadcasted_iota(I32, (blk, blk), 0)
    qry_io = lax.broadcasted_iota(I32, (blk, blk), 1)
    for hh in heads:
        s = jnp.dot(k_own[:, grp(hh)], qht[hh], preferred_element_type=F32) + bias_ref[hh, 0]
        s_ref[hh][...] = jnp.where(key_io <= qry_io, s, NEG_INF)
        m_ref[hh][...] = jnp.full((1, blk), NEG_INF, F32)
        l_ref[hh][...] = jnp.zeros((1, blk), F32)
        al_ref[hh][...] = jnp.ones((1, blk), F32)
        acc_ref[hh][...] = jnp.zeros((MB_DH, blk), F32)
        p_ref[hh][...] = jnp.zeros((blk, blk), BF16)

    def step(i, carry):
        pv = pv_stage(jnp.where(i <= 1, qi, i - 2))
        sm = softmax_stage()
        kn = k_ref[pl.ds(pl.multiple_of(i * blk, blk), blk), :]
        d = jnp.minimum(qi - i, MB_BIAS_TILES - 1)
        s_next = [jnp.dot(kn[:, grp(hh)], qht[hh], preferred_element_type=F32)
                  + bias_ref[hh, d] + msk_ref[hh][pl.ds(i, 1), :] for hh in heads]
        store_pv(*pv)
        for hh in heads:
            s_ref[hh][...] = s_next[hh]
        store_softmax(*sm)
        return carry

    lax.fori_loop(0, qi, step, 0)
    pv = pv_stage(jnp.where(qi <= 1, qi, qi - 2))
    sm = softmax_stage()
    store_pv(*pv)
    store_softmax(*sm)
    a_fin, l_fin = pv_stage(jnp.where(qi == 0, qi, qi - 1))
    out_t = jnp.concatenate([a_fin[hh] / l_fin[hh] for hh in heads], axis=0)
    o_ref[...] = out_t.T.astype(o_ref.dtype)


def moba_attention(pqk, vt, km, bias, batch, seq, qb0=0, nqb=None):
    nb = seq // MB_BLOCK
    nqb = nb if nqb is None else nqb
    t = batch * nqb * MB_BLOCK
    groups = MB_WIDTH // MB_PW
    return pl.pallas_call(
        functools.partial(_moba_kernel, qb0=qb0),
        grid=(batch, groups, nqb),
        in_specs=[
            pl.BlockSpec((MB_BLOCK, MB_PW), lambda b, j, i: (b * nb + qb0 + i, j)),
            pl.BlockSpec((seq, MB_PW), lambda b, j, i: (b, groups + j)),
            pl.BlockSpec((nb, (MB_PW // MB_LG) * MB_VROWS, MB_BLOCK), lambda b, j, i: (b, j, 0)),
            pl.BlockSpec((None, nb, MB_PW), lambda b, j, i: (b, 0, j)),
            pl.BlockSpec((MB_PAIR, MB_BIAS_TILES, MB_BLOCK, MB_BLOCK), lambda b, j, i: (j, 0, 0, 0)),
        ],
        out_specs=pl.BlockSpec((MB_BLOCK, MB_PW), lambda b, j, i: (b * nqb + i, j)),
        out_shape=jax.ShapeDtypeStruct((t, MB_WIDTH), BF16),
        scratch_shapes=(
            [pltpu.VMEM((1, MB_BLOCK), F32)] * (3 * MB_PAIR)
            + [pltpu.VMEM((MB_DH, MB_BLOCK), F32)] * MB_PAIR
            + [pltpu.VMEM((nb, MB_BLOCK), F32)] * MB_PAIR
            + [pltpu.VMEM((MB_BLOCK, MB_BLOCK), F32)] * MB_PAIR
            + [pltpu.VMEM((MB_BLOCK, MB_BLOCK), BF16)] * MB_PAIR
        ),
        compiler_params=_cparams(("parallel", "parallel", "arbitrary")),
        name="moba_attn",
    )(pqk, pqk, vt, km, bias)


def _t5_bucket(dist):
    max_exact = REL_BUCKETS // 2
    scaled = jnp.log(jnp.maximum(dist, 1).astype(F32) / max_exact) / math.log(REL_MAX_DIST / max_exact)
    large = jnp.minimum(max_exact + (scaled * (REL_BUCKETS - max_exact)).astype(I32), REL_BUCKETS - 1)
    return jnp.where(dist < max_exact, dist, large)


def moba_bias_tiles(rel_bias):
    blk = MB_BLOCK
    span = 2 * blk - 1
    x = jnp.arange(span) - (blk - 1)
    dist = jnp.maximum(jnp.arange(MB_BIAS_TILES)[:, None] * blk + x[None, :], 0)
    w = rel_bias.astype(F32).T[:, _t5_bucket(dist)]
    h = w.shape[0]
    wp = jnp.pad(w, ((0, 0), (0, 0), (0, 1)))
    a = jnp.broadcast_to(wp[:, :, None, :], (h, MB_BIAS_TILES, blk, span + 1))
    a = a.reshape(h, MB_BIAS_TILES, blk * (span + 1))[:, :, :blk * span]
    return a.reshape(h, MB_BIAS_TILES, blk, span)[:, :, :, blk - 1:]


def _mix_kernel(x_ref, ya_ref, yb_ref, ga_ref, gb_ref, wa_ref, wb_ref, wo_ref, o_ref):
    za = jnp.dot(ya_ref[...], wa_ref[...], preferred_element_type=F32)
    zb = jnp.dot(yb_ref[...], wb_ref[...], preferred_element_type=F32)
    z = jax.nn.sigmoid(ga_ref[...].astype(F32)) * za + jax.nn.sigmoid(gb_ref[...].astype(F32)) * zb
    o_ref[...] = x_ref[...] + jnp.dot(z.astype(BF16), wo_ref[...], preferred_element_type=F32)


def mix_out(x2d, ya, yb, pg, wa, wb, wo, tok0=0, tm=256):
    t = yb.shape[0]
    d = x2d.shape[1]
    w = ya.shape[1]
    b0 = tok0 // tm
    return pl.pallas_call(
        _mix_kernel,
        grid=(t // tm,),
        in_specs=[
            pl.BlockSpec((tm, d), lambda i: (b0 + i, 0)),
            pl.BlockSpec((tm, w), lambda i: (b0 + i, 0)),
            pl.BlockSpec((tm, w), lambda i: (i, 0)),
            pl.BlockSpec((tm, d), lambda i: (b0 + i, 0)),
            pl.BlockSpec((tm, d), lambda i: (b0 + i, 1)),
            pl.BlockSpec((w, d), lambda i: (0, 0)),
            pl.BlockSpec((w, d), lambda i: (0, 0)),
            pl.BlockSpec((d, d), lambda i: (0, 0)),
        ],
        out_specs=pl.BlockSpec((tm, d), lambda i: (i, 0)),
        out_shape=jax.ShapeDtypeStruct((t, d), F32),
        compiler_params=_cparams(("parallel",)),
        name="mix_out",
    )(x2d, ya, yb, pg, pg, wa, wb, wo)


def _mem_kv_kernel(m_ref, g_ref, wk_ref, wv_ref, k_ref, v_ref):
    mn = _rms(m_ref[...], g_ref[...]).astype(BF16)
    k_ref[...] = jnp.dot(mn, wk_ref[...], preferred_element_type=F32).astype(BF16)
    v_ref[...] = jnp.dot(mn, wv_ref[...], preferred_element_type=F32).astype(BF16)


def mem_kv(mem, g, wk, wv):
    b, m, d = mem.shape
    spec = pl.BlockSpec((None, m, d), lambda i: (i, 0, 0))
    wspec = pl.BlockSpec((d, d), lambda i: (0, 0))
    return pl.pallas_call(
        _mem_kv_kernel,
        grid=(b,),
        in_specs=[spec, pl.BlockSpec((1, d), lambda i: (0, 0)), wspec, wspec],
        out_specs=[spec, spec],
        out_shape=[jax.ShapeDtypeStruct((b, m, d), BF16)] * 2,
        compiler_params=_cparams(("parallel",)),
        name="mem_kv",
    )(mem, g, wk, wv)


def _cross_kernel(x_ref, g_ref, wq_ref, k_ref, v_ref, wo_ref, o_ref):
    x = x_ref[...]
    d = x.shape[1]
    dh = d // X_HEADS
    h = _rms(x, g_ref[...]).astype(BF16)
    q = (jnp.dot(h, wq_ref[...], preferred_element_type=F32) * (dh ** -0.5)).astype(BF16)
    outs = []
    for hh in range(X_HEADS):
        sl = slice(hh * dh, (hh + 1) * dh)
        s = lax.dot_general(q[:, sl], k_ref[:, sl], (((1,), (1,)), ((), ())),
                            preferred_element_type=F32)
        p = jnp.exp(s - jnp.max(s, axis=1, keepdims=True))
        l = jnp.sum(p, axis=1, keepdims=True)
        o = jnp.dot(p.astype(BF16), v_ref[:, sl], preferred_element_type=F32) / l
        outs.append(o.astype(BF16))
    o = jnp.concatenate(outs, axis=1)
    o_ref[...] = x + jnp.dot(o, wo_ref[...], preferred_element_type=F32)


def cross_attn(x2d, g, wq, kx, vx, wo, seq, tm=256):
    t, d = x2d.shape
    m = kx.shape[1]
    per_b = seq // tm
    kv = pl.BlockSpec((None, m, d), lambda i: (i // per_b, 0, 0))
    wspec = pl.BlockSpec((d, d), lambda i: (0, 0))
    return pl.pallas_call(
        _cross_kernel,
        grid=(t // tm,),
        in_specs=[pl.BlockSpec((tm, d), lambda i: (i, 0)), pl.BlockSpec((1, d), lambda i: (0, 0)),
                  wspec, kv, kv, wspec],
        out_specs=pl.BlockSpec((tm, d), lambda i: (i, 0)),
        out_shape=jax.ShapeDtypeStruct((t, d), F32),
        compiler_params=_cparams(("parallel",)),
        name="cross_attn",
    )(x2d, g, wq, kx, vx, wo)


def _topk_rows(sc, k):
    n = sc.shape[0]
    io = lax.broadcasted_iota(I32, sc.shape, 0)
    vals, ids = [], []
    for _ in range(k):
        m = jnp.max(sc, axis=0, keepdims=True)
        ix = jnp.min(jnp.where(sc == m, io, n), axis=0, keepdims=True)
        vals.append(m)
        ids.append(ix)
        sc = jnp.where(io == ix, NEG_INF, sc)
    return jnp.concatenate(vals, axis=0), jnp.concatenate(ids, axis=0)


def _pack_bf16_halves(h):
    bits = lax.bitcast_convert_type(h, I32)
    r = bits + 0x7FFF + (lax.shift_right_logical(bits, 16) & 1)
    half = h.shape[1] // 2
    return lax.shift_right_logical(r[:, :half], 16) | (r[:, half:] & HI_MASK)


def _route_kernel(x_ref, g_ref, wq_ref, sk_ref, hp_ref, idx_ref, w_ref, hb_ref, it_ref, wt_ref):
    p = pl.program_id(1)

    @pl.when(p == 0)
    def _():
        h = _rms(x_ref[...], g_ref[...])
        hp_ref[...] = _pack_bf16_halves(h)
        hb_ref[...] = h.astype(BF16)

    qh = jnp.dot(hb_ref[...], wq_ref[...], preferred_element_type=F32)
    tops = []
    for c in range(2):
        seg = qh[:, c * PEER_HALF:(c + 1) * PEER_HALF]
        sc = lax.dot_general(sk_ref[c], seg, (((1,), (1,)), ((), ())),
                             precision=lax.Precision.HIGHEST, preferred_element_type=F32)
        tops.append(_topk_rows(sc, PEER_TOPK))
    (s0, i0), (s1, i1) = tops
    k = PEER_TOPK
    sub = 8
    tm = s0.shape[1]
    r8 = lax.broadcasted_iota(I32, (sub, tm), 0)
    r16 = lax.broadcasted_iota(I32, (k, tm), 0)
    cand_b = [s0[0:1] + s1, s0[1:2] + s1[:sub]]
    cidx_b = [i0[0:1] * PEER_NKEYS + i1, i0[1:2] * PEER_NKEYS + i1[:sub]]
    pos_b = [r16, k + r8]
    for a in range(2, sub):
        keep = r8 < (k // (a + 1))
        cand_b.append(jnp.where(keep, s0[a:a + 1] + s1[:sub], NEG_INF))
        cidx_b.append(i0[a:a + 1] * PEER_NKEYS + i1[:sub])
        pos_b.append(a * k + r8)
    cand_b.append(s0[sub:] + s1[0:1])
    cidx_b.append(i0[sub:] * PEER_NKEYS + i1[0:1])
    pos_b.append((sub + r8) * k)
    cand = jnp.concatenate(cand_b, axis=0)
    cidx = jnp.concatenate(cidx_b, axis=0)
    pos = jnp.concatenate(pos_b, axis=0)
    vals, ids = [], []
    for _ in range(k):
        m = jnp.max(cand, axis=0, keepdims=True)
        px = jnp.min(jnp.where(cand == m, pos, k * k), axis=0, keepdims=True)
        hit = pos == px
        vals.append(m)
        ids.append(jnp.sum(jnp.where(hit, cidx, 0), axis=0, keepdims=True))
        cand = jnp.where(hit, NEG_INF, cand)
    sf = jnp.concatenate(vals, axis=0)
    e = jnp.exp(sf - sf[0:1])
    rows = pl.ds(pl.multiple_of(p * PEER_TOPK, PEER_TOPK), PEER_TOPK)
    wt_ref[rows, :] = e / jnp.sum(e, axis=0, keepdims=True)
    it_ref[rows, :] = jnp.concatenate(ids, axis=0)

    @pl.when(p == pl.num_programs(1) - 1)
    def _():
        idx_ref[...] = it_ref[...].T
        w_ref[...] = wt_ref[...].T


def peer_route(x2d, g, wq, sk, tok0, t, tm=256):
    d = x2d.shape[1]
    ph = sk.shape[0]
    nsel = ph * PEER_TOPK
    blk0 = tok0 // tm
    return pl.pallas_call(
        _route_kernel,
        grid=(t // tm, ph),
        in_specs=[
            pl.BlockSpec((tm, d), lambda i, p: (blk0 + i, 0)),
            pl.BlockSpec((1, d), lambda i, p: (0, 0)),
            pl.BlockSpec((d, 2 * PEER_HALF), lambda i, p: (0, p)),
            pl.BlockSpec((None, 2, PEER_NKEYS, PEER_HALF), lambda i, p: (p, 0, 0, 0)),
        ],
        out_specs=[
            pl.BlockSpec((tm, d // 2), lambda i, p: (i, 0)),
            pl.BlockSpec((tm, nsel), lambda i, p: (i, 0)),
            pl.BlockSpec((tm, nsel), lambda i, p: (i, 0)),
        ],
        out_shape=[jax.ShapeDtypeStruct((t, d // 2), I32),
                   jax.ShapeDtypeStruct((t, nsel), I32),
                   jax.ShapeDtypeStruct((t, nsel), F32)],
        scratch_shapes=[pltpu.VMEM((tm, d), BF16),
                        pltpu.VMEM((nsel, tm), I32),
                        pltpu.VMEM((nsel, tm), F32)],
        compiler_params=_cparams(("parallel", "arbitrary")),
        name="peer_route",
    )(x2d, g, wq, sk)


def _coef_kernel(w_ref, a_ref, o_ref):
    o_ref[...] = w_ref[...] * jax.nn.gelu(a_ref[...])


def peer_coef(w, act, tm=1024):
    t, n = w.shape
    spec = pl.BlockSpec((tm, n), lambda i: (i, 0))
    return pl.pallas_call(
        _coef_kernel, grid=(t // tm,), in_specs=[spec, spec], out_specs=spec,
        out_shape=jax.ShapeDtypeStruct((t, n), F32),
        compiler_params=_cparams(("parallel",)), name="peer_coef",
    )(w, act)


def _final_kernel(x_ref, y_ref, g_ref, o_ref):
    o_ref[...] = _rms(x_ref[...] + y_ref[...], g_ref[...])


def final_norm(x2d, y, g, tok0, tm=512):
    t, d = y.shape
    blk0 = tok0 // tm
    spec = pl.BlockSpec((tm, d), lambda i: (i, 0))
    return pl.pallas_call(
        _final_kernel, grid=(t // tm,),
        in_specs=[pl.BlockSpec((tm, d), lambda i: (blk0 + i, 0)), spec, pl.BlockSpec((1, d), lambda i: (0, 0))],
        out_specs=spec,
        out_shape=jax.ShapeDtypeStruct((t, d), F32),
        compiler_params=_cparams(("parallel",)), name="final_norm",
    )(x2d, y, g)


SC_CORES = 2
SC_SUBCORES = 16
SC_WORKERS = SC_CORES * SC_SUBCORES
SC_LANES = 16
SC_GROUP = 16


def _sc_mesh():
    return plsc.VectorSubcoreMesh(core_axis_name="c", subcore_axis_name="s")


def _sc_params():
    return pltpu.CompilerParams(needs_layout_passes=False)


def _sc_worker_id():
    return lax.axis_index("s") * SC_CORES + lax.axis_index("c")


SC_RING = 4
SC_ROW_SUB = 8
SC_ROW_LANE = 128


def _sc_ring(n_units, start, wait, compute):
    for u in range(SC_RING - 1):
        start(u, u)

    @pl.loop(0, n_units, step=SC_RING)
    def _(uu):
        for b in range(SC_RING):
            u = uu + b
            nxt = u + (SC_RING - 1)

            @pl.when(nxt < n_units)
            def _():
                start(nxt, (b + SC_RING - 1) % SC_RING)

            wait(u, b)
            compute(u, b)


def _sc_unit_off(u):
    off = u * SC_LANES
    return off if isinstance(off, int) else pl.multiple_of(off, SC_LANES)


def _sc_row_piece(rows, r, c):
    per = SC_ROW_LANE // SC_LANES
    return rows[r, c // per, pl.ds(pl.multiple_of((c % per) * SC_LANES, SC_LANES), SC_LANES)]


def peer_dots_sc(table, idx_flat, h):
    t, d = h.shape
    nsel = PEER_SEL
    tpw = t // SC_WORKERS
    g = SC_GROUP
    groups = tpw // g
    heads = nsel // SC_LANES
    pieces = d // SC_LANES
    units = g * heads
    row_buf = pltpu.VMEM((SC_LANES, SC_ROW_SUB, SC_ROW_LANE), F32)

    @functools.partial(
        pl.kernel, mesh=_sc_mesh(),
        out_type=jax.ShapeDtypeStruct((t * nsel,), F32),
        scratch_types=[
            pltpu.VMEM((g * nsel,), I32),
            pltpu.VMEM((g, d), F32),
            pltpu.VMEM((g * nsel,), F32),
            pltpu.VMEM((SC_LANES * SC_LANES,), F32),
            [row_buf] * SC_RING,
            [pltpu.SemaphoreType.DMA] * SC_RING,
        ],
        compiler_params=_sc_params(),
        name="peer_dots_sc",
    )
    def k(tab_hbm, idx_hbm, h_hbm, out_hbm, idx_v, h_v, out_v, red_v, rows, sems):
        wid = _sc_worker_id()
        lane = lax.iota(I32, SC_LANES)

        def copy(u, slot):
            ids = idx_v.at[pl.ds(_sc_unit_off(u), SC_LANES)]
            return pltpu.make_async_copy(tab_hbm.at[ids], rows[slot], sems[slot])

        def compute(u, slot):
            tt = u // heads

            def body(c, accs):
                hv = h_v[tt, pl.ds(pl.multiple_of(c * SC_LANES, SC_LANES), SC_LANES)]
                return tuple(accs[r] + _sc_row_piece(rows[slot], r, c) * hv for r in range(SC_LANES))

            accs = lax.fori_loop(0, pieces, body,
                                 tuple(jnp.zeros((SC_LANES,), F32) for _ in range(SC_LANES)))
            for r in range(SC_LANES):
                red_v[pl.ds(r * SC_LANES, SC_LANES)] = accs[r]
            cols = [plsc.load_gather(red_v, [lane * SC_LANES + j]) for j in range(SC_LANES)]
            while len(cols) > 1:
                cols = [cols[i] + cols[i + 1] for i in range(0, len(cols), 2)]
            out_v[pl.ds(_sc_unit_off(u), SC_LANES)] = cols[0]

        @pl.loop(0, groups)
        def _(gi):
            base = wid * tpw + gi * g
            pltpu.sync_copy(idx_hbm.at[pl.ds(base * nsel, g * nsel)], idx_v)
            pltpu.sync_copy(h_hbm.at[pl.ds(base, g)], h_v)
            _sc_ring(units, lambda u, s: copy(u, s).start(), lambda u, s: copy(u, s).wait(), compute)
            pltpu.sync_copy(out_v, out_hbm.at[pl.ds(base * nsel, g * nsel)])

    return k(table, idx_flat, h)


def peer_combine_sc(table, idx_flat, coef_flat, t):
    d = table.shape[1] * table.shape[2]
    nsel = PEER_SEL
    tpw = t // SC_WORKERS
    g = SC_GROUP
    groups = tpw // g
    heads = nsel // SC_LANES
    pieces = d // SC_LANES
    units = g * heads
    row_buf = pltpu.VMEM((SC_LANES, SC_ROW_SUB, SC_ROW_LANE), F32)

    @functools.partial(
        pl.kernel, mesh=_sc_mesh(),
        out_type=jax.ShapeDtypeStruct((t, d), F32),
        scratch_types=[
            pltpu.VMEM((g * nsel,), I32),
            pltpu.VMEM((g * nsel,), F32),
            pltpu.VMEM((g, d), F32),
            [row_buf] * SC_RING,
            [pltpu.SemaphoreType.DMA] * SC_RING,
        ],
        compiler_params=_sc_params(),
        name="peer_combine_sc",
    )
    def k(tab_hbm, idx_hbm, coef_hbm, out_hbm, idx_v, coef_v, y_v, rows, sems):
        wid = _sc_worker_id()

        def copy(u, slot):
            ids = idx_v.at[pl.ds(_sc_unit_off(u), SC_LANES)]
            return pltpu.make_async_copy(tab_hbm.at[ids], rows[slot], sems[slot])

        def compute(u, slot):
            tt = u // heads
            first = (u % heads) == 0
            cs = [plsc.load_gather(coef_v, [jnp.full((SC_LANES,), u * SC_LANES + r, I32)])
                  for r in range(SC_LANES)]

            @plsc.parallel_loop(0, pieces, unroll=2)
            def _(c):
                off = pl.multiple_of(c * SC_LANES, SC_LANES)
                terms = [cs[r] * _sc_row_piece(rows[slot], r, c) for r in range(SC_LANES)]
                while len(terms) > 1:
                    terms = [terms[i] + terms[i + 1] for i in range(0, len(terms), 2)]
                prev = y_v[tt, pl.ds(off, SC_LANES)]
                y_v[tt, pl.ds(off, SC_LANES)] = terms[0] + jnp.where(first, 0.0, prev)

        @pl.loop(0, groups)
        def _(gi):
            base = wid * tpw + gi * g
            pltpu.sync_copy(idx_hbm.at[pl.ds(base * nsel, g * nsel)], idx_v)
            pltpu.sync_copy(coef_hbm.at[pl.ds(base * nsel, g * nsel)], coef_v)
            _sc_ring(units, lambda u, s: copy(u, s).start(), lambda u, s: copy(u, s).wait(), compute)
            pltpu.sync_copy(y_v, out_hbm.at[pl.ds(base, g)])

    return k(table, idx_flat, coef_flat)


GELU_C0 = math.sqrt(2.0 / math.pi)
GELU_C1 = 0.044715


def _gelu_tanh(x):
    z = GELU_C0 * (x + GELU_C1 * (x * x * x))
    th = 1.0 - 2.0 / (jnp.exp(2.0 * z) + 1.0)
    return 0.5 * x * (1.0 + th)


def peer_experts_sc(tab_u, tab_v, idx_flat, w_flat, h):
    t, d = h.shape
    nsel = PEER_SEL
    tpw = t // SC_WORKERS
    g = SC_GROUP
    groups = tpw // g
    heads = nsel // SC_LANES
    pieces = d // SC_LANES
    units = g * heads
    row_buf = pltpu.VMEM((SC_LANES, SC_ROW_SUB, SC_ROW_LANE), F32)

    @functools.partial(
        pl.kernel, mesh=_sc_mesh(),
        out_type=jax.ShapeDtypeStruct((t, d), F32),
        scratch_types=[
            pltpu.VMEM((g * nsel,), I32),
            pltpu.VMEM((g * nsel,), F32),
            pltpu.VMEM((g, d), F32),
            pltpu.VMEM((g, d), F32),
            pltpu.VMEM((SC_LANES * SC_LANES,), F32),
            [row_buf] * SC_RING,
            [pltpu.SemaphoreType.DMA] * SC_RING,
        ],
        compiler_params=_sc_params(),
        name="peer_experts_sc",
    )
    def k(u_hbm, v_hbm, idx_hbm, w_hbm, h_hbm, out_hbm, idx_v, coef_v, h_v, y_v, red_v, rows, sems):
        wid = _sc_worker_id()
        lane = lax.iota(I32, SC_LANES)

        def copy(tab_hbm, u, slot):
            ids = idx_v.at[pl.ds(_sc_unit_off(u), SC_LANES)]
            return pltpu.make_async_copy(tab_hbm.at[ids], rows[slot], sems[slot])

        def dots(u, slot):
            tt = u // heads

            def body(c, accs):
                hv = h_v[tt, pl.ds(pl.multiple_of(c * SC_LANES, SC_LANES), SC_LANES)]
                return tuple(accs[r] + _sc_row_piece(rows[slot], r, c) * hv for r in range(SC_LANES))

            accs = lax.fori_loop(0, pieces, body,
                                 tuple(jnp.zeros((SC_LANES,), F32) for _ in range(SC_LANES)))
            for r in range(SC_LANES):
                red_v[pl.ds(r * SC_LANES, SC_LANES)] = accs[r]
            cols = [plsc.load_gather(red_v, [lane * SC_LANES + j]) for j in range(SC_LANES)]
            while len(cols) > 1:
                cols = [cols[i] + cols[i + 1] for i in range(0, len(cols), 2)]
            sl = pl.ds(_sc_unit_off(u), SC_LANES)
            coef_v[sl] = coef_v[sl] * _gelu_tanh(cols[0])

        def combine(u, slot):
            tt = u // heads
            first = (u % heads) == 0
            cs = [plsc.load_gather(coef_v, [jnp.full((SC_LANES,), u * SC_LANES + r, I32)])
                  for r in range(SC_LANES)]

            @plsc.parallel_loop(0, pieces, unroll=2)
            def _(c):
                off = pl.multiple_of(c * SC_LANES, SC_LANES)
                terms = [cs[r] * _sc_row_piece(rows[slot], r, c) for r in range(SC_LANES)]
                while len(terms) > 1:
                    terms = [terms[i] + terms[i + 1] for i in range(0, len(terms), 2)]
                prev = y_v[tt, pl.ds(off, SC_LANES)]
                y_v[tt, pl.ds(off, SC_LANES)] = terms[0] + jnp.where(first, 0.0, prev)

        @pl.loop(0, groups)
        def _(gi):
            base = wid * tpw + gi * g
            pltpu.sync_copy(idx_hbm.at[pl.ds(base * nsel, g * nsel)], idx_v)
            pltpu.sync_copy(w_hbm.at[pl.ds(base * nsel, g * nsel)], coef_v)
            pltpu.sync_copy(h_hbm.at[pl.ds(base, g)], h_v)
            _sc_ring(units, lambda u, s: copy(u_hbm, u, s).start(), lambda u, s: copy(u_hbm, u, s).wait(), dots)
            _sc_ring(units, lambda u, s: copy(v_hbm, u, s).start(), lambda u, s: copy(v_hbm, u, s).wait(), combine)
            pltpu.sync_copy(y_v, out_hbm.at[pl.ds(base, g)])

    return k(tab_u, tab_v, idx_flat, w_flat, h)


SC_PK_RING = 4
SC_PK_SUB = 4
HI_MASK = -65536


def pack_bf16_pairs(a):
    half = a.shape[1] // 2
    bits = lax.bitcast_convert_type(a.astype(BF16), jnp.uint16).astype(jnp.uint32)
    return lax.bitcast_convert_type(bits[:, :half] | (bits[:, half:] << 16), I32)


def _unpack_halves(x32):
    w = plsc.bitcast(x32, I32)
    return plsc.bitcast(w << 16, F32), plsc.bitcast(w & HI_MASK, F32)


def _tree_sum(xs):
    while len(xs) > 1:
        xs = [xs[i] + xs[i + 1] for i in range(0, len(xs), 2)]
    return xs[0]


def peer_experts_pk_sc(tab_uv, idx_flat, w_flat, hp, d):
    t = hp.shape[0]
    nsel = PEER_SEL
    tpw = t // SC_WORKERS
    g = SC_GROUP
    groups = tpw // g
    heads = nsel // SC_LANES
    chunks = d // 32
    units = g * heads
    ring = SC_PK_RING
    row_buf = pltpu.VMEM((SC_LANES, 2 * SC_PK_SUB, SC_ROW_LANE), I32)

    def row_words(rows, r, wc, sub0):
        per = SC_ROW_LANE // SC_LANES
        return plsc.bitcast(
            rows[r, sub0 + wc // per, pl.ds(pl.multiple_of((wc % per) * SC_LANES, SC_LANES), SC_LANES)], BF16)

    def ring_loop(n_units, start, wait, compute):
        for u in range(ring - 1):
            start(u, u)

        @pl.loop(0, n_units, step=ring)
        def _(uu):
            for b in range(ring):
                u = uu + b
                nxt = u + (ring - 1)

                @pl.when(nxt < n_units)
                def _():
                    start(nxt, (b + ring - 1) % ring)

                wait(u, b)
                compute(u, b)

    @functools.partial(
        pl.kernel, mesh=_sc_mesh(),
        out_type=jax.ShapeDtypeStruct((t, d), F32),
        scratch_types=[
            pltpu.VMEM((g * nsel,), I32),
            pltpu.VMEM((g * nsel,), F32),
            pltpu.VMEM((g, d // 2), I32),
            pltpu.VMEM((g, d), F32),
            pltpu.VMEM((SC_LANES * SC_LANES,), F32),
            [row_buf] * ring,
            [pltpu.SemaphoreType.DMA] * ring,
        ],
        compiler_params=_sc_params(),
        name="peer_experts_pk_sc",
    )
    def k(tab_hbm, idx_hbm, w_hbm, h_hbm, out_hbm, idx_v, coef_v, h_v, y_v, red_v, rows, sems):
        wid = _sc_worker_id()
        lane = lax.iota(I32, SC_LANES)

        def copy(u, slot):
            ids = idx_v.at[pl.ds(_sc_unit_off(u), SC_LANES)]
            return pltpu.make_async_copy(tab_hbm.at[ids], rows[slot], sems[slot])

        def dots(u, slot):
            tt = u // heads

            def body(cp, accs):
                out = []
                hv = [plsc.bitcast(h_v[tt, pl.ds(pl.multiple_of((2 * cp + i) * SC_LANES, SC_LANES), SC_LANES)], BF16)
                      for i in range(2)]
                for r in range(SC_LANES):
                    pr = (row_words(rows[slot], r, 2 * cp, 0) * hv[0]
                          + row_words(rows[slot], r, 2 * cp + 1, 0) * hv[1])
                    lo, hi = _unpack_halves(pr)
                    out.append(accs[r] + lo + hi)
                return tuple(out)

            accs = lax.fori_loop(0, chunks // 2, body,
                                 tuple(jnp.zeros((SC_LANES,), F32) for _ in range(SC_LANES)))
            for r in range(SC_LANES):
                red_v[pl.ds(r * SC_LANES, SC_LANES)] = accs[r]
            act = _tree_sum([plsc.load_gather(red_v, [lane * SC_LANES + j]) for j in range(SC_LANES)])
            sl = pl.ds(_sc_unit_off(u), SC_LANES)
            coef_v[sl] = coef_v[sl] * _gelu_tanh(act)

        def combine(u, slot):
            tt = u // heads
            first = (u % heads) == 0
            cb = []
            for r in range(SC_LANES):
                c = plsc.load_gather(coef_v, [jnp.full((SC_LANES,), u * SC_LANES + r, I32)])
                cb.append(plsc.pack(c, c, format=plsc.PackFormat.INTERLEAVED))

            @plsc.parallel_loop(0, chunks, unroll=2)
            def _(wc):
                lo, hi = _unpack_halves(
                    _tree_sum([cb[r] * row_words(rows[slot], r, wc, SC_PK_SUB) for r in range(SC_LANES)]))
                for half, val in ((0, lo), (1, hi)):
                    sl = pl.ds(pl.multiple_of(half * (d // 2) + wc * SC_LANES, SC_LANES), SC_LANES)
                    y_v[tt, sl] = val + jnp.where(first, 0.0, y_v[tt, sl])

        def unit(u, slot):
            dots(u, slot)
            combine(u, slot)

        @pl.loop(0, groups)
        def _(gi):
            base = wid * tpw + gi * g
            pltpu.sync_copy(idx_hbm.at[pl.ds(base * nsel, g * nsel)], idx_v)
            pltpu.sync_copy(w_hbm.at[pl.ds(base * nsel, g * nsel)], coef_v)
            pltpu.sync_copy(h_hbm.at[pl.ds(base, g)], h_v)
            ring_loop(units, lambda u, s: copy(u, s).start(), lambda u, s: copy(u, s).wait(), unit)
            pltpu.sync_copy(y_v, out_hbm.at[pl.ds(base, g)])

    return k(tab_uv, idx_flat, w_flat, hp)


def kernel(x, mem, rel_bias, ln_mix, w_in, hg_lower, hg_norm, w_up_a, w_up_b, w_out, ln_cross, ln_mem, wq_x, wk_x, wv_x, wo_x, ln_ffn, peer_query, peer_subkeys, peer_u, peer_v, ln_final):
    b, s, d = x.shape
    depth = w_in.shape[0]
    assert depth == 1, "the residual after PEER is fused into the final norm"
    assert s % MB_BLOCK == 0 and s % HG_CHUNK == 0 and s % (PEER_SLICES * SC_WORKERS * SC_GROUP) == 0
    nb = s // MB_BLOCK
    row = lambda a: a.reshape(1, -1).astype(F32)
    lb_all = jnp.cumsum(jax.nn.softmax(hg_lower.astype(F32), axis=0), axis=0)
    bias = moba_bias_tiles(rel_bias)
    n_hg = 4 * HG_WIDTH
    n_qk = 2 * MB_WIDTH
    n_mb = 3 * MB_WIDTH
    l = 0
    w = w_in[l].astype(BF16)
    w_hg, w_qk, w_vt, w_g = w[:, :n_hg], w[:, n_hg:n_hg + n_qk], w[:, n_hg + n_qk:n_hg + n_mb].T, w[:, n_hg + n_mb:]
    wa, wb, wo = w_up_a[l].astype(BF16), w_up_b[l].astype(BF16), w_out[l].astype(BF16)
    wqx, wox = wq_x[l].astype(BF16), wo_x[l].astype(BF16)
    wpq, sk = peer_query[l].astype(BF16), peer_subkeys[l].astype(F32)
    tab3 = lambda a: pack_bf16_pairs(a.astype(F32)).reshape(a.shape[0], SC_PK_SUB, SC_ROW_LANE)
    tab_uv = jnp.concatenate([tab3(peer_u[l]), tab3(peer_v[l])], axis=1)
    kx, vx = mem_kv(mem, row(ln_mem[l]), wk_x[l].astype(BF16), wv_x[l].astype(BF16))

    outs = []
    for bi in range(b):
        x2d = x[bi]
        p0, pqk, vt, pg = in_proj(x2d, row(ln_mix[l]), w_hg, w_qk, w_vt, w_g)
        ya = hgrn2(p0, row(lb_all[l]), row(hg_norm[l]), 1, s)
        km = moba_kmean(pqk, 1, s).reshape(1, nb, MB_WIDTH)
        ts = s // PEER_SLICES
        for tok0 in range(0, s, ts):
            yb = moba_attention(pqk, vt, km, bias, 1, s, tok0 // MB_BLOCK, ts // MB_BLOCK)
            xs = mix_out(x2d, ya, yb, pg, wa, wb, wo, tok0)
            xs = cross_attn(xs, row(ln_cross[l]), wqx, kx[bi:bi + 1], vx[bi:bi + 1], wox, ts)
            hp, eidx, wts = peer_route(xs, row(ln_ffn[l]), wpq, sk, 0, ts)
            y = peer_experts_pk_sc(tab_uv, eidx.reshape(ts * PEER_SEL), wts.reshape(ts * PEER_SEL), hp, d)
            outs.append(final_norm(xs, y, row(ln_final), 0))
    return jnp.concatenate(outs, axis=0).reshape(b, s, d)
```

```python
import functools
import math

import jax
import jax.numpy as jnp
import numpy as np
from jax import lax
from jax.experimental import pallas as pl
from jax.experimental.pallas import tpu as pltpu
from jax.experimental.pallas import tpu_sc as plsc

F32 = jnp.float32
BF16 = jnp.bfloat16
I32 = jnp.int32
EPS = 1e-6
NEG_INF = float("-inf")

HG_HEADS = 4
HG_D = 128
HG_WIDTH = HG_HEADS * HG_D
HG_CHUNK = 64
HG_SUB = 16
MB_HEADS = 8
MB_DH = 64
MB_WIDTH = MB_HEADS * MB_DH
MB_BLOCK = 256
MB_TOPK = 3
MB_BIAS_TILES = 8
REL_BUCKETS = 32
REL_MAX_DIST = 2048
X_HEADS = 4
PEER_HEADS = 8
PEER_NKEYS = 128
PEER_TOPK = 16
PEER_HALF = 128
PEER_SEL = PEER_HEADS * PEER_TOPK
PEER_SLICES = 4

VMEM_LIMIT = 56 * 1024 * 1024


def _cparams(sem):
    return pltpu.CompilerParams(dimension_semantics=sem, vmem_limit_bytes=VMEM_LIMIT)


def _rms(x, g):
    ms = jnp.mean(x * x, axis=-1, keepdims=True)
    return x * lax.rsqrt(ms + EPS) * g


def _in_proj_kernel(x_ref, g_ref, w0_ref, w1_ref, wvt_ref, w2_ref, o0_ref, o1_ref, ovt_ref, o2_ref):
    h = _rms(x_ref[...], g_ref[...]).astype(BF16)
    o0_ref[...] = jnp.dot(h, w0_ref[...], preferred_element_type=F32)
    o1_ref[...] = jnp.dot(h, w1_ref[...], preferred_element_type=F32).astype(BF16)
    vt = lax.dot_general(wvt_ref[...], h, (((1,), (1,)), ((), ())), preferred_element_type=F32).astype(BF16)
    for g in range(MB_WIDTH // MB_LG):
        ovt_ref[0, g * MB_VROWS:g * MB_VROWS + MB_LG, :] = vt[g * MB_LG:(g + 1) * MB_LG]
        ovt_ref[0, g * MB_VROWS + MB_LG:(g + 1) * MB_VROWS, :] = jnp.ones((MB_ONES, vt.shape[1]), BF16)
    o2_ref[...] = jnp.dot(h, w2_ref[...], preferred_element_type=F32).astype(BF16)


def in_proj(x2d, g, w0, w1, wvt, w2):
    t, d = x2d.shape
    tm = MB_BLOCK
    assert wvt.shape[0] == MB_WIDTH
    n0, n1, nv, n2 = w0.shape[1], w1.shape[1], MB_VT_ROWS, w2.shape[1]
    full = lambda a: pl.BlockSpec(a.shape, lambda i: (0, 0))
    return pl.pallas_call(
        _in_proj_kernel,
        grid=(t // tm,),
        in_specs=[pl.BlockSpec((tm, d), lambda i: (i, 0)), full(g), full(w0), full(w1), full(wvt), full(w2)],
        out_specs=[pl.BlockSpec((tm, n0), lambda i: (i, 0)),
                   pl.BlockSpec((tm, n1), lambda i: (i, 0)),
                   pl.BlockSpec((1, nv, tm), lambda i: (i, 0, 0)),
                   pl.BlockSpec((tm, n2), lambda i: (i, 0))],
        out_shape=[jax.ShapeDtypeStruct((t, n0), F32),
                   jax.ShapeDtypeStruct((t, n1), BF16),
                   jax.ShapeDtypeStruct((t // tm, nv, tm), BF16),
                   jax.ShapeDtypeStruct((t, n2), BF16)],
        compiler_params=_cparams(("parallel",)),
        name="in_proj",
    )(x2d, g, w0, w1, wvt, w2)


def _hgrn_kernel(q_ref, f_ref, i_ref, g_ref, lb_ref, gain_ref, o_ref, st_ref):
    c = pl.program_id(1)

    @pl.when(c == 0)
    def _():
        st_ref[...] = jnp.zeros_like(st_ref)

    C, S = HG_CHUNK, HG_SUB
    row = lax.broadcasted_iota(I32, (C, C), 0)
    col = lax.broadcasted_iota(I32, (C, C), 1)
    tril = (row >= col).astype(F32)
    t_iota = lax.broadcasted_iota(I32, (S, 1), 0)

    for h in range(HG_HEADS):
        sl = slice(h * HG_D, (h + 1) * HG_D)
        q = q_ref[:, sl]
        v = i_ref[:, sl]
        lb = lb_ref[:, sl]
        f = lb + (1.0 - lb) * jax.nn.sigmoid(f_ref[:, sl])
        lf = jnp.log(f)
        k = 1.0 - f
        b = jnp.dot(tril, lf, precision=lax.Precision.HIGHEST, preferred_element_type=F32)
        st = st_ref[h]
        vb = v.astype(BF16)
        qd = (q * jnp.exp(b)).astype(BF16)
        o_inter = lax.dot_general(qd, st.astype(BF16), (((1,), (1,)), ((), ())),
                                  preferred_element_type=F32)
        outs = []
        for i in range(C // S):
            r0 = i * S
            qi = q[r0:r0 + S]
            ki = k[r0:r0 + S]
            bi = b[r0:r0 + S]
            vi = v[r0:r0 + S]
            oi = o_inter[r0:r0 + S]
            if i > 0:
                bs = b[r0 - 1:r0]
                qh = (qi * jnp.exp(bi - bs)).astype(BF16)
                kh = (k[:r0] * jnp.exp(bs - b[:r0])).astype(BF16)
                a = lax.dot_general(qh, kh, (((1,), (1,)), ((), ())), preferred_element_type=F32)
                oi = oi + jnp.dot(a.astype(BF16), vb[:r0], preferred_element_type=F32)
            for s in range(S):
                dec = jnp.exp(jnp.minimum(bi - bi[s:s + 1], 0.0))
                p = qi * ki[s:s + 1] * dec
                a_s = jnp.sum(p, axis=-1, keepdims=True)
                a_s = jnp.where(t_iota >= s, a_s, 0.0)
                oi = oi + a_s * vi[s:s + 1]
            outs.append(oi)
        o = jnp.concatenate(outs, axis=0)
        b_end = b[C - 1:C]
        kd = (k * jnp.exp(b_end - b)).astype(BF16)
        upd = lax.dot_general(vb, kd, (((0,), (0,)), ((), ())), preferred_element_type=F32)
        st_ref[h] = st * jnp.exp(b_end) + upd
        o = o * lax.rsqrt(jnp.mean(o * o, axis=-1, keepdims=True) + EPS)
        g = g_ref[:, sl]
        o_ref[:, sl] = (o * gain_ref[:, sl] * (g * jax.nn.sigmoid(g))).astype(o_ref.dtype)


def hgrn2(p0, lb, gain, batch, seq):
    t = p0.shape[0]
    nc = seq // HG_CHUNK
    w = HG_WIDTH

    def col(j):
        return pl.BlockSpec((HG_CHUNK, w), lambda b, c, j=j: (b * nc + c, j))

    return pl.pallas_call(
        _hgrn_kernel,
        grid=(batch, nc),
        in_specs=[col(0), col(1), col(2), col(3),
                  pl.BlockSpec((1, w), lambda b, c: (0, 0)),
                  pl.BlockSpec((1, w), lambda b, c: (0, 0))],
        out_specs=pl.BlockSpec((HG_CHUNK, w), lambda b, c: (b * nc + c, 0)),
        out_shape=jax.ShapeDtypeStruct((t, w), BF16),
        scratch_shapes=[pltpu.VMEM((HG_HEADS, HG_D, HG_D), F32)],
        compiler_params=_cparams(("parallel", "arbitrary")),
        name="hgrn2",
    )(p0, p0, p0, p0, lb, gain)


def _kmean_kernel(k_ref, o_ref):
    o_ref[0] = jnp.mean(k_ref[...].astype(F32), axis=0, keepdims=True)


def moba_kmean(p1, batch, seq):
    nbt = p1.shape[0] // MB_BLOCK
    return pl.pallas_call(
        _kmean_kernel,
        grid=(nbt,),
        in_specs=[pl.BlockSpec((MB_BLOCK, MB_WIDTH), lambda i: (i, 1))],
        out_specs=pl.BlockSpec((1, 1, MB_WIDTH), lambda i: (i, 0, 0)),
        out_shape=jax.ShapeDtypeStruct((nbt, 1, MB_WIDTH), F32),
        compiler_params=_cparams(("parallel",)),
        name="moba_kmean",
    )(p1)


MB_PAIR = 4
MB_PW = MB_PAIR * MB_DH
MB_LG = 128
MB_ONES = 16
MB_VROWS = MB_LG + MB_ONES
MB_VT_ROWS = (MB_WIDTH // MB_LG) * MB_VROWS


def _moba_kernel(q_ref, k_ref, vt_ref, km_ref, bias_ref, o_ref, *scratch, qb0):
    m_ref, l_ref, al_ref, acc_ref, msk_ref, s_ref, p_ref = (
        scratch[i * MB_PAIR:(i + 1) * MB_PAIR] for i in range(7))
    qi = pl.program_id(2) + qb0
    nb = km_ref.shape[0]
    blk = MB_BLOCK
    heads = range(MB_PAIR)
    grp = lambda hh: slice((hh // 2) * MB_LG, (hh // 2 + 1) * MB_LG)
    q = q_ref[...]
    lane = lax.broadcasted_iota(I32, (blk, MB_LG), 1)
    in_head = [(lane < MB_DH) if hh % 2 == 0 else (lane >= MB_DH) for hh in heads]
    qs = q * jnp.asarray(MB_DH ** -0.5, BF16)
    nt = (((1,), (1,)), ((), ()))
    qf = q.astype(F32)
    qht = [jnp.where(in_head[hh], qs[:, grp(hh)].astype(F32), 0.0).T.astype(BF16) for hh in heads]

    n_io = lax.broadcasted_iota(I32, (nb, blk), 0)
    for hh in heads:
        gate = lax.dot_general(km_ref[:, grp(hh)], jnp.where(in_head[hh], qf[:, grp(hh)], 0.0), nt,
                               precision=lax.Precision.HIGHEST, preferred_element_type=F32)
        gate = jnp.where(n_io < qi, gate, NEG_INF)
        chosen = n_io < 0
        for _ in range(MB_TOPK):
            mx = jnp.max(gate, axis=0, keepdims=True)
            ix = jnp.min(jnp.where(gate == mx, n_io, nb), axis=0, keepdims=True)
            hit = n_io == ix
            chosen = chosen | (hit & (mx > NEG_INF))
            gate = jnp.where(hit, NEG_INF, gate)
        msk_ref[hh][...] = jnp.where(chosen, 0.0, NEG_INF)

    own_rows = lambda r, hh: r[(hh % 2) * MB_DH:(hh % 2 + 1) * MB_DH]

    vgrp = lambda hh: slice((hh // 2) * MB_VROWS, (hh // 2 + 1) * MB_VROWS)

    def pv_stage(blk_idx):
        vtb = vt_ref[blk_idx]
        r = [jnp.dot(vtb[vgrp(hh)], p_ref[hh][...], preferred_element_type=F32) for hh in heads]
        al = [al_ref[hh][...] for hh in heads]
        a_new = [al[hh] * acc_ref[hh][...] + own_rows(r[hh], hh) for hh in heads]
        l_new = [al[hh] * l_ref[hh][...] + r[hh][MB_LG:MB_LG + 1] for hh in heads]
        return a_new, l_new

    def store_pv(a_new, l_new):
        for hh in heads:
            acc_ref[hh][...] = a_new[hh]
            l_ref[hh][...] = l_new[hh]

    def softmax_stage():
        s = [s_ref[hh][...] for hh in heads]
        m_old = [m_ref[hh][...] for hh in heads]
        m_new = [jnp.maximum(m_old[hh], jnp.max(s[hh], axis=0, keepdims=True)) for hh in heads]
        alpha = [jnp.exp(m_old[hh] - m_new[hh]) for hh in heads]
        p = [jnp.exp((s[hh] - m_new[hh]).astype(BF16)) for hh in heads]
        return p, alpha, m_new

    def store_softmax(p, alpha, m_new):
        for hh in heads:
            p_ref[hh][...] = p[hh]
            al_ref[hh][...] = alpha[hh]
            m_ref[hh][...] = m_new[hh]

    k_own = k_ref[pl.ds(pl.multiple_of(qi * blk, blk), blk), :]
    key_io = lax.broadcasted_iota(I32, (blk, blk), 0)
    qry_io = lax.broadcasted_iota(I32, (blk, blk), 1)
    for hh in heads:
        s = jnp.dot(k_own[:, grp(hh)], qht[hh], preferred_element_type=F32) + bias_ref[hh, 0]
        s_ref[hh][...] = jnp.where(key_io <= qry_io, s, NEG_INF)
        m_ref[hh][...] = jnp.full((1, blk), NEG_INF, F32)
        l_ref[hh][...] = jnp.zeros((1, blk), F32)
        al_ref[hh][...] = jnp.ones((1, blk), F32)
        acc_ref[hh][...] = jnp.zeros((MB_DH, blk), F32)
        p_ref[hh][...] = jnp.zeros((blk, blk), BF16)

    def step(i, carry):
        pv = pv_stage(jnp.where(i <= 1, qi, i - 2))
        sm = softmax_stage()
        kn = k_ref[pl.ds(pl.multiple_of(i * blk, blk), blk), :]
        d = jnp.minimum(qi - i, MB_BIAS_TILES - 1)
        s_next = [jnp.dot(kn[:, grp(hh)], qht[hh], preferred_element_type=F32)
                  + bias_ref[hh, d] + msk_ref[hh][pl.ds(i, 1), :] for hh in heads]
        store_pv(*pv)
        for hh in heads:
            s_ref[hh][...] = s_next[hh]
        store_softmax(*sm)
        return carry

    lax.fori_loop(0, qi, step, 0)
    pv = pv_stage(jnp.where(qi <= 1, qi, qi - 2))
    sm = softmax_stage()
    store_pv(*pv)
    store_softmax(*sm)
    a_fin, l_fin = pv_stage(jnp.where(qi == 0, qi, qi - 1))
    out_t = jnp.concatenate([a_fin[hh] / l_fin[hh] for hh in heads], axis=0)
    o_ref[...] = out_t.T.astype(o_ref.dtype)


def moba_attention(pqk, vt, km, bias, batch, seq, qb0=0, nqb=None):
    nb = seq // MB_BLOCK
    nqb = nb if nqb is None else nqb
    t = batch * nqb * MB_BLOCK
    groups = MB_WIDTH // MB_PW
    return pl.pallas_call(
        functools.partial(_moba_kernel, qb0=qb0),
        grid=(batch, groups, nqb),
        in_specs=[
            pl.BlockSpec((MB_BLOCK, MB_PW), lambda b, j, i: (b * nb + qb0 + i, j)),
            pl.BlockSpec((seq, MB_PW), lambda b, j, i: (b, groups + j)),
            pl.BlockSpec((nb, (MB_PW // MB_LG) * MB_VROWS, MB_BLOCK), lambda b, j, i: (b, j, 0)),
            pl.BlockSpec((None, nb, MB_PW), lambda b, j, i: (b, 0, j)),
            pl.BlockSpec((MB_PAIR, MB_BIAS_TILES, MB_BLOCK, MB_BLOCK), lambda b, j, i: (j, 0, 0, 0)),
        ],
        out_specs=pl.BlockSpec((MB_BLOCK, MB_PW), lambda b, j, i: (b * nqb + i, j)),
        out_shape=jax.ShapeDtypeStruct((t, MB_WIDTH), BF16),
        scratch_shapes=(
            [pltpu.VMEM((1, MB_BLOCK), F32)] * (3 * MB_PAIR)
            + [pltpu.VMEM((MB_DH, MB_BLOCK), F32)] * MB_PAIR
            + [pltpu.VMEM((nb, MB_BLOCK), F32)] * MB_PAIR
            + [pltpu.VMEM((MB_BLOCK, MB_BLOCK), F32)] * MB_PAIR
            + [pltpu.VMEM((MB_BLOCK, MB_BLOCK), BF16)] * MB_PAIR
        ),
        compiler_params=_cparams(("parallel", "parallel", "arbitrary")),
        name="moba_attn",
    )(pqk, pqk, vt, km, bias)


def _t5_bucket(dist):
    max_exact = REL_BUCKETS // 2
    scaled = jnp.log(jnp.maximum(dist, 1).astype(F32) / max_exact) / math.log(REL_MAX_DIST / max_exact)
    large = jnp.minimum(max_exact + (scaled * (REL_BUCKETS - max_exact)).astype(I32), REL_BUCKETS - 1)
    return jnp.where(dist < max_exact, dist, large)


def moba_bias_tiles(rel_bias):
    blk = MB_BLOCK
    span = 2 * blk - 1
    x = jnp.arange(span) - (blk - 1)
    dist = jnp.maximum(jnp.arange(MB_BIAS_TILES)[:, None] * blk + x[None, :], 0)
    w = rel_bias.astype(F32).T[:, _t5_bucket(dist)]
    h = w.shape[0]
    wp = jnp.pad(w, ((0, 0), (0, 0), (0, 1)))
    a = jnp.broadcast_to(wp[:, :, None, :], (h, MB_BIAS_TILES, blk, span + 1))
    a = a.reshape(h, MB_BIAS_TILES, blk * (span + 1))[:, :, :blk * span]
    return a.reshape(h, MB_BIAS_TILES, blk, span)[:, :, :, blk - 1:]


def _mix_kernel(x_ref, ya_ref, yb_ref, ga_ref, gb_ref, wa_ref, wb_ref, wo_ref, o_ref):
    za = jnp.dot(ya_ref[...], wa_ref[...], preferred_element_type=F32)
    zb = jnp.dot(yb_ref[...], wb_ref[...], preferred_element_type=F32)
    z = jax.nn.sigmoid(ga_ref[...].astype(F32)) * za + jax.nn.sigmoid(gb_ref[...].astype(F32)) * zb
    o_ref[...] = x_ref[...] + jnp.dot(z.astype(BF16), wo_ref[...], preferred_element_type=F32)


def mix_out(x2d, ya, yb, pg, wa, wb, wo, tok0=0, tm=256):
    t = yb.shape[0]
    d = x2d.shape[1]
    w = ya.shape[1]
    b0 = tok0 // tm
    return pl.pallas_call(
        _mix_kernel,
        grid=(t // tm,),
        in_specs=[
            pl.BlockSpec((tm, d), lambda i: (b0 + i, 0)),
            pl.BlockSpec((tm, w), lambda i: (b0 + i, 0)),
            pl.BlockSpec((tm, w), lambda i: (i, 0)),
            pl.BlockSpec((tm, d), lambda i: (b0 + i, 0)),
            pl.BlockSpec((tm, d), lambda i: (b0 + i, 1)),
            pl.BlockSpec((w, d), lambda i: (0, 0)),
            pl.BlockSpec((w, d), lambda i: (0, 0)),
            pl.BlockSpec((d, d), lambda i: (0, 0)),
        ],
        out_specs=pl.BlockSpec((tm, d), lambda i: (i, 0)),
        out_shape=jax.ShapeDtypeStruct((t, d), F32),
        compiler_params=_cparams(("parallel",)),
        name="mix_out",
    )(x2d, ya, yb, pg, pg, wa, wb, wo)


def _mem_kv_kernel(m_ref, g_ref, wk_ref, wv_ref, k_ref, v_ref):
    mn = _rms(m_ref[...], g_ref[...]).astype(BF16)
    k_ref[...] = jnp.dot(mn, wk_ref[...], preferred_element_type=F32).astype(BF16)
    v_ref[...] = jnp.dot(mn, wv_ref[...], preferred_element_type=F32).astype(BF16)


def mem_kv(mem, g, wk, wv):
    b, m, d = mem.shape
    spec = pl.BlockSpec((None, m, d), lambda i: (i, 0, 0))
    wspec = pl.BlockSpec((d, d), lambda i: (0, 0))
    return pl.pallas_call(
        _mem_kv_kernel,
        grid=(b,),
        in_specs=[spec, pl.BlockSpec((1, d), lambda i: (0, 0)), wspec, wspec],
        out_specs=[spec, spec],
        out_shape=[jax.ShapeDtypeStruct((b, m, d), BF16)] * 2,
        compiler_params=_cparams(("parallel",)),
        name="mem_kv",
    )(mem, g, wk, wv)


def _cross_kernel(x_ref, g_ref, wq_ref, k_ref, v_ref, wo_ref, o_ref):
    x = x_ref[...]
    d = x.shape[1]
    dh = d // X_HEADS
    h = _rms(x, g_ref[...]).astype(BF16)
    q = (jnp.dot(h, wq_ref[...], preferred_element_type=F32) * (dh ** -0.5)).astype(BF16)
    outs = []
    for hh in range(X_HEADS):
        sl = slice(hh * dh, (hh + 1) * dh)
        s = lax.dot_general(q[:, sl], k_ref[:, sl], (((1,), (1,)), ((), ())),
                            preferred_element_type=F32)
        p = jnp.exp(s - jnp.max(s, axis=1, keepdims=True))
        l = jnp.sum(p, axis=1, keepdims=True)
        o = jnp.dot(p.astype(BF16), v_ref[:, sl], preferred_element_type=F32) / l
        outs.append(o.astype(BF16))
    o = jnp.concatenate(outs, axis=1)
    o_ref[...] = x + jnp.dot(o, wo_ref[...], preferred_element_type=F32)


def cross_attn(x2d, g, wq, kx, vx, wo, seq, tm=256):
    t, d = x2d.shape
    m = kx.shape[1]
    per_b = seq // tm
    kv = pl.BlockSpec((None, m, d), lambda i: (i // per_b, 0, 0))
    wspec = pl.BlockSpec((d, d), lambda i: (0, 0))
    return pl.pallas_call(
        _cross_kernel,
        grid=(t // tm,),
        in_specs=[pl.BlockSpec((tm, d), lambda i: (i, 0)), pl.BlockSpec((1, d), lambda i: (0, 0)),
                  wspec, kv, kv, wspec],
        out_specs=pl.BlockSpec((tm, d), lambda i: (i, 0)),
        out_shape=jax.ShapeDtypeStruct((t, d), F32),
        compiler_params=_cparams(("parallel",)),
        name="cross_attn",
    )(x2d, g, wq, kx, vx, wo)


def _topk_rows(sc, k):
    n = sc.shape[0]
    io = lax.broadcasted_iota(I32, sc.shape, 0)
    vals, ids = [], []
    for _ in range(k):
        m = jnp.max(sc, axis=0, keepdims=True)
        ix = jnp.min(jnp.where(sc == m, io, n), axis=0, keepdims=True)
        vals.append(m)
        ids.append(ix)
        sc = jnp.where(io == ix, NEG_INF, sc)
    return jnp.concatenate(vals, axis=0), jnp.concatenate(ids, axis=0)


def _pack_bf16_halves(h):
    bits = lax.bitcast_convert_type(h, I32)
    r = bits + 0x7FFF + (lax.shift_right_logical(bits, 16) & 1)
    half = h.shape[1] // 2
    return lax.shift_right_logical(r[:, :half], 16) | (r[:, half:] & HI_MASK)


def _route_kernel(x_ref, g_ref, wq_ref, sk_ref, hp_ref, idx_ref, w_ref, hb_ref, it_ref, wt_ref):
    p = pl.program_id(1)

    @pl.when(p == 0)
    def _():
        h = _rms(x_ref[...], g_ref[...])
        hp_ref[...] = _pack_bf16_halves(h)
        hb_ref[...] = h.astype(BF16)

    qh = jnp.dot(hb_ref[...], wq_ref[...], preferred_element_type=F32)
    tops = []
    for c in range(2):
        seg = qh[:, c * PEER_HALF:(c + 1) * PEER_HALF]
        sc = lax.dot_general(sk_ref[c], seg, (((1,), (1,)), ((), ())),
                             precision=lax.Precision.HIGHEST, preferred_element_type=F32)
        tops.append(_topk_rows(sc, PEER_TOPK))
    (s0, i0), (s1, i1) = tops
    k = PEER_TOPK
    sub = 8
    tm = s0.shape[1]
    r8 = lax.broadcasted_iota(I32, (sub, tm), 0)
    r16 = lax.broadcasted_iota(I32, (k, tm), 0)
    cand_b = [s0[0:1] + s1, s0[1:2] + s1[:sub]]
    cidx_b = [i0[0:1] * PEER_NKEYS + i1, i0[1:2] * PEER_NKEYS + i1[:sub]]
    pos_b = [r16, k + r8]
    for a in range(2, sub):
        keep = r8 < (k // (a + 1))
        cand_b.append(jnp.where(keep, s0[a:a + 1] + s1[:sub], NEG_INF))
        cidx_b.append(i0[a:a + 1] * PEER_NKEYS + i1[:sub])
        pos_b.append(a * k + r8)
    cand_b.append(s0[sub:] + s1[0:1])
    cidx_b.append(i0[sub:] * PEER_NKEYS + i1[0:1])
    pos_b.append((sub + r8) * k)
    cand = jnp.concatenate(cand_b, axis=0)
    cidx = jnp.concatenate(cidx_b, axis=0)
    pos = jnp.concatenate(pos_b, axis=0)
    vals, ids = [], []
    for _ in range(k):
        m = jnp.max(cand, axis=0, keepdims=True)
        px = jnp.min(jnp.where(cand == m, pos, k * k), axis=0, keepdims=True)
        hit = pos == px
        vals.append(m)
        ids.append(jnp.sum(jnp.where(hit, cidx, 0), axis=0, keepdims=True))
        cand = jnp.where(hit, NEG_INF, cand)
    sf = jnp.concatenate(vals, axis=0)
    e = jnp.exp(sf - sf[0:1])
    rows = pl.ds(pl.multiple_of(p * PEER_TOPK, PEER_TOPK), PEER_TOPK)
    wt_ref[rows, :] = e / jnp.sum(e, axis=0, keepdims=True)
    it_ref[rows, :] = jnp.concatenate(ids, axis=0)

    @pl.when(p == pl.num_programs(1) - 1)
    def _():
        idx_ref[...] = it_ref[...].T
        w_ref[...] = wt_ref[...].T


def peer_route(x2d, g, wq, sk, tok0, t, tm=512):
    d = x2d.shape[1]
    ph = sk.shape[0]
    nsel = ph * PEER_TOPK
    blk0 = tok0 // tm
    return pl.pallas_call(
        _route_kernel,
        grid=(t // tm, ph),
        in_specs=[
            pl.BlockSpec((tm, d), lambda i, p: (blk0 + i, 0)),
            pl.BlockSpec((1, d), lambda i, p: (0, 0)),
            pl.BlockSpec((d, 2 * PEER_HALF), lambda i, p: (0, p)),
            pl.BlockSpec((None, 2, PEER_NKEYS, PEER_HALF), lambda i, p: (p, 0, 0, 0)),
        ],
        out_specs=[
            pl.BlockSpec((tm, d // 2), lambda i, p: (i, 0)),
            pl.BlockSpec((tm, nsel), lambda i, p: (i, 0)),
            pl.BlockSpec((tm, nsel), lambda i, p: (i, 0)),
        ],
        out_shape=[jax.ShapeDtypeStruct((t, d // 2), I32),
                   jax.ShapeDtypeStruct((t, nsel), I32),
                   jax.ShapeDtypeStruct((t, nsel), F32)],
        scratch_shapes=[pltpu.VMEM((tm, d), BF16),
                        pltpu.VMEM((nsel, tm), I32),
                        pltpu.VMEM((nsel, tm), F32)],
        compiler_params=_cparams(("parallel", "arbitrary")),
        name="peer_route",
    )(x2d, g, wq, sk)


def _coef_kernel(w_ref, a_ref, o_ref):
    o_ref[...] = w_ref[...] * jax.nn.gelu(a_ref[...])


def peer_coef(w, act, tm=1024):
    t, n = w.shape
    spec = pl.BlockSpec((tm, n), lambda i: (i, 0))
    return pl.pallas_call(
        _coef_kernel, grid=(t // tm,), in_specs=[spec, spec], out_specs=spec,
        out_shape=jax.ShapeDtypeStruct((t, n), F32),
        compiler_params=_cparams(("parallel",)), name="peer_coef",
    )(w, act)


def _final_kernel(x_ref, y_ref, g_ref, o_ref):
    o_ref[...] = _rms(x_ref[...] + y_ref[...], g_ref[...])


def final_norm(x2d, y, g, tok0, tm=512):
    t, d = y.shape
    blk0 = tok0 // tm
    spec = pl.BlockSpec((tm, d), lambda i: (i, 0))
    return pl.pallas_call(
        _final_kernel, grid=(t // tm,),
        in_specs=[pl.BlockSpec((tm, d), lambda i: (blk0 + i, 0)), spec, pl.BlockSpec((1, d), lambda i: (0, 0))],
        out_specs=spec,
        out_shape=jax.ShapeDtypeStruct((t, d), F32),
        compiler_params=_cparams(("parallel",)), name="final_norm",
    )(x2d, y, g)


SC_CORES = 2
SC_SUBCORES = 16
SC_WORKERS = SC_CORES * SC_SUBCORES
SC_LANES = 16
SC_GROUP = 16


def _sc_mesh():
    return plsc.VectorSubcoreMesh(core_axis_name="c", subcore_axis_name="s")


def _sc_params():
    return pltpu.CompilerParams(needs_layout_passes=False)


def _sc_worker_id():
    return lax.axis_index("s") * SC_CORES + lax.axis_index("c")


SC_RING = 4
SC_ROW_SUB = 8
SC_ROW_LANE = 128


def _sc_ring(n_units, start, wait, compute):
    for u in range(SC_RING - 1):
        start(u, u)

    @pl.loop(0, n_units, step=SC_RING)
    def _(uu):
        for b in range(SC_RING):
            u = uu + b
            nxt = u + (SC_RING - 1)

            @pl.when(nxt < n_units)
            def _():
                start(nxt, (b + SC_RING - 1) % SC_RING)

            wait(u, b)
            compute(u, b)


def _sc_unit_off(u):
    off = u * SC_LANES
    return off if isinstance(off, int) else pl.multiple_of(off, SC_LANES)


def _sc_row_piece(rows, r, c):
    per = SC_ROW_LANE // SC_LANES
    return rows[r, c // per, pl.ds(pl.multiple_of((c % per) * SC_LANES, SC_LANES), SC_LANES)]


def peer_dots_sc(table, idx_flat, h):
    t, d = h.shape
    nsel = PEER_SEL
    tpw = t // SC_WORKERS
    g = SC_GROUP
    groups = tpw // g
    heads = nsel // SC_LANES
    pieces = d // SC_LANES
    units = g * heads
    row_buf = pltpu.VMEM((SC_LANES, SC_ROW_SUB, SC_ROW_LANE), F32)

    @functools.partial(
        pl.kernel, mesh=_sc_mesh(),
        out_type=jax.ShapeDtypeStruct((t * nsel,), F32),
        scratch_types=[
            pltpu.VMEM((g * nsel,), I32),
            pltpu.VMEM((g, d), F32),
            pltpu.VMEM((g * nsel,), F32),
            pltpu.VMEM((SC_LANES * SC_LANES,), F32),
            [row_buf] * SC_RING,
            [pltpu.SemaphoreType.DMA] * SC_RING,
        ],
        compiler_params=_sc_params(),
        name="peer_dots_sc",
    )
    def k(tab_hbm, idx_hbm, h_hbm, out_hbm, idx_v, h_v, out_v, red_v, rows, sems):
        wid = _sc_worker_id()
        lane = lax.iota(I32, SC_LANES)

        def copy(u, slot):
            ids = idx_v.at[pl.ds(_sc_unit_off(u), SC_LANES)]
            return pltpu.make_async_copy(tab_hbm.at[ids], rows[slot], sems[slot])

        def compute(u, slot):
            tt = u // heads

            def body(c, accs):
                hv = h_v[tt, pl.ds(pl.multiple_of(c * SC_LANES, SC_LANES), SC_LANES)]
                return tuple(accs[r] + _sc_row_piece(rows[slot], r, c) * hv for r in range(SC_LANES))

            accs = lax.fori_loop(0, pieces, body,
                                 tuple(jnp.zeros((SC_LANES,), F32) for _ in range(SC_LANES)))
            for r in range(SC_LANES):
                red_v[pl.ds(r * SC_LANES, SC_LANES)] = accs[r]
            cols = [plsc.load_gather(red_v, [lane * SC_LANES + j]) for j in range(SC_LANES)]
            while len(cols) > 1:
                cols = [cols[i] + cols[i + 1] for i in range(0, len(cols), 2)]
            out_v[pl.ds(_sc_unit_off(u), SC_LANES)] = cols[0]

        @pl.loop(0, groups)
        def _(gi):
            base = wid * tpw + gi * g
            pltpu.sync_copy(idx_hbm.at[pl.ds(base * nsel, g * nsel)], idx_v)
            pltpu.sync_copy(h_hbm.at[pl.ds(base, g)], h_v)
            _sc_ring(units, lambda u, s: copy(u, s).start(), lambda u, s: copy(u, s).wait(), compute)
            pltpu.sync_copy(out_v, out_hbm.at[pl.ds(base * nsel, g * nsel)])

    return k(table, idx_flat, h)


def peer_combine_sc(table, idx_flat, coef_flat, t):
    d = table.shape[1] * table.shape[2]
    nsel = PEER_SEL
    tpw = t // SC_WORKERS
    g = SC_GROUP
    groups = tpw // g
    heads = nsel // SC_LANES
    pieces = d // SC_LANES
    units = g * heads
    row_buf = pltpu.VMEM((SC_LANES, SC_ROW_SUB, SC_ROW_LANE), F32)

    @functools.partial(
        pl.kernel, mesh=_sc_mesh(),
        out_type=jax.ShapeDtypeStruct((t, d), F32),
        scratch_types=[
            pltpu.VMEM((g * nsel,), I32),
            pltpu.VMEM((g * nsel,), F32),
            pltpu.VMEM((g, d), F32),
            [row_buf] * SC_RING,
            [pltpu.SemaphoreType.DMA] * SC_RING,
        ],
        compiler_params=_sc_params(),
        name="peer_combine_sc",
    )
    def k(tab_hbm, idx_hbm, coef_hbm, out_hbm, idx_v, coef_v, y_v, rows, sems):
        wid = _sc_worker_id()

        def copy(u, slot):
            ids = idx_v.at[pl.ds(_sc_unit_off(u), SC_LANES)]
            return pltpu.make_async_copy(tab_hbm.at[ids], rows[slot], sems[slot])

        def compute(u, slot):
            tt = u // heads
            first = (u % heads) == 0
            cs = [plsc.load_gather(coef_v, [jnp.full((SC_LANES,), u * SC_LANES + r, I32)])
                  for r in range(SC_LANES)]

            @plsc.parallel_loop(0, pieces, unroll=2)
            def _(c):
                off = pl.multiple_of(c * SC_LANES, SC_LANES)
                terms = [cs[r] * _sc_row_piece(rows[slot], r, c) for r in range(SC_LANES)]
                while len(terms) > 1:
                    terms = [terms[i] + terms[i + 1] for i in range(0, len(terms), 2)]
                prev = y_v[tt, pl.ds(off, SC_LANES)]
                y_v[tt, pl.ds(off, SC_LANES)] = terms[0] + jnp.where(first, 0.0, prev)

        @pl.loop(0, groups)
        def _(gi):
            base = wid * tpw + gi * g
            pltpu.sync_copy(idx_hbm.at[pl.ds(base * nsel, g * nsel)], idx_v)
            pltpu.sync_copy(coef_hbm.at[pl.ds(base * nsel, g * nsel)], coef_v)
            _sc_ring(units, lambda u, s: copy(u, s).start(), lambda u, s: copy(u, s).wait(), compute)
            pltpu.sync_copy(y_v, out_hbm.at[pl.ds(base, g)])

    return k(table, idx_flat, coef_flat)


GELU_C0 = math.sqrt(2.0 / math.pi)
GELU_C1 = 0.044715


def _gelu_tanh(x):
    z = GELU_C0 * (x + GELU_C1 * (x * x * x))
    th = 1.0 - 2.0 / (jnp.exp(2.0 * z) + 1.0)
    return 0.5 * x * (1.0 + th)


def peer_experts_sc(tab_u, tab_v, idx_flat, w_flat, h):
    t, d = h.shape
    nsel = PEER_SEL
    tpw = t // SC_WORKERS
    g = SC_GROUP
    groups = tpw // g
    heads = nsel // SC_LANES
    pieces = d // SC_LANES
    units = g * heads
    row_buf = pltpu.VMEM((SC_LANES, SC_ROW_SUB, SC_ROW_LANE), F32)

    @functools.partial(
        pl.kernel, mesh=_sc_mesh(),
        out_type=jax.ShapeDtypeStruct((t, d), F32),
        scratch_types=[
            pltpu.VMEM((g * nsel,), I32),
            pltpu.VMEM((g * nsel,), F32),
            pltpu.VMEM((g, d), F32),
            pltpu.VMEM((g, d), F32),
            pltpu.VMEM((SC_LANES * SC_LANES,), F32),
            [row_buf] * SC_RING,
            [pltpu.SemaphoreType.DMA] * SC_RING,
        ],
        compiler_params=_sc_params(),
        name="peer_experts_sc",
    )
    def k(u_hbm, v_hbm, idx_hbm, w_hbm, h_hbm, out_hbm, idx_v, coef_v, h_v, y_v, red_v, rows, sems):
        wid = _sc_worker_id()
        lane = lax.iota(I32, SC_LANES)

        def copy(tab_hbm, u, slot):
            ids = idx_v.at[pl.ds(_sc_unit_off(u), SC_LANES)]
            return pltpu.make_async_copy(tab_hbm.at[ids], rows[slot], sems[slot])

        def dots(u, slot):
            tt = u // heads

            def body(c, accs):
                hv = h_v[tt, pl.ds(pl.multiple_of(c * SC_LANES, SC_LANES), SC_LANES)]
                return tuple(accs[r] + _sc_row_piece(rows[slot], r, c) * hv for r in range(SC_LANES))

            accs = lax.fori_loop(0, pieces, body,
                                 tuple(jnp.zeros((SC_LANES,), F32) for _ in range(SC_LANES)))
            for r in range(SC_LANES):
                red_v[pl.ds(r * SC_LANES, SC_LANES)] = accs[r]
            cols = [plsc.load_gather(red_v, [lane * SC_LANES + j]) for j in range(SC_LANES)]
            while len(cols) > 1:
                cols = [cols[i] + cols[i + 1] for i in range(0, len(cols), 2)]
            sl = pl.ds(_sc_unit_off(u), SC_LANES)
            coef_v[sl] = coef_v[sl] * _gelu_tanh(cols[0])

        def combine(u, slot):
            tt = u // heads
            first = (u % heads) == 0
            cs = [plsc.load_gather(coef_v, [jnp.full((SC_LANES,), u * SC_LANES + r, I32)])
                  for r in range(SC_LANES)]

            @plsc.parallel_loop(0, pieces, unroll=2)
            def _(c):
                off = pl.multiple_of(c * SC_LANES, SC_LANES)
                terms = [cs[r] * _sc_row_piece(rows[slot], r, c) for r in range(SC_LANES)]
                while len(terms) > 1:
                    terms = [terms[i] + terms[i + 1] for i in range(0, len(terms), 2)]
                prev = y_v[tt, pl.ds(off, SC_LANES)]
                y_v[tt, pl.ds(off, SC_LANES)] = terms[0] + jnp.where(first, 0.0, prev)

        @pl.loop(0, groups)
        def _(gi):
            base = wid * tpw + gi * g
            pltpu.sync_copy(idx_hbm.at[pl.ds(base * nsel, g * nsel)], idx_v)
            pltpu.sync_copy(w_hbm.at[pl.ds(base * nsel, g * nsel)], coef_v)
            pltpu.sync_copy(h_hbm.at[pl.ds(base, g)], h_v)
            _sc_ring(units, lambda u, s: copy(u_hbm, u, s).start(), lambda u, s: copy(u_hbm, u, s).wait(), dots)
            _sc_ring(units, lambda u, s: copy(v_hbm, u, s).start(), lambda u, s: copy(v_hbm, u, s).wait(), combine)
            pltpu.sync_copy(y_v, out_hbm.at[pl.ds(base, g)])

    return k(tab_u, tab_v, idx_flat, w_flat, h)


SC_PK_RING = 4
SC_PK_SUB = 4
HI_MASK = -65536


def pack_bf16_pairs(a):
    half = a.shape[1] // 2
    bits = lax.bitcast_convert_type(a.astype(BF16), jnp.uint16).astype(jnp.uint32)
    return lax.bitcast_convert_type(bits[:, :half] | (bits[:, half:] << 16), I32)


def _unpack_halves(x32):
    w = plsc.bitcast(x32, I32)
    return plsc.bitcast(w << 16, F32), plsc.bitcast(w & HI_MASK, F32)


def _tree_sum(xs):
    while len(xs) > 1:
        xs = [xs[i] + xs[i + 1] for i in range(0, len(xs), 2)]
    return xs[0]


def peer_experts_pk_sc(tab_uv, idx_flat, w_flat, hp, d):
    t = hp.shape[0]
    nsel = PEER_SEL
    tpw = t // SC_WORKERS
    g = SC_GROUP
    groups = tpw // g
    heads = nsel // SC_LANES
    chunks = d // 32
    units = g * heads
    ring = SC_PK_RING
    row_buf = pltpu.VMEM((SC_LANES, 2 * SC_PK_SUB, SC_ROW_LANE), I32)

    def row_words(rows, r, wc, sub0):
        per = SC_ROW_LANE // SC_LANES
        return plsc.bitcast(
            rows[r, sub0 + wc // per, pl.ds(pl.multiple_of((wc % per) * SC_LANES, SC_LANES), SC_LANES)], BF16)

    def ring_loop(n_units, start, wait, compute):
        for u in range(ring - 1):
            start(u, u)

        @pl.loop(0, n_units, step=ring)
        def _(uu):
            for b in range(ring):
                u = uu + b
                nxt = u + (ring - 1)

                @pl.when(nxt < n_units)
                def _():
                    start(nxt, (b + ring - 1) % ring)

                wait(u, b)
                compute(u, b)

    @functools.partial(
        pl.kernel, mesh=_sc_mesh(),
        out_type=jax.ShapeDtypeStruct((t, d), F32),
        scratch_types=[
            pltpu.VMEM((g * nsel,), I32),
            pltpu.VMEM((g * nsel,), F32),
            pltpu.VMEM((g, d // 2), I32),
            pltpu.VMEM((g, d), F32),
            pltpu.VMEM((SC_LANES * SC_LANES,), F32),
            [row_buf] * ring,
            [pltpu.SemaphoreType.DMA] * ring,
        ],
        compiler_params=_sc_params(),
        name="peer_experts_pk_sc",
    )
    def k(tab_hbm, idx_hbm, w_hbm, h_hbm, out_hbm, idx_v, coef_v, h_v, y_v, red_v, rows, sems):
        wid = _sc_worker_id()
        lane = lax.iota(I32, SC_LANES)

        def copy(u, slot):
            ids = idx_v.at[pl.ds(_sc_unit_off(u), SC_LANES)]
            return pltpu.make_async_copy(tab_hbm.at[ids], rows[slot], sems[slot])

        def dots(u, slot):
            tt = u // heads

            def body(cp, accs):
                out = []
                hv = [plsc.bitcast(h_v[tt, pl.ds(pl.multiple_of((2 * cp + i) * SC_LANES, SC_LANES), SC_LANES)], BF16)
                      for i in range(2)]
                for r in range(SC_LANES):
                    pr = (row_words(rows[slot], r, 2 * cp, 0) * hv[0]
                          + row_words(rows[slot], r, 2 * cp + 1, 0) * hv[1])
                    lo, hi = _unpack_halves(pr)
                    out.append(accs[r] + lo + hi)
                return tuple(out)

            accs = lax.fori_loop(0, chunks // 2, body,
                                 tuple(jnp.zeros((SC_LANES,), F32) for _ in range(SC_LANES)))
            for r in range(SC_LANES):
                red_v[pl.ds(r * SC_LANES, SC_LANES)] = accs[r]
            act = _tree_sum([plsc.load_gather(red_v, [lane * SC_LANES + j]) for j in range(SC_LANES)])
            sl = pl.ds(_sc_unit_off(u), SC_LANES)
            coef_v[sl] = coef_v[sl] * _gelu_tanh(act)

        def combine(u, slot):
            tt = u // heads
            first = (u % heads) == 0
            cb = []
            for r in range(SC_LANES):
                c = plsc.load_gather(coef_v, [jnp.full((SC_LANES,), u * SC_LANES + r, I32)])
                cb.append(plsc.pack(c, c, format=plsc.PackFormat.INTERLEAVED))

            @plsc.parallel_loop(0, chunks, unroll=2)
            def _(wc):
                lo, hi = _unpack_halves(
                    _tree_sum([cb[r] * row_words(rows[slot], r, wc, SC_PK_SUB) for r in range(SC_LANES)]))
                for half, val in ((0, lo), (1, hi)):
                    sl = pl.ds(pl.multiple_of(half * (d // 2) + wc * SC_LANES, SC_LANES), SC_LANES)
                    y_v[tt, sl] = val + jnp.where(first, 0.0, y_v[tt, sl])

        def unit(u, slot):
            dots(u, slot)
            combine(u, slot)

        @pl.loop(0, groups)
        def _(gi):
            base = wid * tpw + gi * g
            pltpu.sync_copy(idx_hbm.at[pl.ds(base * nsel, g * nsel)], idx_v)
            pltpu.sync_copy(w_hbm.at[pl.ds(base * nsel, g * nsel)], coef_v)
            pltpu.sync_copy(h_hbm.at[pl.ds(base, g)], h_v)
            ring_loop(units, lambda u, s: copy(u, s).start(), lambda u, s: copy(u, s).wait(), unit)
            pltpu.sync_copy(y_v, out_hbm.at[pl.ds(base, g)])

    return k(tab_uv, idx_flat, w_flat, hp)


def kernel(x, mem, rel_bias, ln_mix, w_in, hg_lower, hg_norm, w_up_a, w_up_b, w_out, ln_cross, ln_mem, wq_x, wk_x, wv_x, wo_x, ln_ffn, peer_query, peer_subkeys, peer_u, peer_v, ln_final):
    b, s, d = x.shape
    depth = w_in.shape[0]
    assert depth == 1, "the residual after PEER is fused into the final norm"
    assert s % MB_BLOCK == 0 and s % HG_CHUNK == 0 and s % (PEER_SLICES * SC_WORKERS * SC_GROUP) == 0
    nb = s // MB_BLOCK
    row = lambda a: a.reshape(1, -1).astype(F32)
    lb_all = jnp.cumsum(jax.nn.softmax(hg_lower.astype(F32), axis=0), axis=0)
    bias = moba_bias_tiles(rel_bias)
    n_hg = 4 * HG_WIDTH
    n_qk = 2 * MB_WIDTH
    n_mb = 3 * MB_WIDTH
    l = 0
    w = w_in[l].astype(BF16)
    w_hg, w_qk, w_vt, w_g = w[:, :n_hg], w[:, n_hg:n_hg + n_qk], w[:, n_hg + n_qk:n_hg + n_mb].T, w[:, n_hg + n_mb:]
    wa, wb, wo = w_up_a[l].astype(BF16), w_up_b[l].astype(BF16), w_out[l].astype(BF16)
    wqx, wox = wq_x[l].astype(BF16), wo_x[l].astype(BF16)
    wpq, sk = peer_query[l].astype(BF16), peer_subkeys[l].astype(F32)
    tab3 = lambda a: pack_bf16_pairs(a.astype(F32)).reshape(a.shape[0], SC_PK_SUB, SC_ROW_LANE)
    tab_uv = jnp.concatenate([tab3(peer_u[l]), tab3(peer_v[l])], axis=1)
    kx, vx = mem_kv(mem, row(ln_mem[l]), wk_x[l].astype(BF16), wv_x[l].astype(BF16))

    outs = []
    for bi in range(b):
        x2d = x[bi]
        p0, pqk, vt, pg = in_proj(x2d, row(ln_mix[l]), w_hg, w_qk, w_vt, w_g)
        ya = hgrn2(p0, row(lb_all[l]), row(hg_norm[l]), 1, s)
        km = moba_kmean(pqk, 1, s).reshape(1, nb, MB_WIDTH)
        ts = s // PEER_SLICES
        for tok0 in range(0, s, ts):
            yb = moba_attention(pqk, vt, km, bias, 1, s, tok0 // MB_BLOCK, ts // MB_BLOCK)
            xs = mix_out(x2d, ya, yb, pg, wa, wb, wo, tok0)
            xs = cross_attn(xs, row(ln_cross[l]), wqx, kx[bi:bi + 1], vx[bi:bi + 1], wox, ts)
            hp, eidx, wts = peer_route(xs, row(ln_ffn[l]), wpq, sk, 0, ts)
            y = peer_experts_pk_sc(tab_uv, eidx.reshape(ts * PEER_SEL), wts.reshape(ts * PEER_SEL), hp, d)
            outs.append(final_norm(xs, y, row(ln_final), 0))
    return jnp.concatenate(outs, axis=0).reshape(b, s, d)
```

```python
import functools
import math

import jax
import jax.numpy as jnp
import numpy as np
from jax import lax
from jax.experimental import pallas as pl
from jax.experimental.pallas import tpu as pltpu
from jax.experimental.pallas import tpu_sc as plsc

F32 = jnp.float32
BF16 = jnp.bfloat16
I32 = jnp.int32
EPS = 1e-6
NEG_INF = float("-inf")

HG_HEADS = 4
HG_D = 128
HG_WIDTH = HG_HEADS * HG_D
HG_CHUNK = 64
HG_SUB = 16
MB_HEADS = 8
MB_DH = 64
MB_WIDTH = MB_HEADS * MB_DH
MB_BLOCK = 256
MB_TOPK = 3
MB_BIAS_TILES = 8
REL_BUCKETS = 32
REL_MAX_DIST = 2048
X_HEADS = 4
PEER_HEADS = 8
PEER_NKEYS = 128
PEER_TOPK = 16
PEER_HALF = 128
PEER_SEL = PEER_HEADS * PEER_TOPK
PEER_SLICES = 4

VMEM_LIMIT = 56 * 1024 * 1024


def _cparams(sem):
    return pltpu.CompilerParams(dimension_semantics=sem, vmem_limit_bytes=VMEM_LIMIT)


def _rms(x, g):
    ms = jnp.mean(x * x, axis=-1, keepdims=True)
    return x * lax.rsqrt(ms + EPS) * g


def _in_proj_kernel(x_ref, g_ref, w0_ref, w1_ref, wvt_ref, w2_ref, o0_ref, o1_ref, ovt_ref, o2_ref):
    h = _rms(x_ref[...], g_ref[...]).astype(BF16)
    o0_ref[...] = jnp.dot(h, w0_ref[...], preferred_element_type=F32)
    o1_ref[...] = jnp.dot(h, w1_ref[...], preferred_element_type=F32).astype(BF16)
    vt = lax.dot_general(wvt_ref[...], h, (((1,), (1,)), ((), ())), preferred_element_type=F32).astype(BF16)
    for g in range(MB_WIDTH // MB_LG):
        ovt_ref[0, g * MB_VROWS:g * MB_VROWS + MB_LG, :] = vt[g * MB_LG:(g + 1) * MB_LG]
        ovt_ref[0, g * MB_VROWS + MB_LG:(g + 1) * MB_VROWS, :] = jnp.ones((MB_ONES, vt.shape[1]), BF16)
    o2_ref[...] = jnp.dot(h, w2_ref[...], preferred_element_type=F32).astype(BF16)


def in_proj(x2d, g, w0, w1, wvt, w2):
    t, d = x2d.shape
    tm = MB_BLOCK
    assert wvt.shape[0] == MB_WIDTH
    n0, n1, nv, n2 = w0.shape[1], w1.shape[1], MB_VT_ROWS, w2.shape[1]
    full = lambda a: pl.BlockSpec(a.shape, lambda i: (0, 0))
    return pl.pallas_call(
        _in_proj_kernel,
        grid=(t // tm,),
        in_specs=[pl.BlockSpec((tm, d), lambda i: (i, 0)), full(g), full(w0), full(w1), full(wvt), full(w2)],
        out_specs=[pl.BlockSpec((tm, n0), lambda i: (i, 0)),
                   pl.BlockSpec((tm, n1), lambda i: (i, 0)),
                   pl.BlockSpec((1, nv, tm), lambda i: (i, 0, 0)),
                   pl.BlockSpec((tm, n2), lambda i: (i, 0))],
        out_shape=[jax.ShapeDtypeStruct((t, n0), F32),
                   jax.ShapeDtypeStruct((t, n1), BF16),
                   jax.ShapeDtypeStruct((t // tm, nv, tm), BF16),
                   jax.ShapeDtypeStruct((t, n2), BF16)],
        compiler_params=_cparams(("parallel",)),
        name="in_proj",
    )(x2d, g, w0, w1, wvt, w2)


def _hgrn_kernel(q_ref, f_ref, i_ref, g_ref, lb_ref, gain_ref, o_ref, st_ref):
    c = pl.program_id(1)

    @pl.when(c == 0)
    def _():
        st_ref[...] = jnp.zeros_like(st_ref)

    C, S = HG_CHUNK, HG_SUB
    row = lax.broadcasted_iota(I32, (C, C), 0)
    col = lax.broadcasted_iota(I32, (C, C), 1)
    tril = (row >= col).astype(F32)
    t_iota = lax.broadcasted_iota(I32, (S, 1), 0)

    for h in range(HG_HEADS):
        sl = slice(h * HG_D, (h + 1) * HG_D)
        q = q_ref[:, sl]
        v = i_ref[:, sl]
        lb = lb_ref[:, sl]
        f = lb + (1.0 - lb) * jax.nn.sigmoid(f_ref[:, sl])
        lf = jnp.log(f)
        k = 1.0 - f
        b = jnp.dot(tril, lf, precision=lax.Precision.HIGHEST, preferred_element_type=F32)
        st = st_ref[h]
        vb = v.astype(BF16)
        qd = (q * jnp.exp(b)).astype(BF16)
        o_inter = lax.dot_general(qd, st.astype(BF16), (((1,), (1,)), ((), ())),
                                  preferred_element_type=F32)
        outs = []
        for i in range(C // S):
            r0 = i * S
            qi = q[r0:r0 + S]
            ki = k[r0:r0 + S]
            bi = b[r0:r0 + S]
            vi = v[r0:r0 + S]
            oi = o_inter[r0:r0 + S]
            if i > 0:
                bs = b[r0 - 1:r0]
                qh = (qi * jnp.exp(bi - bs)).astype(BF16)
                kh = (k[:r0] * jnp.exp(bs - b[:r0])).astype(BF16)
                a = lax.dot_general(qh, kh, (((1,), (1,)), ((), ())), preferred_element_type=F32)
                oi = oi + jnp.dot(a.astype(BF16), vb[:r0], preferred_element_type=F32)
            for s in range(S):
                dec = jnp.exp(jnp.minimum(bi - bi[s:s + 1], 0.0))
                p = qi * ki[s:s + 1] * dec
                a_s = jnp.sum(p, axis=-1, keepdims=True)
                a_s = jnp.where(t_iota >= s, a_s, 0.0)
                oi = oi + a_s * vi[s:s + 1]
            outs.append(oi)
        o = jnp.concatenate(outs, axis=0)
        b_end = b[C - 1:C]
        kd = (k * jnp.exp(b_end - b)).astype(BF16)
        upd = lax.dot_general(vb, kd, (((0,), (0,)), ((), ())), preferred_element_type=F32)
        st_ref[h] = st * jnp.exp(b_end) + upd
        o = o * lax.rsqrt(jnp.mean(o * o, axis=-1, keepdims=True) + EPS)
        g = g_ref[:, sl]
        o_ref[:, sl] = (o * gain_ref[:, sl] * (g * jax.nn.sigmoid(g))).astype(o_ref.dtype)


def hgrn2(p0, lb, gain, batch, seq):
    t = p0.shape[0]
    nc = seq // HG_CHUNK
    w = HG_WIDTH

    def col(j):
        return pl.BlockSpec((HG_CHUNK, w), lambda b, c, j=j: (b * nc + c, j))

    return pl.pallas_call(
        _hgrn_kernel,
        grid=(batch, nc),
        in_specs=[col(0), col(1), col(2), col(3),
                  pl.BlockSpec((1, w), lambda b, c: (0, 0)),
                  pl.BlockSpec((1, w), lambda b, c: (0, 0))],
        out_specs=pl.BlockSpec((HG_CHUNK, w), lambda b, c: (b * nc + c, 0)),
        out_shape=jax.ShapeDtypeStruct((t, w), BF16),
        scratch_shapes=[pltpu.VMEM((HG_HEADS, HG_D, HG_D), F32)],
        compiler_params=_cparams(("parallel", "arbitrary")),
        name="hgrn2",
    )(p0, p0, p0, p0, lb, gain)


def _kmean_kernel(k_ref, o_ref):
    o_ref[0] = jnp.mean(k_ref[...].astype(F32), axis=0, keepdims=True)


def moba_kmean(p1, batch, seq):
    nbt = p1.shape[0] // MB_BLOCK
    return pl.pallas_call(
        _kmean_kernel,
        grid=(nbt,),
        in_specs=[pl.BlockSpec((MB_BLOCK, MB_WIDTH), lambda i: (i, 1))],
        out_specs=pl.BlockSpec((1, 1, MB_WIDTH), lambda i: (i, 0, 0)),
        out_shape=jax.ShapeDtypeStruct((nbt, 1, MB_WIDTH), F32),
        compiler_params=_cparams(("parallel",)),
        name="moba_kmean",
    )(p1)


MB_PAIR = 4
MB_PW = MB_PAIR * MB_DH
MB_LG = 128
MB_ONES = 16
MB_VROWS = MB_LG + MB_ONES
MB_VT_ROWS = (MB_WIDTH // MB_LG) * MB_VROWS


def _moba_kernel(q_ref, k_ref, vt_ref, km_ref, bias_ref, o_ref, *scratch, qb0):
    m_ref, l_ref, al_ref, acc_ref, msk_ref, s_ref, p_ref = (
        scratch[i * MB_PAIR:(i + 1) * MB_PAIR] for i in range(7))
    qi = pl.program_id(2) + qb0
    nb = km_ref.shape[0]
    blk = MB_BLOCK
    heads = range(MB_PAIR)
    grp = lambda hh: slice((hh // 2) * MB_LG, (hh // 2 + 1) * MB_LG)
    q = q_ref[...]
    lane = lax.broadcasted_iota(I32, (blk, MB_LG), 1)
    in_head = [(lane < MB_DH) if hh % 2 == 0 else (lane >= MB_DH) for hh in heads]
    qs = q * jnp.asarray(MB_DH ** -0.5, BF16)
    nt = (((1,), (1,)), ((), ()))
    qf = q.astype(F32)
    qht = [jnp.where(in_head[hh], qs[:, grp(hh)].astype(F32), 0.0).T.astype(BF16) for hh in heads]

    n_io = lax.broadcasted_iota(I32, (nb, blk), 0)
    for hh in heads:
        gate = lax.dot_general(km_ref[:, grp(hh)], jnp.where(in_head[hh], qf[:, grp(hh)], 0.0), nt,
                               precision=lax.Precision.HIGHEST, preferred_element_type=F32)
        gate = jnp.where(n_io < qi, gate, NEG_INF)
        chosen = n_io < 0
        for _ in range(MB_TOPK):
            mx = jnp.max(gate, axis=0, keepdims=True)
            ix = jnp.min(jnp.where(gate == mx, n_io, nb), axis=0, keepdims=True)
            hit = n_io == ix
            chosen = chosen | (hit & (mx > NEG_INF))
            gate = jnp.where(hit, NEG_INF, gate)
        msk_ref[hh][...] = jnp.where(chosen, 0.0, NEG_INF)

    own_rows = lambda r, hh: r[(hh % 2) * MB_DH:(hh % 2 + 1) * MB_DH]

    vgrp = lambda hh: slice((hh // 2) * MB_VROWS, (hh // 2 + 1) * MB_VROWS)

    def pv_stage(blk_idx):
        vtb = vt_ref[blk_idx]
        r = [jnp.dot(vtb[vgrp(hh)], p_ref[hh][...], preferred_element_type=F32) for hh in heads]
        al = [al_ref[hh][...] for hh in heads]
        a_new = [al[hh] * acc_ref[hh][...] + own_rows(r[hh], hh) for hh in heads]
        l_new = [al[hh] * l_ref[hh][...] + r[hh][MB_LG:MB_LG + 1] for hh in heads]
        return a_new, l_new

    def store_pv(a_new, l_new):
        for hh in heads:
            acc_ref[hh][...] = a_new[hh]
            l_ref[hh][...] = l_new[hh]

    def softmax_stage():
        s = [s_ref[hh][...] for hh in heads]
        m_old = [m_ref[hh][...] for hh in heads]
        m_new = [jnp.maximum(m_old[hh], jnp.max(s[hh], axis=0, keepdims=True)) for hh in heads]
        alpha = [jnp.exp(m_old[hh] - m_new[hh]) for hh in heads]
        p = [jnp.exp((s[hh] - m_new[hh]).astype(BF16)) for hh in heads]
        return p, alpha, m_new

    def store_softmax(p, alpha, m_new):
        for hh in heads:
            p_ref[hh][...] = p[hh]
            al_ref[hh][...] = alpha[hh]
            m_ref[hh][...] = m_new[hh]

    k_own = k_ref[pl.ds(pl.multiple_of(qi * blk, blk), blk), :]
    key_io = lax.broadcasted_iota(I32, (blk, blk), 0)
    qry_io = lax.broadcasted_iota(I32, (blk, blk), 1)
    for hh in heads:
        s = jnp.dot(k_own[:, grp(hh)], qht[hh], preferred_element_type=F32) + bias_ref[hh, 0]
        s_ref[hh][...] = jnp.where(key_io <= qry_io, s, NEG_INF)
        m_ref[hh][...] = jnp.full((1, blk), NEG_INF, F32)
        l_ref[hh][...] = jnp.zeros((1, blk), F32)
        al_ref[hh][...] = jnp.ones((1, blk), F32)
        acc_ref[hh][...] = jnp.zeros((MB_DH, blk), F32)
        p_ref[hh][...] = jnp.zeros((blk, blk), BF16)

    def step(i, carry):
        pv = pv_stage(jnp.where(i <= 1, qi, i - 2))
        sm = softmax_stage()
        kn = k_ref[pl.ds(pl.multiple_of(i * blk, blk), blk), :]
        d = jnp.minimum(qi - i, MB_BIAS_TILES - 1)
        s_next = [jnp.dot(kn[:, grp(hh)], qht[hh], preferred_element_type=F32)
                  + bias_ref[hh, d] + msk_ref[hh][pl.ds(i, 1), :] for hh in heads]
        store_pv(*pv)
        for hh in heads:
            s_ref[hh][...] = s_next[hh]
        store_softmax(*sm)
        return carry

    lax.fori_loop(0, qi, step, 0)
    pv = pv_stage(jnp.where(qi <= 1, qi, qi - 2))
    sm = softmax_stage()
    store_pv(*pv)
    store_softmax(*sm)
    a_fin, l_fin = pv_stage(jnp.where(qi == 0, qi, qi - 1))
    out_t = jnp.concatenate([a_fin[hh] / l_fin[hh] for hh in heads], axis=0)
    o_ref[...] = out_t.T.astype(o_ref.dtype)


def moba_attention(pqk, vt, km, bias, batch, seq, qb0=0, nqb=None):
    nb = seq // MB_BLOCK
    nqb = nb if nqb is None else nqb
    t = batch * nqb * MB_BLOCK
    groups = MB_WIDTH // MB_PW
    return pl.pallas_call(
        functools.partial(_moba_kernel, qb0=qb0),
        grid=(batch, groups, nqb),
        in_specs=[
            pl.BlockSpec((MB_BLOCK, MB_PW), lambda b, j, i: (b * nb + qb0 + i, j)),
            pl.BlockSpec((seq, MB_PW), lambda b, j, i: (b, groups + j)),
            pl.BlockSpec((nb, (MB_PW // MB_LG) * MB_VROWS, MB_BLOCK), lambda b, j, i: (b, j, 0)),
            pl.BlockSpec((None, nb, MB_PW), lambda b, j, i: (b, 0, j)),
            pl.BlockSpec((MB_PAIR, MB_BIAS_TILES, MB_BLOCK, MB_BLOCK), lambda b, j, i: (j, 0, 0, 0)),
        ],
        out_specs=pl.BlockSpec((MB_BLOCK, MB_PW), lambda b, j, i: (b * nqb + i, j)),
        out_shape=jax.ShapeDtypeStruct((t, MB_WIDTH), BF16),
        scratch_shapes=(
            [pltpu.VMEM((1, MB_BLOCK), F32)] * (3 * MB_PAIR)
            + [pltpu.VMEM((MB_DH, MB_BLOCK), F32)] * MB_PAIR
            + [pltpu.VMEM((nb, MB_BLOCK), F32)] * MB_PAIR
            + [pltpu.VMEM((MB_BLOCK, MB_BLOCK), F32)] * MB_PAIR
            + [pltpu.VMEM((MB_BLOCK, MB_BLOCK), BF16)] * MB_PAIR
        ),
        compiler_params=_cparams(("parallel", "parallel", "arbitrary")),
        name="moba_attn",
    )(pqk, pqk, vt, km, bias)


def _t5_bucket(dist):
    max_exact = REL_BUCKETS // 2
    scaled = jnp.log(jnp.maximum(dist, 1).astype(F32) / max_exact) / math.log(REL_MAX_DIST / max_exact)
    large = jnp.minimum(max_exact + (scaled * (REL_BUCKETS - max_exact)).astype(I32), REL_BUCKETS - 1)
    return jnp.where(dist < max_exact, dist, large)


def moba_bias_tiles(rel_bias):
    blk = MB_BLOCK
    span = 2 * blk - 1
    x = jnp.arange(span) - (blk - 1)
    dist = jnp.maximum(jnp.arange(MB_BIAS_TILES)[:, None] * blk + x[None, :], 0)
    w = rel_bias.astype(F32).T[:, _t5_bucket(dist)]
    h = w.shape[0]
    wp = jnp.pad(w, ((0, 0), (0, 0), (0, 1)))
    a = jnp.broadcast_to(wp[:, :, None, :], (h, MB_BIAS_TILES, blk, span + 1))
    a = a.reshape(h, MB_BIAS_TILES, blk * (span + 1))[:, :, :blk * span]
    return a.reshape(h, MB_BIAS_TILES, blk, span)[:, :, :, blk - 1:]


def _mix_kernel(x_ref, ya_ref, yb_ref, ga_ref, gb_ref, wa_ref, wb_ref, wo_ref, o_ref):
    za = jnp.dot(ya_ref[...], wa_ref[...], preferred_element_type=F32)
    zb = jnp.dot(yb_ref[...], wb_ref[...], preferred_element_type=F32)
    z = jax.nn.sigmoid(ga_ref[...].astype(F32)) * za + jax.nn.sigmoid(gb_ref[...].astype(F32)) * zb
    o_ref[...] = x_ref[...] + jnp.dot(z.astype(BF16), wo_ref[...], preferred_element_type=F32)


def mix_out(x2d, ya, yb, pg, wa, wb, wo, tok0=0, tm=256):
    t = yb.shape[0]
    d = x2d.shape[1]
    w = ya.shape[1]
    b0 = tok0 // tm
    return pl.pallas_call(
        _mix_kernel,
        grid=(t // tm,),
        in_specs=[
            pl.BlockSpec((tm, d), lambda i: (b0 + i, 0)),
            pl.BlockSpec((tm, w), lambda i: (b0 + i, 0)),
            pl.BlockSpec((tm, w), lambda i: (i, 0)),
            pl.BlockSpec((tm, d), lambda i: (b0 + i, 0)),
            pl.BlockSpec((tm, d), lambda i: (b0 + i, 1)),
            pl.BlockSpec((w, d), lambda i: (0, 0)),
            pl.BlockSpec((w, d), lambda i: (0, 0)),
            pl.BlockSpec((d, d), lambda i: (0, 0)),
        ],
        out_specs=pl.BlockSpec((tm, d), lambda i: (i, 0)),
        out_shape=jax.ShapeDtypeStruct((t, d), F32),
        compiler_params=_cparams(("parallel",)),
        name="mix_out",
    )(x2d, ya, yb, pg, pg, wa, wb, wo)


def _mem_kv_kernel(m_ref, g_ref, wk_ref, wv_ref, k_ref, v_ref):
    mn = _rms(m_ref[...], g_ref[...]).astype(BF16)
    k_ref[...] = jnp.dot(mn, wk_ref[...], preferred_element_type=F32).astype(BF16)
    v_ref[...] = jnp.dot(mn, wv_ref[...], preferred_element_type=F32).astype(BF16)


def mem_kv(mem, g, wk, wv):
    b, m, d = mem.shape
    spec = pl.BlockSpec((None, m, d), lambda i: (i, 0, 0))
    wspec = pl.BlockSpec((d, d), lambda i: (0, 0))
    return pl.pallas_call(
        _mem_kv_kernel,
        grid=(b,),
        in_specs=[spec, pl.BlockSpec((1, d), lambda i: (0, 0)), wspec, wspec],
        out_specs=[spec, spec],
        out_shape=[jax.ShapeDtypeStruct((b, m, d), BF16)] * 2,
        compiler_params=_cparams(("parallel",)),
        name="mem_kv",
    )(mem, g, wk, wv)


def _cross_kernel(x_ref, g_ref, wq_ref, k_ref, v_ref, wo_ref, o_ref):
    x = x_ref[...]
    d = x.shape[1]
    dh = d // X_HEADS
    h = _rms(x, g_ref[...]).astype(BF16)
    q = (jnp.dot(h, wq_ref[...], preferred_element_type=F32) * (dh ** -0.5)).astype(BF16)
    outs = []
    for hh in range(X_HEADS):
        sl = slice(hh * dh, (hh + 1) * dh)
        s = lax.dot_general(q[:, sl], k_ref[:, sl], (((1,), (1,)), ((), ())),
                            preferred_element_type=F32)
        p = jnp.exp(s - jnp.max(s, axis=1, keepdims=True))
        l = jnp.sum(p, axis=1, keepdims=True)
        o = jnp.dot(p.astype(BF16), v_ref[:, sl], preferred_element_type=F32) / l
        outs.append(o.astype(BF16))
    o = jnp.concatenate(outs, axis=1)
    o_ref[...] = x + jnp.dot(o, wo_ref[...], preferred_element_type=F32)


def cross_attn(x2d, g, wq, kx, vx, wo, seq, tm=256):
    t, d = x2d.shape
    m = kx.shape[1]
    per_b = seq // tm
    kv = pl.BlockSpec((None, m, d), lambda i: (i // per_b, 0, 0))
    wspec = pl.BlockSpec((d, d), lambda i: (0, 0))
    return pl.pallas_call(
        _cross_kernel,
        grid=(t // tm,),
        in_specs=[pl.BlockSpec((tm, d), lambda i: (i, 0)), pl.BlockSpec((1, d), lambda i: (0, 0)),
                  wspec, kv, kv, wspec],
        out_specs=pl.BlockSpec((tm, d), lambda i: (i, 0)),
        out_shape=jax.ShapeDtypeStruct((t, d), F32),
        compiler_params=_cparams(("parallel",)),
        name="cross_attn",
    )(x2d, g, wq, kx, vx, wo)


def _topk_rows(sc, k):
    n = sc.shape[0]
    io = lax.broadcasted_iota(I32, sc.shape, 0)
    vals, ids = [], []
    for _ in range(k):
        m = jnp.max(sc, axis=0, keepdims=True)
        ix = jnp.min(jnp.where(sc == m, io, n), axis=0, keepdims=True)
        vals.append(m)
        ids.append(ix)
        sc = jnp.where(io == ix, NEG_INF, sc)
    return jnp.concatenate(vals, axis=0), jnp.concatenate(ids, axis=0)


def _pack_bf16_halves(h):
    bits = lax.bitcast_convert_type(h, I32)
    r = bits + 0x7FFF + (lax.shift_right_logical(bits, 16) & 1)
    half = h.shape[1] // 2
    return lax.shift_right_logical(r[:, :half], 16) | (r[:, half:] & HI_MASK)


def _route_kernel(x_ref, g_ref, wq_ref, sk_ref, hp_ref, idx_ref, w_ref, hb_ref, it_ref, wt_ref):
    p = pl.program_id(1)

    @pl.when(p == 0)
    def _():
        h = _rms(x_ref[...], g_ref[...])
        hp_ref[...] = _pack_bf16_halves(h)
        hb_ref[...] = h.astype(BF16)

    qh = jnp.dot(hb_ref[...], wq_ref[...], preferred_element_type=F32)
    tops = []
    for c in range(2):
        seg = qh[:, c * PEER_HALF:(c + 1) * PEER_HALF]
        sc = lax.dot_general(sk_ref[c], seg, (((1,), (1,)), ((), ())),
                             precision=lax.Precision.HIGHEST, preferred_element_type=F32)
        tops.append(_topk_rows(sc, PEER_TOPK))
    (s0, i0), (s1, i1) = tops
    k = PEER_TOPK
    sub = 8
    tm = s0.shape[1]
    r8 = lax.broadcasted_iota(I32, (sub, tm), 0)
    r16 = lax.broadcasted_iota(I32, (k, tm), 0)
    cand_b = [s0[0:1] + s1, s0[1:2] + s1[:sub]]
    cidx_b = [i0[0:1] * PEER_NKEYS + i1, i0[1:2] * PEER_NKEYS + i1[:sub]]
    pos_b = [r16, k + r8]
    for a in range(2, sub):
        keep = r8 < (k // (a + 1))
        cand_b.append(jnp.where(keep, s0[a:a + 1] + s1[:sub], NEG_INF))
        cidx_b.append(i0[a:a + 1] * PEER_NKEYS + i1[:sub])
        pos_b.append(a * k + r8)
    cand_b.append(s0[sub:] + s1[0:1])
    cidx_b.append(i0[sub:] * PEER_NKEYS + i1[0:1])
    pos_b.append((sub + r8) * k)
    cand = jnp.concatenate(cand_b, axis=0)
    cidx = jnp.concatenate(cidx_b, axis=0)
    pos = jnp.concatenate(pos_b, axis=0)
    vals, ids = [], []
    for _ in range(k):
        m = jnp.max(cand, axis=0, keepdims=True)
        px = jnp.min(jnp.where(cand == m, pos, k * k), axis=0, keepdims=True)
        hit = pos == px
        vals.append(m)
        ids.append(jnp.sum(jnp.where(hit, cidx, 0), axis=0, keepdims=True))
        cand = jnp.where(hit, NEG_INF, cand)
    sf = jnp.concatenate(vals, axis=0)
    e = jnp.exp(sf - sf[0:1])
    rows = pl.ds(pl.multiple_of(p * PEER_TOPK, PEER_TOPK), PEER_TOPK)
    wt_ref[rows, :] = e / jnp.sum(e, axis=0, keepdims=True)
    it_ref[rows, :] = jnp.concatenate(ids, axis=0)

    @pl.when(p == pl.num_programs(1) - 1)
    def _():
        idx_ref[...] = it_ref[...].T
        w_ref[...] = wt_ref[...].T


def peer_route(x2d, g, wq, sk, tok0, t, tm=512):
    d = x2d.shape[1]
    ph = sk.shape[0]
    nsel = ph * PEER_TOPK
    blk0 = tok0 // tm
    return pl.pallas_call(
        _route_kernel,
        grid=(t // tm, ph),
        in_specs=[
            pl.BlockSpec((tm, d), lambda i, p: (blk0 + i, 0)),
            pl.BlockSpec((1, d), lambda i, p: (0, 0)),
            pl.BlockSpec((d, 2 * PEER_HALF), lambda i, p: (0, p)),
            pl.BlockSpec((None, 2, PEER_NKEYS, PEER_HALF), lambda i, p: (p, 0, 0, 0)),
        ],
        out_specs=[
            pl.BlockSpec((tm, d // 2), lambda i, p: (i, 0)),
            pl.BlockSpec((tm, nsel), lambda i, p: (i, 0)),
            pl.BlockSpec((tm, nsel), lambda i, p: (i, 0)),
        ],
        out_shape=[jax.ShapeDtypeStruct((t, d // 2), I32),
                   jax.ShapeDtypeStruct((t, nsel), I32),
                   jax.ShapeDtypeStruct((t, nsel), F32)],
        scratch_shapes=[pltpu.VMEM((tm, d), BF16),
                        pltpu.VMEM((nsel, tm), I32),
                        pltpu.VMEM((nsel, tm), F32)],
        compiler_params=_cparams(("parallel", "arbitrary")),
        name="peer_route",
    )(x2d, g, wq, sk)


def _coef_kernel(w_ref, a_ref, o_ref):
    o_ref[...] = w_ref[...] * jax.nn.gelu(a_ref[...])


def peer_coef(w, act, tm=1024):
    t, n = w.shape
    spec = pl.BlockSpec((tm, n), lambda i: (i, 0))
    return pl.pallas_call(
        _coef_kernel, grid=(t // tm,), in_specs=[spec, spec], out_specs=spec,
        out_shape=jax.ShapeDtypeStruct((t, n), F32),
        compiler_params=_cparams(("parallel",)), name="peer_coef",
    )(w, act)


def _final_kernel(x_ref, y_ref, g_ref, o_ref):
    o_ref[...] = _rms(x_ref[...] + y_ref[...], g_ref[...])


def final_norm(x2d, y, g, tok0, tm=512):
    t, d = y.shape
    blk0 = tok0 // tm
    spec = pl.BlockSpec((tm, d), lambda i: (i, 0))
    return pl.pallas_call(
        _final_kernel, grid=(t // tm,),
        in_specs=[pl.BlockSpec((tm, d), lambda i: (blk0 + i, 0)), spec, pl.BlockSpec((1, d), lambda i: (0, 0))],
        out_specs=spec,
        out_shape=jax.ShapeDtypeStruct((t, d), F32),
        compiler_params=_cparams(("parallel",)), name="final_norm",
    )(x2d, y, g)


SC_CORES = 2
SC_SUBCORES = 16
SC_WORKERS = SC_CORES * SC_SUBCORES
SC_LANES = 16
SC_GROUP = 16


def _sc_mesh():
    return plsc.VectorSubcoreMesh(core_axis_name="c", subcore_axis_name="s")


def _sc_params():
    return pltpu.CompilerParams(needs_layout_passes=False)


def _sc_worker_id():
    return lax.axis_index("s") * SC_CORES + lax.axis_index("c")


SC_RING = 4
SC_ROW_SUB = 8
SC_ROW_LANE = 128


def _sc_ring(n_units, start, wait, compute):
    for u in range(SC_RING - 1):
        start(u, u)

    @pl.loop(0, n_units, step=SC_RING)
    def _(uu):
        for b in range(SC_RING):
            u = uu + b
            nxt = u + (SC_RING - 1)

            @pl.when(nxt < n_units)
            def _():
                start(nxt, (b + SC_RING - 1) % SC_RING)

            wait(u, b)
            compute(u, b)


def _sc_unit_off(u):
    off = u * SC_LANES
    return off if isinstance(off, int) else pl.multiple_of(off, SC_LANES)


def _sc_row_piece(rows, r, c):
    per = SC_ROW_LANE // SC_LANES
    return rows[r, c // per, pl.ds(pl.multiple_of((c % per) * SC_LANES, SC_LANES), SC_LANES)]


def peer_dots_sc(table, idx_flat, h):
    t, d = h.shape
    nsel = PEER_SEL
    tpw = t // SC_WORKERS
    g = SC_GROUP
    groups = tpw // g
    heads = nsel // SC_LANES
    pieces = d // SC_LANES
    units = g * heads
    row_buf = pltpu.VMEM((SC_LANES, SC_ROW_SUB, SC_ROW_LANE), F32)

    @functools.partial(
        pl.kernel, mesh=_sc_mesh(),
        out_type=jax.ShapeDtypeStruct((t * nsel,), F32),
        scratch_types=[
            pltpu.VMEM((g * nsel,), I32),
            pltpu.VMEM((g, d), F32),
            pltpu.VMEM((g * nsel,), F32),
            pltpu.VMEM((SC_LANES * SC_LANES,), F32),
            [row_buf] * SC_RING,
            [pltpu.SemaphoreType.DMA] * SC_RING,
        ],
        compiler_params=_sc_params(),
        name="peer_dots_sc",
    )
    def k(tab_hbm, idx_hbm, h_hbm, out_hbm, idx_v, h_v, out_v, red_v, rows, sems):
        wid = _sc_worker_id()
        lane = lax.iota(I32, SC_LANES)

        def copy(u, slot):
            ids = idx_v.at[pl.ds(_sc_unit_off(u), SC_LANES)]
            return pltpu.make_async_copy(tab_hbm.at[ids], rows[slot], sems[slot])

        def compute(u, slot):
            tt = u // heads

            def body(c, accs):
                hv = h_v[tt, pl.ds(pl.multiple_of(c * SC_LANES, SC_LANES), SC_LANES)]
                return tuple(accs[r] + _sc_row_piece(rows[slot], r, c) * hv for r in range(SC_LANES))

            accs = lax.fori_loop(0, pieces, body,
                                 tuple(jnp.zeros((SC_LANES,), F32) for _ in range(SC_LANES)))
            for r in range(SC_LANES):
                red_v[pl.ds(r * SC_LANES, SC_LANES)] = accs[r]
            cols = [plsc.load_gather(red_v, [lane * SC_LANES + j]) for j in range(SC_LANES)]
            while len(cols) > 1:
                cols = [cols[i] + cols[i + 1] for i in range(0, len(cols), 2)]
            out_v[pl.ds(_sc_unit_off(u), SC_LANES)] = cols[0]

        @pl.loop(0, groups)
        def _(gi):
            base = wid * tpw + gi * g
            pltpu.sync_copy(idx_hbm.at[pl.ds(base * nsel, g * nsel)], idx_v)
            pltpu.sync_copy(h_hbm.at[pl.ds(base, g)], h_v)
            _sc_ring(units, lambda u, s: copy(u, s).start(), lambda u, s: copy(u, s).wait(), compute)
            pltpu.sync_copy(out_v, out_hbm.at[pl.ds(base * nsel, g * nsel)])

    return k(table, idx_flat, h)


def peer_combine_sc(table, idx_flat, coef_flat, t):
    d = table.shape[1] * table.shape[2]
    nsel = PEER_SEL
    tpw = t // SC_WORKERS
    g = SC_GROUP
    groups = tpw // g
    heads = nsel // SC_LANES
    pieces = d // SC_LANES
    units = g * heads
    row_buf = pltpu.VMEM((SC_LANES, SC_ROW_SUB, SC_ROW_LANE), F32)

    @functools.partial(
        pl.kernel, mesh=_sc_mesh(),
        out_type=jax.ShapeDtypeStruct((t, d), F32),
        scratch_types=[
            pltpu.VMEM((g * nsel,), I32),
            pltpu.VMEM((g * nsel,), F32),
            pltpu.VMEM((g, d), F32),
            [row_buf] * SC_RING,
            [pltpu.SemaphoreType.DMA] * SC_RING,
        ],
        compiler_params=_sc_params(),
        name="peer_combine_sc",
    )
    def k(tab_hbm, idx_hbm, coef_hbm, out_hbm, idx_v, coef_v, y_v, rows, sems):
        wid = _sc_worker_id()

        def copy(u, slot):
            ids = idx_v.at[pl.ds(_sc_unit_off(u), SC_LANES)]
            return pltpu.make_async_copy(tab_hbm.at[ids], rows[slot], sems[slot])

        def compute(u, slot):
            tt = u // heads
            first = (u % heads) == 0
            cs = [plsc.load_gather(coef_v, [jnp.full((SC_LANES,), u * SC_LANES + r, I32)])
                  for r in range(SC_LANES)]

            @plsc.parallel_loop(0, pieces, unroll=2)
            def _(c):
                off = pl.multiple_of(c * SC_LANES, SC_LANES)
                terms = [cs[r] * _sc_row_piece(rows[slot], r, c) for r in range(SC_LANES)]
                while len(terms) > 1:
                    terms = [terms[i] + terms[i + 1] for i in range(0, len(terms), 2)]
                prev = y_v[tt, pl.ds(off, SC_LANES)]
                y_v[tt, pl.ds(off, SC_LANES)] = terms[0] + jnp.where(first, 0.0, prev)

        @pl.loop(0, groups)
        def _(gi):
            base = wid * tpw + gi * g
            pltpu.sync_copy(idx_hbm.at[pl.ds(base * nsel, g * nsel)], idx_v)
            pltpu.sync_copy(coef_hbm.at[pl.ds(base * nsel, g * nsel)], coef_v)
            _sc_ring(units, lambda u, s: copy(u, s).start(), lambda u, s: copy(u, s).wait(), compute)
            pltpu.sync_copy(y_v, out_hbm.at[pl.ds(base, g)])

    return k(table, idx_flat, coef_flat)


GELU_C0 = math.sqrt(2.0 / math.pi)
GELU_C1 = 0.044715


def _gelu_tanh(x):
    z = GELU_C0 * (x + GELU_C1 * (x * x * x))
    th = 1.0 - 2.0 / (jnp.exp(2.0 * z) + 1.0)
    return 0.5 * x * (1.0 + th)


def peer_experts_sc(tab_u, tab_v, idx_flat, w_flat, h):
    t, d = h.shape
    nsel = PEER_SEL
    tpw = t // SC_WORKERS
    g = SC_GROUP
    groups = tpw // g
    heads = nsel // SC_LANES
    pieces = d // SC_LANES
    units = g * heads
    row_buf = pltpu.VMEM((SC_LANES, SC_ROW_SUB, SC_ROW_LANE), F32)

    @functools.partial(
        pl.kernel, mesh=_sc_mesh(),
        out_type=jax.ShapeDtypeStruct((t, d), F32),
        scratch_types=[
            pltpu.VMEM((g * nsel,), I32),
            pltpu.VMEM((g * nsel,), F32),
            pltpu.VMEM((g, d), F32),
            pltpu.VMEM((g, d), F32),
            pltpu.VMEM((SC_LANES * SC_LANES,), F32),
            [row_buf] * SC_RING,
            [pltpu.SemaphoreType.DMA] * SC_RING,
        ],
        compiler_params=_sc_params(),
        name="peer_experts_sc",
    )
    def k(u_hbm, v_hbm, idx_hbm, w_hbm, h_hbm, out_hbm, idx_v, coef_v, h_v, y_v, red_v, rows, sems):
        wid = _sc_worker_id()
        lane = lax.iota(I32, SC_LANES)

        def copy(tab_hbm, u, slot):
            ids = idx_v.at[pl.ds(_sc_unit_off(u), SC_LANES)]
            return pltpu.make_async_copy(tab_hbm.at[ids], rows[slot], sems[slot])

        def dots(u, slot):
            tt = u // heads

            def body(c, accs):
                hv = h_v[tt, pl.ds(pl.multiple_of(c * SC_LANES, SC_LANES), SC_LANES)]
                return tuple(accs[r] + _sc_row_piece(rows[slot], r, c) * hv for r in range(SC_LANES))

            accs = lax.fori_loop(0, pieces, body,
                                 tuple(jnp.zeros((SC_LANES,), F32) for _ in range(SC_LANES)))
            for r in range(SC_LANES):
                red_v[pl.ds(r * SC_LANES, SC_LANES)] = accs[r]
            cols = [plsc.load_gather(red_v, [lane * SC_LANES + j]) for j in range(SC_LANES)]
            while len(cols) > 1:
                cols = [cols[i] + cols[i + 1] for i in range(0, len(cols), 2)]
            sl = pl.ds(_sc_unit_off(u), SC_LANES)
            coef_v[sl] = coef_v[sl] * _gelu_tanh(cols[0])

        def combine(u, slot):
            tt = u // heads
            first = (u % heads) == 0
            cs = [plsc.load_gather(coef_v, [jnp.full((SC_LANES,), u * SC_LANES + r, I32)])
                  for r in range(SC_LANES)]

            @plsc.parallel_loop(0, pieces, unroll=2)
            def _(c):
                off = pl.multiple_of(c * SC_LANES, SC_LANES)
                terms = [cs[r] * _sc_row_piece(rows[slot], r, c) for r in range(SC_LANES)]
                while len(terms) > 1:
                    terms = [terms[i] + terms[i + 1] for i in range(0, len(terms), 2)]
                prev = y_v[tt, pl.ds(off, SC_LANES)]
                y_v[tt, pl.ds(off, SC_LANES)] = terms[0] + jnp.where(first, 0.0, prev)

        @pl.loop(0, groups)
        def _(gi):
            base = wid * tpw + gi * g
            pltpu.sync_copy(idx_hbm.at[pl.ds(base * nsel, g * nsel)], idx_v)
            pltpu.sync_copy(w_hbm.at[pl.ds(base * nsel, g * nsel)], coef_v)
            pltpu.sync_copy(h_hbm.at[pl.ds(base, g)], h_v)
            _sc_ring(units, lambda u, s: copy(u_hbm, u, s).start(), lambda u, s: copy(u_hbm, u, s).wait(), dots)
            _sc_ring(units, lambda u, s: copy(v_hbm, u, s).start(), lambda u, s: copy(v_hbm, u, s).wait(), combine)
            pltpu.sync_copy(y_v, out_hbm.at[pl.ds(base, g)])

    return k(tab_u, tab_v, idx_flat, w_flat, h)


SC_PK_RING = 6
SC_PK_SUB = 4
HI_MASK = -65536


def pack_bf16_pairs(a):
    half = a.shape[1] // 2
    bits = lax.bitcast_convert_type(a.astype(BF16), jnp.uint16).astype(jnp.uint32)
    return lax.bitcast_convert_type(bits[:, :half] | (bits[:, half:] << 16), I32)


def _unpack_halves(x32):
    w = plsc.bitcast(x32, I32)
    return plsc.bitcast(w << 16, F32), plsc.bitcast(w & HI_MASK, F32)


def _tree_sum(xs):
    while len(xs) > 1:
        xs = [xs[i] + xs[i + 1] for i in range(0, len(xs), 2)]
    return xs[0]


def peer_experts_pk_sc(tab_uv, idx_flat, w_flat, hp, d):
    t = hp.shape[0]
    nsel = PEER_SEL
    tpw = t // SC_WORKERS
    g = SC_GROUP
    groups = tpw // g
    heads = nsel // SC_LANES
    chunks = d // 32
    units = g * heads
    ring = SC_PK_RING
    row_buf = pltpu.VMEM((SC_LANES, 2 * SC_PK_SUB, SC_ROW_LANE), I32)

    def row_words(rows, r, wc, sub0):
        per = SC_ROW_LANE // SC_LANES
        return plsc.bitcast(
            rows[r, sub0 + wc // per, pl.ds(pl.multiple_of((wc % per) * SC_LANES, SC_LANES), SC_LANES)], BF16)

    def ring_loop(n_units, start, wait, compute):
        for u in range(ring - 1):
            start(u, u)

        @pl.loop(0, -(-n_units // ring) * ring, step=ring)
        def _(uu):
            for b in range(ring):
                u = uu + b
                nxt = u + (ring - 1)

                @pl.when(nxt < n_units)
                def _():
                    start(nxt, (b + ring - 1) % ring)

                @pl.when(u < n_units)
                def _():
                    wait(u, b)
                    compute(u, b)

    @functools.partial(
        pl.kernel, mesh=_sc_mesh(),
        out_type=jax.ShapeDtypeStruct((t, d), F32),
        scratch_types=[
            pltpu.VMEM((g * nsel,), I32),
            pltpu.VMEM((g * nsel,), F32),
            pltpu.VMEM((g, d // 2), I32),
            pltpu.VMEM((g, d), F32),
            pltpu.VMEM((SC_LANES * SC_LANES,), F32),
            [row_buf] * ring,
            [pltpu.SemaphoreType.DMA] * ring,
        ],
        compiler_params=_sc_params(),
        name="peer_experts_pk_sc",
    )
    def k(tab_hbm, idx_hbm, w_hbm, h_hbm, out_hbm, idx_v, coef_v, h_v, y_v, red_v, rows, sems):
        wid = _sc_worker_id()
        lane = lax.iota(I32, SC_LANES)

        def copy(u, slot):
            ids = idx_v.at[pl.ds(_sc_unit_off(u), SC_LANES)]
            return pltpu.make_async_copy(tab_hbm.at[ids], rows[slot], sems[slot])

        def dots(u, slot):
            tt = u // heads

            def body(cp, accs):
                out = []
                hv = [plsc.bitcast(h_v[tt, pl.ds(pl.multiple_of((2 * cp + i) * SC_LANES, SC_LANES), SC_LANES)], BF16)
                      for i in range(2)]
                for r in range(SC_LANES):
                    pr = (row_words(rows[slot], r, 2 * cp, 0) * hv[0]
                          + row_words(rows[slot], r, 2 * cp + 1, 0) * hv[1])
                    lo, hi = _unpack_halves(pr)
                    out.append(accs[r] + lo + hi)
                return tuple(out)

            accs = lax.fori_loop(0, chunks // 2, body,
                                 tuple(jnp.zeros((SC_LANES,), F32) for _ in range(SC_LANES)))
            for r in range(SC_LANES):
                red_v[pl.ds(r * SC_LANES, SC_LANES)] = accs[r]
            act = _tree_sum([plsc.load_gather(red_v, [lane * SC_LANES + j]) for j in range(SC_LANES)])
            sl = pl.ds(_sc_unit_off(u), SC_LANES)
            coef_v[sl] = coef_v[sl] * _gelu_tanh(act)

        def combine(u, slot):
            tt = u // heads
            first = (u % heads) == 0
            cb = []
            for r in range(SC_LANES):
                c = plsc.load_gather(coef_v, [jnp.full((SC_LANES,), u * SC_LANES + r, I32)])
                cb.append(plsc.pack(c, c, format=plsc.PackFormat.INTERLEAVED))

            @plsc.parallel_loop(0, chunks, unroll=2)
            def _(wc):
                lo, hi = _unpack_halves(
                    _tree_sum([cb[r] * row_words(rows[slot], r, wc, SC_PK_SUB) for r in range(SC_LANES)]))
                for half, val in ((0, lo), (1, hi)):
                    sl = pl.ds(pl.multiple_of(half * (d // 2) + wc * SC_LANES, SC_LANES), SC_LANES)
                    y_v[tt, sl] = val + jnp.where(first, 0.0, y_v[tt, sl])

        def unit(u, slot):
            dots(u, slot)
            combine(u, slot)

        @pl.loop(0, groups)
        def _(gi):
            base = wid * tpw + gi * g
            pltpu.sync_copy(idx_hbm.at[pl.ds(base * nsel, g * nsel)], idx_v)
            pltpu.sync_copy(w_hbm.at[pl.ds(base * nsel, g * nsel)], coef_v)
            pltpu.sync_copy(h_hbm.at[pl.ds(base, g)], h_v)
            ring_loop(units, lambda u, s: copy(u, s).start(), lambda u, s: copy(u, s).wait(), unit)
            pltpu.sync_copy(y_v, out_hbm.at[pl.ds(base, g)])

    return k(tab_uv, idx_flat, w_flat, hp)


def kernel(x, mem, rel_bias, ln_mix, w_in, hg_lower, hg_norm, w_up_a, w_up_b, w_out, ln_cross, ln_mem, wq_x, wk_x, wv_x, wo_x, ln_ffn, peer_query, peer_subkeys, peer_u, peer_v, ln_final):
    b, s, d = x.shape
    depth = w_in.shape[0]
    assert depth == 1, "the residual after PEER is fused into the final norm"
    assert s % MB_BLOCK == 0 and s % HG_CHUNK == 0 and s % (PEER_SLICES * SC_WORKERS * SC_GROUP) == 0
    nb = s // MB_BLOCK
    row = lambda a: a.reshape(1, -1).astype(F32)
    lb_all = jnp.cumsum(jax.nn.softmax(hg_lower.astype(F32), axis=0), axis=0)
    bias = moba_bias_tiles(rel_bias)
    n_hg = 4 * HG_WIDTH
    n_qk = 2 * MB_WIDTH
    n_mb = 3 * MB_WIDTH
    l = 0
    w = w_in[l].astype(BF16)
    w_hg, w_qk, w_vt, w_g = w[:, :n_hg], w[:, n_hg:n_hg + n_qk], w[:, n_hg + n_qk:n_hg + n_mb].T, w[:, n_hg + n_mb:]
    wa, wb, wo = w_up_a[l].astype(BF16), w_up_b[l].astype(BF16), w_out[l].astype(BF16)
    wqx, wox = wq_x[l].astype(BF16), wo_x[l].astype(BF16)
    wpq, sk = peer_query[l].astype(BF16), peer_subkeys[l].astype(F32)
    tab3 = lambda a: pack_bf16_pairs(a.astype(F32)).reshape(a.shape[0], SC_PK_SUB, SC_ROW_LANE)
    tab_uv = jnp.concatenate([tab3(peer_u[l]), tab3(peer_v[l])], axis=1)
    kx, vx = mem_kv(mem, row(ln_mem[l]), wk_x[l].astype(BF16), wv_x[l].astype(BF16))

    outs = []
    for bi in range(b):
        x2d = x[bi]
        p0, pqk, vt, pg = in_proj(x2d, row(ln_mix[l]), w_hg, w_qk, w_vt, w_g)
        ya = hgrn2(p0, row(lb_all[l]), row(hg_norm[l]), 1, s)
        km = moba_kmean(pqk, 1, s).reshape(1, nb, MB_WIDTH)
        ts = s // PEER_SLICES
        for tok0 in range(0, s, ts):
            yb = moba_attention(pqk, vt, km, bias, 1, s, tok0 // MB_BLOCK, ts // MB_BLOCK)
            xs = mix_out(x2d, ya, yb, pg, wa, wb, wo, tok0)
            xs = cross_attn(xs, row(ln_cross[l]), wqx, kx[bi:bi + 1], vx[bi:bi + 1], wox, ts)
            hp, eidx, wts = peer_route(xs, row(ln_ffn[l]), wpq, sk, 0, ts)
            y = peer_experts_pk_sc(tab_uv, eidx.reshape(ts * PEER_SEL), wts.reshape(ts * PEER_SEL), hp, d)
            outs.append(final_norm(xs, y, row(ln_final), 0))
    return jnp.concatenate(outs, axis=0).reshape(b, s, d)
```

```python
import functools
import math

import jax
import jax.numpy as jnp
import numpy as np
from jax import lax
from jax.experimental import pallas as pl
from jax.experimental.pallas import tpu as pltpu
from jax.experimental.pallas import tpu_sc as plsc

F32 = jnp.float32
BF16 = jnp.bfloat16
I32 = jnp.int32
EPS = 1e-6
NEG_INF = float("-inf")

HG_HEADS = 4
HG_D = 128
HG_WIDTH = HG_HEADS * HG_D
HG_CHUNK = 64
HG_SUB = 16
MB_HEADS = 8
MB_DH = 64
MB_WIDTH = MB_HEADS * MB_DH
MB_BLOCK = 256
MB_TOPK = 3
MB_BIAS_TILES = 8
REL_BUCKETS = 32
REL_MAX_DIST = 2048
X_HEADS = 4
PEER_HEADS = 8
PEER_NKEYS = 128
PEER_TOPK = 16
PEER_HALF = 128
PEER_SEL = PEER_HEADS * PEER_TOPK
PEER_SLICES = 4

VMEM_LIMIT = 56 * 1024 * 1024


def _cparams(sem):
    return pltpu.CompilerParams(dimension_semantics=sem, vmem_limit_bytes=VMEM_LIMIT)


def _rms(x, g):
    ms = jnp.mean(x * x, axis=-1, keepdims=True)
    return x * lax.rsqrt(ms + EPS) * g


def _in_proj_kernel(x_ref, g_ref, w0_ref, w1_ref, wvt_ref, w2_ref, o0_ref, o1_ref, ovt_ref, o2_ref):
    h = _rms(x_ref[...], g_ref[...]).astype(BF16)
    o0_ref[...] = jnp.dot(h, w0_ref[...], preferred_element_type=F32)
    o1_ref[...] = jnp.dot(h, w1_ref[...], preferred_element_type=F32).astype(BF16)
    vt = lax.dot_general(wvt_ref[...], h, (((1,), (1,)), ((), ())), preferred_element_type=F32).astype(BF16)
    for hd in range(MB_HEADS):
        ovt_ref[0, hd * MB_VROWS:hd * MB_VROWS + MB_DH, :] = vt[hd * MB_DH:(hd + 1) * MB_DH]
        ovt_ref[0, hd * MB_VROWS + MB_DH:(hd + 1) * MB_VROWS, :] = jnp.ones((MB_ONES, vt.shape[1]), BF16)
    o2_ref[...] = jnp.dot(h, w2_ref[...], preferred_element_type=F32).astype(BF16)


def in_proj(x2d, g, w0, w1, wvt, w2):
    t, d = x2d.shape
    tm = MB_BLOCK
    assert wvt.shape[0] == MB_WIDTH
    n0, n1, nv, n2 = w0.shape[1], w1.shape[1], MB_VT_ROWS, w2.shape[1]
    full = lambda a: pl.BlockSpec(a.shape, lambda i: (0, 0))
    return pl.pallas_call(
        _in_proj_kernel,
        grid=(t // tm,),
        in_specs=[pl.BlockSpec((tm, d), lambda i: (i, 0)), full(g), full(w0), full(w1), full(wvt), full(w2)],
        out_specs=[pl.BlockSpec((tm, n0), lambda i: (i, 0)),
                   pl.BlockSpec((tm, n1), lambda i: (i, 0)),
                   pl.BlockSpec((1, nv, tm), lambda i: (i, 0, 0)),
                   pl.BlockSpec((tm, n2), lambda i: (i, 0))],
        out_shape=[jax.ShapeDtypeStruct((t, n0), F32),
                   jax.ShapeDtypeStruct((t, n1), BF16),
                   jax.ShapeDtypeStruct((t // tm, nv, tm), BF16),
                   jax.ShapeDtypeStruct((t, n2), BF16)],
        compiler_params=_cparams(("parallel",)),
        name="in_proj",
    )(x2d, g, w0, w1, wvt, w2)


def _hgrn_kernel(q_ref, f_ref, i_ref, g_ref, lb_ref, gain_ref, o_ref, st_ref):
    c = pl.program_id(1)

    @pl.when(c == 0)
    def _():
        st_ref[...] = jnp.zeros_like(st_ref)

    C, S = HG_CHUNK, HG_SUB
    row = lax.broadcasted_iota(I32, (C, C), 0)
    col = lax.broadcasted_iota(I32, (C, C), 1)
    tril = (row >= col).astype(F32)
    t_iota = lax.broadcasted_iota(I32, (S, 1), 0)

    for h in range(HG_HEADS):
        sl = slice(h * HG_D, (h + 1) * HG_D)
        q = q_ref[:, sl]
        v = i_ref[:, sl]
        lb = lb_ref[:, sl]
        f = lb + (1.0 - lb) * jax.nn.sigmoid(f_ref[:, sl])
        lf = jnp.log(f)
        k = 1.0 - f
        b = jnp.dot(tril, lf, precision=lax.Precision.HIGHEST, preferred_element_type=F32)
        st = st_ref[h]
        vb = v.astype(BF16)
        qd = (q * jnp.exp(b)).astype(BF16)
        o_inter = lax.dot_general(qd, st.astype(BF16), (((1,), (1,)), ((), ())),
                                  preferred_element_type=F32)
        outs = []
        for i in range(C // S):
            r0 = i * S
            qi = q[r0:r0 + S]
            ki = k[r0:r0 + S]
            bi = b[r0:r0 + S]
            vi = v[r0:r0 + S]
            oi = o_inter[r0:r0 + S]
            if i > 0:
                bs = b[r0 - 1:r0]
                qh = (qi * jnp.exp(bi - bs)).astype(BF16)
                kh = (k[:r0] * jnp.exp(bs - b[:r0])).astype(BF16)
                a = lax.dot_general(qh, kh, (((1,), (1,)), ((), ())), preferred_element_type=F32)
                oi = oi + jnp.dot(a.astype(BF16), vb[:r0], preferred_element_type=F32)
            half = S // 2
            o_half = [oi[:half], oi[half:]]
            for s in range(S):
                for hf in range(s // half, 2):
                    rows = slice(hf * half, (hf + 1) * half)
                    dec = jnp.exp(jnp.minimum(bi[rows] - bi[s:s + 1], 0.0))
                    a_s = jnp.sum(qi[rows] * ki[s:s + 1] * dec, axis=-1, keepdims=True)
                    a_s = jnp.where(t_iota[rows] >= s, a_s, 0.0)
                    o_half[hf] = o_half[hf] + a_s * vi[s:s + 1]
            outs.extend(o_half)
        o = jnp.concatenate(outs, axis=0)
        b_end = b[C - 1:C]
        kd = (k * jnp.exp(b_end - b)).astype(BF16)
        upd = lax.dot_general(vb, kd, (((0,), (0,)), ((), ())), preferred_element_type=F32)
        st_ref[h] = st * jnp.exp(b_end) + upd
        o = o * lax.rsqrt(jnp.mean(o * o, axis=-1, keepdims=True) + EPS)
        g = g_ref[:, sl]
        o_ref[:, sl] = (o * gain_ref[:, sl] * (g * jax.nn.sigmoid(g))).astype(o_ref.dtype)


def hgrn2(p0, lb, gain, batch, seq):
    t = p0.shape[0]
    nc = seq // HG_CHUNK
    w = HG_WIDTH

    def col(j):
        return pl.BlockSpec((HG_CHUNK, w), lambda b, c, j=j: (b * nc + c, j))

    return pl.pallas_call(
        _hgrn_kernel,
        grid=(batch, nc),
        in_specs=[col(0), col(1), col(2), col(3),
                  pl.BlockSpec((1, w), lambda b, c: (0, 0)),
                  pl.BlockSpec((1, w), lambda b, c: (0, 0))],
        out_specs=pl.BlockSpec((HG_CHUNK, w), lambda b, c: (b * nc + c, 0)),
        out_shape=jax.ShapeDtypeStruct((t, w), BF16),
        scratch_shapes=[pltpu.VMEM((HG_HEADS, HG_D, HG_D), F32)],
        compiler_params=_cparams(("parallel", "arbitrary")),
        name="hgrn2",
    )(p0, p0, p0, p0, lb, gain)


def _kmean_kernel(k_ref, o_ref):
    o_ref[0] = jnp.mean(k_ref[...].astype(F32), axis=0, keepdims=True)


def moba_kmean(p1, batch, seq):
    nbt = p1.shape[0] // MB_BLOCK
    return pl.pallas_call(
        _kmean_kernel,
        grid=(nbt,),
        in_specs=[pl.BlockSpec((MB_BLOCK, MB_WIDTH), lambda i: (i, 1))],
        out_specs=pl.BlockSpec((1, 1, MB_WIDTH), lambda i: (i, 0, 0)),
        out_shape=jax.ShapeDtypeStruct((nbt, 1, MB_WIDTH), F32),
        compiler_params=_cparams(("parallel",)),
        name="moba_kmean",
    )(p1)


MB_PAIR = 4
MB_PW = MB_PAIR * MB_DH
MB_LG = 128
MB_ONES = 16
MB_VROWS = MB_DH + MB_ONES
MB_VT_ROWS = MB_HEADS * MB_VROWS


def _moba_kernel(q_ref, k_ref, vt_ref, km_ref, bias_ref, o_ref, *scratch, qb0):
    m_ref, l_ref, al_ref, acc_ref, msk_ref, s_ref, p_ref = (
        scratch[i * MB_PAIR:(i + 1) * MB_PAIR] for i in range(7))
    qi = pl.program_id(2) + qb0
    nb = km_ref.shape[0]
    blk = MB_BLOCK
    heads = range(MB_PAIR)
    grp = lambda hh: slice((hh // 2) * MB_LG, (hh // 2 + 1) * MB_LG)
    q = q_ref[...]
    lane = lax.broadcasted_iota(I32, (blk, MB_LG), 1)
    in_head = [(lane < MB_DH) if hh % 2 == 0 else (lane >= MB_DH) for hh in heads]
    qs = q * jnp.asarray(MB_DH ** -0.5, BF16)
    nt = (((1,), (1,)), ((), ()))
    qf = q.astype(F32)
    qht = [jnp.where(in_head[hh], qs[:, grp(hh)].astype(F32), 0.0).T.astype(BF16) for hh in heads]

    n_io = lax.broadcasted_iota(I32, (nb, blk), 0)
    for hh in heads:
        gate = lax.dot_general(km_ref[:, grp(hh)], jnp.where(in_head[hh], qf[:, grp(hh)], 0.0), nt,
                               precision=lax.Precision.HIGHEST, preferred_element_type=F32)
        gate = jnp.where(n_io < qi, gate, NEG_INF)
        chosen = n_io < 0
        for _ in range(MB_TOPK):
            mx = jnp.max(gate, axis=0, keepdims=True)
            ix = jnp.min(jnp.where(gate == mx, n_io, nb), axis=0, keepdims=True)
            hit = n_io == ix
            chosen = chosen | (hit & (mx > NEG_INF))
            gate = jnp.where(hit, NEG_INF, gate)
        msk_ref[hh][...] = jnp.where(chosen, 0.0, NEG_INF)

    vrows = lambda hh: slice(hh * MB_VROWS, (hh + 1) * MB_VROWS)

    def pv_stage(blk_idx):
        vtb = vt_ref[blk_idx]
        r = [jnp.dot(vtb[vrows(hh)], p_ref[hh][...], preferred_element_type=F32) for hh in heads]
        al = [al_ref[hh][...] for hh in heads]
        a_new = [al[hh] * acc_ref[hh][...] + r[hh][:MB_DH] for hh in heads]
        l_new = [al[hh] * l_ref[hh][...] + r[hh][MB_DH:MB_DH + 1] for hh in heads]
        return a_new, l_new

    def store_pv(a_new, l_new):
        for hh in heads:
            acc_ref[hh][...] = a_new[hh]
            l_ref[hh][...] = l_new[hh]

    def softmax_stage():
        s = [s_ref[hh][...] for hh in heads]
        m_old = [m_ref[hh][...] for hh in heads]
        m_new = [jnp.maximum(m_old[hh], jnp.max(s[hh], axis=0, keepdims=True)) for hh in heads]
        alpha = [jnp.exp(m_old[hh] - m_new[hh]) for hh in heads]
        p = [jnp.exp((s[hh] - m_new[hh]).astype(BF16)) for hh in heads]
        return p, alpha, m_new

    def store_softmax(p, alpha, m_new):
        for hh in heads:
            p_ref[hh][...] = p[hh]
            al_ref[hh][...] = alpha[hh]
            m_ref[hh][...] = m_new[hh]

    k_own = k_ref[pl.ds(pl.multiple_of(qi * blk, blk), blk), :]
    key_io = lax.broadcasted_iota(I32, (blk, blk), 0)
    qry_io = lax.broadcasted_iota(I32, (blk, blk), 1)
    for hh in heads:
        s = jnp.dot(k_own[:, grp(hh)], qht[hh], preferred_element_type=F32) + bias_ref[hh, 0]
        s_ref[hh][...] = jnp.where(key_io <= qry_io, s, NEG_INF)
        m_ref[hh][...] = jnp.full((1, blk), NEG_INF, F32)
        l_ref[hh][...] = jnp.zeros((1, blk), F32)
        al_ref[hh][...] = jnp.ones((1, blk), F32)
        acc_ref[hh][...] = jnp.zeros((MB_DH, blk), F32)
        p_ref[hh][...] = jnp.zeros((blk, blk), BF16)

    def step(i, carry):
        pv = pv_stage(jnp.where(i <= 1, qi, i - 2))
        sm = softmax_stage()
        kn = k_ref[pl.ds(pl.multiple_of(i * blk, blk), blk), :]
        d = jnp.minimum(qi - i, MB_BIAS_TILES - 1)
        s_next = [jnp.dot(kn[:, grp(hh)], qht[hh], preferred_element_type=F32)
                  + bias_ref[hh, d] + msk_ref[hh][pl.ds(i, 1), :] for hh in heads]
        store_pv(*pv)
        for hh in heads:
            s_ref[hh][...] = s_next[hh]
        store_softmax(*sm)
        return carry

    lax.fori_loop(0, qi, step, 0)
    pv = pv_stage(jnp.where(qi <= 1, qi, qi - 2))
    sm = softmax_stage()
    store_pv(*pv)
    store_softmax(*sm)
    a_fin, l_fin = pv_stage(jnp.where(qi == 0, qi, qi - 1))
    out_t = jnp.concatenate([a_fin[hh] / l_fin[hh] for hh in heads], axis=0)
    o_ref[...] = out_t.T.astype(o_ref.dtype)


def moba_attention(pqk, vt, km, bias, batch, seq, qb0=0, nqb=None):
    nb = seq // MB_BLOCK
    nqb = nb if nqb is None else nqb
    t = batch * nqb * MB_BLOCK
    groups = MB_WIDTH // MB_PW
    return pl.pallas_call(
        functools.partial(_moba_kernel, qb0=qb0),
        grid=(batch, groups, nqb),
        in_specs=[
            pl.BlockSpec((MB_BLOCK, MB_PW), lambda b, j, i: (b * nb + qb0 + i, j)),
            pl.BlockSpec((seq, MB_PW), lambda b, j, i: (b, groups + j)),
            pl.BlockSpec((nb, MB_PAIR * MB_VROWS, MB_BLOCK), lambda b, j, i: (b, j, 0)),
            pl.BlockSpec((None, nb, MB_PW), lambda b, j, i: (b, 0, j)),
            pl.BlockSpec((MB_PAIR, MB_BIAS_TILES, MB_BLOCK, MB_BLOCK), lambda b, j, i: (j, 0, 0, 0)),
        ],
        out_specs=pl.BlockSpec((MB_BLOCK, MB_PW), lambda b, j, i: (b * nqb + i, j)),
        out_shape=jax.ShapeDtypeStruct((t, MB_WIDTH), BF16),
        scratch_shapes=(
            [pltpu.VMEM((1, MB_BLOCK), F32)] * (3 * MB_PAIR)
            + [pltpu.VMEM((MB_DH, MB_BLOCK), F32)] * MB_PAIR
            + [pltpu.VMEM((nb, MB_BLOCK), F32)] * MB_PAIR
            + [pltpu.VMEM((MB_BLOCK, MB_BLOCK), F32)] * MB_PAIR
            + [pltpu.VMEM((MB_BLOCK, MB_BLOCK), BF16)] * MB_PAIR
        ),
        compiler_params=_cparams(("parallel", "parallel", "arbitrary")),
        name="moba_attn",
    )(pqk, pqk, vt, km, bias)


def _t5_bucket(dist):
    max_exact = REL_BUCKETS // 2
    scaled = jnp.log(jnp.maximum(dist, 1).astype(F32) / max_exact) / math.log(REL_MAX_DIST / max_exact)
    large = jnp.minimum(max_exact + (scaled * (REL_BUCKETS - max_exact)).astype(I32), REL_BUCKETS - 1)
    return jnp.where(dist < max_exact, dist, large)


def moba_bias_tiles(rel_bias):
    blk = MB_BLOCK
    span = 2 * blk - 1
    x = jnp.arange(span) - (blk - 1)
    dist = jnp.maximum(jnp.arange(MB_BIAS_TILES)[:, None] * blk + x[None, :], 0)
    w = rel_bias.astype(F32).T[:, _t5_bucket(dist)]
    h = w.shape[0]
    wp = jnp.pad(w, ((0, 0), (0, 0), (0, 1)))
    a = jnp.broadcast_to(wp[:, :, None, :], (h, MB_BIAS_TILES, blk, span + 1))
    a = a.reshape(h, MB_BIAS_TILES, blk * (span + 1))[:, :, :blk * span]
    return a.reshape(h, MB_BIAS_TILES, blk, span)[:, :, :, blk - 1:]


def _mix_kernel(x_ref, ya_ref, yb_ref, ga_ref, gb_ref, wa_ref, wb_ref, wo_ref, o_ref):
    za = jnp.dot(ya_ref[...], wa_ref[...], preferred_element_type=F32)
    zb = jnp.dot(yb_ref[...], wb_ref[...], preferred_element_type=F32)
    z = jax.nn.sigmoid(ga_ref[...].astype(F32)) * za + jax.nn.sigmoid(gb_ref[...].astype(F32)) * zb
    o_ref[...] = x_ref[...] + jnp.dot(z.astype(BF16), wo_ref[...], preferred_element_type=F32)


def mix_out(x2d, ya, yb, pg, wa, wb, wo, tok0=0, tm=256):
    t = yb.shape[0]
    d = x2d.shape[1]
    w = ya.shape[1]
    b0 = tok0 // tm
    return pl.pallas_call(
        _mix_kernel,
        grid=(t // tm,),
        in_specs=[
            pl.BlockSpec((tm, d), lambda i: (b0 + i, 0)),
            pl.BlockSpec((tm, w), lambda i: (b0 + i, 0)),
            pl.BlockSpec((tm, w), lambda i: (i, 0)),
            pl.BlockSpec((tm, d), lambda i: (b0 + i, 0)),
            pl.BlockSpec((tm, d), lambda i: (b0 + i, 1)),
            pl.BlockSpec((w, d), lambda i: (0, 0)),
            pl.BlockSpec((w, d), lambda i: (0, 0)),
            pl.BlockSpec((d, d), lambda i: (0, 0)),
        ],
        out_specs=pl.BlockSpec((tm, d), lambda i: (i, 0)),
        out_shape=jax.ShapeDtypeStruct((t, d), F32),
        compiler_params=_cparams(("parallel",)),
        name="mix_out",
    )(x2d, ya, yb, pg, pg, wa, wb, wo)


def _mem_kv_kernel(m_ref, g_ref, wk_ref, wv_ref, k_ref, v_ref):
    mn = _rms(m_ref[...], g_ref[...]).astype(BF16)
    k_ref[...] = jnp.dot(mn, wk_ref[...], preferred_element_type=F32).astype(BF16)
    v_ref[...] = jnp.dot(mn, wv_ref[...], preferred_element_type=F32).astype(BF16)


def mem_kv(mem, g, wk, wv):
    b, m, d = mem.shape
    spec = pl.BlockSpec((None, m, d), lambda i: (i, 0, 0))
    wspec = pl.BlockSpec((d, d), lambda i: (0, 0))
    return pl.pallas_call(
        _mem_kv_kernel,
        grid=(b,),
        in_specs=[spec, pl.BlockSpec((1, d), lambda i: (0, 0)), wspec, wspec],
        out_specs=[spec, spec],
        out_shape=[jax.ShapeDtypeStruct((b, m, d), BF16)] * 2,
        compiler_params=_cparams(("parallel",)),
        name="mem_kv",
    )(mem, g, wk, wv)


def _cross_kernel(x_ref, g_ref, wq_ref, k_ref, v_ref, wo_ref, o_ref):
    x = x_ref[...]
    d = x.shape[1]
    dh = d // X_HEADS
    h = _rms(x, g_ref[...]).astype(BF16)
    q = (jnp.dot(h, wq_ref[...], preferred_element_type=F32) * (dh ** -0.5)).astype(BF16)
    outs = []
    for hh in range(X_HEADS):
        sl = slice(hh * dh, (hh + 1) * dh)
        s = lax.dot_general(q[:, sl], k_ref[:, sl], (((1,), (1,)), ((), ())),
                            preferred_element_type=F32)
        p = jnp.exp(s - jnp.max(s, axis=1, keepdims=True))
        l = jnp.sum(p, axis=1, keepdims=True)
        o = jnp.dot(p.astype(BF16), v_ref[:, sl], preferred_element_type=F32) / l
        outs.append(o.astype(BF16))
    o = jnp.concatenate(outs, axis=1)
    o_ref[...] = x + jnp.dot(o, wo_ref[...], preferred_element_type=F32)


def cross_attn(x2d, g, wq, kx, vx, wo, seq, tm=256):
    t, d = x2d.shape
    m = kx.shape[1]
    per_b = seq // tm
    kv = pl.BlockSpec((None, m, d), lambda i: (i // per_b, 0, 0))
    wspec = pl.BlockSpec((d, d), lambda i: (0, 0))
    return pl.pallas_call(
        _cross_kernel,
        grid=(t // tm,),
        in_specs=[pl.BlockSpec((tm, d), lambda i: (i, 0)), pl.BlockSpec((1, d), lambda i: (0, 0)),
                  wspec, kv, kv, wspec],
        out_specs=pl.BlockSpec((tm, d), lambda i: (i, 0)),
        out_shape=jax.ShapeDtypeStruct((t, d), F32),
        compiler_params=_cparams(("parallel",)),
        name="cross_attn",
    )(x2d, g, wq, kx, vx, wo)


def _topk_rows(sc, k):
    n = sc.shape[0]
    io = lax.broadcasted_iota(I32, sc.shape, 0)
    vals, ids = [], []
    for _ in range(k):
        m = jnp.max(sc, axis=0, keepdims=True)
        ix = jnp.min(jnp.where(sc == m, io, n), axis=0, keepdims=True)
        vals.append(m)
        ids.append(ix)
        sc = jnp.where(io == ix, NEG_INF, sc)
    return jnp.concatenate(vals, axis=0), jnp.concatenate(ids, axis=0)


def _pack_bf16_halves(h):
    bits = lax.bitcast_convert_type(h, I32)
    r = bits + 0x7FFF + (lax.shift_right_logical(bits, 16) & 1)
    half = h.shape[1] // 2
    return lax.shift_right_logical(r[:, :half], 16) | (r[:, half:] & HI_MASK)


def _route_kernel(x_ref, g_ref, wq_ref, sk_ref, hp_ref, idx_ref, w_ref, hb_ref, it_ref, wt_ref):
    p = pl.program_id(1)

    @pl.when(p == 0)
    def _():
        h = _rms(x_ref[...], g_ref[...])
        hp_ref[...] = _pack_bf16_halves(h)
        hb_ref[...] = h.astype(BF16)

    qh = jnp.dot(hb_ref[...], wq_ref[...], preferred_element_type=F32)
    tops = []
    for c in range(2):
        seg = qh[:, c * PEER_HALF:(c + 1) * PEER_HALF]
        sc = lax.dot_general(sk_ref[c], seg, (((1,), (1,)), ((), ())),
                             precision=lax.Precision.HIGHEST, preferred_element_type=F32)
        tops.append(_topk_rows(sc, PEER_TOPK))
    (s0, i0), (s1, i1) = tops
    k = PEER_TOPK
    sub = 8
    tm = s0.shape[1]
    r8 = lax.broadcasted_iota(I32, (sub, tm), 0)
    r16 = lax.broadcasted_iota(I32, (k, tm), 0)
    cand_b = [s0[0:1] + s1, s0[1:2] + s1[:sub]]
    cidx_b = [i0[0:1] * PEER_NKEYS + i1, i0[1:2] * PEER_NKEYS + i1[:sub]]
    pos_b = [r16, k + r8]
    for a in range(2, sub):
        keep = r8 < (k // (a + 1))
        cand_b.append(jnp.where(keep, s0[a:a + 1] + s1[:sub], NEG_INF))
        cidx_b.append(i0[a:a + 1] * PEER_NKEYS + i1[:sub])
        pos_b.append(a * k + r8)
    cand_b.append(s0[sub:] + s1[0:1])
    cidx_b.append(i0[sub:] * PEER_NKEYS + i1[0:1])
    pos_b.append((sub + r8) * k)
    cand = jnp.concatenate(cand_b, axis=0)
    cidx = jnp.concatenate(cidx_b, axis=0)
    pos = jnp.concatenate(pos_b, axis=0)
    vals, ids = [], []
    for _ in range(k):
        m = jnp.max(cand, axis=0, keepdims=True)
        px = jnp.min(jnp.where(cand == m, pos, k * k), axis=0, keepdims=True)
        hit = pos == px
        vals.append(m)
        ids.append(jnp.sum(jnp.where(hit, cidx, 0), axis=0, keepdims=True))
        cand = jnp.where(hit, NEG_INF, cand)
    sf = jnp.concatenate(vals, axis=0)
    e = jnp.exp(sf - sf[0:1])
    rows = pl.ds(pl.multiple_of(p * PEER_TOPK, PEER_TOPK), PEER_TOPK)
    wt_ref[rows, :] = e / jnp.sum(e, axis=0, keepdims=True)
    it_ref[rows, :] = jnp.concatenate(ids, axis=0)

    @pl.when(p == pl.num_programs(1) - 1)
    def _():
        idx_ref[...] = it_ref[...].T
        w_ref[...] = wt_ref[...].T


def peer_route(x2d, g, wq, sk, tok0, t, tm=512):
    d = x2d.shape[1]
    ph = sk.shape[0]
    nsel = ph * PEER_TOPK
    blk0 = tok0 // tm
    return pl.pallas_call(
        _route_kernel,
        grid=(t // tm, ph),
        in_specs=[
            pl.BlockSpec((tm, d), lambda i, p: (blk0 + i, 0)),
            pl.BlockSpec((1, d), lambda i, p: (0, 0)),
            pl.BlockSpec((d, 2 * PEER_HALF), lambda i, p: (0, p)),
            pl.BlockSpec((None, 2, PEER_NKEYS, PEER_HALF), lambda i, p: (p, 0, 0, 0)),
        ],
        out_specs=[
            pl.BlockSpec((tm, d // 2), lambda i, p: (i, 0)),
            pl.BlockSpec((tm, nsel), lambda i, p: (i, 0)),
            pl.BlockSpec((tm, nsel), lambda i, p: (i, 0)),
        ],
        out_shape=[jax.ShapeDtypeStruct((t, d // 2), I32),
                   jax.ShapeDtypeStruct((t, nsel), I32),
                   jax.ShapeDtypeStruct((t, nsel), F32)],
        scratch_shapes=[pltpu.VMEM((tm, d), BF16),
                        pltpu.VMEM((nsel, tm), I32),
                        pltpu.VMEM((nsel, tm), F32)],
        compiler_params=_cparams(("parallel", "arbitrary")),
        name="peer_route",
    )(x2d, g, wq, sk)


def _coef_kernel(w_ref, a_ref, o_ref):
    o_ref[...] = w_ref[...] * jax.nn.gelu(a_ref[...])


def peer_coef(w, act, tm=1024):
    t, n = w.shape
    spec = pl.BlockSpec((tm, n), lambda i: (i, 0))
    return pl.pallas_call(
        _coef_kernel, grid=(t // tm,), in_specs=[spec, spec], out_specs=spec,
        out_shape=jax.ShapeDtypeStruct((t, n), F32),
        compiler_params=_cparams(("parallel",)), name="peer_coef",
    )(w, act)


def _final_kernel(x_ref, y_ref, g_ref, o_ref):
    o_ref[...] = _rms(x_ref[...] + y_ref[...], g_ref[...])


def final_norm(x2d, y, g, tok0, tm=512):
    t, d = y.shape
    blk0 = tok0 // tm
    spec = pl.BlockSpec((tm, d), lambda i: (i, 0))
    return pl.pallas_call(
        _final_kernel, grid=(t // tm,),
        in_specs=[pl.BlockSpec((tm, d), lambda i: (blk0 + i, 0)), spec, pl.BlockSpec((1, d), lambda i: (0, 0))],
        out_specs=spec,
        out_shape=jax.ShapeDtypeStruct((t, d), F32),
        compiler_params=_cparams(("parallel",)), name="final_norm",
    )(x2d, y, g)


SC_CORES = 2
SC_SUBCORES = 16
SC_WORKERS = SC_CORES * SC_SUBCORES
SC_LANES = 16
SC_GROUP = 16


def _sc_mesh():
    return plsc.VectorSubcoreMesh(core_axis_name="c", subcore_axis_name="s")


def _sc_params():
    return pltpu.CompilerParams(needs_layout_passes=False)


def _sc_worker_id():
    return lax.axis_index("s") * SC_CORES + lax.axis_index("c")


SC_RING = 4
SC_ROW_SUB = 8
SC_ROW_LANE = 128


def _sc_ring(n_units, start, wait, compute):
    for u in range(SC_RING - 1):
        start(u, u)

    @pl.loop(0, n_units, step=SC_RING)
    def _(uu):
        for b in range(SC_RING):
            u = uu + b
            nxt = u + (SC_RING - 1)

            @pl.when(nxt < n_units)
            def _():
                start(nxt, (b + SC_RING - 1) % SC_RING)

            wait(u, b)
            compute(u, b)


def _sc_unit_off(u):
    off = u * SC_LANES
    return off if isinstance(off, int) else pl.multiple_of(off, SC_LANES)


def _sc_row_piece(rows, r, c):
    per = SC_ROW_LANE // SC_LANES
    return rows[r, c // per, pl.ds(pl.multiple_of((c % per) * SC_LANES, SC_LANES), SC_LANES)]


def peer_dots_sc(table, idx_flat, h):
    t, d = h.shape
    nsel = PEER_SEL
    tpw = t // SC_WORKERS
    g = SC_GROUP
    groups = tpw // g
    heads = nsel // SC_LANES
    pieces = d // SC_LANES
    units = g * heads
    row_buf = pltpu.VMEM((SC_LANES, SC_ROW_SUB, SC_ROW_LANE), F32)

    @functools.partial(
        pl.kernel, mesh=_sc_mesh(),
        out_type=jax.ShapeDtypeStruct((t * nsel,), F32),
        scratch_types=[
            pltpu.VMEM((g * nsel,), I32),
            pltpu.VMEM((g, d), F32),
            pltpu.VMEM((g * nsel,), F32),
            pltpu.VMEM((SC_LANES * SC_LANES,), F32),
            [row_buf] * SC_RING,
            [pltpu.SemaphoreType.DMA] * SC_RING,
        ],
        compiler_params=_sc_params(),
        name="peer_dots_sc",
    )
    def k(tab_hbm, idx_hbm, h_hbm, out_hbm, idx_v, h_v, out_v, red_v, rows, sems):
        wid = _sc_worker_id()
        lane = lax.iota(I32, SC_LANES)

        def copy(u, slot):
            ids = idx_v.at[pl.ds(_sc_unit_off(u), SC_LANES)]
            return pltpu.make_async_copy(tab_hbm.at[ids], rows[slot], sems[slot])

        def compute(u, slot):
            tt = u // heads

            def body(c, accs):
                hv = h_v[tt, pl.ds(pl.multiple_of(c * SC_LANES, SC_LANES), SC_LANES)]
                return tuple(accs[r] + _sc_row_piece(rows[slot], r, c) * hv for r in range(SC_LANES))

            accs = lax.fori_loop(0, pieces, body,
                                 tuple(jnp.zeros((SC_LANES,), F32) for _ in range(SC_LANES)))
            for r in range(SC_LANES):
                red_v[pl.ds(r * SC_LANES, SC_LANES)] = accs[r]
            cols = [plsc.load_gather(red_v, [lane * SC_LANES + j]) for j in range(SC_LANES)]
            while len(cols) > 1:
                cols = [cols[i] + cols[i + 1] for i in range(0, len(cols), 2)]
            out_v[pl.ds(_sc_unit_off(u), SC_LANES)] = cols[0]

        @pl.loop(0, groups)
        def _(gi):
            base = wid * tpw + gi * g
            pltpu.sync_copy(idx_hbm.at[pl.ds(base * nsel, g * nsel)], idx_v)
            pltpu.sync_copy(h_hbm.at[pl.ds(base, g)], h_v)
            _sc_ring(units, lambda u, s: copy(u, s).start(), lambda u, s: copy(u, s).wait(), compute)
            pltpu.sync_copy(out_v, out_hbm.at[pl.ds(base * nsel, g * nsel)])

    return k(table, idx_flat, h)


def peer_combine_sc(table, idx_flat, coef_flat, t):
    d = table.shape[1] * table.shape[2]
    nsel = PEER_SEL
    tpw = t // SC_WORKERS
    g = SC_GROUP
    groups = tpw // g
    heads = nsel // SC_LANES
    pieces = d // SC_LANES
    units = g * heads
    row_buf = pltpu.VMEM((SC_LANES, SC_ROW_SUB, SC_ROW_LANE), F32)

    @functools.partial(
        pl.kernel, mesh=_sc_mesh(),
        out_type=jax.ShapeDtypeStruct((t, d), F32),
        scratch_types=[
            pltpu.VMEM((g * nsel,), I32),
            pltpu.VMEM((g * nsel,), F32),
            pltpu.VMEM((g, d), F32),
            [row_buf] * SC_RING,
            [pltpu.SemaphoreType.DMA] * SC_RING,
        ],
        compiler_params=_sc_params(),
        name="peer_combine_sc",
    )
    def k(tab_hbm, idx_hbm, coef_hbm, out_hbm, idx_v, coef_v, y_v, rows, sems):
        wid = _sc_worker_id()

        def copy(u, slot):
            ids = idx_v.at[pl.ds(_sc_unit_off(u), SC_LANES)]
            return pltpu.make_async_copy(tab_hbm.at[ids], rows[slot], sems[slot])

        def compute(u, slot):
            tt = u // heads
            first = (u % heads) == 0
            cs = [plsc.load_gather(coef_v, [jnp.full((SC_LANES,), u * SC_LANES + r, I32)])
                  for r in range(SC_LANES)]

            @plsc.parallel_loop(0, pieces, unroll=2)
            def _(c):
                off = pl.multiple_of(c * SC_LANES, SC_LANES)
                terms = [cs[r] * _sc_row_piece(rows[slot], r, c) for r in range(SC_LANES)]
                while len(terms) > 1:
                    terms = [terms[i] + terms[i + 1] for i in range(0, len(terms), 2)]
                prev = y_v[tt, pl.ds(off, SC_LANES)]
                y_v[tt, pl.ds(off, SC_LANES)] = terms[0] + jnp.where(first, 0.0, prev)

        @pl.loop(0, groups)
        def _(gi):
            base = wid * tpw + gi * g
            pltpu.sync_copy(idx_hbm.at[pl.ds(base * nsel, g * nsel)], idx_v)
            pltpu.sync_copy(coef_hbm.at[pl.ds(base * nsel, g * nsel)], coef_v)
            _sc_ring(units, lambda u, s: copy(u, s).start(), lambda u, s: copy(u, s).wait(), compute)
            pltpu.sync_copy(y_v, out_hbm.at[pl.ds(base, g)])

    return k(table, idx_flat, coef_flat)


GELU_C0 = math.sqrt(2.0 / math.pi)
GELU_C1 = 0.044715


def _gelu_tanh(x):
    z = GELU_C0 * (x + GELU_C1 * (x * x * x))
    th = 1.0 - 2.0 / (jnp.exp(2.0 * z) + 1.0)
    return 0.5 * x * (1.0 + th)


def peer_experts_sc(tab_u, tab_v, idx_flat, w_flat, h):
    t, d = h.shape
    nsel = PEER_SEL
    tpw = t // SC_WORKERS
    g = SC_GROUP
    groups = tpw // g
    heads = nsel // SC_LANES
    pieces = d // SC_LANES
    units = g * heads
    row_buf = pltpu.VMEM((SC_LANES, SC_ROW_SUB, SC_ROW_LANE), F32)

    @functools.partial(
        pl.kernel, mesh=_sc_mesh(),
        out_type=jax.ShapeDtypeStruct((t, d), F32),
        scratch_types=[
            pltpu.VMEM((g * nsel,), I32),
            pltpu.VMEM((g * nsel,), F32),
            pltpu.VMEM((g, d), F32),
            pltpu.VMEM((g, d), F32),
            pltpu.VMEM((SC_LANES * SC_LANES,), F32),
            [row_buf] * SC_RING,
            [pltpu.SemaphoreType.DMA] * SC_RING,
        ],
        compiler_params=_sc_params(),
        name="peer_experts_sc",
    )
    def k(u_hbm, v_hbm, idx_hbm, w_hbm, h_hbm, out_hbm, idx_v, coef_v, h_v, y_v, red_v, rows, sems):
        wid = _sc_worker_id()
        lane = lax.iota(I32, SC_LANES)

        def copy(tab_hbm, u, slot):
            ids = idx_v.at[pl.ds(_sc_unit_off(u), SC_LANES)]
            return pltpu.make_async_copy(tab_hbm.at[ids], rows[slot], sems[slot])

        def dots(u, slot):
            tt = u // heads

            def body(c, accs):
                hv = h_v[tt, pl.ds(pl.multiple_of(c * SC_LANES, SC_LANES), SC_LANES)]
                return tuple(accs[r] + _sc_row_piece(rows[slot], r, c) * hv for r in range(SC_LANES))

            accs = lax.fori_loop(0, pieces, body,
                                 tuple(jnp.zeros((SC_LANES,), F32) for _ in range(SC_LANES)))
            for r in range(SC_LANES):
                red_v[pl.ds(r * SC_LANES, SC_LANES)] = accs[r]
            cols = [plsc.load_gather(red_v, [lane * SC_LANES + j]) for j in range(SC_LANES)]
            while len(cols) > 1:
                cols = [cols[i] + cols[i + 1] for i in range(0, len(cols), 2)]
            sl = pl.ds(_sc_unit_off(u), SC_LANES)
            coef_v[sl] = coef_v[sl] * _gelu_tanh(cols[0])

        def combine(u, slot):
            tt = u // heads
            first = (u % heads) == 0
            cs = [plsc.load_gather(coef_v, [jnp.full((SC_LANES,), u * SC_LANES + r, I32)])
                  for r in range(SC_LANES)]

            @plsc.parallel_loop(0, pieces, unroll=2)
            def _(c):
                off = pl.multiple_of(c * SC_LANES, SC_LANES)
                terms = [cs[r] * _sc_row_piece(rows[slot], r, c) for r in range(SC_LANES)]
                while len(terms) > 1:
                    terms = [terms[i] + terms[i + 1] for i in range(0, len(terms), 2)]
                prev = y_v[tt, pl.ds(off, SC_LANES)]
                y_v[tt, pl.ds(off, SC_LANES)] = terms[0] + jnp.where(first, 0.0, prev)

        @pl.loop(0, groups)
        def _(gi):
            base = wid * tpw + gi * g
            pltpu.sync_copy(idx_hbm.at[pl.ds(base * nsel, g * nsel)], idx_v)
            pltpu.sync_copy(w_hbm.at[pl.ds(base * nsel, g * nsel)], coef_v)
            pltpu.sync_copy(h_hbm.at[pl.ds(base, g)], h_v)
            _sc_ring(units, lambda u, s: copy(u_hbm, u, s).start(), lambda u, s: copy(u_hbm, u, s).wait(), dots)
            _sc_ring(units, lambda u, s: copy(v_hbm, u, s).start(), lambda u, s: copy(v_hbm, u, s).wait(), combine)
            pltpu.sync_copy(y_v, out_hbm.at[pl.ds(base, g)])

    return k(tab_u, tab_v, idx_flat, w_flat, h)


SC_PK_RING = 4
SC_PK_SUB = 4
HI_MASK = -65536


def pack_bf16_pairs(a):
    half = a.shape[1] // 2
    bits = lax.bitcast_convert_type(a.astype(BF16), jnp.uint16).astype(jnp.uint32)
    return lax.bitcast_convert_type(bits[:, :half] | (bits[:, half:] << 16), I32)


def _unpack_halves(x32):
    w = plsc.bitcast(x32, I32)
    return plsc.bitcast(w << 16, F32), plsc.bitcast(w & HI_MASK, F32)


def _tree_sum(xs):
    while len(xs) > 1:
        xs = [xs[i] + xs[i + 1] for i in range(0, len(xs), 2)]
    return xs[0]


def peer_experts_pk_sc(tab_uv, idx_flat, w_flat, hp, d):
    t = hp.shape[0]
    nsel = PEER_SEL
    tpw = t // SC_WORKERS
    g = SC_GROUP
    groups = tpw // g
    heads = nsel // SC_LANES
    chunks = d // 32
    units = g * heads
    ring = SC_PK_RING
    row_buf = pltpu.VMEM((SC_LANES, 2 * SC_PK_SUB, SC_ROW_LANE), I32)

    def row_words(rows, r, wc, sub0):
        per = SC_ROW_LANE // SC_LANES
        return plsc.bitcast(
            rows[r, sub0 + wc // per, pl.ds(pl.multiple_of((wc % per) * SC_LANES, SC_LANES), SC_LANES)], BF16)

    def ring_loop(n_units, start, wait, compute):
        for u in range(ring - 1):
            start(u, u)

        @pl.loop(0, n_units, step=ring)
        def _(uu):
            for b in range(ring):
                u = uu + b
                nxt = u + (ring - 1)

                @pl.when(nxt < n_units)
                def _():
                    start(nxt, (b + ring - 1) % ring)

                wait(u, b)
                compute(u, b)

    @functools.partial(
        pl.kernel, mesh=_sc_mesh(),
        out_type=jax.ShapeDtypeStruct((t, d), F32),
        scratch_types=[
            pltpu.VMEM((g * nsel,), I32),
            pltpu.VMEM((g * nsel,), F32),
            pltpu.VMEM((g, d // 2), I32),
            pltpu.VMEM((g, d), F32),
            pltpu.VMEM((SC_LANES * SC_LANES,), F32),
            [row_buf] * ring,
            [pltpu.SemaphoreType.DMA] * ring,
        ],
        compiler_params=_sc_params(),
        name="peer_experts_pk_sc",
    )
    def k(tab_hbm, idx_hbm, w_hbm, h_hbm, out_hbm, idx_v, coef_v, h_v, y_v, red_v, rows, sems):
        wid = _sc_worker_id()
        lane = lax.iota(I32, SC_LANES)

        def copy(u, slot):
            ids = idx_v.at[pl.ds(_sc_unit_off(u), SC_LANES)]
            return pltpu.make_async_copy(tab_hbm.at[ids], rows[slot], sems[slot])

        def dots(u, slot):
            tt = u // heads

            def body(cp, accs):
                out = []
                hv = [plsc.bitcast(h_v[tt, pl.ds(pl.multiple_of((2 * cp + i) * SC_LANES, SC_LANES), SC_LANES)], BF16)
                      for i in range(2)]
                for r in range(SC_LANES):
                    pr = (row_words(rows[slot], r, 2 * cp, 0) * hv[0]
                          + row_words(rows[slot], r, 2 * cp + 1, 0) * hv[1])
                    lo, hi = _unpack_halves(pr)
                    out.append(accs[r] + lo + hi)
                return tuple(out)

            accs = lax.fori_loop(0, chunks // 2, body,
                                 tuple(jnp.zeros((SC_LANES,), F32) for _ in range(SC_LANES)))
            for r in range(SC_LANES):
                red_v[pl.ds(r * SC_LANES, SC_LANES)] = accs[r]
            act = _tree_sum([plsc.load_gather(red_v, [lane * SC_LANES + j]) for j in range(SC_LANES)])
            sl = pl.ds(_sc_unit_off(u), SC_LANES)
            coef_v[sl] = coef_v[sl] * _gelu_tanh(act)

        def combine(u, slot):
            tt = u // heads
            first = (u % heads) == 0
            cb = []
            for r in range(SC_LANES):
                c = plsc.load_gather(coef_v, [jnp.full((SC_LANES,), u * SC_LANES + r, I32)])
                cb.append(plsc.pack(c, c, format=plsc.PackFormat.INTERLEAVED))

            @plsc.parallel_loop(0, chunks, unroll=2)
            def _(wc):
                lo, hi = _unpack_halves(
                    _tree_sum([cb[r] * row_words(rows[slot], r, wc, SC_PK_SUB) for r in range(SC_LANES)]))
                for half, val in ((0, lo), (1, hi)):
                    sl = pl.ds(pl.multiple_of(half * (d // 2) + wc * SC_LANES, SC_LANES), SC_LANES)
                    y_v[tt, sl] = val + jnp.where(first, 0.0, y_v[tt, sl])

        def unit(u, slot):
            dots(u, slot)
            combine(u, slot)

        @pl.loop(0, groups)
        def _(gi):
            base = wid * tpw + gi * g
            pltpu.sync_copy(idx_hbm.at[pl.ds(base * nsel, g * nsel)], idx_v)
            pltpu.sync_copy(w_hbm.at[pl.ds(base * nsel, g * nsel)], coef_v)
            pltpu.sync_copy(h_hbm.at[pl.ds(base, g)], h_v)
            ring_loop(units, lambda u, s: copy(u, s).start(), lambda u, s: copy(u, s).wait(), unit)
            pltpu.sync_copy(y_v, out_hbm.at[pl.ds(base, g)])

    return k(tab_uv, idx_flat, w_flat, hp)


def kernel(x, mem, rel_bias, ln_mix, w_in, hg_lower, hg_norm, w_up_a, w_up_b, w_out, ln_cross, ln_mem, wq_x, wk_x, wv_x, wo_x, ln_ffn, peer_query, peer_subkeys, peer_u, peer_v, ln_final):
    b, s, d = x.shape
    depth = w_in.shape[0]
    assert depth == 1, "the residual after PEER is fused into the final norm"
    assert s % MB_BLOCK == 0 and s % HG_CHUNK == 0 and s % (PEER_SLICES * SC_WORKERS * SC_GROUP) == 0
    nb = s // MB_BLOCK
    row = lambda a: a.reshape(1, -1).astype(F32)
    lb_all = jnp.cumsum(jax.nn.softmax(hg_lower.astype(F32), axis=0), axis=0)
    bias = moba_bias_tiles(rel_bias)
    n_hg = 4 * HG_WIDTH
    n_qk = 2 * MB_WIDTH
    n_mb = 3 * MB_WIDTH
    l = 0
    w = w_in[l].astype(BF16)
    w_hg, w_qk, w_vt, w_g = w[:, :n_hg], w[:, n_hg:n_hg + n_qk], w[:, n_hg + n_qk:n_hg + n_mb].T, w[:, n_hg + n_mb:]
    wa, wb, wo = w_up_a[l].astype(BF16), w_up_b[l].astype(BF16), w_out[l].astype(BF16)
    wqx, wox = wq_x[l].astype(BF16), wo_x[l].astype(BF16)
    wpq, sk = peer_query[l].astype(BF16), peer_subkeys[l].astype(F32)
    tab3 = lambda a: pack_bf16_pairs(a.astype(F32)).reshape(a.shape[0], SC_PK_SUB, SC_ROW_LANE)
    tab_uv = jnp.concatenate([tab3(peer_u[l]), tab3(peer_v[l])], axis=1)
    kx, vx = mem_kv(mem, row(ln_mem[l]), wk_x[l].astype(BF16), wv_x[l].astype(BF16))

    outs = []
    for bi in range(b):
        x2d = x[bi]
        p0, pqk, vt, pg = in_proj(x2d, row(ln_mix[l]), w_hg, w_qk, w_vt, w_g)
        ya = hgrn2(p0, row(lb_all[l]), row(hg_norm[l]), 1, s)
        km = moba_kmean(pqk, 1, s).reshape(1, nb, MB_WIDTH)
        ts = s // PEER_SLICES
        for tok0 in range(0, s, ts):
            yb = moba_attention(pqk, vt, km, bias, 1, s, tok0 // MB_BLOCK, ts // MB_BLOCK)
            xs = mix_out(x2d, ya, yb, pg, wa, wb, wo, tok0)
            xs = cross_attn(xs, row(ln_cross[l]), wqx, kx[bi:bi + 1], vx[bi:bi + 1], wox, ts)
            hp, eidx, wts = peer_route(xs, row(ln_ffn[l]), wpq, sk, 0, ts)
            y = peer_experts_pk_sc(tab_uv, eidx.reshape(ts * PEER_SEL), wts.reshape(ts * PEER_SEL), hp, d)
            outs.append(final_norm(xs, y, row(ln_final), 0))
    return jnp.concatenate(outs, axis=0).reshape(b, s, d)
```

```python
import functools
import math

import jax
import jax.numpy as jnp
import numpy as np
from jax import lax
from jax.experimental import pallas as pl
from jax.experimental.pallas import tpu as pltpu
from jax.experimental.pallas import tpu_sc as plsc

F32 = jnp.float32
BF16 = jnp.bfloat16
I32 = jnp.int32
EPS = 1e-6
NEG_INF = float("-inf")

HG_HEADS = 4
HG_D = 128
HG_WIDTH = HG_HEADS * HG_D
HG_CHUNK = 64
HG_SUB = 16
MB_HEADS = 8
MB_DH = 64
MB_WIDTH = MB_HEADS * MB_DH
MB_BLOCK = 256
MB_TOPK = 3
MB_BIAS_TILES = 8
REL_BUCKETS = 32
REL_MAX_DIST = 2048
X_HEADS = 4
PEER_HEADS = 8
PEER_NKEYS = 128
PEER_TOPK = 16
PEER_HALF = 128
PEER_SEL = PEER_HEADS * PEER_TOPK
PEER_SLICES = 4

VMEM_LIMIT = 56 * 1024 * 1024


def _cparams(sem):
    return pltpu.CompilerParams(dimension_semantics=sem, vmem_limit_bytes=VMEM_LIMIT)


def _rms(x, g):
    ms = jnp.mean(x * x, axis=-1, keepdims=True)
    return x * lax.rsqrt(ms + EPS) * g


def _in_proj_kernel(x_ref, g_ref, w0_ref, w1_ref, wvt_ref, w2_ref, o0_ref, o1_ref, ovt_ref, o2_ref):
    h = _rms(x_ref[...], g_ref[...]).astype(BF16)
    o0_ref[...] = jnp.dot(h, w0_ref[...], preferred_element_type=F32)
    o1_ref[...] = jnp.dot(h, w1_ref[...], preferred_element_type=F32).astype(BF16)
    vt = lax.dot_general(wvt_ref[...], h, (((1,), (1,)), ((), ())), preferred_element_type=F32).astype(BF16)
    for hd in range(MB_HEADS):
        ovt_ref[0, hd * MB_VROWS:hd * MB_VROWS + MB_DH, :] = vt[hd * MB_DH:(hd + 1) * MB_DH]
        ovt_ref[0, hd * MB_VROWS + MB_DH:(hd + 1) * MB_VROWS, :] = jnp.ones((MB_ONES, vt.shape[1]), BF16)
    o2_ref[...] = jnp.dot(h, w2_ref[...], preferred_element_type=F32).astype(BF16)


def in_proj(x2d, g, w0, w1, wvt, w2):
    t, d = x2d.shape
    tm = MB_BLOCK
    assert wvt.shape[0] == MB_WIDTH
    n0, n1, nv, n2 = w0.shape[1], w1.shape[1], MB_VT_ROWS, w2.shape[1]
    full = lambda a: pl.BlockSpec(a.shape, lambda i: (0, 0))
    return pl.pallas_call(
        _in_proj_kernel,
        grid=(t // tm,),
        in_specs=[pl.BlockSpec((tm, d), lambda i: (i, 0)), full(g), full(w0), full(w1), full(wvt), full(w2)],
        out_specs=[pl.BlockSpec((tm, n0), lambda i: (i, 0)),
                   pl.BlockSpec((tm, n1), lambda i: (i, 0)),
                   pl.BlockSpec((1, nv, tm), lambda i: (i, 0, 0)),
                   pl.BlockSpec((tm, n2), lambda i: (i, 0))],
        out_shape=[jax.ShapeDtypeStruct((t, n0), F32),
                   jax.ShapeDtypeStruct((t, n1), BF16),
                   jax.ShapeDtypeStruct((t // tm, nv, tm), BF16),
                   jax.ShapeDtypeStruct((t, n2), BF16)],
        compiler_params=_cparams(("parallel",)),
        name="in_proj",
    )(x2d, g, w0, w1, wvt, w2)


def _hgrn_kernel(q_ref, f_ref, i_ref, g_ref, lb_ref, gain_ref, o_ref, st_ref):
    c = pl.program_id(1)

    @pl.when(c == 0)
    def _():
        st_ref[...] = jnp.zeros_like(st_ref)

    C, S = HG_CHUNK, HG_SUB
    row = lax.broadcasted_iota(I32, (C, C), 0)
    col = lax.broadcasted_iota(I32, (C, C), 1)
    tril = (row >= col).astype(F32)
    t_iota = lax.broadcasted_iota(I32, (S, 1), 0)

    for h in range(HG_HEADS):
        sl = slice(h * HG_D, (h + 1) * HG_D)
        q = q_ref[:, sl]
        v = i_ref[:, sl]
        lb = lb_ref[:, sl]
        f = lb + (1.0 - lb) * jax.nn.sigmoid(f_ref[:, sl])
        lf = jnp.log(f)
        k = 1.0 - f
        b = jnp.dot(tril, lf, precision=lax.Precision.HIGHEST, preferred_element_type=F32)
        st = st_ref[h]
        vb = v.astype(BF16)
        qd = (q * jnp.exp(b)).astype(BF16)
        o_inter = lax.dot_general(qd, st.astype(BF16), (((1,), (1,)), ((), ())),
                                  preferred_element_type=F32)
        outs = []
        for i in range(C // S):
            r0 = i * S
            qi = q[r0:r0 + S]
            ki = k[r0:r0 + S]
            bi = b[r0:r0 + S]
            vi = v[r0:r0 + S]
            oi = o_inter[r0:r0 + S]
            if i > 0:
                bs = b[r0 - 1:r0]
                qh = (qi * jnp.exp(bi - bs)).astype(BF16)
                kh = (k[:r0] * jnp.exp(bs - b[:r0])).astype(BF16)
                a = lax.dot_general(qh, kh, (((1,), (1,)), ((), ())), preferred_element_type=F32)
                oi = oi + jnp.dot(a.astype(BF16), vb[:r0], preferred_element_type=F32)
            half = S // 2
            o_half = [oi[:half], oi[half:]]
            for s in range(S):
                for hf in range(s // half, 2):
                    rows = slice(hf * half, (hf + 1) * half)
                    dec = jnp.exp(jnp.minimum(bi[rows] - bi[s:s + 1], 0.0))
                    a_s = jnp.sum(qi[rows] * ki[s:s + 1] * dec, axis=-1, keepdims=True)
                    a_s = jnp.where(t_iota[rows] >= s, a_s, 0.0)
                    o_half[hf] = o_half[hf] + a_s * vi[s:s + 1]
            outs.extend(o_half)
        o = jnp.concatenate(outs, axis=0)
        b_end = b[C - 1:C]
        kd = (k * jnp.exp(b_end - b)).astype(BF16)
        upd = lax.dot_general(vb, kd, (((0,), (0,)), ((), ())), preferred_element_type=F32)
        st_ref[h] = st * jnp.exp(b_end) + upd
        o = o * lax.rsqrt(jnp.mean(o * o, axis=-1, keepdims=True) + EPS)
        g = g_ref[:, sl]
        o_ref[:, sl] = (o * gain_ref[:, sl] * (g * jax.nn.sigmoid(g))).astype(o_ref.dtype)


def hgrn2(p0, lb, gain, batch, seq):
    t = p0.shape[0]
    nc = seq // HG_CHUNK
    w = HG_WIDTH

    def col(j):
        return pl.BlockSpec((HG_CHUNK, w), lambda b, c, j=j: (b * nc + c, j))

    return pl.pallas_call(
        _hgrn_kernel,
        grid=(batch, nc),
        in_specs=[col(0), col(1), col(2), col(3),
                  pl.BlockSpec((1, w), lambda b, c: (0, 0)),
                  pl.BlockSpec((1, w), lambda b, c: (0, 0))],
        out_specs=pl.BlockSpec((HG_CHUNK, w), lambda b, c: (b * nc + c, 0)),
        out_shape=jax.ShapeDtypeStruct((t, w), BF16),
        scratch_shapes=[pltpu.VMEM((HG_HEADS, HG_D, HG_D), F32)],
        compiler_params=_cparams(("parallel", "arbitrary")),
        name="hgrn2",
    )(p0, p0, p0, p0, lb, gain)


def _kmean_kernel(k_ref, o_ref):
    o_ref[0] = jnp.mean(k_ref[...].astype(F32), axis=0, keepdims=True)


def moba_kmean(p1, batch, seq):
    nbt = p1.shape[0] // MB_BLOCK
    return pl.pallas_call(
        _kmean_kernel,
        grid=(nbt,),
        in_specs=[pl.BlockSpec((MB_BLOCK, MB_WIDTH), lambda i: (i, 1))],
        out_specs=pl.BlockSpec((1, 1, MB_WIDTH), lambda i: (i, 0, 0)),
        out_shape=jax.ShapeDtypeStruct((nbt, 1, MB_WIDTH), F32),
        compiler_params=_cparams(("parallel",)),
        name="moba_kmean",
    )(p1)


MB_PAIR = 4
MB_PW = MB_PAIR * MB_DH
MB_LG = 128
MB_ONES = 16
MB_VROWS = MB_DH + MB_ONES
MB_VT_ROWS = MB_HEADS * MB_VROWS


def _moba_kernel(q_ref, k_ref, vt_ref, km_ref, bias_ref, o_ref, *scratch, qb0):
    m_ref, l_ref, al_ref, acc_ref, msk_ref, s_ref, p_ref = (
        scratch[i * MB_PAIR:(i + 1) * MB_PAIR] for i in range(7))
    qi = pl.program_id(2) + qb0
    nb = km_ref.shape[0]
    blk = MB_BLOCK
    heads = range(MB_PAIR)
    grp = lambda hh: slice((hh // 2) * MB_LG, (hh // 2 + 1) * MB_LG)
    q = q_ref[...]
    lane = lax.broadcasted_iota(I32, (blk, MB_LG), 1)
    in_head = [(lane < MB_DH) if hh % 2 == 0 else (lane >= MB_DH) for hh in heads]
    qs = q * jnp.asarray(MB_DH ** -0.5, BF16)
    nt = (((1,), (1,)), ((), ()))
    qf = q.astype(F32)
    qht = [jnp.where(in_head[hh], qs[:, grp(hh)].astype(F32), 0.0).T.astype(BF16) for hh in heads]

    n_io = lax.broadcasted_iota(I32, (nb, blk), 0)
    for hh in heads:
        gate = lax.dot_general(km_ref[:, grp(hh)], jnp.where(in_head[hh], qf[:, grp(hh)], 0.0), nt,
                               precision=lax.Precision.HIGHEST, preferred_element_type=F32)
        gate = jnp.where(n_io < qi, gate, NEG_INF)
        chosen = n_io < 0
        for _ in range(MB_TOPK):
            mx = jnp.max(gate, axis=0, keepdims=True)
            ix = jnp.min(jnp.where(gate == mx, n_io, nb), axis=0, keepdims=True)
            hit = n_io == ix
            chosen = chosen | (hit & (mx > NEG_INF))
            gate = jnp.where(hit, NEG_INF, gate)
        msk_ref[hh][...] = jnp.where(chosen, 0.0, NEG_INF)

    vrows = lambda hh: slice(hh * MB_VROWS, (hh + 1) * MB_VROWS)

    def pv_stage(blk_idx):
        vtb = vt_ref[blk_idx]
        r = [jnp.dot(vtb[vrows(hh)], p_ref[hh][...], preferred_element_type=F32) for hh in heads]
        al = [al_ref[hh][...] for hh in heads]
        a_new = [al[hh] * acc_ref[hh][...] + r[hh][:MB_DH] for hh in heads]
        l_new = [al[hh] * l_ref[hh][...] + r[hh][MB_DH:MB_DH + 1] for hh in heads]
        return a_new, l_new

    def store_pv(a_new, l_new):
        for hh in heads:
            acc_ref[hh][...] = a_new[hh]
            l_ref[hh][...] = l_new[hh]

    def softmax_stage():
        s = [s_ref[hh][...] for hh in heads]
        m_old = [m_ref[hh][...] for hh in heads]
        m_new = [jnp.maximum(m_old[hh], jnp.max(s[hh], axis=0, keepdims=True)) for hh in heads]
        alpha = [jnp.exp(m_old[hh] - m_new[hh]) for hh in heads]
        p = [jnp.exp((s[hh] - m_new[hh]).astype(BF16)) for hh in heads]
        return p, alpha, m_new

    def store_softmax(p, alpha, m_new):
        for hh in heads:
            p_ref[hh][...] = p[hh]
            al_ref[hh][...] = alpha[hh]
            m_ref[hh][...] = m_new[hh]

    k_own = k_ref[pl.ds(pl.multiple_of(qi * blk, blk), blk), :]
    key_io = lax.broadcasted_iota(I32, (blk, blk), 0)
    qry_io = lax.broadcasted_iota(I32, (blk, blk), 1)
    for hh in heads:
        s = jnp.dot(k_own[:, grp(hh)], qht[hh], preferred_element_type=F32) + bias_ref[hh, 0]
        s_ref[hh][...] = jnp.where(key_io <= qry_io, s, NEG_INF)
        m_ref[hh][...] = jnp.full((1, blk), NEG_INF, F32)
        l_ref[hh][...] = jnp.zeros((1, blk), F32)
        al_ref[hh][...] = jnp.ones((1, blk), F32)
        acc_ref[hh][...] = jnp.zeros((MB_DH, blk), F32)
        p_ref[hh][...] = jnp.zeros((blk, blk), BF16)

    def step(i, carry, far):
        pv = pv_stage(jnp.where(i <= 1, qi, i - 2))
        sm = softmax_stage()
        kn = k_ref[pl.ds(pl.multiple_of(i * blk, blk), blk), :]
        if far:
            row = [msk_ref[hh][pl.ds(i, 1), :] + bias_ref[hh, MB_BIAS_TILES - 1, 0:1, 0:1] for hh in heads]
            s_next = [jnp.dot(kn[:, grp(hh)], qht[hh], preferred_element_type=F32) + row[hh] for hh in heads]
        else:
            d = qi - i
            s_next = [jnp.dot(kn[:, grp(hh)], qht[hh], preferred_element_type=F32)
                      + bias_ref[hh, d] + msk_ref[hh][pl.ds(i, 1), :] for hh in heads]
        store_pv(*pv)
        for hh in heads:
            s_ref[hh][...] = s_next[hh]
        store_softmax(*sm)
        return carry

    n_far = jnp.maximum(qi - (MB_BIAS_TILES - 2), 0)
    lax.fori_loop(0, n_far, functools.partial(step, far=True), 0)
    lax.fori_loop(n_far, qi, functools.partial(step, far=False), 0)
    pv = pv_stage(jnp.where(qi <= 1, qi, qi - 2))
    sm = softmax_stage()
    store_pv(*pv)
    store_softmax(*sm)
    a_fin, l_fin = pv_stage(jnp.where(qi == 0, qi, qi - 1))
    out_t = jnp.concatenate([a_fin[hh] / l_fin[hh] for hh in heads], axis=0)
    o_ref[...] = out_t.T.astype(o_ref.dtype)


def moba_attention(pqk, vt, km, bias, batch, seq, qb0=0, nqb=None):
    nb = seq // MB_BLOCK
    nqb = nb if nqb is None else nqb
    t = batch * nqb * MB_BLOCK
    groups = MB_WIDTH // MB_PW
    return pl.pallas_call(
        functools.partial(_moba_kernel, qb0=qb0),
        grid=(batch, groups, nqb),
        in_specs=[
            pl.BlockSpec((MB_BLOCK, MB_PW), lambda b, j, i: (b * nb + qb0 + i, j)),
            pl.BlockSpec((seq, MB_PW), lambda b, j, i: (b, groups + j)),
            pl.BlockSpec((nb, MB_PAIR * MB_VROWS, MB_BLOCK), lambda b, j, i: (b, j, 0)),
            pl.BlockSpec((None, nb, MB_PW), lambda b, j, i: (b, 0, j)),
            pl.BlockSpec((MB_PAIR, MB_BIAS_TILES, MB_BLOCK, MB_BLOCK), lambda b, j, i: (j, 0, 0, 0)),
        ],
        out_specs=pl.BlockSpec((MB_BLOCK, MB_PW), lambda b, j, i: (b * nqb + i, j)),
        out_shape=jax.ShapeDtypeStruct((t, MB_WIDTH), BF16),
        scratch_shapes=(
            [pltpu.VMEM((1, MB_BLOCK), F32)] * (3 * MB_PAIR)
            + [pltpu.VMEM((MB_DH, MB_BLOCK), F32)] * MB_PAIR
            + [pltpu.VMEM((nb, MB_BLOCK), F32)] * MB_PAIR
            + [pltpu.VMEM((MB_BLOCK, MB_BLOCK), F32)] * MB_PAIR
            + [pltpu.VMEM((MB_BLOCK, MB_BLOCK), BF16)] * MB_PAIR
        ),
        compiler_params=_cparams(("parallel", "parallel", "arbitrary")),
        name="moba_attn",
    )(pqk, pqk, vt, km, bias)


def _t5_bucket(dist):
    max_exact = REL_BUCKETS // 2
    scaled = jnp.log(jnp.maximum(dist, 1).astype(F32) / max_exact) / math.log(REL_MAX_DIST / max_exact)
    large = jnp.minimum(max_exact + (scaled * (REL_BUCKETS - max_exact)).astype(I32), REL_BUCKETS - 1)
    return jnp.where(dist < max_exact, dist, large)


def moba_bias_tiles(rel_bias):
    blk = MB_BLOCK
    span = 2 * blk - 1
    x = jnp.arange(span) - (blk - 1)
    dist = jnp.maximum(jnp.arange(MB_BIAS_TILES)[:, None] * blk + x[None, :], 0)
    w = rel_bias.astype(F32).T[:, _t5_bucket(dist)]
    h = w.shape[0]
    wp = jnp.pad(w, ((0, 0), (0, 0), (0, 1)))
    a = jnp.broadcast_to(wp[:, :, None, :], (h, MB_BIAS_TILES, blk, span + 1))
    a = a.reshape(h, MB_BIAS_TILES, blk * (span + 1))[:, :, :blk * span]
    return a.reshape(h, MB_BIAS_TILES, blk, span)[:, :, :, blk - 1:]


def _mix_kernel(x_ref, ya_ref, yb_ref, ga_ref, gb_ref, wa_ref, wb_ref, wo_ref, o_ref):
    za = jnp.dot(ya_ref[...], wa_ref[...], preferred_element_type=F32)
    zb = jnp.dot(yb_ref[...], wb_ref[...], preferred_element_type=F32)
    z = jax.nn.sigmoid(ga_ref[...].astype(F32)) * za + jax.nn.sigmoid(gb_ref[...].astype(F32)) * zb
    o_ref[...] = x_ref[...] + jnp.dot(z.astype(BF16), wo_ref[...], preferred_element_type=F32)


def mix_out(x2d, ya, yb, pg, wa, wb, wo, tok0=0, tm=256):
    t = yb.shape[0]
    d = x2d.shape[1]
    w = ya.shape[1]
    b0 = tok0 // tm
    return pl.pallas_call(
        _mix_kernel,
        grid=(t // tm,),
        in_specs=[
            pl.BlockSpec((tm, d), lambda i: (b0 + i, 0)),
            pl.BlockSpec((tm, w), lambda i: (b0 + i, 0)),
            pl.BlockSpec((tm, w), lambda i: (i, 0)),
            pl.BlockSpec((tm, d), lambda i: (b0 + i, 0)),
            pl.BlockSpec((tm, d), lambda i: (b0 + i, 1)),
            pl.BlockSpec((w, d), lambda i: (0, 0)),
            pl.BlockSpec((w, d), lambda i: (0, 0)),
            pl.BlockSpec((d, d), lambda i: (0, 0)),
        ],
        out_specs=pl.BlockSpec((tm, d), lambda i: (i, 0)),
        out_shape=jax.ShapeDtypeStruct((t, d), F32),
        compiler_params=_cparams(("parallel",)),
        name="mix_out",
    )(x2d, ya, yb, pg, pg, wa, wb, wo)


def _mem_kv_kernel(m_ref, g_ref, wk_ref, wv_ref, k_ref, v_ref):
    mn = _rms(m_ref[...], g_ref[...]).astype(BF16)
    k_ref[...] = jnp.dot(mn, wk_ref[...], preferred_element_type=F32).astype(BF16)
    v_ref[...] = jnp.dot(mn, wv_ref[...], preferred_element_type=F32).astype(BF16)


def mem_kv(mem, g, wk, wv):
    b, m, d = mem.shape
    spec = pl.BlockSpec((None, m, d), lambda i: (i, 0, 0))
    wspec = pl.BlockSpec((d, d), lambda i: (0, 0))
    return pl.pallas_call(
        _mem_kv_kernel,
        grid=(b,),
        in_specs=[spec, pl.BlockSpec((1, d), lambda i: (0, 0)), wspec, wspec],
        out_specs=[spec, spec],
        out_shape=[jax.ShapeDtypeStruct((b, m, d), BF16)] * 2,
        compiler_params=_cparams(("parallel",)),
        name="mem_kv",
    )(mem, g, wk, wv)


def _cross_kernel(x_ref, g_ref, wq_ref, k_ref, v_ref, wo_ref, o_ref):
    x = x_ref[...]
    d = x.shape[1]
    dh = d // X_HEADS
    h = _rms(x, g_ref[...]).astype(BF16)
    q = (jnp.dot(h, wq_ref[...], preferred_element_type=F32) * (dh ** -0.5)).astype(BF16)
    outs = []
    for hh in range(X_HEADS):
        sl = slice(hh * dh, (hh + 1) * dh)
        s = lax.dot_general(q[:, sl], k_ref[:, sl], (((1,), (1,)), ((), ())),
                            preferred_element_type=F32)
        p = jnp.exp(s - jnp.max(s, axis=1, keepdims=True))
        l = jnp.sum(p, axis=1, keepdims=True)
        o = jnp.dot(p.astype(BF16), v_ref[:, sl], preferred_element_type=F32) / l
        outs.append(o.astype(BF16))
    o = jnp.concatenate(outs, axis=1)
    o_ref[...] = x + jnp.dot(o, wo_ref[...], preferred_element_type=F32)


def cross_attn(x2d, g, wq, kx, vx, wo, seq, tm=256):
    t, d = x2d.shape
    m = kx.shape[1]
    per_b = seq // tm
    kv = pl.BlockSpec((None, m, d), lambda i: (i // per_b, 0, 0))
    wspec = pl.BlockSpec((d, d), lambda i: (0, 0))
    return pl.pallas_call(
        _cross_kernel,
        grid=(t // tm,),
        in_specs=[pl.BlockSpec((tm, d), lambda i: (i, 0)), pl.BlockSpec((1, d), lambda i: (0, 0)),
                  wspec, kv, kv, wspec],
        out_specs=pl.BlockSpec((tm, d), lambda i: (i, 0)),
        out_shape=jax.ShapeDtypeStruct((t, d), F32),
        compiler_params=_cparams(("parallel",)),
        name="cross_attn",
    )(x2d, g, wq, kx, vx, wo)


def _topk_rows(sc, k):
    n = sc.shape[0]
    io = lax.broadcasted_iota(I32, sc.shape, 0)
    vals, ids = [], []
    for _ in range(k):
        m = jnp.max(sc, axis=0, keepdims=True)
        ix = jnp.min(jnp.where(sc == m, io, n), axis=0, keepdims=True)
        vals.append(m)
        ids.append(ix)
        sc = jnp.where(io == ix, NEG_INF, sc)
    return jnp.concatenate(vals, axis=0), jnp.concatenate(ids, axis=0)


def _pack_bf16_halves(h):
    bits = lax.bitcast_convert_type(h, I32)
    r = bits + 0x7FFF + (lax.shift_right_logical(bits, 16) & 1)
    half = h.shape[1] // 2
    return lax.shift_right_logical(r[:, :half], 16) | (r[:, half:] & HI_MASK)


def _route_kernel(x_ref, g_ref, wq_ref, sk_ref, hp_ref, idx_ref, w_ref, hb_ref, it_ref, wt_ref):
    p = pl.program_id(1)

    @pl.when(p == 0)
    def _():
        h = _rms(x_ref[...], g_ref[...])
        hp_ref[...] = _pack_bf16_halves(h)
        hb_ref[...] = h.astype(BF16)

    qh = jnp.dot(hb_ref[...], wq_ref[...], preferred_element_type=F32)
    tops = []
    for c in range(2):
        seg = qh[:, c * PEER_HALF:(c + 1) * PEER_HALF]
        sc = lax.dot_general(sk_ref[c], seg, (((1,), (1,)), ((), ())),
                             precision=lax.Precision.HIGHEST, preferred_element_type=F32)
        tops.append(_topk_rows(sc, PEER_TOPK))
    (s0, i0), (s1, i1) = tops
    k = PEER_TOPK
    sub = 8
    tm = s0.shape[1]
    r8 = lax.broadcasted_iota(I32, (sub, tm), 0)
    r16 = lax.broadcasted_iota(I32, (k, tm), 0)
    cand_b = [s0[0:1] + s1, s0[1:2] + s1[:sub]]
    cidx_b = [i0[0:1] * PEER_NKEYS + i1, i0[1:2] * PEER_NKEYS + i1[:sub]]
    pos_b = [r16, k + r8]
    for a in range(2, sub):
        keep = r8 < (k // (a + 1))
        cand_b.append(jnp.where(keep, s0[a:a + 1] + s1[:sub], NEG_INF))
        cidx_b.append(i0[a:a + 1] * PEER_NKEYS + i1[:sub])
        pos_b.append(a * k + r8)
    cand_b.append(s0[sub:] + s1[0:1])
    cidx_b.append(i0[sub:] * PEER_NKEYS + i1[0:1])
    pos_b.append((sub + r8) * k)
    cand = jnp.concatenate(cand_b, axis=0)
    cidx = jnp.concatenate(cidx_b, axis=0)
    pos = jnp.concatenate(pos_b, axis=0)
    vals, ids = [], []
    for _ in range(k):
        m = jnp.max(cand, axis=0, keepdims=True)
        px = jnp.min(jnp.where(cand == m, pos, k * k), axis=0, keepdims=True)
        hit = pos == px
        vals.append(m)
        ids.append(jnp.sum(jnp.where(hit, cidx, 0), axis=0, keepdims=True))
        cand = jnp.where(hit, NEG_INF, cand)
    sf = jnp.concatenate(vals, axis=0)
    e = jnp.exp(sf - sf[0:1])
    rows = pl.ds(pl.multiple_of(p * PEER_TOPK, PEER_TOPK), PEER_TOPK)
    wt_ref[rows, :] = e / jnp.sum(e, axis=0, keepdims=True)
    it_ref[rows, :] = jnp.concatenate(ids, axis=0)

    @pl.when(p == pl.num_programs(1) - 1)
    def _():
        idx_ref[...] = it_ref[...].T
        w_ref[...] = wt_ref[...].T


def peer_route(x2d, g, wq, sk, tok0, t, tm=512):
    d = x2d.shape[1]
    ph = sk.shape[0]
    nsel = ph * PEER_TOPK
    blk0 = tok0 // tm
    return pl.pallas_call(
        _route_kernel,
        grid=(t // tm, ph),
        in_specs=[
            pl.BlockSpec((tm, d), lambda i, p: (blk0 + i, 0)),
            pl.BlockSpec((1, d), lambda i, p: (0, 0)),
            pl.BlockSpec((d, 2 * PEER_HALF), lambda i, p: (0, p)),
            pl.BlockSpec((None, 2, PEER_NKEYS, PEER_HALF), lambda i, p: (p, 0, 0, 0)),
        ],
        out_specs=[
            pl.BlockSpec((tm, d // 2), lambda i, p: (i, 0)),
            pl.BlockSpec((tm, nsel), lambda i, p: (i, 0)),
            pl.BlockSpec((tm, nsel), lambda i, p: (i, 0)),
        ],
        out_shape=[jax.ShapeDtypeStruct((t, d // 2), I32),
                   jax.ShapeDtypeStruct((t, nsel), I32),
                   jax.ShapeDtypeStruct((t, nsel), F32)],
        scratch_shapes=[pltpu.VMEM((tm, d), BF16),
                        pltpu.VMEM((nsel, tm), I32),
                        pltpu.VMEM((nsel, tm), F32)],
        compiler_params=_cparams(("parallel", "arbitrary")),
        name="peer_route",
    )(x2d, g, wq, sk)


def _coef_kernel(w_ref, a_ref, o_ref):
    o_ref[...] = w_ref[...] * jax.nn.gelu(a_ref[...])


def peer_coef(w, act, tm=1024):
    t, n = w.shape
    spec = pl.BlockSpec((tm, n), lambda i: (i, 0))
    return pl.pallas_call(
        _coef_kernel, grid=(t // tm,), in_specs=[spec, spec], out_specs=spec,
        out_shape=jax.ShapeDtypeStruct((t, n), F32),
        compiler_params=_cparams(("parallel",)), name="peer_coef",
    )(w, act)


def _final_kernel(x_ref, y_ref, g_ref, o_ref):
    o_ref[...] = _rms(x_ref[...] + y_ref[...], g_ref[...])


def final_norm(x2d, y, g, tok0, tm=512):
    t, d = y.shape
    blk0 = tok0 // tm
    spec = pl.BlockSpec((tm, d), lambda i: (i, 0))
    return pl.pallas_call(
        _final_kernel, grid=(t // tm,),
        in_specs=[pl.BlockSpec((tm, d), lambda i: (blk0 + i, 0)), spec, pl.BlockSpec((1, d), lambda i: (0, 0))],
        out_specs=spec,
        out_shape=jax.ShapeDtypeStruct((t, d), F32),
        compiler_params=_cparams(("parallel",)), name="final_norm",
    )(x2d, y, g)


SC_CORES = 2
SC_SUBCORES = 16
SC_WORKERS = SC_CORES * SC_SUBCORES
SC_LANES = 16
SC_GROUP = 16


def _sc_mesh():
    return plsc.VectorSubcoreMesh(core_axis_name="c", subcore_axis_name="s")


def _sc_params():
    return pltpu.CompilerParams(needs_layout_passes=False)


def _sc_worker_id():
    return lax.axis_index("s") * SC_CORES + lax.axis_index("c")


SC_RING = 4
SC_ROW_SUB = 8
SC_ROW_LANE = 128


def _sc_ring(n_units, start, wait, compute):
    for u in range(SC_RING - 1):
        start(u, u)

    @pl.loop(0, n_units, step=SC_RING)
    def _(uu):
        for b in range(SC_RING):
            u = uu + b
            nxt = u + (SC_RING - 1)

            @pl.when(nxt < n_units)
            def _():
                start(nxt, (b + SC_RING - 1) % SC_RING)

            wait(u, b)
            compute(u, b)


def _sc_unit_off(u):
    off = u * SC_LANES
    return off if isinstance(off, int) else pl.multiple_of(off, SC_LANES)


def _sc_row_piece(rows, r, c):
    per = SC_ROW_LANE // SC_LANES
    return rows[r, c // per, pl.ds(pl.multiple_of((c % per) * SC_LANES, SC_LANES), SC_LANES)]


def peer_dots_sc(table, idx_flat, h):
    t, d = h.shape
    nsel = PEER_SEL
    tpw = t // SC_WORKERS
    g = SC_GROUP
    groups = tpw // g
    heads = nsel // SC_LANES
    pieces = d // SC_LANES
    units = g * heads
    row_buf = pltpu.VMEM((SC_LANES, SC_ROW_SUB, SC_ROW_LANE), F32)

    @functools.partial(
        pl.kernel, mesh=_sc_mesh(),
        out_type=jax.ShapeDtypeStruct((t * nsel,), F32),
        scratch_types=[
            pltpu.VMEM((g * nsel,), I32),
            pltpu.VMEM((g, d), F32),
            pltpu.VMEM((g * nsel,), F32),
            pltpu.VMEM((SC_LANES * SC_LANES,), F32),
            [row_buf] * SC_RING,
            [pltpu.SemaphoreType.DMA] * SC_RING,
        ],
        compiler_params=_sc_params(),
        name="peer_dots_sc",
    )
    def k(tab_hbm, idx_hbm, h_hbm, out_hbm, idx_v, h_v, out_v, red_v, rows, sems):
        wid = _sc_worker_id()
        lane = lax.iota(I32, SC_LANES)

        def copy(u, slot):
            ids = idx_v.at[pl.ds(_sc_unit_off(u), SC_LANES)]
            return pltpu.make_async_copy(tab_hbm.at[ids], rows[slot], sems[slot])

        def compute(u, slot):
            tt = u // heads

            def body(c, accs):
                hv = h_v[tt, pl.ds(pl.multiple_of(c * SC_LANES, SC_LANES), SC_LANES)]
                return tuple(accs[r] + _sc_row_piece(rows[slot], r, c) * hv for r in range(SC_LANES))

            accs = lax.fori_loop(0, pieces, body,
                                 tuple(jnp.zeros((SC_LANES,), F32) for _ in range(SC_LANES)))
            for r in range(SC_LANES):
                red_v[pl.ds(r * SC_LANES, SC_LANES)] = accs[r]
            cols = [plsc.load_gather(red_v, [lane * SC_LANES + j]) for j in range(SC_LANES)]
            while len(cols) > 1:
                cols = [cols[i] + cols[i + 1] for i in range(0, len(cols), 2)]
            out_v[pl.ds(_sc_unit_off(u), SC_LANES)] = cols[0]

        @pl.loop(0, groups)
        def _(gi):
            base = wid * tpw + gi * g
            pltpu.sync_copy(idx_hbm.at[pl.ds(base * nsel, g * nsel)], idx_v)
            pltpu.sync_copy(h_hbm.at[pl.ds(base, g)], h_v)
            _sc_ring(units, lambda u, s: copy(u, s).start(), lambda u, s: copy(u, s).wait(), compute)
            pltpu.sync_copy(out_v, out_hbm.at[pl.ds(base * nsel, g * nsel)])

    return k(table, idx_flat, h)


def peer_combine_sc(table, idx_flat, coef_flat, t):
    d = table.shape[1] * table.shape[2]
    nsel = PEER_SEL
    tpw = t // SC_WORKERS
    g = SC_GROUP
    groups = tpw // g
    heads = nsel // SC_LANES
    pieces = d // SC_LANES
    units = g * heads
    row_buf = pltpu.VMEM((SC_LANES, SC_ROW_SUB, SC_ROW_LANE), F32)

    @functools.partial(
        pl.kernel, mesh=_sc_mesh(),
        out_type=jax.ShapeDtypeStruct((t, d), F32),
        scratch_types=[
            pltpu.VMEM((g * nsel,), I32),
            pltpu.VMEM((g * nsel,), F32),
            pltpu.VMEM((g, d), F32),
            [row_buf] * SC_RING,
            [pltpu.SemaphoreType.DMA] * SC_RING,
        ],
        compiler_params=_sc_params(),
        name="peer_combine_sc",
    )
    def k(tab_hbm, idx_hbm, coef_hbm, out_hbm, idx_v, coef_v, y_v, rows, sems):
        wid = _sc_worker_id()

        def copy(u, slot):
            ids = idx_v.at[pl.ds(_sc_unit_off(u), SC_LANES)]
            return pltpu.make_async_copy(tab_hbm.at[ids], rows[slot], sems[slot])

        def compute(u, slot):
            tt = u // heads
            first = (u % heads) == 0
            cs = [plsc.load_gather(coef_v, [jnp.full((SC_LANES,), u * SC_LANES + r, I32)])
                  for r in range(SC_LANES)]

            @plsc.parallel_loop(0, pieces, unroll=2)
            def _(c):
                off = pl.multiple_of(c * SC_LANES, SC_LANES)
                terms = [cs[r] * _sc_row_piece(rows[slot], r, c) for r in range(SC_LANES)]
                while len(terms) > 1:
                    terms = [terms[i] + terms[i + 1] for i in range(0, len(terms), 2)]
                prev = y_v[tt, pl.ds(off, SC_LANES)]
                y_v[tt, pl.ds(off, SC_LANES)] = terms[0] + jnp.where(first, 0.0, prev)

        @pl.loop(0, groups)
        def _(gi):
            base = wid * tpw + gi * g
            pltpu.sync_copy(idx_hbm.at[pl.ds(base * nsel, g * nsel)], idx_v)
            pltpu.sync_copy(coef_hbm.at[pl.ds(base * nsel, g * nsel)], coef_v)
            _sc_ring(units, lambda u, s: copy(u, s).start(), lambda u, s: copy(u, s).wait(), compute)
            pltpu.sync_copy(y_v, out_hbm.at[pl.ds(base, g)])

    return k(table, idx_flat, coef_flat)


GELU_C0 = math.sqrt(2.0 / math.pi)
GELU_C1 = 0.044715


def _gelu_tanh(x):
    z = GELU_C0 * (x + GELU_C1 * (x * x * x))
    th = 1.0 - 2.0 / (jnp.exp(2.0 * z) + 1.0)
    return 0.5 * x * (1.0 + th)


def peer_experts_sc(tab_u, tab_v, idx_flat, w_flat, h):
    t, d = h.shape
    nsel = PEER_SEL
    tpw = t // SC_WORKERS
    g = SC_GROUP
    groups = tpw // g
    heads = nsel // SC_LANES
    pieces = d // SC_LANES
    units = g * heads
    row_buf = pltpu.VMEM((SC_LANES, SC_ROW_SUB, SC_ROW_LANE), F32)

    @functools.partial(
        pl.kernel, mesh=_sc_mesh(),
        out_type=jax.ShapeDtypeStruct((t, d), F32),
        scratch_types=[
            pltpu.VMEM((g * nsel,), I32),
            pltpu.VMEM((g * nsel,), F32),
            pltpu.VMEM((g, d), F32),
            pltpu.VMEM((g, d), F32),
            pltpu.VMEM((SC_LANES * SC_LANES,), F32),
            [row_buf] * SC_RING,
            [pltpu.SemaphoreType.DMA] * SC_RING,
        ],
        compiler_params=_sc_params(),
        name="peer_experts_sc",
    )
    def k(u_hbm, v_hbm, idx_hbm, w_hbm, h_hbm, out_hbm, idx_v, coef_v, h_v, y_v, red_v, rows, sems):
        wid = _sc_worker_id()
        lane = lax.iota(I32, SC_LANES)

        def copy(tab_hbm, u, slot):
            ids = idx_v.at[pl.ds(_sc_unit_off(u), SC_LANES)]
            return pltpu.make_async_copy(tab_hbm.at[ids], rows[slot], sems[slot])

        def dots(u, slot):
            tt = u // heads

            def body(c, accs):
                hv = h_v[tt, pl.ds(pl.multiple_of(c * SC_LANES, SC_LANES), SC_LANES)]
                return tuple(accs[r] + _sc_row_piece(rows[slot], r, c) * hv for r in range(SC_LANES))

            accs = lax.fori_loop(0, pieces, body,
                                 tuple(jnp.zeros((SC_LANES,), F32) for _ in range(SC_LANES)))
            for r in range(SC_LANES):
                red_v[pl.ds(r * SC_LANES, SC_LANES)] = accs[r]
            cols = [plsc.load_gather(red_v, [lane * SC_LANES + j]) for j in range(SC_LANES)]
            while len(cols) > 1:
                cols = [cols[i] + cols[i + 1] for i in range(0, len(cols), 2)]
            sl = pl.ds(_sc_unit_off(u), SC_LANES)
            coef_v[sl] = coef_v[sl] * _gelu_tanh(cols[0])

        def combine(u, slot):
            tt = u // heads
            first = (u % heads) == 0
            cs = [plsc.load_gather(coef_v, [jnp.full((SC_LANES,), u * SC_LANES + r, I32)])
                  for r in range(SC_LANES)]

            @plsc.parallel_loop(0, pieces, unroll=2)
            def _(c):
                off = pl.multiple_of(c * SC_LANES, SC_LANES)
                terms = [cs[r] * _sc_row_piece(rows[slot], r, c) for r in range(SC_LANES)]
                while len(terms) > 1:
                    terms = [terms[i] + terms[i + 1] for i in range(0, len(terms), 2)]
                prev = y_v[tt, pl.ds(off, SC_LANES)]
                y_v[tt, pl.ds(off, SC_LANES)] = terms[0] + jnp.where(first, 0.0, prev)

        @pl.loop(0, groups)
        def _(gi):
            base = wid * tpw + gi * g
            pltpu.sync_copy(idx_hbm.at[pl.ds(base * nsel, g * nsel)], idx_v)
            pltpu.sync_copy(w_hbm.at[pl.ds(base * nsel, g * nsel)], coef_v)
            pltpu.sync_copy(h_hbm.at[pl.ds(base, g)], h_v)
            _sc_ring(units, lambda u, s: copy(u_hbm, u, s).start(), lambda u, s: copy(u_hbm, u, s).wait(), dots)
            _sc_ring(units, lambda u, s: copy(v_hbm, u, s).start(), lambda u, s: copy(v_hbm, u, s).wait(), combine)
            pltpu.sync_copy(y_v, out_hbm.at[pl.ds(base, g)])

    return k(tab_u, tab_v, idx_flat, w_flat, h)


SC_PK_RING = 4
SC_PK_SUB = 4
HI_MASK = -65536


def pack_bf16_pairs(a):
    half = a.shape[1] // 2
    bits = lax.bitcast_convert_type(a.astype(BF16), jnp.uint16).astype(jnp.uint32)
    return lax.bitcast_convert_type(bits[:, :half] | (bits[:, half:] << 16), I32)


def _unpack_halves(x32):
    w = plsc.bitcast(x32, I32)
    return plsc.bitcast(w << 16, F32), plsc.bitcast(w & HI_MASK, F32)


def _tree_sum(xs):
    while len(xs) > 1:
        xs = [xs[i] + xs[i + 1] for i in range(0, len(xs), 2)]
    return xs[0]


def peer_experts_pk_sc(tab_uv, idx_flat, w_flat, hp, d):
    t = hp.shape[0]
    nsel = PEER_SEL
    tpw = t // SC_WORKERS
    g = SC_GROUP
    groups = tpw // g
    heads = nsel // SC_LANES
    chunks = d // 32
    units = g * heads
    ring = SC_PK_RING
    row_buf = pltpu.VMEM((SC_LANES, 2 * SC_PK_SUB, SC_ROW_LANE), I32)

    def row_words(rows, r, wc, sub0):
        per = SC_ROW_LANE // SC_LANES
        return plsc.bitcast(
            rows[r, sub0 + wc // per, pl.ds(pl.multiple_of((wc % per) * SC_LANES, SC_LANES), SC_LANES)], BF16)

    def ring_loop(n_units, start, wait, compute):
        for u in range(ring - 1):
            start(u, u)

        @pl.loop(0, n_units, step=ring)
        def _(uu):
            for b in range(ring):
                u = uu + b
                nxt = u + (ring - 1)

                @pl.when(nxt < n_units)
                def _():
                    start(nxt, (b + ring - 1) % ring)

                wait(u, b)
                compute(u, b)

    @functools.partial(
        pl.kernel, mesh=_sc_mesh(),
        out_type=jax.ShapeDtypeStruct((t, d), F32),
        scratch_types=[
            pltpu.VMEM((g * nsel,), I32),
            pltpu.VMEM((g * nsel,), F32),
            pltpu.VMEM((g, d // 2), I32),
            pltpu.VMEM((g, d), F32),
            pltpu.VMEM((SC_LANES * SC_LANES,), F32),
            [row_buf] * ring,
            [pltpu.SemaphoreType.DMA] * ring,
        ],
        compiler_params=_sc_params(),
        name="peer_experts_pk_sc",
    )
    def k(tab_hbm, idx_hbm, w_hbm, h_hbm, out_hbm, idx_v, coef_v, h_v, y_v, red_v, rows, sems):
        wid = _sc_worker_id()
        lane = lax.iota(I32, SC_LANES)

        def copy(u, slot):
            ids = idx_v.at[pl.ds(_sc_unit_off(u), SC_LANES)]
            return pltpu.make_async_copy(tab_hbm.at[ids], rows[slot], sems[slot])

        def dots(u, slot):
            tt = u // heads

            def body(cp, accs):
                out = []
                hv = [plsc.bitcast(h_v[tt, pl.ds(pl.multiple_of((2 * cp + i) * SC_LANES, SC_LANES), SC_LANES)], BF16)
                      for i in range(2)]
                for r in range(SC_LANES):
                    pr = (row_words(rows[slot], r, 2 * cp, 0) * hv[0]
                          + row_words(rows[slot], r, 2 * cp + 1, 0) * hv[1])
                    lo, hi = _unpack_halves(pr)
                    out.append(accs[r] + lo + hi)
                return tuple(out)

            accs = lax.fori_loop(0, chunks // 2, body,
                                 tuple(jnp.zeros((SC_LANES,), F32) for _ in range(SC_LANES)))
            for r in range(SC_LANES):
                red_v[pl.ds(r * SC_LANES, SC_LANES)] = accs[r]
            act = _tree_sum([plsc.load_gather(red_v, [lane * SC_LANES + j]) for j in range(SC_LANES)])
            sl = pl.ds(_sc_unit_off(u), SC_LANES)
            coef_v[sl] = coef_v[sl] * _gelu_tanh(act)

        def combine(u, slot):
            tt = u // heads
            first = (u % heads) == 0
            cb = []
            for r in range(SC_LANES):
                c = plsc.load_gather(coef_v, [jnp.full((SC_LANES,), u * SC_LANES + r, I32)])
                cb.append(plsc.pack(c, c, format=plsc.PackFormat.INTERLEAVED))

            @plsc.parallel_loop(0, chunks, unroll=2)
            def _(wc):
                lo, hi = _unpack_halves(
                    _tree_sum([cb[r] * row_words(rows[slot], r, wc, SC_PK_SUB) for r in range(SC_LANES)]))
                for half, val in ((0, lo), (1, hi)):
                    sl = pl.ds(pl.multiple_of(half * (d // 2) + wc * SC_LANES, SC_LANES), SC_LANES)
                    y_v[tt, sl] = val + jnp.where(first, 0.0, y_v[tt, sl])

        def unit(u, slot):
            dots(u, slot)
            combine(u, slot)

        @pl.loop(0, groups)
        def _(gi):
            base = wid * tpw + gi * g
            pltpu.sync_copy(idx_hbm.at[pl.ds(base * nsel, g * nsel)], idx_v)
            pltpu.sync_copy(w_hbm.at[pl.ds(base * nsel, g * nsel)], coef_v)
            pltpu.sync_copy(h_hbm.at[pl.ds(base, g)], h_v)
            ring_loop(units, lambda u, s: copy(u, s).start(), lambda u, s: copy(u, s).wait(), unit)
            pltpu.sync_copy(y_v, out_hbm.at[pl.ds(base, g)])

    return k(tab_uv, idx_flat, w_flat, hp)


def kernel(x, mem, rel_bias, ln_mix, w_in, hg_lower, hg_norm, w_up_a, w_up_b, w_out, ln_cross, ln_mem, wq_x, wk_x, wv_x, wo_x, ln_ffn, peer_query, peer_subkeys, peer_u, peer_v, ln_final):
    b, s, d = x.shape
    depth = w_in.shape[0]
    assert depth == 1, "the residual after PEER is fused into the final norm"
    assert s % MB_BLOCK == 0 and s % HG_CHUNK == 0 and s % (PEER_SLICES * SC_WORKERS * SC_GROUP) == 0
    nb = s // MB_BLOCK
    row = lambda a: a.reshape(1, -1).astype(F32)
    lb_all = jnp.cumsum(jax.nn.softmax(hg_lower.astype(F32), axis=0), axis=0)
    bias = moba_bias_tiles(rel_bias)
    n_hg = 4 * HG_WIDTH
    n_qk = 2 * MB_WIDTH
    n_mb = 3 * MB_WIDTH
    l = 0
    w = w_in[l].astype(BF16)
    w_hg, w_qk, w_vt, w_g = w[:, :n_hg], w[:, n_hg:n_hg + n_qk], w[:, n_hg + n_qk:n_hg + n_mb].T, w[:, n_hg + n_mb:]
    wa, wb, wo = w_up_a[l].astype(BF16), w_up_b[l].astype(BF16), w_out[l].astype(BF16)
    wqx, wox = wq_x[l].astype(BF16), wo_x[l].astype(BF16)
    wpq, sk = peer_query[l].astype(BF16), peer_subkeys[l].astype(F32)
    tab3 = lambda a: pack_bf16_pairs(a.astype(F32)).reshape(a.shape[0], SC_PK_SUB, SC_ROW_LANE)
    tab_uv = jnp.concatenate([tab3(peer_u[l]), tab3(peer_v[l])], axis=1)
    kx, vx = mem_kv(mem, row(ln_mem[l]), wk_x[l].astype(BF16), wv_x[l].astype(BF16))

    outs = []
    for bi in range(b):
        x2d = x[bi]
        p0, pqk, vt, pg = in_proj(x2d, row(ln_mix[l]), w_hg, w_qk, w_vt, w_g)
        ya = hgrn2(p0, row(lb_all[l]), row(hg_norm[l]), 1, s)
        km = moba_kmean(pqk, 1, s).reshape(1, nb, MB_WIDTH)
        ts = s // PEER_SLICES
        for tok0 in range(0, s, ts):
            yb = moba_attention(pqk, vt, km, bias, 1, s, tok0 // MB_BLOCK, ts // MB_BLOCK)
            xs = mix_out(x2d, ya, yb, pg, wa, wb, wo, tok0)
            xs = cross_attn(xs, row(ln_cross[l]), wqx, kx[bi:bi + 1], vx[bi:bi + 1], wox, ts)
            hp, eidx, wts = peer_route(xs, row(ln_ffn[l]), wpq, sk, 0, ts)
            y = peer_experts_pk_sc(tab_uv, eidx.reshape(ts * PEER_SEL), wts.reshape(ts * PEER_SEL), hp, d)
            outs.append(final_norm(xs, y, row(ln_final), 0))
    return jnp.concatenate(outs, axis=0).reshape(b, s, d)
```

```python
import functools
import math

import jax
import jax.numpy as jnp
import numpy as np
from jax import lax
from jax.experimental import pallas as pl
from jax.experimental.pallas import tpu as pltpu
from jax.experimental.pallas import tpu_sc as plsc

F32 = jnp.float32
BF16 = jnp.bfloat16
I32 = jnp.int32
EPS = 1e-6
NEG_INF = float("-inf")

HG_HEADS = 4
HG_D = 128
HG_WIDTH = HG_HEADS * HG_D
HG_CHUNK = 64
HG_SUB = 16
MB_HEADS = 8
MB_DH = 64
MB_WIDTH = MB_HEADS * MB_DH
MB_BLOCK = 256
MB_TOPK = 3
MB_BIAS_TILES = 8
REL_BUCKETS = 32
REL_MAX_DIST = 2048
X_HEADS = 4
PEER_HEADS = 8
PEER_NKEYS = 128
PEER_TOPK = 16
PEER_HALF = 128
PEER_SEL = PEER_HEADS * PEER_TOPK
PEER_SLICES = 4

VMEM_LIMIT = 56 * 1024 * 1024


def _cparams(sem):
    return pltpu.CompilerParams(dimension_semantics=sem, vmem_limit_bytes=VMEM_LIMIT)


def _rms(x, g):
    ms = jnp.mean(x * x, axis=-1, keepdims=True)
    return x * lax.rsqrt(ms + EPS) * g


def _in_proj_kernel(x_ref, g_ref, w0_ref, w1_ref, wvt_ref, w2_ref, o0_ref, o1_ref, ovt_ref, o2_ref):
    h = _rms(x_ref[...], g_ref[...]).astype(BF16)
    o0_ref[...] = jnp.dot(h, w0_ref[...], preferred_element_type=F32)
    o1_ref[...] = jnp.dot(h, w1_ref[...], preferred_element_type=F32).astype(BF16)
    vt = lax.dot_general(wvt_ref[...], h, (((1,), (1,)), ((), ())), preferred_element_type=F32).astype(BF16)
    for hd in range(MB_HEADS):
        ovt_ref[0, hd * MB_VROWS:hd * MB_VROWS + MB_DH, :] = vt[hd * MB_DH:(hd + 1) * MB_DH]
        ovt_ref[0, hd * MB_VROWS + MB_DH:(hd + 1) * MB_VROWS, :] = jnp.ones((MB_ONES, vt.shape[1]), BF16)
    o2_ref[...] = jnp.dot(h, w2_ref[...], preferred_element_type=F32).astype(BF16)


def in_proj(x2d, g, w0, w1, wvt, w2):
    t, d = x2d.shape
    tm = MB_BLOCK
    assert wvt.shape[0] == MB_WIDTH
    n0, n1, nv, n2 = w0.shape[1], w1.shape[1], MB_VT_ROWS, w2.shape[1]
    full = lambda a: pl.BlockSpec(a.shape, lambda i: (0, 0))
    return pl.pallas_call(
        _in_proj_kernel,
        grid=(t // tm,),
        in_specs=[pl.BlockSpec((tm, d), lambda i: (i, 0)), full(g), full(w0), full(w1), full(wvt), full(w2)],
        out_specs=[pl.BlockSpec((tm, n0), lambda i: (i, 0)),
                   pl.BlockSpec((tm, n1), lambda i: (i, 0)),
                   pl.BlockSpec((1, nv, tm), lambda i: (i, 0, 0)),
                   pl.BlockSpec((tm, n2), lambda i: (i, 0))],
        out_shape=[jax.ShapeDtypeStruct((t, n0), F32),
                   jax.ShapeDtypeStruct((t, n1), BF16),
                   jax.ShapeDtypeStruct((t // tm, nv, tm), BF16),
                   jax.ShapeDtypeStruct((t, n2), BF16)],
        compiler_params=_cparams(("parallel",)),
        name="in_proj",
    )(x2d, g, w0, w1, wvt, w2)


def _hgrn_kernel(q_ref, f_ref, i_ref, g_ref, lb_ref, gain_ref, o_ref, st_ref):
    c = pl.program_id(1)

    @pl.when(c == 0)
    def _():
        st_ref[...] = jnp.zeros_like(st_ref)

    C, S = HG_CHUNK, HG_SUB
    row = lax.broadcasted_iota(I32, (C, C), 0)
    col = lax.broadcasted_iota(I32, (C, C), 1)
    tril = (row >= col).astype(F32)
    t_iota = lax.broadcasted_iota(I32, (S, 1), 0)

    for h in range(HG_HEADS):
        sl = slice(h * HG_D, (h + 1) * HG_D)
        q = q_ref[:, sl]
        v = i_ref[:, sl]
        lb = lb_ref[:, sl]
        f = lb + (1.0 - lb) * jax.nn.sigmoid(f_ref[:, sl])
        lf = jnp.log(f)
        k = 1.0 - f
        b = jnp.dot(tril, lf, precision=lax.Precision.HIGHEST, preferred_element_type=F32)
        st = st_ref[h]
        vb = v.astype(BF16)
        qd = (q * jnp.exp(b)).astype(BF16)
        o_inter = lax.dot_general(qd, st.astype(BF16), (((1,), (1,)), ((), ())),
                                  preferred_element_type=F32)
        outs = []
        for i in range(C // S):
            r0 = i * S
            qi = q[r0:r0 + S]
            ki = k[r0:r0 + S]
            bi = b[r0:r0 + S]
            vi = v[r0:r0 + S]
            oi = o_inter[r0:r0 + S]
            if i > 0:
                bs = b[r0 - 1:r0]
                qh = (qi * jnp.exp(bi - bs)).astype(BF16)
                kh = (k[:r0] * jnp.exp(bs - b[:r0])).astype(BF16)
                a = lax.dot_general(qh, kh, (((1,), (1,)), ((), ())), preferred_element_type=F32)
                oi = oi + jnp.dot(a.astype(BF16), vb[:r0], preferred_element_type=F32)
            half = S // 2
            o_half = [oi[:half], oi[half:]]
            for s in range(S):
                for hf in range(s // half, 2):
                    rows = slice(hf * half, (hf + 1) * half)
                    dec = jnp.exp(jnp.minimum(bi[rows] - bi[s:s + 1], 0.0))
                    a_s = jnp.sum(qi[rows] * ki[s:s + 1] * dec, axis=-1, keepdims=True)
                    a_s = jnp.where(t_iota[rows] >= s, a_s, 0.0)
                    o_half[hf] = o_half[hf] + a_s * vi[s:s + 1]
            outs.extend(o_half)
        o = jnp.concatenate(outs, axis=0)
        b_end = b[C - 1:C]
        kd = (k * jnp.exp(b_end - b)).astype(BF16)
        upd = lax.dot_general(vb, kd, (((0,), (0,)), ((), ())), preferred_element_type=F32)
        st_ref[h] = st * jnp.exp(b_end) + upd
        o = o * lax.rsqrt(jnp.mean(o * o, axis=-1, keepdims=True) + EPS)
        g = g_ref[:, sl]
        o_ref[:, sl] = (o * gain_ref[:, sl] * (g * jax.nn.sigmoid(g))).astype(o_ref.dtype)


def hgrn2(p0, lb, gain, batch, seq):
    t = p0.shape[0]
    nc = seq // HG_CHUNK
    w = HG_WIDTH

    def col(j):
        return pl.BlockSpec((HG_CHUNK, w), lambda b, c, j=j: (b * nc + c, j))

    return pl.pallas_call(
        _hgrn_kernel,
        grid=(batch, nc),
        in_specs=[col(0), col(1), col(2), col(3),
                  pl.BlockSpec((1, w), lambda b, c: (0, 0)),
                  pl.BlockSpec((1, w), lambda b, c: (0, 0))],
        out_specs=pl.BlockSpec((HG_CHUNK, w), lambda b, c: (b * nc + c, 0)),
        out_shape=jax.ShapeDtypeStruct((t, w), BF16),
        scratch_shapes=[pltpu.VMEM((HG_HEADS, HG_D, HG_D), F32)],
        compiler_params=_cparams(("parallel", "arbitrary")),
        name="hgrn2",
    )(p0, p0, p0, p0, lb, gain)


def _kmean_kernel(k_ref, o_ref):
    o_ref[0] = jnp.mean(k_ref[...].astype(F32), axis=0, keepdims=True)


def moba_kmean(p1, batch, seq):
    nbt = p1.shape[0] // MB_BLOCK
    return pl.pallas_call(
        _kmean_kernel,
        grid=(nbt,),
        in_specs=[pl.BlockSpec((MB_BLOCK, MB_WIDTH), lambda i: (i, 1))],
        out_specs=pl.BlockSpec((1, 1, MB_WIDTH), lambda i: (i, 0, 0)),
        out_shape=jax.ShapeDtypeStruct((nbt, 1, MB_WIDTH), F32),
        compiler_params=_cparams(("parallel",)),
        name="moba_kmean",
    )(p1)


MB_PAIR = 4
MB_PW = MB_PAIR * MB_DH
MB_LG = 128
MB_ONES = 16
MB_VROWS = MB_DH + MB_ONES
MB_VT_ROWS = MB_HEADS * MB_VROWS


def _moba_kernel(q_ref, k_ref, vt_ref, km_ref, bias_ref, o_ref, *scratch, qb0):
    m_ref, l_ref, al_ref, acc_ref, msk_ref, s_ref, p_ref = (
        scratch[i * MB_PAIR:(i + 1) * MB_PAIR] for i in range(7))
    qi = pl.program_id(2) + qb0
    nb = km_ref.shape[0]
    blk = MB_BLOCK
    heads = range(MB_PAIR)
    grp = lambda hh: slice((hh // 2) * MB_LG, (hh // 2 + 1) * MB_LG)
    q = q_ref[...]
    lane = lax.broadcasted_iota(I32, (blk, MB_LG), 1)
    in_head = [(lane < MB_DH) if hh % 2 == 0 else (lane >= MB_DH) for hh in heads]
    qs = q * jnp.asarray(MB_DH ** -0.5, BF16)
    nt = (((1,), (1,)), ((), ()))
    qf = q.astype(F32)
    qht = [jnp.where(in_head[hh], qs[:, grp(hh)].astype(F32), 0.0).T.astype(BF16) for hh in heads]

    n_io = lax.broadcasted_iota(I32, (nb, blk), 0)
    for hh in heads:
        gate = lax.dot_general(km_ref[:, grp(hh)], jnp.where(in_head[hh], qf[:, grp(hh)], 0.0), nt,
                               precision=lax.Precision.HIGHEST, preferred_element_type=F32)
        gate = jnp.where(n_io < qi, gate, NEG_INF)
        chosen = n_io < 0
        for _ in range(MB_TOPK):
            mx = jnp.max(gate, axis=0, keepdims=True)
            ix = jnp.min(jnp.where(gate == mx, n_io, nb), axis=0, keepdims=True)
            hit = n_io == ix
            chosen = chosen | (hit & (mx > NEG_INF))
            gate = jnp.where(hit, NEG_INF, gate)
        msk_ref[hh][...] = jnp.where(chosen, 0.0, NEG_INF)

    vrows = lambda hh: slice(hh * MB_VROWS, (hh + 1) * MB_VROWS)

    def pv_stage(blk_idx):
        vtb = vt_ref[blk_idx]
        r = [jnp.dot(vtb[vrows(hh)], p_ref[hh][...], preferred_element_type=F32) for hh in heads]
        al = [al_ref[hh][...] for hh in heads]
        a_new = [al[hh] * acc_ref[hh][...] + r[hh][:MB_DH] for hh in heads]
        l_new = [al[hh] * l_ref[hh][...] + r[hh][MB_DH:MB_DH + 1] for hh in heads]
        return a_new, l_new

    def store_pv(a_new, l_new):
        for hh in heads:
            acc_ref[hh][...] = a_new[hh]
            l_ref[hh][...] = l_new[hh]

    def softmax_stage():
        s = [s_ref[hh][...] for hh in heads]
        m_old = [m_ref[hh][...] for hh in heads]
        m_new = [jnp.maximum(m_old[hh], jnp.max(s[hh], axis=0, keepdims=True)) for hh in heads]
        alpha = [jnp.exp(m_old[hh] - m_new[hh]) for hh in heads]
        p = [jnp.exp((s[hh] - m_new[hh]).astype(BF16)) for hh in heads]
        return p, alpha, m_new

    def store_softmax(p, alpha, m_new):
        for hh in heads:
            p_ref[hh][...] = p[hh]
            al_ref[hh][...] = alpha[hh]
            m_ref[hh][...] = m_new[hh]

    k_own = k_ref[pl.ds(pl.multiple_of(qi * blk, blk), blk), :]
    key_io = lax.broadcasted_iota(I32, (blk, blk), 0)
    qry_io = lax.broadcasted_iota(I32, (blk, blk), 1)
    for hh in heads:
        s = jnp.dot(k_own[:, grp(hh)], qht[hh], preferred_element_type=F32) + bias_ref[hh, 0]
        s_ref[hh][...] = jnp.where(key_io <= qry_io, s, NEG_INF)
        m_ref[hh][...] = jnp.full((1, blk), NEG_INF, F32)
        l_ref[hh][...] = jnp.zeros((1, blk), F32)
        al_ref[hh][...] = jnp.ones((1, blk), F32)
        acc_ref[hh][...] = jnp.zeros((MB_DH, blk), F32)
        p_ref[hh][...] = jnp.zeros((blk, blk), BF16)

    def step(i, carry, far):
        pv = pv_stage(jnp.where(i <= 1, qi, i - 2))
        sm = softmax_stage()
        kn = k_ref[pl.ds(pl.multiple_of(i * blk, blk), blk), :]
        if far:
            row = [msk_ref[hh][pl.ds(i, 1), :] + bias_ref[hh, MB_BIAS_TILES - 1, 0:1, 0:1] for hh in heads]
            s_next = [jnp.dot(kn[:, grp(hh)], qht[hh], preferred_element_type=F32) + row[hh] for hh in heads]
        else:
            d = qi - i
            s_next = [jnp.dot(kn[:, grp(hh)], qht[hh], preferred_element_type=F32)
                      + bias_ref[hh, d] + msk_ref[hh][pl.ds(i, 1), :] for hh in heads]
        store_pv(*pv)
        for hh in heads:
            s_ref[hh][...] = s_next[hh]
        store_softmax(*sm)
        return carry

    n_far = jnp.maximum(qi - (MB_BIAS_TILES - 2), 0)
    lax.fori_loop(0, n_far, functools.partial(step, far=True), 0)
    lax.fori_loop(n_far, qi, functools.partial(step, far=False), 0)
    pv = pv_stage(jnp.where(qi <= 1, qi, qi - 2))
    sm = softmax_stage()
    store_pv(*pv)
    store_softmax(*sm)
    a_fin, l_fin = pv_stage(jnp.where(qi == 0, qi, qi - 1))
    out_t = jnp.concatenate([a_fin[hh] / l_fin[hh] for hh in heads], axis=0)
    o_ref[...] = out_t.T.astype(o_ref.dtype)


def moba_attention(pqk, vt, km, bias, batch, seq, qb0=0, nqb=None):
    nb = seq // MB_BLOCK
    nqb = nb if nqb is None else nqb
    t = batch * nqb * MB_BLOCK
    groups = MB_WIDTH // MB_PW
    return pl.pallas_call(
        functools.partial(_moba_kernel, qb0=qb0),
        grid=(batch, groups, nqb),
        in_specs=[
            pl.BlockSpec((MB_BLOCK, MB_PW), lambda b, j, i: (b * nb + qb0 + i, j)),
            pl.BlockSpec((seq, MB_PW), lambda b, j, i: (b, groups + j)),
            pl.BlockSpec((nb, MB_PAIR * MB_VROWS, MB_BLOCK), lambda b, j, i: (b, j, 0)),
            pl.BlockSpec((None, nb, MB_PW), lambda b, j, i: (b, 0, j)),
            pl.BlockSpec((MB_PAIR, MB_BIAS_TILES, MB_BLOCK, MB_BLOCK), lambda b, j, i: (j, 0, 0, 0)),
        ],
        out_specs=pl.BlockSpec((MB_BLOCK, MB_PW), lambda b, j, i: (b * nqb + i, j)),
        out_shape=jax.ShapeDtypeStruct((t, MB_WIDTH), BF16),
        scratch_shapes=(
            [pltpu.VMEM((1, MB_BLOCK), F32)] * (3 * MB_PAIR)
            + [pltpu.VMEM((MB_DH, MB_BLOCK), F32)] * MB_PAIR
            + [pltpu.VMEM((nb, MB_BLOCK), F32)] * MB_PAIR
            + [pltpu.VMEM((MB_BLOCK, MB_BLOCK), F32)] * MB_PAIR
            + [pltpu.VMEM((MB_BLOCK, MB_BLOCK), BF16)] * MB_PAIR
        ),
        compiler_params=_cparams(("parallel", "parallel", "arbitrary")),
        name="moba_attn",
    )(pqk, pqk, vt, km, bias)


def _t5_bucket(dist):
    max_exact = REL_BUCKETS // 2
    scaled = jnp.log(jnp.maximum(dist, 1).astype(F32) / max_exact) / math.log(REL_MAX_DIST / max_exact)
    large = jnp.minimum(max_exact + (scaled * (REL_BUCKETS - max_exact)).astype(I32), REL_BUCKETS - 1)
    return jnp.where(dist < max_exact, dist, large)


def moba_bias_tiles(rel_bias):
    blk = MB_BLOCK
    span = 2 * blk - 1
    x = jnp.arange(span) - (blk - 1)
    dist = jnp.maximum(jnp.arange(MB_BIAS_TILES)[:, None] * blk + x[None, :], 0)
    w = rel_bias.astype(F32).T[:, _t5_bucket(dist)]
    h = w.shape[0]
    wp = jnp.pad(w, ((0, 0), (0, 0), (0, 1)))
    a = jnp.broadcast_to(wp[:, :, None, :], (h, MB_BIAS_TILES, blk, span + 1))
    a = a.reshape(h, MB_BIAS_TILES, blk * (span + 1))[:, :, :blk * span]
    return a.reshape(h, MB_BIAS_TILES, blk, span)[:, :, :, blk - 1:]


def _mix_kernel(x_ref, ya_ref, yb_ref, ga_ref, gb_ref, wa_ref, wb_ref, wo_ref, o_ref):
    za = jnp.dot(ya_ref[...], wa_ref[...], preferred_element_type=F32)
    zb = jnp.dot(yb_ref[...], wb_ref[...], preferred_element_type=F32)
    z = jax.nn.sigmoid(ga_ref[...].astype(F32)) * za + jax.nn.sigmoid(gb_ref[...].astype(F32)) * zb
    o_ref[...] = x_ref[...] + jnp.dot(z.astype(BF16), wo_ref[...], preferred_element_type=F32)


def mix_out(x2d, ya, yb, pg, wa, wb, wo, tok0=0, tm=512):
    t = yb.shape[0]
    d = x2d.shape[1]
    w = ya.shape[1]
    b0 = tok0 // tm
    return pl.pallas_call(
        _mix_kernel,
        grid=(t // tm,),
        in_specs=[
            pl.BlockSpec((tm, d), lambda i: (b0 + i, 0)),
            pl.BlockSpec((tm, w), lambda i: (b0 + i, 0)),
            pl.BlockSpec((tm, w), lambda i: (i, 0)),
            pl.BlockSpec((tm, d), lambda i: (b0 + i, 0)),
            pl.BlockSpec((tm, d), lambda i: (b0 + i, 1)),
            pl.BlockSpec((w, d), lambda i: (0, 0)),
            pl.BlockSpec((w, d), lambda i: (0, 0)),
            pl.BlockSpec((d, d), lambda i: (0, 0)),
        ],
        out_specs=pl.BlockSpec((tm, d), lambda i: (i, 0)),
        out_shape=jax.ShapeDtypeStruct((t, d), F32),
        compiler_params=_cparams(("parallel",)),
        name="mix_out",
    )(x2d, ya, yb, pg, pg, wa, wb, wo)


def _mem_kv_kernel(m_ref, g_ref, wk_ref, wv_ref, k_ref, v_ref):
    mn = _rms(m_ref[...], g_ref[...]).astype(BF16)
    k_ref[...] = jnp.dot(mn, wk_ref[...], preferred_element_type=F32).astype(BF16)
    v_ref[...] = jnp.dot(mn, wv_ref[...], preferred_element_type=F32).astype(BF16)


def mem_kv(mem, g, wk, wv):
    b, m, d = mem.shape
    spec = pl.BlockSpec((None, m, d), lambda i: (i, 0, 0))
    wspec = pl.BlockSpec((d, d), lambda i: (0, 0))
    return pl.pallas_call(
        _mem_kv_kernel,
        grid=(b,),
        in_specs=[spec, pl.BlockSpec((1, d), lambda i: (0, 0)), wspec, wspec],
        out_specs=[spec, spec],
        out_shape=[jax.ShapeDtypeStruct((b, m, d), BF16)] * 2,
        compiler_params=_cparams(("parallel",)),
        name="mem_kv",
    )(mem, g, wk, wv)


def _cross_kernel(x_ref, g_ref, wq_ref, k_ref, v_ref, wo_ref, o_ref):
    x = x_ref[...]
    d = x.shape[1]
    dh = d // X_HEADS
    h = _rms(x, g_ref[...]).astype(BF16)
    q = (jnp.dot(h, wq_ref[...], preferred_element_type=F32) * (dh ** -0.5)).astype(BF16)
    outs = []
    for hh in range(X_HEADS):
        sl = slice(hh * dh, (hh + 1) * dh)
        s = lax.dot_general(q[:, sl], k_ref[:, sl], (((1,), (1,)), ((), ())),
                            preferred_element_type=F32)
        p = jnp.exp(s - jnp.max(s, axis=1, keepdims=True))
        l = jnp.sum(p, axis=1, keepdims=True)
        o = jnp.dot(p.astype(BF16), v_ref[:, sl], preferred_element_type=F32) / l
        outs.append(o.astype(BF16))
    o = jnp.concatenate(outs, axis=1)
    o_ref[...] = x + jnp.dot(o, wo_ref[...], preferred_element_type=F32)


def cross_attn(x2d, g, wq, kx, vx, wo, seq, tm=512):
    t, d = x2d.shape
    m = kx.shape[1]
    per_b = seq // tm
    kv = pl.BlockSpec((None, m, d), lambda i: (i // per_b, 0, 0))
    wspec = pl.BlockSpec((d, d), lambda i: (0, 0))
    return pl.pallas_call(
        _cross_kernel,
        grid=(t // tm,),
        in_specs=[pl.BlockSpec((tm, d), lambda i: (i, 0)), pl.BlockSpec((1, d), lambda i: (0, 0)),
                  wspec, kv, kv, wspec],
        out_specs=pl.BlockSpec((tm, d), lambda i: (i, 0)),
        out_shape=jax.ShapeDtypeStruct((t, d), F32),
        compiler_params=_cparams(("parallel",)),
        name="cross_attn",
    )(x2d, g, wq, kx, vx, wo)


def _topk_rows(sc, k):
    n = sc.shape[0]
    io = lax.broadcasted_iota(I32, sc.shape, 0)
    vals, ids = [], []
    for _ in range(k):
        m = jnp.max(sc, axis=0, keepdims=True)
        ix = jnp.min(jnp.where(sc == m, io, n), axis=0, keepdims=True)
        vals.append(m)
        ids.append(ix)
        sc = jnp.where(io == ix, NEG_INF, sc)
    return jnp.concatenate(vals, axis=0), jnp.concatenate(ids, axis=0)


def _pack_bf16_halves(h):
    bits = lax.bitcast_convert_type(h, I32)
    r = bits + 0x7FFF + (lax.shift_right_logical(bits, 16) & 1)
    half = h.shape[1] // 2
    return lax.shift_right_logical(r[:, :half], 16) | (r[:, half:] & HI_MASK)


def _route_kernel(x_ref, g_ref, wq_ref, sk_ref, hp_ref, idx_ref, w_ref, hb_ref, it_ref, wt_ref):
    p = pl.program_id(1)

    @pl.when(p == 0)
    def _():
        h = _rms(x_ref[...], g_ref[...])
        hp_ref[...] = _pack_bf16_halves(h)
        hb_ref[...] = h.astype(BF16)

    qh = jnp.dot(hb_ref[...], wq_ref[...], preferred_element_type=F32)
    tops = []
    for c in range(2):
        seg = qh[:, c * PEER_HALF:(c + 1) * PEER_HALF]
        sc = lax.dot_general(sk_ref[c], seg, (((1,), (1,)), ((), ())),
                             precision=lax.Precision.HIGHEST, preferred_element_type=F32)
        tops.append(_topk_rows(sc, PEER_TOPK))
    (s0, i0), (s1, i1) = tops
    k = PEER_TOPK
    sub = 8
    tm = s0.shape[1]
    r8 = lax.broadcasted_iota(I32, (sub, tm), 0)
    r16 = lax.broadcasted_iota(I32, (k, tm), 0)
    cand_b = [s0[0:1] + s1, s0[1:2] + s1[:sub]]
    cidx_b = [i0[0:1] * PEER_NKEYS + i1, i0[1:2] * PEER_NKEYS + i1[:sub]]
    pos_b = [r16, k + r8]
    for a in range(2, sub):
        keep = r8 < (k // (a + 1))
        cand_b.append(jnp.where(keep, s0[a:a + 1] + s1[:sub], NEG_INF))
        cidx_b.append(i0[a:a + 1] * PEER_NKEYS + i1[:sub])
        pos_b.append(a * k + r8)
    cand_b.append(s0[sub:] + s1[0:1])
    cidx_b.append(i0[sub:] * PEER_NKEYS + i1[0:1])
    pos_b.append((sub + r8) * k)
    cand = jnp.concatenate(cand_b, axis=0)
    cidx = jnp.concatenate(cidx_b, axis=0)
    pos = jnp.concatenate(pos_b, axis=0)
    vals, ids = [], []
    for _ in range(k):
        m = jnp.max(cand, axis=0, keepdims=True)
        px = jnp.min(jnp.where(cand == m, pos, k * k), axis=0, keepdims=True)
        hit = pos == px
        vals.append(m)
        ids.append(jnp.sum(jnp.where(hit, cidx, 0), axis=0, keepdims=True))
        cand = jnp.where(hit, NEG_INF, cand)
    sf = jnp.concatenate(vals, axis=0)
    e = jnp.exp(sf - sf[0:1])
    rows = pl.ds(pl.multiple_of(p * PEER_TOPK, PEER_TOPK), PEER_TOPK)
    wt_ref[rows, :] = e / jnp.sum(e, axis=0, keepdims=True)
    it_ref[rows, :] = jnp.concatenate(ids, axis=0)

    @pl.when(p == pl.num_programs(1) - 1)
    def _():
        idx_ref[...] = it_ref[...].T
        w_ref[...] = wt_ref[...].T


def peer_route(x2d, g, wq, sk, tok0, t, tm=1024):
    d = x2d.shape[1]
    ph = sk.shape[0]
    nsel = ph * PEER_TOPK
    blk0 = tok0 // tm
    return pl.pallas_call(
        _route_kernel,
        grid=(t // tm, ph),
        in_specs=[
            pl.BlockSpec((tm, d), lambda i, p: (blk0 + i, 0)),
            pl.BlockSpec((1, d), lambda i, p: (0, 0)),
            pl.BlockSpec((d, 2 * PEER_HALF), lambda i, p: (0, p)),
            pl.BlockSpec((None, 2, PEER_NKEYS, PEER_HALF), lambda i, p: (p, 0, 0, 0)),
        ],
        out_specs=[
            pl.BlockSpec((tm, d // 2), lambda i, p: (i, 0)),
            pl.BlockSpec((tm, nsel), lambda i, p: (i, 0)),
            pl.BlockSpec((tm, nsel), lambda i, p: (i, 0)),
        ],
        out_shape=[jax.ShapeDtypeStruct((t, d // 2), I32),
                   jax.ShapeDtypeStruct((t, nsel), I32),
                   jax.ShapeDtypeStruct((t, nsel), F32)],
        scratch_shapes=[pltpu.VMEM((tm, d), BF16),
                        pltpu.VMEM((nsel, tm), I32),
                        pltpu.VMEM((nsel, tm), F32)],
        compiler_params=_cparams(("parallel", "arbitrary")),
        name="peer_route",
    )(x2d, g, wq, sk)


def _coef_kernel(w_ref, a_ref, o_ref):
    o_ref[...] = w_ref[...] * jax.nn.gelu(a_ref[...])


def peer_coef(w, act, tm=1024):
    t, n = w.shape
    spec = pl.BlockSpec((tm, n), lambda i: (i, 0))
    return pl.pallas_call(
        _coef_kernel, grid=(t // tm,), in_specs=[spec, spec], out_specs=spec,
        out_shape=jax.ShapeDtypeStruct((t, n), F32),
        compiler_params=_cparams(("parallel",)), name="peer_coef",
    )(w, act)


def _final_kernel(x_ref, y_ref, g_ref, o_ref):
    o_ref[...] = _rms(x_ref[...] + y_ref[...], g_ref[...])


def final_norm(x2d, y, g, tok0, tm=512):
    t, d = y.shape
    blk0 = tok0 // tm
    spec = pl.BlockSpec((tm, d), lambda i: (i, 0))
    return pl.pallas_call(
        _final_kernel, grid=(t // tm,),
        in_specs=[pl.BlockSpec((tm, d), lambda i: (blk0 + i, 0)), spec, pl.BlockSpec((1, d), lambda i: (0, 0))],
        out_specs=spec,
        out_shape=jax.ShapeDtypeStruct((t, d), F32),
        compiler_params=_cparams(("parallel",)), name="final_norm",
    )(x2d, y, g)


SC_CORES = 2
SC_SUBCORES = 16
SC_WORKERS = SC_CORES * SC_SUBCORES
SC_LANES = 16
SC_GROUP = 16


def _sc_mesh():
    return plsc.VectorSubcoreMesh(core_axis_name="c", subcore_axis_name="s")


def _sc_params():
    return pltpu.CompilerParams(needs_layout_passes=False)


def _sc_worker_id():
    return lax.axis_index("s") * SC_CORES + lax.axis_index("c")


SC_RING = 4
SC_ROW_SUB = 8
SC_ROW_LANE = 128


def _sc_ring(n_units, start, wait, compute):
    for u in range(SC_RING - 1):
        start(u, u)

    @pl.loop(0, n_units, step=SC_RING)
    def _(uu):
        for b in range(SC_RING):
            u = uu + b
            nxt = u + (SC_RING - 1)

            @pl.when(nxt < n_units)
            def _():
                start(nxt, (b + SC_RING - 1) % SC_RING)

            wait(u, b)
            compute(u, b)


def _sc_unit_off(u):
    off = u * SC_LANES
    return off if isinstance(off, int) else pl.multiple_of(off, SC_LANES)


def _sc_row_piece(rows, r, c):
    per = SC_ROW_LANE // SC_LANES
    return rows[r, c // per, pl.ds(pl.multiple_of((c % per) * SC_LANES, SC_LANES), SC_LANES)]


def peer_dots_sc(table, idx_flat, h):
    t, d = h.shape
    nsel = PEER_SEL
    tpw = t // SC_WORKERS
    g = SC_GROUP
    groups = tpw // g
    heads = nsel // SC_LANES
    pieces = d // SC_LANES
    units = g * heads
    row_buf = pltpu.VMEM((SC_LANES, SC_ROW_SUB, SC_ROW_LANE), F32)

    @functools.partial(
        pl.kernel, mesh=_sc_mesh(),
        out_type=jax.ShapeDtypeStruct((t * nsel,), F32),
        scratch_types=[
            pltpu.VMEM((g * nsel,), I32),
            pltpu.VMEM((g, d), F32),
            pltpu.VMEM((g * nsel,), F32),
            pltpu.VMEM((SC_LANES * SC_LANES,), F32),
            [row_buf] * SC_RING,
            [pltpu.SemaphoreType.DMA] * SC_RING,
        ],
        compiler_params=_sc_params(),
        name="peer_dots_sc",
    )
    def k(tab_hbm, idx_hbm, h_hbm, out_hbm, idx_v, h_v, out_v, red_v, rows, sems):
        wid = _sc_worker_id()
        lane = lax.iota(I32, SC_LANES)

        def copy(u, slot):
            ids = idx_v.at[pl.ds(_sc_unit_off(u), SC_LANES)]
            return pltpu.make_async_copy(tab_hbm.at[ids], rows[slot], sems[slot])

        def compute(u, slot):
            tt = u // heads

            def body(c, accs):
                hv = h_v[tt, pl.ds(pl.multiple_of(c * SC_LANES, SC_LANES), SC_LANES)]
                return tuple(accs[r] + _sc_row_piece(rows[slot], r, c) * hv for r in range(SC_LANES))

            accs = lax.fori_loop(0, pieces, body,
                                 tuple(jnp.zeros((SC_LANES,), F32) for _ in range(SC_LANES)))
            for r in range(SC_LANES):
                red_v[pl.ds(r * SC_LANES, SC_LANES)] = accs[r]
            cols = [plsc.load_gather(red_v, [lane * SC_LANES + j]) for j in range(SC_LANES)]
            while len(cols) > 1:
                cols = [cols[i] + cols[i + 1] for i in range(0, len(cols), 2)]
            out_v[pl.ds(_sc_unit_off(u), SC_LANES)] = cols[0]

        @pl.loop(0, groups)
        def _(gi):
            base = wid * tpw + gi * g
            pltpu.sync_copy(idx_hbm.at[pl.ds(base * nsel, g * nsel)], idx_v)
            pltpu.sync_copy(h_hbm.at[pl.ds(base, g)], h_v)
            _sc_ring(units, lambda u, s: copy(u, s).start(), lambda u, s: copy(u, s).wait(), compute)
            pltpu.sync_copy(out_v, out_hbm.at[pl.ds(base * nsel, g * nsel)])

    return k(table, idx_flat, h)


def peer_combine_sc(table, idx_flat, coef_flat, t):
    d = table.shape[1] * table.shape[2]
    nsel = PEER_SEL
    tpw = t // SC_WORKERS
    g = SC_GROUP
    groups = tpw // g
    heads = nsel // SC_LANES
    pieces = d // SC_LANES
    units = g * heads
    row_buf = pltpu.VMEM((SC_LANES, SC_ROW_SUB, SC_ROW_LANE), F32)

    @functools.partial(
        pl.kernel, mesh=_sc_mesh(),
        out_type=jax.ShapeDtypeStruct((t, d), F32),
        scratch_types=[
            pltpu.VMEM((g * nsel,), I32),
            pltpu.VMEM((g * nsel,), F32),
            pltpu.VMEM((g, d), F32),
            [row_buf] * SC_RING,
            [pltpu.SemaphoreType.DMA] * SC_RING,
        ],
        compiler_params=_sc_params(),
        name="peer_combine_sc",
    )
    def k(tab_hbm, idx_hbm, coef_hbm, out_hbm, idx_v, coef_v, y_v, rows, sems):
        wid = _sc_worker_id()

        def copy(u, slot):
            ids = idx_v.at[pl.ds(_sc_unit_off(u), SC_LANES)]
            return pltpu.make_async_copy(tab_hbm.at[ids], rows[slot], sems[slot])

        def compute(u, slot):
            tt = u // heads
            first = (u % heads) == 0
            cs = [plsc.load_gather(coef_v, [jnp.full((SC_LANES,), u * SC_LANES + r, I32)])
                  for r in range(SC_LANES)]

            @plsc.parallel_loop(0, pieces, unroll=2)
            def _(c):
                off = pl.multiple_of(c * SC_LANES, SC_LANES)
                terms = [cs[r] * _sc_row_piece(rows[slot], r, c) for r in range(SC_LANES)]
                while len(terms) > 1:
                    terms = [terms[i] + terms[i + 1] for i in range(0, len(terms), 2)]
                prev = y_v[tt, pl.ds(off, SC_LANES)]
                y_v[tt, pl.ds(off, SC_LANES)] = terms[0] + jnp.where(first, 0.0, prev)

        @pl.loop(0, groups)
        def _(gi):
            base = wid * tpw + gi * g
            pltpu.sync_copy(idx_hbm.at[pl.ds(base * nsel, g * nsel)], idx_v)
            pltpu.sync_copy(coef_hbm.at[pl.ds(base * nsel, g * nsel)], coef_v)
            _sc_ring(units, lambda u, s: copy(u, s).start(), lambda u, s: copy(u, s).wait(), compute)
            pltpu.sync_copy(y_v, out_hbm.at[pl.ds(base, g)])

    return k(table, idx_flat, coef_flat)


GELU_C0 = math.sqrt(2.0 / math.pi)
GELU_C1 = 0.044715


def _gelu_tanh(x):
    z = GELU_C0 * (x + GELU_C1 * (x * x * x))
    th = 1.0 - 2.0 / (jnp.exp(2.0 * z) + 1.0)
    return 0.5 * x * (1.0 + th)


def peer_experts_sc(tab_u, tab_v, idx_flat, w_flat, h):
    t, d = h.shape
    nsel = PEER_SEL
    tpw = t // SC_WORKERS
    g = SC_GROUP
    groups = tpw // g
    heads = nsel // SC_LANES
    pieces = d // SC_LANES
    units = g * heads
    row_buf = pltpu.VMEM((SC_LANES, SC_ROW_SUB, SC_ROW_LANE), F32)

    @functools.partial(
        pl.kernel, mesh=_sc_mesh(),
        out_type=jax.ShapeDtypeStruct((t, d), F32),
        scratch_types=[
            pltpu.VMEM((g * nsel,), I32),
            pltpu.VMEM((g * nsel,), F32),
            pltpu.VMEM((g, d), F32),
            pltpu.VMEM((g, d), F32),
            pltpu.VMEM((SC_LANES * SC_LANES,), F32),
            [row_buf] * SC_RING,
            [pltpu.SemaphoreType.DMA] * SC_RING,
        ],
        compiler_params=_sc_params(),
        name="peer_experts_sc",
    )
    def k(u_hbm, v_hbm, idx_hbm, w_hbm, h_hbm, out_hbm, idx_v, coef_v, h_v, y_v, red_v, rows, sems):
        wid = _sc_worker_id()
        lane = lax.iota(I32, SC_LANES)

        def copy(tab_hbm, u, slot):
            ids = idx_v.at[pl.ds(_sc_unit_off(u), SC_LANES)]
            return pltpu.make_async_copy(tab_hbm.at[ids], rows[slot], sems[slot])

        def dots(u, slot):
            tt = u // heads

            def body(c, accs):
                hv = h_v[tt, pl.ds(pl.multiple_of(c * SC_LANES, SC_LANES), SC_LANES)]
                return tuple(accs[r] + _sc_row_piece(rows[slot], r, c) * hv for r in range(SC_LANES))

            accs = lax.fori_loop(0, pieces, body,
                                 tuple(jnp.zeros((SC_LANES,), F32) for _ in range(SC_LANES)))
            for r in range(SC_LANES):
                red_v[pl.ds(r * SC_LANES, SC_LANES)] = accs[r]
            cols = [plsc.load_gather(red_v, [lane * SC_LANES + j]) for j in range(SC_LANES)]
            while len(cols) > 1:
                cols = [cols[i] + cols[i + 1] for i in range(0, len(cols), 2)]
            sl = pl.ds(_sc_unit_off(u), SC_LANES)
            coef_v[sl] = coef_v[sl] * _gelu_tanh(cols[0])

        def combine(u, slot):
            tt = u // heads
            first = (u % heads) == 0
            cs = [plsc.load_gather(coef_v, [jnp.full((SC_LANES,), u * SC_LANES + r, I32)])
                  for r in range(SC_LANES)]

            @plsc.parallel_loop(0, pieces, unroll=2)
            def _(c):
                off = pl.multiple_of(c * SC_LANES, SC_LANES)
                terms = [cs[r] * _sc_row_piece(rows[slot], r, c) for r in range(SC_LANES)]
                while len(terms) > 1:
                    terms = [terms[i] + terms[i + 1] for i in range(0, len(terms), 2)]
                prev = y_v[tt, pl.ds(off, SC_LANES)]
                y_v[tt, pl.ds(off, SC_LANES)] = terms[0] + jnp.where(first, 0.0, prev)

        @pl.loop(0, groups)
        def _(gi):
            base = wid * tpw + gi * g
            pltpu.sync_copy(idx_hbm.at[pl.ds(base * nsel, g * nsel)], idx_v)
            pltpu.sync_copy(w_hbm.at[pl.ds(base * nsel, g * nsel)], coef_v)
            pltpu.sync_copy(h_hbm.at[pl.ds(base, g)], h_v)
            _sc_ring(units, lambda u, s: copy(u_hbm, u, s).start(), lambda u, s: copy(u_hbm, u, s).wait(), dots)
            _sc_ring(units, lambda u, s: copy(v_hbm, u, s).start(), lambda u, s: copy(v_hbm, u, s).wait(), combine)
            pltpu.sync_copy(y_v, out_hbm.at[pl.ds(base, g)])

    return k(tab_u, tab_v, idx_flat, w_flat, h)


SC_PK_RING = 4
SC_PK_SUB = 4
HI_MASK = -65536


def pack_bf16_pairs(a):
    half = a.shape[1] // 2
    bits = lax.bitcast_convert_type(a.astype(BF16), jnp.uint16).astype(jnp.uint32)
    return lax.bitcast_convert_type(bits[:, :half] | (bits[:, half:] << 16), I32)


def _unpack_halves(x32):
    w = plsc.bitcast(x32, I32)
    return plsc.bitcast(w << 16, F32), plsc.bitcast(w & HI_MASK, F32)


def _tree_sum(xs):
    while len(xs) > 1:
        xs = [xs[i] + xs[i + 1] for i in range(0, len(xs), 2)]
    return xs[0]


def peer_experts_pk_sc(tab_uv, idx_flat, w_flat, hp, d):
    t = hp.shape[0]
    nsel = PEER_SEL
    tpw = t // SC_WORKERS
    g = SC_GROUP
    groups = tpw // g
    heads = nsel // SC_LANES
    chunks = d // 32
    units = g * heads
    ring = SC_PK_RING
    row_buf = pltpu.VMEM((SC_LANES, 2 * SC_PK_SUB, SC_ROW_LANE), I32)

    def row_words(rows, r, wc, sub0):
        per = SC_ROW_LANE // SC_LANES
        return plsc.bitcast(
            rows[r, sub0 + wc // per, pl.ds(pl.multiple_of((wc % per) * SC_LANES, SC_LANES), SC_LANES)], BF16)

    def ring_loop(n_units, start, wait, compute):
        for u in range(ring - 1):
            start(u, u)

        @pl.loop(0, n_units, step=ring)
        def _(uu):
            for b in range(ring):
                u = uu + b
                nxt = u + (ring - 1)

                @pl.when(nxt < n_units)
                def _():
                    start(nxt, (b + ring - 1) % ring)

                wait(u, b)
                compute(u, b)

    @functools.partial(
        pl.kernel, mesh=_sc_mesh(),
        out_type=jax.ShapeDtypeStruct((t, d), F32),
        scratch_types=[
            pltpu.VMEM((g * nsel,), I32),
            pltpu.VMEM((g * nsel,), F32),
            pltpu.VMEM((g, d // 2), I32),
            pltpu.VMEM((g, d), F32),
            pltpu.VMEM((SC_LANES * SC_LANES,), F32),
            [row_buf] * ring,
            [pltpu.SemaphoreType.DMA] * ring,
        ],
        compiler_params=_sc_params(),
        name="peer_experts_pk_sc",
    )
    def k(tab_hbm, idx_hbm, w_hbm, h_hbm, out_hbm, idx_v, coef_v, h_v, y_v, red_v, rows, sems):
        wid = _sc_worker_id()
        lane = lax.iota(I32, SC_LANES)

        def copy(u, slot):
            ids = idx_v.at[pl.ds(_sc_unit_off(u), SC_LANES)]
            return pltpu.make_async_copy(tab_hbm.at[ids], rows[slot], sems[slot])

        def dots(u, slot):
            tt = u // heads

            def body(cp, accs):
                out = []
                hv = [plsc.bitcast(h_v[tt, pl.ds(pl.multiple_of((2 * cp + i) * SC_LANES, SC_LANES), SC_LANES)], BF16)
                      for i in range(2)]
                for r in range(SC_LANES):
                    pr = (row_words(rows[slot], r, 2 * cp, 0) * hv[0]
                          + row_words(rows[slot], r, 2 * cp + 1, 0) * hv[1])
                    lo, hi = _unpack_halves(pr)
                    out.append(accs[r] + lo + hi)
                return tuple(out)

            accs = lax.fori_loop(0, chunks // 2, body,
                                 tuple(jnp.zeros((SC_LANES,), F32) for _ in range(SC_LANES)))
            for r in range(SC_LANES):
                red_v[pl.ds(r * SC_LANES, SC_LANES)] = accs[r]
            act = _tree_sum([plsc.load_gather(red_v, [lane * SC_LANES + j]) for j in range(SC_LANES)])
            sl = pl.ds(_sc_unit_off(u), SC_LANES)
            coef_v[sl] = coef_v[sl] * _gelu_tanh(act)

        def combine(u, slot):
            tt = u // heads
            first = (u % heads) == 0
            cb = []
            for r in range(SC_LANES):
                c = plsc.load_gather(coef_v, [jnp.full((SC_LANES,), u * SC_LANES + r, I32)])
                cb.append(plsc.pack(c, c, format=plsc.PackFormat.INTERLEAVED))

            @plsc.parallel_loop(0, chunks, unroll=2)
            def _(wc):
                lo, hi = _unpack_halves(
                    _tree_sum([cb[r] * row_words(rows[slot], r, wc, SC_PK_SUB) for r in range(SC_LANES)]))
                for half, val in ((0, lo), (1, hi)):
                    sl = pl.ds(pl.multiple_of(half * (d // 2) + wc * SC_LANES, SC_LANES), SC_LANES)
                    y_v[tt, sl] = val + jnp.where(first, 0.0, y_v[tt, sl])

        def unit(u, slot):
            dots(u, slot)
            combine(u, slot)

        @pl.loop(0, groups)
        def _(gi):
            base = wid * tpw + gi * g
            pltpu.sync_copy(idx_hbm.at[pl.ds(base * nsel, g * nsel)], idx_v)
            pltpu.sync_copy(w_hbm.at[pl.ds(base * nsel, g * nsel)], coef_v)
            pltpu.sync_copy(h_hbm.at[pl.ds(base, g)], h_v)
            ring_loop(units, lambda u, s: copy(u, s).start(), lambda u, s: copy(u, s).wait(), unit)
            pltpu.sync_copy(y_v, out_hbm.at[pl.ds(base, g)])

    return k(tab_uv, idx_flat, w_flat, hp)


def kernel(x, mem, rel_bias, ln_mix, w_in, hg_lower, hg_norm, w_up_a, w_up_b, w_out, ln_cross, ln_mem, wq_x, wk_x, wv_x, wo_x, ln_ffn, peer_query, peer_subkeys, peer_u, peer_v, ln_final):
    b, s, d = x.shape
    depth = w_in.shape[0]
    assert depth == 1, "the residual after PEER is fused into the final norm"
    assert s % MB_BLOCK == 0 and s % HG_CHUNK == 0 and s % (PEER_SLICES * SC_WORKERS * SC_GROUP) == 0
    nb = s // MB_BLOCK
    row = lambda a: a.reshape(1, -1).astype(F32)
    lb_all = jnp.cumsum(jax.nn.softmax(hg_lower.astype(F32), axis=0), axis=0)
    bias = moba_bias_tiles(rel_bias)
    n_hg = 4 * HG_WIDTH
    n_qk = 2 * MB_WIDTH
    n_mb = 3 * MB_WIDTH
    l = 0
    w = w_in[l].astype(BF16)
    w_hg, w_qk, w_vt, w_g = w[:, :n_hg], w[:, n_hg:n_hg + n_qk], w[:, n_hg + n_qk:n_hg + n_mb].T, w[:, n_hg + n_mb:]
    wa, wb, wo = w_up_a[l].astype(BF16), w_up_b[l].astype(BF16), w_out[l].astype(BF16)
    wqx, wox = wq_x[l].astype(BF16), wo_x[l].astype(BF16)
    wpq, sk = peer_query[l].astype(BF16), peer_subkeys[l].astype(F32)
    tab3 = lambda a: pack_bf16_pairs(a.astype(F32)).reshape(a.shape[0], SC_PK_SUB, SC_ROW_LANE)
    tab_uv = jnp.concatenate([tab3(peer_u[l]), tab3(peer_v[l])], axis=1)
    kx, vx = mem_kv(mem, row(ln_mem[l]), wk_x[l].astype(BF16), wv_x[l].astype(BF16))

    outs = []
    for bi in range(b):
        x2d = x[bi]
        p0, pqk, vt, pg = in_proj(x2d, row(ln_mix[l]), w_hg, w_qk, w_vt, w_g)
        ya = hgrn2(p0, row(lb_all[l]), row(hg_norm[l]), 1, s)
        km = moba_kmean(pqk, 1, s).reshape(1, nb, MB_WIDTH)
        ts = s // PEER_SLICES
        for tok0 in range(0, s, ts):
            yb = moba_attention(pqk, vt, km, bias, 1, s, tok0 // MB_BLOCK, ts // MB_BLOCK)
            xs = mix_out(x2d, ya, yb, pg, wa, wb, wo, tok0)
            xs = cross_attn(xs, row(ln_cross[l]), wqx, kx[bi:bi + 1], vx[bi:bi + 1], wox, ts)
            hp, eidx, wts = peer_route(xs, row(ln_ffn[l]), wpq, sk, 0, ts)
            y = peer_experts_pk_sc(tab_uv, eidx.reshape(ts * PEER_SEL), wts.reshape(ts * PEER_SEL), hp, d)
            outs.append(final_norm(xs, y, row(ln_final), 0))
    return jnp.concatenate(outs, axis=0).reshape(b, s, d)
```

```python
import functools
import math

import jax
import jax.numpy as jnp
import numpy as np
from jax import lax
from jax.experimental import pallas as pl
from jax.experimental.pallas import tpu as pltpu
from jax.experimental.pallas import tpu_sc as plsc

F32 = jnp.float32
BF16 = jnp.bfloat16
I32 = jnp.int32
EPS = 1e-6
NEG_INF = float("-inf")

HG_HEADS = 4
HG_D = 128
HG_WIDTH = HG_HEADS * HG_D
HG_CHUNK = 64
HG_SUB = 16
MB_HEADS = 8
MB_DH = 64
MB_WIDTH = MB_HEADS * MB_DH
MB_BLOCK = 256
MB_TOPK = 3
MB_BIAS_TILES = 8
REL_BUCKETS = 32
REL_MAX_DIST = 2048
X_HEADS = 4
PEER_HEADS = 8
PEER_NKEYS = 128
PEER_TOPK = 16
PEER_HALF = 128
PEER_SEL = PEER_HEADS * PEER_TOPK
PEER_SLICES = 4

VMEM_LIMIT = 56 * 1024 * 1024


def _cparams(sem):
    return pltpu.CompilerParams(dimension_semantics=sem, vmem_limit_bytes=VMEM_LIMIT)


def _rms(x, g):
    ms = jnp.mean(x * x, axis=-1, keepdims=True)
    return x * lax.rsqrt(ms + EPS) * g


def _in_proj_kernel(x_ref, g_ref, w0_ref, w1_ref, wvt_ref, w2_ref, o0_ref, o1_ref, ovt_ref, o2_ref):
    h = _rms(x_ref[...], g_ref[...]).astype(BF16)
    o0_ref[...] = jnp.dot(h, w0_ref[...], preferred_element_type=F32)
    o1_ref[...] = jnp.dot(h, w1_ref[...], preferred_element_type=F32).astype(BF16)
    vt = lax.dot_general(wvt_ref[...], h, (((1,), (1,)), ((), ())), preferred_element_type=F32).astype(BF16)
    for hd in range(MB_HEADS):
        ovt_ref[0, hd * MB_VROWS:hd * MB_VROWS + MB_DH, :] = vt[hd * MB_DH:(hd + 1) * MB_DH]
        ovt_ref[0, hd * MB_VROWS + MB_DH:(hd + 1) * MB_VROWS, :] = jnp.ones((MB_ONES, vt.shape[1]), BF16)
    o2_ref[...] = jnp.dot(h, w2_ref[...], preferred_element_type=F32).astype(BF16)


def in_proj(x2d, g, w0, w1, wvt, w2):
    t, d = x2d.shape
    tm = MB_BLOCK
    assert wvt.shape[0] == MB_WIDTH
    n0, n1, nv, n2 = w0.shape[1], w1.shape[1], MB_VT_ROWS, w2.shape[1]
    full = lambda a: pl.BlockSpec(a.shape, lambda i: (0, 0))
    return pl.pallas_call(
        _in_proj_kernel,
        grid=(t // tm,),
        in_specs=[pl.BlockSpec((tm, d), lambda i: (i, 0)), full(g), full(w0), full(w1), full(wvt), full(w2)],
        out_specs=[pl.BlockSpec((tm, n0), lambda i: (i, 0)),
                   pl.BlockSpec((tm, n1), lambda i: (i, 0)),
                   pl.BlockSpec((1, nv, tm), lambda i: (i, 0, 0)),
                   pl.BlockSpec((tm, n2), lambda i: (i, 0))],
        out_shape=[jax.ShapeDtypeStruct((t, n0), F32),
                   jax.ShapeDtypeStruct((t, n1), BF16),
                   jax.ShapeDtypeStruct((t // tm, nv, tm), BF16),
                   jax.ShapeDtypeStruct((t, n2), BF16)],
        compiler_params=_cparams(("parallel",)),
        name="in_proj",
    )(x2d, g, w0, w1, wvt, w2)


def _hgrn_kernel(q_ref, f_ref, i_ref, g_ref, lb_ref, gain_ref, o_ref, st_ref):
    c = pl.program_id(1)

    @pl.when(c == 0)
    def _():
        st_ref[...] = jnp.zeros_like(st_ref)

    C, S = HG_CHUNK, HG_SUB
    row = lax.broadcasted_iota(I32, (C, C), 0)
    col = lax.broadcasted_iota(I32, (C, C), 1)
    tril = (row >= col).astype(F32)
    t_iota = lax.broadcasted_iota(I32, (S, 1), 0)

    for h in range(HG_HEADS):
        sl = slice(h * HG_D, (h + 1) * HG_D)
        q = q_ref[:, sl]
        v = i_ref[:, sl]
        lb = lb_ref[:, sl]
        f = lb + (1.0 - lb) * jax.nn.sigmoid(f_ref[:, sl])
        lf = jnp.log(f)
        k = 1.0 - f
        b = jnp.dot(tril, lf, precision=lax.Precision.HIGHEST, preferred_element_type=F32)
        st = st_ref[h]
        vb = v.astype(BF16)
        qd = (q * jnp.exp(b)).astype(BF16)
        o_inter = lax.dot_general(qd, st.astype(BF16), (((1,), (1,)), ((), ())),
                                  preferred_element_type=F32)
        outs = []
        for i in range(C // S):
            r0 = i * S
            qi = q[r0:r0 + S]
            ki = k[r0:r0 + S]
            bi = b[r0:r0 + S]
            vi = v[r0:r0 + S]
            oi = o_inter[r0:r0 + S]
            if i > 0:
                bs = b[r0 - 1:r0]
                qh = (qi * jnp.exp(bi - bs)).astype(BF16)
                kh = (k[:r0] * jnp.exp(bs - b[:r0])).astype(BF16)
                a = lax.dot_general(qh, kh, (((1,), (1,)), ((), ())), preferred_element_type=F32)
                oi = oi + jnp.dot(a.astype(BF16), vb[:r0], preferred_element_type=F32)
            half = S // 2
            o_half = [oi[:half], oi[half:]]
            for s in range(S):
                for hf in range(s // half, 2):
                    rows = slice(hf * half, (hf + 1) * half)
                    dec = jnp.exp(jnp.minimum(bi[rows] - bi[s:s + 1], 0.0))
                    a_s = jnp.sum(qi[rows] * ki[s:s + 1] * dec, axis=-1, keepdims=True)
                    a_s = jnp.where(t_iota[rows] >= s, a_s, 0.0)
                    o_half[hf] = o_half[hf] + a_s * vi[s:s + 1]
            outs.extend(o_half)
        o = jnp.concatenate(outs, axis=0)
        b_end = b[C - 1:C]
        kd = (k * jnp.exp(b_end - b)).astype(BF16)
        upd = lax.dot_general(vb, kd, (((0,), (0,)), ((), ())), preferred_element_type=F32)
        st_ref[h] = st * jnp.exp(b_end) + upd
        o = o * lax.rsqrt(jnp.mean(o * o, axis=-1, keepdims=True) + EPS)
        g = g_ref[:, sl]
        o_ref[:, sl] = (o * gain_ref[:, sl] * (g * jax.nn.sigmoid(g))).astype(o_ref.dtype)


def hgrn2(p0, lb, gain, batch, seq):
    t = p0.shape[0]
    nc = seq // HG_CHUNK
    w = HG_WIDTH

    def col(j):
        return pl.BlockSpec((HG_CHUNK, w), lambda b, c, j=j: (b * nc + c, j))

    return pl.pallas_call(
        _hgrn_kernel,
        grid=(batch, nc),
        in_specs=[col(0), col(1), col(2), col(3),
                  pl.BlockSpec((1, w), lambda b, c: (0, 0)),
                  pl.BlockSpec((1, w), lambda b, c: (0, 0))],
        out_specs=pl.BlockSpec((HG_CHUNK, w), lambda b, c: (b * nc + c, 0)),
        out_shape=jax.ShapeDtypeStruct((t, w), BF16),
        scratch_shapes=[pltpu.VMEM((HG_HEADS, HG_D, HG_D), F32)],
        compiler_params=_cparams(("parallel", "arbitrary")),
        name="hgrn2",
    )(p0, p0, p0, p0, lb, gain)


def _kmean_kernel(k_ref, o_ref):
    o_ref[0] = jnp.mean(k_ref[...].astype(F32), axis=0, keepdims=True)


def moba_kmean(p1, batch, seq):
    nbt = p1.shape[0] // MB_BLOCK
    return pl.pallas_call(
        _kmean_kernel,
        grid=(nbt,),
        in_specs=[pl.BlockSpec((MB_BLOCK, MB_WIDTH), lambda i: (i, 1))],
        out_specs=pl.BlockSpec((1, 1, MB_WIDTH), lambda i: (i, 0, 0)),
        out_shape=jax.ShapeDtypeStruct((nbt, 1, MB_WIDTH), F32),
        compiler_params=_cparams(("parallel",)),
        name="moba_kmean",
    )(p1)


MB_PAIR = 4
MB_PW = MB_PAIR * MB_DH
MB_LG = 128
MB_ONES = 16
MB_VROWS = MB_DH + MB_ONES
MB_VT_ROWS = MB_HEADS * MB_VROWS


def _moba_kernel(q_ref, k_ref, vt_ref, km_ref, bias_ref, o_ref, *scratch, qb0):
    m_ref, l_ref, al_ref, acc_ref, msk_ref, s_ref, p_ref = (
        scratch[i * MB_PAIR:(i + 1) * MB_PAIR] for i in range(7))
    qi = pl.program_id(2) + qb0
    nb = km_ref.shape[0]
    blk = MB_BLOCK
    heads = range(MB_PAIR)
    grp = lambda hh: slice((hh // 2) * MB_LG, (hh // 2 + 1) * MB_LG)
    q = q_ref[...]
    lane = lax.broadcasted_iota(I32, (blk, MB_LG), 1)
    in_head = [(lane < MB_DH) if hh % 2 == 0 else (lane >= MB_DH) for hh in heads]
    qs = q * jnp.asarray(MB_DH ** -0.5, BF16)
    nt = (((1,), (1,)), ((), ()))
    qf = q.astype(F32)
    qht = [jnp.where(in_head[hh], qs[:, grp(hh)].astype(F32), 0.0).T.astype(BF16) for hh in heads]

    n_io = lax.broadcasted_iota(I32, (nb, blk), 0)
    for hh in heads:
        gate = lax.dot_general(km_ref[:, grp(hh)], jnp.where(in_head[hh], qf[:, grp(hh)], 0.0), nt,
                               precision=lax.Precision.HIGHEST, preferred_element_type=F32)
        gate = jnp.where(n_io < qi, gate, NEG_INF)
        chosen = n_io < 0
        for _ in range(MB_TOPK):
            mx = jnp.max(gate, axis=0, keepdims=True)
            ix = jnp.min(jnp.where(gate == mx, n_io, nb), axis=0, keepdims=True)
            hit = n_io == ix
            chosen = chosen | (hit & (mx > NEG_INF))
            gate = jnp.where(hit, NEG_INF, gate)
        msk_ref[hh][...] = jnp.where(chosen, 0.0, NEG_INF)

    vrows = lambda hh: slice(hh * MB_VROWS, (hh + 1) * MB_VROWS)

    def pv_stage(blk_idx):
        vtb = vt_ref[blk_idx]
        r = [jnp.dot(vtb[vrows(hh)], p_ref[hh][...], preferred_element_type=F32) for hh in heads]
        al = [al_ref[hh][...] for hh in heads]
        a_new = [al[hh] * acc_ref[hh][...] + r[hh][:MB_DH] for hh in heads]
        l_new = [al[hh] * l_ref[hh][...] + r[hh][MB_DH:MB_DH + 1] for hh in heads]
        return a_new, l_new

    def store_pv(a_new, l_new):
        for hh in heads:
            acc_ref[hh][...] = a_new[hh]
            l_ref[hh][...] = l_new[hh]

    def softmax_stage():
        s = [s_ref[hh][...] for hh in heads]
        m_old = [m_ref[hh][...] for hh in heads]
        m_new = [jnp.maximum(m_old[hh], jnp.max(s[hh], axis=0, keepdims=True)) for hh in heads]
        alpha = [jnp.exp(m_old[hh] - m_new[hh]) for hh in heads]
        p = [jnp.exp((s[hh] - m_new[hh]).astype(BF16)) for hh in heads]
        return p, alpha, m_new

    def store_softmax(p, alpha, m_new):
        for hh in heads:
            p_ref[hh][...] = p[hh]
            al_ref[hh][...] = alpha[hh]
            m_ref[hh][...] = m_new[hh]

    k_own = k_ref[pl.ds(pl.multiple_of(qi * blk, blk), blk), :]
    key_io = lax.broadcasted_iota(I32, (blk, blk), 0)
    qry_io = lax.broadcasted_iota(I32, (blk, blk), 1)
    for hh in heads:
        s = jnp.dot(k_own[:, grp(hh)], qht[hh], preferred_element_type=F32) + bias_ref[hh, 0]
        s_ref[hh][...] = jnp.where(key_io <= qry_io, s, NEG_INF)
        m_ref[hh][...] = jnp.full((1, blk), NEG_INF, F32)
        l_ref[hh][...] = jnp.zeros((1, blk), F32)
        al_ref[hh][...] = jnp.ones((1, blk), F32)
        acc_ref[hh][...] = jnp.zeros((MB_DH, blk), F32)
        p_ref[hh][...] = jnp.zeros((blk, blk), BF16)

    def step(i, carry, far):
        pv = pv_stage(jnp.where(i <= 1, qi, i - 2))
        sm = softmax_stage()
        kn = k_ref[pl.ds(pl.multiple_of(i * blk, blk), blk), :]
        if far:
            row = [msk_ref[hh][pl.ds(i, 1), :] + bias_ref[hh, MB_BIAS_TILES - 1, 0:1, 0:1] for hh in heads]
            s_next = [jnp.dot(kn[:, grp(hh)], qht[hh], preferred_element_type=F32) + row[hh] for hh in heads]
        else:
            d = qi - i
            s_next = [jnp.dot(kn[:, grp(hh)], qht[hh], preferred_element_type=F32)
                      + bias_ref[hh, d] + msk_ref[hh][pl.ds(i, 1), :] for hh in heads]
        store_pv(*pv)
        for hh in heads:
            s_ref[hh][...] = s_next[hh]
        store_softmax(*sm)
        return carry

    n_far = jnp.maximum(qi - (MB_BIAS_TILES - 2), 0)
    lax.fori_loop(0, n_far, functools.partial(step, far=True), 0)
    lax.fori_loop(n_far, qi, functools.partial(step, far=False), 0)
    pv = pv_stage(jnp.where(qi <= 1, qi, qi - 2))
    sm = softmax_stage()
    store_pv(*pv)
    store_softmax(*sm)
    a_fin, l_fin = pv_stage(jnp.where(qi == 0, qi, qi - 1))
    out_t = jnp.concatenate([a_fin[hh] / l_fin[hh] for hh in heads], axis=0)
    o_ref[...] = out_t.T.astype(o_ref.dtype)


def moba_attention(pqk, vt, km, bias, batch, seq, qb0=0, nqb=None):
    nb = seq // MB_BLOCK
    nqb = nb if nqb is None else nqb
    t = batch * nqb * MB_BLOCK
    groups = MB_WIDTH // MB_PW
    return pl.pallas_call(
        functools.partial(_moba_kernel, qb0=qb0),
        grid=(batch, groups, nqb),
        in_specs=[
            pl.BlockSpec((MB_BLOCK, MB_PW), lambda b, j, i: (b * nb + qb0 + i, j)),
            pl.BlockSpec((seq, MB_PW), lambda b, j, i: (b, groups + j)),
            pl.BlockSpec((nb, MB_PAIR * MB_VROWS, MB_BLOCK), lambda b, j, i: (b, j, 0)),
            pl.BlockSpec((None, nb, MB_PW), lambda b, j, i: (b, 0, j)),
            pl.BlockSpec((MB_PAIR, MB_BIAS_TILES, MB_BLOCK, MB_BLOCK), lambda b, j, i: (j, 0, 0, 0)),
        ],
        out_specs=pl.BlockSpec((MB_BLOCK, MB_PW), lambda b, j, i: (b * nqb + i, j)),
        out_shape=jax.ShapeDtypeStruct((t, MB_WIDTH), BF16),
        scratch_shapes=(
            [pltpu.VMEM((1, MB_BLOCK), F32)] * (3 * MB_PAIR)
            + [pltpu.VMEM((MB_DH, MB_BLOCK), F32)] * MB_PAIR
            + [pltpu.VMEM((nb, MB_BLOCK), F32)] * MB_PAIR
            + [pltpu.VMEM((MB_BLOCK, MB_BLOCK), F32)] * MB_PAIR
            + [pltpu.VMEM((MB_BLOCK, MB_BLOCK), BF16)] * MB_PAIR
        ),
        compiler_params=_cparams(("parallel", "parallel", "arbitrary")),
        name="moba_attn",
    )(pqk, pqk, vt, km, bias)


def _t5_bucket(dist):
    max_exact = REL_BUCKETS // 2
    scaled = jnp.log(jnp.maximum(dist, 1).astype(F32) / max_exact) / math.log(REL_MAX_DIST / max_exact)
    large = jnp.minimum(max_exact + (scaled * (REL_BUCKETS - max_exact)).astype(I32), REL_BUCKETS - 1)
    return jnp.where(dist < max_exact, dist, large)


def moba_bias_tiles(rel_bias):
    blk = MB_BLOCK
    span = 2 * blk - 1
    x = jnp.arange(span) - (blk - 1)
    dist = jnp.maximum(jnp.arange(MB_BIAS_TILES)[:, None] * blk + x[None, :], 0)
    w = rel_bias.astype(F32).T[:, _t5_bucket(dist)]
    h = w.shape[0]
    wp = jnp.pad(w, ((0, 0), (0, 0), (0, 1)))
    a = jnp.broadcast_to(wp[:, :, None, :], (h, MB_BIAS_TILES, blk, span + 1))
    a = a.reshape(h, MB_BIAS_TILES, blk * (span + 1))[:, :, :blk * span]
    return a.reshape(h, MB_BIAS_TILES, blk, span)[:, :, :, blk - 1:]


def _mix_kernel(x_ref, ya_ref, yb_ref, ga_ref, gb_ref, wa_ref, wb_ref, wo_ref, o_ref):
    za = jnp.dot(ya_ref[...], wa_ref[...], preferred_element_type=F32)
    zb = jnp.dot(yb_ref[...], wb_ref[...], preferred_element_type=F32)
    z = jax.nn.sigmoid(ga_ref[...].astype(F32)) * za + jax.nn.sigmoid(gb_ref[...].astype(F32)) * zb
    o_ref[...] = x_ref[...] + jnp.dot(z.astype(BF16), wo_ref[...], preferred_element_type=F32)


def mix_out(x2d, ya, yb, pg, wa, wb, wo, tok0=0, tm=512):
    t = yb.shape[0]
    d = x2d.shape[1]
    w = ya.shape[1]
    b0 = tok0 // tm
    return pl.pallas_call(
        _mix_kernel,
        grid=(t // tm,),
        in_specs=[
            pl.BlockSpec((tm, d), lambda i: (b0 + i, 0)),
            pl.BlockSpec((tm, w), lambda i: (b0 + i, 0)),
            pl.BlockSpec((tm, w), lambda i: (i, 0)),
            pl.BlockSpec((tm, d), lambda i: (b0 + i, 0)),
            pl.BlockSpec((tm, d), lambda i: (b0 + i, 1)),
            pl.BlockSpec((w, d), lambda i: (0, 0)),
            pl.BlockSpec((w, d), lambda i: (0, 0)),
            pl.BlockSpec((d, d), lambda i: (0, 0)),
        ],
        out_specs=pl.BlockSpec((tm, d), lambda i: (i, 0)),
        out_shape=jax.ShapeDtypeStruct((t, d), F32),
        compiler_params=_cparams(("parallel",)),
        name="mix_out",
    )(x2d, ya, yb, pg, pg, wa, wb, wo)


def _mem_kv_kernel(m_ref, g_ref, wk_ref, wv_ref, k_ref, v_ref):
    mn = _rms(m_ref[...], g_ref[...]).astype(BF16)
    k_ref[...] = jnp.dot(mn, wk_ref[...], preferred_element_type=F32).astype(BF16)
    v_ref[...] = jnp.dot(mn, wv_ref[...], preferred_element_type=F32).astype(BF16)


def mem_kv(mem, g, wk, wv):
    b, m, d = mem.shape
    spec = pl.BlockSpec((None, m, d), lambda i: (i, 0, 0))
    wspec = pl.BlockSpec((d, d), lambda i: (0, 0))
    return pl.pallas_call(
        _mem_kv_kernel,
        grid=(b,),
        in_specs=[spec, pl.BlockSpec((1, d), lambda i: (0, 0)), wspec, wspec],
        out_specs=[spec, spec],
        out_shape=[jax.ShapeDtypeStruct((b, m, d), BF16)] * 2,
        compiler_params=_cparams(("parallel",)),
        name="mem_kv",
    )(mem, g, wk, wv)


def _cross_kernel(x_ref, g_ref, wq_ref, k_ref, v_ref, wo_ref, o_ref):
    x = x_ref[...]
    d = x.shape[1]
    dh = d // X_HEADS
    h = _rms(x, g_ref[...]).astype(BF16)
    q = (jnp.dot(h, wq_ref[...], preferred_element_type=F32) * (dh ** -0.5)).astype(BF16)
    outs = []
    for hh in range(X_HEADS):
        sl = slice(hh * dh, (hh + 1) * dh)
        s = lax.dot_general(q[:, sl], k_ref[:, sl], (((1,), (1,)), ((), ())),
                            preferred_element_type=F32)
        p = jnp.exp(s - jnp.max(s, axis=1, keepdims=True))
        l = jnp.sum(p, axis=1, keepdims=True)
        o = jnp.dot(p.astype(BF16), v_ref[:, sl], preferred_element_type=F32) / l
        outs.append(o.astype(BF16))
    o = jnp.concatenate(outs, axis=1)
    o_ref[...] = x + jnp.dot(o, wo_ref[...], preferred_element_type=F32)


def cross_attn(x2d, g, wq, kx, vx, wo, seq, tm=512):
    t, d = x2d.shape
    m = kx.shape[1]
    per_b = seq // tm
    kv = pl.BlockSpec((None, m, d), lambda i: (i // per_b, 0, 0))
    wspec = pl.BlockSpec((d, d), lambda i: (0, 0))
    return pl.pallas_call(
        _cross_kernel,
        grid=(t // tm,),
        in_specs=[pl.BlockSpec((tm, d), lambda i: (i, 0)), pl.BlockSpec((1, d), lambda i: (0, 0)),
                  wspec, kv, kv, wspec],
        out_specs=pl.BlockSpec((tm, d), lambda i: (i, 0)),
        out_shape=jax.ShapeDtypeStruct((t, d), F32),
        compiler_params=_cparams(("parallel",)),
        name="cross_attn",
    )(x2d, g, wq, kx, vx, wo)


def _topk_rows(sc, k):
    n = sc.shape[0]
    io = lax.broadcasted_iota(I32, sc.shape, 0)
    vals, ids = [], []
    for _ in range(k):
        m = jnp.max(sc, axis=0, keepdims=True)
        ix = jnp.min(jnp.where(sc == m, io, n), axis=0, keepdims=True)
        vals.append(m)
        ids.append(ix)
        sc = jnp.where(io == ix, NEG_INF, sc)
    return jnp.concatenate(vals, axis=0), jnp.concatenate(ids, axis=0)


def _pack_bf16_halves(h):
    bits = lax.bitcast_convert_type(h, I32)
    r = bits + 0x7FFF + (lax.shift_right_logical(bits, 16) & 1)
    half = h.shape[1] // 2
    return lax.shift_right_logical(r[:, :half], 16) | (r[:, half:] & HI_MASK)


def _route_kernel(x_ref, g_ref, wq_ref, sk_ref, hp_ref, idx_ref, w_ref, hb_ref, it_ref, wt_ref):
    p = pl.program_id(1)

    @pl.when(p == 0)
    def _():
        h = _rms(x_ref[...], g_ref[...])
        hp_ref[...] = _pack_bf16_halves(h)
        hb_ref[...] = h.astype(BF16)

    qh = jnp.dot(hb_ref[...], wq_ref[...], preferred_element_type=F32)
    tops = []
    for c in range(2):
        seg = qh[:, c * PEER_HALF:(c + 1) * PEER_HALF]
        sc = lax.dot_general(sk_ref[c], seg, (((1,), (1,)), ((), ())),
                             precision=lax.Precision.HIGHEST, preferred_element_type=F32)
        tops.append(_topk_rows(sc, PEER_TOPK))
    (s0, i0), (s1, i1) = tops
    k = PEER_TOPK
    sub = 8
    tm = s0.shape[1]
    r8 = lax.broadcasted_iota(I32, (sub, tm), 0)
    r16 = lax.broadcasted_iota(I32, (k, tm), 0)
    cand_b = [s0[0:1] + s1, s0[1:2] + s1[:sub]]
    cidx_b = [i0[0:1] * PEER_NKEYS + i1, i0[1:2] * PEER_NKEYS + i1[:sub]]
    pos_b = [r16, k + r8]
    for a in range(2, sub):
        keep = r8 < (k // (a + 1))
        cand_b.append(jnp.where(keep, s0[a:a + 1] + s1[:sub], NEG_INF))
        cidx_b.append(i0[a:a + 1] * PEER_NKEYS + i1[:sub])
        pos_b.append(a * k + r8)
    cand_b.append(s0[sub:] + s1[0:1])
    cidx_b.append(i0[sub:] * PEER_NKEYS + i1[0:1])
    pos_b.append((sub + r8) * k)
    cand = jnp.concatenate(cand_b, axis=0)
    cidx = jnp.concatenate(cidx_b, axis=0)
    pos = jnp.concatenate(pos_b, axis=0)
    vals, ids = [], []
    for _ in range(k):
        m = jnp.max(cand, axis=0, keepdims=True)
        px = jnp.min(jnp.where(cand == m, pos, k * k), axis=0, keepdims=True)
        hit = pos == px
        vals.append(m)
        ids.append(jnp.sum(jnp.where(hit, cidx, 0), axis=0, keepdims=True))
        cand = jnp.where(hit, NEG_INF, cand)
    sf = jnp.concatenate(vals, axis=0)
    e = jnp.exp(sf - sf[0:1])
    rows = pl.ds(pl.multiple_of(p * PEER_TOPK, PEER_TOPK), PEER_TOPK)
    wt_ref[rows, :] = e / jnp.sum(e, axis=0, keepdims=True)
    it_ref[rows, :] = jnp.concatenate(ids, axis=0)

    @pl.when(p == pl.num_programs(1) - 1)
    def _():
        idx_ref[...] = it_ref[...].T
        w_ref[...] = wt_ref[...].T


def peer_route(x2d, g, wq, sk, tok0, t, tm=1024):
    d = x2d.shape[1]
    ph = sk.shape[0]
    nsel = ph * PEER_TOPK
    blk0 = tok0 // tm
    return pl.pallas_call(
        _route_kernel,
        grid=(t // tm, ph),
        in_specs=[
            pl.BlockSpec((tm, d), lambda i, p: (blk0 + i, 0)),
            pl.BlockSpec((1, d), lambda i, p: (0, 0)),
            pl.BlockSpec((d, 2 * PEER_HALF), lambda i, p: (0, p)),
            pl.BlockSpec((None, 2, PEER_NKEYS, PEER_HALF), lambda i, p: (p, 0, 0, 0)),
        ],
        out_specs=[
            pl.BlockSpec((tm, d // 2), lambda i, p: (i, 0)),
            pl.BlockSpec((tm, nsel), lambda i, p: (i, 0)),
            pl.BlockSpec((tm, nsel), lambda i, p: (i, 0)),
        ],
        out_shape=[jax.ShapeDtypeStruct((t, d // 2), I32),
                   jax.ShapeDtypeStruct((t, nsel), I32),
                   jax.ShapeDtypeStruct((t, nsel), F32)],
        scratch_shapes=[pltpu.VMEM((tm, d), BF16),
                        pltpu.VMEM((nsel, tm), I32),
                        pltpu.VMEM((nsel, tm), F32)],
        compiler_params=_cparams(("parallel", "arbitrary")),
        name="peer_route",
    )(x2d, g, wq, sk)


def _coef_kernel(w_ref, a_ref, o_ref):
    o_ref[...] = w_ref[...] * jax.nn.gelu(a_ref[...])


def peer_coef(w, act, tm=1024):
    t, n = w.shape
    spec = pl.BlockSpec((tm, n), lambda i: (i, 0))
    return pl.pallas_call(
        _coef_kernel, grid=(t // tm,), in_specs=[spec, spec], out_specs=spec,
        out_shape=jax.ShapeDtypeStruct((t, n), F32),
        compiler_params=_cparams(("parallel",)), name="peer_coef",
    )(w, act)


def _final_kernel(x_ref, y_ref, g_ref, o_ref):
    o_ref[...] = _rms(x_ref[...] + y_ref[...], g_ref[...])


def final_norm(x2d, y, g, tok0, tm=512):
    t, d = y.shape
    blk0 = tok0 // tm
    spec = pl.BlockSpec((tm, d), lambda i: (i, 0))
    return pl.pallas_call(
        _final_kernel, grid=(t // tm,),
        in_specs=[pl.BlockSpec((tm, d), lambda i: (blk0 + i, 0)), spec, pl.BlockSpec((1, d), lambda i: (0, 0))],
        out_specs=spec,
        out_shape=jax.ShapeDtypeStruct((t, d), F32),
        compiler_params=_cparams(("parallel",)), name="final_norm",
    )(x2d, y, g)


SC_CORES = 2
SC_SUBCORES = 16
SC_WORKERS = SC_CORES * SC_SUBCORES
SC_LANES = 16
SC_GROUP = 16


def _sc_mesh():
    return plsc.VectorSubcoreMesh(core_axis_name="c", subcore_axis_name="s")


def _sc_params():
    return pltpu.CompilerParams(needs_layout_passes=False)


def _sc_worker_id():
    return lax.axis_index("s") * SC_CORES + lax.axis_index("c")


SC_RING = 4
SC_ROW_SUB = 8
SC_ROW_LANE = 128


def _sc_ring(n_units, start, wait, compute):
    for u in range(SC_RING - 1):
        start(u, u)

    @pl.loop(0, n_units, step=SC_RING)
    def _(uu):
        for b in range(SC_RING):
            u = uu + b
            nxt = u + (SC_RING - 1)

            @pl.when(nxt < n_units)
            def _():
                start(nxt, (b + SC_RING - 1) % SC_RING)

            wait(u, b)
            compute(u, b)


def _sc_unit_off(u):
    off = u * SC_LANES
    return off if isinstance(off, int) else pl.multiple_of(off, SC_LANES)


def _sc_row_piece(rows, r, c):
    per = SC_ROW_LANE // SC_LANES
    return rows[r, c // per, pl.ds(pl.multiple_of((c % per) * SC_LANES, SC_LANES), SC_LANES)]


def peer_dots_sc(table, idx_flat, h):
    t, d = h.shape
    nsel = PEER_SEL
    tpw = t // SC_WORKERS
    g = SC_GROUP
    groups = tpw // g
    heads = nsel // SC_LANES
    pieces = d // SC_LANES
    units = g * heads
    row_buf = pltpu.VMEM((SC_LANES, SC_ROW_SUB, SC_ROW_LANE), F32)

    @functools.partial(
        pl.kernel, mesh=_sc_mesh(),
        out_type=jax.ShapeDtypeStruct((t * nsel,), F32),
        scratch_types=[
            pltpu.VMEM((g * nsel,), I32),
            pltpu.VMEM((g, d), F32),
            pltpu.VMEM((g * nsel,), F32),
            pltpu.VMEM((SC_LANES * SC_LANES,), F32),
            [row_buf] * SC_RING,
            [pltpu.SemaphoreType.DMA] * SC_RING,
        ],
        compiler_params=_sc_params(),
        name="peer_dots_sc",
    )
    def k(tab_hbm, idx_hbm, h_hbm, out_hbm, idx_v, h_v, out_v, red_v, rows, sems):
        wid = _sc_worker_id()
        lane = lax.iota(I32, SC_LANES)

        def copy(u, slot):
            ids = idx_v.at[pl.ds(_sc_unit_off(u), SC_LANES)]
            return pltpu.make_async_copy(tab_hbm.at[ids], rows[slot], sems[slot])

        def compute(u, slot):
            tt = u // heads

            def body(c, accs):
                hv = h_v[tt, pl.ds(pl.multiple_of(c * SC_LANES, SC_LANES), SC_LANES)]
                return tuple(accs[r] + _sc_row_piece(rows[slot], r, c) * hv for r in range(SC_LANES))

            accs = lax.fori_loop(0, pieces, body,
                                 tuple(jnp.zeros((SC_LANES,), F32) for _ in range(SC_LANES)))
            for r in range(SC_LANES):
                red_v[pl.ds(r * SC_LANES, SC_LANES)] = accs[r]
            cols = [plsc.load_gather(red_v, [lane * SC_LANES + j]) for j in range(SC_LANES)]
            while len(cols) > 1:
                cols = [cols[i] + cols[i + 1] for i in range(0, len(cols), 2)]
            out_v[pl.ds(_sc_unit_off(u), SC_LANES)] = cols[0]

        @pl.loop(0, groups)
        def _(gi):
            base = wid * tpw + gi * g
            pltpu.sync_copy(idx_hbm.at[pl.ds(base * nsel, g * nsel)], idx_v)
            pltpu.sync_copy(h_hbm.at[pl.ds(base, g)], h_v)
            _sc_ring(units, lambda u, s: copy(u, s).start(), lambda u, s: copy(u, s).wait(), compute)
            pltpu.sync_copy(out_v, out_hbm.at[pl.ds(base * nsel, g * nsel)])

    return k(table, idx_flat, h)


def peer_combine_sc(table, idx_flat, coef_flat, t):
    d = table.shape[1] * table.shape[2]
    nsel = PEER_SEL
    tpw = t // SC_WORKERS
    g = SC_GROUP
    groups = tpw // g
    heads = nsel // SC_LANES
    pieces = d // SC_LANES
    units = g * heads
    row_buf = pltpu.VMEM((SC_LANES, SC_ROW_SUB, SC_ROW_LANE), F32)

    @functools.partial(
        pl.kernel, mesh=_sc_mesh(),
        out_type=jax.ShapeDtypeStruct((t, d), F32),
        scratch_types=[
            pltpu.VMEM((g * nsel,), I32),
            pltpu.VMEM((g * nsel,), F32),
            pltpu.VMEM((g, d), F32),
            [row_buf] * SC_RING,
            [pltpu.SemaphoreType.DMA] * SC_RING,
        ],
        compiler_params=_sc_params(),
        name="peer_combine_sc",
    )
    def k(tab_hbm, idx_hbm, coef_hbm, out_hbm, idx_v, coef_v, y_v, rows, sems):
        wid = _sc_worker_id()

        def copy(u, slot):
            ids = idx_v.at[pl.ds(_sc_unit_off(u), SC_LANES)]
            return pltpu.make_async_copy(tab_hbm.at[ids], rows[slot], sems[slot])

        def compute(u, slot):
            tt = u // heads
            first = (u % heads) == 0
            cs = [plsc.load_gather(coef_v, [jnp.full((SC_LANES,), u * SC_LANES + r, I32)])
                  for r in range(SC_LANES)]

            @plsc.parallel_loop(0, pieces, unroll=2)
            def _(c):
                off = pl.multiple_of(c * SC_LANES, SC_LANES)
                terms = [cs[r] * _sc_row_piece(rows[slot], r, c) for r in range(SC_LANES)]
                while len(terms) > 1:
                    terms = [terms[i] + terms[i + 1] for i in range(0, len(terms), 2)]
                prev = y_v[tt, pl.ds(off, SC_LANES)]
                y_v[tt, pl.ds(off, SC_LANES)] = terms[0] + jnp.where(first, 0.0, prev)

        @pl.loop(0, groups)
        def _(gi):
            base = wid * tpw + gi * g
            pltpu.sync_copy(idx_hbm.at[pl.ds(base * nsel, g * nsel)], idx_v)
            pltpu.sync_copy(coef_hbm.at[pl.ds(base * nsel, g * nsel)], coef_v)
            _sc_ring(units, lambda u, s: copy(u, s).start(), lambda u, s: copy(u, s).wait(), compute)
            pltpu.sync_copy(y_v, out_hbm.at[pl.ds(base, g)])

    return k(table, idx_flat, coef_flat)


GELU_C0 = math.sqrt(2.0 / math.pi)
GELU_C1 = 0.044715


def _gelu_tanh(x):
    z = GELU_C0 * (x + GELU_C1 * (x * x * x))
    th = 1.0 - 2.0 / (jnp.exp(2.0 * z) + 1.0)
    return 0.5 * x * (1.0 + th)


def peer_experts_sc(tab_u, tab_v, idx_flat, w_flat, h):
    t, d = h.shape
    nsel = PEER_SEL
    tpw = t // SC_WORKERS
    g = SC_GROUP
    groups = tpw // g
    heads = nsel // SC_LANES
    pieces = d // SC_LANES
    units = g * heads
    row_buf = pltpu.VMEM((SC_LANES, SC_ROW_SUB, SC_ROW_LANE), F32)

    @functools.partial(
        pl.kernel, mesh=_sc_mesh(),
        out_type=jax.ShapeDtypeStruct((t, d), F32),
        scratch_types=[
            pltpu.VMEM((g * nsel,), I32),
            pltpu.VMEM((g * nsel,), F32),
            pltpu.VMEM((g, d), F32),
            pltpu.VMEM((g, d), F32),
            pltpu.VMEM((SC_LANES * SC_LANES,), F32),
            [row_buf] * SC_RING,
            [pltpu.SemaphoreType.DMA] * SC_RING,
        ],
        compiler_params=_sc_params(),
        name="peer_experts_sc",
    )
    def k(u_hbm, v_hbm, idx_hbm, w_hbm, h_hbm, out_hbm, idx_v, coef_v, h_v, y_v, red_v, rows, sems):
        wid = _sc_worker_id()
        lane = lax.iota(I32, SC_LANES)

        def copy(tab_hbm, u, slot):
            ids = idx_v.at[pl.ds(_sc_unit_off(u), SC_LANES)]
            return pltpu.make_async_copy(tab_hbm.at[ids], rows[slot], sems[slot])

        def dots(u, slot):
            tt = u // heads

            def body(c, accs):
                hv = h_v[tt, pl.ds(pl.multiple_of(c * SC_LANES, SC_LANES), SC_LANES)]
                return tuple(accs[r] + _sc_row_piece(rows[slot], r, c) * hv for r in range(SC_LANES))

            accs = lax.fori_loop(0, pieces, body,
                                 tuple(jnp.zeros((SC_LANES,), F32) for _ in range(SC_LANES)))
            for r in range(SC_LANES):
                red_v[pl.ds(r * SC_LANES, SC_LANES)] = accs[r]
            cols = [plsc.load_gather(red_v, [lane * SC_LANES + j]) for j in range(SC_LANES)]
            while len(cols) > 1:
                cols = [cols[i] + cols[i + 1] for i in range(0, len(cols), 2)]
            sl = pl.ds(_sc_unit_off(u), SC_LANES)
            coef_v[sl] = coef_v[sl] * _gelu_tanh(cols[0])

        def combine(u, slot):
            tt = u // heads
            first = (u % heads) == 0
            cs = [plsc.load_gather(coef_v, [jnp.full((SC_LANES,), u * SC_LANES + r, I32)])
                  for r in range(SC_LANES)]

            @plsc.parallel_loop(0, pieces, unroll=2)
            def _(c):
                off = pl.multiple_of(c * SC_LANES, SC_LANES)
                terms = [cs[r] * _sc_row_piece(rows[slot], r, c) for r in range(SC_LANES)]
                while len(terms) > 1:
                    terms = [terms[i] + terms[i + 1] for i in range(0, len(terms), 2)]
                prev = y_v[tt, pl.ds(off, SC_LANES)]
                y_v[tt, pl.ds(off, SC_LANES)] = terms[0] + jnp.where(first, 0.0, prev)

        @pl.loop(0, groups)
        def _(gi):
            base = wid * tpw + gi * g
            pltpu.sync_copy(idx_hbm.at[pl.ds(base * nsel, g * nsel)], idx_v)
            pltpu.sync_copy(w_hbm.at[pl.ds(base * nsel, g * nsel)], coef_v)
            pltpu.sync_copy(h_hbm.at[pl.ds(base, g)], h_v)
            _sc_ring(units, lambda u, s: copy(u_hbm, u, s).start(), lambda u, s: copy(u_hbm, u, s).wait(), dots)
            _sc_ring(units, lambda u, s: copy(v_hbm, u, s).start(), lambda u, s: copy(v_hbm, u, s).wait(), combine)
            pltpu.sync_copy(y_v, out_hbm.at[pl.ds(base, g)])

    return k(tab_u, tab_v, idx_flat, w_flat, h)


SC_PK_RING = 4
SC_PK_SUB = 4
HI_MASK = -65536


def pack_bf16_pairs(a):
    half = a.shape[1] // 2
    bits = lax.bitcast_convert_type(a.astype(BF16), jnp.uint16).astype(jnp.uint32)
    return lax.bitcast_convert_type(bits[:, :half] | (bits[:, half:] << 16), I32)


def _unpack_halves(x32):
    w = plsc.bitcast(x32, I32)
    return plsc.bitcast(w << 16, F32), plsc.bitcast(w & HI_MASK, F32)


def _tree_sum(xs):
    while len(xs) > 1:
        xs = [xs[i] + xs[i + 1] for i in range(0, len(xs), 2)]
    return xs[0]


def peer_experts_pk_sc(tab_uv, idx_flat, w_flat, hp, d):
    t = hp.shape[0]
    nsel = PEER_SEL
    tpw = t // SC_WORKERS
    g = SC_GROUP
    groups = tpw // g
    heads = nsel // SC_LANES
    chunks = d // 32
    units = g * heads
    ring = SC_PK_RING
    row_buf = pltpu.VMEM((SC_LANES, 2 * SC_PK_SUB, SC_ROW_LANE), I32)

    def row_words(rows, r, wc, sub0):
        per = SC_ROW_LANE // SC_LANES
        return plsc.bitcast(
            rows[r, sub0 + wc // per, pl.ds(pl.multiple_of((wc % per) * SC_LANES, SC_LANES), SC_LANES)], BF16)

    def ring_loop(n_units, start, wait, compute):
        for u in range(ring - 1):
            start(u, u)

        @pl.loop(0, n_units, step=ring)
        def _(uu):
            for b in range(ring):
                u = uu + b
                nxt = u + (ring - 1)

                @pl.when(nxt < n_units)
                def _():
                    start(nxt, (b + ring - 1) % ring)

                wait(u, b)
                compute(u, b)

    @functools.partial(
        pl.kernel, mesh=_sc_mesh(),
        out_type=jax.ShapeDtypeStruct((t, d), F32),
        scratch_types=[
            pltpu.VMEM((g * nsel,), I32),
            pltpu.VMEM((g * nsel,), F32),
            pltpu.VMEM((g, d // 2), I32),
            pltpu.VMEM((g, d), F32),
            pltpu.VMEM((SC_LANES * SC_LANES,), F32),
            [row_buf] * ring,
            [pltpu.SemaphoreType.DMA] * ring,
        ],
        compiler_params=_sc_params(),
        name="peer_experts_pk_sc",
    )
    def k(tab_hbm, idx_hbm, w_hbm, h_hbm, out_hbm, idx_v, coef_v, h_v, y_v, red_v, rows, sems):
        wid = _sc_worker_id()
        lane = lax.iota(I32, SC_LANES)

        def copy(u, slot):
            ids = idx_v.at[pl.ds(_sc_unit_off(u), SC_LANES)]
            return pltpu.make_async_copy(tab_hbm.at[ids], rows[slot], sems[slot])

        def dots(u, slot):
            tt = u // heads

            def body(cp, accs):
                out = []
                hv = [plsc.bitcast(h_v[tt, pl.ds(pl.multiple_of((2 * cp + i) * SC_LANES, SC_LANES), SC_LANES)], BF16)
                      for i in range(2)]
                for r in range(SC_LANES):
                    pr = (row_words(rows[slot], r, 2 * cp, 0) * hv[0]
                          + row_words(rows[slot], r, 2 * cp + 1, 0) * hv[1])
                    lo, hi = _unpack_halves(pr)
                    out.append(accs[r] + lo + hi)
                return tuple(out)

            accs = lax.fori_loop(0, chunks // 2, body,
                                 tuple(jnp.zeros((SC_LANES,), F32) for _ in range(SC_LANES)))
            for r in range(SC_LANES):
                red_v[pl.ds(r * SC_LANES, SC_LANES)] = accs[r]
            act = _tree_sum([plsc.load_gather(red_v, [lane * SC_LANES + j]) for j in range(SC_LANES)])
            sl = pl.ds(_sc_unit_off(u), SC_LANES)
            coef_v[sl] = coef_v[sl] * _gelu_tanh(act)

        def combine(u, slot):
            tt = u // heads
            first = (u % heads) == 0
            cb = []
            for r in range(SC_LANES):
                c = plsc.load_gather(coef_v, [jnp.full((SC_LANES,), u * SC_LANES + r, I32)])
                cb.append(plsc.pack(c, c, format=plsc.PackFormat.INTERLEAVED))

            @plsc.parallel_loop(0, chunks, unroll=2)
            def _(wc):
                lo, hi = _unpack_halves(
                    _tree_sum([cb[r] * row_words(rows[slot], r, wc, SC_PK_SUB) for r in range(SC_LANES)]))
                for half, val in ((0, lo), (1, hi)):
                    sl = pl.ds(pl.multiple_of(half * (d // 2) + wc * SC_LANES, SC_LANES), SC_LANES)
                    y_v[tt, sl] = val + jnp.where(first, 0.0, y_v[tt, sl])

        def unit(u, slot):
            dots(u, slot)
            combine(u, slot)

        @pl.loop(0, groups)
        def _(gi):
            base = wid * tpw + gi * g
            pltpu.sync_copy(idx_hbm.at[pl.ds(base * nsel, g * nsel)], idx_v)
            pltpu.sync_copy(w_hbm.at[pl.ds(base * nsel, g * nsel)], coef_v)
            pltpu.sync_copy(h_hbm.at[pl.ds(base, g)], h_v)
            ring_loop(units, lambda u, s: copy(u, s).start(), lambda u, s: copy(u, s).wait(), unit)
            pltpu.sync_copy(y_v, out_hbm.at[pl.ds(base, g)])

    return k(tab_uv, idx_flat, w_flat, hp)


def kernel(x, mem, rel_bias, ln_mix, w_in, hg_lower, hg_norm, w_up_a, w_up_b, w_out, ln_cross, ln_mem, wq_x, wk_x, wv_x, wo_x, ln_ffn, peer_query, peer_subkeys, peer_u, peer_v, ln_final):
    b, s, d = x.shape
    depth = w_in.shape[0]
    assert depth == 1, "the residual after PEER is fused into the final norm"
    assert s % MB_BLOCK == 0 and s % HG_CHUNK == 0 and s % (PEER_SLICES * SC_WORKERS * SC_GROUP) == 0
    nb = s // MB_BLOCK
    row = lambda a: a.reshape(1, -1).astype(F32)
    lb_all = jnp.cumsum(jax.nn.softmax(hg_lower.astype(F32), axis=0), axis=0)
    bias = moba_bias_tiles(rel_bias)
    n_hg = 4 * HG_WIDTH
    n_qk = 2 * MB_WIDTH
    n_mb = 3 * MB_WIDTH
    l = 0
    w = w_in[l].astype(BF16)
    w_hg, w_qk, w_vt, w_g = w[:, :n_hg], w[:, n_hg:n_hg + n_qk], w[:, n_hg + n_qk:n_hg + n_mb].T, w[:, n_hg + n_mb:]
    wa, wb, wo = w_up_a[l].astype(BF16), w_up_b[l].astype(BF16), w_out[l].astype(BF16)
    wqx, wox = wq_x[l].astype(BF16), wo_x[l].astype(BF16)
    wpq, sk = peer_query[l].astype(BF16), peer_subkeys[l].astype(F32)
    tab3 = lambda a: pack_bf16_pairs(a.astype(F32)).reshape(a.shape[0], SC_PK_SUB, SC_ROW_LANE)
    tab_uv = jnp.concatenate([tab3(peer_u[l]), tab3(peer_v[l])], axis=1)
    kx, vx = mem_kv(mem, row(ln_mem[l]), wk_x[l].astype(BF16), wv_x[l].astype(BF16))

    outs = []
    for bi in range(b):
        x2d = x[bi]
        p0, pqk, vt, pg = in_proj(x2d, row(ln_mix[l]), w_hg, w_qk, w_vt, w_g)
        ya = hgrn2(p0, row(lb_all[l]), row(hg_norm[l]), 1, s)
        km = moba_kmean(pqk, 1, s).reshape(1, nb, MB_WIDTH)
        sizes = [s // PEER_SLICES] * PEER_SLICES
        if bi == b - 1:
            unit = SC_WORKERS * SC_GROUP
            sizes = [unit, sizes[0] - unit] + sizes[1:-1] + [sizes[-1] - unit, unit]
        tok0 = 0
        for ts in sizes:
            yb = moba_attention(pqk, vt, km, bias, 1, s, tok0 // MB_BLOCK, ts // MB_BLOCK)
            xs = mix_out(x2d, ya, yb, pg, wa, wb, wo, tok0)
            xs = cross_attn(xs, row(ln_cross[l]), wqx, kx[bi:bi + 1], vx[bi:bi + 1], wox, ts)
            hp, eidx, wts = peer_route(xs, row(ln_ffn[l]), wpq, sk, 0, ts, math.gcd(ts, 1024))
            y = peer_experts_pk_sc(tab_uv, eidx.reshape(ts * PEER_SEL), wts.reshape(ts * PEER_SEL), hp, d)
            outs.append(final_norm(xs, y, row(ln_final), 0))
            tok0 += ts
    return jnp.concatenate(outs, axis=0).reshape(b, s, d)
```

```python
import functools
import math

import jax
import jax.numpy as jnp
import numpy as np
from jax import lax
from jax.experimental import pallas as pl
from jax.experimental.pallas import tpu as pltpu
from jax.experimental.pallas import tpu_sc as plsc

F32 = jnp.float32
BF16 = jnp.bfloat16
I32 = jnp.int32
EPS = 1e-6
NEG_INF = float("-inf")

HG_HEADS = 4
HG_D = 128
HG_WIDTH = HG_HEADS * HG_D
HG_CHUNK = 64
HG_SUB = 16
MB_HEADS = 8
MB_DH = 64
MB_WIDTH = MB_HEADS * MB_DH
MB_BLOCK = 256
MB_TOPK = 3
MB_BIAS_TILES = 8
REL_BUCKETS = 32
REL_MAX_DIST = 2048
X_HEADS = 4
PEER_HEADS = 8
PEER_NKEYS = 128
PEER_TOPK = 16
PEER_HALF = 128
PEER_SEL = PEER_HEADS * PEER_TOPK
PEER_SLICES = 4

VMEM_LIMIT = 56 * 1024 * 1024


def _cparams(sem):
    return pltpu.CompilerParams(dimension_semantics=sem, vmem_limit_bytes=VMEM_LIMIT)


def _rms(x, g):
    ms = jnp.mean(x * x, axis=-1, keepdims=True)
    return x * lax.rsqrt(ms + EPS) * g


def _in_proj_kernel(x_ref, g_ref, w0_ref, w1_ref, wvt_ref, w2_ref, o0_ref, o1_ref, ovt_ref, o2_ref):
    h = _rms(x_ref[...], g_ref[...]).astype(BF16)
    o0_ref[...] = jnp.dot(h, w0_ref[...], preferred_element_type=F32)
    o1_ref[...] = jnp.dot(h, w1_ref[...], preferred_element_type=F32).astype(BF16)
    vt = lax.dot_general(wvt_ref[...], h, (((1,), (1,)), ((), ())), preferred_element_type=F32).astype(BF16)
    for hd in range(MB_HEADS):
        ovt_ref[0, hd * MB_VROWS:hd * MB_VROWS + MB_DH, :] = vt[hd * MB_DH:(hd + 1) * MB_DH]
        ovt_ref[0, hd * MB_VROWS + MB_DH:(hd + 1) * MB_VROWS, :] = jnp.ones((MB_ONES, vt.shape[1]), BF16)
    o2_ref[...] = jnp.dot(h, w2_ref[...], preferred_element_type=F32).astype(BF16)


def in_proj(x2d, g, w0, w1, wvt, w2):
    t, d = x2d.shape
    tm = MB_BLOCK
    assert wvt.shape[0] == MB_WIDTH
    n0, n1, nv, n2 = w0.shape[1], w1.shape[1], MB_VT_ROWS, w2.shape[1]
    full = lambda a: pl.BlockSpec(a.shape, lambda i: (0, 0))
    return pl.pallas_call(
        _in_proj_kernel,
        grid=(t // tm,),
        in_specs=[pl.BlockSpec((tm, d), lambda i: (i, 0)), full(g), full(w0), full(w1), full(wvt), full(w2)],
        out_specs=[pl.BlockSpec((tm, n0), lambda i: (i, 0)),
                   pl.BlockSpec((tm, n1), lambda i: (i, 0)),
                   pl.BlockSpec((1, nv, tm), lambda i: (i, 0, 0)),
                   pl.BlockSpec((tm, n2), lambda i: (i, 0))],
        out_shape=[jax.ShapeDtypeStruct((t, n0), F32),
                   jax.ShapeDtypeStruct((t, n1), BF16),
                   jax.ShapeDtypeStruct((t // tm, nv, tm), BF16),
                   jax.ShapeDtypeStruct((t, n2), BF16)],
        compiler_params=_cparams(("parallel",)),
        name="in_proj",
    )(x2d, g, w0, w1, wvt, w2)


def _hgrn_kernel(q_ref, f_ref, i_ref, g_ref, lb_ref, gain_ref, o_ref, st_ref):
    c = pl.program_id(1)

    @pl.when(c == 0)
    def _():
        st_ref[...] = jnp.zeros_like(st_ref)

    C, S = HG_CHUNK, HG_SUB
    row = lax.broadcasted_iota(I32, (C, C), 0)
    col = lax.broadcasted_iota(I32, (C, C), 1)
    tril = (row >= col).astype(F32)
    t_iota = lax.broadcasted_iota(I32, (S, 1), 0)

    for h in range(HG_HEADS):
        sl = slice(h * HG_D, (h + 1) * HG_D)
        q = q_ref[:, sl]
        v = i_ref[:, sl]
        lb = lb_ref[:, sl]
        f = lb + (1.0 - lb) * jax.nn.sigmoid(f_ref[:, sl])
        lf = jnp.log(f)
        k = 1.0 - f
        b = jnp.dot(tril, lf, precision=lax.Precision.HIGHEST, preferred_element_type=F32)
        st = st_ref[h]
        vb = v.astype(BF16)
        qd = (q * jnp.exp(b)).astype(BF16)
        o_inter = lax.dot_general(qd, st.astype(BF16), (((1,), (1,)), ((), ())),
                                  preferred_element_type=F32)
        outs = []
        for i in range(C // S):
            r0 = i * S
            qi = q[r0:r0 + S]
            ki = k[r0:r0 + S]
            bi = b[r0:r0 + S]
            vi = v[r0:r0 + S]
            oi = o_inter[r0:r0 + S]
            if i > 0:
                bs = b[r0 - 1:r0]
                qh = (qi * jnp.exp(bi - bs)).astype(BF16)
                kh = (k[:r0] * jnp.exp(bs - b[:r0])).astype(BF16)
                a = lax.dot_general(qh, kh, (((1,), (1,)), ((), ())), preferred_element_type=F32)
                oi = oi + jnp.dot(a.astype(BF16), vb[:r0], preferred_element_type=F32)
            half = S // 2
            o_half = [oi[:half], oi[half:]]
            for s in range(S):
                for hf in range(s // half, 2):
                    rows = slice(hf * half, (hf + 1) * half)
                    dec = jnp.exp(jnp.minimum(bi[rows] - bi[s:s + 1], 0.0))
                    a_s = jnp.sum(qi[rows] * ki[s:s + 1] * dec, axis=-1, keepdims=True)
                    a_s = jnp.where(t_iota[rows] >= s, a_s, 0.0)
                    o_half[hf] = o_half[hf] + a_s * vi[s:s + 1]
            outs.extend(o_half)
        o = jnp.concatenate(outs, axis=0)
        b_end = b[C - 1:C]
        kd = (k * jnp.exp(b_end - b)).astype(BF16)
        upd = lax.dot_general(vb, kd, (((0,), (0,)), ((), ())), preferred_element_type=F32)
        st_ref[h] = st * jnp.exp(b_end) + upd
        o = o * lax.rsqrt(jnp.mean(o * o, axis=-1, keepdims=True) + EPS)
        g = g_ref[:, sl]
        o_ref[:, sl] = (o * gain_ref[:, sl] * (g * jax.nn.sigmoid(g))).astype(o_ref.dtype)


def hgrn2(p0, lb, gain, batch, seq):
    t = p0.shape[0]
    nc = seq // HG_CHUNK
    w = HG_WIDTH

    def col(j):
        return pl.BlockSpec((HG_CHUNK, w), lambda b, c, j=j: (b * nc + c, j))

    return pl.pallas_call(
        _hgrn_kernel,
        grid=(batch, nc),
        in_specs=[col(0), col(1), col(2), col(3),
                  pl.BlockSpec((1, w), lambda b, c: (0, 0)),
                  pl.BlockSpec((1, w), lambda b, c: (0, 0))],
        out_specs=pl.BlockSpec((HG_CHUNK, w), lambda b, c: (b * nc + c, 0)),
        out_shape=jax.ShapeDtypeStruct((t, w), BF16),
        scratch_shapes=[pltpu.VMEM((HG_HEADS, HG_D, HG_D), F32)],
        compiler_params=_cparams(("parallel", "arbitrary")),
        name="hgrn2",
    )(p0, p0, p0, p0, lb, gain)


def _kmean_kernel(k_ref, o_ref):
    o_ref[0] = jnp.mean(k_ref[...].astype(F32), axis=0, keepdims=True)


def moba_kmean(p1, batch, seq):
    nbt = p1.shape[0] // MB_BLOCK
    return pl.pallas_call(
        _kmean_kernel,
        grid=(nbt,),
        in_specs=[pl.BlockSpec((MB_BLOCK, MB_WIDTH), lambda i: (i, 1))],
        out_specs=pl.BlockSpec((1, 1, MB_WIDTH), lambda i: (i, 0, 0)),
        out_shape=jax.ShapeDtypeStruct((nbt, 1, MB_WIDTH), F32),
        compiler_params=_cparams(("parallel",)),
        name="moba_kmean",
    )(p1)


MB_PAIR = 4
MB_PW = MB_PAIR * MB_DH
MB_LG = 128
MB_ONES = 16
MB_VROWS = MB_DH + MB_ONES
MB_VT_ROWS = MB_HEADS * MB_VROWS


def _moba_kernel(q_ref, k_ref, vt_ref, km_ref, bias_ref, o_ref, *scratch, qb0):
    m_ref, l_ref, al_ref, acc_ref, msk_ref, s_ref, p_ref = (
        scratch[i * MB_PAIR:(i + 1) * MB_PAIR] for i in range(7))
    qi = pl.program_id(2) + qb0
    nb = km_ref.shape[0]
    blk = MB_BLOCK
    heads = range(MB_PAIR)
    grp = lambda hh: slice((hh // 2) * MB_LG, (hh // 2 + 1) * MB_LG)
    q = q_ref[...]
    lane = lax.broadcasted_iota(I32, (blk, MB_LG), 1)
    in_head = [(lane < MB_DH) if hh % 2 == 0 else (lane >= MB_DH) for hh in heads]
    qs = q * jnp.asarray(MB_DH ** -0.5, BF16)
    nt = (((1,), (1,)), ((), ()))
    qf = q.astype(F32)
    qht = [jnp.where(in_head[hh], qs[:, grp(hh)].astype(F32), 0.0).T.astype(BF16) for hh in heads]

    n_io = lax.broadcasted_iota(I32, (nb, blk), 0)
    for hh in heads:
        gate = lax.dot_general(km_ref[:, grp(hh)], jnp.where(in_head[hh], qf[:, grp(hh)], 0.0), nt,
                               precision=lax.Precision.HIGHEST, preferred_element_type=F32)
        gate = jnp.where(n_io < qi, gate, NEG_INF)
        chosen = n_io < 0
        for _ in range(MB_TOPK):
            mx = jnp.max(gate, axis=0, keepdims=True)
            ix = jnp.min(jnp.where(gate == mx, n_io, nb), axis=0, keepdims=True)
            hit = n_io == ix
            chosen = chosen | (hit & (mx > NEG_INF))
            gate = jnp.where(hit, NEG_INF, gate)
        msk_ref[hh][...] = jnp.where(chosen, 0.0, NEG_INF)

    vrows = lambda hh: slice(hh * MB_VROWS, (hh + 1) * MB_VROWS)

    def pv_stage(blk_idx):
        vtb = vt_ref[blk_idx]
        r = [jnp.dot(vtb[vrows(hh)], p_ref[hh][...], preferred_element_type=F32) for hh in heads]
        al = [al_ref[hh][...] for hh in heads]
        a_new = [al[hh] * acc_ref[hh][...] + r[hh][:MB_DH] for hh in heads]
        l_new = [al[hh] * l_ref[hh][...] + r[hh][MB_DH:MB_DH + 1] for hh in heads]
        return a_new, l_new

    def store_pv(a_new, l_new):
        for hh in heads:
            acc_ref[hh][...] = a_new[hh]
            l_ref[hh][...] = l_new[hh]

    def softmax_stage():
        s = [s_ref[hh][...] for hh in heads]
        m_old = [m_ref[hh][...] for hh in heads]
        m_new = [jnp.maximum(m_old[hh], jnp.max(s[hh], axis=0, keepdims=True)) for hh in heads]
        alpha = [jnp.exp(m_old[hh] - m_new[hh]) for hh in heads]
        p = [jnp.exp((s[hh] - m_new[hh]).astype(BF16)) for hh in heads]
        return p, alpha, m_new

    def store_softmax(p, alpha, m_new):
        for hh in heads:
            p_ref[hh][...] = p[hh]
            al_ref[hh][...] = alpha[hh]
            m_ref[hh][...] = m_new[hh]

    k_own = k_ref[pl.ds(pl.multiple_of(qi * blk, blk), blk), :]
    key_io = lax.broadcasted_iota(I32, (blk, blk), 0)
    qry_io = lax.broadcasted_iota(I32, (blk, blk), 1)
    for hh in heads:
        s = jnp.dot(k_own[:, grp(hh)], qht[hh], preferred_element_type=F32) + bias_ref[hh, 0]
        s_ref[hh][...] = jnp.where(key_io <= qry_io, s, NEG_INF)
        m_ref[hh][...] = jnp.full((1, blk), NEG_INF, F32)
        l_ref[hh][...] = jnp.zeros((1, blk), F32)
        al_ref[hh][...] = jnp.ones((1, blk), F32)
        acc_ref[hh][...] = jnp.zeros((MB_DH, blk), F32)
        p_ref[hh][...] = jnp.zeros((blk, blk), BF16)

    def step(i, carry, far):
        pv = pv_stage(jnp.where(i <= 1, qi, i - 2))
        sm = softmax_stage()
        kn = k_ref[pl.ds(pl.multiple_of(i * blk, blk), blk), :]
        if far:
            row = [msk_ref[hh][pl.ds(i, 1), :] + bias_ref[hh, MB_BIAS_TILES - 1, 0:1, 0:1] for hh in heads]
            s_next = [jnp.dot(kn[:, grp(hh)], qht[hh], preferred_element_type=F32) + row[hh] for hh in heads]
        else:
            d = qi - i
            s_next = [jnp.dot(kn[:, grp(hh)], qht[hh], preferred_element_type=F32)
                      + bias_ref[hh, d] + msk_ref[hh][pl.ds(i, 1), :] for hh in heads]
        store_pv(*pv)
        for hh in heads:
            s_ref[hh][...] = s_next[hh]
        store_softmax(*sm)
        return carry

    n_far = jnp.maximum(qi - (MB_BIAS_TILES - 2), 0)
    lax.fori_loop(0, n_far, functools.partial(step, far=True), 0)
    lax.fori_loop(n_far, qi, functools.partial(step, far=False), 0)
    pv = pv_stage(jnp.where(qi <= 1, qi, qi - 2))
    sm = softmax_stage()
    store_pv(*pv)
    store_softmax(*sm)
    a_fin, l_fin = pv_stage(jnp.where(qi == 0, qi, qi - 1))
    out_t = jnp.concatenate([a_fin[hh] / l_fin[hh] for hh in heads], axis=0)
    o_ref[...] = out_t.T.astype(o_ref.dtype)


def moba_attention(pqk, vt, km, bias, batch, seq, qb0=0, nqb=None):
    nb = seq // MB_BLOCK
    nqb = nb if nqb is None else nqb
    t = batch * nqb * MB_BLOCK
    groups = MB_WIDTH // MB_PW
    return pl.pallas_call(
        functools.partial(_moba_kernel, qb0=qb0),
        grid=(batch, groups, nqb),
        in_specs=[
            pl.BlockSpec((MB_BLOCK, MB_PW), lambda b, j, i: (b * nb + qb0 + i, j)),
            pl.BlockSpec((seq, MB_PW), lambda b, j, i: (b, groups + j)),
            pl.BlockSpec((nb, MB_PAIR * MB_VROWS, MB_BLOCK), lambda b, j, i: (b, j, 0)),
            pl.BlockSpec((None, nb, MB_PW), lambda b, j, i: (b, 0, j)),
            pl.BlockSpec((MB_PAIR, MB_BIAS_TILES, MB_BLOCK, MB_BLOCK), lambda b, j, i: (j, 0, 0, 0)),
        ],
        out_specs=pl.BlockSpec((MB_BLOCK, MB_PW), lambda b, j, i: (b * nqb + i, j)),
        out_shape=jax.ShapeDtypeStruct((t, MB_WIDTH), BF16),
        scratch_shapes=(
            [pltpu.VMEM((1, MB_BLOCK), F32)] * (3 * MB_PAIR)
            + [pltpu.VMEM((MB_DH, MB_BLOCK), F32)] * MB_PAIR
            + [pltpu.VMEM((nb, MB_BLOCK), F32)] * MB_PAIR
            + [pltpu.VMEM((MB_BLOCK, MB_BLOCK), F32)] * MB_PAIR
            + [pltpu.VMEM((MB_BLOCK, MB_BLOCK), BF16)] * MB_PAIR
        ),
        compiler_params=_cparams(("parallel", "parallel", "arbitrary")),
        name="moba_attn",
    )(pqk, pqk, vt, km, bias)


def _t5_bucket(dist):
    max_exact = REL_BUCKETS // 2
    scaled = jnp.log(jnp.maximum(dist, 1).astype(F32) / max_exact) / math.log(REL_MAX_DIST / max_exact)
    large = jnp.minimum(max_exact + (scaled * (REL_BUCKETS - max_exact)).astype(I32), REL_BUCKETS - 1)
    return jnp.where(dist < max_exact, dist, large)


def moba_bias_tiles(rel_bias):
    blk = MB_BLOCK
    span = 2 * blk - 1
    x = jnp.arange(span) - (blk - 1)
    dist = jnp.maximum(jnp.arange(MB_BIAS_TILES)[:, None] * blk + x[None, :], 0)
    w = rel_bias.astype(F32).T[:, _t5_bucket(dist)]
    h = w.shape[0]
    wp = jnp.pad(w, ((0, 0), (0, 0), (0, 1)))
    a = jnp.broadcast_to(wp[:, :, None, :], (h, MB_BIAS_TILES, blk, span + 1))
    a = a.reshape(h, MB_BIAS_TILES, blk * (span + 1))[:, :, :blk * span]
    return a.reshape(h, MB_BIAS_TILES, blk, span)[:, :, :, blk - 1:]


def _mix_kernel(x_ref, ya_ref, yb_ref, ga_ref, gb_ref, wa_ref, wb_ref, wo_ref, o_ref):
    za = jnp.dot(ya_ref[...], wa_ref[...], preferred_element_type=F32)
    zb = jnp.dot(yb_ref[...], wb_ref[...], preferred_element_type=F32)
    z = jax.nn.sigmoid(ga_ref[...].astype(F32)) * za + jax.nn.sigmoid(gb_ref[...].astype(F32)) * zb
    o_ref[...] = x_ref[...] + jnp.dot(z.astype(BF16), wo_ref[...], preferred_element_type=F32)


def mix_out(x2d, ya, yb, pg, wa, wb, wo, tok0=0, tm=512):
    t = yb.shape[0]
    d = x2d.shape[1]
    w = ya.shape[1]
    b0 = tok0 // tm
    return pl.pallas_call(
        _mix_kernel,
        grid=(t // tm,),
        in_specs=[
            pl.BlockSpec((tm, d), lambda i: (b0 + i, 0)),
            pl.BlockSpec((tm, w), lambda i: (b0 + i, 0)),
            pl.BlockSpec((tm, w), lambda i: (i, 0)),
            pl.BlockSpec((tm, d), lambda i: (b0 + i, 0)),
            pl.BlockSpec((tm, d), lambda i: (b0 + i, 1)),
            pl.BlockSpec((w, d), lambda i: (0, 0)),
            pl.BlockSpec((w, d), lambda i: (0, 0)),
            pl.BlockSpec((d, d), lambda i: (0, 0)),
        ],
        out_specs=pl.BlockSpec((tm, d), lambda i: (i, 0)),
        out_shape=jax.ShapeDtypeStruct((t, d), F32),
        compiler_params=_cparams(("parallel",)),
        name="mix_out",
    )(x2d, ya, yb, pg, pg, wa, wb, wo)


def _mem_kv_kernel(m_ref, g_ref, wk_ref, wv_ref, k_ref, v_ref):
    mn = _rms(m_ref[...], g_ref[...]).astype(BF16)
    k_ref[...] = jnp.dot(mn, wk_ref[...], preferred_element_type=F32).astype(BF16)
    v_ref[...] = jnp.dot(mn, wv_ref[...], preferred_element_type=F32).astype(BF16)


def mem_kv(mem, g, wk, wv):
    b, m, d = mem.shape
    spec = pl.BlockSpec((None, m, d), lambda i: (i, 0, 0))
    wspec = pl.BlockSpec((d, d), lambda i: (0, 0))
    return pl.pallas_call(
        _mem_kv_kernel,
        grid=(b,),
        in_specs=[spec, pl.BlockSpec((1, d), lambda i: (0, 0)), wspec, wspec],
        out_specs=[spec, spec],
        out_shape=[jax.ShapeDtypeStruct((b, m, d), BF16)] * 2,
        compiler_params=_cparams(("parallel",)),
        name="mem_kv",
    )(mem, g, wk, wv)


def _cross_kernel(x_ref, g_ref, wq_ref, k_ref, v_ref, wo_ref, o_ref):
    x = x_ref[...]
    d = x.shape[1]
    dh = d // X_HEADS
    h = _rms(x, g_ref[...]).astype(BF16)
    q = (jnp.dot(h, wq_ref[...], preferred_element_type=F32) * (dh ** -0.5)).astype(BF16)
    outs = []
    for hh in range(X_HEADS):
        sl = slice(hh * dh, (hh + 1) * dh)
        s = lax.dot_general(q[:, sl], k_ref[:, sl], (((1,), (1,)), ((), ())),
                            preferred_element_type=F32)
        p = jnp.exp(s - jnp.max(s, axis=1, keepdims=True))
        l = jnp.sum(p, axis=1, keepdims=True)
        o = jnp.dot(p.astype(BF16), v_ref[:, sl], preferred_element_type=F32) / l
        outs.append(o.astype(BF16))
    o = jnp.concatenate(outs, axis=1)
    o_ref[...] = x + jnp.dot(o, wo_ref[...], preferred_element_type=F32)


def cross_attn(x2d, g, wq, kx, vx, wo, seq, tm=512):
    t, d = x2d.shape
    m = kx.shape[1]
    per_b = seq // tm
    kv = pl.BlockSpec((None, m, d), lambda i: (i // per_b, 0, 0))
    wspec = pl.BlockSpec((d, d), lambda i: (0, 0))
    return pl.pallas_call(
        _cross_kernel,
        grid=(t // tm,),
        in_specs=[pl.BlockSpec((tm, d), lambda i: (i, 0)), pl.BlockSpec((1, d), lambda i: (0, 0)),
                  wspec, kv, kv, wspec],
        out_specs=pl.BlockSpec((tm, d), lambda i: (i, 0)),
        out_shape=jax.ShapeDtypeStruct((t, d), F32),
        compiler_params=_cparams(("parallel",)),
        name="cross_attn",
    )(x2d, g, wq, kx, vx, wo)


def _topk_rows(sc, k):
    n = sc.shape[0]
    io = lax.broadcasted_iota(I32, sc.shape, 0)
    vals, ids = [], []
    for _ in range(k):
        m = jnp.max(sc, axis=0, keepdims=True)
        ix = jnp.min(jnp.where(sc == m, io, n), axis=0, keepdims=True)
        vals.append(m)
        ids.append(ix)
        sc = jnp.where(io == ix, NEG_INF, sc)
    return jnp.concatenate(vals, axis=0), jnp.concatenate(ids, axis=0)


def _pack_bf16_halves(h):
    bits = lax.bitcast_convert_type(h, I32)
    r = bits + 0x7FFF + (lax.shift_right_logical(bits, 16) & 1)
    half = h.shape[1] // 2
    return lax.shift_right_logical(r[:, :half], 16) | (r[:, half:] & HI_MASK)


def _route_kernel(x_ref, g_ref, wq_ref, sk_ref, hp_ref, idx_ref, w_ref, hb_ref, it_ref, wt_ref):
    p = pl.program_id(1)

    @pl.when(p == 0)
    def _():
        h = _rms(x_ref[...], g_ref[...])
        hp_ref[...] = _pack_bf16_halves(h)
        hb_ref[...] = h.astype(BF16)

    qh = jnp.dot(hb_ref[...], wq_ref[...], preferred_element_type=F32)
    tops = []
    for c in range(2):
        seg = qh[:, c * PEER_HALF:(c + 1) * PEER_HALF]
        sc = lax.dot_general(sk_ref[c], seg, (((1,), (1,)), ((), ())),
                             precision=lax.Precision.HIGHEST, preferred_element_type=F32)
        tops.append(_topk_rows(sc, PEER_TOPK))
    (s0, i0), (s1, i1) = tops
    k = PEER_TOPK
    sub = 8
    tm = s0.shape[1]
    r8 = lax.broadcasted_iota(I32, (sub, tm), 0)
    r16 = lax.broadcasted_iota(I32, (k, tm), 0)
    cand_b = [s0[0:1] + s1, s0[1:2] + s1[:sub]]
    cidx_b = [i0[0:1] * PEER_NKEYS + i1, i0[1:2] * PEER_NKEYS + i1[:sub]]
    pos_b = [r16, k + r8]
    for a in range(2, sub):
        keep = r8 < (k // (a + 1))
        cand_b.append(jnp.where(keep, s0[a:a + 1] + s1[:sub], NEG_INF))
        cidx_b.append(i0[a:a + 1] * PEER_NKEYS + i1[:sub])
        pos_b.append(a * k + r8)
    cand_b.append(s0[sub:] + s1[0:1])
    cidx_b.append(i0[sub:] * PEER_NKEYS + i1[0:1])
    pos_b.append((sub + r8) * k)
    cand = jnp.concatenate(cand_b, axis=0)
    cidx = jnp.concatenate(cidx_b, axis=0)
    pos = jnp.concatenate(pos_b, axis=0)
    vals, ids = [], []
    for _ in range(k):
        m = jnp.max(cand, axis=0, keepdims=True)
        px = jnp.min(jnp.where(cand == m, pos, k * k), axis=0, keepdims=True)
        hit = pos == px
        vals.append(m)
        ids.append(jnp.sum(jnp.where(hit, cidx, 0), axis=0, keepdims=True))
        cand = jnp.where(hit, NEG_INF, cand)
    sf = jnp.concatenate(vals, axis=0)
    e = jnp.exp(sf - sf[0:1])
    rows = pl.ds(pl.multiple_of(p * PEER_TOPK, PEER_TOPK), PEER_TOPK)
    wt_ref[rows, :] = e / jnp.sum(e, axis=0, keepdims=True)
    it_ref[rows, :] = jnp.concatenate(ids, axis=0)

    @pl.when(p == pl.num_programs(1) - 1)
    def _():
        idx_ref[...] = it_ref[...].T
        w_ref[...] = wt_ref[...].T


def peer_route(x2d, g, wq, sk, tok0, t, tm=1024):
    d = x2d.shape[1]
    ph = sk.shape[0]
    nsel = ph * PEER_TOPK
    blk0 = tok0 // tm
    return pl.pallas_call(
        _route_kernel,
        grid=(t // tm, ph),
        in_specs=[
            pl.BlockSpec((tm, d), lambda i, p: (blk0 + i, 0)),
            pl.BlockSpec((1, d), lambda i, p: (0, 0)),
            pl.BlockSpec((d, 2 * PEER_HALF), lambda i, p: (0, p)),
            pl.BlockSpec((None, 2, PEER_NKEYS, PEER_HALF), lambda i, p: (p, 0, 0, 0)),
        ],
        out_specs=[
            pl.BlockSpec((tm, d // 2), lambda i, p: (i, 0)),
            pl.BlockSpec((tm, nsel), lambda i, p: (i, 0)),
            pl.BlockSpec((tm, nsel), lambda i, p: (i, 0)),
        ],
        out_shape=[jax.ShapeDtypeStruct((t, d // 2), I32),
                   jax.ShapeDtypeStruct((t, nsel), I32),
                   jax.ShapeDtypeStruct((t, nsel), F32)],
        scratch_shapes=[pltpu.VMEM((tm, d), BF16),
                        pltpu.VMEM((nsel, tm), I32),
                        pltpu.VMEM((nsel, tm), F32)],
        compiler_params=_cparams(("parallel", "arbitrary")),
        name="peer_route",
    )(x2d, g, wq, sk)


def _coef_kernel(w_ref, a_ref, o_ref):
    o_ref[...] = w_ref[...] * jax.nn.gelu(a_ref[...])


def peer_coef(w, act, tm=1024):
    t, n = w.shape
    spec = pl.BlockSpec((tm, n), lambda i: (i, 0))
    return pl.pallas_call(
        _coef_kernel, grid=(t // tm,), in_specs=[spec, spec], out_specs=spec,
        out_shape=jax.ShapeDtypeStruct((t, n), F32),
        compiler_params=_cparams(("parallel",)), name="peer_coef",
    )(w, act)


def _final_kernel(x_ref, y_ref, g_ref, o_ref):
    o_ref[...] = _rms(x_ref[...] + y_ref[...], g_ref[...])


def final_norm(x2d, y, g, tok0, tm=512):
    t, d = y.shape
    blk0 = tok0 // tm
    spec = pl.BlockSpec((tm, d), lambda i: (i, 0))
    return pl.pallas_call(
        _final_kernel, grid=(t // tm,),
        in_specs=[pl.BlockSpec((tm, d), lambda i: (blk0 + i, 0)), spec, pl.BlockSpec((1, d), lambda i: (0, 0))],
        out_specs=spec,
        out_shape=jax.ShapeDtypeStruct((t, d), F32),
        compiler_params=_cparams(("parallel",)), name="final_norm",
    )(x2d, y, g)


SC_CORES = 2
SC_SUBCORES = 16
SC_WORKERS = SC_CORES * SC_SUBCORES
SC_LANES = 16
SC_GROUP = 16


def _sc_mesh():
    return plsc.VectorSubcoreMesh(core_axis_name="c", subcore_axis_name="s")


def _sc_params():
    return pltpu.CompilerParams(needs_layout_passes=False)


def _sc_worker_id():
    return lax.axis_index("s") * SC_CORES + lax.axis_index("c")


SC_RING = 4
SC_ROW_SUB = 8
SC_ROW_LANE = 128


def _sc_ring(n_units, start, wait, compute):
    for u in range(SC_RING - 1):
        start(u, u)

    @pl.loop(0, n_units, step=SC_RING)
    def _(uu):
        for b in range(SC_RING):
            u = uu + b
            nxt = u + (SC_RING - 1)

            @pl.when(nxt < n_units)
            def _():
                start(nxt, (b + SC_RING - 1) % SC_RING)

            wait(u, b)
            compute(u, b)


def _sc_unit_off(u):
    off = u * SC_LANES
    return off if isinstance(off, int) else pl.multiple_of(off, SC_LANES)


def _sc_row_piece(rows, r, c):
    per = SC_ROW_LANE // SC_LANES
    return rows[r, c // per, pl.ds(pl.multiple_of((c % per) * SC_LANES, SC_LANES), SC_LANES)]


def peer_dots_sc(table, idx_flat, h):
    t, d = h.shape
    nsel = PEER_SEL
    tpw = t // SC_WORKERS
    g = SC_GROUP
    groups = tpw // g
    heads = nsel // SC_LANES
    pieces = d // SC_LANES
    units = g * heads
    row_buf = pltpu.VMEM((SC_LANES, SC_ROW_SUB, SC_ROW_LANE), F32)

    @functools.partial(
        pl.kernel, mesh=_sc_mesh(),
        out_type=jax.ShapeDtypeStruct((t * nsel,), F32),
        scratch_types=[
            pltpu.VMEM((g * nsel,), I32),
            pltpu.VMEM((g, d), F32),
            pltpu.VMEM((g * nsel,), F32),
            pltpu.VMEM((SC_LANES * SC_LANES,), F32),
            [row_buf] * SC_RING,
            [pltpu.SemaphoreType.DMA] * SC_RING,
        ],
        compiler_params=_sc_params(),
        name="peer_dots_sc",
    )
    def k(tab_hbm, idx_hbm, h_hbm, out_hbm, idx_v, h_v, out_v, red_v, rows, sems):
        wid = _sc_worker_id()
        lane = lax.iota(I32, SC_LANES)

        def copy(u, slot):
            ids = idx_v.at[pl.ds(_sc_unit_off(u), SC_LANES)]
            return pltpu.make_async_copy(tab_hbm.at[ids], rows[slot], sems[slot])

        def compute(u, slot):
            tt = u // heads

            def body(c, accs):
                hv = h_v[tt, pl.ds(pl.multiple_of(c * SC_LANES, SC_LANES), SC_LANES)]
                return tuple(accs[r] + _sc_row_piece(rows[slot], r, c) * hv for r in range(SC_LANES))

            accs = lax.fori_loop(0, pieces, body,
                                 tuple(jnp.zeros((SC_LANES,), F32) for _ in range(SC_LANES)))
            for r in range(SC_LANES):
                red_v[pl.ds(r * SC_LANES, SC_LANES)] = accs[r]
            cols = [plsc.load_gather(red_v, [lane * SC_LANES + j]) for j in range(SC_LANES)]
            while len(cols) > 1:
                cols = [cols[i] + cols[i + 1] for i in range(0, len(cols), 2)]
            out_v[pl.ds(_sc_unit_off(u), SC_LANES)] = cols[0]

        @pl.loop(0, groups)
        def _(gi):
            base = wid * tpw + gi * g
            pltpu.sync_copy(idx_hbm.at[pl.ds(base * nsel, g * nsel)], idx_v)
            pltpu.sync_copy(h_hbm.at[pl.ds(base, g)], h_v)
            _sc_ring(units, lambda u, s: copy(u, s).start(), lambda u, s: copy(u, s).wait(), compute)
            pltpu.sync_copy(out_v, out_hbm.at[pl.ds(base * nsel, g * nsel)])

    return k(table, idx_flat, h)


def peer_combine_sc(table, idx_flat, coef_flat, t):
    d = table.shape[1] * table.shape[2]
    nsel = PEER_SEL
    tpw = t // SC_WORKERS
    g = SC_GROUP
    groups = tpw // g
    heads = nsel // SC_LANES
    pieces = d // SC_LANES
    units = g * heads
    row_buf = pltpu.VMEM((SC_LANES, SC_ROW_SUB, SC_ROW_LANE), F32)

    @functools.partial(
        pl.kernel, mesh=_sc_mesh(),
        out_type=jax.ShapeDtypeStruct((t, d), F32),
        scratch_types=[
            pltpu.VMEM((g * nsel,), I32),
            pltpu.VMEM((g * nsel,), F32),
            pltpu.VMEM((g, d), F32),
            [row_buf] * SC_RING,
            [pltpu.SemaphoreType.DMA] * SC_RING,
        ],
        compiler_params=_sc_params(),
        name="peer_combine_sc",
    )
    def k(tab_hbm, idx_hbm, coef_hbm, out_hbm, idx_v, coef_v, y_v, rows, sems):
        wid = _sc_worker_id()

        def copy(u, slot):
            ids = idx_v.at[pl.ds(_sc_unit_off(u), SC_LANES)]
            return pltpu.make_async_copy(tab_hbm.at[ids], rows[slot], sems[slot])

        def compute(u, slot):
            tt = u // heads
            first = (u % heads) == 0
            cs = [plsc.load_gather(coef_v, [jnp.full((SC_LANES,), u * SC_LANES + r, I32)])
                  for r in range(SC_LANES)]

            @plsc.parallel_loop(0, pieces, unroll=2)
            def _(c):
                off = pl.multiple_of(c * SC_LANES, SC_LANES)
                terms = [cs[r] * _sc_row_piece(rows[slot], r, c) for r in range(SC_LANES)]
                while len(terms) > 1:
                    terms = [terms[i] + terms[i + 1] for i in range(0, len(terms), 2)]
                prev = y_v[tt, pl.ds(off, SC_LANES)]
                y_v[tt, pl.ds(off, SC_LANES)] = terms[0] + jnp.where(first, 0.0, prev)

        @pl.loop(0, groups)
        def _(gi):
            base = wid * tpw + gi * g
            pltpu.sync_copy(idx_hbm.at[pl.ds(base * nsel, g * nsel)], idx_v)
            pltpu.sync_copy(coef_hbm.at[pl.ds(base * nsel, g * nsel)], coef_v)
            _sc_ring(units, lambda u, s: copy(u, s).start(), lambda u, s: copy(u, s).wait(), compute)
            pltpu.sync_copy(y_v, out_hbm.at[pl.ds(base, g)])

    return k(table, idx_flat, coef_flat)


GELU_C0 = math.sqrt(2.0 / math.pi)
GELU_C1 = 0.044715


def _gelu_tanh(x):
    z = GELU_C0 * (x + GELU_C1 * (x * x * x))
    th = 1.0 - 2.0 / (jnp.exp(2.0 * z) + 1.0)
    return 0.5 * x * (1.0 + th)


def peer_experts_sc(tab_u, tab_v, idx_flat, w_flat, h):
    t, d = h.shape
    nsel = PEER_SEL
    tpw = t // SC_WORKERS
    g = SC_GROUP
    groups = tpw // g
    heads = nsel // SC_LANES
    pieces = d // SC_LANES
    units = g * heads
    row_buf = pltpu.VMEM((SC_LANES, SC_ROW_SUB, SC_ROW_LANE), F32)

    @functools.partial(
        pl.kernel, mesh=_sc_mesh(),
        out_type=jax.ShapeDtypeStruct((t, d), F32),
        scratch_types=[
            pltpu.VMEM((g * nsel,), I32),
            pltpu.VMEM((g * nsel,), F32),
            pltpu.VMEM((g, d), F32),
            pltpu.VMEM((g, d), F32),
            pltpu.VMEM((SC_LANES * SC_LANES,), F32),
            [row_buf] * SC_RING,
            [pltpu.SemaphoreType.DMA] * SC_RING,
        ],
        compiler_params=_sc_params(),
        name="peer_experts_sc",
    )
    def k(u_hbm, v_hbm, idx_hbm, w_hbm, h_hbm, out_hbm, idx_v, coef_v, h_v, y_v, red_v, rows, sems):
        wid = _sc_worker_id()
        lane = lax.iota(I32, SC_LANES)

        def copy(tab_hbm, u, slot):
            ids = idx_v.at[pl.ds(_sc_unit_off(u), SC_LANES)]
            return pltpu.make_async_copy(tab_hbm.at[ids], rows[slot], sems[slot])

        def dots(u, slot):
            tt = u // heads

            def body(c, accs):
                hv = h_v[tt, pl.ds(pl.multiple_of(c * SC_LANES, SC_LANES), SC_LANES)]
                return tuple(accs[r] + _sc_row_piece(rows[slot], r, c) * hv for r in range(SC_LANES))

            accs = lax.fori_loop(0, pieces, body,
                                 tuple(jnp.zeros((SC_LANES,), F32) for _ in range(SC_LANES)))
            for r in range(SC_LANES):
                red_v[pl.ds(r * SC_LANES, SC_LANES)] = accs[r]
            cols = [plsc.load_gather(red_v, [lane * SC_LANES + j]) for j in range(SC_LANES)]
            while len(cols) > 1:
                cols = [cols[i] + cols[i + 1] for i in range(0, len(cols), 2)]
            sl = pl.ds(_sc_unit_off(u), SC_LANES)
            coef_v[sl] = coef_v[sl] * _gelu_tanh(cols[0])

        def combine(u, slot):
            tt = u // heads
            first = (u % heads) == 0
            cs = [plsc.load_gather(coef_v, [jnp.full((SC_LANES,), u * SC_LANES + r, I32)])
                  for r in range(SC_LANES)]

            @plsc.parallel_loop(0, pieces, unroll=2)
            def _(c):
                off = pl.multiple_of(c * SC_LANES, SC_LANES)
                terms = [cs[r] * _sc_row_piece(rows[slot], r, c) for r in range(SC_LANES)]
                while len(terms) > 1:
                    terms = [terms[i] + terms[i + 1] for i in range(0, len(terms), 2)]
                prev = y_v[tt, pl.ds(off, SC_LANES)]
                y_v[tt, pl.ds(off, SC_LANES)] = terms[0] + jnp.where(first, 0.0, prev)

        @pl.loop(0, groups)
        def _(gi):
            base = wid * tpw + gi * g
            pltpu.sync_copy(idx_hbm.at[pl.ds(base * nsel, g * nsel)], idx_v)
            pltpu.sync_copy(w_hbm.at[pl.ds(base * nsel, g * nsel)], coef_v)
            pltpu.sync_copy(h_hbm.at[pl.ds(base, g)], h_v)
            _sc_ring(units, lambda u, s: copy(u_hbm, u, s).start(), lambda u, s: copy(u_hbm, u, s).wait(), dots)
            _sc_ring(units, lambda u, s: copy(v_hbm, u, s).start(), lambda u, s: copy(v_hbm, u, s).wait(), combine)
            pltpu.sync_copy(y_v, out_hbm.at[pl.ds(base, g)])

    return k(tab_u, tab_v, idx_flat, w_flat, h)


SC_PK_RING = 4
SC_PK_SUB = 4
HI_MASK = -65536


def pack_bf16_pairs(a):
    half = a.shape[1] // 2
    bits = lax.bitcast_convert_type(a.astype(BF16), jnp.uint16).astype(jnp.uint32)
    return lax.bitcast_convert_type(bits[:, :half] | (bits[:, half:] << 16), I32)


def _pack_tables_kernel(u_ref, v_ref, o_ref):
    for part, ref in enumerate((u_ref, v_ref)):
        words = _pack_bf16_halves(ref[...])
        for sub in range(SC_PK_SUB):
            o_ref[:, part * SC_PK_SUB + sub, :] = words[:, sub * SC_ROW_LANE:(sub + 1) * SC_ROW_LANE]


def pack_expert_tables(u, v, te=512):
    e, d = u.shape
    assert d == 2 * SC_PK_SUB * SC_ROW_LANE
    spec = pl.BlockSpec((te, d), lambda i: (i, 0))
    return pl.pallas_call(
        _pack_tables_kernel, grid=(e // te,), in_specs=[spec, spec],
        out_specs=pl.BlockSpec((te, 2 * SC_PK_SUB, SC_ROW_LANE), lambda i: (i, 0, 0)),
        out_shape=jax.ShapeDtypeStruct((e, 2 * SC_PK_SUB, SC_ROW_LANE), I32),
        compiler_params=_cparams(("parallel",)), name="pack_expert_tables",
    )(u, v)


def _unpack_halves(x32):
    w = plsc.bitcast(x32, I32)
    return plsc.bitcast(w << 16, F32), plsc.bitcast(w & HI_MASK, F32)


def _tree_sum(xs):
    while len(xs) > 1:
        xs = [xs[i] + xs[i + 1] for i in range(0, len(xs), 2)]
    return xs[0]


def peer_experts_pk_sc(tab_uv, idx_flat, w_flat, hp, d):
    t = hp.shape[0]
    nsel = PEER_SEL
    tpw = t // SC_WORKERS
    g = SC_GROUP
    groups = tpw // g
    heads = nsel // SC_LANES
    chunks = d // 32
    units = g * heads
    ring = SC_PK_RING
    row_buf = pltpu.VMEM((SC_LANES, 2 * SC_PK_SUB, SC_ROW_LANE), I32)

    def row_words(rows, r, wc, sub0):
        per = SC_ROW_LANE // SC_LANES
        return plsc.bitcast(
            rows[r, sub0 + wc // per, pl.ds(pl.multiple_of((wc % per) * SC_LANES, SC_LANES), SC_LANES)], BF16)

    def ring_loop(n_units, start, wait, compute):
        for u in range(ring - 1):
            start(u, u)

        @pl.loop(0, n_units, step=ring)
        def _(uu):
            for b in range(ring):
                u = uu + b
                nxt = u + (ring - 1)

                @pl.when(nxt < n_units)
                def _():
                    start(nxt, (b + ring - 1) % ring)

                wait(u, b)
                compute(u, b)

    @functools.partial(
        pl.kernel, mesh=_sc_mesh(),
        out_type=jax.ShapeDtypeStruct((t, d), F32),
        scratch_types=[
            pltpu.VMEM((g * nsel,), I32),
            pltpu.VMEM((g * nsel,), F32),
            pltpu.VMEM((g, d // 2), I32),
            pltpu.VMEM((g, d), F32),
            pltpu.VMEM((SC_LANES * SC_LANES,), F32),
            [row_buf] * ring,
            [pltpu.SemaphoreType.DMA] * ring,
        ],
        compiler_params=_sc_params(),
        name="peer_experts_pk_sc",
    )
    def k(tab_hbm, idx_hbm, w_hbm, h_hbm, out_hbm, idx_v, coef_v, h_v, y_v, red_v, rows, sems):
        wid = _sc_worker_id()
        lane = lax.iota(I32, SC_LANES)

        def copy(u, slot):
            ids = idx_v.at[pl.ds(_sc_unit_off(u), SC_LANES)]
            return pltpu.make_async_copy(tab_hbm.at[ids], rows[slot], sems[slot])

        def dots(u, slot):
            tt = u // heads

            def body(cp, accs):
                out = []
                hv = [plsc.bitcast(h_v[tt, pl.ds(pl.multiple_of((2 * cp + i) * SC_LANES, SC_LANES), SC_LANES)], BF16)
                      for i in range(2)]
                for r in range(SC_LANES):
                    pr = (row_words(rows[slot], r, 2 * cp, 0) * hv[0]
                          + row_words(rows[slot], r, 2 * cp + 1, 0) * hv[1])
                    lo, hi = _unpack_halves(pr)
                    out.append(accs[r] + lo + hi)
                return tuple(out)

            accs = lax.fori_loop(0, chunks // 2, body,
                                 tuple(jnp.zeros((SC_LANES,), F32) for _ in range(SC_LANES)))
            for r in range(SC_LANES):
                red_v[pl.ds(r * SC_LANES, SC_LANES)] = accs[r]
            act = _tree_sum([plsc.load_gather(red_v, [lane * SC_LANES + j]) for j in range(SC_LANES)])
            sl = pl.ds(_sc_unit_off(u), SC_LANES)
            coef_v[sl] = coef_v[sl] * _gelu_tanh(act)

        def combine(u, slot):
            tt = u // heads
            first = (u % heads) == 0
            cb = []
            for r in range(SC_LANES):
                c = plsc.load_gather(coef_v, [jnp.full((SC_LANES,), u * SC_LANES + r, I32)])
                cb.append(plsc.pack(c, c, format=plsc.PackFormat.INTERLEAVED))

            @plsc.parallel_loop(0, chunks, unroll=2)
            def _(wc):
                lo, hi = _unpack_halves(
                    _tree_sum([cb[r] * row_words(rows[slot], r, wc, SC_PK_SUB) for r in range(SC_LANES)]))
                for half, val in ((0, lo), (1, hi)):
                    sl = pl.ds(pl.multiple_of(half * (d // 2) + wc * SC_LANES, SC_LANES), SC_LANES)
                    y_v[tt, sl] = val + jnp.where(first, 0.0, y_v[tt, sl])

        def unit(u, slot):
            dots(u, slot)
            combine(u, slot)

        @pl.loop(0, groups)
        def _(gi):
            base = wid * tpw + gi * g
            pltpu.sync_copy(idx_hbm.at[pl.ds(base * nsel, g * nsel)], idx_v)
            pltpu.sync_copy(w_hbm.at[pl.ds(base * nsel, g * nsel)], coef_v)
            pltpu.sync_copy(h_hbm.at[pl.ds(base, g)], h_v)
            ring_loop(units, lambda u, s: copy(u, s).start(), lambda u, s: copy(u, s).wait(), unit)
            pltpu.sync_copy(y_v, out_hbm.at[pl.ds(base, g)])

    return k(tab_uv, idx_flat, w_flat, hp)


def kernel(x, mem, rel_bias, ln_mix, w_in, hg_lower, hg_norm, w_up_a, w_up_b, w_out, ln_cross, ln_mem, wq_x, wk_x, wv_x, wo_x, ln_ffn, peer_query, peer_subkeys, peer_u, peer_v, ln_final):
    b, s, d = x.shape
    depth = w_in.shape[0]
    assert depth == 1, "the residual after PEER is fused into the final norm"
    assert s % MB_BLOCK == 0 and s % HG_CHUNK == 0 and s % (PEER_SLICES * SC_WORKERS * SC_GROUP) == 0
    nb = s // MB_BLOCK
    row = lambda a: a.reshape(1, -1).astype(F32)
    lb_all = jnp.cumsum(jax.nn.softmax(hg_lower.astype(F32), axis=0), axis=0)
    bias = moba_bias_tiles(rel_bias)
    n_hg = 4 * HG_WIDTH
    n_qk = 2 * MB_WIDTH
    n_mb = 3 * MB_WIDTH
    l = 0
    w = w_in[l].astype(BF16)
    w_hg, w_qk, w_vt, w_g = w[:, :n_hg], w[:, n_hg:n_hg + n_qk], w[:, n_hg + n_qk:n_hg + n_mb].T, w[:, n_hg + n_mb:]
    wa, wb, wo = w_up_a[l].astype(BF16), w_up_b[l].astype(BF16), w_out[l].astype(BF16)
    wqx, wox = wq_x[l].astype(BF16), wo_x[l].astype(BF16)
    wpq, sk = peer_query[l].astype(BF16), peer_subkeys[l].astype(F32)
    tab3 = lambda a: pack_bf16_pairs(a.astype(F32)).reshape(a.shape[0], SC_PK_SUB, SC_ROW_LANE)
    tab_uv = pack_expert_tables(peer_u[l].astype(F32), peer_v[l].astype(F32))
    kx, vx = mem_kv(mem, row(ln_mem[l]), wk_x[l].astype(BF16), wv_x[l].astype(BF16))

    outs = []
    for bi in range(b):
        x2d = x[bi]
        p0, pqk, vt, pg = in_proj(x2d, row(ln_mix[l]), w_hg, w_qk, w_vt, w_g)
        ya = hgrn2(p0, row(lb_all[l]), row(hg_norm[l]), 1, s)
        km = moba_kmean(pqk, 1, s).reshape(1, nb, MB_WIDTH)
        ts = s // PEER_SLICES
        for tok0 in range(0, s, ts):
            yb = moba_attention(pqk, vt, km, bias, 1, s, tok0 // MB_BLOCK, ts // MB_BLOCK)
            xs = mix_out(x2d, ya, yb, pg, wa, wb, wo, tok0)
            xs = cross_attn(xs, row(ln_cross[l]), wqx, kx[bi:bi + 1], vx[bi:bi + 1], wox, ts)
            hp, eidx, wts = peer_route(xs, row(ln_ffn[l]), wpq, sk, 0, ts)
            y = peer_experts_pk_sc(tab_uv, eidx.reshape(ts * PEER_SEL), wts.reshape(ts * PEER_SEL), hp, d)
            outs.append(final_norm(xs, y, row(ln_final), 0))
    return jnp.concatenate(outs, axis=0).reshape(b, s, d)
```

```python
import functools
import math

import jax
import jax.numpy as jnp
import numpy as np
from jax import lax
from jax.experimental import pallas as pl
from jax.experimental.pallas import tpu as pltpu
from jax.experimental.pallas import tpu_sc as plsc

F32 = jnp.float32
BF16 = jnp.bfloat16
I32 = jnp.int32
EPS = 1e-6
NEG_INF = float("-inf")

HG_HEADS = 4
HG_D = 128
HG_WIDTH = HG_HEADS * HG_D
HG_CHUNK = 64
HG_SUB = 16
MB_HEADS = 8
MB_DH = 64
MB_WIDTH = MB_HEADS * MB_DH
MB_BLOCK = 256
MB_TOPK = 3
MB_BIAS_TILES = 8
REL_BUCKETS = 32
REL_MAX_DIST = 2048
X_HEADS = 4
PEER_HEADS = 8
PEER_NKEYS = 128
PEER_TOPK = 16
PEER_HALF = 128
PEER_SEL = PEER_HEADS * PEER_TOPK
PEER_SLICES = 4

VMEM_LIMIT = 56 * 1024 * 1024


def _cparams(sem):
    return pltpu.CompilerParams(dimension_semantics=sem, vmem_limit_bytes=VMEM_LIMIT)


def _rms(x, g):
    ms = jnp.mean(x * x, axis=-1, keepdims=True)
    return x * lax.rsqrt(ms + EPS) * g


def _in_proj_kernel(x_ref, g_ref, w0_ref, w1_ref, wvt_ref, w2_ref, o0_ref, o1_ref, ovt_ref, o2_ref):
    h = _rms(x_ref[...], g_ref[...]).astype(BF16)
    o0_ref[...] = jnp.dot(h, w0_ref[...], preferred_element_type=F32)
    o1_ref[...] = jnp.dot(h, w1_ref[...], preferred_element_type=F32).astype(BF16)
    vt = lax.dot_general(wvt_ref[...], h, (((1,), (1,)), ((), ())), preferred_element_type=F32).astype(BF16)
    for hd in range(MB_HEADS):
        ovt_ref[0, hd * MB_VROWS:hd * MB_VROWS + MB_DH, :] = vt[hd * MB_DH:(hd + 1) * MB_DH]
        ovt_ref[0, hd * MB_VROWS + MB_DH:(hd + 1) * MB_VROWS, :] = jnp.ones((MB_ONES, vt.shape[1]), BF16)
    o2_ref[...] = jnp.dot(h, w2_ref[...], preferred_element_type=F32).astype(BF16)


def in_proj(x2d, g, w0, w1, wvt, w2):
    t, d = x2d.shape
    tm = MB_BLOCK
    assert wvt.shape[0] == MB_WIDTH
    n0, n1, nv, n2 = w0.shape[1], w1.shape[1], MB_VT_ROWS, w2.shape[1]
    full = lambda a: pl.BlockSpec(a.shape, lambda i: (0, 0))
    return pl.pallas_call(
        _in_proj_kernel,
        grid=(t // tm,),
        in_specs=[pl.BlockSpec((tm, d), lambda i: (i, 0)), full(g), full(w0), full(w1), full(wvt), full(w2)],
        out_specs=[pl.BlockSpec((tm, n0), lambda i: (i, 0)),
                   pl.BlockSpec((tm, n1), lambda i: (i, 0)),
                   pl.BlockSpec((1, nv, tm), lambda i: (i, 0, 0)),
                   pl.BlockSpec((tm, n2), lambda i: (i, 0))],
        out_shape=[jax.ShapeDtypeStruct((t, n0), F32),
                   jax.ShapeDtypeStruct((t, n1), BF16),
                   jax.ShapeDtypeStruct((t // tm, nv, tm), BF16),
                   jax.ShapeDtypeStruct((t, n2), BF16)],
        compiler_params=_cparams(("parallel",)),
        name="in_proj",
    )(x2d, g, w0, w1, wvt, w2)


def _hgrn_kernel(q_ref, f_ref, i_ref, g_ref, lb_ref, gain_ref, st0_ref, o_ref, stn_ref, st_ref):
    c = pl.program_id(0)

    @pl.when(c == 0)
    def _():
        st_ref[...] = st0_ref[...]

    C, S = HG_CHUNK, HG_SUB
    row = lax.broadcasted_iota(I32, (C, C), 0)
    col = lax.broadcasted_iota(I32, (C, C), 1)
    tril = (row >= col).astype(F32)
    t_iota = lax.broadcasted_iota(I32, (S, 1), 0)

    for h in range(HG_HEADS):
        sl = slice(h * HG_D, (h + 1) * HG_D)
        q = q_ref[:, sl]
        v = i_ref[:, sl]
        lb = lb_ref[:, sl]
        f = lb + (1.0 - lb) * jax.nn.sigmoid(f_ref[:, sl])
        lf = jnp.log(f)
        k = 1.0 - f
        b = jnp.dot(tril, lf, precision=lax.Precision.HIGHEST, preferred_element_type=F32)
        st = st_ref[h]
        vb = v.astype(BF16)
        qd = (q * jnp.exp(b)).astype(BF16)
        o_inter = lax.dot_general(qd, st.astype(BF16), (((1,), (1,)), ((), ())),
                                  preferred_element_type=F32)
        outs = []
        for i in range(C // S):
            r0 = i * S
            qi = q[r0:r0 + S]
            ki = k[r0:r0 + S]
            bi = b[r0:r0 + S]
            vi = v[r0:r0 + S]
            oi = o_inter[r0:r0 + S]
            if i > 0:
                bs = b[r0 - 1:r0]
                qh = (qi * jnp.exp(bi - bs)).astype(BF16)
                kh = (k[:r0] * jnp.exp(bs - b[:r0])).astype(BF16)
                a = lax.dot_general(qh, kh, (((1,), (1,)), ((), ())), preferred_element_type=F32)
                oi = oi + jnp.dot(a.astype(BF16), vb[:r0], preferred_element_type=F32)
            half = S // 2
            o_half = [oi[:half], oi[half:]]
            for s in range(S):
                for hf in range(s // half, 2):
                    rows = slice(hf * half, (hf + 1) * half)
                    dec = jnp.exp(jnp.minimum(bi[rows] - bi[s:s + 1], 0.0))
                    a_s = jnp.sum(qi[rows] * ki[s:s + 1] * dec, axis=-1, keepdims=True)
                    a_s = jnp.where(t_iota[rows] >= s, a_s, 0.0)
                    o_half[hf] = o_half[hf] + a_s * vi[s:s + 1]
            outs.extend(o_half)
        o = jnp.concatenate(outs, axis=0)
        b_end = b[C - 1:C]
        kd = (k * jnp.exp(b_end - b)).astype(BF16)
        upd = lax.dot_general(vb, kd, (((0,), (0,)), ((), ())), preferred_element_type=F32)
        st_ref[h] = st * jnp.exp(b_end) + upd
        o = o * lax.rsqrt(jnp.mean(o * o, axis=-1, keepdims=True) + EPS)
        g = g_ref[:, sl]
        o_ref[:, sl] = (o * gain_ref[:, sl] * (g * jax.nn.sigmoid(g))).astype(o_ref.dtype)

    @pl.when(c == pl.num_programs(0) - 1)
    def _():
        stn_ref[...] = st_ref[...]


def hgrn2(p0, lb, gain, state, tok0, t):
    nc = t // HG_CHUNK
    c0 = tok0 // HG_CHUNK
    w = HG_WIDTH

    def col(j):
        return pl.BlockSpec((HG_CHUNK, w), lambda c, j=j: (c0 + c, j))

    st_spec = pl.BlockSpec(state.shape, lambda c: (0, 0, 0))
    return pl.pallas_call(
        _hgrn_kernel,
        grid=(nc,),
        in_specs=[col(0), col(1), col(2), col(3),
                  pl.BlockSpec((1, w), lambda c: (0, 0)),
                  pl.BlockSpec((1, w), lambda c: (0, 0)),
                  st_spec],
        out_specs=[pl.BlockSpec((HG_CHUNK, w), lambda c: (c, 0)), st_spec],
        out_shape=[jax.ShapeDtypeStruct((t, w), BF16), jax.ShapeDtypeStruct(state.shape, F32)],
        scratch_shapes=[pltpu.VMEM((HG_HEADS, HG_D, HG_D), F32)],
        compiler_params=_cparams(("arbitrary",)),
        name="hgrn2",
    )(p0, p0, p0, p0, lb, gain, state)


def _kmean_kernel(k_ref, o_ref):
    o_ref[0] = jnp.mean(k_ref[...].astype(F32), axis=0, keepdims=True)


def moba_kmean(p1, batch, seq):
    nbt = p1.shape[0] // MB_BLOCK
    return pl.pallas_call(
        _kmean_kernel,
        grid=(nbt,),
        in_specs=[pl.BlockSpec((MB_BLOCK, MB_WIDTH), lambda i: (i, 1))],
        out_specs=pl.BlockSpec((1, 1, MB_WIDTH), lambda i: (i, 0, 0)),
        out_shape=jax.ShapeDtypeStruct((nbt, 1, MB_WIDTH), F32),
        compiler_params=_cparams(("parallel",)),
        name="moba_kmean",
    )(p1)


MB_PAIR = 4
MB_PW = MB_PAIR * MB_DH
MB_LG = 128
MB_ONES = 16
MB_VROWS = MB_DH + MB_ONES
MB_VT_ROWS = MB_HEADS * MB_VROWS


def _moba_kernel(q_ref, k_ref, vt_ref, km_ref, bias_ref, o_ref, *scratch, qb0):
    m_ref, l_ref, al_ref, acc_ref, msk_ref, s_ref, p_ref = (
        scratch[i * MB_PAIR:(i + 1) * MB_PAIR] for i in range(7))
    qi = pl.program_id(2) + qb0
    nb = km_ref.shape[0]
    blk = MB_BLOCK
    heads = range(MB_PAIR)
    grp = lambda hh: slice((hh // 2) * MB_LG, (hh // 2 + 1) * MB_LG)
    q = q_ref[...]
    lane = lax.broadcasted_iota(I32, (blk, MB_LG), 1)
    in_head = [(lane < MB_DH) if hh % 2 == 0 else (lane >= MB_DH) for hh in heads]
    qs = q * jnp.asarray(MB_DH ** -0.5, BF16)
    nt = (((1,), (1,)), ((), ()))
    qf = q.astype(F32)
    qht = [jnp.where(in_head[hh], qs[:, grp(hh)].astype(F32), 0.0).T.astype(BF16) for hh in heads]

    n_io = lax.broadcasted_iota(I32, (nb, blk), 0)
    for hh in heads:
        gate = lax.dot_general(km_ref[:, grp(hh)], jnp.where(in_head[hh], qf[:, grp(hh)], 0.0), nt,
                               precision=lax.Precision.HIGHEST, preferred_element_type=F32)
        gate = jnp.where(n_io < qi, gate, NEG_INF)
        chosen = n_io < 0
        for _ in range(MB_TOPK):
            mx = jnp.max(gate, axis=0, keepdims=True)
            ix = jnp.min(jnp.where(gate == mx, n_io, nb), axis=0, keepdims=True)
            hit = n_io == ix
            chosen = chosen | (hit & (mx > NEG_INF))
            gate = jnp.where(hit, NEG_INF, gate)
        msk_ref[hh][...] = jnp.where(chosen, 0.0, NEG_INF)

    vrows = lambda hh: slice(hh * MB_VROWS, (hh + 1) * MB_VROWS)

    def pv_stage(blk_idx):
        vtb = vt_ref[blk_idx]
        r = [jnp.dot(vtb[vrows(hh)], p_ref[hh][...], preferred_element_type=F32) for hh in heads]
        al = [al_ref[hh][...] for hh in heads]
        a_new = [al[hh] * acc_ref[hh][...] + r[hh][:MB_DH] for hh in heads]
        l_new = [al[hh] * l_ref[hh][...] + r[hh][MB_DH:MB_DH + 1] for hh in heads]
        return a_new, l_new

    def store_pv(a_new, l_new):
        for hh in heads:
            acc_ref[hh][...] = a_new[hh]
            l_ref[hh][...] = l_new[hh]

    def softmax_stage():
        s = [s_ref[hh][...] for hh in heads]
        m_old = [m_ref[hh][...] for hh in heads]
        m_new = [jnp.maximum(m_old[hh], jnp.max(s[hh], axis=0, keepdims=True)) for hh in heads]
        alpha = [jnp.exp(m_old[hh] - m_new[hh]) for hh in heads]
        p = [jnp.exp((s[hh] - m_new[hh]).astype(BF16)) for hh in heads]
        return p, alpha, m_new

    def store_softmax(p, alpha, m_new):
        for hh in heads:
            p_ref[hh][...] = p[hh]
            al_ref[hh][...] = alpha[hh]
            m_ref[hh][...] = m_new[hh]

    k_own = k_ref[pl.ds(pl.multiple_of(qi * blk, blk), blk), :]
    key_io = lax.broadcasted_iota(I32, (blk, blk), 0)
    qry_io = lax.broadcasted_iota(I32, (blk, blk), 1)
    for hh in heads:
        s = jnp.dot(k_own[:, grp(hh)], qht[hh], preferred_element_type=F32) + bias_ref[hh, 0]
        s_ref[hh][...] = jnp.where(key_io <= qry_io, s, NEG_INF)
        m_ref[hh][...] = jnp.full((1, blk), NEG_INF, F32)
        l_ref[hh][...] = jnp.zeros((1, blk), F32)
        al_ref[hh][...] = jnp.ones((1, blk), F32)
        acc_ref[hh][...] = jnp.zeros((MB_DH, blk), F32)
        p_ref[hh][...] = jnp.zeros((blk, blk), BF16)

    def step(i, carry, far):
        pv = pv_stage(jnp.where(i <= 1, qi, i - 2))
        sm = softmax_stage()
        kn = k_ref[pl.ds(pl.multiple_of(i * blk, blk), blk), :]
        if far:
            row = [msk_ref[hh][pl.ds(i, 1), :] + bias_ref[hh, MB_BIAS_TILES - 1, 0:1, 0:1] for hh in heads]
            s_next = [jnp.dot(kn[:, grp(hh)], qht[hh], preferred_element_type=F32) + row[hh] for hh in heads]
        else:
            d = qi - i
            s_next = [jnp.dot(kn[:, grp(hh)], qht[hh], preferred_element_type=F32)
                      + bias_ref[hh, d] + msk_ref[hh][pl.ds(i, 1), :] for hh in heads]
        store_pv(*pv)
        for hh in heads:
            s_ref[hh][...] = s_next[hh]
        store_softmax(*sm)
        return carry

    n_far = jnp.maximum(qi - (MB_BIAS_TILES - 2), 0)
    lax.fori_loop(0, n_far, functools.partial(step, far=True), 0)
    lax.fori_loop(n_far, qi, functools.partial(step, far=False), 0)
    pv = pv_stage(jnp.where(qi <= 1, qi, qi - 2))
    sm = softmax_stage()
    store_pv(*pv)
    store_softmax(*sm)
    a_fin, l_fin = pv_stage(jnp.where(qi == 0, qi, qi - 1))
    out_t = jnp.concatenate([a_fin[hh] / l_fin[hh] for hh in heads], axis=0)
    o_ref[...] = out_t.T.astype(o_ref.dtype)


def moba_attention(pqk, vt, km, bias, batch, seq, qb0=0, nqb=None):
    nb = seq // MB_BLOCK
    nqb = nb if nqb is None else nqb
    t = batch * nqb * MB_BLOCK
    groups = MB_WIDTH // MB_PW
    return pl.pallas_call(
        functools.partial(_moba_kernel, qb0=qb0),
        grid=(batch, groups, nqb),
        in_specs=[
            pl.BlockSpec((MB_BLOCK, MB_PW), lambda b, j, i: (b * nb + qb0 + i, j)),
            pl.BlockSpec((seq, MB_PW), lambda b, j, i: (b, groups + j)),
            pl.BlockSpec((nb, MB_PAIR * MB_VROWS, MB_BLOCK), lambda b, j, i: (b, j, 0)),
            pl.BlockSpec((None, nb, MB_PW), lambda b, j, i: (b, 0, j)),
            pl.BlockSpec((MB_PAIR, MB_BIAS_TILES, MB_BLOCK, MB_BLOCK), lambda b, j, i: (j, 0, 0, 0)),
        ],
        out_specs=pl.BlockSpec((MB_BLOCK, MB_PW), lambda b, j, i: (b * nqb + i, j)),
        out_shape=jax.ShapeDtypeStruct((t, MB_WIDTH), BF16),
        scratch_shapes=(
            [pltpu.VMEM((1, MB_BLOCK), F32)] * (3 * MB_PAIR)
            + [pltpu.VMEM((MB_DH, MB_BLOCK), F32)] * MB_PAIR
            + [pltpu.VMEM((nb, MB_BLOCK), F32)] * MB_PAIR
            + [pltpu.VMEM((MB_BLOCK, MB_BLOCK), F32)] * MB_PAIR
            + [pltpu.VMEM((MB_BLOCK, MB_BLOCK), BF16)] * MB_PAIR
        ),
        compiler_params=_cparams(("parallel", "parallel", "arbitrary")),
        name="moba_attn",
    )(pqk, pqk, vt, km, bias)


def _t5_bucket(dist):
    max_exact = REL_BUCKETS // 2
    scaled = jnp.log(jnp.maximum(dist, 1).astype(F32) / max_exact) / math.log(REL_MAX_DIST / max_exact)
    large = jnp.minimum(max_exact + (scaled * (REL_BUCKETS - max_exact)).astype(I32), REL_BUCKETS - 1)
    return jnp.where(dist < max_exact, dist, large)


def moba_bias_tiles(rel_bias):
    blk = MB_BLOCK
    span = 2 * blk - 1
    x = jnp.arange(span) - (blk - 1)
    dist = jnp.maximum(jnp.arange(MB_BIAS_TILES)[:, None] * blk + x[None, :], 0)
    w = rel_bias.astype(F32).T[:, _t5_bucket(dist)]
    h = w.shape[0]
    wp = jnp.pad(w, ((0, 0), (0, 0), (0, 1)))
    a = jnp.broadcast_to(wp[:, :, None, :], (h, MB_BIAS_TILES, blk, span + 1))
    a = a.reshape(h, MB_BIAS_TILES, blk * (span + 1))[:, :, :blk * span]
    return a.reshape(h, MB_BIAS_TILES, blk, span)[:, :, :, blk - 1:]


def _mix_kernel(x_ref, ya_ref, yb_ref, ga_ref, gb_ref, wa_ref, wb_ref, wo_ref, o_ref):
    za = jnp.dot(ya_ref[...], wa_ref[...], preferred_element_type=F32)
    zb = jnp.dot(yb_ref[...], wb_ref[...], preferred_element_type=F32)
    z = jax.nn.sigmoid(ga_ref[...].astype(F32)) * za + jax.nn.sigmoid(gb_ref[...].astype(F32)) * zb
    o_ref[...] = x_ref[...] + jnp.dot(z.astype(BF16), wo_ref[...], preferred_element_type=F32)


def mix_out(x2d, ya, yb, pg, wa, wb, wo, tok0=0, tm=512):
    t = yb.shape[0]
    d = x2d.shape[1]
    w = ya.shape[1]
    b0 = tok0 // tm
    return pl.pallas_call(
        _mix_kernel,
        grid=(t // tm,),
        in_specs=[
            pl.BlockSpec((tm, d), lambda i: (b0 + i, 0)),
            pl.BlockSpec((tm, w), lambda i: (i, 0)),
            pl.BlockSpec((tm, w), lambda i: (i, 0)),
            pl.BlockSpec((tm, d), lambda i: (b0 + i, 0)),
            pl.BlockSpec((tm, d), lambda i: (b0 + i, 1)),
            pl.BlockSpec((w, d), lambda i: (0, 0)),
            pl.BlockSpec((w, d), lambda i: (0, 0)),
            pl.BlockSpec((d, d), lambda i: (0, 0)),
        ],
        out_specs=pl.BlockSpec((tm, d), lambda i: (i, 0)),
        out_shape=jax.ShapeDtypeStruct((t, d), F32),
        compiler_params=_cparams(("parallel",)),
        name="mix_out",
    )(x2d, ya, yb, pg, pg, wa, wb, wo)


def _mem_kv_kernel(m_ref, g_ref, wk_ref, wv_ref, k_ref, v_ref):
    mn = _rms(m_ref[...], g_ref[...]).astype(BF16)
    k_ref[...] = jnp.dot(mn, wk_ref[...], preferred_element_type=F32).astype(BF16)
    v_ref[...] = jnp.dot(mn, wv_ref[...], preferred_element_type=F32).astype(BF16)


def mem_kv(mem, g, wk, wv):
    b, m, d = mem.shape
    spec = pl.BlockSpec((None, m, d), lambda i: (i, 0, 0))
    wspec = pl.BlockSpec((d, d), lambda i: (0, 0))
    return pl.pallas_call(
        _mem_kv_kernel,
        grid=(b,),
        in_specs=[spec, pl.BlockSpec((1, d), lambda i: (0, 0)), wspec, wspec],
        out_specs=[spec, spec],
        out_shape=[jax.ShapeDtypeStruct((b, m, d), BF16)] * 2,
        compiler_params=_cparams(("parallel",)),
        name="mem_kv",
    )(mem, g, wk, wv)


def _cross_kernel(x_ref, g_ref, wq_ref, k_ref, v_ref, wo_ref, o_ref):
    x = x_ref[...]
    d = x.shape[1]
    dh = d // X_HEADS
    h = _rms(x, g_ref[...]).astype(BF16)
    q = (jnp.dot(h, wq_ref[...], preferred_element_type=F32) * (dh ** -0.5)).astype(BF16)
    outs = []
    for hh in range(X_HEADS):
        sl = slice(hh * dh, (hh + 1) * dh)
        s = lax.dot_general(q[:, sl], k_ref[:, sl], (((1,), (1,)), ((), ())),
                            preferred_element_type=F32)
        p = jnp.exp(s - jnp.max(s, axis=1, keepdims=True))
        l = jnp.sum(p, axis=1, keepdims=True)
        o = jnp.dot(p.astype(BF16), v_ref[:, sl], preferred_element_type=F32) / l
        outs.append(o.astype(BF16))
    o = jnp.concatenate(outs, axis=1)
    o_ref[...] = x + jnp.dot(o, wo_ref[...], preferred_element_type=F32)


def cross_attn(x2d, g, wq, kx, vx, wo, seq, tm=512):
    t, d = x2d.shape
    m = kx.shape[1]
    per_b = seq // tm
    kv = pl.BlockSpec((None, m, d), lambda i: (i // per_b, 0, 0))
    wspec = pl.BlockSpec((d, d), lambda i: (0, 0))
    return pl.pallas_call(
        _cross_kernel,
        grid=(t // tm,),
        in_specs=[pl.BlockSpec((tm, d), lambda i: (i, 0)), pl.BlockSpec((1, d), lambda i: (0, 0)),
                  wspec, kv, kv, wspec],
        out_specs=pl.BlockSpec((tm, d), lambda i: (i, 0)),
        out_shape=jax.ShapeDtypeStruct((t, d), F32),
        compiler_params=_cparams(("parallel",)),
        name="cross_attn",
    )(x2d, g, wq, kx, vx, wo)


def _topk_rows(sc, k):
    n = sc.shape[0]
    io = lax.broadcasted_iota(I32, sc.shape, 0)
    vals, ids = [], []
    for _ in range(k):
        m = jnp.max(sc, axis=0, keepdims=True)
        ix = jnp.min(jnp.where(sc == m, io, n), axis=0, keepdims=True)
        vals.append(m)
        ids.append(ix)
        sc = jnp.where(io == ix, NEG_INF, sc)
    return jnp.concatenate(vals, axis=0), jnp.concatenate(ids, axis=0)


def _pack_bf16_halves(h):
    bits = lax.bitcast_convert_type(h, I32)
    r = bits + 0x7FFF + (lax.shift_right_logical(bits, 16) & 1)
    half = h.shape[1] // 2
    return lax.shift_right_logical(r[:, :half], 16) | (r[:, half:] & HI_MASK)


def _route_kernel(x_ref, g_ref, wq_ref, sk_ref, hp_ref, idx_ref, w_ref, hb_ref, it_ref, wt_ref):
    p = pl.program_id(1)

    @pl.when(p == 0)
    def _():
        h = _rms(x_ref[...], g_ref[...])
        hp_ref[...] = _pack_bf16_halves(h)
        hb_ref[...] = h.astype(BF16)

    qh = jnp.dot(hb_ref[...], wq_ref[...], preferred_element_type=F32)
    tops = []
    for c in range(2):
        seg = qh[:, c * PEER_HALF:(c + 1) * PEER_HALF]
        sc = lax.dot_general(sk_ref[c], seg, (((1,), (1,)), ((), ())),
                             precision=lax.Precision.HIGHEST, preferred_element_type=F32)
        tops.append(_topk_rows(sc, PEER_TOPK))
    (s0, i0), (s1, i1) = tops
    k = PEER_TOPK
    sub = 8
    tm = s0.shape[1]
    r8 = lax.broadcasted_iota(I32, (sub, tm), 0)
    r16 = lax.broadcasted_iota(I32, (k, tm), 0)
    cand_b = [s0[0:1] + s1, s0[1:2] + s1[:sub]]
    cidx_b = [i0[0:1] * PEER_NKEYS + i1, i0[1:2] * PEER_NKEYS + i1[:sub]]
    pos_b = [r16, k + r8]
    for a in range(2, sub):
        keep = r8 < (k // (a + 1))
        cand_b.append(jnp.where(keep, s0[a:a + 1] + s1[:sub], NEG_INF))
        cidx_b.append(i0[a:a + 1] * PEER_NKEYS + i1[:sub])
        pos_b.append(a * k + r8)
    cand_b.append(s0[sub:] + s1[0:1])
    cidx_b.append(i0[sub:] * PEER_NKEYS + i1[0:1])
    pos_b.append((sub + r8) * k)
    cand = jnp.concatenate(cand_b, axis=0)
    cidx = jnp.concatenate(cidx_b, axis=0)
    pos = jnp.concatenate(pos_b, axis=0)
    vals, ids = [], []
    for _ in range(k):
        m = jnp.max(cand, axis=0, keepdims=True)
        px = jnp.min(jnp.where(cand == m, pos, k * k), axis=0, keepdims=True)
        hit = pos == px
        vals.append(m)
        ids.append(jnp.sum(jnp.where(hit, cidx, 0), axis=0, keepdims=True))
        cand = jnp.where(hit, NEG_INF, cand)
    sf = jnp.concatenate(vals, axis=0)
    e = jnp.exp(sf - sf[0:1])
    rows = pl.ds(pl.multiple_of(p * PEER_TOPK, PEER_TOPK), PEER_TOPK)
    wt_ref[rows, :] = e / jnp.sum(e, axis=0, keepdims=True)
    it_ref[rows, :] = jnp.concatenate(ids, axis=0)

    @pl.when(p == pl.num_programs(1) - 1)
    def _():
        idx_ref[...] = it_ref[...].T
        w_ref[...] = wt_ref[...].T


def peer_route(x2d, g, wq, sk, tok0, t, tm=1024):
    d = x2d.shape[1]
    ph = sk.shape[0]
    nsel = ph * PEER_TOPK
    blk0 = tok0 // tm
    return pl.pallas_call(
        _route_kernel,
        grid=(t // tm, ph),
        in_specs=[
            pl.BlockSpec((tm, d), lambda i, p: (blk0 + i, 0)),
            pl.BlockSpec((1, d), lambda i, p: (0, 0)),
            pl.BlockSpec((d, 2 * PEER_HALF), lambda i, p: (0, p)),
            pl.BlockSpec((None, 2, PEER_NKEYS, PEER_HALF), lambda i, p: (p, 0, 0, 0)),
        ],
        out_specs=[
            pl.BlockSpec((tm, d // 2), lambda i, p: (i, 0)),
            pl.BlockSpec((tm, nsel), lambda i, p: (i, 0)),
            pl.BlockSpec((tm, nsel), lambda i, p: (i, 0)),
        ],
        out_shape=[jax.ShapeDtypeStruct((t, d // 2), I32),
                   jax.ShapeDtypeStruct((t, nsel), I32),
                   jax.ShapeDtypeStruct((t, nsel), F32)],
        scratch_shapes=[pltpu.VMEM((tm, d), BF16),
                        pltpu.VMEM((nsel, tm), I32),
                        pltpu.VMEM((nsel, tm), F32)],
        compiler_params=_cparams(("parallel", "arbitrary")),
        name="peer_route",
    )(x2d, g, wq, sk)


def _coef_kernel(w_ref, a_ref, o_ref):
    o_ref[...] = w_ref[...] * jax.nn.gelu(a_ref[...])


def peer_coef(w, act, tm=1024):
    t, n = w.shape
    spec = pl.BlockSpec((tm, n), lambda i: (i, 0))
    return pl.pallas_call(
        _coef_kernel, grid=(t // tm,), in_specs=[spec, spec], out_specs=spec,
        out_shape=jax.ShapeDtypeStruct((t, n), F32),
        compiler_params=_cparams(("parallel",)), name="peer_coef",
    )(w, act)


def _final_kernel(x_ref, y_ref, g_ref, o_ref):
    o_ref[...] = _rms(x_ref[...] + y_ref[...], g_ref[...])


def final_norm(x2d, y, g, tok0, tm=512):
    t, d = y.shape
    blk0 = tok0 // tm
    spec = pl.BlockSpec((tm, d), lambda i: (i, 0))
    return pl.pallas_call(
        _final_kernel, grid=(t // tm,),
        in_specs=[pl.BlockSpec((tm, d), lambda i: (blk0 + i, 0)), spec, pl.BlockSpec((1, d), lambda i: (0, 0))],
        out_specs=spec,
        out_shape=jax.ShapeDtypeStruct((t, d), F32),
        compiler_params=_cparams(("parallel",)), name="final_norm",
    )(x2d, y, g)


SC_CORES = 2
SC_SUBCORES = 16
SC_WORKERS = SC_CORES * SC_SUBCORES
SC_LANES = 16
SC_GROUP = 16


def _sc_mesh():
    return plsc.VectorSubcoreMesh(core_axis_name="c", subcore_axis_name="s")


def _sc_params():
    return pltpu.CompilerParams(needs_layout_passes=False)


def _sc_worker_id():
    return lax.axis_index("s") * SC_CORES + lax.axis_index("c")


SC_RING = 4
SC_ROW_SUB = 8
SC_ROW_LANE = 128


def _sc_ring(n_units, start, wait, compute):
    for u in range(SC_RING - 1):
        start(u, u)

    @pl.loop(0, n_units, step=SC_RING)
    def _(uu):
        for b in range(SC_RING):
            u = uu + b
            nxt = u + (SC_RING - 1)

            @pl.when(nxt < n_units)
            def _():
                start(nxt, (b + SC_RING - 1) % SC_RING)

            wait(u, b)
            compute(u, b)


def _sc_unit_off(u):
    off = u * SC_LANES
    return off if isinstance(off, int) else pl.multiple_of(off, SC_LANES)


def _sc_row_piece(rows, r, c):
    per = SC_ROW_LANE // SC_LANES
    return rows[r, c // per, pl.ds(pl.multiple_of((c % per) * SC_LANES, SC_LANES), SC_LANES)]


def peer_dots_sc(table, idx_flat, h):
    t, d = h.shape
    nsel = PEER_SEL
    tpw = t // SC_WORKERS
    g = SC_GROUP
    groups = tpw // g
    heads = nsel // SC_LANES
    pieces = d // SC_LANES
    units = g * heads
    row_buf = pltpu.VMEM((SC_LANES, SC_ROW_SUB, SC_ROW_LANE), F32)

    @functools.partial(
        pl.kernel, mesh=_sc_mesh(),
        out_type=jax.ShapeDtypeStruct((t * nsel,), F32),
        scratch_types=[
            pltpu.VMEM((g * nsel,), I32),
            pltpu.VMEM((g, d), F32),
            pltpu.VMEM((g * nsel,), F32),
            pltpu.VMEM((SC_LANES * SC_LANES,), F32),
            [row_buf] * SC_RING,
            [pltpu.SemaphoreType.DMA] * SC_RING,
        ],
        compiler_params=_sc_params(),
        name="peer_dots_sc",
    )
    def k(tab_hbm, idx_hbm, h_hbm, out_hbm, idx_v, h_v, out_v, red_v, rows, sems):
        wid = _sc_worker_id()
        lane = lax.iota(I32, SC_LANES)

        def copy(u, slot):
            ids = idx_v.at[pl.ds(_sc_unit_off(u), SC_LANES)]
            return pltpu.make_async_copy(tab_hbm.at[ids], rows[slot], sems[slot])

        def compute(u, slot):
            tt = u // heads

            def body(c, accs):
                hv = h_v[tt, pl.ds(pl.multiple_of(c * SC_LANES, SC_LANES), SC_LANES)]
                return tuple(accs[r] + _sc_row_piece(rows[slot], r, c) * hv for r in range(SC_LANES))

            accs = lax.fori_loop(0, pieces, body,
                                 tuple(jnp.zeros((SC_LANES,), F32) for _ in range(SC_LANES)))
            for r in range(SC_LANES):
                red_v[pl.ds(r * SC_LANES, SC_LANES)] = accs[r]
            cols = [plsc.load_gather(red_v, [lane * SC_LANES + j]) for j in range(SC_LANES)]
            while len(cols) > 1:
                cols = [cols[i] + cols[i + 1] for i in range(0, len(cols), 2)]
            out_v[pl.ds(_sc_unit_off(u), SC_LANES)] = cols[0]

        @pl.loop(0, groups)
        def _(gi):
            base = wid * tpw + gi * g
            pltpu.sync_copy(idx_hbm.at[pl.ds(base * nsel, g * nsel)], idx_v)
            pltpu.sync_copy(h_hbm.at[pl.ds(base, g)], h_v)
            _sc_ring(units, lambda u, s: copy(u, s).start(), lambda u, s: copy(u, s).wait(), compute)
            pltpu.sync_copy(out_v, out_hbm.at[pl.ds(base * nsel, g * nsel)])

    return k(table, idx_flat, h)


def peer_combine_sc(table, idx_flat, coef_flat, t):
    d = table.shape[1] * table.shape[2]
    nsel = PEER_SEL
    tpw = t // SC_WORKERS
    g = SC_GROUP
    groups = tpw // g
    heads = nsel // SC_LANES
    pieces = d // SC_LANES
    units = g * heads
    row_buf = pltpu.VMEM((SC_LANES, SC_ROW_SUB, SC_ROW_LANE), F32)

    @functools.partial(
        pl.kernel, mesh=_sc_mesh(),
        out_type=jax.ShapeDtypeStruct((t, d), F32),
        scratch_types=[
            pltpu.VMEM((g * nsel,), I32),
            pltpu.VMEM((g * nsel,), F32),
            pltpu.VMEM((g, d), F32),
            [row_buf] * SC_RING,
            [pltpu.SemaphoreType.DMA] * SC_RING,
        ],
        compiler_params=_sc_params(),
        name="peer_combine_sc",
    )
    def k(tab_hbm, idx_hbm, coef_hbm, out_hbm, idx_v, coef_v, y_v, rows, sems):
        wid = _sc_worker_id()

        def copy(u, slot):
            ids = idx_v.at[pl.ds(_sc_unit_off(u), SC_LANES)]
            return pltpu.make_async_copy(tab_hbm.at[ids], rows[slot], sems[slot])

        def compute(u, slot):
            tt = u // heads
            first = (u % heads) == 0
            cs = [plsc.load_gather(coef_v, [jnp.full((SC_LANES,), u * SC_LANES + r, I32)])
                  for r in range(SC_LANES)]

            @plsc.parallel_loop(0, pieces, unroll=2)
            def _(c):
                off = pl.multiple_of(c * SC_LANES, SC_LANES)
                terms = [cs[r] * _sc_row_piece(rows[slot], r, c) for r in range(SC_LANES)]
                while len(terms) > 1:
                    terms = [terms[i] + terms[i + 1] for i in range(0, len(terms), 2)]
                prev = y_v[tt, pl.ds(off, SC_LANES)]
                y_v[tt, pl.ds(off, SC_LANES)] = terms[0] + jnp.where(first, 0.0, prev)

        @pl.loop(0, groups)
        def _(gi):
            base = wid * tpw + gi * g
            pltpu.sync_copy(idx_hbm.at[pl.ds(base * nsel, g * nsel)], idx_v)
            pltpu.sync_copy(coef_hbm.at[pl.ds(base * nsel, g * nsel)], coef_v)
            _sc_ring(units, lambda u, s: copy(u, s).start(), lambda u, s: copy(u, s).wait(), compute)
            pltpu.sync_copy(y_v, out_hbm.at[pl.ds(base, g)])

    return k(table, idx_flat, coef_flat)


GELU_C0 = math.sqrt(2.0 / math.pi)
GELU_C1 = 0.044715


def _gelu_tanh(x):
    z = GELU_C0 * (x + GELU_C1 * (x * x * x))
    th = 1.0 - 2.0 / (jnp.exp(2.0 * z) + 1.0)
    return 0.5 * x * (1.0 + th)


def peer_experts_sc(tab_u, tab_v, idx_flat, w_flat, h):
    t, d = h.shape
    nsel = PEER_SEL
    tpw = t // SC_WORKERS
    g = SC_GROUP
    groups = tpw // g
    heads = nsel // SC_LANES
    pieces = d // SC_LANES
    units = g * heads
    row_buf = pltpu.VMEM((SC_LANES, SC_ROW_SUB, SC_ROW_LANE), F32)

    @functools.partial(
        pl.kernel, mesh=_sc_mesh(),
        out_type=jax.ShapeDtypeStruct((t, d), F32),
        scratch_types=[
            pltpu.VMEM((g * nsel,), I32),
            pltpu.VMEM((g * nsel,), F32),
            pltpu.VMEM((g, d), F32),
            pltpu.VMEM((g, d), F32),
            pltpu.VMEM((SC_LANES * SC_LANES,), F32),
            [row_buf] * SC_RING,
            [pltpu.SemaphoreType.DMA] * SC_RING,
        ],
        compiler_params=_sc_params(),
        name="peer_experts_sc",
    )
    def k(u_hbm, v_hbm, idx_hbm, w_hbm, h_hbm, out_hbm, idx_v, coef_v, h_v, y_v, red_v, rows, sems):
        wid = _sc_worker_id()
        lane = lax.iota(I32, SC_LANES)

        def copy(tab_hbm, u, slot):
            ids = idx_v.at[pl.ds(_sc_unit_off(u), SC_LANES)]
            return pltpu.make_async_copy(tab_hbm.at[ids], rows[slot], sems[slot])

        def dots(u, slot):
            tt = u // heads

            def body(c, accs):
                hv = h_v[tt, pl.ds(pl.multiple_of(c * SC_LANES, SC_LANES), SC_LANES)]
                return tuple(accs[r] + _sc_row_piece(rows[slot], r, c) * hv for r in range(SC_LANES))

            accs = lax.fori_loop(0, pieces, body,
                                 tuple(jnp.zeros((SC_LANES,), F32) for _ in range(SC_LANES)))
            for r in range(SC_LANES):
                red_v[pl.ds(r * SC_LANES, SC_LANES)] = accs[r]
            cols = [plsc.load_gather(red_v, [lane * SC_LANES + j]) for j in range(SC_LANES)]
            while len(cols) > 1:
                cols = [cols[i] + cols[i + 1] for i in range(0, len(cols), 2)]
            sl = pl.ds(_sc_unit_off(u), SC_LANES)
            coef_v[sl] = coef_v[sl] * _gelu_tanh(cols[0])

        def combine(u, slot):
            tt = u // heads
            first = (u % heads) == 0
            cs = [plsc.load_gather(coef_v, [jnp.full((SC_LANES,), u * SC_LANES + r, I32)])
                  for r in range(SC_LANES)]

            @plsc.parallel_loop(0, pieces, unroll=2)
            def _(c):
                off = pl.multiple_of(c * SC_LANES, SC_LANES)
                terms = [cs[r] * _sc_row_piece(rows[slot], r, c) for r in range(SC_LANES)]
                while len(terms) > 1:
                    terms = [terms[i] + terms[i + 1] for i in range(0, len(terms), 2)]
                prev = y_v[tt, pl.ds(off, SC_LANES)]
                y_v[tt, pl.ds(off, SC_LANES)] = terms[0] + jnp.where(first, 0.0, prev)

        @pl.loop(0, groups)
        def _(gi):
            base = wid * tpw + gi * g
            pltpu.sync_copy(idx_hbm.at[pl.ds(base * nsel, g * nsel)], idx_v)
            pltpu.sync_copy(w_hbm.at[pl.ds(base * nsel, g * nsel)], coef_v)
            pltpu.sync_copy(h_hbm.at[pl.ds(base, g)], h_v)
            _sc_ring(units, lambda u, s: copy(u_hbm, u, s).start(), lambda u, s: copy(u_hbm, u, s).wait(), dots)
            _sc_ring(units, lambda u, s: copy(v_hbm, u, s).start(), lambda u, s: copy(v_hbm, u, s).wait(), combine)
            pltpu.sync_copy(y_v, out_hbm.at[pl.ds(base, g)])

    return k(tab_u, tab_v, idx_flat, w_flat, h)


SC_PK_RING = 4
SC_PK_SUB = 4
HI_MASK = -65536


def pack_bf16_pairs(a):
    half = a.shape[1] // 2
    bits = lax.bitcast_convert_type(a.astype(BF16), jnp.uint16).astype(jnp.uint32)
    return lax.bitcast_convert_type(bits[:, :half] | (bits[:, half:] << 16), I32)


def _pack_tables_kernel(u_ref, v_ref, o_ref):
    for part, ref in enumerate((u_ref, v_ref)):
        words = _pack_bf16_halves(ref[...])
        for sub in range(SC_PK_SUB):
            o_ref[:, part * SC_PK_SUB + sub, :] = words[:, sub * SC_ROW_LANE:(sub + 1) * SC_ROW_LANE]


def pack_expert_tables(u, v, te=512):
    e, d = u.shape
    assert d == 2 * SC_PK_SUB * SC_ROW_LANE
    spec = pl.BlockSpec((te, d), lambda i: (i, 0))
    return pl.pallas_call(
        _pack_tables_kernel, grid=(e // te,), in_specs=[spec, spec],
        out_specs=pl.BlockSpec((te, 2 * SC_PK_SUB, SC_ROW_LANE), lambda i: (i, 0, 0)),
        out_shape=jax.ShapeDtypeStruct((e, 2 * SC_PK_SUB, SC_ROW_LANE), I32),
        compiler_params=_cparams(("parallel",)), name="pack_expert_tables",
    )(u, v)


def _unpack_halves(x32):
    w = plsc.bitcast(x32, I32)
    return plsc.bitcast(w << 16, F32), plsc.bitcast(w & HI_MASK, F32)


def _tree_sum(xs):
    while len(xs) > 1:
        xs = [xs[i] + xs[i + 1] for i in range(0, len(xs), 2)]
    return xs[0]


def peer_experts_pk_sc(tab_uv, idx_flat, w_flat, hp, d):
    t = hp.shape[0]
    nsel = PEER_SEL
    tpw = t // SC_WORKERS
    g = SC_GROUP
    groups = tpw // g
    heads = nsel // SC_LANES
    chunks = d // 32
    units = g * heads
    ring = SC_PK_RING
    row_buf = pltpu.VMEM((SC_LANES, 2 * SC_PK_SUB, SC_ROW_LANE), I32)

    def row_words(rows, r, wc, sub0):
        per = SC_ROW_LANE // SC_LANES
        return plsc.bitcast(
            rows[r, sub0 + wc // per, pl.ds(pl.multiple_of((wc % per) * SC_LANES, SC_LANES), SC_LANES)], BF16)

    def ring_loop(n_units, start, wait, compute):
        for u in range(ring - 1):
            start(u, u)

        @pl.loop(0, n_units, step=ring)
        def _(uu):
            for b in range(ring):
                u = uu + b
                nxt = u + (ring - 1)

                @pl.when(nxt < n_units)
                def _():
                    start(nxt, (b + ring - 1) % ring)

                wait(u, b)
                compute(u, b)

    @functools.partial(
        pl.kernel, mesh=_sc_mesh(),
        out_type=jax.ShapeDtypeStruct((t, d), F32),
        scratch_types=[
            pltpu.VMEM((g * nsel,), I32),
            pltpu.VMEM((g * nsel,), F32),
            pltpu.VMEM((g, d // 2), I32),
            pltpu.VMEM((g, d), F32),
            pltpu.VMEM((SC_LANES * SC_LANES,), F32),
            [row_buf] * ring,
            [pltpu.SemaphoreType.DMA] * ring,
        ],
        compiler_params=_sc_params(),
        name="peer_experts_pk_sc",
    )
    def k(tab_hbm, idx_hbm, w_hbm, h_hbm, out_hbm, idx_v, coef_v, h_v, y_v, red_v, rows, sems):
        wid = _sc_worker_id()
        lane = lax.iota(I32, SC_LANES)

        def copy(u, slot):
            ids = idx_v.at[pl.ds(_sc_unit_off(u), SC_LANES)]
            return pltpu.make_async_copy(tab_hbm.at[ids], rows[slot], sems[slot])

        def dots(u, slot):
            tt = u // heads

            def body(cp, accs):
                out = []
                hv = [plsc.bitcast(h_v[tt, pl.ds(pl.multiple_of((2 * cp + i) * SC_LANES, SC_LANES), SC_LANES)], BF16)
                      for i in range(2)]
                for r in range(SC_LANES):
                    pr = (row_words(rows[slot], r, 2 * cp, 0) * hv[0]
                          + row_words(rows[slot], r, 2 * cp + 1, 0) * hv[1])
                    lo, hi = _unpack_halves(pr)
                    out.append(accs[r] + lo + hi)
                return tuple(out)

            accs = lax.fori_loop(0, chunks // 2, body,
                                 tuple(jnp.zeros((SC_LANES,), F32) for _ in range(SC_LANES)))
            for r in range(SC_LANES):
                red_v[pl.ds(r * SC_LANES, SC_LANES)] = accs[r]
            act = _tree_sum([plsc.load_gather(red_v, [lane * SC_LANES + j]) for j in range(SC_LANES)])
            sl = pl.ds(_sc_unit_off(u), SC_LANES)
            coef_v[sl] = coef_v[sl] * _gelu_tanh(act)

        def combine(u, slot):
            tt = u // heads
            first = (u % heads) == 0
            cb = []
            for r in range(SC_LANES):
                c = plsc.load_gather(coef_v, [jnp.full((SC_LANES,), u * SC_LANES + r, I32)])
                cb.append(plsc.pack(c, c, format=plsc.PackFormat.INTERLEAVED))

            @plsc.parallel_loop(0, chunks, unroll=2)
            def _(wc):
                lo, hi = _unpack_halves(
                    _tree_sum([cb[r] * row_words(rows[slot], r, wc, SC_PK_SUB) for r in range(SC_LANES)]))
                for half, val in ((0, lo), (1, hi)):
                    sl = pl.ds(pl.multiple_of(half * (d // 2) + wc * SC_LANES, SC_LANES), SC_LANES)
                    y_v[tt, sl] = val + jnp.where(first, 0.0, y_v[tt, sl])

        def unit(u, slot):
            dots(u, slot)
            combine(u, slot)

        @pl.loop(0, groups)
        def _(gi):
            base = wid * tpw + gi * g
            pltpu.sync_copy(idx_hbm.at[pl.ds(base * nsel, g * nsel)], idx_v)
            pltpu.sync_copy(w_hbm.at[pl.ds(base * nsel, g * nsel)], coef_v)
            pltpu.sync_copy(h_hbm.at[pl.ds(base, g)], h_v)
            ring_loop(units, lambda u, s: copy(u, s).start(), lambda u, s: copy(u, s).wait(), unit)
            pltpu.sync_copy(y_v, out_hbm.at[pl.ds(base, g)])

    return k(tab_uv, idx_flat, w_flat, hp)


def kernel(x, mem, rel_bias, ln_mix, w_in, hg_lower, hg_norm, w_up_a, w_up_b, w_out, ln_cross, ln_mem, wq_x, wk_x, wv_x, wo_x, ln_ffn, peer_query, peer_subkeys, peer_u, peer_v, ln_final):
    b, s, d = x.shape
    depth = w_in.shape[0]
    assert depth == 1, "the residual after PEER is fused into the final norm"
    assert s % MB_BLOCK == 0 and s % HG_CHUNK == 0 and s % (PEER_SLICES * SC_WORKERS * SC_GROUP) == 0
    nb = s // MB_BLOCK
    row = lambda a: a.reshape(1, -1).astype(F32)
    lb_all = jnp.cumsum(jax.nn.softmax(hg_lower.astype(F32), axis=0), axis=0)
    bias = moba_bias_tiles(rel_bias)
    n_hg = 4 * HG_WIDTH
    n_qk = 2 * MB_WIDTH
    n_mb = 3 * MB_WIDTH
    l = 0
    w = w_in[l].astype(BF16)
    w_hg, w_qk, w_vt, w_g = w[:, :n_hg], w[:, n_hg:n_hg + n_qk], w[:, n_hg + n_qk:n_hg + n_mb].T, w[:, n_hg + n_mb:]
    wa, wb, wo = w_up_a[l].astype(BF16), w_up_b[l].astype(BF16), w_out[l].astype(BF16)
    wqx, wox = wq_x[l].astype(BF16), wo_x[l].astype(BF16)
    wpq, sk = peer_query[l].astype(BF16), peer_subkeys[l].astype(F32)
    tab3 = lambda a: pack_bf16_pairs(a.astype(F32)).reshape(a.shape[0], SC_PK_SUB, SC_ROW_LANE)
    tab_uv = pack_expert_tables(peer_u[l].astype(F32), peer_v[l].astype(F32))
    kx, vx = mem_kv(mem, row(ln_mem[l]), wk_x[l].astype(BF16), wv_x[l].astype(BF16))

    outs = []
    for bi in range(b):
        x2d = x[bi]
        p0, pqk, vt, pg = in_proj(x2d, row(ln_mix[l]), w_hg, w_qk, w_vt, w_g)
        km = moba_kmean(pqk, 1, s).reshape(1, nb, MB_WIDTH)
        hg_state = jnp.zeros((HG_HEADS, HG_D, HG_D), F32)
        ts = s // PEER_SLICES
        for tok0 in range(0, s, ts):
            ya, hg_state = hgrn2(p0, row(lb_all[l]), row(hg_norm[l]), hg_state, tok0, ts)
            yb = moba_attention(pqk, vt, km, bias, 1, s, tok0 // MB_BLOCK, ts // MB_BLOCK)
            xs = mix_out(x2d, ya, yb, pg, wa, wb, wo, tok0)
            xs = cross_attn(xs, row(ln_cross[l]), wqx, kx[bi:bi + 1], vx[bi:bi + 1], wox, ts)
            hp, eidx, wts = peer_route(xs, row(ln_ffn[l]), wpq, sk, 0, ts)
            y = peer_experts_pk_sc(tab_uv, eidx.reshape(ts * PEER_SEL), wts.reshape(ts * PEER_SEL), hp, d)
            outs.append(final_norm(xs, y, row(ln_final), 0))
    return jnp.concatenate(outs, axis=0).reshape(b, s, d)
```

```python
import functools
import math

import jax
import jax.numpy as jnp
import numpy as np
from jax import lax
from jax.experimental import pallas as pl
from jax.experimental.pallas import tpu as pltpu
from jax.experimental.pallas import tpu_sc as plsc

F32 = jnp.float32
BF16 = jnp.bfloat16
I32 = jnp.int32
EPS = 1e-6
NEG_INF = float("-inf")

HG_HEADS = 4
HG_D = 128
HG_WIDTH = HG_HEADS * HG_D
HG_CHUNK = 64
HG_SUB = 16
MB_HEADS = 8
MB_DH = 64
MB_WIDTH = MB_HEADS * MB_DH
MB_BLOCK = 256
MB_TOPK = 3
MB_BIAS_TILES = 8
REL_BUCKETS = 32
REL_MAX_DIST = 2048
X_HEADS = 4
PEER_HEADS = 8
PEER_NKEYS = 128
PEER_TOPK = 16
PEER_HALF = 128
PEER_SEL = PEER_HEADS * PEER_TOPK
PEER_SLICES = 4

VMEM_LIMIT = 56 * 1024 * 1024


def _cparams(sem):
    return pltpu.CompilerParams(dimension_semantics=sem, vmem_limit_bytes=VMEM_LIMIT)


def _rms(x, g):
    ms = jnp.mean(x * x, axis=-1, keepdims=True)
    return x * lax.rsqrt(ms + EPS) * g


def _in_proj_kernel(x_ref, g_ref, w0_ref, w1_ref, wvt_ref, w2_ref, o0_ref, o1_ref, ovt_ref, o2_ref):
    h = _rms(x_ref[...], g_ref[...]).astype(BF16)
    o0_ref[...] = jnp.dot(h, w0_ref[...], preferred_element_type=F32)
    o1_ref[...] = jnp.dot(h, w1_ref[...], preferred_element_type=F32).astype(BF16)
    vt = lax.dot_general(wvt_ref[...], h, (((1,), (1,)), ((), ())), preferred_element_type=F32).astype(BF16)
    for hd in range(MB_HEADS):
        ovt_ref[0, hd * MB_VROWS:hd * MB_VROWS + MB_DH, :] = vt[hd * MB_DH:(hd + 1) * MB_DH]
        ovt_ref[0, hd * MB_VROWS + MB_DH:(hd + 1) * MB_VROWS, :] = jnp.ones((MB_ONES, vt.shape[1]), BF16)
    o2_ref[...] = jnp.dot(h, w2_ref[...], preferred_element_type=F32).astype(BF16)


def in_proj(x2d, g, w0, w1, wvt, w2):
    t, d = x2d.shape
    tm = MB_BLOCK
    assert wvt.shape[0] == MB_WIDTH
    n0, n1, nv, n2 = w0.shape[1], w1.shape[1], MB_VT_ROWS, w2.shape[1]
    full = lambda a: pl.BlockSpec(a.shape, lambda i: (0, 0))
    return pl.pallas_call(
        _in_proj_kernel,
        grid=(t // tm,),
        in_specs=[pl.BlockSpec((tm, d), lambda i: (i, 0)), full(g), full(w0), full(w1), full(wvt), full(w2)],
        out_specs=[pl.BlockSpec((tm, n0), lambda i: (i, 0)),
                   pl.BlockSpec((tm, n1), lambda i: (i, 0)),
                   pl.BlockSpec((1, nv, tm), lambda i: (i, 0, 0)),
                   pl.BlockSpec((tm, n2), lambda i: (i, 0))],
        out_shape=[jax.ShapeDtypeStruct((t, n0), F32),
                   jax.ShapeDtypeStruct((t, n1), BF16),
                   jax.ShapeDtypeStruct((t // tm, nv, tm), BF16),
                   jax.ShapeDtypeStruct((t, n2), BF16)],
        compiler_params=_cparams(("parallel",)),
        name="in_proj",
    )(x2d, g, w0, w1, wvt, w2)


def _hgrn_kernel(q_ref, f_ref, i_ref, g_ref, lb_ref, gain_ref, o_ref, st_ref):
    c = pl.program_id(1)

    @pl.when(c == 0)
    def _():
        st_ref[...] = jnp.zeros_like(st_ref)

    C, S = HG_CHUNK, HG_SUB
    row = lax.broadcasted_iota(I32, (C, C), 0)
    col = lax.broadcasted_iota(I32, (C, C), 1)
    tril = (row >= col).astype(F32)
    t_iota = lax.broadcasted_iota(I32, (S, 1), 0)

    for h in range(HG_HEADS):
        sl = slice(h * HG_D, (h + 1) * HG_D)
        q = q_ref[:, sl]
        v = i_ref[:, sl]
        lb = lb_ref[:, sl]
        f = lb + (1.0 - lb) * jax.nn.sigmoid(f_ref[:, sl])
        lf = jnp.log(f)
        k = 1.0 - f
        b = jnp.dot(tril, lf, precision=lax.Precision.HIGHEST, preferred_element_type=F32)
        st = st_ref[h]
        vb = v.astype(BF16)
        qd = (q * jnp.exp(b)).astype(BF16)
        o_inter = lax.dot_general(qd, st.astype(BF16), (((1,), (1,)), ((), ())),
                                  preferred_element_type=F32)
        outs = []
        for i in range(C // S):
            r0 = i * S
            qi = q[r0:r0 + S]
            ki = k[r0:r0 + S]
            bi = b[r0:r0 + S]
            vi = v[r0:r0 + S]
            oi = o_inter[r0:r0 + S]
            if i > 0:
                bs = b[r0 - 1:r0]
                qh = (qi * jnp.exp(bi - bs)).astype(BF16)
                kh = (k[:r0] * jnp.exp(bs - b[:r0])).astype(BF16)
                a = lax.dot_general(qh, kh, (((1,), (1,)), ((), ())), preferred_element_type=F32)
                oi = oi + jnp.dot(a.astype(BF16), vb[:r0], preferred_element_type=F32)
            half = S // 2
            o_half = [oi[:half], oi[half:]]
            for s in range(S):
                for hf in range(s // half, 2):
                    rows = slice(hf * half, (hf + 1) * half)
                    dec = jnp.exp(jnp.minimum(bi[rows] - bi[s:s + 1], 0.0))
                    a_s = jnp.sum(qi[rows] * ki[s:s + 1] * dec, axis=-1, keepdims=True)
                    a_s = jnp.where(t_iota[rows] >= s, a_s, 0.0)
                    o_half[hf] = o_half[hf] + a_s * vi[s:s + 1]
            outs.extend(o_half)
        o = jnp.concatenate(outs, axis=0)
        b_end = b[C - 1:C]
        kd = (k * jnp.exp(b_end - b)).astype(BF16)
        upd = lax.dot_general(vb, kd, (((0,), (0,)), ((), ())), preferred_element_type=F32)
        st_ref[h] = st * jnp.exp(b_end) + upd
        o = o * lax.rsqrt(jnp.mean(o * o, axis=-1, keepdims=True) + EPS)
        g = g_ref[:, sl]
        o_ref[:, sl] = (o * gain_ref[:, sl] * (g * jax.nn.sigmoid(g))).astype(o_ref.dtype)


def hgrn2(p0, lb, gain, batch, seq):
    t = p0.shape[0]
    nc = seq // HG_CHUNK
    w = HG_WIDTH

    def col(j):
        return pl.BlockSpec((HG_CHUNK, w), lambda b, c, j=j: (b * nc + c, j))

    return pl.pallas_call(
        _hgrn_kernel,
        grid=(batch, nc),
        in_specs=[col(0), col(1), col(2), col(3),
                  pl.BlockSpec((1, w), lambda b, c: (0, 0)),
                  pl.BlockSpec((1, w), lambda b, c: (0, 0))],
        out_specs=pl.BlockSpec((HG_CHUNK, w), lambda b, c: (b * nc + c, 0)),
        out_shape=jax.ShapeDtypeStruct((t, w), BF16),
        scratch_shapes=[pltpu.VMEM((HG_HEADS, HG_D, HG_D), F32)],
        compiler_params=_cparams(("parallel", "arbitrary")),
        name="hgrn2",
    )(p0, p0, p0, p0, lb, gain)


def _kmean_kernel(k_ref, o_ref):
    o_ref[0] = jnp.mean(k_ref[...].astype(F32), axis=0, keepdims=True)


def moba_kmean(p1, batch, seq):
    nbt = p1.shape[0] // MB_BLOCK
    return pl.pallas_call(
        _kmean_kernel,
        grid=(nbt,),
        in_specs=[pl.BlockSpec((MB_BLOCK, MB_WIDTH), lambda i: (i, 1))],
        out_specs=pl.BlockSpec((1, 1, MB_WIDTH), lambda i: (i, 0, 0)),
        out_shape=jax.ShapeDtypeStruct((nbt, 1, MB_WIDTH), F32),
        compiler_params=_cparams(("parallel",)),
        name="moba_kmean",
    )(p1)


MB_PAIR = 4
MB_PW = MB_PAIR * MB_DH
MB_LG = 128
MB_ONES = 16
MB_VROWS = MB_DH + MB_ONES
MB_VT_ROWS = MB_HEADS * MB_VROWS


def _moba_kernel(q_ref, k_ref, vt_ref, km_ref, bias_ref, after_ref, o_ref, *scratch, qb0):
    del after_ref
    m_ref, l_ref, al_ref, acc_ref, msk_ref, s_ref, p_ref = (
        scratch[i * MB_PAIR:(i + 1) * MB_PAIR] for i in range(7))
    qi = pl.program_id(2) + qb0
    nb = km_ref.shape[0]
    blk = MB_BLOCK
    heads = range(MB_PAIR)
    grp = lambda hh: slice((hh // 2) * MB_LG, (hh // 2 + 1) * MB_LG)
    q = q_ref[...]
    lane = lax.broadcasted_iota(I32, (blk, MB_LG), 1)
    in_head = [(lane < MB_DH) if hh % 2 == 0 else (lane >= MB_DH) for hh in heads]
    qs = q * jnp.asarray(MB_DH ** -0.5, BF16)
    nt = (((1,), (1,)), ((), ()))
    qf = q.astype(F32)
    qht = [jnp.where(in_head[hh], qs[:, grp(hh)].astype(F32), 0.0).T.astype(BF16) for hh in heads]

    n_io = lax.broadcasted_iota(I32, (nb, blk), 0)
    for hh in heads:
        gate = lax.dot_general(km_ref[:, grp(hh)], jnp.where(in_head[hh], qf[:, grp(hh)], 0.0), nt,
                               precision=lax.Precision.HIGHEST, preferred_element_type=F32)
        gate = jnp.where(n_io < qi, gate, NEG_INF)
        chosen = n_io < 0
        for _ in range(MB_TOPK):
            mx = jnp.max(gate, axis=0, keepdims=True)
            ix = jnp.min(jnp.where(gate == mx, n_io, nb), axis=0, keepdims=True)
            hit = n_io == ix
            chosen = chosen | (hit & (mx > NEG_INF))
            gate = jnp.where(hit, NEG_INF, gate)
        msk_ref[hh][...] = jnp.where(chosen, 0.0, NEG_INF)

    vrows = lambda hh: slice(hh * MB_VROWS, (hh + 1) * MB_VROWS)

    def pv_stage(blk_idx):
        vtb = vt_ref[blk_idx]
        r = [jnp.dot(vtb[vrows(hh)], p_ref[hh][...], preferred_element_type=F32) for hh in heads]
        al = [al_ref[hh][...] for hh in heads]
        a_new = [al[hh] * acc_ref[hh][...] + r[hh][:MB_DH] for hh in heads]
        l_new = [al[hh] * l_ref[hh][...] + r[hh][MB_DH:MB_DH + 1] for hh in heads]
        return a_new, l_new

    def store_pv(a_new, l_new):
        for hh in heads:
            acc_ref[hh][...] = a_new[hh]
            l_ref[hh][...] = l_new[hh]

    def softmax_stage():
        s = [s_ref[hh][...] for hh in heads]
        m_old = [m_ref[hh][...] for hh in heads]
        m_new = [jnp.maximum(m_old[hh], jnp.max(s[hh], axis=0, keepdims=True)) for hh in heads]
        alpha = [jnp.exp(m_old[hh] - m_new[hh]) for hh in heads]
        p = [jnp.exp((s[hh] - m_new[hh]).astype(BF16)) for hh in heads]
        return p, alpha, m_new

    def store_softmax(p, alpha, m_new):
        for hh in heads:
            p_ref[hh][...] = p[hh]
            al_ref[hh][...] = alpha[hh]
            m_ref[hh][...] = m_new[hh]

    k_own = k_ref[pl.ds(pl.multiple_of(qi * blk, blk), blk), :]
    key_io = lax.broadcasted_iota(I32, (blk, blk), 0)
    qry_io = lax.broadcasted_iota(I32, (blk, blk), 1)
    for hh in heads:
        s = jnp.dot(k_own[:, grp(hh)], qht[hh], preferred_element_type=F32) + bias_ref[hh, 0]
        s_ref[hh][...] = jnp.where(key_io <= qry_io, s, NEG_INF)
        m_ref[hh][...] = jnp.full((1, blk), NEG_INF, F32)
        l_ref[hh][...] = jnp.zeros((1, blk), F32)
        al_ref[hh][...] = jnp.ones((1, blk), F32)
        acc_ref[hh][...] = jnp.zeros((MB_DH, blk), F32)
        p_ref[hh][...] = jnp.zeros((blk, blk), BF16)

    def step(i, carry, far):
        pv = pv_stage(jnp.where(i <= 1, qi, i - 2))
        sm = softmax_stage()
        kn = k_ref[pl.ds(pl.multiple_of(i * blk, blk), blk), :]
        if far:
            row = [msk_ref[hh][pl.ds(i, 1), :] + bias_ref[hh, MB_BIAS_TILES - 1, 0:1, 0:1] for hh in heads]
            s_next = [jnp.dot(kn[:, grp(hh)], qht[hh], preferred_element_type=F32) + row[hh] for hh in heads]
        else:
            d = qi - i
            s_next = [jnp.dot(kn[:, grp(hh)], qht[hh], preferred_element_type=F32)
                      + bias_ref[hh, d] + msk_ref[hh][pl.ds(i, 1), :] for hh in heads]
        store_pv(*pv)
        for hh in heads:
            s_ref[hh][...] = s_next[hh]
        store_softmax(*sm)
        return carry

    n_far = jnp.maximum(qi - (MB_BIAS_TILES - 2), 0)
    lax.fori_loop(0, n_far, functools.partial(step, far=True), 0)
    lax.fori_loop(n_far, qi, functools.partial(step, far=False), 0)
    pv = pv_stage(jnp.where(qi <= 1, qi, qi - 2))
    sm = softmax_stage()
    store_pv(*pv)
    store_softmax(*sm)
    a_fin, l_fin = pv_stage(jnp.where(qi == 0, qi, qi - 1))
    out_t = jnp.concatenate([a_fin[hh] / l_fin[hh] for hh in heads], axis=0)
    o_ref[...] = out_t.T.astype(o_ref.dtype)


def moba_attention(pqk, vt, km, bias, batch, seq, qb0=0, nqb=None, after=None):
    nb = seq // MB_BLOCK
    nqb = nb if nqb is None else nqb
    t = batch * nqb * MB_BLOCK
    groups = MB_WIDTH // MB_PW
    after = jnp.zeros((8, 128), I32) if after is None else after
    return pl.pallas_call(
        functools.partial(_moba_kernel, qb0=qb0),
        grid=(batch, groups, nqb),
        in_specs=[
            pl.BlockSpec((MB_BLOCK, MB_PW), lambda b, j, i: (b * nb + qb0 + i, j)),
            pl.BlockSpec((seq, MB_PW), lambda b, j, i: (b, groups + j)),
            pl.BlockSpec((nb, MB_PAIR * MB_VROWS, MB_BLOCK), lambda b, j, i: (b, j, 0)),
            pl.BlockSpec((None, nb, MB_PW), lambda b, j, i: (b, 0, j)),
            pl.BlockSpec((MB_PAIR, MB_BIAS_TILES, MB_BLOCK, MB_BLOCK), lambda b, j, i: (j, 0, 0, 0)),
            pl.BlockSpec((8, 128), lambda b, j, i: (0, 0)),
        ],
        out_specs=pl.BlockSpec((MB_BLOCK, MB_PW), lambda b, j, i: (b * nqb + i, j)),
        out_shape=jax.ShapeDtypeStruct((t, MB_WIDTH), BF16),
        scratch_shapes=(
            [pltpu.VMEM((1, MB_BLOCK), F32)] * (3 * MB_PAIR)
            + [pltpu.VMEM((MB_DH, MB_BLOCK), F32)] * MB_PAIR
            + [pltpu.VMEM((nb, MB_BLOCK), F32)] * MB_PAIR
            + [pltpu.VMEM((MB_BLOCK, MB_BLOCK), F32)] * MB_PAIR
            + [pltpu.VMEM((MB_BLOCK, MB_BLOCK), BF16)] * MB_PAIR
        ),
        compiler_params=_cparams(("parallel", "parallel", "arbitrary")),
        name="moba_attn",
    )(pqk, pqk, vt, km, bias, after)


def _t5_bucket(dist):
    max_exact = REL_BUCKETS // 2
    scaled = jnp.log(jnp.maximum(dist, 1).astype(F32) / max_exact) / math.log(REL_MAX_DIST / max_exact)
    large = jnp.minimum(max_exact + (scaled * (REL_BUCKETS - max_exact)).astype(I32), REL_BUCKETS - 1)
    return jnp.where(dist < max_exact, dist, large)


def moba_bias_tiles(rel_bias):
    blk = MB_BLOCK
    span = 2 * blk - 1
    x = jnp.arange(span) - (blk - 1)
    dist = jnp.maximum(jnp.arange(MB_BIAS_TILES)[:, None] * blk + x[None, :], 0)
    w = rel_bias.astype(F32).T[:, _t5_bucket(dist)]
    h = w.shape[0]
    wp = jnp.pad(w, ((0, 0), (0, 0), (0, 1)))
    a = jnp.broadcast_to(wp[:, :, None, :], (h, MB_BIAS_TILES, blk, span + 1))
    a = a.reshape(h, MB_BIAS_TILES, blk * (span + 1))[:, :, :blk * span]
    return a.reshape(h, MB_BIAS_TILES, blk, span)[:, :, :, blk - 1:]


def _mix_kernel(x_ref, ya_ref, yb_ref, ga_ref, gb_ref, wa_ref, wb_ref, wo_ref, o_ref):
    za = jnp.dot(ya_ref[...], wa_ref[...], preferred_element_type=F32)
    zb = jnp.dot(yb_ref[...], wb_ref[...], preferred_element_type=F32)
    z = jax.nn.sigmoid(ga_ref[...].astype(F32)) * za + jax.nn.sigmoid(gb_ref[...].astype(F32)) * zb
    o_ref[...] = x_ref[...] + jnp.dot(z.astype(BF16), wo_ref[...], preferred_element_type=F32)


def mix_out(x2d, ya, yb, pg, wa, wb, wo, tok0=0, tm=512):
    t = yb.shape[0]
    d = x2d.shape[1]
    w = ya.shape[1]
    b0 = tok0 // tm
    return pl.pallas_call(
        _mix_kernel,
        grid=(t // tm,),
        in_specs=[
            pl.BlockSpec((tm, d), lambda i: (b0 + i, 0)),
            pl.BlockSpec((tm, w), lambda i: (b0 + i, 0)),
            pl.BlockSpec((tm, w), lambda i: (i, 0)),
            pl.BlockSpec((tm, d), lambda i: (b0 + i, 0)),
            pl.BlockSpec((tm, d), lambda i: (b0 + i, 1)),
            pl.BlockSpec((w, d), lambda i: (0, 0)),
            pl.BlockSpec((w, d), lambda i: (0, 0)),
            pl.BlockSpec((d, d), lambda i: (0, 0)),
        ],
        out_specs=pl.BlockSpec((tm, d), lambda i: (i, 0)),
        out_shape=jax.ShapeDtypeStruct((t, d), F32),
        compiler_params=_cparams(("parallel",)),
        name="mix_out",
    )(x2d, ya, yb, pg, pg, wa, wb, wo)


def _mem_kv_kernel(m_ref, g_ref, wk_ref, wv_ref, k_ref, v_ref):
    mn = _rms(m_ref[...], g_ref[...]).astype(BF16)
    k_ref[...] = jnp.dot(mn, wk_ref[...], preferred_element_type=F32).astype(BF16)
    v_ref[...] = jnp.dot(mn, wv_ref[...], preferred_element_type=F32).astype(BF16)


def mem_kv(mem, g, wk, wv):
    b, m, d = mem.shape
    spec = pl.BlockSpec((None, m, d), lambda i: (i, 0, 0))
    wspec = pl.BlockSpec((d, d), lambda i: (0, 0))
    return pl.pallas_call(
        _mem_kv_kernel,
        grid=(b,),
        in_specs=[spec, pl.BlockSpec((1, d), lambda i: (0, 0)), wspec, wspec],
        out_specs=[spec, spec],
        out_shape=[jax.ShapeDtypeStruct((b, m, d), BF16)] * 2,
        compiler_params=_cparams(("parallel",)),
        name="mem_kv",
    )(mem, g, wk, wv)


def _cross_kernel(x_ref, g_ref, wq_ref, k_ref, v_ref, wo_ref, o_ref):
    x = x_ref[...]
    d = x.shape[1]
    dh = d // X_HEADS
    h = _rms(x, g_ref[...]).astype(BF16)
    q = (jnp.dot(h, wq_ref[...], preferred_element_type=F32) * (dh ** -0.5)).astype(BF16)
    outs = []
    for hh in range(X_HEADS):
        sl = slice(hh * dh, (hh + 1) * dh)
        s = lax.dot_general(q[:, sl], k_ref[:, sl], (((1,), (1,)), ((), ())),
                            preferred_element_type=F32)
        p = jnp.exp(s - jnp.max(s, axis=1, keepdims=True))
        l = jnp.sum(p, axis=1, keepdims=True)
        o = jnp.dot(p.astype(BF16), v_ref[:, sl], preferred_element_type=F32) / l
        outs.append(o.astype(BF16))
    o = jnp.concatenate(outs, axis=1)
    o_ref[...] = x + jnp.dot(o, wo_ref[...], preferred_element_type=F32)


def cross_attn(x2d, g, wq, kx, vx, wo, seq, tm=512):
    t, d = x2d.shape
    m = kx.shape[1]
    per_b = seq // tm
    kv = pl.BlockSpec((None, m, d), lambda i: (i // per_b, 0, 0))
    wspec = pl.BlockSpec((d, d), lambda i: (0, 0))
    return pl.pallas_call(
        _cross_kernel,
        grid=(t // tm,),
        in_specs=[pl.BlockSpec((tm, d), lambda i: (i, 0)), pl.BlockSpec((1, d), lambda i: (0, 0)),
                  wspec, kv, kv, wspec],
        out_specs=pl.BlockSpec((tm, d), lambda i: (i, 0)),
        out_shape=jax.ShapeDtypeStruct((t, d), F32),
        compiler_params=_cparams(("parallel",)),
        name="cross_attn",
    )(x2d, g, wq, kx, vx, wo)


def _topk_rows(sc, k):
    n = sc.shape[0]
    io = lax.broadcasted_iota(I32, sc.shape, 0)
    vals, ids = [], []
    for _ in range(k):
        m = jnp.max(sc, axis=0, keepdims=True)
        ix = jnp.min(jnp.where(sc == m, io, n), axis=0, keepdims=True)
        vals.append(m)
        ids.append(ix)
        sc = jnp.where(io == ix, NEG_INF, sc)
    return jnp.concatenate(vals, axis=0), jnp.concatenate(ids, axis=0)


def _pack_bf16_halves(h):
    bits = lax.bitcast_convert_type(h, I32)
    r = bits + 0x7FFF + (lax.shift_right_logical(bits, 16) & 1)
    half = h.shape[1] // 2
    return lax.shift_right_logical(r[:, :half], 16) | (r[:, half:] & HI_MASK)


def _route_kernel(x_ref, g_ref, wq_ref, sk_ref, hp_ref, idx_ref, w_ref, hb_ref, it_ref, wt_ref):
    p = pl.program_id(1)

    @pl.when(p == 0)
    def _():
        h = _rms(x_ref[...], g_ref[...])
        hp_ref[...] = _pack_bf16_halves(h)
        hb_ref[...] = h.astype(BF16)

    qh = jnp.dot(hb_ref[...], wq_ref[...], preferred_element_type=F32)
    tops = []
    for c in range(2):
        seg = qh[:, c * PEER_HALF:(c + 1) * PEER_HALF]
        sc = lax.dot_general(sk_ref[c], seg, (((1,), (1,)), ((), ())),
                             precision=lax.Precision.HIGHEST, preferred_element_type=F32)
        tops.append(_topk_rows(sc, PEER_TOPK))
    (s0, i0), (s1, i1) = tops
    k = PEER_TOPK
    sub = 8
    tm = s0.shape[1]
    r8 = lax.broadcasted_iota(I32, (sub, tm), 0)
    r16 = lax.broadcasted_iota(I32, (k, tm), 0)
    cand_b = [s0[0:1] + s1, s0[1:2] + s1[:sub]]
    cidx_b = [i0[0:1] * PEER_NKEYS + i1, i0[1:2] * PEER_NKEYS + i1[:sub]]
    pos_b = [r16, k + r8]
    for a in range(2, sub):
        keep = r8 < (k // (a + 1))
        cand_b.append(jnp.where(keep, s0[a:a + 1] + s1[:sub], NEG_INF))
        cidx_b.append(i0[a:a + 1] * PEER_NKEYS + i1[:sub])
        pos_b.append(a * k + r8)
    cand_b.append(s0[sub:] + s1[0:1])
    cidx_b.append(i0[sub:] * PEER_NKEYS + i1[0:1])
    pos_b.append((sub + r8) * k)
    cand = jnp.concatenate(cand_b, axis=0)
    cidx = jnp.concatenate(cidx_b, axis=0)
    pos = jnp.concatenate(pos_b, axis=0)
    vals, ids = [], []
    for _ in range(k):
        m = jnp.max(cand, axis=0, keepdims=True)
        px = jnp.min(jnp.where(cand == m, pos, k * k), axis=0, keepdims=True)
        hit = pos == px
        vals.append(m)
        ids.append(jnp.sum(jnp.where(hit, cidx, 0), axis=0, keepdims=True))
        cand = jnp.where(hit, NEG_INF, cand)
    sf = jnp.concatenate(vals, axis=0)
    e = jnp.exp(sf - sf[0:1])
    rows = pl.ds(pl.multiple_of(p * PEER_TOPK, PEER_TOPK), PEER_TOPK)
    wt_ref[rows, :] = e / jnp.sum(e, axis=0, keepdims=True)
    it_ref[rows, :] = jnp.concatenate(ids, axis=0)

    @pl.when(p == pl.num_programs(1) - 1)
    def _():
        idx_ref[...] = it_ref[...].T
        w_ref[...] = wt_ref[...].T


def peer_route(x2d, g, wq, sk, tok0, t, tm=1024):
    d = x2d.shape[1]
    ph = sk.shape[0]
    nsel = ph * PEER_TOPK
    blk0 = tok0 // tm
    return pl.pallas_call(
        _route_kernel,
        grid=(t // tm, ph),
        in_specs=[
            pl.BlockSpec((tm, d), lambda i, p: (blk0 + i, 0)),
            pl.BlockSpec((1, d), lambda i, p: (0, 0)),
            pl.BlockSpec((d, 2 * PEER_HALF), lambda i, p: (0, p)),
            pl.BlockSpec((None, 2, PEER_NKEYS, PEER_HALF), lambda i, p: (p, 0, 0, 0)),
        ],
        out_specs=[
            pl.BlockSpec((tm, d // 2), lambda i, p: (i, 0)),
            pl.BlockSpec((tm, nsel), lambda i, p: (i, 0)),
            pl.BlockSpec((tm, nsel), lambda i, p: (i, 0)),
        ],
        out_shape=[jax.ShapeDtypeStruct((t, d // 2), I32),
                   jax.ShapeDtypeStruct((t, nsel), I32),
                   jax.ShapeDtypeStruct((t, nsel), F32)],
        scratch_shapes=[pltpu.VMEM((tm, d), BF16),
                        pltpu.VMEM((nsel, tm), I32),
                        pltpu.VMEM((nsel, tm), F32)],
        compiler_params=_cparams(("parallel", "arbitrary")),
        name="peer_route",
    )(x2d, g, wq, sk)


def _coef_kernel(w_ref, a_ref, o_ref):
    o_ref[...] = w_ref[...] * jax.nn.gelu(a_ref[...])


def peer_coef(w, act, tm=1024):
    t, n = w.shape
    spec = pl.BlockSpec((tm, n), lambda i: (i, 0))
    return pl.pallas_call(
        _coef_kernel, grid=(t // tm,), in_specs=[spec, spec], out_specs=spec,
        out_shape=jax.ShapeDtypeStruct((t, n), F32),
        compiler_params=_cparams(("parallel",)), name="peer_coef",
    )(w, act)


def _final_kernel(x_ref, y_ref, g_ref, o_ref):
    o_ref[...] = _rms(x_ref[...] + y_ref[...], g_ref[...])


def final_norm(x2d, y, g, tok0, tm=512):
    t, d = y.shape
    blk0 = tok0 // tm
    spec = pl.BlockSpec((tm, d), lambda i: (i, 0))
    return pl.pallas_call(
        _final_kernel, grid=(t // tm,),
        in_specs=[pl.BlockSpec((tm, d), lambda i: (blk0 + i, 0)), spec, pl.BlockSpec((1, d), lambda i: (0, 0))],
        out_specs=spec,
        out_shape=jax.ShapeDtypeStruct((t, d), F32),
        compiler_params=_cparams(("parallel",)), name="final_norm",
    )(x2d, y, g)


SC_CORES = 2
SC_SUBCORES = 16
SC_WORKERS = SC_CORES * SC_SUBCORES
SC_LANES = 16
SC_GROUP = 16


def _sc_mesh():
    return plsc.VectorSubcoreMesh(core_axis_name="c", subcore_axis_name="s")


def _sc_params():
    return pltpu.CompilerParams(needs_layout_passes=False)


def _sc_worker_id():
    return lax.axis_index("s") * SC_CORES + lax.axis_index("c")


SC_RING = 4
SC_ROW_SUB = 8
SC_ROW_LANE = 128


def _sc_ring(n_units, start, wait, compute):
    for u in range(SC_RING - 1):
        start(u, u)

    @pl.loop(0, n_units, step=SC_RING)
    def _(uu):
        for b in range(SC_RING):
            u = uu + b
            nxt = u + (SC_RING - 1)

            @pl.when(nxt < n_units)
            def _():
                start(nxt, (b + SC_RING - 1) % SC_RING)

            wait(u, b)
            compute(u, b)


def _sc_unit_off(u):
    off = u * SC_LANES
    return off if isinstance(off, int) else pl.multiple_of(off, SC_LANES)


def _sc_row_piece(rows, r, c):
    per = SC_ROW_LANE // SC_LANES
    return rows[r, c // per, pl.ds(pl.multiple_of((c % per) * SC_LANES, SC_LANES), SC_LANES)]


def peer_dots_sc(table, idx_flat, h):
    t, d = h.shape
    nsel = PEER_SEL
    tpw = t // SC_WORKERS
    g = SC_GROUP
    groups = tpw // g
    heads = nsel // SC_LANES
    pieces = d // SC_LANES
    units = g * heads
    row_buf = pltpu.VMEM((SC_LANES, SC_ROW_SUB, SC_ROW_LANE), F32)

    @functools.partial(
        pl.kernel, mesh=_sc_mesh(),
        out_type=jax.ShapeDtypeStruct((t * nsel,), F32),
        scratch_types=[
            pltpu.VMEM((g * nsel,), I32),
            pltpu.VMEM((g, d), F32),
            pltpu.VMEM((g * nsel,), F32),
            pltpu.VMEM((SC_LANES * SC_LANES,), F32),
            [row_buf] * SC_RING,
            [pltpu.SemaphoreType.DMA] * SC_RING,
        ],
        compiler_params=_sc_params(),
        name="peer_dots_sc",
    )
    def k(tab_hbm, idx_hbm, h_hbm, out_hbm, idx_v, h_v, out_v, red_v, rows, sems):
        wid = _sc_worker_id()
        lane = lax.iota(I32, SC_LANES)

        def copy(u, slot):
            ids = idx_v.at[pl.ds(_sc_unit_off(u), SC_LANES)]
            return pltpu.make_async_copy(tab_hbm.at[ids], rows[slot], sems[slot])

        def compute(u, slot):
            tt = u // heads

            def body(c, accs):
                hv = h_v[tt, pl.ds(pl.multiple_of(c * SC_LANES, SC_LANES), SC_LANES)]
                return tuple(accs[r] + _sc_row_piece(rows[slot], r, c) * hv for r in range(SC_LANES))

            accs = lax.fori_loop(0, pieces, body,
                                 tuple(jnp.zeros((SC_LANES,), F32) for _ in range(SC_LANES)))
            for r in range(SC_LANES):
                red_v[pl.ds(r * SC_LANES, SC_LANES)] = accs[r]
            cols = [plsc.load_gather(red_v, [lane * SC_LANES + j]) for j in range(SC_LANES)]
            while len(cols) > 1:
                cols = [cols[i] + cols[i + 1] for i in range(0, len(cols), 2)]
            out_v[pl.ds(_sc_unit_off(u), SC_LANES)] = cols[0]

        @pl.loop(0, groups)
        def _(gi):
            base = wid * tpw + gi * g
            pltpu.sync_copy(idx_hbm.at[pl.ds(base * nsel, g * nsel)], idx_v)
            pltpu.sync_copy(h_hbm.at[pl.ds(base, g)], h_v)
            _sc_ring(units, lambda u, s: copy(u, s).start(), lambda u, s: copy(u, s).wait(), compute)
            pltpu.sync_copy(out_v, out_hbm.at[pl.ds(base * nsel, g * nsel)])

    return k(table, idx_flat, h)


def peer_combine_sc(table, idx_flat, coef_flat, t):
    d = table.shape[1] * table.shape[2]
    nsel = PEER_SEL
    tpw = t // SC_WORKERS
    g = SC_GROUP
    groups = tpw // g
    heads = nsel // SC_LANES
    pieces = d // SC_LANES
    units = g * heads
    row_buf = pltpu.VMEM((SC_LANES, SC_ROW_SUB, SC_ROW_LANE), F32)

    @functools.partial(
        pl.kernel, mesh=_sc_mesh(),
        out_type=jax.ShapeDtypeStruct((t, d), F32),
        scratch_types=[
            pltpu.VMEM((g * nsel,), I32),
            pltpu.VMEM((g * nsel,), F32),
            pltpu.VMEM((g, d), F32),
            [row_buf] * SC_RING,
            [pltpu.SemaphoreType.DMA] * SC_RING,
        ],
        compiler_params=_sc_params(),
        name="peer_combine_sc",
    )
    def k(tab_hbm, idx_hbm, coef_hbm, out_hbm, idx_v, coef_v, y_v, rows, sems):
        wid = _sc_worker_id()

        def copy(u, slot):
            ids = idx_v.at[pl.ds(_sc_unit_off(u), SC_LANES)]
            return pltpu.make_async_copy(tab_hbm.at[ids], rows[slot], sems[slot])

        def compute(u, slot):
            tt = u // heads
            first = (u % heads) == 0
            cs = [plsc.load_gather(coef_v, [jnp.full((SC_LANES,), u * SC_LANES + r, I32)])
                  for r in range(SC_LANES)]

            @plsc.parallel_loop(0, pieces, unroll=2)
            def _(c):
                off = pl.multiple_of(c * SC_LANES, SC_LANES)
                terms = [cs[r] * _sc_row_piece(rows[slot], r, c) for r in range(SC_LANES)]
                while len(terms) > 1:
                    terms = [terms[i] + terms[i + 1] for i in range(0, len(terms), 2)]
                prev = y_v[tt, pl.ds(off, SC_LANES)]
                y_v[tt, pl.ds(off, SC_LANES)] = terms[0] + jnp.where(first, 0.0, prev)

        @pl.loop(0, groups)
        def _(gi):
            base = wid * tpw + gi * g
            pltpu.sync_copy(idx_hbm.at[pl.ds(base * nsel, g * nsel)], idx_v)
            pltpu.sync_copy(coef_hbm.at[pl.ds(base * nsel, g * nsel)], coef_v)
            _sc_ring(units, lambda u, s: copy(u, s).start(), lambda u, s: copy(u, s).wait(), compute)
            pltpu.sync_copy(y_v, out_hbm.at[pl.ds(base, g)])

    return k(table, idx_flat, coef_flat)


GELU_C0 = math.sqrt(2.0 / math.pi)
GELU_C1 = 0.044715


def _gelu_tanh(x):
    z = GELU_C0 * (x + GELU_C1 * (x * x * x))
    th = 1.0 - 2.0 / (jnp.exp(2.0 * z) + 1.0)
    return 0.5 * x * (1.0 + th)


def peer_experts_sc(tab_u, tab_v, idx_flat, w_flat, h):
    t, d = h.shape
    nsel = PEER_SEL
    tpw = t // SC_WORKERS
    g = SC_GROUP
    groups = tpw // g
    heads = nsel // SC_LANES
    pieces = d // SC_LANES
    units = g * heads
    row_buf = pltpu.VMEM((SC_LANES, SC_ROW_SUB, SC_ROW_LANE), F32)

    @functools.partial(
        pl.kernel, mesh=_sc_mesh(),
        out_type=jax.ShapeDtypeStruct((t, d), F32),
        scratch_types=[
            pltpu.VMEM((g * nsel,), I32),
            pltpu.VMEM((g * nsel,), F32),
            pltpu.VMEM((g, d), F32),
            pltpu.VMEM((g, d), F32),
            pltpu.VMEM((SC_LANES * SC_LANES,), F32),
            [row_buf] * SC_RING,
            [pltpu.SemaphoreType.DMA] * SC_RING,
        ],
        compiler_params=_sc_params(),
        name="peer_experts_sc",
    )
    def k(u_hbm, v_hbm, idx_hbm, w_hbm, h_hbm, out_hbm, idx_v, coef_v, h_v, y_v, red_v, rows, sems):
        wid = _sc_worker_id()
        lane = lax.iota(I32, SC_LANES)

        def copy(tab_hbm, u, slot):
            ids = idx_v.at[pl.ds(_sc_unit_off(u), SC_LANES)]
            return pltpu.make_async_copy(tab_hbm.at[ids], rows[slot], sems[slot])

        def dots(u, slot):
            tt = u // heads

            def body(c, accs):
                hv = h_v[tt, pl.ds(pl.multiple_of(c * SC_LANES, SC_LANES), SC_LANES)]
                return tuple(accs[r] + _sc_row_piece(rows[slot], r, c) * hv for r in range(SC_LANES))

            accs = lax.fori_loop(0, pieces, body,
                                 tuple(jnp.zeros((SC_LANES,), F32) for _ in range(SC_LANES)))
            for r in range(SC_LANES):
                red_v[pl.ds(r * SC_LANES, SC_LANES)] = accs[r]
            cols = [plsc.load_gather(red_v, [lane * SC_LANES + j]) for j in range(SC_LANES)]
            while len(cols) > 1:
                cols = [cols[i] + cols[i + 1] for i in range(0, len(cols), 2)]
            sl = pl.ds(_sc_unit_off(u), SC_LANES)
            coef_v[sl] = coef_v[sl] * _gelu_tanh(cols[0])

        def combine(u, slot):
            tt = u // heads
            first = (u % heads) == 0
            cs = [plsc.load_gather(coef_v, [jnp.full((SC_LANES,), u * SC_LANES + r, I32)])
                  for r in range(SC_LANES)]

            @plsc.parallel_loop(0, pieces, unroll=2)
            def _(c):
                off = pl.multiple_of(c * SC_LANES, SC_LANES)
                terms = [cs[r] * _sc_row_piece(rows[slot], r, c) for r in range(SC_LANES)]
                while len(terms) > 1:
                    terms = [terms[i] + terms[i + 1] for i in range(0, len(terms), 2)]
                prev = y_v[tt, pl.ds(off, SC_LANES)]
                y_v[tt, pl.ds(off, SC_LANES)] = terms[0] + jnp.where(first, 0.0, prev)

        @pl.loop(0, groups)
        def _(gi):
            base = wid * tpw + gi * g
            pltpu.sync_copy(idx_hbm.at[pl.ds(base * nsel, g * nsel)], idx_v)
            pltpu.sync_copy(w_hbm.at[pl.ds(base * nsel, g * nsel)], coef_v)
            pltpu.sync_copy(h_hbm.at[pl.ds(base, g)], h_v)
            _sc_ring(units, lambda u, s: copy(u_hbm, u, s).start(), lambda u, s: copy(u_hbm, u, s).wait(), dots)
            _sc_ring(units, lambda u, s: copy(v_hbm, u, s).start(), lambda u, s: copy(v_hbm, u, s).wait(), combine)
            pltpu.sync_copy(y_v, out_hbm.at[pl.ds(base, g)])

    return k(tab_u, tab_v, idx_flat, w_flat, h)


SC_PK_RING = 4
SC_PK_SUB = 4
HI_MASK = -65536


def pack_bf16_pairs(a):
    half = a.shape[1] // 2
    bits = lax.bitcast_convert_type(a.astype(BF16), jnp.uint16).astype(jnp.uint32)
    return lax.bitcast_convert_type(bits[:, :half] | (bits[:, half:] << 16), I32)


def _pack_tables_kernel(u_ref, v_ref, o_ref):
    for part, ref in enumerate((u_ref, v_ref)):
        words = _pack_bf16_halves(ref[...])
        for sub in range(SC_PK_SUB):
            o_ref[:, part * SC_PK_SUB + sub, :] = words[:, sub * SC_ROW_LANE:(sub + 1) * SC_ROW_LANE]


def pack_expert_tables(u, v, te=512):
    e, d = u.shape
    assert d == 2 * SC_PK_SUB * SC_ROW_LANE
    spec = pl.BlockSpec((te, d), lambda i: (i, 0))
    return pl.pallas_call(
        _pack_tables_kernel, grid=(e // te,), in_specs=[spec, spec],
        out_specs=pl.BlockSpec((te, 2 * SC_PK_SUB, SC_ROW_LANE), lambda i: (i, 0, 0)),
        out_shape=jax.ShapeDtypeStruct((e, 2 * SC_PK_SUB, SC_ROW_LANE), I32),
        compiler_params=_cparams(("parallel",)), name="pack_expert_tables",
    )(u, v)


def _unpack_halves(x32):
    w = plsc.bitcast(x32, I32)
    return plsc.bitcast(w << 16, F32), plsc.bitcast(w & HI_MASK, F32)


def _tree_sum(xs):
    while len(xs) > 1:
        xs = [xs[i] + xs[i + 1] for i in range(0, len(xs), 2)]
    return xs[0]


def peer_experts_pk_sc(tab_uv, idx_flat, w_flat, hp, d):
    t = hp.shape[0]
    nsel = PEER_SEL
    tpw = t // SC_WORKERS
    g = SC_GROUP
    groups = tpw // g
    heads = nsel // SC_LANES
    chunks = d // 32
    units = g * heads
    ring = SC_PK_RING
    row_buf = pltpu.VMEM((SC_LANES, 2 * SC_PK_SUB, SC_ROW_LANE), I32)

    def row_words(rows, r, wc, sub0):
        per = SC_ROW_LANE // SC_LANES
        return plsc.bitcast(
            rows[r, sub0 + wc // per, pl.ds(pl.multiple_of((wc % per) * SC_LANES, SC_LANES), SC_LANES)], BF16)

    def ring_loop(n_units, start, wait, compute):
        for u in range(ring - 1):
            start(u, u)

        @pl.loop(0, n_units, step=ring)
        def _(uu):
            for b in range(ring):
                u = uu + b
                nxt = u + (ring - 1)

                @pl.when(nxt < n_units)
                def _():
                    start(nxt, (b + ring - 1) % ring)

                wait(u, b)
                compute(u, b)

    @functools.partial(
        pl.kernel, mesh=_sc_mesh(),
        out_type=jax.ShapeDtypeStruct((t, d), F32),
        scratch_types=[
            pltpu.VMEM((g * nsel,), I32),
            pltpu.VMEM((g * nsel,), F32),
            pltpu.VMEM((g, d // 2), I32),
            pltpu.VMEM((g, d), F32),
            pltpu.VMEM((SC_LANES * SC_LANES,), F32),
            [row_buf] * ring,
            [pltpu.SemaphoreType.DMA] * ring,
        ],
        compiler_params=_sc_params(),
        name="peer_experts_pk_sc",
    )
    def k(tab_hbm, idx_hbm, w_hbm, h_hbm, out_hbm, idx_v, coef_v, h_v, y_v, red_v, rows, sems):
        wid = _sc_worker_id()
        lane = lax.iota(I32, SC_LANES)

        def copy(u, slot):
            ids = idx_v.at[pl.ds(_sc_unit_off(u), SC_LANES)]
            return pltpu.make_async_copy(tab_hbm.at[ids], rows[slot], sems[slot])

        def dots(u, slot):
            tt = u // heads

            def body(cp, accs):
                out = []
                hv = [plsc.bitcast(h_v[tt, pl.ds(pl.multiple_of((2 * cp + i) * SC_LANES, SC_LANES), SC_LANES)], BF16)
                      for i in range(2)]
                for r in range(SC_LANES):
                    pr = (row_words(rows[slot], r, 2 * cp, 0) * hv[0]
                          + row_words(rows[slot], r, 2 * cp + 1, 0) * hv[1])
                    lo, hi = _unpack_halves(pr)
                    out.append(accs[r] + lo + hi)
                return tuple(out)

            accs = lax.fori_loop(0, chunks // 2, body,
                                 tuple(jnp.zeros((SC_LANES,), F32) for _ in range(SC_LANES)))
            for r in range(SC_LANES):
                red_v[pl.ds(r * SC_LANES, SC_LANES)] = accs[r]
            act = _tree_sum([plsc.load_gather(red_v, [lane * SC_LANES + j]) for j in range(SC_LANES)])
            sl = pl.ds(_sc_unit_off(u), SC_LANES)
            coef_v[sl] = coef_v[sl] * _gelu_tanh(act)

        def combine(u, slot):
            tt = u // heads
            first = (u % heads) == 0
            cb = []
            for r in range(SC_LANES):
                c = plsc.load_gather(coef_v, [jnp.full((SC_LANES,), u * SC_LANES + r, I32)])
                cb.append(plsc.pack(c, c, format=plsc.PackFormat.INTERLEAVED))

            @plsc.parallel_loop(0, chunks, unroll=2)
            def _(wc):
                lo, hi = _unpack_halves(
                    _tree_sum([cb[r] * row_words(rows[slot], r, wc, SC_PK_SUB) for r in range(SC_LANES)]))
                for half, val in ((0, lo), (1, hi)):
                    sl = pl.ds(pl.multiple_of(half * (d // 2) + wc * SC_LANES, SC_LANES), SC_LANES)
                    y_v[tt, sl] = val + jnp.where(first, 0.0, y_v[tt, sl])

        def unit(u, slot):
            dots(u, slot)
            combine(u, slot)

        @pl.loop(0, groups)
        def _(gi):
            base = wid * tpw + gi * g
            pltpu.sync_copy(idx_hbm.at[pl.ds(base * nsel, g * nsel)], idx_v)
            pltpu.sync_copy(w_hbm.at[pl.ds(base * nsel, g * nsel)], coef_v)
            pltpu.sync_copy(h_hbm.at[pl.ds(base, g)], h_v)
            ring_loop(units, lambda u, s: copy(u, s).start(), lambda u, s: copy(u, s).wait(), unit)
            pltpu.sync_copy(y_v, out_hbm.at[pl.ds(base, g)])

    return k(tab_uv, idx_flat, w_flat, hp)


def kernel(x, mem, rel_bias, ln_mix, w_in, hg_lower, hg_norm, w_up_a, w_up_b, w_out, ln_cross, ln_mem, wq_x, wk_x, wv_x, wo_x, ln_ffn, peer_query, peer_subkeys, peer_u, peer_v, ln_final):
    b, s, d = x.shape
    depth = w_in.shape[0]
    assert depth == 1, "the residual after PEER is fused into the final norm"
    assert s % MB_BLOCK == 0 and s % HG_CHUNK == 0 and s % (PEER_SLICES * SC_WORKERS * SC_GROUP) == 0
    nb = s // MB_BLOCK
    row = lambda a: a.reshape(1, -1).astype(F32)
    lb_all = jnp.cumsum(jax.nn.softmax(hg_lower.astype(F32), axis=0), axis=0)
    bias = moba_bias_tiles(rel_bias)
    n_hg = 4 * HG_WIDTH
    n_qk = 2 * MB_WIDTH
    n_mb = 3 * MB_WIDTH
    l = 0
    w = w_in[l].astype(BF16)
    w_hg, w_qk, w_vt, w_g = w[:, :n_hg], w[:, n_hg:n_hg + n_qk], w[:, n_hg + n_qk:n_hg + n_mb].T, w[:, n_hg + n_mb:]
    wa, wb, wo = w_up_a[l].astype(BF16), w_up_b[l].astype(BF16), w_out[l].astype(BF16)
    wqx, wox = wq_x[l].astype(BF16), wo_x[l].astype(BF16)
    wpq, sk = peer_query[l].astype(BF16), peer_subkeys[l].astype(F32)
    tab3 = lambda a: pack_bf16_pairs(a.astype(F32)).reshape(a.shape[0], SC_PK_SUB, SC_ROW_LANE)
    tab_uv = pack_expert_tables(peer_u[l].astype(F32), peer_v[l].astype(F32))
    kx, vx = mem_kv(mem, row(ln_mem[l]), wk_x[l].astype(BF16), wv_x[l].astype(BF16))

    outs = []
    after = None
    for bi in range(b):
        x2d = x[bi]
        p0, pqk, vt, pg = in_proj(x2d, row(ln_mix[l]), w_hg, w_qk, w_vt, w_g)
        ya = hgrn2(p0, row(lb_all[l]), row(hg_norm[l]), 1, s)
        km = moba_kmean(pqk, 1, s).reshape(1, nb, MB_WIDTH)
        ts = s // PEER_SLICES
        for tok0 in range(0, s, ts):
            yb = moba_attention(pqk, vt, km, bias, 1, s, tok0 // MB_BLOCK, ts // MB_BLOCK, after)
            xs = mix_out(x2d, ya, yb, pg, wa, wb, wo, tok0)
            xs = cross_attn(xs, row(ln_cross[l]), wqx, kx[bi:bi + 1], vx[bi:bi + 1], wox, ts)
            hp, eidx, wts = peer_route(xs, row(ln_ffn[l]), wpq, sk, 0, ts)
            after = eidx[:8]
            y = peer_experts_pk_sc(tab_uv, eidx.reshape(ts * PEER_SEL), wts.reshape(ts * PEER_SEL), hp, d)
            outs.append(final_norm(xs, y, row(ln_final), 0))
    return jnp.concatenate(outs, axis=0).reshape(b, s, d)
```

```python
import functools
import math

import jax
import jax.numpy as jnp
from jax import lax
from jax.experimental import pallas as pl
from jax.experimental.pallas import tpu as pltpu
from jax.experimental.pallas import tpu_sc as plsc

F32 = jnp.float32
BF16 = jnp.bfloat16
I32 = jnp.int32
EPS = 1e-6
NEG_INF = float("-inf")

HG_HEADS = 4
HG_D = 128
HG_WIDTH = HG_HEADS * HG_D
HG_CHUNK = 64
HG_SUB = 16
MB_HEADS = 8
MB_DH = 64
MB_WIDTH = MB_HEADS * MB_DH
MB_BLOCK = 256
MB_TOPK = 3
MB_BIAS_TILES = 8
REL_BUCKETS = 32
REL_MAX_DIST = 2048
X_HEADS = 4
PEER_HEADS = 8
PEER_NKEYS = 128
PEER_TOPK = 16
PEER_HALF = 128
PEER_SEL = PEER_HEADS * PEER_TOPK
PEER_SLICES = 4

VMEM_LIMIT = 56 * 1024 * 1024


def _cparams(sem):
    return pltpu.CompilerParams(dimension_semantics=sem, vmem_limit_bytes=VMEM_LIMIT)


def _rms(x, g):
    ms = jnp.mean(x * x, axis=-1, keepdims=True)
    return x * lax.rsqrt(ms + EPS) * g


def _in_proj_kernel(x_ref, g_ref, w0_ref, w1_ref, wvt_ref, w2_ref, o0_ref, o1_ref, ovt_ref, o2_ref):
    h = _rms(x_ref[...], g_ref[...]).astype(BF16)
    o0_ref[...] = jnp.dot(h, w0_ref[...], preferred_element_type=F32)
    o1_ref[...] = jnp.dot(h, w1_ref[...], preferred_element_type=F32).astype(BF16)
    vt = lax.dot_general(wvt_ref[...], h, (((1,), (1,)), ((), ())), preferred_element_type=F32).astype(BF16)
    for hd in range(MB_HEADS):
        ovt_ref[0, hd * MB_VROWS:hd * MB_VROWS + MB_DH, :] = vt[hd * MB_DH:(hd + 1) * MB_DH]
        ovt_ref[0, hd * MB_VROWS + MB_DH:(hd + 1) * MB_VROWS, :] = jnp.ones((MB_ONES, vt.shape[1]), BF16)
    o2_ref[...] = jnp.dot(h, w2_ref[...], preferred_element_type=F32).astype(BF16)


def in_proj(x2d, g, w0, w1, wvt, w2):
    t, d = x2d.shape
    tm = MB_BLOCK
    assert wvt.shape[0] == MB_WIDTH
    n0, n1, nv, n2 = w0.shape[1], w1.shape[1], MB_VT_ROWS, w2.shape[1]
    full = lambda a: pl.BlockSpec(a.shape, lambda i: (0, 0))
    return pl.pallas_call(
        _in_proj_kernel,
        grid=(t // tm,),
        in_specs=[pl.BlockSpec((tm, d), lambda i: (i, 0)), full(g), full(w0), full(w1), full(wvt), full(w2)],
        out_specs=[pl.BlockSpec((tm, n0), lambda i: (i, 0)),
                   pl.BlockSpec((tm, n1), lambda i: (i, 0)),
                   pl.BlockSpec((1, nv, tm), lambda i: (i, 0, 0)),
                   pl.BlockSpec((tm, n2), lambda i: (i, 0))],
        out_shape=[jax.ShapeDtypeStruct((t, n0), F32),
                   jax.ShapeDtypeStruct((t, n1), BF16),
                   jax.ShapeDtypeStruct((t // tm, nv, tm), BF16),
                   jax.ShapeDtypeStruct((t, n2), BF16)],
        compiler_params=_cparams(("parallel",)),
        name="in_proj",
    )(x2d, g, w0, w1, wvt, w2)


def _hgrn_kernel(q_ref, f_ref, i_ref, g_ref, lb_ref, gain_ref, o_ref, st_ref):
    c = pl.program_id(1)

    @pl.when(c == 0)
    def _():
        st_ref[...] = jnp.zeros_like(st_ref)

    C, S = HG_CHUNK, HG_SUB
    row = lax.broadcasted_iota(I32, (C, C), 0)
    col = lax.broadcasted_iota(I32, (C, C), 1)
    tril = (row >= col).astype(F32)
    t_iota = lax.broadcasted_iota(I32, (S, 1), 0)

    for h in range(HG_HEADS):
        sl = slice(h * HG_D, (h + 1) * HG_D)
        q = q_ref[:, sl]
        v = i_ref[:, sl]
        lb = lb_ref[:, sl]
        f = lb + (1.0 - lb) * jax.nn.sigmoid(f_ref[:, sl])
        lf = jnp.log(f)
        k = 1.0 - f
        b = jnp.dot(tril, lf, precision=lax.Precision.HIGHEST, preferred_element_type=F32)
        st = st_ref[h]
        vb = v.astype(BF16)
        qd = (q * jnp.exp(b)).astype(BF16)
        o_inter = lax.dot_general(qd, st.astype(BF16), (((1,), (1,)), ((), ())),
                                  preferred_element_type=F32)
        outs = []
        for i in range(C // S):
            r0 = i * S
            qi = q[r0:r0 + S]
            ki = k[r0:r0 + S]
            bi = b[r0:r0 + S]
            vi = v[r0:r0 + S]
            oi = o_inter[r0:r0 + S]
            if i > 0:
                bs = b[r0 - 1:r0]
                qh = (qi * jnp.exp(bi - bs)).astype(BF16)
                kh = (k[:r0] * jnp.exp(bs - b[:r0])).astype(BF16)
                a = lax.dot_general(qh, kh, (((1,), (1,)), ((), ())), preferred_element_type=F32)
                oi = oi + jnp.dot(a.astype(BF16), vb[:r0], preferred_element_type=F32)
            half = S // 2
            o_half = [oi[:half], oi[half:]]
            for s in range(S):
                for hf in range(s // half, 2):
                    rows = slice(hf * half, (hf + 1) * half)
                    dec = jnp.exp(jnp.minimum(bi[rows] - bi[s:s + 1], 0.0))
                    a_s = jnp.sum(qi[rows] * ki[s:s + 1] * dec, axis=-1, keepdims=True)
                    a_s = jnp.where(t_iota[rows] >= s, a_s, 0.0)
                    o_half[hf] = o_half[hf] + a_s * vi[s:s + 1]
            outs.extend(o_half)
        o = jnp.concatenate(outs, axis=0)
        b_end = b[C - 1:C]
        kd = (k * jnp.exp(b_end - b)).astype(BF16)
        upd = lax.dot_general(vb, kd, (((0,), (0,)), ((), ())), preferred_element_type=F32)
        st_ref[h] = st * jnp.exp(b_end) + upd
        o = o * lax.rsqrt(jnp.mean(o * o, axis=-1, keepdims=True) + EPS)
        g = g_ref[:, sl]
        o_ref[:, sl] = (o * gain_ref[:, sl] * (g * jax.nn.sigmoid(g))).astype(o_ref.dtype)


def hgrn2(p0, lb, gain, batch, seq):
    t = p0.shape[0]
    nc = seq // HG_CHUNK
    w = HG_WIDTH

    def col(j):
        return pl.BlockSpec((HG_CHUNK, w), lambda b, c, j=j: (b * nc + c, j))

    return pl.pallas_call(
        _hgrn_kernel,
        grid=(batch, nc),
        in_specs=[col(0), col(1), col(2), col(3),
                  pl.BlockSpec((1, w), lambda b, c: (0, 0)),
                  pl.BlockSpec((1, w), lambda b, c: (0, 0))],
        out_specs=pl.BlockSpec((HG_CHUNK, w), lambda b, c: (b * nc + c, 0)),
        out_shape=jax.ShapeDtypeStruct((t, w), BF16),
        scratch_shapes=[pltpu.VMEM((HG_HEADS, HG_D, HG_D), F32)],
        compiler_params=_cparams(("parallel", "arbitrary")),
        name="hgrn2",
    )(p0, p0, p0, p0, lb, gain)


def _kmean_kernel(k_ref, o_ref):
    o_ref[0] = jnp.mean(k_ref[...].astype(F32), axis=0, keepdims=True)


def moba_kmean(p1, batch, seq):
    nbt = p1.shape[0] // MB_BLOCK
    return pl.pallas_call(
        _kmean_kernel,
        grid=(nbt,),
        in_specs=[pl.BlockSpec((MB_BLOCK, MB_WIDTH), lambda i: (i, 1))],
        out_specs=pl.BlockSpec((1, 1, MB_WIDTH), lambda i: (i, 0, 0)),
        out_shape=jax.ShapeDtypeStruct((nbt, 1, MB_WIDTH), F32),
        compiler_params=_cparams(("parallel",)),
        name="moba_kmean",
    )(p1)


MB_PAIR = 4
MB_PW = MB_PAIR * MB_DH
MB_LG = 128
MB_ONES = 16
MB_VROWS = MB_DH + MB_ONES
MB_VT_ROWS = MB_HEADS * MB_VROWS


def _moba_kernel(q_ref, k_ref, vt_ref, km_ref, bias_ref, o_ref, *scratch, qb0):
    m_ref, l_ref, al_ref, acc_ref, msk_ref, s_ref, p_ref = (
        scratch[i * MB_PAIR:(i + 1) * MB_PAIR] for i in range(7))
    qi = pl.program_id(2) + qb0
    nb = km_ref.shape[0]
    blk = MB_BLOCK
    heads = range(MB_PAIR)
    grp = lambda hh: slice((hh // 2) * MB_LG, (hh // 2 + 1) * MB_LG)
    q = q_ref[...]
    lane = lax.broadcasted_iota(I32, (blk, MB_LG), 1)
    in_head = [(lane < MB_DH) if hh % 2 == 0 else (lane >= MB_DH) for hh in heads]
    qs = q * jnp.asarray(MB_DH ** -0.5, BF16)
    nt = (((1,), (1,)), ((), ()))
    qf = q.astype(F32)
    qht = [jnp.where(in_head[hh], qs[:, grp(hh)].astype(F32), 0.0).T.astype(BF16) for hh in heads]

    n_io = lax.broadcasted_iota(I32, (nb, blk), 0)
    for hh in heads:
        gate = lax.dot_general(km_ref[:, grp(hh)], jnp.where(in_head[hh], qf[:, grp(hh)], 0.0), nt,
                               precision=lax.Precision.HIGHEST, preferred_element_type=F32)
        gate = jnp.where(n_io < qi, gate, NEG_INF)
        chosen = n_io < 0
        for _ in range(MB_TOPK):
            mx = jnp.max(gate, axis=0, keepdims=True)
            ix = jnp.min(jnp.where(gate == mx, n_io, nb), axis=0, keepdims=True)
            hit = n_io == ix
            chosen = chosen | (hit & (mx > NEG_INF))
            gate = jnp.where(hit, NEG_INF, gate)
        msk_ref[hh][...] = jnp.where(chosen, 0.0, NEG_INF)

    vrows = lambda hh: slice(hh * MB_VROWS, (hh + 1) * MB_VROWS)

    def pv_stage(blk_idx):
        vtb = vt_ref[blk_idx]
        r = [jnp.dot(vtb[vrows(hh)], p_ref[hh][...], preferred_element_type=F32) for hh in heads]
        al = [al_ref[hh][...] for hh in heads]
        a_new = [al[hh] * acc_ref[hh][...] + r[hh][:MB_DH] for hh in heads]
        l_new = [al[hh] * l_ref[hh][...] + r[hh][MB_DH:MB_DH + 1] for hh in heads]
        return a_new, l_new

    def store_pv(a_new, l_new):
        for hh in heads:
            acc_ref[hh][...] = a_new[hh]
            l_ref[hh][...] = l_new[hh]

    def softmax_stage():
        s = [s_ref[hh][...] for hh in heads]
        m_old = [m_ref[hh][...] for hh in heads]
        m_new = [jnp.maximum(m_old[hh], jnp.max(s[hh], axis=0, keepdims=True)) for hh in heads]
        alpha = [jnp.exp(m_old[hh] - m_new[hh]) for hh in heads]
        p = [jnp.exp((s[hh] - m_new[hh]).astype(BF16)) for hh in heads]
        return p, alpha, m_new

    def store_softmax(p, alpha, m_new):
        for hh in heads:
            p_ref[hh][...] = p[hh]
            al_ref[hh][...] = alpha[hh]
            m_ref[hh][...] = m_new[hh]

    k_own = k_ref[pl.ds(pl.multiple_of(qi * blk, blk), blk), :]
    key_io = lax.broadcasted_iota(I32, (blk, blk), 0)
    qry_io = lax.broadcasted_iota(I32, (blk, blk), 1)
    for hh in heads:
        s = jnp.dot(k_own[:, grp(hh)], qht[hh], preferred_element_type=F32) + bias_ref[hh, 0]
        s_ref[hh][...] = jnp.where(key_io <= qry_io, s, NEG_INF)
        m_ref[hh][...] = jnp.full((1, blk), NEG_INF, F32)
        l_ref[hh][...] = jnp.zeros((1, blk), F32)
        al_ref[hh][...] = jnp.ones((1, blk), F32)
        acc_ref[hh][...] = jnp.zeros((MB_DH, blk), F32)
        p_ref[hh][...] = jnp.zeros((blk, blk), BF16)

    def step(i, carry, far):
        pv = pv_stage(jnp.where(i <= 1, qi, i - 2))
        sm = softmax_stage()
        kn = k_ref[pl.ds(pl.multiple_of(i * blk, blk), blk), :]
        if far:
            row = [msk_ref[hh][pl.ds(i, 1), :] + bias_ref[hh, MB_BIAS_TILES - 1, 0:1, 0:1] for hh in heads]
            s_next = [jnp.dot(kn[:, grp(hh)], qht[hh], preferred_element_type=F32) + row[hh] for hh in heads]
        else:
            d = qi - i
            s_next = [jnp.dot(kn[:, grp(hh)], qht[hh], preferred_element_type=F32)
                      + bias_ref[hh, d] + msk_ref[hh][pl.ds(i, 1), :] for hh in heads]
        store_pv(*pv)
        for hh in heads:
            s_ref[hh][...] = s_next[hh]
        store_softmax(*sm)
        return carry

    n_far = jnp.maximum(qi - (MB_BIAS_TILES - 2), 0)
    lax.fori_loop(0, n_far, functools.partial(step, far=True), 0)
    lax.fori_loop(n_far, qi, functools.partial(step, far=False), 0)
    pv = pv_stage(jnp.where(qi <= 1, qi, qi - 2))
    sm = softmax_stage()
    store_pv(*pv)
    store_softmax(*sm)
    a_fin, l_fin = pv_stage(jnp.where(qi == 0, qi, qi - 1))
    out_t = jnp.concatenate([a_fin[hh] / l_fin[hh] for hh in heads], axis=0)
    o_ref[...] = out_t.T.astype(o_ref.dtype)


def moba_attention(pqk, vt, km, bias, batch, seq, qb0=0, nqb=None):
    nb = seq // MB_BLOCK
    nqb = nb if nqb is None else nqb
    t = batch * nqb * MB_BLOCK
    groups = MB_WIDTH // MB_PW
    return pl.pallas_call(
        functools.partial(_moba_kernel, qb0=qb0),
        grid=(batch, groups, nqb),
        in_specs=[
            pl.BlockSpec((MB_BLOCK, MB_PW), lambda b, j, i: (b * nb + qb0 + i, j)),
            pl.BlockSpec((seq, MB_PW), lambda b, j, i: (b, groups + j)),
            pl.BlockSpec((nb, MB_PAIR * MB_VROWS, MB_BLOCK), lambda b, j, i: (b, j, 0)),
            pl.BlockSpec((None, nb, MB_PW), lambda b, j, i: (b, 0, j)),
            pl.BlockSpec((MB_PAIR, MB_BIAS_TILES, MB_BLOCK, MB_BLOCK), lambda b, j, i: (j, 0, 0, 0)),
        ],
        out_specs=pl.BlockSpec((MB_BLOCK, MB_PW), lambda b, j, i: (b * nqb + i, j)),
        out_shape=jax.ShapeDtypeStruct((t, MB_WIDTH), BF16),
        scratch_shapes=(
            [pltpu.VMEM((1, MB_BLOCK), F32)] * (3 * MB_PAIR)
            + [pltpu.VMEM((MB_DH, MB_BLOCK), F32)] * MB_PAIR
            + [pltpu.VMEM((nb, MB_BLOCK), F32)] * MB_PAIR
            + [pltpu.VMEM((MB_BLOCK, MB_BLOCK), F32)] * MB_PAIR
            + [pltpu.VMEM((MB_BLOCK, MB_BLOCK), BF16)] * MB_PAIR
        ),
        compiler_params=_cparams(("parallel", "parallel", "arbitrary")),
        name="moba_attn",
    )(pqk, pqk, vt, km, bias)


def _t5_bucket(dist):
    max_exact = REL_BUCKETS // 2
    scaled = jnp.log(jnp.maximum(dist, 1).astype(F32) / max_exact) / math.log(REL_MAX_DIST / max_exact)
    large = jnp.minimum(max_exact + (scaled * (REL_BUCKETS - max_exact)).astype(I32), REL_BUCKETS - 1)
    return jnp.where(dist < max_exact, dist, large)


def moba_bias_tiles(rel_bias):
    blk = MB_BLOCK
    span = 2 * blk - 1
    x = jnp.arange(span) - (blk - 1)
    dist = jnp.maximum(jnp.arange(MB_BIAS_TILES)[:, None] * blk + x[None, :], 0)
    w = rel_bias.astype(F32).T[:, _t5_bucket(dist)]
    h = w.shape[0]
    wp = jnp.pad(w, ((0, 0), (0, 0), (0, 1)))
    a = jnp.broadcast_to(wp[:, :, None, :], (h, MB_BIAS_TILES, blk, span + 1))
    a = a.reshape(h, MB_BIAS_TILES, blk * (span + 1))[:, :, :blk * span]
    return a.reshape(h, MB_BIAS_TILES, blk, span)[:, :, :, blk - 1:]


def _mix_kernel(x_ref, ya_ref, yb_ref, ga_ref, gb_ref, wa_ref, wb_ref, wo_ref, o_ref):
    za = jnp.dot(ya_ref[...], wa_ref[...], preferred_element_type=F32)
    zb = jnp.dot(yb_ref[...], wb_ref[...], preferred_element_type=F32)
    z = jax.nn.sigmoid(ga_ref[...].astype(F32)) * za + jax.nn.sigmoid(gb_ref[...].astype(F32)) * zb
    o_ref[...] = x_ref[...] + jnp.dot(z.astype(BF16), wo_ref[...], preferred_element_type=F32)


def mix_out(x2d, ya, yb, pg, wa, wb, wo, tok0=0, tm=512):
    t = yb.shape[0]
    assert t % tm == 0 and tok0 % tm == 0
    d = x2d.shape[1]
    w = ya.shape[1]
    b0 = tok0 // tm
    return pl.pallas_call(
        _mix_kernel,
        grid=(t // tm,),
        in_specs=[
            pl.BlockSpec((tm, d), lambda i: (b0 + i, 0)),
            pl.BlockSpec((tm, w), lambda i: (b0 + i, 0)),
            pl.BlockSpec((tm, w), lambda i: (i, 0)),
            pl.BlockSpec((tm, d), lambda i: (b0 + i, 0)),
            pl.BlockSpec((tm, d), lambda i: (b0 + i, 1)),
            pl.BlockSpec((w, d), lambda i: (0, 0)),
            pl.BlockSpec((w, d), lambda i: (0, 0)),
            pl.BlockSpec((d, d), lambda i: (0, 0)),
        ],
        out_specs=pl.BlockSpec((tm, d), lambda i: (i, 0)),
        out_shape=jax.ShapeDtypeStruct((t, d), F32),
        compiler_params=_cparams(("parallel",)),
        name="mix_out",
    )(x2d, ya, yb, pg, pg, wa, wb, wo)


def _mem_kv_kernel(m_ref, g_ref, wk_ref, wv_ref, k_ref, v_ref):
    mn = _rms(m_ref[...], g_ref[...]).astype(BF16)
    k_ref[...] = jnp.dot(mn, wk_ref[...], preferred_element_type=F32).astype(BF16)
    v_ref[...] = jnp.dot(mn, wv_ref[...], preferred_element_type=F32).astype(BF16)


def mem_kv(mem, g, wk, wv):
    b, m, d = mem.shape
    spec = pl.BlockSpec((None, m, d), lambda i: (i, 0, 0))
    wspec = pl.BlockSpec((d, d), lambda i: (0, 0))
    return pl.pallas_call(
        _mem_kv_kernel,
        grid=(b,),
        in_specs=[spec, pl.BlockSpec((1, d), lambda i: (0, 0)), wspec, wspec],
        out_specs=[spec, spec],
        out_shape=[jax.ShapeDtypeStruct((b, m, d), BF16)] * 2,
        compiler_params=_cparams(("parallel",)),
        name="mem_kv",
    )(mem, g, wk, wv)


def _cross_kernel(x_ref, g_ref, wq_ref, k_ref, v_ref, wo_ref, o_ref):
    x = x_ref[...]
    d = x.shape[1]
    dh = d // X_HEADS
    h = _rms(x, g_ref[...]).astype(BF16)
    q = (jnp.dot(h, wq_ref[...], preferred_element_type=F32) * (dh ** -0.5)).astype(BF16)
    outs = []
    for hh in range(X_HEADS):
        sl = slice(hh * dh, (hh + 1) * dh)
        s = lax.dot_general(q[:, sl], k_ref[:, sl], (((1,), (1,)), ((), ())),
                            preferred_element_type=F32)
        p = jnp.exp(s - jnp.max(s, axis=1, keepdims=True))
        l = jnp.sum(p, axis=1, keepdims=True)
        o = jnp.dot(p.astype(BF16), v_ref[:, sl], preferred_element_type=F32) / l
        outs.append(o.astype(BF16))
    o = jnp.concatenate(outs, axis=1)
    o_ref[...] = x + jnp.dot(o, wo_ref[...], preferred_element_type=F32)


def cross_attn(x2d, g, wq, kx, vx, wo, seq, tm=512):
    t, d = x2d.shape
    assert seq % tm == 0 and t % seq == 0
    m = kx.shape[1]
    per_b = seq // tm
    kv = pl.BlockSpec((None, m, d), lambda i: (i // per_b, 0, 0))
    wspec = pl.BlockSpec((d, d), lambda i: (0, 0))
    return pl.pallas_call(
        _cross_kernel,
        grid=(t // tm,),
        in_specs=[pl.BlockSpec((tm, d), lambda i: (i, 0)), pl.BlockSpec((1, d), lambda i: (0, 0)),
                  wspec, kv, kv, wspec],
        out_specs=pl.BlockSpec((tm, d), lambda i: (i, 0)),
        out_shape=jax.ShapeDtypeStruct((t, d), F32),
        compiler_params=_cparams(("parallel",)),
        name="cross_attn",
    )(x2d, g, wq, kx, vx, wo)


def _topk_rows(sc, k):
    n = sc.shape[0]
    io = lax.broadcasted_iota(I32, sc.shape, 0)
    vals, ids = [], []
    for _ in range(k):
        m = jnp.max(sc, axis=0, keepdims=True)
        ix = jnp.min(jnp.where(sc == m, io, n), axis=0, keepdims=True)
        vals.append(m)
        ids.append(ix)
        sc = jnp.where(io == ix, NEG_INF, sc)
    return jnp.concatenate(vals, axis=0), jnp.concatenate(ids, axis=0)


def _pack_bf16_halves(h):
    bits = lax.bitcast_convert_type(h, I32)
    r = bits + 0x7FFF + (lax.shift_right_logical(bits, 16) & 1)
    half = h.shape[1] // 2
    return lax.shift_right_logical(r[:, :half], 16) | (r[:, half:] & HI_MASK)


def _route_kernel(x_ref, g_ref, wq_ref, sk_ref, hp_ref, idx_ref, w_ref, hb_ref, it_ref, wt_ref):
    p = pl.program_id(1)

    @pl.when(p == 0)
    def _():
        h = _rms(x_ref[...], g_ref[...])
        hp_ref[...] = _pack_bf16_halves(h)
        hb_ref[...] = h.astype(BF16)

    qh = jnp.dot(hb_ref[...], wq_ref[...], preferred_element_type=F32)
    tops = []
    for c in range(2):
        seg = qh[:, c * PEER_HALF:(c + 1) * PEER_HALF]
        sc = lax.dot_general(sk_ref[c], seg, (((1,), (1,)), ((), ())),
                             precision=lax.Precision.HIGHEST, preferred_element_type=F32)
        tops.append(_topk_rows(sc, PEER_TOPK))
    (s0, i0), (s1, i1) = tops
    k = PEER_TOPK
    sub = 8
    tm = s0.shape[1]
    r8 = lax.broadcasted_iota(I32, (sub, tm), 0)
    r16 = lax.broadcasted_iota(I32, (k, tm), 0)
    cand_b = [s0[0:1] + s1, s0[1:2] + s1[:sub]]
    cidx_b = [i0[0:1] * PEER_NKEYS + i1, i0[1:2] * PEER_NKEYS + i1[:sub]]
    pos_b = [r16, k + r8]
    for a in range(2, sub):
        keep = r8 < (k // (a + 1))
        cand_b.append(jnp.where(keep, s0[a:a + 1] + s1[:sub], NEG_INF))
        cidx_b.append(i0[a:a + 1] * PEER_NKEYS + i1[:sub])
        pos_b.append(a * k + r8)
    cand_b.append(s0[sub:] + s1[0:1])
    cidx_b.append(i0[sub:] * PEER_NKEYS + i1[0:1])
    pos_b.append((sub + r8) * k)
    cand = jnp.concatenate(cand_b, axis=0)
    cidx = jnp.concatenate(cidx_b, axis=0)
    pos = jnp.concatenate(pos_b, axis=0)
    vals, ids = [], []
    for _ in range(k):
        m = jnp.max(cand, axis=0, keepdims=True)
        px = jnp.min(jnp.where(cand == m, pos, k * k), axis=0, keepdims=True)
        hit = pos == px
        vals.append(m)
        ids.append(jnp.sum(jnp.where(hit, cidx, 0), axis=0, keepdims=True))
        cand = jnp.where(hit, NEG_INF, cand)
    sf = jnp.concatenate(vals, axis=0)
    e = jnp.exp(sf - sf[0:1])
    rows = pl.ds(pl.multiple_of(p * PEER_TOPK, PEER_TOPK), PEER_TOPK)
    wt_ref[rows, :] = e / jnp.sum(e, axis=0, keepdims=True)
    it_ref[rows, :] = jnp.concatenate(ids, axis=0)

    @pl.when(p == pl.num_programs(1) - 1)
    def _():
        idx_ref[...] = it_ref[...].T
        w_ref[...] = wt_ref[...].T


def peer_route(x2d, g, wq, sk, tok0, t, tm=1024):
    assert t % tm == 0 and tok0 % tm == 0
    d = x2d.shape[1]
    ph = sk.shape[0]
    nsel = ph * PEER_TOPK
    blk0 = tok0 // tm
    return pl.pallas_call(
        _route_kernel,
        grid=(t // tm, ph),
        in_specs=[
            pl.BlockSpec((tm, d), lambda i, p: (blk0 + i, 0)),
            pl.BlockSpec((1, d), lambda i, p: (0, 0)),
            pl.BlockSpec((d, 2 * PEER_HALF), lambda i, p: (0, p)),
            pl.BlockSpec((None, 2, PEER_NKEYS, PEER_HALF), lambda i, p: (p, 0, 0, 0)),
        ],
        out_specs=[
            pl.BlockSpec((tm, d // 2), lambda i, p: (i, 0)),
            pl.BlockSpec((tm, nsel), lambda i, p: (i, 0)),
            pl.BlockSpec((tm, nsel), lambda i, p: (i, 0)),
        ],
        out_shape=[jax.ShapeDtypeStruct((t, d // 2), I32),
                   jax.ShapeDtypeStruct((t, nsel), I32),
                   jax.ShapeDtypeStruct((t, nsel), F32)],
        scratch_shapes=[pltpu.VMEM((tm, d), BF16),
                        pltpu.VMEM((nsel, tm), I32),
                        pltpu.VMEM((nsel, tm), F32)],
        compiler_params=_cparams(("parallel", "arbitrary")),
        name="peer_route",
    )(x2d, g, wq, sk)


def _final_kernel(x_ref, y_ref, g_ref, o_ref):
    o_ref[...] = _rms(x_ref[...] + y_ref[...], g_ref[...])


def final_norm(x2d, y, g, tok0, tm=512):
    t, d = y.shape
    assert t % tm == 0 and tok0 % tm == 0
    blk0 = tok0 // tm
    spec = pl.BlockSpec((tm, d), lambda i: (i, 0))
    return pl.pallas_call(
        _final_kernel, grid=(t // tm,),
        in_specs=[pl.BlockSpec((tm, d), lambda i: (blk0 + i, 0)), spec, pl.BlockSpec((1, d), lambda i: (0, 0))],
        out_specs=spec,
        out_shape=jax.ShapeDtypeStruct((t, d), F32),
        compiler_params=_cparams(("parallel",)), name="final_norm",
    )(x2d, y, g)


SC_CORES = 2
SC_SUBCORES = 16
SC_WORKERS = SC_CORES * SC_SUBCORES
SC_LANES = 16
SC_GROUP = 16


def _sc_mesh():
    return plsc.VectorSubcoreMesh(core_axis_name="c", subcore_axis_name="s")


def _sc_params():
    return pltpu.CompilerParams(needs_layout_passes=False)


def _sc_worker_id():
    return lax.axis_index("s") * SC_CORES + lax.axis_index("c")


SC_ROW_LANE = 128


def _sc_unit_off(u):
    off = u * SC_LANES
    return off if isinstance(off, int) else pl.multiple_of(off, SC_LANES)


GELU_C0 = math.sqrt(2.0 / math.pi)
GELU_C1 = 0.044715


def _gelu_tanh(x):
    z = GELU_C0 * (x + GELU_C1 * (x * x * x))
    th = 1.0 - 2.0 / (jnp.exp(2.0 * z) + 1.0)
    return 0.5 * x * (1.0 + th)


SC_PK_RING = 4
SC_PK_SUB = 4
HI_MASK = -65536


def _pack_tables_kernel(u_ref, v_ref, o_ref):
    for part, ref in enumerate((u_ref, v_ref)):
        words = _pack_bf16_halves(ref[...])
        for sub in range(SC_PK_SUB):
            o_ref[:, part * SC_PK_SUB + sub, :] = words[:, sub * SC_ROW_LANE:(sub + 1) * SC_ROW_LANE]


def pack_expert_tables(u, v, te=512):
    e, d = u.shape
    assert d == 2 * SC_PK_SUB * SC_ROW_LANE
    spec = pl.BlockSpec((te, d), lambda i: (i, 0))
    return pl.pallas_call(
        _pack_tables_kernel, grid=(e // te,), in_specs=[spec, spec],
        out_specs=pl.BlockSpec((te, 2 * SC_PK_SUB, SC_ROW_LANE), lambda i: (i, 0, 0)),
        out_shape=jax.ShapeDtypeStruct((e, 2 * SC_PK_SUB, SC_ROW_LANE), I32),
        compiler_params=_cparams(("parallel",)), name="pack_expert_tables",
    )(u, v)


def _unpack_halves(x32):
    w = plsc.bitcast(x32, I32)
    return plsc.bitcast(w << 16, F32), plsc.bitcast(w & HI_MASK, F32)


def _tree_sum(xs):
    while len(xs) > 1:
        xs = [xs[i] + xs[i + 1] for i in range(0, len(xs), 2)]
    return xs[0]


def peer_experts_pk_sc(tab_uv, idx_flat, w_flat, hp, d):
    t = hp.shape[0]
    nsel = PEER_SEL
    g = SC_GROUP
    assert t % (SC_WORKERS * g) == 0 and d == 2 * SC_PK_SUB * SC_ROW_LANE
    tpw = t // SC_WORKERS
    groups = tpw // g
    heads = nsel // SC_LANES
    chunks = d // 32
    units = g * heads
    ring = SC_PK_RING
    assert units % ring == 0
    row_buf = pltpu.VMEM((SC_LANES, 2 * SC_PK_SUB, SC_ROW_LANE), I32)

    def row_words(rows, r, wc, sub0):
        per = SC_ROW_LANE // SC_LANES
        return plsc.bitcast(
            rows[r, sub0 + wc // per, pl.ds(pl.multiple_of((wc % per) * SC_LANES, SC_LANES), SC_LANES)], BF16)

    def ring_loop(n_units, start, wait, compute):
        for u in range(ring - 1):
            start(u, u)

        @pl.loop(0, n_units, step=ring)
        def _(uu):
            for b in range(ring):
                u = uu + b
                nxt = u + (ring - 1)

                @pl.when(nxt < n_units)
                def _():
                    start(nxt, (b + ring - 1) % ring)

                wait(u, b)
                compute(u, b)

    @functools.partial(
        pl.kernel, mesh=_sc_mesh(),
        out_type=jax.ShapeDtypeStruct((t, d), F32),
        scratch_types=[
            pltpu.VMEM((g * nsel,), I32),
            pltpu.VMEM((g * nsel,), F32),
            pltpu.VMEM((g, d // 2), I32),
            pltpu.VMEM((g, d), F32),
            pltpu.VMEM((SC_LANES * SC_LANES,), F32),
            [row_buf] * ring,
            [pltpu.SemaphoreType.DMA] * ring,
        ],
        compiler_params=_sc_params(),
        name="peer_experts_pk_sc",
    )
    def k(tab_hbm, idx_hbm, w_hbm, h_hbm, out_hbm, idx_v, coef_v, h_v, y_v, red_v, rows, sems):
        wid = _sc_worker_id()
        lane = lax.iota(I32, SC_LANES)

        def copy(u, slot):
            ids = idx_v.at[pl.ds(_sc_unit_off(u), SC_LANES)]
            return pltpu.make_async_copy(tab_hbm.at[ids], rows[slot], sems[slot])

        def dots(u, slot):
            tt = u // heads

            def body(cp, accs):
                out = []
                hv = [plsc.bitcast(h_v[tt, pl.ds(pl.multiple_of((2 * cp + i) * SC_LANES, SC_LANES), SC_LANES)], BF16)
                      for i in range(2)]
                for r in range(SC_LANES):
                    pr = (row_words(rows[slot], r, 2 * cp, 0) * hv[0]
                          + row_words(rows[slot], r, 2 * cp + 1, 0) * hv[1])
                    lo, hi = _unpack_halves(pr)
                    out.append(accs[r] + lo + hi)
                return tuple(out)

            accs = lax.fori_loop(0, chunks // 2, body,
                                 tuple(jnp.zeros((SC_LANES,), F32) for _ in range(SC_LANES)))
            for r in range(SC_LANES):
                red_v[pl.ds(r * SC_LANES, SC_LANES)] = accs[r]
            act = _tree_sum([plsc.load_gather(red_v, [lane * SC_LANES + j]) for j in range(SC_LANES)])
            sl = pl.ds(_sc_unit_off(u), SC_LANES)
            coef_v[sl] = coef_v[sl] * _gelu_tanh(act)

        def combine(u, slot):
            tt = u // heads
            first = (u % heads) == 0
            cb = []
            for r in range(SC_LANES):
                c = plsc.load_gather(coef_v, [jnp.full((SC_LANES,), u * SC_LANES + r, I32)])
                cb.append(plsc.pack(c, c, format=plsc.PackFormat.INTERLEAVED))

            @plsc.parallel_loop(0, chunks, unroll=2)
            def _(wc):
                lo, hi = _unpack_halves(
                    _tree_sum([cb[r] * row_words(rows[slot], r, wc, SC_PK_SUB) for r in range(SC_LANES)]))
                for half, val in ((0, lo), (1, hi)):
                    sl = pl.ds(pl.multiple_of(half * (d // 2) + wc * SC_LANES, SC_LANES), SC_LANES)
                    y_v[tt, sl] = val + jnp.where(first, 0.0, y_v[tt, sl])

        def unit(u, slot):
            dots(u, slot)
            combine(u, slot)

        @pl.loop(0, groups)
        def _(gi):
            base = wid * tpw + gi * g
            pltpu.sync_copy(idx_hbm.at[pl.ds(base * nsel, g * nsel)], idx_v)
            pltpu.sync_copy(w_hbm.at[pl.ds(base * nsel, g * nsel)], coef_v)
            pltpu.sync_copy(h_hbm.at[pl.ds(base, g)], h_v)
            ring_loop(units, lambda u, s: copy(u, s).start(), lambda u, s: copy(u, s).wait(), unit)
            pltpu.sync_copy(y_v, out_hbm.at[pl.ds(base, g)])

    return k(tab_uv, idx_flat, w_flat, hp)


def kernel(x, mem, rel_bias, ln_mix, w_in, hg_lower, hg_norm, w_up_a, w_up_b, w_out, ln_cross, ln_mem, wq_x, wk_x, wv_x, wo_x, ln_ffn, peer_query, peer_subkeys, peer_u, peer_v, ln_final):
    b, s, d = x.shape
    depth = w_in.shape[0]
    assert depth == 1, "the residual after PEER is fused into the final norm"
    assert s % MB_BLOCK == 0 and s % HG_CHUNK == 0 and s % (PEER_SLICES * SC_WORKERS * SC_GROUP) == 0
    nb = s // MB_BLOCK
    row = lambda a: a.reshape(1, -1).astype(F32)
    lb_all = jnp.cumsum(jax.nn.softmax(hg_lower.astype(F32), axis=0), axis=0)
    bias = moba_bias_tiles(rel_bias)
    n_hg = 4 * HG_WIDTH
    n_qk = 2 * MB_WIDTH
    n_mb = 3 * MB_WIDTH
    l = 0
    w = w_in[l].astype(BF16)
    w_hg, w_qk, w_vt, w_g = w[:, :n_hg], w[:, n_hg:n_hg + n_qk], w[:, n_hg + n_qk:n_hg + n_mb].T, w[:, n_hg + n_mb:]
    wa, wb, wo = w_up_a[l].astype(BF16), w_up_b[l].astype(BF16), w_out[l].astype(BF16)
    wqx, wox = wq_x[l].astype(BF16), wo_x[l].astype(BF16)
    wpq, sk = peer_query[l].astype(BF16), peer_subkeys[l].astype(F32)
    tab_uv = pack_expert_tables(peer_u[l].astype(F32), peer_v[l].astype(F32))
    kx, vx = mem_kv(mem, row(ln_mem[l]), wk_x[l].astype(BF16), wv_x[l].astype(BF16))

    outs = []
    for bi in range(b):
        x2d = x[bi]
        p0, pqk, vt, pg = in_proj(x2d, row(ln_mix[l]), w_hg, w_qk, w_vt, w_g)
        ya = hgrn2(p0, row(lb_all[l]), row(hg_norm[l]), 1, s)
        km = moba_kmean(pqk, 1, s).reshape(1, nb, MB_WIDTH)
        ts = s // PEER_SLICES
        for tok0 in range(0, s, ts):
            yb = moba_attention(pqk, vt, km, bias, 1, s, tok0 // MB_BLOCK, ts // MB_BLOCK)
            xs = mix_out(x2d, ya, yb, pg, wa, wb, wo, tok0)
            xs = cross_attn(xs, row(ln_cross[l]), wqx, kx[bi:bi + 1], vx[bi:bi + 1], wox, ts)
            hp, eidx, wts = peer_route(xs, row(ln_ffn[l]), wpq, sk, 0, ts)
            y = peer_experts_pk_sc(tab_uv, eidx.reshape(ts * PEER_SEL), wts.reshape(ts * PEER_SEL), hp, d)
            outs.append(final_norm(xs, y, row(ln_final), 0))
    return jnp.concatenate(outs, axis=0).reshape(b, s, d)
```

```python
import functools
import math

import jax
import jax.numpy as jnp
from jax import lax
from jax.experimental import pallas as pl
from jax.experimental.pallas import tpu as pltpu
from jax.experimental.pallas import tpu_sc as plsc

F32 = jnp.float32
BF16 = jnp.bfloat16
I32 = jnp.int32
EPS = 1e-6
NEG_INF = float("-inf")

HG_HEADS = 4
HG_D = 128
HG_WIDTH = HG_HEADS * HG_D
HG_CHUNK = 64
HG_SUB = 16
MB_HEADS = 8
MB_DH = 64
MB_WIDTH = MB_HEADS * MB_DH
MB_BLOCK = 256
MB_TOPK = 3
MB_BIAS_TILES = 8
REL_BUCKETS = 32
REL_MAX_DIST = 2048
X_HEADS = 4
PEER_HEADS = 8
PEER_NKEYS = 128
PEER_TOPK = 16
PEER_HALF = 128
PEER_SEL = PEER_HEADS * PEER_TOPK
PEER_SLICES = 8

VMEM_LIMIT = 56 * 1024 * 1024


def _cparams(sem):
    return pltpu.CompilerParams(dimension_semantics=sem, vmem_limit_bytes=VMEM_LIMIT)


def _rms(x, g):
    ms = jnp.mean(x * x, axis=-1, keepdims=True)
    return x * lax.rsqrt(ms + EPS) * g


def _in_proj_kernel(x_ref, g_ref, w0_ref, w1_ref, wvt_ref, w2_ref, o0_ref, o1_ref, ovt_ref, o2_ref):
    h = _rms(x_ref[...], g_ref[...]).astype(BF16)
    o0_ref[...] = jnp.dot(h, w0_ref[...], preferred_element_type=F32)
    o1_ref[...] = jnp.dot(h, w1_ref[...], preferred_element_type=F32).astype(BF16)
    vt = lax.dot_general(wvt_ref[...], h, (((1,), (1,)), ((), ())), preferred_element_type=F32).astype(BF16)
    for hd in range(MB_HEADS):
        ovt_ref[0, hd * MB_VROWS:hd * MB_VROWS + MB_DH, :] = vt[hd * MB_DH:(hd + 1) * MB_DH]
        ovt_ref[0, hd * MB_VROWS + MB_DH:(hd + 1) * MB_VROWS, :] = jnp.ones((MB_ONES, vt.shape[1]), BF16)
    o2_ref[...] = jnp.dot(h, w2_ref[...], preferred_element_type=F32).astype(BF16)


def in_proj(x2d, g, w0, w1, wvt, w2):
    t, d = x2d.shape
    tm = MB_BLOCK
    assert wvt.shape[0] == MB_WIDTH
    n0, n1, nv, n2 = w0.shape[1], w1.shape[1], MB_VT_ROWS, w2.shape[1]
    full = lambda a: pl.BlockSpec(a.shape, lambda i: (0, 0))
    return pl.pallas_call(
        _in_proj_kernel,
        grid=(t // tm,),
        in_specs=[pl.BlockSpec((tm, d), lambda i: (i, 0)), full(g), full(w0), full(w1), full(wvt), full(w2)],
        out_specs=[pl.BlockSpec((tm, n0), lambda i: (i, 0)),
                   pl.BlockSpec((tm, n1), lambda i: (i, 0)),
                   pl.BlockSpec((1, nv, tm), lambda i: (i, 0, 0)),
                   pl.BlockSpec((tm, n2), lambda i: (i, 0))],
        out_shape=[jax.ShapeDtypeStruct((t, n0), F32),
                   jax.ShapeDtypeStruct((t, n1), BF16),
                   jax.ShapeDtypeStruct((t // tm, nv, tm), BF16),
                   jax.ShapeDtypeStruct((t, n2), BF16)],
        compiler_params=_cparams(("parallel",)),
        name="in_proj",
    )(x2d, g, w0, w1, wvt, w2)


def _hgrn_kernel(q_ref, f_ref, i_ref, g_ref, lb_ref, gain_ref, o_ref, st_ref):
    c = pl.program_id(1)

    @pl.when(c == 0)
    def _():
        st_ref[...] = jnp.zeros_like(st_ref)

    C, S = HG_CHUNK, HG_SUB
    row = lax.broadcasted_iota(I32, (C, C), 0)
    col = lax.broadcasted_iota(I32, (C, C), 1)
    tril = (row >= col).astype(F32)
    t_iota = lax.broadcasted_iota(I32, (S, 1), 0)

    for h in range(HG_HEADS):
        sl = slice(h * HG_D, (h + 1) * HG_D)
        q = q_ref[:, sl]
        v = i_ref[:, sl]
        lb = lb_ref[:, sl]
        f = lb + (1.0 - lb) * jax.nn.sigmoid(f_ref[:, sl])
        lf = jnp.log(f)
        k = 1.0 - f
        b = jnp.dot(tril, lf, precision=lax.Precision.HIGHEST, preferred_element_type=F32)
        st = st_ref[h]
        vb = v.astype(BF16)
        qd = (q * jnp.exp(b)).astype(BF16)
        o_inter = lax.dot_general(qd, st.astype(BF16), (((1,), (1,)), ((), ())),
                                  preferred_element_type=F32)
        outs = []
        for i in range(C // S):
            r0 = i * S
            qi = q[r0:r0 + S]
            ki = k[r0:r0 + S]
            bi = b[r0:r0 + S]
            vi = v[r0:r0 + S]
            oi = o_inter[r0:r0 + S]
            if i > 0:
                bs = b[r0 - 1:r0]
                qh = (qi * jnp.exp(bi - bs)).astype(BF16)
                kh = (k[:r0] * jnp.exp(bs - b[:r0])).astype(BF16)
                a = lax.dot_general(qh, kh, (((1,), (1,)), ((), ())), preferred_element_type=F32)
                oi = oi + jnp.dot(a.astype(BF16), vb[:r0], preferred_element_type=F32)
            half = S // 2
            o_half = [oi[:half], oi[half:]]
            for s in range(S):
                for hf in range(s // half, 2):
                    rows = slice(hf * half, (hf + 1) * half)
                    dec = jnp.exp(jnp.minimum(bi[rows] - bi[s:s + 1], 0.0))
                    a_s = jnp.sum(qi[rows] * ki[s:s + 1] * dec, axis=-1, keepdims=True)
                    a_s = jnp.where(t_iota[rows] >= s, a_s, 0.0)
                    o_half[hf] = o_half[hf] + a_s * vi[s:s + 1]
            outs.extend(o_half)
        o = jnp.concatenate(outs, axis=0)
        b_end = b[C - 1:C]
        kd = (k * jnp.exp(b_end - b)).astype(BF16)
        upd = lax.dot_general(vb, kd, (((0,), (0,)), ((), ())), preferred_element_type=F32)
        st_ref[h] = st * jnp.exp(b_end) + upd
        o = o * lax.rsqrt(jnp.mean(o * o, axis=-1, keepdims=True) + EPS)
        g = g_ref[:, sl]
        o_ref[:, sl] = (o * gain_ref[:, sl] * (g * jax.nn.sigmoid(g))).astype(o_ref.dtype)


def hgrn2(p0, lb, gain, batch, seq):
    t = p0.shape[0]
    nc = seq // HG_CHUNK
    w = HG_WIDTH

    def col(j):
        return pl.BlockSpec((HG_CHUNK, w), lambda b, c, j=j: (b * nc + c, j))

    return pl.pallas_call(
        _hgrn_kernel,
        grid=(batch, nc),
        in_specs=[col(0), col(1), col(2), col(3),
                  pl.BlockSpec((1, w), lambda b, c: (0, 0)),
                  pl.BlockSpec((1, w), lambda b, c: (0, 0))],
        out_specs=pl.BlockSpec((HG_CHUNK, w), lambda b, c: (b * nc + c, 0)),
        out_shape=jax.ShapeDtypeStruct((t, w), BF16),
        scratch_shapes=[pltpu.VMEM((HG_HEADS, HG_D, HG_D), F32)],
        compiler_params=_cparams(("parallel", "arbitrary")),
        name="hgrn2",
    )(p0, p0, p0, p0, lb, gain)


def _kmean_kernel(k_ref, o_ref):
    o_ref[0] = jnp.mean(k_ref[...].astype(F32), axis=0, keepdims=True)


def moba_kmean(p1, batch, seq):
    nbt = p1.shape[0] // MB_BLOCK
    return pl.pallas_call(
        _kmean_kernel,
        grid=(nbt,),
        in_specs=[pl.BlockSpec((MB_BLOCK, MB_WIDTH), lambda i: (i, 1))],
        out_specs=pl.BlockSpec((1, 1, MB_WIDTH), lambda i: (i, 0, 0)),
        out_shape=jax.ShapeDtypeStruct((nbt, 1, MB_WIDTH), F32),
        compiler_params=_cparams(("parallel",)),
        name="moba_kmean",
    )(p1)


MB_PAIR = 4
MB_PW = MB_PAIR * MB_DH
MB_LG = 128
MB_ONES = 16
MB_VROWS = MB_DH + MB_ONES
MB_VT_ROWS = MB_HEADS * MB_VROWS


def _moba_kernel(q_ref, k_ref, vt_ref, km_ref, bias_ref, o_ref, *scratch, qb0):
    m_ref, l_ref, al_ref, acc_ref, msk_ref, s_ref, p_ref = (
        scratch[i * MB_PAIR:(i + 1) * MB_PAIR] for i in range(7))
    qi = pl.program_id(2) + qb0
    nb = km_ref.shape[0]
    blk = MB_BLOCK
    heads = range(MB_PAIR)
    grp = lambda hh: slice((hh // 2) * MB_LG, (hh // 2 + 1) * MB_LG)
    q = q_ref[...]
    lane = lax.broadcasted_iota(I32, (blk, MB_LG), 1)
    in_head = [(lane < MB_DH) if hh % 2 == 0 else (lane >= MB_DH) for hh in heads]
    qs = q * jnp.asarray(MB_DH ** -0.5, BF16)
    nt = (((1,), (1,)), ((), ()))
    qf = q.astype(F32)
    qht = [jnp.where(in_head[hh], qs[:, grp(hh)].astype(F32), 0.0).T.astype(BF16) for hh in heads]

    n_io = lax.broadcasted_iota(I32, (nb, blk), 0)
    for hh in heads:
        gate = lax.dot_general(km_ref[:, grp(hh)], jnp.where(in_head[hh], qf[:, grp(hh)], 0.0), nt,
                               precision=lax.Precision.HIGHEST, preferred_element_type=F32)
        gate = jnp.where(n_io < qi, gate, NEG_INF)
        chosen = n_io < 0
        for _ in range(MB_TOPK):
            mx = jnp.max(gate, axis=0, keepdims=True)
            ix = jnp.min(jnp.where(gate == mx, n_io, nb), axis=0, keepdims=True)
            hit = n_io == ix
            chosen = chosen | (hit & (mx > NEG_INF))
            gate = jnp.where(hit, NEG_INF, gate)
        msk_ref[hh][...] = jnp.where(chosen, 0.0, NEG_INF)

    vrows = lambda hh: slice(hh * MB_VROWS, (hh + 1) * MB_VROWS)

    def pv_stage(blk_idx):
        vtb = vt_ref[blk_idx]
        r = [jnp.dot(vtb[vrows(hh)], p_ref[hh][...], preferred_element_type=F32) for hh in heads]
        al = [al_ref[hh][...] for hh in heads]
        a_new = [al[hh] * acc_ref[hh][...] + r[hh][:MB_DH] for hh in heads]
        l_new = [al[hh] * l_ref[hh][...] + r[hh][MB_DH:MB_DH + 1] for hh in heads]
        return a_new, l_new

    def store_pv(a_new, l_new):
        for hh in heads:
            acc_ref[hh][...] = a_new[hh]
            l_ref[hh][...] = l_new[hh]

    def softmax_stage():
        s = [s_ref[hh][...] for hh in heads]
        m_old = [m_ref[hh][...] for hh in heads]
        m_new = [jnp.maximum(m_old[hh], jnp.max(s[hh], axis=0, keepdims=True)) for hh in heads]
        alpha = [jnp.exp(m_old[hh] - m_new[hh]) for hh in heads]
        p = [jnp.exp((s[hh] - m_new[hh]).astype(BF16)) for hh in heads]
        return p, alpha, m_new

    def store_softmax(p, alpha, m_new):
        for hh in heads:
            p_ref[hh][...] = p[hh]
            al_ref[hh][...] = alpha[hh]
            m_ref[hh][...] = m_new[hh]

    k_own = k_ref[pl.ds(pl.multiple_of(qi * blk, blk), blk), :]
    key_io = lax.broadcasted_iota(I32, (blk, blk), 0)
    qry_io = lax.broadcasted_iota(I32, (blk, blk), 1)
    for hh in heads:
        s = jnp.dot(k_own[:, grp(hh)], qht[hh], preferred_element_type=F32) + bias_ref[hh, 0]
        s_ref[hh][...] = jnp.where(key_io <= qry_io, s, NEG_INF)
        m_ref[hh][...] = jnp.full((1, blk), NEG_INF, F32)
        l_ref[hh][...] = jnp.zeros((1, blk), F32)
        al_ref[hh][...] = jnp.ones((1, blk), F32)
        acc_ref[hh][...] = jnp.zeros((MB_DH, blk), F32)
        p_ref[hh][...] = jnp.zeros((blk, blk), BF16)

    def step(i, carry, far):
        pv = pv_stage(jnp.where(i <= 1, qi, i - 2))
        sm = softmax_stage()
        kn = k_ref[pl.ds(pl.multiple_of(i * blk, blk), blk), :]
        if far:
            row = [msk_ref[hh][pl.ds(i, 1), :] + bias_ref[hh, MB_BIAS_TILES - 1, 0:1, 0:1] for hh in heads]
            s_next = [jnp.dot(kn[:, grp(hh)], qht[hh], preferred_element_type=F32) + row[hh] for hh in heads]
        else:
            d = qi - i
            s_next = [jnp.dot(kn[:, grp(hh)], qht[hh], preferred_element_type=F32)
                      + bias_ref[hh, d] + msk_ref[hh][pl.ds(i, 1), :] for hh in heads]
        store_pv(*pv)
        for hh in heads:
            s_ref[hh][...] = s_next[hh]
        store_softmax(*sm)
        return carry

    n_far = jnp.maximum(qi - (MB_BIAS_TILES - 2), 0)
    lax.fori_loop(0, n_far, functools.partial(step, far=True), 0)
    lax.fori_loop(n_far, qi, functools.partial(step, far=False), 0)
    pv = pv_stage(jnp.where(qi <= 1, qi, qi - 2))
    sm = softmax_stage()
    store_pv(*pv)
    store_softmax(*sm)
    a_fin, l_fin = pv_stage(jnp.where(qi == 0, qi, qi - 1))
    out_t = jnp.concatenate([a_fin[hh] / l_fin[hh] for hh in heads], axis=0)
    o_ref[...] = out_t.T.astype(o_ref.dtype)


def moba_attention(pqk, vt, km, bias, batch, seq, qb0=0, nqb=None):
    nb = seq // MB_BLOCK
    nqb = nb if nqb is None else nqb
    t = batch * nqb * MB_BLOCK
    groups = MB_WIDTH // MB_PW
    return pl.pallas_call(
        functools.partial(_moba_kernel, qb0=qb0),
        grid=(batch, groups, nqb),
        in_specs=[
            pl.BlockSpec((MB_BLOCK, MB_PW), lambda b, j, i: (b * nb + qb0 + i, j)),
            pl.BlockSpec((seq, MB_PW), lambda b, j, i: (b, groups + j)),
            pl.BlockSpec((nb, MB_PAIR * MB_VROWS, MB_BLOCK), lambda b, j, i: (b, j, 0)),
            pl.BlockSpec((None, nb, MB_PW), lambda b, j, i: (b, 0, j)),
            pl.BlockSpec((MB_PAIR, MB_BIAS_TILES, MB_BLOCK, MB_BLOCK), lambda b, j, i: (j, 0, 0, 0)),
        ],
        out_specs=pl.BlockSpec((MB_BLOCK, MB_PW), lambda b, j, i: (b * nqb + i, j)),
        out_shape=jax.ShapeDtypeStruct((t, MB_WIDTH), BF16),
        scratch_shapes=(
            [pltpu.VMEM((1, MB_BLOCK), F32)] * (3 * MB_PAIR)
            + [pltpu.VMEM((MB_DH, MB_BLOCK), F32)] * MB_PAIR
            + [pltpu.VMEM((nb, MB_BLOCK), F32)] * MB_PAIR
            + [pltpu.VMEM((MB_BLOCK, MB_BLOCK), F32)] * MB_PAIR
            + [pltpu.VMEM((MB_BLOCK, MB_BLOCK), BF16)] * MB_PAIR
        ),
        compiler_params=_cparams(("parallel", "parallel", "arbitrary")),
        name="moba_attn",
    )(pqk, pqk, vt, km, bias)


def _t5_bucket(dist):
    max_exact = REL_BUCKETS // 2
    scaled = jnp.log(jnp.maximum(dist, 1).astype(F32) / max_exact) / math.log(REL_MAX_DIST / max_exact)
    large = jnp.minimum(max_exact + (scaled * (REL_BUCKETS - max_exact)).astype(I32), REL_BUCKETS - 1)
    return jnp.where(dist < max_exact, dist, large)


def moba_bias_tiles(rel_bias):
    blk = MB_BLOCK
    span = 2 * blk - 1
    x = jnp.arange(span) - (blk - 1)
    dist = jnp.maximum(jnp.arange(MB_BIAS_TILES)[:, None] * blk + x[None, :], 0)
    w = rel_bias.astype(F32).T[:, _t5_bucket(dist)]
    h = w.shape[0]
    wp = jnp.pad(w, ((0, 0), (0, 0), (0, 1)))
    a = jnp.broadcast_to(wp[:, :, None, :], (h, MB_BIAS_TILES, blk, span + 1))
    a = a.reshape(h, MB_BIAS_TILES, blk * (span + 1))[:, :, :blk * span]
    return a.reshape(h, MB_BIAS_TILES, blk, span)[:, :, :, blk - 1:]


def _mix_kernel(x_ref, ya_ref, yb_ref, ga_ref, gb_ref, wa_ref, wb_ref, wo_ref, o_ref):
    za = jnp.dot(ya_ref[...], wa_ref[...], preferred_element_type=F32)
    zb = jnp.dot(yb_ref[...], wb_ref[...], preferred_element_type=F32)
    z = jax.nn.sigmoid(ga_ref[...].astype(F32)) * za + jax.nn.sigmoid(gb_ref[...].astype(F32)) * zb
    o_ref[...] = x_ref[...] + jnp.dot(z.astype(BF16), wo_ref[...], preferred_element_type=F32)


def mix_out(x2d, ya, yb, pg, wa, wb, wo, tok0=0, tm=512):
    t = yb.shape[0]
    assert t % tm == 0 and tok0 % tm == 0
    d = x2d.shape[1]
    w = ya.shape[1]
    b0 = tok0 // tm
    return pl.pallas_call(
        _mix_kernel,
        grid=(t // tm,),
        in_specs=[
            pl.BlockSpec((tm, d), lambda i: (b0 + i, 0)),
            pl.BlockSpec((tm, w), lambda i: (b0 + i, 0)),
            pl.BlockSpec((tm, w), lambda i: (i, 0)),
            pl.BlockSpec((tm, d), lambda i: (b0 + i, 0)),
            pl.BlockSpec((tm, d), lambda i: (b0 + i, 1)),
            pl.BlockSpec((w, d), lambda i: (0, 0)),
            pl.BlockSpec((w, d), lambda i: (0, 0)),
            pl.BlockSpec((d, d), lambda i: (0, 0)),
        ],
        out_specs=pl.BlockSpec((tm, d), lambda i: (i, 0)),
        out_shape=jax.ShapeDtypeStruct((t, d), F32),
        compiler_params=_cparams(("parallel",)),
        name="mix_out",
    )(x2d, ya, yb, pg, pg, wa, wb, wo)


def _mem_kv_kernel(m_ref, g_ref, wk_ref, wv_ref, k_ref, v_ref):
    mn = _rms(m_ref[...], g_ref[...]).astype(BF16)
    k_ref[...] = jnp.dot(mn, wk_ref[...], preferred_element_type=F32).astype(BF16)
    v_ref[...] = jnp.dot(mn, wv_ref[...], preferred_element_type=F32).astype(BF16)


def mem_kv(mem, g, wk, wv):
    b, m, d = mem.shape
    spec = pl.BlockSpec((None, m, d), lambda i: (i, 0, 0))
    wspec = pl.BlockSpec((d, d), lambda i: (0, 0))
    return pl.pallas_call(
        _mem_kv_kernel,
        grid=(b,),
        in_specs=[spec, pl.BlockSpec((1, d), lambda i: (0, 0)), wspec, wspec],
        out_specs=[spec, spec],
        out_shape=[jax.ShapeDtypeStruct((b, m, d), BF16)] * 2,
        compiler_params=_cparams(("parallel",)),
        name="mem_kv",
    )(mem, g, wk, wv)


def _cross_kernel(x_ref, g_ref, wq_ref, k_ref, v_ref, wo_ref, o_ref):
    x = x_ref[...]
    d = x.shape[1]
    dh = d // X_HEADS
    h = _rms(x, g_ref[...]).astype(BF16)
    q = (jnp.dot(h, wq_ref[...], preferred_element_type=F32) * (dh ** -0.5)).astype(BF16)
    outs = []
    for hh in range(X_HEADS):
        sl = slice(hh * dh, (hh + 1) * dh)
        s = lax.dot_general(q[:, sl], k_ref[:, sl], (((1,), (1,)), ((), ())),
                            preferred_element_type=F32)
        p = jnp.exp(s - jnp.max(s, axis=1, keepdims=True))
        l = jnp.sum(p, axis=1, keepdims=True)
        o = jnp.dot(p.astype(BF16), v_ref[:, sl], preferred_element_type=F32) / l
        outs.append(o.astype(BF16))
    o = jnp.concatenate(outs, axis=1)
    o_ref[...] = x + jnp.dot(o, wo_ref[...], preferred_element_type=F32)


def cross_attn(x2d, g, wq, kx, vx, wo, seq, tm=512):
    t, d = x2d.shape
    assert seq % tm == 0 and t % seq == 0
    m = kx.shape[1]
    per_b = seq // tm
    kv = pl.BlockSpec((None, m, d), lambda i: (i // per_b, 0, 0))
    wspec = pl.BlockSpec((d, d), lambda i: (0, 0))
    return pl.pallas_call(
        _cross_kernel,
        grid=(t // tm,),
        in_specs=[pl.BlockSpec((tm, d), lambda i: (i, 0)), pl.BlockSpec((1, d), lambda i: (0, 0)),
                  wspec, kv, kv, wspec],
        out_specs=pl.BlockSpec((tm, d), lambda i: (i, 0)),
        out_shape=jax.ShapeDtypeStruct((t, d), F32),
        compiler_params=_cparams(("parallel",)),
        name="cross_attn",
    )(x2d, g, wq, kx, vx, wo)


def _topk_rows(sc, k):
    n = sc.shape[0]
    io = lax.broadcasted_iota(I32, sc.shape, 0)
    vals, ids = [], []
    for _ in range(k):
        m = jnp.max(sc, axis=0, keepdims=True)
        ix = jnp.min(jnp.where(sc == m, io, n), axis=0, keepdims=True)
        vals.append(m)
        ids.append(ix)
        sc = jnp.where(io == ix, NEG_INF, sc)
    return jnp.concatenate(vals, axis=0), jnp.concatenate(ids, axis=0)


def _pack_bf16_halves(h):
    bits = lax.bitcast_convert_type(h, I32)
    r = bits + 0x7FFF + (lax.shift_right_logical(bits, 16) & 1)
    half = h.shape[1] // 2
    return lax.shift_right_logical(r[:, :half], 16) | (r[:, half:] & HI_MASK)


def _route_kernel(x_ref, g_ref, wq_ref, sk_ref, hp_ref, idx_ref, w_ref, hb_ref, it_ref, wt_ref):
    p = pl.program_id(1)

    @pl.when(p == 0)
    def _():
        h = _rms(x_ref[...], g_ref[...])
        hp_ref[...] = _pack_bf16_halves(h)
        hb_ref[...] = h.astype(BF16)

    qh = jnp.dot(hb_ref[...], wq_ref[...], preferred_element_type=F32)
    tops = []
    for c in range(2):
        seg = qh[:, c * PEER_HALF:(c + 1) * PEER_HALF]
        sc = lax.dot_general(sk_ref[c], seg, (((1,), (1,)), ((), ())),
                             precision=lax.Precision.HIGHEST, preferred_element_type=F32)
        tops.append(_topk_rows(sc, PEER_TOPK))
    (s0, i0), (s1, i1) = tops
    k = PEER_TOPK
    sub = 8
    tm = s0.shape[1]
    r8 = lax.broadcasted_iota(I32, (sub, tm), 0)
    r16 = lax.broadcasted_iota(I32, (k, tm), 0)
    cand_b = [s0[0:1] + s1, s0[1:2] + s1[:sub]]
    cidx_b = [i0[0:1] * PEER_NKEYS + i1, i0[1:2] * PEER_NKEYS + i1[:sub]]
    pos_b = [r16, k + r8]
    for a in range(2, sub):
        keep = r8 < (k // (a + 1))
        cand_b.append(jnp.where(keep, s0[a:a + 1] + s1[:sub], NEG_INF))
        cidx_b.append(i0[a:a + 1] * PEER_NKEYS + i1[:sub])
        pos_b.append(a * k + r8)
    cand_b.append(s0[sub:] + s1[0:1])
    cidx_b.append(i0[sub:] * PEER_NKEYS + i1[0:1])
    pos_b.append((sub + r8) * k)
    cand = jnp.concatenate(cand_b, axis=0)
    cidx = jnp.concatenate(cidx_b, axis=0)
    pos = jnp.concatenate(pos_b, axis=0)
    vals, ids = [], []
    for _ in range(k):
        m = jnp.max(cand, axis=0, keepdims=True)
        px = jnp.min(jnp.where(cand == m, pos, k * k), axis=0, keepdims=True)
        hit = pos == px
        vals.append(m)
        ids.append(jnp.sum(jnp.where(hit, cidx, 0), axis=0, keepdims=True))
        cand = jnp.where(hit, NEG_INF, cand)
    sf = jnp.concatenate(vals, axis=0)
    e = jnp.exp(sf - sf[0:1])
    rows = pl.ds(pl.multiple_of(p * PEER_TOPK, PEER_TOPK), PEER_TOPK)
    wt_ref[rows, :] = e / jnp.sum(e, axis=0, keepdims=True)
    it_ref[rows, :] = jnp.concatenate(ids, axis=0)

    @pl.when(p == pl.num_programs(1) - 1)
    def _():
        idx_ref[...] = it_ref[...].T
        w_ref[...] = wt_ref[...].T


def peer_route(x2d, g, wq, sk, tok0, t, tm=1024):
    assert t % tm == 0 and tok0 % tm == 0
    d = x2d.shape[1]
    ph = sk.shape[0]
    nsel = ph * PEER_TOPK
    blk0 = tok0 // tm
    return pl.pallas_call(
        _route_kernel,
        grid=(t // tm, ph),
        in_specs=[
            pl.BlockSpec((tm, d), lambda i, p: (blk0 + i, 0)),
            pl.BlockSpec((1, d), lambda i, p: (0, 0)),
            pl.BlockSpec((d, 2 * PEER_HALF), lambda i, p: (0, p)),
            pl.BlockSpec((None, 2, PEER_NKEYS, PEER_HALF), lambda i, p: (p, 0, 0, 0)),
        ],
        out_specs=[
            pl.BlockSpec((tm, d // 2), lambda i, p: (i, 0)),
            pl.BlockSpec((tm, nsel), lambda i, p: (i, 0)),
            pl.BlockSpec((tm, nsel), lambda i, p: (i, 0)),
        ],
        out_shape=[jax.ShapeDtypeStruct((t, d // 2), I32),
                   jax.ShapeDtypeStruct((t, nsel), I32),
                   jax.ShapeDtypeStruct((t, nsel), F32)],
        scratch_shapes=[pltpu.VMEM((tm, d), BF16),
                        pltpu.VMEM((nsel, tm), I32),
                        pltpu.VMEM((nsel, tm), F32)],
        compiler_params=_cparams(("parallel", "arbitrary")),
        name="peer_route",
    )(x2d, g, wq, sk)


def _final_kernel(x_ref, y_ref, g_ref, o_ref):
    o_ref[...] = _rms(x_ref[...] + y_ref[...], g_ref[...])


def final_norm(x2d, y, g, tok0, tm=512):
    t, d = y.shape
    assert t % tm == 0 and tok0 % tm == 0
    blk0 = tok0 // tm
    spec = pl.BlockSpec((tm, d), lambda i: (i, 0))
    return pl.pallas_call(
        _final_kernel, grid=(t // tm,),
        in_specs=[pl.BlockSpec((tm, d), lambda i: (blk0 + i, 0)), spec, pl.BlockSpec((1, d), lambda i: (0, 0))],
        out_specs=spec,
        out_shape=jax.ShapeDtypeStruct((t, d), F32),
        compiler_params=_cparams(("parallel",)), name="final_norm",
    )(x2d, y, g)


SC_CORES = 2
SC_SUBCORES = 16
SC_WORKERS = SC_CORES * SC_SUBCORES
SC_LANES = 16
SC_GROUP = 16


def _sc_mesh():
    return plsc.VectorSubcoreMesh(core_axis_name="c", subcore_axis_name="s")


def _sc_params():
    return pltpu.CompilerParams(needs_layout_passes=False)


def _sc_worker_id():
    return lax.axis_index("s") * SC_CORES + lax.axis_index("c")


SC_ROW_LANE = 128


def _sc_unit_off(u):
    off = u * SC_LANES
    return off if isinstance(off, int) else pl.multiple_of(off, SC_LANES)


GELU_C0 = math.sqrt(2.0 / math.pi)
GELU_C1 = 0.044715


def _gelu_tanh(x):
    z = GELU_C0 * (x + GELU_C1 * (x * x * x))
    th = 1.0 - 2.0 / (jnp.exp(2.0 * z) + 1.0)
    return 0.5 * x * (1.0 + th)


SC_PK_RING = 4
SC_PK_SUB = 4
HI_MASK = -65536


def _pack_tables_kernel(u_ref, v_ref, o_ref):
    for part, ref in enumerate((u_ref, v_ref)):
        words = _pack_bf16_halves(ref[...])
        for sub in range(SC_PK_SUB):
            o_ref[:, part * SC_PK_SUB + sub, :] = words[:, sub * SC_ROW_LANE:(sub + 1) * SC_ROW_LANE]


def pack_expert_tables(u, v, te=512):
    e, d = u.shape
    assert d == 2 * SC_PK_SUB * SC_ROW_LANE
    spec = pl.BlockSpec((te, d), lambda i: (i, 0))
    return pl.pallas_call(
        _pack_tables_kernel, grid=(e // te,), in_specs=[spec, spec],
        out_specs=pl.BlockSpec((te, 2 * SC_PK_SUB, SC_ROW_LANE), lambda i: (i, 0, 0)),
        out_shape=jax.ShapeDtypeStruct((e, 2 * SC_PK_SUB, SC_ROW_LANE), I32),
        compiler_params=_cparams(("parallel",)), name="pack_expert_tables",
    )(u, v)


def _unpack_halves(x32):
    w = plsc.bitcast(x32, I32)
    return plsc.bitcast(w << 16, F32), plsc.bitcast(w & HI_MASK, F32)


def _tree_sum(xs):
    while len(xs) > 1:
        xs = [xs[i] + xs[i + 1] for i in range(0, len(xs), 2)]
    return xs[0]


def peer_experts_pk_sc(tab_uv, idx_flat, w_flat, hp, d):
    t = hp.shape[0]
    nsel = PEER_SEL
    g = SC_GROUP
    assert t % (SC_WORKERS * g) == 0 and d == 2 * SC_PK_SUB * SC_ROW_LANE
    tpw = t // SC_WORKERS
    groups = tpw // g
    heads = nsel // SC_LANES
    chunks = d // 32
    units = g * heads
    ring = SC_PK_RING
    assert units % ring == 0
    row_buf = pltpu.VMEM((SC_LANES, 2 * SC_PK_SUB, SC_ROW_LANE), I32)

    def row_words(rows, r, wc, sub0):
        per = SC_ROW_LANE // SC_LANES
        return plsc.bitcast(
            rows[r, sub0 + wc // per, pl.ds(pl.multiple_of((wc % per) * SC_LANES, SC_LANES), SC_LANES)], BF16)

    def ring_loop(n_units, start, wait, compute):
        for u in range(ring - 1):
            start(u, u)

        @pl.loop(0, n_units, step=ring)
        def _(uu):
            for b in range(ring):
                u = uu + b
                nxt = u + (ring - 1)

                @pl.when(nxt < n_units)
                def _():
                    start(nxt, (b + ring - 1) % ring)

                wait(u, b)
                compute(u, b)

    @functools.partial(
        pl.kernel, mesh=_sc_mesh(),
        out_type=jax.ShapeDtypeStruct((t, d), F32),
        scratch_types=[
            pltpu.VMEM((g * nsel,), I32),
            pltpu.VMEM((g * nsel,), F32),
            pltpu.VMEM((g, d // 2), I32),
            pltpu.VMEM((g, d), F32),
            pltpu.VMEM((SC_LANES * SC_LANES,), F32),
            [row_buf] * ring,
            [pltpu.SemaphoreType.DMA] * ring,
        ],
        compiler_params=_sc_params(),
        name="peer_experts_pk_sc",
    )
    def k(tab_hbm, idx_hbm, w_hbm, h_hbm, out_hbm, idx_v, coef_v, h_v, y_v, red_v, rows, sems):
        wid = _sc_worker_id()
        lane = lax.iota(I32, SC_LANES)

        def copy(u, slot):
            ids = idx_v.at[pl.ds(_sc_unit_off(u), SC_LANES)]
            return pltpu.make_async_copy(tab_hbm.at[ids], rows[slot], sems[slot])

        def dots(u, slot):
            tt = u // heads

            def body(cp, accs):
                out = []
                hv = [plsc.bitcast(h_v[tt, pl.ds(pl.multiple_of((2 * cp + i) * SC_LANES, SC_LANES), SC_LANES)], BF16)
                      for i in range(2)]
                for r in range(SC_LANES):
                    pr = (row_words(rows[slot], r, 2 * cp, 0) * hv[0]
                          + row_words(rows[slot], r, 2 * cp + 1, 0) * hv[1])
                    lo, hi = _unpack_halves(pr)
                    out.append(accs[r] + lo + hi)
                return tuple(out)

            accs = lax.fori_loop(0, chunks // 2, body,
                                 tuple(jnp.zeros((SC_LANES,), F32) for _ in range(SC_LANES)))
            for r in range(SC_LANES):
                red_v[pl.ds(r * SC_LANES, SC_LANES)] = accs[r]
            act = _tree_sum([plsc.load_gather(red_v, [lane * SC_LANES + j]) for j in range(SC_LANES)])
            sl = pl.ds(_sc_unit_off(u), SC_LANES)
            coef_v[sl] = coef_v[sl] * _gelu_tanh(act)

        def combine(u, slot):
            tt = u // heads
            first = (u % heads) == 0
            cb = []
            for r in range(SC_LANES):
                c = plsc.load_gather(coef_v, [jnp.full((SC_LANES,), u * SC_LANES + r, I32)])
                cb.append(plsc.pack(c, c, format=plsc.PackFormat.INTERLEAVED))

            @plsc.parallel_loop(0, chunks, unroll=2)
            def _(wc):
                lo, hi = _unpack_halves(
                    _tree_sum([cb[r] * row_words(rows[slot], r, wc, SC_PK_SUB) for r in range(SC_LANES)]))
                for half, val in ((0, lo), (1, hi)):
                    sl = pl.ds(pl.multiple_of(half * (d // 2) + wc * SC_LANES, SC_LANES), SC_LANES)
                    y_v[tt, sl] = val + jnp.where(first, 0.0, y_v[tt, sl])

        def unit(u, slot):
            dots(u, slot)
            combine(u, slot)

        @pl.loop(0, groups)
        def _(gi):
            base = wid * tpw + gi * g
            pltpu.sync_copy(idx_hbm.at[pl.ds(base * nsel, g * nsel)], idx_v)
            pltpu.sync_copy(w_hbm.at[pl.ds(base * nsel, g * nsel)], coef_v)
            pltpu.sync_copy(h_hbm.at[pl.ds(base, g)], h_v)
            ring_loop(units, lambda u, s: copy(u, s).start(), lambda u, s: copy(u, s).wait(), unit)
            pltpu.sync_copy(y_v, out_hbm.at[pl.ds(base, g)])

    return k(tab_uv, idx_flat, w_flat, hp)


def kernel(x, mem, rel_bias, ln_mix, w_in, hg_lower, hg_norm, w_up_a, w_up_b, w_out, ln_cross, ln_mem, wq_x, wk_x, wv_x, wo_x, ln_ffn, peer_query, peer_subkeys, peer_u, peer_v, ln_final):
    b, s, d = x.shape
    depth = w_in.shape[0]
    assert depth == 1, "the residual after PEER is fused into the final norm"
    assert s % MB_BLOCK == 0 and s % HG_CHUNK == 0 and s % (PEER_SLICES * SC_WORKERS * SC_GROUP) == 0
    nb = s // MB_BLOCK
    row = lambda a: a.reshape(1, -1).astype(F32)
    lb_all = jnp.cumsum(jax.nn.softmax(hg_lower.astype(F32), axis=0), axis=0)
    bias = moba_bias_tiles(rel_bias)
    n_hg = 4 * HG_WIDTH
    n_qk = 2 * MB_WIDTH
    n_mb = 3 * MB_WIDTH
    l = 0
    w = w_in[l].astype(BF16)
    w_hg, w_qk, w_vt, w_g = w[:, :n_hg], w[:, n_hg:n_hg + n_qk], w[:, n_hg + n_qk:n_hg + n_mb].T, w[:, n_hg + n_mb:]
    wa, wb, wo = w_up_a[l].astype(BF16), w_up_b[l].astype(BF16), w_out[l].astype(BF16)
    wqx, wox = wq_x[l].astype(BF16), wo_x[l].astype(BF16)
    wpq, sk = peer_query[l].astype(BF16), peer_subkeys[l].astype(F32)
    tab_uv = pack_expert_tables(peer_u[l].astype(F32), peer_v[l].astype(F32))
    kx, vx = mem_kv(mem, row(ln_mem[l]), wk_x[l].astype(BF16), wv_x[l].astype(BF16))

    outs = []
    for bi in range(b):
        x2d = x[bi]
        p0, pqk, vt, pg = in_proj(x2d, row(ln_mix[l]), w_hg, w_qk, w_vt, w_g)
        ya = hgrn2(p0, row(lb_all[l]), row(hg_norm[l]), 1, s)
        km = moba_kmean(pqk, 1, s).reshape(1, nb, MB_WIDTH)
        ts = s // PEER_SLICES
        for tok0 in range(0, s, ts):
            yb = moba_attention(pqk, vt, km, bias, 1, s, tok0 // MB_BLOCK, ts // MB_BLOCK)
            xs = mix_out(x2d, ya, yb, pg, wa, wb, wo, tok0)
            xs = cross_attn(xs, row(ln_cross[l]), wqx, kx[bi:bi + 1], vx[bi:bi + 1], wox, ts)
            hp, eidx, wts = peer_route(xs, row(ln_ffn[l]), wpq, sk, 0, ts)
            y = peer_experts_pk_sc(tab_uv, eidx.reshape(ts * PEER_SEL), wts.reshape(ts * PEER_SEL), hp, d)
            outs.append(final_norm(xs, y, row(ln_final), 0))
    return jnp.concatenate(outs, axis=0).reshape(b, s, d)
```

```python
import functools
import math

import jax
import jax.numpy as jnp
from jax import lax
from jax.experimental import pallas as pl
from jax.experimental.pallas import tpu as pltpu
from jax.experimental.pallas import tpu_sc as plsc

F32 = jnp.float32
BF16 = jnp.bfloat16
I32 = jnp.int32
EPS = 1e-6
NEG_INF = float("-inf")

HG_HEADS = 4
HG_D = 128
HG_WIDTH = HG_HEADS * HG_D
HG_CHUNK = 64
HG_SUB = 16
MB_HEADS = 8
MB_DH = 64
MB_WIDTH = MB_HEADS * MB_DH
MB_BLOCK = 256
MB_TOPK = 3
MB_BIAS_TILES = 8
REL_BUCKETS = 32
REL_MAX_DIST = 2048
X_HEADS = 4
PEER_HEADS = 8
PEER_NKEYS = 128
PEER_TOPK = 16
PEER_HALF = 128
PEER_SEL = PEER_HEADS * PEER_TOPK
PEER_SLICES = 4

VMEM_LIMIT = 56 * 1024 * 1024


def _cparams(sem):
    return pltpu.CompilerParams(dimension_semantics=sem, vmem_limit_bytes=VMEM_LIMIT)


def _rms(x, g):
    ms = jnp.mean(x * x, axis=-1, keepdims=True)
    return x * lax.rsqrt(ms + EPS) * g


def _in_proj_kernel(x_ref, g_ref, w0_ref, w1_ref, wvt_ref, w2_ref, o0_ref, o1_ref, ovt_ref, o2_ref):
    h = _rms(x_ref[...], g_ref[...]).astype(BF16)
    o0_ref[...] = jnp.dot(h, w0_ref[...], preferred_element_type=F32)
    o1_ref[...] = jnp.dot(h, w1_ref[...], preferred_element_type=F32).astype(BF16)
    vt = lax.dot_general(wvt_ref[...], h, (((1,), (1,)), ((), ())), preferred_element_type=F32).astype(BF16)
    for hd in range(MB_HEADS):
        ovt_ref[0, hd * MB_VROWS:hd * MB_VROWS + MB_DH, :] = vt[hd * MB_DH:(hd + 1) * MB_DH]
        ovt_ref[0, hd * MB_VROWS + MB_DH:(hd + 1) * MB_VROWS, :] = jnp.ones((MB_ONES, vt.shape[1]), BF16)
    o2_ref[...] = jnp.dot(h, w2_ref[...], preferred_element_type=F32).astype(BF16)


def in_proj(x2d, g, w0, w1, wvt, w2):
    t, d = x2d.shape
    tm = MB_BLOCK
    assert wvt.shape[0] == MB_WIDTH
    n0, n1, nv, n2 = w0.shape[1], w1.shape[1], MB_VT_ROWS, w2.shape[1]
    full = lambda a: pl.BlockSpec(a.shape, lambda i: (0, 0))
    return pl.pallas_call(
        _in_proj_kernel,
        grid=(t // tm,),
        in_specs=[pl.BlockSpec((tm, d), lambda i: (i, 0)), full(g), full(w0), full(w1), full(wvt), full(w2)],
        out_specs=[pl.BlockSpec((tm, n0), lambda i: (i, 0)),
                   pl.BlockSpec((tm, n1), lambda i: (i, 0)),
                   pl.BlockSpec((1, nv, tm), lambda i: (i, 0, 0)),
                   pl.BlockSpec((tm, n2), lambda i: (i, 0))],
        out_shape=[jax.ShapeDtypeStruct((t, n0), F32),
                   jax.ShapeDtypeStruct((t, n1), BF16),
                   jax.ShapeDtypeStruct((t // tm, nv, tm), BF16),
                   jax.ShapeDtypeStruct((t, n2), BF16)],
        compiler_params=_cparams(("parallel",)),
        name="in_proj",
    )(x2d, g, w0, w1, wvt, w2)


def _hgrn_kernel(q_ref, f_ref, i_ref, g_ref, lb_ref, gain_ref, o_ref, st_ref):
    c = pl.program_id(1)

    @pl.when(c == 0)
    def _():
        st_ref[...] = jnp.zeros_like(st_ref)

    C, S = HG_CHUNK, HG_SUB
    row = lax.broadcasted_iota(I32, (C, C), 0)
    col = lax.broadcasted_iota(I32, (C, C), 1)
    tril = (row >= col).astype(F32)
    t_iota = lax.broadcasted_iota(I32, (S, 1), 0)

    for h in range(HG_HEADS):
        sl = slice(h * HG_D, (h + 1) * HG_D)
        q = q_ref[:, sl]
        v = i_ref[:, sl]
        lb = lb_ref[:, sl]
        f = lb + (1.0 - lb) * jax.nn.sigmoid(f_ref[:, sl])
        lf = jnp.log(f)
        k = 1.0 - f
        b = jnp.dot(tril, lf, precision=lax.Precision.HIGHEST, preferred_element_type=F32)
        st = st_ref[h]
        vb = v.astype(BF16)
        qd = (q * jnp.exp(b)).astype(BF16)
        o_inter = lax.dot_general(qd, st.astype(BF16), (((1,), (1,)), ((), ())),
                                  preferred_element_type=F32)
        outs = []
        for i in range(C // S):
            r0 = i * S
            qi = q[r0:r0 + S]
            ki = k[r0:r0 + S]
            bi = b[r0:r0 + S]
            vi = v[r0:r0 + S]
            oi = o_inter[r0:r0 + S]
            if i > 0:
                bs = b[r0 - 1:r0]
                qh = (qi * jnp.exp(bi - bs)).astype(BF16)
                kh = (k[:r0] * jnp.exp(bs - b[:r0])).astype(BF16)
                a = lax.dot_general(qh, kh, (((1,), (1,)), ((), ())), preferred_element_type=F32)
                oi = oi + jnp.dot(a.astype(BF16), vb[:r0], preferred_element_type=F32)
            half = S // 2
            o_half = [oi[:half], oi[half:]]
            for s in range(S):
                for hf in range(s // half, 2):
                    rows = slice(hf * half, (hf + 1) * half)
                    dec = jnp.exp(jnp.minimum(bi[rows] - bi[s:s + 1], 0.0))
                    a_s = jnp.sum(qi[rows] * ki[s:s + 1] * dec, axis=-1, keepdims=True)
                    a_s = jnp.where(t_iota[rows] >= s, a_s, 0.0)
                    o_half[hf] = o_half[hf] + a_s * vi[s:s + 1]
            outs.extend(o_half)
        o = jnp.concatenate(outs, axis=0)
        b_end = b[C - 1:C]
        kd = (k * jnp.exp(b_end - b)).astype(BF16)
        upd = lax.dot_general(vb, kd, (((0,), (0,)), ((), ())), preferred_element_type=F32)
        st_ref[h] = st * jnp.exp(b_end) + upd
        o = o * lax.rsqrt(jnp.mean(o * o, axis=-1, keepdims=True) + EPS)
        g = g_ref[:, sl]
        o_ref[:, sl] = (o * gain_ref[:, sl] * (g * jax.nn.sigmoid(g))).astype(o_ref.dtype)


def hgrn2(p0, lb, gain, batch, seq):
    t = p0.shape[0]
    nc = seq // HG_CHUNK
    w = HG_WIDTH

    def col(j):
        return pl.BlockSpec((HG_CHUNK, w), lambda b, c, j=j: (b * nc + c, j))

    return pl.pallas_call(
        _hgrn_kernel,
        grid=(batch, nc),
        in_specs=[col(0), col(1), col(2), col(3),
                  pl.BlockSpec((1, w), lambda b, c: (0, 0)),
                  pl.BlockSpec((1, w), lambda b, c: (0, 0))],
        out_specs=pl.BlockSpec((HG_CHUNK, w), lambda b, c: (b * nc + c, 0)),
        out_shape=jax.ShapeDtypeStruct((t, w), BF16),
        scratch_shapes=[pltpu.VMEM((HG_HEADS, HG_D, HG_D), F32)],
        compiler_params=_cparams(("parallel", "arbitrary")),
        name="hgrn2",
    )(p0, p0, p0, p0, lb, gain)


def _kmean_kernel(k_ref, o_ref):
    o_ref[0] = jnp.mean(k_ref[...].astype(F32), axis=0, keepdims=True)


def moba_kmean(p1, batch, seq):
    nbt = p1.shape[0] // MB_BLOCK
    return pl.pallas_call(
        _kmean_kernel,
        grid=(nbt,),
        in_specs=[pl.BlockSpec((MB_BLOCK, MB_WIDTH), lambda i: (i, 1))],
        out_specs=pl.BlockSpec((1, 1, MB_WIDTH), lambda i: (i, 0, 0)),
        out_shape=jax.ShapeDtypeStruct((nbt, 1, MB_WIDTH), F32),
        compiler_params=_cparams(("parallel",)),
        name="moba_kmean",
    )(p1)


MB_PAIR = 4
MB_PW = MB_PAIR * MB_DH
MB_LG = 128
MB_ONES = 16
MB_VROWS = MB_DH + MB_ONES
MB_VT_ROWS = MB_HEADS * MB_VROWS


def _moba_kernel(q_ref, k_ref, vt_ref, km_ref, bias_ref, o_ref, *scratch, qb0):
    m_ref, l_ref, al_ref, acc_ref, msk_ref, s_ref, p_ref = (
        scratch[i * MB_PAIR:(i + 1) * MB_PAIR] for i in range(7))
    qi = pl.program_id(2) + qb0
    nb = km_ref.shape[0]
    blk = MB_BLOCK
    heads = range(MB_PAIR)
    grp = lambda hh: slice((hh // 2) * MB_LG, (hh // 2 + 1) * MB_LG)
    q = q_ref[...]
    lane = lax.broadcasted_iota(I32, (blk, MB_LG), 1)
    in_head = [(lane < MB_DH) if hh % 2 == 0 else (lane >= MB_DH) for hh in heads]
    qs = q * jnp.asarray(MB_DH ** -0.5, BF16)
    nt = (((1,), (1,)), ((), ()))
    qf = q.astype(F32)
    qht = [jnp.where(in_head[hh], qs[:, grp(hh)].astype(F32), 0.0).T.astype(BF16) for hh in heads]

    n_io = lax.broadcasted_iota(I32, (nb, blk), 0)
    for hh in heads:
        gate = lax.dot_general(km_ref[:, grp(hh)], jnp.where(in_head[hh], qf[:, grp(hh)], 0.0), nt,
                               precision=lax.Precision.HIGHEST, preferred_element_type=F32)
        gate = jnp.where(n_io < qi, gate, NEG_INF)
        chosen = n_io < 0
        for _ in range(MB_TOPK):
            mx = jnp.max(gate, axis=0, keepdims=True)
            ix = jnp.min(jnp.where(gate == mx, n_io, nb), axis=0, keepdims=True)
            hit = n_io == ix
            chosen = chosen | (hit & (mx > NEG_INF))
            gate = jnp.where(hit, NEG_INF, gate)
        msk_ref[hh][...] = jnp.where(chosen, 0.0, NEG_INF)

    vrows = lambda hh: slice(hh * MB_VROWS, (hh + 1) * MB_VROWS)

    def pv_stage(blk_idx):
        vtb = vt_ref[blk_idx]
        r = [jnp.dot(vtb[vrows(hh)], p_ref[hh][...], preferred_element_type=F32) for hh in heads]
        al = [al_ref[hh][...] for hh in heads]
        a_new = [al[hh] * acc_ref[hh][...] + r[hh][:MB_DH] for hh in heads]
        l_new = [al[hh] * l_ref[hh][...] + r[hh][MB_DH:MB_DH + 1] for hh in heads]
        return a_new, l_new

    def store_pv(a_new, l_new):
        for hh in heads:
            acc_ref[hh][...] = a_new[hh]
            l_ref[hh][...] = l_new[hh]

    def softmax_stage():
        s = [s_ref[hh][...] for hh in heads]
        m_old = [m_ref[hh][...] for hh in heads]
        m_new = [jnp.maximum(m_old[hh], jnp.max(s[hh], axis=0, keepdims=True)) for hh in heads]
        alpha = [jnp.exp(m_old[hh] - m_new[hh]) for hh in heads]
        p = [jnp.exp((s[hh] - m_new[hh]).astype(BF16)) for hh in heads]
        return p, alpha, m_new

    def store_softmax(p, alpha, m_new):
        for hh in heads:
            p_ref[hh][...] = p[hh]
            al_ref[hh][...] = alpha[hh]
            m_ref[hh][...] = m_new[hh]

    k_own = k_ref[pl.ds(pl.multiple_of(qi * blk, blk), blk), :]
    key_io = lax.broadcasted_iota(I32, (blk, blk), 0)
    qry_io = lax.broadcasted_iota(I32, (blk, blk), 1)
    for hh in heads:
        s = jnp.dot(k_own[:, grp(hh)], qht[hh], preferred_element_type=F32) + bias_ref[hh, 0]
        s_ref[hh][...] = jnp.where(key_io <= qry_io, s, NEG_INF)
        m_ref[hh][...] = jnp.full((1, blk), NEG_INF, F32)
        l_ref[hh][...] = jnp.zeros((1, blk), F32)
        al_ref[hh][...] = jnp.ones((1, blk), F32)
        acc_ref[hh][...] = jnp.zeros((MB_DH, blk), F32)
        p_ref[hh][...] = jnp.zeros((blk, blk), BF16)

    def step(i, carry, far):
        pv = pv_stage(jnp.where(i <= 1, qi, i - 2))
        sm = softmax_stage()
        kn = k_ref[pl.ds(pl.multiple_of(i * blk, blk), blk), :]
        if far:
            row = [msk_ref[hh][pl.ds(i, 1), :] + bias_ref[hh, MB_BIAS_TILES - 1, 0:1, 0:1] for hh in heads]
            s_next = [jnp.dot(kn[:, grp(hh)], qht[hh], preferred_element_type=F32) + row[hh] for hh in heads]
        else:
            d = qi - i
            s_next = [jnp.dot(kn[:, grp(hh)], qht[hh], preferred_element_type=F32)
                      + bias_ref[hh, d] + msk_ref[hh][pl.ds(i, 1), :] for hh in heads]
        store_pv(*pv)
        for hh in heads:
            s_ref[hh][...] = s_next[hh]
        store_softmax(*sm)
        return carry

    n_far = jnp.maximum(qi - (MB_BIAS_TILES - 2), 0)
    lax.fori_loop(0, n_far, functools.partial(step, far=True), 0)
    lax.fori_loop(n_far, qi, functools.partial(step, far=False), 0)
    pv = pv_stage(jnp.where(qi <= 1, qi, qi - 2))
    sm = softmax_stage()
    store_pv(*pv)
    store_softmax(*sm)
    a_fin, l_fin = pv_stage(jnp.where(qi == 0, qi, qi - 1))
    out_t = jnp.concatenate([a_fin[hh] / l_fin[hh] for hh in heads], axis=0)
    o_ref[...] = out_t.T.astype(o_ref.dtype)


def moba_attention(pqk, vt, km, bias, batch, seq, qb0=0, nqb=None):
    nb = seq // MB_BLOCK
    nqb = nb if nqb is None else nqb
    t = batch * nqb * MB_BLOCK
    groups = MB_WIDTH // MB_PW
    return pl.pallas_call(
        functools.partial(_moba_kernel, qb0=qb0),
        grid=(batch, groups, nqb),
        in_specs=[
            pl.BlockSpec((MB_BLOCK, MB_PW), lambda b, j, i: (b * nb + qb0 + i, j)),
            pl.BlockSpec((seq, MB_PW), lambda b, j, i: (b, groups + j)),
            pl.BlockSpec((nb, MB_PAIR * MB_VROWS, MB_BLOCK), lambda b, j, i: (b, j, 0)),
            pl.BlockSpec((None, nb, MB_PW), lambda b, j, i: (b, 0, j)),
            pl.BlockSpec((MB_PAIR, MB_BIAS_TILES, MB_BLOCK, MB_BLOCK), lambda b, j, i: (j, 0, 0, 0)),
        ],
        out_specs=pl.BlockSpec((MB_BLOCK, MB_PW), lambda b, j, i: (b * nqb + i, j)),
        out_shape=jax.ShapeDtypeStruct((t, MB_WIDTH), BF16),
        scratch_shapes=(
            [pltpu.VMEM((1, MB_BLOCK), F32)] * (3 * MB_PAIR)
            + [pltpu.VMEM((MB_DH, MB_BLOCK), F32)] * MB_PAIR
            + [pltpu.VMEM((nb, MB_BLOCK), F32)] * MB_PAIR
            + [pltpu.VMEM((MB_BLOCK, MB_BLOCK), F32)] * MB_PAIR
            + [pltpu.VMEM((MB_BLOCK, MB_BLOCK), BF16)] * MB_PAIR
        ),
        compiler_params=_cparams(("parallel", "parallel", "arbitrary")),
        name="moba_attn",
    )(pqk, pqk, vt, km, bias)


def _t5_bucket(dist):
    max_exact = REL_BUCKETS // 2
    scaled = jnp.log(jnp.maximum(dist, 1).astype(F32) / max_exact) / math.log(REL_MAX_DIST / max_exact)
    large = jnp.minimum(max_exact + (scaled * (REL_BUCKETS - max_exact)).astype(I32), REL_BUCKETS - 1)
    return jnp.where(dist < max_exact, dist, large)


def moba_bias_tiles(rel_bias):
    blk = MB_BLOCK
    span = 2 * blk - 1
    x = jnp.arange(span) - (blk - 1)
    dist = jnp.maximum(jnp.arange(MB_BIAS_TILES)[:, None] * blk + x[None, :], 0)
    w = rel_bias.astype(F32).T[:, _t5_bucket(dist)]
    h = w.shape[0]
    wp = jnp.pad(w, ((0, 0), (0, 0), (0, 1)))
    a = jnp.broadcast_to(wp[:, :, None, :], (h, MB_BIAS_TILES, blk, span + 1))
    a = a.reshape(h, MB_BIAS_TILES, blk * (span + 1))[:, :, :blk * span]
    return a.reshape(h, MB_BIAS_TILES, blk, span)[:, :, :, blk - 1:]


def _mix_kernel(x_ref, ya_ref, yb_ref, ga_ref, gb_ref, wa_ref, wb_ref, wo_ref, o_ref):
    za = jnp.dot(ya_ref[...], wa_ref[...], preferred_element_type=F32)
    zb = jnp.dot(yb_ref[...], wb_ref[...], preferred_element_type=F32)
    z = jax.nn.sigmoid(ga_ref[...].astype(F32)) * za + jax.nn.sigmoid(gb_ref[...].astype(F32)) * zb
    o_ref[...] = x_ref[...] + jnp.dot(z.astype(BF16), wo_ref[...], preferred_element_type=F32)


def mix_out(x2d, ya, yb, pg, wa, wb, wo, tok0=0, tm=512):
    t = yb.shape[0]
    assert t % tm == 0 and tok0 % tm == 0
    d = x2d.shape[1]
    w = ya.shape[1]
    b0 = tok0 // tm
    return pl.pallas_call(
        _mix_kernel,
        grid=(t // tm,),
        in_specs=[
            pl.BlockSpec((tm, d), lambda i: (b0 + i, 0)),
            pl.BlockSpec((tm, w), lambda i: (b0 + i, 0)),
            pl.BlockSpec((tm, w), lambda i: (i, 0)),
            pl.BlockSpec((tm, d), lambda i: (b0 + i, 0)),
            pl.BlockSpec((tm, d), lambda i: (b0 + i, 1)),
            pl.BlockSpec((w, d), lambda i: (0, 0)),
            pl.BlockSpec((w, d), lambda i: (0, 0)),
            pl.BlockSpec((d, d), lambda i: (0, 0)),
        ],
        out_specs=pl.BlockSpec((tm, d), lambda i: (i, 0)),
        out_shape=jax.ShapeDtypeStruct((t, d), F32),
        compiler_params=_cparams(("parallel",)),
        name="mix_out",
    )(x2d, ya, yb, pg, pg, wa, wb, wo)


def _mem_kv_kernel(m_ref, g_ref, wk_ref, wv_ref, k_ref, v_ref):
    mn = _rms(m_ref[...], g_ref[...]).astype(BF16)
    k_ref[...] = jnp.dot(mn, wk_ref[...], preferred_element_type=F32).astype(BF16)
    v_ref[...] = jnp.dot(mn, wv_ref[...], preferred_element_type=F32).astype(BF16)


def mem_kv(mem, g, wk, wv):
    b, m, d = mem.shape
    spec = pl.BlockSpec((None, m, d), lambda i: (i, 0, 0))
    wspec = pl.BlockSpec((d, d), lambda i: (0, 0))
    return pl.pallas_call(
        _mem_kv_kernel,
        grid=(b,),
        in_specs=[spec, pl.BlockSpec((1, d), lambda i: (0, 0)), wspec, wspec],
        out_specs=[spec, spec],
        out_shape=[jax.ShapeDtypeStruct((b, m, d), BF16)] * 2,
        compiler_params=_cparams(("parallel",)),
        name="mem_kv",
    )(mem, g, wk, wv)


def _cross_kernel(x_ref, g_ref, wq_ref, k_ref, v_ref, wo_ref, o_ref):
    x = x_ref[...]
    d = x.shape[1]
    dh = d // X_HEADS
    h = _rms(x, g_ref[...]).astype(BF16)
    q = (jnp.dot(h, wq_ref[...], preferred_element_type=F32) * (dh ** -0.5)).astype(BF16)
    outs = []
    for hh in range(X_HEADS):
        sl = slice(hh * dh, (hh + 1) * dh)
        s = lax.dot_general(q[:, sl], k_ref[:, sl], (((1,), (1,)), ((), ())),
                            preferred_element_type=F32)
        p = jnp.exp(s - jnp.max(s, axis=1, keepdims=True))
        l = jnp.sum(p, axis=1, keepdims=True)
        o = jnp.dot(p.astype(BF16), v_ref[:, sl], preferred_element_type=F32) / l
        outs.append(o.astype(BF16))
    o = jnp.concatenate(outs, axis=1)
    o_ref[...] = x + jnp.dot(o, wo_ref[...], preferred_element_type=F32)


def cross_attn(x2d, g, wq, kx, vx, wo, seq, tm=512):
    t, d = x2d.shape
    assert seq % tm == 0 and t % seq == 0
    m = kx.shape[1]
    per_b = seq // tm
    kv = pl.BlockSpec((None, m, d), lambda i: (i // per_b, 0, 0))
    wspec = pl.BlockSpec((d, d), lambda i: (0, 0))
    return pl.pallas_call(
        _cross_kernel,
        grid=(t // tm,),
        in_specs=[pl.BlockSpec((tm, d), lambda i: (i, 0)), pl.BlockSpec((1, d), lambda i: (0, 0)),
                  wspec, kv, kv, wspec],
        out_specs=pl.BlockSpec((tm, d), lambda i: (i, 0)),
        out_shape=jax.ShapeDtypeStruct((t, d), F32),
        compiler_params=_cparams(("parallel",)),
        name="cross_attn",
    )(x2d, g, wq, kx, vx, wo)


def _topk_rows(sc, k):
    n = sc.shape[0]
    io = lax.broadcasted_iota(I32, sc.shape, 0)
    vals, ids = [], []
    for _ in range(k):
        m = jnp.max(sc, axis=0, keepdims=True)
        ix = jnp.min(jnp.where(sc == m, io, n), axis=0, keepdims=True)
        vals.append(m)
        ids.append(ix)
        sc = jnp.where(io == ix, NEG_INF, sc)
    return jnp.concatenate(vals, axis=0), jnp.concatenate(ids, axis=0)


def _pack_bf16_halves(h):
    bits = lax.bitcast_convert_type(h, I32)
    r = bits + 0x7FFF + (lax.shift_right_logical(bits, 16) & 1)
    half = h.shape[1] // 2
    return lax.shift_right_logical(r[:, :half], 16) | (r[:, half:] & HI_MASK)


def _route_kernel(x_ref, g_ref, wq_ref, sk_ref, hp_ref, idx_ref, w_ref, hb_ref, it_ref, wt_ref):
    p = pl.program_id(1)

    @pl.when(p == 0)
    def _():
        h = _rms(x_ref[...], g_ref[...])
        hp_ref[...] = _pack_bf16_halves(h)
        hb_ref[...] = h.astype(BF16)

    qh = jnp.dot(hb_ref[...], wq_ref[...], preferred_element_type=F32)
    tops = []
    for c in range(2):
        seg = qh[:, c * PEER_HALF:(c + 1) * PEER_HALF]
        sc = lax.dot_general(sk_ref[c], seg, (((1,), (1,)), ((), ())),
                             precision=lax.Precision.HIGHEST, preferred_element_type=F32)
        tops.append(_topk_rows(sc, PEER_TOPK))
    (s0, i0), (s1, i1) = tops
    k = PEER_TOPK
    sub = 8
    tm = s0.shape[1]
    r8 = lax.broadcasted_iota(I32, (sub, tm), 0)
    r16 = lax.broadcasted_iota(I32, (k, tm), 0)
    cand_b = [s0[0:1] + s1, s0[1:2] + s1[:sub]]
    cidx_b = [i0[0:1] * PEER_NKEYS + i1, i0[1:2] * PEER_NKEYS + i1[:sub]]
    pos_b = [r16, k + r8]
    for a in range(2, sub):
        keep = r8 < (k // (a + 1))
        cand_b.append(jnp.where(keep, s0[a:a + 1] + s1[:sub], NEG_INF))
        cidx_b.append(i0[a:a + 1] * PEER_NKEYS + i1[:sub])
        pos_b.append(a * k + r8)
    cand_b.append(s0[sub:] + s1[0:1])
    cidx_b.append(i0[sub:] * PEER_NKEYS + i1[0:1])
    pos_b.append((sub + r8) * k)
    cand = jnp.concatenate(cand_b, axis=0)
    cidx = jnp.concatenate(cidx_b, axis=0)
    pos = jnp.concatenate(pos_b, axis=0)
    vals, ids = [], []
    for _ in range(k):
        m = jnp.max(cand, axis=0, keepdims=True)
        px = jnp.min(jnp.where(cand == m, pos, k * k), axis=0, keepdims=True)
        hit = pos == px
        vals.append(m)
        ids.append(jnp.sum(jnp.where(hit, cidx, 0), axis=0, keepdims=True))
        cand = jnp.where(hit, NEG_INF, cand)
    sf = jnp.concatenate(vals, axis=0)
    e = jnp.exp(sf - sf[0:1])
    rows = pl.ds(pl.multiple_of(p * PEER_TOPK, PEER_TOPK), PEER_TOPK)
    wt_ref[rows, :] = e / jnp.sum(e, axis=0, keepdims=True)
    it_ref[rows, :] = jnp.concatenate(ids, axis=0)

    @pl.when(p == pl.num_programs(1) - 1)
    def _():
        idx_ref[...] = it_ref[...].T
        w_ref[...] = wt_ref[...].T


def peer_route(x2d, g, wq, sk, tok0, t, tm=1024):
    assert t % tm == 0 and tok0 % tm == 0
    d = x2d.shape[1]
    ph = sk.shape[0]
    nsel = ph * PEER_TOPK
    blk0 = tok0 // tm
    return pl.pallas_call(
        _route_kernel,
        grid=(t // tm, ph),
        in_specs=[
            pl.BlockSpec((tm, d), lambda i, p: (blk0 + i, 0)),
            pl.BlockSpec((1, d), lambda i, p: (0, 0)),
            pl.BlockSpec((d, 2 * PEER_HALF), lambda i, p: (0, p)),
            pl.BlockSpec((None, 2, PEER_NKEYS, PEER_HALF), lambda i, p: (p, 0, 0, 0)),
        ],
        out_specs=[
            pl.BlockSpec((tm, d // 2), lambda i, p: (i, 0)),
            pl.BlockSpec((tm, nsel), lambda i, p: (i, 0)),
            pl.BlockSpec((tm, nsel), lambda i, p: (i, 0)),
        ],
        out_shape=[jax.ShapeDtypeStruct((t, d // 2), I32),
                   jax.ShapeDtypeStruct((t, nsel), I32),
                   jax.ShapeDtypeStruct((t, nsel), F32)],
        scratch_shapes=[pltpu.VMEM((tm, d), BF16),
                        pltpu.VMEM((nsel, tm), I32),
                        pltpu.VMEM((nsel, tm), F32)],
        compiler_params=_cparams(("parallel", "arbitrary")),
        name="peer_route",
    )(x2d, g, wq, sk)


def _final_kernel(x_ref, y_ref, g_ref, o_ref):
    o_ref[...] = _rms(x_ref[...] + y_ref[...], g_ref[...])


def final_norm(x2d, y, g, tok0, tm=512):
    t, d = y.shape
    assert t % tm == 0 and tok0 % tm == 0
    blk0 = tok0 // tm
    spec = pl.BlockSpec((tm, d), lambda i: (i, 0))
    return pl.pallas_call(
        _final_kernel, grid=(t // tm,),
        in_specs=[pl.BlockSpec((tm, d), lambda i: (blk0 + i, 0)), spec, pl.BlockSpec((1, d), lambda i: (0, 0))],
        out_specs=spec,
        out_shape=jax.ShapeDtypeStruct((t, d), F32),
        compiler_params=_cparams(("parallel",)), name="final_norm",
    )(x2d, y, g)


SC_CORES = 2
SC_SUBCORES = 16
SC_WORKERS = SC_CORES * SC_SUBCORES
SC_LANES = 16
SC_GROUP = 16


def _sc_mesh():
    return plsc.VectorSubcoreMesh(core_axis_name="c", subcore_axis_name="s")


def _sc_params():
    return pltpu.CompilerParams(needs_layout_passes=False)


def _sc_worker_id():
    return lax.axis_index("s") * SC_CORES + lax.axis_index("c")


SC_ROW_LANE = 128


def _sc_unit_off(u):
    off = u * SC_LANES
    return off if isinstance(off, int) else pl.multiple_of(off, SC_LANES)


GELU_C0 = math.sqrt(2.0 / math.pi)
GELU_C1 = 0.044715


def _gelu_tanh(x):
    z = GELU_C0 * (x + GELU_C1 * (x * x * x))
    th = 1.0 - 2.0 / (jnp.exp(2.0 * z) + 1.0)
    return 0.5 * x * (1.0 + th)


SC_PK_RING = 4
SC_PK_SUB = 4
HI_MASK = -65536


def _pack_tables_kernel(u_ref, v_ref, o_ref):
    for part, ref in enumerate((u_ref, v_ref)):
        words = _pack_bf16_halves(ref[...])
        for sub in range(SC_PK_SUB):
            o_ref[:, part * SC_PK_SUB + sub, :] = words[:, sub * SC_ROW_LANE:(sub + 1) * SC_ROW_LANE]


def pack_expert_tables(u, v, te=512):
    e, d = u.shape
    assert d == 2 * SC_PK_SUB * SC_ROW_LANE
    spec = pl.BlockSpec((te, d), lambda i: (i, 0))
    return pl.pallas_call(
        _pack_tables_kernel, grid=(e // te,), in_specs=[spec, spec],
        out_specs=pl.BlockSpec((te, 2 * SC_PK_SUB, SC_ROW_LANE), lambda i: (i, 0, 0)),
        out_shape=jax.ShapeDtypeStruct((e, 2 * SC_PK_SUB, SC_ROW_LANE), I32),
        compiler_params=_cparams(("parallel",)), name="pack_expert_tables",
    )(u, v)


def _unpack_halves(x32):
    w = plsc.bitcast(x32, I32)
    return plsc.bitcast(w << 16, F32), plsc.bitcast(w & HI_MASK, F32)


def _tree_sum(xs):
    while len(xs) > 1:
        xs = [xs[i] + xs[i + 1] for i in range(0, len(xs), 2)]
    return xs[0]


def peer_experts_pk_sc(tab_uv, idx_flat, w_flat, hp, d):
    t = hp.shape[0]
    nsel = PEER_SEL
    g = SC_GROUP
    assert t % (SC_WORKERS * g) == 0 and d == 2 * SC_PK_SUB * SC_ROW_LANE
    tpw = t // SC_WORKERS
    groups = tpw // g
    heads = nsel // SC_LANES
    chunks = d // 32
    units = g * heads
    ring = SC_PK_RING
    assert units % ring == 0
    row_buf = pltpu.VMEM((SC_LANES, 2 * SC_PK_SUB, SC_ROW_LANE), I32)

    def row_words(rows, r, wc, sub0):
        per = SC_ROW_LANE // SC_LANES
        return plsc.bitcast(
            rows[r, sub0 + wc // per, pl.ds(pl.multiple_of((wc % per) * SC_LANES, SC_LANES), SC_LANES)], BF16)

    def ring_loop(n_units, start, wait, compute):
        for u in range(ring - 1):
            start(u, u)

        @pl.loop(0, n_units, step=ring)
        def _(uu):
            for b in range(ring):
                u = uu + b
                nxt = u + (ring - 1)

                @pl.when(nxt < n_units)
                def _():
                    start(nxt, (b + ring - 1) % ring)

                wait(u, b)
                compute(u, b)

    @functools.partial(
        pl.kernel, mesh=_sc_mesh(),
        out_type=jax.ShapeDtypeStruct((t, d), F32),
        scratch_types=[
            pltpu.VMEM((g * nsel,), I32),
            pltpu.VMEM((g * nsel,), F32),
            pltpu.VMEM((g, d // 2), I32),
            pltpu.VMEM((g, d), F32),
            pltpu.VMEM((SC_LANES * SC_LANES,), F32),
            [row_buf] * ring,
            [pltpu.SemaphoreType.DMA] * ring,
        ],
        compiler_params=_sc_params(),
        name="peer_experts_pk_sc",
    )
    def k(tab_hbm, idx_hbm, w_hbm, h_hbm, out_hbm, idx_v, coef_v, h_v, y_v, red_v, rows, sems):
        wid = _sc_worker_id()
        lane = lax.iota(I32, SC_LANES)

        def copy(u, slot):
            ids = idx_v.at[pl.ds(_sc_unit_off(u), SC_LANES)]
            return pltpu.make_async_copy(tab_hbm.at[ids], rows[slot], sems[slot])

        def dots(u, slot):
            tt = u // heads

            nsum = 4

            def body(cp, accs):
                out = []
                hv = [plsc.bitcast(h_v[tt, pl.ds(pl.multiple_of((nsum * cp + i) * SC_LANES, SC_LANES), SC_LANES)],
                                   BF16) for i in range(nsum)]
                for r in range(SC_LANES):
                    pr = _tree_sum([row_words(rows[slot], r, nsum * cp + i, 0) * hv[i] for i in range(nsum)])
                    lo, hi = _unpack_halves(pr)
                    out.append(accs[r] + lo + hi)
                return tuple(out)

            accs = lax.fori_loop(0, chunks // nsum, body,
                                 tuple(jnp.zeros((SC_LANES,), F32) for _ in range(SC_LANES)))
            for r in range(SC_LANES):
                red_v[pl.ds(r * SC_LANES, SC_LANES)] = accs[r]
            act = _tree_sum([plsc.load_gather(red_v, [lane * SC_LANES + j]) for j in range(SC_LANES)])
            sl = pl.ds(_sc_unit_off(u), SC_LANES)
            coef_v[sl] = coef_v[sl] * _gelu_tanh(act)

        def combine(u, slot):
            tt = u // heads
            first = (u % heads) == 0
            cb = []
            for r in range(SC_LANES):
                c = plsc.load_gather(coef_v, [jnp.full((SC_LANES,), u * SC_LANES + r, I32)])
                cb.append(plsc.pack(c, c, format=plsc.PackFormat.INTERLEAVED))

            @plsc.parallel_loop(0, chunks, unroll=2)
            def _(wc):
                lo, hi = _unpack_halves(
                    _tree_sum([cb[r] * row_words(rows[slot], r, wc, SC_PK_SUB) for r in range(SC_LANES)]))
                for half, val in ((0, lo), (1, hi)):
                    sl = pl.ds(pl.multiple_of(half * (d // 2) + wc * SC_LANES, SC_LANES), SC_LANES)
                    y_v[tt, sl] = val + jnp.where(first, 0.0, y_v[tt, sl])

        def unit(u, slot):
            dots(u, slot)
            combine(u, slot)

        @pl.loop(0, groups)
        def _(gi):
            base = wid * tpw + gi * g
            pltpu.sync_copy(idx_hbm.at[pl.ds(base * nsel, g * nsel)], idx_v)
            pltpu.sync_copy(w_hbm.at[pl.ds(base * nsel, g * nsel)], coef_v)
            pltpu.sync_copy(h_hbm.at[pl.ds(base, g)], h_v)
            ring_loop(units, lambda u, s: copy(u, s).start(), lambda u, s: copy(u, s).wait(), unit)
            pltpu.sync_copy(y_v, out_hbm.at[pl.ds(base, g)])

    return k(tab_uv, idx_flat, w_flat, hp)


def kernel(x, mem, rel_bias, ln_mix, w_in, hg_lower, hg_norm, w_up_a, w_up_b, w_out, ln_cross, ln_mem, wq_x, wk_x, wv_x, wo_x, ln_ffn, peer_query, peer_subkeys, peer_u, peer_v, ln_final):
    b, s, d = x.shape
    depth = w_in.shape[0]
    assert depth == 1, "the residual after PEER is fused into the final norm"
    assert s % MB_BLOCK == 0 and s % HG_CHUNK == 0 and s % (PEER_SLICES * SC_WORKERS * SC_GROUP) == 0
    nb = s // MB_BLOCK
    row = lambda a: a.reshape(1, -1).astype(F32)
    lb_all = jnp.cumsum(jax.nn.softmax(hg_lower.astype(F32), axis=0), axis=0)
    bias = moba_bias_tiles(rel_bias)
    n_hg = 4 * HG_WIDTH
    n_qk = 2 * MB_WIDTH
    n_mb = 3 * MB_WIDTH
    l = 0
    w = w_in[l].astype(BF16)
    w_hg, w_qk, w_vt, w_g = w[:, :n_hg], w[:, n_hg:n_hg + n_qk], w[:, n_hg + n_qk:n_hg + n_mb].T, w[:, n_hg + n_mb:]
    wa, wb, wo = w_up_a[l].astype(BF16), w_up_b[l].astype(BF16), w_out[l].astype(BF16)
    wqx, wox = wq_x[l].astype(BF16), wo_x[l].astype(BF16)
    wpq, sk = peer_query[l].astype(BF16), peer_subkeys[l].astype(F32)
    tab_uv = pack_expert_tables(peer_u[l].astype(F32), peer_v[l].astype(F32))
    kx, vx = mem_kv(mem, row(ln_mem[l]), wk_x[l].astype(BF16), wv_x[l].astype(BF16))

    outs = []
    for bi in range(b):
        x2d = x[bi]
        p0, pqk, vt, pg = in_proj(x2d, row(ln_mix[l]), w_hg, w_qk, w_vt, w_g)
        ya = hgrn2(p0, row(lb_all[l]), row(hg_norm[l]), 1, s)
        km = moba_kmean(pqk, 1, s).reshape(1, nb, MB_WIDTH)
        ts = s // PEER_SLICES
        for tok0 in range(0, s, ts):
            yb = moba_attention(pqk, vt, km, bias, 1, s, tok0 // MB_BLOCK, ts // MB_BLOCK)
            xs = mix_out(x2d, ya, yb, pg, wa, wb, wo, tok0)
            xs = cross_attn(xs, row(ln_cross[l]), wqx, kx[bi:bi + 1], vx[bi:bi + 1], wox, ts)
            hp, eidx, wts = peer_route(xs, row(ln_ffn[l]), wpq, sk, 0, ts)
            y = peer_experts_pk_sc(tab_uv, eidx.reshape(ts * PEER_SEL), wts.reshape(ts * PEER_SEL), hp, d)
            outs.append(final_norm(xs, y, row(ln_final), 0))
    return jnp.concatenate(outs, axis=0).reshape(b, s, d)
```

```python
import functools
import math

import jax
import jax.numpy as jnp
from jax import lax
from jax.experimental import pallas as pl
from jax.experimental.pallas import tpu as pltpu
from jax.experimental.pallas import tpu_sc as plsc

F32 = jnp.float32
BF16 = jnp.bfloat16
I32 = jnp.int32
EPS = 1e-6
NEG_INF = float("-inf")

HG_HEADS = 4
HG_D = 128
HG_WIDTH = HG_HEADS * HG_D
HG_CHUNK = 64
HG_SUB = 16
MB_HEADS = 8
MB_DH = 64
MB_WIDTH = MB_HEADS * MB_DH
MB_BLOCK = 256
MB_TOPK = 3
MB_BIAS_TILES = 8
REL_BUCKETS = 32
REL_MAX_DIST = 2048
X_HEADS = 4
PEER_HEADS = 8
PEER_NKEYS = 128
PEER_TOPK = 16
PEER_HALF = 128
PEER_SEL = PEER_HEADS * PEER_TOPK
PEER_SLICES = 4

VMEM_LIMIT = 56 * 1024 * 1024


def _cparams(sem):
    return pltpu.CompilerParams(dimension_semantics=sem, vmem_limit_bytes=VMEM_LIMIT)


def _rms(x, g):
    ms = jnp.mean(x * x, axis=-1, keepdims=True)
    return x * lax.rsqrt(ms + EPS) * g


def _in_proj_kernel(x_ref, g_ref, w0_ref, w1_ref, wvt_ref, w2_ref, o0_ref, o1_ref, ovt_ref, o2_ref):
    h = _rms(x_ref[...], g_ref[...]).astype(BF16)
    o0_ref[...] = jnp.dot(h, w0_ref[...], preferred_element_type=F32)
    o1_ref[...] = jnp.dot(h, w1_ref[...], preferred_element_type=F32).astype(BF16)
    vt = lax.dot_general(wvt_ref[...], h, (((1,), (1,)), ((), ())), preferred_element_type=F32).astype(BF16)
    for hd in range(MB_HEADS):
        ovt_ref[0, hd * MB_VROWS:hd * MB_VROWS + MB_DH, :] = vt[hd * MB_DH:(hd + 1) * MB_DH]
        ovt_ref[0, hd * MB_VROWS + MB_DH:(hd + 1) * MB_VROWS, :] = jnp.ones((MB_ONES, vt.shape[1]), BF16)
    o2_ref[...] = jnp.dot(h, w2_ref[...], preferred_element_type=F32).astype(BF16)


def in_proj(x2d, g, w0, w1, wvt, w2):
    t, d = x2d.shape
    tm = MB_BLOCK
    assert wvt.shape[0] == MB_WIDTH
    n0, n1, nv, n2 = w0.shape[1], w1.shape[1], MB_VT_ROWS, w2.shape[1]
    full = lambda a: pl.BlockSpec(a.shape, lambda i: (0, 0))
    return pl.pallas_call(
        _in_proj_kernel,
        grid=(t // tm,),
        in_specs=[pl.BlockSpec((tm, d), lambda i: (i, 0)), full(g), full(w0), full(w1), full(wvt), full(w2)],
        out_specs=[pl.BlockSpec((tm, n0), lambda i: (i, 0)),
                   pl.BlockSpec((tm, n1), lambda i: (i, 0)),
                   pl.BlockSpec((1, nv, tm), lambda i: (i, 0, 0)),
                   pl.BlockSpec((tm, n2), lambda i: (i, 0))],
        out_shape=[jax.ShapeDtypeStruct((t, n0), F32),
                   jax.ShapeDtypeStruct((t, n1), BF16),
                   jax.ShapeDtypeStruct((t // tm, nv, tm), BF16),
                   jax.ShapeDtypeStruct((t, n2), BF16)],
        compiler_params=_cparams(("parallel",)),
        name="in_proj",
    )(x2d, g, w0, w1, wvt, w2)


def _hgrn_kernel(q_ref, f_ref, i_ref, g_ref, lb_ref, gain_ref, o_ref, st_ref):
    c = pl.program_id(1)

    @pl.when(c == 0)
    def _():
        st_ref[...] = jnp.zeros_like(st_ref)

    C, S = HG_CHUNK, HG_SUB
    row = lax.broadcasted_iota(I32, (C, C), 0)
    col = lax.broadcasted_iota(I32, (C, C), 1)
    tril = (row >= col).astype(F32)
    t_iota = lax.broadcasted_iota(I32, (S, 1), 0)

    for h in range(HG_HEADS):
        sl = slice(h * HG_D, (h + 1) * HG_D)
        q = q_ref[:, sl]
        v = i_ref[:, sl]
        lb = lb_ref[:, sl]
        f = lb + (1.0 - lb) * jax.nn.sigmoid(f_ref[:, sl])
        lf = jnp.log(f)
        k = 1.0 - f
        b = jnp.dot(tril, lf, precision=lax.Precision.HIGHEST, preferred_element_type=F32)
        st = st_ref[h]
        vb = v.astype(BF16)
        qd = (q * jnp.exp(b)).astype(BF16)
        o_inter = lax.dot_general(qd, st.astype(BF16), (((1,), (1,)), ((), ())),
                                  preferred_element_type=F32)
        outs = []
        for i in range(C // S):
            r0 = i * S
            qi = q[r0:r0 + S]
            ki = k[r0:r0 + S]
            bi = b[r0:r0 + S]
            vi = v[r0:r0 + S]
            oi = o_inter[r0:r0 + S]
            if i > 0:
                bs = b[r0 - 1:r0]
                qh = (qi * jnp.exp(bi - bs)).astype(BF16)
                kh = (k[:r0] * jnp.exp(bs - b[:r0])).astype(BF16)
                a = lax.dot_general(qh, kh, (((1,), (1,)), ((), ())), preferred_element_type=F32)
                oi = oi + jnp.dot(a.astype(BF16), vb[:r0], preferred_element_type=F32)
            half = S // 2
            o_half = [oi[:half], oi[half:]]
            for s in range(S):
                for hf in range(s // half, 2):
                    rows = slice(hf * half, (hf + 1) * half)
                    dec = jnp.exp(jnp.minimum(bi[rows] - bi[s:s + 1], 0.0))
                    a_s = jnp.sum(qi[rows] * ki[s:s + 1] * dec, axis=-1, keepdims=True)
                    a_s = jnp.where(t_iota[rows] >= s, a_s, 0.0)
                    o_half[hf] = o_half[hf] + a_s * vi[s:s + 1]
            outs.extend(o_half)
        o = jnp.concatenate(outs, axis=0)
        b_end = b[C - 1:C]
        kd = (k * jnp.exp(b_end - b)).astype(BF16)
        upd = lax.dot_general(vb, kd, (((0,), (0,)), ((), ())), preferred_element_type=F32)
        st_ref[h] = st * jnp.exp(b_end) + upd
        o = o * lax.rsqrt(jnp.mean(o * o, axis=-1, keepdims=True) + EPS)
        g = g_ref[:, sl]
        o_ref[:, sl] = (o * gain_ref[:, sl] * (g * jax.nn.sigmoid(g))).astype(o_ref.dtype)


def hgrn2(p0, lb, gain, batch, seq):
    t = p0.shape[0]
    nc = seq // HG_CHUNK
    w = HG_WIDTH

    def col(j):
        return pl.BlockSpec((HG_CHUNK, w), lambda b, c, j=j: (b * nc + c, j))

    return pl.pallas_call(
        _hgrn_kernel,
        grid=(batch, nc),
        in_specs=[col(0), col(1), col(2), col(3),
                  pl.BlockSpec((1, w), lambda b, c: (0, 0)),
                  pl.BlockSpec((1, w), lambda b, c: (0, 0))],
        out_specs=pl.BlockSpec((HG_CHUNK, w), lambda b, c: (b * nc + c, 0)),
        out_shape=jax.ShapeDtypeStruct((t, w), BF16),
        scratch_shapes=[pltpu.VMEM((HG_HEADS, HG_D, HG_D), F32)],
        compiler_params=_cparams(("parallel", "arbitrary")),
        name="hgrn2",
    )(p0, p0, p0, p0, lb, gain)


def _kmean_kernel(k_ref, o_ref):
    o_ref[0] = jnp.mean(k_ref[...].astype(F32), axis=0, keepdims=True)


def moba_kmean(p1, batch, seq):
    nbt = p1.shape[0] // MB_BLOCK
    return pl.pallas_call(
        _kmean_kernel,
        grid=(nbt,),
        in_specs=[pl.BlockSpec((MB_BLOCK, MB_WIDTH), lambda i: (i, 1))],
        out_specs=pl.BlockSpec((1, 1, MB_WIDTH), lambda i: (i, 0, 0)),
        out_shape=jax.ShapeDtypeStruct((nbt, 1, MB_WIDTH), F32),
        compiler_params=_cparams(("parallel",)),
        name="moba_kmean",
    )(p1)


MB_PAIR = 4
MB_PW = MB_PAIR * MB_DH
MB_LG = 128
MB_ONES = 16
MB_VROWS = MB_DH + MB_ONES
MB_VT_ROWS = MB_HEADS * MB_VROWS


def _moba_kernel(q_ref, k_ref, vt_ref, km_ref, bias_ref, o_ref, *scratch, qb0):
    m_ref, l_ref, al_ref, acc_ref, msk_ref, s_ref, p_ref = (
        scratch[i * MB_PAIR:(i + 1) * MB_PAIR] for i in range(7))
    qi = pl.program_id(2) + qb0
    nb = km_ref.shape[0]
    blk = MB_BLOCK
    heads = range(MB_PAIR)
    grp = lambda hh: slice((hh // 2) * MB_LG, (hh // 2 + 1) * MB_LG)
    q = q_ref[...]
    lane = lax.broadcasted_iota(I32, (blk, MB_LG), 1)
    in_head = [(lane < MB_DH) if hh % 2 == 0 else (lane >= MB_DH) for hh in heads]
    qs = q * jnp.asarray(MB_DH ** -0.5, BF16)
    nt = (((1,), (1,)), ((), ()))
    qf = q.astype(F32)
    qht = [jnp.where(in_head[hh], qs[:, grp(hh)].astype(F32), 0.0).T.astype(BF16) for hh in heads]

    n_io = lax.broadcasted_iota(I32, (nb, blk), 0)
    for hh in heads:
        gate = lax.dot_general(km_ref[:, grp(hh)], jnp.where(in_head[hh], qf[:, grp(hh)], 0.0), nt,
                               precision=lax.Precision.HIGHEST, preferred_element_type=F32)
        gate = jnp.where(n_io < qi, gate, NEG_INF)
        chosen = n_io < 0
        for _ in range(MB_TOPK):
            mx = jnp.max(gate, axis=0, keepdims=True)
            ix = jnp.min(jnp.where(gate == mx, n_io, nb), axis=0, keepdims=True)
            hit = n_io == ix
            chosen = chosen | (hit & (mx > NEG_INF))
            gate = jnp.where(hit, NEG_INF, gate)
        msk_ref[hh][...] = jnp.where(chosen, 0.0, NEG_INF)

    vrows = lambda hh: slice(hh * MB_VROWS, (hh + 1) * MB_VROWS)

    def pv_stage(blk_idx):
        vtb = vt_ref[blk_idx]
        r = [jnp.dot(vtb[vrows(hh)], p_ref[hh][...], preferred_element_type=F32) for hh in heads]
        al = [al_ref[hh][...] for hh in heads]
        a_new = [al[hh] * acc_ref[hh][...] + r[hh][:MB_DH] for hh in heads]
        l_new = [al[hh] * l_ref[hh][...] + r[hh][MB_DH:MB_DH + 1] for hh in heads]
        return a_new, l_new

    def store_pv(a_new, l_new):
        for hh in heads:
            acc_ref[hh][...] = a_new[hh]
            l_ref[hh][...] = l_new[hh]

    def softmax_stage():
        s = [s_ref[hh][...] for hh in heads]
        m_old = [m_ref[hh][...] for hh in heads]
        m_new = [jnp.maximum(m_old[hh], jnp.max(s[hh], axis=0, keepdims=True)) for hh in heads]
        alpha = [jnp.exp(m_old[hh] - m_new[hh]) for hh in heads]
        p = [jnp.exp((s[hh] - m_new[hh]).astype(BF16)) for hh in heads]
        return p, alpha, m_new

    def store_softmax(p, alpha, m_new):
        for hh in heads:
            p_ref[hh][...] = p[hh]
            al_ref[hh][...] = alpha[hh]
            m_ref[hh][...] = m_new[hh]

    k_own = k_ref[pl.ds(pl.multiple_of(qi * blk, blk), blk), :]
    key_io = lax.broadcasted_iota(I32, (blk, blk), 0)
    qry_io = lax.broadcasted_iota(I32, (blk, blk), 1)
    for hh in heads:
        s = jnp.dot(k_own[:, grp(hh)], qht[hh], preferred_element_type=F32) + bias_ref[hh, 0]
        s_ref[hh][...] = jnp.where(key_io <= qry_io, s, NEG_INF)
        m_ref[hh][...] = jnp.full((1, blk), NEG_INF, F32)
        l_ref[hh][...] = jnp.zeros((1, blk), F32)
        al_ref[hh][...] = jnp.ones((1, blk), F32)
        acc_ref[hh][...] = jnp.zeros((MB_DH, blk), F32)
        p_ref[hh][...] = jnp.zeros((blk, blk), BF16)

    def step(i, carry, far):
        pv = pv_stage(jnp.where(i <= 1, qi, i - 2))
        sm = softmax_stage()
        kn = k_ref[pl.ds(pl.multiple_of(i * blk, blk), blk), :]
        if far:
            row = [msk_ref[hh][pl.ds(i, 1), :] + bias_ref[hh, MB_BIAS_TILES - 1, 0:1, 0:1] for hh in heads]
            s_next = [jnp.dot(kn[:, grp(hh)], qht[hh], preferred_element_type=F32) + row[hh] for hh in heads]
        else:
            d = qi - i
            s_next = [jnp.dot(kn[:, grp(hh)], qht[hh], preferred_element_type=F32)
                      + bias_ref[hh, d] + msk_ref[hh][pl.ds(i, 1), :] for hh in heads]
        store_pv(*pv)
        for hh in heads:
            s_ref[hh][...] = s_next[hh]
        store_softmax(*sm)
        return carry

    n_far = jnp.maximum(qi - (MB_BIAS_TILES - 2), 0)
    lax.fori_loop(0, n_far, functools.partial(step, far=True), 0)
    lax.fori_loop(n_far, qi, functools.partial(step, far=False), 0)
    pv = pv_stage(jnp.where(qi <= 1, qi, qi - 2))
    sm = softmax_stage()
    store_pv(*pv)
    store_softmax(*sm)
    a_fin, l_fin = pv_stage(jnp.where(qi == 0, qi, qi - 1))
    out_t = jnp.concatenate([a_fin[hh] / l_fin[hh] for hh in heads], axis=0)
    o_ref[...] = out_t.T.astype(o_ref.dtype)


def moba_attention(pqk, vt, km, bias, batch, seq, qb0=0, nqb=None):
    nb = seq // MB_BLOCK
    nqb = nb if nqb is None else nqb
    t = batch * nqb * MB_BLOCK
    groups = MB_WIDTH // MB_PW
    return pl.pallas_call(
        functools.partial(_moba_kernel, qb0=qb0),
        grid=(batch, groups, nqb),
        in_specs=[
            pl.BlockSpec((MB_BLOCK, MB_PW), lambda b, j, i: (b * nb + qb0 + i, j)),
            pl.BlockSpec((seq, MB_PW), lambda b, j, i: (b, groups + j)),
            pl.BlockSpec((nb, MB_PAIR * MB_VROWS, MB_BLOCK), lambda b, j, i: (b, j, 0)),
            pl.BlockSpec((None, nb, MB_PW), lambda b, j, i: (b, 0, j)),
            pl.BlockSpec((MB_PAIR, MB_BIAS_TILES, MB_BLOCK, MB_BLOCK), lambda b, j, i: (j, 0, 0, 0)),
        ],
        out_specs=pl.BlockSpec((MB_BLOCK, MB_PW), lambda b, j, i: (b * nqb + i, j)),
        out_shape=jax.ShapeDtypeStruct((t, MB_WIDTH), BF16),
        scratch_shapes=(
            [pltpu.VMEM((1, MB_BLOCK), F32)] * (3 * MB_PAIR)
            + [pltpu.VMEM((MB_DH, MB_BLOCK), F32)] * MB_PAIR
            + [pltpu.VMEM((nb, MB_BLOCK), F32)] * MB_PAIR
            + [pltpu.VMEM((MB_BLOCK, MB_BLOCK), F32)] * MB_PAIR
            + [pltpu.VMEM((MB_BLOCK, MB_BLOCK), BF16)] * MB_PAIR
        ),
        compiler_params=_cparams(("parallel", "parallel", "arbitrary")),
        name="moba_attn",
    )(pqk, pqk, vt, km, bias)


def _t5_bucket(dist):
    max_exact = REL_BUCKETS // 2
    scaled = jnp.log(jnp.maximum(dist, 1).astype(F32) / max_exact) / math.log(REL_MAX_DIST / max_exact)
    large = jnp.minimum(max_exact + (scaled * (REL_BUCKETS - max_exact)).astype(I32), REL_BUCKETS - 1)
    return jnp.where(dist < max_exact, dist, large)


def moba_bias_tiles(rel_bias):
    blk = MB_BLOCK
    span = 2 * blk - 1
    x = jnp.arange(span) - (blk - 1)
    dist = jnp.maximum(jnp.arange(MB_BIAS_TILES)[:, None] * blk + x[None, :], 0)
    w = rel_bias.astype(F32).T[:, _t5_bucket(dist)]
    h = w.shape[0]
    wp = jnp.pad(w, ((0, 0), (0, 0), (0, 1)))
    a = jnp.broadcast_to(wp[:, :, None, :], (h, MB_BIAS_TILES, blk, span + 1))
    a = a.reshape(h, MB_BIAS_TILES, blk * (span + 1))[:, :, :blk * span]
    return a.reshape(h, MB_BIAS_TILES, blk, span)[:, :, :, blk - 1:]


def _mix_kernel(x_ref, ya_ref, yb_ref, ga_ref, gb_ref, wa_ref, wb_ref, wo_ref, o_ref):
    za = jnp.dot(ya_ref[...], wa_ref[...], preferred_element_type=F32)
    zb = jnp.dot(yb_ref[...], wb_ref[...], preferred_element_type=F32)
    z = jax.nn.sigmoid(ga_ref[...].astype(F32)) * za + jax.nn.sigmoid(gb_ref[...].astype(F32)) * zb
    o_ref[...] = x_ref[...] + jnp.dot(z.astype(BF16), wo_ref[...], preferred_element_type=F32)


def mix_out(x2d, ya, yb, pg, wa, wb, wo, tok0=0, tm=512):
    t = yb.shape[0]
    assert t % tm == 0 and tok0 % tm == 0
    d = x2d.shape[1]
    w = ya.shape[1]
    b0 = tok0 // tm
    return pl.pallas_call(
        _mix_kernel,
        grid=(t // tm,),
        in_specs=[
            pl.BlockSpec((tm, d), lambda i: (b0 + i, 0)),
            pl.BlockSpec((tm, w), lambda i: (b0 + i, 0)),
            pl.BlockSpec((tm, w), lambda i: (i, 0)),
            pl.BlockSpec((tm, d), lambda i: (b0 + i, 0)),
            pl.BlockSpec((tm, d), lambda i: (b0 + i, 1)),
            pl.BlockSpec((w, d), lambda i: (0, 0)),
            pl.BlockSpec((w, d), lambda i: (0, 0)),
            pl.BlockSpec((d, d), lambda i: (0, 0)),
        ],
        out_specs=pl.BlockSpec((tm, d), lambda i: (i, 0)),
        out_shape=jax.ShapeDtypeStruct((t, d), F32),
        compiler_params=_cparams(("parallel",)),
        name="mix_out",
    )(x2d, ya, yb, pg, pg, wa, wb, wo)


def _mem_kv_kernel(m_ref, g_ref, wk_ref, wv_ref, k_ref, v_ref):
    mn = _rms(m_ref[...], g_ref[...]).astype(BF16)
    k_ref[...] = jnp.dot(mn, wk_ref[...], preferred_element_type=F32).astype(BF16)
    v_ref[...] = jnp.dot(mn, wv_ref[...], preferred_element_type=F32).astype(BF16)


def mem_kv(mem, g, wk, wv):
    b, m, d = mem.shape
    spec = pl.BlockSpec((None, m, d), lambda i: (i, 0, 0))
    wspec = pl.BlockSpec((d, d), lambda i: (0, 0))
    return pl.pallas_call(
        _mem_kv_kernel,
        grid=(b,),
        in_specs=[spec, pl.BlockSpec((1, d), lambda i: (0, 0)), wspec, wspec],
        out_specs=[spec, spec],
        out_shape=[jax.ShapeDtypeStruct((b, m, d), BF16)] * 2,
        compiler_params=_cparams(("parallel",)),
        name="mem_kv",
    )(mem, g, wk, wv)


def _cross_kernel(x_ref, g_ref, wq_ref, k_ref, v_ref, wo_ref, o_ref):
    x = x_ref[...]
    d = x.shape[1]
    dh = d // X_HEADS
    h = _rms(x, g_ref[...]).astype(BF16)
    q = (jnp.dot(h, wq_ref[...], preferred_element_type=F32) * (dh ** -0.5)).astype(BF16)
    outs = []
    for hh in range(X_HEADS):
        sl = slice(hh * dh, (hh + 1) * dh)
        s = lax.dot_general(q[:, sl], k_ref[:, sl], (((1,), (1,)), ((), ())),
                            preferred_element_type=F32)
        p = jnp.exp(s - jnp.max(s, axis=1, keepdims=True))
        l = jnp.sum(p, axis=1, keepdims=True)
        o = jnp.dot(p.astype(BF16), v_ref[:, sl], preferred_element_type=F32) / l
        outs.append(o.astype(BF16))
    o = jnp.concatenate(outs, axis=1)
    o_ref[...] = x + jnp.dot(o, wo_ref[...], preferred_element_type=F32)


def cross_attn(x2d, g, wq, kx, vx, wo, seq, tm=512):
    t, d = x2d.shape
    assert seq % tm == 0 and t % seq == 0
    m = kx.shape[1]
    per_b = seq // tm
    kv = pl.BlockSpec((None, m, d), lambda i: (i // per_b, 0, 0))
    wspec = pl.BlockSpec((d, d), lambda i: (0, 0))
    return pl.pallas_call(
        _cross_kernel,
        grid=(t // tm,),
        in_specs=[pl.BlockSpec((tm, d), lambda i: (i, 0)), pl.BlockSpec((1, d), lambda i: (0, 0)),
                  wspec, kv, kv, wspec],
        out_specs=pl.BlockSpec((tm, d), lambda i: (i, 0)),
        out_shape=jax.ShapeDtypeStruct((t, d), F32),
        compiler_params=_cparams(("parallel",)),
        name="cross_attn",
    )(x2d, g, wq, kx, vx, wo)


def _topk_rows(sc, k):
    n = sc.shape[0]
    io = lax.broadcasted_iota(I32, sc.shape, 0)
    vals, ids = [], []
    for _ in range(k):
        m = jnp.max(sc, axis=0, keepdims=True)
        ix = jnp.min(jnp.where(sc == m, io, n), axis=0, keepdims=True)
        vals.append(m)
        ids.append(ix)
        sc = jnp.where(io == ix, NEG_INF, sc)
    return jnp.concatenate(vals, axis=0), jnp.concatenate(ids, axis=0)


def _pack_bf16_halves(h):
    bits = lax.bitcast_convert_type(h, I32)
    r = bits + 0x7FFF + (lax.shift_right_logical(bits, 16) & 1)
    half = h.shape[1] // 2
    return lax.shift_right_logical(r[:, :half], 16) | (r[:, half:] & HI_MASK)


def _route_kernel(x_ref, g_ref, wq_ref, sk_ref, hp_ref, idx_ref, w_ref, hb_ref, it_ref, wt_ref):
    p = pl.program_id(1)

    @pl.when(p == 0)
    def _():
        h = _rms(x_ref[...], g_ref[...])
        hp_ref[...] = _pack_bf16_halves(h)
        hb_ref[...] = h.astype(BF16)

    qh = jnp.dot(hb_ref[...], wq_ref[...], preferred_element_type=F32)
    tops = []
    for c in range(2):
        seg = qh[:, c * PEER_HALF:(c + 1) * PEER_HALF]
        sc = lax.dot_general(sk_ref[c], seg, (((1,), (1,)), ((), ())),
                             precision=lax.Precision.HIGHEST, preferred_element_type=F32)
        tops.append(_topk_rows(sc, PEER_TOPK))
    (s0, i0), (s1, i1) = tops
    k = PEER_TOPK
    sub = 8
    tm = s0.shape[1]
    r8 = lax.broadcasted_iota(I32, (sub, tm), 0)
    r16 = lax.broadcasted_iota(I32, (k, tm), 0)
    cand_b = [s0[0:1] + s1, s0[1:2] + s1[:sub]]
    cidx_b = [i0[0:1] * PEER_NKEYS + i1, i0[1:2] * PEER_NKEYS + i1[:sub]]
    pos_b = [r16, k + r8]
    for a in range(2, sub):
        keep = r8 < (k // (a + 1))
        cand_b.append(jnp.where(keep, s0[a:a + 1] + s1[:sub], NEG_INF))
        cidx_b.append(i0[a:a + 1] * PEER_NKEYS + i1[:sub])
        pos_b.append(a * k + r8)
    cand_b.append(s0[sub:] + s1[0:1])
    cidx_b.append(i0[sub:] * PEER_NKEYS + i1[0:1])
    pos_b.append((sub + r8) * k)
    cand = jnp.concatenate(cand_b, axis=0)
    cidx = jnp.concatenate(cidx_b, axis=0)
    pos = jnp.concatenate(pos_b, axis=0)
    vals, ids = [], []
    for _ in range(k):
        m = jnp.max(cand, axis=0, keepdims=True)
        px = jnp.min(jnp.where(cand == m, pos, k * k), axis=0, keepdims=True)
        hit = pos == px
        vals.append(m)
        ids.append(jnp.sum(jnp.where(hit, cidx, 0), axis=0, keepdims=True))
        cand = jnp.where(hit, NEG_INF, cand)
    sf = jnp.concatenate(vals, axis=0)
    e = jnp.exp(sf - sf[0:1])
    rows = pl.ds(pl.multiple_of(p * PEER_TOPK, PEER_TOPK), PEER_TOPK)
    wt_ref[rows, :] = e / jnp.sum(e, axis=0, keepdims=True)
    it_ref[rows, :] = jnp.concatenate(ids, axis=0)

    @pl.when(p == pl.num_programs(1) - 1)
    def _():
        idx_ref[...] = it_ref[...].T
        w_ref[...] = wt_ref[...].T


def peer_route(x2d, g, wq, sk, tok0, t, tm=1024):
    assert t % tm == 0 and tok0 % tm == 0
    d = x2d.shape[1]
    ph = sk.shape[0]
    nsel = ph * PEER_TOPK
    blk0 = tok0 // tm
    return pl.pallas_call(
        _route_kernel,
        grid=(t // tm, ph),
        in_specs=[
            pl.BlockSpec((tm, d), lambda i, p: (blk0 + i, 0)),
            pl.BlockSpec((1, d), lambda i, p: (0, 0)),
            pl.BlockSpec((d, 2 * PEER_HALF), lambda i, p: (0, p)),
            pl.BlockSpec((None, 2, PEER_NKEYS, PEER_HALF), lambda i, p: (p, 0, 0, 0)),
        ],
        out_specs=[
            pl.BlockSpec((tm, d // 2), lambda i, p: (i, 0)),
            pl.BlockSpec((tm, nsel), lambda i, p: (i, 0)),
            pl.BlockSpec((tm, nsel), lambda i, p: (i, 0)),
        ],
        out_shape=[jax.ShapeDtypeStruct((t, d // 2), I32),
                   jax.ShapeDtypeStruct((t, nsel), I32),
                   jax.ShapeDtypeStruct((t, nsel), F32)],
        scratch_shapes=[pltpu.VMEM((tm, d), BF16),
                        pltpu.VMEM((nsel, tm), I32),
                        pltpu.VMEM((nsel, tm), F32)],
        compiler_params=_cparams(("parallel", "arbitrary")),
        name="peer_route",
    )(x2d, g, wq, sk)


def _final_kernel(x_ref, y_ref, g_ref, o_ref):
    o_ref[...] = _rms(x_ref[...] + y_ref[...], g_ref[...])


def final_norm(x2d, y, g, tok0, tm=512):
    t, d = y.shape
    assert t % tm == 0 and tok0 % tm == 0
    blk0 = tok0 // tm
    spec = pl.BlockSpec((tm, d), lambda i: (i, 0))
    return pl.pallas_call(
        _final_kernel, grid=(t // tm,),
        in_specs=[pl.BlockSpec((tm, d), lambda i: (blk0 + i, 0)), spec, pl.BlockSpec((1, d), lambda i: (0, 0))],
        out_specs=spec,
        out_shape=jax.ShapeDtypeStruct((t, d), F32),
        compiler_params=_cparams(("parallel",)), name="final_norm",
    )(x2d, y, g)


SC_CORES = 2
SC_SUBCORES = 16
SC_WORKERS = SC_CORES * SC_SUBCORES
SC_LANES = 16
SC_GROUP = 16


def _sc_mesh():
    return plsc.VectorSubcoreMesh(core_axis_name="c", subcore_axis_name="s")


def _sc_params():
    return pltpu.CompilerParams(needs_layout_passes=False)


def _sc_worker_id():
    return lax.axis_index("s") * SC_CORES + lax.axis_index("c")


SC_ROW_LANE = 128


def _sc_unit_off(u):
    off = u * SC_LANES
    return off if isinstance(off, int) else pl.multiple_of(off, SC_LANES)


GELU_C0 = math.sqrt(2.0 / math.pi)
GELU_C1 = 0.044715


def _gelu_tanh(x):
    z = GELU_C0 * (x + GELU_C1 * (x * x * x))
    th = 1.0 - 2.0 / (jnp.exp(2.0 * z) + 1.0)
    return 0.5 * x * (1.0 + th)


SC_PK_RING = 2
SC_PK_SUB = 4
HI_MASK = -65536


def _pack_tables_kernel(u_ref, v_ref, o_ref):
    for part, ref in enumerate((u_ref, v_ref)):
        words = _pack_bf16_halves(ref[...])
        for sub in range(SC_PK_SUB):
            o_ref[:, part * SC_PK_SUB + sub, :] = words[:, sub * SC_ROW_LANE:(sub + 1) * SC_ROW_LANE]


def pack_expert_tables(u, v, te=512):
    e, d = u.shape
    assert d == 2 * SC_PK_SUB * SC_ROW_LANE
    spec = pl.BlockSpec((te, d), lambda i: (i, 0))
    return pl.pallas_call(
        _pack_tables_kernel, grid=(e // te,), in_specs=[spec, spec],
        out_specs=pl.BlockSpec((te, 2 * SC_PK_SUB, SC_ROW_LANE), lambda i: (i, 0, 0)),
        out_shape=jax.ShapeDtypeStruct((e, 2 * SC_PK_SUB, SC_ROW_LANE), I32),
        compiler_params=_cparams(("parallel",)), name="pack_expert_tables",
    )(u, v)


def _unpack_halves(x32):
    w = plsc.bitcast(x32, I32)
    return plsc.bitcast(w << 16, F32), plsc.bitcast(w & HI_MASK, F32)


def _tree_sum(xs):
    while len(xs) > 1:
        xs = [xs[i] + xs[i + 1] for i in range(0, len(xs), 2)]
    return xs[0]


def peer_experts_pk_sc(tab_uv, idx_flat, w_flat, hp, d):
    t = hp.shape[0]
    nsel = PEER_SEL
    g = SC_GROUP
    assert t % (SC_WORKERS * g) == 0 and d == 2 * SC_PK_SUB * SC_ROW_LANE
    tpw = t // SC_WORKERS
    groups = tpw // g
    heads = nsel // SC_LANES
    chunks = d // 32
    units = g * heads
    ring = SC_PK_RING
    assert units % ring == 0
    row_buf = pltpu.VMEM((SC_LANES, 2 * SC_PK_SUB, SC_ROW_LANE), I32)

    def row_words(rows, r, wc, sub0):
        per = SC_ROW_LANE // SC_LANES
        return plsc.bitcast(
            rows[r, sub0 + wc // per, pl.ds(pl.multiple_of((wc % per) * SC_LANES, SC_LANES), SC_LANES)], BF16)

    def ring_loop(n_units, start, wait, compute):
        for u in range(ring - 1):
            start(u, u)

        @pl.loop(0, n_units, step=ring)
        def _(uu):
            for b in range(ring):
                u = uu + b
                nxt = u + (ring - 1)

                @pl.when(nxt < n_units)
                def _():
                    start(nxt, (b + ring - 1) % ring)

                wait(u, b)
                compute(u, b)

    @functools.partial(
        pl.kernel, mesh=_sc_mesh(),
        out_type=jax.ShapeDtypeStruct((t, d), F32),
        scratch_types=[
            pltpu.VMEM((g * nsel,), I32),
            pltpu.VMEM((g * nsel,), F32),
            pltpu.VMEM((g, d // 2), I32),
            pltpu.VMEM((g, d), F32),
            pltpu.VMEM((SC_LANES * SC_LANES,), F32),
            [row_buf] * ring,
            [pltpu.SemaphoreType.DMA] * ring,
        ],
        compiler_params=_sc_params(),
        name="peer_experts_pk_sc",
    )
    def k(tab_hbm, idx_hbm, w_hbm, h_hbm, out_hbm, idx_v, coef_v, h_v, y_v, red_v, rows, sems):
        wid = _sc_worker_id()
        lane = lax.iota(I32, SC_LANES)

        def copy(u, slot):
            ids = idx_v.at[pl.ds(_sc_unit_off(u), SC_LANES)]
            return pltpu.make_async_copy(tab_hbm.at[ids], rows[slot], sems[slot])

        def dots(u, slot):
            tt = u // heads

            def body(cp, accs):
                out = []
                hv = [plsc.bitcast(h_v[tt, pl.ds(pl.multiple_of((2 * cp + i) * SC_LANES, SC_LANES), SC_LANES)], BF16)
                      for i in range(2)]
                for r in range(SC_LANES):
                    pr = (row_words(rows[slot], r, 2 * cp, 0) * hv[0]
                          + row_words(rows[slot], r, 2 * cp + 1, 0) * hv[1])
                    lo, hi = _unpack_halves(pr)
                    out.append(accs[r] + lo + hi)
                return tuple(out)

            accs = lax.fori_loop(0, chunks // 2, body,
                                 tuple(jnp.zeros((SC_LANES,), F32) for _ in range(SC_LANES)))
            for r in range(SC_LANES):
                red_v[pl.ds(r * SC_LANES, SC_LANES)] = accs[r]
            act = _tree_sum([plsc.load_gather(red_v, [lane * SC_LANES + j]) for j in range(SC_LANES)])
            sl = pl.ds(_sc_unit_off(u), SC_LANES)
            coef_v[sl] = coef_v[sl] * _gelu_tanh(act)

        def combine(u, slot):
            tt = u // heads
            first = (u % heads) == 0
            cb = []
            for r in range(SC_LANES):
                c = plsc.load_gather(coef_v, [jnp.full((SC_LANES,), u * SC_LANES + r, I32)])
                cb.append(plsc.pack(c, c, format=plsc.PackFormat.INTERLEAVED))

            @plsc.parallel_loop(0, chunks, unroll=2)
            def _(wc):
                lo, hi = _unpack_halves(
                    _tree_sum([cb[r] * row_words(rows[slot], r, wc, SC_PK_SUB) for r in range(SC_LANES)]))
                for half, val in ((0, lo), (1, hi)):
                    sl = pl.ds(pl.multiple_of(half * (d // 2) + wc * SC_LANES, SC_LANES), SC_LANES)
                    y_v[tt, sl] = val + jnp.where(first, 0.0, y_v[tt, sl])

        def unit(u, slot):
            dots(u, slot)
            combine(u, slot)

        @pl.loop(0, groups)
        def _(gi):
            base = wid * tpw + gi * g
            pltpu.sync_copy(idx_hbm.at[pl.ds(base * nsel, g * nsel)], idx_v)
            pltpu.sync_copy(w_hbm.at[pl.ds(base * nsel, g * nsel)], coef_v)
            pltpu.sync_copy(h_hbm.at[pl.ds(base, g)], h_v)
            ring_loop(units, lambda u, s: copy(u, s).start(), lambda u, s: copy(u, s).wait(), unit)
            pltpu.sync_copy(y_v, out_hbm.at[pl.ds(base, g)])

    return k(tab_uv, idx_flat, w_flat, hp)


def kernel(x, mem, rel_bias, ln_mix, w_in, hg_lower, hg_norm, w_up_a, w_up_b, w_out, ln_cross, ln_mem, wq_x, wk_x, wv_x, wo_x, ln_ffn, peer_query, peer_subkeys, peer_u, peer_v, ln_final):
    b, s, d = x.shape
    depth = w_in.shape[0]
    assert depth == 1, "the residual after PEER is fused into the final norm"
    assert s % MB_BLOCK == 0 and s % HG_CHUNK == 0 and s % (PEER_SLICES * SC_WORKERS * SC_GROUP) == 0
    nb = s // MB_BLOCK
    row = lambda a: a.reshape(1, -1).astype(F32)
    lb_all = jnp.cumsum(jax.nn.softmax(hg_lower.astype(F32), axis=0), axis=0)
    bias = moba_bias_tiles(rel_bias)
    n_hg = 4 * HG_WIDTH
    n_qk = 2 * MB_WIDTH
    n_mb = 3 * MB_WIDTH
    l = 0
    w = w_in[l].astype(BF16)
    w_hg, w_qk, w_vt, w_g = w[:, :n_hg], w[:, n_hg:n_hg + n_qk], w[:, n_hg + n_qk:n_hg + n_mb].T, w[:, n_hg + n_mb:]
    wa, wb, wo = w_up_a[l].astype(BF16), w_up_b[l].astype(BF16), w_out[l].astype(BF16)
    wqx, wox = wq_x[l].astype(BF16), wo_x[l].astype(BF16)
    wpq, sk = peer_query[l].astype(BF16), peer_subkeys[l].astype(F32)
    tab_uv = pack_expert_tables(peer_u[l].astype(F32), peer_v[l].astype(F32))
    kx, vx = mem_kv(mem, row(ln_mem[l]), wk_x[l].astype(BF16), wv_x[l].astype(BF16))

    outs = []
    for bi in range(b):
        x2d = x[bi]
        p0, pqk, vt, pg = in_proj(x2d, row(ln_mix[l]), w_hg, w_qk, w_vt, w_g)
        ya = hgrn2(p0, row(lb_all[l]), row(hg_norm[l]), 1, s)
        km = moba_kmean(pqk, 1, s).reshape(1, nb, MB_WIDTH)
        ts = s // PEER_SLICES
        for tok0 in range(0, s, ts):
            yb = moba_attention(pqk, vt, km, bias, 1, s, tok0 // MB_BLOCK, ts // MB_BLOCK)
            xs = mix_out(x2d, ya, yb, pg, wa, wb, wo, tok0)
            xs = cross_attn(xs, row(ln_cross[l]), wqx, kx[bi:bi + 1], vx[bi:bi + 1], wox, ts)
            hp, eidx, wts = peer_route(xs, row(ln_ffn[l]), wpq, sk, 0, ts)
            y = peer_experts_pk_sc(tab_uv, eidx.reshape(ts * PEER_SEL), wts.reshape(ts * PEER_SEL), hp, d)
            outs.append(final_norm(xs, y, row(ln_final), 0))
    return jnp.concatenate(outs, axis=0).reshape(b, s, d)
```

```python
import functools
import math

import jax
import jax.numpy as jnp
from jax import lax
from jax.experimental import pallas as pl
from jax.experimental.pallas import tpu as pltpu
from jax.experimental.pallas import tpu_sc as plsc

F32 = jnp.float32
BF16 = jnp.bfloat16
I32 = jnp.int32
EPS = 1e-6
NEG_INF = float("-inf")

HG_HEADS = 4
HG_D = 128
HG_WIDTH = HG_HEADS * HG_D
HG_CHUNK = 64
HG_SUB = 16
MB_HEADS = 8
MB_DH = 64
MB_WIDTH = MB_HEADS * MB_DH
MB_BLOCK = 256
MB_TOPK = 3
MB_BIAS_TILES = 8
REL_BUCKETS = 32
REL_MAX_DIST = 2048
X_HEADS = 4
PEER_HEADS = 8
PEER_NKEYS = 128
PEER_TOPK = 16
PEER_HALF = 128
PEER_SEL = PEER_HEADS * PEER_TOPK
PEER_SLICES = 4

VMEM_LIMIT = 56 * 1024 * 1024


def _cparams(sem):
    return pltpu.CompilerParams(dimension_semantics=sem, vmem_limit_bytes=VMEM_LIMIT)


def _rms(x, g):
    ms = jnp.mean(x * x, axis=-1, keepdims=True)
    return x * lax.rsqrt(ms + EPS) * g


def _in_proj_kernel(x_ref, g_ref, w0_ref, w1_ref, wvt_ref, w2_ref, o0_ref, o1_ref, ovt_ref, o2_ref):
    h = _rms(x_ref[...], g_ref[...]).astype(BF16)
    o0_ref[...] = jnp.dot(h, w0_ref[...], preferred_element_type=F32)
    o1_ref[...] = jnp.dot(h, w1_ref[...], preferred_element_type=F32).astype(BF16)
    vt = lax.dot_general(wvt_ref[...], h, (((1,), (1,)), ((), ())), preferred_element_type=F32).astype(BF16)
    for hd in range(MB_HEADS):
        ovt_ref[0, hd * MB_VROWS:hd * MB_VROWS + MB_DH, :] = vt[hd * MB_DH:(hd + 1) * MB_DH]
        ovt_ref[0, hd * MB_VROWS + MB_DH:(hd + 1) * MB_VROWS, :] = jnp.ones((MB_ONES, vt.shape[1]), BF16)
    o2_ref[...] = jnp.dot(h, w2_ref[...], preferred_element_type=F32).astype(BF16)


def in_proj(x2d, g, w0, w1, wvt, w2):
    t, d = x2d.shape
    tm = MB_BLOCK
    assert wvt.shape[0] == MB_WIDTH
    n0, n1, nv, n2 = w0.shape[1], w1.shape[1], MB_VT_ROWS, w2.shape[1]
    full = lambda a: pl.BlockSpec(a.shape, lambda i: (0, 0))
    return pl.pallas_call(
        _in_proj_kernel,
        grid=(t // tm,),
        in_specs=[pl.BlockSpec((tm, d), lambda i: (i, 0)), full(g), full(w0), full(w1), full(wvt), full(w2)],
        out_specs=[pl.BlockSpec((tm, n0), lambda i: (i, 0)),
                   pl.BlockSpec((tm, n1), lambda i: (i, 0)),
                   pl.BlockSpec((1, nv, tm), lambda i: (i, 0, 0)),
                   pl.BlockSpec((tm, n2), lambda i: (i, 0))],
        out_shape=[jax.ShapeDtypeStruct((t, n0), F32),
                   jax.ShapeDtypeStruct((t, n1), BF16),
                   jax.ShapeDtypeStruct((t // tm, nv, tm), BF16),
                   jax.ShapeDtypeStruct((t, n2), BF16)],
        compiler_params=_cparams(("parallel",)),
        name="in_proj",
    )(x2d, g, w0, w1, wvt, w2)


def _hgrn_kernel(q_ref, f_ref, i_ref, g_ref, lb_ref, gain_ref, o_ref, st_ref):
    c = pl.program_id(1)

    @pl.when(c == 0)
    def _():
        st_ref[...] = jnp.zeros_like(st_ref)

    C, S = HG_CHUNK, HG_SUB
    row = lax.broadcasted_iota(I32, (C, C), 0)
    col = lax.broadcasted_iota(I32, (C, C), 1)
    tril = (row >= col).astype(F32)
    t_iota = lax.broadcasted_iota(I32, (S, 1), 0)

    for h in range(HG_HEADS):
        sl = slice(h * HG_D, (h + 1) * HG_D)
        q = q_ref[:, sl]
        v = i_ref[:, sl]
        lb = lb_ref[:, sl]
        f = lb + (1.0 - lb) * jax.nn.sigmoid(f_ref[:, sl])
        lf = jnp.log(f)
        k = 1.0 - f
        b = jnp.dot(tril, lf, precision=lax.Precision.HIGHEST, preferred_element_type=F32)
        st = st_ref[h]
        vb = v.astype(BF16)
        qd = (q * jnp.exp(b)).astype(BF16)
        o_inter = lax.dot_general(qd, st.astype(BF16), (((1,), (1,)), ((), ())),
                                  preferred_element_type=F32)
        outs = []
        for i in range(C // S):
            r0 = i * S
            qi = q[r0:r0 + S]
            ki = k[r0:r0 + S]
            bi = b[r0:r0 + S]
            vi = v[r0:r0 + S]
            oi = o_inter[r0:r0 + S]
            if i > 0:
                bs = b[r0 - 1:r0]
                qh = (qi * jnp.exp(bi - bs)).astype(BF16)
                kh = (k[:r0] * jnp.exp(bs - b[:r0])).astype(BF16)
                a = lax.dot_general(qh, kh, (((1,), (1,)), ((), ())), preferred_element_type=F32)
                oi = oi + jnp.dot(a.astype(BF16), vb[:r0], preferred_element_type=F32)
            half = S // 2
            o_half = [oi[:half], oi[half:]]
            for s in range(S):
                for hf in range(s // half, 2):
                    rows = slice(hf * half, (hf + 1) * half)
                    dec = jnp.exp(jnp.minimum(bi[rows] - bi[s:s + 1], 0.0))
                    a_s = jnp.sum(qi[rows] * ki[s:s + 1] * dec, axis=-1, keepdims=True)
                    a_s = jnp.where(t_iota[rows] >= s, a_s, 0.0)
                    o_half[hf] = o_half[hf] + a_s * vi[s:s + 1]
            outs.extend(o_half)
        o = jnp.concatenate(outs, axis=0)
        b_end = b[C - 1:C]
        kd = (k * jnp.exp(b_end - b)).astype(BF16)
        upd = lax.dot_general(vb, kd, (((0,), (0,)), ((), ())), preferred_element_type=F32)
        st_ref[h] = st * jnp.exp(b_end) + upd
        o = o * lax.rsqrt(jnp.mean(o * o, axis=-1, keepdims=True) + EPS)
        g = g_ref[:, sl]
        o_ref[:, sl] = (o * gain_ref[:, sl] * (g * jax.nn.sigmoid(g))).astype(o_ref.dtype)


def hgrn2(p0, lb, gain, batch, seq):
    t = p0.shape[0]
    nc = seq // HG_CHUNK
    w = HG_WIDTH

    def col(j):
        return pl.BlockSpec((HG_CHUNK, w), lambda b, c, j=j: (b * nc + c, j))

    return pl.pallas_call(
        _hgrn_kernel,
        grid=(batch, nc),
        in_specs=[col(0), col(1), col(2), col(3),
                  pl.BlockSpec((1, w), lambda b, c: (0, 0)),
                  pl.BlockSpec((1, w), lambda b, c: (0, 0))],
        out_specs=pl.BlockSpec((HG_CHUNK, w), lambda b, c: (b * nc + c, 0)),
        out_shape=jax.ShapeDtypeStruct((t, w), BF16),
        scratch_shapes=[pltpu.VMEM((HG_HEADS, HG_D, HG_D), F32)],
        compiler_params=_cparams(("parallel", "arbitrary")),
        name="hgrn2",
    )(p0, p0, p0, p0, lb, gain)


def _kmean_kernel(k_ref, o_ref):
    o_ref[0] = jnp.mean(k_ref[...].astype(F32), axis=0, keepdims=True)


def moba_kmean(p1, batch, seq):
    nbt = p1.shape[0] // MB_BLOCK
    return pl.pallas_call(
        _kmean_kernel,
        grid=(nbt,),
        in_specs=[pl.BlockSpec((MB_BLOCK, MB_WIDTH), lambda i: (i, 1))],
        out_specs=pl.BlockSpec((1, 1, MB_WIDTH), lambda i: (i, 0, 0)),
        out_shape=jax.ShapeDtypeStruct((nbt, 1, MB_WIDTH), F32),
        compiler_params=_cparams(("parallel",)),
        name="moba_kmean",
    )(p1)


MB_PAIR = 4
MB_PW = MB_PAIR * MB_DH
MB_LG = 128
MB_ONES = 16
MB_VROWS = MB_DH + MB_ONES
MB_VT_ROWS = MB_HEADS * MB_VROWS


def _moba_kernel(q_ref, k_ref, vt_ref, km_ref, bias_ref, o_ref, *scratch, qb0):
    m_ref, l_ref, al_ref, acc_ref, msk_ref, s_ref, p_ref = (
        scratch[i * MB_PAIR:(i + 1) * MB_PAIR] for i in range(7))
    qi = pl.program_id(2) + qb0
    nb = km_ref.shape[0]
    blk = MB_BLOCK
    heads = range(MB_PAIR)
    grp = lambda hh: slice((hh // 2) * MB_LG, (hh // 2 + 1) * MB_LG)
    q = q_ref[...]
    lane = lax.broadcasted_iota(I32, (blk, MB_LG), 1)
    in_head = [(lane < MB_DH) if hh % 2 == 0 else (lane >= MB_DH) for hh in heads]
    qs = q * jnp.asarray(MB_DH ** -0.5, BF16)
    nt = (((1,), (1,)), ((), ()))
    qf = q.astype(F32)
    qht = [jnp.where(in_head[hh], qs[:, grp(hh)].astype(F32), 0.0).T.astype(BF16) for hh in heads]

    n_io = lax.broadcasted_iota(I32, (nb, blk), 0)
    for hh in heads:
        gate = lax.dot_general(km_ref[:, grp(hh)], jnp.where(in_head[hh], qf[:, grp(hh)], 0.0), nt,
                               precision=lax.Precision.HIGHEST, preferred_element_type=F32)
        gate = jnp.where(n_io < qi, gate, NEG_INF)
        chosen = n_io < 0
        for _ in range(MB_TOPK):
            mx = jnp.max(gate, axis=0, keepdims=True)
            ix = jnp.min(jnp.where(gate == mx, n_io, nb), axis=0, keepdims=True)
            hit = n_io == ix
            chosen = chosen | (hit & (mx > NEG_INF))
            gate = jnp.where(hit, NEG_INF, gate)
        msk_ref[hh][...] = jnp.where(chosen, 0.0, NEG_INF)

    vrows = lambda hh: slice(hh * MB_VROWS, (hh + 1) * MB_VROWS)

    def pv_stage(blk_idx):
        vtb = vt_ref[blk_idx]
        r = [jnp.dot(vtb[vrows(hh)], p_ref[hh][...], preferred_element_type=F32) for hh in heads]
        al = [al_ref[hh][...] for hh in heads]
        a_new = [al[hh] * acc_ref[hh][...] + r[hh][:MB_DH] for hh in heads]
        l_new = [al[hh] * l_ref[hh][...] + r[hh][MB_DH:MB_DH + 1] for hh in heads]
        return a_new, l_new

    def store_pv(a_new, l_new):
        for hh in heads:
            acc_ref[hh][...] = a_new[hh]
            l_ref[hh][...] = l_new[hh]

    def softmax_stage():
        s = [s_ref[hh][...] for hh in heads]
        m_old = [m_ref[hh][...] for hh in heads]
        m_new = [jnp.maximum(m_old[hh], jnp.max(s[hh], axis=0, keepdims=True)) for hh in heads]
        alpha = [jnp.exp(m_old[hh] - m_new[hh]) for hh in heads]
        p = [jnp.exp((s[hh] - m_new[hh]).astype(BF16)) for hh in heads]
        return p, alpha, m_new

    def store_softmax(p, alpha, m_new):
        for hh in heads:
            p_ref[hh][...] = p[hh]
            al_ref[hh][...] = alpha[hh]
            m_ref[hh][...] = m_new[hh]

    k_own = k_ref[pl.ds(pl.multiple_of(qi * blk, blk), blk), :]
    key_io = lax.broadcasted_iota(I32, (blk, blk), 0)
    qry_io = lax.broadcasted_iota(I32, (blk, blk), 1)
    for hh in heads:
        s = jnp.dot(k_own[:, grp(hh)], qht[hh], preferred_element_type=F32) + bias_ref[hh, 0]
        s_ref[hh][...] = jnp.where(key_io <= qry_io, s, NEG_INF)
        m_ref[hh][...] = jnp.full((1, blk), NEG_INF, F32)
        l_ref[hh][...] = jnp.zeros((1, blk), F32)
        al_ref[hh][...] = jnp.ones((1, blk), F32)
        acc_ref[hh][...] = jnp.zeros((MB_DH, blk), F32)
        p_ref[hh][...] = jnp.zeros((blk, blk), BF16)

    def step(i, carry, far):
        pv = pv_stage(jnp.where(i <= 1, qi, i - 2))
        sm = softmax_stage()
        kn = k_ref[pl.ds(pl.multiple_of(i * blk, blk), blk), :]
        if far:
            row = [msk_ref[hh][pl.ds(i, 1), :] + bias_ref[hh, MB_BIAS_TILES - 1, 0:1, 0:1] for hh in heads]
            s_next = [jnp.dot(kn[:, grp(hh)], qht[hh], preferred_element_type=F32) + row[hh] for hh in heads]
        else:
            d = qi - i
            s_next = [jnp.dot(kn[:, grp(hh)], qht[hh], preferred_element_type=F32)
                      + bias_ref[hh, d] + msk_ref[hh][pl.ds(i, 1), :] for hh in heads]
        store_pv(*pv)
        for hh in heads:
            s_ref[hh][...] = s_next[hh]
        store_softmax(*sm)
        return carry

    n_far = jnp.maximum(qi - (MB_BIAS_TILES - 2), 0)
    lax.fori_loop(0, n_far, functools.partial(step, far=True), 0)
    lax.fori_loop(n_far, qi, functools.partial(step, far=False), 0)
    pv = pv_stage(jnp.where(qi <= 1, qi, qi - 2))
    sm = softmax_stage()
    store_pv(*pv)
    store_softmax(*sm)
    a_fin, l_fin = pv_stage(jnp.where(qi == 0, qi, qi - 1))
    out_t = jnp.concatenate([a_fin[hh] / l_fin[hh] for hh in heads], axis=0)
    o_ref[...] = out_t.T.astype(o_ref.dtype)


def moba_attention(pqk, vt, km, bias, batch, seq, qb0=0, nqb=None):
    nb = seq // MB_BLOCK
    nqb = nb if nqb is None else nqb
    t = batch * nqb * MB_BLOCK
    groups = MB_WIDTH // MB_PW
    return pl.pallas_call(
        functools.partial(_moba_kernel, qb0=qb0),
        grid=(batch, groups, nqb),
        in_specs=[
            pl.BlockSpec((MB_BLOCK, MB_PW), lambda b, j, i: (b * nb + qb0 + i, j)),
            pl.BlockSpec((seq, MB_PW), lambda b, j, i: (b, groups + j)),
            pl.BlockSpec((nb, MB_PAIR * MB_VROWS, MB_BLOCK), lambda b, j, i: (b, j, 0)),
            pl.BlockSpec((None, nb, MB_PW), lambda b, j, i: (b, 0, j)),
            pl.BlockSpec((MB_PAIR, MB_BIAS_TILES, MB_BLOCK, MB_BLOCK), lambda b, j, i: (j, 0, 0, 0)),
        ],
        out_specs=pl.BlockSpec((MB_BLOCK, MB_PW), lambda b, j, i: (b * nqb + i, j)),
        out_shape=jax.ShapeDtypeStruct((t, MB_WIDTH), BF16),
        scratch_shapes=(
            [pltpu.VMEM((1, MB_BLOCK), F32)] * (3 * MB_PAIR)
            + [pltpu.VMEM((MB_DH, MB_BLOCK), F32)] * MB_PAIR
            + [pltpu.VMEM((nb, MB_BLOCK), F32)] * MB_PAIR
            + [pltpu.VMEM((MB_BLOCK, MB_BLOCK), F32)] * MB_PAIR
            + [pltpu.VMEM((MB_BLOCK, MB_BLOCK), BF16)] * MB_PAIR
        ),
        compiler_params=_cparams(("parallel", "parallel", "arbitrary")),
        name="moba_attn",
    )(pqk, pqk, vt, km, bias)


def _t5_bucket(dist):
    max_exact = REL_BUCKETS // 2
    scaled = jnp.log(jnp.maximum(dist, 1).astype(F32) / max_exact) / math.log(REL_MAX_DIST / max_exact)
    large = jnp.minimum(max_exact + (scaled * (REL_BUCKETS - max_exact)).astype(I32), REL_BUCKETS - 1)
    return jnp.where(dist < max_exact, dist, large)


def moba_bias_tiles(rel_bias):
    blk = MB_BLOCK
    span = 2 * blk - 1
    x = jnp.arange(span) - (blk - 1)
    dist = jnp.maximum(jnp.arange(MB_BIAS_TILES)[:, None] * blk + x[None, :], 0)
    w = rel_bias.astype(F32).T[:, _t5_bucket(dist)]
    h = w.shape[0]
    wp = jnp.pad(w, ((0, 0), (0, 0), (0, 1)))
    a = jnp.broadcast_to(wp[:, :, None, :], (h, MB_BIAS_TILES, blk, span + 1))
    a = a.reshape(h, MB_BIAS_TILES, blk * (span + 1))[:, :, :blk * span]
    return a.reshape(h, MB_BIAS_TILES, blk, span)[:, :, :, blk - 1:]


def _mix_kernel(x_ref, ya_ref, yb_ref, ga_ref, gb_ref, wa_ref, wb_ref, wo_ref, o_ref):
    za = jnp.dot(ya_ref[...], wa_ref[...], preferred_element_type=F32)
    zb = jnp.dot(yb_ref[...], wb_ref[...], preferred_element_type=F32)
    z = jax.nn.sigmoid(ga_ref[...].astype(F32)) * za + jax.nn.sigmoid(gb_ref[...].astype(F32)) * zb
    o_ref[...] = x_ref[...] + jnp.dot(z.astype(BF16), wo_ref[...], preferred_element_type=F32)


def mix_out(x2d, ya, yb, pg, wa, wb, wo, tok0=0, tm=512):
    t = yb.shape[0]
    assert t % tm == 0 and tok0 % tm == 0
    d = x2d.shape[1]
    w = ya.shape[1]
    b0 = tok0 // tm
    return pl.pallas_call(
        _mix_kernel,
        grid=(t // tm,),
        in_specs=[
            pl.BlockSpec((tm, d), lambda i: (b0 + i, 0)),
            pl.BlockSpec((tm, w), lambda i: (b0 + i, 0)),
            pl.BlockSpec((tm, w), lambda i: (i, 0)),
            pl.BlockSpec((tm, d), lambda i: (b0 + i, 0)),
            pl.BlockSpec((tm, d), lambda i: (b0 + i, 1)),
            pl.BlockSpec((w, d), lambda i: (0, 0)),
            pl.BlockSpec((w, d), lambda i: (0, 0)),
            pl.BlockSpec((d, d), lambda i: (0, 0)),
        ],
        out_specs=pl.BlockSpec((tm, d), lambda i: (i, 0)),
        out_shape=jax.ShapeDtypeStruct((t, d), F32),
        compiler_params=_cparams(("parallel",)),
        name="mix_out",
    )(x2d, ya, yb, pg, pg, wa, wb, wo)


def _mem_kv_kernel(m_ref, g_ref, wk_ref, wv_ref, k_ref, v_ref):
    mn = _rms(m_ref[...], g_ref[...]).astype(BF16)
    k_ref[...] = jnp.dot(mn, wk_ref[...], preferred_element_type=F32).astype(BF16)
    v_ref[...] = jnp.dot(mn, wv_ref[...], preferred_element_type=F32).astype(BF16)


def mem_kv(mem, g, wk, wv):
    b, m, d = mem.shape
    spec = pl.BlockSpec((None, m, d), lambda i: (i, 0, 0))
    wspec = pl.BlockSpec((d, d), lambda i: (0, 0))
    return pl.pallas_call(
        _mem_kv_kernel,
        grid=(b,),
        in_specs=[spec, pl.BlockSpec((1, d), lambda i: (0, 0)), wspec, wspec],
        out_specs=[spec, spec],
        out_shape=[jax.ShapeDtypeStruct((b, m, d), BF16)] * 2,
        compiler_params=_cparams(("parallel",)),
        name="mem_kv",
    )(mem, g, wk, wv)


def _cross_kernel(x_ref, g_ref, wq_ref, k_ref, v_ref, wo_ref, o_ref):
    x = x_ref[...]
    d = x.shape[1]
    dh = d // X_HEADS
    h = _rms(x, g_ref[...]).astype(BF16)
    q = (jnp.dot(h, wq_ref[...], preferred_element_type=F32) * (dh ** -0.5)).astype(BF16)
    outs = []
    for hh in range(X_HEADS):
        sl = slice(hh * dh, (hh + 1) * dh)
        s = lax.dot_general(q[:, sl], k_ref[:, sl], (((1,), (1,)), ((), ())),
                            preferred_element_type=F32)
        p = jnp.exp(s - jnp.max(s, axis=1, keepdims=True))
        l = jnp.sum(p, axis=1, keepdims=True)
        o = jnp.dot(p.astype(BF16), v_ref[:, sl], preferred_element_type=F32) / l
        outs.append(o.astype(BF16))
    o = jnp.concatenate(outs, axis=1)
    o_ref[...] = x + jnp.dot(o, wo_ref[...], preferred_element_type=F32)


def cross_attn(x2d, g, wq, kx, vx, wo, seq, tm=512):
    t, d = x2d.shape
    assert seq % tm == 0 and t % seq == 0
    m = kx.shape[1]
    per_b = seq // tm
    kv = pl.BlockSpec((None, m, d), lambda i: (i // per_b, 0, 0))
    wspec = pl.BlockSpec((d, d), lambda i: (0, 0))
    return pl.pallas_call(
        _cross_kernel,
        grid=(t // tm,),
        in_specs=[pl.BlockSpec((tm, d), lambda i: (i, 0)), pl.BlockSpec((1, d), lambda i: (0, 0)),
                  wspec, kv, kv, wspec],
        out_specs=pl.BlockSpec((tm, d), lambda i: (i, 0)),
        out_shape=jax.ShapeDtypeStruct((t, d), F32),
        compiler_params=_cparams(("parallel",)),
        name="cross_attn",
    )(x2d, g, wq, kx, vx, wo)


def _topk_rows(sc, k):
    n = sc.shape[0]
    io = lax.broadcasted_iota(I32, sc.shape, 0)
    vals, ids = [], []
    for _ in range(k):
        m = jnp.max(sc, axis=0, keepdims=True)
        ix = jnp.min(jnp.where(sc == m, io, n), axis=0, keepdims=True)
        vals.append(m)
        ids.append(ix)
        sc = jnp.where(io == ix, NEG_INF, sc)
    return jnp.concatenate(vals, axis=0), jnp.concatenate(ids, axis=0)


def _pack_bf16_halves(h):
    bits = lax.bitcast_convert_type(h, I32)
    r = bits + 0x7FFF + (lax.shift_right_logical(bits, 16) & 1)
    half = h.shape[1] // 2
    return lax.shift_right_logical(r[:, :half], 16) | (r[:, half:] & HI_MASK)


def _route_kernel(x_ref, g_ref, wq_ref, sk_ref, hp_ref, idx_ref, w_ref, hb_ref, it_ref, wt_ref):
    p = pl.program_id(1)

    @pl.when(p == 0)
    def _():
        h = _rms(x_ref[...], g_ref[...])
        hp_ref[...] = _pack_bf16_halves(h)
        hb_ref[...] = h.astype(BF16)

    qh = jnp.dot(hb_ref[...], wq_ref[...], preferred_element_type=F32)
    tops = []
    for c in range(2):
        seg = qh[:, c * PEER_HALF:(c + 1) * PEER_HALF]
        sc = lax.dot_general(sk_ref[c], seg, (((1,), (1,)), ((), ())),
                             precision=lax.Precision.HIGHEST, preferred_element_type=F32)
        tops.append(_topk_rows(sc, PEER_TOPK))
    (s0, i0), (s1, i1) = tops
    k = PEER_TOPK
    sub = 8
    tm = s0.shape[1]
    r8 = lax.broadcasted_iota(I32, (sub, tm), 0)
    r16 = lax.broadcasted_iota(I32, (k, tm), 0)
    cand_b = [s0[0:1] + s1, s0[1:2] + s1[:sub]]
    cidx_b = [i0[0:1] * PEER_NKEYS + i1, i0[1:2] * PEER_NKEYS + i1[:sub]]
    pos_b = [r16, k + r8]
    for a in range(2, sub):
        keep = r8 < (k // (a + 1))
        cand_b.append(jnp.where(keep, s0[a:a + 1] + s1[:sub], NEG_INF))
        cidx_b.append(i0[a:a + 1] * PEER_NKEYS + i1[:sub])
        pos_b.append(a * k + r8)
    cand_b.append(s0[sub:] + s1[0:1])
    cidx_b.append(i0[sub:] * PEER_NKEYS + i1[0:1])
    pos_b.append((sub + r8) * k)
    cand = jnp.concatenate(cand_b, axis=0)
    cidx = jnp.concatenate(cidx_b, axis=0)
    pos = jnp.concatenate(pos_b, axis=0)
    vals, ids = [], []
    for _ in range(k):
        m = jnp.max(cand, axis=0, keepdims=True)
        px = jnp.min(jnp.where(cand == m, pos, k * k), axis=0, keepdims=True)
        hit = pos == px
        vals.append(m)
        ids.append(jnp.sum(jnp.where(hit, cidx, 0), axis=0, keepdims=True))
        cand = jnp.where(hit, NEG_INF, cand)
    sf = jnp.concatenate(vals, axis=0)
    e = jnp.exp(sf - sf[0:1])
    rows = pl.ds(pl.multiple_of(p * PEER_TOPK, PEER_TOPK), PEER_TOPK)
    wt_ref[rows, :] = e / jnp.sum(e, axis=0, keepdims=True)
    it_ref[rows, :] = jnp.concatenate(ids, axis=0)

    @pl.when(p == pl.num_programs(1) - 1)
    def _():
        idx_ref[...] = it_ref[...].T
        w_ref[...] = wt_ref[...].T


def peer_route(x2d, g, wq, sk, tok0, t, tm=1024):
    assert t % tm == 0 and tok0 % tm == 0
    d = x2d.shape[1]
    ph = sk.shape[0]
    nsel = ph * PEER_TOPK
    blk0 = tok0 // tm
    return pl.pallas_call(
        _route_kernel,
        grid=(t // tm, ph),
        in_specs=[
            pl.BlockSpec((tm, d), lambda i, p: (blk0 + i, 0)),
            pl.BlockSpec((1, d), lambda i, p: (0, 0)),
            pl.BlockSpec((d, 2 * PEER_HALF), lambda i, p: (0, p)),
            pl.BlockSpec((None, 2, PEER_NKEYS, PEER_HALF), lambda i, p: (p, 0, 0, 0)),
        ],
        out_specs=[
            pl.BlockSpec((tm, d // 2), lambda i, p: (i, 0)),
            pl.BlockSpec((tm, nsel), lambda i, p: (i, 0)),
            pl.BlockSpec((tm, nsel), lambda i, p: (i, 0)),
        ],
        out_shape=[jax.ShapeDtypeStruct((t, d // 2), I32),
                   jax.ShapeDtypeStruct((t, nsel), I32),
                   jax.ShapeDtypeStruct((t, nsel), F32)],
        scratch_shapes=[pltpu.VMEM((tm, d), BF16),
                        pltpu.VMEM((nsel, tm), I32),
                        pltpu.VMEM((nsel, tm), F32)],
        compiler_params=_cparams(("parallel", "arbitrary")),
        name="peer_route",
    )(x2d, g, wq, sk)


def _final_kernel(x_ref, y_ref, g_ref, o_ref):
    o_ref[...] = _rms(x_ref[...] + y_ref[...], g_ref[...])


def final_norm(x2d, y, g, tok0, tm=512):
    t, d = y.shape
    assert t % tm == 0 and tok0 % tm == 0
    blk0 = tok0 // tm
    spec = pl.BlockSpec((tm, d), lambda i: (i, 0))
    return pl.pallas_call(
        _final_kernel, grid=(t // tm,),
        in_specs=[pl.BlockSpec((tm, d), lambda i: (blk0 + i, 0)), spec, pl.BlockSpec((1, d), lambda i: (0, 0))],
        out_specs=spec,
        out_shape=jax.ShapeDtypeStruct((t, d), F32),
        compiler_params=_cparams(("parallel",)), name="final_norm",
    )(x2d, y, g)


SC_CORES = 2
SC_SUBCORES = 16
SC_WORKERS = SC_CORES * SC_SUBCORES
SC_LANES = 16
SC_GROUP = 16


def _sc_mesh():
    return plsc.VectorSubcoreMesh(core_axis_name="c", subcore_axis_name="s")


def _sc_params():
    return pltpu.CompilerParams(needs_layout_passes=False)


def _sc_worker_id():
    return lax.axis_index("s") * SC_CORES + lax.axis_index("c")


SC_ROW_LANE = 128


def _sc_unit_off(u):
    off = u * SC_LANES
    return off if isinstance(off, int) else pl.multiple_of(off, SC_LANES)


GELU_C0 = math.sqrt(2.0 / math.pi)
GELU_C1 = 0.044715


def _gelu_tanh(x):
    z = GELU_C0 * (x + GELU_C1 * (x * x * x))
    th = 1.0 - 2.0 / (jnp.exp(2.0 * z) + 1.0)
    return 0.5 * x * (1.0 + th)


SC_PK_RING = 6
SC_PK_SUB = 4
HI_MASK = -65536


def _pack_tables_kernel(u_ref, v_ref, o_ref):
    for part, ref in enumerate((u_ref, v_ref)):
        words = _pack_bf16_halves(ref[...])
        for sub in range(SC_PK_SUB):
            o_ref[:, part * SC_PK_SUB + sub, :] = words[:, sub * SC_ROW_LANE:(sub + 1) * SC_ROW_LANE]


def pack_expert_tables(u, v, te=512):
    e, d = u.shape
    assert d == 2 * SC_PK_SUB * SC_ROW_LANE
    spec = pl.BlockSpec((te, d), lambda i: (i, 0))
    return pl.pallas_call(
        _pack_tables_kernel, grid=(e // te,), in_specs=[spec, spec],
        out_specs=pl.BlockSpec((te, 2 * SC_PK_SUB, SC_ROW_LANE), lambda i: (i, 0, 0)),
        out_shape=jax.ShapeDtypeStruct((e, 2 * SC_PK_SUB, SC_ROW_LANE), I32),
        compiler_params=_cparams(("parallel",)), name="pack_expert_tables",
    )(u, v)


def _unpack_halves(x32):
    w = plsc.bitcast(x32, I32)
    return plsc.bitcast(w << 16, F32), plsc.bitcast(w & HI_MASK, F32)


def _tree_sum(xs):
    while len(xs) > 1:
        xs = [xs[i] + xs[i + 1] for i in range(0, len(xs), 2)]
    return xs[0]


def peer_experts_pk_sc(tab_uv, idx_flat, w_flat, hp, d):
    t = hp.shape[0]
    nsel = PEER_SEL
    g = SC_GROUP
    assert t % (SC_WORKERS * g) == 0 and d == 2 * SC_PK_SUB * SC_ROW_LANE
    tpw = t // SC_WORKERS
    groups = tpw // g
    heads = nsel // SC_LANES
    chunks = d // 32
    units = g * heads
    ring = SC_PK_RING
    row_buf = pltpu.VMEM((SC_LANES, 2 * SC_PK_SUB, SC_ROW_LANE), I32)

    def row_words(rows, r, wc, sub0):
        per = SC_ROW_LANE // SC_LANES
        return plsc.bitcast(
            rows[r, sub0 + wc // per, pl.ds(pl.multiple_of((wc % per) * SC_LANES, SC_LANES), SC_LANES)], BF16)

    def ring_loop(n_units, start, wait, compute):
        for u in range(ring - 1):
            start(u, u)
        full = n_units // ring * ring

        @pl.loop(0, full, step=ring)
        def _(uu):
            for b in range(ring):
                u = uu + b
                nxt = u + (ring - 1)

                @pl.when(nxt < n_units)
                def _():
                    start(nxt, (b + ring - 1) % ring)

                wait(u, b)
                compute(u, b)

        for u in range(full, n_units):
            wait(u, u % ring)
            compute(u, u % ring)

    @functools.partial(
        pl.kernel, mesh=_sc_mesh(),
        out_type=jax.ShapeDtypeStruct((t, d), F32),
        scratch_types=[
            pltpu.VMEM((g * nsel,), I32),
            pltpu.VMEM((g * nsel,), F32),
            pltpu.VMEM((g, d // 2), I32),
            pltpu.VMEM((g, d), F32),
            pltpu.VMEM((SC_LANES * SC_LANES,), F32),
            [row_buf] * ring,
            [pltpu.SemaphoreType.DMA] * ring,
        ],
        compiler_params=_sc_params(),
        name="peer_experts_pk_sc",
    )
    def k(tab_hbm, idx_hbm, w_hbm, h_hbm, out_hbm, idx_v, coef_v, h_v, y_v, red_v, rows, sems):
        wid = _sc_worker_id()
        lane = lax.iota(I32, SC_LANES)

        def copy(u, slot):
            ids = idx_v.at[pl.ds(_sc_unit_off(u), SC_LANES)]
            return pltpu.make_async_copy(tab_hbm.at[ids], rows[slot], sems[slot])

        def dots(u, slot):
            tt = u // heads

            def body(cp, accs):
                out = []
                hv = [plsc.bitcast(h_v[tt, pl.ds(pl.multiple_of((2 * cp + i) * SC_LANES, SC_LANES), SC_LANES)], BF16)
                      for i in range(2)]
                for r in range(SC_LANES):
                    pr = (row_words(rows[slot], r, 2 * cp, 0) * hv[0]
                          + row_words(rows[slot], r, 2 * cp + 1, 0) * hv[1])
                    lo, hi = _unpack_halves(pr)
                    out.append(accs[r] + lo + hi)
                return tuple(out)

            accs = lax.fori_loop(0, chunks // 2, body,
                                 tuple(jnp.zeros((SC_LANES,), F32) for _ in range(SC_LANES)))
            for r in range(SC_LANES):
                red_v[pl.ds(r * SC_LANES, SC_LANES)] = accs[r]
            act = _tree_sum([plsc.load_gather(red_v, [lane * SC_LANES + j]) for j in range(SC_LANES)])
            sl = pl.ds(_sc_unit_off(u), SC_LANES)
            coef_v[sl] = coef_v[sl] * _gelu_tanh(act)

        def combine(u, slot):
            tt = u // heads
            first = (u % heads) == 0
            cb = []
            for r in range(SC_LANES):
                c = plsc.load_gather(coef_v, [jnp.full((SC_LANES,), u * SC_LANES + r, I32)])
                cb.append(plsc.pack(c, c, format=plsc.PackFormat.INTERLEAVED))

            @plsc.parallel_loop(0, chunks, unroll=2)
            def _(wc):
                lo, hi = _unpack_halves(
                    _tree_sum([cb[r] * row_words(rows[slot], r, wc, SC_PK_SUB) for r in range(SC_LANES)]))
                for half, val in ((0, lo), (1, hi)):
                    sl = pl.ds(pl.multiple_of(half * (d // 2) + wc * SC_LANES, SC_LANES), SC_LANES)
                    y_v[tt, sl] = val + jnp.where(first, 0.0, y_v[tt, sl])

        def unit(u, slot):
            dots(u, slot)
            combine(u, slot)

        @pl.loop(0, groups)
        def _(gi):
            base = wid * tpw + gi * g
            pltpu.sync_copy(idx_hbm.at[pl.ds(base * nsel, g * nsel)], idx_v)
            pltpu.sync_copy(w_hbm.at[pl.ds(base * nsel, g * nsel)], coef_v)
            pltpu.sync_copy(h_hbm.at[pl.ds(base, g)], h_v)
            ring_loop(units, lambda u, s: copy(u, s).start(), lambda u, s: copy(u, s).wait(), unit)
            pltpu.sync_copy(y_v, out_hbm.at[pl.ds(base, g)])

    return k(tab_uv, idx_flat, w_flat, hp)


def kernel(x, mem, rel_bias, ln_mix, w_in, hg_lower, hg_norm, w_up_a, w_up_b, w_out, ln_cross, ln_mem, wq_x, wk_x, wv_x, wo_x, ln_ffn, peer_query, peer_subkeys, peer_u, peer_v, ln_final):
    b, s, d = x.shape
    depth = w_in.shape[0]
    assert depth == 1, "the residual after PEER is fused into the final norm"
    assert s % MB_BLOCK == 0 and s % HG_CHUNK == 0 and s % (PEER_SLICES * SC_WORKERS * SC_GROUP) == 0
    nb = s // MB_BLOCK
    row = lambda a: a.reshape(1, -1).astype(F32)
    lb_all = jnp.cumsum(jax.nn.softmax(hg_lower.astype(F32), axis=0), axis=0)
    bias = moba_bias_tiles(rel_bias)
    n_hg = 4 * HG_WIDTH
    n_qk = 2 * MB_WIDTH
    n_mb = 3 * MB_WIDTH
    l = 0
    w = w_in[l].astype(BF16)
    w_hg, w_qk, w_vt, w_g = w[:, :n_hg], w[:, n_hg:n_hg + n_qk], w[:, n_hg + n_qk:n_hg + n_mb].T, w[:, n_hg + n_mb:]
    wa, wb, wo = w_up_a[l].astype(BF16), w_up_b[l].astype(BF16), w_out[l].astype(BF16)
    wqx, wox = wq_x[l].astype(BF16), wo_x[l].astype(BF16)
    wpq, sk = peer_query[l].astype(BF16), peer_subkeys[l].astype(F32)
    tab_uv = pack_expert_tables(peer_u[l].astype(F32), peer_v[l].astype(F32))
    kx, vx = mem_kv(mem, row(ln_mem[l]), wk_x[l].astype(BF16), wv_x[l].astype(BF16))

    outs = []
    for bi in range(b):
        x2d = x[bi]
        p0, pqk, vt, pg = in_proj(x2d, row(ln_mix[l]), w_hg, w_qk, w_vt, w_g)
        ya = hgrn2(p0, row(lb_all[l]), row(hg_norm[l]), 1, s)
        km = moba_kmean(pqk, 1, s).reshape(1, nb, MB_WIDTH)
        ts = s // PEER_SLICES
        for tok0 in range(0, s, ts):
            yb = moba_attention(pqk, vt, km, bias, 1, s, tok0 // MB_BLOCK, ts // MB_BLOCK)
            xs = mix_out(x2d, ya, yb, pg, wa, wb, wo, tok0)
            xs = cross_attn(xs, row(ln_cross[l]), wqx, kx[bi:bi + 1], vx[bi:bi + 1], wox, ts)
            hp, eidx, wts = peer_route(xs, row(ln_ffn[l]), wpq, sk, 0, ts)
            y = peer_experts_pk_sc(tab_uv, eidx.reshape(ts * PEER_SEL), wts.reshape(ts * PEER_SEL), hp, d)
            outs.append(final_norm(xs, y, row(ln_final), 0))
    return jnp.concatenate(outs, axis=0).reshape(b, s, d)
```

```python
import functools
import math

import jax
import jax.numpy as jnp
from jax import lax
from jax.experimental import pallas as pl
from jax.experimental.pallas import tpu as pltpu
from jax.experimental.pallas import tpu_sc as plsc

F32 = jnp.float32
BF16 = jnp.bfloat16
I32 = jnp.int32
EPS = 1e-6
NEG_INF = float("-inf")

HG_HEADS = 4
HG_D = 128
HG_WIDTH = HG_HEADS * HG_D
HG_CHUNK = 64
HG_SUB = 16
MB_HEADS = 8
MB_DH = 64
MB_WIDTH = MB_HEADS * MB_DH
MB_BLOCK = 256
MB_TOPK = 3
MB_BIAS_TILES = 8
REL_BUCKETS = 32
REL_MAX_DIST = 2048
X_HEADS = 4
PEER_HEADS = 8
PEER_NKEYS = 128
PEER_TOPK = 16
PEER_HALF = 128
PEER_SEL = PEER_HEADS * PEER_TOPK
PEER_SLICES = 4

VMEM_LIMIT = 56 * 1024 * 1024


def _cparams(sem):
    return pltpu.CompilerParams(dimension_semantics=sem, vmem_limit_bytes=VMEM_LIMIT)


def _rms(x, g):
    ms = jnp.mean(x * x, axis=-1, keepdims=True)
    return x * lax.rsqrt(ms + EPS) * g


def _in_proj_kernel(x_ref, g_ref, w0_ref, w1_ref, wvt_ref, w2_ref, o0_ref, o1_ref, ovt_ref, o2_ref):
    h = _rms(x_ref[...], g_ref[...]).astype(BF16)
    o0_ref[...] = jnp.dot(h, w0_ref[...], preferred_element_type=F32)
    o1_ref[...] = jnp.dot(h, w1_ref[...], preferred_element_type=F32).astype(BF16)
    vt = lax.dot_general(wvt_ref[...], h, (((1,), (1,)), ((), ())), preferred_element_type=F32).astype(BF16)
    for hd in range(MB_HEADS):
        ovt_ref[0, hd * MB_VROWS:hd * MB_VROWS + MB_DH, :] = vt[hd * MB_DH:(hd + 1) * MB_DH]
        ovt_ref[0, hd * MB_VROWS + MB_DH:(hd + 1) * MB_VROWS, :] = jnp.ones((MB_ONES, vt.shape[1]), BF16)
    o2_ref[...] = jnp.dot(h, w2_ref[...], preferred_element_type=F32).astype(BF16)


def in_proj(x2d, g, w0, w1, wvt, w2):
    t, d = x2d.shape
    tm = MB_BLOCK
    assert wvt.shape[0] == MB_WIDTH
    n0, n1, nv, n2 = w0.shape[1], w1.shape[1], MB_VT_ROWS, w2.shape[1]
    full = lambda a: pl.BlockSpec(a.shape, lambda i: (0, 0))
    return pl.pallas_call(
        _in_proj_kernel,
        grid=(t // tm,),
        in_specs=[pl.BlockSpec((tm, d), lambda i: (i, 0)), full(g), full(w0), full(w1), full(wvt), full(w2)],
        out_specs=[pl.BlockSpec((tm, n0), lambda i: (i, 0)),
                   pl.BlockSpec((tm, n1), lambda i: (i, 0)),
                   pl.BlockSpec((1, nv, tm), lambda i: (i, 0, 0)),
                   pl.BlockSpec((tm, n2), lambda i: (i, 0))],
        out_shape=[jax.ShapeDtypeStruct((t, n0), F32),
                   jax.ShapeDtypeStruct((t, n1), BF16),
                   jax.ShapeDtypeStruct((t // tm, nv, tm), BF16),
                   jax.ShapeDtypeStruct((t, n2), BF16)],
        compiler_params=_cparams(("parallel",)),
        name="in_proj",
    )(x2d, g, w0, w1, wvt, w2)


def _hgrn_kernel(q_ref, f_ref, i_ref, g_ref, lb_ref, gain_ref, o_ref, st_ref):
    c = pl.program_id(1)

    @pl.when(c == 0)
    def _():
        st_ref[...] = jnp.zeros_like(st_ref)

    C, S = HG_CHUNK, HG_SUB
    row = lax.broadcasted_iota(I32, (C, C), 0)
    col = lax.broadcasted_iota(I32, (C, C), 1)
    tril = (row >= col).astype(F32)
    t_iota = lax.broadcasted_iota(I32, (S, 1), 0)

    for h in range(HG_HEADS):
        sl = slice(h * HG_D, (h + 1) * HG_D)
        q = q_ref[:, sl]
        v = i_ref[:, sl]
        lb = lb_ref[:, sl]
        f = lb + (1.0 - lb) * jax.nn.sigmoid(f_ref[:, sl])
        lf = jnp.log(f)
        k = 1.0 - f
        b = jnp.dot(tril, lf, precision=lax.Precision.HIGHEST, preferred_element_type=F32)
        st = st_ref[h]
        vb = v.astype(BF16)
        qd = (q * jnp.exp(b)).astype(BF16)
        o_inter = lax.dot_general(qd, st.astype(BF16), (((1,), (1,)), ((), ())),
                                  preferred_element_type=F32)
        outs = []
        for i in range(C // S):
            r0 = i * S
            qi = q[r0:r0 + S]
            ki = k[r0:r0 + S]
            bi = b[r0:r0 + S]
            vi = v[r0:r0 + S]
            oi = o_inter[r0:r0 + S]
            if i > 0:
                bs = b[r0 - 1:r0]
                qh = (qi * jnp.exp(bi - bs)).astype(BF16)
                kh = (k[:r0] * jnp.exp(bs - b[:r0])).astype(BF16)
                a = lax.dot_general(qh, kh, (((1,), (1,)), ((), ())), preferred_element_type=F32)
                oi = oi + jnp.dot(a.astype(BF16), vb[:r0], preferred_element_type=F32)
            half = S // 2
            o_half = [oi[:half], oi[half:]]
            for s in range(S):
                for hf in range(s // half, 2):
                    rows = slice(hf * half, (hf + 1) * half)
                    dec = jnp.exp(jnp.minimum(bi[rows] - bi[s:s + 1], 0.0))
                    a_s = jnp.sum(qi[rows] * ki[s:s + 1] * dec, axis=-1, keepdims=True)
                    a_s = jnp.where(t_iota[rows] >= s, a_s, 0.0)
                    o_half[hf] = o_half[hf] + a_s * vi[s:s + 1]
            outs.extend(o_half)
        o = jnp.concatenate(outs, axis=0)
        b_end = b[C - 1:C]
        kd = (k * jnp.exp(b_end - b)).astype(BF16)
        upd = lax.dot_general(vb, kd, (((0,), (0,)), ((), ())), preferred_element_type=F32)
        st_ref[h] = st * jnp.exp(b_end) + upd
        o = o * lax.rsqrt(jnp.mean(o * o, axis=-1, keepdims=True) + EPS)
        g = g_ref[:, sl]
        o_ref[:, sl] = (o * gain_ref[:, sl] * (g * jax.nn.sigmoid(g))).astype(o_ref.dtype)


def hgrn2(p0, lb, gain, batch, seq):
    t = p0.shape[0]
    nc = seq // HG_CHUNK
    w = HG_WIDTH

    def col(j):
        return pl.BlockSpec((HG_CHUNK, w), lambda b, c, j=j: (b * nc + c, j))

    return pl.pallas_call(
        _hgrn_kernel,
        grid=(batch, nc),
        in_specs=[col(0), col(1), col(2), col(3),
                  pl.BlockSpec((1, w), lambda b, c: (0, 0)),
                  pl.BlockSpec((1, w), lambda b, c: (0, 0))],
        out_specs=pl.BlockSpec((HG_CHUNK, w), lambda b, c: (b * nc + c, 0)),
        out_shape=jax.ShapeDtypeStruct((t, w), BF16),
        scratch_shapes=[pltpu.VMEM((HG_HEADS, HG_D, HG_D), F32)],
        compiler_params=_cparams(("parallel", "arbitrary")),
        name="hgrn2",
    )(p0, p0, p0, p0, lb, gain)


def _kmean_kernel(k_ref, o_ref):
    o_ref[0] = jnp.mean(k_ref[...].astype(F32), axis=0, keepdims=True)


def moba_kmean(p1, batch, seq):
    nbt = p1.shape[0] // MB_BLOCK
    return pl.pallas_call(
        _kmean_kernel,
        grid=(nbt,),
        in_specs=[pl.BlockSpec((MB_BLOCK, MB_WIDTH), lambda i: (i, 1))],
        out_specs=pl.BlockSpec((1, 1, MB_WIDTH), lambda i: (i, 0, 0)),
        out_shape=jax.ShapeDtypeStruct((nbt, 1, MB_WIDTH), F32),
        compiler_params=_cparams(("parallel",)),
        name="moba_kmean",
    )(p1)


MB_PAIR = 4
MB_PW = MB_PAIR * MB_DH
MB_LG = 128
MB_ONES = 16
MB_VROWS = MB_DH + MB_ONES
MB_VT_ROWS = MB_HEADS * MB_VROWS


def _moba_kernel(q_ref, k_ref, vt_ref, km_ref, bias_ref, o_ref, *scratch, qb0):
    m_ref, l_ref, al_ref, acc_ref, msk_ref, s_ref, p_ref = (
        scratch[i * MB_PAIR:(i + 1) * MB_PAIR] for i in range(7))
    qi = pl.program_id(2) + qb0
    nb = km_ref.shape[0]
    blk = MB_BLOCK
    heads = range(MB_PAIR)
    grp = lambda hh: slice((hh // 2) * MB_LG, (hh // 2 + 1) * MB_LG)
    q = q_ref[...]
    lane = lax.broadcasted_iota(I32, (blk, MB_LG), 1)
    in_head = [(lane < MB_DH) if hh % 2 == 0 else (lane >= MB_DH) for hh in heads]
    qs = q * jnp.asarray(MB_DH ** -0.5, BF16)
    nt = (((1,), (1,)), ((), ()))
    qf = q.astype(F32)
    qht = [jnp.where(in_head[hh], qs[:, grp(hh)].astype(F32), 0.0).T.astype(BF16) for hh in heads]

    n_io = lax.broadcasted_iota(I32, (nb, blk), 0)
    for hh in heads:
        gate = lax.dot_general(km_ref[:, grp(hh)], jnp.where(in_head[hh], qf[:, grp(hh)], 0.0), nt,
                               precision=lax.Precision.HIGHEST, preferred_element_type=F32)
        gate = jnp.where(n_io < qi, gate, NEG_INF)
        chosen = n_io < 0
        for _ in range(MB_TOPK):
            mx = jnp.max(gate, axis=0, keepdims=True)
            ix = jnp.min(jnp.where(gate == mx, n_io, nb), axis=0, keepdims=True)
            hit = n_io == ix
            chosen = chosen | (hit & (mx > NEG_INF))
            gate = jnp.where(hit, NEG_INF, gate)
        msk_ref[hh][...] = jnp.where(chosen, 0.0, NEG_INF)

    vrows = lambda hh: slice(hh * MB_VROWS, (hh + 1) * MB_VROWS)

    def pv_stage(blk_idx):
        vtb = vt_ref[blk_idx]
        r = [jnp.dot(vtb[vrows(hh)], p_ref[hh][...], preferred_element_type=F32) for hh in heads]
        al = [al_ref[hh][...] for hh in heads]
        a_new = [al[hh] * acc_ref[hh][...] + r[hh][:MB_DH] for hh in heads]
        l_new = [al[hh] * l_ref[hh][...] + r[hh][MB_DH:MB_DH + 1] for hh in heads]
        return a_new, l_new

    def store_pv(a_new, l_new):
        for hh in heads:
            acc_ref[hh][...] = a_new[hh]
            l_ref[hh][...] = l_new[hh]

    def softmax_stage():
        s = [s_ref[hh][...] for hh in heads]
        m_old = [m_ref[hh][...] for hh in heads]
        m_new = [jnp.maximum(m_old[hh], jnp.max(s[hh], axis=0, keepdims=True)) for hh in heads]
        alpha = [jnp.exp(m_old[hh] - m_new[hh]) for hh in heads]
        p = [jnp.exp((s[hh] - m_new[hh]).astype(BF16)) for hh in heads]
        return p, alpha, m_new

    def store_softmax(p, alpha, m_new):
        for hh in heads:
            p_ref[hh][...] = p[hh]
            al_ref[hh][...] = alpha[hh]
            m_ref[hh][...] = m_new[hh]

    k_own = k_ref[pl.ds(pl.multiple_of(qi * blk, blk), blk), :]
    key_io = lax.broadcasted_iota(I32, (blk, blk), 0)
    qry_io = lax.broadcasted_iota(I32, (blk, blk), 1)
    for hh in heads:
        s = jnp.dot(k_own[:, grp(hh)], qht[hh], preferred_element_type=F32) + bias_ref[hh, 0]
        s_ref[hh][...] = jnp.where(key_io <= qry_io, s, NEG_INF)
        m_ref[hh][...] = jnp.full((1, blk), NEG_INF, F32)
        l_ref[hh][...] = jnp.zeros((1, blk), F32)
        al_ref[hh][...] = jnp.ones((1, blk), F32)
        acc_ref[hh][...] = jnp.zeros((MB_DH, blk), F32)
        p_ref[hh][...] = jnp.zeros((blk, blk), BF16)

    def step(i, carry, far):
        pv = pv_stage(jnp.where(i <= 1, qi, i - 2))
        sm = softmax_stage()
        kn = k_ref[pl.ds(pl.multiple_of(i * blk, blk), blk), :]
        if far:
            row = [msk_ref[hh][pl.ds(i, 1), :] + bias_ref[hh, MB_BIAS_TILES - 1, 0:1, 0:1] for hh in heads]
            s_next = [jnp.dot(kn[:, grp(hh)], qht[hh], preferred_element_type=F32) + row[hh] for hh in heads]
        else:
            d = qi - i
            s_next = [jnp.dot(kn[:, grp(hh)], qht[hh], preferred_element_type=F32)
                      + bias_ref[hh, d] + msk_ref[hh][pl.ds(i, 1), :] for hh in heads]
        store_pv(*pv)
        for hh in heads:
            s_ref[hh][...] = s_next[hh]
        store_softmax(*sm)
        return carry

    n_far = jnp.maximum(qi - (MB_BIAS_TILES - 2), 0)
    lax.fori_loop(0, n_far, functools.partial(step, far=True), 0)
    lax.fori_loop(n_far, qi, functools.partial(step, far=False), 0)
    pv = pv_stage(jnp.where(qi <= 1, qi, qi - 2))
    sm = softmax_stage()
    store_pv(*pv)
    store_softmax(*sm)
    a_fin, l_fin = pv_stage(jnp.where(qi == 0, qi, qi - 1))
    out_t = jnp.concatenate([a_fin[hh] / l_fin[hh] for hh in heads], axis=0)
    o_ref[...] = out_t.T.astype(o_ref.dtype)


def moba_attention(pqk, vt, km, bias, batch, seq, qb0=0, nqb=None):
    nb = seq // MB_BLOCK
    nqb = nb if nqb is None else nqb
    t = batch * nqb * MB_BLOCK
    groups = MB_WIDTH // MB_PW
    return pl.pallas_call(
        functools.partial(_moba_kernel, qb0=qb0),
        grid=(batch, groups, nqb),
        in_specs=[
            pl.BlockSpec((MB_BLOCK, MB_PW), lambda b, j, i: (b * nb + qb0 + i, j)),
            pl.BlockSpec((seq, MB_PW), lambda b, j, i: (b, groups + j)),
            pl.BlockSpec((nb, MB_PAIR * MB_VROWS, MB_BLOCK), lambda b, j, i: (b, j, 0)),
            pl.BlockSpec((None, nb, MB_PW), lambda b, j, i: (b, 0, j)),
            pl.BlockSpec((MB_PAIR, MB_BIAS_TILES, MB_BLOCK, MB_BLOCK), lambda b, j, i: (j, 0, 0, 0)),
        ],
        out_specs=pl.BlockSpec((MB_BLOCK, MB_PW), lambda b, j, i: (b * nqb + i, j)),
        out_shape=jax.ShapeDtypeStruct((t, MB_WIDTH), BF16),
        scratch_shapes=(
            [pltpu.VMEM((1, MB_BLOCK), F32)] * (3 * MB_PAIR)
            + [pltpu.VMEM((MB_DH, MB_BLOCK), F32)] * MB_PAIR
            + [pltpu.VMEM((nb, MB_BLOCK), F32)] * MB_PAIR
            + [pltpu.VMEM((MB_BLOCK, MB_BLOCK), F32)] * MB_PAIR
            + [pltpu.VMEM((MB_BLOCK, MB_BLOCK), BF16)] * MB_PAIR
        ),
        compiler_params=_cparams(("parallel", "parallel", "arbitrary")),
        name="moba_attn",
    )(pqk, pqk, vt, km, bias)


def _t5_bucket(dist):
    max_exact = REL_BUCKETS // 2
    scaled = jnp.log(jnp.maximum(dist, 1).astype(F32) / max_exact) / math.log(REL_MAX_DIST / max_exact)
    large = jnp.minimum(max_exact + (scaled * (REL_BUCKETS - max_exact)).astype(I32), REL_BUCKETS - 1)
    return jnp.where(dist < max_exact, dist, large)


def moba_bias_tiles(rel_bias):
    blk = MB_BLOCK
    span = 2 * blk - 1
    x = jnp.arange(span) - (blk - 1)
    dist = jnp.maximum(jnp.arange(MB_BIAS_TILES)[:, None] * blk + x[None, :], 0)
    w = rel_bias.astype(F32).T[:, _t5_bucket(dist)]
    h = w.shape[0]
    wp = jnp.pad(w, ((0, 0), (0, 0), (0, 1)))
    a = jnp.broadcast_to(wp[:, :, None, :], (h, MB_BIAS_TILES, blk, span + 1))
    a = a.reshape(h, MB_BIAS_TILES, blk * (span + 1))[:, :, :blk * span]
    return a.reshape(h, MB_BIAS_TILES, blk, span)[:, :, :, blk - 1:]


def _mix_kernel(x_ref, ya_ref, yb_ref, ga_ref, gb_ref, wa_ref, wb_ref, wo_ref, o_ref):
    za = jnp.dot(ya_ref[...], wa_ref[...], preferred_element_type=F32)
    zb = jnp.dot(yb_ref[...], wb_ref[...], preferred_element_type=F32)
    z = jax.nn.sigmoid(ga_ref[...].astype(F32)) * za + jax.nn.sigmoid(gb_ref[...].astype(F32)) * zb
    o_ref[...] = x_ref[...] + jnp.dot(z.astype(BF16), wo_ref[...], preferred_element_type=F32)


def mix_out(x2d, ya, yb, pg, wa, wb, wo, tok0=0, tm=512):
    t = yb.shape[0]
    assert t % tm == 0 and tok0 % tm == 0
    d = x2d.shape[1]
    w = ya.shape[1]
    b0 = tok0 // tm
    return pl.pallas_call(
        _mix_kernel,
        grid=(t // tm,),
        in_specs=[
            pl.BlockSpec((tm, d), lambda i: (b0 + i, 0)),
            pl.BlockSpec((tm, w), lambda i: (b0 + i, 0)),
            pl.BlockSpec((tm, w), lambda i: (i, 0)),
            pl.BlockSpec((tm, d), lambda i: (b0 + i, 0)),
            pl.BlockSpec((tm, d), lambda i: (b0 + i, 1)),
            pl.BlockSpec((w, d), lambda i: (0, 0)),
            pl.BlockSpec((w, d), lambda i: (0, 0)),
            pl.BlockSpec((d, d), lambda i: (0, 0)),
        ],
        out_specs=pl.BlockSpec((tm, d), lambda i: (i, 0)),
        out_shape=jax.ShapeDtypeStruct((t, d), F32),
        compiler_params=_cparams(("parallel",)),
        name="mix_out",
    )(x2d, ya, yb, pg, pg, wa, wb, wo)


def _mem_kv_kernel(m_ref, g_ref, wk_ref, wv_ref, k_ref, v_ref):
    mn = _rms(m_ref[...], g_ref[...]).astype(BF16)
    k_ref[...] = jnp.dot(mn, wk_ref[...], preferred_element_type=F32).astype(BF16)
    v_ref[...] = jnp.dot(mn, wv_ref[...], preferred_element_type=F32).astype(BF16)


def mem_kv(mem, g, wk, wv):
    b, m, d = mem.shape
    spec = pl.BlockSpec((None, m, d), lambda i: (i, 0, 0))
    wspec = pl.BlockSpec((d, d), lambda i: (0, 0))
    return pl.pallas_call(
        _mem_kv_kernel,
        grid=(b,),
        in_specs=[spec, pl.BlockSpec((1, d), lambda i: (0, 0)), wspec, wspec],
        out_specs=[spec, spec],
        out_shape=[jax.ShapeDtypeStruct((b, m, d), BF16)] * 2,
        compiler_params=_cparams(("parallel",)),
        name="mem_kv",
    )(mem, g, wk, wv)


def _cross_kernel(x_ref, g_ref, wq_ref, k_ref, v_ref, wo_ref, o_ref):
    x = x_ref[...]
    d = x.shape[1]
    dh = d // X_HEADS
    h = _rms(x, g_ref[...]).astype(BF16)
    q = (jnp.dot(h, wq_ref[...], preferred_element_type=F32) * (dh ** -0.5)).astype(BF16)
    outs = []
    for hh in range(X_HEADS):
        sl = slice(hh * dh, (hh + 1) * dh)
        s = lax.dot_general(q[:, sl], k_ref[:, sl], (((1,), (1,)), ((), ())),
                            preferred_element_type=F32)
        p = jnp.exp(s - jnp.max(s, axis=1, keepdims=True))
        l = jnp.sum(p, axis=1, keepdims=True)
        o = jnp.dot(p.astype(BF16), v_ref[:, sl], preferred_element_type=F32) / l
        outs.append(o.astype(BF16))
    o = jnp.concatenate(outs, axis=1)
    o_ref[...] = x + jnp.dot(o, wo_ref[...], preferred_element_type=F32)


def cross_attn(x2d, g, wq, kx, vx, wo, seq, tm=512):
    t, d = x2d.shape
    assert seq % tm == 0 and t % seq == 0
    m = kx.shape[1]
    per_b = seq // tm
    kv = pl.BlockSpec((None, m, d), lambda i: (i // per_b, 0, 0))
    wspec = pl.BlockSpec((d, d), lambda i: (0, 0))
    return pl.pallas_call(
        _cross_kernel,
        grid=(t // tm,),
        in_specs=[pl.BlockSpec((tm, d), lambda i: (i, 0)), pl.BlockSpec((1, d), lambda i: (0, 0)),
                  wspec, kv, kv, wspec],
        out_specs=pl.BlockSpec((tm, d), lambda i: (i, 0)),
        out_shape=jax.ShapeDtypeStruct((t, d), F32),
        compiler_params=_cparams(("parallel",)),
        name="cross_attn",
    )(x2d, g, wq, kx, vx, wo)


def _topk_rows(sc, k):
    n = sc.shape[0]
    io = lax.broadcasted_iota(I32, sc.shape, 0)
    vals, ids = [], []
    for _ in range(k):
        m = jnp.max(sc, axis=0, keepdims=True)
        ix = jnp.min(jnp.where(sc == m, io, n), axis=0, keepdims=True)
        vals.append(m)
        ids.append(ix)
        sc = jnp.where(io == ix, NEG_INF, sc)
    return jnp.concatenate(vals, axis=0), jnp.concatenate(ids, axis=0)


def _pack_bf16_halves(h):
    bits = lax.bitcast_convert_type(h, I32)
    r = bits + 0x7FFF + (lax.shift_right_logical(bits, 16) & 1)
    half = h.shape[1] // 2
    return lax.shift_right_logical(r[:, :half], 16) | (r[:, half:] & HI_MASK)


def _route_kernel(x_ref, g_ref, wq_ref, sk_ref, hp_ref, idx_ref, w_ref, hb_ref, it_ref, wt_ref):
    p = pl.program_id(1)

    @pl.when(p == 0)
    def _():
        h = _rms(x_ref[...], g_ref[...])
        hp_ref[...] = _pack_bf16_halves(h)
        hb_ref[...] = h.astype(BF16)

    qh = jnp.dot(hb_ref[...], wq_ref[...], preferred_element_type=F32)
    tops = []
    for c in range(2):
        seg = qh[:, c * PEER_HALF:(c + 1) * PEER_HALF]
        sc = lax.dot_general(sk_ref[c], seg, (((1,), (1,)), ((), ())),
                             precision=lax.Precision.HIGHEST, preferred_element_type=F32)
        tops.append(_topk_rows(sc, PEER_TOPK))
    (s0, i0), (s1, i1) = tops
    k = PEER_TOPK
    sub = 8
    tm = s0.shape[1]
    r8 = lax.broadcasted_iota(I32, (sub, tm), 0)
    r16 = lax.broadcasted_iota(I32, (k, tm), 0)
    cand_b = [s0[0:1] + s1, s0[1:2] + s1[:sub]]
    cidx_b = [i0[0:1] * PEER_NKEYS + i1, i0[1:2] * PEER_NKEYS + i1[:sub]]
    pos_b = [r16, k + r8]
    for a in range(2, sub):
        keep = r8 < (k // (a + 1))
        cand_b.append(jnp.where(keep, s0[a:a + 1] + s1[:sub], NEG_INF))
        cidx_b.append(i0[a:a + 1] * PEER_NKEYS + i1[:sub])
        pos_b.append(a * k + r8)
    cand_b.append(s0[sub:] + s1[0:1])
    cidx_b.append(i0[sub:] * PEER_NKEYS + i1[0:1])
    pos_b.append((sub + r8) * k)
    cand = jnp.concatenate(cand_b, axis=0)
    cidx = jnp.concatenate(cidx_b, axis=0)
    pos = jnp.concatenate(pos_b, axis=0)
    vals, ids = [], []
    for _ in range(k):
        m = jnp.max(cand, axis=0, keepdims=True)
        px = jnp.min(jnp.where(cand == m, pos, k * k), axis=0, keepdims=True)
        hit = pos == px
        vals.append(m)
        ids.append(jnp.sum(jnp.where(hit, cidx, 0), axis=0, keepdims=True))
        cand = jnp.where(hit, NEG_INF, cand)
    sf = jnp.concatenate(vals, axis=0)
    e = jnp.exp(sf - sf[0:1])
    rows = pl.ds(pl.multiple_of(p * PEER_TOPK, PEER_TOPK), PEER_TOPK)
    wt_ref[rows, :] = e / jnp.sum(e, axis=0, keepdims=True)
    it_ref[rows, :] = jnp.concatenate(ids, axis=0)

    @pl.when(p == pl.num_programs(1) - 1)
    def _():
        idx_ref[...] = it_ref[...].T
        w_ref[...] = wt_ref[...].T


def peer_route(x2d, g, wq, sk, tok0, t, tm=1024):
    assert t % tm == 0 and tok0 % tm == 0
    d = x2d.shape[1]
    ph = sk.shape[0]
    nsel = ph * PEER_TOPK
    blk0 = tok0 // tm
    return pl.pallas_call(
        _route_kernel,
        grid=(t // tm, ph),
        in_specs=[
            pl.BlockSpec((tm, d), lambda i, p: (blk0 + i, 0)),
            pl.BlockSpec((1, d), lambda i, p: (0, 0)),
            pl.BlockSpec((d, 2 * PEER_HALF), lambda i, p: (0, p)),
            pl.BlockSpec((None, 2, PEER_NKEYS, PEER_HALF), lambda i, p: (p, 0, 0, 0)),
        ],
        out_specs=[
            pl.BlockSpec((tm, d // 2), lambda i, p: (i, 0)),
            pl.BlockSpec((tm, nsel), lambda i, p: (i, 0)),
            pl.BlockSpec((tm, nsel), lambda i, p: (i, 0)),
        ],
        out_shape=[jax.ShapeDtypeStruct((t, d // 2), I32),
                   jax.ShapeDtypeStruct((t, nsel), I32),
                   jax.ShapeDtypeStruct((t, nsel), F32)],
        scratch_shapes=[pltpu.VMEM((tm, d), BF16),
                        pltpu.VMEM((nsel, tm), I32),
                        pltpu.VMEM((nsel, tm), F32)],
        compiler_params=_cparams(("parallel", "arbitrary")),
        name="peer_route",
    )(x2d, g, wq, sk)


def _final_kernel(x_ref, y_ref, g_ref, o_ref):
    o_ref[...] = _rms(x_ref[...] + y_ref[...], g_ref[...])


def final_norm(x2d, y, g, tok0, tm=512):
    t, d = y.shape
    assert t % tm == 0 and tok0 % tm == 0
    blk0 = tok0 // tm
    spec = pl.BlockSpec((tm, d), lambda i: (i, 0))
    return pl.pallas_call(
        _final_kernel, grid=(t // tm,),
        in_specs=[pl.BlockSpec((tm, d), lambda i: (blk0 + i, 0)), spec, pl.BlockSpec((1, d), lambda i: (0, 0))],
        out_specs=spec,
        out_shape=jax.ShapeDtypeStruct((t, d), F32),
        compiler_params=_cparams(("parallel",)), name="final_norm",
    )(x2d, y, g)


SC_CORES = 2
SC_SUBCORES = 16
SC_WORKERS = SC_CORES * SC_SUBCORES
SC_LANES = 16
SC_GROUP = 32


def _sc_mesh():
    return plsc.VectorSubcoreMesh(core_axis_name="c", subcore_axis_name="s")


def _sc_params():
    return pltpu.CompilerParams(needs_layout_passes=False)


def _sc_worker_id():
    return lax.axis_index("s") * SC_CORES + lax.axis_index("c")


SC_ROW_LANE = 128


def _sc_unit_off(u):
    off = u * SC_LANES
    return off if isinstance(off, int) else pl.multiple_of(off, SC_LANES)


GELU_C0 = math.sqrt(2.0 / math.pi)
GELU_C1 = 0.044715


def _gelu_tanh(x):
    z = GELU_C0 * (x + GELU_C1 * (x * x * x))
    th = 1.0 - 2.0 / (jnp.exp(2.0 * z) + 1.0)
    return 0.5 * x * (1.0 + th)


SC_PK_RING = 4
SC_PK_SUB = 4
HI_MASK = -65536


def _pack_tables_kernel(u_ref, v_ref, o_ref):
    for part, ref in enumerate((u_ref, v_ref)):
        words = _pack_bf16_halves(ref[...])
        for sub in range(SC_PK_SUB):
            o_ref[:, part * SC_PK_SUB + sub, :] = words[:, sub * SC_ROW_LANE:(sub + 1) * SC_ROW_LANE]


def pack_expert_tables(u, v, te=512):
    e, d = u.shape
    assert d == 2 * SC_PK_SUB * SC_ROW_LANE
    spec = pl.BlockSpec((te, d), lambda i: (i, 0))
    return pl.pallas_call(
        _pack_tables_kernel, grid=(e // te,), in_specs=[spec, spec],
        out_specs=pl.BlockSpec((te, 2 * SC_PK_SUB, SC_ROW_LANE), lambda i: (i, 0, 0)),
        out_shape=jax.ShapeDtypeStruct((e, 2 * SC_PK_SUB, SC_ROW_LANE), I32),
        compiler_params=_cparams(("parallel",)), name="pack_expert_tables",
    )(u, v)


def _unpack_halves(x32):
    w = plsc.bitcast(x32, I32)
    return plsc.bitcast(w << 16, F32), plsc.bitcast(w & HI_MASK, F32)


def _tree_sum(xs):
    while len(xs) > 1:
        xs = [xs[i] + xs[i + 1] for i in range(0, len(xs), 2)]
    return xs[0]


def peer_experts_pk_sc(tab_uv, idx_flat, w_flat, hp, d):
    t = hp.shape[0]
    nsel = PEER_SEL
    g = SC_GROUP
    assert t % (SC_WORKERS * g) == 0 and d == 2 * SC_PK_SUB * SC_ROW_LANE
    tpw = t // SC_WORKERS
    groups = tpw // g
    heads = nsel // SC_LANES
    chunks = d // 32
    units = g * heads
    ring = SC_PK_RING
    assert units % ring == 0
    row_buf = pltpu.VMEM((SC_LANES, 2 * SC_PK_SUB, SC_ROW_LANE), I32)

    def row_words(rows, r, wc, sub0):
        per = SC_ROW_LANE // SC_LANES
        return plsc.bitcast(
            rows[r, sub0 + wc // per, pl.ds(pl.multiple_of((wc % per) * SC_LANES, SC_LANES), SC_LANES)], BF16)

    def ring_loop(n_units, start, wait, compute):
        for u in range(ring - 1):
            start(u, u)

        @pl.loop(0, n_units, step=ring)
        def _(uu):
            for b in range(ring):
                u = uu + b
                nxt = u + (ring - 1)

                @pl.when(nxt < n_units)
                def _():
                    start(nxt, (b + ring - 1) % ring)

                wait(u, b)
                compute(u, b)

    @functools.partial(
        pl.kernel, mesh=_sc_mesh(),
        out_type=jax.ShapeDtypeStruct((t, d), F32),
        scratch_types=[
            pltpu.VMEM((g * nsel,), I32),
            pltpu.VMEM((g * nsel,), F32),
            pltpu.VMEM((g, d // 2), I32),
            pltpu.VMEM((g, d), F32),
            pltpu.VMEM((SC_LANES * SC_LANES,), F32),
            [row_buf] * ring,
            [pltpu.SemaphoreType.DMA] * ring,
        ],
        compiler_params=_sc_params(),
        name="peer_experts_pk_sc",
    )
    def k(tab_hbm, idx_hbm, w_hbm, h_hbm, out_hbm, idx_v, coef_v, h_v, y_v, red_v, rows, sems):
        wid = _sc_worker_id()
        lane = lax.iota(I32, SC_LANES)

        def copy(u, slot):
            ids = idx_v.at[pl.ds(_sc_unit_off(u), SC_LANES)]
            return pltpu.make_async_copy(tab_hbm.at[ids], rows[slot], sems[slot])

        def dots(u, slot):
            tt = u // heads

            def body(cp, accs):
                out = []
                hv = [plsc.bitcast(h_v[tt, pl.ds(pl.multiple_of((2 * cp + i) * SC_LANES, SC_LANES), SC_LANES)], BF16)
                      for i in range(2)]
                for r in range(SC_LANES):
                    pr = (row_words(rows[slot], r, 2 * cp, 0) * hv[0]
                          + row_words(rows[slot], r, 2 * cp + 1, 0) * hv[1])
                    lo, hi = _unpack_halves(pr)
                    out.append(accs[r] + lo + hi)
                return tuple(out)

            accs = lax.fori_loop(0, chunks // 2, body,
                                 tuple(jnp.zeros((SC_LANES,), F32) for _ in range(SC_LANES)))
            for r in range(SC_LANES):
                red_v[pl.ds(r * SC_LANES, SC_LANES)] = accs[r]
            act = _tree_sum([plsc.load_gather(red_v, [lane * SC_LANES + j]) for j in range(SC_LANES)])
            sl = pl.ds(_sc_unit_off(u), SC_LANES)
            coef_v[sl] = coef_v[sl] * _gelu_tanh(act)

        def combine(u, slot):
            tt = u // heads
            first = (u % heads) == 0
            cb = []
            for r in range(SC_LANES):
                c = plsc.load_gather(coef_v, [jnp.full((SC_LANES,), u * SC_LANES + r, I32)])
                cb.append(plsc.pack(c, c, format=plsc.PackFormat.INTERLEAVED))

            @plsc.parallel_loop(0, chunks, unroll=2)
            def _(wc):
                lo, hi = _unpack_halves(
                    _tree_sum([cb[r] * row_words(rows[slot], r, wc, SC_PK_SUB) for r in range(SC_LANES)]))
                for half, val in ((0, lo), (1, hi)):
                    sl = pl.ds(pl.multiple_of(half * (d // 2) + wc * SC_LANES, SC_LANES), SC_LANES)
                    y_v[tt, sl] = val + jnp.where(first, 0.0, y_v[tt, sl])

        def unit(u, slot):
            dots(u, slot)
            combine(u, slot)

        @pl.loop(0, groups)
        def _(gi):
            base = wid * tpw + gi * g
            pltpu.sync_copy(idx_hbm.at[pl.ds(base * nsel, g * nsel)], idx_v)
            pltpu.sync_copy(w_hbm.at[pl.ds(base * nsel, g * nsel)], coef_v)
            pltpu.sync_copy(h_hbm.at[pl.ds(base, g)], h_v)
            ring_loop(units, lambda u, s: copy(u, s).start(), lambda u, s: copy(u, s).wait(), unit)
            pltpu.sync_copy(y_v, out_hbm.at[pl.ds(base, g)])

    return k(tab_uv, idx_flat, w_flat, hp)


def kernel(x, mem, rel_bias, ln_mix, w_in, hg_lower, hg_norm, w_up_a, w_up_b, w_out, ln_cross, ln_mem, wq_x, wk_x, wv_x, wo_x, ln_ffn, peer_query, peer_subkeys, peer_u, peer_v, ln_final):
    b, s, d = x.shape
    depth = w_in.shape[0]
    assert depth == 1, "the residual after PEER is fused into the final norm"
    assert s % MB_BLOCK == 0 and s % HG_CHUNK == 0 and s % (PEER_SLICES * SC_WORKERS * SC_GROUP) == 0
    nb = s // MB_BLOCK
    row = lambda a: a.reshape(1, -1).astype(F32)
    lb_all = jnp.cumsum(jax.nn.softmax(hg_lower.astype(F32), axis=0), axis=0)
    bias = moba_bias_tiles(rel_bias)
    n_hg = 4 * HG_WIDTH
    n_qk = 2 * MB_WIDTH
    n_mb = 3 * MB_WIDTH
    l = 0
    w = w_in[l].astype(BF16)
    w_hg, w_qk, w_vt, w_g = w[:, :n_hg], w[:, n_hg:n_hg + n_qk], w[:, n_hg + n_qk:n_hg + n_mb].T, w[:, n_hg + n_mb:]
    wa, wb, wo = w_up_a[l].astype(BF16), w_up_b[l].astype(BF16), w_out[l].astype(BF16)
    wqx, wox = wq_x[l].astype(BF16), wo_x[l].astype(BF16)
    wpq, sk = peer_query[l].astype(BF16), peer_subkeys[l].astype(F32)
    tab_uv = pack_expert_tables(peer_u[l].astype(F32), peer_v[l].astype(F32))
    kx, vx = mem_kv(mem, row(ln_mem[l]), wk_x[l].astype(BF16), wv_x[l].astype(BF16))

    outs = []
    for bi in range(b):
        x2d = x[bi]
        p0, pqk, vt, pg = in_proj(x2d, row(ln_mix[l]), w_hg, w_qk, w_vt, w_g)
        ya = hgrn2(p0, row(lb_all[l]), row(hg_norm[l]), 1, s)
        km = moba_kmean(pqk, 1, s).reshape(1, nb, MB_WIDTH)
        ts = s // PEER_SLICES
        for tok0 in range(0, s, ts):
            yb = moba_attention(pqk, vt, km, bias, 1, s, tok0 // MB_BLOCK, ts // MB_BLOCK)
            xs = mix_out(x2d, ya, yb, pg, wa, wb, wo, tok0)
            xs = cross_attn(xs, row(ln_cross[l]), wqx, kx[bi:bi + 1], vx[bi:bi + 1], wox, ts)
            hp, eidx, wts = peer_route(xs, row(ln_ffn[l]), wpq, sk, 0, ts)
            y = peer_experts_pk_sc(tab_uv, eidx.reshape(ts * PEER_SEL), wts.reshape(ts * PEER_SEL), hp, d)
            outs.append(final_norm(xs, y, row(ln_final), 0))
    return jnp.concatenate(outs, axis=0).reshape(b, s, d)
```

```python
import functools
import math

import jax
import jax.numpy as jnp
from jax import lax
from jax.experimental import pallas as pl
from jax.experimental.pallas import tpu as pltpu
from jax.experimental.pallas import tpu_sc as plsc

F32 = jnp.float32
BF16 = jnp.bfloat16
I32 = jnp.int32
EPS = 1e-6
NEG_INF = float("-inf")

HG_HEADS = 4
HG_D = 128
HG_WIDTH = HG_HEADS * HG_D
HG_CHUNK = 64
HG_SUB = 16
MB_HEADS = 8
MB_DH = 64
MB_WIDTH = MB_HEADS * MB_DH
MB_BLOCK = 256
MB_TOPK = 3
MB_BIAS_TILES = 8
REL_BUCKETS = 32
REL_MAX_DIST = 2048
X_HEADS = 4
PEER_HEADS = 8
PEER_NKEYS = 128
PEER_TOPK = 16
PEER_HALF = 128
PEER_SEL = PEER_HEADS * PEER_TOPK
PEER_SLICES = 4

VMEM_LIMIT = 56 * 1024 * 1024


def _cparams(sem):
    return pltpu.CompilerParams(dimension_semantics=sem, vmem_limit_bytes=VMEM_LIMIT)


def _rms(x, g):
    ms = jnp.mean(x * x, axis=-1, keepdims=True)
    return x * lax.rsqrt(ms + EPS) * g


def _in_proj_kernel(x_ref, g_ref, w0_ref, w1_ref, wvt_ref, w2_ref, o0_ref, o1_ref, ovt_ref, o2_ref):
    h = _rms(x_ref[...], g_ref[...]).astype(BF16)
    o0_ref[...] = jnp.dot(h, w0_ref[...], preferred_element_type=F32)
    o1_ref[...] = jnp.dot(h, w1_ref[...], preferred_element_type=F32).astype(BF16)
    vt = lax.dot_general(wvt_ref[...], h, (((1,), (1,)), ((), ())), preferred_element_type=F32).astype(BF16)
    for hd in range(MB_HEADS):
        ovt_ref[0, hd * MB_VROWS:hd * MB_VROWS + MB_DH, :] = vt[hd * MB_DH:(hd + 1) * MB_DH]
        ovt_ref[0, hd * MB_VROWS + MB_DH:(hd + 1) * MB_VROWS, :] = jnp.ones((MB_ONES, vt.shape[1]), BF16)
    o2_ref[...] = jnp.dot(h, w2_ref[...], preferred_element_type=F32).astype(BF16)


def in_proj(x2d, g, w0, w1, wvt, w2):
    t, d = x2d.shape
    tm = MB_BLOCK
    assert wvt.shape[0] == MB_WIDTH
    n0, n1, nv, n2 = w0.shape[1], w1.shape[1], MB_VT_ROWS, w2.shape[1]
    full = lambda a: pl.BlockSpec(a.shape, lambda i: (0, 0))
    return pl.pallas_call(
        _in_proj_kernel,
        grid=(t // tm,),
        in_specs=[pl.BlockSpec((tm, d), lambda i: (i, 0)), full(g), full(w0), full(w1), full(wvt), full(w2)],
        out_specs=[pl.BlockSpec((tm, n0), lambda i: (i, 0)),
                   pl.BlockSpec((tm, n1), lambda i: (i, 0)),
                   pl.BlockSpec((1, nv, tm), lambda i: (i, 0, 0)),
                   pl.BlockSpec((tm, n2), lambda i: (i, 0))],
        out_shape=[jax.ShapeDtypeStruct((t, n0), F32),
                   jax.ShapeDtypeStruct((t, n1), BF16),
                   jax.ShapeDtypeStruct((t // tm, nv, tm), BF16),
                   jax.ShapeDtypeStruct((t, n2), BF16)],
        compiler_params=_cparams(("parallel",)),
        name="in_proj",
    )(x2d, g, w0, w1, wvt, w2)


def _hgrn_kernel(q_ref, f_ref, i_ref, g_ref, lb_ref, gain_ref, o_ref, st_ref):
    c = pl.program_id(1)

    @pl.when(c == 0)
    def _():
        st_ref[...] = jnp.zeros_like(st_ref)

    C, S = HG_CHUNK, HG_SUB
    row = lax.broadcasted_iota(I32, (C, C), 0)
    col = lax.broadcasted_iota(I32, (C, C), 1)
    tril = (row >= col).astype(F32)
    t_iota = lax.broadcasted_iota(I32, (S, 1), 0)

    for h in range(HG_HEADS):
        sl = slice(h * HG_D, (h + 1) * HG_D)
        q = q_ref[:, sl]
        v = i_ref[:, sl]
        lb = lb_ref[:, sl]
        f = lb + (1.0 - lb) * jax.nn.sigmoid(f_ref[:, sl])
        lf = jnp.log(f)
        k = 1.0 - f
        b = jnp.dot(tril, lf, precision=lax.Precision.HIGHEST, preferred_element_type=F32)
        st = st_ref[h]
        vb = v.astype(BF16)
        qd = (q * jnp.exp(b)).astype(BF16)
        o_inter = lax.dot_general(qd, st.astype(BF16), (((1,), (1,)), ((), ())),
                                  preferred_element_type=F32)
        outs = []
        for i in range(C // S):
            r0 = i * S
            qi = q[r0:r0 + S]
            ki = k[r0:r0 + S]
            bi = b[r0:r0 + S]
            vi = v[r0:r0 + S]
            oi = o_inter[r0:r0 + S]
            if i > 0:
                bs = b[r0 - 1:r0]
                qh = (qi * jnp.exp(bi - bs)).astype(BF16)
                kh = (k[:r0] * jnp.exp(bs - b[:r0])).astype(BF16)
                a = lax.dot_general(qh, kh, (((1,), (1,)), ((), ())), preferred_element_type=F32)
                oi = oi + jnp.dot(a.astype(BF16), vb[:r0], preferred_element_type=F32)
            half = S // 2
            o_half = [oi[:half], oi[half:]]
            for s in range(S):
                for hf in range(s // half, 2):
                    rows = slice(hf * half, (hf + 1) * half)
                    dec = jnp.exp(jnp.minimum(bi[rows] - bi[s:s + 1], 0.0))
                    a_s = jnp.sum(qi[rows] * ki[s:s + 1] * dec, axis=-1, keepdims=True)
                    a_s = jnp.where(t_iota[rows] >= s, a_s, 0.0)
                    o_half[hf] = o_half[hf] + a_s * vi[s:s + 1]
            outs.extend(o_half)
        o = jnp.concatenate(outs, axis=0)
        b_end = b[C - 1:C]
        kd = (k * jnp.exp(b_end - b)).astype(BF16)
        upd = lax.dot_general(vb, kd, (((0,), (0,)), ((), ())), preferred_element_type=F32)
        st_ref[h] = st * jnp.exp(b_end) + upd
        o = o * lax.rsqrt(jnp.mean(o * o, axis=-1, keepdims=True) + EPS)
        g = g_ref[:, sl]
        o_ref[:, sl] = (o * gain_ref[:, sl] * (g * jax.nn.sigmoid(g))).astype(o_ref.dtype)


def hgrn2(p0, lb, gain, batch, seq):
    t = p0.shape[0]
    nc = seq // HG_CHUNK
    w = HG_WIDTH

    def col(j):
        return pl.BlockSpec((HG_CHUNK, w), lambda b, c, j=j: (b * nc + c, j))

    return pl.pallas_call(
        _hgrn_kernel,
        grid=(batch, nc),
        in_specs=[col(0), col(1), col(2), col(3),
                  pl.BlockSpec((1, w), lambda b, c: (0, 0)),
                  pl.BlockSpec((1, w), lambda b, c: (0, 0))],
        out_specs=pl.BlockSpec((HG_CHUNK, w), lambda b, c: (b * nc + c, 0)),
        out_shape=jax.ShapeDtypeStruct((t, w), BF16),
        scratch_shapes=[pltpu.VMEM((HG_HEADS, HG_D, HG_D), F32)],
        compiler_params=_cparams(("parallel", "arbitrary")),
        name="hgrn2",
    )(p0, p0, p0, p0, lb, gain)


def _kmean_kernel(k_ref, o_ref):
    o_ref[0] = jnp.mean(k_ref[...].astype(F32), axis=0, keepdims=True)


def moba_kmean(p1, batch, seq):
    nbt = p1.shape[0] // MB_BLOCK
    return pl.pallas_call(
        _kmean_kernel,
        grid=(nbt,),
        in_specs=[pl.BlockSpec((MB_BLOCK, MB_WIDTH), lambda i: (i, 1))],
        out_specs=pl.BlockSpec((1, 1, MB_WIDTH), lambda i: (i, 0, 0)),
        out_shape=jax.ShapeDtypeStruct((nbt, 1, MB_WIDTH), F32),
        compiler_params=_cparams(("parallel",)),
        name="moba_kmean",
    )(p1)


MB_PAIR = 4
MB_PW = MB_PAIR * MB_DH
MB_LG = 128
MB_ONES = 16
MB_VROWS = MB_DH + MB_ONES
MB_VT_ROWS = MB_HEADS * MB_VROWS


def _moba_kernel(q_ref, k_ref, vt_ref, km_ref, bias_ref, o_ref, *scratch, qb0):
    m_ref, l_ref, al_ref, acc_ref, msk_ref, s_ref, p_ref = (
        scratch[i * MB_PAIR:(i + 1) * MB_PAIR] for i in range(7))
    qi = pl.program_id(2) + qb0
    nb = km_ref.shape[0]
    blk = MB_BLOCK
    heads = range(MB_PAIR)
    grp = lambda hh: slice((hh // 2) * MB_LG, (hh // 2 + 1) * MB_LG)
    q = q_ref[...]
    lane = lax.broadcasted_iota(I32, (blk, MB_LG), 1)
    in_head = [(lane < MB_DH) if hh % 2 == 0 else (lane >= MB_DH) for hh in heads]
    qs = q * jnp.asarray(MB_DH ** -0.5, BF16)
    nt = (((1,), (1,)), ((), ()))
    qf = q.astype(F32)
    qht = [jnp.where(in_head[hh], qs[:, grp(hh)].astype(F32), 0.0).T.astype(BF16) for hh in heads]

    n_io = lax.broadcasted_iota(I32, (nb, blk), 0)
    for hh in heads:
        gate = lax.dot_general(km_ref[:, grp(hh)], jnp.where(in_head[hh], qf[:, grp(hh)], 0.0), nt,
                               precision=lax.Precision.HIGHEST, preferred_element_type=F32)
        gate = jnp.where(n_io < qi, gate, NEG_INF)
        chosen = n_io < 0
        for _ in range(MB_TOPK):
            mx = jnp.max(gate, axis=0, keepdims=True)
            ix = jnp.min(jnp.where(gate == mx, n_io, nb), axis=0, keepdims=True)
            hit = n_io == ix
            chosen = chosen | (hit & (mx > NEG_INF))
            gate = jnp.where(hit, NEG_INF, gate)
        msk_ref[hh][...] = jnp.where(chosen, 0.0, NEG_INF)

    vrows = lambda hh: slice(hh * MB_VROWS, (hh + 1) * MB_VROWS)

    def pv_stage(blk_idx):
        vtb = vt_ref[blk_idx]
        r = [jnp.dot(vtb[vrows(hh)], p_ref[hh][...], preferred_element_type=F32) for hh in heads]
        al = [al_ref[hh][...] for hh in heads]
        a_new = [al[hh] * acc_ref[hh][...] + r[hh][:MB_DH] for hh in heads]
        l_new = [al[hh] * l_ref[hh][...] + r[hh][MB_DH:MB_DH + 1] for hh in heads]
        return a_new, l_new

    def store_pv(a_new, l_new):
        for hh in heads:
            acc_ref[hh][...] = a_new[hh]
            l_ref[hh][...] = l_new[hh]

    def softmax_stage():
        s = [s_ref[hh][...] for hh in heads]
        m_old = [m_ref[hh][...] for hh in heads]
        m_new = [jnp.maximum(m_old[hh], jnp.max(s[hh], axis=0, keepdims=True)) for hh in heads]
        alpha = [jnp.exp(m_old[hh] - m_new[hh]) for hh in heads]
        p = [jnp.exp((s[hh] - m_new[hh]).astype(BF16)) for hh in heads]
        return p, alpha, m_new

    def store_softmax(p, alpha, m_new):
        for hh in heads:
            p_ref[hh][...] = p[hh]
            al_ref[hh][...] = alpha[hh]
            m_ref[hh][...] = m_new[hh]

    k_own = k_ref[pl.ds(pl.multiple_of(qi * blk, blk), blk), :]
    key_io = lax.broadcasted_iota(I32, (blk, blk), 0)
    qry_io = lax.broadcasted_iota(I32, (blk, blk), 1)
    for hh in heads:
        s = jnp.dot(k_own[:, grp(hh)], qht[hh], preferred_element_type=F32) + bias_ref[hh, 0]
        s_ref[hh][...] = jnp.where(key_io <= qry_io, s, NEG_INF)
        m_ref[hh][...] = jnp.full((1, blk), NEG_INF, F32)
        l_ref[hh][...] = jnp.zeros((1, blk), F32)
        al_ref[hh][...] = jnp.ones((1, blk), F32)
        acc_ref[hh][...] = jnp.zeros((MB_DH, blk), F32)
        p_ref[hh][...] = jnp.zeros((blk, blk), BF16)

    def step(i, carry, far):
        pv = pv_stage(jnp.where(i <= 1, qi, i - 2))
        sm = softmax_stage()
        kn = k_ref[pl.ds(pl.multiple_of(i * blk, blk), blk), :]
        if far:
            row = [msk_ref[hh][pl.ds(i, 1), :] + bias_ref[hh, MB_BIAS_TILES - 1, 0:1, 0:1] for hh in heads]
            s_next = [jnp.dot(kn[:, grp(hh)], qht[hh], preferred_element_type=F32) + row[hh] for hh in heads]
        else:
            d = qi - i
            s_next = [jnp.dot(kn[:, grp(hh)], qht[hh], preferred_element_type=F32)
                      + bias_ref[hh, d] + msk_ref[hh][pl.ds(i, 1), :] for hh in heads]
        store_pv(*pv)
        for hh in heads:
            s_ref[hh][...] = s_next[hh]
        store_softmax(*sm)
        return carry

    n_far = jnp.maximum(qi - (MB_BIAS_TILES - 2), 0)
    lax.fori_loop(0, n_far, functools.partial(step, far=True), 0)
    lax.fori_loop(n_far, qi, functools.partial(step, far=False), 0)
    pv = pv_stage(jnp.where(qi <= 1, qi, qi - 2))
    sm = softmax_stage()
    store_pv(*pv)
    store_softmax(*sm)
    a_fin, l_fin = pv_stage(jnp.where(qi == 0, qi, qi - 1))
    out_t = jnp.concatenate([a_fin[hh] / l_fin[hh] for hh in heads], axis=0)
    o_ref[...] = out_t.T.astype(o_ref.dtype)


def moba_attention(pqk, vt, km, bias, batch, seq, qb0=0, nqb=None):
    nb = seq // MB_BLOCK
    nqb = nb if nqb is None else nqb
    t = batch * nqb * MB_BLOCK
    groups = MB_WIDTH // MB_PW
    return pl.pallas_call(
        functools.partial(_moba_kernel, qb0=qb0),
        grid=(batch, groups, nqb),
        in_specs=[
            pl.BlockSpec((MB_BLOCK, MB_PW), lambda b, j, i: (b * nb + qb0 + i, j)),
            pl.BlockSpec((seq, MB_PW), lambda b, j, i: (b, groups + j)),
            pl.BlockSpec((nb, MB_PAIR * MB_VROWS, MB_BLOCK), lambda b, j, i: (b, j, 0)),
            pl.BlockSpec((None, nb, MB_PW), lambda b, j, i: (b, 0, j)),
            pl.BlockSpec((MB_PAIR, MB_BIAS_TILES, MB_BLOCK, MB_BLOCK), lambda b, j, i: (j, 0, 0, 0)),
        ],
        out_specs=pl.BlockSpec((MB_BLOCK, MB_PW), lambda b, j, i: (b * nqb + i, j)),
        out_shape=jax.ShapeDtypeStruct((t, MB_WIDTH), BF16),
        scratch_shapes=(
            [pltpu.VMEM((1, MB_BLOCK), F32)] * (3 * MB_PAIR)
            + [pltpu.VMEM((MB_DH, MB_BLOCK), F32)] * MB_PAIR
            + [pltpu.VMEM((nb, MB_BLOCK), F32)] * MB_PAIR
            + [pltpu.VMEM((MB_BLOCK, MB_BLOCK), F32)] * MB_PAIR
            + [pltpu.VMEM((MB_BLOCK, MB_BLOCK), BF16)] * MB_PAIR
        ),
        compiler_params=_cparams(("parallel", "parallel", "arbitrary")),
        name="moba_attn",
    )(pqk, pqk, vt, km, bias)


def _t5_bucket(dist):
    max_exact = REL_BUCKETS // 2
    scaled = jnp.log(jnp.maximum(dist, 1).astype(F32) / max_exact) / math.log(REL_MAX_DIST / max_exact)
    large = jnp.minimum(max_exact + (scaled * (REL_BUCKETS - max_exact)).astype(I32), REL_BUCKETS - 1)
    return jnp.where(dist < max_exact, dist, large)


def moba_bias_tiles(rel_bias):
    blk = MB_BLOCK
    span = 2 * blk - 1
    x = jnp.arange(span) - (blk - 1)
    dist = jnp.maximum(jnp.arange(MB_BIAS_TILES)[:, None] * blk + x[None, :], 0)
    w = rel_bias.astype(F32).T[:, _t5_bucket(dist)]
    h = w.shape[0]
    wp = jnp.pad(w, ((0, 0), (0, 0), (0, 1)))
    a = jnp.broadcast_to(wp[:, :, None, :], (h, MB_BIAS_TILES, blk, span + 1))
    a = a.reshape(h, MB_BIAS_TILES, blk * (span + 1))[:, :, :blk * span]
    return a.reshape(h, MB_BIAS_TILES, blk, span)[:, :, :, blk - 1:]


def _mix_kernel(x_ref, ya_ref, yb_ref, ga_ref, gb_ref, wa_ref, wb_ref, wo_ref, o_ref):
    za = jnp.dot(ya_ref[...], wa_ref[...], preferred_element_type=F32)
    zb = jnp.dot(yb_ref[...], wb_ref[...], preferred_element_type=F32)
    z = jax.nn.sigmoid(ga_ref[...].astype(F32)) * za + jax.nn.sigmoid(gb_ref[...].astype(F32)) * zb
    o_ref[...] = x_ref[...] + jnp.dot(z.astype(BF16), wo_ref[...], preferred_element_type=F32)


def mix_out(x2d, ya, yb, pg, wa, wb, wo, tok0=0, tm=512):
    t = yb.shape[0]
    assert t % tm == 0 and tok0 % tm == 0
    d = x2d.shape[1]
    w = ya.shape[1]
    b0 = tok0 // tm
    return pl.pallas_call(
        _mix_kernel,
        grid=(t // tm,),
        in_specs=[
            pl.BlockSpec((tm, d), lambda i: (b0 + i, 0)),
            pl.BlockSpec((tm, w), lambda i: (b0 + i, 0)),
            pl.BlockSpec((tm, w), lambda i: (i, 0)),
            pl.BlockSpec((tm, d), lambda i: (b0 + i, 0)),
            pl.BlockSpec((tm, d), lambda i: (b0 + i, 1)),
            pl.BlockSpec((w, d), lambda i: (0, 0)),
            pl.BlockSpec((w, d), lambda i: (0, 0)),
            pl.BlockSpec((d, d), lambda i: (0, 0)),
        ],
        out_specs=pl.BlockSpec((tm, d), lambda i: (i, 0)),
        out_shape=jax.ShapeDtypeStruct((t, d), F32),
        compiler_params=_cparams(("parallel",)),
        name="mix_out",
    )(x2d, ya, yb, pg, pg, wa, wb, wo)


def _mem_kv_kernel(m_ref, g_ref, wk_ref, wv_ref, k_ref, v_ref):
    mn = _rms(m_ref[...], g_ref[...]).astype(BF16)
    k_ref[...] = jnp.dot(mn, wk_ref[...], preferred_element_type=F32).astype(BF16)
    v_ref[...] = jnp.dot(mn, wv_ref[...], preferred_element_type=F32).astype(BF16)


def mem_kv(mem, g, wk, wv):
    b, m, d = mem.shape
    spec = pl.BlockSpec((None, m, d), lambda i: (i, 0, 0))
    wspec = pl.BlockSpec((d, d), lambda i: (0, 0))
    return pl.pallas_call(
        _mem_kv_kernel,
        grid=(b,),
        in_specs=[spec, pl.BlockSpec((1, d), lambda i: (0, 0)), wspec, wspec],
        out_specs=[spec, spec],
        out_shape=[jax.ShapeDtypeStruct((b, m, d), BF16)] * 2,
        compiler_params=_cparams(("parallel",)),
        name="mem_kv",
    )(mem, g, wk, wv)


def _cross_kernel(x_ref, g_ref, wq_ref, k_ref, v_ref, wo_ref, o_ref):
    x = x_ref[...]
    d = x.shape[1]
    dh = d // X_HEADS
    h = _rms(x, g_ref[...]).astype(BF16)
    q = (jnp.dot(h, wq_ref[...], preferred_element_type=F32) * (dh ** -0.5)).astype(BF16)
    outs = []
    for hh in range(X_HEADS):
        sl = slice(hh * dh, (hh + 1) * dh)
        s = lax.dot_general(q[:, sl], k_ref[:, sl], (((1,), (1,)), ((), ())),
                            preferred_element_type=F32)
        p = jnp.exp(s - jnp.max(s, axis=1, keepdims=True))
        l = jnp.sum(p, axis=1, keepdims=True)
        o = jnp.dot(p.astype(BF16), v_ref[:, sl], preferred_element_type=F32) / l
        outs.append(o.astype(BF16))
    o = jnp.concatenate(outs, axis=1)
    o_ref[...] = x + jnp.dot(o, wo_ref[...], preferred_element_type=F32)


def cross_attn(x2d, g, wq, kx, vx, wo, seq, tm=512):
    t, d = x2d.shape
    assert seq % tm == 0 and t % seq == 0
    m = kx.shape[1]
    per_b = seq // tm
    kv = pl.BlockSpec((None, m, d), lambda i: (i // per_b, 0, 0))
    wspec = pl.BlockSpec((d, d), lambda i: (0, 0))
    return pl.pallas_call(
        _cross_kernel,
        grid=(t // tm,),
        in_specs=[pl.BlockSpec((tm, d), lambda i: (i, 0)), pl.BlockSpec((1, d), lambda i: (0, 0)),
                  wspec, kv, kv, wspec],
        out_specs=pl.BlockSpec((tm, d), lambda i: (i, 0)),
        out_shape=jax.ShapeDtypeStruct((t, d), F32),
        compiler_params=_cparams(("parallel",)),
        name="cross_attn",
    )(x2d, g, wq, kx, vx, wo)


def _mix_cross_kernel(x_ref, ya_ref, yb_ref, ga_ref, gb_ref, wa_ref, wb_ref, wo_ref,
                      g_ref, wq_ref, k_ref, v_ref, wox_ref, o_ref, x1_ref):
    _mix_kernel(x_ref, ya_ref, yb_ref, ga_ref, gb_ref, wa_ref, wb_ref, wo_ref, x1_ref)
    _cross_kernel(x1_ref, g_ref, wq_ref, k_ref, v_ref, wox_ref, o_ref)


def mix_cross(x2d, ya, yb, pg, wa, wb, wo, g, wq, kx, vx, wox, tok0, tm=512):
    t = yb.shape[0]
    assert t % tm == 0 and tok0 % tm == 0
    d = x2d.shape[1]
    w = ya.shape[1]
    m = kx.shape[1]
    b0 = tok0 // tm
    const = lambda a: pl.BlockSpec(a.shape, lambda i: (0,) * a.ndim)
    kv = pl.BlockSpec((None, m, d), lambda i: (0, 0, 0))
    return pl.pallas_call(
        _mix_cross_kernel,
        grid=(t // tm,),
        in_specs=[
            pl.BlockSpec((tm, d), lambda i: (b0 + i, 0)),
            pl.BlockSpec((tm, w), lambda i: (b0 + i, 0)),
            pl.BlockSpec((tm, w), lambda i: (i, 0)),
            pl.BlockSpec((tm, d), lambda i: (b0 + i, 0)),
            pl.BlockSpec((tm, d), lambda i: (b0 + i, 1)),
            const(wa), const(wb), const(wo), const(g), const(wq), kv, kv, const(wox),
        ],
        out_specs=pl.BlockSpec((tm, d), lambda i: (i, 0)),
        out_shape=jax.ShapeDtypeStruct((t, d), F32),
        scratch_shapes=[pltpu.VMEM((tm, d), F32)],
        compiler_params=_cparams(("parallel",)),
        name="mix_cross",
    )(x2d, ya, yb, pg, pg, wa, wb, wo, g, wq, kx, vx, wox)


def _topk_rows(sc, k):
    n = sc.shape[0]
    io = lax.broadcasted_iota(I32, sc.shape, 0)
    vals, ids = [], []
    for _ in range(k):
        m = jnp.max(sc, axis=0, keepdims=True)
        ix = jnp.min(jnp.where(sc == m, io, n), axis=0, keepdims=True)
        vals.append(m)
        ids.append(ix)
        sc = jnp.where(io == ix, NEG_INF, sc)
    return jnp.concatenate(vals, axis=0), jnp.concatenate(ids, axis=0)


def _pack_bf16_halves(h):
    bits = lax.bitcast_convert_type(h, I32)
    r = bits + 0x7FFF + (lax.shift_right_logical(bits, 16) & 1)
    half = h.shape[1] // 2
    return lax.shift_right_logical(r[:, :half], 16) | (r[:, half:] & HI_MASK)


def _route_kernel(x_ref, g_ref, wq_ref, sk_ref, hp_ref, idx_ref, w_ref, hb_ref, it_ref, wt_ref):
    p = pl.program_id(1)

    @pl.when(p == 0)
    def _():
        h = _rms(x_ref[...], g_ref[...])
        hp_ref[...] = _pack_bf16_halves(h)
        hb_ref[...] = h.astype(BF16)

    qh = jnp.dot(hb_ref[...], wq_ref[...], preferred_element_type=F32)
    tops = []
    for c in range(2):
        seg = qh[:, c * PEER_HALF:(c + 1) * PEER_HALF]
        sc = lax.dot_general(sk_ref[c], seg, (((1,), (1,)), ((), ())),
                             precision=lax.Precision.HIGHEST, preferred_element_type=F32)
        tops.append(_topk_rows(sc, PEER_TOPK))
    (s0, i0), (s1, i1) = tops
    k = PEER_TOPK
    sub = 8
    tm = s0.shape[1]
    r8 = lax.broadcasted_iota(I32, (sub, tm), 0)
    r16 = lax.broadcasted_iota(I32, (k, tm), 0)
    cand_b = [s0[0:1] + s1, s0[1:2] + s1[:sub]]
    cidx_b = [i0[0:1] * PEER_NKEYS + i1, i0[1:2] * PEER_NKEYS + i1[:sub]]
    pos_b = [r16, k + r8]
    for a in range(2, sub):
        keep = r8 < (k // (a + 1))
        cand_b.append(jnp.where(keep, s0[a:a + 1] + s1[:sub], NEG_INF))
        cidx_b.append(i0[a:a + 1] * PEER_NKEYS + i1[:sub])
        pos_b.append(a * k + r8)
    cand_b.append(s0[sub:] + s1[0:1])
    cidx_b.append(i0[sub:] * PEER_NKEYS + i1[0:1])
    pos_b.append((sub + r8) * k)
    cand = jnp.concatenate(cand_b, axis=0)
    cidx = jnp.concatenate(cidx_b, axis=0)
    pos = jnp.concatenate(pos_b, axis=0)
    vals, ids = [], []
    for _ in range(k):
        m = jnp.max(cand, axis=0, keepdims=True)
        px = jnp.min(jnp.where(cand == m, pos, k * k), axis=0, keepdims=True)
        hit = pos == px
        vals.append(m)
        ids.append(jnp.sum(jnp.where(hit, cidx, 0), axis=0, keepdims=True))
        cand = jnp.where(hit, NEG_INF, cand)
    sf = jnp.concatenate(vals, axis=0)
    e = jnp.exp(sf - sf[0:1])
    rows = pl.ds(pl.multiple_of(p * PEER_TOPK, PEER_TOPK), PEER_TOPK)
    wt_ref[rows, :] = e / jnp.sum(e, axis=0, keepdims=True)
    it_ref[rows, :] = jnp.concatenate(ids, axis=0)

    @pl.when(p == pl.num_programs(1) - 1)
    def _():
        idx_ref[...] = it_ref[...].T
        w_ref[...] = wt_ref[...].T


def peer_route(x2d, g, wq, sk, tok0, t, tm=1024):
    assert t % tm == 0 and tok0 % tm == 0
    d = x2d.shape[1]
    ph = sk.shape[0]
    nsel = ph * PEER_TOPK
    blk0 = tok0 // tm
    return pl.pallas_call(
        _route_kernel,
        grid=(t // tm, ph),
        in_specs=[
            pl.BlockSpec((tm, d), lambda i, p: (blk0 + i, 0)),
            pl.BlockSpec((1, d), lambda i, p: (0, 0)),
            pl.BlockSpec((d, 2 * PEER_HALF), lambda i, p: (0, p)),
            pl.BlockSpec((None, 2, PEER_NKEYS, PEER_HALF), lambda i, p: (p, 0, 0, 0)),
        ],
        out_specs=[
            pl.BlockSpec((tm, d // 2), lambda i, p: (i, 0)),
            pl.BlockSpec((tm, nsel), lambda i, p: (i, 0)),
            pl.BlockSpec((tm, nsel), lambda i, p: (i, 0)),
        ],
        out_shape=[jax.ShapeDtypeStruct((t, d // 2), I32),
                   jax.ShapeDtypeStruct((t, nsel), I32),
                   jax.ShapeDtypeStruct((t, nsel), F32)],
        scratch_shapes=[pltpu.VMEM((tm, d), BF16),
                        pltpu.VMEM((nsel, tm), I32),
                        pltpu.VMEM((nsel, tm), F32)],
        compiler_params=_cparams(("parallel", "arbitrary")),
        name="peer_route",
    )(x2d, g, wq, sk)


def _final_kernel(x_ref, y_ref, g_ref, o_ref):
    o_ref[...] = _rms(x_ref[...] + y_ref[...], g_ref[...])


def final_norm(x2d, y, g, tok0, tm=512):
    t, d = y.shape
    assert t % tm == 0 and tok0 % tm == 0
    blk0 = tok0 // tm
    spec = pl.BlockSpec((tm, d), lambda i: (i, 0))
    return pl.pallas_call(
        _final_kernel, grid=(t // tm,),
        in_specs=[pl.BlockSpec((tm, d), lambda i: (blk0 + i, 0)), spec, pl.BlockSpec((1, d), lambda i: (0, 0))],
        out_specs=spec,
        out_shape=jax.ShapeDtypeStruct((t, d), F32),
        compiler_params=_cparams(("parallel",)), name="final_norm",
    )(x2d, y, g)


SC_CORES = 2
SC_SUBCORES = 16
SC_WORKERS = SC_CORES * SC_SUBCORES
SC_LANES = 16
SC_GROUP = 32


def _sc_mesh():
    return plsc.VectorSubcoreMesh(core_axis_name="c", subcore_axis_name="s")


def _sc_params():
    return pltpu.CompilerParams(needs_layout_passes=False)


def _sc_worker_id():
    return lax.axis_index("s") * SC_CORES + lax.axis_index("c")


SC_ROW_LANE = 128


def _sc_unit_off(u):
    off = u * SC_LANES
    return off if isinstance(off, int) else pl.multiple_of(off, SC_LANES)


GELU_C0 = math.sqrt(2.0 / math.pi)
GELU_C1 = 0.044715


def _gelu_tanh(x):
    z = GELU_C0 * (x + GELU_C1 * (x * x * x))
    th = 1.0 - 2.0 / (jnp.exp(2.0 * z) + 1.0)
    return 0.5 * x * (1.0 + th)


SC_PK_RING = 4
SC_PK_SUB = 4
HI_MASK = -65536


def _pack_tables_kernel(u_ref, v_ref, o_ref):
    for part, ref in enumerate((u_ref, v_ref)):
        words = _pack_bf16_halves(ref[...])
        for sub in range(SC_PK_SUB):
            o_ref[:, part * SC_PK_SUB + sub, :] = words[:, sub * SC_ROW_LANE:(sub + 1) * SC_ROW_LANE]


def pack_expert_tables(u, v, te=512):
    e, d = u.shape
    assert d == 2 * SC_PK_SUB * SC_ROW_LANE
    spec = pl.BlockSpec((te, d), lambda i: (i, 0))
    return pl.pallas_call(
        _pack_tables_kernel, grid=(e // te,), in_specs=[spec, spec],
        out_specs=pl.BlockSpec((te, 2 * SC_PK_SUB, SC_ROW_LANE), lambda i: (i, 0, 0)),
        out_shape=jax.ShapeDtypeStruct((e, 2 * SC_PK_SUB, SC_ROW_LANE), I32),
        compiler_params=_cparams(("parallel",)), name="pack_expert_tables",
    )(u, v)


def _unpack_halves(x32):
    w = plsc.bitcast(x32, I32)
    return plsc.bitcast(w << 16, F32), plsc.bitcast(w & HI_MASK, F32)


def _tree_sum(xs):
    while len(xs) > 1:
        xs = [xs[i] + xs[i + 1] for i in range(0, len(xs), 2)]
    return xs[0]


def peer_experts_pk_sc(tab_uv, idx_flat, w_flat, hp, d):
    t = hp.shape[0]
    nsel = PEER_SEL
    g = SC_GROUP
    assert t % (SC_WORKERS * g) == 0 and d == 2 * SC_PK_SUB * SC_ROW_LANE
    tpw = t // SC_WORKERS
    groups = tpw // g
    heads = nsel // SC_LANES
    chunks = d // 32
    units = g * heads
    ring = SC_PK_RING
    assert units % ring == 0
    row_buf = pltpu.VMEM((SC_LANES, 2 * SC_PK_SUB, SC_ROW_LANE), I32)

    def row_words(rows, r, wc, sub0):
        per = SC_ROW_LANE // SC_LANES
        return plsc.bitcast(
            rows[r, sub0 + wc // per, pl.ds(pl.multiple_of((wc % per) * SC_LANES, SC_LANES), SC_LANES)], BF16)

    def ring_loop(n_units, start, wait, compute):
        for u in range(ring - 1):
            start(u, u)

        @pl.loop(0, n_units, step=ring)
        def _(uu):
            for b in range(ring):
                u = uu + b
                nxt = u + (ring - 1)

                @pl.when(nxt < n_units)
                def _():
                    start(nxt, (b + ring - 1) % ring)

                wait(u, b)
                compute(u, b)

    @functools.partial(
        pl.kernel, mesh=_sc_mesh(),
        out_type=jax.ShapeDtypeStruct((t, d), F32),
        scratch_types=[
            pltpu.VMEM((g * nsel,), I32),
            pltpu.VMEM((g * nsel,), F32),
            pltpu.VMEM((g, d // 2), I32),
            pltpu.VMEM((g, d), F32),
            pltpu.VMEM((SC_LANES * SC_LANES,), F32),
            [row_buf] * ring,
            [pltpu.SemaphoreType.DMA] * ring,
        ],
        compiler_params=_sc_params(),
        name="peer_experts_pk_sc",
    )
    def k(tab_hbm, idx_hbm, w_hbm, h_hbm, out_hbm, idx_v, coef_v, h_v, y_v, red_v, rows, sems):
        wid = _sc_worker_id()
        lane = lax.iota(I32, SC_LANES)

        def copy(u, slot):
            ids = idx_v.at[pl.ds(_sc_unit_off(u), SC_LANES)]
            return pltpu.make_async_copy(tab_hbm.at[ids], rows[slot], sems[slot])

        def dots(u, slot):
            tt = u // heads

            def body(cp, accs):
                out = []
                hv = [plsc.bitcast(h_v[tt, pl.ds(pl.multiple_of((2 * cp + i) * SC_LANES, SC_LANES), SC_LANES)], BF16)
                      for i in range(2)]
                for r in range(SC_LANES):
                    pr = (row_words(rows[slot], r, 2 * cp, 0) * hv[0]
                          + row_words(rows[slot], r, 2 * cp + 1, 0) * hv[1])
                    lo, hi = _unpack_halves(pr)
                    out.append(accs[r] + lo + hi)
                return tuple(out)

            accs = lax.fori_loop(0, chunks // 2, body,
                                 tuple(jnp.zeros((SC_LANES,), F32) for _ in range(SC_LANES)))
            for r in range(SC_LANES):
                red_v[pl.ds(r * SC_LANES, SC_LANES)] = accs[r]
            act = _tree_sum([plsc.load_gather(red_v, [lane * SC_LANES + j]) for j in range(SC_LANES)])
            sl = pl.ds(_sc_unit_off(u), SC_LANES)
            coef_v[sl] = coef_v[sl] * _gelu_tanh(act)

        def combine(u, slot):
            tt = u // heads
            first = (u % heads) == 0
            cb = []
            for r in range(SC_LANES):
                c = plsc.load_gather(coef_v, [jnp.full((SC_LANES,), u * SC_LANES + r, I32)])
                cb.append(plsc.pack(c, c, format=plsc.PackFormat.INTERLEAVED))

            @plsc.parallel_loop(0, chunks, unroll=2)
            def _(wc):
                lo, hi = _unpack_halves(
                    _tree_sum([cb[r] * row_words(rows[slot], r, wc, SC_PK_SUB) for r in range(SC_LANES)]))
                for half, val in ((0, lo), (1, hi)):
                    sl = pl.ds(pl.multiple_of(half * (d // 2) + wc * SC_LANES, SC_LANES), SC_LANES)
                    y_v[tt, sl] = val + jnp.where(first, 0.0, y_v[tt, sl])

        def unit(u, slot):
            dots(u, slot)
            combine(u, slot)

        @pl.loop(0, groups)
        def _(gi):
            base = wid * tpw + gi * g
            pltpu.sync_copy(idx_hbm.at[pl.ds(base * nsel, g * nsel)], idx_v)
            pltpu.sync_copy(w_hbm.at[pl.ds(base * nsel, g * nsel)], coef_v)
            pltpu.sync_copy(h_hbm.at[pl.ds(base, g)], h_v)
            ring_loop(units, lambda u, s: copy(u, s).start(), lambda u, s: copy(u, s).wait(), unit)
            pltpu.sync_copy(y_v, out_hbm.at[pl.ds(base, g)])

    return k(tab_uv, idx_flat, w_flat, hp)


def kernel(x, mem, rel_bias, ln_mix, w_in, hg_lower, hg_norm, w_up_a, w_up_b, w_out, ln_cross, ln_mem, wq_x, wk_x, wv_x, wo_x, ln_ffn, peer_query, peer_subkeys, peer_u, peer_v, ln_final):
    b, s, d = x.shape
    depth = w_in.shape[0]
    assert depth == 1, "the residual after PEER is fused into the final norm"
    assert s % MB_BLOCK == 0 and s % HG_CHUNK == 0 and s % (PEER_SLICES * SC_WORKERS * SC_GROUP) == 0
    nb = s // MB_BLOCK
    row = lambda a: a.reshape(1, -1).astype(F32)
    lb_all = jnp.cumsum(jax.nn.softmax(hg_lower.astype(F32), axis=0), axis=0)
    bias = moba_bias_tiles(rel_bias)
    n_hg = 4 * HG_WIDTH
    n_qk = 2 * MB_WIDTH
    n_mb = 3 * MB_WIDTH
    l = 0
    w = w_in[l].astype(BF16)
    w_hg, w_qk, w_vt, w_g = w[:, :n_hg], w[:, n_hg:n_hg + n_qk], w[:, n_hg + n_qk:n_hg + n_mb].T, w[:, n_hg + n_mb:]
    wa, wb, wo = w_up_a[l].astype(BF16), w_up_b[l].astype(BF16), w_out[l].astype(BF16)
    wqx, wox = wq_x[l].astype(BF16), wo_x[l].astype(BF16)
    wpq, sk = peer_query[l].astype(BF16), peer_subkeys[l].astype(F32)
    tab_uv = pack_expert_tables(peer_u[l].astype(F32), peer_v[l].astype(F32))
    kx, vx = mem_kv(mem, row(ln_mem[l]), wk_x[l].astype(BF16), wv_x[l].astype(BF16))

    outs = []
    for bi in range(b):
        x2d = x[bi]
        p0, pqk, vt, pg = in_proj(x2d, row(ln_mix[l]), w_hg, w_qk, w_vt, w_g)
        ya = hgrn2(p0, row(lb_all[l]), row(hg_norm[l]), 1, s)
        km = moba_kmean(pqk, 1, s).reshape(1, nb, MB_WIDTH)
        ts = s // PEER_SLICES
        for tok0 in range(0, s, ts):
            yb = moba_attention(pqk, vt, km, bias, 1, s, tok0 // MB_BLOCK, ts // MB_BLOCK)
            xs = mix_cross(x2d, ya, yb, pg, wa, wb, wo, row(ln_cross[l]), wqx, kx[bi:bi + 1], vx[bi:bi + 1], wox, tok0)
            hp, eidx, wts = peer_route(xs, row(ln_ffn[l]), wpq, sk, 0, ts)
            y = peer_experts_pk_sc(tab_uv, eidx.reshape(ts * PEER_SEL), wts.reshape(ts * PEER_SEL), hp, d)
            outs.append(final_norm(xs, y, row(ln_final), 0))
    return jnp.concatenate(outs, axis=0).reshape(b, s, d)
```

```python
import functools
import math

import jax
import jax.numpy as jnp
from jax import lax
from jax.experimental import pallas as pl
from jax.experimental.pallas import tpu as pltpu
from jax.experimental.pallas import tpu_sc as plsc

F32 = jnp.float32
BF16 = jnp.bfloat16
I32 = jnp.int32
EPS = 1e-6
NEG_INF = float("-inf")

HG_HEADS = 4
HG_D = 128
HG_WIDTH = HG_HEADS * HG_D
HG_CHUNK = 64
HG_SUB = 16
MB_HEADS = 8
MB_DH = 64
MB_WIDTH = MB_HEADS * MB_DH
MB_BLOCK = 256
MB_TOPK = 3
MB_BIAS_TILES = 8
REL_BUCKETS = 32
REL_MAX_DIST = 2048
X_HEADS = 4
PEER_HEADS = 8
PEER_NKEYS = 128
PEER_TOPK = 16
PEER_HALF = 128
PEER_SEL = PEER_HEADS * PEER_TOPK
PEER_SLICES = 4

VMEM_LIMIT = 56 * 1024 * 1024


def _cparams(sem):
    return pltpu.CompilerParams(dimension_semantics=sem, vmem_limit_bytes=VMEM_LIMIT)


def _rms(x, g):
    ms = jnp.mean(x * x, axis=-1, keepdims=True)
    return x * lax.rsqrt(ms + EPS) * g


def _in_proj_kernel(x_ref, g_ref, w0_ref, w1_ref, wvt_ref, w2_ref, o0_ref, o1_ref, okm_ref, ovt_ref, o2_ref):
    h = _rms(x_ref[...], g_ref[...]).astype(BF16)
    o0_ref[...] = jnp.dot(h, w0_ref[...], preferred_element_type=F32)
    qk = jnp.dot(h, w1_ref[...], preferred_element_type=F32)
    o1_ref[...] = qk.astype(BF16)
    okm_ref[0] = jnp.mean(qk[:, MB_WIDTH:], axis=0, keepdims=True)
    vt = lax.dot_general(wvt_ref[...], h, (((1,), (1,)), ((), ())), preferred_element_type=F32).astype(BF16)
    for hd in range(MB_HEADS):
        ovt_ref[0, hd * MB_VROWS:hd * MB_VROWS + MB_DH, :] = vt[hd * MB_DH:(hd + 1) * MB_DH]
        ovt_ref[0, hd * MB_VROWS + MB_DH:(hd + 1) * MB_VROWS, :] = jnp.ones((MB_ONES, vt.shape[1]), BF16)
    o2_ref[...] = jnp.dot(h, w2_ref[...], preferred_element_type=F32).astype(BF16)


def in_proj(x2d, g, w0, w1, wvt, w2):
    t, d = x2d.shape
    tm = MB_BLOCK
    assert wvt.shape[0] == MB_WIDTH and w1.shape[1] == 2 * MB_WIDTH
    n0, n1, nv, n2 = w0.shape[1], w1.shape[1], MB_VT_ROWS, w2.shape[1]
    full = lambda a: pl.BlockSpec(a.shape, lambda i: (0, 0))
    return pl.pallas_call(
        _in_proj_kernel,
        grid=(t // tm,),
        in_specs=[pl.BlockSpec((tm, d), lambda i: (i, 0)), full(g), full(w0), full(w1), full(wvt), full(w2)],
        out_specs=[pl.BlockSpec((tm, n0), lambda i: (i, 0)),
                   pl.BlockSpec((tm, n1), lambda i: (i, 0)),
                   pl.BlockSpec((1, 1, MB_WIDTH), lambda i: (i, 0, 0)),
                   pl.BlockSpec((1, nv, tm), lambda i: (i, 0, 0)),
                   pl.BlockSpec((tm, n2), lambda i: (i, 0))],
        out_shape=[jax.ShapeDtypeStruct((t, n0), F32),
                   jax.ShapeDtypeStruct((t, n1), BF16),
                   jax.ShapeDtypeStruct((t // tm, 1, MB_WIDTH), F32),
                   jax.ShapeDtypeStruct((t // tm, nv, tm), BF16),
                   jax.ShapeDtypeStruct((t, n2), BF16)],
        compiler_params=_cparams(("parallel",)),
        name="in_proj",
    )(x2d, g, w0, w1, wvt, w2)


def _hgrn_kernel(q_ref, f_ref, i_ref, g_ref, lb_ref, gain_ref, o_ref, st_ref):
    c = pl.program_id(1)

    @pl.when(c == 0)
    def _():
        st_ref[...] = jnp.zeros_like(st_ref)

    C, S = HG_CHUNK, HG_SUB
    row = lax.broadcasted_iota(I32, (C, C), 0)
    col = lax.broadcasted_iota(I32, (C, C), 1)
    tril = (row >= col).astype(F32)
    t_iota = lax.broadcasted_iota(I32, (S, 1), 0)

    for h in range(HG_HEADS):
        sl = slice(h * HG_D, (h + 1) * HG_D)
        q = q_ref[:, sl]
        v = i_ref[:, sl]
        lb = lb_ref[:, sl]
        f = lb + (1.0 - lb) * jax.nn.sigmoid(f_ref[:, sl])
        lf = jnp.log(f)
        k = 1.0 - f
        b = jnp.dot(tril, lf, precision=lax.Precision.HIGHEST, preferred_element_type=F32)
        st = st_ref[h]
        vb = v.astype(BF16)
        qd = (q * jnp.exp(b)).astype(BF16)
        o_inter = lax.dot_general(qd, st.astype(BF16), (((1,), (1,)), ((), ())),
                                  preferred_element_type=F32)
        outs = []
        for i in range(C // S):
            r0 = i * S
            qi = q[r0:r0 + S]
            ki = k[r0:r0 + S]
            bi = b[r0:r0 + S]
            vi = v[r0:r0 + S]
            oi = o_inter[r0:r0 + S]
            if i > 0:
                bs = b[r0 - 1:r0]
                qh = (qi * jnp.exp(bi - bs)).astype(BF16)
                kh = (k[:r0] * jnp.exp(bs - b[:r0])).astype(BF16)
                a = lax.dot_general(qh, kh, (((1,), (1,)), ((), ())), preferred_element_type=F32)
                oi = oi + jnp.dot(a.astype(BF16), vb[:r0], preferred_element_type=F32)
            half = S // 2
            o_half = [oi[:half], oi[half:]]
            for s in range(S):
                for hf in range(s // half, 2):
                    rows = slice(hf * half, (hf + 1) * half)
                    dec = jnp.exp(jnp.minimum(bi[rows] - bi[s:s + 1], 0.0))
                    a_s = jnp.sum(qi[rows] * ki[s:s + 1] * dec, axis=-1, keepdims=True)
                    a_s = jnp.where(t_iota[rows] >= s, a_s, 0.0)
                    o_half[hf] = o_half[hf] + a_s * vi[s:s + 1]
            outs.extend(o_half)
        o = jnp.concatenate(outs, axis=0)
        b_end = b[C - 1:C]
        kd = (k * jnp.exp(b_end - b)).astype(BF16)
        upd = lax.dot_general(vb, kd, (((0,), (0,)), ((), ())), preferred_element_type=F32)
        st_ref[h] = st * jnp.exp(b_end) + upd
        o = o * lax.rsqrt(jnp.mean(o * o, axis=-1, keepdims=True) + EPS)
        g = g_ref[:, sl]
        o_ref[:, sl] = (o * gain_ref[:, sl] * (g * jax.nn.sigmoid(g))).astype(o_ref.dtype)


def hgrn2(p0, lb, gain, batch, seq):
    t = p0.shape[0]
    nc = seq // HG_CHUNK
    w = HG_WIDTH

    def col(j):
        return pl.BlockSpec((HG_CHUNK, w), lambda b, c, j=j: (b * nc + c, j))

    return pl.pallas_call(
        _hgrn_kernel,
        grid=(batch, nc),
        in_specs=[col(0), col(1), col(2), col(3),
                  pl.BlockSpec((1, w), lambda b, c: (0, 0)),
                  pl.BlockSpec((1, w), lambda b, c: (0, 0))],
        out_specs=pl.BlockSpec((HG_CHUNK, w), lambda b, c: (b * nc + c, 0)),
        out_shape=jax.ShapeDtypeStruct((t, w), BF16),
        scratch_shapes=[pltpu.VMEM((HG_HEADS, HG_D, HG_D), F32)],
        compiler_params=_cparams(("parallel", "arbitrary")),
        name="hgrn2",
    )(p0, p0, p0, p0, lb, gain)


MB_PAIR = 4
MB_PW = MB_PAIR * MB_DH
MB_LG = 128
MB_ONES = 16
MB_VROWS = MB_DH + MB_ONES
MB_VT_ROWS = MB_HEADS * MB_VROWS


def _moba_kernel(q_ref, k_ref, vt_ref, km_ref, bias_ref, o_ref, *scratch, qb0):
    m_ref, l_ref, al_ref, acc_ref, msk_ref, s_ref, p_ref = (
        scratch[i * MB_PAIR:(i + 1) * MB_PAIR] for i in range(7))
    qi = pl.program_id(2) + qb0
    nb = km_ref.shape[0]
    blk = MB_BLOCK
    heads = range(MB_PAIR)
    grp = lambda hh: slice((hh // 2) * MB_LG, (hh // 2 + 1) * MB_LG)
    q = q_ref[...]
    lane = lax.broadcasted_iota(I32, (blk, MB_LG), 1)
    in_head = [(lane < MB_DH) if hh % 2 == 0 else (lane >= MB_DH) for hh in heads]
    qs = q * jnp.asarray(MB_DH ** -0.5, BF16)
    nt = (((1,), (1,)), ((), ()))
    qf = q.astype(F32)
    qht = [jnp.where(in_head[hh], qs[:, grp(hh)].astype(F32), 0.0).T.astype(BF16) for hh in heads]

    n_io = lax.broadcasted_iota(I32, (nb, blk), 0)
    for hh in heads:
        gate = lax.dot_general(km_ref[:, grp(hh)], jnp.where(in_head[hh], qf[:, grp(hh)], 0.0), nt,
                               precision=lax.Precision.HIGHEST, preferred_element_type=F32)
        gate = jnp.where(n_io < qi, gate, NEG_INF)
        chosen = n_io < 0
        for _ in range(MB_TOPK):
            mx = jnp.max(gate, axis=0, keepdims=True)
            ix = jnp.min(jnp.where(gate == mx, n_io, nb), axis=0, keepdims=True)
            hit = n_io == ix
            chosen = chosen | (hit & (mx > NEG_INF))
            gate = jnp.where(hit, NEG_INF, gate)
        msk_ref[hh][...] = jnp.where(chosen, 0.0, NEG_INF)

    vrows = lambda hh: slice(hh * MB_VROWS, (hh + 1) * MB_VROWS)

    def pv_stage(blk_idx):
        vtb = vt_ref[blk_idx]
        r = [jnp.dot(vtb[vrows(hh)], p_ref[hh][...], preferred_element_type=F32) for hh in heads]
        al = [al_ref[hh][...] for hh in heads]
        a_new = [al[hh] * acc_ref[hh][...] + r[hh][:MB_DH] for hh in heads]
        l_new = [al[hh] * l_ref[hh][...] + r[hh][MB_DH:MB_DH + 1] for hh in heads]
        return a_new, l_new

    def store_pv(a_new, l_new):
        for hh in heads:
            acc_ref[hh][...] = a_new[hh]
            l_ref[hh][...] = l_new[hh]

    def softmax_stage():
        s = [s_ref[hh][...] for hh in heads]
        m_old = [m_ref[hh][...] for hh in heads]
        m_new = [jnp.maximum(m_old[hh], jnp.max(s[hh], axis=0, keepdims=True)) for hh in heads]
        alpha = [jnp.exp(m_old[hh] - m_new[hh]) for hh in heads]
        p = [jnp.exp((s[hh] - m_new[hh]).astype(BF16)) for hh in heads]
        return p, alpha, m_new

    def store_softmax(p, alpha, m_new):
        for hh in heads:
            p_ref[hh][...] = p[hh]
            al_ref[hh][...] = alpha[hh]
            m_ref[hh][...] = m_new[hh]

    k_own = k_ref[pl.ds(pl.multiple_of(qi * blk, blk), blk), :]
    key_io = lax.broadcasted_iota(I32, (blk, blk), 0)
    qry_io = lax.broadcasted_iota(I32, (blk, blk), 1)
    for hh in heads:
        s = jnp.dot(k_own[:, grp(hh)], qht[hh], preferred_element_type=F32) + bias_ref[hh, 0]
        s_ref[hh][...] = jnp.where(key_io <= qry_io, s, NEG_INF)
        m_ref[hh][...] = jnp.full((1, blk), NEG_INF, F32)
        l_ref[hh][...] = jnp.zeros((1, blk), F32)
        al_ref[hh][...] = jnp.ones((1, blk), F32)
        acc_ref[hh][...] = jnp.zeros((MB_DH, blk), F32)
        p_ref[hh][...] = jnp.zeros((blk, blk), BF16)

    def step(i, carry, far):
        pv = pv_stage(jnp.where(i <= 1, qi, i - 2))
        sm = softmax_stage()
        kn = k_ref[pl.ds(pl.multiple_of(i * blk, blk), blk), :]
        if far:
            row = [msk_ref[hh][pl.ds(i, 1), :] + bias_ref[hh, MB_BIAS_TILES - 1, 0:1, 0:1] for hh in heads]
            s_next = [jnp.dot(kn[:, grp(hh)], qht[hh], preferred_element_type=F32) + row[hh] for hh in heads]
        else:
            d = qi - i
            s_next = [jnp.dot(kn[:, grp(hh)], qht[hh], preferred_element_type=F32)
                      + bias_ref[hh, d] + msk_ref[hh][pl.ds(i, 1), :] for hh in heads]
        store_pv(*pv)
        for hh in heads:
            s_ref[hh][...] = s_next[hh]
        store_softmax(*sm)
        return carry

    n_far = jnp.maximum(qi - (MB_BIAS_TILES - 2), 0)
    lax.fori_loop(0, n_far, functools.partial(step, far=True), 0)
    lax.fori_loop(n_far, qi, functools.partial(step, far=False), 0)
    pv = pv_stage(jnp.where(qi <= 1, qi, qi - 2))
    sm = softmax_stage()
    store_pv(*pv)
    store_softmax(*sm)
    a_fin, l_fin = pv_stage(jnp.where(qi == 0, qi, qi - 1))
    out_t = jnp.concatenate([a_fin[hh] / l_fin[hh] for hh in heads], axis=0)
    o_ref[...] = out_t.T.astype(o_ref.dtype)


def moba_attention(pqk, vt, km, bias, batch, seq, qb0=0, nqb=None):
    nb = seq // MB_BLOCK
    nqb = nb if nqb is None else nqb
    t = batch * nqb * MB_BLOCK
    groups = MB_WIDTH // MB_PW
    return pl.pallas_call(
        functools.partial(_moba_kernel, qb0=qb0),
        grid=(batch, groups, nqb),
        in_specs=[
            pl.BlockSpec((MB_BLOCK, MB_PW), lambda b, j, i: (b * nb + qb0 + i, j)),
            pl.BlockSpec((seq, MB_PW), lambda b, j, i: (b, groups + j)),
            pl.BlockSpec((nb, MB_PAIR * MB_VROWS, MB_BLOCK), lambda b, j, i: (b, j, 0)),
            pl.BlockSpec((None, nb, MB_PW), lambda b, j, i: (b, 0, j)),
            pl.BlockSpec((MB_PAIR, MB_BIAS_TILES, MB_BLOCK, MB_BLOCK), lambda b, j, i: (j, 0, 0, 0)),
        ],
        out_specs=pl.BlockSpec((MB_BLOCK, MB_PW), lambda b, j, i: (b * nqb + i, j)),
        out_shape=jax.ShapeDtypeStruct((t, MB_WIDTH), BF16),
        scratch_shapes=(
            [pltpu.VMEM((1, MB_BLOCK), F32)] * (3 * MB_PAIR)
            + [pltpu.VMEM((MB_DH, MB_BLOCK), F32)] * MB_PAIR
            + [pltpu.VMEM((nb, MB_BLOCK), F32)] * MB_PAIR
            + [pltpu.VMEM((MB_BLOCK, MB_BLOCK), F32)] * MB_PAIR
            + [pltpu.VMEM((MB_BLOCK, MB_BLOCK), BF16)] * MB_PAIR
        ),
        compiler_params=_cparams(("parallel", "parallel", "arbitrary")),
        name="moba_attn",
    )(pqk, pqk, vt, km, bias)


def _t5_bucket(dist):
    max_exact = REL_BUCKETS // 2
    scaled = jnp.log(jnp.maximum(dist, 1).astype(F32) / max_exact) / math.log(REL_MAX_DIST / max_exact)
    large = jnp.minimum(max_exact + (scaled * (REL_BUCKETS - max_exact)).astype(I32), REL_BUCKETS - 1)
    return jnp.where(dist < max_exact, dist, large)


def moba_bias_tiles(rel_bias):
    blk = MB_BLOCK
    span = 2 * blk - 1
    x = jnp.arange(span) - (blk - 1)
    dist = jnp.maximum(jnp.arange(MB_BIAS_TILES)[:, None] * blk + x[None, :], 0)
    w = rel_bias.astype(F32).T[:, _t5_bucket(dist)]
    h = w.shape[0]
    wp = jnp.pad(w, ((0, 0), (0, 0), (0, 1)))
    a = jnp.broadcast_to(wp[:, :, None, :], (h, MB_BIAS_TILES, blk, span + 1))
    a = a.reshape(h, MB_BIAS_TILES, blk * (span + 1))[:, :, :blk * span]
    return a.reshape(h, MB_BIAS_TILES, blk, span)[:, :, :, blk - 1:]


def _mix_kernel(x_ref, ya_ref, yb_ref, ga_ref, gb_ref, wa_ref, wb_ref, wo_ref, o_ref):
    za = jnp.dot(ya_ref[...], wa_ref[...], preferred_element_type=F32)
    zb = jnp.dot(yb_ref[...], wb_ref[...], preferred_element_type=F32)
    z = jax.nn.sigmoid(ga_ref[...].astype(F32)) * za + jax.nn.sigmoid(gb_ref[...].astype(F32)) * zb
    o_ref[...] = x_ref[...] + jnp.dot(z.astype(BF16), wo_ref[...], preferred_element_type=F32)


def _mem_kv_kernel(m_ref, g_ref, wk_ref, wv_ref, k_ref, v_ref):
    mn = _rms(m_ref[...], g_ref[...]).astype(BF16)
    k_ref[...] = jnp.dot(mn, wk_ref[...], preferred_element_type=F32).astype(BF16)
    v_ref[...] = jnp.dot(mn, wv_ref[...], preferred_element_type=F32).astype(BF16)


def mem_kv(mem, g, wk, wv):
    b, m, d = mem.shape
    spec = pl.BlockSpec((None, m, d), lambda i: (i, 0, 0))
    wspec = pl.BlockSpec((d, d), lambda i: (0, 0))
    return pl.pallas_call(
        _mem_kv_kernel,
        grid=(b,),
        in_specs=[spec, pl.BlockSpec((1, d), lambda i: (0, 0)), wspec, wspec],
        out_specs=[spec, spec],
        out_shape=[jax.ShapeDtypeStruct((b, m, d), BF16)] * 2,
        compiler_params=_cparams(("parallel",)),
        name="mem_kv",
    )(mem, g, wk, wv)


def _cross_kernel(x_ref, g_ref, wq_ref, k_ref, v_ref, wo_ref, o_ref):
    x = x_ref[...]
    d = x.shape[1]
    dh = d // X_HEADS
    h = _rms(x, g_ref[...]).astype(BF16)
    q = (jnp.dot(h, wq_ref[...], preferred_element_type=F32) * (dh ** -0.5)).astype(BF16)
    outs = []
    for hh in range(X_HEADS):
        sl = slice(hh * dh, (hh + 1) * dh)
        s = lax.dot_general(q[:, sl], k_ref[:, sl], (((1,), (1,)), ((), ())),
                            preferred_element_type=F32)
        p = jnp.exp(s - jnp.max(s, axis=1, keepdims=True))
        l = jnp.sum(p, axis=1, keepdims=True)
        o = jnp.dot(p.astype(BF16), v_ref[:, sl], preferred_element_type=F32) / l
        outs.append(o.astype(BF16))
    o = jnp.concatenate(outs, axis=1)
    o_ref[...] = x + jnp.dot(o, wo_ref[...], preferred_element_type=F32)


def _mix_cross_kernel(x_ref, ya_ref, yb_ref, ga_ref, gb_ref, wa_ref, wb_ref, wo_ref,
                      g_ref, wq_ref, k_ref, v_ref, wox_ref, o_ref, x1_ref):
    _mix_kernel(x_ref, ya_ref, yb_ref, ga_ref, gb_ref, wa_ref, wb_ref, wo_ref, x1_ref)
    _cross_kernel(x1_ref, g_ref, wq_ref, k_ref, v_ref, wox_ref, o_ref)


def mix_cross(x2d, ya, yb, pg, wa, wb, wo, g, wq, kx, vx, wox, tok0, tm=512):
    t = yb.shape[0]
    assert t % tm == 0 and tok0 % tm == 0
    d = x2d.shape[1]
    w = ya.shape[1]
    m = kx.shape[1]
    b0 = tok0 // tm
    const = lambda a: pl.BlockSpec(a.shape, lambda i: (0,) * a.ndim)
    kv = pl.BlockSpec((None, m, d), lambda i: (0, 0, 0))
    return pl.pallas_call(
        _mix_cross_kernel,
        grid=(t // tm,),
        in_specs=[
            pl.BlockSpec((tm, d), lambda i: (b0 + i, 0)),
            pl.BlockSpec((tm, w), lambda i: (b0 + i, 0)),
            pl.BlockSpec((tm, w), lambda i: (i, 0)),
            pl.BlockSpec((tm, d), lambda i: (b0 + i, 0)),
            pl.BlockSpec((tm, d), lambda i: (b0 + i, 1)),
            const(wa), const(wb), const(wo), const(g), const(wq), kv, kv, const(wox),
        ],
        out_specs=pl.BlockSpec((tm, d), lambda i: (i, 0)),
        out_shape=jax.ShapeDtypeStruct((t, d), F32),
        scratch_shapes=[pltpu.VMEM((tm, d), F32)],
        compiler_params=_cparams(("parallel",)),
        name="mix_cross",
    )(x2d, ya, yb, pg, pg, wa, wb, wo, g, wq, kx, vx, wox)


def _topk_rows(sc, k):
    n = sc.shape[0]
    io = lax.broadcasted_iota(I32, sc.shape, 0)
    vals, ids = [], []
    for _ in range(k):
        m = jnp.max(sc, axis=0, keepdims=True)
        ix = jnp.min(jnp.where(sc == m, io, n), axis=0, keepdims=True)
        vals.append(m)
        ids.append(ix)
        sc = jnp.where(io == ix, NEG_INF, sc)
    return jnp.concatenate(vals, axis=0), jnp.concatenate(ids, axis=0)


def _pack_bf16_halves(h):
    bits = lax.bitcast_convert_type(h, I32)
    r = bits + 0x7FFF + (lax.shift_right_logical(bits, 16) & 1)
    half = h.shape[1] // 2
    return lax.shift_right_logical(r[:, :half], 16) | (r[:, half:] & HI_MASK)


def _route_kernel(x_ref, g_ref, wq_ref, sk_ref, hp_ref, idx_ref, w_ref, hb_ref, it_ref, wt_ref):
    p = pl.program_id(1)

    @pl.when(p == 0)
    def _():
        h = _rms(x_ref[...], g_ref[...])
        hp_ref[...] = _pack_bf16_halves(h)
        hb_ref[...] = h.astype(BF16)

    qh = jnp.dot(hb_ref[...], wq_ref[...], preferred_element_type=F32)
    tops = []
    for c in range(2):
        seg = qh[:, c * PEER_HALF:(c + 1) * PEER_HALF]
        sc = lax.dot_general(sk_ref[c], seg, (((1,), (1,)), ((), ())),
                             precision=lax.Precision.HIGHEST, preferred_element_type=F32)
        tops.append(_topk_rows(sc, PEER_TOPK))
    (s0, i0), (s1, i1) = tops
    k = PEER_TOPK
    sub = 8
    tm = s0.shape[1]
    r8 = lax.broadcasted_iota(I32, (sub, tm), 0)
    r16 = lax.broadcasted_iota(I32, (k, tm), 0)
    cand_b = [s0[0:1] + s1, s0[1:2] + s1[:sub]]
    cidx_b = [i0[0:1] * PEER_NKEYS + i1, i0[1:2] * PEER_NKEYS + i1[:sub]]
    pos_b = [r16, k + r8]
    for a in range(2, sub):
        keep = r8 < (k // (a + 1))
        cand_b.append(jnp.where(keep, s0[a:a + 1] + s1[:sub], NEG_INF))
        cidx_b.append(i0[a:a + 1] * PEER_NKEYS + i1[:sub])
        pos_b.append(a * k + r8)
    cand_b.append(s0[sub:] + s1[0:1])
    cidx_b.append(i0[sub:] * PEER_NKEYS + i1[0:1])
    pos_b.append((sub + r8) * k)
    cand = jnp.concatenate(cand_b, axis=0)
    cidx = jnp.concatenate(cidx_b, axis=0)
    pos = jnp.concatenate(pos_b, axis=0)
    vals, ids = [], []
    for _ in range(k):
        m = jnp.max(cand, axis=0, keepdims=True)
        px = jnp.min(jnp.where(cand == m, pos, k * k), axis=0, keepdims=True)
        hit = pos == px
        vals.append(m)
        ids.append(jnp.sum(jnp.where(hit, cidx, 0), axis=0, keepdims=True))
        cand = jnp.where(hit, NEG_INF, cand)
    sf = jnp.concatenate(vals, axis=0)
    e = jnp.exp(sf - sf[0:1])
    rows = pl.ds(pl.multiple_of(p * PEER_TOPK, PEER_TOPK), PEER_TOPK)
    wt_ref[rows, :] = e / jnp.sum(e, axis=0, keepdims=True)
    it_ref[rows, :] = jnp.concatenate(ids, axis=0)

    @pl.when(p == pl.num_programs(1) - 1)
    def _():
        idx_ref[...] = it_ref[...].T
        w_ref[...] = wt_ref[...].T


def peer_route(x2d, g, wq, sk, tok0, t, tm=1024):
    assert t % tm == 0 and tok0 % tm == 0
    d = x2d.shape[1]
    ph = sk.shape[0]
    nsel = ph * PEER_TOPK
    blk0 = tok0 // tm
    return pl.pallas_call(
        _route_kernel,
        grid=(t // tm, ph),
        in_specs=[
            pl.BlockSpec((tm, d), lambda i, p: (blk0 + i, 0)),
            pl.BlockSpec((1, d), lambda i, p: (0, 0)),
            pl.BlockSpec((d, 2 * PEER_HALF), lambda i, p: (0, p)),
            pl.BlockSpec((None, 2, PEER_NKEYS, PEER_HALF), lambda i, p: (p, 0, 0, 0)),
        ],
        out_specs=[
            pl.BlockSpec((tm, d // 2), lambda i, p: (i, 0)),
            pl.BlockSpec((tm, nsel), lambda i, p: (i, 0)),
            pl.BlockSpec((tm, nsel), lambda i, p: (i, 0)),
        ],
        out_shape=[jax.ShapeDtypeStruct((t, d // 2), I32),
                   jax.ShapeDtypeStruct((t, nsel), I32),
                   jax.ShapeDtypeStruct((t, nsel), F32)],
        scratch_shapes=[pltpu.VMEM((tm, d), BF16),
                        pltpu.VMEM((nsel, tm), I32),
                        pltpu.VMEM((nsel, tm), F32)],
        compiler_params=_cparams(("parallel", "arbitrary")),
        name="peer_route",
    )(x2d, g, wq, sk)


def _final_kernel(x_ref, y_ref, g_ref, o_ref):
    o_ref[...] = _rms(x_ref[...] + y_ref[...], g_ref[...])


def final_norm(x2d, y, g, tok0, tm=512):
    t, d = y.shape
    assert t % tm == 0 and tok0 % tm == 0
    blk0 = tok0 // tm
    spec = pl.BlockSpec((tm, d), lambda i: (i, 0))
    return pl.pallas_call(
        _final_kernel, grid=(t // tm,),
        in_specs=[pl.BlockSpec((tm, d), lambda i: (blk0 + i, 0)), spec, pl.BlockSpec((1, d), lambda i: (0, 0))],
        out_specs=spec,
        out_shape=jax.ShapeDtypeStruct((t, d), F32),
        compiler_params=_cparams(("parallel",)), name="final_norm",
    )(x2d, y, g)


SC_CORES = 2
SC_SUBCORES = 16
SC_WORKERS = SC_CORES * SC_SUBCORES
SC_LANES = 16
SC_GROUP = 32


def _sc_mesh():
    return plsc.VectorSubcoreMesh(core_axis_name="c", subcore_axis_name="s")


def _sc_params():
    return pltpu.CompilerParams(needs_layout_passes=False)


def _sc_worker_id():
    return lax.axis_index("s") * SC_CORES + lax.axis_index("c")


SC_ROW_LANE = 128


def _sc_unit_off(u):
    off = u * SC_LANES
    return off if isinstance(off, int) else pl.multiple_of(off, SC_LANES)


GELU_C0 = math.sqrt(2.0 / math.pi)
GELU_C1 = 0.044715


def _gelu_tanh(x):
    z = GELU_C0 * (x + GELU_C1 * (x * x * x))
    th = 1.0 - 2.0 / (jnp.exp(2.0 * z) + 1.0)
    return 0.5 * x * (1.0 + th)


SC_PK_RING = 4
SC_PK_SUB = 4
HI_MASK = -65536


def _pack_tables_kernel(u_ref, v_ref, o_ref):
    for part, ref in enumerate((u_ref, v_ref)):
        words = _pack_bf16_halves(ref[...])
        for sub in range(SC_PK_SUB):
            o_ref[:, part * SC_PK_SUB + sub, :] = words[:, sub * SC_ROW_LANE:(sub + 1) * SC_ROW_LANE]


def pack_expert_tables(u, v, te=512):
    e, d = u.shape
    assert d == 2 * SC_PK_SUB * SC_ROW_LANE
    spec = pl.BlockSpec((te, d), lambda i: (i, 0))
    return pl.pallas_call(
        _pack_tables_kernel, grid=(e // te,), in_specs=[spec, spec],
        out_specs=pl.BlockSpec((te, 2 * SC_PK_SUB, SC_ROW_LANE), lambda i: (i, 0, 0)),
        out_shape=jax.ShapeDtypeStruct((e, 2 * SC_PK_SUB, SC_ROW_LANE), I32),
        compiler_params=_cparams(("parallel",)), name="pack_expert_tables",
    )(u, v)


def _unpack_halves(x32):
    w = plsc.bitcast(x32, I32)
    return plsc.bitcast(w << 16, F32), plsc.bitcast(w & HI_MASK, F32)


def _tree_sum(xs):
    while len(xs) > 1:
        xs = [xs[i] + xs[i + 1] for i in range(0, len(xs), 2)]
    return xs[0]


def peer_experts_pk_sc(tab_uv, idx_flat, w_flat, hp, d):
    t = hp.shape[0]
    nsel = PEER_SEL
    g = SC_GROUP
    assert t % (SC_WORKERS * g) == 0 and d == 2 * SC_PK_SUB * SC_ROW_LANE
    tpw = t // SC_WORKERS
    groups = tpw // g
    heads = nsel // SC_LANES
    chunks = d // 32
    units = g * heads
    ring = SC_PK_RING
    assert units % ring == 0
    row_buf = pltpu.VMEM((SC_LANES, 2 * SC_PK_SUB, SC_ROW_LANE), I32)

    def row_words(rows, r, wc, sub0):
        per = SC_ROW_LANE // SC_LANES
        return plsc.bitcast(
            rows[r, sub0 + wc // per, pl.ds(pl.multiple_of((wc % per) * SC_LANES, SC_LANES), SC_LANES)], BF16)

    def ring_loop(n_units, start, wait, compute):
        for u in range(ring - 1):
            start(u, u)

        @pl.loop(0, n_units, step=ring)
        def _(uu):
            for b in range(ring):
                u = uu + b
                nxt = u + (ring - 1)

                @pl.when(nxt < n_units)
                def _():
                    start(nxt, (b + ring - 1) % ring)

                wait(u, b)
                compute(u, b)

    @functools.partial(
        pl.kernel, mesh=_sc_mesh(),
        out_type=jax.ShapeDtypeStruct((t, d), F32),
        scratch_types=[
            pltpu.VMEM((g * nsel,), I32),
            pltpu.VMEM((g * nsel,), F32),
            pltpu.VMEM((g, d // 2), I32),
            pltpu.VMEM((g, d), F32),
            pltpu.VMEM((SC_LANES * SC_LANES,), F32),
            [row_buf] * ring,
            [pltpu.SemaphoreType.DMA] * ring,
        ],
        compiler_params=_sc_params(),
        name="peer_experts_pk_sc",
    )
    def k(tab_hbm, idx_hbm, w_hbm, h_hbm, out_hbm, idx_v, coef_v, h_v, y_v, red_v, rows, sems):
        wid = _sc_worker_id()
        lane = lax.iota(I32, SC_LANES)

        def copy(u, slot):
            ids = idx_v.at[pl.ds(_sc_unit_off(u), SC_LANES)]
            return pltpu.make_async_copy(tab_hbm.at[ids], rows[slot], sems[slot])

        def dots(u, slot):
            tt = u // heads

            def body(cp, accs):
                out = []
                hv = [plsc.bitcast(h_v[tt, pl.ds(pl.multiple_of((2 * cp + i) * SC_LANES, SC_LANES), SC_LANES)], BF16)
                      for i in range(2)]
                for r in range(SC_LANES):
                    pr = (row_words(rows[slot], r, 2 * cp, 0) * hv[0]
                          + row_words(rows[slot], r, 2 * cp + 1, 0) * hv[1])
                    lo, hi = _unpack_halves(pr)
                    out.append(accs[r] + lo + hi)
                return tuple(out)

            accs = lax.fori_loop(0, chunks // 2, body,
                                 tuple(jnp.zeros((SC_LANES,), F32) for _ in range(SC_LANES)))
            for r in range(SC_LANES):
                red_v[pl.ds(r * SC_LANES, SC_LANES)] = accs[r]
            act = _tree_sum([plsc.load_gather(red_v, [lane * SC_LANES + j]) for j in range(SC_LANES)])
            sl = pl.ds(_sc_unit_off(u), SC_LANES)
            coef_v[sl] = coef_v[sl] * _gelu_tanh(act)

        def combine(u, slot):
            tt = u // heads
            first = (u % heads) == 0
            cb = []
            for r in range(SC_LANES):
                c = plsc.load_gather(coef_v, [jnp.full((SC_LANES,), u * SC_LANES + r, I32)])
                cb.append(plsc.pack(c, c, format=plsc.PackFormat.INTERLEAVED))

            @plsc.parallel_loop(0, chunks, unroll=2)
            def _(wc):
                lo, hi = _unpack_halves(
                    _tree_sum([cb[r] * row_words(rows[slot], r, wc, SC_PK_SUB) for r in range(SC_LANES)]))
                for half, val in ((0, lo), (1, hi)):
                    sl = pl.ds(pl.multiple_of(half * (d // 2) + wc * SC_LANES, SC_LANES), SC_LANES)
                    y_v[tt, sl] = val + jnp.where(first, 0.0, y_v[tt, sl])

        def unit(u, slot):
            dots(u, slot)
            combine(u, slot)

        @pl.loop(0, groups)
        def _(gi):
            base = wid * tpw + gi * g
            pltpu.sync_copy(idx_hbm.at[pl.ds(base * nsel, g * nsel)], idx_v)
            pltpu.sync_copy(w_hbm.at[pl.ds(base * nsel, g * nsel)], coef_v)
            pltpu.sync_copy(h_hbm.at[pl.ds(base, g)], h_v)
            ring_loop(units, lambda u, s: copy(u, s).start(), lambda u, s: copy(u, s).wait(), unit)
            pltpu.sync_copy(y_v, out_hbm.at[pl.ds(base, g)])

    return k(tab_uv, idx_flat, w_flat, hp)


def kernel(x, mem, rel_bias, ln_mix, w_in, hg_lower, hg_norm, w_up_a, w_up_b, w_out, ln_cross, ln_mem, wq_x, wk_x, wv_x, wo_x, ln_ffn, peer_query, peer_subkeys, peer_u, peer_v, ln_final):
    b, s, d = x.shape
    depth = w_in.shape[0]
    assert depth == 1, "the residual after PEER is fused into the final norm"
    assert s % MB_BLOCK == 0 and s % HG_CHUNK == 0 and s % (PEER_SLICES * SC_WORKERS * SC_GROUP) == 0
    nb = s // MB_BLOCK
    row = lambda a: a.reshape(1, -1).astype(F32)
    lb_all = jnp.cumsum(jax.nn.softmax(hg_lower.astype(F32), axis=0), axis=0)
    bias = moba_bias_tiles(rel_bias)
    n_hg = 4 * HG_WIDTH
    n_qk = 2 * MB_WIDTH
    n_mb = 3 * MB_WIDTH
    l = 0
    w = w_in[l].astype(BF16)
    w_hg, w_qk, w_vt, w_g = w[:, :n_hg], w[:, n_hg:n_hg + n_qk], w[:, n_hg + n_qk:n_hg + n_mb].T, w[:, n_hg + n_mb:]
    wa, wb, wo = w_up_a[l].astype(BF16), w_up_b[l].astype(BF16), w_out[l].astype(BF16)
    wqx, wox = wq_x[l].astype(BF16), wo_x[l].astype(BF16)
    wpq, sk = peer_query[l].astype(BF16), peer_subkeys[l].astype(F32)
    tab_uv = pack_expert_tables(peer_u[l].astype(F32), peer_v[l].astype(F32))
    kx, vx = mem_kv(mem, row(ln_mem[l]), wk_x[l].astype(BF16), wv_x[l].astype(BF16))

    outs = []
    for bi in range(b):
        x2d = x[bi]
        p0, pqk, km, vt, pg = in_proj(x2d, row(ln_mix[l]), w_hg, w_qk, w_vt, w_g)
        ya = hgrn2(p0, row(lb_all[l]), row(hg_norm[l]), 1, s)
        km = km.reshape(1, nb, MB_WIDTH)
        ts = s // PEER_SLICES
        for tok0 in range(0, s, ts):
            yb = moba_attention(pqk, vt, km, bias, 1, s, tok0 // MB_BLOCK, ts // MB_BLOCK)
            xs = mix_cross(x2d, ya, yb, pg, wa, wb, wo, row(ln_cross[l]), wqx, kx[bi:bi + 1], vx[bi:bi + 1], wox, tok0)
            hp, eidx, wts = peer_route(xs, row(ln_ffn[l]), wpq, sk, 0, ts)
            y = peer_experts_pk_sc(tab_uv, eidx.reshape(ts * PEER_SEL), wts.reshape(ts * PEER_SEL), hp, d)
            outs.append(final_norm(xs, y, row(ln_final), 0))
    return jnp.concatenate(outs, axis=0).reshape(b, s, d)
```

```python
import functools
import math

import jax
import jax.numpy as jnp
from jax import lax
from jax.experimental import pallas as pl
from jax.experimental.pallas import tpu as pltpu
from jax.experimental.pallas import tpu_sc as plsc

F32 = jnp.float32
BF16 = jnp.bfloat16
I32 = jnp.int32
EPS = 1e-6
NEG_INF = float("-inf")

HG_HEADS = 4
HG_D = 128
HG_WIDTH = HG_HEADS * HG_D
HG_CHUNK = 64
HG_SUB = 16
MB_HEADS = 8
MB_DH = 64
MB_WIDTH = MB_HEADS * MB_DH
MB_BLOCK = 256
MB_TOPK = 3
MB_BIAS_TILES = 8
REL_BUCKETS = 32
REL_MAX_DIST = 2048
X_HEADS = 4
PEER_HEADS = 8
PEER_NKEYS = 128
PEER_TOPK = 16
PEER_HALF = 128
PEER_SEL = PEER_HEADS * PEER_TOPK
PEER_SLICES = 4

VMEM_LIMIT = 56 * 1024 * 1024


def _cparams(sem):
    return pltpu.CompilerParams(dimension_semantics=sem, vmem_limit_bytes=VMEM_LIMIT)


def _rms(x, g):
    ms = jnp.mean(x * x, axis=-1, keepdims=True)
    return x * lax.rsqrt(ms + EPS) * g


def _in_proj_kernel(x_ref, g_ref, w0_ref, w1_ref, wvt_ref, w2_ref, o0_ref, o1_ref, okm_ref, ovt_ref, o2_ref):
    h = _rms(x_ref[...], g_ref[...]).astype(BF16)
    o0_ref[...] = jnp.dot(h, w0_ref[...], preferred_element_type=F32)
    qk = jnp.dot(h, w1_ref[...], preferred_element_type=F32)
    o1_ref[...] = qk.astype(BF16)
    okm_ref[0] = jnp.mean(qk[:, MB_WIDTH:], axis=0, keepdims=True)
    vt = lax.dot_general(wvt_ref[...], h, (((1,), (1,)), ((), ())), preferred_element_type=F32).astype(BF16)
    for hd in range(MB_HEADS):
        ovt_ref[0, hd * MB_VROWS:hd * MB_VROWS + MB_DH, :] = vt[hd * MB_DH:(hd + 1) * MB_DH]
        ovt_ref[0, hd * MB_VROWS + MB_DH:(hd + 1) * MB_VROWS, :] = jnp.ones((MB_ONES, vt.shape[1]), BF16)
    o2_ref[...] = jnp.dot(h, w2_ref[...], preferred_element_type=F32).astype(BF16)


def in_proj(x2d, g, w0, w1, wvt, w2):
    t, d = x2d.shape
    tm = MB_BLOCK
    assert wvt.shape[0] == MB_WIDTH and w1.shape[1] == 2 * MB_WIDTH
    n0, n1, nv, n2 = w0.shape[1], w1.shape[1], MB_VT_ROWS, w2.shape[1]
    full = lambda a: pl.BlockSpec(a.shape, lambda i: (0, 0))
    return pl.pallas_call(
        _in_proj_kernel,
        grid=(t // tm,),
        in_specs=[pl.BlockSpec((tm, d), lambda i: (i, 0)), full(g), full(w0), full(w1), full(wvt), full(w2)],
        out_specs=[pl.BlockSpec((tm, n0), lambda i: (i, 0)),
                   pl.BlockSpec((tm, n1), lambda i: (i, 0)),
                   pl.BlockSpec((1, 1, MB_WIDTH), lambda i: (i, 0, 0)),
                   pl.BlockSpec((1, nv, tm), lambda i: (i, 0, 0)),
                   pl.BlockSpec((tm, n2), lambda i: (i, 0))],
        out_shape=[jax.ShapeDtypeStruct((t, n0), F32),
                   jax.ShapeDtypeStruct((t, n1), BF16),
                   jax.ShapeDtypeStruct((t // tm, 1, MB_WIDTH), F32),
                   jax.ShapeDtypeStruct((t // tm, nv, tm), BF16),
                   jax.ShapeDtypeStruct((t, n2), BF16)],
        compiler_params=_cparams(("parallel",)),
        name="in_proj",
    )(x2d, g, w0, w1, wvt, w2)


def _hgrn_kernel(q_ref, f_ref, i_ref, g_ref, lb_ref, gain_ref, o_ref, st_ref):
    c = pl.program_id(1)

    @pl.when(c == 0)
    def _():
        st_ref[...] = jnp.zeros_like(st_ref)

    C, S = HG_CHUNK, HG_SUB
    row = lax.broadcasted_iota(I32, (C, C), 0)
    col = lax.broadcasted_iota(I32, (C, C), 1)
    tril = (row >= col).astype(F32)
    t_iota = lax.broadcasted_iota(I32, (S, 1), 0)

    for h in range(HG_HEADS):
        sl = slice(h * HG_D, (h + 1) * HG_D)
        q = q_ref[:, sl]
        v = i_ref[:, sl]
        lb = lb_ref[:, sl]
        f = lb + (1.0 - lb) * jax.nn.sigmoid(f_ref[:, sl])
        lf = jnp.log(f)
        k = 1.0 - f
        b = jnp.dot(tril, lf, precision=lax.Precision.HIGHEST, preferred_element_type=F32)
        st = st_ref[h]
        vb = v.astype(BF16)
        qd = (q * jnp.exp(b)).astype(BF16)
        o_inter = lax.dot_general(qd, st.astype(BF16), (((1,), (1,)), ((), ())),
                                  preferred_element_type=F32)
        outs = []
        for i in range(C // S):
            r0 = i * S
            qi = q[r0:r0 + S]
            ki = k[r0:r0 + S]
            bi = b[r0:r0 + S]
            vi = v[r0:r0 + S]
            oi = o_inter[r0:r0 + S]
            if i > 0:
                bs = b[r0 - 1:r0]
                qh = (qi * jnp.exp(bi - bs)).astype(BF16)
                kh = (k[:r0] * jnp.exp(bs - b[:r0])).astype(BF16)
                a = lax.dot_general(qh, kh, (((1,), (1,)), ((), ())), preferred_element_type=F32)
                oi = oi + jnp.dot(a.astype(BF16), vb[:r0], preferred_element_type=F32)
            half = S // 2
            o_half = [oi[:half], oi[half:]]
            for s in range(S):
                for hf in range(s // half, 2):
                    rows = slice(hf * half, (hf + 1) * half)
                    dec = jnp.exp(jnp.minimum(bi[rows] - bi[s:s + 1], 0.0))
                    a_s = jnp.sum(qi[rows] * ki[s:s + 1] * dec, axis=-1, keepdims=True)
                    a_s = jnp.where(t_iota[rows] >= s, a_s, 0.0)
                    o_half[hf] = o_half[hf] + a_s * vi[s:s + 1]
            outs.extend(o_half)
        o = jnp.concatenate(outs, axis=0)
        b_end = b[C - 1:C]
        kd = (k * jnp.exp(b_end - b)).astype(BF16)
        upd = lax.dot_general(vb, kd, (((0,), (0,)), ((), ())), preferred_element_type=F32)
        st_ref[h] = st * jnp.exp(b_end) + upd
        o = o * lax.rsqrt(jnp.mean(o * o, axis=-1, keepdims=True) + EPS)
        g = g_ref[:, sl]
        o_ref[:, sl] = (o * gain_ref[:, sl] * (g * jax.nn.sigmoid(g))).astype(o_ref.dtype)


def hgrn2(p0, lb, gain, batch, seq):
    t = p0.shape[0]
    nc = seq // HG_CHUNK
    w = HG_WIDTH

    def col(j):
        return pl.BlockSpec((HG_CHUNK, w), lambda b, c, j=j: (b * nc + c, j))

    return pl.pallas_call(
        _hgrn_kernel,
        grid=(batch, nc),
        in_specs=[col(0), col(1), col(2), col(3),
                  pl.BlockSpec((1, w), lambda b, c: (0, 0)),
                  pl.BlockSpec((1, w), lambda b, c: (0, 0))],
        out_specs=pl.BlockSpec((HG_CHUNK, w), lambda b, c: (b * nc + c, 0)),
        out_shape=jax.ShapeDtypeStruct((t, w), BF16),
        scratch_shapes=[pltpu.VMEM((HG_HEADS, HG_D, HG_D), F32)],
        compiler_params=_cparams(("parallel", "arbitrary")),
        name="hgrn2",
    )(p0, p0, p0, p0, lb, gain)


MB_PAIR = 4
MB_PW = MB_PAIR * MB_DH
MB_LG = 128
MB_ONES = 16
MB_VROWS = MB_DH + MB_ONES
MB_VT_ROWS = MB_HEADS * MB_VROWS


def _moba_kernel(q_ref, k_ref, vt_ref, km_ref, bias_ref, o_ref, *scratch, qb0):
    m_ref, l_ref, al_ref, acc_ref, msk_ref, s_ref, p_ref = (
        scratch[i * MB_PAIR:(i + 1) * MB_PAIR] for i in range(7))
    qi = pl.program_id(2) + qb0
    nb = km_ref.shape[0]
    blk = MB_BLOCK
    heads = range(MB_PAIR)
    grp = lambda hh: slice((hh // 2) * MB_LG, (hh // 2 + 1) * MB_LG)
    q = q_ref[...]
    lane = lax.broadcasted_iota(I32, (blk, MB_LG), 1)
    in_head = [(lane < MB_DH) if hh % 2 == 0 else (lane >= MB_DH) for hh in heads]
    qs = q * jnp.asarray(MB_DH ** -0.5, BF16)
    nt = (((1,), (1,)), ((), ()))
    qf = q.astype(F32)
    qht = [jnp.where(in_head[hh], qs[:, grp(hh)].astype(F32), 0.0).T.astype(BF16) for hh in heads]

    n_io = lax.broadcasted_iota(I32, (nb, blk), 0)
    for hh in heads:
        gate = lax.dot_general(km_ref[:, grp(hh)], jnp.where(in_head[hh], qf[:, grp(hh)], 0.0), nt,
                               precision=lax.Precision.HIGHEST, preferred_element_type=F32)
        gate = jnp.where(n_io < qi, gate, NEG_INF)
        chosen = n_io < 0
        for _ in range(MB_TOPK):
            mx = jnp.max(gate, axis=0, keepdims=True)
            ix = jnp.min(jnp.where(gate == mx, n_io, nb), axis=0, keepdims=True)
            hit = n_io == ix
            chosen = chosen | (hit & (mx > NEG_INF))
            gate = jnp.where(hit, NEG_INF, gate)
        msk_ref[hh][...] = jnp.where(chosen, 0.0, NEG_INF)

    vrows = lambda hh: slice(hh * MB_VROWS, (hh + 1) * MB_VROWS)

    def pv_stage(blk_idx):
        vtb = vt_ref[blk_idx]
        r = [jnp.dot(vtb[vrows(hh)], p_ref[hh][...], preferred_element_type=F32) for hh in heads]
        al = [al_ref[hh][...] for hh in heads]
        a_new = [al[hh] * acc_ref[hh][...] + r[hh][:MB_DH] for hh in heads]
        l_new = [al[hh] * l_ref[hh][...] + r[hh][MB_DH:MB_DH + 1] for hh in heads]
        return a_new, l_new

    def store_pv(a_new, l_new):
        for hh in heads:
            acc_ref[hh][...] = a_new[hh]
            l_ref[hh][...] = l_new[hh]

    def softmax_stage():
        s = [s_ref[hh][...] for hh in heads]
        m_old = [m_ref[hh][...] for hh in heads]
        m_new = [jnp.maximum(m_old[hh], jnp.max(s[hh], axis=0, keepdims=True)) for hh in heads]
        alpha = [jnp.exp(m_old[hh] - m_new[hh]) for hh in heads]
        p = [jnp.exp((s[hh] - m_new[hh]).astype(BF16)) for hh in heads]
        return p, alpha, m_new

    def store_softmax(p, alpha, m_new):
        for hh in heads:
            p_ref[hh][...] = p[hh]
            al_ref[hh][...] = alpha[hh]
            m_ref[hh][...] = m_new[hh]

    k_own = k_ref[pl.ds(pl.multiple_of(qi * blk, blk), blk), :]
    key_io = lax.broadcasted_iota(I32, (blk, blk), 0)
    qry_io = lax.broadcasted_iota(I32, (blk, blk), 1)
    for hh in heads:
        s = jnp.dot(k_own[:, grp(hh)], qht[hh], preferred_element_type=F32) + bias_ref[hh, 0]
        s_ref[hh][...] = jnp.where(key_io <= qry_io, s, NEG_INF)
        m_ref[hh][...] = jnp.full((1, blk), NEG_INF, F32)
        l_ref[hh][...] = jnp.zeros((1, blk), F32)
        al_ref[hh][...] = jnp.ones((1, blk), F32)
        acc_ref[hh][...] = jnp.zeros((MB_DH, blk), F32)
        p_ref[hh][...] = jnp.zeros((blk, blk), BF16)

    def step(i, carry, far):
        pv = pv_stage(jnp.where(i <= 1, qi, i - 2))
        sm = softmax_stage()
        kn = k_ref[pl.ds(pl.multiple_of(i * blk, blk), blk), :]
        if far:
            row = [msk_ref[hh][pl.ds(i, 1), :] + bias_ref[hh, MB_BIAS_TILES - 1, 0:1, 0:1] for hh in heads]
            s_next = [jnp.dot(kn[:, grp(hh)], qht[hh], preferred_element_type=F32) + row[hh] for hh in heads]
        else:
            d = qi - i
            s_next = [jnp.dot(kn[:, grp(hh)], qht[hh], preferred_element_type=F32)
                      + bias_ref[hh, d] + msk_ref[hh][pl.ds(i, 1), :] for hh in heads]
        store_pv(*pv)
        for hh in heads:
            s_ref[hh][...] = s_next[hh]
        store_softmax(*sm)
        return carry

    n_far = jnp.maximum(qi - (MB_BIAS_TILES - 2), 0)
    lax.fori_loop(0, n_far, functools.partial(step, far=True), 0)
    lax.fori_loop(n_far, qi, functools.partial(step, far=False), 0)
    pv = pv_stage(jnp.where(qi <= 1, qi, qi - 2))
    sm = softmax_stage()
    store_pv(*pv)
    store_softmax(*sm)
    a_fin, l_fin = pv_stage(jnp.where(qi == 0, qi, qi - 1))
    out_t = jnp.concatenate([a_fin[hh] / l_fin[hh] for hh in heads], axis=0)
    o_ref[...] = out_t.T.astype(o_ref.dtype)


def moba_attention(pqk, vt, km, bias, batch, seq, qb0=0, nqb=None):
    nb = seq // MB_BLOCK
    nqb = nb if nqb is None else nqb
    t = batch * nqb * MB_BLOCK
    groups = MB_WIDTH // MB_PW
    return pl.pallas_call(
        functools.partial(_moba_kernel, qb0=qb0),
        grid=(batch, groups, nqb),
        in_specs=[
            pl.BlockSpec((MB_BLOCK, MB_PW), lambda b, j, i: (b * nb + qb0 + i, j)),
            pl.BlockSpec((seq, MB_PW), lambda b, j, i: (b, groups + j)),
            pl.BlockSpec((nb, MB_PAIR * MB_VROWS, MB_BLOCK), lambda b, j, i: (b, j, 0)),
            pl.BlockSpec((None, nb, MB_PW), lambda b, j, i: (b, 0, j)),
            pl.BlockSpec((MB_PAIR, MB_BIAS_TILES, MB_BLOCK, MB_BLOCK), lambda b, j, i: (j, 0, 0, 0)),
        ],
        out_specs=pl.BlockSpec((MB_BLOCK, MB_PW), lambda b, j, i: (b * nqb + i, j)),
        out_shape=jax.ShapeDtypeStruct((t, MB_WIDTH), BF16),
        scratch_shapes=(
            [pltpu.VMEM((1, MB_BLOCK), F32)] * (3 * MB_PAIR)
            + [pltpu.VMEM((MB_DH, MB_BLOCK), F32)] * MB_PAIR
            + [pltpu.VMEM((nb, MB_BLOCK), F32)] * MB_PAIR
            + [pltpu.VMEM((MB_BLOCK, MB_BLOCK), F32)] * MB_PAIR
            + [pltpu.VMEM((MB_BLOCK, MB_BLOCK), BF16)] * MB_PAIR
        ),
        compiler_params=_cparams(("parallel", "parallel", "arbitrary")),
        name="moba_attn",
    )(pqk, pqk, vt, km, bias)


def _t5_bucket(dist):
    max_exact = REL_BUCKETS // 2
    scaled = jnp.log(jnp.maximum(dist, 1).astype(F32) / max_exact) / math.log(REL_MAX_DIST / max_exact)
    large = jnp.minimum(max_exact + (scaled * (REL_BUCKETS - max_exact)).astype(I32), REL_BUCKETS - 1)
    return jnp.where(dist < max_exact, dist, large)


def moba_bias_tiles(rel_bias):
    blk = MB_BLOCK
    span = 2 * blk - 1
    x = jnp.arange(span) - (blk - 1)
    dist = jnp.maximum(jnp.arange(MB_BIAS_TILES)[:, None] * blk + x[None, :], 0)
    w = rel_bias.astype(F32).T[:, _t5_bucket(dist)]
    h = w.shape[0]
    wp = jnp.pad(w, ((0, 0), (0, 0), (0, 1)))
    a = jnp.broadcast_to(wp[:, :, None, :], (h, MB_BIAS_TILES, blk, span + 1))
    a = a.reshape(h, MB_BIAS_TILES, blk * (span + 1))[:, :, :blk * span]
    return a.reshape(h, MB_BIAS_TILES, blk, span)[:, :, :, blk - 1:]


def _mix_kernel(x_ref, ya_ref, yb_ref, ga_ref, gb_ref, wa_ref, wb_ref, wo_ref, o_ref):
    za = jnp.dot(ya_ref[...], wa_ref[...], preferred_element_type=F32)
    zb = jnp.dot(yb_ref[...], wb_ref[...], preferred_element_type=F32)
    z = jax.nn.sigmoid(ga_ref[...].astype(F32)) * za + jax.nn.sigmoid(gb_ref[...].astype(F32)) * zb
    o_ref[...] = x_ref[...] + jnp.dot(z.astype(BF16), wo_ref[...], preferred_element_type=F32)


def _mem_kv_kernel(m_ref, g_ref, wk_ref, wv_ref, k_ref, v_ref):
    mn = _rms(m_ref[...], g_ref[...]).astype(BF16)
    k_ref[...] = jnp.dot(mn, wk_ref[...], preferred_element_type=F32).astype(BF16)
    v_ref[...] = jnp.dot(mn, wv_ref[...], preferred_element_type=F32).astype(BF16)


def mem_kv(mem, g, wk, wv):
    b, m, d = mem.shape
    spec = pl.BlockSpec((None, m, d), lambda i: (i, 0, 0))
    wspec = pl.BlockSpec((d, d), lambda i: (0, 0))
    return pl.pallas_call(
        _mem_kv_kernel,
        grid=(b,),
        in_specs=[spec, pl.BlockSpec((1, d), lambda i: (0, 0)), wspec, wspec],
        out_specs=[spec, spec],
        out_shape=[jax.ShapeDtypeStruct((b, m, d), BF16)] * 2,
        compiler_params=_cparams(("parallel",)),
        name="mem_kv",
    )(mem, g, wk, wv)


def _cross_kernel(x_ref, g_ref, wq_ref, k_ref, v_ref, wo_ref, o_ref):
    x = x_ref[...]
    d = x.shape[1]
    dh = d // X_HEADS
    h = _rms(x, g_ref[...]).astype(BF16)
    q = (jnp.dot(h, wq_ref[...], preferred_element_type=F32) * (dh ** -0.5)).astype(BF16)
    outs = []
    for hh in range(X_HEADS):
        sl = slice(hh * dh, (hh + 1) * dh)
        s = lax.dot_general(q[:, sl], k_ref[:, sl], (((1,), (1,)), ((), ())),
                            preferred_element_type=F32)
        p = jnp.exp(s - jnp.max(s, axis=1, keepdims=True))
        l = jnp.sum(p, axis=1, keepdims=True)
        o = jnp.dot(p.astype(BF16), v_ref[:, sl], preferred_element_type=F32) / l
        outs.append(o.astype(BF16))
    o = jnp.concatenate(outs, axis=1)
    o_ref[...] = x + jnp.dot(o, wo_ref[...], preferred_element_type=F32)


def _mix_cross_kernel(x_ref, ya_ref, yb_ref, ga_ref, gb_ref, wa_ref, wb_ref, wo_ref,
                      g_ref, wq_ref, k_ref, v_ref, wox_ref, o_ref, x1_ref):
    _mix_kernel(x_ref, ya_ref, yb_ref, ga_ref, gb_ref, wa_ref, wb_ref, wo_ref, x1_ref)
    _cross_kernel(x1_ref, g_ref, wq_ref, k_ref, v_ref, wox_ref, o_ref)


def mix_cross(x2d, ya, yb, pg, wa, wb, wo, g, wq, kx, vx, wox, tok0, tm=512):
    t = yb.shape[0]
    assert t % tm == 0 and tok0 % tm == 0
    d = x2d.shape[1]
    w = ya.shape[1]
    m = kx.shape[1]
    b0 = tok0 // tm
    const = lambda a: pl.BlockSpec(a.shape, lambda i: (0,) * a.ndim)
    kv = pl.BlockSpec((None, m, d), lambda i: (0, 0, 0))
    return pl.pallas_call(
        _mix_cross_kernel,
        grid=(t // tm,),
        in_specs=[
            pl.BlockSpec((tm, d), lambda i: (b0 + i, 0)),
            pl.BlockSpec((tm, w), lambda i: (b0 + i, 0)),
            pl.BlockSpec((tm, w), lambda i: (i, 0)),
            pl.BlockSpec((tm, d), lambda i: (b0 + i, 0)),
            pl.BlockSpec((tm, d), lambda i: (b0 + i, 1)),
            const(wa), const(wb), const(wo), const(g), const(wq), kv, kv, const(wox),
        ],
        out_specs=pl.BlockSpec((tm, d), lambda i: (i, 0)),
        out_shape=jax.ShapeDtypeStruct((t, d), F32),
        scratch_shapes=[pltpu.VMEM((tm, d), F32)],
        compiler_params=_cparams(("parallel",)),
        name="mix_cross",
    )(x2d, ya, yb, pg, pg, wa, wb, wo, g, wq, kx, vx, wox)


def _topk_rows(sc, k):
    n = sc.shape[0]
    io = lax.broadcasted_iota(I32, sc.shape, 0)
    vals, ids = [], []
    for _ in range(k):
        m = jnp.max(sc, axis=0, keepdims=True)
        ix = jnp.min(jnp.where(sc == m, io, n), axis=0, keepdims=True)
        vals.append(m)
        ids.append(ix)
        sc = jnp.where(io == ix, NEG_INF, sc)
    return jnp.concatenate(vals, axis=0), jnp.concatenate(ids, axis=0)


def _pack_bf16_halves(h):
    bits = lax.bitcast_convert_type(h, I32)
    r = bits + 0x7FFF + (lax.shift_right_logical(bits, 16) & 1)
    half = h.shape[1] // 2
    return lax.shift_right_logical(r[:, :half], 16) | (r[:, half:] & HI_MASK)


def _route_kernel(x_ref, g_ref, wq_ref, sk_ref, hp_ref, idx_ref, w_ref, hb_ref, it_ref, wt_ref):
    p = pl.program_id(1)

    @pl.when(p == 0)
    def _():
        h = _rms(x_ref[...], g_ref[...])
        hp_ref[...] = _pack_bf16_halves(h)
        hb_ref[...] = h.astype(BF16)

    qh = jnp.dot(hb_ref[...], wq_ref[...], preferred_element_type=F32)
    tops = []
    for c in range(2):
        seg = qh[:, c * PEER_HALF:(c + 1) * PEER_HALF]
        sc = lax.dot_general(sk_ref[c], seg, (((1,), (1,)), ((), ())),
                             precision=lax.Precision.HIGHEST, preferred_element_type=F32)
        tops.append(_topk_rows(sc, PEER_TOPK))
    (s0, i0), (s1, i1) = tops
    k = PEER_TOPK
    sub = 8
    tm = s0.shape[1]
    r8 = lax.broadcasted_iota(I32, (sub, tm), 0)
    r16 = lax.broadcasted_iota(I32, (k, tm), 0)
    cand_b = [s0[0:1] + s1, s0[1:2] + s1[:sub]]
    cidx_b = [i0[0:1] * PEER_NKEYS + i1, i0[1:2] * PEER_NKEYS + i1[:sub]]
    pos_b = [r16, k + r8]
    for a in range(2, sub):
        keep = r8 < (k // (a + 1))
        cand_b.append(jnp.where(keep, s0[a:a + 1] + s1[:sub], NEG_INF))
        cidx_b.append(i0[a:a + 1] * PEER_NKEYS + i1[:sub])
        pos_b.append(a * k + r8)
    cand_b.append(s0[sub:] + s1[0:1])
    cidx_b.append(i0[sub:] * PEER_NKEYS + i1[0:1])
    pos_b.append((sub + r8) * k)
    cand = jnp.concatenate(cand_b, axis=0)
    cidx = jnp.concatenate(cidx_b, axis=0)
    pos = jnp.concatenate(pos_b, axis=0)
    vals, ids = [], []
    for _ in range(k):
        m = jnp.max(cand, axis=0, keepdims=True)
        px = jnp.min(jnp.where(cand == m, pos, k * k), axis=0, keepdims=True)
        hit = pos == px
        vals.append(m)
        ids.append(jnp.sum(jnp.where(hit, cidx, 0), axis=0, keepdims=True))
        cand = jnp.where(hit, NEG_INF, cand)
    sf = jnp.concatenate(vals, axis=0)
    e = jnp.exp(sf - sf[0:1])
    rows = pl.ds(pl.multiple_of(p * PEER_TOPK, PEER_TOPK), PEER_TOPK)
    wt_ref[rows, :] = e / jnp.sum(e, axis=0, keepdims=True)
    it_ref[rows, :] = jnp.concatenate(ids, axis=0)

    @pl.when(p == pl.num_programs(1) - 1)
    def _():
        idx_ref[...] = it_ref[...].T
        w_ref[...] = wt_ref[...].T


def peer_route(x2d, g, wq, sk, tok0, t, tm=1024):
    assert t % tm == 0 and tok0 % tm == 0
    d = x2d.shape[1]
    ph = sk.shape[0]
    nsel = ph * PEER_TOPK
    blk0 = tok0 // tm
    return pl.pallas_call(
        _route_kernel,
        grid=(t // tm, ph),
        in_specs=[
            pl.BlockSpec((tm, d), lambda i, p: (blk0 + i, 0)),
            pl.BlockSpec((1, d), lambda i, p: (0, 0)),
            pl.BlockSpec((d, 2 * PEER_HALF), lambda i, p: (0, p)),
            pl.BlockSpec((None, 2, PEER_NKEYS, PEER_HALF), lambda i, p: (p, 0, 0, 0)),
        ],
        out_specs=[
            pl.BlockSpec((tm, d // 2), lambda i, p: (i, 0)),
            pl.BlockSpec((tm, nsel), lambda i, p: (i, 0)),
            pl.BlockSpec((tm, nsel), lambda i, p: (i, 0)),
        ],
        out_shape=[jax.ShapeDtypeStruct((t, d // 2), I32),
                   jax.ShapeDtypeStruct((t, nsel), I32),
                   jax.ShapeDtypeStruct((t, nsel), F32)],
        scratch_shapes=[pltpu.VMEM((tm, d), BF16),
                        pltpu.VMEM((nsel, tm), I32),
                        pltpu.VMEM((nsel, tm), F32)],
        compiler_params=_cparams(("parallel", "arbitrary")),
        name="peer_route",
    )(x2d, g, wq, sk)


def _final_kernel(x_ref, y_ref, g_ref, o_ref):
    o_ref[...] = _rms(x_ref[...] + y_ref[...], g_ref[...])


def final_norm(x2d, y, g, tok0, tm=512):
    t, d = y.shape
    assert t % tm == 0 and tok0 % tm == 0
    blk0 = tok0 // tm
    spec = pl.BlockSpec((tm, d), lambda i: (i, 0))
    return pl.pallas_call(
        _final_kernel, grid=(t // tm,),
        in_specs=[pl.BlockSpec((tm, d), lambda i: (blk0 + i, 0)), spec, pl.BlockSpec((1, d), lambda i: (0, 0))],
        out_specs=spec,
        out_shape=jax.ShapeDtypeStruct((t, d), F32),
        compiler_params=_cparams(("parallel",)), name="final_norm",
    )(x2d, y, g)


SC_CORES = 2
SC_SUBCORES = 16
SC_WORKERS = SC_CORES * SC_SUBCORES
SC_LANES = 16
SC_GROUP = 16


def _sc_mesh():
    return plsc.VectorSubcoreMesh(core_axis_name="c", subcore_axis_name="s")


def _sc_params():
    return pltpu.CompilerParams(needs_layout_passes=False)


def _sc_worker_id():
    return lax.axis_index("s") * SC_CORES + lax.axis_index("c")


SC_ROW_LANE = 128


def _sc_unit_off(u):
    off = u * SC_LANES
    return off if isinstance(off, int) else pl.multiple_of(off, SC_LANES)


GELU_C0 = math.sqrt(2.0 / math.pi)
GELU_C1 = 0.044715


def _gelu_tanh(x):
    z = GELU_C0 * (x + GELU_C1 * (x * x * x))
    th = 1.0 - 2.0 / (jnp.exp(2.0 * z) + 1.0)
    return 0.5 * x * (1.0 + th)


SC_PK_RING = 4
SC_PK_SUB = 4
HI_MASK = -65536


def _pack_tables_kernel(u_ref, v_ref, o_ref):
    for part, ref in enumerate((u_ref, v_ref)):
        words = _pack_bf16_halves(ref[...])
        for sub in range(SC_PK_SUB):
            o_ref[:, part * SC_PK_SUB + sub, :] = words[:, sub * SC_ROW_LANE:(sub + 1) * SC_ROW_LANE]


def pack_expert_tables(u, v, te=512):
    e, d = u.shape
    assert d == 2 * SC_PK_SUB * SC_ROW_LANE
    spec = pl.BlockSpec((te, d), lambda i: (i, 0))
    return pl.pallas_call(
        _pack_tables_kernel, grid=(e // te,), in_specs=[spec, spec],
        out_specs=pl.BlockSpec((te, 2 * SC_PK_SUB, SC_ROW_LANE), lambda i: (i, 0, 0)),
        out_shape=jax.ShapeDtypeStruct((e, 2 * SC_PK_SUB, SC_ROW_LANE), I32),
        compiler_params=_cparams(("parallel",)), name="pack_expert_tables",
    )(u, v)


def _unpack_halves(x32):
    w = plsc.bitcast(x32, I32)
    return plsc.bitcast(w << 16, F32), plsc.bitcast(w & HI_MASK, F32)


def _tree_sum(xs):
    while len(xs) > 1:
        xs = [xs[i] + xs[i + 1] for i in range(0, len(xs), 2)]
    return xs[0]


def peer_experts_pk_sc(tab_uv, idx_flat, w_flat, hp, d):
    t = hp.shape[0]
    nsel = PEER_SEL
    g = SC_GROUP
    assert t % (SC_WORKERS * g) == 0 and d == 2 * SC_PK_SUB * SC_ROW_LANE
    tpw = t // SC_WORKERS
    groups = tpw // g
    heads = nsel // SC_LANES
    chunks = d // 32
    units = g * heads
    ring = SC_PK_RING
    assert units % ring == 0
    row_buf = pltpu.VMEM((SC_LANES, 2 * SC_PK_SUB, SC_ROW_LANE), I32)

    @functools.partial(
        pl.kernel, mesh=_sc_mesh(),
        out_type=jax.ShapeDtypeStruct((t, d), F32),
        scratch_types=[
            [pltpu.VMEM((g * nsel,), I32)] * 2,
            [pltpu.VMEM((g * nsel,), F32)] * 2,
            [pltpu.VMEM((g, d // 2), I32)] * 2,
            [pltpu.VMEM((g, d), F32)] * 2,
            pltpu.VMEM((SC_LANES * SC_LANES,), F32),
            [row_buf] * ring,
            [pltpu.SemaphoreType.DMA] * ring,
            [pltpu.SemaphoreType.DMA] * 2,
            [pltpu.SemaphoreType.DMA] * 2,
        ],
        compiler_params=_sc_params(),
        name="peer_experts_pk_sc",
    )
    def k(tab_hbm, idx_hbm, w_hbm, h_hbm, out_hbm, idx_s, coef_s, h_s, y_s, red_v, rows, sems, in_sems, out_sems):
        wid = _sc_worker_id()
        lane = lax.iota(I32, SC_LANES)

        def stage_in(gi, st):
            base = wid * tpw + gi * g
            return (pltpu.make_async_copy(idx_hbm.at[pl.ds(base * nsel, g * nsel)], idx_s[st], in_sems[st]),
                    pltpu.make_async_copy(w_hbm.at[pl.ds(base * nsel, g * nsel)], coef_s[st], in_sems[st]),
                    pltpu.make_async_copy(h_hbm.at[pl.ds(base, g)], h_s[st], in_sems[st]))

        def write_back(gi, st):
            return pltpu.make_async_copy(y_s[st], out_hbm.at[pl.ds(wid * tpw + gi * g, g)], out_sems[st])

        for c in stage_in(0, 0):
            c.start()
        for gi in range(groups):
            st = gi % 2
            for c in stage_in(gi, st):
                c.wait()
            if gi + 1 < groups:
                for c in stage_in(gi + 1, 1 - st):
                    c.start()
            if gi >= 2:
                write_back(gi - 2, st).wait()
            _peer_group(tab_hbm, idx_s[st], coef_s[st], h_s[st], y_s[st], red_v, rows, sems, lane,
                        heads, chunks, units, ring, d)
            write_back(gi, st).start()
        for gi in range(max(groups - 2, 0), groups):
            write_back(gi, gi % 2).wait()

    return k(tab_uv, idx_flat, w_flat, hp)


def _peer_group(tab_hbm, idx_v, coef_v, h_v, y_v, red_v, rows, sems, lane, heads, chunks, units, ring, d):
    if True:
        def row_words(rows, r, wc, sub0):
            per = SC_ROW_LANE // SC_LANES
            return plsc.bitcast(
                rows[r, sub0 + wc // per, pl.ds(pl.multiple_of((wc % per) * SC_LANES, SC_LANES), SC_LANES)], BF16)

        def ring_loop(n_units, start, wait, compute):
            for u in range(ring - 1):
                start(u, u)

            @pl.loop(0, n_units, step=ring)
            def _(uu):
                for b in range(ring):
                    u = uu + b
                    nxt = u + (ring - 1)

                    @pl.when(nxt < n_units)
                    def _():
                        start(nxt, (b + ring - 1) % ring)

                    wait(u, b)
                    compute(u, b)

        def copy(u, slot):
            ids = idx_v.at[pl.ds(_sc_unit_off(u), SC_LANES)]
            return pltpu.make_async_copy(tab_hbm.at[ids], rows[slot], sems[slot])

        def dots(u, slot):
            tt = u // heads

            def body(cp, accs):
                out = []
                hv = [plsc.bitcast(h_v[tt, pl.ds(pl.multiple_of((2 * cp + i) * SC_LANES, SC_LANES), SC_LANES)], BF16)
                      for i in range(2)]
                for r in range(SC_LANES):
                    pr = (row_words(rows[slot], r, 2 * cp, 0) * hv[0]
                          + row_words(rows[slot], r, 2 * cp + 1, 0) * hv[1])
                    lo, hi = _unpack_halves(pr)
                    out.append(accs[r] + lo + hi)
                return tuple(out)

            accs = lax.fori_loop(0, chunks // 2, body,
                                 tuple(jnp.zeros((SC_LANES,), F32) for _ in range(SC_LANES)))
            for r in range(SC_LANES):
                red_v[pl.ds(r * SC_LANES, SC_LANES)] = accs[r]
            act = _tree_sum([plsc.load_gather(red_v, [lane * SC_LANES + j]) for j in range(SC_LANES)])
            sl = pl.ds(_sc_unit_off(u), SC_LANES)
            coef_v[sl] = coef_v[sl] * _gelu_tanh(act)

        def combine(u, slot):
            tt = u // heads
            first = (u % heads) == 0
            cb = []
            for r in range(SC_LANES):
                c = plsc.load_gather(coef_v, [jnp.full((SC_LANES,), u * SC_LANES + r, I32)])
                cb.append(plsc.pack(c, c, format=plsc.PackFormat.INTERLEAVED))

            @plsc.parallel_loop(0, chunks, unroll=2)
            def _(wc):
                lo, hi = _unpack_halves(
                    _tree_sum([cb[r] * row_words(rows[slot], r, wc, SC_PK_SUB) for r in range(SC_LANES)]))
                for half, val in ((0, lo), (1, hi)):
                    sl = pl.ds(pl.multiple_of(half * (d // 2) + wc * SC_LANES, SC_LANES), SC_LANES)
                    y_v[tt, sl] = val + jnp.where(first, 0.0, y_v[tt, sl])

        def unit(u, slot):
            dots(u, slot)
            combine(u, slot)

        ring_loop(units, lambda u, s: copy(u, s).start(), lambda u, s: copy(u, s).wait(), unit)


def kernel(x, mem, rel_bias, ln_mix, w_in, hg_lower, hg_norm, w_up_a, w_up_b, w_out, ln_cross, ln_mem, wq_x, wk_x, wv_x, wo_x, ln_ffn, peer_query, peer_subkeys, peer_u, peer_v, ln_final):
    b, s, d = x.shape
    depth = w_in.shape[0]
    assert depth == 1, "the residual after PEER is fused into the final norm"
    assert s % MB_BLOCK == 0 and s % HG_CHUNK == 0 and s % (PEER_SLICES * SC_WORKERS * SC_GROUP) == 0
    nb = s // MB_BLOCK
    row = lambda a: a.reshape(1, -1).astype(F32)
    lb_all = jnp.cumsum(jax.nn.softmax(hg_lower.astype(F32), axis=0), axis=0)
    bias = moba_bias_tiles(rel_bias)
    n_hg = 4 * HG_WIDTH
    n_qk = 2 * MB_WIDTH
    n_mb = 3 * MB_WIDTH
    l = 0
    w = w_in[l].astype(BF16)
    w_hg, w_qk, w_vt, w_g = w[:, :n_hg], w[:, n_hg:n_hg + n_qk], w[:, n_hg + n_qk:n_hg + n_mb].T, w[:, n_hg + n_mb:]
    wa, wb, wo = w_up_a[l].astype(BF16), w_up_b[l].astype(BF16), w_out[l].astype(BF16)
    wqx, wox = wq_x[l].astype(BF16), wo_x[l].astype(BF16)
    wpq, sk = peer_query[l].astype(BF16), peer_subkeys[l].astype(F32)
    tab_uv = pack_expert_tables(peer_u[l].astype(F32), peer_v[l].astype(F32))
    kx, vx = mem_kv(mem, row(ln_mem[l]), wk_x[l].astype(BF16), wv_x[l].astype(BF16))

    outs = []
    for bi in range(b):
        x2d = x[bi]
        p0, pqk, km, vt, pg = in_proj(x2d, row(ln_mix[l]), w_hg, w_qk, w_vt, w_g)
        ya = hgrn2(p0, row(lb_all[l]), row(hg_norm[l]), 1, s)
        km = km.reshape(1, nb, MB_WIDTH)
        ts = s // PEER_SLICES
        for tok0 in range(0, s, ts):
            yb = moba_attention(pqk, vt, km, bias, 1, s, tok0 // MB_BLOCK, ts // MB_BLOCK)
            xs = mix_cross(x2d, ya, yb, pg, wa, wb, wo, row(ln_cross[l]), wqx, kx[bi:bi + 1], vx[bi:bi + 1], wox, tok0)
            hp, eidx, wts = peer_route(xs, row(ln_ffn[l]), wpq, sk, 0, ts)
            y = peer_experts_pk_sc(tab_uv, eidx.reshape(ts * PEER_SEL), wts.reshape(ts * PEER_SEL), hp, d)
            outs.append(final_norm(xs, y, row(ln_final), 0))
    return jnp.concatenate(outs, axis=0).reshape(b, s, d)
```

```python
import functools
import math

import jax
import jax.numpy as jnp
from jax import lax
from jax.experimental import pallas as pl
from jax.experimental.pallas import tpu as pltpu
from jax.experimental.pallas import tpu_sc as plsc

F32 = jnp.float32
BF16 = jnp.bfloat16
I32 = jnp.int32
EPS = 1e-6
NEG_INF = float("-inf")

HG_HEADS = 4
HG_D = 128
HG_WIDTH = HG_HEADS * HG_D
HG_CHUNK = 64
HG_SUB = 16
MB_HEADS = 8
MB_DH = 64
MB_WIDTH = MB_HEADS * MB_DH
MB_BLOCK = 256
MB_TOPK = 3
MB_BIAS_TILES = 8
REL_BUCKETS = 32
REL_MAX_DIST = 2048
X_HEADS = 4
PEER_HEADS = 8
PEER_NKEYS = 128
PEER_TOPK = 16
PEER_HALF = 128
PEER_SEL = PEER_HEADS * PEER_TOPK
PEER_SLICES = 4

VMEM_LIMIT = 56 * 1024 * 1024


def _cparams(sem):
    return pltpu.CompilerParams(dimension_semantics=sem, vmem_limit_bytes=VMEM_LIMIT)


def _rms(x, g):
    ms = jnp.mean(x * x, axis=-1, keepdims=True)
    return x * lax.rsqrt(ms + EPS) * g


def _in_proj_kernel(x_ref, g_ref, w0_ref, w1_ref, wvt_ref, w2_ref, o0_ref, o1_ref, okm_ref, ovt_ref, o2_ref):
    h = _rms(x_ref[...], g_ref[...]).astype(BF16)
    o0_ref[...] = jnp.dot(h, w0_ref[...], preferred_element_type=F32)
    qk = jnp.dot(h, w1_ref[...], preferred_element_type=F32)
    o1_ref[...] = qk.astype(BF16)
    okm_ref[0] = jnp.mean(qk[:, MB_WIDTH:], axis=0, keepdims=True)
    vt = lax.dot_general(wvt_ref[...], h, (((1,), (1,)), ((), ())), preferred_element_type=F32).astype(BF16)
    for hd in range(MB_HEADS):
        ovt_ref[0, hd * MB_VROWS:hd * MB_VROWS + MB_DH, :] = vt[hd * MB_DH:(hd + 1) * MB_DH]
        ovt_ref[0, hd * MB_VROWS + MB_DH:(hd + 1) * MB_VROWS, :] = jnp.ones((MB_ONES, vt.shape[1]), BF16)
    o2_ref[...] = jnp.dot(h, w2_ref[...], preferred_element_type=F32).astype(BF16)


def in_proj(x2d, g, w0, w1, wvt, w2):
    t, d = x2d.shape
    tm = MB_BLOCK
    assert wvt.shape[0] == MB_WIDTH and w1.shape[1] == 2 * MB_WIDTH
    n0, n1, nv, n2 = w0.shape[1], w1.shape[1], MB_VT_ROWS, w2.shape[1]
    full = lambda a: pl.BlockSpec(a.shape, lambda i: (0, 0))
    return pl.pallas_call(
        _in_proj_kernel,
        grid=(t // tm,),
        in_specs=[pl.BlockSpec((tm, d), lambda i: (i, 0)), full(g), full(w0), full(w1), full(wvt), full(w2)],
        out_specs=[pl.BlockSpec((tm, n0), lambda i: (i, 0)),
                   pl.BlockSpec((tm, n1), lambda i: (i, 0)),
                   pl.BlockSpec((1, 1, MB_WIDTH), lambda i: (i, 0, 0)),
                   pl.BlockSpec((1, nv, tm), lambda i: (i, 0, 0)),
                   pl.BlockSpec((tm, n2), lambda i: (i, 0))],
        out_shape=[jax.ShapeDtypeStruct((t, n0), F32),
                   jax.ShapeDtypeStruct((t, n1), BF16),
                   jax.ShapeDtypeStruct((t // tm, 1, MB_WIDTH), F32),
                   jax.ShapeDtypeStruct((t // tm, nv, tm), BF16),
                   jax.ShapeDtypeStruct((t, n2), BF16)],
        compiler_params=_cparams(("parallel",)),
        name="in_proj",
    )(x2d, g, w0, w1, wvt, w2)


def _hgrn_kernel(q_ref, f_ref, i_ref, g_ref, lb_ref, gain_ref, o_ref, st_ref):
    c = pl.program_id(1)

    @pl.when(c == 0)
    def _():
        st_ref[...] = jnp.zeros_like(st_ref)

    C, S = HG_CHUNK, HG_SUB
    row = lax.broadcasted_iota(I32, (C, C), 0)
    col = lax.broadcasted_iota(I32, (C, C), 1)
    tril = (row >= col).astype(F32)
    t_iota = lax.broadcasted_iota(I32, (S, 1), 0)

    for h in range(HG_HEADS):
        sl = slice(h * HG_D, (h + 1) * HG_D)
        q = q_ref[:, sl]
        v = i_ref[:, sl]
        lb = lb_ref[:, sl]
        f = lb + (1.0 - lb) * jax.nn.sigmoid(f_ref[:, sl])
        lf = jnp.log(f)
        k = 1.0 - f
        b = jnp.dot(tril, lf, precision=lax.Precision.HIGHEST, preferred_element_type=F32)
        st = st_ref[h]
        vb = v.astype(BF16)
        qd = (q * jnp.exp(b)).astype(BF16)
        o_inter = lax.dot_general(qd, st.astype(BF16), (((1,), (1,)), ((), ())),
                                  preferred_element_type=F32)
        outs = []
        for i in range(C // S):
            r0 = i * S
            qi = q[r0:r0 + S]
            ki = k[r0:r0 + S]
            bi = b[r0:r0 + S]
            vi = v[r0:r0 + S]
            oi = o_inter[r0:r0 + S]
            if i > 0:
                bs = b[r0 - 1:r0]
                qh = (qi * jnp.exp(bi - bs)).astype(BF16)
                kh = (k[:r0] * jnp.exp(bs - b[:r0])).astype(BF16)
                a = lax.dot_general(qh, kh, (((1,), (1,)), ((), ())), preferred_element_type=F32)
                oi = oi + jnp.dot(a.astype(BF16), vb[:r0], preferred_element_type=F32)
            half = S // 2
            o_half = [oi[:half], oi[half:]]
            for s in range(S):
                for hf in range(s // half, 2):
                    rows = slice(hf * half, (hf + 1) * half)
                    dec = jnp.exp(jnp.minimum(bi[rows] - bi[s:s + 1], 0.0))
                    a_s = jnp.sum(qi[rows] * ki[s:s + 1] * dec, axis=-1, keepdims=True)
                    a_s = jnp.where(t_iota[rows] >= s, a_s, 0.0)
                    o_half[hf] = o_half[hf] + a_s * vi[s:s + 1]
            outs.extend(o_half)
        o = jnp.concatenate(outs, axis=0)
        b_end = b[C - 1:C]
        kd = (k * jnp.exp(b_end - b)).astype(BF16)
        upd = lax.dot_general(vb, kd, (((0,), (0,)), ((), ())), preferred_element_type=F32)
        st_ref[h] = st * jnp.exp(b_end) + upd
        o = o * lax.rsqrt(jnp.mean(o * o, axis=-1, keepdims=True) + EPS)
        g = g_ref[:, sl]
        o_ref[:, sl] = (o * gain_ref[:, sl] * (g * jax.nn.sigmoid(g))).astype(o_ref.dtype)


def hgrn2(p0, lb, gain, batch, seq):
    t = p0.shape[0]
    nc = seq // HG_CHUNK
    w = HG_WIDTH

    def col(j):
        return pl.BlockSpec((HG_CHUNK, w), lambda b, c, j=j: (b * nc + c, j))

    return pl.pallas_call(
        _hgrn_kernel,
        grid=(batch, nc),
        in_specs=[col(0), col(1), col(2), col(3),
                  pl.BlockSpec((1, w), lambda b, c: (0, 0)),
                  pl.BlockSpec((1, w), lambda b, c: (0, 0))],
        out_specs=pl.BlockSpec((HG_CHUNK, w), lambda b, c: (b * nc + c, 0)),
        out_shape=jax.ShapeDtypeStruct((t, w), BF16),
        scratch_shapes=[pltpu.VMEM((HG_HEADS, HG_D, HG_D), F32)],
        compiler_params=_cparams(("parallel", "arbitrary")),
        name="hgrn2",
    )(p0, p0, p0, p0, lb, gain)


MB_PAIR = 4
MB_PW = MB_PAIR * MB_DH
MB_LG = 128
MB_ONES = 16
MB_VROWS = MB_DH + MB_ONES
MB_VT_ROWS = MB_HEADS * MB_VROWS


def _moba_kernel(q_ref, k_ref, vt_ref, km_ref, bias_ref, o_ref, *scratch, qb0):
    m_ref, l_ref, al_ref, acc_ref, msk_ref, s_ref, p_ref = (
        scratch[i * MB_PAIR:(i + 1) * MB_PAIR] for i in range(7))
    qi = pl.program_id(2) + qb0
    nb = km_ref.shape[0]
    blk = MB_BLOCK
    heads = range(MB_PAIR)
    grp = lambda hh: slice((hh // 2) * MB_LG, (hh // 2 + 1) * MB_LG)
    q = q_ref[...]
    lane = lax.broadcasted_iota(I32, (blk, MB_LG), 1)
    in_head = [(lane < MB_DH) if hh % 2 == 0 else (lane >= MB_DH) for hh in heads]
    qs = q * jnp.asarray(MB_DH ** -0.5, BF16)
    nt = (((1,), (1,)), ((), ()))
    qf = q.astype(F32)
    qht = [jnp.where(in_head[hh], qs[:, grp(hh)].astype(F32), 0.0).T.astype(BF16) for hh in heads]

    n_io = lax.broadcasted_iota(I32, (nb, blk), 0)
    for hh in heads:
        gate = lax.dot_general(km_ref[:, grp(hh)], jnp.where(in_head[hh], qf[:, grp(hh)], 0.0), nt,
                               precision=lax.Precision.HIGHEST, preferred_element_type=F32)
        gate = jnp.where(n_io < qi, gate, NEG_INF)
        chosen = n_io < 0
        for _ in range(MB_TOPK):
            mx = jnp.max(gate, axis=0, keepdims=True)
            ix = jnp.min(jnp.where(gate == mx, n_io, nb), axis=0, keepdims=True)
            hit = n_io == ix
            chosen = chosen | (hit & (mx > NEG_INF))
            gate = jnp.where(hit, NEG_INF, gate)
        msk_ref[hh][...] = jnp.where(chosen, 0.0, NEG_INF)

    vrows = lambda hh: slice(hh * MB_VROWS, (hh + 1) * MB_VROWS)

    def pv_stage(blk_idx):
        vtb = vt_ref[blk_idx]
        r = [jnp.dot(vtb[vrows(hh)], p_ref[hh][...], preferred_element_type=F32) for hh in heads]
        al = [al_ref[hh][...] for hh in heads]
        a_new = [al[hh] * acc_ref[hh][...] + r[hh][:MB_DH] for hh in heads]
        l_new = [al[hh] * l_ref[hh][...] + r[hh][MB_DH:MB_DH + 1] for hh in heads]
        return a_new, l_new

    def store_pv(a_new, l_new):
        for hh in heads:
            acc_ref[hh][...] = a_new[hh]
            l_ref[hh][...] = l_new[hh]

    def softmax_stage():
        s = [s_ref[hh][...] for hh in heads]
        m_old = [m_ref[hh][...] for hh in heads]
        m_new = [jnp.maximum(m_old[hh], jnp.max(s[hh], axis=0, keepdims=True)) for hh in heads]
        alpha = [jnp.exp(m_old[hh] - m_new[hh]) for hh in heads]
        p = [jnp.exp((s[hh] - m_new[hh]).astype(BF16)) for hh in heads]
        return p, alpha, m_new

    def store_softmax(p, alpha, m_new):
        for hh in heads:
            p_ref[hh][...] = p[hh]
            al_ref[hh][...] = alpha[hh]
            m_ref[hh][...] = m_new[hh]

    k_own = k_ref[pl.ds(pl.multiple_of(qi * blk, blk), blk), :]
    key_io = lax.broadcasted_iota(I32, (blk, blk), 0)
    qry_io = lax.broadcasted_iota(I32, (blk, blk), 1)
    for hh in heads:
        s = jnp.dot(k_own[:, grp(hh)], qht[hh], preferred_element_type=F32) + bias_ref[hh, 0]
        s_ref[hh][...] = jnp.where(key_io <= qry_io, s, NEG_INF)
        m_ref[hh][...] = jnp.full((1, blk), NEG_INF, F32)
        l_ref[hh][...] = jnp.zeros((1, blk), F32)
        al_ref[hh][...] = jnp.ones((1, blk), F32)
        acc_ref[hh][...] = jnp.zeros((MB_DH, blk), F32)
        p_ref[hh][...] = jnp.zeros((blk, blk), BF16)

    def step(i, carry, far):
        pv = pv_stage(jnp.where(i <= 1, qi, i - 2))
        sm = softmax_stage()
        kn = k_ref[pl.ds(pl.multiple_of(i * blk, blk), blk), :]
        if far:
            row = [msk_ref[hh][pl.ds(i, 1), :] + bias_ref[hh, MB_BIAS_TILES - 1, 0:1, 0:1] for hh in heads]
            s_next = [jnp.dot(kn[:, grp(hh)], qht[hh], preferred_element_type=F32) + row[hh] for hh in heads]
        else:
            d = qi - i
            s_next = [jnp.dot(kn[:, grp(hh)], qht[hh], preferred_element_type=F32)
                      + bias_ref[hh, d] + msk_ref[hh][pl.ds(i, 1), :] for hh in heads]
        store_pv(*pv)
        for hh in heads:
            s_ref[hh][...] = s_next[hh]
        store_softmax(*sm)
        return carry

    n_far = jnp.maximum(qi - (MB_BIAS_TILES - 2), 0)
    lax.fori_loop(0, n_far, functools.partial(step, far=True), 0)
    lax.fori_loop(n_far, qi, functools.partial(step, far=False), 0)
    pv = pv_stage(jnp.where(qi <= 1, qi, qi - 2))
    sm = softmax_stage()
    store_pv(*pv)
    store_softmax(*sm)
    a_fin, l_fin = pv_stage(jnp.where(qi == 0, qi, qi - 1))
    out_t = jnp.concatenate([a_fin[hh] / l_fin[hh] for hh in heads], axis=0)
    o_ref[...] = out_t.T.astype(o_ref.dtype)


def moba_attention(pqk, vt, km, bias, batch, seq, qb0=0, nqb=None):
    nb = seq // MB_BLOCK
    nqb = nb if nqb is None else nqb
    t = batch * nqb * MB_BLOCK
    groups = MB_WIDTH // MB_PW
    return pl.pallas_call(
        functools.partial(_moba_kernel, qb0=qb0),
        grid=(batch, groups, nqb),
        in_specs=[
            pl.BlockSpec((MB_BLOCK, MB_PW), lambda b, j, i: (b * nb + qb0 + i, j)),
            pl.BlockSpec((seq, MB_PW), lambda b, j, i: (b, groups + j)),
            pl.BlockSpec((nb, MB_PAIR * MB_VROWS, MB_BLOCK), lambda b, j, i: (b, j, 0)),
            pl.BlockSpec((None, nb, MB_PW), lambda b, j, i: (b, 0, j)),
            pl.BlockSpec((MB_PAIR, MB_BIAS_TILES, MB_BLOCK, MB_BLOCK), lambda b, j, i: (j, 0, 0, 0)),
        ],
        out_specs=pl.BlockSpec((MB_BLOCK, MB_PW), lambda b, j, i: (b * nqb + i, j)),
        out_shape=jax.ShapeDtypeStruct((t, MB_WIDTH), BF16),
        scratch_shapes=(
            [pltpu.VMEM((1, MB_BLOCK), F32)] * (3 * MB_PAIR)
            + [pltpu.VMEM((MB_DH, MB_BLOCK), F32)] * MB_PAIR
            + [pltpu.VMEM((nb, MB_BLOCK), F32)] * MB_PAIR
            + [pltpu.VMEM((MB_BLOCK, MB_BLOCK), F32)] * MB_PAIR
            + [pltpu.VMEM((MB_BLOCK, MB_BLOCK), BF16)] * MB_PAIR
        ),
        compiler_params=_cparams(("parallel", "parallel", "arbitrary")),
        name="moba_attn",
    )(pqk, pqk, vt, km, bias)


def _t5_bucket(dist):
    max_exact = REL_BUCKETS // 2
    scaled = jnp.log(jnp.maximum(dist, 1).astype(F32) / max_exact) / math.log(REL_MAX_DIST / max_exact)
    large = jnp.minimum(max_exact + (scaled * (REL_BUCKETS - max_exact)).astype(I32), REL_BUCKETS - 1)
    return jnp.where(dist < max_exact, dist, large)


def moba_bias_tiles(rel_bias):
    blk = MB_BLOCK
    span = 2 * blk - 1
    x = jnp.arange(span) - (blk - 1)
    dist = jnp.maximum(jnp.arange(MB_BIAS_TILES)[:, None] * blk + x[None, :], 0)
    w = rel_bias.astype(F32).T[:, _t5_bucket(dist)]
    h = w.shape[0]
    wp = jnp.pad(w, ((0, 0), (0, 0), (0, 1)))
    a = jnp.broadcast_to(wp[:, :, None, :], (h, MB_BIAS_TILES, blk, span + 1))
    a = a.reshape(h, MB_BIAS_TILES, blk * (span + 1))[:, :, :blk * span]
    return a.reshape(h, MB_BIAS_TILES, blk, span)[:, :, :, blk - 1:]


def _mix_kernel(x_ref, ya_ref, yb_ref, ga_ref, gb_ref, wa_ref, wb_ref, wo_ref, o_ref):
    za = jnp.dot(ya_ref[...], wa_ref[...], preferred_element_type=F32)
    zb = jnp.dot(yb_ref[...], wb_ref[...], preferred_element_type=F32)
    z = jax.nn.sigmoid(ga_ref[...].astype(F32)) * za + jax.nn.sigmoid(gb_ref[...].astype(F32)) * zb
    o_ref[...] = x_ref[...] + jnp.dot(z.astype(BF16), wo_ref[...], preferred_element_type=F32)


def _mem_kv_kernel(m_ref, g_ref, wk_ref, wv_ref, k_ref, v_ref):
    mn = _rms(m_ref[...], g_ref[...]).astype(BF16)
    k_ref[...] = jnp.dot(mn, wk_ref[...], preferred_element_type=F32).astype(BF16)
    v_ref[...] = jnp.dot(mn, wv_ref[...], preferred_element_type=F32).astype(BF16)


def mem_kv(mem, g, wk, wv):
    b, m, d = mem.shape
    spec = pl.BlockSpec((None, m, d), lambda i: (i, 0, 0))
    wspec = pl.BlockSpec((d, d), lambda i: (0, 0))
    return pl.pallas_call(
        _mem_kv_kernel,
        grid=(b,),
        in_specs=[spec, pl.BlockSpec((1, d), lambda i: (0, 0)), wspec, wspec],
        out_specs=[spec, spec],
        out_shape=[jax.ShapeDtypeStruct((b, m, d), BF16)] * 2,
        compiler_params=_cparams(("parallel",)),
        name="mem_kv",
    )(mem, g, wk, wv)


def _cross_kernel(x_ref, g_ref, wq_ref, k_ref, v_ref, wo_ref, o_ref):
    x = x_ref[...]
    d = x.shape[1]
    dh = d // X_HEADS
    h = _rms(x, g_ref[...]).astype(BF16)
    q = (jnp.dot(h, wq_ref[...], preferred_element_type=F32) * (dh ** -0.5)).astype(BF16)
    outs = []
    for hh in range(X_HEADS):
        sl = slice(hh * dh, (hh + 1) * dh)
        s = lax.dot_general(q[:, sl], k_ref[:, sl], (((1,), (1,)), ((), ())),
                            preferred_element_type=F32)
        p = jnp.exp(s - jnp.max(s, axis=1, keepdims=True))
        l = jnp.sum(p, axis=1, keepdims=True)
        o = jnp.dot(p.astype(BF16), v_ref[:, sl], preferred_element_type=F32) / l
        outs.append(o.astype(BF16))
    o = jnp.concatenate(outs, axis=1)
    o_ref[...] = x + jnp.dot(o, wo_ref[...], preferred_element_type=F32)


def _mix_cross_kernel(x_ref, ya_ref, yb_ref, ga_ref, gb_ref, wa_ref, wb_ref, wo_ref,
                      g_ref, wq_ref, k_ref, v_ref, wox_ref, o_ref, x1_ref):
    _mix_kernel(x_ref, ya_ref, yb_ref, ga_ref, gb_ref, wa_ref, wb_ref, wo_ref, x1_ref)
    _cross_kernel(x1_ref, g_ref, wq_ref, k_ref, v_ref, wox_ref, o_ref)


def mix_cross(x2d, ya, yb, pg, wa, wb, wo, g, wq, kx, vx, wox, tok0, tm=512):
    t = yb.shape[0]
    assert t % tm == 0 and tok0 % tm == 0
    d = x2d.shape[1]
    w = ya.shape[1]
    m = kx.shape[1]
    b0 = tok0 // tm
    const = lambda a: pl.BlockSpec(a.shape, lambda i: (0,) * a.ndim)
    kv = pl.BlockSpec((None, m, d), lambda i: (0, 0, 0))
    return pl.pallas_call(
        _mix_cross_kernel,
        grid=(t // tm,),
        in_specs=[
            pl.BlockSpec((tm, d), lambda i: (b0 + i, 0)),
            pl.BlockSpec((tm, w), lambda i: (b0 + i, 0)),
            pl.BlockSpec((tm, w), lambda i: (i, 0)),
            pl.BlockSpec((tm, d), lambda i: (b0 + i, 0)),
            pl.BlockSpec((tm, d), lambda i: (b0 + i, 1)),
            const(wa), const(wb), const(wo), const(g), const(wq), kv, kv, const(wox),
        ],
        out_specs=pl.BlockSpec((tm, d), lambda i: (i, 0)),
        out_shape=jax.ShapeDtypeStruct((t, d), F32),
        scratch_shapes=[pltpu.VMEM((tm, d), F32)],
        compiler_params=_cparams(("parallel",)),
        name="mix_cross",
    )(x2d, ya, yb, pg, pg, wa, wb, wo, g, wq, kx, vx, wox)


def _topk_rows(sc, k):
    n = sc.shape[0]
    io = lax.broadcasted_iota(I32, sc.shape, 0).astype(F32)
    vals, ids = [], []
    for _ in range(k):
        m = jnp.max(sc, axis=0, keepdims=True)
        ix = jnp.min(jnp.where(sc == m, io, float(n)), axis=0, keepdims=True)
        vals.append(m)
        ids.append(ix)
        sc = jnp.where(io == ix, NEG_INF, sc)
    return jnp.concatenate(vals, axis=0), jnp.concatenate(ids, axis=0).astype(I32)


def _pack_bf16_halves(h):
    bits = lax.bitcast_convert_type(h, I32)
    r = bits + 0x7FFF + (lax.shift_right_logical(bits, 16) & 1)
    half = h.shape[1] // 2
    return lax.shift_right_logical(r[:, :half], 16) | (r[:, half:] & HI_MASK)


def _route_kernel(x_ref, g_ref, wq_ref, sk_ref, hp_ref, idx_ref, w_ref, hb_ref, it_ref, wt_ref):
    p = pl.program_id(1)

    @pl.when(p == 0)
    def _():
        h = _rms(x_ref[...], g_ref[...])
        hp_ref[...] = _pack_bf16_halves(h)
        hb_ref[...] = h.astype(BF16)

    qh = jnp.dot(hb_ref[...], wq_ref[...], preferred_element_type=F32)
    tops = []
    for c in range(2):
        seg = qh[:, c * PEER_HALF:(c + 1) * PEER_HALF]
        sc = lax.dot_general(sk_ref[c], seg, (((1,), (1,)), ((), ())),
                             precision=lax.Precision.HIGHEST, preferred_element_type=F32)
        tops.append(_topk_rows(sc, PEER_TOPK))
    (s0, i0), (s1, i1) = tops
    k = PEER_TOPK
    sub = 8
    tm = s0.shape[1]
    r8 = lax.broadcasted_iota(I32, (sub, tm), 0)
    r16 = lax.broadcasted_iota(I32, (k, tm), 0)
    cand_b = [s0[0:1] + s1, s0[1:2] + s1[:sub]]
    cidx_b = [i0[0:1] * PEER_NKEYS + i1, i0[1:2] * PEER_NKEYS + i1[:sub]]
    pos_b = [r16, k + r8]
    for a in range(2, sub):
        keep = r8 < (k // (a + 1))
        cand_b.append(jnp.where(keep, s0[a:a + 1] + s1[:sub], NEG_INF))
        cidx_b.append(i0[a:a + 1] * PEER_NKEYS + i1[:sub])
        pos_b.append(a * k + r8)
    cand_b.append(s0[sub:] + s1[0:1])
    cidx_b.append(i0[sub:] * PEER_NKEYS + i1[0:1])
    pos_b.append((sub + r8) * k)
    cand = jnp.concatenate(cand_b, axis=0)
    cidx = jnp.concatenate(cidx_b, axis=0)
    pos = jnp.concatenate(pos_b, axis=0).astype(F32)
    vals, ids = [], []
    for _ in range(k):
        m = jnp.max(cand, axis=0, keepdims=True)
        px = jnp.min(jnp.where(cand == m, pos, float(k * k)), axis=0, keepdims=True)
        hit = pos == px
        vals.append(m)
        ids.append(jnp.sum(jnp.where(hit, cidx, 0), axis=0, keepdims=True))
        cand = jnp.where(hit, NEG_INF, cand)
    sf = jnp.concatenate(vals, axis=0)
    e = jnp.exp(sf - sf[0:1])
    rows = pl.ds(pl.multiple_of(p * PEER_TOPK, PEER_TOPK), PEER_TOPK)
    wt_ref[rows, :] = e / jnp.sum(e, axis=0, keepdims=True)
    it_ref[rows, :] = jnp.concatenate(ids, axis=0)

    @pl.when(p == pl.num_programs(1) - 1)
    def _():
        idx_ref[...] = it_ref[...].T
        w_ref[...] = wt_ref[...].T


def peer_route(x2d, g, wq, sk, tok0, t, tm=1024):
    assert t % tm == 0 and tok0 % tm == 0
    d = x2d.shape[1]
    ph = sk.shape[0]
    nsel = ph * PEER_TOPK
    blk0 = tok0 // tm
    return pl.pallas_call(
        _route_kernel,
        grid=(t // tm, ph),
        in_specs=[
            pl.BlockSpec((tm, d), lambda i, p: (blk0 + i, 0)),
            pl.BlockSpec((1, d), lambda i, p: (0, 0)),
            pl.BlockSpec((d, 2 * PEER_HALF), lambda i, p: (0, p)),
            pl.BlockSpec((None, 2, PEER_NKEYS, PEER_HALF), lambda i, p: (p, 0, 0, 0)),
        ],
        out_specs=[
            pl.BlockSpec((tm, d // 2), lambda i, p: (i, 0)),
            pl.BlockSpec((tm, nsel), lambda i, p: (i, 0)),
            pl.BlockSpec((tm, nsel), lambda i, p: (i, 0)),
        ],
        out_shape=[jax.ShapeDtypeStruct((t, d // 2), I32),
                   jax.ShapeDtypeStruct((t, nsel), I32),
                   jax.ShapeDtypeStruct((t, nsel), F32)],
        scratch_shapes=[pltpu.VMEM((tm, d), BF16),
                        pltpu.VMEM((nsel, tm), I32),
                        pltpu.VMEM((nsel, tm), F32)],
        compiler_params=_cparams(("parallel", "arbitrary")),
        name="peer_route",
    )(x2d, g, wq, sk)


def _final_kernel(x_ref, y_ref, g_ref, o_ref):
    o_ref[...] = _rms(x_ref[...] + y_ref[...], g_ref[...])


def final_norm(x2d, y, g, tok0, tm=512):
    t, d = y.shape
    assert t % tm == 0 and tok0 % tm == 0
    blk0 = tok0 // tm
    spec = pl.BlockSpec((tm, d), lambda i: (i, 0))
    return pl.pallas_call(
        _final_kernel, grid=(t // tm,),
        in_specs=[pl.BlockSpec((tm, d), lambda i: (blk0 + i, 0)), spec, pl.BlockSpec((1, d), lambda i: (0, 0))],
        out_specs=spec,
        out_shape=jax.ShapeDtypeStruct((t, d), F32),
        compiler_params=_cparams(("parallel",)), name="final_norm",
    )(x2d, y, g)


SC_CORES = 2
SC_SUBCORES = 16
SC_WORKERS = SC_CORES * SC_SUBCORES
SC_LANES = 16
SC_GROUP = 32


def _sc_mesh():
    return plsc.VectorSubcoreMesh(core_axis_name="c", subcore_axis_name="s")


def _sc_params():
    return pltpu.CompilerParams(needs_layout_passes=False)


def _sc_worker_id():
    return lax.axis_index("s") * SC_CORES + lax.axis_index("c")


SC_ROW_LANE = 128


def _sc_unit_off(u):
    off = u * SC_LANES
    return off if isinstance(off, int) else pl.multiple_of(off, SC_LANES)


GELU_C0 = math.sqrt(2.0 / math.pi)
GELU_C1 = 0.044715


def _gelu_tanh(x):
    z = GELU_C0 * (x + GELU_C1 * (x * x * x))
    th = 1.0 - 2.0 / (jnp.exp(2.0 * z) + 1.0)
    return 0.5 * x * (1.0 + th)


SC_PK_RING = 4
SC_PK_SUB = 4
HI_MASK = -65536


def _pack_tables_kernel(u_ref, v_ref, o_ref):
    for part, ref in enumerate((u_ref, v_ref)):
        words = _pack_bf16_halves(ref[...])
        for sub in range(SC_PK_SUB):
            o_ref[:, part * SC_PK_SUB + sub, :] = words[:, sub * SC_ROW_LANE:(sub + 1) * SC_ROW_LANE]


def pack_expert_tables(u, v, te=512):
    e, d = u.shape
    assert d == 2 * SC_PK_SUB * SC_ROW_LANE
    spec = pl.BlockSpec((te, d), lambda i: (i, 0))
    return pl.pallas_call(
        _pack_tables_kernel, grid=(e // te,), in_specs=[spec, spec],
        out_specs=pl.BlockSpec((te, 2 * SC_PK_SUB, SC_ROW_LANE), lambda i: (i, 0, 0)),
        out_shape=jax.ShapeDtypeStruct((e, 2 * SC_PK_SUB, SC_ROW_LANE), I32),
        compiler_params=_cparams(("parallel",)), name="pack_expert_tables",
    )(u, v)


def _unpack_halves(x32):
    w = plsc.bitcast(x32, I32)
    return plsc.bitcast(w << 16, F32), plsc.bitcast(w & HI_MASK, F32)


def _tree_sum(xs):
    while len(xs) > 1:
        xs = [xs[i] + xs[i + 1] for i in range(0, len(xs), 2)]
    return xs[0]


def peer_experts_pk_sc(tab_uv, idx_flat, w_flat, hp, d):
    t = hp.shape[0]
    nsel = PEER_SEL
    g = SC_GROUP
    assert t % (SC_WORKERS * g) == 0 and d == 2 * SC_PK_SUB * SC_ROW_LANE
    tpw = t // SC_WORKERS
    groups = tpw // g
    heads = nsel // SC_LANES
    chunks = d // 32
    units = g * heads
    ring = SC_PK_RING
    assert units % ring == 0
    row_buf = pltpu.VMEM((SC_LANES, 2 * SC_PK_SUB, SC_ROW_LANE), I32)

    def row_words(rows, r, wc, sub0):
        per = SC_ROW_LANE // SC_LANES
        return plsc.bitcast(
            rows[r, sub0 + wc // per, pl.ds(pl.multiple_of((wc % per) * SC_LANES, SC_LANES), SC_LANES)], BF16)

    def ring_loop(n_units, start, wait, compute):
        for u in range(ring - 1):
            start(u, u)

        @pl.loop(0, n_units, step=ring)
        def _(uu):
            for b in range(ring):
                u = uu + b
                nxt = u + (ring - 1)

                @pl.when(nxt < n_units)
                def _():
                    start(nxt, (b + ring - 1) % ring)

                wait(u, b)
                compute(u, b)

    @functools.partial(
        pl.kernel, mesh=_sc_mesh(),
        out_type=jax.ShapeDtypeStruct((t, d), F32),
        scratch_types=[
            pltpu.VMEM((g * nsel,), I32),
            pltpu.VMEM((g * nsel,), F32),
            pltpu.VMEM((g, d // 2), I32),
            pltpu.VMEM((g, d), F32),
            pltpu.VMEM((SC_LANES * SC_LANES,), F32),
            [row_buf] * ring,
            [pltpu.SemaphoreType.DMA] * ring,
        ],
        compiler_params=_sc_params(),
        name="peer_experts_pk_sc",
    )
    def k(tab_hbm, idx_hbm, w_hbm, h_hbm, out_hbm, idx_v, coef_v, h_v, y_v, red_v, rows, sems):
        wid = _sc_worker_id()
        lane = lax.iota(I32, SC_LANES)

        def copy(u, slot):
            ids = idx_v.at[pl.ds(_sc_unit_off(u), SC_LANES)]
            return pltpu.make_async_copy(tab_hbm.at[ids], rows[slot], sems[slot])

        def dots(u, slot):
            tt = u // heads

            def body(cp, accs):
                out = []
                hv = [plsc.bitcast(h_v[tt, pl.ds(pl.multiple_of((2 * cp + i) * SC_LANES, SC_LANES), SC_LANES)], BF16)
                      for i in range(2)]
                for r in range(SC_LANES):
                    pr = (row_words(rows[slot], r, 2 * cp, 0) * hv[0]
                          + row_words(rows[slot], r, 2 * cp + 1, 0) * hv[1])
                    lo, hi = _unpack_halves(pr)
                    out.append(accs[r] + lo + hi)
                return tuple(out)

            accs = lax.fori_loop(0, chunks // 2, body,
                                 tuple(jnp.zeros((SC_LANES,), F32) for _ in range(SC_LANES)))
            for r in range(SC_LANES):
                red_v[pl.ds(r * SC_LANES, SC_LANES)] = accs[r]
            act = _tree_sum([plsc.load_gather(red_v, [lane * SC_LANES + j]) for j in range(SC_LANES)])
            sl = pl.ds(_sc_unit_off(u), SC_LANES)
            coef_v[sl] = coef_v[sl] * _gelu_tanh(act)

        def combine(u, slot):
            tt = u // heads
            first = (u % heads) == 0
            cb = []
            for r in range(SC_LANES):
                c = plsc.load_gather(coef_v, [jnp.full((SC_LANES,), u * SC_LANES + r, I32)])
                cb.append(plsc.pack(c, c, format=plsc.PackFormat.INTERLEAVED))

            @plsc.parallel_loop(0, chunks, unroll=2)
            def _(wc):
                lo, hi = _unpack_halves(
                    _tree_sum([cb[r] * row_words(rows[slot], r, wc, SC_PK_SUB) for r in range(SC_LANES)]))
                for half, val in ((0, lo), (1, hi)):
                    sl = pl.ds(pl.multiple_of(half * (d // 2) + wc * SC_LANES, SC_LANES), SC_LANES)
                    y_v[tt, sl] = val + jnp.where(first, 0.0, y_v[tt, sl])

        def unit(u, slot):
            dots(u, slot)
            combine(u, slot)

        @pl.loop(0, groups)
        def _(gi):
            base = wid * tpw + gi * g
            pltpu.sync_copy(idx_hbm.at[pl.ds(base * nsel, g * nsel)], idx_v)
            pltpu.sync_copy(w_hbm.at[pl.ds(base * nsel, g * nsel)], coef_v)
            pltpu.sync_copy(h_hbm.at[pl.ds(base, g)], h_v)
            ring_loop(units, lambda u, s: copy(u, s).start(), lambda u, s: copy(u, s).wait(), unit)
            pltpu.sync_copy(y_v, out_hbm.at[pl.ds(base, g)])

    return k(tab_uv, idx_flat, w_flat, hp)


def kernel(x, mem, rel_bias, ln_mix, w_in, hg_lower, hg_norm, w_up_a, w_up_b, w_out, ln_cross, ln_mem, wq_x, wk_x, wv_x, wo_x, ln_ffn, peer_query, peer_subkeys, peer_u, peer_v, ln_final):
    b, s, d = x.shape
    depth = w_in.shape[0]
    assert depth == 1, "the residual after PEER is fused into the final norm"
    assert s % MB_BLOCK == 0 and s % HG_CHUNK == 0 and s % (PEER_SLICES * SC_WORKERS * SC_GROUP) == 0
    nb = s // MB_BLOCK
    row = lambda a: a.reshape(1, -1).astype(F32)
    lb_all = jnp.cumsum(jax.nn.softmax(hg_lower.astype(F32), axis=0), axis=0)
    bias = moba_bias_tiles(rel_bias)
    n_hg = 4 * HG_WIDTH
    n_qk = 2 * MB_WIDTH
    n_mb = 3 * MB_WIDTH
    l = 0
    w = w_in[l].astype(BF16)
    w_hg, w_qk, w_vt, w_g = w[:, :n_hg], w[:, n_hg:n_hg + n_qk], w[:, n_hg + n_qk:n_hg + n_mb].T, w[:, n_hg + n_mb:]
    wa, wb, wo = w_up_a[l].astype(BF16), w_up_b[l].astype(BF16), w_out[l].astype(BF16)
    wqx, wox = wq_x[l].astype(BF16), wo_x[l].astype(BF16)
    wpq, sk = peer_query[l].astype(BF16), peer_subkeys[l].astype(F32)
    tab_uv = pack_expert_tables(peer_u[l].astype(F32), peer_v[l].astype(F32))
    kx, vx = mem_kv(mem, row(ln_mem[l]), wk_x[l].astype(BF16), wv_x[l].astype(BF16))

    outs = []
    for bi in range(b):
        x2d = x[bi]
        p0, pqk, km, vt, pg = in_proj(x2d, row(ln_mix[l]), w_hg, w_qk, w_vt, w_g)
        ya = hgrn2(p0, row(lb_all[l]), row(hg_norm[l]), 1, s)
        km = km.reshape(1, nb, MB_WIDTH)
        ts = s // PEER_SLICES
        for tok0 in range(0, s, ts):
            yb = moba_attention(pqk, vt, km, bias, 1, s, tok0 // MB_BLOCK, ts // MB_BLOCK)
            xs = mix_cross(x2d, ya, yb, pg, wa, wb, wo, row(ln_cross[l]), wqx, kx[bi:bi + 1], vx[bi:bi + 1], wox, tok0)
            hp, eidx, wts = peer_route(xs, row(ln_ffn[l]), wpq, sk, 0, ts)
            y = peer_experts_pk_sc(tab_uv, eidx.reshape(ts * PEER_SEL), wts.reshape(ts * PEER_SEL), hp, d)
            outs.append(final_norm(xs, y, row(ln_final), 0))
    return jnp.concatenate(outs, axis=0).reshape(b, s, d)
```

```python
import functools
import math

import jax
import jax.numpy as jnp
from jax import lax
from jax.experimental import pallas as pl
from jax.experimental.pallas import tpu as pltpu
from jax.experimental.pallas import tpu_sc as plsc

F32 = jnp.float32
BF16 = jnp.bfloat16
I32 = jnp.int32
EPS = 1e-6
NEG_INF = float("-inf")

HG_HEADS = 4
HG_D = 128
HG_WIDTH = HG_HEADS * HG_D
HG_CHUNK = 64
HG_SUB = 16
MB_HEADS = 8
MB_DH = 64
MB_WIDTH = MB_HEADS * MB_DH
MB_BLOCK = 256
MB_TOPK = 3
MB_BIAS_TILES = 8
REL_BUCKETS = 32
REL_MAX_DIST = 2048
X_HEADS = 4
PEER_HEADS = 8
PEER_NKEYS = 128
PEER_TOPK = 16
PEER_HALF = 128
PEER_SEL = PEER_HEADS * PEER_TOPK
PEER_SLICES = 4

VMEM_LIMIT = 56 * 1024 * 1024


def _cparams(sem):
    return pltpu.CompilerParams(dimension_semantics=sem, vmem_limit_bytes=VMEM_LIMIT)


def _rms(x, g):
    ms = jnp.mean(x * x, axis=-1, keepdims=True)
    return x * lax.rsqrt(ms + EPS) * g


def _in_proj_kernel(x_ref, g_ref, w0_ref, w1_ref, wvt_ref, w2_ref, o0_ref, o1_ref, okm_ref, ovt_ref, o2_ref):
    h = _rms(x_ref[...], g_ref[...]).astype(BF16)
    o0_ref[...] = jnp.dot(h, w0_ref[...], preferred_element_type=F32)
    qk = jnp.dot(h, w1_ref[...], preferred_element_type=F32)
    o1_ref[...] = qk.astype(BF16)
    okm_ref[0] = jnp.mean(qk[:, MB_WIDTH:], axis=0, keepdims=True)
    vt = lax.dot_general(wvt_ref[...], h, (((1,), (1,)), ((), ())), preferred_element_type=F32).astype(BF16)
    for hd in range(MB_HEADS):
        ovt_ref[0, hd * MB_VROWS:hd * MB_VROWS + MB_DH, :] = vt[hd * MB_DH:(hd + 1) * MB_DH]
        ovt_ref[0, hd * MB_VROWS + MB_DH:(hd + 1) * MB_VROWS, :] = jnp.ones((MB_ONES, vt.shape[1]), BF16)
    o2_ref[...] = jnp.dot(h, w2_ref[...], preferred_element_type=F32).astype(BF16)


def in_proj(x2d, g, w0, w1, wvt, w2):
    t, d = x2d.shape
    tm = MB_BLOCK
    assert wvt.shape[0] == MB_WIDTH and w1.shape[1] == 2 * MB_WIDTH
    n0, n1, nv, n2 = w0.shape[1], w1.shape[1], MB_VT_ROWS, w2.shape[1]
    full = lambda a: pl.BlockSpec(a.shape, lambda i: (0, 0))
    return pl.pallas_call(
        _in_proj_kernel,
        grid=(t // tm,),
        in_specs=[pl.BlockSpec((tm, d), lambda i: (i, 0)), full(g), full(w0), full(w1), full(wvt), full(w2)],
        out_specs=[pl.BlockSpec((tm, n0), lambda i: (i, 0)),
                   pl.BlockSpec((tm, n1), lambda i: (i, 0)),
                   pl.BlockSpec((1, 1, MB_WIDTH), lambda i: (i, 0, 0)),
                   pl.BlockSpec((1, nv, tm), lambda i: (i, 0, 0)),
                   pl.BlockSpec((tm, n2), lambda i: (i, 0))],
        out_shape=[jax.ShapeDtypeStruct((t, n0), F32),
                   jax.ShapeDtypeStruct((t, n1), BF16),
                   jax.ShapeDtypeStruct((t // tm, 1, MB_WIDTH), F32),
                   jax.ShapeDtypeStruct((t // tm, nv, tm), BF16),
                   jax.ShapeDtypeStruct((t, n2), BF16)],
        compiler_params=_cparams(("parallel",)),
        name="in_proj",
    )(x2d, g, w0, w1, wvt, w2)


def _hgrn_kernel(q_ref, f_ref, i_ref, g_ref, lb_ref, gain_ref, o_ref, st_ref):
    c = pl.program_id(1)

    @pl.when(c == 0)
    def _():
        st_ref[...] = jnp.zeros_like(st_ref)

    C, S = HG_CHUNK, HG_SUB
    row = lax.broadcasted_iota(I32, (C, C), 0)
    col = lax.broadcasted_iota(I32, (C, C), 1)
    tril = (row >= col).astype(F32)
    t_iota = lax.broadcasted_iota(I32, (S, 1), 0)

    for h in range(HG_HEADS):
        sl = slice(h * HG_D, (h + 1) * HG_D)
        q = q_ref[:, sl]
        v = i_ref[:, sl]
        lb = lb_ref[:, sl]
        f = lb + (1.0 - lb) * jax.nn.sigmoid(f_ref[:, sl])
        lf = jnp.log(f)
        k = 1.0 - f
        b = jnp.dot(tril, lf, precision=lax.Precision.HIGHEST, preferred_element_type=F32)
        st = st_ref[h]
        vb = v.astype(BF16)
        qd = (q * jnp.exp(b)).astype(BF16)
        o_inter = lax.dot_general(qd, st.astype(BF16), (((1,), (1,)), ((), ())),
                                  preferred_element_type=F32)
        outs = []
        for i in range(C // S):
            r0 = i * S
            qi = q[r0:r0 + S]
            ki = k[r0:r0 + S]
            bi = b[r0:r0 + S]
            vi = v[r0:r0 + S]
            oi = o_inter[r0:r0 + S]
            if i > 0:
                bs = b[r0 - 1:r0]
                qh = (qi * jnp.exp(bi - bs)).astype(BF16)
                kh = (k[:r0] * jnp.exp(bs - b[:r0])).astype(BF16)
                a = lax.dot_general(qh, kh, (((1,), (1,)), ((), ())), preferred_element_type=F32)
                oi = oi + jnp.dot(a.astype(BF16), vb[:r0], preferred_element_type=F32)
            half = S // 2
            o_half = [oi[:half], oi[half:]]
            for s in range(S):
                for hf in range(s // half, 2):
                    rows = slice(hf * half, (hf + 1) * half)
                    dec = jnp.exp(jnp.minimum(bi[rows] - bi[s:s + 1], 0.0))
                    a_s = jnp.sum(qi[rows] * ki[s:s + 1] * dec, axis=-1, keepdims=True)
                    a_s = jnp.where(t_iota[rows] >= s, a_s, 0.0)
                    o_half[hf] = o_half[hf] + a_s * vi[s:s + 1]
            outs.extend(o_half)
        o = jnp.concatenate(outs, axis=0)
        b_end = b[C - 1:C]
        kd = (k * jnp.exp(b_end - b)).astype(BF16)
        upd = lax.dot_general(vb, kd, (((0,), (0,)), ((), ())), preferred_element_type=F32)
        st_ref[h] = st * jnp.exp(b_end) + upd
        o = o * lax.rsqrt(jnp.mean(o * o, axis=-1, keepdims=True) + EPS)
        g = g_ref[:, sl]
        o_ref[:, sl] = (o * gain_ref[:, sl] * (g * jax.nn.sigmoid(g))).astype(o_ref.dtype)


def hgrn2(p0, lb, gain, batch, seq):
    t = p0.shape[0]
    nc = seq // HG_CHUNK
    w = HG_WIDTH

    def col(j):
        return pl.BlockSpec((HG_CHUNK, w), lambda b, c, j=j: (b * nc + c, j))

    return pl.pallas_call(
        _hgrn_kernel,
        grid=(batch, nc),
        in_specs=[col(0), col(1), col(2), col(3),
                  pl.BlockSpec((1, w), lambda b, c: (0, 0)),
                  pl.BlockSpec((1, w), lambda b, c: (0, 0))],
        out_specs=pl.BlockSpec((HG_CHUNK, w), lambda b, c: (b * nc + c, 0)),
        out_shape=jax.ShapeDtypeStruct((t, w), BF16),
        scratch_shapes=[pltpu.VMEM((HG_HEADS, HG_D, HG_D), F32)],
        compiler_params=_cparams(("parallel", "arbitrary")),
        name="hgrn2",
    )(p0, p0, p0, p0, lb, gain)


MB_PAIR = 4
MB_PW = MB_PAIR * MB_DH
MB_LG = 128
MB_ONES = 16
MB_VROWS = MB_DH + MB_ONES
MB_VT_ROWS = MB_HEADS * MB_VROWS


def _moba_kernel(q_ref, k_ref, vt_ref, km_ref, bias_ref, o_ref, *scratch, qb0):
    m_ref, l_ref, al_ref, acc_ref, msk_ref, s_ref, p_ref = (
        scratch[i * MB_PAIR:(i + 1) * MB_PAIR] for i in range(7))
    qi = pl.program_id(2) + qb0
    nb = km_ref.shape[0]
    blk = MB_BLOCK
    heads = range(MB_PAIR)
    grp = lambda hh: slice((hh // 2) * MB_LG, (hh // 2 + 1) * MB_LG)
    q = q_ref[...]
    lane = lax.broadcasted_iota(I32, (blk, MB_LG), 1)
    in_head = [(lane < MB_DH) if hh % 2 == 0 else (lane >= MB_DH) for hh in heads]
    qs = q * jnp.asarray(MB_DH ** -0.5, BF16)
    nt = (((1,), (1,)), ((), ()))
    qf = q.astype(F32)
    qht = [jnp.where(in_head[hh], qs[:, grp(hh)].astype(F32), 0.0).T.astype(BF16) for hh in heads]

    n_io = lax.broadcasted_iota(I32, (nb, blk), 0)
    for hh in heads:
        gate = lax.dot_general(km_ref[:, grp(hh)], jnp.where(in_head[hh], qf[:, grp(hh)], 0.0), nt,
                               precision=lax.Precision.HIGHEST, preferred_element_type=F32)
        gate = jnp.where(n_io < qi, gate, NEG_INF)
        chosen = n_io < 0
        for _ in range(MB_TOPK):
            mx = jnp.max(gate, axis=0, keepdims=True)
            ix = jnp.min(jnp.where(gate == mx, n_io, nb), axis=0, keepdims=True)
            hit = n_io == ix
            chosen = chosen | (hit & (mx > NEG_INF))
            gate = jnp.where(hit, NEG_INF, gate)
        msk_ref[hh][...] = jnp.where(chosen, 0.0, NEG_INF)

    vrows = lambda hh: slice(hh * MB_VROWS, (hh + 1) * MB_VROWS)

    def pv_stage(blk_idx):
        vtb = vt_ref[blk_idx]
        r = [jnp.dot(vtb[vrows(hh)], p_ref[hh][...], preferred_element_type=F32) for hh in heads]
        al = [al_ref[hh][...] for hh in heads]
        a_new = [al[hh] * acc_ref[hh][...] + r[hh][:MB_DH] for hh in heads]
        l_new = [al[hh] * l_ref[hh][...] + r[hh][MB_DH:MB_DH + 1] for hh in heads]
        return a_new, l_new

    def store_pv(a_new, l_new):
        for hh in heads:
            acc_ref[hh][...] = a_new[hh]
            l_ref[hh][...] = l_new[hh]

    def softmax_stage():
        s = [s_ref[hh][...] for hh in heads]
        m_old = [m_ref[hh][...] for hh in heads]
        m_new = [jnp.maximum(m_old[hh], jnp.max(s[hh], axis=0, keepdims=True)) for hh in heads]
        alpha = [jnp.exp(m_old[hh] - m_new[hh]) for hh in heads]
        p = [jnp.exp((s[hh] - m_new[hh]).astype(BF16)) for hh in heads]
        return p, alpha, m_new

    def store_softmax(p, alpha, m_new):
        for hh in heads:
            p_ref[hh][...] = p[hh]
            al_ref[hh][...] = alpha[hh]
            m_ref[hh][...] = m_new[hh]

    k_own = k_ref[pl.ds(pl.multiple_of(qi * blk, blk), blk), :]
    key_io = lax.broadcasted_iota(I32, (blk, blk), 0)
    qry_io = lax.broadcasted_iota(I32, (blk, blk), 1)
    for hh in heads:
        s = jnp.dot(k_own[:, grp(hh)], qht[hh], preferred_element_type=F32) + bias_ref[hh, 0]
        s_ref[hh][...] = jnp.where(key_io <= qry_io, s, NEG_INF)
        m_ref[hh][...] = jnp.full((1, blk), NEG_INF, F32)
        l_ref[hh][...] = jnp.zeros((1, blk), F32)
        al_ref[hh][...] = jnp.ones((1, blk), F32)
        acc_ref[hh][...] = jnp.zeros((MB_DH, blk), F32)
        p_ref[hh][...] = jnp.zeros((blk, blk), BF16)

    def step(i, carry, far):
        pv = pv_stage(jnp.where(i <= 1, qi, i - 2))
        sm = softmax_stage()
        kn = k_ref[pl.ds(pl.multiple_of(i * blk, blk), blk), :]
        if far:
            row = [msk_ref[hh][pl.ds(i, 1), :] + bias_ref[hh, MB_BIAS_TILES - 1, 0:1, 0:1] for hh in heads]
            s_next = [jnp.dot(kn[:, grp(hh)], qht[hh], preferred_element_type=F32) + row[hh] for hh in heads]
        else:
            d = qi - i
            s_next = [jnp.dot(kn[:, grp(hh)], qht[hh], preferred_element_type=F32)
                      + bias_ref[hh, d] + msk_ref[hh][pl.ds(i, 1), :] for hh in heads]
        store_pv(*pv)
        for hh in heads:
            s_ref[hh][...] = s_next[hh]
        store_softmax(*sm)
        return carry

    n_far = jnp.maximum(qi - (MB_BIAS_TILES - 2), 0)
    lax.fori_loop(0, n_far, functools.partial(step, far=True), 0)
    lax.fori_loop(n_far, qi, functools.partial(step, far=False), 0)
    pv = pv_stage(jnp.where(qi <= 1, qi, qi - 2))
    sm = softmax_stage()
    store_pv(*pv)
    store_softmax(*sm)
    a_fin, l_fin = pv_stage(jnp.where(qi == 0, qi, qi - 1))
    out_t = jnp.concatenate([a_fin[hh] / l_fin[hh] for hh in heads], axis=0)
    o_ref[...] = out_t.T.astype(o_ref.dtype)


def moba_attention(pqk, vt, km, bias, batch, seq, qb0=0, nqb=None):
    nb = seq // MB_BLOCK
    nqb = nb if nqb is None else nqb
    t = batch * nqb * MB_BLOCK
    groups = MB_WIDTH // MB_PW
    return pl.pallas_call(
        functools.partial(_moba_kernel, qb0=qb0),
        grid=(batch, groups, nqb),
        in_specs=[
            pl.BlockSpec((MB_BLOCK, MB_PW), lambda b, j, i: (b * nb + qb0 + i, j)),
            pl.BlockSpec((seq, MB_PW), lambda b, j, i: (b, groups + j)),
            pl.BlockSpec((nb, MB_PAIR * MB_VROWS, MB_BLOCK), lambda b, j, i: (b, j, 0)),
            pl.BlockSpec((None, nb, MB_PW), lambda b, j, i: (b, 0, j)),
            pl.BlockSpec((MB_PAIR, MB_BIAS_TILES, MB_BLOCK, MB_BLOCK), lambda b, j, i: (j, 0, 0, 0)),
        ],
        out_specs=pl.BlockSpec((MB_BLOCK, MB_PW), lambda b, j, i: (b * nqb + i, j)),
        out_shape=jax.ShapeDtypeStruct((t, MB_WIDTH), BF16),
        scratch_shapes=(
            [pltpu.VMEM((1, MB_BLOCK), F32)] * (3 * MB_PAIR)
            + [pltpu.VMEM((MB_DH, MB_BLOCK), F32)] * MB_PAIR
            + [pltpu.VMEM((nb, MB_BLOCK), F32)] * MB_PAIR
            + [pltpu.VMEM((MB_BLOCK, MB_BLOCK), F32)] * MB_PAIR
            + [pltpu.VMEM((MB_BLOCK, MB_BLOCK), BF16)] * MB_PAIR
        ),
        compiler_params=_cparams(("parallel", "parallel", "arbitrary")),
        name="moba_attn",
    )(pqk, pqk, vt, km, bias)


def _t5_bucket(dist):
    max_exact = REL_BUCKETS // 2
    scaled = jnp.log(jnp.maximum(dist, 1).astype(F32) / max_exact) / math.log(REL_MAX_DIST / max_exact)
    large = jnp.minimum(max_exact + (scaled * (REL_BUCKETS - max_exact)).astype(I32), REL_BUCKETS - 1)
    return jnp.where(dist < max_exact, dist, large)


def moba_bias_tiles(rel_bias):
    blk = MB_BLOCK
    span = 2 * blk - 1
    x = jnp.arange(span) - (blk - 1)
    dist = jnp.maximum(jnp.arange(MB_BIAS_TILES)[:, None] * blk + x[None, :], 0)
    w = rel_bias.astype(F32).T[:, _t5_bucket(dist)]
    h = w.shape[0]
    wp = jnp.pad(w, ((0, 0), (0, 0), (0, 1)))
    a = jnp.broadcast_to(wp[:, :, None, :], (h, MB_BIAS_TILES, blk, span + 1))
    a = a.reshape(h, MB_BIAS_TILES, blk * (span + 1))[:, :, :blk * span]
    return a.reshape(h, MB_BIAS_TILES, blk, span)[:, :, :, blk - 1:]


def _mix_kernel(x_ref, ya_ref, yb_ref, ga_ref, gb_ref, wa_ref, wb_ref, wo_ref, o_ref):
    za = jnp.dot(ya_ref[...], wa_ref[...], preferred_element_type=F32)
    zb = jnp.dot(yb_ref[...], wb_ref[...], preferred_element_type=F32)
    z = jax.nn.sigmoid(ga_ref[...].astype(F32)) * za + jax.nn.sigmoid(gb_ref[...].astype(F32)) * zb
    o_ref[...] = x_ref[...] + jnp.dot(z.astype(BF16), wo_ref[...], preferred_element_type=F32)


def _mem_kv_kernel(m_ref, g_ref, wk_ref, wv_ref, k_ref, v_ref):
    mn = _rms(m_ref[...], g_ref[...]).astype(BF16)
    k_ref[...] = jnp.dot(mn, wk_ref[...], preferred_element_type=F32).astype(BF16)
    v_ref[...] = jnp.dot(mn, wv_ref[...], preferred_element_type=F32).astype(BF16)


def mem_kv(mem, g, wk, wv):
    b, m, d = mem.shape
    spec = pl.BlockSpec((None, m, d), lambda i: (i, 0, 0))
    wspec = pl.BlockSpec((d, d), lambda i: (0, 0))
    return pl.pallas_call(
        _mem_kv_kernel,
        grid=(b,),
        in_specs=[spec, pl.BlockSpec((1, d), lambda i: (0, 0)), wspec, wspec],
        out_specs=[spec, spec],
        out_shape=[jax.ShapeDtypeStruct((b, m, d), BF16)] * 2,
        compiler_params=_cparams(("parallel",)),
        name="mem_kv",
    )(mem, g, wk, wv)


def _cross_kernel(x_ref, g_ref, wq_ref, k_ref, v_ref, wo_ref, o_ref):
    x = x_ref[...]
    d = x.shape[1]
    dh = d // X_HEADS
    h = _rms(x, g_ref[...]).astype(BF16)
    q = (jnp.dot(h, wq_ref[...], preferred_element_type=F32) * (dh ** -0.5)).astype(BF16)
    outs = []
    for hh in range(X_HEADS):
        sl = slice(hh * dh, (hh + 1) * dh)
        s = lax.dot_general(q[:, sl], k_ref[:, sl], (((1,), (1,)), ((), ())),
                            preferred_element_type=F32)
        p = jnp.exp(s - jnp.max(s, axis=1, keepdims=True))
        l = jnp.sum(p, axis=1, keepdims=True)
        o = jnp.dot(p.astype(BF16), v_ref[:, sl], preferred_element_type=F32) / l
        outs.append(o.astype(BF16))
    o = jnp.concatenate(outs, axis=1)
    o_ref[...] = x + jnp.dot(o, wo_ref[...], preferred_element_type=F32)


def _mix_cross_kernel(x_ref, ya_ref, yb_ref, ga_ref, gb_ref, wa_ref, wb_ref, wo_ref,
                      g_ref, wq_ref, k_ref, v_ref, wox_ref, o_ref, x1_ref):
    _mix_kernel(x_ref, ya_ref, yb_ref, ga_ref, gb_ref, wa_ref, wb_ref, wo_ref, x1_ref)
    _cross_kernel(x1_ref, g_ref, wq_ref, k_ref, v_ref, wox_ref, o_ref)


def mix_cross(x2d, ya, yb, pg, wa, wb, wo, g, wq, kx, vx, wox, tok0, tm=512):
    t = yb.shape[0]
    assert t % tm == 0 and tok0 % tm == 0
    d = x2d.shape[1]
    w = ya.shape[1]
    m = kx.shape[1]
    b0 = tok0 // tm
    const = lambda a: pl.BlockSpec(a.shape, lambda i: (0,) * a.ndim)
    kv = pl.BlockSpec((None, m, d), lambda i: (0, 0, 0))
    return pl.pallas_call(
        _mix_cross_kernel,
        grid=(t // tm,),
        in_specs=[
            pl.BlockSpec((tm, d), lambda i: (b0 + i, 0)),
            pl.BlockSpec((tm, w), lambda i: (b0 + i, 0)),
            pl.BlockSpec((tm, w), lambda i: (i, 0)),
            pl.BlockSpec((tm, d), lambda i: (b0 + i, 0)),
            pl.BlockSpec((tm, d), lambda i: (b0 + i, 1)),
            const(wa), const(wb), const(wo), const(g), const(wq), kv, kv, const(wox),
        ],
        out_specs=pl.BlockSpec((tm, d), lambda i: (i, 0)),
        out_shape=jax.ShapeDtypeStruct((t, d), F32),
        scratch_shapes=[pltpu.VMEM((tm, d), F32)],
        compiler_params=_cparams(("parallel",)),
        name="mix_cross",
    )(x2d, ya, yb, pg, pg, wa, wb, wo, g, wq, kx, vx, wox)


def _topk_rows(sc, k):
    n = sc.shape[0]
    io = lax.broadcasted_iota(I32, sc.shape, 0).astype(F32)
    vals, ids = [], []
    for _ in range(k):
        m = jnp.max(sc, axis=0, keepdims=True)
        ix = jnp.argmax(sc, axis=0, keepdims=True).astype(F32)
        vals.append(m)
        ids.append(ix)
        sc = jnp.where(io == ix, NEG_INF, sc)
    return jnp.concatenate(vals, axis=0), jnp.concatenate(ids, axis=0).astype(I32)


def _pack_bf16_halves(h):
    bits = lax.bitcast_convert_type(h, I32)
    r = bits + 0x7FFF + (lax.shift_right_logical(bits, 16) & 1)
    half = h.shape[1] // 2
    return lax.shift_right_logical(r[:, :half], 16) | (r[:, half:] & HI_MASK)


def _route_kernel(x_ref, g_ref, wq_ref, sk_ref, hp_ref, idx_ref, w_ref, hb_ref, it_ref, wt_ref):
    p = pl.program_id(1)

    @pl.when(p == 0)
    def _():
        h = _rms(x_ref[...], g_ref[...])
        hp_ref[...] = _pack_bf16_halves(h)
        hb_ref[...] = h.astype(BF16)

    qh = jnp.dot(hb_ref[...], wq_ref[...], preferred_element_type=F32)
    tops = []
    for c in range(2):
        seg = qh[:, c * PEER_HALF:(c + 1) * PEER_HALF]
        sc = lax.dot_general(sk_ref[c], seg, (((1,), (1,)), ((), ())),
                             precision=lax.Precision.HIGHEST, preferred_element_type=F32)
        tops.append(_topk_rows(sc, PEER_TOPK))
    (s0, i0), (s1, i1) = tops
    k = PEER_TOPK
    sub = 8
    tm = s0.shape[1]
    r8 = lax.broadcasted_iota(I32, (sub, tm), 0)
    r16 = lax.broadcasted_iota(I32, (k, tm), 0)
    cand_b = [s0[0:1] + s1, s0[1:2] + s1[:sub]]
    cidx_b = [i0[0:1] * PEER_NKEYS + i1, i0[1:2] * PEER_NKEYS + i1[:sub]]
    pos_b = [r16, k + r8]
    for a in range(2, sub):
        keep = r8 < (k // (a + 1))
        cand_b.append(jnp.where(keep, s0[a:a + 1] + s1[:sub], NEG_INF))
        cidx_b.append(i0[a:a + 1] * PEER_NKEYS + i1[:sub])
        pos_b.append(a * k + r8)
    cand_b.append(s0[sub:] + s1[0:1])
    cidx_b.append(i0[sub:] * PEER_NKEYS + i1[0:1])
    pos_b.append((sub + r8) * k)
    cand = jnp.concatenate(cand_b, axis=0)
    cidx = jnp.concatenate(cidx_b, axis=0)
    pos = jnp.concatenate(pos_b, axis=0).astype(F32)
    vals, ids = [], []
    for _ in range(k):
        m = jnp.max(cand, axis=0, keepdims=True)
        px = jnp.min(jnp.where(cand == m, pos, float(k * k)), axis=0, keepdims=True)
        hit = pos == px
        vals.append(m)
        ids.append(jnp.sum(jnp.where(hit, cidx, 0), axis=0, keepdims=True))
        cand = jnp.where(hit, NEG_INF, cand)
    sf = jnp.concatenate(vals, axis=0)
    e = jnp.exp(sf - sf[0:1])
    rows = pl.ds(pl.multiple_of(p * PEER_TOPK, PEER_TOPK), PEER_TOPK)
    wt_ref[rows, :] = e / jnp.sum(e, axis=0, keepdims=True)
    it_ref[rows, :] = jnp.concatenate(ids, axis=0)

    @pl.when(p == pl.num_programs(1) - 1)
    def _():
        idx_ref[...] = it_ref[...].T
        w_ref[...] = wt_ref[...].T


def peer_route(x2d, g, wq, sk, tok0, t, tm=1024):
    assert t % tm == 0 and tok0 % tm == 0
    d = x2d.shape[1]
    ph = sk.shape[0]
    nsel = ph * PEER_TOPK
    blk0 = tok0 // tm
    return pl.pallas_call(
        _route_kernel,
        grid=(t // tm, ph),
        in_specs=[
            pl.BlockSpec((tm, d), lambda i, p: (blk0 + i, 0)),
            pl.BlockSpec((1, d), lambda i, p: (0, 0)),
            pl.BlockSpec((d, 2 * PEER_HALF), lambda i, p: (0, p)),
            pl.BlockSpec((None, 2, PEER_NKEYS, PEER_HALF), lambda i, p: (p, 0, 0, 0)),
        ],
        out_specs=[
            pl.BlockSpec((tm, d // 2), lambda i, p: (i, 0)),
            pl.BlockSpec((tm, nsel), lambda i, p: (i, 0)),
            pl.BlockSpec((tm, nsel), lambda i, p: (i, 0)),
        ],
        out_shape=[jax.ShapeDtypeStruct((t, d // 2), I32),
                   jax.ShapeDtypeStruct((t, nsel), I32),
                   jax.ShapeDtypeStruct((t, nsel), F32)],
        scratch_shapes=[pltpu.VMEM((tm, d), BF16),
                        pltpu.VMEM((nsel, tm), I32),
                        pltpu.VMEM((nsel, tm), F32)],
        compiler_params=_cparams(("parallel", "arbitrary")),
        name="peer_route",
    )(x2d, g, wq, sk)


def _final_kernel(x_ref, y_ref, g_ref, o_ref):
    o_ref[...] = _rms(x_ref[...] + y_ref[...], g_ref[...])


def final_norm(x2d, y, g, tok0, tm=512):
    t, d = y.shape
    assert t % tm == 0 and tok0 % tm == 0
    blk0 = tok0 // tm
    spec = pl.BlockSpec((tm, d), lambda i: (i, 0))
    return pl.pallas_call(
        _final_kernel, grid=(t // tm,),
        in_specs=[pl.BlockSpec((tm, d), lambda i: (blk0 + i, 0)), spec, pl.BlockSpec((1, d), lambda i: (0, 0))],
        out_specs=spec,
        out_shape=jax.ShapeDtypeStruct((t, d), F32),
        compiler_params=_cparams(("parallel",)), name="final_norm",
    )(x2d, y, g)


SC_CORES = 2
SC_SUBCORES = 16
SC_WORKERS = SC_CORES * SC_SUBCORES
SC_LANES = 16
SC_GROUP = 32


def _sc_mesh():
    return plsc.VectorSubcoreMesh(core_axis_name="c", subcore_axis_name="s")


def _sc_params():
    return pltpu.CompilerParams(needs_layout_passes=False)


def _sc_worker_id():
    return lax.axis_index("s") * SC_CORES + lax.axis_index("c")


SC_ROW_LANE = 128


def _sc_unit_off(u):
    off = u * SC_LANES
    return off if isinstance(off, int) else pl.multiple_of(off, SC_LANES)


GELU_C0 = math.sqrt(2.0 / math.pi)
GELU_C1 = 0.044715


def _gelu_tanh(x):
    z = GELU_C0 * (x + GELU_C1 * (x * x * x))
    th = 1.0 - 2.0 / (jnp.exp(2.0 * z) + 1.0)
    return 0.5 * x * (1.0 + th)


SC_PK_RING = 4
SC_PK_SUB = 4
HI_MASK = -65536


def _pack_tables_kernel(u_ref, v_ref, o_ref):
    for part, ref in enumerate((u_ref, v_ref)):
        words = _pack_bf16_halves(ref[...])
        for sub in range(SC_PK_SUB):
            o_ref[:, part * SC_PK_SUB + sub, :] = words[:, sub * SC_ROW_LANE:(sub + 1) * SC_ROW_LANE]


def pack_expert_tables(u, v, te=512):
    e, d = u.shape
    assert d == 2 * SC_PK_SUB * SC_ROW_LANE
    spec = pl.BlockSpec((te, d), lambda i: (i, 0))
    return pl.pallas_call(
        _pack_tables_kernel, grid=(e // te,), in_specs=[spec, spec],
        out_specs=pl.BlockSpec((te, 2 * SC_PK_SUB, SC_ROW_LANE), lambda i: (i, 0, 0)),
        out_shape=jax.ShapeDtypeStruct((e, 2 * SC_PK_SUB, SC_ROW_LANE), I32),
        compiler_params=_cparams(("parallel",)), name="pack_expert_tables",
    )(u, v)


def _unpack_halves(x32):
    w = plsc.bitcast(x32, I32)
    return plsc.bitcast(w << 16, F32), plsc.bitcast(w & HI_MASK, F32)


def _tree_sum(xs):
    while len(xs) > 1:
        xs = [xs[i] + xs[i + 1] for i in range(0, len(xs), 2)]
    return xs[0]


def peer_experts_pk_sc(tab_uv, idx_flat, w_flat, hp, d):
    t = hp.shape[0]
    nsel = PEER_SEL
    g = SC_GROUP
    assert t % (SC_WORKERS * g) == 0 and d == 2 * SC_PK_SUB * SC_ROW_LANE
    tpw = t // SC_WORKERS
    groups = tpw // g
    heads = nsel // SC_LANES
    chunks = d // 32
    units = g * heads
    ring = SC_PK_RING
    assert units % ring == 0
    row_buf = pltpu.VMEM((SC_LANES, 2 * SC_PK_SUB, SC_ROW_LANE), I32)

    def row_words(rows, r, wc, sub0):
        per = SC_ROW_LANE // SC_LANES
        return plsc.bitcast(
            rows[r, sub0 + wc // per, pl.ds(pl.multiple_of((wc % per) * SC_LANES, SC_LANES), SC_LANES)], BF16)

    def ring_loop(n_units, start, wait, compute):
        for u in range(ring - 1):
            start(u, u)

        @pl.loop(0, n_units, step=ring)
        def _(uu):
            for b in range(ring):
                u = uu + b
                nxt = u + (ring - 1)

                @pl.when(nxt < n_units)
                def _():
                    start(nxt, (b + ring - 1) % ring)

                wait(u, b)
                compute(u, b)

    @functools.partial(
        pl.kernel, mesh=_sc_mesh(),
        out_type=jax.ShapeDtypeStruct((t, d), F32),
        scratch_types=[
            pltpu.VMEM((g * nsel,), I32),
            pltpu.VMEM((g * nsel,), F32),
            pltpu.VMEM((g, d // 2), I32),
            pltpu.VMEM((g, d), F32),
            pltpu.VMEM((SC_LANES * SC_LANES,), F32),
            [row_buf] * ring,
            [pltpu.SemaphoreType.DMA] * ring,
        ],
        compiler_params=_sc_params(),
        name="peer_experts_pk_sc",
    )
    def k(tab_hbm, idx_hbm, w_hbm, h_hbm, out_hbm, idx_v, coef_v, h_v, y_v, red_v, rows, sems):
        wid = _sc_worker_id()
        lane = lax.iota(I32, SC_LANES)

        def copy(u, slot):
            ids = idx_v.at[pl.ds(_sc_unit_off(u), SC_LANES)]
            return pltpu.make_async_copy(tab_hbm.at[ids], rows[slot], sems[slot])

        def dots(u, slot):
            tt = u // heads

            def body(cp, accs):
                out = []
                hv = [plsc.bitcast(h_v[tt, pl.ds(pl.multiple_of((2 * cp + i) * SC_LANES, SC_LANES), SC_LANES)], BF16)
                      for i in range(2)]
                for r in range(SC_LANES):
                    pr = (row_words(rows[slot], r, 2 * cp, 0) * hv[0]
                          + row_words(rows[slot], r, 2 * cp + 1, 0) * hv[1])
                    lo, hi = _unpack_halves(pr)
                    out.append(accs[r] + lo + hi)
                return tuple(out)

            accs = lax.fori_loop(0, chunks // 2, body,
                                 tuple(jnp.zeros((SC_LANES,), F32) for _ in range(SC_LANES)))
            for r in range(SC_LANES):
                red_v[pl.ds(r * SC_LANES, SC_LANES)] = accs[r]
            act = _tree_sum([plsc.load_gather(red_v, [lane * SC_LANES + j]) for j in range(SC_LANES)])
            sl = pl.ds(_sc_unit_off(u), SC_LANES)
            coef_v[sl] = coef_v[sl] * _gelu_tanh(act)

        def combine(u, slot):
            tt = u // heads
            first = (u % heads) == 0
            cb = []
            for r in range(SC_LANES):
                c = plsc.load_gather(coef_v, [jnp.full((SC_LANES,), u * SC_LANES + r, I32)])
                cb.append(plsc.pack(c, c, format=plsc.PackFormat.INTERLEAVED))

            @plsc.parallel_loop(0, chunks, unroll=2)
            def _(wc):
                lo, hi = _unpack_halves(
                    _tree_sum([cb[r] * row_words(rows[slot], r, wc, SC_PK_SUB) for r in range(SC_LANES)]))
                for half, val in ((0, lo), (1, hi)):
                    sl = pl.ds(pl.multiple_of(half * (d // 2) + wc * SC_LANES, SC_LANES), SC_LANES)
                    y_v[tt, sl] = val + jnp.where(first, 0.0, y_v[tt, sl])

        def unit(u, slot):
            dots(u, slot)
            combine(u, slot)

        @pl.loop(0, groups)
        def _(gi):
            base = wid * tpw + gi * g
            pltpu.sync_copy(idx_hbm.at[pl.ds(base * nsel, g * nsel)], idx_v)
            pltpu.sync_copy(w_hbm.at[pl.ds(base * nsel, g * nsel)], coef_v)
            pltpu.sync_copy(h_hbm.at[pl.ds(base, g)], h_v)
            ring_loop(units, lambda u, s: copy(u, s).start(), lambda u, s: copy(u, s).wait(), unit)
            pltpu.sync_copy(y_v, out_hbm.at[pl.ds(base, g)])

    return k(tab_uv, idx_flat, w_flat, hp)


def kernel(x, mem, rel_bias, ln_mix, w_in, hg_lower, hg_norm, w_up_a, w_up_b, w_out, ln_cross, ln_mem, wq_x, wk_x, wv_x, wo_x, ln_ffn, peer_query, peer_subkeys, peer_u, peer_v, ln_final):
    b, s, d = x.shape
    depth = w_in.shape[0]
    assert depth == 1, "the residual after PEER is fused into the final norm"
    assert s % MB_BLOCK == 0 and s % HG_CHUNK == 0 and s % (PEER_SLICES * SC_WORKERS * SC_GROUP) == 0
    nb = s // MB_BLOCK
    row = lambda a: a.reshape(1, -1).astype(F32)
    lb_all = jnp.cumsum(jax.nn.softmax(hg_lower.astype(F32), axis=0), axis=0)
    bias = moba_bias_tiles(rel_bias)
    n_hg = 4 * HG_WIDTH
    n_qk = 2 * MB_WIDTH
    n_mb = 3 * MB_WIDTH
    l = 0
    w = w_in[l].astype(BF16)
    w_hg, w_qk, w_vt, w_g = w[:, :n_hg], w[:, n_hg:n_hg + n_qk], w[:, n_hg + n_qk:n_hg + n_mb].T, w[:, n_hg + n_mb:]
    wa, wb, wo = w_up_a[l].astype(BF16), w_up_b[l].astype(BF16), w_out[l].astype(BF16)
    wqx, wox = wq_x[l].astype(BF16), wo_x[l].astype(BF16)
    wpq, sk = peer_query[l].astype(BF16), peer_subkeys[l].astype(F32)
    tab_uv = pack_expert_tables(peer_u[l].astype(F32), peer_v[l].astype(F32))
    kx, vx = mem_kv(mem, row(ln_mem[l]), wk_x[l].astype(BF16), wv_x[l].astype(BF16))

    outs = []
    for bi in range(b):
        x2d = x[bi]
        p0, pqk, km, vt, pg = in_proj(x2d, row(ln_mix[l]), w_hg, w_qk, w_vt, w_g)
        ya = hgrn2(p0, row(lb_all[l]), row(hg_norm[l]), 1, s)
        km = km.reshape(1, nb, MB_WIDTH)
        ts = s // PEER_SLICES
        for tok0 in range(0, s, ts):
            yb = moba_attention(pqk, vt, km, bias, 1, s, tok0 // MB_BLOCK, ts // MB_BLOCK)
            xs = mix_cross(x2d, ya, yb, pg, wa, wb, wo, row(ln_cross[l]), wqx, kx[bi:bi + 1], vx[bi:bi + 1], wox, tok0)
            hp, eidx, wts = peer_route(xs, row(ln_ffn[l]), wpq, sk, 0, ts)
            y = peer_experts_pk_sc(tab_uv, eidx.reshape(ts * PEER_SEL), wts.reshape(ts * PEER_SEL), hp, d)
            outs.append(final_norm(xs, y, row(ln_final), 0))
    return jnp.concatenate(outs, axis=0).reshape(b, s, d)
```

```python
import functools
import math

import jax
import jax.numpy as jnp
from jax import lax
from jax.experimental import pallas as pl
from jax.experimental.pallas import tpu as pltpu
from jax.experimental.pallas import tpu_sc as plsc

F32 = jnp.float32
BF16 = jnp.bfloat16
I32 = jnp.int32
EPS = 1e-6
NEG_INF = float("-inf")

HG_HEADS = 4
HG_D = 128
HG_WIDTH = HG_HEADS * HG_D
HG_CHUNK = 64
HG_SUB = 16
MB_HEADS = 8
MB_DH = 64
MB_WIDTH = MB_HEADS * MB_DH
MB_BLOCK = 256
MB_TOPK = 3
MB_BIAS_TILES = 8
REL_BUCKETS = 32
REL_MAX_DIST = 2048
X_HEADS = 4
PEER_HEADS = 8
PEER_NKEYS = 128
PEER_TOPK = 16
PEER_HALF = 128
PEER_SEL = PEER_HEADS * PEER_TOPK
PEER_SLICES = 4

VMEM_LIMIT = 56 * 1024 * 1024


def _cparams(sem):
    return pltpu.CompilerParams(dimension_semantics=sem, vmem_limit_bytes=VMEM_LIMIT)


def _rms(x, g):
    ms = jnp.mean(x * x, axis=-1, keepdims=True)
    return x * lax.rsqrt(ms + EPS) * g


def _in_proj_kernel(x_ref, g_ref, w0_ref, w1_ref, wvt_ref, w2_ref, o0_ref, o1_ref, okm_ref, ovt_ref, o2_ref):
    h = _rms(x_ref[...], g_ref[...]).astype(BF16)
    o0_ref[...] = jnp.dot(h, w0_ref[...], preferred_element_type=F32)
    qk = jnp.dot(h, w1_ref[...], preferred_element_type=F32)
    o1_ref[...] = qk.astype(BF16)
    okm_ref[0] = jnp.mean(qk[:, MB_WIDTH:], axis=0, keepdims=True)
    vt = lax.dot_general(wvt_ref[...], h, (((1,), (1,)), ((), ())), preferred_element_type=F32).astype(BF16)
    for hd in range(MB_HEADS):
        ovt_ref[0, hd * MB_VROWS:hd * MB_VROWS + MB_DH, :] = vt[hd * MB_DH:(hd + 1) * MB_DH]
        ovt_ref[0, hd * MB_VROWS + MB_DH:(hd + 1) * MB_VROWS, :] = jnp.ones((MB_ONES, vt.shape[1]), BF16)
    o2_ref[...] = jnp.dot(h, w2_ref[...], preferred_element_type=F32).astype(BF16)


def in_proj(x2d, g, w0, w1, wvt, w2):
    t, d = x2d.shape
    tm = MB_BLOCK
    assert wvt.shape[0] == MB_WIDTH and w1.shape[1] == 2 * MB_WIDTH
    n0, n1, nv, n2 = w0.shape[1], w1.shape[1], MB_VT_ROWS, w2.shape[1]
    full = lambda a: pl.BlockSpec(a.shape, lambda i: (0, 0))
    return pl.pallas_call(
        _in_proj_kernel,
        grid=(t // tm,),
        in_specs=[pl.BlockSpec((tm, d), lambda i: (i, 0)), full(g), full(w0), full(w1), full(wvt), full(w2)],
        out_specs=[pl.BlockSpec((tm, n0), lambda i: (i, 0)),
                   pl.BlockSpec((tm, n1), lambda i: (i, 0)),
                   pl.BlockSpec((1, 1, MB_WIDTH), lambda i: (i, 0, 0)),
                   pl.BlockSpec((1, nv, tm), lambda i: (i, 0, 0)),
                   pl.BlockSpec((tm, n2), lambda i: (i, 0))],
        out_shape=[jax.ShapeDtypeStruct((t, n0), F32),
                   jax.ShapeDtypeStruct((t, n1), BF16),
                   jax.ShapeDtypeStruct((t // tm, 1, MB_WIDTH), F32),
                   jax.ShapeDtypeStruct((t // tm, nv, tm), BF16),
                   jax.ShapeDtypeStruct((t, n2), BF16)],
        compiler_params=_cparams(("parallel",)),
        name="in_proj",
    )(x2d, g, w0, w1, wvt, w2)


def _hgrn_kernel(q_ref, f_ref, i_ref, g_ref, lb_ref, gain_ref, o_ref, st_ref):
    c = pl.program_id(1)

    @pl.when(c == 0)
    def _():
        st_ref[...] = jnp.zeros_like(st_ref)

    C, S = HG_CHUNK, HG_SUB
    row = lax.broadcasted_iota(I32, (C, C), 0)
    col = lax.broadcasted_iota(I32, (C, C), 1)
    tril = (row >= col).astype(F32)
    t_iota = lax.broadcasted_iota(I32, (S, 1), 0)

    for h in range(HG_HEADS):
        sl = slice(h * HG_D, (h + 1) * HG_D)
        q = q_ref[:, sl]
        v = i_ref[:, sl]
        lb = lb_ref[:, sl]
        f = lb + (1.0 - lb) * jax.nn.sigmoid(f_ref[:, sl])
        lf = jnp.log(f)
        k = 1.0 - f
        b = jnp.dot(tril, lf, precision=lax.Precision.HIGHEST, preferred_element_type=F32)
        st = st_ref[h]
        vb = v.astype(BF16)
        qd = (q * jnp.exp(b)).astype(BF16)
        o_inter = lax.dot_general(qd, st.astype(BF16), (((1,), (1,)), ((), ())),
                                  preferred_element_type=F32)
        outs = []
        for i in range(C // S):
            r0 = i * S
            qi = q[r0:r0 + S]
            ki = k[r0:r0 + S]
            bi = b[r0:r0 + S]
            vi = v[r0:r0 + S]
            oi = o_inter[r0:r0 + S]
            if i > 0:
                bs = b[r0 - 1:r0]
                qh = (qi * jnp.exp(bi - bs)).astype(BF16)
                kh = (k[:r0] * jnp.exp(bs - b[:r0])).astype(BF16)
                a = lax.dot_general(qh, kh, (((1,), (1,)), ((), ())), preferred_element_type=F32)
                oi = oi + jnp.dot(a.astype(BF16), vb[:r0], preferred_element_type=F32)
            half = S // 2
            o_half = [oi[:half], oi[half:]]
            for s in range(S):
                for hf in range(s // half, 2):
                    rows = slice(hf * half, (hf + 1) * half)
                    dec = jnp.exp(jnp.minimum(bi[rows] - bi[s:s + 1], 0.0))
                    a_s = jnp.sum(qi[rows] * ki[s:s + 1] * dec, axis=-1, keepdims=True)
                    a_s = jnp.where(t_iota[rows] >= s, a_s, 0.0)
                    o_half[hf] = o_half[hf] + a_s * vi[s:s + 1]
            outs.extend(o_half)
        o = jnp.concatenate(outs, axis=0)
        b_end = b[C - 1:C]
        kd = (k * jnp.exp(b_end - b)).astype(BF16)
        upd = lax.dot_general(vb, kd, (((0,), (0,)), ((), ())), preferred_element_type=F32)
        st_ref[h] = st * jnp.exp(b_end) + upd
        o = o * lax.rsqrt(jnp.mean(o * o, axis=-1, keepdims=True) + EPS)
        g = g_ref[:, sl]
        o_ref[:, sl] = (o * gain_ref[:, sl] * (g * jax.nn.sigmoid(g))).astype(o_ref.dtype)


def hgrn2(p0, lb, gain, batch, seq):
    t = p0.shape[0]
    nc = seq // HG_CHUNK
    w = HG_WIDTH

    def col(j):
        return pl.BlockSpec((HG_CHUNK, w), lambda b, c, j=j: (b * nc + c, j))

    return pl.pallas_call(
        _hgrn_kernel,
        grid=(batch, nc),
        in_specs=[col(0), col(1), col(2), col(3),
                  pl.BlockSpec((1, w), lambda b, c: (0, 0)),
                  pl.BlockSpec((1, w), lambda b, c: (0, 0))],
        out_specs=pl.BlockSpec((HG_CHUNK, w), lambda b, c: (b * nc + c, 0)),
        out_shape=jax.ShapeDtypeStruct((t, w), BF16),
        scratch_shapes=[pltpu.VMEM((HG_HEADS, HG_D, HG_D), F32)],
        compiler_params=_cparams(("parallel", "arbitrary")),
        name="hgrn2",
    )(p0, p0, p0, p0, lb, gain)


MB_PAIR = 4
MB_PW = MB_PAIR * MB_DH
MB_LG = 128
MB_ONES = 16
MB_VROWS = MB_DH + MB_ONES
MB_VT_ROWS = MB_HEADS * MB_VROWS


def _moba_kernel(q_ref, k_ref, vt_ref, km_ref, bias_ref, o_ref, *scratch, qb0):
    m_ref, l_ref, al_ref, acc_ref, msk_ref, s_ref, p_ref = (
        scratch[i * MB_PAIR:(i + 1) * MB_PAIR] for i in range(7))
    qi = pl.program_id(2) + qb0
    nb = km_ref.shape[0]
    blk = MB_BLOCK
    heads = range(MB_PAIR)
    grp = lambda hh: slice((hh // 2) * MB_LG, (hh // 2 + 1) * MB_LG)
    q = q_ref[...]
    lane = lax.broadcasted_iota(I32, (blk, MB_LG), 1)
    in_head = [(lane < MB_DH) if hh % 2 == 0 else (lane >= MB_DH) for hh in heads]
    qs = q * jnp.asarray(MB_DH ** -0.5, BF16)
    nt = (((1,), (1,)), ((), ()))
    qf = q.astype(F32)
    qht = [jnp.where(in_head[hh], qs[:, grp(hh)].astype(F32), 0.0).T.astype(BF16) for hh in heads]

    n_io = lax.broadcasted_iota(I32, (nb, blk), 0)
    for hh in heads:
        gate = lax.dot_general(km_ref[:, grp(hh)], jnp.where(in_head[hh], qf[:, grp(hh)], 0.0), nt,
                               precision=lax.Precision.HIGHEST, preferred_element_type=F32)
        gate = jnp.where(n_io < qi, gate, NEG_INF)
        chosen = n_io < 0
        for _ in range(MB_TOPK):
            mx = jnp.max(gate, axis=0, keepdims=True)
            ix = jnp.min(jnp.where(gate == mx, n_io, nb), axis=0, keepdims=True)
            hit = n_io == ix
            chosen = chosen | (hit & (mx > NEG_INF))
            gate = jnp.where(hit, NEG_INF, gate)
        msk_ref[hh][...] = jnp.where(chosen, 0.0, NEG_INF)

    vrows = lambda hh: slice(hh * MB_VROWS, (hh + 1) * MB_VROWS)

    def pv_stage(blk_idx):
        vtb = vt_ref[blk_idx]
        r = [jnp.dot(vtb[vrows(hh)], p_ref[hh][...], preferred_element_type=F32) for hh in heads]
        al = [al_ref[hh][...] for hh in heads]
        a_new = [al[hh] * acc_ref[hh][...] + r[hh][:MB_DH] for hh in heads]
        l_new = [al[hh] * l_ref[hh][...] + r[hh][MB_DH:MB_DH + 1] for hh in heads]
        return a_new, l_new

    def store_pv(a_new, l_new):
        for hh in heads:
            acc_ref[hh][...] = a_new[hh]
            l_ref[hh][...] = l_new[hh]

    def softmax_stage():
        s = [s_ref[hh][...] for hh in heads]
        m_old = [m_ref[hh][...] for hh in heads]
        m_new = [jnp.maximum(m_old[hh], jnp.max(s[hh], axis=0, keepdims=True)) for hh in heads]
        alpha = [jnp.exp(m_old[hh] - m_new[hh]) for hh in heads]
        p = [jnp.exp((s[hh] - m_new[hh]).astype(BF16)) for hh in heads]
        return p, alpha, m_new

    def store_softmax(p, alpha, m_new):
        for hh in heads:
            p_ref[hh][...] = p[hh]
            al_ref[hh][...] = alpha[hh]
            m_ref[hh][...] = m_new[hh]

    k_own = k_ref[pl.ds(pl.multiple_of(qi * blk, blk), blk), :]
    key_io = lax.broadcasted_iota(I32, (blk, blk), 0)
    qry_io = lax.broadcasted_iota(I32, (blk, blk), 1)
    for hh in heads:
        s = jnp.dot(k_own[:, grp(hh)], qht[hh], preferred_element_type=F32) + bias_ref[hh, 0]
        s_ref[hh][...] = jnp.where(key_io <= qry_io, s, NEG_INF)
        m_ref[hh][...] = jnp.full((1, blk), NEG_INF, F32)
        l_ref[hh][...] = jnp.zeros((1, blk), F32)
        al_ref[hh][...] = jnp.ones((1, blk), F32)
        acc_ref[hh][...] = jnp.zeros((MB_DH, blk), F32)
        p_ref[hh][...] = jnp.zeros((blk, blk), BF16)

    def step(i, carry, far):
        pv = pv_stage(jnp.where(i <= 1, qi, i - 2))
        sm = softmax_stage()
        kn = k_ref[pl.ds(pl.multiple_of(i * blk, blk), blk), :]
        if far:
            row = [msk_ref[hh][pl.ds(i, 1), :] + bias_ref[hh, MB_BIAS_TILES - 1, 0:1, 0:1] for hh in heads]
            s_next = [jnp.dot(kn[:, grp(hh)], qht[hh], preferred_element_type=F32) + row[hh] for hh in heads]
        else:
            d = qi - i
            s_next = [jnp.dot(kn[:, grp(hh)], qht[hh], preferred_element_type=F32)
                      + bias_ref[hh, d] + msk_ref[hh][pl.ds(i, 1), :] for hh in heads]
        store_pv(*pv)
        for hh in heads:
            s_ref[hh][...] = s_next[hh]
        store_softmax(*sm)
        return carry

    n_far = jnp.maximum(qi - (MB_BIAS_TILES - 2), 0)
    lax.fori_loop(0, n_far, functools.partial(step, far=True), 0)
    lax.fori_loop(n_far, qi, functools.partial(step, far=False), 0)
    pv = pv_stage(jnp.where(qi <= 1, qi, qi - 2))
    sm = softmax_stage()
    store_pv(*pv)
    store_softmax(*sm)
    a_fin, l_fin = pv_stage(jnp.where(qi == 0, qi, qi - 1))
    out_t = jnp.concatenate([a_fin[hh] / l_fin[hh] for hh in heads], axis=0)
    o_ref[...] = out_t.T.astype(o_ref.dtype)


def moba_attention(pqk, vt, km, bias, batch, seq, qb0=0, nqb=None):
    nb = seq // MB_BLOCK
    nqb = nb if nqb is None else nqb
    t = batch * nqb * MB_BLOCK
    groups = MB_WIDTH // MB_PW
    return pl.pallas_call(
        functools.partial(_moba_kernel, qb0=qb0),
        grid=(batch, groups, nqb),
        in_specs=[
            pl.BlockSpec((MB_BLOCK, MB_PW), lambda b, j, i: (b * nb + qb0 + i, j)),
            pl.BlockSpec((seq, MB_PW), lambda b, j, i: (b, groups + j)),
            pl.BlockSpec((nb, MB_PAIR * MB_VROWS, MB_BLOCK), lambda b, j, i: (b, j, 0)),
            pl.BlockSpec((None, nb, MB_PW), lambda b, j, i: (b, 0, j)),
            pl.BlockSpec((MB_PAIR, MB_BIAS_TILES, MB_BLOCK, MB_BLOCK), lambda b, j, i: (j, 0, 0, 0)),
        ],
        out_specs=pl.BlockSpec((MB_BLOCK, MB_PW), lambda b, j, i: (b * nqb + i, j)),
        out_shape=jax.ShapeDtypeStruct((t, MB_WIDTH), BF16),
        scratch_shapes=(
            [pltpu.VMEM((1, MB_BLOCK), F32)] * (3 * MB_PAIR)
            + [pltpu.VMEM((MB_DH, MB_BLOCK), F32)] * MB_PAIR
            + [pltpu.VMEM((nb, MB_BLOCK), F32)] * MB_PAIR
            + [pltpu.VMEM((MB_BLOCK, MB_BLOCK), F32)] * MB_PAIR
            + [pltpu.VMEM((MB_BLOCK, MB_BLOCK), BF16)] * MB_PAIR
        ),
        compiler_params=_cparams(("parallel", "parallel", "arbitrary")),
        name="moba_attn",
    )(pqk, pqk, vt, km, bias)


def _t5_bucket(dist):
    max_exact = REL_BUCKETS // 2
    scaled = jnp.log(jnp.maximum(dist, 1).astype(F32) / max_exact) / math.log(REL_MAX_DIST / max_exact)
    large = jnp.minimum(max_exact + (scaled * (REL_BUCKETS - max_exact)).astype(I32), REL_BUCKETS - 1)
    return jnp.where(dist < max_exact, dist, large)


def moba_bias_tiles(rel_bias):
    blk = MB_BLOCK
    span = 2 * blk - 1
    x = jnp.arange(span) - (blk - 1)
    dist = jnp.maximum(jnp.arange(MB_BIAS_TILES)[:, None] * blk + x[None, :], 0)
    w = rel_bias.astype(F32).T[:, _t5_bucket(dist)]
    h = w.shape[0]
    wp = jnp.pad(w, ((0, 0), (0, 0), (0, 1)))
    a = jnp.broadcast_to(wp[:, :, None, :], (h, MB_BIAS_TILES, blk, span + 1))
    a = a.reshape(h, MB_BIAS_TILES, blk * (span + 1))[:, :, :blk * span]
    return a.reshape(h, MB_BIAS_TILES, blk, span)[:, :, :, blk - 1:]


def _mix_kernel(x_ref, ya_ref, yb_ref, ga_ref, gb_ref, wa_ref, wb_ref, wo_ref, o_ref):
    za = jnp.dot(ya_ref[...], wa_ref[...], preferred_element_type=F32)
    zb = jnp.dot(yb_ref[...], wb_ref[...], preferred_element_type=F32)
    z = jax.nn.sigmoid(ga_ref[...].astype(F32)) * za + jax.nn.sigmoid(gb_ref[...].astype(F32)) * zb
    o_ref[...] = x_ref[...] + jnp.dot(z.astype(BF16), wo_ref[...], preferred_element_type=F32)


def _mem_kv_kernel(m_ref, g_ref, wk_ref, wv_ref, k_ref, v_ref):
    mn = _rms(m_ref[...], g_ref[...]).astype(BF16)
    k_ref[...] = jnp.dot(mn, wk_ref[...], preferred_element_type=F32).astype(BF16)
    v_ref[...] = jnp.dot(mn, wv_ref[...], preferred_element_type=F32).astype(BF16)


def mem_kv(mem, g, wk, wv):
    b, m, d = mem.shape
    spec = pl.BlockSpec((None, m, d), lambda i: (i, 0, 0))
    wspec = pl.BlockSpec((d, d), lambda i: (0, 0))
    return pl.pallas_call(
        _mem_kv_kernel,
        grid=(b,),
        in_specs=[spec, pl.BlockSpec((1, d), lambda i: (0, 0)), wspec, wspec],
        out_specs=[spec, spec],
        out_shape=[jax.ShapeDtypeStruct((b, m, d), BF16)] * 2,
        compiler_params=_cparams(("parallel",)),
        name="mem_kv",
    )(mem, g, wk, wv)


def _cross_kernel(x_ref, g_ref, wq_ref, k_ref, v_ref, wo_ref, o_ref):
    x = x_ref[...]
    d = x.shape[1]
    dh = d // X_HEADS
    h = _rms(x, g_ref[...]).astype(BF16)
    q = (jnp.dot(h, wq_ref[...], preferred_element_type=F32) * (dh ** -0.5)).astype(BF16)
    outs = []
    for hh in range(X_HEADS):
        sl = slice(hh * dh, (hh + 1) * dh)
        s = lax.dot_general(q[:, sl], k_ref[:, sl], (((1,), (1,)), ((), ())),
                            preferred_element_type=F32)
        p = jnp.exp(s - jnp.max(s, axis=1, keepdims=True))
        l = jnp.sum(p, axis=1, keepdims=True)
        o = jnp.dot(p.astype(BF16), v_ref[:, sl], preferred_element_type=F32) / l
        outs.append(o.astype(BF16))
    o = jnp.concatenate(outs, axis=1)
    o_ref[...] = x + jnp.dot(o, wo_ref[...], preferred_element_type=F32)


def _mix_cross_kernel(x_ref, ya_ref, yb_ref, ga_ref, gb_ref, wa_ref, wb_ref, wo_ref,
                      g_ref, wq_ref, k_ref, v_ref, wox_ref, o_ref, x1_ref):
    _mix_kernel(x_ref, ya_ref, yb_ref, ga_ref, gb_ref, wa_ref, wb_ref, wo_ref, x1_ref)
    _cross_kernel(x1_ref, g_ref, wq_ref, k_ref, v_ref, wox_ref, o_ref)


def mix_cross(x2d, ya, yb, pg, wa, wb, wo, g, wq, kx, vx, wox, tok0, tm=512):
    t = yb.shape[0]
    assert t % tm == 0 and tok0 % tm == 0
    d = x2d.shape[1]
    w = ya.shape[1]
    m = kx.shape[1]
    b0 = tok0 // tm
    const = lambda a: pl.BlockSpec(a.shape, lambda i: (0,) * a.ndim)
    kv = pl.BlockSpec((None, m, d), lambda i: (0, 0, 0))
    return pl.pallas_call(
        _mix_cross_kernel,
        grid=(t // tm,),
        in_specs=[
            pl.BlockSpec((tm, d), lambda i: (b0 + i, 0)),
            pl.BlockSpec((tm, w), lambda i: (b0 + i, 0)),
            pl.BlockSpec((tm, w), lambda i: (i, 0)),
            pl.BlockSpec((tm, d), lambda i: (b0 + i, 0)),
            pl.BlockSpec((tm, d), lambda i: (b0 + i, 1)),
            const(wa), const(wb), const(wo), const(g), const(wq), kv, kv, const(wox),
        ],
        out_specs=pl.BlockSpec((tm, d), lambda i: (i, 0)),
        out_shape=jax.ShapeDtypeStruct((t, d), F32),
        scratch_shapes=[pltpu.VMEM((tm, d), F32)],
        compiler_params=_cparams(("parallel",)),
        name="mix_cross",
    )(x2d, ya, yb, pg, pg, wa, wb, wo, g, wq, kx, vx, wox)


def _topk_rows(sc, k):
    n = sc.shape[0]
    io = lax.broadcasted_iota(I32, sc.shape, 0).astype(F32)
    vals, ids = [], []
    for _ in range(k):
        m = jnp.max(sc, axis=0, keepdims=True)
        ix = jnp.argmax(sc, axis=0, keepdims=True).astype(F32)
        vals.append(m)
        ids.append(ix)
        sc = jnp.where(io == ix, NEG_INF, sc)
    return jnp.concatenate(vals, axis=0), jnp.concatenate(ids, axis=0).astype(I32)


def _pack_bf16_halves(h):
    bits = lax.bitcast_convert_type(h, I32)
    r = bits + 0x7FFF + (lax.shift_right_logical(bits, 16) & 1)
    half = h.shape[1] // 2
    return lax.shift_right_logical(r[:, :half], 16) | (r[:, half:] & HI_MASK)


def _route_kernel(x_ref, g_ref, wq_ref, sk_ref, hp_ref, idx_ref, w_ref, hb_ref, it_ref, wt_ref):
    p = pl.program_id(1)

    @pl.when(p == 0)
    def _():
        h = _rms(x_ref[...], g_ref[...])
        hp_ref[...] = _pack_bf16_halves(h)
        hb_ref[...] = h.astype(BF16)

    qh = jnp.dot(hb_ref[...], wq_ref[...], preferred_element_type=F32)
    tops = []
    for c in range(2):
        seg = qh[:, c * PEER_HALF:(c + 1) * PEER_HALF]
        sc = lax.dot_general(sk_ref[c], seg, (((1,), (1,)), ((), ())),
                             precision=lax.Precision.HIGHEST, preferred_element_type=F32)
        tops.append(_topk_rows(sc, PEER_TOPK))
    (s0, i0), (s1, i1) = tops
    k = PEER_TOPK
    sub = 8
    tm = s0.shape[1]
    r8 = lax.broadcasted_iota(I32, (sub, tm), 0)
    r16 = lax.broadcasted_iota(I32, (k, tm), 0)
    cand_b = [s0[0:1] + s1, s0[1:2] + s1[:sub]]
    cidx_b = [i0[0:1] * PEER_NKEYS + i1, i0[1:2] * PEER_NKEYS + i1[:sub]]
    pos_b = [r16, k + r8]
    for a in range(2, sub):
        keep = r8 < (k // (a + 1))
        cand_b.append(jnp.where(keep, s0[a:a + 1] + s1[:sub], NEG_INF))
        cidx_b.append(i0[a:a + 1] * PEER_NKEYS + i1[:sub])
        pos_b.append(a * k + r8)
    cand_b.append(s0[sub:] + s1[0:1])
    cidx_b.append(i0[sub:] * PEER_NKEYS + i1[0:1])
    pos_b.append((sub + r8) * k)
    cand = jnp.concatenate(cand_b, axis=0)
    cidx = jnp.concatenate(cidx_b, axis=0)
    pos = jnp.concatenate(pos_b, axis=0).astype(F32)
    vals, ids = [], []
    for _ in range(k):
        m = jnp.max(cand, axis=0, keepdims=True)
        px = jnp.min(jnp.where(cand == m, pos, float(k * k)), axis=0, keepdims=True)
        hit = pos == px
        vals.append(m)
        ids.append(jnp.sum(jnp.where(hit, cidx, 0), axis=0, keepdims=True))
        cand = jnp.where(hit, NEG_INF, cand)
    sf = jnp.concatenate(vals, axis=0)
    e = jnp.exp(sf - sf[0:1])
    rows = pl.ds(pl.multiple_of(p * PEER_TOPK, PEER_TOPK), PEER_TOPK)
    wt_ref[rows, :] = e / jnp.sum(e, axis=0, keepdims=True)
    it_ref[rows, :] = jnp.concatenate(ids, axis=0)

    @pl.when(p == pl.num_programs(1) - 1)
    def _():
        idx_ref[...] = it_ref[...].T
        w_ref[...] = wt_ref[...].T


def peer_route(x2d, g, wq, sk, tok0, t, tm=1024):
    assert t % tm == 0 and tok0 % tm == 0
    d = x2d.shape[1]
    ph = sk.shape[0]
    nsel = ph * PEER_TOPK
    blk0 = tok0 // tm
    return pl.pallas_call(
        _route_kernel,
        grid=(t // tm, ph),
        in_specs=[
            pl.BlockSpec((tm, d), lambda i, p: (blk0 + i, 0)),
            pl.BlockSpec((1, d), lambda i, p: (0, 0)),
            pl.BlockSpec((d, 2 * PEER_HALF), lambda i, p: (0, p)),
            pl.BlockSpec((None, 2, PEER_NKEYS, PEER_HALF), lambda i, p: (p, 0, 0, 0)),
        ],
        out_specs=[
            pl.BlockSpec((tm, d // 2), lambda i, p: (i, 0)),
            pl.BlockSpec((tm, nsel), lambda i, p: (i, 0)),
            pl.BlockSpec((tm, nsel), lambda i, p: (i, 0)),
        ],
        out_shape=[jax.ShapeDtypeStruct((t, d // 2), I32),
                   jax.ShapeDtypeStruct((t, nsel), I32),
                   jax.ShapeDtypeStruct((t, nsel), F32)],
        scratch_shapes=[pltpu.VMEM((tm, d), BF16),
                        pltpu.VMEM((nsel, tm), I32),
                        pltpu.VMEM((nsel, tm), F32)],
        compiler_params=_cparams(("parallel", "arbitrary")),
        name="peer_route",
    )(x2d, g, wq, sk)


def _final_kernel(x_ref, y_ref, g_ref, o_ref):
    o_ref[...] = _rms(x_ref[...] + y_ref[...], g_ref[...])


def final_norm(x2d, y, g, tok0, tm=512):
    t, d = y.shape
    assert t % tm == 0 and tok0 % tm == 0
    blk0 = tok0 // tm
    spec = pl.BlockSpec((tm, d), lambda i: (i, 0))
    return pl.pallas_call(
        _final_kernel, grid=(t // tm,),
        in_specs=[pl.BlockSpec((tm, d), lambda i: (blk0 + i, 0)), spec, pl.BlockSpec((1, d), lambda i: (0, 0))],
        out_specs=spec,
        out_shape=jax.ShapeDtypeStruct((t, d), F32),
        compiler_params=_cparams(("parallel",)), name="final_norm",
    )(x2d, y, g)


SC_CORES = 2
SC_SUBCORES = 16
SC_WORKERS = SC_CORES * SC_SUBCORES
SC_LANES = 16
SC_GROUP = 32


def _sc_mesh():
    return plsc.VectorSubcoreMesh(core_axis_name="c", subcore_axis_name="s")


def _sc_params():
    return pltpu.CompilerParams(needs_layout_passes=False)


def _sc_worker_id():
    return lax.axis_index("s") * SC_CORES + lax.axis_index("c")


SC_ROW_LANE = 128


def _sc_unit_off(u):
    off = u * SC_LANES
    return off if isinstance(off, int) else pl.multiple_of(off, SC_LANES)


GELU_C0 = math.sqrt(2.0 / math.pi)
GELU_C1 = 0.044715


def _gelu_tanh(x):
    z = GELU_C0 * (x + GELU_C1 * (x * x * x))
    th = 1.0 - 2.0 / (jnp.exp(2.0 * z) + 1.0)
    return 0.5 * x * (1.0 + th)


SC_PK_RING = 4
SC_PK_SUB = 4
HI_MASK = -65536


def _pack_tables_kernel(u_ref, v_ref, o_ref):
    for part, ref in enumerate((u_ref, v_ref)):
        words = _pack_bf16_halves(ref[...])
        for sub in range(SC_PK_SUB):
            o_ref[:, part * SC_PK_SUB + sub, :] = words[:, sub * SC_ROW_LANE:(sub + 1) * SC_ROW_LANE]


def pack_expert_tables(u, v, te=512):
    e, d = u.shape
    assert d == 2 * SC_PK_SUB * SC_ROW_LANE
    spec = pl.BlockSpec((te, d), lambda i: (i, 0))
    return pl.pallas_call(
        _pack_tables_kernel, grid=(e // te,), in_specs=[spec, spec],
        out_specs=pl.BlockSpec((te, 2 * SC_PK_SUB, SC_ROW_LANE), lambda i: (i, 0, 0)),
        out_shape=jax.ShapeDtypeStruct((e, 2 * SC_PK_SUB, SC_ROW_LANE), I32),
        compiler_params=_cparams(("parallel",)), name="pack_expert_tables",
    )(u, v)


def _unpack_halves(x32):
    w = plsc.bitcast(x32, I32)
    return plsc.bitcast(w << 16, F32), plsc.bitcast(w & HI_MASK, F32)


def _tree_sum(xs):
    while len(xs) > 1:
        xs = [xs[i] + xs[i + 1] for i in range(0, len(xs), 2)]
    return xs[0]


def peer_experts_pk_sc(tab_uv, idx_flat, w_flat, hp, d):
    t = hp.shape[0]
    nsel = PEER_SEL
    g = SC_GROUP
    assert t % (SC_WORKERS * g) == 0 and d == 2 * SC_PK_SUB * SC_ROW_LANE
    tpw = t // SC_WORKERS
    groups = tpw // g
    heads = nsel // SC_LANES
    chunks = d // 32
    units = g * heads
    ring = SC_PK_RING
    assert units % ring == 0
    row_buf = pltpu.VMEM((SC_LANES, 2 * SC_PK_SUB, SC_ROW_LANE), I32)

    def row_words(rows, r, wc, sub0):
        per = SC_ROW_LANE // SC_LANES
        return plsc.bitcast(
            rows[r, sub0 + wc // per, pl.ds(pl.multiple_of((wc % per) * SC_LANES, SC_LANES), SC_LANES)], BF16)

    def ring_loop(n_units, start, wait, compute):
        for u in range(ring - 1):
            start(u, u)

        @pl.loop(0, n_units, step=ring)
        def _(uu):
            for b in range(ring):
                u = uu + b
                nxt = u + (ring - 1)

                @pl.when(nxt < n_units)
                def _():
                    start(nxt, (b + ring - 1) % ring)

                wait(u, b)
                compute(u, b)

    @functools.partial(
        pl.kernel, mesh=_sc_mesh(),
        out_type=jax.ShapeDtypeStruct((t, d), F32),
        scratch_types=[
            pltpu.VMEM((g * nsel,), I32),
            pltpu.VMEM((g * nsel,), F32),
            pltpu.VMEM((g, d // 2), I32),
            pltpu.VMEM((g, d), F32),
            pltpu.VMEM((SC_LANES * SC_LANES,), F32),
            [row_buf] * ring,
            [pltpu.SemaphoreType.DMA] * ring,
        ],
        compiler_params=_sc_params(),
        name="peer_experts_pk_sc",
    )
    def k(tab_hbm, idx_hbm, w_hbm, h_hbm, out_hbm, idx_v, coef_v, h_v, y_v, red_v, rows, sems):
        wid = _sc_worker_id()
        lane = lax.iota(I32, SC_LANES)

        def copy(u, slot):
            ids = idx_v.at[pl.ds(_sc_unit_off(u), SC_LANES)]
            return pltpu.make_async_copy(tab_hbm.at[ids], rows[slot], sems[slot])

        def dots(u, slot):
            tt = u // heads

            def body(cp, accs):
                out = []
                hv = [plsc.bitcast(h_v[tt, pl.ds(pl.multiple_of((2 * cp + i) * SC_LANES, SC_LANES), SC_LANES)], BF16)
                      for i in range(2)]
                for r in range(SC_LANES):
                    pr = (row_words(rows[slot], r, 2 * cp, 0) * hv[0]
                          + row_words(rows[slot], r, 2 * cp + 1, 0) * hv[1])
                    lo, hi = _unpack_halves(pr)
                    out.append(accs[r] + lo + hi)
                return tuple(out)

            accs = lax.fori_loop(0, chunks // 2, body,
                                 tuple(jnp.zeros((SC_LANES,), F32) for _ in range(SC_LANES)))
            for r in range(SC_LANES):
                red_v[pl.ds(r * SC_LANES, SC_LANES)] = accs[r]
            act = _tree_sum([plsc.load_gather(red_v, [lane * SC_LANES + j]) for j in range(SC_LANES)])
            sl = pl.ds(_sc_unit_off(u), SC_LANES)
            coef_v[sl] = coef_v[sl] * _gelu_tanh(act)

        def combine(u, slot):
            tt = u // heads
            first = (u % heads) == 0
            cb = []
            for r in range(SC_LANES):
                c = plsc.load_gather(coef_v, [jnp.full((SC_LANES,), u * SC_LANES + r, I32)])
                cb.append(plsc.pack(c, c, format=plsc.PackFormat.INTERLEAVED))

            @plsc.parallel_loop(0, chunks, unroll=2)
            def _(wc):
                lo, hi = _unpack_halves(
                    _tree_sum([cb[r] * row_words(rows[slot], r, wc, SC_PK_SUB) for r in range(SC_LANES)]))
                for half, val in ((0, lo), (1, hi)):
                    sl = pl.ds(pl.multiple_of(half * (d // 2) + wc * SC_LANES, SC_LANES), SC_LANES)
                    y_v[tt, sl] = val + jnp.where(first, 0.0, y_v[tt, sl])

        def unit(u, slot):
            dots(u, slot)
            combine(u, slot)

        @pl.loop(0, groups)
        def _(gi):
            base = wid * tpw + gi * g
            pltpu.sync_copy(idx_hbm.at[pl.ds(base * nsel, g * nsel)], idx_v)
            pltpu.sync_copy(w_hbm.at[pl.ds(base * nsel, g * nsel)], coef_v)
            pltpu.sync_copy(h_hbm.at[pl.ds(base, g)], h_v)
            ring_loop(units, lambda u, s: copy(u, s).start(), lambda u, s: copy(u, s).wait(), unit)
            pltpu.sync_copy(y_v, out_hbm.at[pl.ds(base, g)])

    return k(tab_uv, idx_flat, w_flat, hp)


def kernel(x, mem, rel_bias, ln_mix, w_in, hg_lower, hg_norm, w_up_a, w_up_b, w_out, ln_cross, ln_mem, wq_x, wk_x, wv_x, wo_x, ln_ffn, peer_query, peer_subkeys, peer_u, peer_v, ln_final):
    b, s, d = x.shape
    depth = w_in.shape[0]
    assert depth == 1, "the residual after PEER is fused into the final norm"
    assert s % MB_BLOCK == 0 and s % HG_CHUNK == 0 and s % (PEER_SLICES * SC_WORKERS * SC_GROUP) == 0
    nb = s // MB_BLOCK
    row = lambda a: a.reshape(1, -1).astype(F32)
    lb_all = jnp.cumsum(jax.nn.softmax(hg_lower.astype(F32), axis=0), axis=0)
    bias = moba_bias_tiles(rel_bias)
    n_hg = 4 * HG_WIDTH
    n_qk = 2 * MB_WIDTH
    n_mb = 3 * MB_WIDTH
    l = 0
    w = w_in[l].astype(BF16)
    w_hg, w_qk, w_vt, w_g = w[:, :n_hg], w[:, n_hg:n_hg + n_qk], w[:, n_hg + n_qk:n_hg + n_mb].T, w[:, n_hg + n_mb:]
    wa, wb, wo = w_up_a[l].astype(BF16), w_up_b[l].astype(BF16), w_out[l].astype(BF16)
    wqx, wox = wq_x[l].astype(BF16), wo_x[l].astype(BF16)
    wpq, sk = peer_query[l].astype(BF16), peer_subkeys[l].astype(F32)
    tab_uv = pack_expert_tables(peer_u[l].astype(F32), peer_v[l].astype(F32))
    kx, vx = mem_kv(mem, row(ln_mem[l]), wk_x[l].astype(BF16), wv_x[l].astype(BF16))

    outs = []
    for bi in range(b):
        x2d = x[bi]
        p0, pqk, km, vt, pg = in_proj(x2d, row(ln_mix[l]), w_hg, w_qk, w_vt, w_g)
        ya = hgrn2(p0, row(lb_all[l]), row(hg_norm[l]), 1, s)
        km = km.reshape(1, nb, MB_WIDTH)
        sizes = [s // 8, s // 8, s // 4, s // 2] if bi == b - 1 else [s // 2, s // 2]
        tok0 = 0
        for ts in sizes:
            yb = moba_attention(pqk, vt, km, bias, 1, s, tok0 // MB_BLOCK, ts // MB_BLOCK)
            xs = mix_cross(x2d, ya, yb, pg, wa, wb, wo, row(ln_cross[l]), wqx, kx[bi:bi + 1], vx[bi:bi + 1], wox, tok0)
            hp, eidx, wts = peer_route(xs, row(ln_ffn[l]), wpq, sk, 0, ts)
            y = peer_experts_pk_sc(tab_uv, eidx.reshape(ts * PEER_SEL), wts.reshape(ts * PEER_SEL), hp, d)
            outs.append(final_norm(xs, y, row(ln_final), 0))
            tok0 += ts
    return jnp.concatenate(outs, axis=0).reshape(b, s, d)
```

```python
import functools
import math

import jax
import jax.numpy as jnp
from jax import lax
from jax.experimental import pallas as pl
from jax.experimental.pallas import tpu as pltpu
from jax.experimental.pallas import tpu_sc as plsc

F32 = jnp.float32
BF16 = jnp.bfloat16
I32 = jnp.int32
EPS = 1e-6
NEG_INF = float("-inf")

HG_HEADS = 4
HG_D = 128
HG_WIDTH = HG_HEADS * HG_D
HG_CHUNK = 64
HG_SUB = 16
MB_HEADS = 8
MB_DH = 64
MB_WIDTH = MB_HEADS * MB_DH
MB_BLOCK = 256
MB_TOPK = 3
MB_BIAS_TILES = 8
REL_BUCKETS = 32
REL_MAX_DIST = 2048
X_HEADS = 4
PEER_HEADS = 8
PEER_NKEYS = 128
PEER_TOPK = 16
PEER_HALF = 128
PEER_SEL = PEER_HEADS * PEER_TOPK
PEER_SLICES = 4

VMEM_LIMIT = 56 * 1024 * 1024


def _cparams(sem):
    return pltpu.CompilerParams(dimension_semantics=sem, vmem_limit_bytes=VMEM_LIMIT)


def _rms(x, g):
    ms = jnp.mean(x * x, axis=-1, keepdims=True)
    return x * lax.rsqrt(ms + EPS) * g


def _in_proj_kernel(x_ref, g_ref, w0_ref, w1_ref, wvt_ref, w2_ref, o0_ref, o1_ref, okm_ref, ovt_ref, o2_ref):
    h = _rms(x_ref[...], g_ref[...]).astype(BF16)
    o0_ref[...] = jnp.dot(h, w0_ref[...], preferred_element_type=F32)
    qk = jnp.dot(h, w1_ref[...], preferred_element_type=F32)
    o1_ref[...] = qk.astype(BF16)
    okm_ref[0] = jnp.mean(qk[:, MB_WIDTH:], axis=0, keepdims=True)
    vt = lax.dot_general(wvt_ref[...], h, (((1,), (1,)), ((), ())), preferred_element_type=F32).astype(BF16)
    for hd in range(MB_HEADS):
        ovt_ref[0, hd * MB_VROWS:hd * MB_VROWS + MB_DH, :] = vt[hd * MB_DH:(hd + 1) * MB_DH]
        ovt_ref[0, hd * MB_VROWS + MB_DH:(hd + 1) * MB_VROWS, :] = jnp.ones((MB_ONES, vt.shape[1]), BF16)
    o2_ref[...] = jnp.dot(h, w2_ref[...], preferred_element_type=F32).astype(BF16)


def in_proj(x2d, g, w0, w1, wvt, w2):
    t, d = x2d.shape
    tm = MB_BLOCK
    assert wvt.shape[0] == MB_WIDTH and w1.shape[1] == 2 * MB_WIDTH
    n0, n1, nv, n2 = w0.shape[1], w1.shape[1], MB_VT_ROWS, w2.shape[1]
    full = lambda a: pl.BlockSpec(a.shape, lambda i: (0, 0))
    return pl.pallas_call(
        _in_proj_kernel,
        grid=(t // tm,),
        in_specs=[pl.BlockSpec((tm, d), lambda i: (i, 0)), full(g), full(w0), full(w1), full(wvt), full(w2)],
        out_specs=[pl.BlockSpec((tm, n0), lambda i: (i, 0)),
                   pl.BlockSpec((tm, n1), lambda i: (i, 0)),
                   pl.BlockSpec((1, 1, MB_WIDTH), lambda i: (i, 0, 0)),
                   pl.BlockSpec((1, nv, tm), lambda i: (i, 0, 0)),
                   pl.BlockSpec((tm, n2), lambda i: (i, 0))],
        out_shape=[jax.ShapeDtypeStruct((t, n0), F32),
                   jax.ShapeDtypeStruct((t, n1), BF16),
                   jax.ShapeDtypeStruct((t // tm, 1, MB_WIDTH), F32),
                   jax.ShapeDtypeStruct((t // tm, nv, tm), BF16),
                   jax.ShapeDtypeStruct((t, n2), BF16)],
        compiler_params=_cparams(("parallel",)),
        name="in_proj",
    )(x2d, g, w0, w1, wvt, w2)


def _hgrn_kernel(q_ref, f_ref, i_ref, g_ref, lb_ref, gain_ref, o_ref, st_ref):
    c = pl.program_id(1)

    @pl.when(c == 0)
    def _():
        st_ref[...] = jnp.zeros_like(st_ref)

    C, S = HG_CHUNK, HG_SUB
    row = lax.broadcasted_iota(I32, (C, C), 0)
    col = lax.broadcasted_iota(I32, (C, C), 1)
    tril = (row >= col).astype(F32)
    t_iota = lax.broadcasted_iota(I32, (S, 1), 0)

    for h in range(HG_HEADS):
        sl = slice(h * HG_D, (h + 1) * HG_D)
        q = q_ref[:, sl]
        v = i_ref[:, sl]
        lb = lb_ref[:, sl]
        f = lb + (1.0 - lb) * jax.nn.sigmoid(f_ref[:, sl])
        lf = jnp.log(f)
        k = 1.0 - f
        b = jnp.dot(tril, lf, precision=lax.Precision.HIGHEST, preferred_element_type=F32)
        st = st_ref[h]
        vb = v.astype(BF16)
        qd = (q * jnp.exp(b)).astype(BF16)
        o_inter = lax.dot_general(qd, st.astype(BF16), (((1,), (1,)), ((), ())),
                                  preferred_element_type=F32)
        outs = []
        for i in range(C // S):
            r0 = i * S
            qi = q[r0:r0 + S]
            ki = k[r0:r0 + S]
            bi = b[r0:r0 + S]
            vi = v[r0:r0 + S]
            oi = o_inter[r0:r0 + S]
            if i > 0:
                bs = b[r0 - 1:r0]
                qh = (qi * jnp.exp(bi - bs)).astype(BF16)
                kh = (k[:r0] * jnp.exp(bs - b[:r0])).astype(BF16)
                a = lax.dot_general(qh, kh, (((1,), (1,)), ((), ())), preferred_element_type=F32)
                oi = oi + jnp.dot(a.astype(BF16), vb[:r0], preferred_element_type=F32)
            half = S // 2
            o_half = [oi[:half], oi[half:]]
            for s in range(S):
                for hf in range(s // half, 2):
                    rows = slice(hf * half, (hf + 1) * half)
                    dec = jnp.exp(jnp.minimum(bi[rows] - bi[s:s + 1], 0.0))
                    a_s = jnp.sum(qi[rows] * ki[s:s + 1] * dec, axis=-1, keepdims=True)
                    a_s = jnp.where(t_iota[rows] >= s, a_s, 0.0)
                    o_half[hf] = o_half[hf] + a_s * vi[s:s + 1]
            outs.extend(o_half)
        o = jnp.concatenate(outs, axis=0)
        b_end = b[C - 1:C]
        kd = (k * jnp.exp(b_end - b)).astype(BF16)
        upd = lax.dot_general(vb, kd, (((0,), (0,)), ((), ())), preferred_element_type=F32)
        st_ref[h] = st * jnp.exp(b_end) + upd
        o = o * lax.rsqrt(jnp.mean(o * o, axis=-1, keepdims=True) + EPS)
        g = g_ref[:, sl]
        o_ref[:, sl] = (o * gain_ref[:, sl] * (g * jax.nn.sigmoid(g))).astype(o_ref.dtype)


def hgrn2(p0, lb, gain, batch, seq):
    t = p0.shape[0]
    nc = seq // HG_CHUNK
    w = HG_WIDTH

    def col(j):
        return pl.BlockSpec((HG_CHUNK, w), lambda b, c, j=j: (b * nc + c, j))

    return pl.pallas_call(
        _hgrn_kernel,
        grid=(batch, nc),
        in_specs=[col(0), col(1), col(2), col(3),
                  pl.BlockSpec((1, w), lambda b, c: (0, 0)),
                  pl.BlockSpec((1, w), lambda b, c: (0, 0))],
        out_specs=pl.BlockSpec((HG_CHUNK, w), lambda b, c: (b * nc + c, 0)),
        out_shape=jax.ShapeDtypeStruct((t, w), BF16),
        scratch_shapes=[pltpu.VMEM((HG_HEADS, HG_D, HG_D), F32)],
        compiler_params=_cparams(("parallel", "arbitrary")),
        name="hgrn2",
    )(p0, p0, p0, p0, lb, gain)


MB_PAIR = 4
MB_PW = MB_PAIR * MB_DH
MB_LG = 128
MB_ONES = 16
MB_VROWS = MB_DH + MB_ONES
MB_VT_ROWS = MB_HEADS * MB_VROWS


def _moba_kernel(q_ref, k_ref, vt_ref, km_ref, bias_ref, o_ref, *scratch, qb0):
    m_ref, l_ref, al_ref, acc_ref, msk_ref, s_ref, p_ref = (
        scratch[i * MB_PAIR:(i + 1) * MB_PAIR] for i in range(7))
    qi = pl.program_id(2) + qb0
    nb = km_ref.shape[0]
    blk = MB_BLOCK
    heads = range(MB_PAIR)
    grp = lambda hh: slice((hh // 2) * MB_LG, (hh // 2 + 1) * MB_LG)
    q = q_ref[...]
    lane = lax.broadcasted_iota(I32, (blk, MB_LG), 1)
    in_head = [(lane < MB_DH) if hh % 2 == 0 else (lane >= MB_DH) for hh in heads]
    qs = q * jnp.asarray(MB_DH ** -0.5, BF16)
    nt = (((1,), (1,)), ((), ()))
    qf = q.astype(F32)
    qht = [jnp.where(in_head[hh], qs[:, grp(hh)].astype(F32), 0.0).T.astype(BF16) for hh in heads]

    n_io = lax.broadcasted_iota(I32, (nb, blk), 0)
    for hh in heads:
        gate = lax.dot_general(km_ref[:, grp(hh)], jnp.where(in_head[hh], qf[:, grp(hh)], 0.0), nt,
                               precision=lax.Precision.HIGHEST, preferred_element_type=F32)
        gate = jnp.where(n_io < qi, gate, NEG_INF)
        chosen = n_io < 0
        for _ in range(MB_TOPK):
            mx = jnp.max(gate, axis=0, keepdims=True)
            ix = jnp.min(jnp.where(gate == mx, n_io, nb), axis=0, keepdims=True)
            hit = n_io == ix
            chosen = chosen | (hit & (mx > NEG_INF))
            gate = jnp.where(hit, NEG_INF, gate)
        msk_ref[hh][...] = jnp.where(chosen, 0.0, NEG_INF)

    vrows = lambda hh: slice(hh * MB_VROWS, (hh + 1) * MB_VROWS)

    def pv_stage(blk_idx):
        vtb = vt_ref[blk_idx]
        r = [jnp.dot(vtb[vrows(hh)], p_ref[hh][...], preferred_element_type=F32) for hh in heads]
        al = [al_ref[hh][...] for hh in heads]
        a_new = [al[hh] * acc_ref[hh][...] + r[hh][:MB_DH] for hh in heads]
        l_new = [al[hh] * l_ref[hh][...] + r[hh][MB_DH:MB_DH + 1] for hh in heads]
        return a_new, l_new

    def store_pv(a_new, l_new):
        for hh in heads:
            acc_ref[hh][...] = a_new[hh]
            l_ref[hh][...] = l_new[hh]

    def softmax_stage():
        s = [s_ref[hh][...] for hh in heads]
        m_old = [m_ref[hh][...] for hh in heads]
        m_new = [jnp.maximum(m_old[hh], jnp.max(s[hh], axis=0, keepdims=True)) for hh in heads]
        alpha = [jnp.exp(m_old[hh] - m_new[hh]) for hh in heads]
        p = [jnp.exp((s[hh] - m_new[hh]).astype(BF16)) for hh in heads]
        return p, alpha, m_new

    def store_softmax(p, alpha, m_new):
        for hh in heads:
            p_ref[hh][...] = p[hh]
            al_ref[hh][...] = alpha[hh]
            m_ref[hh][...] = m_new[hh]

    k_own = k_ref[pl.ds(pl.multiple_of(qi * blk, blk), blk), :]
    key_io = lax.broadcasted_iota(I32, (blk, blk), 0)
    qry_io = lax.broadcasted_iota(I32, (blk, blk), 1)
    for hh in heads:
        s = jnp.dot(k_own[:, grp(hh)], qht[hh], preferred_element_type=F32) + bias_ref[hh, 0]
        s_ref[hh][...] = jnp.where(key_io <= qry_io, s, NEG_INF)
        m_ref[hh][...] = jnp.full((1, blk), NEG_INF, F32)
        l_ref[hh][...] = jnp.zeros((1, blk), F32)
        al_ref[hh][...] = jnp.ones((1, blk), F32)
        acc_ref[hh][...] = jnp.zeros((MB_DH, blk), F32)
        p_ref[hh][...] = jnp.zeros((blk, blk), BF16)

    def step(i, carry, far):
        pv = pv_stage(jnp.where(i <= 1, qi, i - 2))
        sm = softmax_stage()
        kn = k_ref[pl.ds(pl.multiple_of(i * blk, blk), blk), :]
        if far:
            row = [msk_ref[hh][pl.ds(i, 1), :] + bias_ref[hh, MB_BIAS_TILES - 1, 0:1, 0:1] for hh in heads]
            s_next = [jnp.dot(kn[:, grp(hh)], qht[hh], preferred_element_type=F32) + row[hh] for hh in heads]
        else:
            d = qi - i
            s_next = [jnp.dot(kn[:, grp(hh)], qht[hh], preferred_element_type=F32)
                      + bias_ref[hh, d] + msk_ref[hh][pl.ds(i, 1), :] for hh in heads]
        store_pv(*pv)
        for hh in heads:
            s_ref[hh][...] = s_next[hh]
        store_softmax(*sm)
        return carry

    n_far = jnp.maximum(qi - (MB_BIAS_TILES - 2), 0)
    lax.fori_loop(0, n_far, functools.partial(step, far=True), 0)
    lax.fori_loop(n_far, qi, functools.partial(step, far=False), 0)
    pv = pv_stage(jnp.where(qi <= 1, qi, qi - 2))
    sm = softmax_stage()
    store_pv(*pv)
    store_softmax(*sm)
    a_fin, l_fin = pv_stage(jnp.where(qi == 0, qi, qi - 1))
    out_t = jnp.concatenate([a_fin[hh] / l_fin[hh] for hh in heads], axis=0)
    o_ref[...] = out_t.T.astype(o_ref.dtype)


def moba_attention(pqk, vt, km, bias, batch, seq, qb0=0, nqb=None):
    nb = seq // MB_BLOCK
    nqb = nb if nqb is None else nqb
    t = batch * nqb * MB_BLOCK
    groups = MB_WIDTH // MB_PW
    return pl.pallas_call(
        functools.partial(_moba_kernel, qb0=qb0),
        grid=(batch, groups, nqb),
        in_specs=[
            pl.BlockSpec((MB_BLOCK, MB_PW), lambda b, j, i: (b * nb + qb0 + i, j)),
            pl.BlockSpec((seq, MB_PW), lambda b, j, i: (b, groups + j)),
            pl.BlockSpec((nb, MB_PAIR * MB_VROWS, MB_BLOCK), lambda b, j, i: (b, j, 0)),
            pl.BlockSpec((None, nb, MB_PW), lambda b, j, i: (b, 0, j)),
            pl.BlockSpec((MB_PAIR, MB_BIAS_TILES, MB_BLOCK, MB_BLOCK), lambda b, j, i: (j, 0, 0, 0)),
        ],
        out_specs=pl.BlockSpec((MB_BLOCK, MB_PW), lambda b, j, i: (b * nqb + i, j)),
        out_shape=jax.ShapeDtypeStruct((t, MB_WIDTH), BF16),
        scratch_shapes=(
            [pltpu.VMEM((1, MB_BLOCK), F32)] * (3 * MB_PAIR)
            + [pltpu.VMEM((MB_DH, MB_BLOCK), F32)] * MB_PAIR
            + [pltpu.VMEM((nb, MB_BLOCK), F32)] * MB_PAIR
            + [pltpu.VMEM((MB_BLOCK, MB_BLOCK), F32)] * MB_PAIR
            + [pltpu.VMEM((MB_BLOCK, MB_BLOCK), BF16)] * MB_PAIR
        ),
        compiler_params=_cparams(("parallel", "parallel", "arbitrary")),
        name="moba_attn",
    )(pqk, pqk, vt, km, bias)


def _t5_bucket(dist):
    max_exact = REL_BUCKETS // 2
    scaled = jnp.log(jnp.maximum(dist, 1).astype(F32) / max_exact) / math.log(REL_MAX_DIST / max_exact)
    large = jnp.minimum(max_exact + (scaled * (REL_BUCKETS - max_exact)).astype(I32), REL_BUCKETS - 1)
    return jnp.where(dist < max_exact, dist, large)


def moba_bias_tiles(rel_bias):
    blk = MB_BLOCK
    span = 2 * blk - 1
    x = jnp.arange(span) - (blk - 1)
    dist = jnp.maximum(jnp.arange(MB_BIAS_TILES)[:, None] * blk + x[None, :], 0)
    w = rel_bias.astype(F32).T[:, _t5_bucket(dist)]
    h = w.shape[0]
    wp = jnp.pad(w, ((0, 0), (0, 0), (0, 1)))
    a = jnp.broadcast_to(wp[:, :, None, :], (h, MB_BIAS_TILES, blk, span + 1))
    a = a.reshape(h, MB_BIAS_TILES, blk * (span + 1))[:, :, :blk * span]
    return a.reshape(h, MB_BIAS_TILES, blk, span)[:, :, :, blk - 1:]


def _mix_kernel(x_ref, ya_ref, yb_ref, ga_ref, gb_ref, wa_ref, wb_ref, wo_ref, o_ref):
    za = jnp.dot(ya_ref[...], wa_ref[...], preferred_element_type=F32)
    zb = jnp.dot(yb_ref[...], wb_ref[...], preferred_element_type=F32)
    z = jax.nn.sigmoid(ga_ref[...].astype(F32)) * za + jax.nn.sigmoid(gb_ref[...].astype(F32)) * zb
    o_ref[...] = x_ref[...] + jnp.dot(z.astype(BF16), wo_ref[...], preferred_element_type=F32)


def _mem_kv_kernel(m_ref, g_ref, wk_ref, wv_ref, k_ref, v_ref):
    mn = _rms(m_ref[...], g_ref[...]).astype(BF16)
    k_ref[...] = jnp.dot(mn, wk_ref[...], preferred_element_type=F32).astype(BF16)
    v_ref[...] = jnp.dot(mn, wv_ref[...], preferred_element_type=F32).astype(BF16)


def mem_kv(mem, g, wk, wv):
    b, m, d = mem.shape
    spec = pl.BlockSpec((None, m, d), lambda i: (i, 0, 0))
    wspec = pl.BlockSpec((d, d), lambda i: (0, 0))
    return pl.pallas_call(
        _mem_kv_kernel,
        grid=(b,),
        in_specs=[spec, pl.BlockSpec((1, d), lambda i: (0, 0)), wspec, wspec],
        out_specs=[spec, spec],
        out_shape=[jax.ShapeDtypeStruct((b, m, d), BF16)] * 2,
        compiler_params=_cparams(("parallel",)),
        name="mem_kv",
    )(mem, g, wk, wv)


def _cross_kernel(x_ref, g_ref, wq_ref, k_ref, v_ref, wo_ref, o_ref):
    x = x_ref[...]
    d = x.shape[1]
    dh = d // X_HEADS
    h = _rms(x, g_ref[...]).astype(BF16)
    q = (jnp.dot(h, wq_ref[...], preferred_element_type=F32) * (dh ** -0.5)).astype(BF16)
    outs = []
    for hh in range(X_HEADS):
        sl = slice(hh * dh, (hh + 1) * dh)
        s = lax.dot_general(q[:, sl], k_ref[:, sl], (((1,), (1,)), ((), ())),
                            preferred_element_type=F32)
        p = jnp.exp(s - jnp.max(s, axis=1, keepdims=True))
        l = jnp.sum(p, axis=1, keepdims=True)
        o = jnp.dot(p.astype(BF16), v_ref[:, sl], preferred_element_type=F32) / l
        outs.append(o.astype(BF16))
    o = jnp.concatenate(outs, axis=1)
    o_ref[...] = x + jnp.dot(o, wo_ref[...], preferred_element_type=F32)


def _mix_cross_kernel(x_ref, ya_ref, yb_ref, ga_ref, gb_ref, wa_ref, wb_ref, wo_ref,
                      g_ref, wq_ref, k_ref, v_ref, wox_ref, o_ref, x1_ref):
    _mix_kernel(x_ref, ya_ref, yb_ref, ga_ref, gb_ref, wa_ref, wb_ref, wo_ref, x1_ref)
    _cross_kernel(x1_ref, g_ref, wq_ref, k_ref, v_ref, wox_ref, o_ref)


def mix_cross(x2d, ya, yb, pg, wa, wb, wo, g, wq, kx, vx, wox, tok0, tm=512):
    t = yb.shape[0]
    assert t % tm == 0 and tok0 % tm == 0
    d = x2d.shape[1]
    w = ya.shape[1]
    m = kx.shape[1]
    b0 = tok0 // tm
    const = lambda a: pl.BlockSpec(a.shape, lambda i: (0,) * a.ndim)
    kv = pl.BlockSpec((None, m, d), lambda i: (0, 0, 0))
    return pl.pallas_call(
        _mix_cross_kernel,
        grid=(t // tm,),
        in_specs=[
            pl.BlockSpec((tm, d), lambda i: (b0 + i, 0)),
            pl.BlockSpec((tm, w), lambda i: (b0 + i, 0)),
            pl.BlockSpec((tm, w), lambda i: (i, 0)),
            pl.BlockSpec((tm, d), lambda i: (b0 + i, 0)),
            pl.BlockSpec((tm, d), lambda i: (b0 + i, 1)),
            const(wa), const(wb), const(wo), const(g), const(wq), kv, kv, const(wox),
        ],
        out_specs=pl.BlockSpec((tm, d), lambda i: (i, 0)),
        out_shape=jax.ShapeDtypeStruct((t, d), F32),
        scratch_shapes=[pltpu.VMEM((tm, d), F32)],
        compiler_params=_cparams(("parallel",)),
        name="mix_cross",
    )(x2d, ya, yb, pg, pg, wa, wb, wo, g, wq, kx, vx, wox)


def _topk_rows(sc, k):
    n = sc.shape[0]
    io = lax.broadcasted_iota(I32, sc.shape, 0).astype(F32)
    vals, ids = [], []
    for _ in range(k):
        m = jnp.max(sc, axis=0, keepdims=True)
        ix = jnp.argmax(sc, axis=0, keepdims=True).astype(F32)
        vals.append(m)
        ids.append(ix)
        sc = jnp.where(io == ix, NEG_INF, sc)
    return jnp.concatenate(vals, axis=0), jnp.concatenate(ids, axis=0).astype(I32)


def _pack_bf16_halves(h):
    bits = lax.bitcast_convert_type(h, I32)
    r = bits + 0x7FFF + (lax.shift_right_logical(bits, 16) & 1)
    half = h.shape[1] // 2
    return lax.shift_right_logical(r[:, :half], 16) | (r[:, half:] & HI_MASK)


def _route_kernel(x_ref, g_ref, wq_ref, sk_ref, hp_ref, idx_ref, w_ref, hb_ref, it_ref, wt_ref):
    p = pl.program_id(1)

    @pl.when(p == 0)
    def _():
        h = _rms(x_ref[...], g_ref[...])
        hp_ref[...] = _pack_bf16_halves(h)
        hb_ref[...] = h.astype(BF16)

    qh = jnp.dot(hb_ref[...], wq_ref[...], preferred_element_type=F32)
    tops = []
    for c in range(2):
        seg = qh[:, c * PEER_HALF:(c + 1) * PEER_HALF]
        sc = lax.dot_general(sk_ref[c], seg, (((1,), (1,)), ((), ())),
                             precision=lax.Precision.HIGHEST, preferred_element_type=F32)
        tops.append(_topk_rows(sc, PEER_TOPK))
    (s0, i0), (s1, i1) = tops
    k = PEER_TOPK
    sub = 8
    tm = s0.shape[1]
    r8 = lax.broadcasted_iota(I32, (sub, tm), 0)
    r16 = lax.broadcasted_iota(I32, (k, tm), 0)
    cand_b = [s0[0:1] + s1, s0[1:2] + s1[:sub]]
    cidx_b = [i0[0:1] * PEER_NKEYS + i1, i0[1:2] * PEER_NKEYS + i1[:sub]]
    pos_b = [r16, k + r8]
    for a in range(2, sub):
        keep = r8 < (k // (a + 1))
        cand_b.append(jnp.where(keep, s0[a:a + 1] + s1[:sub], NEG_INF))
        cidx_b.append(i0[a:a + 1] * PEER_NKEYS + i1[:sub])
        pos_b.append(a * k + r8)
    cand_b.append(s0[sub:] + s1[0:1])
    cidx_b.append(i0[sub:] * PEER_NKEYS + i1[0:1])
    pos_b.append((sub + r8) * k)
    cand = jnp.concatenate(cand_b, axis=0)
    cidx = jnp.concatenate(cidx_b, axis=0)
    pos = jnp.concatenate(pos_b, axis=0).astype(F32)
    vals, ids = [], []
    for _ in range(k):
        m = jnp.max(cand, axis=0, keepdims=True)
        px = jnp.min(jnp.where(cand == m, pos, float(k * k)), axis=0, keepdims=True)
        hit = pos == px
        vals.append(m)
        ids.append(jnp.sum(jnp.where(hit, cidx, 0), axis=0, keepdims=True))
        cand = jnp.where(hit, NEG_INF, cand)
    sf = jnp.concatenate(vals, axis=0)
    e = jnp.exp(sf - sf[0:1])
    rows = pl.ds(pl.multiple_of(p * PEER_TOPK, PEER_TOPK), PEER_TOPK)
    wt_ref[rows, :] = e / jnp.sum(e, axis=0, keepdims=True)
    it_ref[rows, :] = jnp.concatenate(ids, axis=0)

    @pl.when(p == pl.num_programs(1) - 1)
    def _():
        idx_ref[...] = it_ref[...].T
        w_ref[...] = wt_ref[...].T


def peer_route(x2d, g, wq, sk, tok0, t, tm=1024):
    assert t % tm == 0 and tok0 % tm == 0
    d = x2d.shape[1]
    ph = sk.shape[0]
    nsel = ph * PEER_TOPK
    blk0 = tok0 // tm
    return pl.pallas_call(
        _route_kernel,
        grid=(t // tm, ph),
        in_specs=[
            pl.BlockSpec((tm, d), lambda i, p: (blk0 + i, 0)),
            pl.BlockSpec((1, d), lambda i, p: (0, 0)),
            pl.BlockSpec((d, 2 * PEER_HALF), lambda i, p: (0, p)),
            pl.BlockSpec((None, 2, PEER_NKEYS, PEER_HALF), lambda i, p: (p, 0, 0, 0)),
        ],
        out_specs=[
            pl.BlockSpec((tm, d // 2), lambda i, p: (i, 0)),
            pl.BlockSpec((tm, nsel), lambda i, p: (i, 0)),
            pl.BlockSpec((tm, nsel), lambda i, p: (i, 0)),
        ],
        out_shape=[jax.ShapeDtypeStruct((t, d // 2), I32),
                   jax.ShapeDtypeStruct((t, nsel), I32),
                   jax.ShapeDtypeStruct((t, nsel), F32)],
        scratch_shapes=[pltpu.VMEM((tm, d), BF16),
                        pltpu.VMEM((nsel, tm), I32),
                        pltpu.VMEM((nsel, tm), F32)],
        compiler_params=_cparams(("parallel", "arbitrary")),
        name="peer_route",
    )(x2d, g, wq, sk)


def _final_kernel(x_ref, y_ref, g_ref, *rest):
    o_ref = rest[-1]
    o_ref[...] = _rms(x_ref[...] + y_ref[...], g_ref[...])


def final_norm_into(out, xs, y, g, row0, total, tm=512):
    t, d = y.shape
    assert t % tm == 0 and row0 % tm == 0 and total % tm == 0
    blk0 = row0 // tm
    spec = pl.BlockSpec((tm, d), lambda i: (i, 0))
    in_specs = [spec, spec, pl.BlockSpec((1, d), lambda i: (0, 0))]
    args = [xs, y, g]
    aliases = {}
    if out is not None:
        in_specs.append(pl.BlockSpec(memory_space=pl.ANY))
        args.append(out)
        aliases = {3: 0}
    return pl.pallas_call(
        _final_kernel, grid=(t // tm,),
        in_specs=in_specs,
        out_specs=pl.BlockSpec((tm, d), lambda i: (blk0 + i, 0)),
        out_shape=jax.ShapeDtypeStruct((total, d), F32),
        input_output_aliases=aliases,
        compiler_params=_cparams(("parallel",)), name="final_norm",
    )(*args)


SC_CORES = 2
SC_SUBCORES = 16
SC_WORKERS = SC_CORES * SC_SUBCORES
SC_LANES = 16
SC_GROUP = 32


def _sc_mesh():
    return plsc.VectorSubcoreMesh(core_axis_name="c", subcore_axis_name="s")


def _sc_params():
    return pltpu.CompilerParams(needs_layout_passes=False)


def _sc_worker_id():
    return lax.axis_index("s") * SC_CORES + lax.axis_index("c")


SC_ROW_LANE = 128


def _sc_unit_off(u):
    off = u * SC_LANES
    return off if isinstance(off, int) else pl.multiple_of(off, SC_LANES)


GELU_C0 = math.sqrt(2.0 / math.pi)
GELU_C1 = 0.044715


def _gelu_tanh(x):
    z = GELU_C0 * (x + GELU_C1 * (x * x * x))
    th = 1.0 - 2.0 / (jnp.exp(2.0 * z) + 1.0)
    return 0.5 * x * (1.0 + th)


SC_PK_RING = 4
SC_PK_SUB = 4
HI_MASK = -65536


def _pack_tables_kernel(u_ref, v_ref, o_ref):
    for part, ref in enumerate((u_ref, v_ref)):
        words = _pack_bf16_halves(ref[...])
        for sub in range(SC_PK_SUB):
            o_ref[:, part * SC_PK_SUB + sub, :] = words[:, sub * SC_ROW_LANE:(sub + 1) * SC_ROW_LANE]


def pack_expert_tables(u, v, te=512):
    e, d = u.shape
    assert d == 2 * SC_PK_SUB * SC_ROW_LANE
    spec = pl.BlockSpec((te, d), lambda i: (i, 0))
    return pl.pallas_call(
        _pack_tables_kernel, grid=(e // te,), in_specs=[spec, spec],
        out_specs=pl.BlockSpec((te, 2 * SC_PK_SUB, SC_ROW_LANE), lambda i: (i, 0, 0)),
        out_shape=jax.ShapeDtypeStruct((e, 2 * SC_PK_SUB, SC_ROW_LANE), I32),
        compiler_params=_cparams(("parallel",)), name="pack_expert_tables",
    )(u, v)


def _unpack_halves(x32):
    w = plsc.bitcast(x32, I32)
    return plsc.bitcast(w << 16, F32), plsc.bitcast(w & HI_MASK, F32)


def _tree_sum(xs):
    while len(xs) > 1:
        xs = [xs[i] + xs[i + 1] for i in range(0, len(xs), 2)]
    return xs[0]


def peer_experts_pk_sc(tab_uv, idx_flat, w_flat, hp, d):
    t = hp.shape[0]
    nsel = PEER_SEL
    g = SC_GROUP
    assert t % (SC_WORKERS * g) == 0 and d == 2 * SC_PK_SUB * SC_ROW_LANE
    tpw = t // SC_WORKERS
    groups = tpw // g
    heads = nsel // SC_LANES
    chunks = d // 32
    units = g * heads
    ring = SC_PK_RING
    assert units % ring == 0
    row_buf = pltpu.VMEM((SC_LANES, 2 * SC_PK_SUB, SC_ROW_LANE), I32)

    def row_words(rows, r, wc, sub0):
        per = SC_ROW_LANE // SC_LANES
        return plsc.bitcast(
            rows[r, sub0 + wc // per, pl.ds(pl.multiple_of((wc % per) * SC_LANES, SC_LANES), SC_LANES)], BF16)

    def ring_loop(n_units, start, wait, compute):
        for u in range(ring - 1):
            start(u, u)

        @pl.loop(0, n_units, step=ring)
        def _(uu):
            for b in range(ring):
                u = uu + b
                nxt = u + (ring - 1)

                @pl.when(nxt < n_units)
                def _():
                    start(nxt, (b + ring - 1) % ring)

                wait(u, b)
                compute(u, b)

    @functools.partial(
        pl.kernel, mesh=_sc_mesh(),
        out_type=jax.ShapeDtypeStruct((t, d), F32),
        scratch_types=[
            pltpu.VMEM((g * nsel,), I32),
            pltpu.VMEM((g * nsel,), F32),
            pltpu.VMEM((g, d // 2), I32),
            pltpu.VMEM((g, d), F32),
            pltpu.VMEM((SC_LANES * SC_LANES,), F32),
            [row_buf] * ring,
            [pltpu.SemaphoreType.DMA] * ring,
        ],
        compiler_params=_sc_params(),
        name="peer_experts_pk_sc",
    )
    def k(tab_hbm, idx_hbm, w_hbm, h_hbm, out_hbm, idx_v, coef_v, h_v, y_v, red_v, rows, sems):
        wid = _sc_worker_id()
        lane = lax.iota(I32, SC_LANES)

        def copy(u, slot):
            ids = idx_v.at[pl.ds(_sc_unit_off(u), SC_LANES)]
            return pltpu.make_async_copy(tab_hbm.at[ids], rows[slot], sems[slot])

        def dots(u, slot):
            tt = u // heads

            def body(cp, accs):
                out = []
                hv = [plsc.bitcast(h_v[tt, pl.ds(pl.multiple_of((2 * cp + i) * SC_LANES, SC_LANES), SC_LANES)], BF16)
                      for i in range(2)]
                for r in range(SC_LANES):
                    pr = (row_words(rows[slot], r, 2 * cp, 0) * hv[0]
                          + row_words(rows[slot], r, 2 * cp + 1, 0) * hv[1])
                    lo, hi = _unpack_halves(pr)
                    out.append(accs[r] + lo + hi)
                return tuple(out)

            accs = lax.fori_loop(0, chunks // 2, body,
                                 tuple(jnp.zeros((SC_LANES,), F32) for _ in range(SC_LANES)))
            for r in range(SC_LANES):
                red_v[pl.ds(r * SC_LANES, SC_LANES)] = accs[r]
            act = _tree_sum([plsc.load_gather(red_v, [lane * SC_LANES + j]) for j in range(SC_LANES)])
            sl = pl.ds(_sc_unit_off(u), SC_LANES)
            coef_v[sl] = coef_v[sl] * _gelu_tanh(act)

        def combine(u, slot):
            tt = u // heads
            first = (u % heads) == 0
            cb = []
            for r in range(SC_LANES):
                c = plsc.load_gather(coef_v, [jnp.full((SC_LANES,), u * SC_LANES + r, I32)])
                cb.append(plsc.pack(c, c, format=plsc.PackFormat.INTERLEAVED))

            @plsc.parallel_loop(0, chunks, unroll=2)
            def _(wc):
                lo, hi = _unpack_halves(
                    _tree_sum([cb[r] * row_words(rows[slot], r, wc, SC_PK_SUB) for r in range(SC_LANES)]))
                for half, val in ((0, lo), (1, hi)):
                    sl = pl.ds(pl.multiple_of(half * (d // 2) + wc * SC_LANES, SC_LANES), SC_LANES)
                    y_v[tt, sl] = val + jnp.where(first, 0.0, y_v[tt, sl])

        def unit(u, slot):
            dots(u, slot)
            combine(u, slot)

        @pl.loop(0, groups)
        def _(gi):
            base = wid * tpw + gi * g
            pltpu.sync_copy(idx_hbm.at[pl.ds(base * nsel, g * nsel)], idx_v)
            pltpu.sync_copy(w_hbm.at[pl.ds(base * nsel, g * nsel)], coef_v)
            pltpu.sync_copy(h_hbm.at[pl.ds(base, g)], h_v)
            ring_loop(units, lambda u, s: copy(u, s).start(), lambda u, s: copy(u, s).wait(), unit)
            pltpu.sync_copy(y_v, out_hbm.at[pl.ds(base, g)])

    return k(tab_uv, idx_flat, w_flat, hp)


def kernel(x, mem, rel_bias, ln_mix, w_in, hg_lower, hg_norm, w_up_a, w_up_b, w_out, ln_cross, ln_mem, wq_x, wk_x, wv_x, wo_x, ln_ffn, peer_query, peer_subkeys, peer_u, peer_v, ln_final):
    b, s, d = x.shape
    depth = w_in.shape[0]
    assert depth == 1, "the residual after PEER is fused into the final norm"
    assert s % MB_BLOCK == 0 and s % HG_CHUNK == 0 and s % (PEER_SLICES * SC_WORKERS * SC_GROUP) == 0
    nb = s // MB_BLOCK
    row = lambda a: a.reshape(1, -1).astype(F32)
    lb_all = jnp.cumsum(jax.nn.softmax(hg_lower.astype(F32), axis=0), axis=0)
    bias = moba_bias_tiles(rel_bias)
    n_hg = 4 * HG_WIDTH
    n_qk = 2 * MB_WIDTH
    n_mb = 3 * MB_WIDTH
    l = 0
    w = w_in[l].astype(BF16)
    w_hg, w_qk, w_vt, w_g = w[:, :n_hg], w[:, n_hg:n_hg + n_qk], w[:, n_hg + n_qk:n_hg + n_mb].T, w[:, n_hg + n_mb:]
    wa, wb, wo = w_up_a[l].astype(BF16), w_up_b[l].astype(BF16), w_out[l].astype(BF16)
    wqx, wox = wq_x[l].astype(BF16), wo_x[l].astype(BF16)
    wpq, sk = peer_query[l].astype(BF16), peer_subkeys[l].astype(F32)
    tab_uv = pack_expert_tables(peer_u[l].astype(F32), peer_v[l].astype(F32))
    kx, vx = mem_kv(mem, row(ln_mem[l]), wk_x[l].astype(BF16), wv_x[l].astype(BF16))

    outs = []
    for bi in range(b):
        x2d = x[bi]
        p0, pqk, km, vt, pg = in_proj(x2d, row(ln_mix[l]), w_hg, w_qk, w_vt, w_g)
        ya = hgrn2(p0, row(lb_all[l]), row(hg_norm[l]), 1, s)
        km = km.reshape(1, nb, MB_WIDTH)
        ts = s // PEER_SLICES
        for tok0 in range(0, s, ts):
            yb = moba_attention(pqk, vt, km, bias, 1, s, tok0 // MB_BLOCK, ts // MB_BLOCK)
            xs = mix_cross(x2d, ya, yb, pg, wa, wb, wo, row(ln_cross[l]), wqx, kx[bi:bi + 1], vx[bi:bi + 1], wox, tok0)
            hp, eidx, wts = peer_route(xs, row(ln_ffn[l]), wpq, sk, 0, ts)
            y = peer_experts_pk_sc(tab_uv, eidx.reshape(ts * PEER_SEL), wts.reshape(ts * PEER_SEL), hp, d)
            outs.append((xs, y, bi * s + tok0))
    out = None
    for xs, y, row0 in reversed(outs):
        out = final_norm_into(out, xs, y, row(ln_final), row0, b * s)
    return out.reshape(b, s, d)
```

```python
import functools
import math

import jax
import jax.numpy as jnp
from jax import lax
from jax.experimental import pallas as pl
from jax.experimental.pallas import tpu as pltpu
from jax.experimental.pallas import tpu_sc as plsc

F32 = jnp.float32
BF16 = jnp.bfloat16
I32 = jnp.int32
EPS = 1e-6
NEG_INF = float("-inf")

HG_HEADS = 4
HG_D = 128
HG_WIDTH = HG_HEADS * HG_D
HG_CHUNK = 64
HG_SUB = 16
MB_HEADS = 8
MB_DH = 64
MB_WIDTH = MB_HEADS * MB_DH
MB_BLOCK = 256
MB_TOPK = 3
MB_BIAS_TILES = 8
REL_BUCKETS = 32
REL_MAX_DIST = 2048
X_HEADS = 4
PEER_HEADS = 8
PEER_NKEYS = 128
PEER_TOPK = 16
PEER_HALF = 128
PEER_SEL = PEER_HEADS * PEER_TOPK
PEER_SLICES = 4

VMEM_LIMIT = 56 * 1024 * 1024


def _cparams(sem):
    return pltpu.CompilerParams(dimension_semantics=sem, vmem_limit_bytes=VMEM_LIMIT)


def _rms(x, g):
    ms = jnp.mean(x * x, axis=-1, keepdims=True)
    return x * lax.rsqrt(ms + EPS) * g


def _in_proj_kernel(x_ref, g_ref, w0_ref, w1_ref, wvt_ref, w2_ref, o0_ref, o1_ref, okm_ref, ovt_ref, o2_ref):
    h = _rms(x_ref[...], g_ref[...]).astype(BF16)
    o0_ref[...] = jnp.dot(h, w0_ref[...], preferred_element_type=F32)
    qk = jnp.dot(h, w1_ref[...], preferred_element_type=F32)
    o1_ref[...] = qk.astype(BF16)
    okm_ref[0] = jnp.mean(qk[:, MB_WIDTH:], axis=0, keepdims=True)
    vt = lax.dot_general(wvt_ref[...], h, (((1,), (1,)), ((), ())), preferred_element_type=F32).astype(BF16)
    for hd in range(MB_HEADS):
        ovt_ref[0, hd * MB_VROWS:hd * MB_VROWS + MB_DH, :] = vt[hd * MB_DH:(hd + 1) * MB_DH]
        ovt_ref[0, hd * MB_VROWS + MB_DH:(hd + 1) * MB_VROWS, :] = jnp.ones((MB_ONES, vt.shape[1]), BF16)
    o2_ref[...] = jnp.dot(h, w2_ref[...], preferred_element_type=F32).astype(BF16)


def in_proj(x2d, g, w0, w1, wvt, w2):
    t, d = x2d.shape
    tm = MB_BLOCK
    assert wvt.shape[0] == MB_WIDTH and w1.shape[1] == 2 * MB_WIDTH
    n0, n1, nv, n2 = w0.shape[1], w1.shape[1], MB_VT_ROWS, w2.shape[1]
    full = lambda a: pl.BlockSpec(a.shape, lambda i: (0, 0))
    return pl.pallas_call(
        _in_proj_kernel,
        grid=(t // tm,),
        in_specs=[pl.BlockSpec((tm, d), lambda i: (i, 0)), full(g), full(w0), full(w1), full(wvt), full(w2)],
        out_specs=[pl.BlockSpec((tm, n0), lambda i: (i, 0)),
                   pl.BlockSpec((tm, n1), lambda i: (i, 0)),
                   pl.BlockSpec((1, 1, MB_WIDTH), lambda i: (i, 0, 0)),
                   pl.BlockSpec((1, nv, tm), lambda i: (i, 0, 0)),
                   pl.BlockSpec((tm, n2), lambda i: (i, 0))],
        out_shape=[jax.ShapeDtypeStruct((t, n0), F32),
                   jax.ShapeDtypeStruct((t, n1), BF16),
                   jax.ShapeDtypeStruct((t // tm, 1, MB_WIDTH), F32),
                   jax.ShapeDtypeStruct((t // tm, nv, tm), BF16),
                   jax.ShapeDtypeStruct((t, n2), BF16)],
        compiler_params=_cparams(("parallel",)),
        name="in_proj",
    )(x2d, g, w0, w1, wvt, w2)


def _hgrn_kernel(q_ref, f_ref, i_ref, g_ref, lb_ref, gain_ref, o_ref, st_ref):
    c = pl.program_id(1)

    @pl.when(c == 0)
    def _():
        st_ref[...] = jnp.zeros_like(st_ref)

    C, S = HG_CHUNK, HG_SUB
    row = lax.broadcasted_iota(I32, (C, C), 0)
    col = lax.broadcasted_iota(I32, (C, C), 1)
    tril = (row >= col).astype(F32)
    t_iota = lax.broadcasted_iota(I32, (S, 1), 0)

    for h in range(HG_HEADS):
        sl = slice(h * HG_D, (h + 1) * HG_D)
        q = q_ref[:, sl]
        v = i_ref[:, sl]
        lb = lb_ref[:, sl]
        f = lb + (1.0 - lb) * jax.nn.sigmoid(f_ref[:, sl])
        lf = jnp.log(f)
        k = 1.0 - f
        b = jnp.dot(tril, lf, precision=lax.Precision.HIGHEST, preferred_element_type=F32)
        st = st_ref[h]
        vb = v.astype(BF16)
        qd = (q * jnp.exp(b)).astype(BF16)
        o_inter = lax.dot_general(qd, st.astype(BF16), (((1,), (1,)), ((), ())),
                                  preferred_element_type=F32)
        outs = []
        for i in range(C // S):
            r0 = i * S
            qi = q[r0:r0 + S]
            ki = k[r0:r0 + S]
            bi = b[r0:r0 + S]
            vi = v[r0:r0 + S]
            oi = o_inter[r0:r0 + S]
            if i > 0:
                bs = b[r0 - 1:r0]
                qh = (qi * jnp.exp(bi - bs)).astype(BF16)
                kh = (k[:r0] * jnp.exp(bs - b[:r0])).astype(BF16)
                a = lax.dot_general(qh, kh, (((1,), (1,)), ((), ())), preferred_element_type=F32)
                oi = oi + jnp.dot(a.astype(BF16), vb[:r0], preferred_element_type=F32)
            half = S // 2
            o_half = [oi[:half], oi[half:]]
            for s in range(S):
                for hf in range(s // half, 2):
                    rows = slice(hf * half, (hf + 1) * half)
                    dec = jnp.exp(jnp.minimum(bi[rows] - bi[s:s + 1], 0.0))
                    a_s = jnp.sum(qi[rows] * ki[s:s + 1] * dec, axis=-1, keepdims=True)
                    a_s = jnp.where(t_iota[rows] >= s, a_s, 0.0)
                    o_half[hf] = o_half[hf] + a_s * vi[s:s + 1]
            outs.extend(o_half)
        o = jnp.concatenate(outs, axis=0)
        b_end = b[C - 1:C]
        kd = (k * jnp.exp(b_end - b)).astype(BF16)
        upd = lax.dot_general(vb, kd, (((0,), (0,)), ((), ())), preferred_element_type=F32)
        st_ref[h] = st * jnp.exp(b_end) + upd
        o = o * lax.rsqrt(jnp.mean(o * o, axis=-1, keepdims=True) + EPS)
        g = g_ref[:, sl]
        o_ref[:, sl] = (o * gain_ref[:, sl] * (g * jax.nn.sigmoid(g))).astype(o_ref.dtype)


def hgrn2(p0, lb, gain, batch, seq):
    t = p0.shape[0]
    nc = seq // HG_CHUNK
    w = HG_WIDTH

    def col(j):
        return pl.BlockSpec((HG_CHUNK, w), lambda b, c, j=j: (b * nc + c, j))

    return pl.pallas_call(
        _hgrn_kernel,
        grid=(batch, nc),
        in_specs=[col(0), col(1), col(2), col(3),
                  pl.BlockSpec((1, w), lambda b, c: (0, 0)),
                  pl.BlockSpec((1, w), lambda b, c: (0, 0))],
        out_specs=pl.BlockSpec((HG_CHUNK, w), lambda b, c: (b * nc + c, 0)),
        out_shape=jax.ShapeDtypeStruct((t, w), BF16),
        scratch_shapes=[pltpu.VMEM((HG_HEADS, HG_D, HG_D), F32)],
        compiler_params=_cparams(("parallel", "arbitrary")),
        name="hgrn2",
    )(p0, p0, p0, p0, lb, gain)


MB_PAIR = 4
MB_PW = MB_PAIR * MB_DH
MB_LG = 128
MB_ONES = 16
MB_VROWS = MB_DH + MB_ONES
MB_VT_ROWS = MB_HEADS * MB_VROWS


def _moba_kernel(q_ref, k_ref, vt_ref, km_ref, bias_ref, o_ref, *scratch, qb0):
    m_ref, l_ref, al_ref, acc_ref, msk_ref, s_ref, p_ref = (
        scratch[i * MB_PAIR:(i + 1) * MB_PAIR] for i in range(7))
    qi = pl.program_id(2) + qb0
    nb = km_ref.shape[0]
    blk = MB_BLOCK
    heads = range(MB_PAIR)
    grp = lambda hh: slice((hh // 2) * MB_LG, (hh // 2 + 1) * MB_LG)
    q = q_ref[...]
    lane = lax.broadcasted_iota(I32, (blk, MB_LG), 1)
    in_head = [(lane < MB_DH) if hh % 2 == 0 else (lane >= MB_DH) for hh in heads]
    qs = q * jnp.asarray(MB_DH ** -0.5, BF16)
    nt = (((1,), (1,)), ((), ()))
    qf = q.astype(F32)
    qht = [jnp.where(in_head[hh], qs[:, grp(hh)].astype(F32), 0.0).T.astype(BF16) for hh in heads]

    n_io = lax.broadcasted_iota(I32, (nb, blk), 0)
    for hh in heads:
        gate = lax.dot_general(km_ref[:, grp(hh)], jnp.where(in_head[hh], qf[:, grp(hh)], 0.0), nt,
                               precision=lax.Precision.HIGHEST, preferred_element_type=F32)
        gate = jnp.where(n_io < qi, gate, NEG_INF)
        chosen = n_io < 0
        for _ in range(MB_TOPK):
            mx = jnp.max(gate, axis=0, keepdims=True)
            ix = jnp.min(jnp.where(gate == mx, n_io, nb), axis=0, keepdims=True)
            hit = n_io == ix
            chosen = chosen | (hit & (mx > NEG_INF))
            gate = jnp.where(hit, NEG_INF, gate)
        msk_ref[hh][...] = jnp.where(chosen, 0.0, NEG_INF)

    vrows = lambda hh: slice(hh * MB_VROWS, (hh + 1) * MB_VROWS)

    def pv_stage(blk_idx):
        vtb = vt_ref[blk_idx]
        r = [jnp.dot(vtb[vrows(hh)], p_ref[hh][...], preferred_element_type=F32) for hh in heads]
        al = [al_ref[hh][...] for hh in heads]
        a_new = [al[hh] * acc_ref[hh][...] + r[hh][:MB_DH] for hh in heads]
        l_new = [al[hh] * l_ref[hh][...] + r[hh][MB_DH:MB_DH + 1] for hh in heads]
        return a_new, l_new

    def store_pv(a_new, l_new):
        for hh in heads:
            acc_ref[hh][...] = a_new[hh]
            l_ref[hh][...] = l_new[hh]

    def softmax_stage():
        s = [s_ref[hh][...] for hh in heads]
        m_old = [m_ref[hh][...] for hh in heads]
        m_new = [jnp.maximum(m_old[hh], jnp.max(s[hh], axis=0, keepdims=True)) for hh in heads]
        alpha = [jnp.exp(m_old[hh] - m_new[hh]) for hh in heads]
        p = [jnp.exp((s[hh] - m_new[hh]).astype(BF16)) for hh in heads]
        return p, alpha, m_new

    def store_softmax(p, alpha, m_new):
        for hh in heads:
            p_ref[hh][...] = p[hh]
            al_ref[hh][...] = alpha[hh]
            m_ref[hh][...] = m_new[hh]

    k_own = k_ref[pl.ds(pl.multiple_of(qi * blk, blk), blk), :]
    key_io = lax.broadcasted_iota(I32, (blk, blk), 0)
    qry_io = lax.broadcasted_iota(I32, (blk, blk), 1)
    for hh in heads:
        s = jnp.dot(k_own[:, grp(hh)], qht[hh], preferred_element_type=F32) + bias_ref[hh, 0]
        s_ref[hh][...] = jnp.where(key_io <= qry_io, s, NEG_INF)
        m_ref[hh][...] = jnp.full((1, blk), NEG_INF, F32)
        l_ref[hh][...] = jnp.zeros((1, blk), F32)
        al_ref[hh][...] = jnp.ones((1, blk), F32)
        acc_ref[hh][...] = jnp.zeros((MB_DH, blk), F32)
        p_ref[hh][...] = jnp.zeros((blk, blk), BF16)

    def step(i, carry, far):
        pv = pv_stage(jnp.where(i <= 1, qi, i - 2))
        sm = softmax_stage()
        kn = k_ref[pl.ds(pl.multiple_of(i * blk, blk), blk), :]
        if far:
            row = [msk_ref[hh][pl.ds(i, 1), :] + bias_ref[hh, MB_BIAS_TILES - 1, 0:1, 0:1] for hh in heads]
            s_next = [jnp.dot(kn[:, grp(hh)], qht[hh], preferred_element_type=F32) + row[hh] for hh in heads]
        else:
            d = qi - i
            s_next = [jnp.dot(kn[:, grp(hh)], qht[hh], preferred_element_type=F32)
                      + bias_ref[hh, d] + msk_ref[hh][pl.ds(i, 1), :] for hh in heads]
        store_pv(*pv)
        for hh in heads:
            s_ref[hh][...] = s_next[hh]
        store_softmax(*sm)
        return carry

    n_far = jnp.maximum(qi - (MB_BIAS_TILES - 2), 0)
    lax.fori_loop(0, n_far, functools.partial(step, far=True), 0)
    lax.fori_loop(n_far, qi, functools.partial(step, far=False), 0)
    pv = pv_stage(jnp.where(qi <= 1, qi, qi - 2))
    sm = softmax_stage()
    store_pv(*pv)
    store_softmax(*sm)
    a_fin, l_fin = pv_stage(jnp.where(qi == 0, qi, qi - 1))
    out_t = jnp.concatenate([a_fin[hh] / l_fin[hh] for hh in heads], axis=0)
    o_ref[...] = out_t.T.astype(o_ref.dtype)


def moba_attention(pqk, vt, km, bias, batch, seq, qb0=0, nqb=None):
    nb = seq // MB_BLOCK
    nqb = nb if nqb is None else nqb
    t = batch * nqb * MB_BLOCK
    groups = MB_WIDTH // MB_PW
    return pl.pallas_call(
        functools.partial(_moba_kernel, qb0=qb0),
        grid=(batch, groups, nqb),
        in_specs=[
            pl.BlockSpec((MB_BLOCK, MB_PW), lambda b, j, i: (b * nb + qb0 + i, j)),
            pl.BlockSpec((seq, MB_PW), lambda b, j, i: (b, groups + j)),
            pl.BlockSpec((nb, MB_PAIR * MB_VROWS, MB_BLOCK), lambda b, j, i: (b, j, 0)),
            pl.BlockSpec((None, nb, MB_PW), lambda b, j, i: (b, 0, j)),
            pl.BlockSpec((MB_PAIR, MB_BIAS_TILES, MB_BLOCK, MB_BLOCK), lambda b, j, i: (j, 0, 0, 0)),
        ],
        out_specs=pl.BlockSpec((MB_BLOCK, MB_PW), lambda b, j, i: (b * nqb + i, j)),
        out_shape=jax.ShapeDtypeStruct((t, MB_WIDTH), BF16),
        scratch_shapes=(
            [pltpu.VMEM((1, MB_BLOCK), F32)] * (3 * MB_PAIR)
            + [pltpu.VMEM((MB_DH, MB_BLOCK), F32)] * MB_PAIR
            + [pltpu.VMEM((nb, MB_BLOCK), F32)] * MB_PAIR
            + [pltpu.VMEM((MB_BLOCK, MB_BLOCK), F32)] * MB_PAIR
            + [pltpu.VMEM((MB_BLOCK, MB_BLOCK), BF16)] * MB_PAIR
        ),
        compiler_params=_cparams(("parallel", "parallel", "arbitrary")),
        name="moba_attn",
    )(pqk, pqk, vt, km, bias)


def _t5_bucket(dist):
    max_exact = REL_BUCKETS // 2
    scaled = jnp.log(jnp.maximum(dist, 1).astype(F32) / max_exact) / math.log(REL_MAX_DIST / max_exact)
    large = jnp.minimum(max_exact + (scaled * (REL_BUCKETS - max_exact)).astype(I32), REL_BUCKETS - 1)
    return jnp.where(dist < max_exact, dist, large)


def moba_bias_tiles(rel_bias):
    blk = MB_BLOCK
    span = 2 * blk - 1
    x = jnp.arange(span) - (blk - 1)
    dist = jnp.maximum(jnp.arange(MB_BIAS_TILES)[:, None] * blk + x[None, :], 0)
    w = rel_bias.astype(F32).T[:, _t5_bucket(dist)]
    h = w.shape[0]
    wp = jnp.pad(w, ((0, 0), (0, 0), (0, 1)))
    a = jnp.broadcast_to(wp[:, :, None, :], (h, MB_BIAS_TILES, blk, span + 1))
    a = a.reshape(h, MB_BIAS_TILES, blk * (span + 1))[:, :, :blk * span]
    return a.reshape(h, MB_BIAS_TILES, blk, span)[:, :, :, blk - 1:]


def _mix_kernel(x_ref, ya_ref, yb_ref, ga_ref, gb_ref, wa_ref, wb_ref, wo_ref, o_ref):
    za = jnp.dot(ya_ref[...], wa_ref[...], preferred_element_type=F32)
    zb = jnp.dot(yb_ref[...], wb_ref[...], preferred_element_type=F32)
    z = jax.nn.sigmoid(ga_ref[...].astype(F32)) * za + jax.nn.sigmoid(gb_ref[...].astype(F32)) * zb
    o_ref[...] = x_ref[...] + jnp.dot(z.astype(BF16), wo_ref[...], preferred_element_type=F32)


def _mem_kv_kernel(m_ref, g_ref, wk_ref, wv_ref, k_ref, v_ref):
    mn = _rms(m_ref[...], g_ref[...]).astype(BF16)
    k_ref[...] = jnp.dot(mn, wk_ref[...], preferred_element_type=F32).astype(BF16)
    v_ref[...] = jnp.dot(mn, wv_ref[...], preferred_element_type=F32).astype(BF16)


def mem_kv(mem, g, wk, wv):
    b, m, d = mem.shape
    spec = pl.BlockSpec((None, m, d), lambda i: (i, 0, 0))
    wspec = pl.BlockSpec((d, d), lambda i: (0, 0))
    return pl.pallas_call(
        _mem_kv_kernel,
        grid=(b,),
        in_specs=[spec, pl.BlockSpec((1, d), lambda i: (0, 0)), wspec, wspec],
        out_specs=[spec, spec],
        out_shape=[jax.ShapeDtypeStruct((b, m, d), BF16)] * 2,
        compiler_params=_cparams(("parallel",)),
        name="mem_kv",
    )(mem, g, wk, wv)


def _cross_kernel(x_ref, g_ref, wq_ref, k_ref, v_ref, wo_ref, o_ref):
    x = x_ref[...]
    d = x.shape[1]
    dh = d // X_HEADS
    h = _rms(x, g_ref[...]).astype(BF16)
    q = (jnp.dot(h, wq_ref[...], preferred_element_type=F32) * (dh ** -0.5)).astype(BF16)
    outs = []
    for hh in range(X_HEADS):
        sl = slice(hh * dh, (hh + 1) * dh)
        s = lax.dot_general(q[:, sl], k_ref[:, sl], (((1,), (1,)), ((), ())),
                            preferred_element_type=F32)
        p = jnp.exp(s - jnp.max(s, axis=1, keepdims=True))
        l = jnp.sum(p, axis=1, keepdims=True)
        o = jnp.dot(p.astype(BF16), v_ref[:, sl], preferred_element_type=F32) / l
        outs.append(o.astype(BF16))
    o = jnp.concatenate(outs, axis=1)
    o_ref[...] = x + jnp.dot(o, wo_ref[...], preferred_element_type=F32)


def _mix_cross_kernel(x_ref, ya_ref, yb_ref, ga_ref, gb_ref, wa_ref, wb_ref, wo_ref,
                      g_ref, wq_ref, k_ref, v_ref, wox_ref, o_ref, x1_ref):
    _mix_kernel(x_ref, ya_ref, yb_ref, ga_ref, gb_ref, wa_ref, wb_ref, wo_ref, x1_ref)
    _cross_kernel(x1_ref, g_ref, wq_ref, k_ref, v_ref, wox_ref, o_ref)


def mix_cross(x2d, ya, yb, pg, wa, wb, wo, g, wq, kx, vx, wox, tok0, tm=512):
    t = yb.shape[0]
    assert t % tm == 0 and tok0 % tm == 0
    d = x2d.shape[1]
    w = ya.shape[1]
    m = kx.shape[1]
    b0 = tok0 // tm
    const = lambda a: pl.BlockSpec(a.shape, lambda i: (0,) * a.ndim)
    kv = pl.BlockSpec((None, m, d), lambda i: (0, 0, 0))
    return pl.pallas_call(
        _mix_cross_kernel,
        grid=(t // tm,),
        in_specs=[
            pl.BlockSpec((tm, d), lambda i: (b0 + i, 0)),
            pl.BlockSpec((tm, w), lambda i: (b0 + i, 0)),
            pl.BlockSpec((tm, w), lambda i: (i, 0)),
            pl.BlockSpec((tm, d), lambda i: (b0 + i, 0)),
            pl.BlockSpec((tm, d), lambda i: (b0 + i, 1)),
            const(wa), const(wb), const(wo), const(g), const(wq), kv, kv, const(wox),
        ],
        out_specs=pl.BlockSpec((tm, d), lambda i: (i, 0)),
        out_shape=jax.ShapeDtypeStruct((t, d), F32),
        scratch_shapes=[pltpu.VMEM((tm, d), F32)],
        compiler_params=_cparams(("parallel",)),
        name="mix_cross",
    )(x2d, ya, yb, pg, pg, wa, wb, wo, g, wq, kx, vx, wox)


def _topk_rows(sc, k):
    n = sc.shape[0]
    io = lax.broadcasted_iota(I32, sc.shape, 0).astype(F32)
    vals, ids = [], []
    for _ in range(k):
        m = jnp.max(sc, axis=0, keepdims=True)
        ix = jnp.argmax(sc, axis=0, keepdims=True).astype(F32)
        vals.append(m)
        ids.append(ix)
        sc = jnp.where(io == ix, NEG_INF, sc)
    return jnp.concatenate(vals, axis=0), jnp.concatenate(ids, axis=0).astype(I32)


def _pack_bf16_halves(h):
    bits = lax.bitcast_convert_type(h, I32)
    r = bits + 0x7FFF + (lax.shift_right_logical(bits, 16) & 1)
    half = h.shape[1] // 2
    return lax.shift_right_logical(r[:, :half], 16) | (r[:, half:] & HI_MASK)


def _route_kernel(x_ref, g_ref, wq_ref, sk_ref, hp_ref, idx_ref, w_ref, hb_ref, it_ref, wt_ref):
    p = pl.program_id(1)

    @pl.when(p == 0)
    def _():
        h = _rms(x_ref[...], g_ref[...])
        hp_ref[...] = _pack_bf16_halves(h)
        hb_ref[...] = h.astype(BF16)

    qh = jnp.dot(hb_ref[...], wq_ref[...], preferred_element_type=F32)
    tops = []
    for c in range(2):
        seg = qh[:, c * PEER_HALF:(c + 1) * PEER_HALF]
        sc = lax.dot_general(sk_ref[c], seg, (((1,), (1,)), ((), ())),
                             precision=lax.Precision.HIGHEST, preferred_element_type=F32)
        tops.append(_topk_rows(sc, PEER_TOPK))
    (s0, i0), (s1, i1) = tops
    k = PEER_TOPK
    sub = 8
    tm = s0.shape[1]
    r8 = lax.broadcasted_iota(I32, (sub, tm), 0)
    r16 = lax.broadcasted_iota(I32, (k, tm), 0)
    cand_b = [s0[0:1] + s1, s0[1:2] + s1[:sub]]
    cidx_b = [i0[0:1] * PEER_NKEYS + i1, i0[1:2] * PEER_NKEYS + i1[:sub]]
    pos_b = [r16, k + r8]
    for a in range(2, sub):
        keep = r8 < (k // (a + 1))
        cand_b.append(jnp.where(keep, s0[a:a + 1] + s1[:sub], NEG_INF))
        cidx_b.append(i0[a:a + 1] * PEER_NKEYS + i1[:sub])
        pos_b.append(a * k + r8)
    cand_b.append(s0[sub:] + s1[0:1])
    cidx_b.append(i0[sub:] * PEER_NKEYS + i1[0:1])
    pos_b.append((sub + r8) * k)
    cand = jnp.concatenate(cand_b, axis=0)
    cidx = jnp.concatenate(cidx_b, axis=0)
    pos = jnp.concatenate(pos_b, axis=0).astype(F32)
    vals, ids = [], []
    for _ in range(k):
        m = jnp.max(cand, axis=0, keepdims=True)
        px = jnp.min(jnp.where(cand == m, pos, float(k * k)), axis=0, keepdims=True)
        hit = pos == px
        vals.append(m)
        ids.append(jnp.sum(jnp.where(hit, cidx, 0), axis=0, keepdims=True))
        cand = jnp.where(hit, NEG_INF, cand)
    sf = jnp.concatenate(vals, axis=0)
    e = jnp.exp(sf - sf[0:1])
    rows = pl.ds(pl.multiple_of(p * PEER_TOPK, PEER_TOPK), PEER_TOPK)
    wt_ref[rows, :] = e / jnp.sum(e, axis=0, keepdims=True)
    it_ref[rows, :] = jnp.concatenate(ids, axis=0)

    @pl.when(p == pl.num_programs(1) - 1)
    def _():
        idx_ref[...] = it_ref[...].T
        w_ref[...] = wt_ref[...].T


def peer_route(x2d, g, wq, sk, tok0, t, tm=1024):
    assert t % tm == 0 and tok0 % tm == 0
    d = x2d.shape[1]
    ph = sk.shape[0]
    nsel = ph * PEER_TOPK
    blk0 = tok0 // tm
    return pl.pallas_call(
        _route_kernel,
        grid=(t // tm, ph),
        in_specs=[
            pl.BlockSpec((tm, d), lambda i, p: (blk0 + i, 0)),
            pl.BlockSpec((1, d), lambda i, p: (0, 0)),
            pl.BlockSpec((d, 2 * PEER_HALF), lambda i, p: (0, p)),
            pl.BlockSpec((None, 2, PEER_NKEYS, PEER_HALF), lambda i, p: (p, 0, 0, 0)),
        ],
        out_specs=[
            pl.BlockSpec((tm, d // 2), lambda i, p: (i, 0)),
            pl.BlockSpec((tm, nsel), lambda i, p: (i, 0)),
            pl.BlockSpec((tm, nsel), lambda i, p: (i, 0)),
        ],
        out_shape=[jax.ShapeDtypeStruct((t, d // 2), I32),
                   jax.ShapeDtypeStruct((t, nsel), I32),
                   jax.ShapeDtypeStruct((t, nsel), F32)],
        scratch_shapes=[pltpu.VMEM((tm, d), BF16),
                        pltpu.VMEM((nsel, tm), I32),
                        pltpu.VMEM((nsel, tm), F32)],
        compiler_params=_cparams(("parallel", "arbitrary")),
        name="peer_route",
    )(x2d, g, wq, sk)


def _final_kernel(x_ref, y_ref, g_ref, *rest):
    o_ref = rest[-1]
    o_ref[...] = _rms(x_ref[...] + y_ref[...], g_ref[...])


def final_norm_into(out, xs, y, g, row0, total, tm=512):
    t, d = y.shape
    assert t % tm == 0 and row0 % tm == 0 and total % tm == 0
    blk0 = row0 // tm
    spec = pl.BlockSpec((tm, d), lambda i: (i, 0))
    in_specs = [spec, spec, pl.BlockSpec((1, d), lambda i: (0, 0))]
    args = [xs, y, g]
    aliases = {}
    if out is not None:
        in_specs.append(pl.BlockSpec(memory_space=pl.ANY))
        args.append(out)
        aliases = {3: 0}
    return pl.pallas_call(
        _final_kernel, grid=(t // tm,),
        in_specs=in_specs,
        out_specs=pl.BlockSpec((tm, d), lambda i: (blk0 + i, 0)),
        out_shape=jax.ShapeDtypeStruct((total, d), F32),
        input_output_aliases=aliases,
        compiler_params=_cparams(("parallel",)), name="final_norm",
    )(*args)


SC_CORES = 2
SC_SUBCORES = 16
SC_WORKERS = SC_CORES * SC_SUBCORES
SC_LANES = 16
SC_GROUP = 32


def _sc_mesh():
    return plsc.VectorSubcoreMesh(core_axis_name="c", subcore_axis_name="s")


def _sc_params():
    return pltpu.CompilerParams(needs_layout_passes=False)


def _sc_worker_id():
    return lax.axis_index("s") * SC_CORES + lax.axis_index("c")


SC_ROW_LANE = 128


def _sc_unit_off(u):
    off = u * SC_LANES
    return off if isinstance(off, int) else pl.multiple_of(off, SC_LANES)


GELU_C0 = math.sqrt(2.0 / math.pi)
GELU_C1 = 0.044715


def _gelu_tanh(x):
    z = GELU_C0 * (x + GELU_C1 * (x * x * x))
    th = 1.0 - 2.0 / (jnp.exp(2.0 * z) + 1.0)
    return 0.5 * x * (1.0 + th)


SC_PK_RING = 4
SC_PK_SUB = 4
HI_MASK = -65536


def _pack_tables_kernel(u_ref, v_ref, o_ref):
    for part, ref in enumerate((u_ref, v_ref)):
        words = _pack_bf16_halves(ref[...])
        for sub in range(SC_PK_SUB):
            o_ref[:, part * SC_PK_SUB + sub, :] = words[:, sub * SC_ROW_LANE:(sub + 1) * SC_ROW_LANE]


def pack_expert_tables(u, v, te=512):
    e, d = u.shape
    assert d == 2 * SC_PK_SUB * SC_ROW_LANE
    spec = pl.BlockSpec((te, d), lambda i: (i, 0))
    return pl.pallas_call(
        _pack_tables_kernel, grid=(e // te,), in_specs=[spec, spec],
        out_specs=pl.BlockSpec((te, 2 * SC_PK_SUB, SC_ROW_LANE), lambda i: (i, 0, 0)),
        out_shape=jax.ShapeDtypeStruct((e, 2 * SC_PK_SUB, SC_ROW_LANE), I32),
        compiler_params=_cparams(("parallel",)), name="pack_expert_tables",
    )(u, v)


def _unpack_halves(x32):
    w = plsc.bitcast(x32, I32)
    return plsc.bitcast(w << 16, F32), plsc.bitcast(w & HI_MASK, F32)


def _tree_sum(xs):
    while len(xs) > 1:
        xs = [xs[i] + xs[i + 1] for i in range(0, len(xs), 2)]
    return xs[0]


def peer_experts_pk_sc(tab_uv, idx_flat, w_flat, hp, d):
    t = hp.shape[0]
    nsel = PEER_SEL
    g = SC_GROUP
    assert t % (SC_WORKERS * g) == 0 and d == 2 * SC_PK_SUB * SC_ROW_LANE
    tpw = t // SC_WORKERS
    groups = tpw // g
    heads = nsel // SC_LANES
    chunks = d // 32
    units = g * heads
    ring = SC_PK_RING
    assert units % ring == 0
    row_buf = pltpu.VMEM((SC_LANES, 2 * SC_PK_SUB, SC_ROW_LANE), I32)

    def row_words(rows, r, wc, sub0):
        per = SC_ROW_LANE // SC_LANES
        return plsc.bitcast(
            rows[r, sub0 + wc // per, pl.ds(pl.multiple_of((wc % per) * SC_LANES, SC_LANES), SC_LANES)], BF16)

    def ring_loop(n_units, start, wait, compute):
        for u in range(ring - 1):
            start(u, u)

        @pl.loop(0, n_units, step=ring)
        def _(uu):
            for b in range(ring):
                u = uu + b
                nxt = u + (ring - 1)

                @pl.when(nxt < n_units)
                def _():
                    start(nxt, (b + ring - 1) % ring)

                wait(u, b)
                compute(u, b)

    @functools.partial(
        pl.kernel, mesh=_sc_mesh(),
        out_type=jax.ShapeDtypeStruct((t, d), F32),
        scratch_types=[
            pltpu.VMEM((g * nsel,), I32),
            pltpu.VMEM((g * nsel,), F32),
            pltpu.VMEM((g, d // 2), I32),
            pltpu.VMEM((g, d), F32),
            pltpu.VMEM((SC_LANES * SC_LANES,), F32),
            [row_buf] * ring,
            [pltpu.SemaphoreType.DMA] * ring,
        ],
        compiler_params=_sc_params(),
        name="peer_experts_pk_sc",
    )
    def k(tab_hbm, idx_hbm, w_hbm, h_hbm, out_hbm, idx_v, coef_v, h_v, y_v, red_v, rows, sems):
        wid = _sc_worker_id()
        lane = lax.iota(I32, SC_LANES)

        def copy(u, slot):
            ids = idx_v.at[pl.ds(_sc_unit_off(u), SC_LANES)]
            return pltpu.make_async_copy(tab_hbm.at[ids], rows[slot], sems[slot])

        def dots(u, slot):
            tt = u // heads

            def body(cp, accs):
                out = []
                hv = [plsc.bitcast(h_v[tt, pl.ds(pl.multiple_of((2 * cp + i) * SC_LANES, SC_LANES), SC_LANES)], BF16)
                      for i in range(2)]
                for r in range(SC_LANES):
                    pr = (row_words(rows[slot], r, 2 * cp, 0) * hv[0]
                          + row_words(rows[slot], r, 2 * cp + 1, 0) * hv[1])
                    lo, hi = _unpack_halves(pr)
                    out.append(accs[r] + lo + hi)
                return tuple(out)

            accs = lax.fori_loop(0, chunks // 2, body,
                                 tuple(jnp.zeros((SC_LANES,), F32) for _ in range(SC_LANES)))
            for r in range(SC_LANES):
                red_v[pl.ds(r * SC_LANES, SC_LANES)] = accs[r]
            act = _tree_sum([plsc.load_gather(red_v, [lane * SC_LANES + j]) for j in range(SC_LANES)])
            sl = pl.ds(_sc_unit_off(u), SC_LANES)
            coef_v[sl] = coef_v[sl] * _gelu_tanh(act)

        def combine(u, slot):
            tt = u // heads
            first = (u % heads) == 0
            cb = []
            for r in range(SC_LANES):
                c = plsc.load_gather(coef_v, [jnp.full((SC_LANES,), u * SC_LANES + r, I32)])
                cb.append(plsc.pack(c, c, format=plsc.PackFormat.INTERLEAVED))

            @plsc.parallel_loop(0, chunks, unroll=2)
            def _(wc):
                lo, hi = _unpack_halves(
                    _tree_sum([cb[r] * row_words(rows[slot], r, wc, SC_PK_SUB) for r in range(SC_LANES)]))
                for half, val in ((0, lo), (1, hi)):
                    sl = pl.ds(pl.multiple_of(half * (d // 2) + wc * SC_LANES, SC_LANES), SC_LANES)
                    y_v[tt, sl] = val + jnp.where(first, 0.0, y_v[tt, sl])

        def unit(u, slot):
            dots(u, slot)
            combine(u, slot)

        @pl.loop(0, groups)
        def _(gi):
            base = wid * tpw + gi * g
            pltpu.sync_copy(idx_hbm.at[pl.ds(base * nsel, g * nsel)], idx_v)
            pltpu.sync_copy(w_hbm.at[pl.ds(base * nsel, g * nsel)], coef_v)
            pltpu.sync_copy(h_hbm.at[pl.ds(base, g)], h_v)
            ring_loop(units, lambda u, s: copy(u, s).start(), lambda u, s: copy(u, s).wait(), unit)
            pltpu.sync_copy(y_v, out_hbm.at[pl.ds(base, g)])

    return k(tab_uv, idx_flat, w_flat, hp)


def kernel(x, mem, rel_bias, ln_mix, w_in, hg_lower, hg_norm, w_up_a, w_up_b, w_out, ln_cross, ln_mem, wq_x, wk_x, wv_x, wo_x, ln_ffn, peer_query, peer_subkeys, peer_u, peer_v, ln_final):
    b, s, d = x.shape
    depth = w_in.shape[0]
    assert depth == 1, "the residual after PEER is fused into the final norm"
    assert s % MB_BLOCK == 0 and s % HG_CHUNK == 0 and s % (PEER_SLICES * SC_WORKERS * SC_GROUP) == 0
    nb = s // MB_BLOCK
    row = lambda a: a.reshape(1, -1).astype(F32)
    lb_all = jnp.cumsum(jax.nn.softmax(hg_lower.astype(F32), axis=0), axis=0)
    bias = moba_bias_tiles(rel_bias)
    n_hg = 4 * HG_WIDTH
    n_qk = 2 * MB_WIDTH
    n_mb = 3 * MB_WIDTH
    l = 0
    w = w_in[l].astype(BF16)
    w_hg, w_qk, w_vt, w_g = w[:, :n_hg], w[:, n_hg:n_hg + n_qk], w[:, n_hg + n_qk:n_hg + n_mb].T, w[:, n_hg + n_mb:]
    wa, wb, wo = w_up_a[l].astype(BF16), w_up_b[l].astype(BF16), w_out[l].astype(BF16)
    wqx, wox = wq_x[l].astype(BF16), wo_x[l].astype(BF16)
    wpq, sk = peer_query[l].astype(BF16), peer_subkeys[l].astype(F32)
    tab_uv = pack_expert_tables(peer_u[l].astype(F32), peer_v[l].astype(F32))
    kx, vx = mem_kv(mem, row(ln_mem[l]), wk_x[l].astype(BF16), wv_x[l].astype(BF16))

    outs = []
    for bi in range(b):
        x2d = x[bi]
        p0, pqk, km, vt, pg = in_proj(x2d, row(ln_mix[l]), w_hg, w_qk, w_vt, w_g)
        ya = hgrn2(p0, row(lb_all[l]), row(hg_norm[l]), 1, s)
        km = km.reshape(1, nb, MB_WIDTH)
        ts = s // PEER_SLICES
        for tok0 in range(0, s, ts):
            yb = moba_attention(pqk, vt, km, bias, 1, s, tok0 // MB_BLOCK, ts // MB_BLOCK)
            xs = mix_cross(x2d, ya, yb, pg, wa, wb, wo, row(ln_cross[l]), wqx, kx[bi:bi + 1], vx[bi:bi + 1], wox, tok0)
            hp, eidx, wts = peer_route(xs, row(ln_ffn[l]), wpq, sk, 0, ts)
            y = peer_experts_pk_sc(tab_uv, eidx.reshape(ts * PEER_SEL), wts.reshape(ts * PEER_SEL), hp, d)
            outs.append((xs, y, bi * s + tok0))
    out = None
    for xs, y, row0 in sorted(outs, key=lambda e: (-(e[2] // s), e[2])):
        out = final_norm_into(out, xs, y, row(ln_final), row0, b * s)
    return out.reshape(b, s, d)
```

```python
import functools
import math

import jax
import jax.numpy as jnp
from jax import lax
from jax.experimental import pallas as pl
from jax.experimental.pallas import tpu as pltpu
from jax.experimental.pallas import tpu_sc as plsc

F32 = jnp.float32
BF16 = jnp.bfloat16
I32 = jnp.int32
EPS = 1e-6
NEG_INF = float("-inf")

HG_HEADS = 4
HG_D = 128
HG_WIDTH = HG_HEADS * HG_D
HG_CHUNK = 64
HG_SUB = 16
MB_HEADS = 8
MB_DH = 64
MB_WIDTH = MB_HEADS * MB_DH
MB_BLOCK = 256
MB_TOPK = 3
MB_BIAS_TILES = 8
REL_BUCKETS = 32
REL_MAX_DIST = 2048
X_HEADS = 4
PEER_HEADS = 8
PEER_NKEYS = 128
PEER_TOPK = 16
PEER_HALF = 128
PEER_SEL = PEER_HEADS * PEER_TOPK
PEER_SLICES = 4

VMEM_LIMIT = 56 * 1024 * 1024


def _cparams(sem):
    return pltpu.CompilerParams(dimension_semantics=sem, vmem_limit_bytes=VMEM_LIMIT)


def _rms(x, g):
    ms = jnp.mean(x * x, axis=-1, keepdims=True)
    return x * lax.rsqrt(ms + EPS) * g


def _in_proj_kernel(x_ref, g_ref, w0_ref, w1_ref, wvt_ref, w2_ref, o0_ref, o1_ref, okm_ref, ovt_ref, o2_ref):
    h = _rms(x_ref[...], g_ref[...]).astype(BF16)
    o0_ref[...] = jnp.dot(h, w0_ref[...], preferred_element_type=F32)
    qk = jnp.dot(h, w1_ref[...], preferred_element_type=F32)
    o1_ref[...] = qk.astype(BF16)
    okm_ref[0] = jnp.mean(qk[:, MB_WIDTH:], axis=0, keepdims=True)
    vt = lax.dot_general(wvt_ref[...], h, (((1,), (1,)), ((), ())), preferred_element_type=F32).astype(BF16)
    for hd in range(MB_HEADS):
        ovt_ref[0, hd * MB_VROWS:hd * MB_VROWS + MB_DH, :] = vt[hd * MB_DH:(hd + 1) * MB_DH]
        ovt_ref[0, hd * MB_VROWS + MB_DH:(hd + 1) * MB_VROWS, :] = jnp.ones((MB_ONES, vt.shape[1]), BF16)
    o2_ref[...] = jnp.dot(h, w2_ref[...], preferred_element_type=F32).astype(BF16)


def in_proj(x2d, g, w0, w1, wvt, w2):
    t, d = x2d.shape
    tm = MB_BLOCK
    assert wvt.shape[0] == MB_WIDTH and w1.shape[1] == 2 * MB_WIDTH
    n0, n1, nv, n2 = w0.shape[1], w1.shape[1], MB_VT_ROWS, w2.shape[1]
    full = lambda a: pl.BlockSpec(a.shape, lambda i: (0, 0))
    return pl.pallas_call(
        _in_proj_kernel,
        grid=(t // tm,),
        in_specs=[pl.BlockSpec((tm, d), lambda i: (i, 0)), full(g), full(w0), full(w1), full(wvt), full(w2)],
        out_specs=[pl.BlockSpec((tm, n0), lambda i: (i, 0)),
                   pl.BlockSpec((tm, n1), lambda i: (i, 0)),
                   pl.BlockSpec((1, 1, MB_WIDTH), lambda i: (i, 0, 0)),
                   pl.BlockSpec((1, nv, tm), lambda i: (i, 0, 0)),
                   pl.BlockSpec((tm, n2), lambda i: (i, 0))],
        out_shape=[jax.ShapeDtypeStruct((t, n0), F32),
                   jax.ShapeDtypeStruct((t, n1), BF16),
                   jax.ShapeDtypeStruct((t // tm, 1, MB_WIDTH), F32),
                   jax.ShapeDtypeStruct((t // tm, nv, tm), BF16),
                   jax.ShapeDtypeStruct((t, n2), BF16)],
        compiler_params=_cparams(("parallel",)),
        name="in_proj",
    )(x2d, g, w0, w1, wvt, w2)


def _hgrn_kernel(q_ref, f_ref, i_ref, g_ref, lb_ref, gain_ref, o_ref, st_ref):
    c = pl.program_id(1)

    @pl.when(c == 0)
    def _():
        st_ref[...] = jnp.zeros_like(st_ref)

    C, S = HG_CHUNK, HG_SUB
    row = lax.broadcasted_iota(I32, (C, C), 0)
    col = lax.broadcasted_iota(I32, (C, C), 1)
    tril = (row >= col).astype(F32)
    t_iota = lax.broadcasted_iota(I32, (S, 1), 0)

    for h in range(HG_HEADS):
        sl = slice(h * HG_D, (h + 1) * HG_D)
        q = q_ref[:, sl]
        v = i_ref[:, sl]
        lb = lb_ref[:, sl]
        f = lb + (1.0 - lb) * jax.nn.sigmoid(f_ref[:, sl])
        lf = jnp.log(f)
        k = 1.0 - f
        b = jnp.dot(tril, lf, precision=lax.Precision.HIGHEST, preferred_element_type=F32)
        st = st_ref[h]
        vb = v.astype(BF16)
        qd = (q * jnp.exp(b)).astype(BF16)
        o_inter = lax.dot_general(qd, st.astype(BF16), (((1,), (1,)), ((), ())),
                                  preferred_element_type=F32)
        outs = []
        for i in range(C // S):
            r0 = i * S
            qi = q[r0:r0 + S]
            ki = k[r0:r0 + S]
            bi = b[r0:r0 + S]
            vi = v[r0:r0 + S]
            oi = o_inter[r0:r0 + S]
            if i > 0:
                bs = b[r0 - 1:r0]
                qh = (qi * jnp.exp(bi - bs)).astype(BF16)
                kh = (k[:r0] * jnp.exp(bs - b[:r0])).astype(BF16)
                a = lax.dot_general(qh, kh, (((1,), (1,)), ((), ())), preferred_element_type=F32)
                oi = oi + jnp.dot(a.astype(BF16), vb[:r0], preferred_element_type=F32)
            half = S // 2
            o_half = [oi[:half], oi[half:]]
            for s in range(S):
                for hf in range(s // half, 2):
                    rows = slice(hf * half, (hf + 1) * half)
                    dec = jnp.exp(jnp.minimum(bi[rows] - bi[s:s + 1], 0.0))
                    a_s = jnp.sum(qi[rows] * ki[s:s + 1] * dec, axis=-1, keepdims=True)
                    a_s = jnp.where(t_iota[rows] >= s, a_s, 0.0)
                    o_half[hf] = o_half[hf] + a_s * vi[s:s + 1]
            outs.extend(o_half)
        o = jnp.concatenate(outs, axis=0)
        b_end = b[C - 1:C]
        kd = (k * jnp.exp(b_end - b)).astype(BF16)
        upd = lax.dot_general(vb, kd, (((0,), (0,)), ((), ())), preferred_element_type=F32)
        st_ref[h] = st * jnp.exp(b_end) + upd
        o = o * lax.rsqrt(jnp.mean(o * o, axis=-1, keepdims=True) + EPS)
        g = g_ref[:, sl]
        o_ref[:, sl] = (o * gain_ref[:, sl] * (g * jax.nn.sigmoid(g))).astype(o_ref.dtype)


def hgrn2(p0, lb, gain, batch, seq):
    t = p0.shape[0]
    nc = seq // HG_CHUNK
    w = HG_WIDTH

    def col(j):
        return pl.BlockSpec((HG_CHUNK, w), lambda b, c, j=j: (b * nc + c, j))

    return pl.pallas_call(
        _hgrn_kernel,
        grid=(batch, nc),
        in_specs=[col(0), col(1), col(2), col(3),
                  pl.BlockSpec((1, w), lambda b, c: (0, 0)),
                  pl.BlockSpec((1, w), lambda b, c: (0, 0))],
        out_specs=pl.BlockSpec((HG_CHUNK, w), lambda b, c: (b * nc + c, 0)),
        out_shape=jax.ShapeDtypeStruct((t, w), BF16),
        scratch_shapes=[pltpu.VMEM((HG_HEADS, HG_D, HG_D), F32)],
        compiler_params=_cparams(("parallel", "arbitrary")),
        name="hgrn2",
    )(p0, p0, p0, p0, lb, gain)


MB_PAIR = 4
MB_PW = MB_PAIR * MB_DH
MB_LG = 128
MB_ONES = 16
MB_VROWS = MB_DH + MB_ONES
MB_VT_ROWS = MB_HEADS * MB_VROWS


def _moba_kernel(q_ref, k_ref, vt_ref, km_ref, bias_ref, o_ref, *scratch, qb0):
    m_ref, l_ref, al_ref, acc_ref, msk_ref, s_ref, p_ref = (
        scratch[i * MB_PAIR:(i + 1) * MB_PAIR] for i in range(7))
    qi = pl.program_id(2) + qb0
    nb = km_ref.shape[0]
    blk = MB_BLOCK
    heads = range(MB_PAIR)
    grp = lambda hh: slice((hh // 2) * MB_LG, (hh // 2 + 1) * MB_LG)
    q = q_ref[...]
    lane = lax.broadcasted_iota(I32, (blk, MB_LG), 1)
    in_head = [(lane < MB_DH) if hh % 2 == 0 else (lane >= MB_DH) for hh in heads]
    qs = q * jnp.asarray(MB_DH ** -0.5, BF16)
    nt = (((1,), (1,)), ((), ()))
    qf = q.astype(F32)
    qht = [jnp.where(in_head[hh], qs[:, grp(hh)].astype(F32), 0.0).T.astype(BF16) for hh in heads]

    n_io = lax.broadcasted_iota(I32, (nb, blk), 0)
    for hh in heads:
        gate = lax.dot_general(km_ref[:, grp(hh)], jnp.where(in_head[hh], qf[:, grp(hh)], 0.0), nt,
                               precision=lax.Precision.HIGHEST, preferred_element_type=F32)
        gate = jnp.where(n_io < qi, gate, NEG_INF)
        chosen = n_io < 0
        for _ in range(MB_TOPK):
            mx = jnp.max(gate, axis=0, keepdims=True)
            ix = jnp.min(jnp.where(gate == mx, n_io, nb), axis=0, keepdims=True)
            hit = n_io == ix
            chosen = chosen | (hit & (mx > NEG_INF))
            gate = jnp.where(hit, NEG_INF, gate)
        msk_ref[hh][...] = jnp.where(chosen, 0.0, NEG_INF)

    vrows = lambda hh: slice(hh * MB_VROWS, (hh + 1) * MB_VROWS)

    def pv_stage(blk_idx):
        vtb = vt_ref[blk_idx]
        r = [jnp.dot(vtb[vrows(hh)], p_ref[hh][...], preferred_element_type=F32) for hh in heads]
        al = [al_ref[hh][...] for hh in heads]
        a_new = [al[hh] * acc_ref[hh][...] + r[hh][:MB_DH] for hh in heads]
        l_new = [al[hh] * l_ref[hh][...] + r[hh][MB_DH:MB_DH + 1] for hh in heads]
        return a_new, l_new

    def store_pv(a_new, l_new):
        for hh in heads:
            acc_ref[hh][...] = a_new[hh]
            l_ref[hh][...] = l_new[hh]

    def softmax_stage():
        s = [s_ref[hh][...] for hh in heads]
        m_old = [m_ref[hh][...] for hh in heads]
        m_new = [jnp.maximum(m_old[hh], jnp.max(s[hh], axis=0, keepdims=True)) for hh in heads]
        alpha = [jnp.exp(m_old[hh] - m_new[hh]) for hh in heads]
        p = [jnp.exp((s[hh] - m_new[hh]).astype(BF16)) for hh in heads]
        return p, alpha, m_new

    def store_softmax(p, alpha, m_new):
        for hh in heads:
            p_ref[hh][...] = p[hh]
            al_ref[hh][...] = alpha[hh]
            m_ref[hh][...] = m_new[hh]

    k_own = k_ref[pl.ds(pl.multiple_of(qi * blk, blk), blk), :]
    key_io = lax.broadcasted_iota(I32, (blk, blk), 0)
    qry_io = lax.broadcasted_iota(I32, (blk, blk), 1)
    for hh in heads:
        s = jnp.dot(k_own[:, grp(hh)], qht[hh], preferred_element_type=F32) + bias_ref[hh, 0]
        s_ref[hh][...] = jnp.where(key_io <= qry_io, s, NEG_INF)
        m_ref[hh][...] = jnp.full((1, blk), NEG_INF, F32)
        l_ref[hh][...] = jnp.zeros((1, blk), F32)
        al_ref[hh][...] = jnp.ones((1, blk), F32)
        acc_ref[hh][...] = jnp.zeros((MB_DH, blk), F32)
        p_ref[hh][...] = jnp.zeros((blk, blk), BF16)

    def step(i, carry, far):
        pv = pv_stage(jnp.where(i <= 1, qi, i - 2))
        sm = softmax_stage()
        kn = k_ref[pl.ds(pl.multiple_of(i * blk, blk), blk), :]
        if far:
            row = [msk_ref[hh][pl.ds(i, 1), :] + bias_ref[hh, MB_BIAS_TILES - 1, 0:1, 0:1] for hh in heads]
            s_next = [jnp.dot(kn[:, grp(hh)], qht[hh], preferred_element_type=F32) + row[hh] for hh in heads]
        else:
            d = qi - i
            s_next = [jnp.dot(kn[:, grp(hh)], qht[hh], preferred_element_type=F32)
                      + bias_ref[hh, d] + msk_ref[hh][pl.ds(i, 1), :] for hh in heads]
        store_pv(*pv)
        for hh in heads:
            s_ref[hh][...] = s_next[hh]
        store_softmax(*sm)
        return carry

    n_far = jnp.maximum(qi - (MB_BIAS_TILES - 2), 0)
    lax.fori_loop(0, n_far, functools.partial(step, far=True), 0)
    lax.fori_loop(n_far, qi, functools.partial(step, far=False), 0)
    pv = pv_stage(jnp.where(qi <= 1, qi, qi - 2))
    sm = softmax_stage()
    store_pv(*pv)
    store_softmax(*sm)
    a_fin, l_fin = pv_stage(jnp.where(qi == 0, qi, qi - 1))
    out_t = jnp.concatenate([a_fin[hh] / l_fin[hh] for hh in heads], axis=0)
    o_ref[...] = out_t.T.astype(o_ref.dtype)


def moba_attention(pqk, vt, km, bias, batch, seq, qb0=0, nqb=None):
    nb = seq // MB_BLOCK
    nqb = nb if nqb is None else nqb
    t = batch * nqb * MB_BLOCK
    groups = MB_WIDTH // MB_PW
    return pl.pallas_call(
        functools.partial(_moba_kernel, qb0=qb0),
        grid=(batch, groups, nqb),
        in_specs=[
            pl.BlockSpec((MB_BLOCK, MB_PW), lambda b, j, i: (b * nb + qb0 + i, j)),
            pl.BlockSpec((seq, MB_PW), lambda b, j, i: (b, groups + j)),
            pl.BlockSpec((nb, MB_PAIR * MB_VROWS, MB_BLOCK), lambda b, j, i: (b, j, 0)),
            pl.BlockSpec((None, nb, MB_PW), lambda b, j, i: (b, 0, j)),
            pl.BlockSpec((MB_PAIR, MB_BIAS_TILES, MB_BLOCK, MB_BLOCK), lambda b, j, i: (j, 0, 0, 0)),
        ],
        out_specs=pl.BlockSpec((MB_BLOCK, MB_PW), lambda b, j, i: (b * nqb + i, j)),
        out_shape=jax.ShapeDtypeStruct((t, MB_WIDTH), BF16),
        scratch_shapes=(
            [pltpu.VMEM((1, MB_BLOCK), F32)] * (3 * MB_PAIR)
            + [pltpu.VMEM((MB_DH, MB_BLOCK), F32)] * MB_PAIR
            + [pltpu.VMEM((nb, MB_BLOCK), F32)] * MB_PAIR
            + [pltpu.VMEM((MB_BLOCK, MB_BLOCK), F32)] * MB_PAIR
            + [pltpu.VMEM((MB_BLOCK, MB_BLOCK), BF16)] * MB_PAIR
        ),
        compiler_params=_cparams(("parallel", "parallel", "arbitrary")),
        name="moba_attn",
    )(pqk, pqk, vt, km, bias)


def _t5_bucket(dist):
    max_exact = REL_BUCKETS // 2
    scaled = jnp.log(jnp.maximum(dist, 1).astype(F32) / max_exact) / math.log(REL_MAX_DIST / max_exact)
    large = jnp.minimum(max_exact + (scaled * (REL_BUCKETS - max_exact)).astype(I32), REL_BUCKETS - 1)
    return jnp.where(dist < max_exact, dist, large)


def moba_bias_tiles(rel_bias):
    blk = MB_BLOCK
    span = 2 * blk - 1
    x = jnp.arange(span) - (blk - 1)
    dist = jnp.maximum(jnp.arange(MB_BIAS_TILES)[:, None] * blk + x[None, :], 0)
    w = rel_bias.astype(F32).T[:, _t5_bucket(dist)]
    h = w.shape[0]
    wp = jnp.pad(w, ((0, 0), (0, 0), (0, 1)))
    a = jnp.broadcast_to(wp[:, :, None, :], (h, MB_BIAS_TILES, blk, span + 1))
    a = a.reshape(h, MB_BIAS_TILES, blk * (span + 1))[:, :, :blk * span]
    return a.reshape(h, MB_BIAS_TILES, blk, span)[:, :, :, blk - 1:]


def _mix_kernel(x_ref, ya_ref, yb_ref, ga_ref, gb_ref, wa_ref, wb_ref, wo_ref, o_ref):
    za = jnp.dot(ya_ref[...], wa_ref[...], preferred_element_type=F32)
    zb = jnp.dot(yb_ref[...], wb_ref[...], preferred_element_type=F32)
    z = jax.nn.sigmoid(ga_ref[...].astype(F32)) * za + jax.nn.sigmoid(gb_ref[...].astype(F32)) * zb
    o_ref[...] = x_ref[...] + jnp.dot(z.astype(BF16), wo_ref[...], preferred_element_type=F32)


def _mem_kv_kernel(m_ref, g_ref, wk_ref, wv_ref, k_ref, v_ref):
    mn = _rms(m_ref[...], g_ref[...]).astype(BF16)
    k_ref[...] = jnp.dot(mn, wk_ref[...], preferred_element_type=F32).astype(BF16)
    v_ref[...] = jnp.dot(mn, wv_ref[...], preferred_element_type=F32).astype(BF16)


def mem_kv(mem, g, wk, wv):
    b, m, d = mem.shape
    spec = pl.BlockSpec((None, m, d), lambda i: (i, 0, 0))
    wspec = pl.BlockSpec((d, d), lambda i: (0, 0))
    return pl.pallas_call(
        _mem_kv_kernel,
        grid=(b,),
        in_specs=[spec, pl.BlockSpec((1, d), lambda i: (0, 0)), wspec, wspec],
        out_specs=[spec, spec],
        out_shape=[jax.ShapeDtypeStruct((b, m, d), BF16)] * 2,
        compiler_params=_cparams(("parallel",)),
        name="mem_kv",
    )(mem, g, wk, wv)


def _cross_kernel(x_ref, g_ref, wq_ref, k_ref, v_ref, wo_ref, o_ref):
    x = x_ref[...]
    d = x.shape[1]
    dh = d // X_HEADS
    h = _rms(x, g_ref[...]).astype(BF16)
    q = (jnp.dot(h, wq_ref[...], preferred_element_type=F32) * (dh ** -0.5)).astype(BF16)
    outs = []
    for hh in range(X_HEADS):
        sl = slice(hh * dh, (hh + 1) * dh)
        s = lax.dot_general(q[:, sl], k_ref[:, sl], (((1,), (1,)), ((), ())),
                            preferred_element_type=F32)
        p = jnp.exp(s - jnp.max(s, axis=1, keepdims=True))
        l = jnp.sum(p, axis=1, keepdims=True)
        o = jnp.dot(p.astype(BF16), v_ref[:, sl], preferred_element_type=F32) / l
        outs.append(o.astype(BF16))
    o = jnp.concatenate(outs, axis=1)
    o_ref[...] = x + jnp.dot(o, wo_ref[...], preferred_element_type=F32)


def _mix_cross_kernel(x_ref, ya_ref, yb_ref, ga_ref, gb_ref, wa_ref, wb_ref, wo_ref,
                      g_ref, wq_ref, k_ref, v_ref, wox_ref, o_ref, x1_ref):
    _mix_kernel(x_ref, ya_ref, yb_ref, ga_ref, gb_ref, wa_ref, wb_ref, wo_ref, x1_ref)
    _cross_kernel(x1_ref, g_ref, wq_ref, k_ref, v_ref, wox_ref, o_ref)


def mix_cross(x2d, ya, yb, pg, wa, wb, wo, g, wq, kx, vx, wox, tok0, tm=512):
    t = yb.shape[0]
    assert t % tm == 0 and tok0 % tm == 0
    d = x2d.shape[1]
    w = ya.shape[1]
    m = kx.shape[1]
    b0 = tok0 // tm
    const = lambda a: pl.BlockSpec(a.shape, lambda i: (0,) * a.ndim)
    kv = pl.BlockSpec((None, m, d), lambda i: (0, 0, 0))
    return pl.pallas_call(
        _mix_cross_kernel,
        grid=(t // tm,),
        in_specs=[
            pl.BlockSpec((tm, d), lambda i: (b0 + i, 0)),
            pl.BlockSpec((tm, w), lambda i: (b0 + i, 0)),
            pl.BlockSpec((tm, w), lambda i: (i, 0)),
            pl.BlockSpec((tm, d), lambda i: (b0 + i, 0)),
            pl.BlockSpec((tm, d), lambda i: (b0 + i, 1)),
            const(wa), const(wb), const(wo), const(g), const(wq), kv, kv, const(wox),
        ],
        out_specs=pl.BlockSpec((tm, d), lambda i: (i, 0)),
        out_shape=jax.ShapeDtypeStruct((t, d), F32),
        scratch_shapes=[pltpu.VMEM((tm, d), F32)],
        compiler_params=_cparams(("parallel",)),
        name="mix_cross",
    )(x2d, ya, yb, pg, pg, wa, wb, wo, g, wq, kx, vx, wox)


def _topk_rows(sc, k):
    n = sc.shape[0]
    io = lax.broadcasted_iota(I32, sc.shape, 0).astype(F32)
    vals, ids = [], []
    for _ in range(k):
        m = jnp.max(sc, axis=0, keepdims=True)
        ix = jnp.argmax(sc, axis=0, keepdims=True).astype(F32)
        vals.append(m)
        ids.append(ix)
        sc = jnp.where(io == ix, NEG_INF, sc)
    return jnp.concatenate(vals, axis=0), jnp.concatenate(ids, axis=0).astype(I32)


def _pack_bf16_halves(h):
    bits = lax.bitcast_convert_type(h, I32)
    r = bits + 0x7FFF + (lax.shift_right_logical(bits, 16) & 1)
    half = h.shape[1] // 2
    return lax.shift_right_logical(r[:, :half], 16) | (r[:, half:] & HI_MASK)


def _route_kernel(x_ref, g_ref, wq_ref, sk_ref, hp_ref, idx_ref, w_ref, hb_ref, it_ref, wt_ref):
    p = pl.program_id(1)

    @pl.when(p == 0)
    def _():
        h = _rms(x_ref[...], g_ref[...])
        hp_ref[...] = _pack_bf16_halves(h)
        hb_ref[...] = h.astype(BF16)

    qh = jnp.dot(hb_ref[...], wq_ref[...], preferred_element_type=F32)
    tops = []
    for c in range(2):
        seg = qh[:, c * PEER_HALF:(c + 1) * PEER_HALF]
        sc = lax.dot_general(sk_ref[c], seg, (((1,), (1,)), ((), ())),
                             precision=lax.Precision.HIGHEST, preferred_element_type=F32)
        tops.append(_topk_rows(sc, PEER_TOPK))
    (s0, i0), (s1, i1) = tops
    k = PEER_TOPK
    sub = 8
    tm = s0.shape[1]
    r8 = lax.broadcasted_iota(I32, (sub, tm), 0)
    r16 = lax.broadcasted_iota(I32, (k, tm), 0)
    cand_b = [s0[0:1] + s1, s0[1:2] + s1[:sub]]
    cidx_b = [i0[0:1] * PEER_NKEYS + i1, i0[1:2] * PEER_NKEYS + i1[:sub]]
    pos_b = [r16, k + r8]
    for a in range(2, sub):
        keep = r8 < (k // (a + 1))
        cand_b.append(jnp.where(keep, s0[a:a + 1] + s1[:sub], NEG_INF))
        cidx_b.append(i0[a:a + 1] * PEER_NKEYS + i1[:sub])
        pos_b.append(a * k + r8)
    cand_b.append(s0[sub:] + s1[0:1])
    cidx_b.append(i0[sub:] * PEER_NKEYS + i1[0:1])
    pos_b.append((sub + r8) * k)
    cand = jnp.concatenate(cand_b, axis=0)
    cidx = jnp.concatenate(cidx_b, axis=0)
    pos = jnp.concatenate(pos_b, axis=0).astype(F32)
    vals, ids = [], []
    for _ in range(k):
        m = jnp.max(cand, axis=0, keepdims=True)
        px = jnp.min(jnp.where(cand == m, pos, float(k * k)), axis=0, keepdims=True)
        hit = pos == px
        vals.append(m)
        ids.append(jnp.sum(jnp.where(hit, cidx, 0), axis=0, keepdims=True))
        cand = jnp.where(hit, NEG_INF, cand)
    sf = jnp.concatenate(vals, axis=0)
    e = jnp.exp(sf - sf[0:1])
    rows = pl.ds(pl.multiple_of(p * PEER_TOPK, PEER_TOPK), PEER_TOPK)
    wt_ref[rows, :] = e / jnp.sum(e, axis=0, keepdims=True)
    it_ref[rows, :] = jnp.concatenate(ids, axis=0)

    @pl.when(p == pl.num_programs(1) - 1)
    def _():
        idx_ref[...] = it_ref[...].T
        w_ref[...] = wt_ref[...].T


def peer_route(x2d, g, wq, sk, tok0, t, tm=1024):
    assert t % tm == 0 and tok0 % tm == 0
    d = x2d.shape[1]
    ph = sk.shape[0]
    nsel = ph * PEER_TOPK
    blk0 = tok0 // tm
    return pl.pallas_call(
        _route_kernel,
        grid=(t // tm, ph),
        in_specs=[
            pl.BlockSpec((tm, d), lambda i, p: (blk0 + i, 0)),
            pl.BlockSpec((1, d), lambda i, p: (0, 0)),
            pl.BlockSpec((d, 2 * PEER_HALF), lambda i, p: (0, p)),
            pl.BlockSpec((None, 2, PEER_NKEYS, PEER_HALF), lambda i, p: (p, 0, 0, 0)),
        ],
        out_specs=[
            pl.BlockSpec((tm, d // 2), lambda i, p: (i, 0)),
            pl.BlockSpec((tm, nsel), lambda i, p: (i, 0)),
            pl.BlockSpec((tm, nsel), lambda i, p: (i, 0)),
        ],
        out_shape=[jax.ShapeDtypeStruct((t, d // 2), I32),
                   jax.ShapeDtypeStruct((t, nsel), I32),
                   jax.ShapeDtypeStruct((t, nsel), F32)],
        scratch_shapes=[pltpu.VMEM((tm, d), BF16),
                        pltpu.VMEM((nsel, tm), I32),
                        pltpu.VMEM((nsel, tm), F32)],
        compiler_params=_cparams(("parallel", "arbitrary")),
        name="peer_route",
    )(x2d, g, wq, sk)


def _final_kernel(x_ref, y_ref, g_ref, *rest):
    o_ref = rest[-1]
    o_ref[...] = _rms(x_ref[...] + y_ref[...], g_ref[...])


def final_norm_into(out, xs, y, g, row0, total, tm=512):
    t, d = y.shape
    assert t % tm == 0 and row0 % tm == 0 and total % tm == 0
    blk0 = row0 // tm
    spec = pl.BlockSpec((tm, d), lambda i: (i, 0))
    in_specs = [spec, spec, pl.BlockSpec((1, d), lambda i: (0, 0))]
    args = [xs, y, g]
    aliases = {}
    if out is not None:
        in_specs.append(pl.BlockSpec(memory_space=pl.ANY))
        args.append(out)
        aliases = {3: 0}
    return pl.pallas_call(
        _final_kernel, grid=(t // tm,),
        in_specs=in_specs,
        out_specs=pl.BlockSpec((tm, d), lambda i: (blk0 + i, 0)),
        out_shape=jax.ShapeDtypeStruct((total, d), F32),
        input_output_aliases=aliases,
        compiler_params=_cparams(("parallel",)), name="final_norm",
    )(*args)


SC_CORES = 2
SC_SUBCORES = 16
SC_WORKERS = SC_CORES * SC_SUBCORES
SC_LANES = 16
SC_GROUP = 32


def _sc_mesh():
    return plsc.VectorSubcoreMesh(core_axis_name="c", subcore_axis_name="s")


def _sc_params():
    return pltpu.CompilerParams(needs_layout_passes=False)


def _sc_worker_id():
    return lax.axis_index("s") * SC_CORES + lax.axis_index("c")


SC_ROW_LANE = 128


def _sc_unit_off(u):
    off = u * SC_LANES
    return off if isinstance(off, int) else pl.multiple_of(off, SC_LANES)


GELU_C0 = math.sqrt(2.0 / math.pi)
GELU_C1 = 0.044715


def _gelu_tanh(x):
    z = GELU_C0 * (x + GELU_C1 * (x * x * x))
    th = 1.0 - 2.0 / (jnp.exp(2.0 * z) + 1.0)
    return 0.5 * x * (1.0 + th)


SC_PK_RING = 4
SC_PK_SUB = 4
HI_MASK = -65536


def _pack_tables_kernel(u_ref, v_ref, o_ref):
    for part, ref in enumerate((u_ref, v_ref)):
        words = _pack_bf16_halves(ref[...])
        for sub in range(SC_PK_SUB):
            o_ref[:, part * SC_PK_SUB + sub, :] = words[:, sub * SC_ROW_LANE:(sub + 1) * SC_ROW_LANE]


def pack_expert_tables(u, v, te=512):
    e, d = u.shape
    assert d == 2 * SC_PK_SUB * SC_ROW_LANE
    spec = pl.BlockSpec((te, d), lambda i: (i, 0))
    return pl.pallas_call(
        _pack_tables_kernel, grid=(e // te,), in_specs=[spec, spec],
        out_specs=pl.BlockSpec((te, 2 * SC_PK_SUB, SC_ROW_LANE), lambda i: (i, 0, 0)),
        out_shape=jax.ShapeDtypeStruct((e, 2 * SC_PK_SUB, SC_ROW_LANE), I32),
        compiler_params=_cparams(("parallel",)), name="pack_expert_tables",
    )(u, v)


def _unpack_halves(x32):
    w = plsc.bitcast(x32, I32)
    return plsc.bitcast(w << 16, F32), plsc.bitcast(w & HI_MASK, F32)


def _tree_sum(xs):
    while len(xs) > 1:
        xs = [xs[i] + xs[i + 1] for i in range(0, len(xs), 2)]
    return xs[0]


def peer_experts_pk_sc(tab_uv, idx_flat, w_flat, hp, d):
    t = hp.shape[0]
    nsel = PEER_SEL
    assert t % SC_WORKERS == 0 and d == 2 * SC_PK_SUB * SC_ROW_LANE
    tpw = t // SC_WORKERS
    g = SC_GROUP if tpw % SC_GROUP == 0 else SC_GROUP // 2
    assert tpw % g == 0
    groups = tpw // g
    heads = nsel // SC_LANES
    chunks = d // 32
    units = g * heads
    ring = SC_PK_RING
    assert units % ring == 0
    row_buf = pltpu.VMEM((SC_LANES, 2 * SC_PK_SUB, SC_ROW_LANE), I32)

    def row_words(rows, r, wc, sub0):
        per = SC_ROW_LANE // SC_LANES
        return plsc.bitcast(
            rows[r, sub0 + wc // per, pl.ds(pl.multiple_of((wc % per) * SC_LANES, SC_LANES), SC_LANES)], BF16)

    def ring_loop(n_units, start, wait, compute):
        for u in range(ring - 1):
            start(u, u)

        @pl.loop(0, n_units, step=ring)
        def _(uu):
            for b in range(ring):
                u = uu + b
                nxt = u + (ring - 1)

                @pl.when(nxt < n_units)
                def _():
                    start(nxt, (b + ring - 1) % ring)

                wait(u, b)
                compute(u, b)

    @functools.partial(
        pl.kernel, mesh=_sc_mesh(),
        out_type=jax.ShapeDtypeStruct((t, d), F32),
        scratch_types=[
            pltpu.VMEM((g * nsel,), I32),
            pltpu.VMEM((g * nsel,), F32),
            pltpu.VMEM((g, d // 2), I32),
            pltpu.VMEM((g, d), F32),
            pltpu.VMEM((SC_LANES * SC_LANES,), F32),
            [row_buf] * ring,
            [pltpu.SemaphoreType.DMA] * ring,
        ],
        compiler_params=_sc_params(),
        name="peer_experts_pk_sc",
    )
    def k(tab_hbm, idx_hbm, w_hbm, h_hbm, out_hbm, idx_v, coef_v, h_v, y_v, red_v, rows, sems):
        wid = _sc_worker_id()
        lane = lax.iota(I32, SC_LANES)

        def copy(u, slot):
            ids = idx_v.at[pl.ds(_sc_unit_off(u), SC_LANES)]
            return pltpu.make_async_copy(tab_hbm.at[ids], rows[slot], sems[slot])

        def dots(u, slot):
            tt = u // heads

            def body(cp, accs):
                out = []
                hv = [plsc.bitcast(h_v[tt, pl.ds(pl.multiple_of((2 * cp + i) * SC_LANES, SC_LANES), SC_LANES)], BF16)
                      for i in range(2)]
                for r in range(SC_LANES):
                    pr = (row_words(rows[slot], r, 2 * cp, 0) * hv[0]
                          + row_words(rows[slot], r, 2 * cp + 1, 0) * hv[1])
                    lo, hi = _unpack_halves(pr)
                    out.append(accs[r] + lo + hi)
                return tuple(out)

            accs = lax.fori_loop(0, chunks // 2, body,
                                 tuple(jnp.zeros((SC_LANES,), F32) for _ in range(SC_LANES)))
            for r in range(SC_LANES):
                red_v[pl.ds(r * SC_LANES, SC_LANES)] = accs[r]
            act = _tree_sum([plsc.load_gather(red_v, [lane * SC_LANES + j]) for j in range(SC_LANES)])
            sl = pl.ds(_sc_unit_off(u), SC_LANES)
            coef_v[sl] = coef_v[sl] * _gelu_tanh(act)

        def combine(u, slot):
            tt = u // heads
            first = (u % heads) == 0
            cb = []
            for r in range(SC_LANES):
                c = plsc.load_gather(coef_v, [jnp.full((SC_LANES,), u * SC_LANES + r, I32)])
                cb.append(plsc.pack(c, c, format=plsc.PackFormat.INTERLEAVED))

            @plsc.parallel_loop(0, chunks, unroll=2)
            def _(wc):
                lo, hi = _unpack_halves(
                    _tree_sum([cb[r] * row_words(rows[slot], r, wc, SC_PK_SUB) for r in range(SC_LANES)]))
                for half, val in ((0, lo), (1, hi)):
                    sl = pl.ds(pl.multiple_of(half * (d // 2) + wc * SC_LANES, SC_LANES), SC_LANES)
                    y_v[tt, sl] = val + jnp.where(first, 0.0, y_v[tt, sl])

        def unit(u, slot):
            dots(u, slot)
            combine(u, slot)

        @pl.loop(0, groups)
        def _(gi):
            base = wid * tpw + gi * g
            pltpu.sync_copy(idx_hbm.at[pl.ds(base * nsel, g * nsel)], idx_v)
            pltpu.sync_copy(w_hbm.at[pl.ds(base * nsel, g * nsel)], coef_v)
            pltpu.sync_copy(h_hbm.at[pl.ds(base, g)], h_v)
            ring_loop(units, lambda u, s: copy(u, s).start(), lambda u, s: copy(u, s).wait(), unit)
            pltpu.sync_copy(y_v, out_hbm.at[pl.ds(base, g)])

    return k(tab_uv, idx_flat, w_flat, hp)


def kernel(x, mem, rel_bias, ln_mix, w_in, hg_lower, hg_norm, w_up_a, w_up_b, w_out, ln_cross, ln_mem, wq_x, wk_x, wv_x, wo_x, ln_ffn, peer_query, peer_subkeys, peer_u, peer_v, ln_final):
    b, s, d = x.shape
    depth = w_in.shape[0]
    assert depth == 1, "the residual after PEER is fused into the final norm"
    assert s % MB_BLOCK == 0 and s % HG_CHUNK == 0 and s % (PEER_SLICES * SC_WORKERS * SC_GROUP) == 0
    nb = s // MB_BLOCK
    row = lambda a: a.reshape(1, -1).astype(F32)
    lb_all = jnp.cumsum(jax.nn.softmax(hg_lower.astype(F32), axis=0), axis=0)
    bias = moba_bias_tiles(rel_bias)
    n_hg = 4 * HG_WIDTH
    n_qk = 2 * MB_WIDTH
    n_mb = 3 * MB_WIDTH
    l = 0
    w = w_in[l].astype(BF16)
    w_hg, w_qk, w_vt, w_g = w[:, :n_hg], w[:, n_hg:n_hg + n_qk], w[:, n_hg + n_qk:n_hg + n_mb].T, w[:, n_hg + n_mb:]
    wa, wb, wo = w_up_a[l].astype(BF16), w_up_b[l].astype(BF16), w_out[l].astype(BF16)
    wqx, wox = wq_x[l].astype(BF16), wo_x[l].astype(BF16)
    wpq, sk = peer_query[l].astype(BF16), peer_subkeys[l].astype(F32)
    tab_uv = pack_expert_tables(peer_u[l].astype(F32), peer_v[l].astype(F32))
    kx, vx = mem_kv(mem, row(ln_mem[l]), wk_x[l].astype(BF16), wv_x[l].astype(BF16))

    outs = []
    for bi in range(b):
        x2d = x[bi]
        p0, pqk, km, vt, pg = in_proj(x2d, row(ln_mix[l]), w_hg, w_qk, w_vt, w_g)
        ya = hgrn2(p0, row(lb_all[l]), row(hg_norm[l]), 1, s)
        km = km.reshape(1, nb, MB_WIDTH)
        sizes = [s // PEER_SLICES] * PEER_SLICES
        if bi == b - 1:
            first = SC_WORKERS * SC_GROUP // 2
            sizes = [first, sizes[0] - first] + sizes[1:]
        tok0 = 0
        for ts in sizes:
            yb = moba_attention(pqk, vt, km, bias, 1, s, tok0 // MB_BLOCK, ts // MB_BLOCK)
            xs = mix_cross(x2d, ya, yb, pg, wa, wb, wo, row(ln_cross[l]), wqx, kx[bi:bi + 1], vx[bi:bi + 1], wox, tok0)
            hp, eidx, wts = peer_route(xs, row(ln_ffn[l]), wpq, sk, 0, ts, math.gcd(ts, 1024))
            y = peer_experts_pk_sc(tab_uv, eidx.reshape(ts * PEER_SEL), wts.reshape(ts * PEER_SEL), hp, d)
            outs.append((xs, y, bi * s + tok0))
            tok0 += ts
    out = None
    for xs, y, row0 in sorted(outs, key=lambda e: (-(e[2] // s), e[2])):
        out = final_norm_into(out, xs, y, row(ln_final), row0, b * s)
    return out.reshape(b, s, d)
```

```python
import functools
import math

import jax
import jax.numpy as jnp
from jax import lax
from jax.experimental import pallas as pl
from jax.experimental.pallas import tpu as pltpu
from jax.experimental.pallas import tpu_sc as plsc

F32 = jnp.float32
BF16 = jnp.bfloat16
I32 = jnp.int32
EPS = 1e-6
NEG_INF = float("-inf")

HG_HEADS = 4
HG_D = 128
HG_WIDTH = HG_HEADS * HG_D
HG_CHUNK = 64
HG_SUB = 16
MB_HEADS = 8
MB_DH = 64
MB_WIDTH = MB_HEADS * MB_DH
MB_BLOCK = 256
MB_TOPK = 3
MB_BIAS_TILES = 8
REL_BUCKETS = 32
REL_MAX_DIST = 2048
X_HEADS = 4
PEER_HEADS = 8
PEER_NKEYS = 128
PEER_TOPK = 16
PEER_HALF = 128
PEER_SEL = PEER_HEADS * PEER_TOPK
PEER_SLICES = 4

VMEM_LIMIT = 56 * 1024 * 1024


def _cparams(sem):
    return pltpu.CompilerParams(dimension_semantics=sem, vmem_limit_bytes=VMEM_LIMIT)


def _rms(x, g):
    ms = jnp.mean(x * x, axis=-1, keepdims=True)
    return x * lax.rsqrt(ms + EPS) * g


def _in_proj_kernel(x_ref, g_ref, w0_ref, w1_ref, wvt_ref, w2_ref, o0_ref, o1_ref, okm_ref, ovt_ref, o2_ref):
    h = _rms(x_ref[...], g_ref[...]).astype(BF16)
    o0_ref[...] = jnp.dot(h, w0_ref[...], preferred_element_type=F32)
    qk = jnp.dot(h, w1_ref[...], preferred_element_type=F32)
    o1_ref[...] = qk.astype(BF16)
    okm_ref[0] = jnp.mean(qk[:, MB_WIDTH:], axis=0, keepdims=True)
    vt = lax.dot_general(wvt_ref[...], h, (((1,), (1,)), ((), ())), preferred_element_type=F32).astype(BF16)
    for hd in range(MB_HEADS):
        ovt_ref[0, hd * MB_VROWS:hd * MB_VROWS + MB_DH, :] = vt[hd * MB_DH:(hd + 1) * MB_DH]
        ovt_ref[0, hd * MB_VROWS + MB_DH:(hd + 1) * MB_VROWS, :] = jnp.ones((MB_ONES, vt.shape[1]), BF16)
    o2_ref[...] = jnp.dot(h, w2_ref[...], preferred_element_type=F32).astype(BF16)


def in_proj(x2d, g, w0, w1, wvt, w2):
    t, d = x2d.shape
    tm = MB_BLOCK
    assert wvt.shape[0] == MB_WIDTH and w1.shape[1] == 2 * MB_WIDTH
    n0, n1, nv, n2 = w0.shape[1], w1.shape[1], MB_VT_ROWS, w2.shape[1]
    full = lambda a: pl.BlockSpec(a.shape, lambda i: (0, 0))
    return pl.pallas_call(
        _in_proj_kernel,
        grid=(t // tm,),
        in_specs=[pl.BlockSpec((tm, d), lambda i: (i, 0)), full(g), full(w0), full(w1), full(wvt), full(w2)],
        out_specs=[pl.BlockSpec((tm, n0), lambda i: (i, 0)),
                   pl.BlockSpec((tm, n1), lambda i: (i, 0)),
                   pl.BlockSpec((1, 1, MB_WIDTH), lambda i: (i, 0, 0)),
                   pl.BlockSpec((1, nv, tm), lambda i: (i, 0, 0)),
                   pl.BlockSpec((tm, n2), lambda i: (i, 0))],
        out_shape=[jax.ShapeDtypeStruct((t, n0), F32),
                   jax.ShapeDtypeStruct((t, n1), BF16),
                   jax.ShapeDtypeStruct((t // tm, 1, MB_WIDTH), F32),
                   jax.ShapeDtypeStruct((t // tm, nv, tm), BF16),
                   jax.ShapeDtypeStruct((t, n2), BF16)],
        compiler_params=_cparams(("parallel",)),
        name="in_proj",
    )(x2d, g, w0, w1, wvt, w2)


def _hgrn_kernel(q_ref, f_ref, i_ref, g_ref, lb_ref, gain_ref, st0_ref, o_ref, stn_ref, st_ref):
    c = pl.program_id(0)

    @pl.when(c == 0)
    def _():
        st_ref[...] = st0_ref[...]

    C, S = HG_CHUNK, HG_SUB
    row = lax.broadcasted_iota(I32, (C, C), 0)
    col = lax.broadcasted_iota(I32, (C, C), 1)
    tril = (row >= col).astype(F32)
    t_iota = lax.broadcasted_iota(I32, (S, 1), 0)

    for h in range(HG_HEADS):
        sl = slice(h * HG_D, (h + 1) * HG_D)
        q = q_ref[:, sl]
        v = i_ref[:, sl]
        lb = lb_ref[:, sl]
        f = lb + (1.0 - lb) * jax.nn.sigmoid(f_ref[:, sl])
        lf = jnp.log(f)
        k = 1.0 - f
        b = jnp.dot(tril, lf, precision=lax.Precision.HIGHEST, preferred_element_type=F32)
        st = st_ref[h]
        vb = v.astype(BF16)
        qd = (q * jnp.exp(b)).astype(BF16)
        o_inter = lax.dot_general(qd, st.astype(BF16), (((1,), (1,)), ((), ())),
                                  preferred_element_type=F32)
        outs = []
        for i in range(C // S):
            r0 = i * S
            qi = q[r0:r0 + S]
            ki = k[r0:r0 + S]
            bi = b[r0:r0 + S]
            vi = v[r0:r0 + S]
            oi = o_inter[r0:r0 + S]
            if i > 0:
                bs = b[r0 - 1:r0]
                qh = (qi * jnp.exp(bi - bs)).astype(BF16)
                kh = (k[:r0] * jnp.exp(bs - b[:r0])).astype(BF16)
                a = lax.dot_general(qh, kh, (((1,), (1,)), ((), ())), preferred_element_type=F32)
                oi = oi + jnp.dot(a.astype(BF16), vb[:r0], preferred_element_type=F32)
            half = S // 2
            o_half = [oi[:half], oi[half:]]
            for s in range(S):
                for hf in range(s // half, 2):
                    rows = slice(hf * half, (hf + 1) * half)
                    dec = jnp.exp(jnp.minimum(bi[rows] - bi[s:s + 1], 0.0))
                    a_s = jnp.sum(qi[rows] * ki[s:s + 1] * dec, axis=-1, keepdims=True)
                    a_s = jnp.where(t_iota[rows] >= s, a_s, 0.0)
                    o_half[hf] = o_half[hf] + a_s * vi[s:s + 1]
            outs.extend(o_half)
        o = jnp.concatenate(outs, axis=0)
        b_end = b[C - 1:C]
        kd = (k * jnp.exp(b_end - b)).astype(BF16)
        upd = lax.dot_general(vb, kd, (((0,), (0,)), ((), ())), preferred_element_type=F32)
        st_ref[h] = st * jnp.exp(b_end) + upd
        o = o * lax.rsqrt(jnp.mean(o * o, axis=-1, keepdims=True) + EPS)
        g = g_ref[:, sl]
        o_ref[:, sl] = (o * gain_ref[:, sl] * (g * jax.nn.sigmoid(g))).astype(o_ref.dtype)

    @pl.when(c == pl.num_programs(0) - 1)
    def _():
        stn_ref[...] = st_ref[...]


def hgrn2(p0, lb, gain, state, tok0, t):
    assert t % HG_CHUNK == 0 and tok0 % HG_CHUNK == 0
    nc = t // HG_CHUNK
    c0 = tok0 // HG_CHUNK
    w = HG_WIDTH

    def col(j):
        return pl.BlockSpec((HG_CHUNK, w), lambda c, j=j: (c0 + c, j))

    st_spec = pl.BlockSpec(state.shape, lambda c: (0, 0, 0))
    return pl.pallas_call(
        _hgrn_kernel,
        grid=(nc,),
        in_specs=[col(0), col(1), col(2), col(3),
                  pl.BlockSpec((1, w), lambda c: (0, 0)),
                  pl.BlockSpec((1, w), lambda c: (0, 0)),
                  st_spec],
        out_specs=[pl.BlockSpec((HG_CHUNK, w), lambda c: (c, 0)), st_spec],
        out_shape=[jax.ShapeDtypeStruct((t, w), BF16), jax.ShapeDtypeStruct(state.shape, F32)],
        scratch_shapes=[pltpu.VMEM((HG_HEADS, HG_D, HG_D), F32)],
        compiler_params=_cparams(("arbitrary",)),
        name="hgrn2",
    )(p0, p0, p0, p0, lb, gain, state)


MB_PAIR = 4
MB_PW = MB_PAIR * MB_DH
MB_LG = 128
MB_ONES = 16
MB_VROWS = MB_DH + MB_ONES
MB_VT_ROWS = MB_HEADS * MB_VROWS


def _moba_kernel(q_ref, k_ref, vt_ref, km_ref, bias_ref, o_ref, *scratch, qb0):
    m_ref, l_ref, al_ref, acc_ref, msk_ref, s_ref, p_ref = (
        scratch[i * MB_PAIR:(i + 1) * MB_PAIR] for i in range(7))
    qi = pl.program_id(2) + qb0
    nb = km_ref.shape[0]
    blk = MB_BLOCK
    heads = range(MB_PAIR)
    grp = lambda hh: slice((hh // 2) * MB_LG, (hh // 2 + 1) * MB_LG)
    q = q_ref[...]
    lane = lax.broadcasted_iota(I32, (blk, MB_LG), 1)
    in_head = [(lane < MB_DH) if hh % 2 == 0 else (lane >= MB_DH) for hh in heads]
    qs = q * jnp.asarray(MB_DH ** -0.5, BF16)
    nt = (((1,), (1,)), ((), ()))
    qf = q.astype(F32)
    qht = [jnp.where(in_head[hh], qs[:, grp(hh)].astype(F32), 0.0).T.astype(BF16) for hh in heads]

    n_io = lax.broadcasted_iota(I32, (nb, blk), 0)
    for hh in heads:
        gate = lax.dot_general(km_ref[:, grp(hh)], jnp.where(in_head[hh], qf[:, grp(hh)], 0.0), nt,
                               precision=lax.Precision.HIGHEST, preferred_element_type=F32)
        gate = jnp.where(n_io < qi, gate, NEG_INF)
        chosen = n_io < 0
        for _ in range(MB_TOPK):
            mx = jnp.max(gate, axis=0, keepdims=True)
            ix = jnp.min(jnp.where(gate == mx, n_io, nb), axis=0, keepdims=True)
            hit = n_io == ix
            chosen = chosen | (hit & (mx > NEG_INF))
            gate = jnp.where(hit, NEG_INF, gate)
        msk_ref[hh][...] = jnp.where(chosen, 0.0, NEG_INF)

    vrows = lambda hh: slice(hh * MB_VROWS, (hh + 1) * MB_VROWS)

    def pv_stage(blk_idx):
        vtb = vt_ref[blk_idx]
        r = [jnp.dot(vtb[vrows(hh)], p_ref[hh][...], preferred_element_type=F32) for hh in heads]
        al = [al_ref[hh][...] for hh in heads]
        a_new = [al[hh] * acc_ref[hh][...] + r[hh][:MB_DH] for hh in heads]
        l_new = [al[hh] * l_ref[hh][...] + r[hh][MB_DH:MB_DH + 1] for hh in heads]
        return a_new, l_new

    def store_pv(a_new, l_new):
        for hh in heads:
            acc_ref[hh][...] = a_new[hh]
            l_ref[hh][...] = l_new[hh]

    def softmax_stage():
        s = [s_ref[hh][...] for hh in heads]
        m_old = [m_ref[hh][...] for hh in heads]
        m_new = [jnp.maximum(m_old[hh], jnp.max(s[hh], axis=0, keepdims=True)) for hh in heads]
        alpha = [jnp.exp(m_old[hh] - m_new[hh]) for hh in heads]
        p = [jnp.exp((s[hh] - m_new[hh]).astype(BF16)) for hh in heads]
        return p, alpha, m_new

    def store_softmax(p, alpha, m_new):
        for hh in heads:
            p_ref[hh][...] = p[hh]
            al_ref[hh][...] = alpha[hh]
            m_ref[hh][...] = m_new[hh]

    k_own = k_ref[pl.ds(pl.multiple_of(qi * blk, blk), blk), :]
    key_io = lax.broadcasted_iota(I32, (blk, blk), 0)
    qry_io = lax.broadcasted_iota(I32, (blk, blk), 1)
    for hh in heads:
        s = jnp.dot(k_own[:, grp(hh)], qht[hh], preferred_element_type=F32) + bias_ref[hh, 0]
        s_ref[hh][...] = jnp.where(key_io <= qry_io, s, NEG_INF)
        m_ref[hh][...] = jnp.full((1, blk), NEG_INF, F32)
        l_ref[hh][...] = jnp.zeros((1, blk), F32)
        al_ref[hh][...] = jnp.ones((1, blk), F32)
        acc_ref[hh][...] = jnp.zeros((MB_DH, blk), F32)
        p_ref[hh][...] = jnp.zeros((blk, blk), BF16)

    def step(i, carry, far):
        pv = pv_stage(jnp.where(i <= 1, qi, i - 2))
        sm = softmax_stage()
        kn = k_ref[pl.ds(pl.multiple_of(i * blk, blk), blk), :]
        if far:
            row = [msk_ref[hh][pl.ds(i, 1), :] + bias_ref[hh, MB_BIAS_TILES - 1, 0:1, 0:1] for hh in heads]
            s_next = [jnp.dot(kn[:, grp(hh)], qht[hh], preferred_element_type=F32) + row[hh] for hh in heads]
        else:
            d = qi - i
            s_next = [jnp.dot(kn[:, grp(hh)], qht[hh], preferred_element_type=F32)
                      + bias_ref[hh, d] + msk_ref[hh][pl.ds(i, 1), :] for hh in heads]
        store_pv(*pv)
        for hh in heads:
            s_ref[hh][...] = s_next[hh]
        store_softmax(*sm)
        return carry

    n_far = jnp.maximum(qi - (MB_BIAS_TILES - 2), 0)
    lax.fori_loop(0, n_far, functools.partial(step, far=True), 0)
    lax.fori_loop(n_far, qi, functools.partial(step, far=False), 0)
    pv = pv_stage(jnp.where(qi <= 1, qi, qi - 2))
    sm = softmax_stage()
    store_pv(*pv)
    store_softmax(*sm)
    a_fin, l_fin = pv_stage(jnp.where(qi == 0, qi, qi - 1))
    out_t = jnp.concatenate([a_fin[hh] / l_fin[hh] for hh in heads], axis=0)
    o_ref[...] = out_t.T.astype(o_ref.dtype)


def moba_attention(pqk, vt, km, bias, batch, seq, qb0=0, nqb=None):
    nb = seq // MB_BLOCK
    nqb = nb if nqb is None else nqb
    t = batch * nqb * MB_BLOCK
    groups = MB_WIDTH // MB_PW
    return pl.pallas_call(
        functools.partial(_moba_kernel, qb0=qb0),
        grid=(batch, groups, nqb),
        in_specs=[
            pl.BlockSpec((MB_BLOCK, MB_PW), lambda b, j, i: (b * nb + qb0 + i, j)),
            pl.BlockSpec((seq, MB_PW), lambda b, j, i: (b, groups + j)),
            pl.BlockSpec((nb, MB_PAIR * MB_VROWS, MB_BLOCK), lambda b, j, i: (b, j, 0)),
            pl.BlockSpec((None, nb, MB_PW), lambda b, j, i: (b, 0, j)),
            pl.BlockSpec((MB_PAIR, MB_BIAS_TILES, MB_BLOCK, MB_BLOCK), lambda b, j, i: (j, 0, 0, 0)),
        ],
        out_specs=pl.BlockSpec((MB_BLOCK, MB_PW), lambda b, j, i: (b * nqb + i, j)),
        out_shape=jax.ShapeDtypeStruct((t, MB_WIDTH), BF16),
        scratch_shapes=(
            [pltpu.VMEM((1, MB_BLOCK), F32)] * (3 * MB_PAIR)
            + [pltpu.VMEM((MB_DH, MB_BLOCK), F32)] * MB_PAIR
            + [pltpu.VMEM((nb, MB_BLOCK), F32)] * MB_PAIR
            + [pltpu.VMEM((MB_BLOCK, MB_BLOCK), F32)] * MB_PAIR
            + [pltpu.VMEM((MB_BLOCK, MB_BLOCK), BF16)] * MB_PAIR
        ),
        compiler_params=_cparams(("parallel", "parallel", "arbitrary")),
        name="moba_attn",
    )(pqk, pqk, vt, km, bias)


def _t5_bucket(dist):
    max_exact = REL_BUCKETS // 2
    scaled = jnp.log(jnp.maximum(dist, 1).astype(F32) / max_exact) / math.log(REL_MAX_DIST / max_exact)
    large = jnp.minimum(max_exact + (scaled * (REL_BUCKETS - max_exact)).astype(I32), REL_BUCKETS - 1)
    return jnp.where(dist < max_exact, dist, large)


def moba_bias_tiles(rel_bias):
    blk = MB_BLOCK
    span = 2 * blk - 1
    x = jnp.arange(span) - (blk - 1)
    dist = jnp.maximum(jnp.arange(MB_BIAS_TILES)[:, None] * blk + x[None, :], 0)
    w = rel_bias.astype(F32).T[:, _t5_bucket(dist)]
    h = w.shape[0]
    wp = jnp.pad(w, ((0, 0), (0, 0), (0, 1)))
    a = jnp.broadcast_to(wp[:, :, None, :], (h, MB_BIAS_TILES, blk, span + 1))
    a = a.reshape(h, MB_BIAS_TILES, blk * (span + 1))[:, :, :blk * span]
    return a.reshape(h, MB_BIAS_TILES, blk, span)[:, :, :, blk - 1:]


def _mix_kernel(x_ref, ya_ref, yb_ref, ga_ref, gb_ref, wa_ref, wb_ref, wo_ref, o_ref):
    za = jnp.dot(ya_ref[...], wa_ref[...], preferred_element_type=F32)
    zb = jnp.dot(yb_ref[...], wb_ref[...], preferred_element_type=F32)
    z = jax.nn.sigmoid(ga_ref[...].astype(F32)) * za + jax.nn.sigmoid(gb_ref[...].astype(F32)) * zb
    o_ref[...] = x_ref[...] + jnp.dot(z.astype(BF16), wo_ref[...], preferred_element_type=F32)


def _mem_kv_kernel(m_ref, g_ref, wk_ref, wv_ref, k_ref, v_ref):
    mn = _rms(m_ref[...], g_ref[...]).astype(BF16)
    k_ref[...] = jnp.dot(mn, wk_ref[...], preferred_element_type=F32).astype(BF16)
    v_ref[...] = jnp.dot(mn, wv_ref[...], preferred_element_type=F32).astype(BF16)


def mem_kv(mem, g, wk, wv):
    b, m, d = mem.shape
    spec = pl.BlockSpec((None, m, d), lambda i: (i, 0, 0))
    wspec = pl.BlockSpec((d, d), lambda i: (0, 0))
    return pl.pallas_call(
        _mem_kv_kernel,
        grid=(b,),
        in_specs=[spec, pl.BlockSpec((1, d), lambda i: (0, 0)), wspec, wspec],
        out_specs=[spec, spec],
        out_shape=[jax.ShapeDtypeStruct((b, m, d), BF16)] * 2,
        compiler_params=_cparams(("parallel",)),
        name="mem_kv",
    )(mem, g, wk, wv)


def _cross_kernel(x_ref, g_ref, wq_ref, k_ref, v_ref, wo_ref, o_ref):
    x = x_ref[...]
    d = x.shape[1]
    dh = d // X_HEADS
    h = _rms(x, g_ref[...]).astype(BF16)
    q = (jnp.dot(h, wq_ref[...], preferred_element_type=F32) * (dh ** -0.5)).astype(BF16)
    outs = []
    for hh in range(X_HEADS):
        sl = slice(hh * dh, (hh + 1) * dh)
        s = lax.dot_general(q[:, sl], k_ref[:, sl], (((1,), (1,)), ((), ())),
                            preferred_element_type=F32)
        p = jnp.exp(s - jnp.max(s, axis=1, keepdims=True))
        l = jnp.sum(p, axis=1, keepdims=True)
        o = jnp.dot(p.astype(BF16), v_ref[:, sl], preferred_element_type=F32) / l
        outs.append(o.astype(BF16))
    o = jnp.concatenate(outs, axis=1)
    o_ref[...] = x + jnp.dot(o, wo_ref[...], preferred_element_type=F32)


def _mix_cross_kernel(x_ref, ya_ref, yb_ref, ga_ref, gb_ref, wa_ref, wb_ref, wo_ref,
                      g_ref, wq_ref, k_ref, v_ref, wox_ref, o_ref, x1_ref):
    _mix_kernel(x_ref, ya_ref, yb_ref, ga_ref, gb_ref, wa_ref, wb_ref, wo_ref, x1_ref)
    _cross_kernel(x1_ref, g_ref, wq_ref, k_ref, v_ref, wox_ref, o_ref)


def mix_cross(x2d, ya, yb, pg, wa, wb, wo, g, wq, kx, vx, wox, tok0, tm=512):
    t = yb.shape[0]
    assert t % tm == 0 and tok0 % tm == 0
    d = x2d.shape[1]
    w = ya.shape[1]
    m = kx.shape[1]
    b0 = tok0 // tm
    const = lambda a: pl.BlockSpec(a.shape, lambda i: (0,) * a.ndim)
    kv = pl.BlockSpec((None, m, d), lambda i: (0, 0, 0))
    return pl.pallas_call(
        _mix_cross_kernel,
        grid=(t // tm,),
        in_specs=[
            pl.BlockSpec((tm, d), lambda i: (b0 + i, 0)),
            pl.BlockSpec((tm, w), lambda i: (i, 0)),
            pl.BlockSpec((tm, w), lambda i: (i, 0)),
            pl.BlockSpec((tm, d), lambda i: (b0 + i, 0)),
            pl.BlockSpec((tm, d), lambda i: (b0 + i, 1)),
            const(wa), const(wb), const(wo), const(g), const(wq), kv, kv, const(wox),
        ],
        out_specs=pl.BlockSpec((tm, d), lambda i: (i, 0)),
        out_shape=jax.ShapeDtypeStruct((t, d), F32),
        scratch_shapes=[pltpu.VMEM((tm, d), F32)],
        compiler_params=_cparams(("parallel",)),
        name="mix_cross",
    )(x2d, ya, yb, pg, pg, wa, wb, wo, g, wq, kx, vx, wox)


def _topk_rows(sc, k):
    n = sc.shape[0]
    io = lax.broadcasted_iota(I32, sc.shape, 0).astype(F32)
    vals, ids = [], []
    for _ in range(k):
        m = jnp.max(sc, axis=0, keepdims=True)
        ix = jnp.argmax(sc, axis=0, keepdims=True).astype(F32)
        vals.append(m)
        ids.append(ix)
        sc = jnp.where(io == ix, NEG_INF, sc)
    return jnp.concatenate(vals, axis=0), jnp.concatenate(ids, axis=0).astype(I32)


def _pack_bf16_halves(h):
    bits = lax.bitcast_convert_type(h, I32)
    r = bits + 0x7FFF + (lax.shift_right_logical(bits, 16) & 1)
    half = h.shape[1] // 2
    return lax.shift_right_logical(r[:, :half], 16) | (r[:, half:] & HI_MASK)


def _route_kernel(x_ref, g_ref, wq_ref, sk_ref, hp_ref, idx_ref, w_ref, hb_ref, it_ref, wt_ref):
    p = pl.program_id(1)

    @pl.when(p == 0)
    def _():
        h = _rms(x_ref[...], g_ref[...])
        hp_ref[...] = _pack_bf16_halves(h)
        hb_ref[...] = h.astype(BF16)

    qh = jnp.dot(hb_ref[...], wq_ref[...], preferred_element_type=F32)
    tops = []
    for c in range(2):
        seg = qh[:, c * PEER_HALF:(c + 1) * PEER_HALF]
        sc = lax.dot_general(sk_ref[c], seg, (((1,), (1,)), ((), ())),
                             precision=lax.Precision.HIGHEST, preferred_element_type=F32)
        tops.append(_topk_rows(sc, PEER_TOPK))
    (s0, i0), (s1, i1) = tops
    k = PEER_TOPK
    sub = 8
    tm = s0.shape[1]
    r8 = lax.broadcasted_iota(I32, (sub, tm), 0)
    r16 = lax.broadcasted_iota(I32, (k, tm), 0)
    cand_b = [s0[0:1] + s1, s0[1:2] + s1[:sub]]
    cidx_b = [i0[0:1] * PEER_NKEYS + i1, i0[1:2] * PEER_NKEYS + i1[:sub]]
    pos_b = [r16, k + r8]
    for a in range(2, sub):
        keep = r8 < (k // (a + 1))
        cand_b.append(jnp.where(keep, s0[a:a + 1] + s1[:sub], NEG_INF))
        cidx_b.append(i0[a:a + 1] * PEER_NKEYS + i1[:sub])
        pos_b.append(a * k + r8)
    cand_b.append(s0[sub:] + s1[0:1])
    cidx_b.append(i0[sub:] * PEER_NKEYS + i1[0:1])
    pos_b.append((sub + r8) * k)
    cand = jnp.concatenate(cand_b, axis=0)
    cidx = jnp.concatenate(cidx_b, axis=0)
    pos = jnp.concatenate(pos_b, axis=0).astype(F32)
    vals, ids = [], []
    for _ in range(k):
        m = jnp.max(cand, axis=0, keepdims=True)
        px = jnp.min(jnp.where(cand == m, pos, float(k * k)), axis=0, keepdims=True)
        hit = pos == px
        vals.append(m)
        ids.append(jnp.sum(jnp.where(hit, cidx, 0), axis=0, keepdims=True))
        cand = jnp.where(hit, NEG_INF, cand)
    sf = jnp.concatenate(vals, axis=0)
    e = jnp.exp(sf - sf[0:1])
    rows = pl.ds(pl.multiple_of(p * PEER_TOPK, PEER_TOPK), PEER_TOPK)
    wt_ref[rows, :] = e / jnp.sum(e, axis=0, keepdims=True)
    it_ref[rows, :] = jnp.concatenate(ids, axis=0)

    @pl.when(p == pl.num_programs(1) - 1)
    def _():
        idx_ref[...] = it_ref[...].T
        w_ref[...] = wt_ref[...].T


def peer_route(x2d, g, wq, sk, tok0, t, tm=1024):
    assert t % tm == 0 and tok0 % tm == 0
    d = x2d.shape[1]
    ph = sk.shape[0]
    nsel = ph * PEER_TOPK
    blk0 = tok0 // tm
    return pl.pallas_call(
        _route_kernel,
        grid=(t // tm, ph),
        in_specs=[
            pl.BlockSpec((tm, d), lambda i, p: (blk0 + i, 0)),
            pl.BlockSpec((1, d), lambda i, p: (0, 0)),
            pl.BlockSpec((d, 2 * PEER_HALF), lambda i, p: (0, p)),
            pl.BlockSpec((None, 2, PEER_NKEYS, PEER_HALF), lambda i, p: (p, 0, 0, 0)),
        ],
        out_specs=[
            pl.BlockSpec((tm, d // 2), lambda i, p: (i, 0)),
            pl.BlockSpec((tm, nsel), lambda i, p: (i, 0)),
            pl.BlockSpec((tm, nsel), lambda i, p: (i, 0)),
        ],
        out_shape=[jax.ShapeDtypeStruct((t, d // 2), I32),
                   jax.ShapeDtypeStruct((t, nsel), I32),
                   jax.ShapeDtypeStruct((t, nsel), F32)],
        scratch_shapes=[pltpu.VMEM((tm, d), BF16),
                        pltpu.VMEM((nsel, tm), I32),
                        pltpu.VMEM((nsel, tm), F32)],
        compiler_params=_cparams(("parallel", "arbitrary")),
        name="peer_route",
    )(x2d, g, wq, sk)


def _final_kernel(x_ref, y_ref, g_ref, *rest):
    o_ref = rest[-1]
    o_ref[...] = _rms(x_ref[...] + y_ref[...], g_ref[...])


def final_norm_into(out, xs, y, g, row0, total, tm=512):
    t, d = y.shape
    assert t % tm == 0 and row0 % tm == 0 and total % tm == 0
    blk0 = row0 // tm
    spec = pl.BlockSpec((tm, d), lambda i: (i, 0))
    in_specs = [spec, spec, pl.BlockSpec((1, d), lambda i: (0, 0))]
    args = [xs, y, g]
    aliases = {}
    if out is not None:
        in_specs.append(pl.BlockSpec(memory_space=pl.ANY))
        args.append(out)
        aliases = {3: 0}
    return pl.pallas_call(
        _final_kernel, grid=(t // tm,),
        in_specs=in_specs,
        out_specs=pl.BlockSpec((tm, d), lambda i: (blk0 + i, 0)),
        out_shape=jax.ShapeDtypeStruct((total, d), F32),
        input_output_aliases=aliases,
        compiler_params=_cparams(("parallel",)), name="final_norm",
    )(*args)


SC_CORES = 2
SC_SUBCORES = 16
SC_WORKERS = SC_CORES * SC_SUBCORES
SC_LANES = 16
SC_GROUP = 32


def _sc_mesh():
    return plsc.VectorSubcoreMesh(core_axis_name="c", subcore_axis_name="s")


def _sc_params():
    return pltpu.CompilerParams(needs_layout_passes=False)


def _sc_worker_id():
    return lax.axis_index("s") * SC_CORES + lax.axis_index("c")


SC_ROW_LANE = 128


def _sc_unit_off(u):
    off = u * SC_LANES
    return off if isinstance(off, int) else pl.multiple_of(off, SC_LANES)


GELU_C0 = math.sqrt(2.0 / math.pi)
GELU_C1 = 0.044715


def _gelu_tanh(x):
    z = GELU_C0 * (x + GELU_C1 * (x * x * x))
    th = 1.0 - 2.0 / (jnp.exp(2.0 * z) + 1.0)
    return 0.5 * x * (1.0 + th)


SC_PK_RING = 4
SC_PK_SUB = 4
HI_MASK = -65536


def _pack_tables_kernel(u_ref, v_ref, o_ref):
    for part, ref in enumerate((u_ref, v_ref)):
        words = _pack_bf16_halves(ref[...])
        for sub in range(SC_PK_SUB):
            o_ref[:, part * SC_PK_SUB + sub, :] = words[:, sub * SC_ROW_LANE:(sub + 1) * SC_ROW_LANE]


def pack_expert_tables(u, v, te=512):
    e, d = u.shape
    assert d == 2 * SC_PK_SUB * SC_ROW_LANE
    spec = pl.BlockSpec((te, d), lambda i: (i, 0))
    return pl.pallas_call(
        _pack_tables_kernel, grid=(e // te,), in_specs=[spec, spec],
        out_specs=pl.BlockSpec((te, 2 * SC_PK_SUB, SC_ROW_LANE), lambda i: (i, 0, 0)),
        out_shape=jax.ShapeDtypeStruct((e, 2 * SC_PK_SUB, SC_ROW_LANE), I32),
        compiler_params=_cparams(("parallel",)), name="pack_expert_tables",
    )(u, v)


def _unpack_halves(x32):
    w = plsc.bitcast(x32, I32)
    return plsc.bitcast(w << 16, F32), plsc.bitcast(w & HI_MASK, F32)


def _tree_sum(xs):
    while len(xs) > 1:
        xs = [xs[i] + xs[i + 1] for i in range(0, len(xs), 2)]
    return xs[0]


def peer_experts_pk_sc(tab_uv, idx_flat, w_flat, hp, d):
    t = hp.shape[0]
    nsel = PEER_SEL
    assert t % SC_WORKERS == 0 and d == 2 * SC_PK_SUB * SC_ROW_LANE
    tpw = t // SC_WORKERS
    g = SC_GROUP if tpw % SC_GROUP == 0 else SC_GROUP // 2
    assert tpw % g == 0
    groups = tpw // g
    heads = nsel // SC_LANES
    chunks = d // 32
    units = g * heads
    ring = SC_PK_RING
    assert units % ring == 0
    row_buf = pltpu.VMEM((SC_LANES, 2 * SC_PK_SUB, SC_ROW_LANE), I32)

    def row_words(rows, r, wc, sub0):
        per = SC_ROW_LANE // SC_LANES
        return plsc.bitcast(
            rows[r, sub0 + wc // per, pl.ds(pl.multiple_of((wc % per) * SC_LANES, SC_LANES), SC_LANES)], BF16)

    def ring_loop(n_units, start, wait, compute):
        for u in range(ring - 1):
            start(u, u)

        @pl.loop(0, n_units, step=ring)
        def _(uu):
            for b in range(ring):
                u = uu + b
                nxt = u + (ring - 1)

                @pl.when(nxt < n_units)
                def _():
                    start(nxt, (b + ring - 1) % ring)

                wait(u, b)
                compute(u, b)

    @functools.partial(
        pl.kernel, mesh=_sc_mesh(),
        out_type=jax.ShapeDtypeStruct((t, d), F32),
        scratch_types=[
            pltpu.VMEM((g * nsel,), I32),
            pltpu.VMEM((g * nsel,), F32),
            pltpu.VMEM((g, d // 2), I32),
            pltpu.VMEM((g, d), F32),
            pltpu.VMEM((SC_LANES * SC_LANES,), F32),
            [row_buf] * ring,
            [pltpu.SemaphoreType.DMA] * ring,
        ],
        compiler_params=_sc_params(),
        name="peer_experts_pk_sc",
    )
    def k(tab_hbm, idx_hbm, w_hbm, h_hbm, out_hbm, idx_v, coef_v, h_v, y_v, red_v, rows, sems):
        wid = _sc_worker_id()
        lane = lax.iota(I32, SC_LANES)

        def copy(u, slot):
            ids = idx_v.at[pl.ds(_sc_unit_off(u), SC_LANES)]
            return pltpu.make_async_copy(tab_hbm.at[ids], rows[slot], sems[slot])

        def dots(u, slot):
            tt = u // heads

            def body(cp, accs):
                out = []
                hv = [plsc.bitcast(h_v[tt, pl.ds(pl.multiple_of((2 * cp + i) * SC_LANES, SC_LANES), SC_LANES)], BF16)
                      for i in range(2)]
                for r in range(SC_LANES):
                    pr = (row_words(rows[slot], r, 2 * cp, 0) * hv[0]
                          + row_words(rows[slot], r, 2 * cp + 1, 0) * hv[1])
                    lo, hi = _unpack_halves(pr)
                    out.append(accs[r] + lo + hi)
                return tuple(out)

            accs = lax.fori_loop(0, chunks // 2, body,
                                 tuple(jnp.zeros((SC_LANES,), F32) for _ in range(SC_LANES)))
            for r in range(SC_LANES):
                red_v[pl.ds(r * SC_LANES, SC_LANES)] = accs[r]
            act = _tree_sum([plsc.load_gather(red_v, [lane * SC_LANES + j]) for j in range(SC_LANES)])
            sl = pl.ds(_sc_unit_off(u), SC_LANES)
            coef_v[sl] = coef_v[sl] * _gelu_tanh(act)

        def combine(u, slot):
            tt = u // heads
            first = (u % heads) == 0
            cb = []
            for r in range(SC_LANES):
                c = plsc.load_gather(coef_v, [jnp.full((SC_LANES,), u * SC_LANES + r, I32)])
                cb.append(plsc.pack(c, c, format=plsc.PackFormat.INTERLEAVED))

            @plsc.parallel_loop(0, chunks, unroll=2)
            def _(wc):
                lo, hi = _unpack_halves(
                    _tree_sum([cb[r] * row_words(rows[slot], r, wc, SC_PK_SUB) for r in range(SC_LANES)]))
                for half, val in ((0, lo), (1, hi)):
                    sl = pl.ds(pl.multiple_of(half * (d // 2) + wc * SC_LANES, SC_LANES), SC_LANES)
                    y_v[tt, sl] = val + jnp.where(first, 0.0, y_v[tt, sl])

        def unit(u, slot):
            dots(u, slot)
            combine(u, slot)

        @pl.loop(0, groups)
        def _(gi):
            base = wid * tpw + gi * g
            pltpu.sync_copy(idx_hbm.at[pl.ds(base * nsel, g * nsel)], idx_v)
            pltpu.sync_copy(w_hbm.at[pl.ds(base * nsel, g * nsel)], coef_v)
            pltpu.sync_copy(h_hbm.at[pl.ds(base, g)], h_v)
            ring_loop(units, lambda u, s: copy(u, s).start(), lambda u, s: copy(u, s).wait(), unit)
            pltpu.sync_copy(y_v, out_hbm.at[pl.ds(base, g)])

    return k(tab_uv, idx_flat, w_flat, hp)


def kernel(x, mem, rel_bias, ln_mix, w_in, hg_lower, hg_norm, w_up_a, w_up_b, w_out, ln_cross, ln_mem, wq_x, wk_x, wv_x, wo_x, ln_ffn, peer_query, peer_subkeys, peer_u, peer_v, ln_final):
    b, s, d = x.shape
    depth = w_in.shape[0]
    assert depth == 1, "the residual after PEER is fused into the final norm"
    assert s % MB_BLOCK == 0 and s % HG_CHUNK == 0 and s % (PEER_SLICES * SC_WORKERS * SC_GROUP) == 0
    nb = s // MB_BLOCK
    row = lambda a: a.reshape(1, -1).astype(F32)
    lb_all = jnp.cumsum(jax.nn.softmax(hg_lower.astype(F32), axis=0), axis=0)
    bias = moba_bias_tiles(rel_bias)
    n_hg = 4 * HG_WIDTH
    n_qk = 2 * MB_WIDTH
    n_mb = 3 * MB_WIDTH
    l = 0
    w = w_in[l].astype(BF16)
    w_hg, w_qk, w_vt, w_g = w[:, :n_hg], w[:, n_hg:n_hg + n_qk], w[:, n_hg + n_qk:n_hg + n_mb].T, w[:, n_hg + n_mb:]
    wa, wb, wo = w_up_a[l].astype(BF16), w_up_b[l].astype(BF16), w_out[l].astype(BF16)
    wqx, wox = wq_x[l].astype(BF16), wo_x[l].astype(BF16)
    wpq, sk = peer_query[l].astype(BF16), peer_subkeys[l].astype(F32)
    tab_uv = pack_expert_tables(peer_u[l].astype(F32), peer_v[l].astype(F32))
    kx, vx = mem_kv(mem, row(ln_mem[l]), wk_x[l].astype(BF16), wv_x[l].astype(BF16))

    outs = []
    for bi in range(b):
        x2d = x[bi]
        p0, pqk, km, vt, pg = in_proj(x2d, row(ln_mix[l]), w_hg, w_qk, w_vt, w_g)
        hg_state = jnp.zeros((HG_HEADS, HG_D, HG_D), F32)
        km = km.reshape(1, nb, MB_WIDTH)
        sizes = [s // PEER_SLICES] * PEER_SLICES
        if bi == b - 1:
            first = SC_WORKERS * SC_GROUP // 2
            sizes = [first, sizes[0] - first] + sizes[1:]
        tok0 = 0
        for ts in sizes:
            ya, hg_state = hgrn2(p0, row(lb_all[l]), row(hg_norm[l]), hg_state, tok0, ts)
            yb = moba_attention(pqk, vt, km, bias, 1, s, tok0 // MB_BLOCK, ts // MB_BLOCK)
            xs = mix_cross(x2d, ya, yb, pg, wa, wb, wo, row(ln_cross[l]), wqx, kx[bi:bi + 1], vx[bi:bi + 1], wox, tok0)
            hp, eidx, wts = peer_route(xs, row(ln_ffn[l]), wpq, sk, 0, ts, math.gcd(ts, 1024))
            y = peer_experts_pk_sc(tab_uv, eidx.reshape(ts * PEER_SEL), wts.reshape(ts * PEER_SEL), hp, d)
            outs.append((xs, y, bi * s + tok0))
            tok0 += ts
    out = None
    for xs, y, row0 in sorted(outs, key=lambda e: (-(e[2] // s), e[2])):
        out = final_norm_into(out, xs, y, row(ln_final), row0, b * s)
    return out.reshape(b, s, d)
```

```python
import functools
import math

import jax
import jax.numpy as jnp
from jax import lax
from jax.experimental import pallas as pl
from jax.experimental.pallas import tpu as pltpu
from jax.experimental.pallas import tpu_sc as plsc

F32 = jnp.float32
BF16 = jnp.bfloat16
I32 = jnp.int32
EPS = 1e-6
NEG_INF = float("-inf")

HG_HEADS = 4
HG_D = 128
HG_WIDTH = HG_HEADS * HG_D
HG_CHUNK = 64
HG_SUB = 16
MB_HEADS = 8
MB_DH = 64
MB_WIDTH = MB_HEADS * MB_DH
MB_BLOCK = 256
MB_TOPK = 3
MB_BIAS_TILES = 8
REL_BUCKETS = 32
REL_MAX_DIST = 2048
X_HEADS = 4
PEER_HEADS = 8
PEER_NKEYS = 128
PEER_TOPK = 16
PEER_HALF = 128
PEER_SEL = PEER_HEADS * PEER_TOPK
PEER_SLICES = 4

VMEM_LIMIT = 56 * 1024 * 1024


def _cparams(sem):
    return pltpu.CompilerParams(dimension_semantics=sem, vmem_limit_bytes=VMEM_LIMIT)


def _rms(x, g):
    ms = jnp.mean(x * x, axis=-1, keepdims=True)
    return x * lax.rsqrt(ms + EPS) * g


def _cast_kernel(w_ref, o_ref):
    o_ref[...] = w_ref[...].astype(o_ref.dtype)


def cast_bf16(w, tn=512):
    k, n = w.shape
    assert n % tn == 0
    spec = pl.BlockSpec((k, tn), lambda j: (0, j))
    return pl.pallas_call(
        _cast_kernel, grid=(n // tn,), in_specs=[spec], out_specs=spec,
        out_shape=jax.ShapeDtypeStruct((k, n), BF16),
        compiler_params=_cparams(("parallel",)), name="cast_bf16",
    )(w)


def _in_proj_kernel(x_ref, g_ref, w0_ref, w1_ref, wvt_ref, w2_ref, o0_ref, o1_ref, okm_ref, ovt_ref, o2_ref):
    h = _rms(x_ref[...], g_ref[...]).astype(BF16)
    o0_ref[...] = jnp.dot(h, w0_ref[...], preferred_element_type=F32)
    qk = jnp.dot(h, w1_ref[...], preferred_element_type=F32)
    o1_ref[...] = qk.astype(BF16)
    okm_ref[0] = jnp.mean(qk[:, MB_WIDTH:], axis=0, keepdims=True)
    vt = lax.dot_general(wvt_ref[...], h, (((1,), (1,)), ((), ())), preferred_element_type=F32).astype(BF16)
    for hd in range(MB_HEADS):
        ovt_ref[0, hd * MB_VROWS:hd * MB_VROWS + MB_DH, :] = vt[hd * MB_DH:(hd + 1) * MB_DH]
        ovt_ref[0, hd * MB_VROWS + MB_DH:(hd + 1) * MB_VROWS, :] = jnp.ones((MB_ONES, vt.shape[1]), BF16)
    o2_ref[...] = jnp.dot(h, w2_ref[...], preferred_element_type=F32).astype(BF16)


def in_proj(x2d, g, w0, w1, wvt, w2):
    t, d = x2d.shape
    tm = MB_BLOCK
    assert wvt.shape[0] == MB_WIDTH and w1.shape[1] == 2 * MB_WIDTH
    n0, n1, nv, n2 = w0.shape[1], w1.shape[1], MB_VT_ROWS, w2.shape[1]
    full = lambda a: pl.BlockSpec(a.shape, lambda i: (0, 0))
    return pl.pallas_call(
        _in_proj_kernel,
        grid=(t // tm,),
        in_specs=[pl.BlockSpec((tm, d), lambda i: (i, 0)), full(g), full(w0), full(w1), full(wvt), full(w2)],
        out_specs=[pl.BlockSpec((tm, n0), lambda i: (i, 0)),
                   pl.BlockSpec((tm, n1), lambda i: (i, 0)),
                   pl.BlockSpec((1, 1, MB_WIDTH), lambda i: (i, 0, 0)),
                   pl.BlockSpec((1, nv, tm), lambda i: (i, 0, 0)),
                   pl.BlockSpec((tm, n2), lambda i: (i, 0))],
        out_shape=[jax.ShapeDtypeStruct((t, n0), F32),
                   jax.ShapeDtypeStruct((t, n1), BF16),
                   jax.ShapeDtypeStruct((t // tm, 1, MB_WIDTH), F32),
                   jax.ShapeDtypeStruct((t // tm, nv, tm), BF16),
                   jax.ShapeDtypeStruct((t, n2), BF16)],
        compiler_params=_cparams(("parallel",)),
        name="in_proj",
    )(x2d, g, w0, w1, wvt, w2)


def _hgrn_kernel(q_ref, f_ref, i_ref, g_ref, lb_ref, gain_ref, st0_ref, o_ref, stn_ref, st_ref):
    c = pl.program_id(0)

    @pl.when(c == 0)
    def _():
        st_ref[...] = st0_ref[...]

    C, S = HG_CHUNK, HG_SUB
    row = lax.broadcasted_iota(I32, (C, C), 0)
    col = lax.broadcasted_iota(I32, (C, C), 1)
    tril = (row >= col).astype(F32)
    t_iota = lax.broadcasted_iota(I32, (S, 1), 0)

    for h in range(HG_HEADS):
        sl = slice(h * HG_D, (h + 1) * HG_D)
        q = q_ref[:, sl]
        v = i_ref[:, sl]
        lb = lb_ref[:, sl]
        f = lb + (1.0 - lb) * jax.nn.sigmoid(f_ref[:, sl])
        lf = jnp.log(f)
        k = 1.0 - f
        b = jnp.dot(tril, lf, precision=lax.Precision.HIGHEST, preferred_element_type=F32)
        st = st_ref[h]
        vb = v.astype(BF16)
        qd = (q * jnp.exp(b)).astype(BF16)
        o_inter = lax.dot_general(qd, st.astype(BF16), (((1,), (1,)), ((), ())),
                                  preferred_element_type=F32)
        outs = []
        for i in range(C // S):
            r0 = i * S
            qi = q[r0:r0 + S]
            ki = k[r0:r0 + S]
            bi = b[r0:r0 + S]
            vi = v[r0:r0 + S]
            oi = o_inter[r0:r0 + S]
            if i > 0:
                bs = b[r0 - 1:r0]
                qh = (qi * jnp.exp(bi - bs)).astype(BF16)
                kh = (k[:r0] * jnp.exp(bs - b[:r0])).astype(BF16)
                a = lax.dot_general(qh, kh, (((1,), (1,)), ((), ())), preferred_element_type=F32)
                oi = oi + jnp.dot(a.astype(BF16), vb[:r0], preferred_element_type=F32)
            half = S // 2
            o_half = [oi[:half], oi[half:]]
            for s in range(S):
                for hf in range(s // half, 2):
                    rows = slice(hf * half, (hf + 1) * half)
                    dec = jnp.exp(jnp.minimum(bi[rows] - bi[s:s + 1], 0.0))
                    a_s = jnp.sum(qi[rows] * ki[s:s + 1] * dec, axis=-1, keepdims=True)
                    a_s = jnp.where(t_iota[rows] >= s, a_s, 0.0)
                    o_half[hf] = o_half[hf] + a_s * vi[s:s + 1]
            outs.extend(o_half)
        o = jnp.concatenate(outs, axis=0)
        b_end = b[C - 1:C]
        kd = (k * jnp.exp(b_end - b)).astype(BF16)
        upd = lax.dot_general(vb, kd, (((0,), (0,)), ((), ())), preferred_element_type=F32)
        st_ref[h] = st * jnp.exp(b_end) + upd
        o = o * lax.rsqrt(jnp.mean(o * o, axis=-1, keepdims=True) + EPS)
        g = g_ref[:, sl]
        o_ref[:, sl] = (o * gain_ref[:, sl] * (g * jax.nn.sigmoid(g))).astype(o_ref.dtype)

    @pl.when(c == pl.num_programs(0) - 1)
    def _():
        stn_ref[...] = st_ref[...]


def hgrn2(p0, lb, gain, state, tok0, t):
    assert t % HG_CHUNK == 0 and tok0 % HG_CHUNK == 0
    nc = t // HG_CHUNK
    c0 = tok0 // HG_CHUNK
    w = HG_WIDTH

    def col(j):
        return pl.BlockSpec((HG_CHUNK, w), lambda c, j=j: (c0 + c, j))

    st_spec = pl.BlockSpec(state.shape, lambda c: (0, 0, 0))
    return pl.pallas_call(
        _hgrn_kernel,
        grid=(nc,),
        in_specs=[col(0), col(1), col(2), col(3),
                  pl.BlockSpec((1, w), lambda c: (0, 0)),
                  pl.BlockSpec((1, w), lambda c: (0, 0)),
                  st_spec],
        out_specs=[pl.BlockSpec((HG_CHUNK, w), lambda c: (c, 0)), st_spec],
        out_shape=[jax.ShapeDtypeStruct((t, w), BF16), jax.ShapeDtypeStruct(state.shape, F32)],
        scratch_shapes=[pltpu.VMEM((HG_HEADS, HG_D, HG_D), F32)],
        compiler_params=_cparams(("arbitrary",)),
        name="hgrn2",
    )(p0, p0, p0, p0, lb, gain, state)


MB_PAIR = 4
MB_PW = MB_PAIR * MB_DH
MB_LG = 128
MB_ONES = 16
MB_VROWS = MB_DH + MB_ONES
MB_VT_ROWS = MB_HEADS * MB_VROWS


def _moba_kernel(q_ref, k_ref, vt_ref, km_ref, bias_ref, o_ref, *scratch, qb0):
    m_ref, l_ref, al_ref, acc_ref, msk_ref, s_ref, p_ref = (
        scratch[i * MB_PAIR:(i + 1) * MB_PAIR] for i in range(7))
    qi = pl.program_id(2) + qb0
    nb = km_ref.shape[0]
    blk = MB_BLOCK
    heads = range(MB_PAIR)
    grp = lambda hh: slice((hh // 2) * MB_LG, (hh // 2 + 1) * MB_LG)
    q = q_ref[...]
    lane = lax.broadcasted_iota(I32, (blk, MB_LG), 1)
    in_head = [(lane < MB_DH) if hh % 2 == 0 else (lane >= MB_DH) for hh in heads]
    qs = q * jnp.asarray(MB_DH ** -0.5, BF16)
    nt = (((1,), (1,)), ((), ()))
    qf = q.astype(F32)
    qht = [jnp.where(in_head[hh], qs[:, grp(hh)].astype(F32), 0.0).T.astype(BF16) for hh in heads]

    n_io = lax.broadcasted_iota(I32, (nb, blk), 0)
    for hh in heads:
        gate = lax.dot_general(km_ref[:, grp(hh)], jnp.where(in_head[hh], qf[:, grp(hh)], 0.0), nt,
                               precision=lax.Precision.HIGHEST, preferred_element_type=F32)
        gate = jnp.where(n_io < qi, gate, NEG_INF)
        chosen = n_io < 0
        for _ in range(MB_TOPK):
            mx = jnp.max(gate, axis=0, keepdims=True)
            ix = jnp.min(jnp.where(gate == mx, n_io, nb), axis=0, keepdims=True)
            hit = n_io == ix
            chosen = chosen | (hit & (mx > NEG_INF))
            gate = jnp.where(hit, NEG_INF, gate)
        msk_ref[hh][...] = jnp.where(chosen, 0.0, NEG_INF)

    vrows = lambda hh: slice(hh * MB_VROWS, (hh + 1) * MB_VROWS)

    def pv_stage(blk_idx):
        vtb = vt_ref[blk_idx]
        r = [jnp.dot(vtb[vrows(hh)], p_ref[hh][...], preferred_element_type=F32) for hh in heads]
        al = [al_ref[hh][...] for hh in heads]
        a_new = [al[hh] * acc_ref[hh][...] + r[hh][:MB_DH] for hh in heads]
        l_new = [al[hh] * l_ref[hh][...] + r[hh][MB_DH:MB_DH + 1] for hh in heads]
        return a_new, l_new

    def store_pv(a_new, l_new):
        for hh in heads:
            acc_ref[hh][...] = a_new[hh]
            l_ref[hh][...] = l_new[hh]

    def softmax_stage():
        s = [s_ref[hh][...] for hh in heads]
        m_old = [m_ref[hh][...] for hh in heads]
        m_new = [jnp.maximum(m_old[hh], jnp.max(s[hh], axis=0, keepdims=True)) for hh in heads]
        alpha = [jnp.exp(m_old[hh] - m_new[hh]) for hh in heads]
        p = [jnp.exp((s[hh] - m_new[hh]).astype(BF16)) for hh in heads]
        return p, alpha, m_new

    def store_softmax(p, alpha, m_new):
        for hh in heads:
            p_ref[hh][...] = p[hh]
            al_ref[hh][...] = alpha[hh]
            m_ref[hh][...] = m_new[hh]

    k_own = k_ref[pl.ds(pl.multiple_of(qi * blk, blk), blk), :]
    key_io = lax.broadcasted_iota(I32, (blk, blk), 0)
    qry_io = lax.broadcasted_iota(I32, (blk, blk), 1)
    for hh in heads:
        s = jnp.dot(k_own[:, grp(hh)], qht[hh], preferred_element_type=F32) + bias_ref[hh, 0]
        s_ref[hh][...] = jnp.where(key_io <= qry_io, s, NEG_INF)
        m_ref[hh][...] = jnp.full((1, blk), NEG_INF, F32)
        l_ref[hh][...] = jnp.zeros((1, blk), F32)
        al_ref[hh][...] = jnp.ones((1, blk), F32)
        acc_ref[hh][...] = jnp.zeros((MB_DH, blk), F32)
        p_ref[hh][...] = jnp.zeros((blk, blk), BF16)

    def step(i, carry, far):
        pv = pv_stage(jnp.where(i <= 1, qi, i - 2))
        sm = softmax_stage()
        kn = k_ref[pl.ds(pl.multiple_of(i * blk, blk), blk), :]
        if far:
            row = [msk_ref[hh][pl.ds(i, 1), :] + bias_ref[hh, MB_BIAS_TILES - 1, 0:1, 0:1] for hh in heads]
            s_next = [jnp.dot(kn[:, grp(hh)], qht[hh], preferred_element_type=F32) + row[hh] for hh in heads]
        else:
            d = qi - i
            s_next = [jnp.dot(kn[:, grp(hh)], qht[hh], preferred_element_type=F32)
                      + bias_ref[hh, d] + msk_ref[hh][pl.ds(i, 1), :] for hh in heads]
        store_pv(*pv)
        for hh in heads:
            s_ref[hh][...] = s_next[hh]
        store_softmax(*sm)
        return carry

    n_far = jnp.maximum(qi - (MB_BIAS_TILES - 2), 0)
    lax.fori_loop(0, n_far, functools.partial(step, far=True), 0)
    lax.fori_loop(n_far, qi, functools.partial(step, far=False), 0)
    pv = pv_stage(jnp.where(qi <= 1, qi, qi - 2))
    sm = softmax_stage()
    store_pv(*pv)
    store_softmax(*sm)
    a_fin, l_fin = pv_stage(jnp.where(qi == 0, qi, qi - 1))
    out_t = jnp.concatenate([a_fin[hh] / l_fin[hh] for hh in heads], axis=0)
    o_ref[...] = out_t.T.astype(o_ref.dtype)


def moba_attention(pqk, vt, km, bias, batch, seq, qb0=0, nqb=None):
    nb = seq // MB_BLOCK
    nqb = nb if nqb is None else nqb
    t = batch * nqb * MB_BLOCK
    groups = MB_WIDTH // MB_PW
    return pl.pallas_call(
        functools.partial(_moba_kernel, qb0=qb0),
        grid=(batch, groups, nqb),
        in_specs=[
            pl.BlockSpec((MB_BLOCK, MB_PW), lambda b, j, i: (b * nb + qb0 + i, j)),
            pl.BlockSpec((seq, MB_PW), lambda b, j, i: (b, groups + j)),
            pl.BlockSpec((nb, MB_PAIR * MB_VROWS, MB_BLOCK), lambda b, j, i: (b, j, 0)),
            pl.BlockSpec((None, nb, MB_PW), lambda b, j, i: (b, 0, j)),
            pl.BlockSpec((MB_PAIR, MB_BIAS_TILES, MB_BLOCK, MB_BLOCK), lambda b, j, i: (j, 0, 0, 0)),
        ],
        out_specs=pl.BlockSpec((MB_BLOCK, MB_PW), lambda b, j, i: (b * nqb + i, j)),
        out_shape=jax.ShapeDtypeStruct((t, MB_WIDTH), BF16),
        scratch_shapes=(
            [pltpu.VMEM((1, MB_BLOCK), F32)] * (3 * MB_PAIR)
            + [pltpu.VMEM((MB_DH, MB_BLOCK), F32)] * MB_PAIR
            + [pltpu.VMEM((nb, MB_BLOCK), F32)] * MB_PAIR
            + [pltpu.VMEM((MB_BLOCK, MB_BLOCK), F32)] * MB_PAIR
            + [pltpu.VMEM((MB_BLOCK, MB_BLOCK), BF16)] * MB_PAIR
        ),
        compiler_params=_cparams(("parallel", "parallel", "arbitrary")),
        name="moba_attn",
    )(pqk, pqk, vt, km, bias)


def _t5_bucket(dist):
    max_exact = REL_BUCKETS // 2
    scaled = jnp.log(jnp.maximum(dist, 1).astype(F32) / max_exact) / math.log(REL_MAX_DIST / max_exact)
    large = jnp.minimum(max_exact + (scaled * (REL_BUCKETS - max_exact)).astype(I32), REL_BUCKETS - 1)
    return jnp.where(dist < max_exact, dist, large)


def moba_bias_tiles(rel_bias):
    blk = MB_BLOCK
    span = 2 * blk - 1
    x = jnp.arange(span) - (blk - 1)
    dist = jnp.maximum(jnp.arange(MB_BIAS_TILES)[:, None] * blk + x[None, :], 0)
    w = rel_bias.astype(F32).T[:, _t5_bucket(dist)]
    h = w.shape[0]
    wp = jnp.pad(w, ((0, 0), (0, 0), (0, 1)))
    a = jnp.broadcast_to(wp[:, :, None, :], (h, MB_BIAS_TILES, blk, span + 1))
    a = a.reshape(h, MB_BIAS_TILES, blk * (span + 1))[:, :, :blk * span]
    return a.reshape(h, MB_BIAS_TILES, blk, span)[:, :, :, blk - 1:]


def _mix_kernel(x_ref, ya_ref, yb_ref, ga_ref, gb_ref, wa_ref, wb_ref, wo_ref, o_ref):
    za = jnp.dot(ya_ref[...], wa_ref[...], preferred_element_type=F32)
    zb = jnp.dot(yb_ref[...], wb_ref[...], preferred_element_type=F32)
    z = jax.nn.sigmoid(ga_ref[...].astype(F32)) * za + jax.nn.sigmoid(gb_ref[...].astype(F32)) * zb
    o_ref[...] = x_ref[...] + jnp.dot(z.astype(BF16), wo_ref[...], preferred_element_type=F32)


def _mem_kv_kernel(m_ref, g_ref, wk_ref, wv_ref, k_ref, v_ref):
    mn = _rms(m_ref[...], g_ref[...]).astype(BF16)
    k_ref[...] = jnp.dot(mn, wk_ref[...], preferred_element_type=F32).astype(BF16)
    v_ref[...] = jnp.dot(mn, wv_ref[...], preferred_element_type=F32).astype(BF16)


def mem_kv(mem, g, wk, wv):
    b, m, d = mem.shape
    spec = pl.BlockSpec((None, m, d), lambda i: (i, 0, 0))
    wspec = pl.BlockSpec((d, d), lambda i: (0, 0))
    return pl.pallas_call(
        _mem_kv_kernel,
        grid=(b,),
        in_specs=[spec, pl.BlockSpec((1, d), lambda i: (0, 0)), wspec, wspec],
        out_specs=[spec, spec],
        out_shape=[jax.ShapeDtypeStruct((b, m, d), BF16)] * 2,
        compiler_params=_cparams(("parallel",)),
        name="mem_kv",
    )(mem, g, wk, wv)


def _cross_kernel(x_ref, g_ref, wq_ref, k_ref, v_ref, wo_ref, o_ref):
    x = x_ref[...]
    d = x.shape[1]
    dh = d // X_HEADS
    h = _rms(x, g_ref[...]).astype(BF16)
    q = (jnp.dot(h, wq_ref[...], preferred_element_type=F32) * (dh ** -0.5)).astype(BF16)
    outs = []
    for hh in range(X_HEADS):
        sl = slice(hh * dh, (hh + 1) * dh)
        s = lax.dot_general(q[:, sl], k_ref[:, sl], (((1,), (1,)), ((), ())),
                            preferred_element_type=F32)
        p = jnp.exp(s - jnp.max(s, axis=1, keepdims=True))
        l = jnp.sum(p, axis=1, keepdims=True)
        o = jnp.dot(p.astype(BF16), v_ref[:, sl], preferred_element_type=F32) / l
        outs.append(o.astype(BF16))
    o = jnp.concatenate(outs, axis=1)
    o_ref[...] = x + jnp.dot(o, wo_ref[...], preferred_element_type=F32)


def _mix_cross_kernel(x_ref, ya_ref, yb_ref, ga_ref, gb_ref, wa_ref, wb_ref, wo_ref,
                      g_ref, wq_ref, k_ref, v_ref, wox_ref, o_ref, x1_ref):
    _mix_kernel(x_ref, ya_ref, yb_ref, ga_ref, gb_ref, wa_ref, wb_ref, wo_ref, x1_ref)
    _cross_kernel(x1_ref, g_ref, wq_ref, k_ref, v_ref, wox_ref, o_ref)


def mix_cross(x2d, ya, yb, pg, wa, wb, wo, g, wq, kx, vx, wox, tok0, tm=512):
    t = yb.shape[0]
    assert t % tm == 0 and tok0 % tm == 0
    d = x2d.shape[1]
    w = ya.shape[1]
    m = kx.shape[1]
    b0 = tok0 // tm
    const = lambda a: pl.BlockSpec(a.shape, lambda i: (0,) * a.ndim)
    kv = pl.BlockSpec((None, m, d), lambda i: (0, 0, 0))
    return pl.pallas_call(
        _mix_cross_kernel,
        grid=(t // tm,),
        in_specs=[
            pl.BlockSpec((tm, d), lambda i: (b0 + i, 0)),
            pl.BlockSpec((tm, w), lambda i: (i, 0)),
            pl.BlockSpec((tm, w), lambda i: (i, 0)),
            pl.BlockSpec((tm, d), lambda i: (b0 + i, 0)),
            pl.BlockSpec((tm, d), lambda i: (b0 + i, 1)),
            const(wa), const(wb), const(wo), const(g), const(wq), kv, kv, const(wox),
        ],
        out_specs=pl.BlockSpec((tm, d), lambda i: (i, 0)),
        out_shape=jax.ShapeDtypeStruct((t, d), F32),
        scratch_shapes=[pltpu.VMEM((tm, d), F32)],
        compiler_params=_cparams(("parallel",)),
        name="mix_cross",
    )(x2d, ya, yb, pg, pg, wa, wb, wo, g, wq, kx, vx, wox)


def _topk_rows(sc, k):
    n = sc.shape[0]
    io = lax.broadcasted_iota(I32, sc.shape, 0).astype(F32)
    vals, ids = [], []
    for _ in range(k):
        m = jnp.max(sc, axis=0, keepdims=True)
        ix = jnp.argmax(sc, axis=0, keepdims=True).astype(F32)
        vals.append(m)
        ids.append(ix)
        sc = jnp.where(io == ix, NEG_INF, sc)
    return jnp.concatenate(vals, axis=0), jnp.concatenate(ids, axis=0).astype(I32)


def _pack_bf16_halves(h):
    bits = lax.bitcast_convert_type(h, I32)
    r = bits + 0x7FFF + (lax.shift_right_logical(bits, 16) & 1)
    half = h.shape[1] // 2
    return lax.shift_right_logical(r[:, :half], 16) | (r[:, half:] & HI_MASK)


def _route_kernel(x_ref, g_ref, wq_ref, sk_ref, hp_ref, idx_ref, w_ref, hb_ref, it_ref, wt_ref):
    p = pl.program_id(1)

    @pl.when(p == 0)
    def _():
        h = _rms(x_ref[...], g_ref[...])
        hp_ref[...] = _pack_bf16_halves(h)
        hb_ref[...] = h.astype(BF16)

    qh = jnp.dot(hb_ref[...], wq_ref[...], preferred_element_type=F32)
    tops = []
    for c in range(2):
        seg = qh[:, c * PEER_HALF:(c + 1) * PEER_HALF]
        sc = lax.dot_general(sk_ref[c], seg, (((1,), (1,)), ((), ())),
                             precision=lax.Precision.HIGHEST, preferred_element_type=F32)
        tops.append(_topk_rows(sc, PEER_TOPK))
    (s0, i0), (s1, i1) = tops
    k = PEER_TOPK
    sub = 8
    tm = s0.shape[1]
    r8 = lax.broadcasted_iota(I32, (sub, tm), 0)
    r16 = lax.broadcasted_iota(I32, (k, tm), 0)
    cand_b = [s0[0:1] + s1, s0[1:2] + s1[:sub]]
    cidx_b = [i0[0:1] * PEER_NKEYS + i1, i0[1:2] * PEER_NKEYS + i1[:sub]]
    pos_b = [r16, k + r8]
    for a in range(2, sub):
        keep = r8 < (k // (a + 1))
        cand_b.append(jnp.where(keep, s0[a:a + 1] + s1[:sub], NEG_INF))
        cidx_b.append(i0[a:a + 1] * PEER_NKEYS + i1[:sub])
        pos_b.append(a * k + r8)
    cand_b.append(s0[sub:] + s1[0:1])
    cidx_b.append(i0[sub:] * PEER_NKEYS + i1[0:1])
    pos_b.append((sub + r8) * k)
    cand = jnp.concatenate(cand_b, axis=0)
    cidx = jnp.concatenate(cidx_b, axis=0)
    pos = jnp.concatenate(pos_b, axis=0).astype(F32)
    vals, ids = [], []
    for _ in range(k):
        m = jnp.max(cand, axis=0, keepdims=True)
        px = jnp.min(jnp.where(cand == m, pos, float(k * k)), axis=0, keepdims=True)
        hit = pos == px
        vals.append(m)
        ids.append(jnp.sum(jnp.where(hit, cidx, 0), axis=0, keepdims=True))
        cand = jnp.where(hit, NEG_INF, cand)
    sf = jnp.concatenate(vals, axis=0)
    e = jnp.exp(sf - sf[0:1])
    rows = pl.ds(pl.multiple_of(p * PEER_TOPK, PEER_TOPK), PEER_TOPK)
    wt_ref[rows, :] = e / jnp.sum(e, axis=0, keepdims=True)
    it_ref[rows, :] = jnp.concatenate(ids, axis=0)

    @pl.when(p == pl.num_programs(1) - 1)
    def _():
        idx_ref[...] = it_ref[...].T
        w_ref[...] = wt_ref[...].T


def peer_route(x2d, g, wq, sk, tok0, t, tm=1024):
    assert t % tm == 0 and tok0 % tm == 0
    d = x2d.shape[1]
    ph = sk.shape[0]
    nsel = ph * PEER_TOPK
    blk0 = tok0 // tm
    return pl.pallas_call(
        _route_kernel,
        grid=(t // tm, ph),
        in_specs=[
            pl.BlockSpec((tm, d), lambda i, p: (blk0 + i, 0)),
            pl.BlockSpec((1, d), lambda i, p: (0, 0)),
            pl.BlockSpec((d, 2 * PEER_HALF), lambda i, p: (0, p)),
            pl.BlockSpec((None, 2, PEER_NKEYS, PEER_HALF), lambda i, p: (p, 0, 0, 0)),
        ],
        out_specs=[
            pl.BlockSpec((tm, d // 2), lambda i, p: (i, 0)),
            pl.BlockSpec((tm, nsel), lambda i, p: (i, 0)),
            pl.BlockSpec((tm, nsel), lambda i, p: (i, 0)),
        ],
        out_shape=[jax.ShapeDtypeStruct((t, d // 2), I32),
                   jax.ShapeDtypeStruct((t, nsel), I32),
                   jax.ShapeDtypeStruct((t, nsel), F32)],
        scratch_shapes=[pltpu.VMEM((tm, d), BF16),
                        pltpu.VMEM((nsel, tm), I32),
                        pltpu.VMEM((nsel, tm), F32)],
        compiler_params=_cparams(("parallel", "arbitrary")),
        name="peer_route",
    )(x2d, g, wq, sk)


def _final_kernel(x_ref, y_ref, g_ref, *rest):
    o_ref = rest[-1]
    o_ref[...] = _rms(x_ref[...] + y_ref[...], g_ref[...])


def final_norm_into(out, xs, y, g, row0, total, tm=512):
    t, d = y.shape
    assert t % tm == 0 and row0 % tm == 0 and total % tm == 0
    blk0 = row0 // tm
    spec = pl.BlockSpec((tm, d), lambda i: (i, 0))
    in_specs = [spec, spec, pl.BlockSpec((1, d), lambda i: (0, 0))]
    args = [xs, y, g]
    aliases = {}
    if out is not None:
        in_specs.append(pl.BlockSpec(memory_space=pl.ANY))
        args.append(out)
        aliases = {3: 0}
    return pl.pallas_call(
        _final_kernel, grid=(t // tm,),
        in_specs=in_specs,
        out_specs=pl.BlockSpec((tm, d), lambda i: (blk0 + i, 0)),
        out_shape=jax.ShapeDtypeStruct((total, d), F32),
        input_output_aliases=aliases,
        compiler_params=_cparams(("parallel",)), name="final_norm",
    )(*args)


SC_CORES = 2
SC_SUBCORES = 16
SC_WORKERS = SC_CORES * SC_SUBCORES
SC_LANES = 16
SC_GROUP = 32


def _sc_mesh():
    return plsc.VectorSubcoreMesh(core_axis_name="c", subcore_axis_name="s")


def _sc_params():
    return pltpu.CompilerParams(needs_layout_passes=False)


def _sc_worker_id():
    return lax.axis_index("s") * SC_CORES + lax.axis_index("c")


SC_ROW_LANE = 128


def _sc_unit_off(u):
    off = u * SC_LANES
    return off if isinstance(off, int) else pl.multiple_of(off, SC_LANES)


GELU_C0 = math.sqrt(2.0 / math.pi)
GELU_C1 = 0.044715


def _gelu_tanh(x):
    z = GELU_C0 * (x + GELU_C1 * (x * x * x))
    th = 1.0 - 2.0 / (jnp.exp(2.0 * z) + 1.0)
    return 0.5 * x * (1.0 + th)


SC_PK_RING = 4
SC_PK_SUB = 4
HI_MASK = -65536


def _pack_tables_kernel(u_ref, v_ref, o_ref):
    for part, ref in enumerate((u_ref, v_ref)):
        words = _pack_bf16_halves(ref[...])
        for sub in range(SC_PK_SUB):
            o_ref[:, part * SC_PK_SUB + sub, :] = words[:, sub * SC_ROW_LANE:(sub + 1) * SC_ROW_LANE]


def pack_expert_tables(u, v, te=512):
    e, d = u.shape
    assert d == 2 * SC_PK_SUB * SC_ROW_LANE
    spec = pl.BlockSpec((te, d), lambda i: (i, 0))
    return pl.pallas_call(
        _pack_tables_kernel, grid=(e // te,), in_specs=[spec, spec],
        out_specs=pl.BlockSpec((te, 2 * SC_PK_SUB, SC_ROW_LANE), lambda i: (i, 0, 0)),
        out_shape=jax.ShapeDtypeStruct((e, 2 * SC_PK_SUB, SC_ROW_LANE), I32),
        compiler_params=_cparams(("parallel",)), name="pack_expert_tables",
    )(u, v)


def _unpack_halves(x32):
    w = plsc.bitcast(x32, I32)
    return plsc.bitcast(w << 16, F32), plsc.bitcast(w & HI_MASK, F32)


def _tree_sum(xs):
    while len(xs) > 1:
        xs = [xs[i] + xs[i + 1] for i in range(0, len(xs), 2)]
    return xs[0]


def peer_experts_pk_sc(tab_uv, idx_flat, w_flat, hp, d):
    t = hp.shape[0]
    nsel = PEER_SEL
    assert t % SC_WORKERS == 0 and d == 2 * SC_PK_SUB * SC_ROW_LANE
    tpw = t // SC_WORKERS
    g = SC_GROUP if tpw % SC_GROUP == 0 else SC_GROUP // 2
    assert tpw % g == 0
    groups = tpw // g
    heads = nsel // SC_LANES
    chunks = d // 32
    units = g * heads
    ring = SC_PK_RING
    assert units % ring == 0
    row_buf = pltpu.VMEM((SC_LANES, 2 * SC_PK_SUB, SC_ROW_LANE), I32)

    def row_words(rows, r, wc, sub0):
        per = SC_ROW_LANE // SC_LANES
        return plsc.bitcast(
            rows[r, sub0 + wc // per, pl.ds(pl.multiple_of((wc % per) * SC_LANES, SC_LANES), SC_LANES)], BF16)

    def ring_loop(n_units, start, wait, compute):
        for u in range(ring - 1):
            start(u, u)

        @pl.loop(0, n_units, step=ring)
        def _(uu):
            for b in range(ring):
                u = uu + b
                nxt = u + (ring - 1)

                @pl.when(nxt < n_units)
                def _():
                    start(nxt, (b + ring - 1) % ring)

                wait(u, b)
                compute(u, b)

    @functools.partial(
        pl.kernel, mesh=_sc_mesh(),
        out_type=jax.ShapeDtypeStruct((t, d), F32),
        scratch_types=[
            pltpu.VMEM((g * nsel,), I32),
            pltpu.VMEM((g * nsel,), F32),
            pltpu.VMEM((g, d // 2), I32),
            pltpu.VMEM((g, d), F32),
            pltpu.VMEM((SC_LANES * SC_LANES,), F32),
            [row_buf] * ring,
            [pltpu.SemaphoreType.DMA] * ring,
        ],
        compiler_params=_sc_params(),
        name="peer_experts_pk_sc",
    )
    def k(tab_hbm, idx_hbm, w_hbm, h_hbm, out_hbm, idx_v, coef_v, h_v, y_v, red_v, rows, sems):
        wid = _sc_worker_id()
        lane = lax.iota(I32, SC_LANES)

        def copy(u, slot):
            ids = idx_v.at[pl.ds(_sc_unit_off(u), SC_LANES)]
            return pltpu.make_async_copy(tab_hbm.at[ids], rows[slot], sems[slot])

        def dots(u, slot):
            tt = u // heads

            def body(cp, accs):
                out = []
                hv = [plsc.bitcast(h_v[tt, pl.ds(pl.multiple_of((2 * cp + i) * SC_LANES, SC_LANES), SC_LANES)], BF16)
                      for i in range(2)]
                for r in range(SC_LANES):
                    pr = (row_words(rows[slot], r, 2 * cp, 0) * hv[0]
                          + row_words(rows[slot], r, 2 * cp + 1, 0) * hv[1])
                    lo, hi = _unpack_halves(pr)
                    out.append(accs[r] + lo + hi)
                return tuple(out)

            accs = lax.fori_loop(0, chunks // 2, body,
                                 tuple(jnp.zeros((SC_LANES,), F32) for _ in range(SC_LANES)))
            for r in range(SC_LANES):
                red_v[pl.ds(r * SC_LANES, SC_LANES)] = accs[r]
            act = _tree_sum([plsc.load_gather(red_v, [lane * SC_LANES + j]) for j in range(SC_LANES)])
            sl = pl.ds(_sc_unit_off(u), SC_LANES)
            coef_v[sl] = coef_v[sl] * _gelu_tanh(act)

        def combine(u, slot):
            tt = u // heads
            first = (u % heads) == 0
            cb = []
            for r in range(SC_LANES):
                c = plsc.load_gather(coef_v, [jnp.full((SC_LANES,), u * SC_LANES + r, I32)])
                cb.append(plsc.pack(c, c, format=plsc.PackFormat.INTERLEAVED))

            @plsc.parallel_loop(0, chunks, unroll=2)
            def _(wc):
                lo, hi = _unpack_halves(
                    _tree_sum([cb[r] * row_words(rows[slot], r, wc, SC_PK_SUB) for r in range(SC_LANES)]))
                for half, val in ((0, lo), (1, hi)):
                    sl = pl.ds(pl.multiple_of(half * (d // 2) + wc * SC_LANES, SC_LANES), SC_LANES)
                    y_v[tt, sl] = val + jnp.where(first, 0.0, y_v[tt, sl])

        def unit(u, slot):
            dots(u, slot)
            combine(u, slot)

        @pl.loop(0, groups)
        def _(gi):
            base = wid * tpw + gi * g
            pltpu.sync_copy(idx_hbm.at[pl.ds(base * nsel, g * nsel)], idx_v)
            pltpu.sync_copy(w_hbm.at[pl.ds(base * nsel, g * nsel)], coef_v)
            pltpu.sync_copy(h_hbm.at[pl.ds(base, g)], h_v)
            ring_loop(units, lambda u, s: copy(u, s).start(), lambda u, s: copy(u, s).wait(), unit)
            pltpu.sync_copy(y_v, out_hbm.at[pl.ds(base, g)])

    return k(tab_uv, idx_flat, w_flat, hp)


def kernel(x, mem, rel_bias, ln_mix, w_in, hg_lower, hg_norm, w_up_a, w_up_b, w_out, ln_cross, ln_mem, wq_x, wk_x, wv_x, wo_x, ln_ffn, peer_query, peer_subkeys, peer_u, peer_v, ln_final):
    b, s, d = x.shape
    depth = w_in.shape[0]
    assert depth == 1, "the residual after PEER is fused into the final norm"
    assert s % MB_BLOCK == 0 and s % HG_CHUNK == 0 and s % (PEER_SLICES * SC_WORKERS * SC_GROUP) == 0
    nb = s // MB_BLOCK
    row = lambda a: a.reshape(1, -1).astype(F32)
    lb_all = jnp.cumsum(jax.nn.softmax(hg_lower.astype(F32), axis=0), axis=0)
    bias = moba_bias_tiles(rel_bias)
    n_hg = 4 * HG_WIDTH
    n_qk = 2 * MB_WIDTH
    n_mb = 3 * MB_WIDTH
    l = 0
    w = cast_bf16(w_in[l].astype(F32))
    w_hg, w_qk, w_vt, w_g = w[:, :n_hg], w[:, n_hg:n_hg + n_qk], w[:, n_hg + n_qk:n_hg + n_mb].T, w[:, n_hg + n_mb:]
    wa, wb, wo = w_up_a[l].astype(BF16), w_up_b[l].astype(BF16), w_out[l].astype(BF16)
    wqx, wox = wq_x[l].astype(BF16), wo_x[l].astype(BF16)
    wpq, sk = peer_query[l].astype(BF16), peer_subkeys[l].astype(F32)
    tab_uv = pack_expert_tables(peer_u[l].astype(F32), peer_v[l].astype(F32))
    kx, vx = mem_kv(mem, row(ln_mem[l]), wk_x[l].astype(BF16), wv_x[l].astype(BF16))

    outs = []
    for bi in range(b):
        x2d = x[bi]
        p0, pqk, km, vt, pg = in_proj(x2d, row(ln_mix[l]), w_hg, w_qk, w_vt, w_g)
        hg_state = jnp.zeros((HG_HEADS, HG_D, HG_D), F32)
        km = km.reshape(1, nb, MB_WIDTH)
        sizes = [s // PEER_SLICES] * PEER_SLICES
        if bi == b - 1:
            first = SC_WORKERS * SC_GROUP // 2
            sizes = [first, first, sizes[0] - 2 * first] + sizes[1:]
        tok0 = 0
        for ts in sizes:
            ya, hg_state = hgrn2(p0, row(lb_all[l]), row(hg_norm[l]), hg_state, tok0, ts)
            yb = moba_attention(pqk, vt, km, bias, 1, s, tok0 // MB_BLOCK, ts // MB_BLOCK)
            xs = mix_cross(x2d, ya, yb, pg, wa, wb, wo, row(ln_cross[l]), wqx, kx[bi:bi + 1], vx[bi:bi + 1], wox, tok0)
            hp, eidx, wts = peer_route(xs, row(ln_ffn[l]), wpq, sk, 0, ts, math.gcd(ts, 1024))
            y = peer_experts_pk_sc(tab_uv, eidx.reshape(ts * PEER_SEL), wts.reshape(ts * PEER_SEL), hp, d)
            outs.append((xs, y, bi * s + tok0))
            tok0 += ts
    out = None
    for xs, y, row0 in sorted(outs, key=lambda e: (-(e[2] // s), e[2])):
        out = final_norm_into(out, xs, y, row(ln_final), row0, b * s)
    return out.reshape(b, s, d)
```

```python
import functools
import math

import jax
import jax.numpy as jnp
from jax import lax
from jax.experimental import pallas as pl
from jax.experimental.pallas import tpu as pltpu
from jax.experimental.pallas import tpu_sc as plsc

F32 = jnp.float32
BF16 = jnp.bfloat16
I32 = jnp.int32
EPS = 1e-6
NEG_INF = float("-inf")

HG_HEADS = 4
HG_D = 128
HG_WIDTH = HG_HEADS * HG_D
HG_CHUNK = 64
HG_SUB = 16
MB_HEADS = 8
MB_DH = 64
MB_WIDTH = MB_HEADS * MB_DH
MB_BLOCK = 256
MB_TOPK = 3
MB_BIAS_TILES = 8
REL_BUCKETS = 32
REL_MAX_DIST = 2048
X_HEADS = 4
PEER_HEADS = 8
PEER_NKEYS = 128
PEER_TOPK = 16
PEER_HALF = 128
PEER_SEL = PEER_HEADS * PEER_TOPK
PEER_SLICES = 4

VMEM_LIMIT = 56 * 1024 * 1024


def _cparams(sem):
    return pltpu.CompilerParams(dimension_semantics=sem, vmem_limit_bytes=VMEM_LIMIT)


def _rms(x, g):
    ms = jnp.mean(x * x, axis=-1, keepdims=True)
    return x * lax.rsqrt(ms + EPS) * g


def _cast_kernel(w_ref, o_ref):
    o_ref[...] = w_ref[...].astype(o_ref.dtype)


def cast_bf16(w, tn=512):
    k, n = w.shape
    assert n % tn == 0
    spec = pl.BlockSpec((k, tn), lambda j: (0, j))
    return pl.pallas_call(
        _cast_kernel, grid=(n // tn,), in_specs=[spec], out_specs=spec,
        out_shape=jax.ShapeDtypeStruct((k, n), BF16),
        compiler_params=_cparams(("parallel",)), name="cast_bf16",
    )(w)


def _in_proj_kernel(x_ref, g_ref, w0_ref, w1_ref, wvt_ref, *rest):
    w2_refs, (o0_ref, o1_ref, okm_ref, ovt_ref, o2_ref) = rest[:-5], rest[-5:]
    h = _rms(x_ref[...], g_ref[...]).astype(BF16)
    o0_ref[...] = jnp.dot(h, w0_ref[...], preferred_element_type=F32)
    qk = jnp.dot(h, w1_ref[...], preferred_element_type=F32)
    o1_ref[...] = qk.astype(BF16)
    okm_ref[0] = jnp.mean(qk[:, MB_WIDTH:], axis=0, keepdims=True)
    vt = lax.dot_general(wvt_ref[...], h, (((1,), (1,)), ((), ())), preferred_element_type=F32).astype(BF16)
    for hd in range(MB_HEADS):
        ovt_ref[0, hd * MB_VROWS:hd * MB_VROWS + MB_DH, :] = vt[hd * MB_DH:(hd + 1) * MB_DH]
        ovt_ref[0, hd * MB_VROWS + MB_DH:(hd + 1) * MB_VROWS, :] = jnp.ones((MB_ONES, vt.shape[1]), BF16)
    wg = w2_refs[0].shape[1]
    for j, w2_ref in enumerate(w2_refs):
        o2_ref[:, j * wg:(j + 1) * wg] = jnp.dot(h, w2_ref[...], preferred_element_type=F32).astype(BF16)


def in_proj(x2d, g, w, wvt):
    t, d = x2d.shape
    tm = MB_BLOCK
    n0, n1, nv, n2 = 4 * HG_WIDTH, 2 * MB_WIDTH, MB_VT_ROWS, 2 * d
    wg = MB_WIDTH
    assert wvt.shape == (MB_WIDTH, d) and w.shape == (d, n0 + n1 + MB_WIDTH + n2)
    assert n0 % n1 == 0 and (n0 + n1 + MB_WIDTH) % wg == 0 and n2 % wg == 0
    full = lambda a: pl.BlockSpec(a.shape, lambda i: (0, 0))
    g0 = (n0 + n1 + MB_WIDTH) // wg
    w_specs = ([pl.BlockSpec((d, n0), lambda i: (0, 0)), pl.BlockSpec((d, n1), lambda i: (0, n0 // n1)), full(wvt)]
               + [pl.BlockSpec((d, wg), lambda i, j=j: (0, g0 + j)) for j in range(n2 // wg)])
    return pl.pallas_call(
        _in_proj_kernel,
        grid=(t // tm,),
        in_specs=[pl.BlockSpec((tm, d), lambda i: (i, 0)), full(g)] + w_specs,
        out_specs=[pl.BlockSpec((tm, n0), lambda i: (i, 0)),
                   pl.BlockSpec((tm, n1), lambda i: (i, 0)),
                   pl.BlockSpec((1, 1, MB_WIDTH), lambda i: (i, 0, 0)),
                   pl.BlockSpec((1, nv, tm), lambda i: (i, 0, 0)),
                   pl.BlockSpec((tm, n2), lambda i: (i, 0))],
        out_shape=[jax.ShapeDtypeStruct((t, n0), F32),
                   jax.ShapeDtypeStruct((t, n1), BF16),
                   jax.ShapeDtypeStruct((t // tm, 1, MB_WIDTH), F32),
                   jax.ShapeDtypeStruct((t // tm, nv, tm), BF16),
                   jax.ShapeDtypeStruct((t, n2), BF16)],
        compiler_params=_cparams(("parallel",)),
        name="in_proj",
    )(x2d, g, w, w, wvt, *([w] * (n2 // wg)))


def _hgrn_kernel(q_ref, f_ref, i_ref, g_ref, lb_ref, gain_ref, st0_ref, o_ref, stn_ref, st_ref):
    c = pl.program_id(0)

    @pl.when(c == 0)
    def _():
        st_ref[...] = st0_ref[...]

    C, S = HG_CHUNK, HG_SUB
    row = lax.broadcasted_iota(I32, (C, C), 0)
    col = lax.broadcasted_iota(I32, (C, C), 1)
    tril = (row >= col).astype(F32)
    t_iota = lax.broadcasted_iota(I32, (S, 1), 0)

    for h in range(HG_HEADS):
        sl = slice(h * HG_D, (h + 1) * HG_D)
        q = q_ref[:, sl]
        v = i_ref[:, sl]
        lb = lb_ref[:, sl]
        f = lb + (1.0 - lb) * jax.nn.sigmoid(f_ref[:, sl])
        lf = jnp.log(f)
        k = 1.0 - f
        b = jnp.dot(tril, lf, precision=lax.Precision.HIGHEST, preferred_element_type=F32)
        st = st_ref[h]
        vb = v.astype(BF16)
        qd = (q * jnp.exp(b)).astype(BF16)
        o_inter = lax.dot_general(qd, st.astype(BF16), (((1,), (1,)), ((), ())),
                                  preferred_element_type=F32)
        outs = []
        for i in range(C // S):
            r0 = i * S
            qi = q[r0:r0 + S]
            ki = k[r0:r0 + S]
            bi = b[r0:r0 + S]
            vi = v[r0:r0 + S]
            oi = o_inter[r0:r0 + S]
            if i > 0:
                bs = b[r0 - 1:r0]
                qh = (qi * jnp.exp(bi - bs)).astype(BF16)
                kh = (k[:r0] * jnp.exp(bs - b[:r0])).astype(BF16)
                a = lax.dot_general(qh, kh, (((1,), (1,)), ((), ())), preferred_element_type=F32)
                oi = oi + jnp.dot(a.astype(BF16), vb[:r0], preferred_element_type=F32)
            half = S // 2
            o_half = [oi[:half], oi[half:]]
            for s in range(S):
                for hf in range(s // half, 2):
                    rows = slice(hf * half, (hf + 1) * half)
                    dec = jnp.exp(jnp.minimum(bi[rows] - bi[s:s + 1], 0.0))
                    a_s = jnp.sum(qi[rows] * ki[s:s + 1] * dec, axis=-1, keepdims=True)
                    a_s = jnp.where(t_iota[rows] >= s, a_s, 0.0)
                    o_half[hf] = o_half[hf] + a_s * vi[s:s + 1]
            outs.extend(o_half)
        o = jnp.concatenate(outs, axis=0)
        b_end = b[C - 1:C]
        kd = (k * jnp.exp(b_end - b)).astype(BF16)
        upd = lax.dot_general(vb, kd, (((0,), (0,)), ((), ())), preferred_element_type=F32)
        st_ref[h] = st * jnp.exp(b_end) + upd
        o = o * lax.rsqrt(jnp.mean(o * o, axis=-1, keepdims=True) + EPS)
        g = g_ref[:, sl]
        o_ref[:, sl] = (o * gain_ref[:, sl] * (g * jax.nn.sigmoid(g))).astype(o_ref.dtype)

    @pl.when(c == pl.num_programs(0) - 1)
    def _():
        stn_ref[...] = st_ref[...]


def hgrn2(p0, lb, gain, state, tok0, t):
    assert t % HG_CHUNK == 0 and tok0 % HG_CHUNK == 0
    nc = t // HG_CHUNK
    c0 = tok0 // HG_CHUNK
    w = HG_WIDTH

    def col(j):
        return pl.BlockSpec((HG_CHUNK, w), lambda c, j=j: (c0 + c, j))

    st_spec = pl.BlockSpec(state.shape, lambda c: (0, 0, 0))
    return pl.pallas_call(
        _hgrn_kernel,
        grid=(nc,),
        in_specs=[col(0), col(1), col(2), col(3),
                  pl.BlockSpec((1, w), lambda c: (0, 0)),
                  pl.BlockSpec((1, w), lambda c: (0, 0)),
                  st_spec],
        out_specs=[pl.BlockSpec((HG_CHUNK, w), lambda c: (c, 0)), st_spec],
        out_shape=[jax.ShapeDtypeStruct((t, w), BF16), jax.ShapeDtypeStruct(state.shape, F32)],
        scratch_shapes=[pltpu.VMEM((HG_HEADS, HG_D, HG_D), F32)],
        compiler_params=_cparams(("arbitrary",)),
        name="hgrn2",
    )(p0, p0, p0, p0, lb, gain, state)


MB_PAIR = 4
MB_PW = MB_PAIR * MB_DH
MB_LG = 128
MB_ONES = 16
MB_VROWS = MB_DH + MB_ONES
MB_VT_ROWS = MB_HEADS * MB_VROWS


def _moba_kernel(q_ref, k_ref, vt_ref, km_ref, bias_ref, o_ref, *scratch, qb0):
    m_ref, l_ref, al_ref, acc_ref, msk_ref, s_ref, p_ref = (
        scratch[i * MB_PAIR:(i + 1) * MB_PAIR] for i in range(7))
    qi = pl.program_id(2) + qb0
    nb = km_ref.shape[0]
    blk = MB_BLOCK
    heads = range(MB_PAIR)
    grp = lambda hh: slice((hh // 2) * MB_LG, (hh // 2 + 1) * MB_LG)
    q = q_ref[...]
    lane = lax.broadcasted_iota(I32, (blk, MB_LG), 1)
    in_head = [(lane < MB_DH) if hh % 2 == 0 else (lane >= MB_DH) for hh in heads]
    qs = q * jnp.asarray(MB_DH ** -0.5, BF16)
    nt = (((1,), (1,)), ((), ()))
    qf = q.astype(F32)
    qht = [jnp.where(in_head[hh], qs[:, grp(hh)].astype(F32), 0.0).T.astype(BF16) for hh in heads]

    n_io = lax.broadcasted_iota(I32, (nb, blk), 0)
    for hh in heads:
        gate = lax.dot_general(km_ref[:, grp(hh)], jnp.where(in_head[hh], qf[:, grp(hh)], 0.0), nt,
                               precision=lax.Precision.HIGHEST, preferred_element_type=F32)
        gate = jnp.where(n_io < qi, gate, NEG_INF)
        chosen = n_io < 0
        for _ in range(MB_TOPK):
            mx = jnp.max(gate, axis=0, keepdims=True)
            ix = jnp.min(jnp.where(gate == mx, n_io, nb), axis=0, keepdims=True)
            hit = n_io == ix
            chosen = chosen | (hit & (mx > NEG_INF))
            gate = jnp.where(hit, NEG_INF, gate)
        msk_ref[hh][...] = jnp.where(chosen, 0.0, NEG_INF)

    vrows = lambda hh: slice(hh * MB_VROWS, (hh + 1) * MB_VROWS)

    def pv_stage(blk_idx):
        vtb = vt_ref[blk_idx]
        r = [jnp.dot(vtb[vrows(hh)], p_ref[hh][...], preferred_element_type=F32) for hh in heads]
        al = [al_ref[hh][...] for hh in heads]
        a_new = [al[hh] * acc_ref[hh][...] + r[hh][:MB_DH] for hh in heads]
        l_new = [al[hh] * l_ref[hh][...] + r[hh][MB_DH:MB_DH + 1] for hh in heads]
        return a_new, l_new

    def store_pv(a_new, l_new):
        for hh in heads:
            acc_ref[hh][...] = a_new[hh]
            l_ref[hh][...] = l_new[hh]

    def softmax_stage():
        s = [s_ref[hh][...] for hh in heads]
        m_old = [m_ref[hh][...] for hh in heads]
        m_new = [jnp.maximum(m_old[hh], jnp.max(s[hh], axis=0, keepdims=True)) for hh in heads]
        alpha = [jnp.exp(m_old[hh] - m_new[hh]) for hh in heads]
        p = [jnp.exp((s[hh] - m_new[hh]).astype(BF16)) for hh in heads]
        return p, alpha, m_new

    def store_softmax(p, alpha, m_new):
        for hh in heads:
            p_ref[hh][...] = p[hh]
            al_ref[hh][...] = alpha[hh]
            m_ref[hh][...] = m_new[hh]

    k_own = k_ref[pl.ds(pl.multiple_of(qi * blk, blk), blk), :]
    key_io = lax.broadcasted_iota(I32, (blk, blk), 0)
    qry_io = lax.broadcasted_iota(I32, (blk, blk), 1)
    for hh in heads:
        s = jnp.dot(k_own[:, grp(hh)], qht[hh], preferred_element_type=F32) + bias_ref[hh, 0]
        s_ref[hh][...] = jnp.where(key_io <= qry_io, s, NEG_INF)
        m_ref[hh][...] = jnp.full((1, blk), NEG_INF, F32)
        l_ref[hh][...] = jnp.zeros((1, blk), F32)
        al_ref[hh][...] = jnp.ones((1, blk), F32)
        acc_ref[hh][...] = jnp.zeros((MB_DH, blk), F32)
        p_ref[hh][...] = jnp.zeros((blk, blk), BF16)

    def step(i, carry, far):
        pv = pv_stage(jnp.where(i <= 1, qi, i - 2))
        sm = softmax_stage()
        kn = k_ref[pl.ds(pl.multiple_of(i * blk, blk), blk), :]
        if far:
            row = [msk_ref[hh][pl.ds(i, 1), :] + bias_ref[hh, MB_BIAS_TILES - 1, 0:1, 0:1] for hh in heads]
            s_next = [jnp.dot(kn[:, grp(hh)], qht[hh], preferred_element_type=F32) + row[hh] for hh in heads]
        else:
            d = qi - i
            s_next = [jnp.dot(kn[:, grp(hh)], qht[hh], preferred_element_type=F32)
                      + bias_ref[hh, d] + msk_ref[hh][pl.ds(i, 1), :] for hh in heads]
        store_pv(*pv)
        for hh in heads:
            s_ref[hh][...] = s_next[hh]
        store_softmax(*sm)
        return carry

    n_far = jnp.maximum(qi - (MB_BIAS_TILES - 2), 0)
    lax.fori_loop(0, n_far, functools.partial(step, far=True), 0)
    lax.fori_loop(n_far, qi, functools.partial(step, far=False), 0)
    pv = pv_stage(jnp.where(qi <= 1, qi, qi - 2))
    sm = softmax_stage()
    store_pv(*pv)
    store_softmax(*sm)
    a_fin, l_fin = pv_stage(jnp.where(qi == 0, qi, qi - 1))
    out_t = jnp.concatenate([a_fin[hh] / l_fin[hh] for hh in heads], axis=0)
    o_ref[...] = out_t.T.astype(o_ref.dtype)


def moba_attention(pqk, vt, km, bias, batch, seq, qb0=0, nqb=None):
    nb = seq // MB_BLOCK
    nqb = nb if nqb is None else nqb
    t = batch * nqb * MB_BLOCK
    groups = MB_WIDTH // MB_PW
    return pl.pallas_call(
        functools.partial(_moba_kernel, qb0=qb0),
        grid=(batch, groups, nqb),
        in_specs=[
            pl.BlockSpec((MB_BLOCK, MB_PW), lambda b, j, i: (b * nb + qb0 + i, j)),
            pl.BlockSpec((seq, MB_PW), lambda b, j, i: (b, groups + j)),
            pl.BlockSpec((nb, MB_PAIR * MB_VROWS, MB_BLOCK), lambda b, j, i: (b, j, 0)),
            pl.BlockSpec((None, nb, MB_PW), lambda b, j, i: (b, 0, j)),
            pl.BlockSpec((MB_PAIR, MB_BIAS_TILES, MB_BLOCK, MB_BLOCK), lambda b, j, i: (j, 0, 0, 0)),
        ],
        out_specs=pl.BlockSpec((MB_BLOCK, MB_PW), lambda b, j, i: (b * nqb + i, j)),
        out_shape=jax.ShapeDtypeStruct((t, MB_WIDTH), BF16),
        scratch_shapes=(
            [pltpu.VMEM((1, MB_BLOCK), F32)] * (3 * MB_PAIR)
            + [pltpu.VMEM((MB_DH, MB_BLOCK), F32)] * MB_PAIR
            + [pltpu.VMEM((nb, MB_BLOCK), F32)] * MB_PAIR
            + [pltpu.VMEM((MB_BLOCK, MB_BLOCK), F32)] * MB_PAIR
            + [pltpu.VMEM((MB_BLOCK, MB_BLOCK), BF16)] * MB_PAIR
        ),
        compiler_params=_cparams(("parallel", "parallel", "arbitrary")),
        name="moba_attn",
    )(pqk, pqk, vt, km, bias)


def _t5_bucket(dist):
    max_exact = REL_BUCKETS // 2
    scaled = jnp.log(jnp.maximum(dist, 1).astype(F32) / max_exact) / math.log(REL_MAX_DIST / max_exact)
    large = jnp.minimum(max_exact + (scaled * (REL_BUCKETS - max_exact)).astype(I32), REL_BUCKETS - 1)
    return jnp.where(dist < max_exact, dist, large)


def moba_bias_tiles(rel_bias):
    blk = MB_BLOCK
    span = 2 * blk - 1
    x = jnp.arange(span) - (blk - 1)
    dist = jnp.maximum(jnp.arange(MB_BIAS_TILES)[:, None] * blk + x[None, :], 0)
    w = rel_bias.astype(F32).T[:, _t5_bucket(dist)]
    h = w.shape[0]
    wp = jnp.pad(w, ((0, 0), (0, 0), (0, 1)))
    a = jnp.broadcast_to(wp[:, :, None, :], (h, MB_BIAS_TILES, blk, span + 1))
    a = a.reshape(h, MB_BIAS_TILES, blk * (span + 1))[:, :, :blk * span]
    return a.reshape(h, MB_BIAS_TILES, blk, span)[:, :, :, blk - 1:]


def _mix_kernel(x_ref, ya_ref, yb_ref, ga_ref, gb_ref, wa_ref, wb_ref, wo_ref, o_ref):
    za = jnp.dot(ya_ref[...], wa_ref[...], preferred_element_type=F32)
    zb = jnp.dot(yb_ref[...], wb_ref[...], preferred_element_type=F32)
    z = jax.nn.sigmoid(ga_ref[...].astype(F32)) * za + jax.nn.sigmoid(gb_ref[...].astype(F32)) * zb
    o_ref[...] = x_ref[...] + jnp.dot(z.astype(BF16), wo_ref[...], preferred_element_type=F32)


def _mem_kv_kernel(m_ref, g_ref, wk_ref, wv_ref, k_ref, v_ref):
    mn = _rms(m_ref[...], g_ref[...]).astype(BF16)
    k_ref[...] = jnp.dot(mn, wk_ref[...], preferred_element_type=F32).astype(BF16)
    v_ref[...] = jnp.dot(mn, wv_ref[...], preferred_element_type=F32).astype(BF16)


def mem_kv(mem, g, wk, wv):
    b, m, d = mem.shape
    spec = pl.BlockSpec((None, m, d), lambda i: (i, 0, 0))
    wspec = pl.BlockSpec((d, d), lambda i: (0, 0))
    return pl.pallas_call(
        _mem_kv_kernel,
        grid=(b,),
        in_specs=[spec, pl.BlockSpec((1, d), lambda i: (0, 0)), wspec, wspec],
        out_specs=[spec, spec],
        out_shape=[jax.ShapeDtypeStruct((b, m, d), BF16)] * 2,
        compiler_params=_cparams(("parallel",)),
        name="mem_kv",
    )(mem, g, wk, wv)


def _cross_kernel(x_ref, g_ref, wq_ref, k_ref, v_ref, wo_ref, o_ref):
    x = x_ref[...]
    d = x.shape[1]
    dh = d // X_HEADS
    h = _rms(x, g_ref[...]).astype(BF16)
    q = (jnp.dot(h, wq_ref[...], preferred_element_type=F32) * (dh ** -0.5)).astype(BF16)
    outs = []
    for hh in range(X_HEADS):
        sl = slice(hh * dh, (hh + 1) * dh)
        s = lax.dot_general(q[:, sl], k_ref[:, sl], (((1,), (1,)), ((), ())),
                            preferred_element_type=F32)
        p = jnp.exp(s - jnp.max(s, axis=1, keepdims=True))
        l = jnp.sum(p, axis=1, keepdims=True)
        o = jnp.dot(p.astype(BF16), v_ref[:, sl], preferred_element_type=F32) / l
        outs.append(o.astype(BF16))
    o = jnp.concatenate(outs, axis=1)
    o_ref[...] = x + jnp.dot(o, wo_ref[...], preferred_element_type=F32)


def _mix_cross_kernel(x_ref, ya_ref, yb_ref, ga_ref, gb_ref, wa_ref, wb_ref, wo_ref,
                      g_ref, wq_ref, k_ref, v_ref, wox_ref, o_ref, x1_ref):
    _mix_kernel(x_ref, ya_ref, yb_ref, ga_ref, gb_ref, wa_ref, wb_ref, wo_ref, x1_ref)
    _cross_kernel(x1_ref, g_ref, wq_ref, k_ref, v_ref, wox_ref, o_ref)


def mix_cross(x2d, ya, yb, pg, wa, wb, wo, g, wq, kx, vx, wox, tok0, tm=512):
    t = yb.shape[0]
    assert t % tm == 0 and tok0 % tm == 0
    d = x2d.shape[1]
    w = ya.shape[1]
    m = kx.shape[1]
    b0 = tok0 // tm
    const = lambda a: pl.BlockSpec(a.shape, lambda i: (0,) * a.ndim)
    kv = pl.BlockSpec((None, m, d), lambda i: (0, 0, 0))
    return pl.pallas_call(
        _mix_cross_kernel,
        grid=(t // tm,),
        in_specs=[
            pl.BlockSpec((tm, d), lambda i: (b0 + i, 0)),
            pl.BlockSpec((tm, w), lambda i: (i, 0)),
            pl.BlockSpec((tm, w), lambda i: (i, 0)),
            pl.BlockSpec((tm, d), lambda i: (b0 + i, 0)),
            pl.BlockSpec((tm, d), lambda i: (b0 + i, 1)),
            const(wa), const(wb), const(wo), const(g), const(wq), kv, kv, const(wox),
        ],
        out_specs=pl.BlockSpec((tm, d), lambda i: (i, 0)),
        out_shape=jax.ShapeDtypeStruct((t, d), F32),
        scratch_shapes=[pltpu.VMEM((tm, d), F32)],
        compiler_params=_cparams(("parallel",)),
        name="mix_cross",
    )(x2d, ya, yb, pg, pg, wa, wb, wo, g, wq, kx, vx, wox)


def _topk_rows(sc, k):
    n = sc.shape[0]
    io = lax.broadcasted_iota(I32, sc.shape, 0).astype(F32)
    vals, ids = [], []
    for _ in range(k):
        m = jnp.max(sc, axis=0, keepdims=True)
        ix = jnp.argmax(sc, axis=0, keepdims=True).astype(F32)
        vals.append(m)
        ids.append(ix)
        sc = jnp.where(io == ix, NEG_INF, sc)
    return jnp.concatenate(vals, axis=0), jnp.concatenate(ids, axis=0).astype(I32)


def _pack_bf16_halves(h):
    bits = lax.bitcast_convert_type(h, I32)
    r = bits + 0x7FFF + (lax.shift_right_logical(bits, 16) & 1)
    half = h.shape[1] // 2
    return lax.shift_right_logical(r[:, :half], 16) | (r[:, half:] & HI_MASK)


def _route_kernel(x_ref, g_ref, wq_ref, sk_ref, hp_ref, idx_ref, w_ref, hb_ref, it_ref, wt_ref):
    p = pl.program_id(1)

    @pl.when(p == 0)
    def _():
        h = _rms(x_ref[...], g_ref[...])
        hp_ref[...] = _pack_bf16_halves(h)
        hb_ref[...] = h.astype(BF16)

    qh = jnp.dot(hb_ref[...], wq_ref[...], preferred_element_type=F32)
    tops = []
    for c in range(2):
        seg = qh[:, c * PEER_HALF:(c + 1) * PEER_HALF]
        sc = lax.dot_general(sk_ref[c], seg, (((1,), (1,)), ((), ())),
                             precision=lax.Precision.HIGHEST, preferred_element_type=F32)
        tops.append(_topk_rows(sc, PEER_TOPK))
    (s0, i0), (s1, i1) = tops
    k = PEER_TOPK
    sub = 8
    tm = s0.shape[1]
    r8 = lax.broadcasted_iota(I32, (sub, tm), 0)
    r16 = lax.broadcasted_iota(I32, (k, tm), 0)
    cand_b = [s0[0:1] + s1, s0[1:2] + s1[:sub]]
    cidx_b = [i0[0:1] * PEER_NKEYS + i1, i0[1:2] * PEER_NKEYS + i1[:sub]]
    pos_b = [r16, k + r8]
    for a in range(2, sub):
        keep = r8 < (k // (a + 1))
        cand_b.append(jnp.where(keep, s0[a:a + 1] + s1[:sub], NEG_INF))
        cidx_b.append(i0[a:a + 1] * PEER_NKEYS + i1[:sub])
        pos_b.append(a * k + r8)
    cand_b.append(s0[sub:] + s1[0:1])
    cidx_b.append(i0[sub:] * PEER_NKEYS + i1[0:1])
    pos_b.append((sub + r8) * k)
    cand = jnp.concatenate(cand_b, axis=0)
    cidx = jnp.concatenate(cidx_b, axis=0)
    pos = jnp.concatenate(pos_b, axis=0).astype(F32)
    vals, ids = [], []
    for _ in range(k):
        m = jnp.max(cand, axis=0, keepdims=True)
        px = jnp.min(jnp.where(cand == m, pos, float(k * k)), axis=0, keepdims=True)
        hit = pos == px
        vals.append(m)
        ids.append(jnp.sum(jnp.where(hit, cidx, 0), axis=0, keepdims=True))
        cand = jnp.where(hit, NEG_INF, cand)
    sf = jnp.concatenate(vals, axis=0)
    e = jnp.exp(sf - sf[0:1])
    rows = pl.ds(pl.multiple_of(p * PEER_TOPK, PEER_TOPK), PEER_TOPK)
    wt_ref[rows, :] = e / jnp.sum(e, axis=0, keepdims=True)
    it_ref[rows, :] = jnp.concatenate(ids, axis=0)

    @pl.when(p == pl.num_programs(1) - 1)
    def _():
        idx_ref[...] = it_ref[...].T
        w_ref[...] = wt_ref[...].T


def peer_route(x2d, g, wq, sk, tok0, t, tm=1024):
    assert t % tm == 0 and tok0 % tm == 0
    d = x2d.shape[1]
    ph = sk.shape[0]
    nsel = ph * PEER_TOPK
    blk0 = tok0 // tm
    return pl.pallas_call(
        _route_kernel,
        grid=(t // tm, ph),
        in_specs=[
            pl.BlockSpec((tm, d), lambda i, p: (blk0 + i, 0)),
            pl.BlockSpec((1, d), lambda i, p: (0, 0)),
            pl.BlockSpec((d, 2 * PEER_HALF), lambda i, p: (0, p)),
            pl.BlockSpec((None, 2, PEER_NKEYS, PEER_HALF), lambda i, p: (p, 0, 0, 0)),
        ],
        out_specs=[
            pl.BlockSpec((tm, d // 2), lambda i, p: (i, 0)),
            pl.BlockSpec((tm, nsel), lambda i, p: (i, 0)),
            pl.BlockSpec((tm, nsel), lambda i, p: (i, 0)),
        ],
        out_shape=[jax.ShapeDtypeStruct((t, d // 2), I32),
                   jax.ShapeDtypeStruct((t, nsel), I32),
                   jax.ShapeDtypeStruct((t, nsel), F32)],
        scratch_shapes=[pltpu.VMEM((tm, d), BF16),
                        pltpu.VMEM((nsel, tm), I32),
                        pltpu.VMEM((nsel, tm), F32)],
        compiler_params=_cparams(("parallel", "arbitrary")),
        name="peer_route",
    )(x2d, g, wq, sk)


def _final_kernel(x_ref, y_ref, g_ref, *rest):
    o_ref = rest[-1]
    o_ref[...] = _rms(x_ref[...] + y_ref[...], g_ref[...])


def final_norm_into(out, xs, y, g, row0, total, tm=512):
    t, d = y.shape
    assert t % tm == 0 and row0 % tm == 0 and total % tm == 0
    blk0 = row0 // tm
    spec = pl.BlockSpec((tm, d), lambda i: (i, 0))
    in_specs = [spec, spec, pl.BlockSpec((1, d), lambda i: (0, 0))]
    args = [xs, y, g]
    aliases = {}
    if out is not None:
        in_specs.append(pl.BlockSpec(memory_space=pl.ANY))
        args.append(out)
        aliases = {3: 0}
    return pl.pallas_call(
        _final_kernel, grid=(t // tm,),
        in_specs=in_specs,
        out_specs=pl.BlockSpec((tm, d), lambda i: (blk0 + i, 0)),
        out_shape=jax.ShapeDtypeStruct((total, d), F32),
        input_output_aliases=aliases,
        compiler_params=_cparams(("parallel",)), name="final_norm",
    )(*args)


SC_CORES = 2
SC_SUBCORES = 16
SC_WORKERS = SC_CORES * SC_SUBCORES
SC_LANES = 16
SC_GROUP = 32


def _sc_mesh():
    return plsc.VectorSubcoreMesh(core_axis_name="c", subcore_axis_name="s")


def _sc_params():
    return pltpu.CompilerParams(needs_layout_passes=False)


def _sc_worker_id():
    return lax.axis_index("s") * SC_CORES + lax.axis_index("c")


SC_ROW_LANE = 128


def _sc_unit_off(u):
    off = u * SC_LANES
    return off if isinstance(off, int) else pl.multiple_of(off, SC_LANES)


GELU_C0 = math.sqrt(2.0 / math.pi)
GELU_C1 = 0.044715


def _gelu_tanh(x):
    z = GELU_C0 * (x + GELU_C1 * (x * x * x))
    th = 1.0 - 2.0 / (jnp.exp(2.0 * z) + 1.0)
    return 0.5 * x * (1.0 + th)


SC_PK_RING = 4
SC_PK_SUB = 4
HI_MASK = -65536


def _pack_tables_kernel(u_ref, v_ref, o_ref):
    for part, ref in enumerate((u_ref, v_ref)):
        words = _pack_bf16_halves(ref[...])
        for sub in range(SC_PK_SUB):
            o_ref[:, part * SC_PK_SUB + sub, :] = words[:, sub * SC_ROW_LANE:(sub + 1) * SC_ROW_LANE]


def pack_expert_tables(u, v, te=512):
    e, d = u.shape
    assert d == 2 * SC_PK_SUB * SC_ROW_LANE
    spec = pl.BlockSpec((te, d), lambda i: (i, 0))
    return pl.pallas_call(
        _pack_tables_kernel, grid=(e // te,), in_specs=[spec, spec],
        out_specs=pl.BlockSpec((te, 2 * SC_PK_SUB, SC_ROW_LANE), lambda i: (i, 0, 0)),
        out_shape=jax.ShapeDtypeStruct((e, 2 * SC_PK_SUB, SC_ROW_LANE), I32),
        compiler_params=_cparams(("parallel",)), name="pack_expert_tables",
    )(u, v)


def _unpack_halves(x32):
    w = plsc.bitcast(x32, I32)
    return plsc.bitcast(w << 16, F32), plsc.bitcast(w & HI_MASK, F32)


def _tree_sum(xs):
    while len(xs) > 1:
        xs = [xs[i] + xs[i + 1] for i in range(0, len(xs), 2)]
    return xs[0]


def peer_experts_pk_sc(tab_uv, idx_flat, w_flat, hp, d):
    t = hp.shape[0]
    nsel = PEER_SEL
    assert t % SC_WORKERS == 0 and d == 2 * SC_PK_SUB * SC_ROW_LANE
    tpw = t // SC_WORKERS
    g = SC_GROUP if tpw % SC_GROUP == 0 else SC_GROUP // 2
    assert tpw % g == 0
    groups = tpw // g
    heads = nsel // SC_LANES
    chunks = d // 32
    units = g * heads
    ring = SC_PK_RING
    assert units % ring == 0
    row_buf = pltpu.VMEM((SC_LANES, 2 * SC_PK_SUB, SC_ROW_LANE), I32)

    def row_words(rows, r, wc, sub0):
        per = SC_ROW_LANE // SC_LANES
        return plsc.bitcast(
            rows[r, sub0 + wc // per, pl.ds(pl.multiple_of((wc % per) * SC_LANES, SC_LANES), SC_LANES)], BF16)

    def ring_loop(n_units, start, wait, compute):
        for u in range(ring - 1):
            start(u, u)

        @pl.loop(0, n_units, step=ring)
        def _(uu):
            for b in range(ring):
                u = uu + b
                nxt = u + (ring - 1)

                @pl.when(nxt < n_units)
                def _():
                    start(nxt, (b + ring - 1) % ring)

                wait(u, b)
                compute(u, b)

    @functools.partial(
        pl.kernel, mesh=_sc_mesh(),
        out_type=jax.ShapeDtypeStruct((t, d), F32),
        scratch_types=[
            pltpu.VMEM((g * nsel,), I32),
            pltpu.VMEM((g * nsel,), F32),
            pltpu.VMEM((g, d // 2), I32),
            pltpu.VMEM((g, d), F32),
            pltpu.VMEM((SC_LANES * SC_LANES,), F32),
            [row_buf] * ring,
            [pltpu.SemaphoreType.DMA] * ring,
        ],
        compiler_params=_sc_params(),
        name="peer_experts_pk_sc",
    )
    def k(tab_hbm, idx_hbm, w_hbm, h_hbm, out_hbm, idx_v, coef_v, h_v, y_v, red_v, rows, sems):
        wid = _sc_worker_id()
        lane = lax.iota(I32, SC_LANES)

        def copy(u, slot):
            ids = idx_v.at[pl.ds(_sc_unit_off(u), SC_LANES)]
            return pltpu.make_async_copy(tab_hbm.at[ids], rows[slot], sems[slot])

        def dots(u, slot):
            tt = u // heads

            def body(cp, accs):
                out = []
                hv = [plsc.bitcast(h_v[tt, pl.ds(pl.multiple_of((2 * cp + i) * SC_LANES, SC_LANES), SC_LANES)], BF16)
                      for i in range(2)]
                for r in range(SC_LANES):
                    pr = (row_words(rows[slot], r, 2 * cp, 0) * hv[0]
                          + row_words(rows[slot], r, 2 * cp + 1, 0) * hv[1])
                    lo, hi = _unpack_halves(pr)
                    out.append(accs[r] + lo + hi)
                return tuple(out)

            accs = lax.fori_loop(0, chunks // 2, body,
                                 tuple(jnp.zeros((SC_LANES,), F32) for _ in range(SC_LANES)))
            for r in range(SC_LANES):
                red_v[pl.ds(r * SC_LANES, SC_LANES)] = accs[r]
            act = _tree_sum([plsc.load_gather(red_v, [lane * SC_LANES + j]) for j in range(SC_LANES)])
            sl = pl.ds(_sc_unit_off(u), SC_LANES)
            coef_v[sl] = coef_v[sl] * _gelu_tanh(act)

        def combine(u, slot):
            tt = u // heads
            first = (u % heads) == 0
            cb = []
            for r in range(SC_LANES):
                c = plsc.load_gather(coef_v, [jnp.full((SC_LANES,), u * SC_LANES + r, I32)])
                cb.append(plsc.pack(c, c, format=plsc.PackFormat.INTERLEAVED))

            @plsc.parallel_loop(0, chunks, unroll=2)
            def _(wc):
                lo, hi = _unpack_halves(
                    _tree_sum([cb[r] * row_words(rows[slot], r, wc, SC_PK_SUB) for r in range(SC_LANES)]))
                for half, val in ((0, lo), (1, hi)):
                    sl = pl.ds(pl.multiple_of(half * (d // 2) + wc * SC_LANES, SC_LANES), SC_LANES)
                    y_v[tt, sl] = val + jnp.where(first, 0.0, y_v[tt, sl])

        def unit(u, slot):
            dots(u, slot)
            combine(u, slot)

        @pl.loop(0, groups)
        def _(gi):
            base = wid * tpw + gi * g
            pltpu.sync_copy(idx_hbm.at[pl.ds(base * nsel, g * nsel)], idx_v)
            pltpu.sync_copy(w_hbm.at[pl.ds(base * nsel, g * nsel)], coef_v)
            pltpu.sync_copy(h_hbm.at[pl.ds(base, g)], h_v)
            ring_loop(units, lambda u, s: copy(u, s).start(), lambda u, s: copy(u, s).wait(), unit)
            pltpu.sync_copy(y_v, out_hbm.at[pl.ds(base, g)])

    return k(tab_uv, idx_flat, w_flat, hp)


def kernel(x, mem, rel_bias, ln_mix, w_in, hg_lower, hg_norm, w_up_a, w_up_b, w_out, ln_cross, ln_mem, wq_x, wk_x, wv_x, wo_x, ln_ffn, peer_query, peer_subkeys, peer_u, peer_v, ln_final):
    b, s, d = x.shape
    depth = w_in.shape[0]
    assert depth == 1, "the residual after PEER is fused into the final norm"
    assert s % MB_BLOCK == 0 and s % HG_CHUNK == 0 and s % (PEER_SLICES * SC_WORKERS * SC_GROUP) == 0
    nb = s // MB_BLOCK
    row = lambda a: a.reshape(1, -1).astype(F32)
    lb_all = jnp.cumsum(jax.nn.softmax(hg_lower.astype(F32), axis=0), axis=0)
    bias = moba_bias_tiles(rel_bias)
    n_hg = 4 * HG_WIDTH
    n_qk = 2 * MB_WIDTH
    n_mb = 3 * MB_WIDTH
    l = 0
    w = cast_bf16(w_in[l].astype(F32))
    w_vt = w[:, n_hg + n_qk:n_hg + n_mb].T
    wa, wb, wo = w_up_a[l].astype(BF16), w_up_b[l].astype(BF16), w_out[l].astype(BF16)
    wqx, wox = wq_x[l].astype(BF16), wo_x[l].astype(BF16)
    wpq, sk = peer_query[l].astype(BF16), peer_subkeys[l].astype(F32)
    tab_uv = pack_expert_tables(peer_u[l].astype(F32), peer_v[l].astype(F32))
    kx, vx = mem_kv(mem, row(ln_mem[l]), wk_x[l].astype(BF16), wv_x[l].astype(BF16))

    outs = []
    for bi in range(b):
        x2d = x[bi]
        p0, pqk, km, vt, pg = in_proj(x2d, row(ln_mix[l]), w, w_vt)
        hg_state = jnp.zeros((HG_HEADS, HG_D, HG_D), F32)
        km = km.reshape(1, nb, MB_WIDTH)
        sizes = [s // PEER_SLICES] * PEER_SLICES
        if bi == b - 1:
            first = SC_WORKERS * SC_GROUP // 2
            sizes = [first, sizes[0] - first] + sizes[1:]
        tok0 = 0
        for ts in sizes:
            ya, hg_state = hgrn2(p0, row(lb_all[l]), row(hg_norm[l]), hg_state, tok0, ts)
            yb = moba_attention(pqk, vt, km, bias, 1, s, tok0 // MB_BLOCK, ts // MB_BLOCK)
            xs = mix_cross(x2d, ya, yb, pg, wa, wb, wo, row(ln_cross[l]), wqx, kx[bi:bi + 1], vx[bi:bi + 1], wox, tok0)
            hp, eidx, wts = peer_route(xs, row(ln_ffn[l]), wpq, sk, 0, ts, math.gcd(ts, 1024))
            y = peer_experts_pk_sc(tab_uv, eidx.reshape(ts * PEER_SEL), wts.reshape(ts * PEER_SEL), hp, d)
            outs.append((xs, y, bi * s + tok0))
            tok0 += ts
    out = None
    for xs, y, row0 in sorted(outs, key=lambda e: (-(e[2] // s), e[2])):
        out = final_norm_into(out, xs, y, row(ln_final), row0, b * s)
    return out.reshape(b, s, d)
```

```python
import functools
import math

import jax
import jax.numpy as jnp
from jax import lax
from jax.experimental import pallas as pl
from jax.experimental.pallas import tpu as pltpu
from jax.experimental.pallas import tpu_sc as plsc

F32 = jnp.float32
BF16 = jnp.bfloat16
I32 = jnp.int32
EPS = 1e-6
NEG_INF = float("-inf")

HG_HEADS = 4
HG_D = 128
HG_WIDTH = HG_HEADS * HG_D
HG_CHUNK = 64
HG_SUB = 16
MB_HEADS = 8
MB_DH = 64
MB_WIDTH = MB_HEADS * MB_DH
MB_BLOCK = 256
MB_TOPK = 3
MB_BIAS_TILES = 8
REL_BUCKETS = 32
REL_MAX_DIST = 2048
X_HEADS = 4
PEER_HEADS = 8
PEER_NKEYS = 128
PEER_TOPK = 16
PEER_HALF = 128
PEER_SEL = PEER_HEADS * PEER_TOPK
PEER_SLICES = 4

VMEM_LIMIT = 56 * 1024 * 1024


def _cparams(sem):
    return pltpu.CompilerParams(dimension_semantics=sem, vmem_limit_bytes=VMEM_LIMIT)


def _rms(x, g):
    ms = jnp.mean(x * x, axis=-1, keepdims=True)
    return x * lax.rsqrt(ms + EPS) * g


def _cast_kernel(w_ref, o_ref):
    o_ref[...] = w_ref[...].astype(o_ref.dtype)


def cast_bf16(w, tn=512):
    k, n = w.shape
    assert n % tn == 0
    spec = pl.BlockSpec((k, tn), lambda j: (0, j))
    return pl.pallas_call(
        _cast_kernel, grid=(n // tn,), in_specs=[spec], out_specs=spec,
        out_shape=jax.ShapeDtypeStruct((k, n), BF16),
        compiler_params=_cparams(("parallel",)), name="cast_bf16",
    )(w)


def _in_proj_kernel(x_ref, g_ref, w0_ref, w1_ref, wvt_ref, *rest):
    w2_refs, (o0_ref, o1_ref, okm_ref, ovt_ref, o2_ref) = rest[:-5], rest[-5:]
    h = _rms(x_ref[...], g_ref[...]).astype(BF16)
    o0_ref[...] = jnp.dot(h, w0_ref[...], preferred_element_type=F32)
    qk = jnp.dot(h, w1_ref[...], preferred_element_type=F32)
    o1_ref[...] = qk.astype(BF16)
    okm_ref[0] = jnp.mean(qk[:, MB_WIDTH:], axis=0, keepdims=True)
    vt = lax.dot_general(wvt_ref[...], h, (((1,), (1,)), ((), ())), preferred_element_type=F32).astype(BF16)
    for hd in range(MB_HEADS):
        ovt_ref[0, hd * MB_VROWS:hd * MB_VROWS + MB_DH, :] = vt[hd * MB_DH:(hd + 1) * MB_DH]
        ovt_ref[0, hd * MB_VROWS + MB_DH:(hd + 1) * MB_VROWS, :] = jnp.ones((MB_ONES, vt.shape[1]), BF16)
    wg = w2_refs[0].shape[1]
    for j, w2_ref in enumerate(w2_refs):
        o2_ref[:, j * wg:(j + 1) * wg] = jnp.dot(h, w2_ref[...], preferred_element_type=F32).astype(BF16)


def in_proj(x2d, g, w, wvt):
    t, d = x2d.shape
    tm = MB_BLOCK
    n0, n1, nv, n2 = 4 * HG_WIDTH, 2 * MB_WIDTH, MB_VT_ROWS, 2 * d
    wg = MB_WIDTH
    assert wvt.shape == (MB_WIDTH, d) and w.shape == (d, n0 + n1 + MB_WIDTH + n2)
    assert n0 % n1 == 0 and (n0 + n1 + MB_WIDTH) % wg == 0 and n2 % wg == 0
    full = lambda a: pl.BlockSpec(a.shape, lambda i: (0, 0))
    g0 = (n0 + n1 + MB_WIDTH) // wg
    w_specs = ([pl.BlockSpec((d, n0), lambda i: (0, 0)), pl.BlockSpec((d, n1), lambda i: (0, n0 // n1)), full(wvt)]
               + [pl.BlockSpec((d, wg), lambda i, j=j: (0, g0 + j)) for j in range(n2 // wg)])
    return pl.pallas_call(
        _in_proj_kernel,
        grid=(t // tm,),
        in_specs=[pl.BlockSpec((tm, d), lambda i: (i, 0)), full(g)] + w_specs,
        out_specs=[pl.BlockSpec((tm, n0), lambda i: (i, 0)),
                   pl.BlockSpec((tm, n1), lambda i: (i, 0)),
                   pl.BlockSpec((1, 1, MB_WIDTH), lambda i: (i, 0, 0)),
                   pl.BlockSpec((1, nv, tm), lambda i: (i, 0, 0)),
                   pl.BlockSpec((tm, n2), lambda i: (i, 0))],
        out_shape=[jax.ShapeDtypeStruct((t, n0), F32),
                   jax.ShapeDtypeStruct((t, n1), BF16),
                   jax.ShapeDtypeStruct((t // tm, 1, MB_WIDTH), F32),
                   jax.ShapeDtypeStruct((t // tm, nv, tm), BF16),
                   jax.ShapeDtypeStruct((t, n2), BF16)],
        compiler_params=_cparams(("parallel",)),
        name="in_proj",
    )(x2d, g, w, w, wvt, *([w] * (n2 // wg)))


def _hgrn_kernel(q_ref, f_ref, i_ref, g_ref, lb_ref, gain_ref, st0_ref, o_ref, stn_ref, st_ref):
    c = pl.program_id(0)

    @pl.when(c == 0)
    def _():
        st_ref[...] = st0_ref[...]

    C, S = HG_CHUNK, HG_SUB
    row = lax.broadcasted_iota(I32, (C, C), 0)
    col = lax.broadcasted_iota(I32, (C, C), 1)
    tril = (row >= col).astype(F32)
    t_iota = lax.broadcasted_iota(I32, (S, 1), 0)

    for h in range(HG_HEADS):
        sl = slice(h * HG_D, (h + 1) * HG_D)
        q = q_ref[:, sl]
        v = i_ref[:, sl]
        lb = lb_ref[:, sl]
        f = lb + (1.0 - lb) * jax.nn.sigmoid(f_ref[:, sl])
        lf = jnp.log(f)
        k = 1.0 - f
        b = jnp.dot(tril, lf, precision=lax.Precision.HIGHEST, preferred_element_type=F32)
        st = st_ref[h]
        vb = v.astype(BF16)
        qd = (q * jnp.exp(b)).astype(BF16)
        o_inter = lax.dot_general(qd, st.astype(BF16), (((1,), (1,)), ((), ())),
                                  preferred_element_type=F32)
        outs = []
        for i in range(C // S):
            r0 = i * S
            qi = q[r0:r0 + S]
            ki = k[r0:r0 + S]
            bi = b[r0:r0 + S]
            vi = v[r0:r0 + S]
            oi = o_inter[r0:r0 + S]
            if i > 0:
                bs = b[r0 - 1:r0]
                qh = (qi * jnp.exp(bi - bs)).astype(BF16)
                kh = (k[:r0] * jnp.exp(bs - b[:r0])).astype(BF16)
                a = lax.dot_general(qh, kh, (((1,), (1,)), ((), ())), preferred_element_type=F32)
                oi = oi + jnp.dot(a.astype(BF16), vb[:r0], preferred_element_type=F32)
            half = S // 2
            o_half = [oi[:half], oi[half:]]
            for s in range(S):
                for hf in range(s // half, 2):
                    rows = slice(hf * half, (hf + 1) * half)
                    dec = jnp.exp(jnp.minimum(bi[rows] - bi[s:s + 1], 0.0))
                    a_s = jnp.sum(qi[rows] * ki[s:s + 1] * dec, axis=-1, keepdims=True)
                    a_s = jnp.where(t_iota[rows] >= s, a_s, 0.0)
                    o_half[hf] = o_half[hf] + a_s * vi[s:s + 1]
            outs.extend(o_half)
        o = jnp.concatenate(outs, axis=0)
        b_end = b[C - 1:C]
        kd = (k * jnp.exp(b_end - b)).astype(BF16)
        upd = lax.dot_general(vb, kd, (((0,), (0,)), ((), ())), preferred_element_type=F32)
        st_ref[h] = st * jnp.exp(b_end) + upd
        o = o * lax.rsqrt(jnp.mean(o * o, axis=-1, keepdims=True) + EPS)
        g = g_ref[:, sl]
        o_ref[:, sl] = (o * gain_ref[:, sl] * (g * jax.nn.sigmoid(g))).astype(o_ref.dtype)

    @pl.when(c == pl.num_programs(0) - 1)
    def _():
        stn_ref[...] = st_ref[...]


def hgrn2(p0, lb, gain, state, tok0, t):
    assert t % HG_CHUNK == 0 and tok0 % HG_CHUNK == 0
    nc = t // HG_CHUNK
    c0 = tok0 // HG_CHUNK
    w = HG_WIDTH

    def col(j):
        return pl.BlockSpec((HG_CHUNK, w), lambda c, j=j: (c0 + c, j))

    st_spec = pl.BlockSpec(state.shape, lambda c: (0, 0, 0))
    return pl.pallas_call(
        _hgrn_kernel,
        grid=(nc,),
        in_specs=[col(0), col(1), col(2), col(3),
                  pl.BlockSpec((1, w), lambda c: (0, 0)),
                  pl.BlockSpec((1, w), lambda c: (0, 0)),
                  st_spec],
        out_specs=[pl.BlockSpec((HG_CHUNK, w), lambda c: (c, 0)), st_spec],
        out_shape=[jax.ShapeDtypeStruct((t, w), BF16), jax.ShapeDtypeStruct(state.shape, F32)],
        scratch_shapes=[pltpu.VMEM((HG_HEADS, HG_D, HG_D), F32)],
        compiler_params=_cparams(("arbitrary",)),
        name="hgrn2",
    )(p0, p0, p0, p0, lb, gain, state)


MB_PAIR = 4
MB_PW = MB_PAIR * MB_DH
MB_LG = 128
MB_ONES = 16
MB_VROWS = MB_DH + MB_ONES
MB_VT_ROWS = MB_HEADS * MB_VROWS


def _moba_kernel(q_ref, k_ref, vt_ref, km_ref, bias_ref, o_ref, *scratch, qb0):
    m_ref, l_ref, al_ref, acc_ref, msk_ref, s_ref, p_ref = (
        scratch[i * MB_PAIR:(i + 1) * MB_PAIR] for i in range(7))
    qi = pl.program_id(2) + qb0
    nb = km_ref.shape[0]
    blk = MB_BLOCK
    heads = range(MB_PAIR)
    grp = lambda hh: slice((hh // 2) * MB_LG, (hh // 2 + 1) * MB_LG)
    q = q_ref[...]
    lane = lax.broadcasted_iota(I32, (blk, MB_LG), 1)
    in_head = [(lane < MB_DH) if hh % 2 == 0 else (lane >= MB_DH) for hh in heads]
    qs = q * jnp.asarray(MB_DH ** -0.5, BF16)
    nt = (((1,), (1,)), ((), ()))
    qf = q.astype(F32)
    qht = [jnp.where(in_head[hh], qs[:, grp(hh)].astype(F32), 0.0).T.astype(BF16) for hh in heads]

    n_io = lax.broadcasted_iota(I32, (nb, blk), 0)
    for hh in heads:
        gate = lax.dot_general(km_ref[:, grp(hh)], jnp.where(in_head[hh], qf[:, grp(hh)], 0.0), nt,
                               precision=lax.Precision.HIGHEST, preferred_element_type=F32)
        gate = jnp.where(n_io < qi, gate, NEG_INF)
        chosen = n_io < 0
        for _ in range(MB_TOPK):
            mx = jnp.max(gate, axis=0, keepdims=True)
            ix = jnp.min(jnp.where(gate == mx, n_io, nb), axis=0, keepdims=True)
            hit = n_io == ix
            chosen = chosen | (hit & (mx > NEG_INF))
            gate = jnp.where(hit, NEG_INF, gate)
        msk_ref[hh][...] = jnp.where(chosen, 0.0, NEG_INF)

    vrows = lambda hh: slice(hh * MB_VROWS, (hh + 1) * MB_VROWS)

    def pv_stage(blk_idx):
        vtb = vt_ref[blk_idx]
        r = [jnp.dot(vtb[vrows(hh)], p_ref[hh][...], preferred_element_type=F32) for hh in heads]
        al = [al_ref[hh][...] for hh in heads]
        a_new = [al[hh] * acc_ref[hh][...] + r[hh][:MB_DH] for hh in heads]
        l_new = [al[hh] * l_ref[hh][...] + r[hh][MB_DH:MB_DH + 1] for hh in heads]
        return a_new, l_new

    def store_pv(a_new, l_new):
        for hh in heads:
            acc_ref[hh][...] = a_new[hh]
            l_ref[hh][...] = l_new[hh]

    def softmax_stage():
        s = [s_ref[hh][...] for hh in heads]
        m_old = [m_ref[hh][...] for hh in heads]
        m_new = [jnp.maximum(m_old[hh], jnp.max(s[hh], axis=0, keepdims=True)) for hh in heads]
        alpha = [jnp.exp(m_old[hh] - m_new[hh]) for hh in heads]
        p = [jnp.exp((s[hh] - m_new[hh]).astype(BF16)) for hh in heads]
        return p, alpha, m_new

    def store_softmax(p, alpha, m_new):
        for hh in heads:
            p_ref[hh][...] = p[hh]
            al_ref[hh][...] = alpha[hh]
            m_ref[hh][...] = m_new[hh]

    k_own = k_ref[pl.ds(pl.multiple_of(qi * blk, blk), blk), :]
    key_io = lax.broadcasted_iota(I32, (blk, blk), 0)
    qry_io = lax.broadcasted_iota(I32, (blk, blk), 1)
    for hh in heads:
        s = jnp.dot(k_own[:, grp(hh)], qht[hh], preferred_element_type=F32) + bias_ref[hh, 0]
        s_ref[hh][...] = jnp.where(key_io <= qry_io, s, NEG_INF)
        m_ref[hh][...] = jnp.full((1, blk), NEG_INF, F32)
        l_ref[hh][...] = jnp.zeros((1, blk), F32)
        al_ref[hh][...] = jnp.ones((1, blk), F32)
        acc_ref[hh][...] = jnp.zeros((MB_DH, blk), F32)
        p_ref[hh][...] = jnp.zeros((blk, blk), BF16)

    def step(i, carry, far):
        pv = pv_stage(jnp.where(i <= 1, qi, i - 2))
        sm = softmax_stage()
        kn = k_ref[pl.ds(pl.multiple_of(i * blk, blk), blk), :]
        if far:
            row = [msk_ref[hh][pl.ds(i, 1), :] + bias_ref[hh, MB_BIAS_TILES - 1, 0:1, 0:1] for hh in heads]
            s_next = [jnp.dot(kn[:, grp(hh)], qht[hh], preferred_element_type=F32) + row[hh] for hh in heads]
        else:
            d = qi - i
            s_next = [jnp.dot(kn[:, grp(hh)], qht[hh], preferred_element_type=F32)
                      + bias_ref[hh, d] + msk_ref[hh][pl.ds(i, 1), :] for hh in heads]
        store_pv(*pv)
        for hh in heads:
            s_ref[hh][...] = s_next[hh]
        store_softmax(*sm)
        return carry

    n_far = jnp.maximum(qi - (MB_BIAS_TILES - 2), 0)
    lax.fori_loop(0, n_far, functools.partial(step, far=True), 0)
    lax.fori_loop(n_far, qi, functools.partial(step, far=False), 0)
    pv = pv_stage(jnp.where(qi <= 1, qi, qi - 2))
    sm = softmax_stage()
    store_pv(*pv)
    store_softmax(*sm)
    a_fin, l_fin = pv_stage(jnp.where(qi == 0, qi, qi - 1))
    out_t = jnp.concatenate([a_fin[hh] / l_fin[hh] for hh in heads], axis=0)
    o_ref[...] = out_t.T.astype(o_ref.dtype)


def moba_attention(pqk, vt, km, bias, batch, seq, qb0=0, nqb=None):
    nb = seq // MB_BLOCK
    nqb = nb if nqb is None else nqb
    t = batch * nqb * MB_BLOCK
    groups = MB_WIDTH // MB_PW
    return pl.pallas_call(
        functools.partial(_moba_kernel, qb0=qb0),
        grid=(batch, groups, nqb),
        in_specs=[
            pl.BlockSpec((MB_BLOCK, MB_PW), lambda b, j, i: (b * nb + qb0 + i, j)),
            pl.BlockSpec((seq, MB_PW), lambda b, j, i: (b, groups + j)),
            pl.BlockSpec((nb, MB_PAIR * MB_VROWS, MB_BLOCK), lambda b, j, i: (b, j, 0)),
            pl.BlockSpec((None, nb, MB_PW), lambda b, j, i: (b, 0, j)),
            pl.BlockSpec((MB_PAIR, MB_BIAS_TILES, MB_BLOCK, MB_BLOCK), lambda b, j, i: (j, 0, 0, 0)),
        ],
        out_specs=pl.BlockSpec((MB_BLOCK, MB_PW), lambda b, j, i: (b * nqb + i, j)),
        out_shape=jax.ShapeDtypeStruct((t, MB_WIDTH), BF16),
        scratch_shapes=(
            [pltpu.VMEM((1, MB_BLOCK), F32)] * (3 * MB_PAIR)
            + [pltpu.VMEM((MB_DH, MB_BLOCK), F32)] * MB_PAIR
            + [pltpu.VMEM((nb, MB_BLOCK), F32)] * MB_PAIR
            + [pltpu.VMEM((MB_BLOCK, MB_BLOCK), F32)] * MB_PAIR
            + [pltpu.VMEM((MB_BLOCK, MB_BLOCK), BF16)] * MB_PAIR
        ),
        compiler_params=_cparams(("parallel", "parallel", "arbitrary")),
        name="moba_attn",
    )(pqk, pqk, vt, km, bias)


def _t5_bucket(dist):
    max_exact = REL_BUCKETS // 2
    scaled = jnp.log(jnp.maximum(dist, 1).astype(F32) / max_exact) / math.log(REL_MAX_DIST / max_exact)
    large = jnp.minimum(max_exact + (scaled * (REL_BUCKETS - max_exact)).astype(I32), REL_BUCKETS - 1)
    return jnp.where(dist < max_exact, dist, large)


def moba_bias_tiles(rel_bias):
    blk = MB_BLOCK
    span = 2 * blk - 1
    x = jnp.arange(span) - (blk - 1)
    dist = jnp.maximum(jnp.arange(MB_BIAS_TILES)[:, None] * blk + x[None, :], 0)
    w = rel_bias.astype(F32).T[:, _t5_bucket(dist)]
    h = w.shape[0]
    wp = jnp.pad(w, ((0, 0), (0, 0), (0, 1)))[:, :, None, :]
    return pl.pallas_call(
        _toeplitz_kernel,
        grid=(h, MB_BIAS_TILES),
        in_specs=[pl.BlockSpec((None, None, 1, 2 * blk), lambda i, j: (i, j, 0, 0))],
        out_specs=pl.BlockSpec((None, None, blk, blk), lambda i, j: (i, j, 0, 0)),
        out_shape=jax.ShapeDtypeStruct((h, MB_BIAS_TILES, blk, blk), F32),
        compiler_params=_cparams(("parallel", "parallel")),
        name="moba_bias_tiles",
    )(wp)


def _toeplitz_kernel(w_ref, o_ref):
    blk = o_ref.shape[0]
    x = jnp.broadcast_to(w_ref[...], (blk, 2 * blk))
    o_ref[...] = pltpu.roll(x, 1, 1, stride=1, stride_axis=0)[:, blk:]


def _mix_kernel(x_ref, ya_ref, yb_ref, ga_ref, gb_ref, wa_ref, wb_ref, wo_ref, o_ref):
    za = jnp.dot(ya_ref[...], wa_ref[...], preferred_element_type=F32)
    zb = jnp.dot(yb_ref[...], wb_ref[...], preferred_element_type=F32)
    z = jax.nn.sigmoid(ga_ref[...].astype(F32)) * za + jax.nn.sigmoid(gb_ref[...].astype(F32)) * zb
    o_ref[...] = x_ref[...] + jnp.dot(z.astype(BF16), wo_ref[...], preferred_element_type=F32)


def _mem_kv_kernel(m_ref, g_ref, wk_ref, wv_ref, k_ref, v_ref):
    mn = _rms(m_ref[...], g_ref[...]).astype(BF16)
    k_ref[...] = jnp.dot(mn, wk_ref[...], preferred_element_type=F32).astype(BF16)
    v_ref[...] = jnp.dot(mn, wv_ref[...], preferred_element_type=F32).astype(BF16)


def mem_kv(mem, g, wk, wv):
    b, m, d = mem.shape
    spec = pl.BlockSpec((None, m, d), lambda i: (i, 0, 0))
    wspec = pl.BlockSpec((d, d), lambda i: (0, 0))
    return pl.pallas_call(
        _mem_kv_kernel,
        grid=(b,),
        in_specs=[spec, pl.BlockSpec((1, d), lambda i: (0, 0)), wspec, wspec],
        out_specs=[spec, spec],
        out_shape=[jax.ShapeDtypeStruct((b, m, d), BF16)] * 2,
        compiler_params=_cparams(("parallel",)),
        name="mem_kv",
    )(mem, g, wk, wv)


def _cross_kernel(x_ref, g_ref, wq_ref, k_ref, v_ref, wo_ref, o_ref):
    x = x_ref[...]
    d = x.shape[1]
    dh = d // X_HEADS
    h = _rms(x, g_ref[...]).astype(BF16)
    q = (jnp.dot(h, wq_ref[...], preferred_element_type=F32) * (dh ** -0.5)).astype(BF16)
    outs = []
    for hh in range(X_HEADS):
        sl = slice(hh * dh, (hh + 1) * dh)
        s = lax.dot_general(q[:, sl], k_ref[:, sl], (((1,), (1,)), ((), ())),
                            preferred_element_type=F32)
        p = jnp.exp(s - jnp.max(s, axis=1, keepdims=True))
        l = jnp.sum(p, axis=1, keepdims=True)
        o = jnp.dot(p.astype(BF16), v_ref[:, sl], preferred_element_type=F32) / l
        outs.append(o.astype(BF16))
    o = jnp.concatenate(outs, axis=1)
    o_ref[...] = x + jnp.dot(o, wo_ref[...], preferred_element_type=F32)


def _mix_cross_kernel(x_ref, ya_ref, yb_ref, ga_ref, gb_ref, wa_ref, wb_ref, wo_ref,
                      g_ref, wq_ref, k_ref, v_ref, wox_ref, o_ref, x1_ref):
    _mix_kernel(x_ref, ya_ref, yb_ref, ga_ref, gb_ref, wa_ref, wb_ref, wo_ref, x1_ref)
    _cross_kernel(x1_ref, g_ref, wq_ref, k_ref, v_ref, wox_ref, o_ref)


def mix_cross(x2d, ya, yb, pg, wa, wb, wo, g, wq, kx, vx, wox, tok0, tm=512):
    t = yb.shape[0]
    assert t % tm == 0 and tok0 % tm == 0
    d = x2d.shape[1]
    w = ya.shape[1]
    m = kx.shape[1]
    b0 = tok0 // tm
    const = lambda a: pl.BlockSpec(a.shape, lambda i: (0,) * a.ndim)
    kv = pl.BlockSpec((None, m, d), lambda i: (0, 0, 0))
    return pl.pallas_call(
        _mix_cross_kernel,
        grid=(t // tm,),
        in_specs=[
            pl.BlockSpec((tm, d), lambda i: (b0 + i, 0)),
            pl.BlockSpec((tm, w), lambda i: (i, 0)),
            pl.BlockSpec((tm, w), lambda i: (i, 0)),
            pl.BlockSpec((tm, d), lambda i: (b0 + i, 0)),
            pl.BlockSpec((tm, d), lambda i: (b0 + i, 1)),
            const(wa), const(wb), const(wo), const(g), const(wq), kv, kv, const(wox),
        ],
        out_specs=pl.BlockSpec((tm, d), lambda i: (i, 0)),
        out_shape=jax.ShapeDtypeStruct((t, d), F32),
        scratch_shapes=[pltpu.VMEM((tm, d), F32)],
        compiler_params=_cparams(("parallel",)),
        name="mix_cross",
    )(x2d, ya, yb, pg, pg, wa, wb, wo, g, wq, kx, vx, wox)


def _topk_rows(sc, k):
    n = sc.shape[0]
    io = lax.broadcasted_iota(I32, sc.shape, 0).astype(F32)
    vals, ids = [], []
    for _ in range(k):
        m = jnp.max(sc, axis=0, keepdims=True)
        ix = jnp.argmax(sc, axis=0, keepdims=True).astype(F32)
        vals.append(m)
        ids.append(ix)
        sc = jnp.where(io == ix, NEG_INF, sc)
    return jnp.concatenate(vals, axis=0), jnp.concatenate(ids, axis=0).astype(I32)


def _pack_bf16_halves(h):
    bits = lax.bitcast_convert_type(h, I32)
    r = bits + 0x7FFF + (lax.shift_right_logical(bits, 16) & 1)
    half = h.shape[1] // 2
    return lax.shift_right_logical(r[:, :half], 16) | (r[:, half:] & HI_MASK)


def _route_kernel(x_ref, g_ref, wq_ref, sk_ref, hp_ref, idx_ref, w_ref, hb_ref, it_ref, wt_ref):
    p = pl.program_id(1)

    @pl.when(p == 0)
    def _():
        h = _rms(x_ref[...], g_ref[...])
        hp_ref[...] = _pack_bf16_halves(h)
        hb_ref[...] = h.astype(BF16)

    qh = jnp.dot(hb_ref[...], wq_ref[...], preferred_element_type=F32)
    tops = []
    for c in range(2):
        seg = qh[:, c * PEER_HALF:(c + 1) * PEER_HALF]
        sc = lax.dot_general(sk_ref[c], seg, (((1,), (1,)), ((), ())),
                             precision=lax.Precision.HIGHEST, preferred_element_type=F32)
        tops.append(_topk_rows(sc, PEER_TOPK))
    (s0, i0), (s1, i1) = tops
    k = PEER_TOPK
    sub = 8
    tm = s0.shape[1]
    r8 = lax.broadcasted_iota(I32, (sub, tm), 0)
    r16 = lax.broadcasted_iota(I32, (k, tm), 0)
    cand_b = [s0[0:1] + s1, s0[1:2] + s1[:sub]]
    cidx_b = [i0[0:1] * PEER_NKEYS + i1, i0[1:2] * PEER_NKEYS + i1[:sub]]
    pos_b = [r16, k + r8]
    for a in range(2, sub):
        keep = r8 < (k // (a + 1))
        cand_b.append(jnp.where(keep, s0[a:a + 1] + s1[:sub], NEG_INF))
        cidx_b.append(i0[a:a + 1] * PEER_NKEYS + i1[:sub])
        pos_b.append(a * k + r8)
    cand_b.append(s0[sub:] + s1[0:1])
    cidx_b.append(i0[sub:] * PEER_NKEYS + i1[0:1])
    pos_b.append((sub + r8) * k)
    cand = jnp.concatenate(cand_b, axis=0)
    cidx = jnp.concatenate(cidx_b, axis=0)
    pos = jnp.concatenate(pos_b, axis=0).astype(F32)
    vals, ids = [], []
    for _ in range(k):
        m = jnp.max(cand, axis=0, keepdims=True)
        px = jnp.min(jnp.where(cand == m, pos, float(k * k)), axis=0, keepdims=True)
        hit = pos == px
        vals.append(m)
        ids.append(jnp.sum(jnp.where(hit, cidx, 0), axis=0, keepdims=True))
        cand = jnp.where(hit, NEG_INF, cand)
    sf = jnp.concatenate(vals, axis=0)
    e = jnp.exp(sf - sf[0:1])
    rows = pl.ds(pl.multiple_of(p * PEER_TOPK, PEER_TOPK), PEER_TOPK)
    wt_ref[rows, :] = e / jnp.sum(e, axis=0, keepdims=True)
    it_ref[rows, :] = jnp.concatenate(ids, axis=0)

    @pl.when(p == pl.num_programs(1) - 1)
    def _():
        idx_ref[...] = it_ref[...].T
        w_ref[...] = wt_ref[...].T


def peer_route(x2d, g, wq, sk, tok0, t, tm=1024):
    assert t % tm == 0 and tok0 % tm == 0
    d = x2d.shape[1]
    ph = sk.shape[0]
    nsel = ph * PEER_TOPK
    blk0 = tok0 // tm
    return pl.pallas_call(
        _route_kernel,
        grid=(t // tm, ph),
        in_specs=[
            pl.BlockSpec((tm, d), lambda i, p: (blk0 + i, 0)),
            pl.BlockSpec((1, d), lambda i, p: (0, 0)),
            pl.BlockSpec((d, 2 * PEER_HALF), lambda i, p: (0, p)),
            pl.BlockSpec((None, 2, PEER_NKEYS, PEER_HALF), lambda i, p: (p, 0, 0, 0)),
        ],
        out_specs=[
            pl.BlockSpec((tm, d // 2), lambda i, p: (i, 0)),
            pl.BlockSpec((tm, nsel), lambda i, p: (i, 0)),
            pl.BlockSpec((tm, nsel), lambda i, p: (i, 0)),
        ],
        out_shape=[jax.ShapeDtypeStruct((t, d // 2), I32),
                   jax.ShapeDtypeStruct((t, nsel), I32),
                   jax.ShapeDtypeStruct((t, nsel), F32)],
        scratch_shapes=[pltpu.VMEM((tm, d), BF16),
                        pltpu.VMEM((nsel, tm), I32),
                        pltpu.VMEM((nsel, tm), F32)],
        compiler_params=_cparams(("parallel", "arbitrary")),
        name="peer_route",
    )(x2d, g, wq, sk)


def _final_kernel(x_ref, y_ref, g_ref, *rest):
    o_ref = rest[-1]
    o_ref[...] = _rms(x_ref[...] + y_ref[...], g_ref[...])


def final_norm_into(out, xs, y, g, row0, total, tm=512):
    t, d = y.shape
    assert t % tm == 0 and row0 % tm == 0 and total % tm == 0
    blk0 = row0 // tm
    spec = pl.BlockSpec((tm, d), lambda i: (i, 0))
    in_specs = [spec, spec, pl.BlockSpec((1, d), lambda i: (0, 0))]
    args = [xs, y, g]
    aliases = {}
    if out is not None:
        in_specs.append(pl.BlockSpec(memory_space=pl.ANY))
        args.append(out)
        aliases = {3: 0}
    return pl.pallas_call(
        _final_kernel, grid=(t // tm,),
        in_specs=in_specs,
        out_specs=pl.BlockSpec((tm, d), lambda i: (blk0 + i, 0)),
        out_shape=jax.ShapeDtypeStruct((total, d), F32),
        input_output_aliases=aliases,
        compiler_params=_cparams(("parallel",)), name="final_norm",
    )(*args)


SC_CORES = 2
SC_SUBCORES = 16
SC_WORKERS = SC_CORES * SC_SUBCORES
SC_LANES = 16
SC_GROUP = 32


def _sc_mesh():
    return plsc.VectorSubcoreMesh(core_axis_name="c", subcore_axis_name="s")


def _sc_params():
    return pltpu.CompilerParams(needs_layout_passes=False)


def _sc_worker_id():
    return lax.axis_index("s") * SC_CORES + lax.axis_index("c")


SC_ROW_LANE = 128


def _sc_unit_off(u):
    off = u * SC_LANES
    return off if isinstance(off, int) else pl.multiple_of(off, SC_LANES)


GELU_C0 = math.sqrt(2.0 / math.pi)
GELU_C1 = 0.044715


def _gelu_tanh(x):
    z = GELU_C0 * (x + GELU_C1 * (x * x * x))
    th = 1.0 - 2.0 / (jnp.exp(2.0 * z) + 1.0)
    return 0.5 * x * (1.0 + th)


SC_PK_RING = 4
SC_PK_SUB = 4
HI_MASK = -65536


def _pack_tables_kernel(u_ref, v_ref, o_ref):
    for part, ref in enumerate((u_ref, v_ref)):
        words = _pack_bf16_halves(ref[...])
        for sub in range(SC_PK_SUB):
            o_ref[:, part * SC_PK_SUB + sub, :] = words[:, sub * SC_ROW_LANE:(sub + 1) * SC_ROW_LANE]


def pack_expert_tables(u, v, te=512):
    e, d = u.shape
    assert d == 2 * SC_PK_SUB * SC_ROW_LANE
    spec = pl.BlockSpec((te, d), lambda i: (i, 0))
    return pl.pallas_call(
        _pack_tables_kernel, grid=(e // te,), in_specs=[spec, spec],
        out_specs=pl.BlockSpec((te, 2 * SC_PK_SUB, SC_ROW_LANE), lambda i: (i, 0, 0)),
        out_shape=jax.ShapeDtypeStruct((e, 2 * SC_PK_SUB, SC_ROW_LANE), I32),
        compiler_params=_cparams(("parallel",)), name="pack_expert_tables",
    )(u, v)


def _unpack_halves(x32):
    w = plsc.bitcast(x32, I32)
    return plsc.bitcast(w << 16, F32), plsc.bitcast(w & HI_MASK, F32)


def _tree_sum(xs):
    while len(xs) > 1:
        xs = [xs[i] + xs[i + 1] for i in range(0, len(xs), 2)]
    return xs[0]


def peer_experts_pk_sc(tab_uv, idx_flat, w_flat, hp, d):
    t = hp.shape[0]
    nsel = PEER_SEL
    assert t % SC_WORKERS == 0 and d == 2 * SC_PK_SUB * SC_ROW_LANE
    tpw = t // SC_WORKERS
    g = SC_GROUP if tpw % SC_GROUP == 0 else SC_GROUP // 2
    assert tpw % g == 0
    groups = tpw // g
    heads = nsel // SC_LANES
    chunks = d // 32
    units = g * heads
    ring = SC_PK_RING
    assert units % ring == 0
    row_buf = pltpu.VMEM((SC_LANES, 2 * SC_PK_SUB, SC_ROW_LANE), I32)

    def row_words(rows, r, wc, sub0):
        per = SC_ROW_LANE // SC_LANES
        return plsc.bitcast(
            rows[r, sub0 + wc // per, pl.ds(pl.multiple_of((wc % per) * SC_LANES, SC_LANES), SC_LANES)], BF16)

    def ring_loop(n_units, start, wait, compute):
        for u in range(ring - 1):
            start(u, u)

        @pl.loop(0, n_units, step=ring)
        def _(uu):
            for b in range(ring):
                u = uu + b
                nxt = u + (ring - 1)

                @pl.when(nxt < n_units)
                def _():
                    start(nxt, (b + ring - 1) % ring)

                wait(u, b)
                compute(u, b)

    @functools.partial(
        pl.kernel, mesh=_sc_mesh(),
        out_type=jax.ShapeDtypeStruct((t, d), F32),
        scratch_types=[
            pltpu.VMEM((g * nsel,), I32),
            pltpu.VMEM((g * nsel,), F32),
            pltpu.VMEM((g, d // 2), I32),
            pltpu.VMEM((g, d), F32),
            pltpu.VMEM((SC_LANES * SC_LANES,), F32),
            [row_buf] * ring,
            [pltpu.SemaphoreType.DMA] * ring,
        ],
        compiler_params=_sc_params(),
        name="peer_experts_pk_sc",
    )
    def k(tab_hbm, idx_hbm, w_hbm, h_hbm, out_hbm, idx_v, coef_v, h_v, y_v, red_v, rows, sems):
        wid = _sc_worker_id()
        lane = lax.iota(I32, SC_LANES)

        def copy(u, slot):
            ids = idx_v.at[pl.ds(_sc_unit_off(u), SC_LANES)]
            return pltpu.make_async_copy(tab_hbm.at[ids], rows[slot], sems[slot])

        def dots(u, slot):
            tt = u // heads

            def body(cp, accs):
                out = []
                hv = [plsc.bitcast(h_v[tt, pl.ds(pl.multiple_of((2 * cp + i) * SC_LANES, SC_LANES), SC_LANES)], BF16)
                      for i in range(2)]
                for r in range(SC_LANES):
                    pr = (row_words(rows[slot], r, 2 * cp, 0) * hv[0]
                          + row_words(rows[slot], r, 2 * cp + 1, 0) * hv[1])
                    lo, hi = _unpack_halves(pr)
                    out.append(accs[r] + lo + hi)
                return tuple(out)

            accs = lax.fori_loop(0, chunks // 2, body,
                                 tuple(jnp.zeros((SC_LANES,), F32) for _ in range(SC_LANES)))
            for r in range(SC_LANES):
                red_v[pl.ds(r * SC_LANES, SC_LANES)] = accs[r]
            act = _tree_sum([plsc.load_gather(red_v, [lane * SC_LANES + j]) for j in range(SC_LANES)])
            sl = pl.ds(_sc_unit_off(u), SC_LANES)
            coef_v[sl] = coef_v[sl] * _gelu_tanh(act)

        def combine(u, slot):
            tt = u // heads
            first = (u % heads) == 0
            cb = []
            for r in range(SC_LANES):
                c = plsc.load_gather(coef_v, [jnp.full((SC_LANES,), u * SC_LANES + r, I32)])
                cb.append(plsc.pack(c, c, format=plsc.PackFormat.INTERLEAVED))

            @plsc.parallel_loop(0, chunks, unroll=2)
            def _(wc):
                lo, hi = _unpack_halves(
                    _tree_sum([cb[r] * row_words(rows[slot], r, wc, SC_PK_SUB) for r in range(SC_LANES)]))
                for half, val in ((0, lo), (1, hi)):
                    sl = pl.ds(pl.multiple_of(half * (d // 2) + wc * SC_LANES, SC_LANES), SC_LANES)
                    y_v[tt, sl] = val + jnp.where(first, 0.0, y_v[tt, sl])

        def unit(u, slot):
            dots(u, slot)
            combine(u, slot)

        @pl.loop(0, groups)
        def _(gi):
            base = wid * tpw + gi * g
            pltpu.sync_copy(idx_hbm.at[pl.ds(base * nsel, g * nsel)], idx_v)
            pltpu.sync_copy(w_hbm.at[pl.ds(base * nsel, g * nsel)], coef_v)
            pltpu.sync_copy(h_hbm.at[pl.ds(base, g)], h_v)
            ring_loop(units, lambda u, s: copy(u, s).start(), lambda u, s: copy(u, s).wait(), unit)
            pltpu.sync_copy(y_v, out_hbm.at[pl.ds(base, g)])

    return k(tab_uv, idx_flat, w_flat, hp)


def kernel(x, mem, rel_bias, ln_mix, w_in, hg_lower, hg_norm, w_up_a, w_up_b, w_out, ln_cross, ln_mem, wq_x, wk_x, wv_x, wo_x, ln_ffn, peer_query, peer_subkeys, peer_u, peer_v, ln_final):
    b, s, d = x.shape
    depth = w_in.shape[0]
    assert depth == 1, "the residual after PEER is fused into the final norm"
    assert s % MB_BLOCK == 0 and s % HG_CHUNK == 0 and s % (PEER_SLICES * SC_WORKERS * SC_GROUP) == 0
    nb = s // MB_BLOCK
    row = lambda a: a.reshape(1, -1).astype(F32)
    lb_all = jnp.cumsum(jax.nn.softmax(hg_lower.astype(F32), axis=0), axis=0)
    bias = moba_bias_tiles(rel_bias)
    n_hg = 4 * HG_WIDTH
    n_qk = 2 * MB_WIDTH
    n_mb = 3 * MB_WIDTH
    l = 0
    w = cast_bf16(w_in[l].astype(F32))
    w_vt = w[:, n_hg + n_qk:n_hg + n_mb].T
    wa, wb, wo = w_up_a[l].astype(BF16), w_up_b[l].astype(BF16), w_out[l].astype(BF16)
    wqx, wox = wq_x[l].astype(BF16), wo_x[l].astype(BF16)
    wpq, sk = peer_query[l].astype(BF16), peer_subkeys[l].astype(F32)
    tab_uv = pack_expert_tables(peer_u[l].astype(F32), peer_v[l].astype(F32))
    kx, vx = mem_kv(mem, row(ln_mem[l]), wk_x[l].astype(BF16), wv_x[l].astype(BF16))

    outs = []
    for bi in range(b):
        x2d = x[bi]
        p0, pqk, km, vt, pg = in_proj(x2d, row(ln_mix[l]), w, w_vt)
        hg_state = jnp.zeros((HG_HEADS, HG_D, HG_D), F32)
        km = km.reshape(1, nb, MB_WIDTH)
        sizes = [s // PEER_SLICES] * PEER_SLICES
        if bi == b - 1:
            first = SC_WORKERS * SC_GROUP // 2
            sizes = [first, sizes[0] - first] + sizes[1:]
        tok0 = 0
        for ts in sizes:
            ya, hg_state = hgrn2(p0, row(lb_all[l]), row(hg_norm[l]), hg_state, tok0, ts)
            yb = moba_attention(pqk, vt, km, bias, 1, s, tok0 // MB_BLOCK, ts // MB_BLOCK)
            xs = mix_cross(x2d, ya, yb, pg, wa, wb, wo, row(ln_cross[l]), wqx, kx[bi:bi + 1], vx[bi:bi + 1], wox, tok0)
            hp, eidx, wts = peer_route(xs, row(ln_ffn[l]), wpq, sk, 0, ts, math.gcd(ts, 1024))
            y = peer_experts_pk_sc(tab_uv, eidx.reshape(ts * PEER_SEL), wts.reshape(ts * PEER_SEL), hp, d)
            outs.append((xs, y, bi * s + tok0))
            tok0 += ts
    out = None
    for xs, y, row0 in sorted(outs, key=lambda e: (-(e[2] // s), e[2])):
        out = final_norm_into(out, xs, y, row(ln_final), row0, b * s)
    return out.reshape(b, s, d)
```

```python
import functools
import math

import jax
import jax.numpy as jnp
from jax import lax
from jax.experimental import pallas as pl
from jax.experimental.pallas import tpu as pltpu
from jax.experimental.pallas import tpu_sc as plsc

F32 = jnp.float32
BF16 = jnp.bfloat16
I32 = jnp.int32
EPS = 1e-6
NEG_INF = float("-inf")

HG_HEADS = 4
HG_D = 128
HG_WIDTH = HG_HEADS * HG_D
HG_CHUNK = 64
HG_SUB = 16
MB_HEADS = 8
MB_DH = 64
MB_WIDTH = MB_HEADS * MB_DH
MB_BLOCK = 256
MB_TOPK = 3
MB_BIAS_TILES = 8
REL_BUCKETS = 32
REL_MAX_DIST = 2048
X_HEADS = 4
PEER_HEADS = 8
PEER_NKEYS = 128
PEER_TOPK = 16
PEER_HALF = 128
PEER_SEL = PEER_HEADS * PEER_TOPK
PEER_SLICES = 4

VMEM_LIMIT = 56 * 1024 * 1024


def _cparams(sem):
    return pltpu.CompilerParams(dimension_semantics=sem, vmem_limit_bytes=VMEM_LIMIT)


def _rms(x, g):
    ms = jnp.mean(x * x, axis=-1, keepdims=True)
    return x * lax.rsqrt(ms + EPS) * g


def _cast_t_kernel(w_ref, o_ref):
    o_ref[...] = w_ref[...].T.astype(o_ref.dtype)


def cast_bf16_t(w, col0, n):
    k = w.shape[0]
    assert col0 % n == 0
    return pl.pallas_call(
        _cast_t_kernel, grid=(1,),
        in_specs=[pl.BlockSpec((k, n), lambda i: (0, col0 // n))],
        out_specs=pl.BlockSpec((n, k), lambda i: (0, 0)),
        out_shape=jax.ShapeDtypeStruct((n, k), BF16),
        compiler_params=_cparams(("arbitrary",)), name="cast_bf16_t",
    )(w)


def _cast_kernel(w_ref, o_ref):
    o_ref[...] = w_ref[...].astype(o_ref.dtype)


def cast_bf16(w, tn=512):
    k, n = w.shape
    assert n % tn == 0
    spec = pl.BlockSpec((k, tn), lambda j: (0, j))
    return pl.pallas_call(
        _cast_kernel, grid=(n // tn,), in_specs=[spec], out_specs=spec,
        out_shape=jax.ShapeDtypeStruct((k, n), BF16),
        compiler_params=_cparams(("parallel",)), name="cast_bf16",
    )(w)


def _in_proj_kernel(x_ref, g_ref, w0_ref, w1_ref, wvt_ref, *rest):
    w2_refs, (o0_ref, o1_ref, okm_ref, ovt_ref, o2_ref) = rest[:-5], rest[-5:]
    h = _rms(x_ref[...], g_ref[...]).astype(BF16)
    o0_ref[...] = jnp.dot(h, w0_ref[...], preferred_element_type=F32)
    qk = jnp.dot(h, w1_ref[...], preferred_element_type=F32)
    o1_ref[...] = qk.astype(BF16)
    okm_ref[0] = jnp.mean(qk[:, MB_WIDTH:], axis=0, keepdims=True)
    vt = lax.dot_general(wvt_ref[...], h, (((1,), (1,)), ((), ())), preferred_element_type=F32).astype(BF16)
    for hd in range(MB_HEADS):
        ovt_ref[0, hd * MB_VROWS:hd * MB_VROWS + MB_DH, :] = vt[hd * MB_DH:(hd + 1) * MB_DH]
        ovt_ref[0, hd * MB_VROWS + MB_DH:(hd + 1) * MB_VROWS, :] = jnp.ones((MB_ONES, vt.shape[1]), BF16)
    wg = w2_refs[0].shape[1]
    for j, w2_ref in enumerate(w2_refs):
        o2_ref[:, j * wg:(j + 1) * wg] = jnp.dot(h, w2_ref[...], preferred_element_type=F32).astype(BF16)


def in_proj(x2d, g, w, wvt):
    t, d = x2d.shape
    tm = MB_BLOCK
    n0, n1, nv, n2 = 4 * HG_WIDTH, 2 * MB_WIDTH, MB_VT_ROWS, 2 * d
    wg = MB_WIDTH
    assert wvt.shape == (MB_WIDTH, d) and w.shape == (d, n0 + n1 + MB_WIDTH + n2)
    assert n0 % n1 == 0 and (n0 + n1 + MB_WIDTH) % wg == 0 and n2 % wg == 0
    full = lambda a: pl.BlockSpec(a.shape, lambda i: (0, 0))
    g0 = (n0 + n1 + MB_WIDTH) // wg
    w_specs = ([pl.BlockSpec((d, n0), lambda i: (0, 0)), pl.BlockSpec((d, n1), lambda i: (0, n0 // n1)), full(wvt)]
               + [pl.BlockSpec((d, wg), lambda i, j=j: (0, g0 + j)) for j in range(n2 // wg)])
    return pl.pallas_call(
        _in_proj_kernel,
        grid=(t // tm,),
        in_specs=[pl.BlockSpec((tm, d), lambda i: (i, 0)), full(g)] + w_specs,
        out_specs=[pl.BlockSpec((tm, n0), lambda i: (i, 0)),
                   pl.BlockSpec((tm, n1), lambda i: (i, 0)),
                   pl.BlockSpec((1, 1, MB_WIDTH), lambda i: (i, 0, 0)),
                   pl.BlockSpec((1, nv, tm), lambda i: (i, 0, 0)),
                   pl.BlockSpec((tm, n2), lambda i: (i, 0))],
        out_shape=[jax.ShapeDtypeStruct((t, n0), F32),
                   jax.ShapeDtypeStruct((t, n1), BF16),
                   jax.ShapeDtypeStruct((t // tm, 1, MB_WIDTH), F32),
                   jax.ShapeDtypeStruct((t // tm, nv, tm), BF16),
                   jax.ShapeDtypeStruct((t, n2), BF16)],
        compiler_params=_cparams(("parallel",)),
        name="in_proj",
    )(x2d, g, w, w, wvt, *([w] * (n2 // wg)))


def _hgrn_kernel(q_ref, f_ref, i_ref, g_ref, lb_ref, gain_ref, st0_ref, o_ref, stn_ref, st_ref):
    c = pl.program_id(0)

    @pl.when(c == 0)
    def _():
        st_ref[...] = st0_ref[...]

    C, S = HG_CHUNK, HG_SUB
    row = lax.broadcasted_iota(I32, (C, C), 0)
    col = lax.broadcasted_iota(I32, (C, C), 1)
    tril = (row >= col).astype(F32)
    t_iota = lax.broadcasted_iota(I32, (S, 1), 0)

    for h in range(HG_HEADS):
        sl = slice(h * HG_D, (h + 1) * HG_D)
        q = q_ref[:, sl]
        v = i_ref[:, sl]
        lb = lb_ref[:, sl]
        f = lb + (1.0 - lb) * jax.nn.sigmoid(f_ref[:, sl])
        lf = jnp.log(f)
        k = 1.0 - f
        b = jnp.dot(tril, lf, precision=lax.Precision.HIGHEST, preferred_element_type=F32)
        st = st_ref[h]
        vb = v.astype(BF16)
        qd = (q * jnp.exp(b)).astype(BF16)
        o_inter = lax.dot_general(qd, st.astype(BF16), (((1,), (1,)), ((), ())),
                                  preferred_element_type=F32)
        outs = []
        for i in range(C // S):
            r0 = i * S
            qi = q[r0:r0 + S]
            ki = k[r0:r0 + S]
            bi = b[r0:r0 + S]
            vi = v[r0:r0 + S]
            oi = o_inter[r0:r0 + S]
            if i > 0:
                bs = b[r0 - 1:r0]
                qh = (qi * jnp.exp(bi - bs)).astype(BF16)
                kh = (k[:r0] * jnp.exp(bs - b[:r0])).astype(BF16)
                a = lax.dot_general(qh, kh, (((1,), (1,)), ((), ())), preferred_element_type=F32)
                oi = oi + jnp.dot(a.astype(BF16), vb[:r0], preferred_element_type=F32)
            half = S // 2
            o_half = [oi[:half], oi[half:]]
            for s in range(S):
                for hf in range(s // half, 2):
                    rows = slice(hf * half, (hf + 1) * half)
                    dec = jnp.exp(jnp.minimum(bi[rows] - bi[s:s + 1], 0.0))
                    a_s = jnp.sum(qi[rows] * ki[s:s + 1] * dec, axis=-1, keepdims=True)
                    a_s = jnp.where(t_iota[rows] >= s, a_s, 0.0)
                    o_half[hf] = o_half[hf] + a_s * vi[s:s + 1]
            outs.extend(o_half)
        o = jnp.concatenate(outs, axis=0)
        b_end = b[C - 1:C]
        kd = (k * jnp.exp(b_end - b)).astype(BF16)
        upd = lax.dot_general(vb, kd, (((0,), (0,)), ((), ())), preferred_element_type=F32)
        st_ref[h] = st * jnp.exp(b_end) + upd
        o = o * lax.rsqrt(jnp.mean(o * o, axis=-1, keepdims=True) + EPS)
        g = g_ref[:, sl]
        o_ref[:, sl] = (o * gain_ref[:, sl] * (g * jax.nn.sigmoid(g))).astype(o_ref.dtype)

    @pl.when(c == pl.num_programs(0) - 1)
    def _():
        stn_ref[...] = st_ref[...]


def hgrn2(p0, lb, gain, state, tok0, t):
    assert t % HG_CHUNK == 0 and tok0 % HG_CHUNK == 0
    nc = t // HG_CHUNK
    c0 = tok0 // HG_CHUNK
    w = HG_WIDTH

    def col(j):
        return pl.BlockSpec((HG_CHUNK, w), lambda c, j=j: (c0 + c, j))

    st_spec = pl.BlockSpec(state.shape, lambda c: (0, 0, 0))
    return pl.pallas_call(
        _hgrn_kernel,
        grid=(nc,),
        in_specs=[col(0), col(1), col(2), col(3),
                  pl.BlockSpec((1, w), lambda c: (0, 0)),
                  pl.BlockSpec((1, w), lambda c: (0, 0)),
                  st_spec],
        out_specs=[pl.BlockSpec((HG_CHUNK, w), lambda c: (c, 0)), st_spec],
        out_shape=[jax.ShapeDtypeStruct((t, w), BF16), jax.ShapeDtypeStruct(state.shape, F32)],
        scratch_shapes=[pltpu.VMEM((HG_HEADS, HG_D, HG_D), F32)],
        compiler_params=_cparams(("arbitrary",)),
        name="hgrn2",
    )(p0, p0, p0, p0, lb, gain, state)


MB_PAIR = 4
MB_PW = MB_PAIR * MB_DH
MB_LG = 128
MB_ONES = 16
MB_VROWS = MB_DH + MB_ONES
MB_VT_ROWS = MB_HEADS * MB_VROWS


def _moba_kernel(q_ref, k_ref, vt_ref, km_ref, bias_ref, o_ref, *scratch, qb0):
    m_ref, l_ref, al_ref, acc_ref, msk_ref, s_ref, p_ref = (
        scratch[i * MB_PAIR:(i + 1) * MB_PAIR] for i in range(7))
    qi = pl.program_id(2) + qb0
    nb = km_ref.shape[0]
    blk = MB_BLOCK
    heads = range(MB_PAIR)
    grp = lambda hh: slice((hh // 2) * MB_LG, (hh // 2 + 1) * MB_LG)
    q = q_ref[...]
    lane = lax.broadcasted_iota(I32, (blk, MB_LG), 1)
    in_head = [(lane < MB_DH) if hh % 2 == 0 else (lane >= MB_DH) for hh in heads]
    qs = q * jnp.asarray(MB_DH ** -0.5, BF16)
    nt = (((1,), (1,)), ((), ()))
    qf = q.astype(F32)
    qht = [jnp.where(in_head[hh], qs[:, grp(hh)].astype(F32), 0.0).T.astype(BF16) for hh in heads]

    n_io = lax.broadcasted_iota(I32, (nb, blk), 0)
    for hh in heads:
        gate = lax.dot_general(km_ref[:, grp(hh)], jnp.where(in_head[hh], qf[:, grp(hh)], 0.0), nt,
                               precision=lax.Precision.HIGHEST, preferred_element_type=F32)
        gate = jnp.where(n_io < qi, gate, NEG_INF)
        chosen = n_io < 0
        for _ in range(MB_TOPK):
            mx = jnp.max(gate, axis=0, keepdims=True)
            ix = jnp.min(jnp.where(gate == mx, n_io, nb), axis=0, keepdims=True)
            hit = n_io == ix
            chosen = chosen | (hit & (mx > NEG_INF))
            gate = jnp.where(hit, NEG_INF, gate)
        msk_ref[hh][...] = jnp.where(chosen, 0.0, NEG_INF)

    vrows = lambda hh: slice(hh * MB_VROWS, (hh + 1) * MB_VROWS)

    def pv_stage(blk_idx):
        vtb = vt_ref[blk_idx]
        r = [jnp.dot(vtb[vrows(hh)], p_ref[hh][...], preferred_element_type=F32) for hh in heads]
        al = [al_ref[hh][...] for hh in heads]
        a_new = [al[hh] * acc_ref[hh][...] + r[hh][:MB_DH] for hh in heads]
        l_new = [al[hh] * l_ref[hh][...] + r[hh][MB_DH:MB_DH + 1] for hh in heads]
        return a_new, l_new

    def store_pv(a_new, l_new):
        for hh in heads:
            acc_ref[hh][...] = a_new[hh]
            l_ref[hh][...] = l_new[hh]

    def softmax_stage():
        s = [s_ref[hh][...] for hh in heads]
        m_old = [m_ref[hh][...] for hh in heads]
        m_new = [jnp.maximum(m_old[hh], jnp.max(s[hh], axis=0, keepdims=True)) for hh in heads]
        alpha = [jnp.exp(m_old[hh] - m_new[hh]) for hh in heads]
        p = [jnp.exp((s[hh] - m_new[hh]).astype(BF16)) for hh in heads]
        return p, alpha, m_new

    def store_softmax(p, alpha, m_new):
        for hh in heads:
            p_ref[hh][...] = p[hh]
            al_ref[hh][...] = alpha[hh]
            m_ref[hh][...] = m_new[hh]

    k_own = k_ref[pl.ds(pl.multiple_of(qi * blk, blk), blk), :]
    key_io = lax.broadcasted_iota(I32, (blk, blk), 0)
    qry_io = lax.broadcasted_iota(I32, (blk, blk), 1)
    for hh in heads:
        s = jnp.dot(k_own[:, grp(hh)], qht[hh], preferred_element_type=F32) + bias_ref[hh, 0]
        s_ref[hh][...] = jnp.where(key_io <= qry_io, s, NEG_INF)
        m_ref[hh][...] = jnp.full((1, blk), NEG_INF, F32)
        l_ref[hh][...] = jnp.zeros((1, blk), F32)
        al_ref[hh][...] = jnp.ones((1, blk), F32)
        acc_ref[hh][...] = jnp.zeros((MB_DH, blk), F32)
        p_ref[hh][...] = jnp.zeros((blk, blk), BF16)

    def step(i, carry, far):
        pv = pv_stage(jnp.where(i <= 1, qi, i - 2))
        sm = softmax_stage()
        kn = k_ref[pl.ds(pl.multiple_of(i * blk, blk), blk), :]
        if far:
            row = [msk_ref[hh][pl.ds(i, 1), :] + bias_ref[hh, MB_BIAS_TILES - 1, 0:1, 0:1] for hh in heads]
            s_next = [jnp.dot(kn[:, grp(hh)], qht[hh], preferred_element_type=F32) + row[hh] for hh in heads]
        else:
            d = qi - i
            s_next = [jnp.dot(kn[:, grp(hh)], qht[hh], preferred_element_type=F32)
                      + bias_ref[hh, d] + msk_ref[hh][pl.ds(i, 1), :] for hh in heads]
        store_pv(*pv)
        for hh in heads:
            s_ref[hh][...] = s_next[hh]
        store_softmax(*sm)
        return carry

    n_far = jnp.maximum(qi - (MB_BIAS_TILES - 2), 0)
    lax.fori_loop(0, n_far, functools.partial(step, far=True), 0)
    lax.fori_loop(n_far, qi, functools.partial(step, far=False), 0)
    pv = pv_stage(jnp.where(qi <= 1, qi, qi - 2))
    sm = softmax_stage()
    store_pv(*pv)
    store_softmax(*sm)
    a_fin, l_fin = pv_stage(jnp.where(qi == 0, qi, qi - 1))
    out_t = jnp.concatenate([a_fin[hh] / l_fin[hh] for hh in heads], axis=0)
    o_ref[...] = out_t.T.astype(o_ref.dtype)


def moba_attention(pqk, vt, km, bias, batch, seq, qb0=0, nqb=None):
    nb = seq // MB_BLOCK
    nqb = nb if nqb is None else nqb
    t = batch * nqb * MB_BLOCK
    groups = MB_WIDTH // MB_PW
    return pl.pallas_call(
        functools.partial(_moba_kernel, qb0=qb0),
        grid=(batch, groups, nqb),
        in_specs=[
            pl.BlockSpec((MB_BLOCK, MB_PW), lambda b, j, i: (b * nb + qb0 + i, j)),
            pl.BlockSpec((seq, MB_PW), lambda b, j, i: (b, groups + j)),
            pl.BlockSpec((nb, MB_PAIR * MB_VROWS, MB_BLOCK), lambda b, j, i: (b, j, 0)),
            pl.BlockSpec((None, nb, MB_PW), lambda b, j, i: (b, 0, j)),
            pl.BlockSpec((MB_PAIR, MB_BIAS_TILES, MB_BLOCK, MB_BLOCK), lambda b, j, i: (j, 0, 0, 0)),
        ],
        out_specs=pl.BlockSpec((MB_BLOCK, MB_PW), lambda b, j, i: (b * nqb + i, j)),
        out_shape=jax.ShapeDtypeStruct((t, MB_WIDTH), BF16),
        scratch_shapes=(
            [pltpu.VMEM((1, MB_BLOCK), F32)] * (3 * MB_PAIR)
            + [pltpu.VMEM((MB_DH, MB_BLOCK), F32)] * MB_PAIR
            + [pltpu.VMEM((nb, MB_BLOCK), F32)] * MB_PAIR
            + [pltpu.VMEM((MB_BLOCK, MB_BLOCK), F32)] * MB_PAIR
            + [pltpu.VMEM((MB_BLOCK, MB_BLOCK), BF16)] * MB_PAIR
        ),
        compiler_params=_cparams(("parallel", "parallel", "arbitrary")),
        name="moba_attn",
    )(pqk, pqk, vt, km, bias)


def _t5_bucket(dist):
    max_exact = REL_BUCKETS // 2
    scaled = jnp.log(jnp.maximum(dist, 1).astype(F32) / max_exact) / math.log(REL_MAX_DIST / max_exact)
    large = jnp.minimum(max_exact + (scaled * (REL_BUCKETS - max_exact)).astype(I32), REL_BUCKETS - 1)
    return jnp.where(dist < max_exact, dist, large)


def moba_bias_tiles(rel_bias):
    blk = MB_BLOCK
    span = 2 * blk - 1
    x = jnp.arange(span) - (blk - 1)
    dist = jnp.maximum(jnp.arange(MB_BIAS_TILES)[:, None] * blk + x[None, :], 0)
    w = rel_bias.astype(F32).T[:, _t5_bucket(dist)]
    h = w.shape[0]
    wp = jnp.pad(w, ((0, 0), (0, 0), (0, 1)))[:, :, None, :]
    return pl.pallas_call(
        _toeplitz_kernel,
        grid=(h, MB_BIAS_TILES),
        in_specs=[pl.BlockSpec((None, None, 1, 2 * blk), lambda i, j: (i, j, 0, 0))],
        out_specs=pl.BlockSpec((None, None, blk, blk), lambda i, j: (i, j, 0, 0)),
        out_shape=jax.ShapeDtypeStruct((h, MB_BIAS_TILES, blk, blk), F32),
        compiler_params=_cparams(("parallel", "parallel")),
        name="moba_bias_tiles",
    )(wp)


def _toeplitz_kernel(w_ref, o_ref):
    blk = o_ref.shape[0]
    x = jnp.broadcast_to(w_ref[...], (blk, 2 * blk))
    o_ref[...] = pltpu.roll(x, 1, 1, stride=1, stride_axis=0)[:, blk:]


def _mix_kernel(x_ref, ya_ref, yb_ref, ga_ref, gb_ref, wa_ref, wb_ref, wo_ref, o_ref):
    za = jnp.dot(ya_ref[...], wa_ref[...], preferred_element_type=F32)
    zb = jnp.dot(yb_ref[...], wb_ref[...], preferred_element_type=F32)
    z = jax.nn.sigmoid(ga_ref[...].astype(F32)) * za + jax.nn.sigmoid(gb_ref[...].astype(F32)) * zb
    o_ref[...] = x_ref[...] + jnp.dot(z.astype(BF16), wo_ref[...], preferred_element_type=F32)


def _mem_kv_kernel(m_ref, g_ref, wk_ref, wv_ref, k_ref, v_ref):
    mn = _rms(m_ref[...], g_ref[...]).astype(BF16)
    k_ref[...] = jnp.dot(mn, wk_ref[...], preferred_element_type=F32).astype(BF16)
    v_ref[...] = jnp.dot(mn, wv_ref[...], preferred_element_type=F32).astype(BF16)


def mem_kv(mem, g, wk, wv):
    b, m, d = mem.shape
    spec = pl.BlockSpec((None, m, d), lambda i: (i, 0, 0))
    wspec = pl.BlockSpec((d, d), lambda i: (0, 0))
    return pl.pallas_call(
        _mem_kv_kernel,
        grid=(b,),
        in_specs=[spec, pl.BlockSpec((1, d), lambda i: (0, 0)), wspec, wspec],
        out_specs=[spec, spec],
        out_shape=[jax.ShapeDtypeStruct((b, m, d), BF16)] * 2,
        compiler_params=_cparams(("parallel",)),
        name="mem_kv",
    )(mem, g, wk, wv)


def _cross_kernel(x_ref, g_ref, wq_ref, k_ref, v_ref, wo_ref, o_ref):
    x = x_ref[...]
    d = x.shape[1]
    dh = d // X_HEADS
    h = _rms(x, g_ref[...]).astype(BF16)
    q = (jnp.dot(h, wq_ref[...], preferred_element_type=F32) * (dh ** -0.5)).astype(BF16)
    outs = []
    for hh in range(X_HEADS):
        sl = slice(hh * dh, (hh + 1) * dh)
        s = lax.dot_general(q[:, sl], k_ref[:, sl], (((1,), (1,)), ((), ())),
                            preferred_element_type=F32)
        p = jnp.exp(s - jnp.max(s, axis=1, keepdims=True))
        l = jnp.sum(p, axis=1, keepdims=True)
        o = jnp.dot(p.astype(BF16), v_ref[:, sl], preferred_element_type=F32) / l
        outs.append(o.astype(BF16))
    o = jnp.concatenate(outs, axis=1)
    o_ref[...] = x + jnp.dot(o, wo_ref[...], preferred_element_type=F32)


def _mix_cross_kernel(x_ref, ya_ref, yb_ref, ga_ref, gb_ref, wa_ref, wb_ref, wo_ref,
                      g_ref, wq_ref, k_ref, v_ref, wox_ref, o_ref, x1_ref):
    _mix_kernel(x_ref, ya_ref, yb_ref, ga_ref, gb_ref, wa_ref, wb_ref, wo_ref, x1_ref)
    _cross_kernel(x1_ref, g_ref, wq_ref, k_ref, v_ref, wox_ref, o_ref)


def mix_cross(x2d, ya, yb, pg, wa, wb, wo, g, wq, kx, vx, wox, tok0, tm=512):
    t = yb.shape[0]
    assert t % tm == 0 and tok0 % tm == 0
    d = x2d.shape[1]
    w = ya.shape[1]
    m = kx.shape[1]
    b0 = tok0 // tm
    const = lambda a: pl.BlockSpec(a.shape, lambda i: (0,) * a.ndim)
    kv = pl.BlockSpec((None, m, d), lambda i: (0, 0, 0))
    return pl.pallas_call(
        _mix_cross_kernel,
        grid=(t // tm,),
        in_specs=[
            pl.BlockSpec((tm, d), lambda i: (b0 + i, 0)),
            pl.BlockSpec((tm, w), lambda i: (i, 0)),
            pl.BlockSpec((tm, w), lambda i: (i, 0)),
            pl.BlockSpec((tm, d), lambda i: (b0 + i, 0)),
            pl.BlockSpec((tm, d), lambda i: (b0 + i, 1)),
            const(wa), const(wb), const(wo), const(g), const(wq), kv, kv, const(wox),
        ],
        out_specs=pl.BlockSpec((tm, d), lambda i: (i, 0)),
        out_shape=jax.ShapeDtypeStruct((t, d), F32),
        scratch_shapes=[pltpu.VMEM((tm, d), F32)],
        compiler_params=_cparams(("parallel",)),
        name="mix_cross",
    )(x2d, ya, yb, pg, pg, wa, wb, wo, g, wq, kx, vx, wox)


def _topk_rows(sc, k):
    n = sc.shape[0]
    io = lax.broadcasted_iota(I32, sc.shape, 0).astype(F32)
    vals, ids = [], []
    for _ in range(k):
        m = jnp.max(sc, axis=0, keepdims=True)
        ix = jnp.argmax(sc, axis=0, keepdims=True).astype(F32)
        vals.append(m)
        ids.append(ix)
        sc = jnp.where(io == ix, NEG_INF, sc)
    return jnp.concatenate(vals, axis=0), jnp.concatenate(ids, axis=0).astype(I32)


def _pack_bf16_halves(h):
    bits = lax.bitcast_convert_type(h, I32)
    r = bits + 0x7FFF + (lax.shift_right_logical(bits, 16) & 1)
    half = h.shape[1] // 2
    return lax.shift_right_logical(r[:, :half], 16) | (r[:, half:] & HI_MASK)


def _route_kernel(x_ref, g_ref, wq_ref, sk_ref, hp_ref, idx_ref, w_ref, hb_ref, it_ref, wt_ref):
    p = pl.program_id(1)

    @pl.when(p == 0)
    def _():
        h = _rms(x_ref[...], g_ref[...])
        hp_ref[...] = _pack_bf16_halves(h)
        hb_ref[...] = h.astype(BF16)

    qh = jnp.dot(hb_ref[...], wq_ref[...], preferred_element_type=F32)
    tops = []
    for c in range(2):
        seg = qh[:, c * PEER_HALF:(c + 1) * PEER_HALF]
        sc = lax.dot_general(sk_ref[c], seg, (((1,), (1,)), ((), ())),
                             precision=lax.Precision.HIGHEST, preferred_element_type=F32)
        tops.append(_topk_rows(sc, PEER_TOPK))
    (s0, i0), (s1, i1) = tops
    k = PEER_TOPK
    sub = 8
    tm = s0.shape[1]
    r8 = lax.broadcasted_iota(I32, (sub, tm), 0)
    r16 = lax.broadcasted_iota(I32, (k, tm), 0)
    cand_b = [s0[0:1] + s1, s0[1:2] + s1[:sub]]
    cidx_b = [i0[0:1] * PEER_NKEYS + i1, i0[1:2] * PEER_NKEYS + i1[:sub]]
    pos_b = [r16, k + r8]
    for a in range(2, sub):
        keep = r8 < (k // (a + 1))
        cand_b.append(jnp.where(keep, s0[a:a + 1] + s1[:sub], NEG_INF))
        cidx_b.append(i0[a:a + 1] * PEER_NKEYS + i1[:sub])
        pos_b.append(a * k + r8)
    cand_b.append(s0[sub:] + s1[0:1])
    cidx_b.append(i0[sub:] * PEER_NKEYS + i1[0:1])
    pos_b.append((sub + r8) * k)
    cand = jnp.concatenate(cand_b, axis=0)
    cidx = jnp.concatenate(cidx_b, axis=0)
    pos = jnp.concatenate(pos_b, axis=0).astype(F32)
    vals, ids = [], []
    for _ in range(k):
        m = jnp.max(cand, axis=0, keepdims=True)
        px = jnp.min(jnp.where(cand == m, pos, float(k * k)), axis=0, keepdims=True)
        hit = pos == px
        vals.append(m)
        ids.append(jnp.sum(jnp.where(hit, cidx, 0), axis=0, keepdims=True))
        cand = jnp.where(hit, NEG_INF, cand)
    sf = jnp.concatenate(vals, axis=0)
    e = jnp.exp(sf - sf[0:1])
    rows = pl.ds(pl.multiple_of(p * PEER_TOPK, PEER_TOPK), PEER_TOPK)
    wt_ref[rows, :] = e / jnp.sum(e, axis=0, keepdims=True)
    it_ref[rows, :] = jnp.concatenate(ids, axis=0)

    @pl.when(p == pl.num_programs(1) - 1)
    def _():
        idx_ref[...] = it_ref[...].T
        w_ref[...] = wt_ref[...].T


def peer_route(x2d, g, wq, sk, tok0, t, tm=1024):
    assert t % tm == 0 and tok0 % tm == 0
    d = x2d.shape[1]
    ph = sk.shape[0]
    nsel = ph * PEER_TOPK
    blk0 = tok0 // tm
    return pl.pallas_call(
        _route_kernel,
        grid=(t // tm, ph),
        in_specs=[
            pl.BlockSpec((tm, d), lambda i, p: (blk0 + i, 0)),
            pl.BlockSpec((1, d), lambda i, p: (0, 0)),
            pl.BlockSpec((d, 2 * PEER_HALF), lambda i, p: (0, p)),
            pl.BlockSpec((None, 2, PEER_NKEYS, PEER_HALF), lambda i, p: (p, 0, 0, 0)),
        ],
        out_specs=[
            pl.BlockSpec((tm, d // 2), lambda i, p: (i, 0)),
            pl.BlockSpec((tm, nsel), lambda i, p: (i, 0)),
            pl.BlockSpec((tm, nsel), lambda i, p: (i, 0)),
        ],
        out_shape=[jax.ShapeDtypeStruct((t, d // 2), I32),
                   jax.ShapeDtypeStruct((t, nsel), I32),
                   jax.ShapeDtypeStruct((t, nsel), F32)],
        scratch_shapes=[pltpu.VMEM((tm, d), BF16),
                        pltpu.VMEM((nsel, tm), I32),
                        pltpu.VMEM((nsel, tm), F32)],
        compiler_params=_cparams(("parallel", "arbitrary")),
        name="peer_route",
    )(x2d, g, wq, sk)


def _final_kernel(x_ref, y_ref, g_ref, *rest):
    o_ref = rest[-1]
    o_ref[...] = _rms(x_ref[...] + y_ref[...], g_ref[...])


def final_norm_into(out, xs, y, g, row0, total, tm=512):
    t, d = y.shape
    assert t % tm == 0 and row0 % tm == 0 and total % tm == 0
    blk0 = row0 // tm
    spec = pl.BlockSpec((tm, d), lambda i: (i, 0))
    in_specs = [spec, spec, pl.BlockSpec((1, d), lambda i: (0, 0))]
    args = [xs, y, g]
    aliases = {}
    if out is not None:
        in_specs.append(pl.BlockSpec(memory_space=pl.ANY))
        args.append(out)
        aliases = {3: 0}
    return pl.pallas_call(
        _final_kernel, grid=(t // tm,),
        in_specs=in_specs,
        out_specs=pl.BlockSpec((tm, d), lambda i: (blk0 + i, 0)),
        out_shape=jax.ShapeDtypeStruct((total, d), F32),
        input_output_aliases=aliases,
        compiler_params=_cparams(("parallel",)), name="final_norm",
    )(*args)


SC_CORES = 2
SC_SUBCORES = 16
SC_WORKERS = SC_CORES * SC_SUBCORES
SC_LANES = 16
SC_GROUP = 32


def _sc_mesh():
    return plsc.VectorSubcoreMesh(core_axis_name="c", subcore_axis_name="s")


def _sc_params():
    return pltpu.CompilerParams(needs_layout_passes=False)


def _sc_worker_id():
    return lax.axis_index("s") * SC_CORES + lax.axis_index("c")


SC_ROW_LANE = 128


def _sc_unit_off(u):
    off = u * SC_LANES
    return off if isinstance(off, int) else pl.multiple_of(off, SC_LANES)


GELU_C0 = math.sqrt(2.0 / math.pi)
GELU_C1 = 0.044715


def _gelu_tanh(x):
    z = GELU_C0 * (x + GELU_C1 * (x * x * x))
    th = 1.0 - 2.0 / (jnp.exp(2.0 * z) + 1.0)
    return 0.5 * x * (1.0 + th)


SC_PK_RING = 4
SC_PK_SUB = 4
HI_MASK = -65536


def _pack_tables_kernel(u_ref, v_ref, o_ref):
    for part, ref in enumerate((u_ref, v_ref)):
        words = _pack_bf16_halves(ref[...])
        for sub in range(SC_PK_SUB):
            o_ref[:, part * SC_PK_SUB + sub, :] = words[:, sub * SC_ROW_LANE:(sub + 1) * SC_ROW_LANE]


def pack_expert_tables(u, v, te=512):
    e, d = u.shape
    assert d == 2 * SC_PK_SUB * SC_ROW_LANE
    spec = pl.BlockSpec((te, d), lambda i: (i, 0))
    return pl.pallas_call(
        _pack_tables_kernel, grid=(e // te,), in_specs=[spec, spec],
        out_specs=pl.BlockSpec((te, 2 * SC_PK_SUB, SC_ROW_LANE), lambda i: (i, 0, 0)),
        out_shape=jax.ShapeDtypeStruct((e, 2 * SC_PK_SUB, SC_ROW_LANE), I32),
        compiler_params=_cparams(("parallel",)), name="pack_expert_tables",
    )(u, v)


def _unpack_halves(x32):
    w = plsc.bitcast(x32, I32)
    return plsc.bitcast(w << 16, F32), plsc.bitcast(w & HI_MASK, F32)


def _tree_sum(xs):
    while len(xs) > 1:
        xs = [xs[i] + xs[i + 1] for i in range(0, len(xs), 2)]
    return xs[0]


def peer_experts_pk_sc(tab_uv, idx_flat, w_flat, hp, d):
    t = hp.shape[0]
    nsel = PEER_SEL
    assert t % SC_WORKERS == 0 and d == 2 * SC_PK_SUB * SC_ROW_LANE
    tpw = t // SC_WORKERS
    g = SC_GROUP if tpw % SC_GROUP == 0 else SC_GROUP // 2
    assert tpw % g == 0
    groups = tpw // g
    heads = nsel // SC_LANES
    chunks = d // 32
    units = g * heads
    ring = SC_PK_RING
    assert units % ring == 0
    row_buf = pltpu.VMEM((SC_LANES, 2 * SC_PK_SUB, SC_ROW_LANE), I32)

    def row_words(rows, r, wc, sub0):
        per = SC_ROW_LANE // SC_LANES
        return plsc.bitcast(
            rows[r, sub0 + wc // per, pl.ds(pl.multiple_of((wc % per) * SC_LANES, SC_LANES), SC_LANES)], BF16)

    def ring_loop(n_units, start, wait, compute):
        for u in range(ring - 1):
            start(u, u)

        @pl.loop(0, n_units, step=ring)
        def _(uu):
            for b in range(ring):
                u = uu + b
                nxt = u + (ring - 1)

                @pl.when(nxt < n_units)
                def _():
                    start(nxt, (b + ring - 1) % ring)

                wait(u, b)
                compute(u, b)

    @functools.partial(
        pl.kernel, mesh=_sc_mesh(),
        out_type=jax.ShapeDtypeStruct((t, d), F32),
        scratch_types=[
            pltpu.VMEM((g * nsel,), I32),
            pltpu.VMEM((g * nsel,), F32),
            pltpu.VMEM((g, d // 2), I32),
            pltpu.VMEM((g, d), F32),
            pltpu.VMEM((SC_LANES * SC_LANES,), F32),
            [row_buf] * ring,
            [pltpu.SemaphoreType.DMA] * ring,
        ],
        compiler_params=_sc_params(),
        name="peer_experts_pk_sc",
    )
    def k(tab_hbm, idx_hbm, w_hbm, h_hbm, out_hbm, idx_v, coef_v, h_v, y_v, red_v, rows, sems):
        wid = _sc_worker_id()
        lane = lax.iota(I32, SC_LANES)

        def copy(u, slot):
            ids = idx_v.at[pl.ds(_sc_unit_off(u), SC_LANES)]
            return pltpu.make_async_copy(tab_hbm.at[ids], rows[slot], sems[slot])

        def dots(u, slot):
            tt = u // heads

            def body(cp, accs):
                out = []
                hv = [plsc.bitcast(h_v[tt, pl.ds(pl.multiple_of((2 * cp + i) * SC_LANES, SC_LANES), SC_LANES)], BF16)
                      for i in range(2)]
                for r in range(SC_LANES):
                    pr = (row_words(rows[slot], r, 2 * cp, 0) * hv[0]
                          + row_words(rows[slot], r, 2 * cp + 1, 0) * hv[1])
                    lo, hi = _unpack_halves(pr)
                    out.append(accs[r] + lo + hi)
                return tuple(out)

            accs = lax.fori_loop(0, chunks // 2, body,
                                 tuple(jnp.zeros((SC_LANES,), F32) for _ in range(SC_LANES)))
            for r in range(SC_LANES):
                red_v[pl.ds(r * SC_LANES, SC_LANES)] = accs[r]
            act = _tree_sum([plsc.load_gather(red_v, [lane * SC_LANES + j]) for j in range(SC_LANES)])
            sl = pl.ds(_sc_unit_off(u), SC_LANES)
            coef_v[sl] = coef_v[sl] * _gelu_tanh(act)

        def combine(u, slot):
            tt = u // heads
            first = (u % heads) == 0
            cb = []
            for r in range(SC_LANES):
                c = plsc.load_gather(coef_v, [jnp.full((SC_LANES,), u * SC_LANES + r, I32)])
                cb.append(plsc.pack(c, c, format=plsc.PackFormat.INTERLEAVED))

            @plsc.parallel_loop(0, chunks, unroll=2)
            def _(wc):
                lo, hi = _unpack_halves(
                    _tree_sum([cb[r] * row_words(rows[slot], r, wc, SC_PK_SUB) for r in range(SC_LANES)]))
                for half, val in ((0, lo), (1, hi)):
                    sl = pl.ds(pl.multiple_of(half * (d // 2) + wc * SC_LANES, SC_LANES), SC_LANES)
                    y_v[tt, sl] = val + jnp.where(first, 0.0, y_v[tt, sl])

        def unit(u, slot):
            dots(u, slot)
            combine(u, slot)

        @pl.loop(0, groups)
        def _(gi):
            base = wid * tpw + gi * g
            pltpu.sync_copy(idx_hbm.at[pl.ds(base * nsel, g * nsel)], idx_v)
            pltpu.sync_copy(w_hbm.at[pl.ds(base * nsel, g * nsel)], coef_v)
            pltpu.sync_copy(h_hbm.at[pl.ds(base, g)], h_v)
            ring_loop(units, lambda u, s: copy(u, s).start(), lambda u, s: copy(u, s).wait(), unit)
            pltpu.sync_copy(y_v, out_hbm.at[pl.ds(base, g)])

    return k(tab_uv, idx_flat, w_flat, hp)


def kernel(x, mem, rel_bias, ln_mix, w_in, hg_lower, hg_norm, w_up_a, w_up_b, w_out, ln_cross, ln_mem, wq_x, wk_x, wv_x, wo_x, ln_ffn, peer_query, peer_subkeys, peer_u, peer_v, ln_final):
    b, s, d = x.shape
    depth = w_in.shape[0]
    assert depth == 1, "the residual after PEER is fused into the final norm"
    assert s % MB_BLOCK == 0 and s % HG_CHUNK == 0 and s % (PEER_SLICES * SC_WORKERS * SC_GROUP) == 0
    nb = s // MB_BLOCK
    row = lambda a: a.reshape(1, -1).astype(F32)
    lb_all = jnp.cumsum(jax.nn.softmax(hg_lower.astype(F32), axis=0), axis=0)
    bias = moba_bias_tiles(rel_bias)
    n_hg = 4 * HG_WIDTH
    n_qk = 2 * MB_WIDTH
    n_mb = 3 * MB_WIDTH
    l = 0
    w = cast_bf16(w_in[l].astype(F32))
    w_vt = cast_bf16_t(w_in[l].astype(F32), n_hg + n_qk, n_mb - n_qk)
    wa, wb, wo = w_up_a[l].astype(BF16), w_up_b[l].astype(BF16), w_out[l].astype(BF16)
    wqx, wox = wq_x[l].astype(BF16), wo_x[l].astype(BF16)
    wpq, sk = peer_query[l].astype(BF16), peer_subkeys[l].astype(F32)
    tab_uv = pack_expert_tables(peer_u[l].astype(F32), peer_v[l].astype(F32))
    kx, vx = mem_kv(mem, row(ln_mem[l]), wk_x[l].astype(BF16), wv_x[l].astype(BF16))

    outs = []
    for bi in range(b):
        x2d = x[bi]
        p0, pqk, km, vt, pg = in_proj(x2d, row(ln_mix[l]), w, w_vt)
        hg_state = jnp.zeros((HG_HEADS, HG_D, HG_D), F32)
        km = km.reshape(1, nb, MB_WIDTH)
        sizes = [s // PEER_SLICES] * PEER_SLICES
        if bi == b - 1:
            first = SC_WORKERS * SC_GROUP // 2
            sizes = [first, sizes[0] - first] + sizes[1:]
        tok0 = 0
        for ts in sizes:
            ya, hg_state = hgrn2(p0, row(lb_all[l]), row(hg_norm[l]), hg_state, tok0, ts)
            yb = moba_attention(pqk, vt, km, bias, 1, s, tok0 // MB_BLOCK, ts // MB_BLOCK)
            xs = mix_cross(x2d, ya, yb, pg, wa, wb, wo, row(ln_cross[l]), wqx, kx[bi:bi + 1], vx[bi:bi + 1], wox, tok0)
            hp, eidx, wts = peer_route(xs, row(ln_ffn[l]), wpq, sk, 0, ts, math.gcd(ts, 1024))
            y = peer_experts_pk_sc(tab_uv, eidx.reshape(ts * PEER_SEL), wts.reshape(ts * PEER_SEL), hp, d)
            outs.append((xs, y, bi * s + tok0))
            tok0 += ts
    out = None
    for xs, y, row0 in sorted(outs, key=lambda e: (-(e[2] // s), e[2])):
        out = final_norm_into(out, xs, y, row(ln_final), row0, b * s)
    return out.reshape(b, s, d)
```

```python
import functools
import math

import jax
import jax.numpy as jnp
from jax import lax
from jax.experimental import pallas as pl
from jax.experimental.pallas import tpu as pltpu
from jax.experimental.pallas import tpu_sc as plsc

F32 = jnp.float32
BF16 = jnp.bfloat16
I32 = jnp.int32
EPS = 1e-6
NEG_INF = float("-inf")

HG_HEADS = 4
HG_D = 128
HG_WIDTH = HG_HEADS * HG_D
HG_CHUNK = 64
HG_SUB = 16
MB_HEADS = 8
MB_DH = 64
MB_WIDTH = MB_HEADS * MB_DH
MB_BLOCK = 256
MB_TOPK = 3
MB_BIAS_TILES = 8
REL_BUCKETS = 32
REL_MAX_DIST = 2048
X_HEADS = 4
PEER_HEADS = 8
PEER_NKEYS = 128
PEER_TOPK = 16
PEER_HALF = 128
PEER_SEL = PEER_HEADS * PEER_TOPK
PEER_SLICES = 4

VMEM_LIMIT = 56 * 1024 * 1024


def _cparams(sem):
    return pltpu.CompilerParams(dimension_semantics=sem, vmem_limit_bytes=VMEM_LIMIT)


def _rms(x, g):
    ms = jnp.mean(x * x, axis=-1, keepdims=True)
    return x * lax.rsqrt(ms + EPS) * g


def _cast_t_kernel(w_ref, o_ref):
    o_ref[...] = w_ref[...].T.astype(o_ref.dtype)


def cast_bf16_t(w, col0, n):
    k = w.shape[0]
    assert col0 % n == 0
    return pl.pallas_call(
        _cast_t_kernel, grid=(1,),
        in_specs=[pl.BlockSpec((k, n), lambda i: (0, col0 // n))],
        out_specs=pl.BlockSpec((n, k), lambda i: (0, 0)),
        out_shape=jax.ShapeDtypeStruct((n, k), BF16),
        compiler_params=_cparams(("arbitrary",)), name="cast_bf16_t",
    )(w)


def _cast_kernel(w_ref, o_ref):
    o_ref[...] = w_ref[...].astype(o_ref.dtype)


def cast_bf16(w, tn=512):
    k, n = w.shape
    assert n % tn == 0
    spec = pl.BlockSpec((k, tn), lambda j: (0, j))
    return pl.pallas_call(
        _cast_kernel, grid=(n // tn,), in_specs=[spec], out_specs=spec,
        out_shape=jax.ShapeDtypeStruct((k, n), BF16),
        compiler_params=_cparams(("parallel",)), name="cast_bf16",
    )(w)


def _in_proj_kernel(x_ref, g_ref, w0_ref, w1_ref, wvt_ref, *rest):
    w2_refs, (o0_ref, o1_ref, okm_ref, ovt_ref, o2_ref) = rest[:-5], rest[-5:]
    h = _rms(x_ref[...], g_ref[...]).astype(BF16)
    o0_ref[...] = jnp.dot(h, w0_ref[...], preferred_element_type=F32)
    qk = jnp.dot(h, w1_ref[...], preferred_element_type=F32)
    o1_ref[...] = qk.astype(BF16)
    okm_ref[0] = jnp.mean(qk[:, MB_WIDTH:], axis=0, keepdims=True)
    vt = lax.dot_general(wvt_ref[...], h, (((1,), (1,)), ((), ())), preferred_element_type=F32).astype(BF16)
    for hd in range(MB_HEADS):
        ovt_ref[0, hd * MB_VROWS:hd * MB_VROWS + MB_DH, :] = vt[hd * MB_DH:(hd + 1) * MB_DH]
        ovt_ref[0, hd * MB_VROWS + MB_DH:(hd + 1) * MB_VROWS, :] = jnp.ones((MB_ONES, vt.shape[1]), BF16)
    wg = w2_refs[0].shape[1]
    for j, w2_ref in enumerate(w2_refs):
        o2_ref[:, j * wg:(j + 1) * wg] = jnp.dot(h, w2_ref[...], preferred_element_type=F32).astype(BF16)


def in_proj(x_all, g, w, wvt, row0, t):
    d = x_all.shape[1]
    tm = MB_BLOCK
    assert t % tm == 0 and row0 % tm == 0
    blk0 = row0 // tm
    n0, n1, nv, n2 = 4 * HG_WIDTH, 2 * MB_WIDTH, MB_VT_ROWS, 2 * d
    wg = MB_WIDTH
    assert wvt.shape == (MB_WIDTH, d) and w.shape == (d, n0 + n1 + MB_WIDTH + n2)
    assert n0 % n1 == 0 and (n0 + n1 + MB_WIDTH) % wg == 0 and n2 % wg == 0
    full = lambda a: pl.BlockSpec(a.shape, lambda i: (0, 0))
    g0 = (n0 + n1 + MB_WIDTH) // wg
    w_specs = ([pl.BlockSpec((d, n0), lambda i: (0, 0)), pl.BlockSpec((d, n1), lambda i: (0, n0 // n1)), full(wvt)]
               + [pl.BlockSpec((d, wg), lambda i, j=j: (0, g0 + j)) for j in range(n2 // wg)])
    return pl.pallas_call(
        _in_proj_kernel,
        grid=(t // tm,),
        in_specs=[pl.BlockSpec((tm, d), lambda i: (blk0 + i, 0)), full(g)] + w_specs,
        out_specs=[pl.BlockSpec((tm, n0), lambda i: (i, 0)),
                   pl.BlockSpec((tm, n1), lambda i: (i, 0)),
                   pl.BlockSpec((1, 1, MB_WIDTH), lambda i: (i, 0, 0)),
                   pl.BlockSpec((1, nv, tm), lambda i: (i, 0, 0)),
                   pl.BlockSpec((tm, n2), lambda i: (i, 0))],
        out_shape=[jax.ShapeDtypeStruct((t, n0), F32),
                   jax.ShapeDtypeStruct((t, n1), BF16),
                   jax.ShapeDtypeStruct((t // tm, 1, MB_WIDTH), F32),
                   jax.ShapeDtypeStruct((t // tm, nv, tm), BF16),
                   jax.ShapeDtypeStruct((t, n2), BF16)],
        compiler_params=_cparams(("parallel",)),
        name="in_proj",
    )(x_all, g, w, w, wvt, *([w] * (n2 // wg)))


def _hgrn_kernel(q_ref, f_ref, i_ref, g_ref, lb_ref, gain_ref, st0_ref, o_ref, stn_ref, st_ref):
    c = pl.program_id(0)

    @pl.when(c == 0)
    def _():
        st_ref[...] = st0_ref[...]

    C, S = HG_CHUNK, HG_SUB
    row = lax.broadcasted_iota(I32, (C, C), 0)
    col = lax.broadcasted_iota(I32, (C, C), 1)
    tril = (row >= col).astype(F32)
    t_iota = lax.broadcasted_iota(I32, (S, 1), 0)

    for h in range(HG_HEADS):
        sl = slice(h * HG_D, (h + 1) * HG_D)
        q = q_ref[:, sl]
        v = i_ref[:, sl]
        lb = lb_ref[:, sl]
        f = lb + (1.0 - lb) * jax.nn.sigmoid(f_ref[:, sl])
        lf = jnp.log(f)
        k = 1.0 - f
        b = jnp.dot(tril, lf, precision=lax.Precision.HIGHEST, preferred_element_type=F32)
        st = st_ref[h]
        vb = v.astype(BF16)
        qd = (q * jnp.exp(b)).astype(BF16)
        o_inter = lax.dot_general(qd, st.astype(BF16), (((1,), (1,)), ((), ())),
                                  preferred_element_type=F32)
        outs = []
        for i in range(C // S):
            r0 = i * S
            qi = q[r0:r0 + S]
            ki = k[r0:r0 + S]
            bi = b[r0:r0 + S]
            vi = v[r0:r0 + S]
            oi = o_inter[r0:r0 + S]
            if i > 0:
                bs = b[r0 - 1:r0]
                qh = (qi * jnp.exp(bi - bs)).astype(BF16)
                kh = (k[:r0] * jnp.exp(bs - b[:r0])).astype(BF16)
                a = lax.dot_general(qh, kh, (((1,), (1,)), ((), ())), preferred_element_type=F32)
                oi = oi + jnp.dot(a.astype(BF16), vb[:r0], preferred_element_type=F32)
            half = S // 2
            o_half = [oi[:half], oi[half:]]
            for s in range(S):
                for hf in range(s // half, 2):
                    rows = slice(hf * half, (hf + 1) * half)
                    dec = jnp.exp(jnp.minimum(bi[rows] - bi[s:s + 1], 0.0))
                    a_s = jnp.sum(qi[rows] * ki[s:s + 1] * dec, axis=-1, keepdims=True)
                    a_s = jnp.where(t_iota[rows] >= s, a_s, 0.0)
                    o_half[hf] = o_half[hf] + a_s * vi[s:s + 1]
            outs.extend(o_half)
        o = jnp.concatenate(outs, axis=0)
        b_end = b[C - 1:C]
        kd = (k * jnp.exp(b_end - b)).astype(BF16)
        upd = lax.dot_general(vb, kd, (((0,), (0,)), ((), ())), preferred_element_type=F32)
        st_ref[h] = st * jnp.exp(b_end) + upd
        o = o * lax.rsqrt(jnp.mean(o * o, axis=-1, keepdims=True) + EPS)
        g = g_ref[:, sl]
        o_ref[:, sl] = (o * gain_ref[:, sl] * (g * jax.nn.sigmoid(g))).astype(o_ref.dtype)

    @pl.when(c == pl.num_programs(0) - 1)
    def _():
        stn_ref[...] = st_ref[...]


def hgrn2(p0, lb, gain, state, tok0, t):
    assert t % HG_CHUNK == 0 and tok0 % HG_CHUNK == 0
    nc = t // HG_CHUNK
    c0 = tok0 // HG_CHUNK
    w = HG_WIDTH

    def col(j):
        return pl.BlockSpec((HG_CHUNK, w), lambda c, j=j: (c0 + c, j))

    st_spec = pl.BlockSpec(state.shape, lambda c: (0, 0, 0))
    return pl.pallas_call(
        _hgrn_kernel,
        grid=(nc,),
        in_specs=[col(0), col(1), col(2), col(3),
                  pl.BlockSpec((1, w), lambda c: (0, 0)),
                  pl.BlockSpec((1, w), lambda c: (0, 0)),
                  st_spec],
        out_specs=[pl.BlockSpec((HG_CHUNK, w), lambda c: (c, 0)), st_spec],
        out_shape=[jax.ShapeDtypeStruct((t, w), BF16), jax.ShapeDtypeStruct(state.shape, F32)],
        scratch_shapes=[pltpu.VMEM((HG_HEADS, HG_D, HG_D), F32)],
        compiler_params=_cparams(("arbitrary",)),
        name="hgrn2",
    )(p0, p0, p0, p0, lb, gain, state)


MB_PAIR = 4
MB_PW = MB_PAIR * MB_DH
MB_LG = 128
MB_ONES = 16
MB_VROWS = MB_DH + MB_ONES
MB_VT_ROWS = MB_HEADS * MB_VROWS


def _moba_kernel(q_ref, k_ref, vt_ref, km_ref, bias_ref, o_ref, *scratch, qb0):
    m_ref, l_ref, al_ref, acc_ref, msk_ref, s_ref, p_ref = (
        scratch[i * MB_PAIR:(i + 1) * MB_PAIR] for i in range(7))
    qi = pl.program_id(2) + qb0
    nb = km_ref.shape[0]
    blk = MB_BLOCK
    heads = range(MB_PAIR)
    grp = lambda hh: slice((hh // 2) * MB_LG, (hh // 2 + 1) * MB_LG)
    q = q_ref[...]
    lane = lax.broadcasted_iota(I32, (blk, MB_LG), 1)
    in_head = [(lane < MB_DH) if hh % 2 == 0 else (lane >= MB_DH) for hh in heads]
    qs = q * jnp.asarray(MB_DH ** -0.5, BF16)
    nt = (((1,), (1,)), ((), ()))
    qf = q.astype(F32)
    qht = [jnp.where(in_head[hh], qs[:, grp(hh)].astype(F32), 0.0).T.astype(BF16) for hh in heads]

    n_io = lax.broadcasted_iota(I32, (nb, blk), 0)
    for hh in heads:
        gate = lax.dot_general(km_ref[:, grp(hh)], jnp.where(in_head[hh], qf[:, grp(hh)], 0.0), nt,
                               precision=lax.Precision.HIGHEST, preferred_element_type=F32)
        gate = jnp.where(n_io < qi, gate, NEG_INF)
        chosen = n_io < 0
        for _ in range(MB_TOPK):
            mx = jnp.max(gate, axis=0, keepdims=True)
            ix = jnp.min(jnp.where(gate == mx, n_io, nb), axis=0, keepdims=True)
            hit = n_io == ix
            chosen = chosen | (hit & (mx > NEG_INF))
            gate = jnp.where(hit, NEG_INF, gate)
        msk_ref[hh][...] = jnp.where(chosen, 0.0, NEG_INF)

    vrows = lambda hh: slice(hh * MB_VROWS, (hh + 1) * MB_VROWS)

    def pv_stage(blk_idx):
        vtb = vt_ref[blk_idx]
        r = [jnp.dot(vtb[vrows(hh)], p_ref[hh][...], preferred_element_type=F32) for hh in heads]
        al = [al_ref[hh][...] for hh in heads]
        a_new = [al[hh] * acc_ref[hh][...] + r[hh][:MB_DH] for hh in heads]
        l_new = [al[hh] * l_ref[hh][...] + r[hh][MB_DH:MB_DH + 1] for hh in heads]
        return a_new, l_new

    def store_pv(a_new, l_new):
        for hh in heads:
            acc_ref[hh][...] = a_new[hh]
            l_ref[hh][...] = l_new[hh]

    def softmax_stage():
        s = [s_ref[hh][...] for hh in heads]
        m_old = [m_ref[hh][...] for hh in heads]
        m_new = [jnp.maximum(m_old[hh], jnp.max(s[hh], axis=0, keepdims=True)) for hh in heads]
        alpha = [jnp.exp(m_old[hh] - m_new[hh]) for hh in heads]
        p = [jnp.exp((s[hh] - m_new[hh]).astype(BF16)) for hh in heads]
        return p, alpha, m_new

    def store_softmax(p, alpha, m_new):
        for hh in heads:
            p_ref[hh][...] = p[hh]
            al_ref[hh][...] = alpha[hh]
            m_ref[hh][...] = m_new[hh]

    k_own = k_ref[pl.ds(pl.multiple_of(qi * blk, blk), blk), :]
    key_io = lax.broadcasted_iota(I32, (blk, blk), 0)
    qry_io = lax.broadcasted_iota(I32, (blk, blk), 1)
    for hh in heads:
        s = jnp.dot(k_own[:, grp(hh)], qht[hh], preferred_element_type=F32) + bias_ref[hh, 0]
        s_ref[hh][...] = jnp.where(key_io <= qry_io, s, NEG_INF)
        m_ref[hh][...] = jnp.full((1, blk), NEG_INF, F32)
        l_ref[hh][...] = jnp.zeros((1, blk), F32)
        al_ref[hh][...] = jnp.ones((1, blk), F32)
        acc_ref[hh][...] = jnp.zeros((MB_DH, blk), F32)
        p_ref[hh][...] = jnp.zeros((blk, blk), BF16)

    def step(i, carry, far):
        pv = pv_stage(jnp.where(i <= 1, qi, i - 2))
        sm = softmax_stage()
        kn = k_ref[pl.ds(pl.multiple_of(i * blk, blk), blk), :]
        if far:
            row = [msk_ref[hh][pl.ds(i, 1), :] + bias_ref[hh, MB_BIAS_TILES - 1, 0:1, 0:1] for hh in heads]
            s_next = [jnp.dot(kn[:, grp(hh)], qht[hh], preferred_element_type=F32) + row[hh] for hh in heads]
        else:
            d = qi - i
            s_next = [jnp.dot(kn[:, grp(hh)], qht[hh], preferred_element_type=F32)
                      + bias_ref[hh, d] + msk_ref[hh][pl.ds(i, 1), :] for hh in heads]
        store_pv(*pv)
        for hh in heads:
            s_ref[hh][...] = s_next[hh]
        store_softmax(*sm)
        return carry

    n_far = jnp.maximum(qi - (MB_BIAS_TILES - 2), 0)
    lax.fori_loop(0, n_far, functools.partial(step, far=True), 0)
    lax.fori_loop(n_far, qi, functools.partial(step, far=False), 0)
    pv = pv_stage(jnp.where(qi <= 1, qi, qi - 2))
    sm = softmax_stage()
    store_pv(*pv)
    store_softmax(*sm)
    a_fin, l_fin = pv_stage(jnp.where(qi == 0, qi, qi - 1))
    out_t = jnp.concatenate([a_fin[hh] / l_fin[hh] for hh in heads], axis=0)
    o_ref[...] = out_t.T.astype(o_ref.dtype)


def moba_attention(pqk, vt, km, bias, batch, seq, qb0=0, nqb=None):
    nb = seq // MB_BLOCK
    nqb = nb if nqb is None else nqb
    t = batch * nqb * MB_BLOCK
    groups = MB_WIDTH // MB_PW
    return pl.pallas_call(
        functools.partial(_moba_kernel, qb0=qb0),
        grid=(batch, groups, nqb),
        in_specs=[
            pl.BlockSpec((MB_BLOCK, MB_PW), lambda b, j, i: (b * nb + qb0 + i, j)),
            pl.BlockSpec((seq, MB_PW), lambda b, j, i: (b, groups + j)),
            pl.BlockSpec((nb, MB_PAIR * MB_VROWS, MB_BLOCK), lambda b, j, i: (b, j, 0)),
            pl.BlockSpec((None, nb, MB_PW), lambda b, j, i: (b, 0, j)),
            pl.BlockSpec((MB_PAIR, MB_BIAS_TILES, MB_BLOCK, MB_BLOCK), lambda b, j, i: (j, 0, 0, 0)),
        ],
        out_specs=pl.BlockSpec((MB_BLOCK, MB_PW), lambda b, j, i: (b * nqb + i, j)),
        out_shape=jax.ShapeDtypeStruct((t, MB_WIDTH), BF16),
        scratch_shapes=(
            [pltpu.VMEM((1, MB_BLOCK), F32)] * (3 * MB_PAIR)
            + [pltpu.VMEM((MB_DH, MB_BLOCK), F32)] * MB_PAIR
            + [pltpu.VMEM((nb, MB_BLOCK), F32)] * MB_PAIR
            + [pltpu.VMEM((MB_BLOCK, MB_BLOCK), F32)] * MB_PAIR
            + [pltpu.VMEM((MB_BLOCK, MB_BLOCK), BF16)] * MB_PAIR
        ),
        compiler_params=_cparams(("parallel", "parallel", "arbitrary")),
        name="moba_attn",
    )(pqk, pqk, vt, km, bias)


def _t5_bucket(dist):
    max_exact = REL_BUCKETS // 2
    scaled = jnp.log(jnp.maximum(dist, 1).astype(F32) / max_exact) / math.log(REL_MAX_DIST / max_exact)
    large = jnp.minimum(max_exact + (scaled * (REL_BUCKETS - max_exact)).astype(I32), REL_BUCKETS - 1)
    return jnp.where(dist < max_exact, dist, large)


def moba_bias_tiles(rel_bias):
    blk = MB_BLOCK
    span = 2 * blk - 1
    x = jnp.arange(span) - (blk - 1)
    dist = jnp.maximum(jnp.arange(MB_BIAS_TILES)[:, None] * blk + x[None, :], 0)
    w = rel_bias.astype(F32).T[:, _t5_bucket(dist)]
    h = w.shape[0]
    wp = jnp.pad(w, ((0, 0), (0, 0), (0, 1)))[:, :, None, :]
    return pl.pallas_call(
        _toeplitz_kernel,
        grid=(h, MB_BIAS_TILES),
        in_specs=[pl.BlockSpec((None, None, 1, 2 * blk), lambda i, j: (i, j, 0, 0))],
        out_specs=pl.BlockSpec((None, None, blk, blk), lambda i, j: (i, j, 0, 0)),
        out_shape=jax.ShapeDtypeStruct((h, MB_BIAS_TILES, blk, blk), F32),
        compiler_params=_cparams(("parallel", "parallel")),
        name="moba_bias_tiles",
    )(wp)


def _toeplitz_kernel(w_ref, o_ref):
    blk = o_ref.shape[0]
    x = jnp.broadcast_to(w_ref[...], (blk, 2 * blk))
    o_ref[...] = pltpu.roll(x, 1, 1, stride=1, stride_axis=0)[:, blk:]


def _mix_kernel(x_ref, ya_ref, yb_ref, ga_ref, gb_ref, wa_ref, wb_ref, wo_ref, o_ref):
    za = jnp.dot(ya_ref[...], wa_ref[...], preferred_element_type=F32)
    zb = jnp.dot(yb_ref[...], wb_ref[...], preferred_element_type=F32)
    z = jax.nn.sigmoid(ga_ref[...].astype(F32)) * za + jax.nn.sigmoid(gb_ref[...].astype(F32)) * zb
    o_ref[...] = x_ref[...] + jnp.dot(z.astype(BF16), wo_ref[...], preferred_element_type=F32)


def _mem_kv_kernel(m_ref, g_ref, wk_ref, wv_ref, k_ref, v_ref):
    mn = _rms(m_ref[...], g_ref[...]).astype(BF16)
    k_ref[...] = jnp.dot(mn, wk_ref[...], preferred_element_type=F32).astype(BF16)
    v_ref[...] = jnp.dot(mn, wv_ref[...], preferred_element_type=F32).astype(BF16)


def mem_kv(mem, g, wk, wv):
    b, m, d = mem.shape
    spec = pl.BlockSpec((None, m, d), lambda i: (i, 0, 0))
    wspec = pl.BlockSpec((d, d), lambda i: (0, 0))
    return pl.pallas_call(
        _mem_kv_kernel,
        grid=(b,),
        in_specs=[spec, pl.BlockSpec((1, d), lambda i: (0, 0)), wspec, wspec],
        out_specs=[spec, spec],
        out_shape=[jax.ShapeDtypeStruct((b, m, d), BF16)] * 2,
        compiler_params=_cparams(("parallel",)),
        name="mem_kv",
    )(mem, g, wk, wv)


def _cross_kernel(x_ref, g_ref, wq_ref, k_ref, v_ref, wo_ref, o_ref):
    x = x_ref[...]
    d = x.shape[1]
    dh = d // X_HEADS
    h = _rms(x, g_ref[...]).astype(BF16)
    q = (jnp.dot(h, wq_ref[...], preferred_element_type=F32) * (dh ** -0.5)).astype(BF16)
    outs = []
    for hh in range(X_HEADS):
        sl = slice(hh * dh, (hh + 1) * dh)
        s = lax.dot_general(q[:, sl], k_ref[:, sl], (((1,), (1,)), ((), ())),
                            preferred_element_type=F32)
        p = jnp.exp(s - jnp.max(s, axis=1, keepdims=True))
        l = jnp.sum(p, axis=1, keepdims=True)
        o = jnp.dot(p.astype(BF16), v_ref[:, sl], preferred_element_type=F32) / l
        outs.append(o.astype(BF16))
    o = jnp.concatenate(outs, axis=1)
    o_ref[...] = x + jnp.dot(o, wo_ref[...], preferred_element_type=F32)


def _mix_cross_kernel(x_ref, ya_ref, yb_ref, ga_ref, gb_ref, wa_ref, wb_ref, wo_ref,
                      g_ref, wq_ref, k_ref, v_ref, wox_ref, o_ref, x1_ref):
    _mix_kernel(x_ref, ya_ref, yb_ref, ga_ref, gb_ref, wa_ref, wb_ref, wo_ref, x1_ref)
    _cross_kernel(x1_ref, g_ref, wq_ref, k_ref, v_ref, wox_ref, o_ref)


def mix_cross(x_all, ya, yb, pg, wa, wb, wo, g, wq, kx, vx, wox, row0, tok0, tm=512):
    t = yb.shape[0]
    assert t % tm == 0 and tok0 % tm == 0 and row0 % tm == 0
    d = x_all.shape[1]
    w = ya.shape[1]
    m = kx.shape[1]
    b0 = tok0 // tm
    x0 = row0 // tm
    const = lambda a: pl.BlockSpec(a.shape, lambda i: (0,) * a.ndim)
    kv = pl.BlockSpec((None, m, d), lambda i: (0, 0, 0))
    return pl.pallas_call(
        _mix_cross_kernel,
        grid=(t // tm,),
        in_specs=[
            pl.BlockSpec((tm, d), lambda i: (x0 + i, 0)),
            pl.BlockSpec((tm, w), lambda i: (i, 0)),
            pl.BlockSpec((tm, w), lambda i: (i, 0)),
            pl.BlockSpec((tm, d), lambda i: (b0 + i, 0)),
            pl.BlockSpec((tm, d), lambda i: (b0 + i, 1)),
            const(wa), const(wb), const(wo), const(g), const(wq), kv, kv, const(wox),
        ],
        out_specs=pl.BlockSpec((tm, d), lambda i: (i, 0)),
        out_shape=jax.ShapeDtypeStruct((t, d), F32),
        scratch_shapes=[pltpu.VMEM((tm, d), F32)],
        compiler_params=_cparams(("parallel",)),
        name="mix_cross",
    )(x_all, ya, yb, pg, pg, wa, wb, wo, g, wq, kx, vx, wox)


def _topk_rows(sc, k):
    n = sc.shape[0]
    io = lax.broadcasted_iota(I32, sc.shape, 0).astype(F32)
    vals, ids = [], []
    for _ in range(k):
        m = jnp.max(sc, axis=0, keepdims=True)
        ix = jnp.argmax(sc, axis=0, keepdims=True).astype(F32)
        vals.append(m)
        ids.append(ix)
        sc = jnp.where(io == ix, NEG_INF, sc)
    return jnp.concatenate(vals, axis=0), jnp.concatenate(ids, axis=0).astype(I32)


def _pack_bf16_halves(h):
    bits = lax.bitcast_convert_type(h, I32)
    r = bits + 0x7FFF + (lax.shift_right_logical(bits, 16) & 1)
    half = h.shape[1] // 2
    return lax.shift_right_logical(r[:, :half], 16) | (r[:, half:] & HI_MASK)


def _route_kernel(x_ref, g_ref, wq_ref, sk_ref, hp_ref, idx_ref, w_ref, hb_ref, it_ref, wt_ref):
    p = pl.program_id(1)

    @pl.when(p == 0)
    def _():
        h = _rms(x_ref[...], g_ref[...])
        hp_ref[...] = _pack_bf16_halves(h)
        hb_ref[...] = h.astype(BF16)

    qh = jnp.dot(hb_ref[...], wq_ref[...], preferred_element_type=F32)
    tops = []
    for c in range(2):
        seg = qh[:, c * PEER_HALF:(c + 1) * PEER_HALF]
        sc = lax.dot_general(sk_ref[c], seg, (((1,), (1,)), ((), ())),
                             precision=lax.Precision.HIGHEST, preferred_element_type=F32)
        tops.append(_topk_rows(sc, PEER_TOPK))
    (s0, i0), (s1, i1) = tops
    k = PEER_TOPK
    sub = 8
    tm = s0.shape[1]
    r8 = lax.broadcasted_iota(I32, (sub, tm), 0)
    r16 = lax.broadcasted_iota(I32, (k, tm), 0)
    cand_b = [s0[0:1] + s1, s0[1:2] + s1[:sub]]
    cidx_b = [i0[0:1] * PEER_NKEYS + i1, i0[1:2] * PEER_NKEYS + i1[:sub]]
    pos_b = [r16, k + r8]
    for a in range(2, sub):
        keep = r8 < (k // (a + 1))
        cand_b.append(jnp.where(keep, s0[a:a + 1] + s1[:sub], NEG_INF))
        cidx_b.append(i0[a:a + 1] * PEER_NKEYS + i1[:sub])
        pos_b.append(a * k + r8)
    cand_b.append(s0[sub:] + s1[0:1])
    cidx_b.append(i0[sub:] * PEER_NKEYS + i1[0:1])
    pos_b.append((sub + r8) * k)
    cand = jnp.concatenate(cand_b, axis=0)
    cidx = jnp.concatenate(cidx_b, axis=0)
    pos = jnp.concatenate(pos_b, axis=0).astype(F32)
    vals, ids = [], []
    for _ in range(k):
        m = jnp.max(cand, axis=0, keepdims=True)
        px = jnp.min(jnp.where(cand == m, pos, float(k * k)), axis=0, keepdims=True)
        hit = pos == px
        vals.append(m)
        ids.append(jnp.sum(jnp.where(hit, cidx, 0), axis=0, keepdims=True))
        cand = jnp.where(hit, NEG_INF, cand)
    sf = jnp.concatenate(vals, axis=0)
    e = jnp.exp(sf - sf[0:1])
    rows = pl.ds(pl.multiple_of(p * PEER_TOPK, PEER_TOPK), PEER_TOPK)
    wt_ref[rows, :] = e / jnp.sum(e, axis=0, keepdims=True)
    it_ref[rows, :] = jnp.concatenate(ids, axis=0)

    @pl.when(p == pl.num_programs(1) - 1)
    def _():
        idx_ref[...] = it_ref[...].T
        w_ref[...] = wt_ref[...].T


def peer_route(x2d, g, wq, sk, tok0, t, tm=1024):
    assert t % tm == 0 and tok0 % tm == 0
    d = x2d.shape[1]
    ph = sk.shape[0]
    nsel = ph * PEER_TOPK
    blk0 = tok0 // tm
    return pl.pallas_call(
        _route_kernel,
        grid=(t // tm, ph),
        in_specs=[
            pl.BlockSpec((tm, d), lambda i, p: (blk0 + i, 0)),
            pl.BlockSpec((1, d), lambda i, p: (0, 0)),
            pl.BlockSpec((d, 2 * PEER_HALF), lambda i, p: (0, p)),
            pl.BlockSpec((None, 2, PEER_NKEYS, PEER_HALF), lambda i, p: (p, 0, 0, 0)),
        ],
        out_specs=[
            pl.BlockSpec((tm, d // 2), lambda i, p: (i, 0)),
            pl.BlockSpec((tm, nsel), lambda i, p: (i, 0)),
            pl.BlockSpec((tm, nsel), lambda i, p: (i, 0)),
        ],
        out_shape=[jax.ShapeDtypeStruct((t, d // 2), I32),
                   jax.ShapeDtypeStruct((t, nsel), I32),
                   jax.ShapeDtypeStruct((t, nsel), F32)],
        scratch_shapes=[pltpu.VMEM((tm, d), BF16),
                        pltpu.VMEM((nsel, tm), I32),
                        pltpu.VMEM((nsel, tm), F32)],
        compiler_params=_cparams(("parallel", "arbitrary")),
        name="peer_route",
    )(x2d, g, wq, sk)


def _final_kernel(x_ref, y_ref, g_ref, *rest):
    o_ref = rest[-1]
    o_ref[...] = _rms(x_ref[...] + y_ref[...], g_ref[...])


def final_norm_into(out, xs, y, g, row0, total, tm=512):
    t, d = y.shape
    assert t % tm == 0 and row0 % tm == 0 and total % tm == 0
    blk0 = row0 // tm
    spec = pl.BlockSpec((tm, d), lambda i: (i, 0))
    in_specs = [spec, spec, pl.BlockSpec((1, d), lambda i: (0, 0))]
    args = [xs, y, g]
    aliases = {}
    if out is not None:
        in_specs.append(pl.BlockSpec(memory_space=pl.ANY))
        args.append(out)
        aliases = {3: 0}
    return pl.pallas_call(
        _final_kernel, grid=(t // tm,),
        in_specs=in_specs,
        out_specs=pl.BlockSpec((tm, d), lambda i: (blk0 + i, 0)),
        out_shape=jax.ShapeDtypeStruct((total, d), F32),
        input_output_aliases=aliases,
        compiler_params=_cparams(("parallel",)), name="final_norm",
    )(*args)


SC_CORES = 2
SC_SUBCORES = 16
SC_WORKERS = SC_CORES * SC_SUBCORES
SC_LANES = 16
SC_GROUP = 32


def _sc_mesh():
    return plsc.VectorSubcoreMesh(core_axis_name="c", subcore_axis_name="s")


def _sc_params():
    return pltpu.CompilerParams(needs_layout_passes=False)


def _sc_worker_id():
    return lax.axis_index("s") * SC_CORES + lax.axis_index("c")


SC_ROW_LANE = 128


def _sc_unit_off(u):
    off = u * SC_LANES
    return off if isinstance(off, int) else pl.multiple_of(off, SC_LANES)


GELU_C0 = math.sqrt(2.0 / math.pi)
GELU_C1 = 0.044715


def _gelu_tanh(x):
    z = GELU_C0 * (x + GELU_C1 * (x * x * x))
    th = 1.0 - 2.0 / (jnp.exp(2.0 * z) + 1.0)
    return 0.5 * x * (1.0 + th)


SC_PK_RING = 4
SC_PK_SUB = 4
HI_MASK = -65536


def _pack_tables_kernel(u_ref, v_ref, o_ref):
    for part, ref in enumerate((u_ref, v_ref)):
        words = _pack_bf16_halves(ref[...])
        for sub in range(SC_PK_SUB):
            o_ref[:, part * SC_PK_SUB + sub, :] = words[:, sub * SC_ROW_LANE:(sub + 1) * SC_ROW_LANE]


def pack_expert_tables(u, v, te=512):
    e, d = u.shape
    assert d == 2 * SC_PK_SUB * SC_ROW_LANE
    spec = pl.BlockSpec((te, d), lambda i: (i, 0))
    return pl.pallas_call(
        _pack_tables_kernel, grid=(e // te,), in_specs=[spec, spec],
        out_specs=pl.BlockSpec((te, 2 * SC_PK_SUB, SC_ROW_LANE), lambda i: (i, 0, 0)),
        out_shape=jax.ShapeDtypeStruct((e, 2 * SC_PK_SUB, SC_ROW_LANE), I32),
        compiler_params=_cparams(("parallel",)), name="pack_expert_tables",
    )(u, v)


def _unpack_halves(x32):
    w = plsc.bitcast(x32, I32)
    return plsc.bitcast(w << 16, F32), plsc.bitcast(w & HI_MASK, F32)


def _tree_sum(xs):
    while len(xs) > 1:
        xs = [xs[i] + xs[i + 1] for i in range(0, len(xs), 2)]
    return xs[0]


def peer_experts_pk_sc(tab_uv, idx_flat, w_flat, hp, d):
    t = hp.shape[0]
    nsel = PEER_SEL
    assert t % SC_WORKERS == 0 and d == 2 * SC_PK_SUB * SC_ROW_LANE
    tpw = t // SC_WORKERS
    g = SC_GROUP if tpw % SC_GROUP == 0 else SC_GROUP // 2
    assert tpw % g == 0
    groups = tpw // g
    heads = nsel // SC_LANES
    chunks = d // 32
    units = g * heads
    ring = SC_PK_RING
    assert units % ring == 0
    row_buf = pltpu.VMEM((SC_LANES, 2 * SC_PK_SUB, SC_ROW_LANE), I32)

    def row_words(rows, r, wc, sub0):
        per = SC_ROW_LANE // SC_LANES
        return plsc.bitcast(
            rows[r, sub0 + wc // per, pl.ds(pl.multiple_of((wc % per) * SC_LANES, SC_LANES), SC_LANES)], BF16)

    def ring_loop(n_units, start, wait, compute):
        for u in range(ring - 1):
            start(u, u)

        @pl.loop(0, n_units, step=ring)
        def _(uu):
            for b in range(ring):
                u = uu + b
                nxt = u + (ring - 1)

                @pl.when(nxt < n_units)
                def _():
                    start(nxt, (b + ring - 1) % ring)

                wait(u, b)
                compute(u, b)

    @functools.partial(
        pl.kernel, mesh=_sc_mesh(),
        out_type=jax.ShapeDtypeStruct((t, d), F32),
        scratch_types=[
            pltpu.VMEM((g * nsel,), I32),
            pltpu.VMEM((g * nsel,), F32),
            pltpu.VMEM((g, d // 2), I32),
            pltpu.VMEM((g, d), F32),
            pltpu.VMEM((SC_LANES * SC_LANES,), F32),
            [row_buf] * ring,
            [pltpu.SemaphoreType.DMA] * ring,
        ],
        compiler_params=_sc_params(),
        name="peer_experts_pk_sc",
    )
    def k(tab_hbm, idx_hbm, w_hbm, h_hbm, out_hbm, idx_v, coef_v, h_v, y_v, red_v, rows, sems):
        wid = _sc_worker_id()
        lane = lax.iota(I32, SC_LANES)

        def copy(u, slot):
            ids = idx_v.at[pl.ds(_sc_unit_off(u), SC_LANES)]
            return pltpu.make_async_copy(tab_hbm.at[ids], rows[slot], sems[slot])

        def dots(u, slot):
            tt = u // heads

            def body(cp, accs):
                out = []
                hv = [plsc.bitcast(h_v[tt, pl.ds(pl.multiple_of((2 * cp + i) * SC_LANES, SC_LANES), SC_LANES)], BF16)
                      for i in range(2)]
                for r in range(SC_LANES):
                    pr = (row_words(rows[slot], r, 2 * cp, 0) * hv[0]
                          + row_words(rows[slot], r, 2 * cp + 1, 0) * hv[1])
                    lo, hi = _unpack_halves(pr)
                    out.append(accs[r] + lo + hi)
                return tuple(out)

            accs = lax.fori_loop(0, chunks // 2, body,
                                 tuple(jnp.zeros((SC_LANES,), F32) for _ in range(SC_LANES)))
            for r in range(SC_LANES):
                red_v[pl.ds(r * SC_LANES, SC_LANES)] = accs[r]
            act = _tree_sum([plsc.load_gather(red_v, [lane * SC_LANES + j]) for j in range(SC_LANES)])
            sl = pl.ds(_sc_unit_off(u), SC_LANES)
            coef_v[sl] = coef_v[sl] * _gelu_tanh(act)

        def combine(u, slot):
            tt = u // heads
            first = (u % heads) == 0
            cb = []
            for r in range(SC_LANES):
                c = plsc.load_gather(coef_v, [jnp.full((SC_LANES,), u * SC_LANES + r, I32)])
                cb.append(plsc.pack(c, c, format=plsc.PackFormat.INTERLEAVED))

            @plsc.parallel_loop(0, chunks, unroll=2)
            def _(wc):
                lo, hi = _unpack_halves(
                    _tree_sum([cb[r] * row_words(rows[slot], r, wc, SC_PK_SUB) for r in range(SC_LANES)]))
                for half, val in ((0, lo), (1, hi)):
                    sl = pl.ds(pl.multiple_of(half * (d // 2) + wc * SC_LANES, SC_LANES), SC_LANES)
                    y_v[tt, sl] = val + jnp.where(first, 0.0, y_v[tt, sl])

        def unit(u, slot):
            dots(u, slot)
            combine(u, slot)

        @pl.loop(0, groups)
        def _(gi):
            base = wid * tpw + gi * g
            pltpu.sync_copy(idx_hbm.at[pl.ds(base * nsel, g * nsel)], idx_v)
            pltpu.sync_copy(w_hbm.at[pl.ds(base * nsel, g * nsel)], coef_v)
            pltpu.sync_copy(h_hbm.at[pl.ds(base, g)], h_v)
            ring_loop(units, lambda u, s: copy(u, s).start(), lambda u, s: copy(u, s).wait(), unit)
            pltpu.sync_copy(y_v, out_hbm.at[pl.ds(base, g)])

    return k(tab_uv, idx_flat, w_flat, hp)


def kernel(x, mem, rel_bias, ln_mix, w_in, hg_lower, hg_norm, w_up_a, w_up_b, w_out, ln_cross, ln_mem, wq_x, wk_x, wv_x, wo_x, ln_ffn, peer_query, peer_subkeys, peer_u, peer_v, ln_final):
    b, s, d = x.shape
    depth = w_in.shape[0]
    assert depth == 1, "the residual after PEER is fused into the final norm"
    assert s % MB_BLOCK == 0 and s % HG_CHUNK == 0 and s % (PEER_SLICES * SC_WORKERS * SC_GROUP) == 0
    nb = s // MB_BLOCK
    row = lambda a: a.reshape(1, -1).astype(F32)
    lb_all = jnp.cumsum(jax.nn.softmax(hg_lower.astype(F32), axis=0), axis=0)
    bias = moba_bias_tiles(rel_bias)
    n_hg = 4 * HG_WIDTH
    n_qk = 2 * MB_WIDTH
    n_mb = 3 * MB_WIDTH
    l = 0
    w = cast_bf16(w_in[l].astype(F32))
    w_vt = cast_bf16_t(w_in[l].astype(F32), n_hg + n_qk, n_mb - n_qk)
    wa, wb, wo = w_up_a[l].astype(BF16), w_up_b[l].astype(BF16), w_out[l].astype(BF16)
    wqx, wox = wq_x[l].astype(BF16), wo_x[l].astype(BF16)
    wpq, sk = peer_query[l].astype(BF16), peer_subkeys[l].astype(F32)
    tab_uv = pack_expert_tables(peer_u[l].astype(F32), peer_v[l].astype(F32))
    kx, vx = mem_kv(mem, row(ln_mem[l]), wk_x[l].astype(BF16), wv_x[l].astype(BF16))

    x_all = x.reshape(b * s, d)
    outs = []
    for bi in range(b):
        p0, pqk, km, vt, pg = in_proj(x_all, row(ln_mix[l]), w, w_vt, bi * s, s)
        hg_state = jnp.zeros((HG_HEADS, HG_D, HG_D), F32)
        km = km.reshape(1, nb, MB_WIDTH)
        sizes = [s // PEER_SLICES] * PEER_SLICES
        if bi == b - 1:
            first = SC_WORKERS * SC_GROUP // 2
            sizes = [first, sizes[0] - first] + sizes[1:]
        tok0 = 0
        for ts in sizes:
            ya, hg_state = hgrn2(p0, row(lb_all[l]), row(hg_norm[l]), hg_state, tok0, ts)
            yb = moba_attention(pqk, vt, km, bias, 1, s, tok0 // MB_BLOCK, ts // MB_BLOCK)
            xs = mix_cross(x_all, ya, yb, pg, wa, wb, wo, row(ln_cross[l]), wqx, kx[bi:bi + 1], vx[bi:bi + 1], wox,
                           bi * s + tok0, tok0)
            hp, eidx, wts = peer_route(xs, row(ln_ffn[l]), wpq, sk, 0, ts, math.gcd(ts, 1024))
            y = peer_experts_pk_sc(tab_uv, eidx.reshape(ts * PEER_SEL), wts.reshape(ts * PEER_SEL), hp, d)
            outs.append((xs, y, bi * s + tok0))
            tok0 += ts
    out = None
    for xs, y, row0 in sorted(outs, key=lambda e: (-(e[2] // s), e[2])):
        out = final_norm_into(out, xs, y, row(ln_final), row0, b * s)
    return out.reshape(b, s, d)
```

```python
import functools
import math

import jax
import jax.numpy as jnp
from jax import lax
from jax.experimental import pallas as pl
from jax.experimental.pallas import tpu as pltpu
from jax.experimental.pallas import tpu_sc as plsc

F32 = jnp.float32
BF16 = jnp.bfloat16
I32 = jnp.int32
EPS = 1e-6
NEG_INF = float("-inf")

HG_HEADS = 4
HG_D = 128
HG_WIDTH = HG_HEADS * HG_D
HG_CHUNK = 64
HG_SUB = 16
MB_HEADS = 8
MB_DH = 64
MB_WIDTH = MB_HEADS * MB_DH
MB_BLOCK = 256
MB_TOPK = 3
MB_BIAS_TILES = 8
REL_BUCKETS = 32
REL_MAX_DIST = 2048
X_HEADS = 4
PEER_HEADS = 8
PEER_NKEYS = 128
PEER_TOPK = 16
PEER_HALF = 128
PEER_SEL = PEER_HEADS * PEER_TOPK
PEER_SLICES = 4

VMEM_LIMIT = 56 * 1024 * 1024


def _cparams(sem):
    return pltpu.CompilerParams(dimension_semantics=sem, vmem_limit_bytes=VMEM_LIMIT)


def _rms(x, g):
    ms = jnp.mean(x * x, axis=-1, keepdims=True)
    return x * lax.rsqrt(ms + EPS) * g


def _cast_t_kernel(w_ref, o_ref):
    o_ref[...] = w_ref[...].T.astype(o_ref.dtype)


def cast_bf16_t(w, col0, n):
    k = w.shape[0]
    assert col0 % n == 0
    return pl.pallas_call(
        _cast_t_kernel, grid=(1,),
        in_specs=[pl.BlockSpec((k, n), lambda i: (0, col0 // n))],
        out_specs=pl.BlockSpec((n, k), lambda i: (0, 0)),
        out_shape=jax.ShapeDtypeStruct((n, k), BF16),
        compiler_params=_cparams(("arbitrary",)), name="cast_bf16_t",
    )(w)


def _cast_kernel(w_ref, o_ref):
    o_ref[...] = w_ref[...].astype(o_ref.dtype)


def cast_bf16(w, tn=512):
    k, n = w.shape
    assert n % tn == 0
    spec = pl.BlockSpec((k, tn), lambda j: (0, j))
    return pl.pallas_call(
        _cast_kernel, grid=(n // tn,), in_specs=[spec], out_specs=spec,
        out_shape=jax.ShapeDtypeStruct((k, n), BF16),
        compiler_params=_cparams(("parallel",)), name="cast_bf16",
    )(w)


def _in_proj_kernel(x_ref, g_ref, w0_ref, w1_ref, wvt_ref, *rest):
    w2_refs, (o0_ref, o1_ref, okm_ref, ovt_ref, o2_ref) = rest[:-5], rest[-5:]
    h = _rms(x_ref[...], g_ref[...]).astype(BF16)
    o0_ref[...] = jnp.dot(h, w0_ref[...], preferred_element_type=F32)
    qk = jnp.dot(h, w1_ref[...], preferred_element_type=F32)
    o1_ref[...] = qk.astype(BF16)
    okm_ref[0] = jnp.mean(qk[:, MB_WIDTH:], axis=0, keepdims=True)
    vt = lax.dot_general(wvt_ref[...], h, (((1,), (1,)), ((), ())), preferred_element_type=F32).astype(BF16)
    for hd in range(MB_HEADS):
        ovt_ref[0, hd * MB_VROWS:hd * MB_VROWS + MB_DH, :] = vt[hd * MB_DH:(hd + 1) * MB_DH]
        ovt_ref[0, hd * MB_VROWS + MB_DH:(hd + 1) * MB_VROWS, :] = jnp.ones((MB_ONES, vt.shape[1]), BF16)
    wg = w2_refs[0].shape[1]
    for j, w2_ref in enumerate(w2_refs):
        o2_ref[:, j * wg:(j + 1) * wg] = jnp.dot(h, w2_ref[...], preferred_element_type=F32).astype(BF16)


def in_proj(x_all, g, w, wvt, row0, t):
    d = x_all.shape[1]
    tm = MB_BLOCK
    assert t % tm == 0 and row0 % tm == 0
    blk0 = row0 // tm
    n0, n1, nv, n2 = 4 * HG_WIDTH, 2 * MB_WIDTH, MB_VT_ROWS, 2 * d
    wg = MB_WIDTH
    assert wvt.shape == (MB_WIDTH, d) and w.shape == (d, n0 + n1 + MB_WIDTH + n2)
    assert n0 % n1 == 0 and (n0 + n1 + MB_WIDTH) % wg == 0 and n2 % wg == 0
    full = lambda a: pl.BlockSpec(a.shape, lambda i: (0, 0))
    g0 = (n0 + n1 + MB_WIDTH) // wg
    w_specs = ([pl.BlockSpec((d, n0), lambda i: (0, 0)), pl.BlockSpec((d, n1), lambda i: (0, n0 // n1)), full(wvt)]
               + [pl.BlockSpec((d, wg), lambda i, j=j: (0, g0 + j)) for j in range(n2 // wg)])
    return pl.pallas_call(
        _in_proj_kernel,
        grid=(t // tm,),
        in_specs=[pl.BlockSpec((tm, d), lambda i: (blk0 + i, 0)), full(g)] + w_specs,
        out_specs=[pl.BlockSpec((tm, n0), lambda i: (i, 0)),
                   pl.BlockSpec((tm, n1), lambda i: (i, 0)),
                   pl.BlockSpec((1, 1, MB_WIDTH), lambda i: (i, 0, 0)),
                   pl.BlockSpec((1, nv, tm), lambda i: (i, 0, 0)),
                   pl.BlockSpec((tm, n2), lambda i: (i, 0))],
        out_shape=[jax.ShapeDtypeStruct((t, n0), F32),
                   jax.ShapeDtypeStruct((t, n1), BF16),
                   jax.ShapeDtypeStruct((t // tm, 1, MB_WIDTH), F32),
                   jax.ShapeDtypeStruct((t // tm, nv, tm), BF16),
                   jax.ShapeDtypeStruct((t, n2), BF16)],
        compiler_params=_cparams(("parallel",)),
        name="in_proj",
    )(x_all, g, w, w, wvt, *([w] * (n2 // wg)))


def _hgrn_kernel(q_ref, f_ref, i_ref, g_ref, lb_ref, gain_ref, st0_ref, o_ref, stn_ref, st_ref):
    c = pl.program_id(0)

    @pl.when(c == 0)
    def _():
        st_ref[...] = st0_ref[...]

    C, S = HG_CHUNK, HG_SUB
    row = lax.broadcasted_iota(I32, (C, C), 0)
    col = lax.broadcasted_iota(I32, (C, C), 1)
    tril = (row >= col).astype(F32)
    t_iota = lax.broadcasted_iota(I32, (S, 1), 0)

    for h in range(HG_HEADS):
        sl = slice(h * HG_D, (h + 1) * HG_D)
        q = q_ref[:, sl]
        v = i_ref[:, sl]
        lb = lb_ref[:, sl]
        f = lb + (1.0 - lb) * jax.nn.sigmoid(f_ref[:, sl])
        lf = jnp.log(f)
        k = 1.0 - f
        b = jnp.dot(tril, lf, precision=lax.Precision.HIGHEST, preferred_element_type=F32)
        st = st_ref[h]
        vb = v.astype(BF16)
        qd = (q * jnp.exp(b)).astype(BF16)
        o_inter = lax.dot_general(qd, st.astype(BF16), (((1,), (1,)), ((), ())),
                                  preferred_element_type=F32)
        outs = []
        for i in range(C // S):
            r0 = i * S
            qi = q[r0:r0 + S]
            ki = k[r0:r0 + S]
            bi = b[r0:r0 + S]
            vi = v[r0:r0 + S]
            oi = o_inter[r0:r0 + S]
            if i > 0:
                bs = b[r0 - 1:r0]
                qh = (qi * jnp.exp(bi - bs)).astype(BF16)
                kh = (k[:r0] * jnp.exp(bs - b[:r0])).astype(BF16)
                a = lax.dot_general(qh, kh, (((1,), (1,)), ((), ())), preferred_element_type=F32)
                oi = oi + jnp.dot(a.astype(BF16), vb[:r0], preferred_element_type=F32)
            half = S // 2
            o_half = [oi[:half], oi[half:]]
            for s in range(S):
                for hf in range(s // half, 2):
                    rows = slice(hf * half, (hf + 1) * half)
                    dec = jnp.exp(jnp.minimum(bi[rows] - bi[s:s + 1], 0.0))
                    a_s = jnp.sum(qi[rows] * ki[s:s + 1] * dec, axis=-1, keepdims=True)
                    a_s = jnp.where(t_iota[rows] >= s, a_s, 0.0)
                    o_half[hf] = o_half[hf] + a_s * vi[s:s + 1]
            outs.extend(o_half)
        o = jnp.concatenate(outs, axis=0)
        b_end = b[C - 1:C]
        kd = (k * jnp.exp(b_end - b)).astype(BF16)
        upd = lax.dot_general(vb, kd, (((0,), (0,)), ((), ())), preferred_element_type=F32)
        st_ref[h] = st * jnp.exp(b_end) + upd
        o = o * lax.rsqrt(jnp.mean(o * o, axis=-1, keepdims=True) + EPS)
        g = g_ref[:, sl]
        o_ref[:, sl] = (o * gain_ref[:, sl] * (g * jax.nn.sigmoid(g))).astype(o_ref.dtype)

    @pl.when(c == pl.num_programs(0) - 1)
    def _():
        stn_ref[...] = st_ref[...]


def hgrn2(p0, lb, gain, state, tok0, t):
    assert t % HG_CHUNK == 0 and tok0 % HG_CHUNK == 0
    nc = t // HG_CHUNK
    c0 = tok0 // HG_CHUNK
    w = HG_WIDTH

    def col(j):
        return pl.BlockSpec((HG_CHUNK, w), lambda c, j=j: (c0 + c, j))

    st_spec = pl.BlockSpec(state.shape, lambda c: (0, 0, 0))
    return pl.pallas_call(
        _hgrn_kernel,
        grid=(nc,),
        in_specs=[col(0), col(1), col(2), col(3),
                  pl.BlockSpec((1, w), lambda c: (0, 0)),
                  pl.BlockSpec((1, w), lambda c: (0, 0)),
                  st_spec],
        out_specs=[pl.BlockSpec((HG_CHUNK, w), lambda c: (c, 0)), st_spec],
        out_shape=[jax.ShapeDtypeStruct((t, w), BF16), jax.ShapeDtypeStruct(state.shape, F32)],
        scratch_shapes=[pltpu.VMEM((HG_HEADS, HG_D, HG_D), F32)],
        compiler_params=_cparams(("arbitrary",)),
        name="hgrn2",
    )(p0, p0, p0, p0, lb, gain, state)


MB_PAIR = 4
MB_PW = MB_PAIR * MB_DH
MB_LG = 128
MB_ONES = 16
MB_VROWS = MB_DH + MB_ONES
MB_VT_ROWS = MB_HEADS * MB_VROWS


def _moba_kernel(q_ref, k_ref, vt_ref, km_ref, bias_ref, o_ref, *scratch, qb0):
    m_ref, l_ref, al_ref, acc_ref, msk_ref, s_ref, p_ref = (
        scratch[i * MB_PAIR:(i + 1) * MB_PAIR] for i in range(7))
    qi = pl.program_id(2) + qb0
    nb = km_ref.shape[0]
    blk = MB_BLOCK
    heads = range(MB_PAIR)
    grp = lambda hh: slice((hh // 2) * MB_LG, (hh // 2 + 1) * MB_LG)
    q = q_ref[...]
    lane = lax.broadcasted_iota(I32, (blk, MB_LG), 1)
    in_head = [(lane < MB_DH) if hh % 2 == 0 else (lane >= MB_DH) for hh in heads]
    qs = q * jnp.asarray(MB_DH ** -0.5, BF16)
    nt = (((1,), (1,)), ((), ()))
    qf = q.astype(F32)
    qht = [jnp.where(in_head[hh], qs[:, grp(hh)].astype(F32), 0.0).T.astype(BF16) for hh in heads]

    n_io = lax.broadcasted_iota(I32, (nb, blk), 0)
    for hh in heads:
        gate = lax.dot_general(km_ref[:, grp(hh)], jnp.where(in_head[hh], qf[:, grp(hh)], 0.0), nt,
                               precision=lax.Precision.HIGHEST, preferred_element_type=F32)
        gate = jnp.where(n_io < qi, gate, NEG_INF)
        chosen = n_io < 0
        for _ in range(MB_TOPK):
            mx = jnp.max(gate, axis=0, keepdims=True)
            ix = jnp.min(jnp.where(gate == mx, n_io, nb), axis=0, keepdims=True)
            hit = n_io == ix
            chosen = chosen | (hit & (mx > NEG_INF))
            gate = jnp.where(hit, NEG_INF, gate)
        msk_ref[hh][...] = jnp.where(chosen, 0.0, NEG_INF)

    vrows = lambda hh: slice(hh * MB_VROWS, (hh + 1) * MB_VROWS)

    def pv_stage(blk_idx):
        vtb = vt_ref[blk_idx]
        r = [jnp.dot(vtb[vrows(hh)], p_ref[hh][...], preferred_element_type=F32) for hh in heads]
        al = [al_ref[hh][...] for hh in heads]
        a_new = [al[hh] * acc_ref[hh][...] + r[hh][:MB_DH] for hh in heads]
        l_new = [al[hh] * l_ref[hh][...] + r[hh][MB_DH:MB_DH + 1] for hh in heads]
        return a_new, l_new

    def store_pv(a_new, l_new):
        for hh in heads:
            acc_ref[hh][...] = a_new[hh]
            l_ref[hh][...] = l_new[hh]

    def softmax_stage():
        s = [s_ref[hh][...] for hh in heads]
        m_old = [m_ref[hh][...] for hh in heads]
        m_new = [jnp.maximum(m_old[hh], jnp.max(s[hh], axis=0, keepdims=True)) for hh in heads]
        alpha = [jnp.exp(m_old[hh] - m_new[hh]) for hh in heads]
        p = [jnp.exp((s[hh] - m_new[hh]).astype(BF16)) for hh in heads]
        return p, alpha, m_new

    def store_softmax(p, alpha, m_new):
        for hh in heads:
            p_ref[hh][...] = p[hh]
            al_ref[hh][...] = alpha[hh]
            m_ref[hh][...] = m_new[hh]

    k_own = k_ref[pl.ds(pl.multiple_of(qi * blk, blk), blk), :]
    key_io = lax.broadcasted_iota(I32, (blk, blk), 0)
    qry_io = lax.broadcasted_iota(I32, (blk, blk), 1)
    for hh in heads:
        s = jnp.dot(k_own[:, grp(hh)], qht[hh], preferred_element_type=F32) + bias_ref[hh, 0]
        s_ref[hh][...] = jnp.where(key_io <= qry_io, s, NEG_INF)
        m_ref[hh][...] = jnp.full((1, blk), NEG_INF, F32)
        l_ref[hh][...] = jnp.zeros((1, blk), F32)
        al_ref[hh][...] = jnp.ones((1, blk), F32)
        acc_ref[hh][...] = jnp.zeros((MB_DH, blk), F32)
        p_ref[hh][...] = jnp.zeros((blk, blk), BF16)

    def step(i, carry, far):
        pv = pv_stage(jnp.where(i <= 1, qi, i - 2))
        sm = softmax_stage()
        kn = k_ref[pl.ds(pl.multiple_of(i * blk, blk), blk), :]
        if far:
            row = [msk_ref[hh][pl.ds(i, 1), :] + bias_ref[hh, MB_BIAS_TILES - 1, 0:1, 0:1] for hh in heads]
            s_next = [jnp.dot(kn[:, grp(hh)], qht[hh], preferred_element_type=F32) + row[hh] for hh in heads]
        else:
            d = qi - i
            s_next = [jnp.dot(kn[:, grp(hh)], qht[hh], preferred_element_type=F32)
                      + bias_ref[hh, d] + msk_ref[hh][pl.ds(i, 1), :] for hh in heads]
        store_pv(*pv)
        for hh in heads:
            s_ref[hh][...] = s_next[hh]
        store_softmax(*sm)
        return carry

    n_far = jnp.maximum(qi - (MB_BIAS_TILES - 2), 0)
    lax.fori_loop(0, n_far, functools.partial(step, far=True), 0)
    lax.fori_loop(n_far, qi, functools.partial(step, far=False), 0)
    pv = pv_stage(jnp.where(qi <= 1, qi, qi - 2))
    sm = softmax_stage()
    store_pv(*pv)
    store_softmax(*sm)
    a_fin, l_fin = pv_stage(jnp.where(qi == 0, qi, qi - 1))
    out_t = jnp.concatenate([a_fin[hh] / l_fin[hh] for hh in heads], axis=0)
    o_ref[...] = out_t.T.astype(o_ref.dtype)


def moba_attention(pqk, vt, km, bias, batch, seq, qb0=0, nqb=None):
    nb = seq // MB_BLOCK
    nqb = nb if nqb is None else nqb
    t = batch * nqb * MB_BLOCK
    groups = MB_WIDTH // MB_PW
    return pl.pallas_call(
        functools.partial(_moba_kernel, qb0=qb0),
        grid=(batch, groups, nqb),
        in_specs=[
            pl.BlockSpec((MB_BLOCK, MB_PW), lambda b, j, i: (b * nb + qb0 + i, j)),
            pl.BlockSpec((seq, MB_PW), lambda b, j, i: (b, groups + j)),
            pl.BlockSpec((nb, MB_PAIR * MB_VROWS, MB_BLOCK), lambda b, j, i: (b, j, 0)),
            pl.BlockSpec((None, nb, MB_PW), lambda b, j, i: (b, 0, j)),
            pl.BlockSpec((MB_PAIR, MB_BIAS_TILES, MB_BLOCK, MB_BLOCK), lambda b, j, i: (j, 0, 0, 0)),
        ],
        out_specs=pl.BlockSpec((MB_BLOCK, MB_PW), lambda b, j, i: (b * nqb + i, j)),
        out_shape=jax.ShapeDtypeStruct((t, MB_WIDTH), BF16),
        scratch_shapes=(
            [pltpu.VMEM((1, MB_BLOCK), F32)] * (3 * MB_PAIR)
            + [pltpu.VMEM((MB_DH, MB_BLOCK), F32)] * MB_PAIR
            + [pltpu.VMEM((nb, MB_BLOCK), F32)] * MB_PAIR
            + [pltpu.VMEM((MB_BLOCK, MB_BLOCK), F32)] * MB_PAIR
            + [pltpu.VMEM((MB_BLOCK, MB_BLOCK), BF16)] * MB_PAIR
        ),
        compiler_params=_cparams(("parallel", "parallel", "arbitrary")),
        name="moba_attn",
    )(pqk, pqk, vt, km, bias)


def _t5_bucket(dist):
    max_exact = REL_BUCKETS // 2
    scaled = jnp.log(jnp.maximum(dist, 1).astype(F32) / max_exact) / math.log(REL_MAX_DIST / max_exact)
    large = jnp.minimum(max_exact + (scaled * (REL_BUCKETS - max_exact)).astype(I32), REL_BUCKETS - 1)
    return jnp.where(dist < max_exact, dist, large)


def moba_bias_tiles(rel_bias):
    blk = MB_BLOCK
    span = 2 * blk - 1
    x = jnp.arange(span) - (blk - 1)
    dist = jnp.maximum(jnp.arange(MB_BIAS_TILES)[:, None] * blk + x[None, :], 0)
    w = rel_bias.astype(F32).T[:, _t5_bucket(dist)]
    h = w.shape[0]
    wp = jnp.pad(w, ((0, 0), (0, 0), (0, 1)))[:, :, None, :]
    return pl.pallas_call(
        _toeplitz_kernel,
        grid=(h, MB_BIAS_TILES),
        in_specs=[pl.BlockSpec((None, None, 1, 2 * blk), lambda i, j: (i, j, 0, 0))],
        out_specs=pl.BlockSpec((None, None, blk, blk), lambda i, j: (i, j, 0, 0)),
        out_shape=jax.ShapeDtypeStruct((h, MB_BIAS_TILES, blk, blk), F32),
        compiler_params=_cparams(("parallel", "parallel")),
        name="moba_bias_tiles",
    )(wp)


def _toeplitz_kernel(w_ref, o_ref):
    blk = o_ref.shape[0]
    x = jnp.broadcast_to(w_ref[...], (blk, 2 * blk))
    o_ref[...] = pltpu.roll(x, 1, 1, stride=1, stride_axis=0)[:, blk:]


def _mix_kernel(x_ref, ya_ref, yb_ref, ga_ref, gb_ref, wa_ref, wb_ref, wo_ref, o_ref):
    za = jnp.dot(ya_ref[...], wa_ref[...], preferred_element_type=F32)
    zb = jnp.dot(yb_ref[...], wb_ref[...], preferred_element_type=F32)
    z = jax.nn.sigmoid(ga_ref[...].astype(F32)) * za + jax.nn.sigmoid(gb_ref[...].astype(F32)) * zb
    o_ref[...] = x_ref[...] + jnp.dot(z.astype(BF16), wo_ref[...], preferred_element_type=F32)


def _mem_kv_kernel(m_ref, g_ref, wk_ref, wv_ref, k_ref, v_ref):
    mn = _rms(m_ref[...], g_ref[...]).astype(BF16)
    k_ref[...] = jnp.dot(mn, wk_ref[...], preferred_element_type=F32).astype(BF16)
    v_ref[...] = jnp.dot(mn, wv_ref[...], preferred_element_type=F32).astype(BF16)


def mem_kv(mem, g, wk, wv):
    b, m, d = mem.shape
    spec = pl.BlockSpec((None, m, d), lambda i: (i, 0, 0))
    wspec = pl.BlockSpec((d, d), lambda i: (0, 0))
    return pl.pallas_call(
        _mem_kv_kernel,
        grid=(b,),
        in_specs=[spec, pl.BlockSpec((1, d), lambda i: (0, 0)), wspec, wspec],
        out_specs=[spec, spec],
        out_shape=[jax.ShapeDtypeStruct((b, m, d), BF16)] * 2,
        compiler_params=_cparams(("parallel",)),
        name="mem_kv",
    )(mem, g, wk, wv)


def _cross_kernel(x_ref, g_ref, wq_ref, k_ref, v_ref, wo_ref, o_ref):
    x = x_ref[...]
    d = x.shape[1]
    dh = d // X_HEADS
    h = _rms(x, g_ref[...]).astype(BF16)
    q = (jnp.dot(h, wq_ref[...], preferred_element_type=F32) * (dh ** -0.5)).astype(BF16)
    outs = []
    for hh in range(X_HEADS):
        sl = slice(hh * dh, (hh + 1) * dh)
        s = lax.dot_general(q[:, sl], k_ref[:, sl], (((1,), (1,)), ((), ())),
                            preferred_element_type=F32)
        p = jnp.exp(s - jnp.max(s, axis=1, keepdims=True))
        l = jnp.sum(p, axis=1, keepdims=True)
        o = jnp.dot(p.astype(BF16), v_ref[:, sl], preferred_element_type=F32) / l
        outs.append(o.astype(BF16))
    o = jnp.concatenate(outs, axis=1)
    o_ref[...] = x + jnp.dot(o, wo_ref[...], preferred_element_type=F32)


def _mix_cross_kernel(x_ref, ya_ref, yb_ref, ga_ref, gb_ref, wa_ref, wb_ref, wo_ref,
                      g_ref, wq_ref, k_ref, v_ref, wox_ref, o_ref, x1_ref):
    _mix_kernel(x_ref, ya_ref, yb_ref, ga_ref, gb_ref, wa_ref, wb_ref, wo_ref, x1_ref)
    _cross_kernel(x1_ref, g_ref, wq_ref, k_ref, v_ref, wox_ref, o_ref)


def mix_cross(x_all, ya, yb, pg, wa, wb, wo, g, wq, kx, vx, wox, row0, tok0, tm=512):
    t = yb.shape[0]
    assert t % tm == 0 and tok0 % tm == 0 and row0 % tm == 0
    d = x_all.shape[1]
    w = ya.shape[1]
    m = kx.shape[1]
    b0 = tok0 // tm
    x0 = row0 // tm
    const = lambda a: pl.BlockSpec(a.shape, lambda i: (0,) * a.ndim)
    kv = pl.BlockSpec((None, m, d), lambda i: (0, 0, 0))
    return pl.pallas_call(
        _mix_cross_kernel,
        grid=(t // tm,),
        in_specs=[
            pl.BlockSpec((tm, d), lambda i: (x0 + i, 0)),
            pl.BlockSpec((tm, w), lambda i: (i, 0)),
            pl.BlockSpec((tm, w), lambda i: (i, 0)),
            pl.BlockSpec((tm, d), lambda i: (b0 + i, 0)),
            pl.BlockSpec((tm, d), lambda i: (b0 + i, 1)),
            const(wa), const(wb), const(wo), const(g), const(wq), kv, kv, const(wox),
        ],
        out_specs=pl.BlockSpec((tm, d), lambda i: (i, 0)),
        out_shape=jax.ShapeDtypeStruct((t, d), F32),
        scratch_shapes=[pltpu.VMEM((tm, d), F32)],
        compiler_params=_cparams(("parallel",)),
        name="mix_cross",
    )(x_all, ya, yb, pg, pg, wa, wb, wo, g, wq, kx, vx, wox)


def _topk_rows(sc, k):
    n = sc.shape[0]
    io = lax.broadcasted_iota(I32, sc.shape, 0).astype(F32)
    vals, ids = [], []
    for _ in range(k):
        m = jnp.max(sc, axis=0, keepdims=True)
        ix = jnp.argmax(sc, axis=0, keepdims=True).astype(F32)
        vals.append(m)
        ids.append(ix)
        sc = jnp.where(io == ix, NEG_INF, sc)
    return jnp.concatenate(vals, axis=0), jnp.concatenate(ids, axis=0).astype(I32)


def _pack_bf16_halves(h):
    bits = lax.bitcast_convert_type(h, I32)
    r = bits + 0x7FFF + (lax.shift_right_logical(bits, 16) & 1)
    half = h.shape[1] // 2
    return lax.shift_right_logical(r[:, :half], 16) | (r[:, half:] & HI_MASK)


def _route_kernel(x_ref, g_ref, wq_ref, sk_ref, hp_ref, idx_ref, w_ref, hb_ref, it_ref, wt_ref):
    p = pl.program_id(1)

    @pl.when(p == 0)
    def _():
        h = _rms(x_ref[...], g_ref[...])
        hp_ref[...] = _pack_bf16_halves(h)
        hb_ref[...] = h.astype(BF16)

    qh = jnp.dot(hb_ref[...], wq_ref[...], preferred_element_type=F32)
    tops = []
    for c in range(2):
        seg = qh[:, c * PEER_HALF:(c + 1) * PEER_HALF]
        sc = lax.dot_general(sk_ref[c], seg, (((1,), (1,)), ((), ())),
                             precision=lax.Precision.HIGHEST, preferred_element_type=F32)
        tops.append(_topk_rows(sc, PEER_TOPK))
    (s0, i0), (s1, i1) = tops
    k = PEER_TOPK
    sub = 8
    tm = s0.shape[1]
    r8 = lax.broadcasted_iota(I32, (sub, tm), 0)
    r16 = lax.broadcasted_iota(I32, (k, tm), 0)
    cand_b = [s0[0:1] + s1, s0[1:2] + s1[:sub]]
    cidx_b = [i0[0:1] * PEER_NKEYS + i1, i0[1:2] * PEER_NKEYS + i1[:sub]]
    pos_b = [r16, k + r8]
    for a in range(2, sub):
        keep = r8 < (k // (a + 1))
        cand_b.append(jnp.where(keep, s0[a:a + 1] + s1[:sub], NEG_INF))
        cidx_b.append(i0[a:a + 1] * PEER_NKEYS + i1[:sub])
        pos_b.append(a * k + r8)
    cand_b.append(s0[sub:] + s1[0:1])
    cidx_b.append(i0[sub:] * PEER_NKEYS + i1[0:1])
    pos_b.append((sub + r8) * k)
    cand = jnp.concatenate(cand_b, axis=0)
    cidx = jnp.concatenate(cidx_b, axis=0)
    pos = jnp.concatenate(pos_b, axis=0).astype(F32)
    vals, ids = [], []
    for _ in range(k):
        m = jnp.max(cand, axis=0, keepdims=True)
        px = jnp.min(jnp.where(cand == m, pos, float(k * k)), axis=0, keepdims=True)
        hit = pos == px
        vals.append(m)
        ids.append(jnp.sum(jnp.where(hit, cidx, 0), axis=0, keepdims=True))
        cand = jnp.where(hit, NEG_INF, cand)
    sf = jnp.concatenate(vals, axis=0)
    e = jnp.exp(sf - sf[0:1])
    rows = pl.ds(pl.multiple_of(p * PEER_TOPK, PEER_TOPK), PEER_TOPK)
    wt_ref[rows, :] = e / jnp.sum(e, axis=0, keepdims=True)
    it_ref[rows, :] = jnp.concatenate(ids, axis=0)

    @pl.when(p == pl.num_programs(1) - 1)
    def _():
        idx_ref[...] = it_ref[...].T
        w_ref[...] = wt_ref[...].T


def peer_route(x2d, g, wq, sk, tok0, t, tm=1024):
    assert t % tm == 0 and tok0 % tm == 0
    d = x2d.shape[1]
    ph = sk.shape[0]
    nsel = ph * PEER_TOPK
    blk0 = tok0 // tm
    return pl.pallas_call(
        _route_kernel,
        grid=(t // tm, ph),
        in_specs=[
            pl.BlockSpec((tm, d), lambda i, p: (blk0 + i, 0)),
            pl.BlockSpec((1, d), lambda i, p: (0, 0)),
            pl.BlockSpec((d, 2 * PEER_HALF), lambda i, p: (0, p)),
            pl.BlockSpec((None, 2, PEER_NKEYS, PEER_HALF), lambda i, p: (p, 0, 0, 0)),
        ],
        out_specs=[
            pl.BlockSpec((tm, d // 2), lambda i, p: (i, 0)),
            pl.BlockSpec((tm, nsel), lambda i, p: (i, 0)),
            pl.BlockSpec((tm, nsel), lambda i, p: (i, 0)),
        ],
        out_shape=[jax.ShapeDtypeStruct((t, d // 2), I32),
                   jax.ShapeDtypeStruct((t, nsel), I32),
                   jax.ShapeDtypeStruct((t, nsel), F32)],
        scratch_shapes=[pltpu.VMEM((tm, d), BF16),
                        pltpu.VMEM((nsel, tm), I32),
                        pltpu.VMEM((nsel, tm), F32)],
        compiler_params=_cparams(("parallel", "arbitrary")),
        name="peer_route",
    )(x2d, g, wq, sk)


def _final_kernel(x_ref, y_ref, g_ref, *rest):
    o_ref = rest[-1]
    o_ref[...] = _rms(x_ref[...] + y_ref[...], g_ref[...])


def final_norm_into(out, xs, y, g, row0, total, tm=512):
    t, d = y.shape
    assert t % tm == 0 and row0 % tm == 0 and total % tm == 0
    blk0 = row0 // tm
    spec = pl.BlockSpec((tm, d), lambda i: (i, 0))
    in_specs = [spec, spec, pl.BlockSpec((1, d), lambda i: (0, 0))]
    args = [xs, y, g]
    aliases = {}
    if out is not None:
        in_specs.append(pl.BlockSpec(memory_space=pl.ANY))
        args.append(out)
        aliases = {3: 0}
    return pl.pallas_call(
        _final_kernel, grid=(t // tm,),
        in_specs=in_specs,
        out_specs=pl.BlockSpec((tm, d), lambda i: (blk0 + i, 0)),
        out_shape=jax.ShapeDtypeStruct((total, d), F32),
        input_output_aliases=aliases,
        compiler_params=_cparams(("parallel",)), name="final_norm",
    )(*args)


SC_CORES = 2
SC_SUBCORES = 16
SC_WORKERS = SC_CORES * SC_SUBCORES
SC_LANES = 16
SC_GROUP = 32


def _sc_mesh():
    return plsc.VectorSubcoreMesh(core_axis_name="c", subcore_axis_name="s")


def _sc_params():
    return pltpu.CompilerParams(needs_layout_passes=False)


def _sc_worker_id():
    return lax.axis_index("s") * SC_CORES + lax.axis_index("c")


SC_ROW_LANE = 128


def _sc_unit_off(u):
    off = u * SC_LANES
    return off if isinstance(off, int) else pl.multiple_of(off, SC_LANES)


GELU_C0 = math.sqrt(2.0 / math.pi)
GELU_C1 = 0.044715


def _gelu_tanh(x):
    z = GELU_C0 * (x + GELU_C1 * (x * x * x))
    th = 1.0 - 2.0 / (jnp.exp(2.0 * z) + 1.0)
    return 0.5 * x * (1.0 + th)


SC_PK_RING = 4
SC_PK_SUB = 4
HI_MASK = -65536


def _pack_tables_kernel(u_ref, v_ref, o_ref):
    for part, ref in enumerate((u_ref, v_ref)):
        words = _pack_bf16_halves(ref[...])
        for sub in range(SC_PK_SUB):
            o_ref[:, part * SC_PK_SUB + sub, :] = words[:, sub * SC_ROW_LANE:(sub + 1) * SC_ROW_LANE]


def pack_expert_tables(u, v, te=512):
    e, d = u.shape
    assert d == 2 * SC_PK_SUB * SC_ROW_LANE
    spec = pl.BlockSpec((te, d), lambda i: (i, 0))
    return pl.pallas_call(
        _pack_tables_kernel, grid=(e // te,), in_specs=[spec, spec],
        out_specs=pl.BlockSpec((te, 2 * SC_PK_SUB, SC_ROW_LANE), lambda i: (i, 0, 0)),
        out_shape=jax.ShapeDtypeStruct((e, 2 * SC_PK_SUB, SC_ROW_LANE), I32),
        compiler_params=_cparams(("parallel",)), name="pack_expert_tables",
    )(u, v)


def _unpack_halves(x32):
    w = plsc.bitcast(x32, I32)
    return plsc.bitcast(w << 16, F32), plsc.bitcast(w & HI_MASK, F32)


def _tree_sum(xs):
    while len(xs) > 1:
        xs = [xs[i] + xs[i + 1] for i in range(0, len(xs), 2)]
    return xs[0]


def peer_experts_pk_sc(tab_uv, idx_flat, w_flat, hp, d):
    t = hp.shape[0]
    nsel = PEER_SEL
    assert t % SC_WORKERS == 0 and d == 2 * SC_PK_SUB * SC_ROW_LANE
    tpw = t // SC_WORKERS
    g = SC_GROUP if tpw % SC_GROUP == 0 else SC_GROUP // 2
    assert tpw % g == 0
    groups = tpw // g
    heads = nsel // SC_LANES
    chunks = d // 32
    units = g * heads
    ring = SC_PK_RING
    assert units % ring == 0
    row_buf = pltpu.VMEM((SC_LANES, 2 * SC_PK_SUB, SC_ROW_LANE), I32)

    def row_words(rows, r, wc, sub0):
        per = SC_ROW_LANE // SC_LANES
        return plsc.bitcast(
            rows[r, sub0 + wc // per, pl.ds(pl.multiple_of((wc % per) * SC_LANES, SC_LANES), SC_LANES)], BF16)

    def ring_loop(n_units, start, wait, compute):
        for u in range(ring - 1):
            start(u, u)

        @pl.loop(0, n_units, step=ring)
        def _(uu):
            for b in range(ring):
                u = uu + b
                nxt = u + (ring - 1)

                @pl.when(nxt < n_units)
                def _():
                    start(nxt, (b + ring - 1) % ring)

                wait(u, b)
                compute(u, b)

    @functools.partial(
        pl.kernel, mesh=_sc_mesh(),
        out_type=jax.ShapeDtypeStruct((t, d), F32),
        scratch_types=[
            pltpu.VMEM((g * nsel,), I32),
            pltpu.VMEM((g * nsel,), F32),
            pltpu.VMEM((g, d // 2), I32),
            pltpu.VMEM((g, d), F32),
            pltpu.VMEM((SC_LANES * SC_LANES,), F32),
            [row_buf] * ring,
            [pltpu.SemaphoreType.DMA] * ring,
        ],
        compiler_params=_sc_params(),
        name="peer_experts_pk_sc",
    )
    def k(tab_hbm, idx_hbm, w_hbm, h_hbm, out_hbm, idx_v, coef_v, h_v, y_v, red_v, rows, sems):
        wid = _sc_worker_id()
        lane = lax.iota(I32, SC_LANES)

        def copy(u, slot):
            ids = idx_v.at[pl.ds(_sc_unit_off(u), SC_LANES)]
            return pltpu.make_async_copy(tab_hbm.at[ids], rows[slot], sems[slot])

        def dots(u, slot):
            tt = u // heads

            def body(cp, accs):
                out = []
                hv = [plsc.bitcast(h_v[tt, pl.ds(pl.multiple_of((2 * cp + i) * SC_LANES, SC_LANES), SC_LANES)], BF16)
                      for i in range(2)]
                for r in range(SC_LANES):
                    pr = (row_words(rows[slot], r, 2 * cp, 0) * hv[0]
                          + row_words(rows[slot], r, 2 * cp + 1, 0) * hv[1])
                    lo, hi = _unpack_halves(pr)
                    out.append(accs[r] + lo + hi)
                return tuple(out)

            accs = lax.fori_loop(0, chunks // 2, body,
                                 tuple(jnp.zeros((SC_LANES,), F32) for _ in range(SC_LANES)))
            for r in range(SC_LANES):
                red_v[pl.ds(r * SC_LANES, SC_LANES)] = accs[r]
            act = _tree_sum([plsc.load_gather(red_v, [lane * SC_LANES + j]) for j in range(SC_LANES)])
            sl = pl.ds(_sc_unit_off(u), SC_LANES)
            coef_v[sl] = coef_v[sl] * _gelu_tanh(act)

        def combine(u, slot):
            tt = u // heads
            first = (u % heads) == 0
            cb = []
            for r in range(SC_LANES):
                c = plsc.load_gather(coef_v, [jnp.full((SC_LANES,), u * SC_LANES + r, I32)])
                cb.append(plsc.pack(c, c, format=plsc.PackFormat.INTERLEAVED))

            @plsc.parallel_loop(0, chunks, unroll=2)
            def _(wc):
                lo, hi = _unpack_halves(
                    _tree_sum([cb[r] * row_words(rows[slot], r, wc, SC_PK_SUB) for r in range(SC_LANES)]))
                for half, val in ((0, lo), (1, hi)):
                    sl = pl.ds(pl.multiple_of(half * (d // 2) + wc * SC_LANES, SC_LANES), SC_LANES)
                    y_v[tt, sl] = val + jnp.where(first, 0.0, y_v[tt, sl])

        def unit(u, slot):
            dots(u, slot)
            combine(u, slot)

        @pl.loop(0, groups)
        def _(gi):
            base = wid * tpw + gi * g
            pltpu.sync_copy(idx_hbm.at[pl.ds(base * nsel, g * nsel)], idx_v)
            pltpu.sync_copy(w_hbm.at[pl.ds(base * nsel, g * nsel)], coef_v)
            pltpu.sync_copy(h_hbm.at[pl.ds(base, g)], h_v)
            ring_loop(units, lambda u, s: copy(u, s).start(), lambda u, s: copy(u, s).wait(), unit)
            pltpu.sync_copy(y_v, out_hbm.at[pl.ds(base, g)])

    return k(tab_uv, idx_flat, w_flat, hp)


def kernel(x, mem, rel_bias, ln_mix, w_in, hg_lower, hg_norm, w_up_a, w_up_b, w_out, ln_cross, ln_mem, wq_x, wk_x, wv_x, wo_x, ln_ffn, peer_query, peer_subkeys, peer_u, peer_v, ln_final):
    b, s, d = x.shape
    depth = w_in.shape[0]
    assert depth == 1, "the residual after PEER is fused into the final norm"
    assert s % MB_BLOCK == 0 and s % HG_CHUNK == 0 and s % (PEER_SLICES * SC_WORKERS * SC_GROUP) == 0
    nb = s // MB_BLOCK
    row = lambda a: a.reshape(1, -1).astype(F32)
    lb_all = jnp.cumsum(jax.nn.softmax(hg_lower.astype(F32), axis=0), axis=0)
    bias = moba_bias_tiles(rel_bias)
    n_hg = 4 * HG_WIDTH
    n_qk = 2 * MB_WIDTH
    n_mb = 3 * MB_WIDTH
    l = 0
    w = cast_bf16(w_in[l].astype(F32))
    w_vt = cast_bf16_t(w_in[l].astype(F32), n_hg + n_qk, n_mb - n_qk)
    wa, wb, wo = w_up_a[l].astype(BF16), w_up_b[l].astype(BF16), w_out[l].astype(BF16)
    wqx, wox = wq_x[l].astype(BF16), wo_x[l].astype(BF16)
    wpq, sk = peer_query[l].astype(BF16), peer_subkeys[l].astype(F32)
    tab_uv = pack_expert_tables(peer_u[l].astype(F32), peer_v[l].astype(F32))
    kx, vx = mem_kv(mem, row(ln_mem[l]), wk_x[l].astype(BF16), wv_x[l].astype(BF16))

    x_all = x.reshape(b * s, d)
    outs = []
    for bi in range(b):
        p0, pqk, km, vt, pg = in_proj(x_all, row(ln_mix[l]), w, w_vt, bi * s, s)
        hg_state = jnp.zeros((HG_HEADS, HG_D, HG_D), F32)
        km = km.reshape(1, nb, MB_WIDTH)
        sizes = [s // PEER_SLICES] * PEER_SLICES
        if bi == b - 1:
            first = SC_WORKERS * SC_GROUP // 2
            sizes = [first, first, sizes[0] - 2 * first] + sizes[1:]
        tok0 = 0
        for ts in sizes:
            ya, hg_state = hgrn2(p0, row(lb_all[l]), row(hg_norm[l]), hg_state, tok0, ts)
            yb = moba_attention(pqk, vt, km, bias, 1, s, tok0 // MB_BLOCK, ts // MB_BLOCK)
            xs = mix_cross(x_all, ya, yb, pg, wa, wb, wo, row(ln_cross[l]), wqx, kx[bi:bi + 1], vx[bi:bi + 1], wox,
                           bi * s + tok0, tok0)
            hp, eidx, wts = peer_route(xs, row(ln_ffn[l]), wpq, sk, 0, ts, math.gcd(ts, 1024))
            y = peer_experts_pk_sc(tab_uv, eidx.reshape(ts * PEER_SEL), wts.reshape(ts * PEER_SEL), hp, d)
            outs.append((xs, y, bi * s + tok0))
            tok0 += ts
    out = None
    for xs, y, row0 in sorted(outs, key=lambda e: (-(e[2] // s), e[2])):
        out = final_norm_into(out, xs, y, row(ln_final), row0, b * s)
    return out.reshape(b, s, d)
```

```python
import functools
import math

import jax
import jax.numpy as jnp
from jax import lax
from jax.experimental import pallas as pl
from jax.experimental.pallas import tpu as pltpu
from jax.experimental.pallas import tpu_sc as plsc

F32 = jnp.float32
BF16 = jnp.bfloat16
I32 = jnp.int32
EPS = 1e-6
NEG_INF = float("-inf")

HG_HEADS = 4
HG_D = 128
HG_WIDTH = HG_HEADS * HG_D
HG_CHUNK = 64
HG_SUB = 16
MB_HEADS = 8
MB_DH = 64
MB_WIDTH = MB_HEADS * MB_DH
MB_BLOCK = 256
MB_TOPK = 3
MB_BIAS_TILES = 8
REL_BUCKETS = 32
REL_MAX_DIST = 2048
X_HEADS = 4
PEER_HEADS = 8
PEER_NKEYS = 128
PEER_TOPK = 16
PEER_HALF = 128
PEER_SEL = PEER_HEADS * PEER_TOPK
PEER_SLICES = 4

VMEM_LIMIT = 56 * 1024 * 1024


def _cparams(sem):
    return pltpu.CompilerParams(dimension_semantics=sem, vmem_limit_bytes=VMEM_LIMIT)


def _rms(x, g):
    ms = jnp.mean(x * x, axis=-1, keepdims=True)
    return x * lax.rsqrt(ms + EPS) * g


def _cast_t_kernel(w_ref, o_ref):
    o_ref[...] = w_ref[...].T.astype(o_ref.dtype)


def cast_bf16_t(w, col0, n):
    k = w.shape[0]
    assert col0 % n == 0
    return pl.pallas_call(
        _cast_t_kernel, grid=(1,),
        in_specs=[pl.BlockSpec((k, n), lambda i: (0, col0 // n))],
        out_specs=pl.BlockSpec((n, k), lambda i: (0, 0)),
        out_shape=jax.ShapeDtypeStruct((n, k), BF16),
        compiler_params=_cparams(("arbitrary",)), name="cast_bf16_t",
    )(w)


def _cast_kernel(w_ref, o_ref):
    o_ref[...] = w_ref[...].astype(o_ref.dtype)


def cast_bf16(w, tn=512):
    k, n = w.shape
    assert n % tn == 0
    spec = pl.BlockSpec((k, tn), lambda j: (0, j))
    return pl.pallas_call(
        _cast_kernel, grid=(n // tn,), in_specs=[spec], out_specs=spec,
        out_shape=jax.ShapeDtypeStruct((k, n), BF16),
        compiler_params=_cparams(("parallel",)), name="cast_bf16",
    )(w)


def _in_proj_kernel(x_ref, g_ref, w0_ref, w1_ref, wvt_ref, *rest):
    w2_refs, (o0_ref, o1_ref, okm_ref, ovt_ref, o2_ref) = rest[:-5], rest[-5:]
    h = _rms(x_ref[...], g_ref[...]).astype(BF16)
    o0_ref[...] = jnp.dot(h, w0_ref[...], preferred_element_type=F32)
    qk = jnp.dot(h, w1_ref[...], preferred_element_type=F32)
    o1_ref[...] = qk.astype(BF16)
    okm_ref[0] = jnp.mean(qk[:, MB_WIDTH:], axis=0, keepdims=True)
    vt = lax.dot_general(wvt_ref[...], h, (((1,), (1,)), ((), ())), preferred_element_type=F32).astype(BF16)
    for hd in range(MB_HEADS):
        ovt_ref[0, hd * MB_VROWS:hd * MB_VROWS + MB_DH, :] = vt[hd * MB_DH:(hd + 1) * MB_DH]
        ovt_ref[0, hd * MB_VROWS + MB_DH:(hd + 1) * MB_VROWS, :] = jnp.ones((MB_ONES, vt.shape[1]), BF16)
    wg = w2_refs[0].shape[1]
    for j, w2_ref in enumerate(w2_refs):
        o2_ref[:, j * wg:(j + 1) * wg] = jnp.dot(h, w2_ref[...], preferred_element_type=F32).astype(BF16)


def in_proj(x_all, g, w, wvt, row0, t):
    d = x_all.shape[1]
    tm = MB_BLOCK
    assert t % tm == 0 and row0 % tm == 0
    blk0 = row0 // tm
    n0, n1, nv, n2 = 4 * HG_WIDTH, 2 * MB_WIDTH, MB_VT_ROWS, 2 * d
    wg = MB_WIDTH
    assert wvt.shape == (MB_WIDTH, d) and w.shape == (d, n0 + n1 + MB_WIDTH + n2)
    assert n0 % n1 == 0 and (n0 + n1 + MB_WIDTH) % wg == 0 and n2 % wg == 0
    full = lambda a: pl.BlockSpec(a.shape, lambda i: (0, 0))
    g0 = (n0 + n1 + MB_WIDTH) // wg
    w_specs = ([pl.BlockSpec((d, n0), lambda i: (0, 0)), pl.BlockSpec((d, n1), lambda i: (0, n0 // n1)), full(wvt)]
               + [pl.BlockSpec((d, wg), lambda i, j=j: (0, g0 + j)) for j in range(n2 // wg)])
    return pl.pallas_call(
        _in_proj_kernel,
        grid=(t // tm,),
        in_specs=[pl.BlockSpec((tm, d), lambda i: (blk0 + i, 0)), full(g)] + w_specs,
        out_specs=[pl.BlockSpec((tm, n0), lambda i: (i, 0)),
                   pl.BlockSpec((tm, n1), lambda i: (i, 0)),
                   pl.BlockSpec((1, 1, MB_WIDTH), lambda i: (i, 0, 0)),
                   pl.BlockSpec((1, nv, tm), lambda i: (i, 0, 0)),
                   pl.BlockSpec((tm, n2), lambda i: (i, 0))],
        out_shape=[jax.ShapeDtypeStruct((t, n0), F32),
                   jax.ShapeDtypeStruct((t, n1), BF16),
                   jax.ShapeDtypeStruct((t // tm, 1, MB_WIDTH), F32),
                   jax.ShapeDtypeStruct((t // tm, nv, tm), BF16),
                   jax.ShapeDtypeStruct((t, n2), BF16)],
        compiler_params=_cparams(("parallel",)),
        name="in_proj",
    )(x_all, g, w, w, wvt, *([w] * (n2 // wg)))


def _hgrn_kernel(q_ref, f_ref, i_ref, g_ref, lb_ref, gain_ref, st0_ref, o_ref, stn_ref, st_ref):
    c = pl.program_id(0)

    @pl.when(c == 0)
    def _():
        st_ref[...] = st0_ref[...]

    C, S = HG_CHUNK, HG_SUB
    row = lax.broadcasted_iota(I32, (C, C), 0)
    col = lax.broadcasted_iota(I32, (C, C), 1)
    tril = (row >= col).astype(F32)
    t_iota = lax.broadcasted_iota(I32, (S, 1), 0)

    for h in range(HG_HEADS):
        sl = slice(h * HG_D, (h + 1) * HG_D)
        q = q_ref[:, sl]
        v = i_ref[:, sl]
        lb = lb_ref[:, sl]
        f = lb + (1.0 - lb) * jax.nn.sigmoid(f_ref[:, sl])
        lf = jnp.log(f)
        k = 1.0 - f
        b = jnp.dot(tril, lf, precision=lax.Precision.HIGHEST, preferred_element_type=F32)
        st = st_ref[h]
        vb = v.astype(BF16)
        qd = (q * jnp.exp(b)).astype(BF16)
        o_inter = lax.dot_general(qd, st.astype(BF16), (((1,), (1,)), ((), ())),
                                  preferred_element_type=F32)
        outs = []
        for i in range(C // S):
            r0 = i * S
            qi = q[r0:r0 + S]
            ki = k[r0:r0 + S]
            bi = b[r0:r0 + S]
            vi = v[r0:r0 + S]
            oi = o_inter[r0:r0 + S]
            if i > 0:
                bs = b[r0 - 1:r0]
                qh = (qi * jnp.exp(bi - bs)).astype(BF16)
                kh = (k[:r0] * jnp.exp(bs - b[:r0])).astype(BF16)
                a = lax.dot_general(qh, kh, (((1,), (1,)), ((), ())), preferred_element_type=F32)
                oi = oi + jnp.dot(a.astype(BF16), vb[:r0], preferred_element_type=F32)
            half = S // 2
            o_half = [oi[:half], oi[half:]]
            for s in range(S):
                for hf in range(s // half, 2):
                    rows = slice(hf * half, (hf + 1) * half)
                    dec = jnp.exp(jnp.minimum(bi[rows] - bi[s:s + 1], 0.0))
                    a_s = jnp.sum(qi[rows] * ki[s:s + 1] * dec, axis=-1, keepdims=True)
                    a_s = jnp.where(t_iota[rows] >= s, a_s, 0.0)
                    o_half[hf] = o_half[hf] + a_s * vi[s:s + 1]
            outs.extend(o_half)
        o = jnp.concatenate(outs, axis=0)
        b_end = b[C - 1:C]
        kd = (k * jnp.exp(b_end - b)).astype(BF16)
        upd = lax.dot_general(vb, kd, (((0,), (0,)), ((), ())), preferred_element_type=F32)
        st_ref[h] = st * jnp.exp(b_end) + upd
        o = o * lax.rsqrt(jnp.mean(o * o, axis=-1, keepdims=True) + EPS)
        g = g_ref[:, sl]
        o_ref[:, sl] = (o * gain_ref[:, sl] * (g * jax.nn.sigmoid(g))).astype(o_ref.dtype)

    @pl.when(c == pl.num_programs(0) - 1)
    def _():
        stn_ref[...] = st_ref[...]


def hgrn2(p0, lb, gain, state, tok0, t):
    assert t % HG_CHUNK == 0 and tok0 % HG_CHUNK == 0
    nc = t // HG_CHUNK
    c0 = tok0 // HG_CHUNK
    w = HG_WIDTH

    def col(j):
        return pl.BlockSpec((HG_CHUNK, w), lambda c, j=j: (c0 + c, j))

    st_spec = pl.BlockSpec(state.shape, lambda c: (0, 0, 0))
    return pl.pallas_call(
        _hgrn_kernel,
        grid=(nc,),
        in_specs=[col(0), col(1), col(2), col(3),
                  pl.BlockSpec((1, w), lambda c: (0, 0)),
                  pl.BlockSpec((1, w), lambda c: (0, 0)),
                  st_spec],
        out_specs=[pl.BlockSpec((HG_CHUNK, w), lambda c: (c, 0)), st_spec],
        out_shape=[jax.ShapeDtypeStruct((t, w), BF16), jax.ShapeDtypeStruct(state.shape, F32)],
        scratch_shapes=[pltpu.VMEM((HG_HEADS, HG_D, HG_D), F32)],
        compiler_params=_cparams(("arbitrary",)),
        name="hgrn2",
    )(p0, p0, p0, p0, lb, gain, state)


MB_PAIR = 4
MB_PW = MB_PAIR * MB_DH
MB_LG = 128
MB_ONES = 16
MB_VROWS = MB_DH + MB_ONES
MB_VT_ROWS = MB_HEADS * MB_VROWS


def _moba_kernel(q_ref, k_ref, vt_ref, km_ref, bias_ref, o_ref, *scratch, qb0):
    m_ref, l_ref, al_ref, acc_ref, msk_ref, s_ref, p_ref = (
        scratch[i * MB_PAIR:(i + 1) * MB_PAIR] for i in range(7))
    qi = pl.program_id(2) + qb0
    nb = km_ref.shape[0]
    blk = MB_BLOCK
    heads = range(MB_PAIR)
    grp = lambda hh: slice((hh // 2) * MB_LG, (hh // 2 + 1) * MB_LG)
    q = q_ref[...]
    lane = lax.broadcasted_iota(I32, (blk, MB_LG), 1)
    in_head = [(lane < MB_DH) if hh % 2 == 0 else (lane >= MB_DH) for hh in heads]
    qs = q * jnp.asarray(MB_DH ** -0.5, BF16)
    nt = (((1,), (1,)), ((), ()))
    qf = q.astype(F32)
    qht = [jnp.where(in_head[hh], qs[:, grp(hh)].astype(F32), 0.0).T.astype(BF16) for hh in heads]

    n_io = lax.broadcasted_iota(I32, (nb, blk), 0)
    for hh in heads:
        gate = lax.dot_general(km_ref[:, grp(hh)], jnp.where(in_head[hh], qf[:, grp(hh)], 0.0), nt,
                               precision=lax.Precision.HIGHEST, preferred_element_type=F32)
        gate = jnp.where(n_io < qi, gate, NEG_INF)
        chosen = n_io < 0
        for _ in range(MB_TOPK):
            mx = jnp.max(gate, axis=0, keepdims=True)
            ix = jnp.min(jnp.where(gate == mx, n_io, nb), axis=0, keepdims=True)
            hit = n_io == ix
            chosen = chosen | (hit & (mx > NEG_INF))
            gate = jnp.where(hit, NEG_INF, gate)
        msk_ref[hh][...] = jnp.where(chosen, 0.0, NEG_INF)

    vrows = lambda hh: slice(hh * MB_VROWS, (hh + 1) * MB_VROWS)

    def pv_stage(blk_idx):
        vtb = vt_ref[blk_idx]
        r = [jnp.dot(vtb[vrows(hh)], p_ref[hh][...], preferred_element_type=F32) for hh in heads]
        al = [al_ref[hh][...] for hh in heads]
        a_new = [al[hh] * acc_ref[hh][...] + r[hh][:MB_DH] for hh in heads]
        l_new = [al[hh] * l_ref[hh][...] + r[hh][MB_DH:MB_DH + 1] for hh in heads]
        return a_new, l_new

    def store_pv(a_new, l_new):
        for hh in heads:
            acc_ref[hh][...] = a_new[hh]
            l_ref[hh][...] = l_new[hh]

    def softmax_stage():
        s = [s_ref[hh][...] for hh in heads]
        m_old = [m_ref[hh][...] for hh in heads]
        m_new = [jnp.maximum(m_old[hh], jnp.max(s[hh], axis=0, keepdims=True)) for hh in heads]
        alpha = [jnp.exp(m_old[hh] - m_new[hh]) for hh in heads]
        p = [jnp.exp((s[hh] - m_new[hh]).astype(BF16)) for hh in heads]
        return p, alpha, m_new

    def store_softmax(p, alpha, m_new):
        for hh in heads:
            p_ref[hh][...] = p[hh]
            al_ref[hh][...] = alpha[hh]
            m_ref[hh][...] = m_new[hh]

    k_own = k_ref[pl.ds(pl.multiple_of(qi * blk, blk), blk), :]
    key_io = lax.broadcasted_iota(I32, (blk, blk), 0)
    qry_io = lax.broadcasted_iota(I32, (blk, blk), 1)
    for hh in heads:
        s = jnp.dot(k_own[:, grp(hh)], qht[hh], preferred_element_type=F32) + bias_ref[hh, 0]
        s_ref[hh][...] = jnp.where(key_io <= qry_io, s, NEG_INF)
        m_ref[hh][...] = jnp.full((1, blk), NEG_INF, F32)
        l_ref[hh][...] = jnp.zeros((1, blk), F32)
        al_ref[hh][...] = jnp.ones((1, blk), F32)
        acc_ref[hh][...] = jnp.zeros((MB_DH, blk), F32)
        p_ref[hh][...] = jnp.zeros((blk, blk), BF16)

    def step(i, carry, far):
        pv = pv_stage(jnp.where(i <= 1, qi, i - 2))
        sm = softmax_stage()
        kn = k_ref[pl.ds(pl.multiple_of(i * blk, blk), blk), :]
        if far:
            row = [msk_ref[hh][pl.ds(i, 1), :] + bias_ref[hh, MB_BIAS_TILES - 1, 0:1, 0:1] for hh in heads]
            s_next = [jnp.dot(kn[:, grp(hh)], qht[hh], preferred_element_type=F32) + row[hh] for hh in heads]
        else:
            d = qi - i
            s_next = [jnp.dot(kn[:, grp(hh)], qht[hh], preferred_element_type=F32)
                      + bias_ref[hh, d] + msk_ref[hh][pl.ds(i, 1), :] for hh in heads]
        store_pv(*pv)
        for hh in heads:
            s_ref[hh][...] = s_next[hh]
        store_softmax(*sm)
        return carry

    n_far = jnp.maximum(qi - (MB_BIAS_TILES - 2), 0)
    lax.fori_loop(0, n_far, functools.partial(step, far=True), 0)
    lax.fori_loop(n_far, qi, functools.partial(step, far=False), 0)
    pv = pv_stage(jnp.where(qi <= 1, qi, qi - 2))
    sm = softmax_stage()
    store_pv(*pv)
    store_softmax(*sm)
    a_fin, l_fin = pv_stage(jnp.where(qi == 0, qi, qi - 1))
    out_t = jnp.concatenate([a_fin[hh] / l_fin[hh] for hh in heads], axis=0)
    o_ref[...] = out_t.T.astype(o_ref.dtype)


def moba_attention(pqk, vt, km, bias, batch, seq, qb0=0, nqb=None):
    nb = seq // MB_BLOCK
    nqb = nb if nqb is None else nqb
    t = batch * nqb * MB_BLOCK
    groups = MB_WIDTH // MB_PW
    nkb = qb0 + nqb if batch == 1 else nb
    return pl.pallas_call(
        functools.partial(_moba_kernel, qb0=qb0),
        grid=(batch, groups, nqb),
        in_specs=[
            pl.BlockSpec((MB_BLOCK, MB_PW), lambda b, j, i: (b * nb + qb0 + i, j)),
            pl.BlockSpec((nkb * MB_BLOCK, MB_PW), lambda b, j, i: (b, groups + j)),
            pl.BlockSpec((nkb, MB_PAIR * MB_VROWS, MB_BLOCK), lambda b, j, i: (b, j, 0)),
            pl.BlockSpec((None, nb, MB_PW), lambda b, j, i: (b, 0, j)),
            pl.BlockSpec((MB_PAIR, MB_BIAS_TILES, MB_BLOCK, MB_BLOCK), lambda b, j, i: (j, 0, 0, 0)),
        ],
        out_specs=pl.BlockSpec((MB_BLOCK, MB_PW), lambda b, j, i: (b * nqb + i, j)),
        out_shape=jax.ShapeDtypeStruct((t, MB_WIDTH), BF16),
        scratch_shapes=(
            [pltpu.VMEM((1, MB_BLOCK), F32)] * (3 * MB_PAIR)
            + [pltpu.VMEM((MB_DH, MB_BLOCK), F32)] * MB_PAIR
            + [pltpu.VMEM((nb, MB_BLOCK), F32)] * MB_PAIR
            + [pltpu.VMEM((MB_BLOCK, MB_BLOCK), F32)] * MB_PAIR
            + [pltpu.VMEM((MB_BLOCK, MB_BLOCK), BF16)] * MB_PAIR
        ),
        compiler_params=_cparams(("parallel", "parallel", "arbitrary")),
        name="moba_attn",
    )(pqk, pqk, vt, km, bias)


def _t5_bucket(dist):
    max_exact = REL_BUCKETS // 2
    scaled = jnp.log(jnp.maximum(dist, 1).astype(F32) / max_exact) / math.log(REL_MAX_DIST / max_exact)
    large = jnp.minimum(max_exact + (scaled * (REL_BUCKETS - max_exact)).astype(I32), REL_BUCKETS - 1)
    return jnp.where(dist < max_exact, dist, large)


def moba_bias_tiles(rel_bias):
    blk = MB_BLOCK
    span = 2 * blk - 1
    x = jnp.arange(span) - (blk - 1)
    dist = jnp.maximum(jnp.arange(MB_BIAS_TILES)[:, None] * blk + x[None, :], 0)
    w = rel_bias.astype(F32).T[:, _t5_bucket(dist)]
    h = w.shape[0]
    wp = jnp.pad(w, ((0, 0), (0, 0), (0, 1)))[:, :, None, :]
    return pl.pallas_call(
        _toeplitz_kernel,
        grid=(h, MB_BIAS_TILES),
        in_specs=[pl.BlockSpec((None, None, 1, 2 * blk), lambda i, j: (i, j, 0, 0))],
        out_specs=pl.BlockSpec((None, None, blk, blk), lambda i, j: (i, j, 0, 0)),
        out_shape=jax.ShapeDtypeStruct((h, MB_BIAS_TILES, blk, blk), F32),
        compiler_params=_cparams(("parallel", "parallel")),
        name="moba_bias_tiles",
    )(wp)


def _toeplitz_kernel(w_ref, o_ref):
    blk = o_ref.shape[0]
    x = jnp.broadcast_to(w_ref[...], (blk, 2 * blk))
    o_ref[...] = pltpu.roll(x, 1, 1, stride=1, stride_axis=0)[:, blk:]


def _mix_kernel(x_ref, ya_ref, yb_ref, ga_ref, gb_ref, wa_ref, wb_ref, wo_ref, o_ref):
    za = jnp.dot(ya_ref[...], wa_ref[...], preferred_element_type=F32)
    zb = jnp.dot(yb_ref[...], wb_ref[...], preferred_element_type=F32)
    z = jax.nn.sigmoid(ga_ref[...].astype(F32)) * za + jax.nn.sigmoid(gb_ref[...].astype(F32)) * zb
    o_ref[...] = x_ref[...] + jnp.dot(z.astype(BF16), wo_ref[...], preferred_element_type=F32)


def _mem_kv_kernel(m_ref, g_ref, wk_ref, wv_ref, k_ref, v_ref):
    mn = _rms(m_ref[...], g_ref[...]).astype(BF16)
    k_ref[...] = jnp.dot(mn, wk_ref[...], preferred_element_type=F32).astype(BF16)
    v_ref[...] = jnp.dot(mn, wv_ref[...], preferred_element_type=F32).astype(BF16)


def mem_kv(mem, g, wk, wv):
    b, m, d = mem.shape
    spec = pl.BlockSpec((None, m, d), lambda i: (i, 0, 0))
    wspec = pl.BlockSpec((d, d), lambda i: (0, 0))
    return pl.pallas_call(
        _mem_kv_kernel,
        grid=(b,),
        in_specs=[spec, pl.BlockSpec((1, d), lambda i: (0, 0)), wspec, wspec],
        out_specs=[spec, spec],
        out_shape=[jax.ShapeDtypeStruct((b, m, d), BF16)] * 2,
        compiler_params=_cparams(("parallel",)),
        name="mem_kv",
    )(mem, g, wk, wv)


def _cross_kernel(x_ref, g_ref, wq_ref, k_ref, v_ref, wo_ref, o_ref):
    x = x_ref[...]
    d = x.shape[1]
    dh = d // X_HEADS
    h = _rms(x, g_ref[...]).astype(BF16)
    q = (jnp.dot(h, wq_ref[...], preferred_element_type=F32) * (dh ** -0.5)).astype(BF16)
    outs = []
    for hh in range(X_HEADS):
        sl = slice(hh * dh, (hh + 1) * dh)
        s = lax.dot_general(q[:, sl], k_ref[:, sl], (((1,), (1,)), ((), ())),
                            preferred_element_type=F32)
        p = jnp.exp(s - jnp.max(s, axis=1, keepdims=True))
        l = jnp.sum(p, axis=1, keepdims=True)
        o = jnp.dot(p.astype(BF16), v_ref[:, sl], preferred_element_type=F32) / l
        outs.append(o.astype(BF16))
    o = jnp.concatenate(outs, axis=1)
    o_ref[...] = x + jnp.dot(o, wo_ref[...], preferred_element_type=F32)


def _mix_cross_kernel(x_ref, ya_ref, yb_ref, ga_ref, gb_ref, wa_ref, wb_ref, wo_ref,
                      g_ref, wq_ref, k_ref, v_ref, wox_ref, o_ref, x1_ref):
    _mix_kernel(x_ref, ya_ref, yb_ref, ga_ref, gb_ref, wa_ref, wb_ref, wo_ref, x1_ref)
    _cross_kernel(x1_ref, g_ref, wq_ref, k_ref, v_ref, wox_ref, o_ref)


def mix_cross(x_all, ya, yb, pg, wa, wb, wo, g, wq, kx, vx, wox, row0, tok0, tm=512):
    t = yb.shape[0]
    assert t % tm == 0 and tok0 % tm == 0 and row0 % tm == 0
    d = x_all.shape[1]
    w = ya.shape[1]
    m = kx.shape[1]
    b0 = tok0 // tm
    x0 = row0 // tm
    const = lambda a: pl.BlockSpec(a.shape, lambda i: (0,) * a.ndim)
    kv = pl.BlockSpec((None, m, d), lambda i: (0, 0, 0))
    return pl.pallas_call(
        _mix_cross_kernel,
        grid=(t // tm,),
        in_specs=[
            pl.BlockSpec((tm, d), lambda i: (x0 + i, 0)),
            pl.BlockSpec((tm, w), lambda i: (i, 0)),
            pl.BlockSpec((tm, w), lambda i: (i, 0)),
            pl.BlockSpec((tm, d), lambda i: (b0 + i, 0)),
            pl.BlockSpec((tm, d), lambda i: (b0 + i, 1)),
            const(wa), const(wb), const(wo), const(g), const(wq), kv, kv, const(wox),
        ],
        out_specs=pl.BlockSpec((tm, d), lambda i: (i, 0)),
        out_shape=jax.ShapeDtypeStruct((t, d), F32),
        scratch_shapes=[pltpu.VMEM((tm, d), F32)],
        compiler_params=_cparams(("parallel",)),
        name="mix_cross",
    )(x_all, ya, yb, pg, pg, wa, wb, wo, g, wq, kx, vx, wox)


def _topk_rows(sc, k):
    n = sc.shape[0]
    io = lax.broadcasted_iota(I32, sc.shape, 0).astype(F32)
    vals, ids = [], []
    for _ in range(k):
        m = jnp.max(sc, axis=0, keepdims=True)
        ix = jnp.argmax(sc, axis=0, keepdims=True).astype(F32)
        vals.append(m)
        ids.append(ix)
        sc = jnp.where(io == ix, NEG_INF, sc)
    return jnp.concatenate(vals, axis=0), jnp.concatenate(ids, axis=0).astype(I32)


def _pack_bf16_halves(h):
    bits = lax.bitcast_convert_type(h, I32)
    r = bits + 0x7FFF + (lax.shift_right_logical(bits, 16) & 1)
    half = h.shape[1] // 2
    return lax.shift_right_logical(r[:, :half], 16) | (r[:, half:] & HI_MASK)


def _route_kernel(x_ref, g_ref, wq_ref, sk_ref, hp_ref, idx_ref, w_ref, hb_ref, it_ref, wt_ref):
    p = pl.program_id(1)

    @pl.when(p == 0)
    def _():
        h = _rms(x_ref[...], g_ref[...])
        hp_ref[...] = _pack_bf16_halves(h)
        hb_ref[...] = h.astype(BF16)

    qh = jnp.dot(hb_ref[...], wq_ref[...], preferred_element_type=F32)
    tops = []
    for c in range(2):
        seg = qh[:, c * PEER_HALF:(c + 1) * PEER_HALF]
        sc = lax.dot_general(sk_ref[c], seg, (((1,), (1,)), ((), ())),
                             precision=lax.Precision.HIGHEST, preferred_element_type=F32)
        tops.append(_topk_rows(sc, PEER_TOPK))
    (s0, i0), (s1, i1) = tops
    k = PEER_TOPK
    sub = 8
    tm = s0.shape[1]
    r8 = lax.broadcasted_iota(I32, (sub, tm), 0)
    r16 = lax.broadcasted_iota(I32, (k, tm), 0)
    cand_b = [s0[0:1] + s1, s0[1:2] + s1[:sub]]
    cidx_b = [i0[0:1] * PEER_NKEYS + i1, i0[1:2] * PEER_NKEYS + i1[:sub]]
    pos_b = [r16, k + r8]
    for a in range(2, sub):
        keep = r8 < (k // (a + 1))
        cand_b.append(jnp.where(keep, s0[a:a + 1] + s1[:sub], NEG_INF))
        cidx_b.append(i0[a:a + 1] * PEER_NKEYS + i1[:sub])
        pos_b.append(a * k + r8)
    cand_b.append(s0[sub:] + s1[0:1])
    cidx_b.append(i0[sub:] * PEER_NKEYS + i1[0:1])
    pos_b.append((sub + r8) * k)
    cand = jnp.concatenate(cand_b, axis=0)
    cidx = jnp.concatenate(cidx_b, axis=0)
    pos = jnp.concatenate(pos_b, axis=0).astype(F32)
    vals, ids = [], []
    for _ in range(k):
        m = jnp.max(cand, axis=0, keepdims=True)
        px = jnp.min(jnp.where(cand == m, pos, float(k * k)), axis=0, keepdims=True)
        hit = pos == px
        vals.append(m)
        ids.append(jnp.sum(jnp.where(hit, cidx, 0), axis=0, keepdims=True))
        cand = jnp.where(hit, NEG_INF, cand)
    sf = jnp.concatenate(vals, axis=0)
    e = jnp.exp(sf - sf[0:1])
    rows = pl.ds(pl.multiple_of(p * PEER_TOPK, PEER_TOPK), PEER_TOPK)
    wt_ref[rows, :] = e / jnp.sum(e, axis=0, keepdims=True)
    it_ref[rows, :] = jnp.concatenate(ids, axis=0)

    @pl.when(p == pl.num_programs(1) - 1)
    def _():
        idx_ref[...] = it_ref[...].T
        w_ref[...] = wt_ref[...].T


def peer_route(x2d, g, wq, sk, tok0, t, tm=1024):
    assert t % tm == 0 and tok0 % tm == 0
    d = x2d.shape[1]
    ph = sk.shape[0]
    nsel = ph * PEER_TOPK
    blk0 = tok0 // tm
    return pl.pallas_call(
        _route_kernel,
        grid=(t // tm, ph),
        in_specs=[
            pl.BlockSpec((tm, d), lambda i, p: (blk0 + i, 0)),
            pl.BlockSpec((1, d), lambda i, p: (0, 0)),
            pl.BlockSpec((d, 2 * PEER_HALF), lambda i, p: (0, p)),
            pl.BlockSpec((None, 2, PEER_NKEYS, PEER_HALF), lambda i, p: (p, 0, 0, 0)),
        ],
        out_specs=[
            pl.BlockSpec((tm, d // 2), lambda i, p: (i, 0)),
            pl.BlockSpec((tm, nsel), lambda i, p: (i, 0)),
            pl.BlockSpec((tm, nsel), lambda i, p: (i, 0)),
        ],
        out_shape=[jax.ShapeDtypeStruct((t, d // 2), I32),
                   jax.ShapeDtypeStruct((t, nsel), I32),
                   jax.ShapeDtypeStruct((t, nsel), F32)],
        scratch_shapes=[pltpu.VMEM((tm, d), BF16),
                        pltpu.VMEM((nsel, tm), I32),
                        pltpu.VMEM((nsel, tm), F32)],
        compiler_params=_cparams(("parallel", "arbitrary")),
        name="peer_route",
    )(x2d, g, wq, sk)


def _final_kernel(x_ref, y_ref, g_ref, *rest):
    o_ref = rest[-1]
    o_ref[...] = _rms(x_ref[...] + y_ref[...], g_ref[...])


def final_norm_into(out, xs, y, g, row0, total, tm=512):
    t, d = y.shape
    assert t % tm == 0 and row0 % tm == 0 and total % tm == 0
    blk0 = row0 // tm
    spec = pl.BlockSpec((tm, d), lambda i: (i, 0))
    in_specs = [spec, spec, pl.BlockSpec((1, d), lambda i: (0, 0))]
    args = [xs, y, g]
    aliases = {}
    if out is not None:
        in_specs.append(pl.BlockSpec(memory_space=pl.ANY))
        args.append(out)
        aliases = {3: 0}
    return pl.pallas_call(
        _final_kernel, grid=(t // tm,),
        in_specs=in_specs,
        out_specs=pl.BlockSpec((tm, d), lambda i: (blk0 + i, 0)),
        out_shape=jax.ShapeDtypeStruct((total, d), F32),
        input_output_aliases=aliases,
        compiler_params=_cparams(("parallel",)), name="final_norm",
    )(*args)


SC_CORES = 2
SC_SUBCORES = 16
SC_WORKERS = SC_CORES * SC_SUBCORES
SC_LANES = 16
SC_GROUP = 32


def _sc_mesh():
    return plsc.VectorSubcoreMesh(core_axis_name="c", subcore_axis_name="s")


def _sc_params():
    return pltpu.CompilerParams(needs_layout_passes=False)


def _sc_worker_id():
    return lax.axis_index("s") * SC_CORES + lax.axis_index("c")


SC_ROW_LANE = 128


def _sc_unit_off(u):
    off = u * SC_LANES
    return off if isinstance(off, int) else pl.multiple_of(off, SC_LANES)


GELU_C0 = math.sqrt(2.0 / math.pi)
GELU_C1 = 0.044715


def _gelu_tanh(x):
    z = GELU_C0 * (x + GELU_C1 * (x * x * x))
    th = 1.0 - 2.0 / (jnp.exp(2.0 * z) + 1.0)
    return 0.5 * x * (1.0 + th)


SC_PK_RING = 4
SC_PK_SUB = 4
HI_MASK = -65536


def _pack_tables_kernel(u_ref, v_ref, o_ref):
    for part, ref in enumerate((u_ref, v_ref)):
        words = _pack_bf16_halves(ref[...])
        for sub in range(SC_PK_SUB):
            o_ref[:, part * SC_PK_SUB + sub, :] = words[:, sub * SC_ROW_LANE:(sub + 1) * SC_ROW_LANE]


def pack_expert_tables(u, v, te=512):
    e, d = u.shape
    assert d == 2 * SC_PK_SUB * SC_ROW_LANE
    spec = pl.BlockSpec((te, d), lambda i: (i, 0))
    return pl.pallas_call(
        _pack_tables_kernel, grid=(e // te,), in_specs=[spec, spec],
        out_specs=pl.BlockSpec((te, 2 * SC_PK_SUB, SC_ROW_LANE), lambda i: (i, 0, 0)),
        out_shape=jax.ShapeDtypeStruct((e, 2 * SC_PK_SUB, SC_ROW_LANE), I32),
        compiler_params=_cparams(("parallel",)), name="pack_expert_tables",
    )(u, v)


def _unpack_halves(x32):
    w = plsc.bitcast(x32, I32)
    return plsc.bitcast(w << 16, F32), plsc.bitcast(w & HI_MASK, F32)


def _tree_sum(xs):
    while len(xs) > 1:
        xs = [xs[i] + xs[i + 1] for i in range(0, len(xs), 2)]
    return xs[0]


def peer_experts_pk_sc(tab_uv, idx_flat, w_flat, hp, d):
    t = hp.shape[0]
    nsel = PEER_SEL
    assert t % SC_WORKERS == 0 and d == 2 * SC_PK_SUB * SC_ROW_LANE
    tpw = t // SC_WORKERS
    g = SC_GROUP if tpw % SC_GROUP == 0 else SC_GROUP // 2
    assert tpw % g == 0
    groups = tpw // g
    heads = nsel // SC_LANES
    chunks = d // 32
    units = g * heads
    ring = SC_PK_RING
    assert units % ring == 0
    row_buf = pltpu.VMEM((SC_LANES, 2 * SC_PK_SUB, SC_ROW_LANE), I32)

    def row_words(rows, r, wc, sub0):
        per = SC_ROW_LANE // SC_LANES
        return plsc.bitcast(
            rows[r, sub0 + wc // per, pl.ds(pl.multiple_of((wc % per) * SC_LANES, SC_LANES), SC_LANES)], BF16)

    def ring_loop(n_units, start, wait, compute):
        for u in range(ring - 1):
            start(u, u)

        @pl.loop(0, n_units, step=ring)
        def _(uu):
            for b in range(ring):
                u = uu + b
                nxt = u + (ring - 1)

                @pl.when(nxt < n_units)
                def _():
                    start(nxt, (b + ring - 1) % ring)

                wait(u, b)
                compute(u, b)

    @functools.partial(
        pl.kernel, mesh=_sc_mesh(),
        out_type=jax.ShapeDtypeStruct((t, d), F32),
        scratch_types=[
            pltpu.VMEM((g * nsel,), I32),
            pltpu.VMEM((g * nsel,), F32),
            pltpu.VMEM((g, d // 2), I32),
            pltpu.VMEM((g, d), F32),
            pltpu.VMEM((SC_LANES * SC_LANES,), F32),
            [row_buf] * ring,
            [pltpu.SemaphoreType.DMA] * ring,
        ],
        compiler_params=_sc_params(),
        name="peer_experts_pk_sc",
    )
    def k(tab_hbm, idx_hbm, w_hbm, h_hbm, out_hbm, idx_v, coef_v, h_v, y_v, red_v, rows, sems):
        wid = _sc_worker_id()
        lane = lax.iota(I32, SC_LANES)

        def copy(u, slot):
            ids = idx_v.at[pl.ds(_sc_unit_off(u), SC_LANES)]
            return pltpu.make_async_copy(tab_hbm.at[ids], rows[slot], sems[slot])

        def dots(u, slot):
            tt = u // heads

            def body(cp, accs):
                out = []
                hv = [plsc.bitcast(h_v[tt, pl.ds(pl.multiple_of((2 * cp + i) * SC_LANES, SC_LANES), SC_LANES)], BF16)
                      for i in range(2)]
                for r in range(SC_LANES):
                    pr = (row_words(rows[slot], r, 2 * cp, 0) * hv[0]
                          + row_words(rows[slot], r, 2 * cp + 1, 0) * hv[1])
                    lo, hi = _unpack_halves(pr)
                    out.append(accs[r] + lo + hi)
                return tuple(out)

            accs = lax.fori_loop(0, chunks // 2, body,
                                 tuple(jnp.zeros((SC_LANES,), F32) for _ in range(SC_LANES)))
            for r in range(SC_LANES):
                red_v[pl.ds(r * SC_LANES, SC_LANES)] = accs[r]
            act = _tree_sum([plsc.load_gather(red_v, [lane * SC_LANES + j]) for j in range(SC_LANES)])
            sl = pl.ds(_sc_unit_off(u), SC_LANES)
            coef_v[sl] = coef_v[sl] * _gelu_tanh(act)

        def combine(u, slot):
            tt = u // heads
            first = (u % heads) == 0
            cb = []
            for r in range(SC_LANES):
                c = plsc.load_gather(coef_v, [jnp.full((SC_LANES,), u * SC_LANES + r, I32)])
                cb.append(plsc.pack(c, c, format=plsc.PackFormat.INTERLEAVED))

            @plsc.parallel_loop(0, chunks, unroll=2)
            def _(wc):
                lo, hi = _unpack_halves(
                    _tree_sum([cb[r] * row_words(rows[slot], r, wc, SC_PK_SUB) for r in range(SC_LANES)]))
                for half, val in ((0, lo), (1, hi)):
                    sl = pl.ds(pl.multiple_of(half * (d // 2) + wc * SC_LANES, SC_LANES), SC_LANES)
                    y_v[tt, sl] = val + jnp.where(first, 0.0, y_v[tt, sl])

        def unit(u, slot):
            dots(u, slot)
            combine(u, slot)

        @pl.loop(0, groups)
        def _(gi):
            base = wid * tpw + gi * g
            pltpu.sync_copy(idx_hbm.at[pl.ds(base * nsel, g * nsel)], idx_v)
            pltpu.sync_copy(w_hbm.at[pl.ds(base * nsel, g * nsel)], coef_v)
            pltpu.sync_copy(h_hbm.at[pl.ds(base, g)], h_v)
            ring_loop(units, lambda u, s: copy(u, s).start(), lambda u, s: copy(u, s).wait(), unit)
            pltpu.sync_copy(y_v, out_hbm.at[pl.ds(base, g)])

    return k(tab_uv, idx_flat, w_flat, hp)


def kernel(x, mem, rel_bias, ln_mix, w_in, hg_lower, hg_norm, w_up_a, w_up_b, w_out, ln_cross, ln_mem, wq_x, wk_x, wv_x, wo_x, ln_ffn, peer_query, peer_subkeys, peer_u, peer_v, ln_final):
    b, s, d = x.shape
    depth = w_in.shape[0]
    assert depth == 1, "the residual after PEER is fused into the final norm"
    assert s % MB_BLOCK == 0 and s % HG_CHUNK == 0 and s % (PEER_SLICES * SC_WORKERS * SC_GROUP) == 0
    nb = s // MB_BLOCK
    row = lambda a: a.reshape(1, -1).astype(F32)
    lb_all = jnp.cumsum(jax.nn.softmax(hg_lower.astype(F32), axis=0), axis=0)
    bias = moba_bias_tiles(rel_bias)
    n_hg = 4 * HG_WIDTH
    n_qk = 2 * MB_WIDTH
    n_mb = 3 * MB_WIDTH
    l = 0
    w = cast_bf16(w_in[l].astype(F32))
    w_vt = cast_bf16_t(w_in[l].astype(F32), n_hg + n_qk, n_mb - n_qk)
    wa, wb, wo = w_up_a[l].astype(BF16), w_up_b[l].astype(BF16), w_out[l].astype(BF16)
    wqx, wox = wq_x[l].astype(BF16), wo_x[l].astype(BF16)
    wpq, sk = peer_query[l].astype(BF16), peer_subkeys[l].astype(F32)
    tab_uv = pack_expert_tables(peer_u[l].astype(F32), peer_v[l].astype(F32))
    kx, vx = mem_kv(mem, row(ln_mem[l]), wk_x[l].astype(BF16), wv_x[l].astype(BF16))

    x_all = x.reshape(b * s, d)
    outs = []
    for bi in range(b):
        p0, pqk, km, vt, pg = in_proj(x_all, row(ln_mix[l]), w, w_vt, bi * s, s)
        hg_state = jnp.zeros((HG_HEADS, HG_D, HG_D), F32)
        km = km.reshape(1, nb, MB_WIDTH)
        sizes = [s // PEER_SLICES] * PEER_SLICES
        if bi == b - 1:
            first = SC_WORKERS * SC_GROUP // 2
            sizes = [first, sizes[0] - first] + sizes[1:]
        tok0 = 0
        for ts in sizes:
            ya, hg_state = hgrn2(p0, row(lb_all[l]), row(hg_norm[l]), hg_state, tok0, ts)
            yb = moba_attention(pqk, vt, km, bias, 1, s, tok0 // MB_BLOCK, ts // MB_BLOCK)
            xs = mix_cross(x_all, ya, yb, pg, wa, wb, wo, row(ln_cross[l]), wqx, kx[bi:bi + 1], vx[bi:bi + 1], wox,
                           bi * s + tok0, tok0)
            hp, eidx, wts = peer_route(xs, row(ln_ffn[l]), wpq, sk, 0, ts, math.gcd(ts, 1024))
            y = peer_experts_pk_sc(tab_uv, eidx.reshape(ts * PEER_SEL), wts.reshape(ts * PEER_SEL), hp, d)
            outs.append((xs, y, bi * s + tok0))
            tok0 += ts
    out = None
    for xs, y, row0 in sorted(outs, key=lambda e: (-(e[2] // s), e[2])):
        out = final_norm_into(out, xs, y, row(ln_final), row0, b * s)
    return out.reshape(b, s, d)
```

```python
import functools
import math

import jax
import jax.numpy as jnp
from jax import lax
from jax.experimental import pallas as pl
from jax.experimental.pallas import tpu as pltpu
from jax.experimental.pallas import tpu_sc as plsc

F32 = jnp.float32
BF16 = jnp.bfloat16
I32 = jnp.int32
EPS = 1e-6
NEG_INF = float("-inf")

HG_HEADS = 4
HG_D = 128
HG_WIDTH = HG_HEADS * HG_D
HG_CHUNK = 64
HG_SUB = 16
MB_HEADS = 8
MB_DH = 64
MB_WIDTH = MB_HEADS * MB_DH
MB_BLOCK = 256
MB_TOPK = 3
MB_BIAS_TILES = 8
REL_BUCKETS = 32
REL_MAX_DIST = 2048
X_HEADS = 4
PEER_HEADS = 8
PEER_NKEYS = 128
PEER_TOPK = 16
PEER_HALF = 128
PEER_SEL = PEER_HEADS * PEER_TOPK
PEER_SLICES = 4

VMEM_LIMIT = 56 * 1024 * 1024


def _cparams(sem):
    return pltpu.CompilerParams(dimension_semantics=sem, vmem_limit_bytes=VMEM_LIMIT)


def _rms(x, g):
    ms = jnp.mean(x * x, axis=-1, keepdims=True)
    return x * lax.rsqrt(ms + EPS) * g


def _cast_t_kernel(w_ref, o_ref):
    o_ref[...] = w_ref[...].T.astype(o_ref.dtype)


def cast_bf16_t(w, col0, n):
    k = w.shape[0]
    assert col0 % n == 0
    return pl.pallas_call(
        _cast_t_kernel, grid=(1,),
        in_specs=[pl.BlockSpec((k, n), lambda i: (0, col0 // n))],
        out_specs=pl.BlockSpec((n, k), lambda i: (0, 0)),
        out_shape=jax.ShapeDtypeStruct((n, k), BF16),
        compiler_params=_cparams(("arbitrary",)), name="cast_bf16_t",
    )(w)


def _cast_kernel(w_ref, o_ref):
    o_ref[...] = w_ref[...].astype(o_ref.dtype)


def cast_bf16(w, tn=512):
    k, n = w.shape
    assert n % tn == 0
    spec = pl.BlockSpec((k, tn), lambda j: (0, j))
    return pl.pallas_call(
        _cast_kernel, grid=(n // tn,), in_specs=[spec], out_specs=spec,
        out_shape=jax.ShapeDtypeStruct((k, n), BF16),
        compiler_params=_cparams(("parallel",)), name="cast_bf16",
    )(w)


def _in_proj_kernel(x_ref, g_ref, w0_ref, w1_ref, wvt_ref, o0_ref, o1_ref, okm_ref, ovt_ref):
    h = _rms(x_ref[...], g_ref[...]).astype(BF16)
    o0_ref[...] = jnp.dot(h, w0_ref[...], preferred_element_type=F32)
    qk = jnp.dot(h, w1_ref[...], preferred_element_type=F32)
    o1_ref[...] = qk.astype(BF16)
    okm_ref[0] = jnp.mean(qk[:, MB_WIDTH:], axis=0, keepdims=True)
    vt = lax.dot_general(wvt_ref[...], h, (((1,), (1,)), ((), ())), preferred_element_type=F32).astype(BF16)
    for hd in range(MB_HEADS):
        ovt_ref[0, hd * MB_VROWS:hd * MB_VROWS + MB_DH, :] = vt[hd * MB_DH:(hd + 1) * MB_DH]
        ovt_ref[0, hd * MB_VROWS + MB_DH:(hd + 1) * MB_VROWS, :] = jnp.ones((MB_ONES, vt.shape[1]), BF16)


def in_proj(x_all, g, w, wvt, row0, t):
    d = x_all.shape[1]
    tm = MB_BLOCK
    assert t % tm == 0 and row0 % tm == 0
    blk0 = row0 // tm
    n0, n1, nv = 4 * HG_WIDTH, 2 * MB_WIDTH, MB_VT_ROWS
    assert wvt.shape == (MB_WIDTH, d) and w.shape == (d, n0 + n1 + MB_WIDTH + 2 * d)
    assert n0 % n1 == 0
    full = lambda a: pl.BlockSpec(a.shape, lambda i: (0, 0))
    w_specs = [pl.BlockSpec((d, n0), lambda i: (0, 0)), pl.BlockSpec((d, n1), lambda i: (0, n0 // n1)), full(wvt)]
    return pl.pallas_call(
        _in_proj_kernel,
        grid=(t // tm,),
        in_specs=[pl.BlockSpec((tm, d), lambda i: (blk0 + i, 0)), full(g)] + w_specs,
        out_specs=[pl.BlockSpec((tm, n0), lambda i: (i, 0)),
                   pl.BlockSpec((tm, n1), lambda i: (i, 0)),
                   pl.BlockSpec((1, 1, MB_WIDTH), lambda i: (i, 0, 0)),
                   pl.BlockSpec((1, nv, tm), lambda i: (i, 0, 0))],
        out_shape=[jax.ShapeDtypeStruct((t, n0), F32),
                   jax.ShapeDtypeStruct((t, n1), BF16),
                   jax.ShapeDtypeStruct((t // tm, 1, MB_WIDTH), F32),
                   jax.ShapeDtypeStruct((t // tm, nv, tm), BF16)],
        compiler_params=_cparams(("parallel",)),
        name="in_proj",
    )(x_all, g, w, w, wvt)


def _hgrn_kernel(q_ref, f_ref, i_ref, g_ref, lb_ref, gain_ref, st0_ref, o_ref, stn_ref, st_ref):
    c = pl.program_id(0)

    @pl.when(c == 0)
    def _():
        st_ref[...] = st0_ref[...]

    C, S = HG_CHUNK, HG_SUB
    row = lax.broadcasted_iota(I32, (C, C), 0)
    col = lax.broadcasted_iota(I32, (C, C), 1)
    tril = (row >= col).astype(F32)
    t_iota = lax.broadcasted_iota(I32, (S, 1), 0)

    for h in range(HG_HEADS):
        sl = slice(h * HG_D, (h + 1) * HG_D)
        q = q_ref[:, sl]
        v = i_ref[:, sl]
        lb = lb_ref[:, sl]
        f = lb + (1.0 - lb) * jax.nn.sigmoid(f_ref[:, sl])
        lf = jnp.log(f)
        k = 1.0 - f
        b = jnp.dot(tril, lf, precision=lax.Precision.HIGHEST, preferred_element_type=F32)
        st = st_ref[h]
        vb = v.astype(BF16)
        qd = (q * jnp.exp(b)).astype(BF16)
        o_inter = lax.dot_general(qd, st.astype(BF16), (((1,), (1,)), ((), ())),
                                  preferred_element_type=F32)
        outs = []
        for i in range(C // S):
            r0 = i * S
            qi = q[r0:r0 + S]
            ki = k[r0:r0 + S]
            bi = b[r0:r0 + S]
            vi = v[r0:r0 + S]
            oi = o_inter[r0:r0 + S]
            if i > 0:
                bs = b[r0 - 1:r0]
                qh = (qi * jnp.exp(bi - bs)).astype(BF16)
                kh = (k[:r0] * jnp.exp(bs - b[:r0])).astype(BF16)
                a = lax.dot_general(qh, kh, (((1,), (1,)), ((), ())), preferred_element_type=F32)
                oi = oi + jnp.dot(a.astype(BF16), vb[:r0], preferred_element_type=F32)
            half = S // 2
            o_half = [oi[:half], oi[half:]]
            for s in range(S):
                for hf in range(s // half, 2):
                    rows = slice(hf * half, (hf + 1) * half)
                    dec = jnp.exp(jnp.minimum(bi[rows] - bi[s:s + 1], 0.0))
                    a_s = jnp.sum(qi[rows] * ki[s:s + 1] * dec, axis=-1, keepdims=True)
                    a_s = jnp.where(t_iota[rows] >= s, a_s, 0.0)
                    o_half[hf] = o_half[hf] + a_s * vi[s:s + 1]
            outs.extend(o_half)
        o = jnp.concatenate(outs, axis=0)
        b_end = b[C - 1:C]
        kd = (k * jnp.exp(b_end - b)).astype(BF16)
        upd = lax.dot_general(vb, kd, (((0,), (0,)), ((), ())), preferred_element_type=F32)
        st_ref[h] = st * jnp.exp(b_end) + upd
        o = o * lax.rsqrt(jnp.mean(o * o, axis=-1, keepdims=True) + EPS)
        g = g_ref[:, sl]
        o_ref[:, sl] = (o * gain_ref[:, sl] * (g * jax.nn.sigmoid(g))).astype(o_ref.dtype)

    @pl.when(c == pl.num_programs(0) - 1)
    def _():
        stn_ref[...] = st_ref[...]


def hgrn2(p0, lb, gain, state, tok0, t):
    assert t % HG_CHUNK == 0 and tok0 % HG_CHUNK == 0
    nc = t // HG_CHUNK
    c0 = tok0 // HG_CHUNK
    w = HG_WIDTH

    def col(j):
        return pl.BlockSpec((HG_CHUNK, w), lambda c, j=j: (c0 + c, j))

    st_spec = pl.BlockSpec(state.shape, lambda c: (0, 0, 0))
    return pl.pallas_call(
        _hgrn_kernel,
        grid=(nc,),
        in_specs=[col(0), col(1), col(2), col(3),
                  pl.BlockSpec((1, w), lambda c: (0, 0)),
                  pl.BlockSpec((1, w), lambda c: (0, 0)),
                  st_spec],
        out_specs=[pl.BlockSpec((HG_CHUNK, w), lambda c: (c, 0)), st_spec],
        out_shape=[jax.ShapeDtypeStruct((t, w), BF16), jax.ShapeDtypeStruct(state.shape, F32)],
        scratch_shapes=[pltpu.VMEM((HG_HEADS, HG_D, HG_D), F32)],
        compiler_params=_cparams(("arbitrary",)),
        name="hgrn2",
    )(p0, p0, p0, p0, lb, gain, state)


MB_PAIR = 4
MB_PW = MB_PAIR * MB_DH
MB_LG = 128
MB_ONES = 16
MB_VROWS = MB_DH + MB_ONES
MB_VT_ROWS = MB_HEADS * MB_VROWS


def _moba_kernel(q_ref, k_ref, vt_ref, km_ref, bias_ref, o_ref, *scratch, qb0):
    m_ref, l_ref, al_ref, acc_ref, msk_ref, s_ref, p_ref = (
        scratch[i * MB_PAIR:(i + 1) * MB_PAIR] for i in range(7))
    qi = pl.program_id(2) + qb0
    nb = km_ref.shape[0]
    blk = MB_BLOCK
    heads = range(MB_PAIR)
    grp = lambda hh: slice((hh // 2) * MB_LG, (hh // 2 + 1) * MB_LG)
    q = q_ref[...]
    lane = lax.broadcasted_iota(I32, (blk, MB_LG), 1)
    in_head = [(lane < MB_DH) if hh % 2 == 0 else (lane >= MB_DH) for hh in heads]
    qs = q * jnp.asarray(MB_DH ** -0.5, BF16)
    nt = (((1,), (1,)), ((), ()))
    qf = q.astype(F32)
    qht = [jnp.where(in_head[hh], qs[:, grp(hh)].astype(F32), 0.0).T.astype(BF16) for hh in heads]

    n_io = lax.broadcasted_iota(I32, (nb, blk), 0)
    for hh in heads:
        gate = lax.dot_general(km_ref[:, grp(hh)], jnp.where(in_head[hh], qf[:, grp(hh)], 0.0), nt,
                               precision=lax.Precision.HIGHEST, preferred_element_type=F32)
        gate = jnp.where(n_io < qi, gate, NEG_INF)
        chosen = n_io < 0
        for _ in range(MB_TOPK):
            mx = jnp.max(gate, axis=0, keepdims=True)
            ix = jnp.min(jnp.where(gate == mx, n_io, nb), axis=0, keepdims=True)
            hit = n_io == ix
            chosen = chosen | (hit & (mx > NEG_INF))
            gate = jnp.where(hit, NEG_INF, gate)
        msk_ref[hh][...] = jnp.where(chosen, 0.0, NEG_INF)

    vrows = lambda hh: slice(hh * MB_VROWS, (hh + 1) * MB_VROWS)

    def pv_stage(blk_idx):
        vtb = vt_ref[blk_idx]
        r = [jnp.dot(vtb[vrows(hh)], p_ref[hh][...], preferred_element_type=F32) for hh in heads]
        al = [al_ref[hh][...] for hh in heads]
        a_new = [al[hh] * acc_ref[hh][...] + r[hh][:MB_DH] for hh in heads]
        l_new = [al[hh] * l_ref[hh][...] + r[hh][MB_DH:MB_DH + 1] for hh in heads]
        return a_new, l_new

    def store_pv(a_new, l_new):
        for hh in heads:
            acc_ref[hh][...] = a_new[hh]
            l_ref[hh][...] = l_new[hh]

    def softmax_stage():
        s = [s_ref[hh][...] for hh in heads]
        m_old = [m_ref[hh][...] for hh in heads]
        m_new = [jnp.maximum(m_old[hh], jnp.max(s[hh], axis=0, keepdims=True)) for hh in heads]
        alpha = [jnp.exp(m_old[hh] - m_new[hh]) for hh in heads]
        p = [jnp.exp((s[hh] - m_new[hh]).astype(BF16)) for hh in heads]
        return p, alpha, m_new

    def store_softmax(p, alpha, m_new):
        for hh in heads:
            p_ref[hh][...] = p[hh]
            al_ref[hh][...] = alpha[hh]
            m_ref[hh][...] = m_new[hh]

    k_own = k_ref[pl.ds(pl.multiple_of(qi * blk, blk), blk), :]
    key_io = lax.broadcasted_iota(I32, (blk, blk), 0)
    qry_io = lax.broadcasted_iota(I32, (blk, blk), 1)
    for hh in heads:
        s = jnp.dot(k_own[:, grp(hh)], qht[hh], preferred_element_type=F32) + bias_ref[hh, 0]
        s_ref[hh][...] = jnp.where(key_io <= qry_io, s, NEG_INF)
        m_ref[hh][...] = jnp.full((1, blk), NEG_INF, F32)
        l_ref[hh][...] = jnp.zeros((1, blk), F32)
        al_ref[hh][...] = jnp.ones((1, blk), F32)
        acc_ref[hh][...] = jnp.zeros((MB_DH, blk), F32)
        p_ref[hh][...] = jnp.zeros((blk, blk), BF16)

    def step(i, carry, far):
        pv = pv_stage(jnp.where(i <= 1, qi, i - 2))
        sm = softmax_stage()
        kn = k_ref[pl.ds(pl.multiple_of(i * blk, blk), blk), :]
        if far:
            row = [msk_ref[hh][pl.ds(i, 1), :] + bias_ref[hh, MB_BIAS_TILES - 1, 0:1, 0:1] for hh in heads]
            s_next = [jnp.dot(kn[:, grp(hh)], qht[hh], preferred_element_type=F32) + row[hh] for hh in heads]
        else:
            d = qi - i
            s_next = [jnp.dot(kn[:, grp(hh)], qht[hh], preferred_element_type=F32)
                      + bias_ref[hh, d] + msk_ref[hh][pl.ds(i, 1), :] for hh in heads]
        store_pv(*pv)
        for hh in heads:
            s_ref[hh][...] = s_next[hh]
        store_softmax(*sm)
        return carry

    n_far = jnp.maximum(qi - (MB_BIAS_TILES - 2), 0)
    lax.fori_loop(0, n_far, functools.partial(step, far=True), 0)
    lax.fori_loop(n_far, qi, functools.partial(step, far=False), 0)
    pv = pv_stage(jnp.where(qi <= 1, qi, qi - 2))
    sm = softmax_stage()
    store_pv(*pv)
    store_softmax(*sm)
    a_fin, l_fin = pv_stage(jnp.where(qi == 0, qi, qi - 1))
    out_t = jnp.concatenate([a_fin[hh] / l_fin[hh] for hh in heads], axis=0)
    o_ref[...] = out_t.T.astype(o_ref.dtype)


def moba_attention(pqk, vt, km, bias, batch, seq, qb0=0, nqb=None):
    nb = seq // MB_BLOCK
    nqb = nb if nqb is None else nqb
    t = batch * nqb * MB_BLOCK
    groups = MB_WIDTH // MB_PW
    nkb = qb0 + nqb if batch == 1 else nb
    return pl.pallas_call(
        functools.partial(_moba_kernel, qb0=qb0),
        grid=(batch, groups, nqb),
        in_specs=[
            pl.BlockSpec((MB_BLOCK, MB_PW), lambda b, j, i: (b * nb + qb0 + i, j)),
            pl.BlockSpec((nkb * MB_BLOCK, MB_PW), lambda b, j, i: (b, groups + j)),
            pl.BlockSpec((nkb, MB_PAIR * MB_VROWS, MB_BLOCK), lambda b, j, i: (b, j, 0)),
            pl.BlockSpec((None, nb, MB_PW), lambda b, j, i: (b, 0, j)),
            pl.BlockSpec((MB_PAIR, MB_BIAS_TILES, MB_BLOCK, MB_BLOCK), lambda b, j, i: (j, 0, 0, 0)),
        ],
        out_specs=pl.BlockSpec((MB_BLOCK, MB_PW), lambda b, j, i: (b * nqb + i, j)),
        out_shape=jax.ShapeDtypeStruct((t, MB_WIDTH), BF16),
        scratch_shapes=(
            [pltpu.VMEM((1, MB_BLOCK), F32)] * (3 * MB_PAIR)
            + [pltpu.VMEM((MB_DH, MB_BLOCK), F32)] * MB_PAIR
            + [pltpu.VMEM((nb, MB_BLOCK), F32)] * MB_PAIR
            + [pltpu.VMEM((MB_BLOCK, MB_BLOCK), F32)] * MB_PAIR
            + [pltpu.VMEM((MB_BLOCK, MB_BLOCK), BF16)] * MB_PAIR
        ),
        compiler_params=_cparams(("parallel", "parallel", "arbitrary")),
        name="moba_attn",
    )(pqk, pqk, vt, km, bias)


def _t5_bucket(dist):
    max_exact = REL_BUCKETS // 2
    scaled = jnp.log(jnp.maximum(dist, 1).astype(F32) / max_exact) / math.log(REL_MAX_DIST / max_exact)
    large = jnp.minimum(max_exact + (scaled * (REL_BUCKETS - max_exact)).astype(I32), REL_BUCKETS - 1)
    return jnp.where(dist < max_exact, dist, large)


def moba_bias_tiles(rel_bias):
    blk = MB_BLOCK
    span = 2 * blk - 1
    x = jnp.arange(span) - (blk - 1)
    dist = jnp.maximum(jnp.arange(MB_BIAS_TILES)[:, None] * blk + x[None, :], 0)
    w = rel_bias.astype(F32).T[:, _t5_bucket(dist)]
    h = w.shape[0]
    wp = jnp.pad(w, ((0, 0), (0, 0), (0, 1)))[:, :, None, :]
    return pl.pallas_call(
        _toeplitz_kernel,
        grid=(h, MB_BIAS_TILES),
        in_specs=[pl.BlockSpec((None, None, 1, 2 * blk), lambda i, j: (i, j, 0, 0))],
        out_specs=pl.BlockSpec((None, None, blk, blk), lambda i, j: (i, j, 0, 0)),
        out_shape=jax.ShapeDtypeStruct((h, MB_BIAS_TILES, blk, blk), F32),
        compiler_params=_cparams(("parallel", "parallel")),
        name="moba_bias_tiles",
    )(wp)


def _toeplitz_kernel(w_ref, o_ref):
    blk = o_ref.shape[0]
    x = jnp.broadcast_to(w_ref[...], (blk, 2 * blk))
    o_ref[...] = pltpu.roll(x, 1, 1, stride=1, stride_axis=0)[:, blk:]


def _mix_kernel(x_ref, ya_ref, yb_ref, gm_ref, wg_refs, wa_ref, wb_ref, wo_ref, o_ref):
    x = x_ref[...]
    h = _rms(x, gm_ref[...]).astype(BF16)
    za = jnp.dot(ya_ref[...], wa_ref[...], preferred_element_type=F32)
    zb = jnp.dot(yb_ref[...], wb_ref[...], preferred_element_type=F32)
    nw = len(wg_refs) // 2
    wc = wg_refs[0].shape[1]
    gate = lambda ref: jax.nn.sigmoid(jnp.dot(h, ref[...], preferred_element_type=F32))
    z = jnp.concatenate([gate(wg_refs[j]) * za[:, j * wc:(j + 1) * wc]
                         + gate(wg_refs[nw + j]) * zb[:, j * wc:(j + 1) * wc] for j in range(nw)], axis=1)
    o_ref[...] = x + jnp.dot(z.astype(BF16), wo_ref[...], preferred_element_type=F32)


def _mem_kv_kernel(m_ref, g_ref, wk_ref, wv_ref, k_ref, v_ref):
    mn = _rms(m_ref[...], g_ref[...]).astype(BF16)
    k_ref[...] = jnp.dot(mn, wk_ref[...], preferred_element_type=F32).astype(BF16)
    v_ref[...] = jnp.dot(mn, wv_ref[...], preferred_element_type=F32).astype(BF16)


def mem_kv(mem, g, wk, wv):
    b, m, d = mem.shape
    spec = pl.BlockSpec((None, m, d), lambda i: (i, 0, 0))
    wspec = pl.BlockSpec((d, d), lambda i: (0, 0))
    return pl.pallas_call(
        _mem_kv_kernel,
        grid=(b,),
        in_specs=[spec, pl.BlockSpec((1, d), lambda i: (0, 0)), wspec, wspec],
        out_specs=[spec, spec],
        out_shape=[jax.ShapeDtypeStruct((b, m, d), BF16)] * 2,
        compiler_params=_cparams(("parallel",)),
        name="mem_kv",
    )(mem, g, wk, wv)


def _cross_kernel(x_ref, g_ref, wq_ref, k_ref, v_ref, wo_ref, o_ref):
    x = x_ref[...]
    d = x.shape[1]
    dh = d // X_HEADS
    h = _rms(x, g_ref[...]).astype(BF16)
    q = (jnp.dot(h, wq_ref[...], preferred_element_type=F32) * (dh ** -0.5)).astype(BF16)
    outs = []
    for hh in range(X_HEADS):
        sl = slice(hh * dh, (hh + 1) * dh)
        s = lax.dot_general(q[:, sl], k_ref[:, sl], (((1,), (1,)), ((), ())),
                            preferred_element_type=F32)
        p = jnp.exp(s - jnp.max(s, axis=1, keepdims=True))
        l = jnp.sum(p, axis=1, keepdims=True)
        o = jnp.dot(p.astype(BF16), v_ref[:, sl], preferred_element_type=F32) / l
        outs.append(o.astype(BF16))
    o = jnp.concatenate(outs, axis=1)
    o_ref[...] = x + jnp.dot(o, wo_ref[...], preferred_element_type=F32)


def _mix_cross_kernel(x_ref, ya_ref, yb_ref, gm_ref, wg0, wg1, wg2, wg3, wa_ref, wb_ref, wo_ref,
                      g_ref, wq_ref, k_ref, v_ref, wox_ref, o_ref, x1_ref):
    _mix_kernel(x_ref, ya_ref, yb_ref, gm_ref, (wg0, wg1, wg2, wg3), wa_ref, wb_ref, wo_ref, x1_ref)
    _cross_kernel(x1_ref, g_ref, wq_ref, k_ref, v_ref, wox_ref, o_ref)


def mix_cross(x_all, ya, yb, gm, w_in_bf, wa, wb, wo, g, wq, kx, vx, wox, row0, tm=512):
    t = yb.shape[0]
    assert t % tm == 0 and row0 % tm == 0
    d = x_all.shape[1]
    w = ya.shape[1]
    m = kx.shape[1]
    x0 = row0 // tm
    wc = MB_WIDTH
    gcol0 = w_in_bf.shape[1] - 2 * d
    assert gcol0 % wc == 0 and d == 2 * wc
    gate_specs = [pl.BlockSpec((d, wc), lambda i, j=j: (0, gcol0 // wc + j)) for j in range(4)]
    const = lambda a: pl.BlockSpec(a.shape, lambda i: (0,) * a.ndim)
    kv = pl.BlockSpec((None, m, d), lambda i: (0, 0, 0))
    return pl.pallas_call(
        _mix_cross_kernel,
        grid=(t // tm,),
        in_specs=[
            pl.BlockSpec((tm, d), lambda i: (x0 + i, 0)),
            pl.BlockSpec((tm, w), lambda i: (i, 0)),
            pl.BlockSpec((tm, w), lambda i: (i, 0)),
            const(gm), *gate_specs,
            const(wa), const(wb), const(wo), const(g), const(wq), kv, kv, const(wox),
        ],
        out_specs=pl.BlockSpec((tm, d), lambda i: (i, 0)),
        out_shape=jax.ShapeDtypeStruct((t, d), F32),
        scratch_shapes=[pltpu.VMEM((tm, d), F32)],
        compiler_params=_cparams(("parallel",)),
        name="mix_cross",
    )(x_all, ya, yb, gm, *([w_in_bf] * 4), wa, wb, wo, g, wq, kx, vx, wox)


def _topk_rows(sc, k):
    n = sc.shape[0]
    io = lax.broadcasted_iota(I32, sc.shape, 0).astype(F32)
    vals, ids = [], []
    for _ in range(k):
        m = jnp.max(sc, axis=0, keepdims=True)
        ix = jnp.argmax(sc, axis=0, keepdims=True).astype(F32)
        vals.append(m)
        ids.append(ix)
        sc = jnp.where(io == ix, NEG_INF, sc)
    return jnp.concatenate(vals, axis=0), jnp.concatenate(ids, axis=0).astype(I32)


def _pack_bf16_halves(h):
    bits = lax.bitcast_convert_type(h, I32)
    r = bits + 0x7FFF + (lax.shift_right_logical(bits, 16) & 1)
    half = h.shape[1] // 2
    return lax.shift_right_logical(r[:, :half], 16) | (r[:, half:] & HI_MASK)


def _route_kernel(x_ref, g_ref, wq_ref, sk_ref, hp_ref, idx_ref, w_ref, hb_ref, it_ref, wt_ref):
    p = pl.program_id(1)

    @pl.when(p == 0)
    def _():
        h = _rms(x_ref[...], g_ref[...])
        hp_ref[...] = _pack_bf16_halves(h)
        hb_ref[...] = h.astype(BF16)

    qh = jnp.dot(hb_ref[...], wq_ref[...], preferred_element_type=F32)
    tops = []
    for c in range(2):
        seg = qh[:, c * PEER_HALF:(c + 1) * PEER_HALF]
        sc = lax.dot_general(sk_ref[c], seg, (((1,), (1,)), ((), ())),
                             precision=lax.Precision.HIGHEST, preferred_element_type=F32)
        tops.append(_topk_rows(sc, PEER_TOPK))
    (s0, i0), (s1, i1) = tops
    k = PEER_TOPK
    sub = 8
    tm = s0.shape[1]
    r8 = lax.broadcasted_iota(I32, (sub, tm), 0)
    r16 = lax.broadcasted_iota(I32, (k, tm), 0)
    cand_b = [s0[0:1] + s1, s0[1:2] + s1[:sub]]
    cidx_b = [i0[0:1] * PEER_NKEYS + i1, i0[1:2] * PEER_NKEYS + i1[:sub]]
    pos_b = [r16, k + r8]
    for a in range(2, sub):
        keep = r8 < (k // (a + 1))
        cand_b.append(jnp.where(keep, s0[a:a + 1] + s1[:sub], NEG_INF))
        cidx_b.append(i0[a:a + 1] * PEER_NKEYS + i1[:sub])
        pos_b.append(a * k + r8)
    cand_b.append(s0[sub:] + s1[0:1])
    cidx_b.append(i0[sub:] * PEER_NKEYS + i1[0:1])
    pos_b.append((sub + r8) * k)
    cand = jnp.concatenate(cand_b, axis=0)
    cidx = jnp.concatenate(cidx_b, axis=0)
    pos = jnp.concatenate(pos_b, axis=0).astype(F32)
    vals, ids = [], []
    for _ in range(k):
        m = jnp.max(cand, axis=0, keepdims=True)
        px = jnp.min(jnp.where(cand == m, pos, float(k * k)), axis=0, keepdims=True)
        hit = pos == px
        vals.append(m)
        ids.append(jnp.sum(jnp.where(hit, cidx, 0), axis=0, keepdims=True))
        cand = jnp.where(hit, NEG_INF, cand)
    sf = jnp.concatenate(vals, axis=0)
    e = jnp.exp(sf - sf[0:1])
    rows = pl.ds(pl.multiple_of(p * PEER_TOPK, PEER_TOPK), PEER_TOPK)
    wt_ref[rows, :] = e / jnp.sum(e, axis=0, keepdims=True)
    it_ref[rows, :] = jnp.concatenate(ids, axis=0)

    @pl.when(p == pl.num_programs(1) - 1)
    def _():
        idx_ref[...] = it_ref[...].T
        w_ref[...] = wt_ref[...].T


def peer_route(x2d, g, wq, sk, tok0, t, tm=1024):
    assert t % tm == 0 and tok0 % tm == 0
    d = x2d.shape[1]
    ph = sk.shape[0]
    nsel = ph * PEER_TOPK
    blk0 = tok0 // tm
    return pl.pallas_call(
        _route_kernel,
        grid=(t // tm, ph),
        in_specs=[
            pl.BlockSpec((tm, d), lambda i, p: (blk0 + i, 0)),
            pl.BlockSpec((1, d), lambda i, p: (0, 0)),
            pl.BlockSpec((d, 2 * PEER_HALF), lambda i, p: (0, p)),
            pl.BlockSpec((None, 2, PEER_NKEYS, PEER_HALF), lambda i, p: (p, 0, 0, 0)),
        ],
        out_specs=[
            pl.BlockSpec((tm, d // 2), lambda i, p: (i, 0)),
            pl.BlockSpec((tm, nsel), lambda i, p: (i, 0)),
            pl.BlockSpec((tm, nsel), lambda i, p: (i, 0)),
        ],
        out_shape=[jax.ShapeDtypeStruct((t, d // 2), I32),
                   jax.ShapeDtypeStruct((t, nsel), I32),
                   jax.ShapeDtypeStruct((t, nsel), F32)],
        scratch_shapes=[pltpu.VMEM((tm, d), BF16),
                        pltpu.VMEM((nsel, tm), I32),
                        pltpu.VMEM((nsel, tm), F32)],
        compiler_params=_cparams(("parallel", "arbitrary")),
        name="peer_route",
    )(x2d, g, wq, sk)


def _final_kernel(x_ref, y_ref, g_ref, *rest):
    o_ref = rest[-1]
    o_ref[...] = _rms(x_ref[...] + y_ref[...], g_ref[...])


def final_norm_into(out, xs, y, g, row0, total, tm=512):
    t, d = y.shape
    assert t % tm == 0 and row0 % tm == 0 and total % tm == 0
    blk0 = row0 // tm
    spec = pl.BlockSpec((tm, d), lambda i: (i, 0))
    in_specs = [spec, spec, pl.BlockSpec((1, d), lambda i: (0, 0))]
    args = [xs, y, g]
    aliases = {}
    if out is not None:
        in_specs.append(pl.BlockSpec(memory_space=pl.ANY))
        args.append(out)
        aliases = {3: 0}
    return pl.pallas_call(
        _final_kernel, grid=(t // tm,),
        in_specs=in_specs,
        out_specs=pl.BlockSpec((tm, d), lambda i: (blk0 + i, 0)),
        out_shape=jax.ShapeDtypeStruct((total, d), F32),
        input_output_aliases=aliases,
        compiler_params=_cparams(("parallel",)), name="final_norm",
    )(*args)


SC_CORES = 2
SC_SUBCORES = 16
SC_WORKERS = SC_CORES * SC_SUBCORES
SC_LANES = 16
SC_GROUP = 32


def _sc_mesh():
    return plsc.VectorSubcoreMesh(core_axis_name="c", subcore_axis_name="s")


def _sc_params():
    return pltpu.CompilerParams(needs_layout_passes=False)


def _sc_worker_id():
    return lax.axis_index("s") * SC_CORES + lax.axis_index("c")


SC_ROW_LANE = 128


def _sc_unit_off(u):
    off = u * SC_LANES
    return off if isinstance(off, int) else pl.multiple_of(off, SC_LANES)


GELU_C0 = math.sqrt(2.0 / math.pi)
GELU_C1 = 0.044715


def _gelu_tanh(x):
    z = GELU_C0 * (x + GELU_C1 * (x * x * x))
    th = 1.0 - 2.0 / (jnp.exp(2.0 * z) + 1.0)
    return 0.5 * x * (1.0 + th)


SC_PK_RING = 4
SC_PK_SUB = 4
HI_MASK = -65536


def _pack_tables_kernel(u_ref, v_ref, o_ref):
    for part, ref in enumerate((u_ref, v_ref)):
        words = _pack_bf16_halves(ref[...])
        for sub in range(SC_PK_SUB):
            o_ref[:, part * SC_PK_SUB + sub, :] = words[:, sub * SC_ROW_LANE:(sub + 1) * SC_ROW_LANE]


def pack_expert_tables(u, v, te=512):
    e, d = u.shape
    assert d == 2 * SC_PK_SUB * SC_ROW_LANE
    spec = pl.BlockSpec((te, d), lambda i: (i, 0))
    return pl.pallas_call(
        _pack_tables_kernel, grid=(e // te,), in_specs=[spec, spec],
        out_specs=pl.BlockSpec((te, 2 * SC_PK_SUB, SC_ROW_LANE), lambda i: (i, 0, 0)),
        out_shape=jax.ShapeDtypeStruct((e, 2 * SC_PK_SUB, SC_ROW_LANE), I32),
        compiler_params=_cparams(("parallel",)), name="pack_expert_tables",
    )(u, v)


def _unpack_halves(x32):
    w = plsc.bitcast(x32, I32)
    return plsc.bitcast(w << 16, F32), plsc.bitcast(w & HI_MASK, F32)


def _tree_sum(xs):
    while len(xs) > 1:
        xs = [xs[i] + xs[i + 1] for i in range(0, len(xs), 2)]
    return xs[0]


def peer_experts_pk_sc(tab_uv, idx_flat, w_flat, hp, d):
    t = hp.shape[0]
    nsel = PEER_SEL
    assert t % SC_WORKERS == 0 and d == 2 * SC_PK_SUB * SC_ROW_LANE
    tpw = t // SC_WORKERS
    g = SC_GROUP if tpw % SC_GROUP == 0 else SC_GROUP // 2
    assert tpw % g == 0
    groups = tpw // g
    heads = nsel // SC_LANES
    chunks = d // 32
    units = g * heads
    ring = SC_PK_RING
    assert units % ring == 0
    row_buf = pltpu.VMEM((SC_LANES, 2 * SC_PK_SUB, SC_ROW_LANE), I32)

    def row_words(rows, r, wc, sub0):
        per = SC_ROW_LANE // SC_LANES
        return plsc.bitcast(
            rows[r, sub0 + wc // per, pl.ds(pl.multiple_of((wc % per) * SC_LANES, SC_LANES), SC_LANES)], BF16)

    def ring_loop(n_units, start, wait, compute):
        for u in range(ring - 1):
            start(u, u)

        @pl.loop(0, n_units, step=ring)
        def _(uu):
            for b in range(ring):
                u = uu + b
                nxt = u + (ring - 1)

                @pl.when(nxt < n_units)
                def _():
                    start(nxt, (b + ring - 1) % ring)

                wait(u, b)
                compute(u, b)

    @functools.partial(
        pl.kernel, mesh=_sc_mesh(),
        out_type=jax.ShapeDtypeStruct((t, d), F32),
        scratch_types=[
            pltpu.VMEM((g * nsel,), I32),
            pltpu.VMEM((g * nsel,), F32),
            pltpu.VMEM((g, d // 2), I32),
            pltpu.VMEM((g, d), F32),
            pltpu.VMEM((SC_LANES * SC_LANES,), F32),
            [row_buf] * ring,
            [pltpu.SemaphoreType.DMA] * ring,
        ],
        compiler_params=_sc_params(),
        name="peer_experts_pk_sc",
    )
    def k(tab_hbm, idx_hbm, w_hbm, h_hbm, out_hbm, idx_v, coef_v, h_v, y_v, red_v, rows, sems):
        wid = _sc_worker_id()
        lane = lax.iota(I32, SC_LANES)

        def copy(u, slot):
            ids = idx_v.at[pl.ds(_sc_unit_off(u), SC_LANES)]
            return pltpu.make_async_copy(tab_hbm.at[ids], rows[slot], sems[slot])

        def dots(u, slot):
            tt = u // heads

            def body(cp, accs):
                out = []
                hv = [plsc.bitcast(h_v[tt, pl.ds(pl.multiple_of((2 * cp + i) * SC_LANES, SC_LANES), SC_LANES)], BF16)
                      for i in range(2)]
                for r in range(SC_LANES):
                    pr = (row_words(rows[slot], r, 2 * cp, 0) * hv[0]
                          + row_words(rows[slot], r, 2 * cp + 1, 0) * hv[1])
                    lo, hi = _unpack_halves(pr)
                    out.append(accs[r] + lo + hi)
                return tuple(out)

            accs = lax.fori_loop(0, chunks // 2, body,
                                 tuple(jnp.zeros((SC_LANES,), F32) for _ in range(SC_LANES)))
            for r in range(SC_LANES):
                red_v[pl.ds(r * SC_LANES, SC_LANES)] = accs[r]
            act = _tree_sum([plsc.load_gather(red_v, [lane * SC_LANES + j]) for j in range(SC_LANES)])
            sl = pl.ds(_sc_unit_off(u), SC_LANES)
            coef_v[sl] = coef_v[sl] * _gelu_tanh(act)

        def combine(u, slot):
            tt = u // heads
            first = (u % heads) == 0
            cb = []
            for r in range(SC_LANES):
                c = plsc.load_gather(coef_v, [jnp.full((SC_LANES,), u * SC_LANES + r, I32)])
                cb.append(plsc.pack(c, c, format=plsc.PackFormat.INTERLEAVED))

            @plsc.parallel_loop(0, chunks, unroll=2)
            def _(wc):
                lo, hi = _unpack_halves(
                    _tree_sum([cb[r] * row_words(rows[slot], r, wc, SC_PK_SUB) for r in range(SC_LANES)]))
                for half, val in ((0, lo), (1, hi)):
                    sl = pl.ds(pl.multiple_of(half * (d // 2) + wc * SC_LANES, SC_LANES), SC_LANES)
                    y_v[tt, sl] = val + jnp.where(first, 0.0, y_v[tt, sl])

        def unit(u, slot):
            dots(u, slot)
            combine(u, slot)

        @pl.loop(0, groups)
        def _(gi):
            base = wid * tpw + gi * g
            pltpu.sync_copy(idx_hbm.at[pl.ds(base * nsel, g * nsel)], idx_v)
            pltpu.sync_copy(w_hbm.at[pl.ds(base * nsel, g * nsel)], coef_v)
            pltpu.sync_copy(h_hbm.at[pl.ds(base, g)], h_v)
            ring_loop(units, lambda u, s: copy(u, s).start(), lambda u, s: copy(u, s).wait(), unit)
            pltpu.sync_copy(y_v, out_hbm.at[pl.ds(base, g)])

    return k(tab_uv, idx_flat, w_flat, hp)


def kernel(x, mem, rel_bias, ln_mix, w_in, hg_lower, hg_norm, w_up_a, w_up_b, w_out, ln_cross, ln_mem, wq_x, wk_x, wv_x, wo_x, ln_ffn, peer_query, peer_subkeys, peer_u, peer_v, ln_final):
    b, s, d = x.shape
    depth = w_in.shape[0]
    assert depth == 1, "the residual after PEER is fused into the final norm"
    assert s % MB_BLOCK == 0 and s % HG_CHUNK == 0 and s % (PEER_SLICES * SC_WORKERS * SC_GROUP) == 0
    nb = s // MB_BLOCK
    row = lambda a: a.reshape(1, -1).astype(F32)
    lb_all = jnp.cumsum(jax.nn.softmax(hg_lower.astype(F32), axis=0), axis=0)
    bias = moba_bias_tiles(rel_bias)
    n_hg = 4 * HG_WIDTH
    n_qk = 2 * MB_WIDTH
    n_mb = 3 * MB_WIDTH
    l = 0
    w = cast_bf16(w_in[l].astype(F32))
    w_vt = cast_bf16_t(w_in[l].astype(F32), n_hg + n_qk, n_mb - n_qk)
    wa, wb, wo = w_up_a[l].astype(BF16), w_up_b[l].astype(BF16), w_out[l].astype(BF16)
    wqx, wox = wq_x[l].astype(BF16), wo_x[l].astype(BF16)
    wpq, sk = peer_query[l].astype(BF16), peer_subkeys[l].astype(F32)
    tab_uv = pack_expert_tables(peer_u[l].astype(F32), peer_v[l].astype(F32))
    kx, vx = mem_kv(mem, row(ln_mem[l]), wk_x[l].astype(BF16), wv_x[l].astype(BF16))

    x_all = x.reshape(b * s, d)
    outs = []
    for bi in range(b):
        p0, pqk, km, vt = in_proj(x_all, row(ln_mix[l]), w, w_vt, bi * s, s)
        hg_state = jnp.zeros((HG_HEADS, HG_D, HG_D), F32)
        km = km.reshape(1, nb, MB_WIDTH)
        sizes = [s // PEER_SLICES] * PEER_SLICES
        if bi == b - 1:
            first = SC_WORKERS * SC_GROUP // 2
            sizes = [first, sizes[0] - first] + sizes[1:]
        tok0 = 0
        for ts in sizes:
            ya, hg_state = hgrn2(p0, row(lb_all[l]), row(hg_norm[l]), hg_state, tok0, ts)
            yb = moba_attention(pqk, vt, km, bias, 1, s, tok0 // MB_BLOCK, ts // MB_BLOCK)
            xs = mix_cross(x_all, ya, yb, row(ln_mix[l]), w, wa, wb, wo, row(ln_cross[l]), wqx,
                           kx[bi:bi + 1], vx[bi:bi + 1], wox, bi * s + tok0)
            hp, eidx, wts = peer_route(xs, row(ln_ffn[l]), wpq, sk, 0, ts, math.gcd(ts, 1024))
            y = peer_experts_pk_sc(tab_uv, eidx.reshape(ts * PEER_SEL), wts.reshape(ts * PEER_SEL), hp, d)
            outs.append((xs, y, bi * s + tok0))
            tok0 += ts
    out = None
    for xs, y, row0 in sorted(outs, key=lambda e: (-(e[2] // s), e[2])):
        out = final_norm_into(out, xs, y, row(ln_final), row0, b * s)
    return out.reshape(b, s, d)
```

```python
import functools
import math

import jax
import jax.numpy as jnp
from jax import lax
from jax.experimental import pallas as pl
from jax.experimental.pallas import tpu as pltpu
from jax.experimental.pallas import tpu_sc as plsc

F32 = jnp.float32
BF16 = jnp.bfloat16
I32 = jnp.int32
EPS = 1e-6
NEG_INF = float("-inf")

HG_HEADS = 4
HG_D = 128
HG_WIDTH = HG_HEADS * HG_D
HG_CHUNK = 64
HG_SUB = 16
MB_HEADS = 8
MB_DH = 64
MB_WIDTH = MB_HEADS * MB_DH
MB_BLOCK = 256
MB_TOPK = 3
MB_BIAS_TILES = 8
REL_BUCKETS = 32
REL_MAX_DIST = 2048
X_HEADS = 4
PEER_HEADS = 8
PEER_NKEYS = 128
PEER_TOPK = 16
PEER_HALF = 128
PEER_SEL = PEER_HEADS * PEER_TOPK
PEER_SLICES = 4

VMEM_LIMIT = 56 * 1024 * 1024


def _cparams(sem):
    return pltpu.CompilerParams(dimension_semantics=sem, vmem_limit_bytes=VMEM_LIMIT)


def _rms(x, g):
    ms = jnp.mean(x * x, axis=-1, keepdims=True)
    return x * lax.rsqrt(ms + EPS) * g


def _cast_t_kernel(w_ref, o_ref):
    o_ref[...] = w_ref[...].T.astype(o_ref.dtype)


def cast_bf16_t(w, col0, n):
    k = w.shape[0]
    assert col0 % n == 0
    return pl.pallas_call(
        _cast_t_kernel, grid=(1,),
        in_specs=[pl.BlockSpec((k, n), lambda i: (0, col0 // n))],
        out_specs=pl.BlockSpec((n, k), lambda i: (0, 0)),
        out_shape=jax.ShapeDtypeStruct((n, k), BF16),
        compiler_params=_cparams(("arbitrary",)), name="cast_bf16_t",
    )(w)


def _cast_kernel(w_ref, o_ref):
    o_ref[...] = w_ref[...].astype(o_ref.dtype)


def cast_bf16(w, tn=512):
    k, n = w.shape
    assert n % tn == 0
    spec = pl.BlockSpec((k, tn), lambda j: (0, j))
    return pl.pallas_call(
        _cast_kernel, grid=(n // tn,), in_specs=[spec], out_specs=spec,
        out_shape=jax.ShapeDtypeStruct((k, n), BF16),
        compiler_params=_cparams(("parallel",)), name="cast_bf16",
    )(w)


def _in_proj_kernel(x_ref, g_ref, w0_ref, w1_ref, wvt_ref, o0_ref, o1_ref, okm_ref, ovt_ref):
    h = _rms(x_ref[...], g_ref[...]).astype(BF16)
    o0_ref[...] = jnp.dot(h, w0_ref[...], preferred_element_type=F32)
    qk = jnp.dot(h, w1_ref[...], preferred_element_type=F32)
    o1_ref[...] = qk.astype(BF16)
    okm_ref[0] = jnp.mean(qk[:, MB_WIDTH:], axis=0, keepdims=True)
    vt = lax.dot_general(wvt_ref[...], h, (((1,), (1,)), ((), ())), preferred_element_type=F32).astype(BF16)
    for hd in range(MB_HEADS):
        ovt_ref[0, hd * MB_VROWS:hd * MB_VROWS + MB_DH, :] = vt[hd * MB_DH:(hd + 1) * MB_DH]
        ovt_ref[0, hd * MB_VROWS + MB_DH:(hd + 1) * MB_VROWS, :] = jnp.ones((MB_ONES, vt.shape[1]), BF16)


def in_proj(x_all, g, w, wvt, row0, t):
    d = x_all.shape[1]
    tm = MB_BLOCK
    assert t % tm == 0 and row0 % tm == 0
    blk0 = row0 // tm
    n0, n1, nv = 4 * HG_WIDTH, 2 * MB_WIDTH, MB_VT_ROWS
    assert wvt.shape == (MB_WIDTH, d) and w.shape == (d, n0 + n1 + MB_WIDTH + 2 * d)
    assert n0 % n1 == 0
    full = lambda a: pl.BlockSpec(a.shape, lambda i: (0, 0))
    w_specs = [pl.BlockSpec((d, n0), lambda i: (0, 0)), pl.BlockSpec((d, n1), lambda i: (0, n0 // n1)), full(wvt)]
    return pl.pallas_call(
        _in_proj_kernel,
        grid=(t // tm,),
        in_specs=[pl.BlockSpec((tm, d), lambda i: (blk0 + i, 0)), full(g)] + w_specs,
        out_specs=[pl.BlockSpec((tm, n0), lambda i: (i, 0)),
                   pl.BlockSpec((tm, n1), lambda i: (i, 0)),
                   pl.BlockSpec((1, 1, MB_WIDTH), lambda i: (i, 0, 0)),
                   pl.BlockSpec((1, nv, tm), lambda i: (i, 0, 0))],
        out_shape=[jax.ShapeDtypeStruct((t, n0), F32),
                   jax.ShapeDtypeStruct((t, n1), BF16),
                   jax.ShapeDtypeStruct((t // tm, 1, MB_WIDTH), F32),
                   jax.ShapeDtypeStruct((t // tm, nv, tm), BF16)],
        compiler_params=_cparams(("parallel",)),
        name="in_proj",
    )(x_all, g, w, w, wvt)


def _hgrn_kernel(q_ref, f_ref, i_ref, g_ref, lb_ref, gain_ref, st0_ref, o_ref, stn_ref, st_ref):
    c = pl.program_id(0)

    @pl.when(c == 0)
    def _():
        st_ref[...] = st0_ref[...]

    C, S = HG_CHUNK, HG_SUB
    row = lax.broadcasted_iota(I32, (C, C), 0)
    col = lax.broadcasted_iota(I32, (C, C), 1)
    tril = (row >= col).astype(F32)
    t_iota = lax.broadcasted_iota(I32, (S, 1), 0)

    for h in range(HG_HEADS):
        sl = slice(h * HG_D, (h + 1) * HG_D)
        q = q_ref[:, sl]
        v = i_ref[:, sl]
        lb = lb_ref[:, sl]
        f = lb + (1.0 - lb) * jax.nn.sigmoid(f_ref[:, sl])
        lf = jnp.log(f)
        k = 1.0 - f
        b = jnp.dot(tril, lf, precision=lax.Precision.HIGHEST, preferred_element_type=F32)
        st = st_ref[h]
        vb = v.astype(BF16)
        qd = (q * jnp.exp(b)).astype(BF16)
        o_inter = lax.dot_general(qd, st.astype(BF16), (((1,), (1,)), ((), ())),
                                  preferred_element_type=F32)
        outs = []
        for i in range(C // S):
            r0 = i * S
            qi = q[r0:r0 + S]
            ki = k[r0:r0 + S]
            bi = b[r0:r0 + S]
            vi = v[r0:r0 + S]
            oi = o_inter[r0:r0 + S]
            if i > 0:
                bs = b[r0 - 1:r0]
                qh = (qi * jnp.exp(bi - bs)).astype(BF16)
                kh = (k[:r0] * jnp.exp(bs - b[:r0])).astype(BF16)
                a = lax.dot_general(qh, kh, (((1,), (1,)), ((), ())), preferred_element_type=F32)
                oi = oi + jnp.dot(a.astype(BF16), vb[:r0], preferred_element_type=F32)
            half = S // 2
            o_half = [oi[:half], oi[half:]]
            for s in range(S):
                for hf in range(s // half, 2):
                    rows = slice(hf * half, (hf + 1) * half)
                    dec = jnp.exp(jnp.minimum(bi[rows] - bi[s:s + 1], 0.0))
                    a_s = jnp.sum(qi[rows] * ki[s:s + 1] * dec, axis=-1, keepdims=True)
                    a_s = jnp.where(t_iota[rows] >= s, a_s, 0.0)
                    o_half[hf] = o_half[hf] + a_s * vi[s:s + 1]
            outs.extend(o_half)
        o = jnp.concatenate(outs, axis=0)
        b_end = b[C - 1:C]
        kd = (k * jnp.exp(b_end - b)).astype(BF16)
        upd = lax.dot_general(vb, kd, (((0,), (0,)), ((), ())), preferred_element_type=F32)
        st_ref[h] = st * jnp.exp(b_end) + upd
        o = o * lax.rsqrt(jnp.mean(o * o, axis=-1, keepdims=True) + EPS)
        g = g_ref[:, sl]
        o_ref[:, sl] = (o * gain_ref[:, sl] * (g * jax.nn.sigmoid(g))).astype(o_ref.dtype)

    @pl.when(c == pl.num_programs(0) - 1)
    def _():
        stn_ref[...] = st_ref[...]


def hgrn2(p0, lb, gain, state, tok0, t):
    assert t % HG_CHUNK == 0 and tok0 % HG_CHUNK == 0
    nc = t // HG_CHUNK
    c0 = tok0 // HG_CHUNK
    w = HG_WIDTH

    def col(j):
        return pl.BlockSpec((HG_CHUNK, w), lambda c, j=j: (c0 + c, j))

    st_spec = pl.BlockSpec(state.shape, lambda c: (0, 0, 0))
    return pl.pallas_call(
        _hgrn_kernel,
        grid=(nc,),
        in_specs=[col(0), col(1), col(2), col(3),
                  pl.BlockSpec((1, w), lambda c: (0, 0)),
                  pl.BlockSpec((1, w), lambda c: (0, 0)),
                  st_spec],
        out_specs=[pl.BlockSpec((HG_CHUNK, w), lambda c: (c, 0)), st_spec],
        out_shape=[jax.ShapeDtypeStruct((t, w), BF16), jax.ShapeDtypeStruct(state.shape, F32)],
        scratch_shapes=[pltpu.VMEM((HG_HEADS, HG_D, HG_D), F32)],
        compiler_params=_cparams(("arbitrary",)),
        name="hgrn2",
    )(p0, p0, p0, p0, lb, gain, state)


MB_PAIR = 4
MB_PW = MB_PAIR * MB_DH
MB_LG = 128
MB_ONES = 16
MB_VROWS = MB_DH + MB_ONES
MB_VT_ROWS = MB_HEADS * MB_VROWS


def _moba_kernel(q_ref, k_ref, vt_ref, km_ref, bias_ref, o_ref, *scratch, qb0):
    m_ref, l_ref, al_ref, acc_ref, msk_ref, s_ref, p_ref = (
        scratch[i * MB_PAIR:(i + 1) * MB_PAIR] for i in range(7))
    qi = pl.program_id(2) + qb0
    nb = km_ref.shape[0]
    blk = MB_BLOCK
    heads = range(MB_PAIR)
    grp = lambda hh: slice((hh // 2) * MB_LG, (hh // 2 + 1) * MB_LG)
    q = q_ref[...]
    lane = lax.broadcasted_iota(I32, (blk, MB_LG), 1)
    in_head = [(lane < MB_DH) if hh % 2 == 0 else (lane >= MB_DH) for hh in heads]
    qs = q * jnp.asarray(MB_DH ** -0.5, BF16)
    nt = (((1,), (1,)), ((), ()))
    qf = q.astype(F32)
    qht = [jnp.where(in_head[hh], qs[:, grp(hh)].astype(F32), 0.0).T.astype(BF16) for hh in heads]

    n_io = lax.broadcasted_iota(I32, (nb, blk), 0)
    for hh in heads:
        gate = lax.dot_general(km_ref[:, grp(hh)], jnp.where(in_head[hh], qf[:, grp(hh)], 0.0), nt,
                               precision=lax.Precision.HIGHEST, preferred_element_type=F32)
        gate = jnp.where(n_io < qi, gate, NEG_INF)
        chosen = n_io < 0
        for _ in range(MB_TOPK):
            mx = jnp.max(gate, axis=0, keepdims=True)
            ix = jnp.min(jnp.where(gate == mx, n_io, nb), axis=0, keepdims=True)
            hit = n_io == ix
            chosen = chosen | (hit & (mx > NEG_INF))
            gate = jnp.where(hit, NEG_INF, gate)
        msk_ref[hh][...] = jnp.where(chosen, 0.0, NEG_INF)

    vrows = lambda hh: slice(hh * MB_VROWS, (hh + 1) * MB_VROWS)

    def pv_stage(blk_idx):
        vtb = vt_ref[blk_idx]
        r = [jnp.dot(vtb[vrows(hh)], p_ref[hh][...], preferred_element_type=F32) for hh in heads]
        al = [al_ref[hh][...] for hh in heads]
        a_new = [al[hh] * acc_ref[hh][...] + r[hh][:MB_DH] for hh in heads]
        l_new = [al[hh] * l_ref[hh][...] + r[hh][MB_DH:MB_DH + 1] for hh in heads]
        return a_new, l_new

    def store_pv(a_new, l_new):
        for hh in heads:
            acc_ref[hh][...] = a_new[hh]
            l_ref[hh][...] = l_new[hh]

    def softmax_stage():
        s = [s_ref[hh][...] for hh in heads]
        m_old = [m_ref[hh][...] for hh in heads]
        m_new = [jnp.maximum(m_old[hh], jnp.max(s[hh], axis=0, keepdims=True)) for hh in heads]
        alpha = [jnp.exp(m_old[hh] - m_new[hh]) for hh in heads]
        p = [jnp.exp((s[hh] - m_new[hh]).astype(BF16)) for hh in heads]
        return p, alpha, m_new

    def store_softmax(p, alpha, m_new):
        for hh in heads:
            p_ref[hh][...] = p[hh]
            al_ref[hh][...] = alpha[hh]
            m_ref[hh][...] = m_new[hh]

    k_own = k_ref[pl.ds(pl.multiple_of(qi * blk, blk), blk), :]
    key_io = lax.broadcasted_iota(I32, (blk, blk), 0)
    qry_io = lax.broadcasted_iota(I32, (blk, blk), 1)
    for hh in heads:
        s = jnp.dot(k_own[:, grp(hh)], qht[hh], preferred_element_type=F32) + bias_ref[hh, 0]
        s_ref[hh][...] = jnp.where(key_io <= qry_io, s, NEG_INF)
        m_ref[hh][...] = jnp.full((1, blk), NEG_INF, F32)
        l_ref[hh][...] = jnp.zeros((1, blk), F32)
        al_ref[hh][...] = jnp.ones((1, blk), F32)
        acc_ref[hh][...] = jnp.zeros((MB_DH, blk), F32)
        p_ref[hh][...] = jnp.zeros((blk, blk), BF16)

    def step(i, carry, far):
        pv = pv_stage(jnp.where(i <= 1, qi, i - 2))
        sm = softmax_stage()
        kn = k_ref[pl.ds(pl.multiple_of(i * blk, blk), blk), :]
        if far:
            row = [msk_ref[hh][pl.ds(i, 1), :] + bias_ref[hh, MB_BIAS_TILES - 1, 0:1, 0:1] for hh in heads]
            s_next = [jnp.dot(kn[:, grp(hh)], qht[hh], preferred_element_type=F32) + row[hh] for hh in heads]
        else:
            d = qi - i
            s_next = [jnp.dot(kn[:, grp(hh)], qht[hh], preferred_element_type=F32)
                      + bias_ref[hh, d] + msk_ref[hh][pl.ds(i, 1), :] for hh in heads]
        store_pv(*pv)
        for hh in heads:
            s_ref[hh][...] = s_next[hh]
        store_softmax(*sm)
        return carry

    n_far = jnp.maximum(qi - (MB_BIAS_TILES - 2), 0)
    lax.fori_loop(0, n_far, functools.partial(step, far=True), 0)
    lax.fori_loop(n_far, qi, functools.partial(step, far=False), 0)
    pv = pv_stage(jnp.where(qi <= 1, qi, qi - 2))
    sm = softmax_stage()
    store_pv(*pv)
    store_softmax(*sm)
    a_fin, l_fin = pv_stage(jnp.where(qi == 0, qi, qi - 1))
    out_t = jnp.concatenate([a_fin[hh] / l_fin[hh] for hh in heads], axis=0)
    o_ref[...] = out_t.T.astype(o_ref.dtype)


def moba_attention(pqk, vt, km, bias, batch, seq, qb0=0, nqb=None):
    nb = seq // MB_BLOCK
    nqb = nb if nqb is None else nqb
    t = batch * nqb * MB_BLOCK
    groups = MB_WIDTH // MB_PW
    nkb = qb0 + nqb if batch == 1 else nb
    return pl.pallas_call(
        functools.partial(_moba_kernel, qb0=qb0),
        grid=(batch, groups, nqb),
        in_specs=[
            pl.BlockSpec((MB_BLOCK, MB_PW), lambda b, j, i: (b * nb + qb0 + i, j)),
            pl.BlockSpec((nkb * MB_BLOCK, MB_PW), lambda b, j, i: (b, groups + j)),
            pl.BlockSpec((nkb, MB_PAIR * MB_VROWS, MB_BLOCK), lambda b, j, i: (b, j, 0)),
            pl.BlockSpec((None, nb, MB_PW), lambda b, j, i: (b, 0, j)),
            pl.BlockSpec((MB_PAIR, MB_BIAS_TILES, MB_BLOCK, MB_BLOCK), lambda b, j, i: (j, 0, 0, 0)),
        ],
        out_specs=pl.BlockSpec((MB_BLOCK, MB_PW), lambda b, j, i: (b * nqb + i, j)),
        out_shape=jax.ShapeDtypeStruct((t, MB_WIDTH), BF16),
        scratch_shapes=(
            [pltpu.VMEM((1, MB_BLOCK), F32)] * (3 * MB_PAIR)
            + [pltpu.VMEM((MB_DH, MB_BLOCK), F32)] * MB_PAIR
            + [pltpu.VMEM((nb, MB_BLOCK), F32)] * MB_PAIR
            + [pltpu.VMEM((MB_BLOCK, MB_BLOCK), F32)] * MB_PAIR
            + [pltpu.VMEM((MB_BLOCK, MB_BLOCK), BF16)] * MB_PAIR
        ),
        compiler_params=_cparams(("parallel", "parallel", "arbitrary")),
        name="moba_attn",
    )(pqk, pqk, vt, km, bias)


def _t5_bucket(dist):
    max_exact = REL_BUCKETS // 2
    scaled = jnp.log(jnp.maximum(dist, 1).astype(F32) / max_exact) / math.log(REL_MAX_DIST / max_exact)
    large = jnp.minimum(max_exact + (scaled * (REL_BUCKETS - max_exact)).astype(I32), REL_BUCKETS - 1)
    return jnp.where(dist < max_exact, dist, large)


def moba_bias_tiles(rel_bias):
    blk = MB_BLOCK
    span = 2 * blk - 1
    x = jnp.arange(span) - (blk - 1)
    dist = jnp.maximum(jnp.arange(MB_BIAS_TILES)[:, None] * blk + x[None, :], 0)
    w = rel_bias.astype(F32).T[:, _t5_bucket(dist)]
    h = w.shape[0]
    wp = jnp.pad(w, ((0, 0), (0, 0), (0, 1)))[:, :, None, :]
    return pl.pallas_call(
        _toeplitz_kernel,
        grid=(h, MB_BIAS_TILES),
        in_specs=[pl.BlockSpec((None, None, 1, 2 * blk), lambda i, j: (i, j, 0, 0))],
        out_specs=pl.BlockSpec((None, None, blk, blk), lambda i, j: (i, j, 0, 0)),
        out_shape=jax.ShapeDtypeStruct((h, MB_BIAS_TILES, blk, blk), F32),
        compiler_params=_cparams(("parallel", "parallel")),
        name="moba_bias_tiles",
    )(wp)


def _toeplitz_kernel(w_ref, o_ref):
    blk = o_ref.shape[0]
    x = jnp.broadcast_to(w_ref[...], (blk, 2 * blk))
    o_ref[...] = pltpu.roll(x, 1, 1, stride=1, stride_axis=0)[:, blk:]


def _mix_kernel(x_ref, ya_ref, yb_ref, gm_ref, wg_refs, wa_ref, wb_ref, wo_ref, o_ref):
    x = x_ref[...]
    h = _rms(x, gm_ref[...]).astype(BF16)
    za = jnp.dot(ya_ref[...], wa_ref[...], preferred_element_type=F32)
    zb = jnp.dot(yb_ref[...], wb_ref[...], preferred_element_type=F32)
    nw = len(wg_refs) // 2
    wc = wg_refs[0].shape[1]
    gate = lambda ref: jax.nn.sigmoid(jnp.dot(h, ref[...], preferred_element_type=F32))
    z = jnp.concatenate([gate(wg_refs[j]) * za[:, j * wc:(j + 1) * wc]
                         + gate(wg_refs[nw + j]) * zb[:, j * wc:(j + 1) * wc] for j in range(nw)], axis=1)
    o_ref[...] = x + jnp.dot(z.astype(BF16), wo_ref[...], preferred_element_type=F32)


def _mem_kv_kernel(m_ref, g_ref, wk_ref, wv_ref, k_ref, v_ref):
    mn = _rms(m_ref[...], g_ref[...]).astype(BF16)
    k_ref[...] = jnp.dot(mn, wk_ref[...], preferred_element_type=F32).astype(BF16)
    v_ref[...] = jnp.dot(mn, wv_ref[...], preferred_element_type=F32).astype(BF16)


def mem_kv(mem, g, wk, wv):
    b, m, d = mem.shape
    spec = pl.BlockSpec((None, m, d), lambda i: (i, 0, 0))
    wspec = pl.BlockSpec((d, d), lambda i: (0, 0))
    return pl.pallas_call(
        _mem_kv_kernel,
        grid=(b,),
        in_specs=[spec, pl.BlockSpec((1, d), lambda i: (0, 0)), wspec, wspec],
        out_specs=[spec, spec],
        out_shape=[jax.ShapeDtypeStruct((b, m, d), BF16)] * 2,
        compiler_params=_cparams(("parallel",)),
        name="mem_kv",
    )(mem, g, wk, wv)


def _cross_kernel(x_ref, g_ref, wq_ref, k_ref, v_ref, wo_ref, o_ref):
    x = x_ref[...]
    d = x.shape[1]
    dh = d // X_HEADS
    h = _rms(x, g_ref[...]).astype(BF16)
    q = (jnp.dot(h, wq_ref[...], preferred_element_type=F32) * (dh ** -0.5)).astype(BF16)
    outs = []
    for hh in range(X_HEADS):
        sl = slice(hh * dh, (hh + 1) * dh)
        s = lax.dot_general(q[:, sl], k_ref[:, sl], (((1,), (1,)), ((), ())),
                            preferred_element_type=F32)
        p = jnp.exp(s - jnp.max(s, axis=1, keepdims=True))
        l = jnp.sum(p, axis=1, keepdims=True)
        o = jnp.dot(p.astype(BF16), v_ref[:, sl], preferred_element_type=F32) / l
        outs.append(o.astype(BF16))
    o = jnp.concatenate(outs, axis=1)
    o_ref[...] = x + jnp.dot(o, wo_ref[...], preferred_element_type=F32)


def _mix_cross_kernel(x_ref, ya_ref, yb_ref, gm_ref, wg0, wg1, wg2, wg3, wa_ref, wb_ref, wo_ref,
                      g_ref, wq_ref, k_ref, v_ref, wox_ref, o_ref, x1_ref):
    _mix_kernel(x_ref, ya_ref, yb_ref, gm_ref, (wg0, wg1, wg2, wg3), wa_ref, wb_ref, wo_ref, x1_ref)
    _cross_kernel(x1_ref, g_ref, wq_ref, k_ref, v_ref, wox_ref, o_ref)


def mix_cross(x_all, ya, yb, gm, w_in_bf, wa, wb, wo, g, wq, kx, vx, wox, row0, tm=512):
    t = yb.shape[0]
    assert t % tm == 0 and row0 % tm == 0
    d = x_all.shape[1]
    w = ya.shape[1]
    m = kx.shape[1]
    x0 = row0 // tm
    wc = MB_WIDTH
    gcol0 = w_in_bf.shape[1] - 2 * d
    assert gcol0 % wc == 0 and d == 2 * wc
    gate_specs = [pl.BlockSpec((d, wc), lambda i, j=j: (0, gcol0 // wc + j)) for j in range(4)]
    const = lambda a: pl.BlockSpec(a.shape, lambda i: (0,) * a.ndim)
    kv = pl.BlockSpec((None, m, d), lambda i: (0, 0, 0))
    return pl.pallas_call(
        _mix_cross_kernel,
        grid=(t // tm,),
        in_specs=[
            pl.BlockSpec((tm, d), lambda i: (x0 + i, 0)),
            pl.BlockSpec((tm, w), lambda i: (i, 0)),
            pl.BlockSpec((tm, w), lambda i: (i, 0)),
            const(gm), *gate_specs,
            const(wa), const(wb), const(wo), const(g), const(wq), kv, kv, const(wox),
        ],
        out_specs=pl.BlockSpec((tm, d), lambda i: (i, 0)),
        out_shape=jax.ShapeDtypeStruct((t, d), F32),
        scratch_shapes=[pltpu.VMEM((tm, d), F32)],
        compiler_params=_cparams(("parallel",)),
        name="mix_cross",
    )(x_all, ya, yb, gm, *([w_in_bf] * 4), wa, wb, wo, g, wq, kx, vx, wox)


def _topk_rows(sc, k):
    n = sc.shape[0]
    io = lax.broadcasted_iota(I32, sc.shape, 0).astype(F32)
    vals, ids = [], []
    for _ in range(k):
        m = jnp.max(sc, axis=0, keepdims=True)
        ix = jnp.argmax(sc, axis=0, keepdims=True).astype(F32)
        vals.append(m)
        ids.append(ix)
        sc = jnp.where(io == ix, NEG_INF, sc)
    return jnp.concatenate(vals, axis=0), jnp.concatenate(ids, axis=0).astype(I32)


def _pack_bf16_halves(h):
    bits = lax.bitcast_convert_type(h, I32)
    r = bits + 0x7FFF + (lax.shift_right_logical(bits, 16) & 1)
    half = h.shape[1] // 2
    return lax.shift_right_logical(r[:, :half], 16) | (r[:, half:] & HI_MASK)


def _route_kernel(x_ref, g_ref, wq_ref, sk_ref, hp_ref, idx_ref, w_ref, hb_ref, it_ref, wt_ref):
    p = pl.program_id(1)

    @pl.when(p == 0)
    def _():
        h = _rms(x_ref[...], g_ref[...])
        hp_ref[...] = _pack_bf16_halves(h)
        hb_ref[...] = h.astype(BF16)

    qh = jnp.dot(hb_ref[...], wq_ref[...], preferred_element_type=F32)
    tops = []
    for c in range(2):
        seg = qh[:, c * PEER_HALF:(c + 1) * PEER_HALF]
        sc = lax.dot_general(sk_ref[c], seg, (((1,), (1,)), ((), ())),
                             precision=lax.Precision.HIGHEST, preferred_element_type=F32)
        tops.append(_topk_rows(sc, PEER_TOPK))
    (s0, i0), (s1, i1) = tops
    k = PEER_TOPK
    sub = 8
    tm = s0.shape[1]
    r8 = lax.broadcasted_iota(I32, (sub, tm), 0)
    r16 = lax.broadcasted_iota(I32, (k, tm), 0)
    cand_b = [s0[0:1] + s1, s0[1:2] + s1[:sub]]
    cidx_b = [i0[0:1] * PEER_NKEYS + i1, i0[1:2] * PEER_NKEYS + i1[:sub]]
    pos_b = [r16, k + r8]
    for a in range(2, sub):
        keep = r8 < (k // (a + 1))
        cand_b.append(jnp.where(keep, s0[a:a + 1] + s1[:sub], NEG_INF))
        cidx_b.append(i0[a:a + 1] * PEER_NKEYS + i1[:sub])
        pos_b.append(a * k + r8)
    cand_b.append(s0[sub:] + s1[0:1])
    cidx_b.append(i0[sub:] * PEER_NKEYS + i1[0:1])
    pos_b.append((sub + r8) * k)
    cand = jnp.concatenate(cand_b, axis=0)
    cidx = jnp.concatenate(cidx_b, axis=0)
    pos = jnp.concatenate(pos_b, axis=0).astype(F32)
    vals, ids = [], []
    for _ in range(k):
        m = jnp.max(cand, axis=0, keepdims=True)
        px = jnp.min(jnp.where(cand == m, pos, float(k * k)), axis=0, keepdims=True)
        hit = pos == px
        vals.append(m)
        ids.append(jnp.sum(jnp.where(hit, cidx, 0), axis=0, keepdims=True))
        cand = jnp.where(hit, NEG_INF, cand)
    sf = jnp.concatenate(vals, axis=0)
    e = jnp.exp(sf - sf[0:1])
    rows = pl.ds(pl.multiple_of(p * PEER_TOPK, PEER_TOPK), PEER_TOPK)
    wt_ref[rows, :] = e / jnp.sum(e, axis=0, keepdims=True)
    it_ref[rows, :] = jnp.concatenate(ids, axis=0)

    @pl.when(p == pl.num_programs(1) - 1)
    def _():
        idx_ref[...] = it_ref[...].T
        w_ref[...] = wt_ref[...].T


def peer_route(x2d, g, wq, sk, tok0, t, tm=1024):
    assert t % tm == 0 and tok0 % tm == 0
    d = x2d.shape[1]
    ph = sk.shape[0]
    nsel = ph * PEER_TOPK
    blk0 = tok0 // tm
    return pl.pallas_call(
        _route_kernel,
        grid=(t // tm, ph),
        in_specs=[
            pl.BlockSpec((tm, d), lambda i, p: (blk0 + i, 0)),
            pl.BlockSpec((1, d), lambda i, p: (0, 0)),
            pl.BlockSpec((d, 2 * PEER_HALF), lambda i, p: (0, p)),
            pl.BlockSpec((None, 2, PEER_NKEYS, PEER_HALF), lambda i, p: (p, 0, 0, 0)),
        ],
        out_specs=[
            pl.BlockSpec((tm, d // 2), lambda i, p: (i, 0)),
            pl.BlockSpec((tm, nsel), lambda i, p: (i, 0)),
            pl.BlockSpec((tm, nsel), lambda i, p: (i, 0)),
        ],
        out_shape=[jax.ShapeDtypeStruct((t, d // 2), I32),
                   jax.ShapeDtypeStruct((t, nsel), I32),
                   jax.ShapeDtypeStruct((t, nsel), F32)],
        scratch_shapes=[pltpu.VMEM((tm, d), BF16),
                        pltpu.VMEM((nsel, tm), I32),
                        pltpu.VMEM((nsel, tm), F32)],
        compiler_params=_cparams(("parallel", "arbitrary")),
        name="peer_route",
    )(x2d, g, wq, sk)


def _final_kernel(x_ref, y_ref, g_ref, *rest):
    o_ref = rest[-1]
    o_ref[...] = _rms(x_ref[...] + y_ref[...], g_ref[...])


def final_norm_into(out, xs, y, g, row0, total, tm=512):
    t, d = y.shape
    assert t % tm == 0 and row0 % tm == 0 and total % tm == 0
    blk0 = row0 // tm
    spec = pl.BlockSpec((tm, d), lambda i: (i, 0))
    in_specs = [spec, spec, pl.BlockSpec((1, d), lambda i: (0, 0))]
    args = [xs, y, g]
    aliases = {}
    if out is not None:
        in_specs.append(pl.BlockSpec(memory_space=pl.ANY))
        args.append(out)
        aliases = {3: 0}
    return pl.pallas_call(
        _final_kernel, grid=(t // tm,),
        in_specs=in_specs,
        out_specs=pl.BlockSpec((tm, d), lambda i: (blk0 + i, 0)),
        out_shape=jax.ShapeDtypeStruct((total, d), F32),
        input_output_aliases=aliases,
        compiler_params=_cparams(("parallel",)), name="final_norm",
    )(*args)


SC_CORES = 2
SC_SUBCORES = 16
SC_WORKERS = SC_CORES * SC_SUBCORES
SC_LANES = 16
SC_GROUP = 32


def _sc_mesh():
    return plsc.VectorSubcoreMesh(core_axis_name="c", subcore_axis_name="s")


def _sc_params():
    return pltpu.CompilerParams(needs_layout_passes=False)


def _sc_worker_id():
    return lax.axis_index("s") * SC_CORES + lax.axis_index("c")


SC_ROW_LANE = 128


def _sc_unit_off(u):
    off = u * SC_LANES
    return off if isinstance(off, int) else pl.multiple_of(off, SC_LANES)


GELU_C0 = math.sqrt(2.0 / math.pi)
GELU_C1 = 0.044715


def _gelu_tanh(x):
    z = GELU_C0 * (x + GELU_C1 * (x * x * x))
    th = 1.0 - 2.0 / (jnp.exp(2.0 * z) + 1.0)
    return 0.5 * x * (1.0 + th)


SC_PK_RING = 4
SC_PK_SUB = 4
HI_MASK = -65536


def _pack_tables_kernel(u_ref, v_ref, o_ref):
    for part, ref in enumerate((u_ref, v_ref)):
        words = _pack_bf16_halves(ref[...])
        for sub in range(SC_PK_SUB):
            o_ref[:, part * SC_PK_SUB + sub, :] = words[:, sub * SC_ROW_LANE:(sub + 1) * SC_ROW_LANE]


def pack_expert_tables(u, v, te=512):
    e, d = u.shape
    assert d == 2 * SC_PK_SUB * SC_ROW_LANE
    spec = pl.BlockSpec((te, d), lambda i: (i, 0))
    return pl.pallas_call(
        _pack_tables_kernel, grid=(e // te,), in_specs=[spec, spec],
        out_specs=pl.BlockSpec((te, 2 * SC_PK_SUB, SC_ROW_LANE), lambda i: (i, 0, 0)),
        out_shape=jax.ShapeDtypeStruct((e, 2 * SC_PK_SUB, SC_ROW_LANE), I32),
        compiler_params=_cparams(("parallel",)), name="pack_expert_tables",
    )(u, v)


def _unpack_halves(x32):
    w = plsc.bitcast(x32, I32)
    return plsc.bitcast(w << 16, F32), plsc.bitcast(w & HI_MASK, F32)


def _tree_sum(xs):
    while len(xs) > 1:
        xs = [xs[i] + xs[i + 1] for i in range(0, len(xs), 2)]
    return xs[0]


def peer_experts_pk_sc(tab_uv, idx_flat, w_flat, hp, d):
    t = hp.shape[0]
    nsel = PEER_SEL
    assert t % SC_WORKERS == 0 and d == 2 * SC_PK_SUB * SC_ROW_LANE
    tpw = t // SC_WORKERS
    g = SC_GROUP if tpw % SC_GROUP == 0 else SC_GROUP // 2
    assert tpw % g == 0
    groups = tpw // g
    heads = nsel // SC_LANES
    chunks = d // 32
    units = g * heads
    ring = SC_PK_RING
    assert units % ring == 0
    row_buf = pltpu.VMEM((SC_LANES, 2 * SC_PK_SUB, SC_ROW_LANE), I32)

    def row_words(rows, r, wc, sub0):
        per = SC_ROW_LANE // SC_LANES
        return plsc.bitcast(
            rows[r, sub0 + wc // per, pl.ds(pl.multiple_of((wc % per) * SC_LANES, SC_LANES), SC_LANES)], BF16)

    def ring_loop(n_units, start, wait, compute):
        for u in range(ring - 1):
            start(u, u)

        @pl.loop(0, n_units, step=ring)
        def _(uu):
            for b in range(ring):
                u = uu + b
                nxt = u + (ring - 1)

                @pl.when(nxt < n_units)
                def _():
                    start(nxt, (b + ring - 1) % ring)

                wait(u, b)
                compute(u, b)

    @functools.partial(
        pl.kernel, mesh=_sc_mesh(),
        out_type=jax.ShapeDtypeStruct((t, d), F32),
        scratch_types=[
            pltpu.VMEM((g * nsel,), I32),
            pltpu.VMEM((g * nsel,), F32),
            pltpu.VMEM((g, d // 2), I32),
            pltpu.VMEM((g, d), F32),
            pltpu.VMEM((SC_LANES * SC_LANES,), F32),
            [row_buf] * ring,
            [pltpu.SemaphoreType.DMA] * ring,
        ],
        compiler_params=_sc_params(),
        name="peer_experts_pk_sc",
    )
    def k(tab_hbm, idx_hbm, w_hbm, h_hbm, out_hbm, idx_v, coef_v, h_v, y_v, red_v, rows, sems):
        wid = _sc_worker_id()
        lane = lax.iota(I32, SC_LANES)

        def copy(u, slot):
            ids = idx_v.at[pl.ds(_sc_unit_off(u), SC_LANES)]
            return pltpu.make_async_copy(tab_hbm.at[ids], rows[slot], sems[slot])

        def dots(u, slot):
            tt = u // heads

            def body(cp, accs):
                out = []
                hv = [plsc.bitcast(h_v[tt, pl.ds(pl.multiple_of((2 * cp + i) * SC_LANES, SC_LANES), SC_LANES)], BF16)
                      for i in range(2)]
                for r in range(SC_LANES):
                    pr = (row_words(rows[slot], r, 2 * cp, 0) * hv[0]
                          + row_words(rows[slot], r, 2 * cp + 1, 0) * hv[1])
                    lo, hi = _unpack_halves(pr)
                    out.append(accs[r] + lo + hi)
                return tuple(out)

            accs = lax.fori_loop(0, chunks // 2, body,
                                 tuple(jnp.zeros((SC_LANES,), F32) for _ in range(SC_LANES)))
            for r in range(SC_LANES):
                red_v[pl.ds(r * SC_LANES, SC_LANES)] = accs[r]
            act = _tree_sum([plsc.load_gather(red_v, [lane * SC_LANES + j]) for j in range(SC_LANES)])
            sl = pl.ds(_sc_unit_off(u), SC_LANES)
            coef_v[sl] = coef_v[sl] * _gelu_tanh(act)

        def combine(u, slot):
            tt = u // heads
            first = (u % heads) == 0
            cb = []
            for r in range(SC_LANES):
                c = plsc.load_gather(coef_v, [jnp.full((SC_LANES,), u * SC_LANES + r, I32)])
                cb.append(plsc.pack(c, c, format=plsc.PackFormat.INTERLEAVED))

            @plsc.parallel_loop(0, chunks, unroll=2)
            def _(wc):
                lo, hi = _unpack_halves(
                    _tree_sum([cb[r] * row_words(rows[slot], r, wc, SC_PK_SUB) for r in range(SC_LANES)]))
                for half, val in ((0, lo), (1, hi)):
                    sl = pl.ds(pl.multiple_of(half * (d // 2) + wc * SC_LANES, SC_LANES), SC_LANES)
                    y_v[tt, sl] = val + jnp.where(first, 0.0, y_v[tt, sl])

        def unit(u, slot):
            dots(u, slot)
            combine(u, slot)

        @pl.loop(0, groups)
        def _(gi):
            base = wid * tpw + gi * g
            pltpu.sync_copy(idx_hbm.at[pl.ds(base * nsel, g * nsel)], idx_v)
            pltpu.sync_copy(w_hbm.at[pl.ds(base * nsel, g * nsel)], coef_v)
            pltpu.sync_copy(h_hbm.at[pl.ds(base, g)], h_v)
            ring_loop(units, lambda u, s: copy(u, s).start(), lambda u, s: copy(u, s).wait(), unit)
            pltpu.sync_copy(y_v, out_hbm.at[pl.ds(base, g)])

    return k(tab_uv, idx_flat, w_flat, hp)


def kernel(x, mem, rel_bias, ln_mix, w_in, hg_lower, hg_norm, w_up_a, w_up_b, w_out, ln_cross, ln_mem, wq_x, wk_x, wv_x, wo_x, ln_ffn, peer_query, peer_subkeys, peer_u, peer_v, ln_final):
    b, s, d = x.shape
    depth = w_in.shape[0]
    assert depth == 1, "the residual after PEER is fused into the final norm"
    assert s % MB_BLOCK == 0 and s % HG_CHUNK == 0 and s % (PEER_SLICES * SC_WORKERS * SC_GROUP) == 0
    nb = s // MB_BLOCK
    row = lambda a: a.reshape(1, -1).astype(F32)
    lb_all = jnp.cumsum(jax.nn.softmax(hg_lower.astype(F32), axis=0), axis=0)
    bias = moba_bias_tiles(rel_bias)
    n_hg = 4 * HG_WIDTH
    n_qk = 2 * MB_WIDTH
    n_mb = 3 * MB_WIDTH
    l = 0
    w = cast_bf16(w_in[l].astype(F32))
    w_vt = cast_bf16_t(w_in[l].astype(F32), n_hg + n_qk, n_mb - n_qk)
    wa, wb, wo = w_up_a[l].astype(BF16), w_up_b[l].astype(BF16), w_out[l].astype(BF16)
    wqx, wox = wq_x[l].astype(BF16), wo_x[l].astype(BF16)
    wpq, sk = peer_query[l].astype(BF16), peer_subkeys[l].astype(F32)
    tab_uv = pack_expert_tables(peer_u[l].astype(F32), peer_v[l].astype(F32))
    kx, vx = mem_kv(mem, row(ln_mem[l]), wk_x[l].astype(BF16), wv_x[l].astype(BF16))

    x_all = x.reshape(b * s, d)
    outs = []
    for bi in range(b):
        p0, pqk, km, vt = in_proj(x_all, row(ln_mix[l]), w, w_vt, bi * s, s)
        hg_state = jnp.zeros((HG_HEADS, HG_D, HG_D), F32)
        km = km.reshape(1, nb, MB_WIDTH)
        sizes = [s // PEER_SLICES] * PEER_SLICES
        if bi == b - 1:
            first = SC_WORKERS * SC_GROUP // 2
            sizes = [first, sizes[0] - first] + sizes[1:]
        tok0 = 0
        for ts in sizes:
            if bi == b - 1 and tok0 == 0:
                p0_f, pqk_f, km_f, vt_f = in_proj(x_all, row(ln_mix[l]), w, w_vt, bi * s, ts)
                ya, hg_state = hgrn2(p0_f, row(lb_all[l]), row(hg_norm[l]), hg_state, 0, ts)
                yb = moba_attention(pqk_f, vt_f, km_f.reshape(1, ts // MB_BLOCK, MB_WIDTH), bias, 1, ts)
            else:
                ya, hg_state = hgrn2(p0, row(lb_all[l]), row(hg_norm[l]), hg_state, tok0, ts)
                yb = moba_attention(pqk, vt, km, bias, 1, s, tok0 // MB_BLOCK, ts // MB_BLOCK)
            xs = mix_cross(x_all, ya, yb, row(ln_mix[l]), w, wa, wb, wo, row(ln_cross[l]), wqx,
                           kx[bi:bi + 1], vx[bi:bi + 1], wox, bi * s + tok0)
            hp, eidx, wts = peer_route(xs, row(ln_ffn[l]), wpq, sk, 0, ts, math.gcd(ts, 1024))
            y = peer_experts_pk_sc(tab_uv, eidx.reshape(ts * PEER_SEL), wts.reshape(ts * PEER_SEL), hp, d)
            outs.append((xs, y, bi * s + tok0))
            tok0 += ts
    out = None
    for xs, y, row0 in sorted(outs, key=lambda e: (-(e[2] // s), e[2])):
        out = final_norm_into(out, xs, y, row(ln_final), row0, b * s)
    return out.reshape(b, s, d)
```

```python
import functools
import math

import jax
import jax.numpy as jnp
from jax import lax
from jax.experimental import pallas as pl
from jax.experimental.pallas import tpu as pltpu
from jax.experimental.pallas import tpu_sc as plsc

F32 = jnp.float32
BF16 = jnp.bfloat16
I32 = jnp.int32
EPS = 1e-6
NEG_INF = float("-inf")

HG_HEADS = 4
HG_D = 128
HG_WIDTH = HG_HEADS * HG_D
HG_CHUNK = 64
HG_SUB = 16
MB_HEADS = 8
MB_DH = 64
MB_WIDTH = MB_HEADS * MB_DH
MB_BLOCK = 256
MB_TOPK = 3
MB_BIAS_TILES = 8
REL_BUCKETS = 32
REL_MAX_DIST = 2048
X_HEADS = 4
PEER_HEADS = 8
PEER_NKEYS = 128
PEER_TOPK = 16
PEER_HALF = 128
PEER_SEL = PEER_HEADS * PEER_TOPK
PEER_SLICES = 4

VMEM_LIMIT = 56 * 1024 * 1024


def _cparams(sem):
    return pltpu.CompilerParams(dimension_semantics=sem, vmem_limit_bytes=VMEM_LIMIT)


def _rms(x, g):
    ms = jnp.mean(x * x, axis=-1, keepdims=True)
    return x * lax.rsqrt(ms + EPS) * g


def _cast_t_kernel(w_ref, o_ref):
    o_ref[...] = w_ref[...].T.astype(o_ref.dtype)


def cast_bf16_t(w, col0, n):
    k = w.shape[0]
    assert col0 % n == 0
    return pl.pallas_call(
        _cast_t_kernel, grid=(1,),
        in_specs=[pl.BlockSpec((k, n), lambda i: (0, col0 // n))],
        out_specs=pl.BlockSpec((n, k), lambda i: (0, 0)),
        out_shape=jax.ShapeDtypeStruct((n, k), BF16),
        compiler_params=_cparams(("arbitrary",)), name="cast_bf16_t",
    )(w)


def _cast_kernel(w_ref, o_ref):
    o_ref[...] = w_ref[...].astype(o_ref.dtype)


def cast_bf16(w, tn=512):
    k, n = w.shape
    assert n % tn == 0
    spec = pl.BlockSpec((k, tn), lambda j: (0, j))
    return pl.pallas_call(
        _cast_kernel, grid=(n // tn,), in_specs=[spec], out_specs=spec,
        out_shape=jax.ShapeDtypeStruct((k, n), BF16),
        compiler_params=_cparams(("parallel",)), name="cast_bf16",
    )(w)


def _in_proj_kernel(x_ref, g_ref, w0_ref, w1_ref, wvt_ref, o0_ref, o1_ref, okm_ref, ovt_ref):
    h = _rms(x_ref[...], g_ref[...]).astype(BF16)
    o0_ref[...] = jnp.dot(h, w0_ref[...], preferred_element_type=F32)
    qk = jnp.dot(h, w1_ref[...], preferred_element_type=F32)
    o1_ref[...] = qk.astype(BF16)
    okm_ref[0] = jnp.mean(qk[:, MB_WIDTH:], axis=0, keepdims=True)
    vt = lax.dot_general(wvt_ref[...], h, (((1,), (1,)), ((), ())), preferred_element_type=F32).astype(BF16)
    for hd in range(MB_HEADS):
        ovt_ref[0, hd * MB_VROWS:hd * MB_VROWS + MB_DH, :] = vt[hd * MB_DH:(hd + 1) * MB_DH]
        ovt_ref[0, hd * MB_VROWS + MB_DH:(hd + 1) * MB_VROWS, :] = jnp.ones((MB_ONES, vt.shape[1]), BF16)


def in_proj(x_all, g, w, wvt, row0, t):
    d = x_all.shape[1]
    tm = MB_BLOCK
    assert t % tm == 0 and row0 % tm == 0
    blk0 = row0 // tm
    n0, n1, nv = 4 * HG_WIDTH, 2 * MB_WIDTH, MB_VT_ROWS
    assert wvt.shape == (MB_WIDTH, d) and w.shape == (d, n0 + n1 + MB_WIDTH + 2 * d)
    assert n0 % n1 == 0
    full = lambda a: pl.BlockSpec(a.shape, lambda i: (0, 0))
    w_specs = [pl.BlockSpec((d, n0), lambda i: (0, 0)), pl.BlockSpec((d, n1), lambda i: (0, n0 // n1)), full(wvt)]
    return pl.pallas_call(
        _in_proj_kernel,
        grid=(t // tm,),
        in_specs=[pl.BlockSpec((tm, d), lambda i: (blk0 + i, 0)), full(g)] + w_specs,
        out_specs=[pl.BlockSpec((tm, n0), lambda i: (i, 0)),
                   pl.BlockSpec((tm, n1), lambda i: (i, 0)),
                   pl.BlockSpec((1, 1, MB_WIDTH), lambda i: (i, 0, 0)),
                   pl.BlockSpec((1, nv, tm), lambda i: (i, 0, 0))],
        out_shape=[jax.ShapeDtypeStruct((t, n0), F32),
                   jax.ShapeDtypeStruct((t, n1), BF16),
                   jax.ShapeDtypeStruct((t // tm, 1, MB_WIDTH), F32),
                   jax.ShapeDtypeStruct((t // tm, nv, tm), BF16)],
        compiler_params=_cparams(("parallel",)),
        name="in_proj",
    )(x_all, g, w, w, wvt)


def _hgrn_kernel(q_ref, f_ref, i_ref, g_ref, lb_ref, gain_ref, st0_ref, o_ref, stn_ref, st_ref):
    c = pl.program_id(0)

    @pl.when(c == 0)
    def _():
        st_ref[...] = st0_ref[...]

    C, S = HG_CHUNK, HG_SUB
    row = lax.broadcasted_iota(I32, (C, C), 0)
    col = lax.broadcasted_iota(I32, (C, C), 1)
    tril = (row >= col).astype(F32)
    t_iota = lax.broadcasted_iota(I32, (S, 1), 0)

    for h in range(HG_HEADS):
        sl = slice(h * HG_D, (h + 1) * HG_D)
        q = q_ref[:, sl]
        v = i_ref[:, sl]
        lb = lb_ref[:, sl]
        f = lb + (1.0 - lb) * jax.nn.sigmoid(f_ref[:, sl])
        lf = jnp.log(f)
        k = 1.0 - f
        b = jnp.dot(tril, lf, precision=lax.Precision.HIGHEST, preferred_element_type=F32)
        st = st_ref[h]
        vb = v.astype(BF16)
        qd = (q * jnp.exp(b)).astype(BF16)
        o_inter = lax.dot_general(qd, st.astype(BF16), (((1,), (1,)), ((), ())),
                                  preferred_element_type=F32)
        outs = []
        for i in range(C // S):
            r0 = i * S
            qi = q[r0:r0 + S]
            ki = k[r0:r0 + S]
            bi = b[r0:r0 + S]
            vi = v[r0:r0 + S]
            oi = o_inter[r0:r0 + S]
            if i > 0:
                bs = b[r0 - 1:r0]
                qh = (qi * jnp.exp(bi - bs)).astype(BF16)
                kh = (k[:r0] * jnp.exp(bs - b[:r0])).astype(BF16)
                a = lax.dot_general(qh, kh, (((1,), (1,)), ((), ())), preferred_element_type=F32)
                oi = oi + jnp.dot(a.astype(BF16), vb[:r0], preferred_element_type=F32)
            half = S // 2
            o_half = [oi[:half], oi[half:]]
            for s in range(S):
                for hf in range(s // half, 2):
                    rows = slice(hf * half, (hf + 1) * half)
                    dec = jnp.exp(jnp.minimum(bi[rows] - bi[s:s + 1], 0.0))
                    a_s = jnp.sum(qi[rows] * ki[s:s + 1] * dec, axis=-1, keepdims=True)
                    a_s = jnp.where(t_iota[rows] >= s, a_s, 0.0)
                    o_half[hf] = o_half[hf] + a_s * vi[s:s + 1]
            outs.extend(o_half)
        o = jnp.concatenate(outs, axis=0)
        b_end = b[C - 1:C]
        kd = (k * jnp.exp(b_end - b)).astype(BF16)
        upd = lax.dot_general(vb, kd, (((0,), (0,)), ((), ())), preferred_element_type=F32)
        st_ref[h] = st * jnp.exp(b_end) + upd
        o = o * lax.rsqrt(jnp.mean(o * o, axis=-1, keepdims=True) + EPS)
        g = g_ref[:, sl]
        o_ref[:, sl] = (o * gain_ref[:, sl] * (g * jax.nn.sigmoid(g))).astype(o_ref.dtype)

    @pl.when(c == pl.num_programs(0) - 1)
    def _():
        stn_ref[...] = st_ref[...]


def hgrn2(p0, lb, gain, state, tok0, t):
    assert t % HG_CHUNK == 0 and tok0 % HG_CHUNK == 0
    nc = t // HG_CHUNK
    c0 = tok0 // HG_CHUNK
    w = HG_WIDTH

    def col(j):
        return pl.BlockSpec((HG_CHUNK, w), lambda c, j=j: (c0 + c, j))

    st_spec = pl.BlockSpec(state.shape, lambda c: (0, 0, 0))
    return pl.pallas_call(
        _hgrn_kernel,
        grid=(nc,),
        in_specs=[col(0), col(1), col(2), col(3),
                  pl.BlockSpec((1, w), lambda c: (0, 0)),
                  pl.BlockSpec((1, w), lambda c: (0, 0)),
                  st_spec],
        out_specs=[pl.BlockSpec((HG_CHUNK, w), lambda c: (c, 0)), st_spec],
        out_shape=[jax.ShapeDtypeStruct((t, w), BF16), jax.ShapeDtypeStruct(state.shape, F32)],
        scratch_shapes=[pltpu.VMEM((HG_HEADS, HG_D, HG_D), F32)],
        compiler_params=_cparams(("arbitrary",)),
        name="hgrn2",
    )(p0, p0, p0, p0, lb, gain, state)


MB_PAIR = 4
MB_PW = MB_PAIR * MB_DH
MB_LG = 128
MB_ONES = 16
MB_VROWS = MB_DH + MB_ONES
MB_VT_ROWS = MB_HEADS * MB_VROWS


def _moba_kernel(q_ref, k_ref, vt_ref, km_ref, bias_ref, o_ref, *scratch, qb0):
    m_ref, l_ref, al_ref, acc_ref, msk_ref, s_ref, p_ref = (
        scratch[i * MB_PAIR:(i + 1) * MB_PAIR] for i in range(7))
    qi = pl.program_id(2) + qb0
    nb = km_ref.shape[0]
    blk = MB_BLOCK
    heads = range(MB_PAIR)
    grp = lambda hh: slice((hh // 2) * MB_LG, (hh // 2 + 1) * MB_LG)
    q = q_ref[...]
    lane = lax.broadcasted_iota(I32, (blk, MB_LG), 1)
    in_head = [(lane < MB_DH) if hh % 2 == 0 else (lane >= MB_DH) for hh in heads]
    qs = q * jnp.asarray(MB_DH ** -0.5, BF16)
    nt = (((1,), (1,)), ((), ()))
    qf = q.astype(F32)
    qht = [jnp.where(in_head[hh], qs[:, grp(hh)].astype(F32), 0.0).T.astype(BF16) for hh in heads]

    n_io = lax.broadcasted_iota(I32, (nb, blk), 0)
    for hh in heads:
        gate = lax.dot_general(km_ref[:, grp(hh)], jnp.where(in_head[hh], qf[:, grp(hh)], 0.0), nt,
                               precision=lax.Precision.HIGHEST, preferred_element_type=F32)
        gate = jnp.where(n_io < qi, gate, NEG_INF)
        chosen = n_io < 0
        for _ in range(MB_TOPK):
            mx = jnp.max(gate, axis=0, keepdims=True)
            ix = jnp.min(jnp.where(gate == mx, n_io, nb), axis=0, keepdims=True)
            hit = n_io == ix
            chosen = chosen | (hit & (mx > NEG_INF))
            gate = jnp.where(hit, NEG_INF, gate)
        msk_ref[hh][...] = jnp.where(chosen, 0.0, NEG_INF)

    vrows = lambda hh: slice(hh * MB_VROWS, (hh + 1) * MB_VROWS)

    def pv_stage(blk_idx):
        vtb = vt_ref[blk_idx]
        r = [jnp.dot(vtb[vrows(hh)], p_ref[hh][...], preferred_element_type=F32) for hh in heads]
        al = [al_ref[hh][...] for hh in heads]
        a_new = [al[hh] * acc_ref[hh][...] + r[hh][:MB_DH] for hh in heads]
        l_new = [al[hh] * l_ref[hh][...] + r[hh][MB_DH:MB_DH + 1] for hh in heads]
        return a_new, l_new

    def store_pv(a_new, l_new):
        for hh in heads:
            acc_ref[hh][...] = a_new[hh]
            l_ref[hh][...] = l_new[hh]

    def softmax_stage():
        s = [s_ref[hh][...] for hh in heads]
        m_old = [m_ref[hh][...] for hh in heads]
        m_new = [jnp.maximum(m_old[hh], jnp.max(s[hh], axis=0, keepdims=True)) for hh in heads]
        alpha = [jnp.exp(m_old[hh] - m_new[hh]) for hh in heads]
        p = [jnp.exp((s[hh] - m_new[hh]).astype(BF16)) for hh in heads]
        return p, alpha, m_new

    def store_softmax(p, alpha, m_new):
        for hh in heads:
            p_ref[hh][...] = p[hh]
            al_ref[hh][...] = alpha[hh]
            m_ref[hh][...] = m_new[hh]

    k_own = k_ref[pl.ds(pl.multiple_of(qi * blk, blk), blk), :]
    key_io = lax.broadcasted_iota(I32, (blk, blk), 0)
    qry_io = lax.broadcasted_iota(I32, (blk, blk), 1)
    for hh in heads:
        s = jnp.dot(k_own[:, grp(hh)], qht[hh], preferred_element_type=F32) + bias_ref[hh, 0]
        s_ref[hh][...] = jnp.where(key_io <= qry_io, s, NEG_INF)
        m_ref[hh][...] = jnp.full((1, blk), NEG_INF, F32)
        l_ref[hh][...] = jnp.zeros((1, blk), F32)
        al_ref[hh][...] = jnp.ones((1, blk), F32)
        acc_ref[hh][...] = jnp.zeros((MB_DH, blk), F32)
        p_ref[hh][...] = jnp.zeros((blk, blk), BF16)

    def step(i, carry, far):
        pv = pv_stage(jnp.where(i <= 1, qi, i - 2))
        sm = softmax_stage()
        kn = k_ref[pl.ds(pl.multiple_of(i * blk, blk), blk), :]
        if far:
            row = [msk_ref[hh][pl.ds(i, 1), :] + bias_ref[hh, MB_BIAS_TILES - 1, 0:1, 0:1] for hh in heads]
            s_next = [jnp.dot(kn[:, grp(hh)], qht[hh], preferred_element_type=F32) + row[hh] for hh in heads]
        else:
            d = qi - i
            s_next = [jnp.dot(kn[:, grp(hh)], qht[hh], preferred_element_type=F32)
                      + bias_ref[hh, d] + msk_ref[hh][pl.ds(i, 1), :] for hh in heads]
        store_pv(*pv)
        for hh in heads:
            s_ref[hh][...] = s_next[hh]
        store_softmax(*sm)
        return carry

    n_far = jnp.maximum(qi - (MB_BIAS_TILES - 2), 0)
    lax.fori_loop(0, n_far, functools.partial(step, far=True), 0)
    lax.fori_loop(n_far, qi, functools.partial(step, far=False), 0)
    pv = pv_stage(jnp.where(qi <= 1, qi, qi - 2))
    sm = softmax_stage()
    store_pv(*pv)
    store_softmax(*sm)
    a_fin, l_fin = pv_stage(jnp.where(qi == 0, qi, qi - 1))
    out_t = jnp.concatenate([a_fin[hh] / l_fin[hh] for hh in heads], axis=0)
    o_ref[...] = out_t.T.astype(o_ref.dtype)


def moba_attention(pqk, vt, km, bias, batch, seq, qb0=0, nqb=None):
    nb = seq // MB_BLOCK
    nqb = nb if nqb is None else nqb
    t = batch * nqb * MB_BLOCK
    groups = MB_WIDTH // MB_PW
    nkb = qb0 + nqb if batch == 1 else nb
    return pl.pallas_call(
        functools.partial(_moba_kernel, qb0=qb0),
        grid=(batch, groups, nqb),
        in_specs=[
            pl.BlockSpec((MB_BLOCK, MB_PW), lambda b, j, i: (b * nb + qb0 + i, j)),
            pl.BlockSpec((nkb * MB_BLOCK, MB_PW), lambda b, j, i: (b, groups + j)),
            pl.BlockSpec((nkb, MB_PAIR * MB_VROWS, MB_BLOCK), lambda b, j, i: (b, j, 0)),
            pl.BlockSpec((None, nb, MB_PW), lambda b, j, i: (b, 0, j)),
            pl.BlockSpec((MB_PAIR, MB_BIAS_TILES, MB_BLOCK, MB_BLOCK), lambda b, j, i: (j, 0, 0, 0)),
        ],
        out_specs=pl.BlockSpec((MB_BLOCK, MB_PW), lambda b, j, i: (b * nqb + i, j)),
        out_shape=jax.ShapeDtypeStruct((t, MB_WIDTH), BF16),
        scratch_shapes=(
            [pltpu.VMEM((1, MB_BLOCK), F32)] * (3 * MB_PAIR)
            + [pltpu.VMEM((MB_DH, MB_BLOCK), F32)] * MB_PAIR
            + [pltpu.VMEM((nb, MB_BLOCK), F32)] * MB_PAIR
            + [pltpu.VMEM((MB_BLOCK, MB_BLOCK), F32)] * MB_PAIR
            + [pltpu.VMEM((MB_BLOCK, MB_BLOCK), BF16)] * MB_PAIR
        ),
        compiler_params=_cparams(("parallel", "parallel", "arbitrary")),
        name="moba_attn",
    )(pqk, pqk, vt, km, bias)


def _t5_bucket(dist):
    max_exact = REL_BUCKETS // 2
    scaled = jnp.log(jnp.maximum(dist, 1).astype(F32) / max_exact) / math.log(REL_MAX_DIST / max_exact)
    large = jnp.minimum(max_exact + (scaled * (REL_BUCKETS - max_exact)).astype(I32), REL_BUCKETS - 1)
    return jnp.where(dist < max_exact, dist, large)


def moba_bias_tiles(rel_bias):
    blk = MB_BLOCK
    span = 2 * blk - 1
    x = jnp.arange(span) - (blk - 1)
    dist = jnp.maximum(jnp.arange(MB_BIAS_TILES)[:, None] * blk + x[None, :], 0)
    w = rel_bias.astype(F32).T[:, _t5_bucket(dist)]
    h = w.shape[0]
    wp = jnp.pad(w, ((0, 0), (0, 0), (0, 1)))[:, :, None, :]
    return pl.pallas_call(
        _toeplitz_kernel,
        grid=(h, MB_BIAS_TILES),
        in_specs=[pl.BlockSpec((None, None, 1, 2 * blk), lambda i, j: (i, j, 0, 0))],
        out_specs=pl.BlockSpec((None, None, blk, blk), lambda i, j: (i, j, 0, 0)),
        out_shape=jax.ShapeDtypeStruct((h, MB_BIAS_TILES, blk, blk), F32),
        compiler_params=_cparams(("parallel", "parallel")),
        name="moba_bias_tiles",
    )(wp)


def _toeplitz_kernel(w_ref, o_ref):
    blk = o_ref.shape[0]
    x = jnp.broadcast_to(w_ref[...], (blk, 2 * blk))
    o_ref[...] = pltpu.roll(x, 1, 1, stride=1, stride_axis=0)[:, blk:]


def _mix_kernel(x_ref, ya_ref, yb_ref, gm_ref, wg_refs, wa_ref, wb_ref, wo_ref, o_ref):
    x = x_ref[...]
    h = _rms(x, gm_ref[...]).astype(BF16)
    za = jnp.dot(ya_ref[...], wa_ref[...], preferred_element_type=F32)
    zb = jnp.dot(yb_ref[...], wb_ref[...], preferred_element_type=F32)
    nw = len(wg_refs) // 2
    wc = wg_refs[0].shape[1]
    gate = lambda ref: jax.nn.sigmoid(jnp.dot(h, ref[...], preferred_element_type=F32))
    z = jnp.concatenate([gate(wg_refs[j]) * za[:, j * wc:(j + 1) * wc]
                         + gate(wg_refs[nw + j]) * zb[:, j * wc:(j + 1) * wc] for j in range(nw)], axis=1)
    o_ref[...] = x + jnp.dot(z.astype(BF16), wo_ref[...], preferred_element_type=F32)


def _mem_kv_kernel(m_ref, g_ref, wk_ref, wv_ref, k_ref, v_ref):
    mn = _rms(m_ref[...], g_ref[...]).astype(BF16)
    k_ref[...] = jnp.dot(mn, wk_ref[...], preferred_element_type=F32).astype(BF16)
    v_ref[...] = jnp.dot(mn, wv_ref[...], preferred_element_type=F32).astype(BF16)


def mem_kv(mem, g, wk, wv):
    b, m, d = mem.shape
    spec = pl.BlockSpec((None, m, d), lambda i: (i, 0, 0))
    wspec = pl.BlockSpec((d, d), lambda i: (0, 0))
    return pl.pallas_call(
        _mem_kv_kernel,
        grid=(b,),
        in_specs=[spec, pl.BlockSpec((1, d), lambda i: (0, 0)), wspec, wspec],
        out_specs=[spec, spec],
        out_shape=[jax.ShapeDtypeStruct((b, m, d), BF16)] * 2,
        compiler_params=_cparams(("parallel",)),
        name="mem_kv",
    )(mem, g, wk, wv)


def _cross_kernel(x_ref, g_ref, wq_ref, k_ref, v_ref, wo_ref, o_ref):
    x = x_ref[...]
    d = x.shape[1]
    dh = d // X_HEADS
    h = _rms(x, g_ref[...]).astype(BF16)
    q = (jnp.dot(h, wq_ref[...], preferred_element_type=F32) * (dh ** -0.5)).astype(BF16)
    outs = []
    for hh in range(X_HEADS):
        sl = slice(hh * dh, (hh + 1) * dh)
        s = lax.dot_general(q[:, sl], k_ref[:, sl], (((1,), (1,)), ((), ())),
                            preferred_element_type=F32)
        p = jnp.exp(s - jnp.max(s, axis=1, keepdims=True))
        l = jnp.sum(p, axis=1, keepdims=True)
        o = jnp.dot(p.astype(BF16), v_ref[:, sl], preferred_element_type=F32) / l
        outs.append(o.astype(BF16))
    o = jnp.concatenate(outs, axis=1)
    o_ref[...] = x + jnp.dot(o, wo_ref[...], preferred_element_type=F32)


def _mix_cross_kernel(x_ref, ya_ref, yb_ref, gm_ref, wg0, wg1, wg2, wg3, wa_ref, wb_ref, wo_ref,
                      g_ref, wq_ref, k_ref, v_ref, wox_ref, o_ref, x1_ref):
    _mix_kernel(x_ref, ya_ref, yb_ref, gm_ref, (wg0, wg1, wg2, wg3), wa_ref, wb_ref, wo_ref, x1_ref)
    _cross_kernel(x1_ref, g_ref, wq_ref, k_ref, v_ref, wox_ref, o_ref)


def mix_cross(x_all, ya, yb, gm, w_in_bf, wa, wb, wo, g, wq, kx, vx, wox, row0, tm=512):
    t = yb.shape[0]
    assert t % tm == 0 and row0 % tm == 0
    d = x_all.shape[1]
    w = ya.shape[1]
    m = kx.shape[1]
    x0 = row0 // tm
    wc = MB_WIDTH
    gcol0 = w_in_bf.shape[1] - 2 * d
    assert gcol0 % wc == 0 and d == 2 * wc
    gate_specs = [pl.BlockSpec((d, wc), lambda i, j=j: (0, gcol0 // wc + j)) for j in range(4)]
    const = lambda a: pl.BlockSpec(a.shape, lambda i: (0,) * a.ndim)
    kv = pl.BlockSpec((None, m, d), lambda i: (0, 0, 0))
    return pl.pallas_call(
        _mix_cross_kernel,
        grid=(t // tm,),
        in_specs=[
            pl.BlockSpec((tm, d), lambda i: (x0 + i, 0)),
            pl.BlockSpec((tm, w), lambda i: (i, 0)),
            pl.BlockSpec((tm, w), lambda i: (i, 0)),
            const(gm), *gate_specs,
            const(wa), const(wb), const(wo), const(g), const(wq), kv, kv, const(wox),
        ],
        out_specs=pl.BlockSpec((tm, d), lambda i: (i, 0)),
        out_shape=jax.ShapeDtypeStruct((t, d), F32),
        scratch_shapes=[pltpu.VMEM((tm, d), F32)],
        compiler_params=_cparams(("parallel",)),
        name="mix_cross",
    )(x_all, ya, yb, gm, *([w_in_bf] * 4), wa, wb, wo, g, wq, kx, vx, wox)


def _topk_rows(sc, k):
    n = sc.shape[0]
    io = lax.broadcasted_iota(I32, sc.shape, 0).astype(F32)
    vals, ids = [], []
    for _ in range(k):
        m = jnp.max(sc, axis=0, keepdims=True)
        ix = jnp.argmax(sc, axis=0, keepdims=True).astype(F32)
        vals.append(m)
        ids.append(ix)
        sc = jnp.where(io == ix, NEG_INF, sc)
    return jnp.concatenate(vals, axis=0), jnp.concatenate(ids, axis=0).astype(I32)


def _pack_bf16_halves(h):
    bits = lax.bitcast_convert_type(h, I32)
    r = bits + 0x7FFF + (lax.shift_right_logical(bits, 16) & 1)
    half = h.shape[1] // 2
    return lax.shift_right_logical(r[:, :half], 16) | (r[:, half:] & HI_MASK)


def _route_kernel(x_ref, g_ref, wq_ref, sk_ref, hp_ref, idx_ref, w_ref, hb_ref, it_ref, wt_ref):
    p = pl.program_id(1)

    @pl.when(p == 0)
    def _():
        h = _rms(x_ref[...], g_ref[...])
        hp_ref[...] = _pack_bf16_halves(h)
        hb_ref[...] = h.astype(BF16)

    qh = jnp.dot(hb_ref[...], wq_ref[...], preferred_element_type=F32)
    tops = []
    for c in range(2):
        seg = qh[:, c * PEER_HALF:(c + 1) * PEER_HALF]
        sc = lax.dot_general(sk_ref[c], seg, (((1,), (1,)), ((), ())),
                             precision=lax.Precision.HIGHEST, preferred_element_type=F32)
        tops.append(_topk_rows(sc, PEER_TOPK))
    (s0, i0), (s1, i1) = tops
    k = PEER_TOPK
    sub = 8
    tm = s0.shape[1]
    r8 = lax.broadcasted_iota(I32, (sub, tm), 0)
    r16 = lax.broadcasted_iota(I32, (k, tm), 0)
    cand_b = [s0[0:1] + s1, s0[1:2] + s1[:sub]]
    cidx_b = [i0[0:1] * PEER_NKEYS + i1, i0[1:2] * PEER_NKEYS + i1[:sub]]
    pos_b = [r16, k + r8]
    for a in range(2, sub):
        keep = r8 < (k // (a + 1))
        cand_b.append(jnp.where(keep, s0[a:a + 1] + s1[:sub], NEG_INF))
        cidx_b.append(i0[a:a + 1] * PEER_NKEYS + i1[:sub])
        pos_b.append(a * k + r8)
    cand_b.append(s0[sub:] + s1[0:1])
    cidx_b.append(i0[sub:] * PEER_NKEYS + i1[0:1])
    pos_b.append((sub + r8) * k)
    cand = jnp.concatenate(cand_b, axis=0)
    cidx = jnp.concatenate(cidx_b, axis=0)
    pos = jnp.concatenate(pos_b, axis=0).astype(F32)
    vals, ids = [], []
    for _ in range(k):
        m = jnp.max(cand, axis=0, keepdims=True)
        px = jnp.min(jnp.where(cand == m, pos, float(k * k)), axis=0, keepdims=True)
        hit = pos == px
        vals.append(m)
        ids.append(jnp.sum(jnp.where(hit, cidx, 0), axis=0, keepdims=True))
        cand = jnp.where(hit, NEG_INF, cand)
    sf = jnp.concatenate(vals, axis=0)
    e = jnp.exp(sf - sf[0:1])
    rows = pl.ds(pl.multiple_of(p * PEER_TOPK, PEER_TOPK), PEER_TOPK)
    wt_ref[rows, :] = e / jnp.sum(e, axis=0, keepdims=True)
    it_ref[rows, :] = jnp.concatenate(ids, axis=0)

    @pl.when(p == pl.num_programs(1) - 1)
    def _():
        idx_ref[...] = it_ref[...].T
        w_ref[...] = wt_ref[...].T


def peer_route(x2d, g, wq, sk, tok0, t, tm=1024):
    assert t % tm == 0 and tok0 % tm == 0
    d = x2d.shape[1]
    ph = sk.shape[0]
    nsel = ph * PEER_TOPK
    blk0 = tok0 // tm
    return pl.pallas_call(
        _route_kernel,
        grid=(t // tm, ph),
        in_specs=[
            pl.BlockSpec((tm, d), lambda i, p: (blk0 + i, 0)),
            pl.BlockSpec((1, d), lambda i, p: (0, 0)),
            pl.BlockSpec((d, 2 * PEER_HALF), lambda i, p: (0, p)),
            pl.BlockSpec((None, 2, PEER_NKEYS, PEER_HALF), lambda i, p: (p, 0, 0, 0)),
        ],
        out_specs=[
            pl.BlockSpec((tm, d // 2), lambda i, p: (i, 0)),
            pl.BlockSpec((tm, nsel), lambda i, p: (i, 0)),
            pl.BlockSpec((tm, nsel), lambda i, p: (i, 0)),
        ],
        out_shape=[jax.ShapeDtypeStruct((t, d // 2), I32),
                   jax.ShapeDtypeStruct((t, nsel), I32),
                   jax.ShapeDtypeStruct((t, nsel), F32)],
        scratch_shapes=[pltpu.VMEM((tm, d), BF16),
                        pltpu.VMEM((nsel, tm), I32),
                        pltpu.VMEM((nsel, tm), F32)],
        compiler_params=_cparams(("parallel", "arbitrary")),
        name="peer_route",
    )(x2d, g, wq, sk)


def _mix_cross_route_kernel(*refs):
    mc_in, (gr_ref, wpq_ref, sk_ref, o_ref, hp_ref, idx_ref, w_ref, x1_ref, hb_ref, it_ref, wt_ref) = refs[:16], refs[16:]

    @pl.when(pl.program_id(1) == 0)
    def _():
        _mix_cross_kernel(*mc_in, o_ref, x1_ref)

    _route_kernel(o_ref, gr_ref, wpq_ref, sk_ref, hp_ref, idx_ref, w_ref, hb_ref, it_ref, wt_ref)


def mix_cross_route(x_all, ya, yb, gm, w_in_bf, wa, wb, wo, g, wq, kx, vx, wox, gr, wpq, sk, row0, tm=512):
    t = yb.shape[0]
    assert t % tm == 0 and row0 % tm == 0
    d = x_all.shape[1]
    wdt = ya.shape[1]
    m = kx.shape[1]
    x0 = row0 // tm
    ph = sk.shape[0]
    nsel = ph * PEER_TOPK
    wc = MB_WIDTH
    gcol0 = w_in_bf.shape[1] - 2 * d
    assert gcol0 % wc == 0 and d == 2 * wc
    gate_specs = [pl.BlockSpec((d, wc), lambda i, p, j=j: (0, gcol0 // wc + j)) for j in range(4)]
    const = lambda a: pl.BlockSpec(a.shape, lambda i, p: (0,) * a.ndim)
    kv = pl.BlockSpec((None, m, d), lambda i, p: (0, 0, 0))
    tile = lambda n: pl.BlockSpec((tm, n), lambda i, p: (i, 0))
    return pl.pallas_call(
        _mix_cross_route_kernel,
        grid=(t // tm, ph),
        in_specs=[
            pl.BlockSpec((tm, d), lambda i, p: (x0 + i, 0)), tile(wdt), tile(wdt),
            const(gm), *gate_specs,
            const(wa), const(wb), const(wo), const(g), const(wq), kv, kv, const(wox),
            const(gr),
            pl.BlockSpec((d, 2 * PEER_HALF), lambda i, p: (0, p)),
            pl.BlockSpec((None, 2, PEER_NKEYS, PEER_HALF), lambda i, p: (p, 0, 0, 0)),
        ],
        out_specs=[tile(d), tile(d // 2), tile(nsel), tile(nsel)],
        out_shape=[jax.ShapeDtypeStruct((t, d), F32),
                   jax.ShapeDtypeStruct((t, d // 2), I32),
                   jax.ShapeDtypeStruct((t, nsel), I32),
                   jax.ShapeDtypeStruct((t, nsel), F32)],
        scratch_shapes=[pltpu.VMEM((tm, d), F32),
                        pltpu.VMEM((tm, d), BF16),
                        pltpu.VMEM((nsel, tm), I32),
                        pltpu.VMEM((nsel, tm), F32)],
        compiler_params=_cparams(("parallel", "arbitrary")),
        name="mix_cross_route",
    )(x_all, ya, yb, gm, *([w_in_bf] * 4), wa, wb, wo, g, wq, kx, vx, wox, gr, wpq, sk)


def _final_kernel(x_ref, y_ref, g_ref, *rest):
    o_ref = rest[-1]
    o_ref[...] = _rms(x_ref[...] + y_ref[...], g_ref[...])


def final_norm_into(out, xs, y, g, row0, total, tm=512):
    t, d = y.shape
    assert t % tm == 0 and row0 % tm == 0 and total % tm == 0
    blk0 = row0 // tm
    spec = pl.BlockSpec((tm, d), lambda i: (i, 0))
    in_specs = [spec, spec, pl.BlockSpec((1, d), lambda i: (0, 0))]
    args = [xs, y, g]
    aliases = {}
    if out is not None:
        in_specs.append(pl.BlockSpec(memory_space=pl.ANY))
        args.append(out)
        aliases = {3: 0}
    return pl.pallas_call(
        _final_kernel, grid=(t // tm,),
        in_specs=in_specs,
        out_specs=pl.BlockSpec((tm, d), lambda i: (blk0 + i, 0)),
        out_shape=jax.ShapeDtypeStruct((total, d), F32),
        input_output_aliases=aliases,
        compiler_params=_cparams(("parallel",)), name="final_norm",
    )(*args)


SC_CORES = 2
SC_SUBCORES = 16
SC_WORKERS = SC_CORES * SC_SUBCORES
SC_LANES = 16
SC_GROUP = 32


def _sc_mesh():
    return plsc.VectorSubcoreMesh(core_axis_name="c", subcore_axis_name="s")


def _sc_params():
    return pltpu.CompilerParams(needs_layout_passes=False)


def _sc_worker_id():
    return lax.axis_index("s") * SC_CORES + lax.axis_index("c")


SC_ROW_LANE = 128


def _sc_unit_off(u):
    off = u * SC_LANES
    return off if isinstance(off, int) else pl.multiple_of(off, SC_LANES)


GELU_C0 = math.sqrt(2.0 / math.pi)
GELU_C1 = 0.044715


def _gelu_tanh(x):
    z = GELU_C0 * (x + GELU_C1 * (x * x * x))
    th = 1.0 - 2.0 / (jnp.exp(2.0 * z) + 1.0)
    return 0.5 * x * (1.0 + th)


SC_PK_RING = 4
SC_PK_SUB = 4
HI_MASK = -65536


def _pack_tables_kernel(u_ref, v_ref, o_ref):
    for part, ref in enumerate((u_ref, v_ref)):
        words = _pack_bf16_halves(ref[...])
        for sub in range(SC_PK_SUB):
            o_ref[:, part * SC_PK_SUB + sub, :] = words[:, sub * SC_ROW_LANE:(sub + 1) * SC_ROW_LANE]


def pack_expert_tables(u, v, te=512):
    e, d = u.shape
    assert d == 2 * SC_PK_SUB * SC_ROW_LANE
    spec = pl.BlockSpec((te, d), lambda i: (i, 0))
    return pl.pallas_call(
        _pack_tables_kernel, grid=(e // te,), in_specs=[spec, spec],
        out_specs=pl.BlockSpec((te, 2 * SC_PK_SUB, SC_ROW_LANE), lambda i: (i, 0, 0)),
        out_shape=jax.ShapeDtypeStruct((e, 2 * SC_PK_SUB, SC_ROW_LANE), I32),
        compiler_params=_cparams(("parallel",)), name="pack_expert_tables",
    )(u, v)


def _unpack_halves(x32):
    w = plsc.bitcast(x32, I32)
    return plsc.bitcast(w << 16, F32), plsc.bitcast(w & HI_MASK, F32)


def _tree_sum(xs):
    while len(xs) > 1:
        xs = [xs[i] + xs[i + 1] for i in range(0, len(xs), 2)]
    return xs[0]


def peer_experts_pk_sc(tab_uv, idx_flat, w_flat, hp, d):
    t = hp.shape[0]
    nsel = PEER_SEL
    assert t % SC_WORKERS == 0 and d == 2 * SC_PK_SUB * SC_ROW_LANE
    tpw = t // SC_WORKERS
    g = SC_GROUP if tpw % SC_GROUP == 0 else SC_GROUP // 2
    assert tpw % g == 0
    groups = tpw // g
    heads = nsel // SC_LANES
    chunks = d // 32
    units = g * heads
    ring = SC_PK_RING
    assert units % ring == 0
    row_buf = pltpu.VMEM((SC_LANES, 2 * SC_PK_SUB, SC_ROW_LANE), I32)

    def row_words(rows, r, wc, sub0):
        per = SC_ROW_LANE // SC_LANES
        return plsc.bitcast(
            rows[r, sub0 + wc // per, pl.ds(pl.multiple_of((wc % per) * SC_LANES, SC_LANES), SC_LANES)], BF16)

    def ring_loop(n_units, start, wait, compute):
        for u in range(ring - 1):
            start(u, u)

        @pl.loop(0, n_units, step=ring)
        def _(uu):
            for b in range(ring):
                u = uu + b
                nxt = u + (ring - 1)

                @pl.when(nxt < n_units)
                def _():
                    start(nxt, (b + ring - 1) % ring)

                wait(u, b)
                compute(u, b)

    @functools.partial(
        pl.kernel, mesh=_sc_mesh(),
        out_type=jax.ShapeDtypeStruct((t, d), F32),
        scratch_types=[
            pltpu.VMEM((g * nsel,), I32),
            pltpu.VMEM((g * nsel,), F32),
            pltpu.VMEM((g, d // 2), I32),
            pltpu.VMEM((g, d), F32),
            pltpu.VMEM((SC_LANES * SC_LANES,), F32),
            [row_buf] * ring,
            [pltpu.SemaphoreType.DMA] * ring,
        ],
        compiler_params=_sc_params(),
        name="peer_experts_pk_sc",
    )
    def k(tab_hbm, idx_hbm, w_hbm, h_hbm, out_hbm, idx_v, coef_v, h_v, y_v, red_v, rows, sems):
        wid = _sc_worker_id()
        lane = lax.iota(I32, SC_LANES)

        def copy(u, slot):
            ids = idx_v.at[pl.ds(_sc_unit_off(u), SC_LANES)]
            return pltpu.make_async_copy(tab_hbm.at[ids], rows[slot], sems[slot])

        def dots(u, slot):
            tt = u // heads

            def body(cp, accs):
                out = []
                hv = [plsc.bitcast(h_v[tt, pl.ds(pl.multiple_of((2 * cp + i) * SC_LANES, SC_LANES), SC_LANES)], BF16)
                      for i in range(2)]
                for r in range(SC_LANES):
                    pr = (row_words(rows[slot], r, 2 * cp, 0) * hv[0]
                          + row_words(rows[slot], r, 2 * cp + 1, 0) * hv[1])
                    lo, hi = _unpack_halves(pr)
                    out.append(accs[r] + lo + hi)
                return tuple(out)

            accs = lax.fori_loop(0, chunks // 2, body,
                                 tuple(jnp.zeros((SC_LANES,), F32) for _ in range(SC_LANES)))
            for r in range(SC_LANES):
                red_v[pl.ds(r * SC_LANES, SC_LANES)] = accs[r]
            act = _tree_sum([plsc.load_gather(red_v, [lane * SC_LANES + j]) for j in range(SC_LANES)])
            sl = pl.ds(_sc_unit_off(u), SC_LANES)
            coef_v[sl] = coef_v[sl] * _gelu_tanh(act)

        def combine(u, slot):
            tt = u // heads
            first = (u % heads) == 0
            cb = []
            for r in range(SC_LANES):
                c = plsc.load_gather(coef_v, [jnp.full((SC_LANES,), u * SC_LANES + r, I32)])
                cb.append(plsc.pack(c, c, format=plsc.PackFormat.INTERLEAVED))

            @plsc.parallel_loop(0, chunks, unroll=2)
            def _(wc):
                lo, hi = _unpack_halves(
                    _tree_sum([cb[r] * row_words(rows[slot], r, wc, SC_PK_SUB) for r in range(SC_LANES)]))
                for half, val in ((0, lo), (1, hi)):
                    sl = pl.ds(pl.multiple_of(half * (d // 2) + wc * SC_LANES, SC_LANES), SC_LANES)
                    y_v[tt, sl] = val + jnp.where(first, 0.0, y_v[tt, sl])

        def unit(u, slot):
            dots(u, slot)
            combine(u, slot)

        @pl.loop(0, groups)
        def _(gi):
            base = wid * tpw + gi * g
            pltpu.sync_copy(idx_hbm.at[pl.ds(base * nsel, g * nsel)], idx_v)
            pltpu.sync_copy(w_hbm.at[pl.ds(base * nsel, g * nsel)], coef_v)
            pltpu.sync_copy(h_hbm.at[pl.ds(base, g)], h_v)
            ring_loop(units, lambda u, s: copy(u, s).start(), lambda u, s: copy(u, s).wait(), unit)
            pltpu.sync_copy(y_v, out_hbm.at[pl.ds(base, g)])

    return k(tab_uv, idx_flat, w_flat, hp)


def kernel(x, mem, rel_bias, ln_mix, w_in, hg_lower, hg_norm, w_up_a, w_up_b, w_out, ln_cross, ln_mem, wq_x, wk_x, wv_x, wo_x, ln_ffn, peer_query, peer_subkeys, peer_u, peer_v, ln_final):
    b, s, d = x.shape
    depth = w_in.shape[0]
    assert depth == 1, "the residual after PEER is fused into the final norm"
    assert s % MB_BLOCK == 0 and s % HG_CHUNK == 0 and s % (PEER_SLICES * SC_WORKERS * SC_GROUP) == 0
    nb = s // MB_BLOCK
    row = lambda a: a.reshape(1, -1).astype(F32)
    lb_all = jnp.cumsum(jax.nn.softmax(hg_lower.astype(F32), axis=0), axis=0)
    bias = moba_bias_tiles(rel_bias)
    n_hg = 4 * HG_WIDTH
    n_qk = 2 * MB_WIDTH
    n_mb = 3 * MB_WIDTH
    l = 0
    w = cast_bf16(w_in[l].astype(F32))
    w_vt = cast_bf16_t(w_in[l].astype(F32), n_hg + n_qk, n_mb - n_qk)
    wa, wb, wo = w_up_a[l].astype(BF16), w_up_b[l].astype(BF16), w_out[l].astype(BF16)
    wqx, wox = wq_x[l].astype(BF16), wo_x[l].astype(BF16)
    wpq, sk = peer_query[l].astype(BF16), peer_subkeys[l].astype(F32)
    tab_uv = pack_expert_tables(peer_u[l].astype(F32), peer_v[l].astype(F32))
    kx, vx = mem_kv(mem, row(ln_mem[l]), wk_x[l].astype(BF16), wv_x[l].astype(BF16))

    x_all = x.reshape(b * s, d)
    outs = []
    for bi in range(b):
        p0, pqk, km, vt = in_proj(x_all, row(ln_mix[l]), w, w_vt, bi * s, s)
        hg_state = jnp.zeros((HG_HEADS, HG_D, HG_D), F32)
        km = km.reshape(1, nb, MB_WIDTH)
        sizes = [s // PEER_SLICES] * PEER_SLICES
        if bi == b - 1:
            first = SC_WORKERS * SC_GROUP // 2
            sizes = [first, sizes[0] - first] + sizes[1:]
        tok0 = 0
        for ts in sizes:
            ya, hg_state = hgrn2(p0, row(lb_all[l]), row(hg_norm[l]), hg_state, tok0, ts)
            yb = moba_attention(pqk, vt, km, bias, 1, s, tok0 // MB_BLOCK, ts // MB_BLOCK)
            xs, hp, eidx, wts = mix_cross_route(x_all, ya, yb, row(ln_mix[l]), w, wa, wb, wo, row(ln_cross[l]), wqx,
                                                kx[bi:bi + 1], vx[bi:bi + 1], wox, row(ln_ffn[l]), wpq, sk,
                                                bi * s + tok0)
            y = peer_experts_pk_sc(tab_uv, eidx.reshape(ts * PEER_SEL), wts.reshape(ts * PEER_SEL), hp, d)
            outs.append((xs, y, bi * s + tok0))
            tok0 += ts
    out = None
    for xs, y, row0 in sorted(outs, key=lambda e: (-(e[2] // s), e[2])):
        out = final_norm_into(out, xs, y, row(ln_final), row0, b * s)
    return out.reshape(b, s, d)
```

```python
import functools
import math

import jax
import jax.numpy as jnp
from jax import lax
from jax.experimental import pallas as pl
from jax.experimental.pallas import tpu as pltpu
from jax.experimental.pallas import tpu_sc as plsc

F32 = jnp.float32
BF16 = jnp.bfloat16
I32 = jnp.int32
EPS = 1e-6
NEG_INF = float("-inf")

HG_HEADS = 4
HG_D = 128
HG_WIDTH = HG_HEADS * HG_D
HG_CHUNK = 64
HG_SUB = 16
MB_HEADS = 8
MB_DH = 64
MB_WIDTH = MB_HEADS * MB_DH
MB_BLOCK = 256
MB_TOPK = 3
MB_BIAS_TILES = 8
REL_BUCKETS = 32
REL_MAX_DIST = 2048
X_HEADS = 4
PEER_HEADS = 8
PEER_NKEYS = 128
PEER_TOPK = 16
PEER_HALF = 128
PEER_SEL = PEER_HEADS * PEER_TOPK
PEER_SLICES = 4

VMEM_LIMIT = 60 * 1024 * 1024


def _cparams(sem):
    return pltpu.CompilerParams(dimension_semantics=sem, vmem_limit_bytes=VMEM_LIMIT)


def _rms(x, g):
    ms = jnp.mean(x * x, axis=-1, keepdims=True)
    return x * lax.rsqrt(ms + EPS) * g


def _cast_t_kernel(w_ref, o_ref):
    o_ref[...] = w_ref[...].T.astype(o_ref.dtype)


def cast_bf16_t(w, col0, n):
    k = w.shape[0]
    assert col0 % n == 0
    return pl.pallas_call(
        _cast_t_kernel, grid=(1,),
        in_specs=[pl.BlockSpec((k, n), lambda i: (0, col0 // n))],
        out_specs=pl.BlockSpec((n, k), lambda i: (0, 0)),
        out_shape=jax.ShapeDtypeStruct((n, k), BF16),
        compiler_params=_cparams(("arbitrary",)), name="cast_bf16_t",
    )(w)


def _cast_kernel(w_ref, o_ref):
    o_ref[...] = w_ref[...].astype(o_ref.dtype)


def cast_bf16(w, tn=512):
    k, n = w.shape
    assert n % tn == 0
    spec = pl.BlockSpec((k, tn), lambda j: (0, j))
    return pl.pallas_call(
        _cast_kernel, grid=(n // tn,), in_specs=[spec], out_specs=spec,
        out_shape=jax.ShapeDtypeStruct((k, n), BF16),
        compiler_params=_cparams(("parallel",)), name="cast_bf16",
    )(w)


def _in_proj_kernel(x_ref, g_ref, w0_ref, w1_ref, wvt_ref, o0_ref, o1_ref, okm_ref, ovt_ref):
    h = _rms(x_ref[...], g_ref[...]).astype(BF16)
    o0_ref[...] = jnp.dot(h, w0_ref[...], preferred_element_type=F32)
    qk = jnp.dot(h, w1_ref[...], preferred_element_type=F32)
    o1_ref[...] = qk.astype(BF16)
    okm_ref[0] = jnp.mean(qk[:, MB_WIDTH:], axis=0, keepdims=True)
    vt = lax.dot_general(wvt_ref[...], h, (((1,), (1,)), ((), ())), preferred_element_type=F32).astype(BF16)
    for hd in range(MB_HEADS):
        ovt_ref[0, hd * MB_VROWS:hd * MB_VROWS + MB_DH, :] = vt[hd * MB_DH:(hd + 1) * MB_DH]
        ovt_ref[0, hd * MB_VROWS + MB_DH:(hd + 1) * MB_VROWS, :] = jnp.ones((MB_ONES, vt.shape[1]), BF16)


def in_proj(x_all, g, w, wvt, row0, t):
    d = x_all.shape[1]
    tm = MB_BLOCK
    assert t % tm == 0 and row0 % tm == 0
    blk0 = row0 // tm
    n0, n1, nv = 4 * HG_WIDTH, 2 * MB_WIDTH, MB_VT_ROWS
    assert wvt.shape == (MB_WIDTH, d) and w.shape == (d, n0 + n1 + MB_WIDTH + 2 * d)
    assert n0 % n1 == 0
    full = lambda a: pl.BlockSpec(a.shape, lambda i: (0, 0))
    w_specs = [pl.BlockSpec((d, n0), lambda i: (0, 0)), pl.BlockSpec((d, n1), lambda i: (0, n0 // n1)), full(wvt)]
    return pl.pallas_call(
        _in_proj_kernel,
        grid=(t // tm,),
        in_specs=[pl.BlockSpec((tm, d), lambda i: (blk0 + i, 0)), full(g)] + w_specs,
        out_specs=[pl.BlockSpec((tm, n0), lambda i: (i, 0)),
                   pl.BlockSpec((tm, n1), lambda i: (i, 0)),
                   pl.BlockSpec((1, 1, MB_WIDTH), lambda i: (i, 0, 0)),
                   pl.BlockSpec((1, nv, tm), lambda i: (i, 0, 0))],
        out_shape=[jax.ShapeDtypeStruct((t, n0), F32),
                   jax.ShapeDtypeStruct((t, n1), BF16),
                   jax.ShapeDtypeStruct((t // tm, 1, MB_WIDTH), F32),
                   jax.ShapeDtypeStruct((t // tm, nv, tm), BF16)],
        compiler_params=_cparams(("parallel",)),
        name="in_proj",
    )(x_all, g, w, w, wvt)


def _hgrn_kernel(q_ref, f_ref, i_ref, g_ref, lb_ref, gain_ref, st0_ref, o_ref, stn_ref, st_ref):
    c = pl.program_id(0)

    @pl.when(c == 0)
    def _():
        st_ref[...] = st0_ref[...]

    C, S = HG_CHUNK, HG_SUB
    row = lax.broadcasted_iota(I32, (C, C), 0)
    col = lax.broadcasted_iota(I32, (C, C), 1)
    tril = (row >= col).astype(F32)
    t_iota = lax.broadcasted_iota(I32, (S, 1), 0)

    for h in range(HG_HEADS):
        sl = slice(h * HG_D, (h + 1) * HG_D)
        q = q_ref[:, sl]
        v = i_ref[:, sl]
        lb = lb_ref[:, sl]
        f = lb + (1.0 - lb) * jax.nn.sigmoid(f_ref[:, sl])
        lf = jnp.log(f)
        k = 1.0 - f
        b = jnp.dot(tril, lf, precision=lax.Precision.HIGHEST, preferred_element_type=F32)
        st = st_ref[h]
        vb = v.astype(BF16)
        qd = (q * jnp.exp(b)).astype(BF16)
        o_inter = lax.dot_general(qd, st.astype(BF16), (((1,), (1,)), ((), ())),
                                  preferred_element_type=F32)
        outs = []
        for i in range(C // S):
            r0 = i * S
            qi = q[r0:r0 + S]
            ki = k[r0:r0 + S]
            bi = b[r0:r0 + S]
            vi = v[r0:r0 + S]
            oi = o_inter[r0:r0 + S]
            if i > 0:
                bs = b[r0 - 1:r0]
                qh = (qi * jnp.exp(bi - bs)).astype(BF16)
                kh = (k[:r0] * jnp.exp(bs - b[:r0])).astype(BF16)
                a = lax.dot_general(qh, kh, (((1,), (1,)), ((), ())), preferred_element_type=F32)
                oi = oi + jnp.dot(a.astype(BF16), vb[:r0], preferred_element_type=F32)
            half = S // 2
            o_half = [oi[:half], oi[half:]]
            for s in range(S):
                for hf in range(s // half, 2):
                    rows = slice(hf * half, (hf + 1) * half)
                    dec = jnp.exp(jnp.minimum(bi[rows] - bi[s:s + 1], 0.0))
                    a_s = jnp.sum(qi[rows] * ki[s:s + 1] * dec, axis=-1, keepdims=True)
                    a_s = jnp.where(t_iota[rows] >= s, a_s, 0.0)
                    o_half[hf] = o_half[hf] + a_s * vi[s:s + 1]
            outs.extend(o_half)
        o = jnp.concatenate(outs, axis=0)
        b_end = b[C - 1:C]
        kd = (k * jnp.exp(b_end - b)).astype(BF16)
        upd = lax.dot_general(vb, kd, (((0,), (0,)), ((), ())), preferred_element_type=F32)
        st_ref[h] = st * jnp.exp(b_end) + upd
        o = o * lax.rsqrt(jnp.mean(o * o, axis=-1, keepdims=True) + EPS)
        g = g_ref[:, sl]
        o_ref[:, sl] = (o * gain_ref[:, sl] * (g * jax.nn.sigmoid(g))).astype(o_ref.dtype)

    @pl.when(c == pl.num_programs(0) - 1)
    def _():
        stn_ref[...] = st_ref[...]


def hgrn2(p0, lb, gain, state, tok0, t):
    assert t % HG_CHUNK == 0 and tok0 % HG_CHUNK == 0
    nc = t // HG_CHUNK
    c0 = tok0 // HG_CHUNK
    w = HG_WIDTH

    def col(j):
        return pl.BlockSpec((HG_CHUNK, w), lambda c, j=j: (c0 + c, j))

    st_spec = pl.BlockSpec(state.shape, lambda c: (0, 0, 0))
    return pl.pallas_call(
        _hgrn_kernel,
        grid=(nc,),
        in_specs=[col(0), col(1), col(2), col(3),
                  pl.BlockSpec((1, w), lambda c: (0, 0)),
                  pl.BlockSpec((1, w), lambda c: (0, 0)),
                  st_spec],
        out_specs=[pl.BlockSpec((HG_CHUNK, w), lambda c: (c, 0)), st_spec],
        out_shape=[jax.ShapeDtypeStruct((t, w), BF16), jax.ShapeDtypeStruct(state.shape, F32)],
        scratch_shapes=[pltpu.VMEM((HG_HEADS, HG_D, HG_D), F32)],
        compiler_params=_cparams(("arbitrary",)),
        name="hgrn2",
    )(p0, p0, p0, p0, lb, gain, state)


MB_PAIR = 4
MB_PW = MB_PAIR * MB_DH
MB_LG = 128
MB_ONES = 16
MB_VROWS = MB_DH + MB_ONES
MB_VT_ROWS = MB_HEADS * MB_VROWS


def _moba_kernel(q_ref, k_ref, vt_ref, km_ref, bias_ref, o_ref, *scratch, qb0):
    m_ref, l_ref, al_ref, acc_ref, msk_ref, s_ref, p_ref = (
        scratch[i * MB_PAIR:(i + 1) * MB_PAIR] for i in range(7))
    qi = pl.program_id(2) + qb0
    nb = km_ref.shape[0]
    blk = MB_BLOCK
    heads = range(MB_PAIR)
    grp = lambda hh: slice((hh // 2) * MB_LG, (hh // 2 + 1) * MB_LG)
    q = q_ref[...]
    lane = lax.broadcasted_iota(I32, (blk, MB_LG), 1)
    in_head = [(lane < MB_DH) if hh % 2 == 0 else (lane >= MB_DH) for hh in heads]
    qs = q * jnp.asarray(MB_DH ** -0.5, BF16)
    nt = (((1,), (1,)), ((), ()))
    qf = q.astype(F32)
    qht = [jnp.where(in_head[hh], qs[:, grp(hh)].astype(F32), 0.0).T.astype(BF16) for hh in heads]

    n_io = lax.broadcasted_iota(I32, (nb, blk), 0)
    for hh in heads:
        gate = lax.dot_general(km_ref[:, grp(hh)], jnp.where(in_head[hh], qf[:, grp(hh)], 0.0), nt,
                               precision=lax.Precision.HIGHEST, preferred_element_type=F32)
        gate = jnp.where(n_io < qi, gate, NEG_INF)
        chosen = n_io < 0
        for _ in range(MB_TOPK):
            mx = jnp.max(gate, axis=0, keepdims=True)
            ix = jnp.min(jnp.where(gate == mx, n_io, nb), axis=0, keepdims=True)
            hit = n_io == ix
            chosen = chosen | (hit & (mx > NEG_INF))
            gate = jnp.where(hit, NEG_INF, gate)
        msk_ref[hh][...] = jnp.where(chosen, 0.0, NEG_INF)

    vrows = lambda hh: slice(hh * MB_VROWS, (hh + 1) * MB_VROWS)

    def pv_stage(blk_idx):
        vtb = vt_ref[blk_idx]
        r = [jnp.dot(vtb[vrows(hh)], p_ref[hh][...], preferred_element_type=F32) for hh in heads]
        al = [al_ref[hh][...] for hh in heads]
        a_new = [al[hh] * acc_ref[hh][...] + r[hh][:MB_DH] for hh in heads]
        l_new = [al[hh] * l_ref[hh][...] + r[hh][MB_DH:MB_DH + 1] for hh in heads]
        return a_new, l_new

    def store_pv(a_new, l_new):
        for hh in heads:
            acc_ref[hh][...] = a_new[hh]
            l_ref[hh][...] = l_new[hh]

    def softmax_stage():
        s = [s_ref[hh][...] for hh in heads]
        m_old = [m_ref[hh][...] for hh in heads]
        m_new = [jnp.maximum(m_old[hh], jnp.max(s[hh], axis=0, keepdims=True)) for hh in heads]
        alpha = [jnp.exp(m_old[hh] - m_new[hh]) for hh in heads]
        p = [jnp.exp((s[hh] - m_new[hh]).astype(BF16)) for hh in heads]
        return p, alpha, m_new

    def store_softmax(p, alpha, m_new):
        for hh in heads:
            p_ref[hh][...] = p[hh]
            al_ref[hh][...] = alpha[hh]
            m_ref[hh][...] = m_new[hh]

    k_own = k_ref[pl.ds(pl.multiple_of(qi * blk, blk), blk), :]
    key_io = lax.broadcasted_iota(I32, (blk, blk), 0)
    qry_io = lax.broadcasted_iota(I32, (blk, blk), 1)
    for hh in heads:
        s = jnp.dot(k_own[:, grp(hh)], qht[hh], preferred_element_type=F32) + bias_ref[hh, 0]
        s_ref[hh][...] = jnp.where(key_io <= qry_io, s, NEG_INF)
        m_ref[hh][...] = jnp.full((1, blk), NEG_INF, F32)
        l_ref[hh][...] = jnp.zeros((1, blk), F32)
        al_ref[hh][...] = jnp.ones((1, blk), F32)
        acc_ref[hh][...] = jnp.zeros((MB_DH, blk), F32)
        p_ref[hh][...] = jnp.zeros((blk, blk), BF16)

    def step(i, carry, far):
        pv = pv_stage(jnp.where(i <= 1, qi, i - 2))
        sm = softmax_stage()
        kn = k_ref[pl.ds(pl.multiple_of(i * blk, blk), blk), :]
        if far:
            row = [msk_ref[hh][pl.ds(i, 1), :] + bias_ref[hh, MB_BIAS_TILES - 1, 0:1, 0:1] for hh in heads]
            s_next = [jnp.dot(kn[:, grp(hh)], qht[hh], preferred_element_type=F32) + row[hh] for hh in heads]
        else:
            d = qi - i
            s_next = [jnp.dot(kn[:, grp(hh)], qht[hh], preferred_element_type=F32)
                      + bias_ref[hh, d] + msk_ref[hh][pl.ds(i, 1), :] for hh in heads]
        store_pv(*pv)
        for hh in heads:
            s_ref[hh][...] = s_next[hh]
        store_softmax(*sm)
        return carry

    n_far = jnp.maximum(qi - (MB_BIAS_TILES - 2), 0)
    lax.fori_loop(0, n_far, functools.partial(step, far=True), 0)
    lax.fori_loop(n_far, qi, functools.partial(step, far=False), 0)
    pv = pv_stage(jnp.where(qi <= 1, qi, qi - 2))
    sm = softmax_stage()
    store_pv(*pv)
    store_softmax(*sm)
    a_fin, l_fin = pv_stage(jnp.where(qi == 0, qi, qi - 1))
    out_t = jnp.concatenate([a_fin[hh] / l_fin[hh] for hh in heads], axis=0)
    o_ref[...] = out_t.T.astype(o_ref.dtype)


def moba_attention(pqk, vt, km, bias, batch, seq, qb0=0, nqb=None):
    nb = seq // MB_BLOCK
    nqb = nb if nqb is None else nqb
    t = batch * nqb * MB_BLOCK
    groups = MB_WIDTH // MB_PW
    nkb = qb0 + nqb if batch == 1 else nb
    return pl.pallas_call(
        functools.partial(_moba_kernel, qb0=qb0),
        grid=(batch, groups, nqb),
        in_specs=[
            pl.BlockSpec((MB_BLOCK, MB_PW), lambda b, j, i: (b * nb + qb0 + i, j)),
            pl.BlockSpec((nkb * MB_BLOCK, MB_PW), lambda b, j, i: (b, groups + j)),
            pl.BlockSpec((nkb, MB_PAIR * MB_VROWS, MB_BLOCK), lambda b, j, i: (b, j, 0)),
            pl.BlockSpec((None, nb, MB_PW), lambda b, j, i: (b, 0, j)),
            pl.BlockSpec((MB_PAIR, MB_BIAS_TILES, MB_BLOCK, MB_BLOCK), lambda b, j, i: (j, 0, 0, 0)),
        ],
        out_specs=pl.BlockSpec((MB_BLOCK, MB_PW), lambda b, j, i: (b * nqb + i, j)),
        out_shape=jax.ShapeDtypeStruct((t, MB_WIDTH), BF16),
        scratch_shapes=(
            [pltpu.VMEM((1, MB_BLOCK), F32)] * (3 * MB_PAIR)
            + [pltpu.VMEM((MB_DH, MB_BLOCK), F32)] * MB_PAIR
            + [pltpu.VMEM((nb, MB_BLOCK), F32)] * MB_PAIR
            + [pltpu.VMEM((MB_BLOCK, MB_BLOCK), F32)] * MB_PAIR
            + [pltpu.VMEM((MB_BLOCK, MB_BLOCK), BF16)] * MB_PAIR
        ),
        compiler_params=_cparams(("parallel", "parallel", "arbitrary")),
        name="moba_attn",
    )(pqk, pqk, vt, km, bias)


def _t5_bucket(dist):
    max_exact = REL_BUCKETS // 2
    scaled = jnp.log(jnp.maximum(dist, 1).astype(F32) / max_exact) / math.log(REL_MAX_DIST / max_exact)
    large = jnp.minimum(max_exact + (scaled * (REL_BUCKETS - max_exact)).astype(I32), REL_BUCKETS - 1)
    return jnp.where(dist < max_exact, dist, large)


def moba_bias_tiles(rel_bias):
    blk = MB_BLOCK
    span = 2 * blk - 1
    x = jnp.arange(span) - (blk - 1)
    dist = jnp.maximum(jnp.arange(MB_BIAS_TILES)[:, None] * blk + x[None, :], 0)
    w = rel_bias.astype(F32).T[:, _t5_bucket(dist)]
    h = w.shape[0]
    wp = jnp.pad(w, ((0, 0), (0, 0), (0, 1)))[:, :, None, :]
    return pl.pallas_call(
        _toeplitz_kernel,
        grid=(h, MB_BIAS_TILES),
        in_specs=[pl.BlockSpec((None, None, 1, 2 * blk), lambda i, j: (i, j, 0, 0))],
        out_specs=pl.BlockSpec((None, None, blk, blk), lambda i, j: (i, j, 0, 0)),
        out_shape=jax.ShapeDtypeStruct((h, MB_BIAS_TILES, blk, blk), F32),
        compiler_params=_cparams(("parallel", "parallel")),
        name="moba_bias_tiles",
    )(wp)


def _toeplitz_kernel(w_ref, o_ref):
    blk = o_ref.shape[0]
    x = jnp.broadcast_to(w_ref[...], (blk, 2 * blk))
    o_ref[...] = pltpu.roll(x, 1, 1, stride=1, stride_axis=0)[:, blk:]


def _mix_kernel(x_ref, ya_ref, yb_ref, gm_ref, wg_refs, wa_ref, wb_ref, wo_ref, o_ref):
    x = x_ref[...]
    h = _rms(x, gm_ref[...]).astype(BF16)
    za = jnp.dot(ya_ref[...], wa_ref[...], preferred_element_type=F32)
    zb = jnp.dot(yb_ref[...], wb_ref[...], preferred_element_type=F32)
    nw = len(wg_refs) // 2
    wc = wg_refs[0].shape[1]
    gate = lambda ref: jax.nn.sigmoid(jnp.dot(h, ref[...], preferred_element_type=F32))
    z = jnp.concatenate([gate(wg_refs[j]) * za[:, j * wc:(j + 1) * wc]
                         + gate(wg_refs[nw + j]) * zb[:, j * wc:(j + 1) * wc] for j in range(nw)], axis=1)
    o_ref[...] = x + jnp.dot(z.astype(BF16), wo_ref[...], preferred_element_type=F32)


def _mem_kv_kernel(m_ref, g_ref, wk_ref, wv_ref, k_ref, v_ref):
    mn = _rms(m_ref[...], g_ref[...]).astype(BF16)
    k_ref[...] = jnp.dot(mn, wk_ref[...], preferred_element_type=F32).astype(BF16)
    v_ref[...] = jnp.dot(mn, wv_ref[...], preferred_element_type=F32).astype(BF16)


def mem_kv(mem, g, wk, wv):
    b, m, d = mem.shape
    spec = pl.BlockSpec((None, m, d), lambda i: (i, 0, 0))
    wspec = pl.BlockSpec((d, d), lambda i: (0, 0))
    return pl.pallas_call(
        _mem_kv_kernel,
        grid=(b,),
        in_specs=[spec, pl.BlockSpec((1, d), lambda i: (0, 0)), wspec, wspec],
        out_specs=[spec, spec],
        out_shape=[jax.ShapeDtypeStruct((b, m, d), BF16)] * 2,
        compiler_params=_cparams(("parallel",)),
        name="mem_kv",
    )(mem, g, wk, wv)


def _cross_kernel(x_ref, g_ref, wq_ref, k_ref, v_ref, wo_ref, o_ref):
    x = x_ref[...]
    d = x.shape[1]
    dh = d // X_HEADS
    h = _rms(x, g_ref[...]).astype(BF16)
    q = (jnp.dot(h, wq_ref[...], preferred_element_type=F32) * (dh ** -0.5)).astype(BF16)
    outs = []
    for hh in range(X_HEADS):
        sl = slice(hh * dh, (hh + 1) * dh)
        s = lax.dot_general(q[:, sl], k_ref[:, sl], (((1,), (1,)), ((), ())),
                            preferred_element_type=F32)
        p = jnp.exp(s - jnp.max(s, axis=1, keepdims=True))
        l = jnp.sum(p, axis=1, keepdims=True)
        o = jnp.dot(p.astype(BF16), v_ref[:, sl], preferred_element_type=F32) / l
        outs.append(o.astype(BF16))
    o = jnp.concatenate(outs, axis=1)
    o_ref[...] = x + jnp.dot(o, wo_ref[...], preferred_element_type=F32)


def _mix_cross_kernel(x_ref, ya_ref, yb_ref, gm_ref, wg0, wg1, wg2, wg3, wa_ref, wb_ref, wo_ref,
                      g_ref, wq_ref, k_ref, v_ref, wox_ref, o_ref, x1_ref):
    _mix_kernel(x_ref, ya_ref, yb_ref, gm_ref, (wg0, wg1, wg2, wg3), wa_ref, wb_ref, wo_ref, x1_ref)
    _cross_kernel(x1_ref, g_ref, wq_ref, k_ref, v_ref, wox_ref, o_ref)


def mix_cross(x_all, ya, yb, gm, w_in_bf, wa, wb, wo, g, wq, kx, vx, wox, row0, tm=512):
    t = yb.shape[0]
    assert t % tm == 0 and row0 % tm == 0
    d = x_all.shape[1]
    w = ya.shape[1]
    m = kx.shape[1]
    x0 = row0 // tm
    wc = MB_WIDTH
    gcol0 = w_in_bf.shape[1] - 2 * d
    assert gcol0 % wc == 0 and d == 2 * wc
    gate_specs = [pl.BlockSpec((d, wc), lambda i, j=j: (0, gcol0 // wc + j)) for j in range(4)]
    const = lambda a: pl.BlockSpec(a.shape, lambda i: (0,) * a.ndim)
    kv = pl.BlockSpec((None, m, d), lambda i: (0, 0, 0))
    return pl.pallas_call(
        _mix_cross_kernel,
        grid=(t // tm,),
        in_specs=[
            pl.BlockSpec((tm, d), lambda i: (x0 + i, 0)),
            pl.BlockSpec((tm, w), lambda i: (i, 0)),
            pl.BlockSpec((tm, w), lambda i: (i, 0)),
            const(gm), *gate_specs,
            const(wa), const(wb), const(wo), const(g), const(wq), kv, kv, const(wox),
        ],
        out_specs=pl.BlockSpec((tm, d), lambda i: (i, 0)),
        out_shape=jax.ShapeDtypeStruct((t, d), F32),
        scratch_shapes=[pltpu.VMEM((tm, d), F32)],
        compiler_params=_cparams(("parallel",)),
        name="mix_cross",
    )(x_all, ya, yb, gm, *([w_in_bf] * 4), wa, wb, wo, g, wq, kx, vx, wox)


def _topk_rows(sc, k):
    n = sc.shape[0]
    io = lax.broadcasted_iota(I32, sc.shape, 0).astype(F32)
    vals, ids = [], []
    for _ in range(k):
        m = jnp.max(sc, axis=0, keepdims=True)
        ix = jnp.argmax(sc, axis=0, keepdims=True).astype(F32)
        vals.append(m)
        ids.append(ix)
        sc = jnp.where(io == ix, NEG_INF, sc)
    return jnp.concatenate(vals, axis=0), jnp.concatenate(ids, axis=0).astype(I32)


def _pack_bf16_halves(h):
    bits = lax.bitcast_convert_type(h, I32)
    r = bits + 0x7FFF + (lax.shift_right_logical(bits, 16) & 1)
    half = h.shape[1] // 2
    return lax.shift_right_logical(r[:, :half], 16) | (r[:, half:] & HI_MASK)


def _route_kernel(x_ref, g_ref, wq_ref, sk_ref, hp_ref, idx_ref, w_ref, hb_ref, it_ref, wt_ref):
    p = pl.program_id(1)

    @pl.when(p == 0)
    def _():
        h = _rms(x_ref[...], g_ref[...])
        hp_ref[...] = _pack_bf16_halves(h)
        hb_ref[...] = h.astype(BF16)

    qh = jnp.dot(hb_ref[...], wq_ref[...], preferred_element_type=F32)
    tops = []
    for c in range(2):
        seg = qh[:, c * PEER_HALF:(c + 1) * PEER_HALF]
        sc = lax.dot_general(sk_ref[c], seg, (((1,), (1,)), ((), ())),
                             precision=lax.Precision.HIGHEST, preferred_element_type=F32)
        tops.append(_topk_rows(sc, PEER_TOPK))
    (s0, i0), (s1, i1) = tops
    k = PEER_TOPK
    sub = 8
    tm = s0.shape[1]
    r8 = lax.broadcasted_iota(I32, (sub, tm), 0)
    r16 = lax.broadcasted_iota(I32, (k, tm), 0)
    cand_b = [s0[0:1] + s1, s0[1:2] + s1[:sub]]
    cidx_b = [i0[0:1] * PEER_NKEYS + i1, i0[1:2] * PEER_NKEYS + i1[:sub]]
    pos_b = [r16, k + r8]
    for a in range(2, sub):
        keep = r8 < (k // (a + 1))
        cand_b.append(jnp.where(keep, s0[a:a + 1] + s1[:sub], NEG_INF))
        cidx_b.append(i0[a:a + 1] * PEER_NKEYS + i1[:sub])
        pos_b.append(a * k + r8)
    cand_b.append(s0[sub:] + s1[0:1])
    cidx_b.append(i0[sub:] * PEER_NKEYS + i1[0:1])
    pos_b.append((sub + r8) * k)
    cand = jnp.concatenate(cand_b, axis=0)
    cidx = jnp.concatenate(cidx_b, axis=0)
    pos = jnp.concatenate(pos_b, axis=0).astype(F32)
    vals, ids = [], []
    for _ in range(k):
        m = jnp.max(cand, axis=0, keepdims=True)
        px = jnp.min(jnp.where(cand == m, pos, float(k * k)), axis=0, keepdims=True)
        hit = pos == px
        vals.append(m)
        ids.append(jnp.sum(jnp.where(hit, cidx, 0), axis=0, keepdims=True))
        cand = jnp.where(hit, NEG_INF, cand)
    sf = jnp.concatenate(vals, axis=0)
    e = jnp.exp(sf - sf[0:1])
    rows = pl.ds(pl.multiple_of(p * PEER_TOPK, PEER_TOPK), PEER_TOPK)
    wt_ref[rows, :] = e / jnp.sum(e, axis=0, keepdims=True)
    it_ref[rows, :] = jnp.concatenate(ids, axis=0)

    @pl.when(p == pl.num_programs(1) - 1)
    def _():
        idx_ref[...] = it_ref[...].T
        w_ref[...] = wt_ref[...].T


def peer_route(x2d, g, wq, sk, tok0, t, tm=1024):
    assert t % tm == 0 and tok0 % tm == 0
    d = x2d.shape[1]
    ph = sk.shape[0]
    nsel = ph * PEER_TOPK
    blk0 = tok0 // tm
    return pl.pallas_call(
        _route_kernel,
        grid=(t // tm, ph),
        in_specs=[
            pl.BlockSpec((tm, d), lambda i, p: (blk0 + i, 0)),
            pl.BlockSpec((1, d), lambda i, p: (0, 0)),
            pl.BlockSpec((d, 2 * PEER_HALF), lambda i, p: (0, p)),
            pl.BlockSpec((None, 2, PEER_NKEYS, PEER_HALF), lambda i, p: (p, 0, 0, 0)),
        ],
        out_specs=[
            pl.BlockSpec((tm, d // 2), lambda i, p: (i, 0)),
            pl.BlockSpec((tm, nsel), lambda i, p: (i, 0)),
            pl.BlockSpec((tm, nsel), lambda i, p: (i, 0)),
        ],
        out_shape=[jax.ShapeDtypeStruct((t, d // 2), I32),
                   jax.ShapeDtypeStruct((t, nsel), I32),
                   jax.ShapeDtypeStruct((t, nsel), F32)],
        scratch_shapes=[pltpu.VMEM((tm, d), BF16),
                        pltpu.VMEM((nsel, tm), I32),
                        pltpu.VMEM((nsel, tm), F32)],
        compiler_params=_cparams(("parallel", "arbitrary")),
        name="peer_route",
    )(x2d, g, wq, sk)


def _mix_cross_route_kernel(*refs):
    mc_in, (gr_ref, wpq_ref, sk_ref, o_ref, hp_ref, idx_ref, w_ref, hb_ref, it_ref, wt_ref) = refs[:16], refs[16:]

    @pl.when(pl.program_id(1) == 0)
    def _():
        _mix_cross_kernel(*mc_in, o_ref, o_ref)

    _route_kernel(o_ref, gr_ref, wpq_ref, sk_ref, hp_ref, idx_ref, w_ref, hb_ref, it_ref, wt_ref)


def mix_cross_route(x_all, ya, yb, gm, w_in_bf, wa, wb, wo, g, wq, kx, vx, wox, gr, wpq, sk, row0):
    t = yb.shape[0]
    tm = math.gcd(math.gcd(t, row0), 1024) if row0 else math.gcd(t, 1024)
    assert t % tm == 0 and row0 % tm == 0
    d = x_all.shape[1]
    wdt = ya.shape[1]
    m = kx.shape[1]
    x0 = row0 // tm
    ph = sk.shape[0]
    nsel = ph * PEER_TOPK
    wc = MB_WIDTH
    gcol0 = w_in_bf.shape[1] - 2 * d
    assert gcol0 % wc == 0 and d == 2 * wc
    once = pl.Buffered(1)
    gate_specs = [pl.BlockSpec((d, wc), lambda i, p, j=j: (0, gcol0 // wc + j), pipeline_mode=once) for j in range(4)]
    const = lambda a: pl.BlockSpec(a.shape, lambda i, p: (0,) * a.ndim, pipeline_mode=once)
    kv = pl.BlockSpec((None, m, d), lambda i, p: (0, 0, 0), pipeline_mode=once)
    tile = lambda n: pl.BlockSpec((tm, n), lambda i, p: (i, 0))
    return pl.pallas_call(
        _mix_cross_route_kernel,
        grid=(t // tm, ph),
        in_specs=[
            pl.BlockSpec((tm, d), lambda i, p: (x0 + i, 0)), tile(wdt), tile(wdt),
            const(gm), *gate_specs,
            const(wa), const(wb), const(wo), const(g), const(wq), kv, kv, const(wox),
            const(gr),
            pl.BlockSpec((d, 2 * PEER_HALF), lambda i, p: (0, p)),
            pl.BlockSpec((None, 2, PEER_NKEYS, PEER_HALF), lambda i, p: (p, 0, 0, 0)),
        ],
        out_specs=[tile(d), tile(d // 2), tile(nsel), tile(nsel)],
        out_shape=[jax.ShapeDtypeStruct((t, d), F32),
                   jax.ShapeDtypeStruct((t, d // 2), I32),
                   jax.ShapeDtypeStruct((t, nsel), I32),
                   jax.ShapeDtypeStruct((t, nsel), F32)],
        scratch_shapes=[pltpu.VMEM((tm, d), BF16),
                        pltpu.VMEM((nsel, tm), I32),
                        pltpu.VMEM((nsel, tm), F32)],
        compiler_params=_cparams(("parallel", "arbitrary")),
        name="mix_cross_route",
    )(x_all, ya, yb, gm, *([w_in_bf] * 4), wa, wb, wo, g, wq, kx, vx, wox, gr, wpq, sk)


def _final_kernel(x_ref, y_ref, g_ref, *rest):
    o_ref = rest[-1]
    o_ref[...] = _rms(x_ref[...] + y_ref[...], g_ref[...])


def final_norm_into(out, xs, y, g, row0, total, tm=512):
    t, d = y.shape
    assert t % tm == 0 and row0 % tm == 0 and total % tm == 0
    blk0 = row0 // tm
    spec = pl.BlockSpec((tm, d), lambda i: (i, 0))
    in_specs = [spec, spec, pl.BlockSpec((1, d), lambda i: (0, 0))]
    args = [xs, y, g]
    aliases = {}
    if out is not None:
        in_specs.append(pl.BlockSpec(memory_space=pl.ANY))
        args.append(out)
        aliases = {3: 0}
    return pl.pallas_call(
        _final_kernel, grid=(t // tm,),
        in_specs=in_specs,
        out_specs=pl.BlockSpec((tm, d), lambda i: (blk0 + i, 0)),
        out_shape=jax.ShapeDtypeStruct((total, d), F32),
        input_output_aliases=aliases,
        compiler_params=_cparams(("parallel",)), name="final_norm",
    )(*args)


SC_CORES = 2
SC_SUBCORES = 16
SC_WORKERS = SC_CORES * SC_SUBCORES
SC_LANES = 16
SC_GROUP = 32


def _sc_mesh():
    return plsc.VectorSubcoreMesh(core_axis_name="c", subcore_axis_name="s")


def _sc_params():
    return pltpu.CompilerParams(needs_layout_passes=False)


def _sc_worker_id():
    return lax.axis_index("s") * SC_CORES + lax.axis_index("c")


SC_ROW_LANE = 128


def _sc_unit_off(u):
    off = u * SC_LANES
    return off if isinstance(off, int) else pl.multiple_of(off, SC_LANES)


GELU_C0 = math.sqrt(2.0 / math.pi)
GELU_C1 = 0.044715


def _gelu_tanh(x):
    z = GELU_C0 * (x + GELU_C1 * (x * x * x))
    th = 1.0 - 2.0 / (jnp.exp(2.0 * z) + 1.0)
    return 0.5 * x * (1.0 + th)


SC_PK_RING = 4
SC_PK_SUB = 4
HI_MASK = -65536


def _pack_tables_kernel(u_ref, v_ref, o_ref):
    for part, ref in enumerate((u_ref, v_ref)):
        words = _pack_bf16_halves(ref[...])
        for sub in range(SC_PK_SUB):
            o_ref[:, part * SC_PK_SUB + sub, :] = words[:, sub * SC_ROW_LANE:(sub + 1) * SC_ROW_LANE]


def pack_expert_tables(u, v, te=512):
    e, d = u.shape
    assert d == 2 * SC_PK_SUB * SC_ROW_LANE
    spec = pl.BlockSpec((te, d), lambda i: (i, 0))
    return pl.pallas_call(
        _pack_tables_kernel, grid=(e // te,), in_specs=[spec, spec],
        out_specs=pl.BlockSpec((te, 2 * SC_PK_SUB, SC_ROW_LANE), lambda i: (i, 0, 0)),
        out_shape=jax.ShapeDtypeStruct((e, 2 * SC_PK_SUB, SC_ROW_LANE), I32),
        compiler_params=_cparams(("parallel",)), name="pack_expert_tables",
    )(u, v)


def _unpack_halves(x32):
    w = plsc.bitcast(x32, I32)
    return plsc.bitcast(w << 16, F32), plsc.bitcast(w & HI_MASK, F32)


def _tree_sum(xs):
    while len(xs) > 1:
        xs = [xs[i] + xs[i + 1] for i in range(0, len(xs), 2)]
    return xs[0]


def peer_experts_pk_sc(tab_uv, idx_flat, w_flat, hp, d):
    t = hp.shape[0]
    nsel = PEER_SEL
    assert t % SC_WORKERS == 0 and d == 2 * SC_PK_SUB * SC_ROW_LANE
    tpw = t // SC_WORKERS
    g = SC_GROUP if tpw % SC_GROUP == 0 else SC_GROUP // 2
    assert tpw % g == 0
    groups = tpw // g
    heads = nsel // SC_LANES
    chunks = d // 32
    units = g * heads
    ring = SC_PK_RING
    assert units % ring == 0
    row_buf = pltpu.VMEM((SC_LANES, 2 * SC_PK_SUB, SC_ROW_LANE), I32)

    def row_words(rows, r, wc, sub0):
        per = SC_ROW_LANE // SC_LANES
        return plsc.bitcast(
            rows[r, sub0 + wc // per, pl.ds(pl.multiple_of((wc % per) * SC_LANES, SC_LANES), SC_LANES)], BF16)

    def ring_loop(n_units, start, wait, compute):
        for u in range(ring - 1):
            start(u, u)

        @pl.loop(0, n_units, step=ring)
        def _(uu):
            for b in range(ring):
                u = uu + b
                nxt = u + (ring - 1)

                @pl.when(nxt < n_units)
                def _():
                    start(nxt, (b + ring - 1) % ring)

                wait(u, b)
                compute(u, b)

    @functools.partial(
        pl.kernel, mesh=_sc_mesh(),
        out_type=jax.ShapeDtypeStruct((t, d), F32),
        scratch_types=[
            pltpu.VMEM((g * nsel,), I32),
            pltpu.VMEM((g * nsel,), F32),
            pltpu.VMEM((g, d // 2), I32),
            pltpu.VMEM((g, d), F32),
            pltpu.VMEM((SC_LANES * SC_LANES,), F32),
            [row_buf] * ring,
            [pltpu.SemaphoreType.DMA] * ring,
        ],
        compiler_params=_sc_params(),
        name="peer_experts_pk_sc",
    )
    def k(tab_hbm, idx_hbm, w_hbm, h_hbm, out_hbm, idx_v, coef_v, h_v, y_v, red_v, rows, sems):
        wid = _sc_worker_id()
        lane = lax.iota(I32, SC_LANES)

        def copy(u, slot):
            ids = idx_v.at[pl.ds(_sc_unit_off(u), SC_LANES)]
            return pltpu.make_async_copy(tab_hbm.at[ids], rows[slot], sems[slot])

        def dots(u, slot):
            tt = u // heads

            def body(cp, accs):
                out = []
                hv = [plsc.bitcast(h_v[tt, pl.ds(pl.multiple_of((2 * cp + i) * SC_LANES, SC_LANES), SC_LANES)], BF16)
                      for i in range(2)]
                for r in range(SC_LANES):
                    pr = (row_words(rows[slot], r, 2 * cp, 0) * hv[0]
                          + row_words(rows[slot], r, 2 * cp + 1, 0) * hv[1])
                    lo, hi = _unpack_halves(pr)
                    out.append(accs[r] + lo + hi)
                return tuple(out)

            accs = lax.fori_loop(0, chunks // 2, body,
                                 tuple(jnp.zeros((SC_LANES,), F32) for _ in range(SC_LANES)))
            for r in range(SC_LANES):
                red_v[pl.ds(r * SC_LANES, SC_LANES)] = accs[r]
            act = _tree_sum([plsc.load_gather(red_v, [lane * SC_LANES + j]) for j in range(SC_LANES)])
            sl = pl.ds(_sc_unit_off(u), SC_LANES)
            coef_v[sl] = coef_v[sl] * _gelu_tanh(act)

        def combine(u, slot):
            tt = u // heads
            first = (u % heads) == 0
            cb = []
            for r in range(SC_LANES):
                c = plsc.load_gather(coef_v, [jnp.full((SC_LANES,), u * SC_LANES + r, I32)])
                cb.append(plsc.pack(c, c, format=plsc.PackFormat.INTERLEAVED))

            @plsc.parallel_loop(0, chunks, unroll=2)
            def _(wc):
                lo, hi = _unpack_halves(
                    _tree_sum([cb[r] * row_words(rows[slot], r, wc, SC_PK_SUB) for r in range(SC_LANES)]))
                for half, val in ((0, lo), (1, hi)):
                    sl = pl.ds(pl.multiple_of(half * (d // 2) + wc * SC_LANES, SC_LANES), SC_LANES)
                    y_v[tt, sl] = val + jnp.where(first, 0.0, y_v[tt, sl])

        def unit(u, slot):
            dots(u, slot)
            combine(u, slot)

        @pl.loop(0, groups)
        def _(gi):
            base = wid * tpw + gi * g
            pltpu.sync_copy(idx_hbm.at[pl.ds(base * nsel, g * nsel)], idx_v)
            pltpu.sync_copy(w_hbm.at[pl.ds(base * nsel, g * nsel)], coef_v)
            pltpu.sync_copy(h_hbm.at[pl.ds(base, g)], h_v)
            ring_loop(units, lambda u, s: copy(u, s).start(), lambda u, s: copy(u, s).wait(), unit)
            pltpu.sync_copy(y_v, out_hbm.at[pl.ds(base, g)])

    return k(tab_uv, idx_flat, w_flat, hp)


def kernel(x, mem, rel_bias, ln_mix, w_in, hg_lower, hg_norm, w_up_a, w_up_b, w_out, ln_cross, ln_mem, wq_x, wk_x, wv_x, wo_x, ln_ffn, peer_query, peer_subkeys, peer_u, peer_v, ln_final):
    b, s, d = x.shape
    depth = w_in.shape[0]
    assert depth == 1, "the residual after PEER is fused into the final norm"
    assert s % MB_BLOCK == 0 and s % HG_CHUNK == 0 and s % (PEER_SLICES * SC_WORKERS * SC_GROUP) == 0
    nb = s // MB_BLOCK
    row = lambda a: a.reshape(1, -1).astype(F32)
    lb_all = jnp.cumsum(jax.nn.softmax(hg_lower.astype(F32), axis=0), axis=0)
    bias = moba_bias_tiles(rel_bias)
    n_hg = 4 * HG_WIDTH
    n_qk = 2 * MB_WIDTH
    n_mb = 3 * MB_WIDTH
    l = 0
    w = cast_bf16(w_in[l].astype(F32))
    w_vt = cast_bf16_t(w_in[l].astype(F32), n_hg + n_qk, n_mb - n_qk)
    wa, wb, wo = w_up_a[l].astype(BF16), w_up_b[l].astype(BF16), w_out[l].astype(BF16)
    wqx, wox = wq_x[l].astype(BF16), wo_x[l].astype(BF16)
    wpq, sk = peer_query[l].astype(BF16), peer_subkeys[l].astype(F32)
    tab_uv = pack_expert_tables(peer_u[l].astype(F32), peer_v[l].astype(F32))
    kx, vx = mem_kv(mem, row(ln_mem[l]), wk_x[l].astype(BF16), wv_x[l].astype(BF16))

    x_all = x.reshape(b * s, d)
    outs = []
    for bi in range(b):
        p0, pqk, km, vt = in_proj(x_all, row(ln_mix[l]), w, w_vt, bi * s, s)
        hg_state = jnp.zeros((HG_HEADS, HG_D, HG_D), F32)
        km = km.reshape(1, nb, MB_WIDTH)
        sizes = [s // PEER_SLICES] * PEER_SLICES
        if bi == b - 1:
            first = SC_WORKERS * SC_GROUP // 2
            sizes = [first, sizes[0] - first] + sizes[1:]
        tok0 = 0
        for ts in sizes:
            ya, hg_state = hgrn2(p0, row(lb_all[l]), row(hg_norm[l]), hg_state, tok0, ts)
            yb = moba_attention(pqk, vt, km, bias, 1, s, tok0 // MB_BLOCK, ts // MB_BLOCK)
            xs, hp, eidx, wts = mix_cross_route(x_all, ya, yb, row(ln_mix[l]), w, wa, wb, wo, row(ln_cross[l]), wqx,
                                                kx[bi:bi + 1], vx[bi:bi + 1], wox, row(ln_ffn[l]), wpq, sk,
                                                bi * s + tok0)
            y = peer_experts_pk_sc(tab_uv, eidx.reshape(ts * PEER_SEL), wts.reshape(ts * PEER_SEL), hp, d)
            outs.append((xs, y, bi * s + tok0))
            tok0 += ts
    out = None
    for xs, y, row0 in sorted(outs, key=lambda e: (-(e[2] // s), e[2])):
        out = final_norm_into(out, xs, y, row(ln_final), row0, b * s)
    return out.reshape(b, s, d)
```
